```python
import jax, jax.numpy as jnp
from jax import lax
import numpy as np

D_MODEL = 1024
BATCH = 16
SEQ = 2048
DEPTH = 2

N_Q_HEADS = 8
N_KV_HEADS = 2
HEAD_DIM = 64
Q_GROUP = N_Q_HEADS // N_KV_HEADS
ATT_WIDTH = N_Q_HEADS * HEAD_DIM
KV_WIDTH = N_KV_HEADS * HEAD_DIM
WINDOW = 128
ATT_BLOCK = 128
SGU_WIDTH = D_MODEL // 2
SGU_GROUPS = 8
SGU_GROUP_DIM = SGU_WIDTH // SGU_GROUPS
SGU_CHUNK = 128
N_BRANCHES = 2
IN_WIDTH = ATT_WIDTH + 2 * KV_WIDTH + 2 * SGU_WIDTH + N_BRANCHES * D_MODEL
D_FF = 256 * ((8 * D_MODEL // 3 + 255) // 256)
CONV_WIDTH = 3
NORM_EPS = 1e-6
NEG_INF = -1e30

kernel_name = "hybrid_gated_swa_sgu_convffn"


def rmsnorm(x, gain):
    xf = x.astype(jnp.float32)
    y = xf * lax.rsqrt(jnp.mean(xf * xf, axis=-1, keepdims=True) + NORM_EPS)
    return (y * gain.astype(jnp.float32)).astype(x.dtype)


def alibi_slopes():
    return jnp.exp2(-8.0 * (jnp.arange(N_Q_HEADS, dtype=jnp.float32) + 1.0) / N_Q_HEADS)


def sliding_window_attention(q, k, v, q_gain, k_gain, sinks):
    B, S, _ = q.shape
    nb = S // ATT_BLOCK
    q = rmsnorm(q.reshape(B, S, N_Q_HEADS, HEAD_DIM), q_gain)
    k = rmsnorm(k.reshape(B, S, N_KV_HEADS, HEAD_DIM), k_gain)
    v = v.reshape(B, S, N_KV_HEADS, HEAD_DIM)
    qb = q.reshape(B, nb, ATT_BLOCK, N_KV_HEADS, Q_GROUP, HEAD_DIM)

    def band(t):
        tp = jnp.pad(t, ((0, 0), (ATT_BLOCK, 0), (0, 0), (0, 0)))
        tp = tp.reshape(B, nb + 1, ATT_BLOCK, N_KV_HEADS, HEAD_DIM)
        return jnp.concatenate([tp[:, :-1], tp[:, 1:]], axis=2)

    kb, vb = band(k), band(v)
    scores = jnp.einsum('bnqhgd,bnkhd->bnhgqk', qb, kb,
                        preferred_element_type=jnp.float32) * (HEAD_DIM ** -0.5)
    qi = jnp.arange(ATT_BLOCK)[:, None]
    kj = jnp.arange(2 * ATT_BLOCK)[None, :]
    dist = qi + ATT_BLOCK - kj
    key_pos = jnp.arange(nb)[:, None] * ATT_BLOCK - ATT_BLOCK + kj
    valid = ((dist >= 0) & (dist < WINDOW))[None] & (key_pos >= 0)[:, None, :]
    slopes = alibi_slopes().reshape(N_KV_HEADS, Q_GROUP)
    alibi = -slopes[:, :, None, None] * dist.astype(jnp.float32)[None, None]
    scores = jnp.where(valid[None, :, None, None], scores + alibi[None, None], NEG_INF)
    sink = jnp.broadcast_to(
        sinks.astype(jnp.float32).reshape(1, 1, N_KV_HEADS, Q_GROUP, 1, 1),
        scores.shape[:-1] + (1,))
    probs = jax.nn.softmax(jnp.concatenate([scores, sink], axis=-1), axis=-1)[..., :-1]
    out = jnp.einsum('bnhgqk,bnkhd->bnqhgd', probs.astype(v.dtype), vb)
    return out.reshape(B, S, ATT_WIDTH)


def chunked_spatial_gating(u, v, sgu_gain, w_s, b_s):
    B, S, _ = u.shape
    nc = S // SGU_CHUNK
    u = jax.nn.gelu(u)
    v = rmsnorm(jax.nn.gelu(v), sgu_gain)
    vc = v.reshape(B, nc, SGU_CHUNK, SGU_GROUPS, SGU_GROUP_DIM)
    causal = jnp.tril(jnp.ones((SGU_CHUNK, SGU_CHUNK), dtype=bool))
    w = jnp.where(causal[None], w_s, 0)
    mixed = jnp.einsum('gij,bcjgd->bcigd', w, vc) + b_s.T[:, :, None]
    return u * mixed.reshape(B, S, SGU_WIDTH)


def conv_gated_mlp(h, w_up, conv_w, conv_b, w_down):
    S = h.shape[1]
    z = h @ w_up
    zp = jnp.pad(z, ((0, 0), (CONV_WIDTH - 1, 0), (0, 0)))
    zc = conv_b
    for tap in range(CONV_WIDTH):
        zc = zc + conv_w[tap] * zp[:, tap:tap + S]
    gate, val = jnp.split(zc, 2, axis=-1)
    return (jax.nn.silu(gate) * val) @ w_down


def _fwd_setup_inputs(seed: int = 0) -> dict:
    key = jax.random.key(seed)
    ks = jax.random.split(key, 20)
    f32 = jnp.float32

    def nrm(k, shape, scale):
        return jax.random.normal(k, shape, f32) * scale

    return {
        "x": nrm(ks[0], (BATCH, SEQ, D_MODEL), 1.0),
        "mix_norm": 1.0 + nrm(ks[1], (DEPTH, D_MODEL), 0.05),
        "w_in": nrm(ks[2], (DEPTH, D_MODEL, IN_WIDTH), D_MODEL ** -0.5),
        "q_norm": 1.0 + nrm(ks[3], (DEPTH, HEAD_DIM), 0.05),
        "k_norm": 1.0 + nrm(ks[4], (DEPTH, HEAD_DIM), 0.05),
        "sinks": nrm(ks[5], (DEPTH, N_Q_HEADS), 0.5),
        "sgu_norm": 1.0 + nrm(ks[6], (DEPTH, SGU_WIDTH), 0.05),
        "w_s": nrm(ks[7], (DEPTH, SGU_GROUPS, SGU_CHUNK, SGU_CHUNK), SGU_CHUNK ** -0.5),
        "b_s": 1.0 + nrm(ks[8], (DEPTH, SGU_GROUPS, SGU_CHUNK), 0.1),
        "w_oa": nrm(ks[9], (DEPTH, ATT_WIDTH, D_MODEL), ATT_WIDTH ** -0.5),
        "w_ob": nrm(ks[10], (DEPTH, SGU_WIDTH, D_MODEL), SGU_WIDTH ** -0.5),
        "w_out": nrm(ks[11], (DEPTH, D_MODEL, D_MODEL), D_MODEL ** -0.5),
        "ffn_norm": 1.0 + nrm(ks[12], (DEPTH, D_MODEL), 0.05),
        "w_up": nrm(ks[13], (DEPTH, D_MODEL, 2 * D_FF), D_MODEL ** -0.5),
        "conv_w": nrm(ks[14], (DEPTH, CONV_WIDTH, 2 * D_FF), CONV_WIDTH ** -0.5),
        "conv_b": nrm(ks[15], (DEPTH, 2 * D_FF), 0.02),
        "w_down": nrm(ks[16], (DEPTH, D_FF, D_MODEL), D_FF ** -0.5),
    }


def _fwd_reference(x, mix_norm, w_in, q_norm, k_norm, sinks, sgu_norm, w_s, b_s,
              w_oa, w_ob, w_out, ffn_norm, w_up, conv_w, conv_b, w_down):
    splits = (ATT_WIDTH,
              ATT_WIDTH + KV_WIDTH,
              ATT_WIDTH + 2 * KV_WIDTH,
              ATT_WIDTH + 2 * KV_WIDTH + SGU_WIDTH,
              ATT_WIDTH + 2 * KV_WIDTH + 2 * SGU_WIDTH,
              ATT_WIDTH + 2 * KV_WIDTH + 2 * SGU_WIDTH + D_MODEL)
    for l in range(DEPTH):
        h = rmsnorm(x, mix_norm[l])
        proj = h @ w_in[l]
        q, k, v, su, sv, g_a, g_b = jnp.split(proj, splits, axis=-1)
        y_att = sliding_window_attention(q, k, v, q_norm[l], k_norm[l], sinks[l])
        y_sgu = chunked_spatial_gating(su, sv, sgu_norm[l], w_s[l], b_s[l])
        merged = (jax.nn.sigmoid(g_a) * (y_att @ w_oa[l])
                  + jax.nn.sigmoid(g_b) * (y_sgu @ w_ob[l]))
        x = x + merged @ w_out[l]
        x = x + conv_gated_mlp(rmsnorm(x, ffn_norm[l]), w_up[l], conv_w[l], conv_b[l], w_down[l])
    return x


import jax as _jax
import jax.numpy as _jnp

TWIN_FORMAT = 'train_step'
FWD_PARAMS = ['x', 'mix_norm', 'w_in', 'q_norm', 'k_norm', 'sinks', 'sgu_norm', 'w_s', 'b_s', 'w_oa', 'w_ob', 'w_out', 'ffn_norm', 'w_up', 'conv_w', 'conv_b', 'w_down']
TWIN_WEIGHTS = ['mix_norm', 'w_in', 'q_norm', 'k_norm', 'sinks', 'sgu_norm', 'w_s', 'b_s', 'w_oa', 'w_ob', 'w_out', 'ffn_norm', 'w_up', 'conv_w', 'conv_b', 'w_down']
TWIN_DIFF_INPUT = 'x'
TWIN_INPUTS = ['x', 'mix_norm', 'w_in', 'q_norm', 'k_norm', 'sinks', 'sgu_norm', 'w_s', 'b_s', 'w_oa', 'w_ob', 'w_out', 'ffn_norm', 'w_up', 'conv_w', 'conv_b', 'w_down', 'loss_target', 'm_mix_norm', 'm_w_in', 'm_q_norm', 'm_k_norm', 'm_sinks', 'm_sgu_norm', 'm_w_s', 'm_b_s', 'm_w_oa', 'm_w_ob', 'm_w_out', 'm_ffn_norm', 'm_w_up', 'm_conv_w', 'm_conv_b', 'm_w_down', 'v_mix_norm', 'v_w_in', 'v_q_norm', 'v_k_norm', 'v_sinks', 'v_sgu_norm', 'v_w_s', 'v_b_s', 'v_w_oa', 'v_w_ob', 'v_w_out', 'v_ffn_norm', 'v_w_up', 'v_conv_w', 'v_conv_b', 'v_w_down']
TWIN_OUTPUTS = ['loss', 'grad_x', 'grad_mix_norm', 'grad_w_in', 'grad_q_norm', 'grad_k_norm', 'grad_sinks', 'grad_sgu_norm', 'grad_w_s', 'grad_b_s', 'grad_w_oa', 'grad_w_ob', 'grad_w_out', 'grad_ffn_norm', 'grad_w_up', 'grad_conv_w', 'grad_conv_b', 'grad_w_down', 'delta_mix_norm', 'delta_w_in', 'delta_q_norm', 'delta_k_norm', 'delta_sinks', 'delta_sgu_norm', 'delta_w_s', 'delta_b_s', 'delta_w_oa', 'delta_w_ob', 'delta_w_out', 'delta_ffn_norm', 'delta_w_up', 'delta_conv_w', 'delta_conv_b', 'delta_w_down', 'new_m_mix_norm', 'new_m_w_in', 'new_m_q_norm', 'new_m_k_norm', 'new_m_sinks', 'new_m_sgu_norm', 'new_m_w_s', 'new_m_b_s', 'new_m_w_oa', 'new_m_w_ob', 'new_m_w_out', 'new_m_ffn_norm', 'new_m_w_up', 'new_m_conv_w', 'new_m_conv_b', 'new_m_w_down', 'new_v_mix_norm', 'new_v_w_in', 'new_v_q_norm', 'new_v_k_norm', 'new_v_sinks', 'new_v_sgu_norm', 'new_v_w_s', 'new_v_b_s', 'new_v_w_oa', 'new_v_w_ob', 'new_v_w_out', 'new_v_ffn_norm', 'new_v_w_up', 'new_v_conv_w', 'new_v_conv_b', 'new_v_w_down']
TWIN_LEAF_KINDS = {'loss': 'loss', 'grad_x': 'grad_x', 'grad_mix_norm': 'grad_w', 'grad_w_in': 'grad_w', 'grad_q_norm': 'grad_w', 'grad_k_norm': 'grad_w', 'grad_sinks': 'grad_w', 'grad_sgu_norm': 'grad_w', 'grad_w_s': 'grad_w', 'grad_b_s': 'grad_w', 'grad_w_oa': 'grad_w', 'grad_w_ob': 'grad_w', 'grad_w_out': 'grad_w', 'grad_ffn_norm': 'grad_w', 'grad_w_up': 'grad_w', 'grad_conv_w': 'grad_w', 'grad_conv_b': 'grad_w', 'grad_w_down': 'grad_w', 'delta_mix_norm': 'delta_w', 'delta_w_in': 'delta_w', 'delta_q_norm': 'delta_w', 'delta_k_norm': 'delta_w', 'delta_sinks': 'delta_w', 'delta_sgu_norm': 'delta_w', 'delta_w_s': 'delta_w', 'delta_b_s': 'delta_w', 'delta_w_oa': 'delta_w', 'delta_w_ob': 'delta_w', 'delta_w_out': 'delta_w', 'delta_ffn_norm': 'delta_w', 'delta_w_up': 'delta_w', 'delta_conv_w': 'delta_w', 'delta_conv_b': 'delta_w', 'delta_w_down': 'delta_w', 'new_m_mix_norm': 'new_m', 'new_m_w_in': 'new_m', 'new_m_q_norm': 'new_m', 'new_m_k_norm': 'new_m', 'new_m_sinks': 'new_m', 'new_m_sgu_norm': 'new_m', 'new_m_w_s': 'new_m', 'new_m_b_s': 'new_m', 'new_m_w_oa': 'new_m', 'new_m_w_ob': 'new_m', 'new_m_w_out': 'new_m', 'new_m_ffn_norm': 'new_m', 'new_m_w_up': 'new_m', 'new_m_conv_w': 'new_m', 'new_m_conv_b': 'new_m', 'new_m_w_down': 'new_m', 'new_v_mix_norm': 'new_v', 'new_v_w_in': 'new_v', 'new_v_q_norm': 'new_v', 'new_v_k_norm': 'new_v', 'new_v_sinks': 'new_v', 'new_v_sgu_norm': 'new_v', 'new_v_w_s': 'new_v', 'new_v_b_s': 'new_v', 'new_v_w_oa': 'new_v', 'new_v_w_ob': 'new_v', 'new_v_w_out': 'new_v', 'new_v_ffn_norm': 'new_v', 'new_v_w_up': 'new_v', 'new_v_conv_w': 'new_v', 'new_v_conv_b': 'new_v', 'new_v_w_down': 'new_v'}


def _forward(args):
    return _fwd_reference(*[args[k] for k in FWD_PARAMS])


def _output_shape():
    out = _jax.eval_shape(lambda: _forward(_fwd_setup_inputs(0)))
    return out.shape, out.dtype

N_MICROBATCH = 1
ADAM_LR = 0.001
ADAM_B1 = 0.9
ADAM_B2 = 0.999
ADAM_EPS = 1e-08
ADAM_WD = 0.01
ADAM_STEP = 10
PER_EXAMPLE_BATCH_AXIS = {'x': 0, 'loss_target': 0}
SHARED_INPUTS = []
_WEIGHT_DTYPES = {'mix_norm': _jnp.float32, 'w_in': _jnp.float32, 'q_norm': _jnp.float32, 'k_norm': _jnp.float32, 'sinks': _jnp.float32, 'sgu_norm': _jnp.float32, 'w_s': _jnp.float32, 'b_s': _jnp.float32, 'w_oa': _jnp.float32, 'w_ob': _jnp.float32, 'w_out': _jnp.float32, 'ffn_norm': _jnp.float32, 'w_up': _jnp.float32, 'conv_w': _jnp.float32, 'conv_b': _jnp.float32, 'w_down': _jnp.float32}
MOMENT_SCALE = {'mix_norm': 8.085481e+00, 'w_in': 3.736482e-01, 'q_norm': 5.334945e+00, 'k_norm': 5.311041e+00, 'sinks': 2.208999e+01, 'sgu_norm': 3.916703e+00, 'w_s': 1.135302e+00, 'b_s': 4.011993e+00, 'w_oa': 2.859775e-01, 'w_ob': 1.960660e+00, 'w_out': 1.729884e+00, 'ffn_norm': 2.533554e+01, 'w_up': 5.416816e-01, 'conv_w': 3.590569e+00, 'conv_b': 3.338238e+00, 'w_down': 4.461365e-01}


def _to_microbatches(a, axis):
    t = _jnp.moveaxis(a, axis, 0)
    t = t.reshape((N_MICROBATCH, t.shape[0] // N_MICROBATCH) + t.shape[1:])
    return _jnp.moveaxis(t, 1, axis + 1)


def setup_inputs(seed: int = 0) -> dict:
    inp = _fwd_setup_inputs(seed)
    key = _jax.random.fold_in(_jax.random.key(seed), 7919)
    shape, _ = _output_shape()
    out = dict(inp)
    out["loss_target"] = _jax.random.normal(_jax.random.fold_in(key, 0), shape, _jnp.float32)
    for i, name in enumerate(TWIN_WEIGHTS):
        w = inp[name].astype(_jnp.float32)
        if MOMENT_SCALE is None:
            s = _jnp.sqrt(_jnp.mean(_jnp.square(w)) + 1e-30)
        else:
            s = MOMENT_SCALE[name]
        km, kv = _jax.random.split(_jax.random.fold_in(key, i + 1))
        out[name] = w
        out["m_" + name] = s * _jax.random.normal(km, w.shape, _jnp.float32)
        out["v_" + name] = (s * s) * _jax.random.uniform(kv, w.shape, _jnp.float32, 0.5, 1.5)
    if N_MICROBATCH > 1:
        for name, axis in PER_EXAMPLE_BATCH_AXIS.items():
            out[name] = _to_microbatches(out[name], axis)
    return {'x': out['x'], 'mix_norm': out['mix_norm'], 'w_in': out['w_in'], 'q_norm': out['q_norm'], 'k_norm': out['k_norm'], 'sinks': out['sinks'], 'sgu_norm': out['sgu_norm'], 'w_s': out['w_s'], 'b_s': out['b_s'], 'w_oa': out['w_oa'], 'w_ob': out['w_ob'], 'w_out': out['w_out'], 'ffn_norm': out['ffn_norm'], 'w_up': out['w_up'], 'conv_w': out['conv_w'], 'conv_b': out['conv_b'], 'w_down': out['w_down'], 'loss_target': out['loss_target'], 'm_mix_norm': out['m_mix_norm'], 'm_w_in': out['m_w_in'], 'm_q_norm': out['m_q_norm'], 'm_k_norm': out['m_k_norm'], 'm_sinks': out['m_sinks'], 'm_sgu_norm': out['m_sgu_norm'], 'm_w_s': out['m_w_s'], 'm_b_s': out['m_b_s'], 'm_w_oa': out['m_w_oa'], 'm_w_ob': out['m_w_ob'], 'm_w_out': out['m_w_out'], 'm_ffn_norm': out['m_ffn_norm'], 'm_w_up': out['m_w_up'], 'm_conv_w': out['m_conv_w'], 'm_conv_b': out['m_conv_b'], 'm_w_down': out['m_w_down'], 'v_mix_norm': out['v_mix_norm'], 'v_w_in': out['v_w_in'], 'v_q_norm': out['v_q_norm'], 'v_k_norm': out['v_k_norm'], 'v_sinks': out['v_sinks'], 'v_sgu_norm': out['v_sgu_norm'], 'v_w_s': out['v_w_s'], 'v_b_s': out['v_b_s'], 'v_w_oa': out['v_w_oa'], 'v_w_ob': out['v_w_ob'], 'v_w_out': out['v_w_out'], 'v_ffn_norm': out['v_ffn_norm'], 'v_w_up': out['v_w_up'], 'v_conv_w': out['v_conv_w'], 'v_conv_b': out['v_conv_b'], 'v_w_down': out['v_w_down']}


def _loss(weights, diff, rest, loss_target):
    with _jax.named_scope("forward"):
        args = {**rest, TWIN_DIFF_INPUT: diff, **{k: w.astype(_WEIGHT_DTYPES[k]) for k, w in weights.items()}}
        y = _forward(args)
    with _jax.named_scope("loss_head"):
        err = _jnp.square(y.astype(_jnp.float32) - loss_target)
        return 0.5 * _jnp.sum(_jnp.mean(err, axis=-1)) if err.ndim else 0.5 * err


def _adamw(w, g, m, v):
    m = ADAM_B1 * m + (1.0 - ADAM_B1) * g
    v = ADAM_B2 * v + (1.0 - ADAM_B2) * _jnp.square(g)
    m_hat = m / (1.0 - ADAM_B1 ** ADAM_STEP)
    v_hat = v / (1.0 - ADAM_B2 ** ADAM_STEP)
    delta = -ADAM_LR * (m_hat / (_jnp.sqrt(v_hat) + ADAM_EPS) + ADAM_WD * w)
    return delta, m, v


def reference(x, mix_norm, w_in, q_norm, k_norm, sinks, sgu_norm, w_s, b_s, w_oa, w_ob, w_out, ffn_norm, w_up, conv_w, conv_b, w_down, loss_target, m_mix_norm, m_w_in, m_q_norm, m_k_norm, m_sinks, m_sgu_norm, m_w_s, m_b_s, m_w_oa, m_w_ob, m_w_out, m_ffn_norm, m_w_up, m_conv_w, m_conv_b, m_w_down, v_mix_norm, v_w_in, v_q_norm, v_k_norm, v_sinks, v_sgu_norm, v_w_s, v_b_s, v_w_oa, v_w_ob, v_w_out, v_ffn_norm, v_w_up, v_conv_w, v_conv_b, v_w_down):
    given = dict(x=x, mix_norm=mix_norm, w_in=w_in, q_norm=q_norm, k_norm=k_norm, sinks=sinks, sgu_norm=sgu_norm, w_s=w_s, b_s=b_s, w_oa=w_oa, w_ob=w_ob, w_out=w_out, ffn_norm=ffn_norm, w_up=w_up, conv_w=conv_w, conv_b=conv_b, w_down=w_down, loss_target=loss_target, m_mix_norm=m_mix_norm, m_w_in=m_w_in, m_q_norm=m_q_norm, m_k_norm=m_k_norm, m_sinks=m_sinks, m_sgu_norm=m_sgu_norm, m_w_s=m_w_s, m_b_s=m_b_s, m_w_oa=m_w_oa, m_w_ob=m_w_ob, m_w_out=m_w_out, m_ffn_norm=m_ffn_norm, m_w_up=m_w_up, m_conv_w=m_conv_w, m_conv_b=m_conv_b, m_w_down=m_w_down, v_mix_norm=v_mix_norm, v_w_in=v_w_in, v_q_norm=v_q_norm, v_k_norm=v_k_norm, v_sinks=v_sinks, v_sgu_norm=v_sgu_norm, v_w_s=v_w_s, v_b_s=v_b_s, v_w_oa=v_w_oa, v_w_ob=v_w_ob, v_w_out=v_w_out, v_ffn_norm=v_ffn_norm, v_w_up=v_w_up, v_conv_w=v_conv_w, v_conv_b=v_conv_b, v_w_down=v_w_down)
    weights = {n: given[n] for n in TWIN_WEIGHTS}
    shared = {n: given[n] for n in SHARED_INPUTS}
    per_example = {n: given[n] for n in ['x']}
    grad_fn = _jax.value_and_grad(_loss, argnums=(0, 1))

    def one_microbatch(ex, loss_target):
        ex = dict(ex)
        diff = ex.pop(TWIN_DIFF_INPUT)
        return grad_fn(weights, diff, {**shared, **ex}, loss_target)

    if N_MICROBATCH == 1:
        loss, (grad_w, grad_x) = one_microbatch(per_example, given["loss_target"])
    else:
        def body(carry, xs):
            loss_sum, grad_sum = carry
            l_k, (gw_k, gx_k) = one_microbatch(xs[0], xs[1])
            with _jax.named_scope("update"):
                return (loss_sum + l_k, _jax.tree.map(_jnp.add, grad_sum, gw_k)), gx_k

        init = (_jnp.zeros((), _jnp.float32), _jax.tree.map(_jnp.zeros_like, weights))
        (loss, grad_w), grad_x = _jax.lax.scan(body, init, (per_example, given["loss_target"]))
    with _jax.named_scope("update"):
        delta_w, new_m, new_v = {}, {}, {}
        for n in TWIN_WEIGHTS:
            delta_w[n], new_m[n], new_v[n] = _adamw(weights[n], grad_w[n], given["m_" + n], given["v_" + n])
    return (loss, grad_x, *[grad_w[n] for n in TWIN_WEIGHTS], *[delta_w[n] for n in TWIN_WEIGHTS],
            *[new_m[n] for n in TWIN_WEIGHTS], *[new_v[n] for n in TWIN_WEIGHTS])
```

```python
import functools
import math

import jax
import jax.numpy as jnp
from jax import lax
from jax.experimental import pallas as pl
from jax.experimental.pallas import tpu as pltpu

F32 = jnp.float32
BF16 = jnp.bfloat16
MESH = pl.DeviceIdType.MESH

D_MODEL = 1024
N_Q_HEADS = 8
HEAD_DIM = 64
ATT_WIDTH = 512
KV_WIDTH = 128
BLOCK = 128
SGU_WIDTH = 512
SGU_GROUPS = 8
IN_WIDTH = 3840
D_FF = 2816
NORM_EPS = 1e-6
NEG_INF = -1e30
ATT_SCALE = HEAD_DIM ** -0.5
ALIBI_SLOPES = tuple(2.0 ** (-(h + 1)) for h in range(N_Q_HEADS))
ADAM_LR, ADAM_B1, ADAM_B2, ADAM_EPS, ADAM_WD, ADAM_STEP = 0.001, 0.9, 0.999, 1e-08, 0.01, 10
N_DEV = 8

QKV_WIDTH = ATT_WIDTH + 2 * KV_WIDTH
REST_WIDTH = IN_WIDTH - QKV_WIDTH
COL_SUV, COL_GA, COL_GB, COL_QKV = 0, 1024, 2048, 3072

LANES = 128
VMEM_LIMIT_V7X = 56 * 1024 * 1024
GELU_C = math.sqrt(2.0 / math.pi)
GELU_K = 0.044715


def _params(sem=None):
    return pltpu.CompilerParams(dimension_semantics=sem, vmem_limit_bytes=VMEM_LIMIT_V7X)


def _sigmoid(x):
    return 1.0 / (1.0 + jnp.exp(-x))


def _gelu(x):
    th = jnp.tanh(GELU_C * (x + GELU_K * x * x * x))
    return 0.5 * x * (1.0 + th)


def _gelu_and_grad(x):
    x2 = x * x
    th = jnp.tanh(GELU_C * (x + GELU_K * x2 * x))
    g = 0.5 * x * (1.0 + th)
    dg = 0.5 * (1.0 + th) + 0.5 * x * (1.0 - th * th) * (GELU_C * (1.0 + 3.0 * GELU_K * x2))
    return g, dg


def _dot(a, b, dims):
    return lax.dot_general(a, b, (dims, ((), ())), preferred_element_type=F32)


def _dot_nn(a, b):
    return _dot(a, b, ((1,), (0,)))


def _dot_nt(a, b):
    return _dot(a, b, ((1,), (1,)))


def _dot_tn(a, b):
    return _dot(a, b, ((0,), (0,)))


def _lo_mask(shape):
    return lax.broadcasted_iota(jnp.int32, shape, len(shape) - 1) < (LANES // 2)


def _half_sums(x, lo):
    s_lo = jnp.sum(jnp.where(lo, x, 0.0), axis=-1, keepdims=True)
    s_all = jnp.sum(x, axis=-1, keepdims=True)
    return jnp.where(lo, s_lo, s_all - s_lo)


def _dup_half(x, half, lo):
    r = pltpu.roll(x, LANES // 2, axis=1)
    return jnp.where(lo, x, r) if half == 0 else jnp.where(lo, r, x)


def _mm(a, b, *, mode, out_dtype, tm, tn, tk, name, epilogue=None, extras=()):
    if mode == "nn":
        (M, K), N = a.shape, b.shape[1]
    elif mode == "nt":
        (M, K), N = a.shape, b.shape[0]
    else:
        (K, M), N = a.shape, b.shape[1]
    assert M % tm == 0 and N % tn == 0 and K % tk == 0, (name, M, N, K, tm, tn, tk)
    gm, gn, gk = M // tm, N // tn, K // tk
    if mode == "nn":
        a_spec = pl.BlockSpec((tm, tk), lambda i, j, k: (i, k))
        b_spec = pl.BlockSpec((tk, tn), lambda i, j, k: (k, j))
        contract = ((1,), (0,))
    elif mode == "nt":
        a_spec = pl.BlockSpec((tm, tk), lambda i, j, k: (i, k))
        b_spec = pl.BlockSpec((tn, tk), lambda i, j, k: (j, k))
        contract = ((1,), (1,))
    else:
        a_spec = pl.BlockSpec((tk, tm), lambda i, j, k: (k, i))
        b_spec = pl.BlockSpec((tk, tn), lambda i, j, k: (k, j))
        contract = ((0,), (0,))
    o_spec = pl.BlockSpec((tm, tn), lambda i, j, k: (i, j))
    n_extra = len(extras)

    def finish(acc, extra_refs, o_ref):
        if epilogue is not None:
            acc = epilogue(acc, *[r[...] for r in extra_refs])
        o_ref[...] = acc.astype(out_dtype)

    def body(a_ref, b_ref, *rest):
        extra_refs, o_ref = rest[:n_extra], rest[n_extra]
        part = _dot(a_ref[...].astype(BF16), b_ref[...].astype(BF16), contract)
        if gk == 1:
            finish(part, extra_refs, o_ref)
            return
        acc_ref = rest[n_extra + 1]
        k = pl.program_id(2)

        @pl.when(k == 0)
        def _():
            acc_ref[...] = part

        @pl.when(k > 0)
        def _():
            acc_ref[...] += part

        @pl.when(k == gk - 1)
        def _():
            finish(acc_ref[...], extra_refs, o_ref)

    return pl.pallas_call(
        body,
        name=name,
        grid=(gm, gn, gk),
        in_specs=[a_spec, b_spec] + [o_spec] * n_extra,
        out_specs=o_spec,
        out_shape=jax.ShapeDtypeStruct((M, N), out_dtype),
        scratch_shapes=[] if gk == 1 else [pltpu.VMEM((tm, tn), F32)],
        compiler_params=_params(("parallel", "parallel", "arbitrary")),
    )(a, b, *extras)


def _add(acc, r):
    return acc + r


def _rms_fwd(x, gain, *, name, tm=512):
    T, D = x.shape

    def body(x_ref, g_ref, h_ref):
        xv = x_ref[...]
        r = lax.rsqrt(jnp.mean(xv * xv, axis=-1, keepdims=True) + NORM_EPS)
        h_ref[...] = (xv * r * g_ref[...]).astype(BF16)

    return pl.pallas_call(
        body, name=name, grid=(T // tm,),
        in_specs=[pl.BlockSpec((tm, D), lambda i: (i, 0)), pl.BlockSpec((1, D), lambda i: (0, 0))],
        out_specs=pl.BlockSpec((tm, D), lambda i: (i, 0)),
        out_shape=jax.ShapeDtypeStruct((T, D), BF16),
        compiler_params=_params(("parallel",)),
    )(x, gain.reshape(1, D))


def _rms_bwd(x, gain, dh, dres, *, name, tm=512):
    T, D = x.shape

    def body(x_ref, g_ref, dh_ref, dres_ref, dx_ref, dg_ref):
        xv = x_ref[...]
        r = lax.rsqrt(jnp.mean(xv * xv, axis=-1, keepdims=True) + NORM_EPS)
        xh = xv * r
        dhv = dh_ref[...]
        dxh = dhv * g_ref[...]
        dx = r * (dxh - xh * jnp.mean(dxh * xh, axis=-1, keepdims=True))
        dx_ref[...] = dres_ref[...] + dx
        part = jnp.sum(dhv * xh, axis=0, keepdims=True)

        @pl.when(pl.program_id(0) == 0)
        def _():
            dg_ref[...] = part

        @pl.when(pl.program_id(0) > 0)
        def _():
            dg_ref[...] += part

    row = pl.BlockSpec((tm, D), lambda i: (i, 0))
    vec = pl.BlockSpec((1, D), lambda i: (0, 0))
    dx, dg = pl.pallas_call(
        body, name=name, grid=(T // tm,),
        in_specs=[row, vec, row, row],
        out_specs=[row, vec],
        out_shape=[jax.ShapeDtypeStruct((T, D), F32), jax.ShapeDtypeStruct((1, D), F32)],
        compiler_params=_params(("arbitrary",)),
    )(x, gain.reshape(1, D), dh, dres)
    return dx, dg.reshape(D)


def _head_norm(x, gain2, lo):
    ms = _half_sums(x * x, lo) * (1.0 / HEAD_DIM)
    r = lax.rsqrt(ms + NORM_EPS)
    xh = x * r
    return xh * gain2, xh, r


def _head_norm_bwd(xh, r, gain2, dy, lo):
    dxh = dy * gain2
    dx = r * (dxh - xh * (_half_sums(dxh * xh, lo) * (1.0 / HEAD_DIM)))
    return dx, dy * xh


def _att_masks():
    qi = lax.broadcasted_iota(jnp.int32, (BLOCK, BLOCK), 0)
    kj = lax.broadcasted_iota(jnp.int32, (BLOCK, BLOCK), 1)
    d_cur = qi - kj
    d_prev = qi - kj + BLOCK
    return d_cur >= 0, d_prev < BLOCK, d_cur.astype(F32), d_prev.astype(F32)


def _att_probs(qm, k2c, k2p, sink, slope, masks, has_prev):
    ok_c, ok_p, d_c, d_p = masks
    s_c = jnp.where(ok_c, _dot_nt(qm, k2c) * ATT_SCALE - slope * d_c, NEG_INF)
    s_p = jnp.where(jnp.logical_and(ok_p, has_prev), _dot_nt(qm, k2p) * ATT_SCALE - slope * d_p, NEG_INF)
    m = jnp.maximum(jnp.maximum(jnp.max(s_c, axis=-1, keepdims=True), jnp.max(s_p, axis=-1, keepdims=True)), sink)
    e_c = jnp.exp(s_c - m)
    e_p = jnp.exp(s_p - m)
    e_s = jnp.exp(sink - m)
    inv = 1.0 / (jnp.sum(e_c, axis=-1, keepdims=True) + jnp.sum(e_p, axis=-1, keepdims=True) + e_s)
    return e_c * inv, e_p * inv, e_s * inv


def _attention_fwd(proj, q_gain, k_gain, sinks, *, n_seq, seq, name):
    T = n_seq * seq
    nb = seq // BLOCK
    qcol, kvcol = COL_QKV // ATT_WIDTH, (COL_QKV + ATT_WIDTH) // (2 * KV_WIDTH)

    def body(q_ref, kv_ref, qg_ref, kg_ref, sink_ref, y_ref):
        lo = _lo_mask((BLOCK, LANES))
        masks = _att_masks()
        qg, kg = qg_ref[...], kg_ref[...]

        def block(i, carry):
            r0 = pl.multiple_of(i * BLOCK, BLOCK)
            rp = pl.multiple_of(jnp.maximum(i - 1, 0) * BLOCK, BLOCK)
            has_prev = i > 0
            kn_c = _head_norm(kv_ref[pl.ds(r0, BLOCK), 0:KV_WIDTH], kg, lo)[0].astype(BF16)
            kn_p = _head_norm(kv_ref[pl.ds(rp, BLOCK), 0:KV_WIDTH], kg, lo)[0].astype(BF16)
            v_c = kv_ref[pl.ds(r0, BLOCK), KV_WIDTH:2 * KV_WIDTH].astype(BF16)
            v_p = kv_ref[pl.ds(rp, BLOCK), KV_WIDTH:2 * KV_WIDTH].astype(BF16)
            for pair in range(N_Q_HEADS // 2):
                kv = pair // 2
                k2c, k2p = _dup_half(kn_c, kv, lo), _dup_half(kn_p, kv, lo)
                v2c, v2p = _dup_half(v_c, kv, lo), _dup_half(v_p, kv, lo)
                qn = _head_norm(q_ref[pl.ds(r0, BLOCK), pair * LANES:(pair + 1) * LANES], qg, lo)[0]
                out = None
                for half in range(2):
                    h = 2 * pair + half
                    mine = lo if half == 0 else jnp.logical_not(lo)
                    qm = jnp.where(mine, qn, 0.0).astype(BF16)
                    p_c, p_p, _ = _att_probs(qm, k2c, k2p, sink_ref[h], ALIBI_SLOPES[h], masks, has_prev)
                    o = _dot_nn(p_c.astype(BF16), v2c) + _dot_nn(p_p.astype(BF16), v2p)
                    out = o if out is None else jnp.where(lo, out, o)
                y_ref[pl.ds(r0, BLOCK), pair * LANES:(pair + 1) * LANES] = out.astype(BF16)
            return carry

        lax.fori_loop(0, nb, block, 0)

    vec = pl.BlockSpec((1, LANES), lambda b: (0, 0))
    return pl.pallas_call(
        body, name=name, grid=(n_seq,),
        in_specs=[pl.BlockSpec((seq, ATT_WIDTH), lambda b: (b, qcol)),
                  pl.BlockSpec((seq, 2 * KV_WIDTH), lambda b: (b, kvcol)),
                  vec, vec, pl.BlockSpec(memory_space=pltpu.SMEM)],
        out_specs=pl.BlockSpec((seq, ATT_WIDTH), lambda b: (b, 0)),
        out_shape=jax.ShapeDtypeStruct((T, ATT_WIDTH), BF16),
        compiler_params=_params(("parallel",)),
    )(proj, proj, jnp.tile(q_gain, 2).reshape(1, LANES), jnp.tile(k_gain, 2).reshape(1, LANES), sinks)


def _attention_bwd(proj, dy, q_gain, k_gain, sinks, *, n_seq, seq, name):
    T = n_seq * seq
    nb = seq // BLOCK
    qcol, kvcol = COL_QKV // ATT_WIDTH, (COL_QKV + ATT_WIDTH) // (2 * KV_WIDTH)

    def body(q_ref, kv_ref, dy_ref, qg_ref, kg_ref, sink_ref, dqkv_ref, dqg_ref, dkg_ref, dsink_ref,
             dkn_acc, dv_acc, qg_acc, kg_acc, sink_acc):
        lo = _lo_mask((BLOCK, LANES))
        hi = jnp.logical_not(lo)
        lane = lax.broadcasted_iota(jnp.int32, (BLOCK, LANES), 1)
        masks = _att_masks()
        qg, kg = qg_ref[...], kg_ref[...]
        first = pl.program_id(0) == 0

        @pl.when(first)
        def _():
            qg_acc[...] = jnp.zeros_like(qg_acc)
            kg_acc[...] = jnp.zeros_like(kg_acc)
            sink_acc[...] = jnp.zeros_like(sink_acc)

        dkn_acc[...] = jnp.zeros_like(dkn_acc)
        dv_acc[...] = jnp.zeros_like(dv_acc)

        def block(i, carry):
            r0 = pl.multiple_of(i * BLOCK, BLOCK)
            rp = pl.multiple_of(jnp.maximum(i - 1, 0) * BLOCK, BLOCK)
            has_prev = i > 0
            kn_c = _head_norm(kv_ref[pl.ds(r0, BLOCK), 0:KV_WIDTH], kg, lo)[0].astype(BF16)
            kn_p = _head_norm(kv_ref[pl.ds(rp, BLOCK), 0:KV_WIDTH], kg, lo)[0].astype(BF16)
            v_c = kv_ref[pl.ds(r0, BLOCK), KV_WIDTH:2 * KV_WIDTH].astype(BF16)
            v_p = kv_ref[pl.ds(rp, BLOCK), KV_WIDTH:2 * KV_WIDTH].astype(BF16)
            dk_c = [jnp.zeros((BLOCK, LANES), F32) for _ in range(2)]
            dk_p = [jnp.zeros((BLOCK, LANES), F32) for _ in range(2)]
            dv_c = [jnp.zeros((BLOCK, LANES), F32) for _ in range(2)]
            dv_p = [jnp.zeros((BLOCK, LANES), F32) for _ in range(2)]
            for pair in range(N_Q_HEADS // 2):
                kv = pair // 2
                cols = slice(pair * LANES, (pair + 1) * LANES)
                k2c, k2p = _dup_half(kn_c, kv, lo), _dup_half(kn_p, kv, lo)
                v2c, v2p = _dup_half(v_c, kv, lo), _dup_half(v_p, kv, lo)
                qn, qh, qr = _head_norm(q_ref[pl.ds(r0, BLOCK), cols], qg, lo)
                do_pair = dy_ref[pl.ds(r0, BLOCK), cols]
                dqn = None
                for half in range(2):
                    h = 2 * pair + half
                    mine = lo if half == 0 else hi
                    qm = jnp.where(mine, qn, 0.0).astype(BF16)
                    dom = jnp.where(mine, do_pair, jnp.zeros_like(do_pair))
                    p_c, p_p, p_s = _att_probs(qm, k2c, k2p, sink_ref[h], ALIBI_SLOPES[h], masks, has_prev)
                    dp_c = _dot_nt(dom, v2c)
                    dp_p = _dot_nt(dom, v2p)
                    delta = jnp.sum(p_c * dp_c, axis=-1, keepdims=True) + jnp.sum(p_p * dp_p, axis=-1, keepdims=True)
                    ds_c = (p_c * (dp_c - delta)).astype(BF16)
                    ds_p = (p_p * (dp_p - delta)).astype(BF16)
                    sink_acc[...] += jnp.where(lane == h, -(p_s * delta), 0.0)
                    dq_h = (_dot_nn(ds_c, k2c) + _dot_nn(ds_p, k2p)) * ATT_SCALE
                    dqn = dq_h if dqn is None else jnp.where(lo, dqn, dq_h)
                    dk_c[kv] = dk_c[kv] + _dot_tn(ds_c, qm)
                    dk_p[kv] = dk_p[kv] + _dot_tn(ds_p, qm)
                    dv_c[kv] = dv_c[kv] + _dot_tn(p_c.astype(BF16), dom)
                    dv_p[kv] = dv_p[kv] + _dot_tn(p_p.astype(BF16), dom)
                dq, dg = _head_norm_bwd(qh, qr, qg, dqn, lo)
                dqkv_ref[pl.ds(r0, BLOCK), cols] = dq.astype(BF16)
                qg_acc[...] += dg

            def fold(parts):
                a = parts[0] + pltpu.roll(parts[0], LANES // 2, axis=1)
                b = parts[1] + pltpu.roll(parts[1], LANES // 2, axis=1)
                return jnp.where(lo, a, b)

            dkn_acc[pl.ds(r0, BLOCK), :] += fold(dk_c) * ATT_SCALE
            dkn_acc[pl.ds(rp, BLOCK), :] += fold(dk_p) * ATT_SCALE
            dv_acc[pl.ds(r0, BLOCK), :] += fold(dv_c)
            dv_acc[pl.ds(rp, BLOCK), :] += fold(dv_p)
            return carry

        lax.fori_loop(0, nb, block, 0)

        def finish(i, carry):
            r0 = pl.multiple_of(i * BLOCK, BLOCK)
            _, kh, kr = _head_norm(kv_ref[pl.ds(r0, BLOCK), 0:KV_WIDTH], kg, lo)
            dk, dg = _head_norm_bwd(kh, kr, kg, dkn_acc[pl.ds(r0, BLOCK), :], lo)
            dqkv_ref[pl.ds(r0, BLOCK), ATT_WIDTH:ATT_WIDTH + KV_WIDTH] = dk.astype(BF16)
            dqkv_ref[pl.ds(r0, BLOCK), ATT_WIDTH + KV_WIDTH:QKV_WIDTH] = dv_acc[pl.ds(r0, BLOCK), :].astype(BF16)
            kg_acc[...] += dg
            return carry

        lax.fori_loop(0, nb, finish, 0)

        @pl.when(pl.program_id(0) == n_seq - 1)
        def _():
            dqg_ref[...] = jnp.sum(qg_acc[...], axis=0, keepdims=True)
            dkg_ref[...] = jnp.sum(kg_acc[...], axis=0, keepdims=True)
            dsink_ref[...] = jnp.sum(sink_acc[...], axis=0, keepdims=True)

    vec = pl.BlockSpec((1, LANES), lambda b: (0, 0))
    acc = pltpu.VMEM((BLOCK, LANES), F32)
    dqkv, dqg, dkg, dsink = pl.pallas_call(
        body, name=name, grid=(n_seq,),
        in_specs=[pl.BlockSpec((seq, ATT_WIDTH), lambda b: (b, qcol)),
                  pl.BlockSpec((seq, 2 * KV_WIDTH), lambda b: (b, kvcol)),
                  pl.BlockSpec((seq, ATT_WIDTH), lambda b: (b, 0)),
                  vec, vec, pl.BlockSpec(memory_space=pltpu.SMEM)],
        out_specs=[pl.BlockSpec((seq, QKV_WIDTH), lambda b: (b, 0)), vec, vec, vec],
        out_shape=[jax.ShapeDtypeStruct((T, QKV_WIDTH), BF16)] + [jax.ShapeDtypeStruct((1, LANES), F32)] * 3,
        scratch_shapes=[pltpu.VMEM((seq, KV_WIDTH), F32), pltpu.VMEM((seq, KV_WIDTH), F32), acc, acc, acc],
        compiler_params=_params(("arbitrary",)),
    )(proj, proj, dy, jnp.tile(q_gain, 2).reshape(1, LANES), jnp.tile(k_gain, 2).reshape(1, LANES), sinks)
    half = LANES // 2
    return dqkv, dqg[0, :half] + dqg[0, half:], dkg[0, :half] + dkg[0, half:], dsink[0, :N_Q_HEADS]


def _sgu_weights(w_ref):
    r = lax.broadcasted_iota(jnp.int32, (BLOCK, BLOCK), 0)
    c = lax.broadcasted_iota(jnp.int32, (BLOCK, BLOCK), 1)
    return [jnp.where(r >= c, w_ref[g], 0.0).astype(BF16) for g in range(SGU_GROUPS)]


def _sgu_fwd(proj, gain, w_s, bias_full, *, n_seq, seq, name):
    T = n_seq * seq
    nc = seq // BLOCK

    def body(suv_ref, g_ref, w_ref, b_ref, y_ref):
        lo = _lo_mask((BLOCK, LANES))
        wm = _sgu_weights(w_ref)
        gain_v = g_ref[...]

        def chunk(c, carry):
            r0 = pl.multiple_of(c * BLOCK, BLOCK)
            gv = _gelu(suv_ref[pl.ds(r0, BLOCK), SGU_WIDTH:2 * SGU_WIDTH])
            r = lax.rsqrt(jnp.mean(gv * gv, axis=-1, keepdims=True) + NORM_EPS)
            vn = (gv * r * gain_v).astype(BF16)
            for p in range(SGU_WIDTH // LANES):
                cols = slice(p * LANES, (p + 1) * LANES)
                vp = vn[:, cols]
                mixed = jnp.where(lo, _dot_nn(wm[2 * p], vp), _dot_nn(wm[2 * p + 1], vp)) + b_ref[:, cols]
                u = _gelu(suv_ref[pl.ds(r0, BLOCK), cols])
                y_ref[pl.ds(r0, BLOCK), cols] = (u * mixed).astype(BF16)
            return carry

        lax.fori_loop(0, nc, chunk, 0)

    return pl.pallas_call(
        body, name=name, grid=(n_seq,),
        in_specs=[pl.BlockSpec((seq, 2 * SGU_WIDTH), lambda b: (b, COL_SUV // (2 * SGU_WIDTH))),
                  pl.BlockSpec((1, SGU_WIDTH), lambda b: (0, 0)),
                  pl.BlockSpec((SGU_GROUPS, BLOCK, BLOCK), lambda b: (0, 0, 0)),
                  pl.BlockSpec((BLOCK, SGU_WIDTH), lambda b: (0, 0))],
        out_specs=pl.BlockSpec((seq, SGU_WIDTH), lambda b: (b, 0)),
        out_shape=jax.ShapeDtypeStruct((T, SGU_WIDTH), BF16),
        compiler_params=_params(("parallel",)),
    )(proj, gain.reshape(1, SGU_WIDTH), w_s, bias_full)


def _sgu_bwd(proj, dy, gain, w_s, bias_full, *, n_seq, seq, name):
    T = n_seq * seq
    nc = seq // BLOCK
    n_tiles = SGU_WIDTH // LANES

    def body(suv_ref, dy_ref, g_ref, w_ref, b_ref, dsuv_ref, dg_ref, dw_ref, db_ref, dg_acc, dw_acc, db_acc):
        lo = _lo_mask((BLOCK, LANES))
        hi = jnp.logical_not(lo)
        wm = _sgu_weights(w_ref)
        wmt = [jnp.where(lax.broadcasted_iota(jnp.int32, (BLOCK, BLOCK), 1) >= lax.broadcasted_iota(jnp.int32, (BLOCK, BLOCK), 0),
                         w_ref[g].T, 0.0).astype(BF16) for g in range(SGU_GROUPS)]
        gain_v = g_ref[...]

        @pl.when(pl.program_id(0) == 0)
        def _():
            dg_acc[...] = jnp.zeros_like(dg_acc)
            dw_acc[...] = jnp.zeros_like(dw_acc)
            db_acc[...] = jnp.zeros_like(db_acc)

        def chunk(c, carry):
            r0 = pl.multiple_of(c * BLOCK, BLOCK)
            gv, dgelu_v = _gelu_and_grad(suv_ref[pl.ds(r0, BLOCK), SGU_WIDTH:2 * SGU_WIDTH])
            r = lax.rsqrt(jnp.mean(gv * gv, axis=-1, keepdims=True) + NORM_EPS)
            vh = gv * r
            vn = (vh * gain_v).astype(BF16)
            dvn_tiles = []
            for p in range(n_tiles):
                cols = slice(p * LANES, (p + 1) * LANES)
                vp = vn[:, cols]
                mixed = jnp.where(lo, _dot_nn(wm[2 * p], vp), _dot_nn(wm[2 * p + 1], vp)) + b_ref[:, cols]
                u, dgelu_u = _gelu_and_grad(suv_ref[pl.ds(r0, BLOCK), cols])
                dyv = dy_ref[pl.ds(r0, BLOCK), cols]
                dsuv_ref[pl.ds(r0, BLOCK), cols] = (dyv * mixed * dgelu_u).astype(BF16)
                dm = dyv * u
                db_acc[:, cols] += dm
                dm_bf = dm.astype(BF16)
                dvn_tiles.append(jnp.where(lo, _dot_nn(wmt[2 * p], dm_bf), _dot_nn(wmt[2 * p + 1], dm_bf)))
                dw_acc[2 * p] += _dot_nt(jnp.where(lo, dm, 0.0).astype(BF16), vp)
                dw_acc[2 * p + 1] += _dot_nt(jnp.where(hi, dm, 0.0).astype(BF16), vp)
            dvn = jnp.concatenate(dvn_tiles, axis=1)
            dg_acc[...] += dvn * vh
            dvh = dvn * gain_v
            dgv = r * (dvh - vh * jnp.mean(dvh * vh, axis=-1, keepdims=True))
            dsuv_ref[pl.ds(r0, BLOCK), SGU_WIDTH:2 * SGU_WIDTH] = (dgv * dgelu_v).astype(BF16)
            return carry

        lax.fori_loop(0, nc, chunk, 0)

        @pl.when(pl.program_id(0) == n_seq - 1)
        def _():
            dg_ref[...] = jnp.sum(dg_acc[...], axis=0, keepdims=True)
            r = lax.broadcasted_iota(jnp.int32, (BLOCK, BLOCK), 0)
            c = lax.broadcasted_iota(jnp.int32, (BLOCK, BLOCK), 1)
            for g in range(SGU_GROUPS):
                dw_ref[g] = jnp.where(r >= c, dw_acc[g], 0.0)
            lane = lax.broadcasted_iota(jnp.int32, (BLOCK, LANES), 1)
            out = jnp.zeros((BLOCK, LANES), F32)
            for p in range(n_tiles):
                tile = db_acc[:, p * LANES:(p + 1) * LANES]
                s_lo = jnp.sum(jnp.where(lo, tile, 0.0), axis=-1, keepdims=True)
                s_hi = jnp.sum(jnp.where(hi, tile, 0.0), axis=-1, keepdims=True)
                out = jnp.where(lane == 2 * p, s_lo, out)
                out = jnp.where(lane == 2 * p + 1, s_hi, out)
            db_ref[...] = out

    dsuv, dg, dw, db = pl.pallas_call(
        body, name=name, grid=(n_seq,),
        in_specs=[pl.BlockSpec((seq, 2 * SGU_WIDTH), lambda b: (b, COL_SUV // (2 * SGU_WIDTH))),
                  pl.BlockSpec((seq, SGU_WIDTH), lambda b: (b, 0)),
                  pl.BlockSpec((1, SGU_WIDTH), lambda b: (0, 0)),
                  pl.BlockSpec((SGU_GROUPS, BLOCK, BLOCK), lambda b: (0, 0, 0)),
                  pl.BlockSpec((BLOCK, SGU_WIDTH), lambda b: (0, 0))],
        out_specs=[pl.BlockSpec((seq, 2 * SGU_WIDTH), lambda b: (b, 0)),
                   pl.BlockSpec((1, SGU_WIDTH), lambda b: (0, 0)),
                   pl.BlockSpec((SGU_GROUPS, BLOCK, BLOCK), lambda b: (0, 0, 0)),
                   pl.BlockSpec((BLOCK, LANES), lambda b: (0, 0))],
        out_shape=[jax.ShapeDtypeStruct((T, 2 * SGU_WIDTH), BF16), jax.ShapeDtypeStruct((1, SGU_WIDTH), F32),
                   jax.ShapeDtypeStruct((SGU_GROUPS, BLOCK, BLOCK), F32), jax.ShapeDtypeStruct((BLOCK, LANES), F32)],
        scratch_shapes=[pltpu.VMEM((BLOCK, SGU_WIDTH), F32), pltpu.VMEM((SGU_GROUPS, BLOCK, BLOCK), F32),
                        pltpu.VMEM((BLOCK, SGU_WIDTH), F32)],
        compiler_params=_params(("arbitrary",)),
    )(proj, dy, gain.reshape(1, SGU_WIDTH), w_s, bias_full)
    return dsuv, dg.reshape(SGU_WIDTH), dw, db[:, :SGU_GROUPS].T


def _merge_fwd(y_att, y_sgu, w_oa, w_ob, proj, *, name, tm=512, tn=512):
    T = y_att.shape[0]

    def body(ya_ref, ys_ref, wa_ref, wb_ref, ga_ref, gb_ref, o_ref):
        pa = _dot_nn(ya_ref[...], wa_ref[...])
        pb = _dot_nn(ys_ref[...], wb_ref[...])
        o_ref[...] = (_sigmoid(ga_ref[...]) * pa + _sigmoid(gb_ref[...]) * pb).astype(BF16)

    act = pl.BlockSpec((tm, ATT_WIDTH), lambda i, j: (i, 0))
    wgt = pl.BlockSpec((ATT_WIDTH, tn), lambda i, j: (0, j))
    return pl.pallas_call(
        body, name=name, grid=(T // tm, D_MODEL // tn),
        in_specs=[act, act, wgt, wgt,
                  pl.BlockSpec((tm, tn), lambda i, j: (i, j + COL_GA // tn)),
                  pl.BlockSpec((tm, tn), lambda i, j: (i, j + COL_GB // tn))],
        out_specs=pl.BlockSpec((tm, tn), lambda i, j: (i, j)),
        out_shape=jax.ShapeDtypeStruct((T, D_MODEL), BF16),
        compiler_params=_params(("parallel", "parallel")),
    )(y_att, y_sgu, w_oa, w_ob, proj, proj)


def _merge_bwd(dx1_bf, w_out, y_att, y_sgu, w_oa, w_ob, proj, *, name, tm=512, tn=512):
    T = y_att.shape[0]

    def body(dx_ref, wo_ref, ya_ref, ys_ref, wa_ref, wb_ref, ga_ref, gb_ref, dpa_ref, dpb_ref, dga_ref, dgb_ref):
        dm = _dot_nt(dx_ref[...], wo_ref[...])
        pa = _dot_nn(ya_ref[...], wa_ref[...])
        pb = _dot_nn(ys_ref[...], wb_ref[...])
        sa = _sigmoid(ga_ref[...])
        sb = _sigmoid(gb_ref[...])
        dpa_ref[...] = (dm * sa).astype(BF16)
        dpb_ref[...] = (dm * sb).astype(BF16)
        dga_ref[...] = (dm * pa * sa * (1.0 - sa)).astype(BF16)
        dgb_ref[...] = (dm * pb * sb * (1.0 - sb)).astype(BF16)

    act = pl.BlockSpec((tm, ATT_WIDTH), lambda i, j: (i, 0))
    wgt = pl.BlockSpec((ATT_WIDTH, tn), lambda i, j: (0, j))
    out = pl.BlockSpec((tm, tn), lambda i, j: (i, j))
    return pl.pallas_call(
        body, name=name, grid=(T // tm, D_MODEL // tn),
        in_specs=[pl.BlockSpec((tm, D_MODEL), lambda i, j: (i, 0)),
                  pl.BlockSpec((tn, D_MODEL), lambda i, j: (j, 0)),
                  act, act, wgt, wgt,
                  pl.BlockSpec((tm, tn), lambda i, j: (i, j + COL_GA // tn)),
                  pl.BlockSpec((tm, tn), lambda i, j: (i, j + COL_GB // tn))],
        out_specs=[out] * 4,
        out_shape=[jax.ShapeDtypeStruct((T, D_MODEL), BF16)] * 4,
        compiler_params=_params(("parallel", "parallel")),
    )(dx1_bf, w_out, y_att, y_sgu, w_oa, w_ob, proj, proj)


CONV_ROWS = 256
CONV_TN = 256
SUBLANES = 8


def _shift_rows(cur, prev8, k):
    rolled = pltpu.roll(cur, k, axis=0)
    head = jnp.where(lax.broadcasted_iota(jnp.int32, prev8.shape, 0) < k, pltpu.roll(prev8, k, axis=0), rolled[:SUBLANES])
    return jnp.concatenate([head, rolled[SUBLANES:]], axis=0)


def _shift_rows_up(cur, next8, k):
    n = cur.shape[0]
    rolled = pltpu.roll(cur, n - k, axis=0)
    tail = jnp.where(lax.broadcasted_iota(jnp.int32, next8.shape, 0) >= SUBLANES - k,
                     pltpu.roll(next8, SUBLANES - k, axis=0), rolled[n - SUBLANES:])
    return jnp.concatenate([rolled[:n - SUBLANES], tail], axis=0)


def _conv_rows(z_ref, r0, first, w_ref, b_ref, rows):
    cur = z_ref[pl.ds(r0, rows), :]
    rp = pl.multiple_of(jnp.maximum(r0 - SUBLANES, 0), SUBLANES)
    prev8 = jnp.where(first, 0.0, z_ref[pl.ds(rp, SUBLANES), :])
    z1 = _shift_rows(cur, prev8, 1)
    z2 = _shift_rows(cur, prev8, 2)
    return b_ref[...] + w_ref[0:1, :] * z2 + w_ref[1:2, :] * z1 + w_ref[2:3, :] * cur


def _conv_fwd(z_g, z_v, cw_g, cw_v, cb_g, cb_v, *, n_seq, seq, name):
    T = n_seq * seq
    tn, rows = CONV_TN, CONV_ROWS

    def body(zg_ref, zv_ref, wg_ref, wv_ref, bg_ref, bv_ref, a_ref):
        def step(s, carry):
            r0 = pl.multiple_of(s * rows, rows)
            first = s == 0
            g = _conv_rows(zg_ref, r0, first, wg_ref, bg_ref, rows)
            v = _conv_rows(zv_ref, r0, first, wv_ref, bv_ref, rows)
            a_ref[pl.ds(r0, rows), :] = (g * _sigmoid(g) * v).astype(BF16)
            return carry

        lax.fori_loop(0, seq // rows, step, 0)

    zs = pl.BlockSpec((seq, tn), lambda b, j: (b, j))
    ws = pl.BlockSpec((3, tn), lambda b, j: (0, j))
    bs = pl.BlockSpec((1, tn), lambda b, j: (0, j))
    return pl.pallas_call(
        body, name=name, grid=(n_seq, D_FF // tn),
        in_specs=[zs, zs, ws, ws, bs, bs], out_specs=zs,
        out_shape=jax.ShapeDtypeStruct((T, D_FF), BF16),
        compiler_params=_params(("parallel", "parallel")),
    )(z_g, z_v, cw_g, cw_v, cb_g.reshape(1, D_FF), cb_v.reshape(1, D_FF))


def _conv_bwd(z_g, z_v, da, cw_g, cw_v, cb_g, cb_v, *, n_seq, seq, name):
    T = n_seq * seq
    tn, rows = CONV_TN, CONV_ROWS
    n_steps = seq // rows

    def body(zg_ref, zv_ref, da_ref, wg_ref, wv_ref, bg_ref, bv_ref,
             dzg_ref, dzv_ref, dwg_ref, dwv_ref, dbg_ref, dbv_ref, dcg_ref, dcv_ref):
        def grads(s, carry):
            r0 = pl.multiple_of(s * rows, rows)
            first = s == 0
            accs = carry
            cur_g = zg_ref[pl.ds(r0, rows), :]
            cur_v = zv_ref[pl.ds(r0, rows), :]
            rp = pl.multiple_of(jnp.maximum(r0 - SUBLANES, 0), SUBLANES)
            pg = jnp.where(first, 0.0, zg_ref[pl.ds(rp, SUBLANES), :])
            pv = jnp.where(first, 0.0, zv_ref[pl.ds(rp, SUBLANES), :])
            g1, g2 = _shift_rows(cur_g, pg, 1), _shift_rows(cur_g, pg, 2)
            v1, v2 = _shift_rows(cur_v, pv, 1), _shift_rows(cur_v, pv, 2)
            g = bg_ref[...] + wg_ref[0:1, :] * g2 + wg_ref[1:2, :] * g1 + wg_ref[2:3, :] * cur_g
            v = bv_ref[...] + wv_ref[0:1, :] * v2 + wv_ref[1:2, :] * v1 + wv_ref[2:3, :] * cur_v
            sg = _sigmoid(g)
            dav = da_ref[pl.ds(r0, rows), :]
            dcg = dav * v * (sg * (1.0 + g * (1.0 - sg)))
            dcv = dav * (g * sg)
            dcg_ref[pl.ds(r0, rows), :] = dcg
            dcv_ref[pl.ds(r0, rows), :] = dcv

            def colsum(x):
                return jnp.sum(x, axis=0, keepdims=True)

            new = (accs[0] + colsum(dcg * g2), accs[1] + colsum(dcg * g1), accs[2] + colsum(dcg * cur_g), accs[3] + colsum(dcg),
                   accs[4] + colsum(dcv * v2), accs[5] + colsum(dcv * v1), accs[6] + colsum(dcv * cur_v), accs[7] + colsum(dcv))
            return new

        zero = jnp.zeros((1, tn), F32)
        sums = lax.fori_loop(0, n_steps, grads, (zero,) * 8)
        first_seq = pl.program_id(1) == 0

        @pl.when(first_seq)
        def _():
            dwg_ref[...] = jnp.concatenate(sums[0:3], axis=0)
            dbg_ref[...] = sums[3]
            dwv_ref[...] = jnp.concatenate(sums[4:7], axis=0)
            dbv_ref[...] = sums[7]

        @pl.when(jnp.logical_not(first_seq))
        def _():
            dwg_ref[...] += jnp.concatenate(sums[0:3], axis=0)
            dbg_ref[...] += sums[3]
            dwv_ref[...] += jnp.concatenate(sums[4:7], axis=0)
            dbv_ref[...] += sums[7]

        def back(s, carry):
            r0 = pl.multiple_of(s * rows, rows)
            last = s == n_steps - 1
            rn = pl.multiple_of(jnp.minimum(r0 + rows, seq - SUBLANES), SUBLANES)
            for dc_ref, w_ref, dz_ref in ((dcg_ref, wg_ref, dzg_ref), (dcv_ref, wv_ref, dzv_ref)):
                cur = dc_ref[pl.ds(r0, rows), :]
                nxt = jnp.where(last, 0.0, dc_ref[pl.ds(rn, SUBLANES), :])
                u1, u2 = _shift_rows_up(cur, nxt, 1), _shift_rows_up(cur, nxt, 2)
                dz_ref[pl.ds(r0, rows), :] = (w_ref[2:3, :] * cur + w_ref[1:2, :] * u1 + w_ref[0:1, :] * u2).astype(BF16)
            return carry

        lax.fori_loop(0, n_steps, back, 0)

    zs = pl.BlockSpec((seq, tn), lambda j, b: (b, j))
    ws = pl.BlockSpec((3, tn), lambda j, b: (0, j))
    bs = pl.BlockSpec((1, tn), lambda j, b: (0, j))
    outs = pl.pallas_call(
        body, name=name, grid=(D_FF // tn, n_seq),
        in_specs=[zs, zs, zs, ws, ws, bs, bs],
        out_specs=[zs, zs, ws, ws, bs, bs],
        out_shape=[jax.ShapeDtypeStruct((T, D_FF), BF16)] * 2 + [jax.ShapeDtypeStruct((3, D_FF), F32)] * 2
        + [jax.ShapeDtypeStruct((1, D_FF), F32)] * 2,
        scratch_shapes=[pltpu.VMEM((seq, tn), F32), pltpu.VMEM((seq, tn), F32)],
        compiler_params=_params(("parallel", "arbitrary")),
    )(z_g, z_v, da, cw_g, cw_v, cb_g.reshape(1, D_FF), cb_v.reshape(1, D_FF))
    dz_g, dz_v, dw_g, dw_v, db_g, db_v = outs
    return dz_g, dz_v, dw_g, dw_v, db_g.reshape(D_FF), db_v.reshape(D_FF)


def _loss_head(y, target, *, name, tm=512):
    T, D = y.shape

    def body(y_ref, t_ref, dy_ref, dyb_ref, l_ref):
        err = y_ref[...] - t_ref[...]
        dyv = err * (1.0 / D)
        dy_ref[...] = dyv
        dyb_ref[...] = dyv.astype(BF16)
        part = jnp.sum(jnp.sum(err * err, axis=0, keepdims=True), axis=1, keepdims=True) * (0.5 / D)

        @pl.when(pl.program_id(0) == 0)
        def _():
            l_ref[...] = jnp.broadcast_to(part, l_ref.shape)

        @pl.when(pl.program_id(0) > 0)
        def _():
            l_ref[...] += jnp.broadcast_to(part, l_ref.shape)

    row = pl.BlockSpec((tm, D), lambda i: (i, 0))
    dy, dyb, l = pl.pallas_call(
        body, name=name, grid=(T // tm,),
        in_specs=[row, row],
        out_specs=[row, row, pl.BlockSpec((SUBLANES, LANES), lambda i: (0, 0))],
        out_shape=[jax.ShapeDtypeStruct((T, D), F32), jax.ShapeDtypeStruct((T, D), BF16),
                   jax.ShapeDtypeStruct((SUBLANES, LANES), F32)],
        compiler_params=_params(("arbitrary",)),
    )(y, target)
    return l[0, 0], dy, dyb


def _cast_bf16(x, *, name, tm=512):
    T, D = x.shape

    def body(x_ref, o_ref):
        o_ref[...] = x_ref[...].astype(BF16)

    row = pl.BlockSpec((tm, D), lambda i: (i, 0))
    return pl.pallas_call(body, name=name, grid=(T // tm,), in_specs=[row], out_specs=row,
                          out_shape=jax.ShapeDtypeStruct((T, D), BF16), compiler_params=_params(("parallel",)))(x)


def _layer_fwd(x, w, *, n_seq, seq, l):
    tag = f"l{l}"
    h = _rms_fwd(x, w["mix_norm"], name=f"{tag}_mix_norm")
    proj = _mm(h, w["w_in"], mode="nn", out_dtype=F32, tm=512, tn=768, tk=D_MODEL, name=f"{tag}_proj")
    y_att = _attention_fwd(proj, w["q_norm"], w["k_norm"], w["sinks"], n_seq=n_seq, seq=seq, name=f"{tag}_att")
    y_sgu = _sgu_fwd(proj, w["sgu_norm"], w["w_s"], w["bias_full"], n_seq=n_seq, seq=seq, name=f"{tag}_sgu")
    merged = _merge_fwd(y_att, y_sgu, w["w_oa"], w["w_ob"], proj, name=f"{tag}_merge")
    x1 = _mm(merged, w["w_out"], mode="nn", out_dtype=F32, tm=512, tn=1024, tk=D_MODEL, name=f"{tag}_out",
             epilogue=_add, extras=(x,))
    h2 = _rms_fwd(x1, w["ffn_norm"], name=f"{tag}_ffn_norm")
    z_g = _mm(h2, w["w_up_g"], mode="nn", out_dtype=F32, tm=512, tn=1408, tk=D_MODEL, name=f"{tag}_up_g")
    z_v = _mm(h2, w["w_up_v"], mode="nn", out_dtype=F32, tm=512, tn=1408, tk=D_MODEL, name=f"{tag}_up_v")
    a = _conv_fwd(z_g, z_v, w["cw_g"], w["cw_v"], w["cb_g"], w["cb_v"], n_seq=n_seq, seq=seq, name=f"{tag}_conv")
    x2 = _mm(a, w["w_down"], mode="nn", out_dtype=F32, tm=512, tn=1024, tk=D_FF, name=f"{tag}_down",
             epilogue=_add, extras=(x1,))
    saved = dict(x=x, h=h, proj=proj, y_att=y_att, y_sgu=y_sgu, merged=merged, x1=x1, h2=h2, z_g=z_g, z_v=z_v, a=a)
    return x2, saved


def _layer_bwd(dx2, dx2_bf, w, s, *, n_seq, seq, l):
    tag = f"l{l}b"
    g = {}
    da = _mm(dx2_bf, w["w_down"], mode="nt", out_dtype=F32, tm=512, tn=1408, tk=D_MODEL, name=f"{tag}_da")
    g["w_down"] = _mm(s["a"], dx2_bf, mode="tn", out_dtype=F32, tm=1408, tn=1024, tk=512, name=f"{tag}_dw_down")
    dz_g, dz_v, g["cw_g"], g["cw_v"], g["cb_g"], g["cb_v"] = _conv_bwd(
        s["z_g"], s["z_v"], da, w["cw_g"], w["cw_v"], w["cb_g"], w["cb_v"], n_seq=n_seq, seq=seq, name=f"{tag}_conv")
    dh2 = _mm(dz_g, w["w_up_g"], mode="nt", out_dtype=F32, tm=512, tn=1024, tk=1408, name=f"{tag}_dh2_g")
    dh2 = _mm(dz_v, w["w_up_v"], mode="nt", out_dtype=F32, tm=512, tn=1024, tk=1408, name=f"{tag}_dh2_v",
              epilogue=_add, extras=(dh2,))
    g["w_up_g"] = _mm(s["h2"], dz_g, mode="tn", out_dtype=F32, tm=1024, tn=1408, tk=512, name=f"{tag}_dw_up_g")
    g["w_up_v"] = _mm(s["h2"], dz_v, mode="tn", out_dtype=F32, tm=1024, tn=1408, tk=512, name=f"{tag}_dw_up_v")
    dx1, g["ffn_norm"] = _rms_bwd(s["x1"], w["ffn_norm"], dh2, dx2, name=f"{tag}_ffn_norm")
    dx1_bf = _cast_bf16(dx1, name=f"{tag}_dx1_bf")
    dpa, dpb, dga, dgb = _merge_bwd(dx1_bf, w["w_out"], s["y_att"], s["y_sgu"], w["w_oa"], w["w_ob"], s["proj"],
                                    name=f"{tag}_merge")
    g["w_out"] = _mm(s["merged"], dx1_bf, mode="tn", out_dtype=F32, tm=1024, tn=1024, tk=512, name=f"{tag}_dw_out")
    dy_att = _mm(dpa, w["w_oa"], mode="nt", out_dtype=BF16, tm=512, tn=512, tk=D_MODEL, name=f"{tag}_dy_att")
    dy_sgu = _mm(dpb, w["w_ob"], mode="nt", out_dtype=F32, tm=512, tn=512, tk=D_MODEL, name=f"{tag}_dy_sgu")
    g["w_oa"] = _mm(s["y_att"], dpa, mode="tn", out_dtype=F32, tm=512, tn=1024, tk=512, name=f"{tag}_dw_oa")
    g["w_ob"] = _mm(s["y_sgu"], dpb, mode="tn", out_dtype=F32, tm=512, tn=1024, tk=512, name=f"{tag}_dw_ob")
    dqkv, g["q_norm"], g["k_norm"], g["sinks"] = _attention_bwd(
        s["proj"], dy_att, w["q_norm"], w["k_norm"], w["sinks"], n_seq=n_seq, seq=seq, name=f"{tag}_att")
    dsuv, g["sgu_norm"], g["w_s"], g["b_s"] = _sgu_bwd(
        s["proj"], dy_sgu, w["sgu_norm"], w["w_s"], w["bias_full"], n_seq=n_seq, seq=seq, name=f"{tag}_sgu")
    dproj = jnp.concatenate([dsuv, dga, dgb, dqkv], axis=1)
    dh = _mm(dproj, w["w_in"], mode="nt", out_dtype=F32, tm=512, tn=1024, tk=1280, name=f"{tag}_dh")
    g["w_in"] = _mm(s["h"], dproj, mode="tn", out_dtype=F32, tm=1024, tn=768, tk=512, name=f"{tag}_dw_in")
    dx, g["mix_norm"] = _rms_bwd(s["x"], w["mix_norm"], dh, dx1, name=f"{tag}_mix_norm")
    return dx, g


def _local_step(x, target, weights, *, n_seq, seq):
    depth = len(weights)
    saved = []
    h = x
    for l in range(depth):
        h, s = _layer_fwd(h, weights[l], n_seq=n_seq, seq=seq, l=l)
        saved.append(s)
    loss, dy, dy_bf = _loss_head(h, target, name="loss_head")
    grads = [None] * depth
    for l in reversed(range(depth)):
        if l < depth - 1:
            dy_bf = _cast_bf16(dy, name=f"l{l}b_dx2_bf")
        dy, grads[l] = _layer_bwd(dy, dy_bf, weights[l], saved[l], n_seq=n_seq, seq=seq, l=l)
    return loss, dy, grads


def _prep_layer(full):
    w = dict(full)
    w["w_in"] = jnp.concatenate([full["w_in"][:, QKV_WIDTH:], full["w_in"][:, :QKV_WIDTH]], axis=1)
    w["w_up_g"], w["w_up_v"] = full["w_up"][:, :D_FF], full["w_up"][:, D_FF:]
    w["cw_g"], w["cw_v"] = full["conv_w"][:, :D_FF], full["conv_w"][:, D_FF:]
    w["cb_g"], w["cb_v"] = full["conv_b"][:D_FF], full["conv_b"][D_FF:]
    w["bias_full"] = jnp.repeat(full["b_s"].T, SGU_WIDTH // SGU_GROUPS, axis=1)
    return w


def _unprep_grads(g):
    out = {k: g[k] for k in ("mix_norm", "q_norm", "k_norm", "sinks", "sgu_norm", "w_s", "b_s", "w_oa", "w_ob", "w_out",
                             "ffn_norm", "w_down")}
    out["w_in"] = jnp.concatenate([g["w_in"][:, REST_WIDTH:], g["w_in"][:, :REST_WIDTH]], axis=1)
    out["w_up"] = jnp.concatenate([g["w_up_g"], g["w_up_v"]], axis=1)
    out["conv_w"] = jnp.concatenate([g["cw_g"], g["cw_v"]], axis=1)
    out["conv_b"] = jnp.concatenate([g["cb_g"], g["cb_v"]], axis=0)
    return out


ANY = pl.BlockSpec(memory_space=pl.ANY)
N_CHIPS = 4


def _my_place():
    return lax.axis_index("x"), lax.axis_index("y"), lax.axis_index("c")


def _all_gather(shard, *, name):
    R, C = shard.shape

    def body(x_ref, out_ref, send_sems, recv_sems, local_sem):
        x, y, c = _my_place()
        me, sibling = (x, y, c), (x, y, 1 - c)
        chips = [(1 - x, y), (x, 1 - y), (1 - x, 1 - y)]

        def rows(px, py, pc):
            return out_ref.at[4 * px + 2 * py + pc]

        def copy(k, block, to, src=None):
            return pltpu.make_async_remote_copy(
                src_ref=rows(*block) if src is None else src, dst_ref=rows(*block),
                send_sem=send_sems.at[k], recv_sem=recv_sems.at[k], device_id=to, device_id_type=MESH)

        mine = pltpu.make_async_copy(x_ref, rows(*me), local_sem)
        mine.start()
        first = [copy(0, me, sibling, src=x_ref)]
        first += [copy(1 + j, me, (*chip, c), src=x_ref) for j, chip in enumerate(chips)]
        for cp in first:
            cp.start()
        passed = [copy(4 + j, (*chip, c), sibling) for j, chip in enumerate(chips)]
        for j, chip in enumerate(chips):
            copy(1 + j, (*chip, c), me).wait_recv()
            passed[j].start()
        copy(0, sibling, me).wait_recv()
        for j, chip in enumerate(chips):
            copy(4 + j, (*chip, 1 - c), me).wait_recv()
        for cp in first + passed:
            cp.wait_send()
        mine.wait()

    return pl.pallas_call(
        body, name=name,
        out_shape=jax.ShapeDtypeStruct((N_DEV, R, C), shard.dtype),
        in_specs=[ANY], out_specs=ANY,
        scratch_shapes=[pltpu.SemaphoreType.DMA((7,)), pltpu.SemaphoreType.DMA((7,)), pltpu.SemaphoreType.DMA],
        compiler_params=pltpu.CompilerParams(has_side_effects=True),
    )(shard)


def _rs_pair_exchange(g4, *, name):
    _, _, R, C = g4.shape

    def body(g_ref, recv_ref, send_sems, recv_sems):
        x, y, c = _my_place()
        copies = [pltpu.make_async_remote_copy(
            src_ref=g_ref.at[k, 1 - c], dst_ref=recv_ref.at[k], send_sem=send_sems.at[k], recv_sem=recv_sems.at[k],
            device_id=(x, y, 1 - c), device_id_type=MESH) for k in range(N_CHIPS)]
        for cp in copies:
            cp.start()
        for cp in copies:
            cp.wait_recv()
        for cp in copies:
            cp.wait_send()

    return pl.pallas_call(
        body, name=name,
        out_shape=jax.ShapeDtypeStruct((N_CHIPS, R, C), g4.dtype),
        in_specs=[ANY], out_specs=ANY,
        scratch_shapes=[pltpu.SemaphoreType.DMA((N_CHIPS,)), pltpu.SemaphoreType.DMA((N_CHIPS,))],
        compiler_params=pltpu.CompilerParams(has_side_effects=True),
    )(g4)


def _rs_chip_exchange(h, *, name):
    _, R, C = h.shape

    def body(h_ref, recv_ref, send_sems, recv_sems):
        x, y, c = _my_place()
        chips = [(1 - x, y), (x, 1 - y), (1 - x, 1 - y)]
        copies = [pltpu.make_async_remote_copy(
            src_ref=h_ref.at[2 * px + py], dst_ref=recv_ref.at[k], send_sem=send_sems.at[k], recv_sem=recv_sems.at[k],
            device_id=(px, py, c), device_id_type=MESH) for k, (px, py) in enumerate(chips)]
        for cp in copies:
            cp.start()
        for cp in copies:
            cp.wait_recv()
        for cp in copies:
            cp.wait_send()

    return pl.pallas_call(
        body, name=name,
        out_shape=jax.ShapeDtypeStruct((N_CHIPS - 1, R, C), h.dtype),
        in_specs=[ANY], out_specs=ANY,
        scratch_shapes=[pltpu.SemaphoreType.DMA((N_CHIPS - 1,)), pltpu.SemaphoreType.DMA((N_CHIPS - 1,))],
        compiler_params=pltpu.CompilerParams(has_side_effects=True),
    )(h)


def _rs_pair_sum(g4, recv, core, *, name, tr):
    _, _, R, C = g4.shape

    def body(core_ref, g_ref, r_ref, o_ref):
        o_ref[...] = g_ref[...] + r_ref[...]

    return pl.pallas_call(
        body, name=name,
        grid_spec=pltpu.PrefetchScalarGridSpec(
            num_scalar_prefetch=1, grid=(N_CHIPS, R // tr),
            in_specs=[pl.BlockSpec((None, None, tr, C), lambda k, i, core_ref: (k, core_ref[0], i, 0)),
                      pl.BlockSpec((None, tr, C), lambda k, i, core_ref: (k, i, 0))],
            out_specs=pl.BlockSpec((None, tr, C), lambda k, i, core_ref: (k, i, 0))),
        out_shape=jax.ShapeDtypeStruct((N_CHIPS, R, C), F32),
        compiler_params=_params(("parallel", "parallel")),
    )(core, g4, recv)


def _rs_chip_sum(h, recv, chip, *, name, tr):
    _, R, C = h.shape

    def body(chip_ref, h_ref, r_ref, o_ref):
        o_ref[...] = ((h_ref[...] + r_ref[0]) + r_ref[1]) + r_ref[2]

    return pl.pallas_call(
        body, name=name,
        grid_spec=pltpu.PrefetchScalarGridSpec(
            num_scalar_prefetch=1, grid=(R // tr,),
            in_specs=[pl.BlockSpec((None, tr, C), lambda i, chip_ref: (chip_ref[0], i, 0)),
                      pl.BlockSpec((N_CHIPS - 1, tr, C), lambda i, chip_ref: (0, i, 0))],
            out_specs=pl.BlockSpec((tr, C), lambda i, chip_ref: (i, 0))),
        out_shape=jax.ShapeDtypeStruct((R, C), F32),
        compiler_params=_params(("parallel",)),
    )(chip, h, recv)


DEPTH = 2
SHARDED = (("w_in", (D_MODEL, IN_WIDTH), 1), ("w_oa", (ATT_WIDTH, D_MODEL), 1), ("w_ob", (SGU_WIDTH, D_MODEL), 1),
           ("w_out", (D_MODEL, D_MODEL), 0), ("w_up", (D_MODEL, 2 * D_FF), 1), ("conv_w", (3, 2 * D_FF), 1),
           ("w_down", (D_FF, D_MODEL), 0))
REPLICATED = (("mix_norm", (D_MODEL,)), ("q_norm", (HEAD_DIM,)), ("k_norm", (HEAD_DIM,)), ("sinks", (N_Q_HEADS,)),
              ("sgu_norm", (SGU_WIDTH,)), ("w_s", (SGU_GROUPS, BLOCK, BLOCK)), ("b_s", (SGU_GROUPS, BLOCK)),
              ("ffn_norm", (D_MODEL,)), ("conv_b", (2 * D_FF,)))
WEIGHT_ORDER = ("mix_norm", "w_in", "q_norm", "k_norm", "sinks", "sgu_norm", "w_s", "b_s", "w_oa", "w_ob", "w_out",
                "ffn_norm", "w_up", "conv_w", "conv_b", "w_down")
PACK_GRAIN = LANES * 256


def _round_up(n, m):
    return -(-n // m) * m


def _shard_shape(shape, axis):
    return tuple(d // N_DEV if a == axis else d for a, d in enumerate(shape))


def _segments(per_elem):
    segs, off = {}, 0
    for l in range(DEPTH):
        for name, shape, axis in SHARDED:
            n = math.prod(_shard_shape(shape, axis)) * (per_elem if name == "conv_w" else 1)
            segs[(l, name)] = (off, n)
            off += _round_up(n, LANES)
    return segs, off


def _pack_weight_shards(shards):
    segs, total = _segments(2)
    parts = []
    for (l, name), (off, n) in segs.items():
        v = shards[name][l]
        v = lax.bitcast_convert_type(v, BF16) if name == "conv_w" else v.astype(BF16)
        v = v.reshape(-1)
        parts.append(jnp.pad(v, (0, _round_up(n, LANES) - n)))
    padded = _round_up(total, PACK_GRAIN)
    parts.append(jnp.zeros((padded - total,), BF16))
    return jnp.concatenate(parts).reshape(padded // LANES, LANES)


def _unpack_weights(gathered):
    segs, _ = _segments(2)
    flat = gathered.reshape(N_DEV, -1)
    out = [dict() for _ in range(DEPTH)]
    for name, shape, axis in SHARDED:
        r, c = _shard_shape(shape, axis)
        for l in range(DEPTH):
            off, n = segs[(l, name)]
            v = flat[:, off:off + n]
            if name == "conv_w":
                v = lax.bitcast_convert_type(v.reshape(N_DEV, r, c, 2), F32)
            else:
                v = v.reshape(N_DEV, r, c)
            out[l][name] = v.reshape(N_DEV * r, c) if axis == 0 else v.transpose(1, 0, 2).reshape(r, N_DEV * c)
    return out


def _small_layout():
    segs, off = {}, 0
    for l in range(DEPTH):
        for name, shape in REPLICATED:
            n = math.prod(shape)
            segs[(l, name)] = (off, n)
            off += n
    per_dev = _round_up(-(-off // N_DEV), SUBLANES * LANES)
    return segs, off, per_dev


def _pack_grads(grads):
    segs, total = _segments(1)
    ssegs, stotal, per_dev = _small_layout()
    parts = []
    for (l, name), (off, n) in segs.items():
        shape, axis = next((s, a) for nm, s, a in SHARDED if nm == name)
        r, c = _shard_shape(shape, axis)
        g = grads[l][name]
        g = g.reshape(N_DEV, r * c) if axis == 0 else g.reshape(r, N_DEV, c).transpose(1, 0, 2).reshape(N_DEV, r * c)
        parts.append(jnp.pad(g, ((0, 0), (0, _round_up(n, LANES) - n))))
    small = jnp.concatenate([grads[l][name].reshape(-1) for (l, name) in ssegs])
    parts.append(jnp.pad(small, (0, N_DEV * per_dev - stotal)).reshape(N_DEV, per_dev))
    padded = _round_up(total + per_dev, PACK_GRAIN)
    parts.append(jnp.zeros((N_DEV, padded - total - per_dev), F32))
    return jnp.concatenate(parts, axis=1).reshape(N_DEV, padded // LANES, LANES)


def _unpack_grad_shard(shard):
    segs, total = _segments(1)
    _, _, per_dev = _small_layout()
    flat = shard.reshape(-1)
    out = {}
    for name, shape, axis in SHARDED:
        out[name] = jnp.stack([flat[segs[(l, name)][0]:segs[(l, name)][0] + segs[(l, name)][1]].reshape(_shard_shape(shape, axis))
                               for l in range(DEPTH)])
    return out, flat[total:total + per_dev].reshape(per_dev // LANES, LANES)


def _unpack_small(gathered):
    ssegs, _, _ = _small_layout()
    flat = gathered.reshape(-1)
    shapes = dict(REPLICATED)
    return {name: jnp.stack([flat[ssegs[(l, name)][0]:ssegs[(l, name)][0] + ssegs[(l, name)][1]].reshape(shapes[name])
                             for l in range(DEPTH)]) for name, _ in REPLICATED}


ADAMW_BLOCK_BYTES = 1 << 20


def _adamw(w, g, m, v, *, name):
    shape = w.shape
    C = shape[-1]
    R = math.prod(shape[:-1])
    tr = R
    while tr * C * 4 > ADAMW_BLOCK_BYTES and tr % 2 == 0 and (tr // 2) % SUBLANES == 0:
        tr //= 2

    def body(w_ref, g_ref, m_ref, v_ref, d_ref, nm_ref, nv_ref):
        gv = g_ref[...]
        nm = ADAM_B1 * m_ref[...] + (1.0 - ADAM_B1) * gv
        nv = ADAM_B2 * v_ref[...] + (1.0 - ADAM_B2) * (gv * gv)
        m_hat = nm / (1.0 - ADAM_B1 ** ADAM_STEP)
        v_hat = nv / (1.0 - ADAM_B2 ** ADAM_STEP)
        d_ref[...] = -ADAM_LR * (m_hat / (jnp.sqrt(v_hat) + ADAM_EPS) + ADAM_WD * w_ref[...])
        nm_ref[...] = nm
        nv_ref[...] = nv

    spec = pl.BlockSpec((tr, C), lambda i: (i, 0))
    outs = pl.pallas_call(
        body, name=name, grid=(R // tr,),
        in_specs=[spec] * 4, out_specs=[spec] * 3,
        out_shape=[jax.ShapeDtypeStruct((R, C), F32)] * 3,
        compiler_params=_params(("parallel",)),
    )(*[a.reshape(R, C) for a in (w, g, m, v)])
    return tuple(o.reshape(shape) for o in outs)


RS_ROW_TILE = 19 * 256


def kernel(x, mix_norm, w_in, q_norm, k_norm, sinks, sgu_norm, w_s, b_s, w_oa, w_ob, w_out, ffn_norm, w_up, conv_w, conv_b, w_down, loss_target, m_mix_norm, m_w_in, m_q_norm, m_k_norm, m_sinks, m_sgu_norm, m_w_s, m_b_s, m_w_oa, m_w_ob, m_w_out, m_ffn_norm, m_w_up, m_conv_w, m_conv_b, m_w_down, v_mix_norm, v_w_in, v_q_norm, v_k_norm, v_sinks, v_sgu_norm, v_w_s, v_b_s, v_w_oa, v_w_ob, v_w_out, v_ffn_norm, v_w_up, v_conv_w, v_conv_b, v_w_down):
    W = dict(mix_norm=mix_norm, w_in=w_in, q_norm=q_norm, k_norm=k_norm, sinks=sinks, sgu_norm=sgu_norm, w_s=w_s, b_s=b_s,
             w_oa=w_oa, w_ob=w_ob, w_out=w_out, ffn_norm=ffn_norm, w_up=w_up, conv_w=conv_w, conv_b=conv_b, w_down=w_down)
    M = dict(mix_norm=m_mix_norm, w_in=m_w_in, q_norm=m_q_norm, k_norm=m_k_norm, sinks=m_sinks, sgu_norm=m_sgu_norm,
             w_s=m_w_s, b_s=m_b_s, w_oa=m_w_oa, w_ob=m_w_ob, w_out=m_w_out, ffn_norm=m_ffn_norm, w_up=m_w_up,
             conv_w=m_conv_w, conv_b=m_conv_b, w_down=m_w_down)
    V = dict(mix_norm=v_mix_norm, w_in=v_w_in, q_norm=v_q_norm, k_norm=v_k_norm, sinks=v_sinks, sgu_norm=v_sgu_norm,
             w_s=v_w_s, b_s=v_b_s, w_oa=v_w_oa, w_ob=v_w_ob, w_out=v_w_out, ffn_norm=v_ffn_norm, w_up=v_w_up,
             conv_w=v_conv_w, conv_b=v_conv_b, w_down=v_w_down)
    n_seq, seq, d_model = x.shape
    tokens = n_seq * seq
    mx, my, mc = _my_place()

    gathered = _all_gather(_pack_weight_shards(W), name="gather_weights")
    fulls = _unpack_weights(gathered)
    weights = [_prep_layer({**fulls[l], **{name: W[name][l] for name, _ in REPLICATED}}) for l in range(DEPTH)]

    loss_part, dx, grads = _local_step(x.reshape(tokens, d_model), loss_target.reshape(tokens, d_model), weights,
                                       n_seq=n_seq, seq=seq)
    loss = lax.psum(loss_part, ("x", "y", "c"))

    packed = _pack_grads([_unprep_grads(g) for g in grads])
    _, rows, _ = packed.shape
    g4 = packed.reshape(N_CHIPS, 2, rows, LANES)
    from_sibling = _rs_pair_exchange(g4, name="reduce_pair_exchange")
    pair_sum = _rs_pair_sum(g4, from_sibling, mc.reshape(1).astype(jnp.int32), name="reduce_pair_sum", tr=RS_ROW_TILE)
    from_chips = _rs_chip_exchange(pair_sum, name="reduce_chip_exchange")
    reduced = _rs_chip_sum(pair_sum, from_chips, (2 * mx + my).reshape(1).astype(jnp.int32), name="reduce_chip_sum",
                           tr=RS_ROW_TILE)
    G, small_part = _unpack_grad_shard(reduced)
    G.update(_unpack_small(_all_gather(small_part, name="gather_small_grads")))

    delta, new_m, new_v = {}, {}, {}
    for name in WEIGHT_ORDER:
        delta[name], new_m[name], new_v[name] = _adamw(W[name], G[name], M[name], V[name], name=f"adamw_{name}")
    return (loss, dx.reshape(n_seq, seq, d_model), *[G[n] for n in WEIGHT_ORDER], *[delta[n] for n in WEIGHT_ORDER],
            *[new_m[n] for n in WEIGHT_ORDER], *[new_v[n] for n in WEIGHT_ORDER])
```

```python
import math

import jax
import jax.numpy as jnp
from jax import lax
from jax.experimental import pallas as pl
from jax.experimental.pallas import tpu as pltpu

F32 = jnp.float32
BF16 = jnp.bfloat16
MESH = pl.DeviceIdType.MESH

DEPTH = 2
D_MODEL = 1024
N_Q_HEADS = 8
HEAD_DIM = 64
ATT_WIDTH = 512
KV_WIDTH = 128
BLOCK = 128
SGU_WIDTH = 512
SGU_GROUPS = 8
IN_WIDTH = 3840
D_FF = 2816
NORM_EPS = 1e-6
NEG_INF = -1e30
ATT_SCALE = HEAD_DIM ** -0.5
ALIBI_SLOPES = tuple(2.0 ** (-(h + 1)) for h in range(N_Q_HEADS))
ADAM_LR, ADAM_B1, ADAM_B2, ADAM_EPS, ADAM_WD, ADAM_STEP = 0.001, 0.9, 0.999, 1e-08, 0.01, 10
N_DEV = 8
N_CHIPS = 4

QKV_WIDTH = ATT_WIDTH + 2 * KV_WIDTH
REST_WIDTH = IN_WIDTH - QKV_WIDTH
COL_SUV, COL_GA, COL_GB, COL_QKV = 0, 1024, 2048, 3072

LANES = 128
SUBLANES = 8
VMEM_LIMIT_V7X = 56 * 1024 * 1024
GELU_C = math.sqrt(2.0 / math.pi)
GELU_K = 0.044715
ANY = pl.BlockSpec(memory_space=pl.ANY)


def _params(sem=None):
    return pltpu.CompilerParams(dimension_semantics=sem, vmem_limit_bytes=VMEM_LIMIT_V7X)


def _sigmoid(x):
    return 1.0 / (1.0 + jnp.exp(-x))


def _gelu(x):
    th = jnp.tanh(GELU_C * (x + GELU_K * x * x * x))
    return 0.5 * x * (1.0 + th)


def _gelu_and_grad(x):
    x2 = x * x
    th = jnp.tanh(GELU_C * (x + GELU_K * x2 * x))
    g = 0.5 * x * (1.0 + th)
    dg = 0.5 * (1.0 + th) + 0.5 * x * (1.0 - th * th) * (GELU_C * (1.0 + 3.0 * GELU_K * x2))
    return g, dg


def _dot(a, b, dims):
    return lax.dot_general(a, b, (dims, ((), ())), preferred_element_type=F32)


def _dot_nn(a, b):
    return _dot(a, b, ((1,), (0,)))


def _dot_nt(a, b):
    return _dot(a, b, ((1,), (1,)))


def _dot_tn(a, b):
    return _dot(a, b, ((0,), (0,)))


def _lo_mask(shape):
    return lax.broadcasted_iota(jnp.int32, shape, len(shape) - 1) < (LANES // 2)


def _half_sums(x, lo):
    s_lo = jnp.sum(jnp.where(lo, x, 0.0), axis=-1, keepdims=True)
    s_all = jnp.sum(x, axis=-1, keepdims=True)
    return jnp.where(lo, s_lo, s_all - s_lo)


def _dup_half(x, half, lo):
    r = pltpu.roll(x, LANES // 2, axis=1)
    return jnp.where(lo, x, r) if half == 0 else jnp.where(lo, r, x)


def _mm(a, b, *, mode, out_dtype, tm, tn, tk, name, epilogue=None, extras=()):
    if mode == "nn":
        (M, K), N = a.shape, b.shape[1]
    elif mode == "nt":
        (M, K), N = a.shape, b.shape[0]
    else:
        (K, M), N = a.shape, b.shape[1]
    assert M % tm == 0 and N % tn == 0 and K % tk == 0, (name, M, N, K, tm, tn, tk)
    gm, gn, gk = M // tm, N // tn, K // tk
    if mode == "nn":
        a_spec = pl.BlockSpec((tm, tk), lambda i, j, k: (i, k))
        b_spec = pl.BlockSpec((tk, tn), lambda i, j, k: (k, j))
        contract = ((1,), (0,))
    elif mode == "nt":
        a_spec = pl.BlockSpec((tm, tk), lambda i, j, k: (i, k))
        b_spec = pl.BlockSpec((tn, tk), lambda i, j, k: (j, k))
        contract = ((1,), (1,))
    else:
        a_spec = pl.BlockSpec((tk, tm), lambda i, j, k: (k, i))
        b_spec = pl.BlockSpec((tk, tn), lambda i, j, k: (k, j))
        contract = ((0,), (0,))
    o_spec = pl.BlockSpec((tm, tn), lambda i, j, k: (i, j))
    n_extra = len(extras)

    def finish(acc, extra_refs, o_ref):
        if epilogue is not None:
            acc = epilogue(acc, *[r[...] for r in extra_refs])
        o_ref[...] = acc.astype(out_dtype)

    def body(a_ref, b_ref, *rest):
        extra_refs, o_ref = rest[:n_extra], rest[n_extra]
        part = _dot(a_ref[...].astype(BF16), b_ref[...].astype(BF16), contract)
        if gk == 1:
            finish(part, extra_refs, o_ref)
            return
        acc_ref = rest[n_extra + 1]
        k = pl.program_id(2)

        @pl.when(k == 0)
        def _():
            acc_ref[...] = part

        @pl.when(k > 0)
        def _():
            acc_ref[...] += part

        @pl.when(k == gk - 1)
        def _():
            finish(acc_ref[...], extra_refs, o_ref)

    return pl.pallas_call(
        body,
        name=name,
        grid=(gm, gn, gk),
        in_specs=[a_spec, b_spec] + [o_spec] * n_extra,
        out_specs=o_spec,
        out_shape=jax.ShapeDtypeStruct((M, N), out_dtype),
        scratch_shapes=[] if gk == 1 else [pltpu.VMEM((tm, tn), F32)],
        compiler_params=_params(("parallel", "parallel", "arbitrary")),
    )(a, b, *extras)


def _add(acc, r):
    return acc + r


def _rms_fwd(x, gain, *, name, tm=512):
    T, D = x.shape

    def body(x_ref, g_ref, h_ref):
        xv = x_ref[...]
        r = lax.rsqrt(jnp.mean(xv * xv, axis=-1, keepdims=True) + NORM_EPS)
        h_ref[...] = (xv * r * g_ref[...]).astype(BF16)

    return pl.pallas_call(
        body, name=name, grid=(T // tm,),
        in_specs=[pl.BlockSpec((tm, D), lambda i: (i, 0)), pl.BlockSpec((1, D), lambda i: (0, 0))],
        out_specs=pl.BlockSpec((tm, D), lambda i: (i, 0)),
        out_shape=jax.ShapeDtypeStruct((T, D), BF16),
        compiler_params=_params(("parallel",)),
    )(x, gain.reshape(1, D))


def _rms_bwd(x, gain, dh, dres, *, name, tm=512):
    T, D = x.shape

    def body(x_ref, g_ref, dh_ref, dres_ref, dx_ref, dg_ref):
        xv = x_ref[...]
        r = lax.rsqrt(jnp.mean(xv * xv, axis=-1, keepdims=True) + NORM_EPS)
        xh = xv * r
        dhv = dh_ref[...]
        dxh = dhv * g_ref[...]
        dx = r * (dxh - xh * jnp.mean(dxh * xh, axis=-1, keepdims=True))
        dx_ref[...] = dres_ref[...] + dx
        part = jnp.sum(dhv * xh, axis=0, keepdims=True)

        @pl.when(pl.program_id(0) == 0)
        def _():
            dg_ref[...] = part

        @pl.when(pl.program_id(0) > 0)
        def _():
            dg_ref[...] += part

    row = pl.BlockSpec((tm, D), lambda i: (i, 0))
    vec = pl.BlockSpec((1, D), lambda i: (0, 0))
    dx, dg = pl.pallas_call(
        body, name=name, grid=(T // tm,),
        in_specs=[row, vec, row, row],
        out_specs=[row, vec],
        out_shape=[jax.ShapeDtypeStruct((T, D), F32), jax.ShapeDtypeStruct((1, D), F32)],
        compiler_params=_params(("arbitrary",)),
    )(x, gain.reshape(1, D), dh, dres)
    return dx, dg.reshape(D)


def _head_norm(x, gain2, lo):
    ms = _half_sums(x * x, lo) * (1.0 / HEAD_DIM)
    r = lax.rsqrt(ms + NORM_EPS)
    xh = x * r
    return xh * gain2, xh, r


def _head_norm_bwd(xh, r, gain2, dy, lo):
    dxh = dy * gain2
    dx = r * (dxh - xh * (_half_sums(dxh * xh, lo) * (1.0 / HEAD_DIM)))
    return dx, dy * xh


def _att_masks():
    qi = lax.broadcasted_iota(jnp.int32, (BLOCK, BLOCK), 0)
    kj = lax.broadcasted_iota(jnp.int32, (BLOCK, BLOCK), 1)
    d_cur = qi - kj
    d_prev = qi - kj + BLOCK
    return d_cur >= 0, d_prev < BLOCK, d_cur.astype(F32), d_prev.astype(F32)


def _att_probs(qm, k2c, k2p, sink, slope, masks, has_prev):
    ok_c, ok_p, d_c, d_p = masks
    s_c = jnp.where(ok_c, _dot_nt(qm, k2c) * ATT_SCALE - slope * d_c, NEG_INF)
    s_p = jnp.where(jnp.logical_and(ok_p, has_prev), _dot_nt(qm, k2p) * ATT_SCALE - slope * d_p, NEG_INF)
    m = jnp.maximum(jnp.maximum(jnp.max(s_c, axis=-1, keepdims=True), jnp.max(s_p, axis=-1, keepdims=True)), sink)
    e_c = jnp.exp(s_c - m)
    e_p = jnp.exp(s_p - m)
    e_s = jnp.exp(sink - m)
    inv = 1.0 / (jnp.sum(e_c, axis=-1, keepdims=True) + jnp.sum(e_p, axis=-1, keepdims=True) + e_s)
    return e_c * inv, e_p * inv, e_s * inv


def _attention_fwd(proj, q_gain, k_gain, sinks, *, n_seq, seq, name):
    T = n_seq * seq
    nb = seq // BLOCK
    qcol, kvcol = COL_QKV // ATT_WIDTH, (COL_QKV + ATT_WIDTH) // (2 * KV_WIDTH)

    def body(q_ref, kv_ref, qg_ref, kg_ref, sink_ref, y_ref):
        lo = _lo_mask((BLOCK, LANES))
        masks = _att_masks()
        qg, kg = qg_ref[...], kg_ref[...]

        def block(i, carry):
            r0 = pl.multiple_of(i * BLOCK, BLOCK)
            rp = pl.multiple_of(jnp.maximum(i - 1, 0) * BLOCK, BLOCK)
            has_prev = i > 0
            kn_c = _head_norm(kv_ref[pl.ds(r0, BLOCK), 0:KV_WIDTH], kg, lo)[0].astype(BF16)
            kn_p = _head_norm(kv_ref[pl.ds(rp, BLOCK), 0:KV_WIDTH], kg, lo)[0].astype(BF16)
            v_c = kv_ref[pl.ds(r0, BLOCK), KV_WIDTH:2 * KV_WIDTH].astype(BF16)
            v_p = kv_ref[pl.ds(rp, BLOCK), KV_WIDTH:2 * KV_WIDTH].astype(BF16)
            for pair in range(N_Q_HEADS // 2):
                kv = pair // 2
                k2c, k2p = _dup_half(kn_c, kv, lo), _dup_half(kn_p, kv, lo)
                v2c, v2p = _dup_half(v_c, kv, lo), _dup_half(v_p, kv, lo)
                qn = _head_norm(q_ref[pl.ds(r0, BLOCK), pair * LANES:(pair + 1) * LANES], qg, lo)[0]
                out = None
                for half in range(2):
                    h = 2 * pair + half
                    mine = lo if half == 0 else jnp.logical_not(lo)
                    qm = jnp.where(mine, qn, 0.0).astype(BF16)
                    p_c, p_p, _ = _att_probs(qm, k2c, k2p, sink_ref[h], ALIBI_SLOPES[h], masks, has_prev)
                    o = _dot_nn(p_c.astype(BF16), v2c) + _dot_nn(p_p.astype(BF16), v2p)
                    out = o if out is None else jnp.where(lo, out, o)
                y_ref[pl.ds(r0, BLOCK), pair * LANES:(pair + 1) * LANES] = out.astype(BF16)
            return carry

        lax.fori_loop(0, nb, block, 0)

    vec = pl.BlockSpec((1, LANES), lambda b: (0, 0))
    return pl.pallas_call(
        body, name=name, grid=(n_seq,),
        in_specs=[pl.BlockSpec((seq, ATT_WIDTH), lambda b: (b, qcol)),
                  pl.BlockSpec((seq, 2 * KV_WIDTH), lambda b: (b, kvcol)),
                  vec, vec, pl.BlockSpec(memory_space=pltpu.SMEM)],
        out_specs=pl.BlockSpec((seq, ATT_WIDTH), lambda b: (b, 0)),
        out_shape=jax.ShapeDtypeStruct((T, ATT_WIDTH), BF16),
        compiler_params=_params(("parallel",)),
    )(proj, proj, jnp.tile(q_gain, 2).reshape(1, LANES), jnp.tile(k_gain, 2).reshape(1, LANES), sinks)


def _attention_bwd(proj, dy, q_gain, k_gain, sinks, *, n_seq, seq, name):
    T = n_seq * seq
    nb = seq // BLOCK
    qcol, kvcol = COL_QKV // ATT_WIDTH, (COL_QKV + ATT_WIDTH) // (2 * KV_WIDTH)

    def body(q_ref, kv_ref, dy_ref, qg_ref, kg_ref, sink_ref, dqkv_ref, dqg_ref, dkg_ref, dsink_ref,
             dkn_acc, dv_acc, qg_acc, kg_acc, sink_acc):
        lo = _lo_mask((BLOCK, LANES))
        hi = jnp.logical_not(lo)
        lane = lax.broadcasted_iota(jnp.int32, (BLOCK, LANES), 1)
        masks = _att_masks()
        qg, kg = qg_ref[...], kg_ref[...]
        first = pl.program_id(0) == 0

        @pl.when(first)
        def _():
            qg_acc[...] = jnp.zeros_like(qg_acc)
            kg_acc[...] = jnp.zeros_like(kg_acc)
            sink_acc[...] = jnp.zeros_like(sink_acc)

        dkn_acc[...] = jnp.zeros_like(dkn_acc)
        dv_acc[...] = jnp.zeros_like(dv_acc)

        def block(i, carry):
            r0 = pl.multiple_of(i * BLOCK, BLOCK)
            rp = pl.multiple_of(jnp.maximum(i - 1, 0) * BLOCK, BLOCK)
            has_prev = i > 0
            kn_c = _head_norm(kv_ref[pl.ds(r0, BLOCK), 0:KV_WIDTH], kg, lo)[0].astype(BF16)
            kn_p = _head_norm(kv_ref[pl.ds(rp, BLOCK), 0:KV_WIDTH], kg, lo)[0].astype(BF16)
            v_c = kv_ref[pl.ds(r0, BLOCK), KV_WIDTH:2 * KV_WIDTH].astype(BF16)
            v_p = kv_ref[pl.ds(rp, BLOCK), KV_WIDTH:2 * KV_WIDTH].astype(BF16)
            dk_c = [jnp.zeros((BLOCK, LANES), F32) for _ in range(2)]
            dk_p = [jnp.zeros((BLOCK, LANES), F32) for _ in range(2)]
            dv_c = [jnp.zeros((BLOCK, LANES), F32) for _ in range(2)]
            dv_p = [jnp.zeros((BLOCK, LANES), F32) for _ in range(2)]
            for pair in range(N_Q_HEADS // 2):
                kv = pair // 2
                cols = slice(pair * LANES, (pair + 1) * LANES)
                k2c, k2p = _dup_half(kn_c, kv, lo), _dup_half(kn_p, kv, lo)
                v2c, v2p = _dup_half(v_c, kv, lo), _dup_half(v_p, kv, lo)
                qn, qh, qr = _head_norm(q_ref[pl.ds(r0, BLOCK), cols], qg, lo)
                do_pair = dy_ref[pl.ds(r0, BLOCK), cols]
                dqn = None
                for half in range(2):
                    h = 2 * pair + half
                    mine = lo if half == 0 else hi
                    qm = jnp.where(mine, qn, 0.0).astype(BF16)
                    dom = jnp.where(mine, do_pair, jnp.zeros_like(do_pair))
                    p_c, p_p, p_s = _att_probs(qm, k2c, k2p, sink_ref[h], ALIBI_SLOPES[h], masks, has_prev)
                    dp_c = _dot_nt(dom, v2c)
                    dp_p = _dot_nt(dom, v2p)
                    delta = jnp.sum(p_c * dp_c, axis=-1, keepdims=True) + jnp.sum(p_p * dp_p, axis=-1, keepdims=True)
                    ds_c = (p_c * (dp_c - delta)).astype(BF16)
                    ds_p = (p_p * (dp_p - delta)).astype(BF16)
                    sink_acc[...] += jnp.where(lane == h, -(p_s * delta), 0.0)
                    dq_h = (_dot_nn(ds_c, k2c) + _dot_nn(ds_p, k2p)) * ATT_SCALE
                    dqn = dq_h if dqn is None else jnp.where(lo, dqn, dq_h)
                    dk_c[kv] = dk_c[kv] + _dot_tn(ds_c, qm)
                    dk_p[kv] = dk_p[kv] + _dot_tn(ds_p, qm)
                    dv_c[kv] = dv_c[kv] + _dot_tn(p_c.astype(BF16), dom)
                    dv_p[kv] = dv_p[kv] + _dot_tn(p_p.astype(BF16), dom)
                dq, dg = _head_norm_bwd(qh, qr, qg, dqn, lo)
                dqkv_ref[pl.ds(r0, BLOCK), cols] = dq.astype(BF16)
                qg_acc[...] += dg

            def fold(parts):
                a = parts[0] + pltpu.roll(parts[0], LANES // 2, axis=1)
                b = parts[1] + pltpu.roll(parts[1], LANES // 2, axis=1)
                return jnp.where(lo, a, b)

            dkn_acc[pl.ds(r0, BLOCK), :] += fold(dk_c) * ATT_SCALE
            dkn_acc[pl.ds(rp, BLOCK), :] += fold(dk_p) * ATT_SCALE
            dv_acc[pl.ds(r0, BLOCK), :] += fold(dv_c)
            dv_acc[pl.ds(rp, BLOCK), :] += fold(dv_p)
            return carry

        lax.fori_loop(0, nb, block, 0)

        def finish(i, carry):
            r0 = pl.multiple_of(i * BLOCK, BLOCK)
            _, kh, kr = _head_norm(kv_ref[pl.ds(r0, BLOCK), 0:KV_WIDTH], kg, lo)
            dk, dg = _head_norm_bwd(kh, kr, kg, dkn_acc[pl.ds(r0, BLOCK), :], lo)
            dqkv_ref[pl.ds(r0, BLOCK), ATT_WIDTH:ATT_WIDTH + KV_WIDTH] = dk.astype(BF16)
            dqkv_ref[pl.ds(r0, BLOCK), ATT_WIDTH + KV_WIDTH:QKV_WIDTH] = dv_acc[pl.ds(r0, BLOCK), :].astype(BF16)
            kg_acc[...] += dg
            return carry

        lax.fori_loop(0, nb, finish, 0)

        @pl.when(pl.program_id(0) == n_seq - 1)
        def _():
            dqg_ref[...] = jnp.sum(qg_acc[...], axis=0, keepdims=True)
            dkg_ref[...] = jnp.sum(kg_acc[...], axis=0, keepdims=True)
            dsink_ref[...] = jnp.sum(sink_acc[...], axis=0, keepdims=True)

    vec = pl.BlockSpec((1, LANES), lambda b: (0, 0))
    acc = pltpu.VMEM((BLOCK, LANES), F32)
    dqkv, dqg, dkg, dsink = pl.pallas_call(
        body, name=name, grid=(n_seq,),
        in_specs=[pl.BlockSpec((seq, ATT_WIDTH), lambda b: (b, qcol)),
                  pl.BlockSpec((seq, 2 * KV_WIDTH), lambda b: (b, kvcol)),
                  pl.BlockSpec((seq, ATT_WIDTH), lambda b: (b, 0)),
                  vec, vec, pl.BlockSpec(memory_space=pltpu.SMEM)],
        out_specs=[pl.BlockSpec((seq, QKV_WIDTH), lambda b: (b, 0)), vec, vec, vec],
        out_shape=[jax.ShapeDtypeStruct((T, QKV_WIDTH), BF16)] + [jax.ShapeDtypeStruct((1, LANES), F32)] * 3,
        scratch_shapes=[pltpu.VMEM((seq, KV_WIDTH), F32), pltpu.VMEM((seq, KV_WIDTH), F32), acc, acc, acc],
        compiler_params=_params(("arbitrary",)),
    )(proj, proj, dy, jnp.tile(q_gain, 2).reshape(1, LANES), jnp.tile(k_gain, 2).reshape(1, LANES), sinks)
    half = LANES // 2
    return dqkv, dqg[0, :half] + dqg[0, half:], dkg[0, :half] + dkg[0, half:], dsink[0, :N_Q_HEADS]


def _sgu_weights(w_ref):
    r = lax.broadcasted_iota(jnp.int32, (BLOCK, BLOCK), 0)
    c = lax.broadcasted_iota(jnp.int32, (BLOCK, BLOCK), 1)
    return [jnp.where(r >= c, w_ref[g], 0.0).astype(BF16) for g in range(SGU_GROUPS)]


def _sgu_fwd(proj, gain, w_s, bias_full, *, n_seq, seq, name):
    T = n_seq * seq
    nc = seq // BLOCK

    def body(suv_ref, g_ref, w_ref, b_ref, y_ref):
        lo = _lo_mask((BLOCK, LANES))
        wm = _sgu_weights(w_ref)
        gain_v = g_ref[...]

        def chunk(c, carry):
            r0 = pl.multiple_of(c * BLOCK, BLOCK)
            gv = _gelu(suv_ref[pl.ds(r0, BLOCK), SGU_WIDTH:2 * SGU_WIDTH])
            r = lax.rsqrt(jnp.mean(gv * gv, axis=-1, keepdims=True) + NORM_EPS)
            vn = (gv * r * gain_v).astype(BF16)
            for p in range(SGU_WIDTH // LANES):
                cols = slice(p * LANES, (p + 1) * LANES)
                vp = vn[:, cols]
                mixed = jnp.where(lo, _dot_nn(wm[2 * p], vp), _dot_nn(wm[2 * p + 1], vp)) + b_ref[:, cols]
                u = _gelu(suv_ref[pl.ds(r0, BLOCK), cols])
                y_ref[pl.ds(r0, BLOCK), cols] = (u * mixed).astype(BF16)
            return carry

        lax.fori_loop(0, nc, chunk, 0)

    return pl.pallas_call(
        body, name=name, grid=(n_seq,),
        in_specs=[pl.BlockSpec((seq, 2 * SGU_WIDTH), lambda b: (b, COL_SUV // (2 * SGU_WIDTH))),
                  pl.BlockSpec((1, SGU_WIDTH), lambda b: (0, 0)),
                  pl.BlockSpec((SGU_GROUPS, BLOCK, BLOCK), lambda b: (0, 0, 0)),
                  pl.BlockSpec((BLOCK, SGU_WIDTH), lambda b: (0, 0))],
        out_specs=pl.BlockSpec((seq, SGU_WIDTH), lambda b: (b, 0)),
        out_shape=jax.ShapeDtypeStruct((T, SGU_WIDTH), BF16),
        compiler_params=_params(("parallel",)),
    )(proj, gain.reshape(1, SGU_WIDTH), w_s, bias_full)


def _sgu_bwd(proj, dy, gain, w_s, bias_full, *, n_seq, seq, name):
    T = n_seq * seq
    nc = seq // BLOCK
    n_tiles = SGU_WIDTH // LANES

    def body(suv_ref, dy_ref, g_ref, w_ref, b_ref, dsuv_ref, dg_ref, dw_ref, db_ref, dg_acc, dw_acc, db_acc):
        lo = _lo_mask((BLOCK, LANES))
        hi = jnp.logical_not(lo)
        wm = _sgu_weights(w_ref)
        wmt = [jnp.where(lax.broadcasted_iota(jnp.int32, (BLOCK, BLOCK), 1) >= lax.broadcasted_iota(jnp.int32, (BLOCK, BLOCK), 0),
                         w_ref[g].T, 0.0).astype(BF16) for g in range(SGU_GROUPS)]
        gain_v = g_ref[...]

        @pl.when(pl.program_id(0) == 0)
        def _():
            dg_acc[...] = jnp.zeros_like(dg_acc)
            dw_acc[...] = jnp.zeros_like(dw_acc)
            db_acc[...] = jnp.zeros_like(db_acc)

        def chunk(c, carry):
            r0 = pl.multiple_of(c * BLOCK, BLOCK)
            gv, dgelu_v = _gelu_and_grad(suv_ref[pl.ds(r0, BLOCK), SGU_WIDTH:2 * SGU_WIDTH])
            r = lax.rsqrt(jnp.mean(gv * gv, axis=-1, keepdims=True) + NORM_EPS)
            vh = gv * r
            vn = (vh * gain_v).astype(BF16)
            dvn_tiles = []
            for p in range(n_tiles):
                cols = slice(p * LANES, (p + 1) * LANES)
                vp = vn[:, cols]
                mixed = jnp.where(lo, _dot_nn(wm[2 * p], vp), _dot_nn(wm[2 * p + 1], vp)) + b_ref[:, cols]
                u, dgelu_u = _gelu_and_grad(suv_ref[pl.ds(r0, BLOCK), cols])
                dyv = dy_ref[pl.ds(r0, BLOCK), cols]
                dsuv_ref[pl.ds(r0, BLOCK), cols] = (dyv * mixed * dgelu_u).astype(BF16)
                dm = dyv * u
                db_acc[:, cols] += dm
                dm_bf = dm.astype(BF16)
                dvn_tiles.append(jnp.where(lo, _dot_nn(wmt[2 * p], dm_bf), _dot_nn(wmt[2 * p + 1], dm_bf)))
                dw_acc[2 * p] += _dot_nt(jnp.where(lo, dm, 0.0).astype(BF16), vp)
                dw_acc[2 * p + 1] += _dot_nt(jnp.where(hi, dm, 0.0).astype(BF16), vp)
            dvn = jnp.concatenate(dvn_tiles, axis=1)
            dg_acc[...] += dvn * vh
            dvh = dvn * gain_v
            dgv = r * (dvh - vh * jnp.mean(dvh * vh, axis=-1, keepdims=True))
            dsuv_ref[pl.ds(r0, BLOCK), SGU_WIDTH:2 * SGU_WIDTH] = (dgv * dgelu_v).astype(BF16)
            return carry

        lax.fori_loop(0, nc, chunk, 0)

        @pl.when(pl.program_id(0) == n_seq - 1)
        def _():
            dg_ref[...] = jnp.sum(dg_acc[...], axis=0, keepdims=True)
            r = lax.broadcasted_iota(jnp.int32, (BLOCK, BLOCK), 0)
            c = lax.broadcasted_iota(jnp.int32, (BLOCK, BLOCK), 1)
            for g in range(SGU_GROUPS):
                dw_ref[g] = jnp.where(r >= c, dw_acc[g], 0.0)
            lane = lax.broadcasted_iota(jnp.int32, (BLOCK, LANES), 1)
            out = jnp.zeros((BLOCK, LANES), F32)
            for p in range(n_tiles):
                tile = db_acc[:, p * LANES:(p + 1) * LANES]
                s_lo = jnp.sum(jnp.where(lo, tile, 0.0), axis=-1, keepdims=True)
                s_hi = jnp.sum(jnp.where(hi, tile, 0.0), axis=-1, keepdims=True)
                out = jnp.where(lane == 2 * p, s_lo, out)
                out = jnp.where(lane == 2 * p + 1, s_hi, out)
            db_ref[...] = out

    dsuv, dg, dw, db = pl.pallas_call(
        body, name=name, grid=(n_seq,),
        in_specs=[pl.BlockSpec((seq, 2 * SGU_WIDTH), lambda b: (b, COL_SUV // (2 * SGU_WIDTH))),
                  pl.BlockSpec((seq, SGU_WIDTH), lambda b: (b, 0)),
                  pl.BlockSpec((1, SGU_WIDTH), lambda b: (0, 0)),
                  pl.BlockSpec((SGU_GROUPS, BLOCK, BLOCK), lambda b: (0, 0, 0)),
                  pl.BlockSpec((BLOCK, SGU_WIDTH), lambda b: (0, 0))],
        out_specs=[pl.BlockSpec((seq, 2 * SGU_WIDTH), lambda b: (b, 0)),
                   pl.BlockSpec((1, SGU_WIDTH), lambda b: (0, 0)),
                   pl.BlockSpec((SGU_GROUPS, BLOCK, BLOCK), lambda b: (0, 0, 0)),
                   pl.BlockSpec((BLOCK, LANES), lambda b: (0, 0))],
        out_shape=[jax.ShapeDtypeStruct((T, 2 * SGU_WIDTH), BF16), jax.ShapeDtypeStruct((1, SGU_WIDTH), F32),
                   jax.ShapeDtypeStruct((SGU_GROUPS, BLOCK, BLOCK), F32), jax.ShapeDtypeStruct((BLOCK, LANES), F32)],
        scratch_shapes=[pltpu.VMEM((BLOCK, SGU_WIDTH), F32), pltpu.VMEM((SGU_GROUPS, BLOCK, BLOCK), F32),
                        pltpu.VMEM((BLOCK, SGU_WIDTH), F32)],
        compiler_params=_params(("arbitrary",)),
    )(proj, dy, gain.reshape(1, SGU_WIDTH), w_s, bias_full)
    return dsuv, dg.reshape(SGU_WIDTH), dw, db[:, :SGU_GROUPS].T


def _merge_fwd(y_att, y_sgu, w_oa, w_ob, proj, *, name, tm=512, tn=512):
    T = y_att.shape[0]

    def body(ya_ref, ys_ref, wa_ref, wb_ref, ga_ref, gb_ref, o_ref):
        pa = _dot_nn(ya_ref[...], wa_ref[...])
        pb = _dot_nn(ys_ref[...], wb_ref[...])
        o_ref[...] = (_sigmoid(ga_ref[...]) * pa + _sigmoid(gb_ref[...]) * pb).astype(BF16)

    act = pl.BlockSpec((tm, ATT_WIDTH), lambda i, j: (i, 0))
    wgt = pl.BlockSpec((ATT_WIDTH, tn), lambda i, j: (0, j))
    return pl.pallas_call(
        body, name=name, grid=(T // tm, D_MODEL // tn),
        in_specs=[act, act, wgt, wgt,
                  pl.BlockSpec((tm, tn), lambda i, j: (i, j + COL_GA // tn)),
                  pl.BlockSpec((tm, tn), lambda i, j: (i, j + COL_GB // tn))],
        out_specs=pl.BlockSpec((tm, tn), lambda i, j: (i, j)),
        out_shape=jax.ShapeDtypeStruct((T, D_MODEL), BF16),
        compiler_params=_params(("parallel", "parallel")),
    )(y_att, y_sgu, w_oa, w_ob, proj, proj)


def _merge_bwd(dx1_bf, w_out, y_att, y_sgu, w_oa, w_ob, proj, *, name, tm=512, tn=512):
    T = y_att.shape[0]

    def body(dx_ref, wo_ref, ya_ref, ys_ref, wa_ref, wb_ref, ga_ref, gb_ref, dpa_ref, dpb_ref, dga_ref, dgb_ref):
        dm = _dot_nt(dx_ref[...], wo_ref[...])
        pa = _dot_nn(ya_ref[...], wa_ref[...])
        pb = _dot_nn(ys_ref[...], wb_ref[...])
        sa = _sigmoid(ga_ref[...])
        sb = _sigmoid(gb_ref[...])
        dpa_ref[...] = (dm * sa).astype(BF16)
        dpb_ref[...] = (dm * sb).astype(BF16)
        dga_ref[...] = (dm * pa * sa * (1.0 - sa)).astype(BF16)
        dgb_ref[...] = (dm * pb * sb * (1.0 - sb)).astype(BF16)

    act = pl.BlockSpec((tm, ATT_WIDTH), lambda i, j: (i, 0))
    wgt = pl.BlockSpec((ATT_WIDTH, tn), lambda i, j: (0, j))
    out = pl.BlockSpec((tm, tn), lambda i, j: (i, j))
    return pl.pallas_call(
        body, name=name, grid=(T // tm, D_MODEL // tn),
        in_specs=[pl.BlockSpec((tm, D_MODEL), lambda i, j: (i, 0)),
                  pl.BlockSpec((tn, D_MODEL), lambda i, j: (j, 0)),
                  act, act, wgt, wgt,
                  pl.BlockSpec((tm, tn), lambda i, j: (i, j + COL_GA // tn)),
                  pl.BlockSpec((tm, tn), lambda i, j: (i, j + COL_GB // tn))],
        out_specs=[out] * 4,
        out_shape=[jax.ShapeDtypeStruct((T, D_MODEL), BF16)] * 4,
        compiler_params=_params(("parallel", "parallel")),
    )(dx1_bf, w_out, y_att, y_sgu, w_oa, w_ob, proj, proj)


CONV_ROWS = 256
CONV_TN = 256


def _shift_rows(cur, prev8, k):
    rolled = pltpu.roll(cur, k, axis=0)
    head = jnp.where(lax.broadcasted_iota(jnp.int32, prev8.shape, 0) < k, pltpu.roll(prev8, k, axis=0), rolled[:SUBLANES])
    return jnp.concatenate([head, rolled[SUBLANES:]], axis=0)


def _shift_rows_up(cur, next8, k):
    n = cur.shape[0]
    rolled = pltpu.roll(cur, n - k, axis=0)
    tail = jnp.where(lax.broadcasted_iota(jnp.int32, next8.shape, 0) >= SUBLANES - k,
                     pltpu.roll(next8, SUBLANES - k, axis=0), rolled[n - SUBLANES:])
    return jnp.concatenate([rolled[:n - SUBLANES], tail], axis=0)


def _conv_rows(z_ref, r0, first, w_ref, b_ref, rows):
    cur = z_ref[pl.ds(r0, rows), :]
    rp = pl.multiple_of(jnp.maximum(r0 - SUBLANES, 0), SUBLANES)
    prev8 = jnp.where(first, 0.0, z_ref[pl.ds(rp, SUBLANES), :])
    z1 = _shift_rows(cur, prev8, 1)
    z2 = _shift_rows(cur, prev8, 2)
    return b_ref[...] + w_ref[0:1, :] * z2 + w_ref[1:2, :] * z1 + w_ref[2:3, :] * cur


def _conv_fwd(z_g, z_v, cw_g, cw_v, cb_g, cb_v, *, n_seq, seq, name):
    T = n_seq * seq
    tn, rows = CONV_TN, CONV_ROWS

    def body(zg_ref, zv_ref, wg_ref, wv_ref, bg_ref, bv_ref, a_ref):
        def step(s, carry):
            r0 = pl.multiple_of(s * rows, rows)
            first = s == 0
            g = _conv_rows(zg_ref, r0, first, wg_ref, bg_ref, rows)
            v = _conv_rows(zv_ref, r0, first, wv_ref, bv_ref, rows)
            a_ref[pl.ds(r0, rows), :] = (g * _sigmoid(g) * v).astype(BF16)
            return carry

        lax.fori_loop(0, seq // rows, step, 0)

    zs = pl.BlockSpec((seq, tn), lambda b, j: (b, j))
    ws = pl.BlockSpec((3, tn), lambda b, j: (0, j))
    bs = pl.BlockSpec((1, tn), lambda b, j: (0, j))
    return pl.pallas_call(
        body, name=name, grid=(n_seq, D_FF // tn),
        in_specs=[zs, zs, ws, ws, bs, bs], out_specs=zs,
        out_shape=jax.ShapeDtypeStruct((T, D_FF), BF16),
        compiler_params=_params(("parallel", "parallel")),
    )(z_g, z_v, cw_g, cw_v, cb_g.reshape(1, D_FF), cb_v.reshape(1, D_FF))


def _conv_bwd(z_g, z_v, da, cw_g, cw_v, cb_g, cb_v, *, n_seq, seq, name):
    T = n_seq * seq
    tn, rows = CONV_TN, CONV_ROWS
    n_steps = seq // rows

    def body(zg_ref, zv_ref, da_ref, wg_ref, wv_ref, bg_ref, bv_ref,
             dzg_ref, dzv_ref, dwg_ref, dwv_ref, dbg_ref, dbv_ref, dcg_ref, dcv_ref):
        def grads(s, accs):
            r0 = pl.multiple_of(s * rows, rows)
            first = s == 0
            cur_g = zg_ref[pl.ds(r0, rows), :]
            cur_v = zv_ref[pl.ds(r0, rows), :]
            rp = pl.multiple_of(jnp.maximum(r0 - SUBLANES, 0), SUBLANES)
            pg = jnp.where(first, 0.0, zg_ref[pl.ds(rp, SUBLANES), :])
            pv = jnp.where(first, 0.0, zv_ref[pl.ds(rp, SUBLANES), :])
            g1, g2 = _shift_rows(cur_g, pg, 1), _shift_rows(cur_g, pg, 2)
            v1, v2 = _shift_rows(cur_v, pv, 1), _shift_rows(cur_v, pv, 2)
            g = bg_ref[...] + wg_ref[0:1, :] * g2 + wg_ref[1:2, :] * g1 + wg_ref[2:3, :] * cur_g
            v = bv_ref[...] + wv_ref[0:1, :] * v2 + wv_ref[1:2, :] * v1 + wv_ref[2:3, :] * cur_v
            sg = _sigmoid(g)
            dav = da_ref[pl.ds(r0, rows), :]
            dcg = dav * v * (sg * (1.0 + g * (1.0 - sg)))
            dcv = dav * (g * sg)
            dcg_ref[pl.ds(r0, rows), :] = dcg
            dcv_ref[pl.ds(r0, rows), :] = dcv

            def colsum(x):
                return jnp.sum(x, axis=0, keepdims=True)

            return (accs[0] + colsum(dcg * g2), accs[1] + colsum(dcg * g1), accs[2] + colsum(dcg * cur_g), accs[3] + colsum(dcg),
                    accs[4] + colsum(dcv * v2), accs[5] + colsum(dcv * v1), accs[6] + colsum(dcv * cur_v), accs[7] + colsum(dcv))

        zero = jnp.zeros((1, tn), F32)
        sums = lax.fori_loop(0, n_steps, grads, (zero,) * 8)
        first_seq = pl.program_id(1) == 0

        @pl.when(first_seq)
        def _():
            dwg_ref[...] = jnp.concatenate(sums[0:3], axis=0)
            dbg_ref[...] = sums[3]
            dwv_ref[...] = jnp.concatenate(sums[4:7], axis=0)
            dbv_ref[...] = sums[7]

        @pl.when(jnp.logical_not(first_seq))
        def _():
            dwg_ref[...] += jnp.concatenate(sums[0:3], axis=0)
            dbg_ref[...] += sums[3]
            dwv_ref[...] += jnp.concatenate(sums[4:7], axis=0)
            dbv_ref[...] += sums[7]

        def back(s, carry):
            r0 = pl.multiple_of(s * rows, rows)
            last = s == n_steps - 1
            rn = pl.multiple_of(jnp.minimum(r0 + rows, seq - SUBLANES), SUBLANES)
            for dc_ref, w_ref, dz_ref in ((dcg_ref, wg_ref, dzg_ref), (dcv_ref, wv_ref, dzv_ref)):
                cur = dc_ref[pl.ds(r0, rows), :]
                nxt = jnp.where(last, 0.0, dc_ref[pl.ds(rn, SUBLANES), :])
                u1, u2 = _shift_rows_up(cur, nxt, 1), _shift_rows_up(cur, nxt, 2)
                dz_ref[pl.ds(r0, rows), :] = (w_ref[2:3, :] * cur + w_ref[1:2, :] * u1 + w_ref[0:1, :] * u2).astype(BF16)
            return carry

        lax.fori_loop(0, n_steps, back, 0)

    zs = pl.BlockSpec((seq, tn), lambda j, b: (b, j))
    ws = pl.BlockSpec((3, tn), lambda j, b: (0, j))
    bs = pl.BlockSpec((1, tn), lambda j, b: (0, j))
    outs = pl.pallas_call(
        body, name=name, grid=(D_FF // tn, n_seq),
        in_specs=[zs, zs, zs, ws, ws, bs, bs],
        out_specs=[zs, zs, ws, ws, bs, bs],
        out_shape=[jax.ShapeDtypeStruct((T, D_FF), BF16)] * 2 + [jax.ShapeDtypeStruct((3, D_FF), F32)] * 2
        + [jax.ShapeDtypeStruct((1, D_FF), F32)] * 2,
        scratch_shapes=[pltpu.VMEM((seq, tn), F32), pltpu.VMEM((seq, tn), F32)],
        compiler_params=_params(("parallel", "arbitrary")),
    )(z_g, z_v, da, cw_g, cw_v, cb_g.reshape(1, D_FF), cb_v.reshape(1, D_FF))
    dz_g, dz_v, dw_g, dw_v, db_g, db_v = outs
    return dz_g, dz_v, dw_g, dw_v, db_g.reshape(D_FF), db_v.reshape(D_FF)


def _loss_head(y, target, *, name, tm=512):
    T, D = y.shape

    def body(y_ref, t_ref, dy_ref, dyb_ref, l_ref):
        err = y_ref[...] - t_ref[...]
        dyv = err * (1.0 / D)
        dy_ref[...] = dyv
        dyb_ref[...] = dyv.astype(BF16)
        part = jnp.sum(jnp.sum(err * err, axis=0, keepdims=True), axis=1, keepdims=True) * (0.5 / D)

        @pl.when(pl.program_id(0) == 0)
        def _():
            l_ref[...] = jnp.broadcast_to(part, l_ref.shape)

        @pl.when(pl.program_id(0) > 0)
        def _():
            l_ref[...] += jnp.broadcast_to(part, l_ref.shape)

    row = pl.BlockSpec((tm, D), lambda i: (i, 0))
    dy, dyb, l = pl.pallas_call(
        body, name=name, grid=(T // tm,),
        in_specs=[row, row],
        out_specs=[row, row, pl.BlockSpec((SUBLANES, LANES), lambda i: (0, 0))],
        out_shape=[jax.ShapeDtypeStruct((T, D), F32), jax.ShapeDtypeStruct((T, D), BF16),
                   jax.ShapeDtypeStruct((SUBLANES, LANES), F32)],
        compiler_params=_params(("arbitrary",)),
    )(y, target)
    return l[0, 0], dy, dyb


def _cast_bf16(x, *, name, tm=512):
    T, D = x.shape

    def body(x_ref, o_ref):
        o_ref[...] = x_ref[...].astype(BF16)

    row = pl.BlockSpec((tm, D), lambda i: (i, 0))
    return pl.pallas_call(body, name=name, grid=(T // tm,), in_specs=[row], out_specs=row,
                          out_shape=jax.ShapeDtypeStruct((T, D), BF16), compiler_params=_params(("parallel",)))(x)


def _layer_fwd(x, w, *, n_seq, seq, l):
    tag = f"l{l}"
    h = _rms_fwd(x, w["mix_norm"], name=f"{tag}_mix_norm")
    proj = _mm(h, w["w_in"], mode="nn", out_dtype=F32, tm=512, tn=768, tk=D_MODEL, name=f"{tag}_proj")
    y_att = _attention_fwd(proj, w["q_norm"], w["k_norm"], w["sinks"], n_seq=n_seq, seq=seq, name=f"{tag}_att")
    y_sgu = _sgu_fwd(proj, w["sgu_norm"], w["w_s"], w["bias_full"], n_seq=n_seq, seq=seq, name=f"{tag}_sgu")
    merged = _merge_fwd(y_att, y_sgu, w["w_oa"], w["w_ob"], proj, name=f"{tag}_merge")
    x1 = _mm(merged, w["w_out"], mode="nn", out_dtype=F32, tm=512, tn=1024, tk=D_MODEL, name=f"{tag}_out",
             epilogue=_add, extras=(x,))
    h2 = _rms_fwd(x1, w["ffn_norm"], name=f"{tag}_ffn_norm")
    z_g = _mm(h2, w["w_up_g"], mode="nn", out_dtype=F32, tm=512, tn=1408, tk=D_MODEL, name=f"{tag}_up_g")
    z_v = _mm(h2, w["w_up_v"], mode="nn", out_dtype=F32, tm=512, tn=1408, tk=D_MODEL, name=f"{tag}_up_v")
    a = _conv_fwd(z_g, z_v, w["cw_g"], w["cw_v"], w["cb_g"], w["cb_v"], n_seq=n_seq, seq=seq, name=f"{tag}_conv")
    x2 = _mm(a, w["w_down"], mode="nn", out_dtype=F32, tm=512, tn=1024, tk=D_FF, name=f"{tag}_down",
             epilogue=_add, extras=(x1,))
    saved = dict(x=x, h=h, proj=proj, y_att=y_att, y_sgu=y_sgu, merged=merged, x1=x1, h2=h2, z_g=z_g, z_v=z_v, a=a)
    return x2, saved


def _layer_bwd(dx2, dx2_bf, w, s, *, n_seq, seq, l):
    tag = f"l{l}b"
    g = {}
    da = _mm(dx2_bf, w["w_down"], mode="nt", out_dtype=F32, tm=512, tn=1408, tk=D_MODEL, name=f"{tag}_da")
    g["w_down"] = _mm(s["a"], dx2_bf, mode="tn", out_dtype=F32, tm=1408, tn=1024, tk=512, name=f"{tag}_dw_down")
    dz_g, dz_v, g["cw_g"], g["cw_v"], g["cb_g"], g["cb_v"] = _conv_bwd(
        s["z_g"], s["z_v"], da, w["cw_g"], w["cw_v"], w["cb_g"], w["cb_v"], n_seq=n_seq, seq=seq, name=f"{tag}_conv")
    dh2 = _mm(dz_g, w["w_up_g"], mode="nt", out_dtype=F32, tm=512, tn=1024, tk=1408, name=f"{tag}_dh2_g")
    dh2 = _mm(dz_v, w["w_up_v"], mode="nt", out_dtype=F32, tm=512, tn=1024, tk=1408, name=f"{tag}_dh2_v",
              epilogue=_add, extras=(dh2,))
    g["w_up_g"] = _mm(s["h2"], dz_g, mode="tn", out_dtype=F32, tm=1024, tn=1408, tk=512, name=f"{tag}_dw_up_g")
    g["w_up_v"] = _mm(s["h2"], dz_v, mode="tn", out_dtype=F32, tm=1024, tn=1408, tk=512, name=f"{tag}_dw_up_v")
    dx1, g["ffn_norm"] = _rms_bwd(s["x1"], w["ffn_norm"], dh2, dx2, name=f"{tag}_ffn_norm")
    dx1_bf = _cast_bf16(dx1, name=f"{tag}_dx1_bf")
    dpa, dpb, dga, dgb = _merge_bwd(dx1_bf, w["w_out"], s["y_att"], s["y_sgu"], w["w_oa"], w["w_ob"], s["proj"],
                                    name=f"{tag}_merge")
    g["w_out"] = _mm(s["merged"], dx1_bf, mode="tn", out_dtype=F32, tm=1024, tn=1024, tk=512, name=f"{tag}_dw_out")
    dy_att = _mm(dpa, w["w_oa"], mode="nt", out_dtype=BF16, tm=512, tn=512, tk=D_MODEL, name=f"{tag}_dy_att")
    dy_sgu = _mm(dpb, w["w_ob"], mode="nt", out_dtype=F32, tm=512, tn=512, tk=D_MODEL, name=f"{tag}_dy_sgu")
    g["w_oa"] = _mm(s["y_att"], dpa, mode="tn", out_dtype=F32, tm=512, tn=1024, tk=512, name=f"{tag}_dw_oa")
    g["w_ob"] = _mm(s["y_sgu"], dpb, mode="tn", out_dtype=F32, tm=512, tn=1024, tk=512, name=f"{tag}_dw_ob")
    dqkv, g["q_norm"], g["k_norm"], g["sinks"] = _attention_bwd(
        s["proj"], dy_att, w["q_norm"], w["k_norm"], w["sinks"], n_seq=n_seq, seq=seq, name=f"{tag}_att")
    dsuv, g["sgu_norm"], g["w_s"], g["b_s"] = _sgu_bwd(
        s["proj"], dy_sgu, w["sgu_norm"], w["w_s"], w["bias_full"], n_seq=n_seq, seq=seq, name=f"{tag}_sgu")
    dproj = jnp.concatenate([dsuv, dga, dgb, dqkv], axis=1)
    dh = _mm(dproj, w["w_in"], mode="nt", out_dtype=F32, tm=512, tn=1024, tk=1280, name=f"{tag}_dh")
    g["w_in"] = _mm(s["h"], dproj, mode="tn", out_dtype=F32, tm=1024, tn=768, tk=512, name=f"{tag}_dw_in")
    dx, g["mix_norm"] = _rms_bwd(s["x"], w["mix_norm"], dh, dx1, name=f"{tag}_mix_norm")
    return dx, g


def _local_step(x, target, weights, *, n_seq, seq):
    depth = len(weights)
    saved = []
    h = x
    for l in range(depth):
        h, s = _layer_fwd(h, weights[l], n_seq=n_seq, seq=seq, l=l)
        saved.append(s)
    loss, dy, dy_bf = _loss_head(h, target, name="loss_head")
    grads = [None] * depth
    for l in reversed(range(depth)):
        if l < depth - 1:
            dy_bf = _cast_bf16(dy, name=f"l{l}b_dx2_bf")
        dy, grads[l] = _layer_bwd(dy, dy_bf, weights[l], saved[l], n_seq=n_seq, seq=seq, l=l)
    return loss, dy, grads


W_IN_SHARD = IN_WIDTH // N_DEV
W_UP_SHARD = 2 * D_FF // N_DEV
COL_MOVE_ROWS = 256


def _w_in_moves():
    moves = []
    for j in range(N_DEV):
        a, b = j * W_IN_SHARD, (j + 1) * W_IN_SHARD
        if a < QKV_WIDTH:
            moves.append((j, 0, min(b, QKV_WIDTH) - a, 0, a + REST_WIDTH))
        if b > QKV_WIDTH:
            lo = max(a, QKV_WIDTH)
            moves.append((j, lo - a, b - a, 0, lo - QKV_WIDTH))
    return tuple(moves)


def _w_up_moves():
    half = N_DEV // 2
    return tuple((j, 0, W_UP_SHARD, j // half, (j % half) * W_UP_SHARD) for j in range(N_DEV))


def _w_o_moves():
    return tuple((j, 0, LANES, 0, j * LANES) for j in range(N_DEV))


def _assemble(blocks, widths, moves, *, name):
    _, R, w = blocks.shape
    tr = min(R, COL_MOVE_ROWS)

    def body(b_ref, *o_refs):
        for j, lo, hi, which, at in moves:
            o_refs[which][:, at:at + hi - lo] = b_ref[j, :, lo:hi]

    return pl.pallas_call(
        body, name=name, grid=(R // tr,),
        in_specs=[pl.BlockSpec((N_DEV, tr, w), lambda i: (0, i, 0))],
        out_specs=[pl.BlockSpec((tr, n), lambda i: (i, 0)) for n in widths],
        out_shape=[jax.ShapeDtypeStruct((R, n), blocks.dtype) for n in widths],
        compiler_params=_params(("parallel",)),
    )(blocks)


def _disassemble(mats, w, moves, *, name):
    R = mats[0].shape[0]
    tr = min(R, COL_MOVE_ROWS)
    n = len(mats)

    def body(*refs):
        m_refs, o_ref = refs[:n], refs[n]
        for j, lo, hi, which, at in moves:
            o_ref[j, :, lo:hi] = m_refs[which][:, at:at + hi - lo]

    return pl.pallas_call(
        body, name=name, grid=(R // tr,),
        in_specs=[pl.BlockSpec((tr, m.shape[1]), lambda i: (i, 0)) for m in mats],
        out_specs=pl.BlockSpec((N_DEV, tr, w), lambda i: (0, i, 0)),
        out_shape=jax.ShapeDtypeStruct((N_DEV, R, w), mats[0].dtype),
        compiler_params=_params(("parallel",)),
    )(*mats)


def _my_place():
    return lax.axis_index("x"), lax.axis_index("y"), lax.axis_index("c")


def _gathered_shape(shape, kind):
    r, c = shape
    return {"blocks": (N_DEV, r, c), "rows": (N_DEV * r, c), "cols": (r, N_DEV * c)}[kind]


def _gather_window(ref, kind, shape, j):
    r, c = shape
    if kind == "blocks":
        return ref.at[j]
    if kind == "rows":
        return ref.at[pl.ds(pl.multiple_of(j * r, r), r), :]
    return ref.at[:, pl.ds(pl.multiple_of(j * c, c), c)]


def _gather(srcs, kinds, *, name):
    n = len(srcs)
    shapes = [s.shape for s in srcs]
    per = 7

    def body(*refs):
        src_refs, dst_refs = refs[:n], refs[n:2 * n]
        send_sems, recv_sems, local_sems = refs[2 * n:]
        x, y, c = _my_place()
        me, sibling = (x, y, c), (x, y, 1 - c)
        chips = [(1 - x, y), (x, 1 - y), (1 - x, 1 - y)]

        def at(i, px, py, pc):
            return _gather_window(dst_refs[i], kinds[i], shapes[i], 4 * px + 2 * py + pc)

        def copy(i, k, block, to, src=None):
            return pltpu.make_async_remote_copy(
                src_ref=at(i, *block) if src is None else src, dst_ref=at(i, *block),
                send_sem=send_sems.at[per * i + k], recv_sem=recv_sems.at[per * i + k], device_id=to, device_id_type=MESH)

        mine = [pltpu.make_async_copy(src_refs[i], at(i, *me), local_sems.at[i]) for i in range(n)]
        for cp in mine:
            cp.start()
        started = []
        for i in range(n):
            first = [copy(i, 0, me, sibling, src=src_refs[i])]
            first += [copy(i, 1 + j, me, (*chip, c), src=src_refs[i]) for j, chip in enumerate(chips)]
            for cp in first:
                cp.start()
            started += first
        for i in range(n):
            for j, chip in enumerate(chips):
                copy(i, 1 + j, (*chip, c), me).wait_recv()
                fwd = copy(i, 4 + j, (*chip, c), sibling)
                fwd.start()
                started.append(fwd)
        for i in range(n):
            copy(i, 0, sibling, me).wait_recv()
            for j, chip in enumerate(chips):
                copy(i, 4 + j, (*chip, 1 - c), me).wait_recv()
        for cp in started:
            cp.wait_send()
        for cp in mine:
            cp.wait()

    return pl.pallas_call(
        body, name=name,
        out_shape=[jax.ShapeDtypeStruct(_gathered_shape(s.shape, k), s.dtype) for s, k in zip(srcs, kinds)],
        in_specs=[ANY] * n, out_specs=[ANY] * n,
        scratch_shapes=[pltpu.SemaphoreType.DMA((per * n,)), pltpu.SemaphoreType.DMA((per * n,)),
                        pltpu.SemaphoreType.DMA((n,))],
    )(*srcs)


def _reduce_pair_exchange(blocked, *, name):
    n = len(blocked)

    def body(*refs):
        b_refs, r_refs = refs[:n], refs[n:2 * n]
        send_sems, recv_sems = refs[2 * n:]
        x, y, c = _my_place()
        copies = [pltpu.make_async_remote_copy(
            src_ref=b_refs[i].at[2 * k + (1 - c)], dst_ref=r_refs[i].at[k],
            send_sem=send_sems.at[N_CHIPS * i + k], recv_sem=recv_sems.at[N_CHIPS * i + k],
            device_id=(x, y, 1 - c), device_id_type=MESH) for i in range(n) for k in range(N_CHIPS)]
        for cp in copies:
            cp.start()
        for cp in copies:
            cp.wait_recv()
        for cp in copies:
            cp.wait_send()

    return pl.pallas_call(
        body, name=name,
        out_shape=[jax.ShapeDtypeStruct((N_CHIPS,) + b.shape[1:], b.dtype) for b in blocked],
        in_specs=[ANY] * n, out_specs=[ANY] * n,
        scratch_shapes=[pltpu.SemaphoreType.DMA((N_CHIPS * n,)), pltpu.SemaphoreType.DMA((N_CHIPS * n,))],
    )(*blocked)


def _reduce_chip_exchange(pair_sums, *, name):
    n = len(pair_sums)
    others = N_CHIPS - 1

    def body(*refs):
        h_refs, r_refs = refs[:n], refs[n:2 * n]
        send_sems, recv_sems = refs[2 * n:]
        x, y, c = _my_place()
        chips = [(1 - x, y), (x, 1 - y), (1 - x, 1 - y)]
        copies = [pltpu.make_async_remote_copy(
            src_ref=h_refs[i].at[2 * px + py], dst_ref=r_refs[i].at[k],
            send_sem=send_sems.at[others * i + k], recv_sem=recv_sems.at[others * i + k],
            device_id=(px, py, c), device_id_type=MESH) for i in range(n) for k, (px, py) in enumerate(chips)]
        for cp in copies:
            cp.start()
        for cp in copies:
            cp.wait_recv()
        for cp in copies:
            cp.wait_send()

    return pl.pallas_call(
        body, name=name,
        out_shape=[jax.ShapeDtypeStruct((others,) + h.shape[1:], h.dtype) for h in pair_sums],
        in_specs=[ANY] * n, out_specs=[ANY] * n,
        scratch_shapes=[pltpu.SemaphoreType.DMA((others * n,)), pltpu.SemaphoreType.DMA((others * n,))],
    )(*pair_sums)


REDUCE_BLOCK_BYTES = 1 << 20


def _row_tile(r, c):
    row_bytes = 4 * (-(-c // LANES) * LANES)
    best = r
    for d in range(SUBLANES, r, SUBLANES):
        if r % d == 0 and d * row_bytes <= REDUCE_BLOCK_BYTES:
            best = d
    return best if r * row_bytes > REDUCE_BLOCK_BYTES else r


def _reduce_pair_sum(blocked, recv, core, *, name):
    _, r, c = blocked.shape
    tr = _row_tile(r, c)

    def body(core_ref, g_ref, r_ref, o_ref):
        o_ref[...] = g_ref[...] + r_ref[...]

    return pl.pallas_call(
        body, name=name,
        grid_spec=pltpu.PrefetchScalarGridSpec(
            num_scalar_prefetch=1, grid=(N_CHIPS, r // tr),
            in_specs=[pl.BlockSpec((None, None, tr, c), lambda k, i, core_ref: (k, core_ref[0], i, 0)),
                      pl.BlockSpec((None, tr, c), lambda k, i, core_ref: (k, i, 0))],
            out_specs=pl.BlockSpec((None, tr, c), lambda k, i, core_ref: (k, i, 0))),
        out_shape=jax.ShapeDtypeStruct((N_CHIPS, r, c), F32),
        compiler_params=_params(("parallel", "parallel")),
    )(core, blocked.reshape(N_CHIPS, 2, r, c), recv)


def _chip_sum(h_ref, r_ref):
    return ((h_ref[...] + r_ref[0]) + r_ref[1]) + r_ref[2]


def _reduce_chip_sum(pair_sum, recv, chip, *, name):
    _, r, c = pair_sum.shape
    tr = _row_tile(r, c)

    def body(chip_ref, h_ref, r_ref, o_ref):
        o_ref[...] = _chip_sum(h_ref, r_ref)

    return pl.pallas_call(
        body, name=name,
        grid_spec=pltpu.PrefetchScalarGridSpec(
            num_scalar_prefetch=1, grid=(r // tr,),
            in_specs=[pl.BlockSpec((None, tr, c), lambda i, chip_ref: (chip_ref[0], i, 0)),
                      pl.BlockSpec((N_CHIPS - 1, tr, c), lambda i, chip_ref: (0, i, 0))],
            out_specs=pl.BlockSpec((tr, c), lambda i, chip_ref: (i, 0))),
        out_shape=jax.ShapeDtypeStruct((r, c), F32),
        compiler_params=_params(("parallel",)),
    )(chip, pair_sum, recv)


def _adamw_math(w, g, m, v):
    nm = ADAM_B1 * m + (1.0 - ADAM_B1) * g
    nv = ADAM_B2 * v + (1.0 - ADAM_B2) * (g * g)
    m_hat = nm / (1.0 - ADAM_B1 ** ADAM_STEP)
    v_hat = nv / (1.0 - ADAM_B2 ** ADAM_STEP)
    return -ADAM_LR * (m_hat / (jnp.sqrt(v_hat) + ADAM_EPS) + ADAM_WD * w), nm, nv


def _adamw(w, g, m, v, *, name):
    shape = w.shape
    C = shape[-1]
    R = math.prod(shape[:-1])
    tr = _row_tile(R, C)

    def body(w_ref, g_ref, m_ref, v_ref, d_ref, nm_ref, nv_ref):
        d_ref[...], nm_ref[...], nv_ref[...] = _adamw_math(w_ref[...], g_ref[...], m_ref[...], v_ref[...])

    spec = pl.BlockSpec((tr, C), lambda i: (i, 0))
    outs = pl.pallas_call(
        body, name=name, grid=(R // tr,),
        in_specs=[spec] * 4, out_specs=[spec] * 3,
        out_shape=[jax.ShapeDtypeStruct((R, C), F32)] * 3,
        compiler_params=_params(("parallel",)),
    )(*[a.reshape(R, C) for a in (w, g, m, v)])
    return tuple(o.reshape(shape) for o in outs)


def _reduce_adamw(pair_sum, recv, chip, w, m, v, layer, prev, *, name):
    _, r, c = pair_sum.shape
    tr = _row_tile(r, c)
    n_prev = 0 if prev is None else len(prev)

    def body(chip_ref, h_ref, r_ref, w_ref, m_ref, v_ref, *rest):
        g_ref, d_ref, nm_ref, nv_ref = rest[n_prev:]
        g = _chip_sum(h_ref, r_ref)
        g_ref[...] = g
        d_ref[...], nm_ref[...], nv_ref[...] = _adamw_math(w_ref[...], g, m_ref[...], v_ref[...])

    slot = pl.BlockSpec((None, tr, c), lambda i, chip_ref: (layer, i, 0))
    out_shape = [jax.ShapeDtypeStruct((DEPTH, r, c), F32)] * 4
    return pl.pallas_call(
        body, name=name,
        grid_spec=pltpu.PrefetchScalarGridSpec(
            num_scalar_prefetch=1, grid=(r // tr,),
            in_specs=[pl.BlockSpec((None, tr, c), lambda i, chip_ref: (chip_ref[0], i, 0)),
                      pl.BlockSpec((N_CHIPS - 1, tr, c), lambda i, chip_ref: (0, i, 0)),
                      slot, slot, slot] + [ANY] * n_prev,
            out_specs=[slot] * 4),
        out_shape=out_shape,
        input_output_aliases={6 + k: k for k in range(n_prev)},
        compiler_params=_params(("parallel",)),
    )(chip, pair_sum, recv, w, m, v, *(prev or ()))


REPLICATED = (("mix_norm", (D_MODEL,)), ("q_norm", (HEAD_DIM,)), ("k_norm", (HEAD_DIM,)), ("sinks", (N_Q_HEADS,)),
              ("sgu_norm", (SGU_WIDTH,)), ("w_s", (SGU_GROUPS, BLOCK, BLOCK)), ("b_s", (SGU_GROUPS, BLOCK)),
              ("ffn_norm", (D_MODEL,)), ("conv_b", (2 * D_FF,)))
SHARDED = (("w_in", "blocks"), ("w_oa", "cols"), ("w_ob", "cols"), ("w_out", "rows"), ("w_up", "blocks"),
           ("conv_w", "blocks"), ("w_down", "rows"))
WEIGHT_ORDER = ("mix_norm", "w_in", "q_norm", "k_norm", "sinks", "sgu_norm", "w_s", "b_s", "w_oa", "w_ob", "w_out",
                "ffn_norm", "w_up", "conv_w", "conv_b", "w_down")


def _small_layout():
    segs, off = {}, 0
    for l in range(DEPTH):
        for name, shape in REPLICATED:
            n = math.prod(shape)
            segs[(l, name)] = (off, n)
            off += n
    per_dev = -(-off // (N_DEV * SUBLANES * LANES)) * SUBLANES * LANES
    return segs, off, per_dev


def _pack_small(grads):
    ssegs, total, per_dev = _small_layout()
    flat = jnp.concatenate([grads[l][name].reshape(-1) for (l, name) in ssegs])
    return jnp.pad(flat, (0, N_DEV * per_dev - total)).reshape(N_DEV, per_dev // LANES, LANES)


def _unpack_small(gathered):
    ssegs, _, _ = _small_layout()
    flat = gathered.reshape(-1)
    shapes = dict(REPLICATED)
    return {name: jnp.stack([flat[ssegs[(l, name)][0]:ssegs[(l, name)][0] + ssegs[(l, name)][1]].reshape(shapes[name])
                             for l in range(DEPTH)]) for name, _ in REPLICATED}


def kernel(x, mix_norm, w_in, q_norm, k_norm, sinks, sgu_norm, w_s, b_s, w_oa, w_ob, w_out, ffn_norm, w_up, conv_w, conv_b, w_down, loss_target, m_mix_norm, m_w_in, m_q_norm, m_k_norm, m_sinks, m_sgu_norm, m_w_s, m_b_s, m_w_oa, m_w_ob, m_w_out, m_ffn_norm, m_w_up, m_conv_w, m_conv_b, m_w_down, v_mix_norm, v_w_in, v_q_norm, v_k_norm, v_sinks, v_sgu_norm, v_w_s, v_b_s, v_w_oa, v_w_ob, v_w_out, v_ffn_norm, v_w_up, v_conv_w, v_conv_b, v_w_down):
    W = dict(mix_norm=mix_norm, w_in=w_in, q_norm=q_norm, k_norm=k_norm, sinks=sinks, sgu_norm=sgu_norm, w_s=w_s, b_s=b_s,
             w_oa=w_oa, w_ob=w_ob, w_out=w_out, ffn_norm=ffn_norm, w_up=w_up, conv_w=conv_w, conv_b=conv_b, w_down=w_down)
    M = dict(mix_norm=m_mix_norm, w_in=m_w_in, q_norm=m_q_norm, k_norm=m_k_norm, sinks=m_sinks, sgu_norm=m_sgu_norm,
             w_s=m_w_s, b_s=m_b_s, w_oa=m_w_oa, w_ob=m_w_ob, w_out=m_w_out, ffn_norm=m_ffn_norm, w_up=m_w_up,
             conv_w=m_conv_w, conv_b=m_conv_b, w_down=m_w_down)
    V = dict(mix_norm=v_mix_norm, w_in=v_w_in, q_norm=v_q_norm, k_norm=v_k_norm, sinks=v_sinks, sgu_norm=v_sgu_norm,
             w_s=v_w_s, b_s=v_b_s, w_oa=v_w_oa, w_ob=v_w_ob, w_out=v_w_out, ffn_norm=v_ffn_norm, w_up=v_w_up,
             conv_w=v_conv_w, conv_b=v_conv_b, w_down=v_w_down)
    n_seq, seq, d_model = x.shape
    tokens = n_seq * seq
    mx, my, mc = _my_place()
    core = mc.reshape(1).astype(jnp.int32)
    chip = (2 * mx + my).reshape(1).astype(jnp.int32)

    items = [(l, name, kind) for l in range(DEPTH) for name, kind in SHARDED]
    srcs = [W[name][l] if name == "conv_w" else W[name][l].astype(BF16) for l, name, _ in items]
    gathered = dict(zip([(l, name) for l, name, _ in items],
                        _gather(srcs, [kind for _, _, kind in items], name="gather_weights")))
    weights = []
    for l in range(DEPTH):
        w = {name: W[name][l] for name, _ in REPLICATED}
        (w["w_in"],) = _assemble(gathered[(l, "w_in")], (IN_WIDTH,), _w_in_moves(), name=f"l{l}_assemble_w_in")
        w["w_up_g"], w["w_up_v"] = _assemble(gathered[(l, "w_up")], (D_FF, D_FF), _w_up_moves(), name=f"l{l}_assemble_w_up")
        for name in ("w_oa", "w_ob", "w_out", "w_down"):
            w[name] = gathered[(l, name)]
        cw = gathered[(l, "conv_w")]
        half = N_DEV // 2
        w["cw_g"] = cw[:half].transpose(1, 0, 2).reshape(3, D_FF)
        w["cw_v"] = cw[half:].transpose(1, 0, 2).reshape(3, D_FF)
        w["cb_g"], w["cb_v"] = W["conv_b"][l][:D_FF], W["conv_b"][l][D_FF:]
        w["bias_full"] = jnp.repeat(W["b_s"][l].T, SGU_WIDTH // SGU_GROUPS, axis=1)
        weights.append(w)

    loss_part, dx, grads = _local_step(x.reshape(tokens, d_model), loss_target.reshape(tokens, d_model), weights,
                                       n_seq=n_seq, seq=seq)
    loss = lax.psum(loss_part, ("x", "y", "c"))

    blocked = {}
    for l in range(DEPTH):
        g = grads[l]
        half = N_DEV // 2
        blocked[(l, "w_in")] = _disassemble((g["w_in"],), W_IN_SHARD, _w_in_moves(), name=f"l{l}_split_dw_in")
        blocked[(l, "w_up")] = _disassemble((g["w_up_g"], g["w_up_v"]), W_UP_SHARD, _w_up_moves(), name=f"l{l}_split_dw_up")
        blocked[(l, "w_oa")] = _disassemble((g["w_oa"],), LANES, _w_o_moves(), name=f"l{l}_split_dw_oa")
        blocked[(l, "w_ob")] = _disassemble((g["w_ob"],), LANES, _w_o_moves(), name=f"l{l}_split_dw_ob")
        blocked[(l, "w_out")] = g["w_out"].reshape(N_DEV, D_MODEL // N_DEV, D_MODEL)
        blocked[(l, "w_down")] = g["w_down"].reshape(N_DEV, D_FF // N_DEV, D_MODEL)
        blocked[(l, "conv_w")] = jnp.concatenate([g[k].reshape(3, half, W_UP_SHARD).transpose(1, 0, 2) for k in ("cw_g", "cw_v")])
        g["conv_b"] = jnp.concatenate([g["cb_g"], g["cb_v"]])
    keys = [(l, name) for l in range(DEPTH) for name, _ in SHARDED] + ["small"]
    blocked["small"] = _pack_small(grads)

    from_sibling = _reduce_pair_exchange([blocked[k] for k in keys], name="reduce_pair_exchange")
    pair_sums = [_reduce_pair_sum(blocked[k], r, core, name=f"reduce_pair_sum_{i}") for i, (k, r) in enumerate(zip(keys, from_sibling))]
    from_chips = dict(zip(keys, _reduce_chip_exchange(pair_sums, name="reduce_chip_exchange")))
    pair_sums = dict(zip(keys, pair_sums))

    G, delta, new_m, new_v = {}, {}, {}, {}
    for name, _ in SHARDED:
        outs = None
        for l in range(DEPTH):
            outs = _reduce_adamw(pair_sums[(l, name)], from_chips[(l, name)], chip, W[name], M[name], V[name], l, outs,
                                 name=f"l{l}_reduce_adamw_{name}")
        G[name], delta[name], new_m[name], new_v[name] = outs
    small = _reduce_chip_sum(pair_sums["small"], from_chips["small"], chip, name="reduce_chip_sum_small")
    G.update(_unpack_small(_gather([small], ["blocks"], name="gather_small_grads")[0]))
    for name, _ in REPLICATED:
        delta[name], new_m[name], new_v[name] = _adamw(W[name], G[name], M[name], V[name], name=f"adamw_{name}")
    return (loss, dx.reshape(n_seq, seq, d_model), *[G[n] for n in WEIGHT_ORDER], *[delta[n] for n in WEIGHT_ORDER],
            *[new_m[n] for n in WEIGHT_ORDER], *[new_v[n] for n in WEIGHT_ORDER])
```

```python
import math

import jax
import jax.numpy as jnp
from jax import lax
from jax.experimental import pallas as pl
from jax.experimental.pallas import tpu as pltpu

F32 = jnp.float32
BF16 = jnp.bfloat16
MESH = pl.DeviceIdType.MESH

DEPTH = 2
D_MODEL = 1024
N_Q_HEADS = 8
HEAD_DIM = 64
ATT_WIDTH = 512
KV_WIDTH = 128
BLOCK = 128
SGU_WIDTH = 512
SGU_GROUPS = 8
IN_WIDTH = 3840
D_FF = 2816
NORM_EPS = 1e-6
NEG_INF = -1e30
ATT_SCALE = HEAD_DIM ** -0.5
ALIBI_SLOPES = tuple(2.0 ** (-(h + 1)) for h in range(N_Q_HEADS))
ADAM_LR, ADAM_B1, ADAM_B2, ADAM_EPS, ADAM_WD, ADAM_STEP = 0.001, 0.9, 0.999, 1e-08, 0.01, 10
N_DEV = 8
N_CHIPS = 4

QKV_WIDTH = ATT_WIDTH + 2 * KV_WIDTH
REST_WIDTH = IN_WIDTH - QKV_WIDTH
COL_SUV, COL_GA, COL_GB, COL_QKV = 0, 1024, 2048, 3072

LANES = 128
SUBLANES = 8
VMEM_LIMIT_V7X = 56 * 1024 * 1024
GELU_C = math.sqrt(2.0 / math.pi)
GELU_K = 0.044715
ANY = pl.BlockSpec(memory_space=pl.ANY)


def _params(sem=None):
    return pltpu.CompilerParams(dimension_semantics=sem, vmem_limit_bytes=VMEM_LIMIT_V7X)


def _sigmoid(x):
    return 1.0 / (1.0 + jnp.exp(-x))


def _gelu(x):
    th = jnp.tanh(GELU_C * (x + GELU_K * x * x * x))
    return 0.5 * x * (1.0 + th)


def _gelu_and_grad(x):
    x2 = x * x
    th = jnp.tanh(GELU_C * (x + GELU_K * x2 * x))
    g = 0.5 * x * (1.0 + th)
    dg = 0.5 * (1.0 + th) + 0.5 * x * (1.0 - th * th) * (GELU_C * (1.0 + 3.0 * GELU_K * x2))
    return g, dg


def _dot(a, b, dims):
    return lax.dot_general(a, b, (dims, ((), ())), preferred_element_type=F32)


def _dot_nn(a, b):
    return _dot(a, b, ((1,), (0,)))


def _dot_nt(a, b):
    return _dot(a, b, ((1,), (1,)))


def _dot_tn(a, b):
    return _dot(a, b, ((0,), (0,)))


def _lo_mask(shape):
    return lax.broadcasted_iota(jnp.int32, shape, len(shape) - 1) < (LANES // 2)


def _half_sums(x, lo):
    s_lo = jnp.sum(jnp.where(lo, x, 0.0), axis=-1, keepdims=True)
    s_all = jnp.sum(x, axis=-1, keepdims=True)
    return jnp.where(lo, s_lo, s_all - s_lo)


def _dup_half(x, half, lo):
    r = pltpu.roll(x, LANES // 2, axis=1)
    return jnp.where(lo, x, r) if half == 0 else jnp.where(lo, r, x)


def _with_deps(body, n_in, deps):
    k = len(deps)
    if not k:
        return body, [], ()

    def skipping(*refs):
        return body(*refs[:n_in], *refs[n_in + k:])

    return skipping, [ANY] * k, tuple(deps)


def _mm(a, b, *, mode, out_dtype, tm, tn, tk, name, epilogue=None, extras=(), deps=()):
    if mode == "nn":
        (M, K), N = a.shape, b.shape[1]
    elif mode == "nt":
        (M, K), N = a.shape, b.shape[0]
    else:
        (K, M), N = a.shape, b.shape[1]
    assert M % tm == 0 and N % tn == 0 and K % tk == 0, (name, M, N, K, tm, tn, tk)
    gm, gn, gk = M // tm, N // tn, K // tk
    if mode == "nn":
        a_spec = pl.BlockSpec((tm, tk), lambda i, j, k: (i, k))
        b_spec = pl.BlockSpec((tk, tn), lambda i, j, k: (k, j))
        contract = ((1,), (0,))
    elif mode == "nt":
        a_spec = pl.BlockSpec((tm, tk), lambda i, j, k: (i, k))
        b_spec = pl.BlockSpec((tn, tk), lambda i, j, k: (j, k))
        contract = ((1,), (1,))
    else:
        a_spec = pl.BlockSpec((tk, tm), lambda i, j, k: (k, i))
        b_spec = pl.BlockSpec((tk, tn), lambda i, j, k: (k, j))
        contract = ((0,), (0,))
    o_spec = pl.BlockSpec((tm, tn), lambda i, j, k: (i, j))
    n_extra = len(extras)

    def finish(acc, extra_refs, o_ref):
        if epilogue is not None:
            acc = epilogue(acc, *[r[...] for r in extra_refs])
        o_ref[...] = acc.astype(out_dtype)

    def body(a_ref, b_ref, *rest):
        extra_refs, o_ref = rest[:n_extra], rest[n_extra]
        part = _dot(a_ref[...].astype(BF16), b_ref[...].astype(BF16), contract)
        if gk == 1:
            finish(part, extra_refs, o_ref)
            return
        acc_ref = rest[n_extra + 1]
        k = pl.program_id(2)

        @pl.when(k == 0)
        def _():
            acc_ref[...] = part

        @pl.when(k > 0)
        def _():
            acc_ref[...] += part

        @pl.when(k == gk - 1)
        def _():
            finish(acc_ref[...], extra_refs, o_ref)

    body, dep_specs, dep_args = _with_deps(body, 2 + n_extra, deps)
    return pl.pallas_call(
        body,
        name=name,
        grid=(gm, gn, gk),
        in_specs=[a_spec, b_spec] + [o_spec] * n_extra + dep_specs,
        out_specs=o_spec,
        out_shape=jax.ShapeDtypeStruct((M, N), out_dtype),
        scratch_shapes=[] if gk == 1 else [pltpu.VMEM((tm, tn), F32)],
        compiler_params=_params(("parallel", "parallel", "arbitrary")),
    )(a, b, *extras, *dep_args)


def _add(acc, r):
    return acc + r


def _rms_fwd(x, gain, *, name, tm=512):
    T, D = x.shape

    def body(x_ref, g_ref, h_ref):
        xv = x_ref[...]
        r = lax.rsqrt(jnp.mean(xv * xv, axis=-1, keepdims=True) + NORM_EPS)
        h_ref[...] = (xv * r * g_ref[...]).astype(BF16)

    return pl.pallas_call(
        body, name=name, grid=(T // tm,),
        in_specs=[pl.BlockSpec((tm, D), lambda i: (i, 0)), pl.BlockSpec((1, D), lambda i: (0, 0))],
        out_specs=pl.BlockSpec((tm, D), lambda i: (i, 0)),
        out_shape=jax.ShapeDtypeStruct((T, D), BF16),
        compiler_params=_params(("parallel",)),
    )(x, gain.reshape(1, D))


def _rms_bwd(x, gain, dh, dres, *, name, tm=512, deps=()):
    T, D = x.shape

    def body(x_ref, g_ref, dh_ref, dres_ref, dx_ref, dg_ref):
        xv = x_ref[...]
        r = lax.rsqrt(jnp.mean(xv * xv, axis=-1, keepdims=True) + NORM_EPS)
        xh = xv * r
        dhv = dh_ref[...]
        dxh = dhv * g_ref[...]
        dx = r * (dxh - xh * jnp.mean(dxh * xh, axis=-1, keepdims=True))
        dx_ref[...] = dres_ref[...] + dx
        part = jnp.sum(dhv * xh, axis=0, keepdims=True)

        @pl.when(pl.program_id(0) == 0)
        def _():
            dg_ref[...] = part

        @pl.when(pl.program_id(0) > 0)
        def _():
            dg_ref[...] += part

    row = pl.BlockSpec((tm, D), lambda i: (i, 0))
    vec = pl.BlockSpec((1, D), lambda i: (0, 0))
    body, dep_specs, dep_args = _with_deps(body, 4, deps)
    dx, dg = pl.pallas_call(
        body, name=name, grid=(T // tm,),
        in_specs=[row, vec, row, row] + dep_specs,
        out_specs=[row, vec],
        out_shape=[jax.ShapeDtypeStruct((T, D), F32), jax.ShapeDtypeStruct((1, D), F32)],
        compiler_params=_params(("arbitrary",)),
    )(x, gain.reshape(1, D), dh, dres, *dep_args)
    return dx, dg.reshape(D)


def _head_norm(x, gain2, lo):
    ms = _half_sums(x * x, lo) * (1.0 / HEAD_DIM)
    r = lax.rsqrt(ms + NORM_EPS)
    xh = x * r
    return xh * gain2, xh, r


def _head_norm_bwd(xh, r, gain2, dy, lo):
    dxh = dy * gain2
    dx = r * (dxh - xh * (_half_sums(dxh * xh, lo) * (1.0 / HEAD_DIM)))
    return dx, dy * xh


def _att_masks():
    qi = lax.broadcasted_iota(jnp.int32, (BLOCK, BLOCK), 0)
    kj = lax.broadcasted_iota(jnp.int32, (BLOCK, BLOCK), 1)
    d_cur = qi - kj
    d_prev = qi - kj + BLOCK
    return d_cur >= 0, d_prev < BLOCK, d_cur.astype(F32), d_prev.astype(F32)


def _att_probs(qm, k2c, k2p, sink, slope, masks, has_prev):
    ok_c, ok_p, d_c, d_p = masks
    s_c = jnp.where(ok_c, _dot_nt(qm, k2c) * ATT_SCALE - slope * d_c, NEG_INF)
    s_p = jnp.where(jnp.logical_and(ok_p, has_prev), _dot_nt(qm, k2p) * ATT_SCALE - slope * d_p, NEG_INF)
    m = jnp.maximum(jnp.maximum(jnp.max(s_c, axis=-1, keepdims=True), jnp.max(s_p, axis=-1, keepdims=True)), sink)
    e_c = jnp.exp(s_c - m)
    e_p = jnp.exp(s_p - m)
    e_s = jnp.exp(sink - m)
    inv = 1.0 / (jnp.sum(e_c, axis=-1, keepdims=True) + jnp.sum(e_p, axis=-1, keepdims=True) + e_s)
    return e_c * inv, e_p * inv, e_s * inv


def _attention_fwd(proj, q_gain, k_gain, sinks, *, n_seq, seq, name):
    T = n_seq * seq
    nb = seq // BLOCK
    qcol, kvcol = COL_QKV // ATT_WIDTH, (COL_QKV + ATT_WIDTH) // (2 * KV_WIDTH)

    def body(q_ref, kv_ref, qg_ref, kg_ref, sink_ref, y_ref):
        lo = _lo_mask((BLOCK, LANES))
        masks = _att_masks()
        qg, kg = qg_ref[...], kg_ref[...]

        def block(i, carry):
            r0 = pl.multiple_of(i * BLOCK, BLOCK)
            rp = pl.multiple_of(jnp.maximum(i - 1, 0) * BLOCK, BLOCK)
            has_prev = i > 0
            kn_c = _head_norm(kv_ref[pl.ds(r0, BLOCK), 0:KV_WIDTH], kg, lo)[0].astype(BF16)
            kn_p = _head_norm(kv_ref[pl.ds(rp, BLOCK), 0:KV_WIDTH], kg, lo)[0].astype(BF16)
            v_c = kv_ref[pl.ds(r0, BLOCK), KV_WIDTH:2 * KV_WIDTH].astype(BF16)
            v_p = kv_ref[pl.ds(rp, BLOCK), KV_WIDTH:2 * KV_WIDTH].astype(BF16)
            for pair in range(N_Q_HEADS // 2):
                kv = pair // 2
                k2c, k2p = _dup_half(kn_c, kv, lo), _dup_half(kn_p, kv, lo)
                v2c, v2p = _dup_half(v_c, kv, lo), _dup_half(v_p, kv, lo)
                qn = _head_norm(q_ref[pl.ds(r0, BLOCK), pair * LANES:(pair + 1) * LANES], qg, lo)[0]
                out = None
                for half in range(2):
                    h = 2 * pair + half
                    mine = lo if half == 0 else jnp.logical_not(lo)
                    qm = jnp.where(mine, qn, 0.0).astype(BF16)
                    p_c, p_p, _ = _att_probs(qm, k2c, k2p, sink_ref[h], ALIBI_SLOPES[h], masks, has_prev)
                    o = _dot_nn(p_c.astype(BF16), v2c) + _dot_nn(p_p.astype(BF16), v2p)
                    out = o if out is None else jnp.where(lo, out, o)
                y_ref[pl.ds(r0, BLOCK), pair * LANES:(pair + 1) * LANES] = out.astype(BF16)
            return carry

        lax.fori_loop(0, nb, block, 0)

    vec = pl.BlockSpec((1, LANES), lambda b: (0, 0))
    return pl.pallas_call(
        body, name=name, grid=(n_seq,),
        in_specs=[pl.BlockSpec((seq, ATT_WIDTH), lambda b: (b, qcol)),
                  pl.BlockSpec((seq, 2 * KV_WIDTH), lambda b: (b, kvcol)),
                  vec, vec, pl.BlockSpec(memory_space=pltpu.SMEM)],
        out_specs=pl.BlockSpec((seq, ATT_WIDTH), lambda b: (b, 0)),
        out_shape=jax.ShapeDtypeStruct((T, ATT_WIDTH), BF16),
        compiler_params=_params(("parallel",)),
    )(proj, proj, jnp.tile(q_gain, 2).reshape(1, LANES), jnp.tile(k_gain, 2).reshape(1, LANES), sinks)


def _attention_bwd(proj, dy, q_gain, k_gain, sinks, *, n_seq, seq, name, deps=()):
    T = n_seq * seq
    nb = seq // BLOCK
    qcol, kvcol = COL_QKV // ATT_WIDTH, (COL_QKV + ATT_WIDTH) // (2 * KV_WIDTH)

    def body(q_ref, kv_ref, dy_ref, qg_ref, kg_ref, sink_ref, dqkv_ref, dqg_ref, dkg_ref, dsink_ref,
             dkn_acc, dv_acc, qg_acc, kg_acc, sink_acc):
        lo = _lo_mask((BLOCK, LANES))
        hi = jnp.logical_not(lo)
        lane = lax.broadcasted_iota(jnp.int32, (BLOCK, LANES), 1)
        masks = _att_masks()
        qg, kg = qg_ref[...], kg_ref[...]
        first = pl.program_id(0) == 0

        @pl.when(first)
        def _():
            qg_acc[...] = jnp.zeros_like(qg_acc)
            kg_acc[...] = jnp.zeros_like(kg_acc)
            sink_acc[...] = jnp.zeros_like(sink_acc)

        dkn_acc[...] = jnp.zeros_like(dkn_acc)
        dv_acc[...] = jnp.zeros_like(dv_acc)

        def block(i, carry):
            r0 = pl.multiple_of(i * BLOCK, BLOCK)
            rp = pl.multiple_of(jnp.maximum(i - 1, 0) * BLOCK, BLOCK)
            has_prev = i > 0
            kn_c = _head_norm(kv_ref[pl.ds(r0, BLOCK), 0:KV_WIDTH], kg, lo)[0].astype(BF16)
            kn_p = _head_norm(kv_ref[pl.ds(rp, BLOCK), 0:KV_WIDTH], kg, lo)[0].astype(BF16)
            v_c = kv_ref[pl.ds(r0, BLOCK), KV_WIDTH:2 * KV_WIDTH].astype(BF16)
            v_p = kv_ref[pl.ds(rp, BLOCK), KV_WIDTH:2 * KV_WIDTH].astype(BF16)
            dk_c = [jnp.zeros((BLOCK, LANES), F32) for _ in range(2)]
            dk_p = [jnp.zeros((BLOCK, LANES), F32) for _ in range(2)]
            dv_c = [jnp.zeros((BLOCK, LANES), F32) for _ in range(2)]
            dv_p = [jnp.zeros((BLOCK, LANES), F32) for _ in range(2)]
            for pair in range(N_Q_HEADS // 2):
                kv = pair // 2
                cols = slice(pair * LANES, (pair + 1) * LANES)
                k2c, k2p = _dup_half(kn_c, kv, lo), _dup_half(kn_p, kv, lo)
                v2c, v2p = _dup_half(v_c, kv, lo), _dup_half(v_p, kv, lo)
                qn, qh, qr = _head_norm(q_ref[pl.ds(r0, BLOCK), cols], qg, lo)
                do_pair = dy_ref[pl.ds(r0, BLOCK), cols]
                dqn = None
                for half in range(2):
                    h = 2 * pair + half
                    mine = lo if half == 0 else hi
                    qm = jnp.where(mine, qn, 0.0).astype(BF16)
                    dom = jnp.where(mine, do_pair, jnp.zeros_like(do_pair))
                    p_c, p_p, p_s = _att_probs(qm, k2c, k2p, sink_ref[h], ALIBI_SLOPES[h], masks, has_prev)
                    dp_c = _dot_nt(dom, v2c)
                    dp_p = _dot_nt(dom, v2p)
                    delta = jnp.sum(p_c * dp_c, axis=-1, keepdims=True) + jnp.sum(p_p * dp_p, axis=-1, keepdims=True)
                    ds_c = (p_c * (dp_c - delta)).astype(BF16)
                    ds_p = (p_p * (dp_p - delta)).astype(BF16)
                    sink_acc[...] += jnp.where(lane == h, -(p_s * delta), 0.0)
                    dq_h = (_dot_nn(ds_c, k2c) + _dot_nn(ds_p, k2p)) * ATT_SCALE
                    dqn = dq_h if dqn is None else jnp.where(lo, dqn, dq_h)
                    dk_c[kv] = dk_c[kv] + _dot_tn(ds_c, qm)
                    dk_p[kv] = dk_p[kv] + _dot_tn(ds_p, qm)
                    dv_c[kv] = dv_c[kv] + _dot_tn(p_c.astype(BF16), dom)
                    dv_p[kv] = dv_p[kv] + _dot_tn(p_p.astype(BF16), dom)
                dq, dg = _head_norm_bwd(qh, qr, qg, dqn, lo)
                dqkv_ref[pl.ds(r0, BLOCK), cols] = dq.astype(BF16)
                qg_acc[...] += dg

            def fold(parts):
                a = parts[0] + pltpu.roll(parts[0], LANES // 2, axis=1)
                b = parts[1] + pltpu.roll(parts[1], LANES // 2, axis=1)
                return jnp.where(lo, a, b)

            dkn_acc[pl.ds(r0, BLOCK), :] += fold(dk_c) * ATT_SCALE
            dkn_acc[pl.ds(rp, BLOCK), :] += fold(dk_p) * ATT_SCALE
            dv_acc[pl.ds(r0, BLOCK), :] += fold(dv_c)
            dv_acc[pl.ds(rp, BLOCK), :] += fold(dv_p)
            return carry

        lax.fori_loop(0, nb, block, 0)

        def finish(i, carry):
            r0 = pl.multiple_of(i * BLOCK, BLOCK)
            _, kh, kr = _head_norm(kv_ref[pl.ds(r0, BLOCK), 0:KV_WIDTH], kg, lo)
            dk, dg = _head_norm_bwd(kh, kr, kg, dkn_acc[pl.ds(r0, BLOCK), :], lo)
            dqkv_ref[pl.ds(r0, BLOCK), ATT_WIDTH:ATT_WIDTH + KV_WIDTH] = dk.astype(BF16)
            dqkv_ref[pl.ds(r0, BLOCK), ATT_WIDTH + KV_WIDTH:QKV_WIDTH] = dv_acc[pl.ds(r0, BLOCK), :].astype(BF16)
            kg_acc[...] += dg
            return carry

        lax.fori_loop(0, nb, finish, 0)

        @pl.when(pl.program_id(0) == n_seq - 1)
        def _():
            dqg_ref[...] = jnp.sum(qg_acc[...], axis=0, keepdims=True)
            dkg_ref[...] = jnp.sum(kg_acc[...], axis=0, keepdims=True)
            dsink_ref[...] = jnp.sum(sink_acc[...], axis=0, keepdims=True)

    vec = pl.BlockSpec((1, LANES), lambda b: (0, 0))
    acc = pltpu.VMEM((BLOCK, LANES), F32)
    body, dep_specs, dep_args = _with_deps(body, 6, deps)
    dqkv, dqg, dkg, dsink = pl.pallas_call(
        body, name=name, grid=(n_seq,),
        in_specs=[pl.BlockSpec((seq, ATT_WIDTH), lambda b: (b, qcol)),
                  pl.BlockSpec((seq, 2 * KV_WIDTH), lambda b: (b, kvcol)),
                  pl.BlockSpec((seq, ATT_WIDTH), lambda b: (b, 0)),
                  vec, vec, pl.BlockSpec(memory_space=pltpu.SMEM)] + dep_specs,
        out_specs=[pl.BlockSpec((seq, QKV_WIDTH), lambda b: (b, 0)), vec, vec, vec],
        out_shape=[jax.ShapeDtypeStruct((T, QKV_WIDTH), BF16)] + [jax.ShapeDtypeStruct((1, LANES), F32)] * 3,
        scratch_shapes=[pltpu.VMEM((seq, KV_WIDTH), F32), pltpu.VMEM((seq, KV_WIDTH), F32), acc, acc, acc],
        compiler_params=_params(("arbitrary",)),
    )(proj, proj, dy, jnp.tile(q_gain, 2).reshape(1, LANES), jnp.tile(k_gain, 2).reshape(1, LANES), sinks, *dep_args)
    half = LANES // 2
    return dqkv, dqg[0, :half] + dqg[0, half:], dkg[0, :half] + dkg[0, half:], dsink[0, :N_Q_HEADS]


def _sgu_weights(w_ref):
    r = lax.broadcasted_iota(jnp.int32, (BLOCK, BLOCK), 0)
    c = lax.broadcasted_iota(jnp.int32, (BLOCK, BLOCK), 1)
    return [jnp.where(r >= c, w_ref[g], 0.0).astype(BF16) for g in range(SGU_GROUPS)]


def _sgu_fwd(proj, gain, w_s, bias_full, *, n_seq, seq, name):
    T = n_seq * seq
    nc = seq // BLOCK

    def body(suv_ref, g_ref, w_ref, b_ref, y_ref):
        lo = _lo_mask((BLOCK, LANES))
        wm = _sgu_weights(w_ref)
        gain_v = g_ref[...]

        def chunk(c, carry):
            r0 = pl.multiple_of(c * BLOCK, BLOCK)
            gv = _gelu(suv_ref[pl.ds(r0, BLOCK), SGU_WIDTH:2 * SGU_WIDTH])
            r = lax.rsqrt(jnp.mean(gv * gv, axis=-1, keepdims=True) + NORM_EPS)
            vn = (gv * r * gain_v).astype(BF16)
            for p in range(SGU_WIDTH // LANES):
                cols = slice(p * LANES, (p + 1) * LANES)
                vp = vn[:, cols]
                mixed = jnp.where(lo, _dot_nn(wm[2 * p], vp), _dot_nn(wm[2 * p + 1], vp)) + b_ref[:, cols]
                u = _gelu(suv_ref[pl.ds(r0, BLOCK), cols])
                y_ref[pl.ds(r0, BLOCK), cols] = (u * mixed).astype(BF16)
            return carry

        lax.fori_loop(0, nc, chunk, 0)

    return pl.pallas_call(
        body, name=name, grid=(n_seq,),
        in_specs=[pl.BlockSpec((seq, 2 * SGU_WIDTH), lambda b: (b, COL_SUV // (2 * SGU_WIDTH))),
                  pl.BlockSpec((1, SGU_WIDTH), lambda b: (0, 0)),
                  pl.BlockSpec((SGU_GROUPS, BLOCK, BLOCK), lambda b: (0, 0, 0)),
                  pl.BlockSpec((BLOCK, SGU_WIDTH), lambda b: (0, 0))],
        out_specs=pl.BlockSpec((seq, SGU_WIDTH), lambda b: (b, 0)),
        out_shape=jax.ShapeDtypeStruct((T, SGU_WIDTH), BF16),
        compiler_params=_params(("parallel",)),
    )(proj, gain.reshape(1, SGU_WIDTH), w_s, bias_full)


def _sgu_bwd(proj, dy, gain, w_s, bias_full, *, n_seq, seq, name, deps=()):
    T = n_seq * seq
    nc = seq // BLOCK
    n_tiles = SGU_WIDTH // LANES

    def body(suv_ref, dy_ref, g_ref, w_ref, b_ref, dsuv_ref, dg_ref, dw_ref, db_ref, dg_acc, dw_acc, db_acc):
        lo = _lo_mask((BLOCK, LANES))
        hi = jnp.logical_not(lo)
        wm = _sgu_weights(w_ref)
        wmt = [jnp.where(lax.broadcasted_iota(jnp.int32, (BLOCK, BLOCK), 1) >= lax.broadcasted_iota(jnp.int32, (BLOCK, BLOCK), 0),
                         w_ref[g].T, 0.0).astype(BF16) for g in range(SGU_GROUPS)]
        gain_v = g_ref[...]

        @pl.when(pl.program_id(0) == 0)
        def _():
            dg_acc[...] = jnp.zeros_like(dg_acc)
            dw_acc[...] = jnp.zeros_like(dw_acc)
            db_acc[...] = jnp.zeros_like(db_acc)

        def chunk(c, carry):
            r0 = pl.multiple_of(c * BLOCK, BLOCK)
            gv, dgelu_v = _gelu_and_grad(suv_ref[pl.ds(r0, BLOCK), SGU_WIDTH:2 * SGU_WIDTH])
            r = lax.rsqrt(jnp.mean(gv * gv, axis=-1, keepdims=True) + NORM_EPS)
            vh = gv * r
            vn = (vh * gain_v).astype(BF16)
            dvn_tiles = []
            for p in range(n_tiles):
                cols = slice(p * LANES, (p + 1) * LANES)
                vp = vn[:, cols]
                mixed = jnp.where(lo, _dot_nn(wm[2 * p], vp), _dot_nn(wm[2 * p + 1], vp)) + b_ref[:, cols]
                u, dgelu_u = _gelu_and_grad(suv_ref[pl.ds(r0, BLOCK), cols])
                dyv = dy_ref[pl.ds(r0, BLOCK), cols]
                dsuv_ref[pl.ds(r0, BLOCK), cols] = (dyv * mixed * dgelu_u).astype(BF16)
                dm = dyv * u
                db_acc[:, cols] += dm
                dm_bf = dm.astype(BF16)
                dvn_tiles.append(jnp.where(lo, _dot_nn(wmt[2 * p], dm_bf), _dot_nn(wmt[2 * p + 1], dm_bf)))
                dw_acc[2 * p] += _dot_nt(jnp.where(lo, dm, 0.0).astype(BF16), vp)
                dw_acc[2 * p + 1] += _dot_nt(jnp.where(hi, dm, 0.0).astype(BF16), vp)
            dvn = jnp.concatenate(dvn_tiles, axis=1)
            dg_acc[...] += dvn * vh
            dvh = dvn * gain_v
            dgv = r * (dvh - vh * jnp.mean(dvh * vh, axis=-1, keepdims=True))
            dsuv_ref[pl.ds(r0, BLOCK), SGU_WIDTH:2 * SGU_WIDTH] = (dgv * dgelu_v).astype(BF16)
            return carry

        lax.fori_loop(0, nc, chunk, 0)

        @pl.when(pl.program_id(0) == n_seq - 1)
        def _():
            dg_ref[...] = jnp.sum(dg_acc[...], axis=0, keepdims=True)
            r = lax.broadcasted_iota(jnp.int32, (BLOCK, BLOCK), 0)
            c = lax.broadcasted_iota(jnp.int32, (BLOCK, BLOCK), 1)
            for g in range(SGU_GROUPS):
                dw_ref[g] = jnp.where(r >= c, dw_acc[g], 0.0)
            lane = lax.broadcasted_iota(jnp.int32, (BLOCK, LANES), 1)
            out = jnp.zeros((BLOCK, LANES), F32)
            for p in range(n_tiles):
                tile = db_acc[:, p * LANES:(p + 1) * LANES]
                s_lo = jnp.sum(jnp.where(lo, tile, 0.0), axis=-1, keepdims=True)
                s_hi = jnp.sum(jnp.where(hi, tile, 0.0), axis=-1, keepdims=True)
                out = jnp.where(lane == 2 * p, s_lo, out)
                out = jnp.where(lane == 2 * p + 1, s_hi, out)
            db_ref[...] = out

    body, dep_specs, dep_args = _with_deps(body, 5, deps)
    dsuv, dg, dw, db = pl.pallas_call(
        body, name=name, grid=(n_seq,),
        in_specs=[pl.BlockSpec((seq, 2 * SGU_WIDTH), lambda b: (b, COL_SUV // (2 * SGU_WIDTH))),
                  pl.BlockSpec((seq, SGU_WIDTH), lambda b: (b, 0)),
                  pl.BlockSpec((1, SGU_WIDTH), lambda b: (0, 0)),
                  pl.BlockSpec((SGU_GROUPS, BLOCK, BLOCK), lambda b: (0, 0, 0)),
                  pl.BlockSpec((BLOCK, SGU_WIDTH), lambda b: (0, 0))] + dep_specs,
        out_specs=[pl.BlockSpec((seq, 2 * SGU_WIDTH), lambda b: (b, 0)),
                   pl.BlockSpec((1, SGU_WIDTH), lambda b: (0, 0)),
                   pl.BlockSpec((SGU_GROUPS, BLOCK, BLOCK), lambda b: (0, 0, 0)),
                   pl.BlockSpec((BLOCK, LANES), lambda b: (0, 0))],
        out_shape=[jax.ShapeDtypeStruct((T, 2 * SGU_WIDTH), BF16), jax.ShapeDtypeStruct((1, SGU_WIDTH), F32),
                   jax.ShapeDtypeStruct((SGU_GROUPS, BLOCK, BLOCK), F32), jax.ShapeDtypeStruct((BLOCK, LANES), F32)],
        scratch_shapes=[pltpu.VMEM((BLOCK, SGU_WIDTH), F32), pltpu.VMEM((SGU_GROUPS, BLOCK, BLOCK), F32),
                        pltpu.VMEM((BLOCK, SGU_WIDTH), F32)],
        compiler_params=_params(("arbitrary",)),
    )(proj, dy, gain.reshape(1, SGU_WIDTH), w_s, bias_full, *dep_args)
    return dsuv, dg.reshape(SGU_WIDTH), dw, db[:, :SGU_GROUPS].T


def _merge_fwd(y_att, y_sgu, w_oa, w_ob, proj, *, name, tm=512, tn=512, deps=()):
    T = y_att.shape[0]

    def body(ya_ref, ys_ref, wa_ref, wb_ref, ga_ref, gb_ref, o_ref):
        pa = _dot_nn(ya_ref[...], wa_ref[...])
        pb = _dot_nn(ys_ref[...], wb_ref[...])
        o_ref[...] = (_sigmoid(ga_ref[...]) * pa + _sigmoid(gb_ref[...]) * pb).astype(BF16)

    act = pl.BlockSpec((tm, ATT_WIDTH), lambda i, j: (i, 0))
    wgt = pl.BlockSpec((ATT_WIDTH, tn), lambda i, j: (0, j))
    body, dep_specs, dep_args = _with_deps(body, 6, deps)
    return pl.pallas_call(
        body, name=name, grid=(T // tm, D_MODEL // tn),
        in_specs=[act, act, wgt, wgt,
                  pl.BlockSpec((tm, tn), lambda i, j: (i, j + COL_GA // tn)),
                  pl.BlockSpec((tm, tn), lambda i, j: (i, j + COL_GB // tn))] + dep_specs,
        out_specs=pl.BlockSpec((tm, tn), lambda i, j: (i, j)),
        out_shape=jax.ShapeDtypeStruct((T, D_MODEL), BF16),
        compiler_params=_params(("parallel", "parallel")),
    )(y_att, y_sgu, w_oa, w_ob, proj, proj, *dep_args)


def _merge_bwd(dx1_bf, w_out, y_att, y_sgu, w_oa, w_ob, proj, *, name, tm=512, tn=512):
    T = y_att.shape[0]

    def body(dx_ref, wo_ref, ya_ref, ys_ref, wa_ref, wb_ref, ga_ref, gb_ref, dpa_ref, dpb_ref, dga_ref, dgb_ref):
        dm = _dot_nt(dx_ref[...], wo_ref[...])
        pa = _dot_nn(ya_ref[...], wa_ref[...])
        pb = _dot_nn(ys_ref[...], wb_ref[...])
        sa = _sigmoid(ga_ref[...])
        sb = _sigmoid(gb_ref[...])
        dpa_ref[...] = (dm * sa).astype(BF16)
        dpb_ref[...] = (dm * sb).astype(BF16)
        dga_ref[...] = (dm * pa * sa * (1.0 - sa)).astype(BF16)
        dgb_ref[...] = (dm * pb * sb * (1.0 - sb)).astype(BF16)

    act = pl.BlockSpec((tm, ATT_WIDTH), lambda i, j: (i, 0))
    wgt = pl.BlockSpec((ATT_WIDTH, tn), lambda i, j: (0, j))
    out = pl.BlockSpec((tm, tn), lambda i, j: (i, j))
    return pl.pallas_call(
        body, name=name, grid=(T // tm, D_MODEL // tn),
        in_specs=[pl.BlockSpec((tm, D_MODEL), lambda i, j: (i, 0)),
                  pl.BlockSpec((tn, D_MODEL), lambda i, j: (j, 0)),
                  act, act, wgt, wgt,
                  pl.BlockSpec((tm, tn), lambda i, j: (i, j + COL_GA // tn)),
                  pl.BlockSpec((tm, tn), lambda i, j: (i, j + COL_GB // tn))],
        out_specs=[out] * 4,
        out_shape=[jax.ShapeDtypeStruct((T, D_MODEL), BF16)] * 4,
        compiler_params=_params(("parallel", "parallel")),
    )(dx1_bf, w_out, y_att, y_sgu, w_oa, w_ob, proj, proj)


CONV_ROWS = 256
CONV_TN = 256


def _shift_rows(cur, prev8, k):
    rolled = pltpu.roll(cur, k, axis=0)
    head = jnp.where(lax.broadcasted_iota(jnp.int32, prev8.shape, 0) < k, pltpu.roll(prev8, k, axis=0), rolled[:SUBLANES])
    return jnp.concatenate([head, rolled[SUBLANES:]], axis=0)


def _shift_rows_up(cur, next8, k):
    n = cur.shape[0]
    rolled = pltpu.roll(cur, n - k, axis=0)
    tail = jnp.where(lax.broadcasted_iota(jnp.int32, next8.shape, 0) >= SUBLANES - k,
                     pltpu.roll(next8, SUBLANES - k, axis=0), rolled[n - SUBLANES:])
    return jnp.concatenate([rolled[:n - SUBLANES], tail], axis=0)


def _conv_rows(z_ref, r0, first, w_ref, b_ref, rows):
    cur = z_ref[pl.ds(r0, rows), :]
    rp = pl.multiple_of(jnp.maximum(r0 - SUBLANES, 0), SUBLANES)
    prev8 = jnp.where(first, 0.0, z_ref[pl.ds(rp, SUBLANES), :])
    z1 = _shift_rows(cur, prev8, 1)
    z2 = _shift_rows(cur, prev8, 2)
    return b_ref[...] + w_ref[0:1, :] * z2 + w_ref[1:2, :] * z1 + w_ref[2:3, :] * cur


def _conv_fwd(z_g, z_v, cw_g, cw_v, cb_g, cb_v, *, n_seq, seq, name):
    T = n_seq * seq
    tn, rows = CONV_TN, CONV_ROWS

    def body(zg_ref, zv_ref, wg_ref, wv_ref, bg_ref, bv_ref, a_ref):
        def step(s, carry):
            r0 = pl.multiple_of(s * rows, rows)
            first = s == 0
            g = _conv_rows(zg_ref, r0, first, wg_ref, bg_ref, rows)
            v = _conv_rows(zv_ref, r0, first, wv_ref, bv_ref, rows)
            a_ref[pl.ds(r0, rows), :] = (g * _sigmoid(g) * v).astype(BF16)
            return carry

        lax.fori_loop(0, seq // rows, step, 0)

    zs = pl.BlockSpec((seq, tn), lambda b, j: (b, j))
    ws = pl.BlockSpec((3, tn), lambda b, j: (0, j))
    bs = pl.BlockSpec((1, tn), lambda b, j: (0, j))
    return pl.pallas_call(
        body, name=name, grid=(n_seq, D_FF // tn),
        in_specs=[zs, zs, ws, ws, bs, bs], out_specs=zs,
        out_shape=jax.ShapeDtypeStruct((T, D_FF), BF16),
        compiler_params=_params(("parallel", "parallel")),
    )(z_g, z_v, cw_g, cw_v, cb_g.reshape(1, D_FF), cb_v.reshape(1, D_FF))


def _conv_bwd(z_g, z_v, da, cw_g, cw_v, cb_g, cb_v, *, n_seq, seq, name):
    T = n_seq * seq
    tn, rows = CONV_TN, CONV_ROWS
    n_steps = seq // rows

    def body(zg_ref, zv_ref, da_ref, wg_ref, wv_ref, bg_ref, bv_ref,
             dzg_ref, dzv_ref, dwg_ref, dwv_ref, dbg_ref, dbv_ref, dcg_ref, dcv_ref):
        def grads(s, accs):
            r0 = pl.multiple_of(s * rows, rows)
            first = s == 0
            cur_g = zg_ref[pl.ds(r0, rows), :]
            cur_v = zv_ref[pl.ds(r0, rows), :]
            rp = pl.multiple_of(jnp.maximum(r0 - SUBLANES, 0), SUBLANES)
            pg = jnp.where(first, 0.0, zg_ref[pl.ds(rp, SUBLANES), :])
            pv = jnp.where(first, 0.0, zv_ref[pl.ds(rp, SUBLANES), :])
            g1, g2 = _shift_rows(cur_g, pg, 1), _shift_rows(cur_g, pg, 2)
            v1, v2 = _shift_rows(cur_v, pv, 1), _shift_rows(cur_v, pv, 2)
            g = bg_ref[...] + wg_ref[0:1, :] * g2 + wg_ref[1:2, :] * g1 + wg_ref[2:3, :] * cur_g
            v = bv_ref[...] + wv_ref[0:1, :] * v2 + wv_ref[1:2, :] * v1 + wv_ref[2:3, :] * cur_v
            sg = _sigmoid(g)
            dav = da_ref[pl.ds(r0, rows), :]
            dcg = dav * v * (sg * (1.0 + g * (1.0 - sg)))
            dcv = dav * (g * sg)
            dcg_ref[pl.ds(r0, rows), :] = dcg
            dcv_ref[pl.ds(r0, rows), :] = dcv

            def colsum(x):
                return jnp.sum(x, axis=0, keepdims=True)

            return (accs[0] + colsum(dcg * g2), accs[1] + colsum(dcg * g1), accs[2] + colsum(dcg * cur_g), accs[3] + colsum(dcg),
                    accs[4] + colsum(dcv * v2), accs[5] + colsum(dcv * v1), accs[6] + colsum(dcv * cur_v), accs[7] + colsum(dcv))

        zero = jnp.zeros((1, tn), F32)
        sums = lax.fori_loop(0, n_steps, grads, (zero,) * 8)
        first_seq = pl.program_id(1) == 0

        @pl.when(first_seq)
        def _():
            dwg_ref[...] = jnp.concatenate(sums[0:3], axis=0)
            dbg_ref[...] = sums[3]
            dwv_ref[...] = jnp.concatenate(sums[4:7], axis=0)
            dbv_ref[...] = sums[7]

        @pl.when(jnp.logical_not(first_seq))
        def _():
            dwg_ref[...] += jnp.concatenate(sums[0:3], axis=0)
            dbg_ref[...] += sums[3]
            dwv_ref[...] += jnp.concatenate(sums[4:7], axis=0)
            dbv_ref[...] += sums[7]

        def back(s, carry):
            r0 = pl.multiple_of(s * rows, rows)
            last = s == n_steps - 1
            rn = pl.multiple_of(jnp.minimum(r0 + rows, seq - SUBLANES), SUBLANES)
            for dc_ref, w_ref, dz_ref in ((dcg_ref, wg_ref, dzg_ref), (dcv_ref, wv_ref, dzv_ref)):
                cur = dc_ref[pl.ds(r0, rows), :]
                nxt = jnp.where(last, 0.0, dc_ref[pl.ds(rn, SUBLANES), :])
                u1, u2 = _shift_rows_up(cur, nxt, 1), _shift_rows_up(cur, nxt, 2)
                dz_ref[pl.ds(r0, rows), :] = (w_ref[2:3, :] * cur + w_ref[1:2, :] * u1 + w_ref[0:1, :] * u2).astype(BF16)
            return carry

        lax.fori_loop(0, n_steps, back, 0)

    zs = pl.BlockSpec((seq, tn), lambda j, b: (b, j))
    ws = pl.BlockSpec((3, tn), lambda j, b: (0, j))
    bs = pl.BlockSpec((1, tn), lambda j, b: (0, j))
    outs = pl.pallas_call(
        body, name=name, grid=(D_FF // tn, n_seq),
        in_specs=[zs, zs, zs, ws, ws, bs, bs],
        out_specs=[zs, zs, ws, ws, bs, bs],
        out_shape=[jax.ShapeDtypeStruct((T, D_FF), BF16)] * 2 + [jax.ShapeDtypeStruct((3, D_FF), F32)] * 2
        + [jax.ShapeDtypeStruct((1, D_FF), F32)] * 2,
        scratch_shapes=[pltpu.VMEM((seq, tn), F32), pltpu.VMEM((seq, tn), F32)],
        compiler_params=_params(("parallel", "arbitrary")),
    )(z_g, z_v, da, cw_g, cw_v, cb_g.reshape(1, D_FF), cb_v.reshape(1, D_FF))
    dz_g, dz_v, dw_g, dw_v, db_g, db_v = outs
    return dz_g, dz_v, dw_g, dw_v, db_g.reshape(D_FF), db_v.reshape(D_FF)


def _loss_head(y, target, *, name, tm=512):
    T, D = y.shape

    def body(y_ref, t_ref, dy_ref, dyb_ref, l_ref):
        err = y_ref[...] - t_ref[...]
        dyv = err * (1.0 / D)
        dy_ref[...] = dyv
        dyb_ref[...] = dyv.astype(BF16)
        part = jnp.sum(jnp.sum(err * err, axis=0, keepdims=True), axis=1, keepdims=True) * (0.5 / D)

        @pl.when(pl.program_id(0) == 0)
        def _():
            l_ref[...] = jnp.broadcast_to(part, l_ref.shape)

        @pl.when(pl.program_id(0) > 0)
        def _():
            l_ref[...] += jnp.broadcast_to(part, l_ref.shape)

    row = pl.BlockSpec((tm, D), lambda i: (i, 0))
    dy, dyb, l = pl.pallas_call(
        body, name=name, grid=(T // tm,),
        in_specs=[row, row],
        out_specs=[row, row, pl.BlockSpec((SUBLANES, LANES), lambda i: (0, 0))],
        out_shape=[jax.ShapeDtypeStruct((T, D), F32), jax.ShapeDtypeStruct((T, D), BF16),
                   jax.ShapeDtypeStruct((SUBLANES, LANES), F32)],
        compiler_params=_params(("arbitrary",)),
    )(y, target)
    return l[0, 0], dy, dyb


def _cast_bf16(x, *, name, tm=512):
    T, D = x.shape

    def body(x_ref, o_ref):
        o_ref[...] = x_ref[...].astype(BF16)

    row = pl.BlockSpec((tm, D), lambda i: (i, 0))
    return pl.pallas_call(body, name=name, grid=(T // tm,), in_specs=[row], out_specs=row,
                          out_shape=jax.ShapeDtypeStruct((T, D), BF16), compiler_params=_params(("parallel",)))(x)


def _layer_fwd(x, w, sched, *, n_seq, seq, l):
    tag = f"l{l}"
    sched("fwd_start", l, x)
    h = _rms_fwd(x, w["mix_norm"], name=f"{tag}_mix_norm")
    proj = _mm(h, w["w_in"], mode="nn", out_dtype=F32, tm=512, tn=768, tk=D_MODEL, name=f"{tag}_proj")
    y_att = _attention_fwd(proj, w["q_norm"], w["k_norm"], w["sinks"], n_seq=n_seq, seq=seq, name=f"{tag}_att")
    deps = sched("fwd_att", l, y_att)
    y_sgu = _sgu_fwd(proj, w["sgu_norm"], w["w_s"], w["bias_full"], n_seq=n_seq, seq=seq, name=f"{tag}_sgu")
    merged = _merge_fwd(y_att, y_sgu, w["w_oa"], w["w_ob"], proj, name=f"{tag}_merge", deps=deps)
    x1 = _mm(merged, w["w_out"], mode="nn", out_dtype=F32, tm=512, tn=1024, tk=D_MODEL, name=f"{tag}_out",
             epilogue=_add, extras=(x,))
    sched("fwd_mixer_done", l, x1)
    h2 = _rms_fwd(x1, w["ffn_norm"], name=f"{tag}_ffn_norm")
    z_g = _mm(h2, w["w_up_g"], mode="nn", out_dtype=F32, tm=512, tn=1408, tk=D_MODEL, name=f"{tag}_up_g")
    z_v = _mm(h2, w["w_up_v"], mode="nn", out_dtype=F32, tm=512, tn=1408, tk=D_MODEL, name=f"{tag}_up_v")
    a = _conv_fwd(z_g, z_v, w["cw_g"], w["cw_v"], w["cb_g"], w["cb_v"], n_seq=n_seq, seq=seq, name=f"{tag}_conv")
    deps = sched("fwd_conv", l, a)
    x2 = _mm(a, w["w_down"], mode="nn", out_dtype=F32, tm=512, tn=1024, tk=D_FF, name=f"{tag}_down",
             epilogue=_add, extras=(x1,), deps=deps)
    saved = dict(x=x, h=h, proj=proj, y_att=y_att, y_sgu=y_sgu, merged=merged, x1=x1, h2=h2, z_g=z_g, z_v=z_v, a=a)
    return x2, saved


def _layer_bwd(dx2, dx2_bf, w, s, sched, *, n_seq, seq, l):
    tag = f"l{l}b"
    g = {}
    da = _mm(dx2_bf, w["w_down"], mode="nt", out_dtype=F32, tm=512, tn=1408, tk=D_MODEL, name=f"{tag}_da")
    g["w_down"] = _mm(s["a"], dx2_bf, mode="tn", out_dtype=F32, tm=1408, tn=1024, tk=512, name=f"{tag}_dw_down")
    dz_g, dz_v, g["cw_g"], g["cw_v"], g["cb_g"], g["cb_v"] = _conv_bwd(
        s["z_g"], s["z_v"], da, w["cw_g"], w["cw_v"], w["cb_g"], w["cb_v"], n_seq=n_seq, seq=seq, name=f"{tag}_conv")
    dh2 = _mm(dz_g, w["w_up_g"], mode="nt", out_dtype=F32, tm=512, tn=1024, tk=1408, name=f"{tag}_dh2_g")
    dh2 = _mm(dz_v, w["w_up_v"], mode="nt", out_dtype=F32, tm=512, tn=1024, tk=1408, name=f"{tag}_dh2_v",
              epilogue=_add, extras=(dh2,))
    g["w_up_g"] = _mm(s["h2"], dz_g, mode="tn", out_dtype=F32, tm=1024, tn=1408, tk=512, name=f"{tag}_dw_up_g")
    g["w_up_v"] = _mm(s["h2"], dz_v, mode="tn", out_dtype=F32, tm=1024, tn=1408, tk=512, name=f"{tag}_dw_up_v")
    deps = sched("bwd_ffn_grads", l, dh2, g)
    dx1, g["ffn_norm"] = _rms_bwd(s["x1"], w["ffn_norm"], dh2, dx2, name=f"{tag}_ffn_norm", deps=deps)
    dx1_bf = _cast_bf16(dx1, name=f"{tag}_dx1_bf")
    dpa, dpb, dga, dgb = _merge_bwd(dx1_bf, w["w_out"], s["y_att"], s["y_sgu"], w["w_oa"], w["w_ob"], s["proj"],
                                    name=f"{tag}_merge")
    deps = sched("bwd_merge", l, dpa)
    g["w_out"] = _mm(s["merged"], dx1_bf, mode="tn", out_dtype=F32, tm=1024, tn=1024, tk=512, name=f"{tag}_dw_out",
                     deps=deps)
    dy_att = _mm(dpa, w["w_oa"], mode="nt", out_dtype=BF16, tm=512, tn=512, tk=D_MODEL, name=f"{tag}_dy_att")
    dy_sgu = _mm(dpb, w["w_ob"], mode="nt", out_dtype=F32, tm=512, tn=512, tk=D_MODEL, name=f"{tag}_dy_sgu")
    g["w_oa"] = _mm(s["y_att"], dpa, mode="tn", out_dtype=F32, tm=512, tn=1024, tk=512, name=f"{tag}_dw_oa")
    g["w_ob"] = _mm(s["y_sgu"], dpb, mode="tn", out_dtype=F32, tm=512, tn=1024, tk=512, name=f"{tag}_dw_ob")
    deps = sched("bwd_out_grads", l, dy_att, g)
    dqkv, g["q_norm"], g["k_norm"], g["sinks"] = _attention_bwd(
        s["proj"], dy_att, w["q_norm"], w["k_norm"], w["sinks"], n_seq=n_seq, seq=seq, name=f"{tag}_att", deps=deps)
    deps = sched("bwd_att", l, dqkv)
    dsuv, g["sgu_norm"], g["w_s"], g["b_s"] = _sgu_bwd(
        s["proj"], dy_sgu, w["sgu_norm"], w["w_s"], w["bias_full"], n_seq=n_seq, seq=seq, name=f"{tag}_sgu", deps=deps)
    dproj = jnp.concatenate([dsuv, dga, dgb, dqkv], axis=1)
    g["w_in"] = _mm(s["h"], dproj, mode="tn", out_dtype=F32, tm=1024, tn=768, tk=512, name=f"{tag}_dw_in")
    deps = sched("bwd_w_in_grad", l, dproj, g)
    dh = _mm(dproj, w["w_in"], mode="nt", out_dtype=F32, tm=512, tn=1024, tk=1280, name=f"{tag}_dh", deps=deps)
    deps = sched("bwd_dh", l, dh)
    dx, g["mix_norm"] = _rms_bwd(s["x"], w["mix_norm"], dh, dx1, name=f"{tag}_mix_norm", deps=deps)
    return dx, g


def _local_step(x, target, weights, sched, *, n_seq, seq):
    depth = len(weights)
    saved = []
    h = x
    for l in range(depth):
        h, s = _layer_fwd(h, weights[l], sched, n_seq=n_seq, seq=seq, l=l)
        saved.append(s)
    loss, dy, dy_bf = _loss_head(h, target, name="loss_head")
    grads = [None] * depth
    for l in reversed(range(depth)):
        if l < depth - 1:
            dy_bf = _cast_bf16(dy, name=f"l{l}b_dx2_bf")
        dy, grads[l] = _layer_bwd(dy, dy_bf, weights[l], saved[l], sched, n_seq=n_seq, seq=seq, l=l)
    return loss, dy, grads


W_IN_SHARD = IN_WIDTH // N_DEV
W_UP_SHARD = 2 * D_FF // N_DEV
COL_MOVE_ROWS = 256


def _w_in_moves():
    moves = []
    for j in range(N_DEV):
        a, b = j * W_IN_SHARD, (j + 1) * W_IN_SHARD
        if a < QKV_WIDTH:
            moves.append((j, 0, min(b, QKV_WIDTH) - a, 0, a + REST_WIDTH))
        if b > QKV_WIDTH:
            lo = max(a, QKV_WIDTH)
            moves.append((j, lo - a, b - a, 0, lo - QKV_WIDTH))
    return tuple(moves)


def _w_up_moves():
    half = N_DEV // 2
    return tuple((j, 0, W_UP_SHARD, j // half, (j % half) * W_UP_SHARD) for j in range(N_DEV))


def _w_o_moves():
    return tuple((j, 0, LANES, 0, j * LANES) for j in range(N_DEV))


def _assemble(blocks, widths, moves, *, name):
    _, R, w = blocks.shape
    tr = min(R, COL_MOVE_ROWS)

    def body(b_ref, *o_refs):
        for j, lo, hi, which, at in moves:
            o_refs[which][:, at:at + hi - lo] = b_ref[j, :, lo:hi]

    return pl.pallas_call(
        body, name=name, grid=(R // tr,),
        in_specs=[pl.BlockSpec((N_DEV, tr, w), lambda i: (0, i, 0))],
        out_specs=[pl.BlockSpec((tr, n), lambda i: (i, 0)) for n in widths],
        out_shape=[jax.ShapeDtypeStruct((R, n), blocks.dtype) for n in widths],
        compiler_params=_params(("parallel",)),
    )(blocks)


def _disassemble(mats, w, moves, *, name):
    R = mats[0].shape[0]
    tr = min(R, COL_MOVE_ROWS)
    n = len(mats)

    def body(*refs):
        m_refs, o_ref = refs[:n], refs[n]
        for j, lo, hi, which, at in moves:
            o_ref[j, :, lo:hi] = m_refs[which][:, at:at + hi - lo]

    return pl.pallas_call(
        body, name=name, grid=(R // tr,),
        in_specs=[pl.BlockSpec((tr, m.shape[1]), lambda i: (i, 0)) for m in mats],
        out_specs=pl.BlockSpec((N_DEV, tr, w), lambda i: (0, i, 0)),
        out_shape=jax.ShapeDtypeStruct((N_DEV, R, w), mats[0].dtype),
        compiler_params=_params(("parallel",)),
    )(*mats)


def _my_place():
    return lax.axis_index("x"), lax.axis_index("y"), lax.axis_index("c")


def _gathered_shape(shape, kind):
    r, c = shape
    return {"blocks": (N_DEV, r, c), "rows": (N_DEV * r, c), "cols": (r, N_DEV * c)}[kind]


def _gather_window(ref, kind, shape, j):
    r, c = shape
    if kind == "blocks":
        return ref.at[j]
    if kind == "rows":
        return ref.at[pl.ds(pl.multiple_of(j * r, r), r), :]
    return ref.at[:, pl.ds(pl.multiple_of(j * c, c), c)]


def _gather(srcs, kinds, *, name):
    n = len(srcs)
    shapes = [s.shape for s in srcs]
    per = 7

    def body(*refs):
        src_refs, dst_refs = refs[:n], refs[n:2 * n]
        send_sems, recv_sems, local_sems = refs[2 * n:]
        x, y, c = _my_place()
        me, sibling = (x, y, c), (x, y, 1 - c)
        chips = [(1 - x, y), (x, 1 - y), (1 - x, 1 - y)]

        def at(i, px, py, pc):
            return _gather_window(dst_refs[i], kinds[i], shapes[i], 4 * px + 2 * py + pc)

        def copy(i, k, block, to, src=None):
            return pltpu.make_async_remote_copy(
                src_ref=at(i, *block) if src is None else src, dst_ref=at(i, *block),
                send_sem=send_sems.at[per * i + k], recv_sem=recv_sems.at[per * i + k], device_id=to, device_id_type=MESH)

        mine = [pltpu.make_async_copy(src_refs[i], at(i, *me), local_sems.at[i]) for i in range(n)]
        for cp in mine:
            cp.start()
        started = []
        for i in range(n):
            first = [copy(i, 0, me, sibling, src=src_refs[i])]
            first += [copy(i, 1 + j, me, (*chip, c), src=src_refs[i]) for j, chip in enumerate(chips)]
            for cp in first:
                cp.start()
            started += first
        for i in range(n):
            for j, chip in enumerate(chips):
                copy(i, 1 + j, (*chip, c), me).wait_recv()
                fwd = copy(i, 4 + j, (*chip, c), sibling)
                fwd.start()
                started.append(fwd)
        for i in range(n):
            copy(i, 0, sibling, me).wait_recv()
            for j, chip in enumerate(chips):
                copy(i, 4 + j, (*chip, 1 - c), me).wait_recv()
        for cp in started:
            cp.wait_send()
        for cp in mine:
            cp.wait()

    return pl.pallas_call(
        body, name=name,
        out_shape=[jax.ShapeDtypeStruct(_gathered_shape(s.shape, k), s.dtype) for s, k in zip(srcs, kinds)],
        in_specs=[ANY] * n, out_specs=[ANY] * n,
        scratch_shapes=[pltpu.SemaphoreType.DMA((per * n,)), pltpu.SemaphoreType.DMA((per * n,)),
                        pltpu.SemaphoreType.DMA((n,))],
    )(*srcs)


HBM = pl.BlockSpec(memory_space=pltpu.HBM)
SEM = pl.BlockSpec(memory_space=pltpu.SEMAPHORE)
TOKEN = jax.ShapeDtypeStruct((SUBLANES, LANES), F32)
TOKEN_SPEC = pl.BlockSpec(memory_space=pltpu.VMEM)
SPLIT_PARAMS = pltpu.CompilerParams(has_side_effects=pltpu.SideEffectType.DATAFLOW_SIDE_EFFECTING)


def _in_hbm(x):
    return pltpu.with_memory_space_constraint(x, pltpu.HBM)


def _hbm_like(shape, dtype):
    return pltpu.HBM(shape, dtype)


def _gather_start(srcs, kinds, group_sizes, *, name):
    n = len(srcs)
    n_groups = len(group_sizes)
    shapes = [s.shape for s in srcs]
    group_of = [(g, i) for g, size in enumerate(group_sizes) for i in range(size)]

    def body(*refs):
        src_refs, land_refs = refs[:n], refs[n:2 * n]
        sem_refs = refs[2 * n:2 * n + 2 * n_groups]
        token, local_sems = refs[-2], refs[-1]
        x, y, c = _my_place()
        me = 4 * x + 2 * y + c
        targets = [(x, y, 1 - c), (1 - x, y, c), (x, 1 - y, c), (1 - x, 1 - y, c)]
        mine = [pltpu.make_async_copy(src_refs[i], _gather_window(land_refs[i], kinds[i], shapes[i], me), local_sems.at[i])
                for i in range(n)]
        for cp in mine:
            cp.start()
        for i in range(n):
            g, li = group_of[i]
            for k, to in enumerate(targets):
                pltpu.make_async_remote_copy(
                    src_ref=src_refs[i], dst_ref=_gather_window(land_refs[i], kinds[i], shapes[i], me),
                    send_sem=sem_refs[2 * g].at[4 * li + k], recv_sem=sem_refs[2 * g + 1].at[4 * li + k],
                    device_id=to, device_id_type=MESH).start()
        for cp in mine:
            cp.wait()
        token[...] = jnp.zeros_like(token)

    lands = [lax.empty(_gathered_shape(s.shape, k), s.dtype) for s, k in zip(srcs, kinds)]
    n_sems = 2 * n_groups
    outs = pl.pallas_call(
        body, name=name,
        out_shape=[pltpu.SemaphoreType.DMA((4 * size,)) for size in group_sizes for _ in range(2)]
        + [_hbm_like(s.shape, s.dtype) for s in srcs] + [_hbm_like(a.shape, a.dtype) for a in lands] + [TOKEN],
        in_specs=[HBM] * (2 * n), out_specs=[SEM] * n_sems + [HBM] * (2 * n) + [TOKEN_SPEC],
        input_output_aliases={i: n_sems + i for i in range(2 * n)},
        scratch_shapes=[pltpu.SemaphoreType.DMA((n,))],
        compiler_params=SPLIT_PARAMS,
    )(*[_in_hbm(s) for s in srcs], *[_in_hbm(a) for a in lands])
    sems = [(outs[2 * g], outs[2 * g + 1]) for g in range(n_groups)]
    return sems, outs[n_sems:n_sems + n], outs[n_sems + n:n_sems + 2 * n], outs[-1]


def _gather_forward(recv_sems, lands, kinds, shapes, after, *, name):
    n = len(lands)

    def body(*refs):
        recv_ref, land_refs = refs[0], refs[1:1 + n]
        fwd_send, fwd_recv = refs[2 + n], refs[3 + n]
        token = refs[-1]
        x, y, c = _my_place()
        chips = [(1 - x, y), (x, 1 - y), (1 - x, 1 - y)]
        for i in range(n):
            for j, (px, py) in enumerate(chips):
                block = _gather_window(land_refs[i], kinds[i], shapes[i], 4 * px + 2 * py + c)
                pltpu.make_async_remote_copy(
                    src_ref=block, dst_ref=block, send_sem=fwd_send.at[3 * i + j], recv_sem=recv_ref.at[4 * i + 1 + j],
                    device_id=(px, py, c), device_id_type=MESH).wait_recv()
                pltpu.make_async_remote_copy(
                    src_ref=block, dst_ref=block, send_sem=fwd_send.at[3 * i + j], recv_sem=fwd_recv.at[3 * i + j],
                    device_id=(x, y, 1 - c), device_id_type=MESH).start()
        token[...] = jnp.zeros_like(token)

    outs = pl.pallas_call(
        body, name=name,
        out_shape=[pltpu.SemaphoreType.DMA((3 * n,)), pltpu.SemaphoreType.DMA((3 * n,))]
        + [_hbm_like(a.shape, a.dtype) for a in lands] + [TOKEN],
        in_specs=[SEM] + [HBM] * n + [ANY], out_specs=[SEM, SEM] + [HBM] * n + [TOKEN_SPEC],
        input_output_aliases={1 + i: 2 + i for i in range(n)},
        compiler_params=SPLIT_PARAMS,
    )(recv_sems, *lands, after)
    return outs[0], outs[1], outs[2:2 + n], outs[-1]


def _gather_finish(send_sems, recv_sems, fwd_send, fwd_recv, srcs, lands, kinds, after, *, name):
    n = len(lands)
    shapes = [s.shape for s in srcs]

    def body(*refs):
        send_ref, recv_ref, fsend_ref, frecv_ref = refs[:4]
        src_refs, land_refs = refs[4:4 + n], refs[4 + n:4 + 2 * n]
        x, y, c = _my_place()
        chips = [(1 - x, y), (x, 1 - y), (1 - x, 1 - y)]
        sibling = (x, y, 1 - c)
        for i in range(n):
            def window(j):
                return _gather_window(land_refs[i], kinds[i], shapes[i], j)

            own = window(4 * x + 2 * y + (1 - c))
            pltpu.make_async_remote_copy(src_ref=src_refs[i], dst_ref=own, send_sem=send_ref.at[4 * i],
                                         recv_sem=recv_ref.at[4 * i], device_id=sibling, device_id_type=MESH).wait_recv()
            for j, (px, py) in enumerate(chips):
                theirs = window(4 * px + 2 * py + (1 - c))
                pltpu.make_async_remote_copy(src_ref=theirs, dst_ref=theirs, send_sem=fsend_ref.at[3 * i + j],
                                             recv_sem=frecv_ref.at[3 * i + j], device_id=sibling,
                                             device_id_type=MESH).wait_recv()
            for k in range(4):
                pltpu.make_async_remote_copy(src_ref=src_refs[i], dst_ref=own, send_sem=send_ref.at[4 * i + k],
                                             recv_sem=recv_ref.at[4 * i + k], device_id=sibling,
                                             device_id_type=MESH).wait_send()
            for j, (px, py) in enumerate(chips):
                block = window(4 * px + 2 * py + c)
                pltpu.make_async_remote_copy(src_ref=block, dst_ref=block, send_sem=fsend_ref.at[3 * i + j],
                                             recv_sem=frecv_ref.at[3 * i + j], device_id=sibling,
                                             device_id_type=MESH).wait_send()

    outs = pl.pallas_call(
        body, name=name,
        out_shape=[_hbm_like(s.shape, s.dtype) for s in srcs] + [_hbm_like(a.shape, a.dtype) for a in lands],
        in_specs=[SEM] * 4 + [HBM] * (2 * n) + [ANY], out_specs=[HBM] * (2 * n),
        input_output_aliases={4 + i: i for i in range(2 * n)},
        compiler_params=SPLIT_PARAMS,
    )(send_sems, recv_sems, fwd_send, fwd_recv, *srcs, *lands, after)
    return outs[n:]


def _pair_plan(src_ref, land_ref, x, y, c):
    return [(src_ref.at[2 * k + (1 - c)], land_ref.at[k], (x, y, 1 - c)) for k in range(N_CHIPS)]


def _chip_plan(src_ref, land_ref, x, y, c):
    chips = [(1 - x, y), (x, 1 - y), (1 - x, 1 - y)]
    return [(src_ref.at[2 * px + py], land_ref.at[k], (px, py, c)) for k, (px, py) in enumerate(chips)]


def _exchange_copies(plan, per, src_refs, land_refs, send_sems, recv_sems):
    x, y, c = _my_place()
    copies = []
    for i, (s_ref, l_ref) in enumerate(zip(src_refs, land_refs)):
        for q, (src, dst, to) in enumerate(plan(s_ref, l_ref, x, y, c)):
            copies.append(pltpu.make_async_remote_copy(
                src_ref=src, dst_ref=dst, send_sem=send_sems.at[per * i + q], recv_sem=recv_sems.at[per * i + q],
                device_id=to, device_id_type=MESH))
    return copies


def _exchange_start(srcs, plan, per, *, name):
    n = len(srcs)

    def body(*refs):
        src_refs, land_refs = refs[:n], refs[n:2 * n]
        send_sems, recv_sems = refs[2 * n], refs[2 * n + 1]
        for cp in _exchange_copies(plan, per, src_refs, land_refs, send_sems, recv_sems):
            cp.start()
        refs[-1][...] = jnp.zeros_like(refs[-1])

    lands = [lax.empty((per,) + s.shape[1:], s.dtype) for s in srcs]
    outs = pl.pallas_call(
        body, name=name,
        out_shape=[pltpu.SemaphoreType.DMA((per * n,)), pltpu.SemaphoreType.DMA((per * n,))]
        + [_hbm_like(s.shape, s.dtype) for s in srcs] + [_hbm_like(a.shape, a.dtype) for a in lands] + [TOKEN],
        in_specs=[HBM] * (2 * n), out_specs=[SEM, SEM] + [HBM] * (2 * n) + [TOKEN_SPEC],
        input_output_aliases={i: 2 + i for i in range(2 * n)},
        compiler_params=SPLIT_PARAMS,
    )(*[_in_hbm(s) for s in srcs], *[_in_hbm(a) for a in lands])
    return outs[0], outs[1], outs[2:2 + n], outs[2 + n:2 + 2 * n], outs[-1]


def _exchange_wait(send_sems, recv_sems, srcs, lands, plan, per, after, *, name):
    n = len(srcs)

    def body(*refs):
        send_ref, recv_ref = refs[0], refs[1]
        src_refs, land_refs = refs[2:2 + n], refs[2 + n:2 + 2 * n]
        copies = _exchange_copies(plan, per, src_refs, land_refs, send_ref, recv_ref)
        for cp in copies:
            cp.wait_recv()
        for cp in copies:
            cp.wait_send()

    outs = pl.pallas_call(
        body, name=name,
        out_shape=[_hbm_like(s.shape, s.dtype) for s in srcs] + [_hbm_like(a.shape, a.dtype) for a in lands],
        in_specs=[SEM, SEM] + [HBM] * (2 * n) + [ANY], out_specs=[HBM] * (2 * n),
        input_output_aliases={2 + i: i for i in range(2 * n)},
        compiler_params=SPLIT_PARAMS,
    )(send_sems, recv_sems, *srcs, *lands, after)
    return outs[:n], outs[n:]


REDUCE_BLOCK_BYTES = 1 << 20


def _row_tile(r, c):
    row_bytes = 4 * (-(-c // LANES) * LANES)
    best = r
    for d in range(SUBLANES, r, SUBLANES):
        if r % d == 0 and d * row_bytes <= REDUCE_BLOCK_BYTES:
            best = d
    return best if r * row_bytes > REDUCE_BLOCK_BYTES else r


def _reduce_pair_sum(blocked, recv, place, wire_dtype, *, name):
    _, r, c = blocked.shape
    tr = _row_tile(r, c)

    def body(place_ref, g_ref, r_ref, own_ref, send_ref):
        s = g_ref[...] + r_ref[...]
        send_ref[...] = s.astype(wire_dtype)

        @pl.when(pl.program_id(1) == place_ref[1])
        def _():
            own_ref[...] = s

    return pl.pallas_call(
        body, name=name,
        grid_spec=pltpu.PrefetchScalarGridSpec(
            num_scalar_prefetch=1, grid=(r // tr, N_CHIPS),
            in_specs=[pl.BlockSpec((None, None, tr, c), lambda i, k, place_ref: (k, place_ref[0], i, 0)),
                      pl.BlockSpec((None, tr, c), lambda i, k, place_ref: (k, i, 0))],
            out_specs=[pl.BlockSpec((tr, c), lambda i, k, place_ref: (i, 0)),
                       pl.BlockSpec((None, tr, c), lambda i, k, place_ref: (k, i, 0))]),
        out_shape=[jax.ShapeDtypeStruct((r, c), F32), jax.ShapeDtypeStruct((N_CHIPS, r, c), wire_dtype)],
        compiler_params=_params(("parallel", "arbitrary")),
    )(place, blocked.reshape(N_CHIPS, 2, r, c), recv)


def _chip_sum(own_ref, r_ref):
    return ((own_ref[...] + r_ref[0].astype(F32)) + r_ref[1].astype(F32)) + r_ref[2].astype(F32)


def _reduce_chip_sum(own, recv, *, name):
    r, c = own.shape
    tr = _row_tile(r, c)

    def body(own_ref, r_ref, o_ref):
        o_ref[...] = _chip_sum(own_ref, r_ref)

    return pl.pallas_call(
        body, name=name, grid=(r // tr,),
        in_specs=[pl.BlockSpec((tr, c), lambda i: (i, 0)), pl.BlockSpec((N_CHIPS - 1, tr, c), lambda i: (0, i, 0))],
        out_specs=pl.BlockSpec((tr, c), lambda i: (i, 0)),
        out_shape=jax.ShapeDtypeStruct((r, c), F32),
        compiler_params=_params(("parallel",)),
    )(own, recv)


def _adamw_math(w, g, m, v):
    nm = ADAM_B1 * m + (1.0 - ADAM_B1) * g
    nv = ADAM_B2 * v + (1.0 - ADAM_B2) * (g * g)
    m_hat = nm / (1.0 - ADAM_B1 ** ADAM_STEP)
    v_hat = nv / (1.0 - ADAM_B2 ** ADAM_STEP)
    return -ADAM_LR * (m_hat / (jnp.sqrt(v_hat) + ADAM_EPS) + ADAM_WD * w), nm, nv


def _adamw(w, g, m, v, *, name):
    shape = w.shape
    C = shape[-1]
    R = math.prod(shape[:-1])
    tr = _row_tile(R, C)

    def body(w_ref, g_ref, m_ref, v_ref, d_ref, nm_ref, nv_ref):
        d_ref[...], nm_ref[...], nv_ref[...] = _adamw_math(w_ref[...], g_ref[...], m_ref[...], v_ref[...])

    spec = pl.BlockSpec((tr, C), lambda i: (i, 0))
    outs = pl.pallas_call(
        body, name=name, grid=(R // tr,),
        in_specs=[spec] * 4, out_specs=[spec] * 3,
        out_shape=[jax.ShapeDtypeStruct((R, C), F32)] * 3,
        compiler_params=_params(("parallel",)),
    )(*[a.reshape(R, C) for a in (w, g, m, v)])
    return tuple(o.reshape(shape) for o in outs)


def _reduce_adamw(own, recv, w, m, v, layer, prev, *, name):
    r, c = own.shape
    tr = _row_tile(r, c)
    n_prev = 0 if prev is None else len(prev)

    def body(own_ref, r_ref, w_ref, m_ref, v_ref, *rest):
        g_ref, d_ref, nm_ref, nv_ref = rest[n_prev:]
        g = _chip_sum(own_ref, r_ref)
        g_ref[...] = g
        d_ref[...], nm_ref[...], nv_ref[...] = _adamw_math(w_ref[...], g, m_ref[...], v_ref[...])

    slot = pl.BlockSpec((None, tr, c), lambda i: (layer, i, 0))
    return pl.pallas_call(
        body, name=name, grid=(r // tr,),
        in_specs=[pl.BlockSpec((tr, c), lambda i: (i, 0)), pl.BlockSpec((N_CHIPS - 1, tr, c), lambda i: (0, i, 0)),
                  slot, slot, slot] + [ANY] * n_prev,
        out_specs=[slot] * 4,
        out_shape=[jax.ShapeDtypeStruct((DEPTH, r, c), F32)] * 4,
        input_output_aliases={5 + k: k for k in range(n_prev)},
        compiler_params=_params(("parallel",)),
    )(own, recv, w, m, v, *(prev or ()))


REPLICATED = (("mix_norm", (D_MODEL,)), ("q_norm", (HEAD_DIM,)), ("k_norm", (HEAD_DIM,)), ("sinks", (N_Q_HEADS,)),
              ("sgu_norm", (SGU_WIDTH,)), ("w_s", (SGU_GROUPS, BLOCK, BLOCK)), ("b_s", (SGU_GROUPS, BLOCK)),
              ("ffn_norm", (D_MODEL,)), ("conv_b", (2 * D_FF,)))
SHARDED = (("w_in", "blocks"), ("w_oa", "cols"), ("w_ob", "cols"), ("w_out", "rows"), ("w_up", "blocks"),
           ("conv_w", "blocks"), ("w_down", "rows"))
WEIGHT_ORDER = ("mix_norm", "w_in", "q_norm", "k_norm", "sinks", "sgu_norm", "w_s", "b_s", "w_oa", "w_ob", "w_out",
                "ffn_norm", "w_up", "conv_w", "conv_b", "w_down")
MIXER_WEIGHTS = ["w_in", "w_oa", "w_ob", "w_out"]
FFN_WEIGHTS = ["w_up", "conv_w", "w_down"]


def _small_layout():
    segs, off = {}, 0
    for l in range(DEPTH):
        for name, shape in REPLICATED:
            n = math.prod(shape)
            segs[(l, name)] = (off, n)
            off += n
    per_dev = -(-off // (N_DEV * SUBLANES * LANES)) * SUBLANES * LANES
    return segs, off, per_dev


def _pack_small(grads):
    ssegs, total, per_dev = _small_layout()
    flat = jnp.concatenate([grads[l][name].reshape(-1) for (l, name) in ssegs])
    return jnp.pad(flat, (0, N_DEV * per_dev - total)).reshape(N_DEV, per_dev // LANES, LANES)


def _unpack_small(gathered):
    ssegs, _, _ = _small_layout()
    flat = gathered.reshape(-1)
    shapes = dict(REPLICATED)
    return {name: jnp.stack([flat[ssegs[(l, name)][0]:ssegs[(l, name)][0] + ssegs[(l, name)][1]].reshape(shapes[name])
                             for l in range(DEPTH)]) for name, _ in REPLICATED}


def kernel(x, mix_norm, w_in, q_norm, k_norm, sinks, sgu_norm, w_s, b_s, w_oa, w_ob, w_out, ffn_norm, w_up, conv_w, conv_b, w_down, loss_target, m_mix_norm, m_w_in, m_q_norm, m_k_norm, m_sinks, m_sgu_norm, m_w_s, m_b_s, m_w_oa, m_w_ob, m_w_out, m_ffn_norm, m_w_up, m_conv_w, m_conv_b, m_w_down, v_mix_norm, v_w_in, v_q_norm, v_k_norm, v_sinks, v_sgu_norm, v_w_s, v_b_s, v_w_oa, v_w_ob, v_w_out, v_ffn_norm, v_w_up, v_conv_w, v_conv_b, v_w_down):
    W = dict(mix_norm=mix_norm, w_in=w_in, q_norm=q_norm, k_norm=k_norm, sinks=sinks, sgu_norm=sgu_norm, w_s=w_s, b_s=b_s,
             w_oa=w_oa, w_ob=w_ob, w_out=w_out, ffn_norm=ffn_norm, w_up=w_up, conv_w=conv_w, conv_b=conv_b, w_down=w_down)
    M = dict(mix_norm=m_mix_norm, w_in=m_w_in, q_norm=m_q_norm, k_norm=m_k_norm, sinks=m_sinks, sgu_norm=m_sgu_norm,
             w_s=m_w_s, b_s=m_b_s, w_oa=m_w_oa, w_ob=m_w_ob, w_out=m_w_out, ffn_norm=m_ffn_norm, w_up=m_w_up,
             conv_w=m_conv_w, conv_b=m_conv_b, w_down=m_w_down)
    V = dict(mix_norm=v_mix_norm, w_in=v_w_in, q_norm=v_q_norm, k_norm=v_k_norm, sinks=v_sinks, sgu_norm=v_sgu_norm,
             w_s=v_w_s, b_s=v_b_s, w_oa=v_w_oa, w_ob=v_w_ob, w_out=v_w_out, ffn_norm=v_ffn_norm, w_up=v_w_up,
             conv_w=v_conv_w, conv_b=v_conv_b, w_down=v_w_down)
    n_seq, seq, d_model = x.shape
    tokens = n_seq * seq
    mx, my, mc = _my_place()
    place = jnp.stack([mc, 2 * mx + my]).astype(jnp.int32)
    half = N_DEV // 2
    kind_of = dict(SHARDED)

    gather_groups = [[(0, n) for n in MIXER_WEIGHTS], [(0, n) for n in FFN_WEIGHTS],
                     [(1, n) for n in MIXER_WEIGHTS + FFN_WEIGHTS]]
    items = [it for group in gather_groups for it in group]
    srcs = [W[name][l] if name == "conv_w" else W[name][l].astype(BF16) for l, name in items]
    kinds = [kind_of[name] for _, name in items]
    sems, srcs_thru, lands, start_token = _gather_start(srcs, kinds, [len(g) for g in gather_groups],
                                                        name="gather_weights_start")
    bounds = [sum(len(g) for g in gather_groups[:i]) for i in range(len(gather_groups) + 1)]
    in_flight = {}
    weights = []
    for l in range(DEPTH):
        w = {name: W[name][l] for name, _ in REPLICATED}
        w["cb_g"], w["cb_v"] = W["conv_b"][l][:D_FF], W["conv_b"][l][D_FF:]
        w["bias_full"] = jnp.repeat(W["b_s"][l].T, SGU_WIDTH // SGU_GROUPS, axis=1)
        weights.append(w)

    def gather_forward(gi, after):
        lo, hi = bounds[gi], bounds[gi + 1]
        in_flight[gi] = _gather_forward(sems[gi][1], lands[lo:hi], kinds[lo:hi], [s.shape for s in srcs[lo:hi]], after,
                                        name=f"gather_weights_forward_{gi}")

    def gather_finish(gi, after):
        lo, hi = bounds[gi], bounds[gi + 1]
        fwd_send, fwd_recv, lands_g, _ = in_flight.pop(gi)
        whole = _gather_finish(sems[gi][0], sems[gi][1], fwd_send, fwd_recv, srcs_thru[lo:hi], lands_g, kinds[lo:hi], after,
                               name=f"gather_weights_finish_{gi}")
        for (l, name), arr in zip(gather_groups[gi], whole):
            w = weights[l]
            if name == "w_in":
                (w["w_in"],) = _assemble(arr, (IN_WIDTH,), _w_in_moves(), name=f"l{l}_assemble_w_in")
            elif name == "w_up":
                w["w_up_g"], w["w_up_v"] = _assemble(arr, (D_FF, D_FF), _w_up_moves(), name=f"l{l}_assemble_w_up")
            elif name == "conv_w":
                w["cw_g"] = arr[:half].transpose(1, 0, 2).reshape(3, D_FF)
                w["cw_v"] = arr[half:].transpose(1, 0, 2).reshape(3, D_FF)
            else:
                w[name] = arr

    reduce_state, results = {}, {}
    wire = {"conv_w": F32, "small": F32}

    def reduce_begin(key, names, arrays):
        send, recv, srcs_, lands_, token = _exchange_start(arrays, _pair_plan, N_CHIPS, name=f"reduce_pair_start_{key}")
        reduce_state[key] = dict(names=names, pair=(send, recv, srcs_, lands_))
        return [token]

    def reduce_pair(key, after):
        st = reduce_state[key]
        send, recv, srcs_, lands_ = st.pop("pair")
        blocked_, from_sibling = _exchange_wait(send, recv, srcs_, lands_, _pair_plan, N_CHIPS, after,
                                                name=f"reduce_pair_wait_{key}")
        sums = [_reduce_pair_sum(b, r, place, wire.get(n if isinstance(n, str) else n[1], BF16),
                                 name=f"reduce_pair_sum_{key}_{i}")
                for i, (n, b, r) in enumerate(zip(st["names"], blocked_, from_sibling))]
        st["own"] = [s[0] for s in sums]
        *st["chip"], token = _exchange_start([s[1] for s in sums], _chip_plan, N_CHIPS - 1, name=f"reduce_chip_start_{key}")
        return [token]

    def reduce_end(key, after):
        st = reduce_state.pop(key)
        send, recv, srcs_, lands_ = st["chip"]
        _, from_chips = _exchange_wait(send, recv, srcs_, lands_, _chip_plan, N_CHIPS - 1, after,
                                       name=f"reduce_chip_wait_{key}")
        done = []
        for n, own, got in zip(st["names"], st["own"], from_chips):
            if n == "small":
                results["small"] = _reduce_chip_sum(own, got, name="reduce_chip_sum_small")
            else:
                l, name = n
                results[name] = _reduce_adamw(own, got, W[name], M[name], V[name], l, results.get(name),
                                              name=f"l{l}_reduce_adamw_{name}")
                done.append(results[name][0])
        return done

    def sched(point, l, carry, g=None):
        deps = []
        if point == "fwd_start" and l == 0:
            gather_forward(0, start_token)
            gather_finish(0, in_flight[0][3])
        elif point == "fwd_att" and l == 0:
            gather_forward(1, carry)
            deps = [in_flight[1][3]]
        elif point == "fwd_mixer_done" and l == 0:
            gather_finish(1, carry)
        elif point == "fwd_conv" and l == 0:
            gather_forward(2, carry)
            deps = [in_flight[2][3]]
        elif point == "fwd_start" and l == 1:
            gather_finish(2, carry)
        elif point == "bwd_ffn_grads":
            if l + 1 < DEPTH:
                deps += reduce_end(f"l{l + 1}_in", g["w_up_v"])
            conv_w = jnp.concatenate([g[k].reshape(3, half, W_UP_SHARD).transpose(1, 0, 2) for k in ("cw_g", "cw_v")])
            deps += reduce_begin(
                f"l{l}_ffn", [(l, "w_down"), (l, "w_up"), (l, "conv_w")],
                [g["w_down"].reshape(N_DEV, D_FF // N_DEV, D_MODEL),
                 _disassemble((g["w_up_g"], g["w_up_v"]), W_UP_SHARD, _w_up_moves(), name=f"l{l}_split_dw_up"), conv_w])
        elif point == "bwd_merge":
            deps = reduce_pair(f"l{l}_ffn", carry)
        elif point == "bwd_out_grads":
            deps = reduce_begin(
                f"l{l}_out", [(l, "w_out"), (l, "w_oa"), (l, "w_ob")],
                [g["w_out"].reshape(N_DEV, D_MODEL // N_DEV, D_MODEL),
                 _disassemble((g["w_oa"],), LANES, _w_o_moves(), name=f"l{l}_split_dw_oa"),
                 _disassemble((g["w_ob"],), LANES, _w_o_moves(), name=f"l{l}_split_dw_ob")])
        elif point == "bwd_att":
            deps = reduce_pair(f"l{l}_out", carry) + reduce_end(f"l{l}_ffn", carry)
        elif point == "bwd_w_in_grad":
            deps = reduce_begin(f"l{l}_in", [(l, "w_in")],
                                [_disassemble((g["w_in"],), W_IN_SHARD, _w_in_moves(), name=f"l{l}_split_dw_in")])
        elif point == "bwd_dh":
            deps = reduce_pair(f"l{l}_in", carry) + reduce_end(f"l{l}_out", carry)
        return deps

    loss_part, dx, grads = _local_step(x.reshape(tokens, d_model), loss_target.reshape(tokens, d_model), weights, sched,
                                       n_seq=n_seq, seq=seq)
    loss = lax.psum(loss_part, ("x", "y", "c"))

    for g in grads:
        g["conv_b"] = jnp.concatenate([g["cb_g"], g["cb_v"]])
    reduce_begin("small", ["small"], [_pack_small(grads)])
    reduce_end("l0_in", dx)
    reduce_pair("small", results["w_in"][0])
    reduce_end("small", results["w_in"][1])

    G, delta, new_m, new_v = {}, {}, {}, {}
    for name, _ in SHARDED:
        G[name], delta[name], new_m[name], new_v[name] = results[name]
    G.update(_unpack_small(_gather([results["small"]], ["blocks"], name="gather_small_grads")[0]))
    for name, _ in REPLICATED:
        delta[name], new_m[name], new_v[name] = _adamw(W[name], G[name], M[name], V[name], name=f"adamw_{name}")
    return (loss, dx.reshape(n_seq, seq, d_model), *[G[n] for n in WEIGHT_ORDER], *[delta[n] for n in WEIGHT_ORDER],
            *[new_m[n] for n in WEIGHT_ORDER], *[new_v[n] for n in WEIGHT_ORDER])
```

```python
import math

import jax
import jax.numpy as jnp
from jax import lax
from jax.experimental import pallas as pl
from jax.experimental.pallas import tpu as pltpu

F32 = jnp.float32
BF16 = jnp.bfloat16
MESH = pl.DeviceIdType.MESH

DEPTH = 2
D_MODEL = 1024
N_Q_HEADS = 8
HEAD_DIM = 64
ATT_WIDTH = 512
KV_WIDTH = 128
BLOCK = 128
SGU_WIDTH = 512
SGU_GROUPS = 8
IN_WIDTH = 3840
D_FF = 2816
NORM_EPS = 1e-6
NEG_INF = -1e30
ATT_SCALE = HEAD_DIM ** -0.5
ALIBI_SLOPES = tuple(2.0 ** (-(h + 1)) for h in range(N_Q_HEADS))
ADAM_LR, ADAM_B1, ADAM_B2, ADAM_EPS, ADAM_WD, ADAM_STEP = 0.001, 0.9, 0.999, 1e-08, 0.01, 10
N_DEV = 8
N_CHIPS = 4

QKV_WIDTH = ATT_WIDTH + 2 * KV_WIDTH
REST_WIDTH = IN_WIDTH - QKV_WIDTH
COL_SUV, COL_GA, COL_GB, COL_QKV = 0, 1024, 2048, 3072

LANES = 128
SUBLANES = 8
VMEM_LIMIT_V7X = 56 * 1024 * 1024
GELU_C = math.sqrt(2.0 / math.pi)
GELU_K = 0.044715
ANY = pl.BlockSpec(memory_space=pl.ANY)


def _params(sem=None):
    return pltpu.CompilerParams(dimension_semantics=sem, vmem_limit_bytes=VMEM_LIMIT_V7X)


def _sigmoid(x):
    return 1.0 / (1.0 + jnp.exp(-x))


def _gelu(x):
    th = jnp.tanh(GELU_C * (x + GELU_K * x * x * x))
    return 0.5 * x * (1.0 + th)


def _gelu_and_grad(x):
    x2 = x * x
    th = jnp.tanh(GELU_C * (x + GELU_K * x2 * x))
    g = 0.5 * x * (1.0 + th)
    dg = 0.5 * (1.0 + th) + 0.5 * x * (1.0 - th * th) * (GELU_C * (1.0 + 3.0 * GELU_K * x2))
    return g, dg


def _dot(a, b, dims):
    return lax.dot_general(a, b, (dims, ((), ())), preferred_element_type=F32)


def _dot_nn(a, b):
    return _dot(a, b, ((1,), (0,)))


def _dot_nt(a, b):
    return _dot(a, b, ((1,), (1,)))


def _dot_tn(a, b):
    return _dot(a, b, ((0,), (0,)))


def _lo_mask(shape):
    return lax.broadcasted_iota(jnp.int32, shape, len(shape) - 1) < (LANES // 2)


def _half_sums(x, lo):
    s_lo = jnp.sum(jnp.where(lo, x, 0.0), axis=-1, keepdims=True)
    s_all = jnp.sum(x, axis=-1, keepdims=True)
    return jnp.where(lo, s_lo, s_all - s_lo)


def _dup_half(x, half, lo):
    r = pltpu.roll(x, LANES // 2, axis=1)
    return jnp.where(lo, x, r) if half == 0 else jnp.where(lo, r, x)


def _with_deps(body, n_in, deps):
    k = len(deps)
    if not k:
        return body, [], ()

    def skipping(*refs):
        return body(*refs[:n_in], *refs[n_in + k:])

    return skipping, [ANY] * k, tuple(deps)


def _mm(a, b, *, mode, out_dtype, tm, tn, tk, name, epilogue=None, extras=(), deps=()):
    if mode == "nn":
        (M, K), N = a.shape, b.shape[1]
    elif mode == "nt":
        (M, K), N = a.shape, b.shape[0]
    else:
        (K, M), N = a.shape, b.shape[1]
    assert M % tm == 0 and N % tn == 0 and K % tk == 0, (name, M, N, K, tm, tn, tk)
    gm, gn, gk = M // tm, N // tn, K // tk
    if mode == "nn":
        a_spec = pl.BlockSpec((tm, tk), lambda i, j, k: (i, k))
        b_spec = pl.BlockSpec((tk, tn), lambda i, j, k: (k, j))
        contract = ((1,), (0,))
    elif mode == "nt":
        a_spec = pl.BlockSpec((tm, tk), lambda i, j, k: (i, k))
        b_spec = pl.BlockSpec((tn, tk), lambda i, j, k: (j, k))
        contract = ((1,), (1,))
    else:
        a_spec = pl.BlockSpec((tk, tm), lambda i, j, k: (k, i))
        b_spec = pl.BlockSpec((tk, tn), lambda i, j, k: (k, j))
        contract = ((0,), (0,))
    o_spec = pl.BlockSpec((tm, tn), lambda i, j, k: (i, j))
    n_extra = len(extras)

    def finish(acc, extra_refs, o_ref):
        if epilogue is not None:
            acc = epilogue(acc, *[r[...] for r in extra_refs])
        o_ref[...] = acc.astype(out_dtype)

    def body(a_ref, b_ref, *rest):
        extra_refs, o_ref = rest[:n_extra], rest[n_extra]
        part = _dot(a_ref[...].astype(BF16), b_ref[...].astype(BF16), contract)
        if gk == 1:
            finish(part, extra_refs, o_ref)
            return
        acc_ref = rest[n_extra + 1]
        k = pl.program_id(2)

        @pl.when(k == 0)
        def _():
            acc_ref[...] = part

        @pl.when(k > 0)
        def _():
            acc_ref[...] += part

        @pl.when(k == gk - 1)
        def _():
            finish(acc_ref[...], extra_refs, o_ref)

    body, dep_specs, dep_args = _with_deps(body, 2 + n_extra, deps)
    return pl.pallas_call(
        body,
        name=name,
        grid=(gm, gn, gk),
        in_specs=[a_spec, b_spec] + [o_spec] * n_extra + dep_specs,
        out_specs=o_spec,
        out_shape=jax.ShapeDtypeStruct((M, N), out_dtype),
        scratch_shapes=[] if gk == 1 else [pltpu.VMEM((tm, tn), F32)],
        compiler_params=_params(("parallel", "parallel", "arbitrary")),
    )(a, b, *extras, *dep_args)


def _add(acc, r):
    return acc + r


def _rms_fwd(x, gain, *, name, tm=512):
    T, D = x.shape

    def body(x_ref, g_ref, h_ref):
        xv = x_ref[...]
        r = lax.rsqrt(jnp.mean(xv * xv, axis=-1, keepdims=True) + NORM_EPS)
        h_ref[...] = (xv * r * g_ref[...]).astype(BF16)

    return pl.pallas_call(
        body, name=name, grid=(T // tm,),
        in_specs=[pl.BlockSpec((tm, D), lambda i: (i, 0)), pl.BlockSpec((1, D), lambda i: (0, 0))],
        out_specs=pl.BlockSpec((tm, D), lambda i: (i, 0)),
        out_shape=jax.ShapeDtypeStruct((T, D), BF16),
        compiler_params=_params(("parallel",)),
    )(x, gain.reshape(1, D))


def _rms_bwd(x, gain, dh, dres, *, name, tm=512, deps=()):
    T, D = x.shape

    def body(x_ref, g_ref, dh_ref, dres_ref, dx_ref, dg_ref):
        xv = x_ref[...]
        r = lax.rsqrt(jnp.mean(xv * xv, axis=-1, keepdims=True) + NORM_EPS)
        xh = xv * r
        dhv = dh_ref[...]
        dxh = dhv * g_ref[...]
        dx = r * (dxh - xh * jnp.mean(dxh * xh, axis=-1, keepdims=True))
        dx_ref[...] = dres_ref[...] + dx
        part = jnp.sum(dhv * xh, axis=0, keepdims=True)

        @pl.when(pl.program_id(0) == 0)
        def _():
            dg_ref[...] = part

        @pl.when(pl.program_id(0) > 0)
        def _():
            dg_ref[...] += part

    row = pl.BlockSpec((tm, D), lambda i: (i, 0))
    vec = pl.BlockSpec((1, D), lambda i: (0, 0))
    body, dep_specs, dep_args = _with_deps(body, 4, deps)
    dx, dg = pl.pallas_call(
        body, name=name, grid=(T // tm,),
        in_specs=[row, vec, row, row] + dep_specs,
        out_specs=[row, vec],
        out_shape=[jax.ShapeDtypeStruct((T, D), F32), jax.ShapeDtypeStruct((1, D), F32)],
        compiler_params=_params(("arbitrary",)),
    )(x, gain.reshape(1, D), dh, dres, *dep_args)
    return dx, dg.reshape(D)


def _head_norm(x, gain2, lo):
    ms = _half_sums(x * x, lo) * (1.0 / HEAD_DIM)
    r = lax.rsqrt(ms + NORM_EPS)
    xh = x * r
    return xh * gain2, xh, r


def _head_norm_bwd(xh, r, gain2, dy, lo):
    dxh = dy * gain2
    dx = r * (dxh - xh * (_half_sums(dxh * xh, lo) * (1.0 / HEAD_DIM)))
    return dx, dy * xh


def _att_masks():
    qi = lax.broadcasted_iota(jnp.int32, (BLOCK, BLOCK), 0)
    kj = lax.broadcasted_iota(jnp.int32, (BLOCK, BLOCK), 1)
    d_cur = qi - kj
    d_prev = qi - kj + BLOCK
    return d_cur >= 0, d_prev < BLOCK, d_cur.astype(F32), d_prev.astype(F32)


def _att_probs(qm, k2c, k2p, sink, slope, masks, has_prev):
    ok_c, ok_p, d_c, d_p = masks
    s_c = jnp.where(ok_c, _dot_nt(qm, k2c) * ATT_SCALE - slope * d_c, NEG_INF)
    s_p = jnp.where(jnp.logical_and(ok_p, has_prev), _dot_nt(qm, k2p) * ATT_SCALE - slope * d_p, NEG_INF)
    m = jnp.maximum(jnp.maximum(jnp.max(s_c, axis=-1, keepdims=True), jnp.max(s_p, axis=-1, keepdims=True)), sink)
    e_c = jnp.exp(s_c - m)
    e_p = jnp.exp(s_p - m)
    e_s = jnp.exp(sink - m)
    inv = 1.0 / (jnp.sum(e_c, axis=-1, keepdims=True) + jnp.sum(e_p, axis=-1, keepdims=True) + e_s)
    return e_c * inv, e_p * inv, e_s * inv


def _attention_fwd(proj, q_gain, k_gain, sinks, *, n_seq, seq, name):
    T = n_seq * seq
    nb = seq // BLOCK
    qcol, kvcol = COL_QKV // ATT_WIDTH, (COL_QKV + ATT_WIDTH) // (2 * KV_WIDTH)

    def body(q_ref, kv_ref, qg_ref, kg_ref, sink_ref, y_ref):
        lo = _lo_mask((BLOCK, LANES))
        masks = _att_masks()
        qg, kg = qg_ref[...], kg_ref[...]

        def block(i, carry):
            r0 = pl.multiple_of(i * BLOCK, BLOCK)
            rp = pl.multiple_of(jnp.maximum(i - 1, 0) * BLOCK, BLOCK)
            has_prev = i > 0
            kn_c = _head_norm(kv_ref[pl.ds(r0, BLOCK), 0:KV_WIDTH], kg, lo)[0].astype(BF16)
            kn_p = _head_norm(kv_ref[pl.ds(rp, BLOCK), 0:KV_WIDTH], kg, lo)[0].astype(BF16)
            v_c = kv_ref[pl.ds(r0, BLOCK), KV_WIDTH:2 * KV_WIDTH].astype(BF16)
            v_p = kv_ref[pl.ds(rp, BLOCK), KV_WIDTH:2 * KV_WIDTH].astype(BF16)
            for pair in range(N_Q_HEADS // 2):
                kv = pair // 2
                k2c, k2p = _dup_half(kn_c, kv, lo), _dup_half(kn_p, kv, lo)
                v2c, v2p = _dup_half(v_c, kv, lo), _dup_half(v_p, kv, lo)
                qn = _head_norm(q_ref[pl.ds(r0, BLOCK), pair * LANES:(pair + 1) * LANES], qg, lo)[0]
                out = None
                for half in range(2):
                    h = 2 * pair + half
                    mine = lo if half == 0 else jnp.logical_not(lo)
                    qm = jnp.where(mine, qn, 0.0).astype(BF16)
                    p_c, p_p, _ = _att_probs(qm, k2c, k2p, sink_ref[h], ALIBI_SLOPES[h], masks, has_prev)
                    o = _dot_nn(p_c.astype(BF16), v2c) + _dot_nn(p_p.astype(BF16), v2p)
                    out = o if out is None else jnp.where(lo, out, o)
                y_ref[pl.ds(r0, BLOCK), pair * LANES:(pair + 1) * LANES] = out.astype(BF16)
            return carry

        lax.fori_loop(0, nb, block, 0)

    vec = pl.BlockSpec((1, LANES), lambda b: (0, 0))
    return pl.pallas_call(
        body, name=name, grid=(n_seq,),
        in_specs=[pl.BlockSpec((seq, ATT_WIDTH), lambda b: (b, qcol)),
                  pl.BlockSpec((seq, 2 * KV_WIDTH), lambda b: (b, kvcol)),
                  vec, vec, pl.BlockSpec(memory_space=pltpu.SMEM)],
        out_specs=pl.BlockSpec((seq, ATT_WIDTH), lambda b: (b, 0)),
        out_shape=jax.ShapeDtypeStruct((T, ATT_WIDTH), BF16),
        compiler_params=_params(("parallel",)),
    )(proj, proj, jnp.tile(q_gain, 2).reshape(1, LANES), jnp.tile(k_gain, 2).reshape(1, LANES), sinks)


def _attention_bwd(proj, dy, q_gain, k_gain, sinks, *, n_seq, seq, name, deps=()):
    T = n_seq * seq
    nb = seq // BLOCK
    qcol, kvcol = COL_QKV // ATT_WIDTH, (COL_QKV + ATT_WIDTH) // (2 * KV_WIDTH)

    def body(q_ref, kv_ref, dy_ref, qg_ref, kg_ref, sink_ref, dqkv_ref, dqg_ref, dkg_ref, dsink_ref,
             dkn_acc, dv_acc, qg_acc, kg_acc, sink_acc):
        lo = _lo_mask((BLOCK, LANES))
        hi = jnp.logical_not(lo)
        lane = lax.broadcasted_iota(jnp.int32, (BLOCK, LANES), 1)
        masks = _att_masks()
        qg, kg = qg_ref[...], kg_ref[...]
        first = pl.program_id(0) == 0

        @pl.when(first)
        def _():
            qg_acc[...] = jnp.zeros_like(qg_acc)
            kg_acc[...] = jnp.zeros_like(kg_acc)
            sink_acc[...] = jnp.zeros_like(sink_acc)

        dkn_acc[...] = jnp.zeros_like(dkn_acc)
        dv_acc[...] = jnp.zeros_like(dv_acc)

        def block(i, carry):
            r0 = pl.multiple_of(i * BLOCK, BLOCK)
            rp = pl.multiple_of(jnp.maximum(i - 1, 0) * BLOCK, BLOCK)
            has_prev = i > 0
            kn_c = _head_norm(kv_ref[pl.ds(r0, BLOCK), 0:KV_WIDTH], kg, lo)[0].astype(BF16)
            kn_p = _head_norm(kv_ref[pl.ds(rp, BLOCK), 0:KV_WIDTH], kg, lo)[0].astype(BF16)
            v_c = kv_ref[pl.ds(r0, BLOCK), KV_WIDTH:2 * KV_WIDTH].astype(BF16)
            v_p = kv_ref[pl.ds(rp, BLOCK), KV_WIDTH:2 * KV_WIDTH].astype(BF16)
            dk_c = [jnp.zeros((BLOCK, LANES), F32) for _ in range(2)]
            dk_p = [jnp.zeros((BLOCK, LANES), F32) for _ in range(2)]
            dv_c = [jnp.zeros((BLOCK, LANES), F32) for _ in range(2)]
            dv_p = [jnp.zeros((BLOCK, LANES), F32) for _ in range(2)]
            for pair in range(N_Q_HEADS // 2):
                kv = pair // 2
                cols = slice(pair * LANES, (pair + 1) * LANES)
                k2c, k2p = _dup_half(kn_c, kv, lo), _dup_half(kn_p, kv, lo)
                v2c, v2p = _dup_half(v_c, kv, lo), _dup_half(v_p, kv, lo)
                qn, qh, qr = _head_norm(q_ref[pl.ds(r0, BLOCK), cols], qg, lo)
                do_pair = dy_ref[pl.ds(r0, BLOCK), cols]
                dqn = None
                for half in range(2):
                    h = 2 * pair + half
                    mine = lo if half == 0 else hi
                    qm = jnp.where(mine, qn, 0.0).astype(BF16)
                    dom = jnp.where(mine, do_pair, jnp.zeros_like(do_pair))
                    p_c, p_p, p_s = _att_probs(qm, k2c, k2p, sink_ref[h], ALIBI_SLOPES[h], masks, has_prev)
                    dp_c = _dot_nt(dom, v2c)
                    dp_p = _dot_nt(dom, v2p)
                    delta = jnp.sum(p_c * dp_c, axis=-1, keepdims=True) + jnp.sum(p_p * dp_p, axis=-1, keepdims=True)
                    ds_c = (p_c * (dp_c - delta)).astype(BF16)
                    ds_p = (p_p * (dp_p - delta)).astype(BF16)
                    sink_acc[...] += jnp.where(lane == h, -(p_s * delta), 0.0)
                    dq_h = (_dot_nn(ds_c, k2c) + _dot_nn(ds_p, k2p)) * ATT_SCALE
                    dqn = dq_h if dqn is None else jnp.where(lo, dqn, dq_h)
                    dk_c[kv] = dk_c[kv] + _dot_tn(ds_c, qm)
                    dk_p[kv] = dk_p[kv] + _dot_tn(ds_p, qm)
                    dv_c[kv] = dv_c[kv] + _dot_tn(p_c.astype(BF16), dom)
                    dv_p[kv] = dv_p[kv] + _dot_tn(p_p.astype(BF16), dom)
                dq, dg = _head_norm_bwd(qh, qr, qg, dqn, lo)
                dqkv_ref[pl.ds(r0, BLOCK), cols] = dq.astype(BF16)
                qg_acc[...] += dg

            def fold(parts):
                a = parts[0] + pltpu.roll(parts[0], LANES // 2, axis=1)
                b = parts[1] + pltpu.roll(parts[1], LANES // 2, axis=1)
                return jnp.where(lo, a, b)

            dkn_acc[pl.ds(r0, BLOCK), :] += fold(dk_c) * ATT_SCALE
            dkn_acc[pl.ds(rp, BLOCK), :] += fold(dk_p) * ATT_SCALE
            dv_acc[pl.ds(r0, BLOCK), :] += fold(dv_c)
            dv_acc[pl.ds(rp, BLOCK), :] += fold(dv_p)
            return carry

        lax.fori_loop(0, nb, block, 0)

        def finish(i, carry):
            r0 = pl.multiple_of(i * BLOCK, BLOCK)
            _, kh, kr = _head_norm(kv_ref[pl.ds(r0, BLOCK), 0:KV_WIDTH], kg, lo)
            dk, dg = _head_norm_bwd(kh, kr, kg, dkn_acc[pl.ds(r0, BLOCK), :], lo)
            dqkv_ref[pl.ds(r0, BLOCK), ATT_WIDTH:ATT_WIDTH + KV_WIDTH] = dk.astype(BF16)
            dqkv_ref[pl.ds(r0, BLOCK), ATT_WIDTH + KV_WIDTH:QKV_WIDTH] = dv_acc[pl.ds(r0, BLOCK), :].astype(BF16)
            kg_acc[...] += dg
            return carry

        lax.fori_loop(0, nb, finish, 0)

        @pl.when(pl.program_id(0) == n_seq - 1)
        def _():
            dqg_ref[...] = jnp.sum(qg_acc[...], axis=0, keepdims=True)
            dkg_ref[...] = jnp.sum(kg_acc[...], axis=0, keepdims=True)
            dsink_ref[...] = jnp.sum(sink_acc[...], axis=0, keepdims=True)

    vec = pl.BlockSpec((1, LANES), lambda b: (0, 0))
    acc = pltpu.VMEM((BLOCK, LANES), F32)
    body, dep_specs, dep_args = _with_deps(body, 6, deps)
    dqkv, dqg, dkg, dsink = pl.pallas_call(
        body, name=name, grid=(n_seq,),
        in_specs=[pl.BlockSpec((seq, ATT_WIDTH), lambda b: (b, qcol)),
                  pl.BlockSpec((seq, 2 * KV_WIDTH), lambda b: (b, kvcol)),
                  pl.BlockSpec((seq, ATT_WIDTH), lambda b: (b, 0)),
                  vec, vec, pl.BlockSpec(memory_space=pltpu.SMEM)] + dep_specs,
        out_specs=[pl.BlockSpec((seq, QKV_WIDTH), lambda b: (b, 0)), vec, vec, vec],
        out_shape=[jax.ShapeDtypeStruct((T, QKV_WIDTH), BF16)] + [jax.ShapeDtypeStruct((1, LANES), F32)] * 3,
        scratch_shapes=[pltpu.VMEM((seq, KV_WIDTH), F32), pltpu.VMEM((seq, KV_WIDTH), F32), acc, acc, acc],
        compiler_params=_params(("arbitrary",)),
    )(proj, proj, dy, jnp.tile(q_gain, 2).reshape(1, LANES), jnp.tile(k_gain, 2).reshape(1, LANES), sinks, *dep_args)
    half = LANES // 2
    return dqkv, dqg[0, :half] + dqg[0, half:], dkg[0, :half] + dkg[0, half:], dsink[0, :N_Q_HEADS]


def _sgu_weights(w_ref):
    r = lax.broadcasted_iota(jnp.int32, (BLOCK, BLOCK), 0)
    c = lax.broadcasted_iota(jnp.int32, (BLOCK, BLOCK), 1)
    return [jnp.where(r >= c, w_ref[g], 0.0).astype(BF16) for g in range(SGU_GROUPS)]


def _sgu_fwd(proj, gain, w_s, bias_full, *, n_seq, seq, name):
    T = n_seq * seq
    nc = seq // BLOCK

    def body(suv_ref, g_ref, w_ref, b_ref, y_ref):
        lo = _lo_mask((BLOCK, LANES))
        wm = _sgu_weights(w_ref)
        gain_v = g_ref[...]

        def chunk(c, carry):
            r0 = pl.multiple_of(c * BLOCK, BLOCK)
            gv = _gelu(suv_ref[pl.ds(r0, BLOCK), SGU_WIDTH:2 * SGU_WIDTH])
            r = lax.rsqrt(jnp.mean(gv * gv, axis=-1, keepdims=True) + NORM_EPS)
            vn = (gv * r * gain_v).astype(BF16)
            for p in range(SGU_WIDTH // LANES):
                cols = slice(p * LANES, (p + 1) * LANES)
                vp = vn[:, cols]
                mixed = jnp.where(lo, _dot_nn(wm[2 * p], vp), _dot_nn(wm[2 * p + 1], vp)) + b_ref[:, cols]
                u = _gelu(suv_ref[pl.ds(r0, BLOCK), cols])
                y_ref[pl.ds(r0, BLOCK), cols] = (u * mixed).astype(BF16)
            return carry

        lax.fori_loop(0, nc, chunk, 0)

    return pl.pallas_call(
        body, name=name, grid=(n_seq,),
        in_specs=[pl.BlockSpec((seq, 2 * SGU_WIDTH), lambda b: (b, COL_SUV // (2 * SGU_WIDTH))),
                  pl.BlockSpec((1, SGU_WIDTH), lambda b: (0, 0)),
                  pl.BlockSpec((SGU_GROUPS, BLOCK, BLOCK), lambda b: (0, 0, 0)),
                  pl.BlockSpec((BLOCK, SGU_WIDTH), lambda b: (0, 0))],
        out_specs=pl.BlockSpec((seq, SGU_WIDTH), lambda b: (b, 0)),
        out_shape=jax.ShapeDtypeStruct((T, SGU_WIDTH), BF16),
        compiler_params=_params(("parallel",)),
    )(proj, gain.reshape(1, SGU_WIDTH), w_s, bias_full)


def _sgu_bwd(proj, dy, gain, w_s, bias_full, *, n_seq, seq, name, deps=()):
    T = n_seq * seq
    nc = seq // BLOCK
    n_tiles = SGU_WIDTH // LANES

    def body(suv_ref, dy_ref, g_ref, w_ref, b_ref, dsuv_ref, dg_ref, dw_ref, db_ref, dg_acc, dw_acc, db_acc):
        lo = _lo_mask((BLOCK, LANES))
        hi = jnp.logical_not(lo)
        wm = _sgu_weights(w_ref)
        wmt = [jnp.where(lax.broadcasted_iota(jnp.int32, (BLOCK, BLOCK), 1) >= lax.broadcasted_iota(jnp.int32, (BLOCK, BLOCK), 0),
                         w_ref[g].T, 0.0).astype(BF16) for g in range(SGU_GROUPS)]
        gain_v = g_ref[...]

        @pl.when(pl.program_id(0) == 0)
        def _():
            dg_acc[...] = jnp.zeros_like(dg_acc)
            dw_acc[...] = jnp.zeros_like(dw_acc)
            db_acc[...] = jnp.zeros_like(db_acc)

        def chunk(c, carry):
            r0 = pl.multiple_of(c * BLOCK, BLOCK)
            gv, dgelu_v = _gelu_and_grad(suv_ref[pl.ds(r0, BLOCK), SGU_WIDTH:2 * SGU_WIDTH])
            r = lax.rsqrt(jnp.mean(gv * gv, axis=-1, keepdims=True) + NORM_EPS)
            vh = gv * r
            vn = (vh * gain_v).astype(BF16)
            dvn_tiles = []
            for p in range(n_tiles):
                cols = slice(p * LANES, (p + 1) * LANES)
                vp = vn[:, cols]
                mixed = jnp.where(lo, _dot_nn(wm[2 * p], vp), _dot_nn(wm[2 * p + 1], vp)) + b_ref[:, cols]
                u, dgelu_u = _gelu_and_grad(suv_ref[pl.ds(r0, BLOCK), cols])
                dyv = dy_ref[pl.ds(r0, BLOCK), cols]
                dsuv_ref[pl.ds(r0, BLOCK), cols] = (dyv * mixed * dgelu_u).astype(BF16)
                dm = dyv * u
                db_acc[:, cols] += dm
                dm_bf = dm.astype(BF16)
                dvn_tiles.append(jnp.where(lo, _dot_nn(wmt[2 * p], dm_bf), _dot_nn(wmt[2 * p + 1], dm_bf)))
                dw_acc[2 * p] += _dot_nt(jnp.where(lo, dm, 0.0).astype(BF16), vp)
                dw_acc[2 * p + 1] += _dot_nt(jnp.where(hi, dm, 0.0).astype(BF16), vp)
            dvn = jnp.concatenate(dvn_tiles, axis=1)
            dg_acc[...] += dvn * vh
            dvh = dvn * gain_v
            dgv = r * (dvh - vh * jnp.mean(dvh * vh, axis=-1, keepdims=True))
            dsuv_ref[pl.ds(r0, BLOCK), SGU_WIDTH:2 * SGU_WIDTH] = (dgv * dgelu_v).astype(BF16)
            return carry

        lax.fori_loop(0, nc, chunk, 0)

        @pl.when(pl.program_id(0) == n_seq - 1)
        def _():
            dg_ref[...] = jnp.sum(dg_acc[...], axis=0, keepdims=True)
            r = lax.broadcasted_iota(jnp.int32, (BLOCK, BLOCK), 0)
            c = lax.broadcasted_iota(jnp.int32, (BLOCK, BLOCK), 1)
            for g in range(SGU_GROUPS):
                dw_ref[g] = jnp.where(r >= c, dw_acc[g], 0.0)
            lane = lax.broadcasted_iota(jnp.int32, (BLOCK, LANES), 1)
            out = jnp.zeros((BLOCK, LANES), F32)
            for p in range(n_tiles):
                tile = db_acc[:, p * LANES:(p + 1) * LANES]
                s_lo = jnp.sum(jnp.where(lo, tile, 0.0), axis=-1, keepdims=True)
                s_hi = jnp.sum(jnp.where(hi, tile, 0.0), axis=-1, keepdims=True)
                out = jnp.where(lane == 2 * p, s_lo, out)
                out = jnp.where(lane == 2 * p + 1, s_hi, out)
            db_ref[...] = out

    body, dep_specs, dep_args = _with_deps(body, 5, deps)
    dsuv, dg, dw, db = pl.pallas_call(
        body, name=name, grid=(n_seq,),
        in_specs=[pl.BlockSpec((seq, 2 * SGU_WIDTH), lambda b: (b, COL_SUV // (2 * SGU_WIDTH))),
                  pl.BlockSpec((seq, SGU_WIDTH), lambda b: (b, 0)),
                  pl.BlockSpec((1, SGU_WIDTH), lambda b: (0, 0)),
                  pl.BlockSpec((SGU_GROUPS, BLOCK, BLOCK), lambda b: (0, 0, 0)),
                  pl.BlockSpec((BLOCK, SGU_WIDTH), lambda b: (0, 0))] + dep_specs,
        out_specs=[pl.BlockSpec((seq, 2 * SGU_WIDTH), lambda b: (b, 0)),
                   pl.BlockSpec((1, SGU_WIDTH), lambda b: (0, 0)),
                   pl.BlockSpec((SGU_GROUPS, BLOCK, BLOCK), lambda b: (0, 0, 0)),
                   pl.BlockSpec((BLOCK, LANES), lambda b: (0, 0))],
        out_shape=[jax.ShapeDtypeStruct((T, 2 * SGU_WIDTH), BF16), jax.ShapeDtypeStruct((1, SGU_WIDTH), F32),
                   jax.ShapeDtypeStruct((SGU_GROUPS, BLOCK, BLOCK), F32), jax.ShapeDtypeStruct((BLOCK, LANES), F32)],
        scratch_shapes=[pltpu.VMEM((BLOCK, SGU_WIDTH), F32), pltpu.VMEM((SGU_GROUPS, BLOCK, BLOCK), F32),
                        pltpu.VMEM((BLOCK, SGU_WIDTH), F32)],
        compiler_params=_params(("arbitrary",)),
    )(proj, dy, gain.reshape(1, SGU_WIDTH), w_s, bias_full, *dep_args)
    return dsuv, dg.reshape(SGU_WIDTH), dw, db[:, :SGU_GROUPS].T


def _merge_fwd(y_att, y_sgu, w_oa, w_ob, proj, *, name, tm=512, tn=512, deps=()):
    T = y_att.shape[0]

    def body(ya_ref, ys_ref, wa_ref, wb_ref, ga_ref, gb_ref, o_ref):
        pa = _dot_nn(ya_ref[...], wa_ref[...])
        pb = _dot_nn(ys_ref[...], wb_ref[...])
        o_ref[...] = (_sigmoid(ga_ref[...]) * pa + _sigmoid(gb_ref[...]) * pb).astype(BF16)

    act = pl.BlockSpec((tm, ATT_WIDTH), lambda i, j: (i, 0))
    wgt = pl.BlockSpec((ATT_WIDTH, tn), lambda i, j: (0, j))
    body, dep_specs, dep_args = _with_deps(body, 6, deps)
    return pl.pallas_call(
        body, name=name, grid=(T // tm, D_MODEL // tn),
        in_specs=[act, act, wgt, wgt,
                  pl.BlockSpec((tm, tn), lambda i, j: (i, j + COL_GA // tn)),
                  pl.BlockSpec((tm, tn), lambda i, j: (i, j + COL_GB // tn))] + dep_specs,
        out_specs=pl.BlockSpec((tm, tn), lambda i, j: (i, j)),
        out_shape=jax.ShapeDtypeStruct((T, D_MODEL), BF16),
        compiler_params=_params(("parallel", "parallel")),
    )(y_att, y_sgu, w_oa, w_ob, proj, proj, *dep_args)


def _merge_bwd(dx1_bf, w_out, y_att, y_sgu, w_oa, w_ob, proj, *, name, tm=512, tn=512):
    T = y_att.shape[0]

    def body(dx_ref, wo_ref, ya_ref, ys_ref, wa_ref, wb_ref, ga_ref, gb_ref, dpa_ref, dpb_ref, dga_ref, dgb_ref):
        dm = _dot_nt(dx_ref[...], wo_ref[...])
        pa = _dot_nn(ya_ref[...], wa_ref[...])
        pb = _dot_nn(ys_ref[...], wb_ref[...])
        sa = _sigmoid(ga_ref[...])
        sb = _sigmoid(gb_ref[...])
        dpa_ref[...] = (dm * sa).astype(BF16)
        dpb_ref[...] = (dm * sb).astype(BF16)
        dga_ref[...] = (dm * pa * sa * (1.0 - sa)).astype(BF16)
        dgb_ref[...] = (dm * pb * sb * (1.0 - sb)).astype(BF16)

    act = pl.BlockSpec((tm, ATT_WIDTH), lambda i, j: (i, 0))
    wgt = pl.BlockSpec((ATT_WIDTH, tn), lambda i, j: (0, j))
    out = pl.BlockSpec((tm, tn), lambda i, j: (i, j))
    return pl.pallas_call(
        body, name=name, grid=(T // tm, D_MODEL // tn),
        in_specs=[pl.BlockSpec((tm, D_MODEL), lambda i, j: (i, 0)),
                  pl.BlockSpec((tn, D_MODEL), lambda i, j: (j, 0)),
                  act, act, wgt, wgt,
                  pl.BlockSpec((tm, tn), lambda i, j: (i, j + COL_GA // tn)),
                  pl.BlockSpec((tm, tn), lambda i, j: (i, j + COL_GB // tn))],
        out_specs=[out] * 4,
        out_shape=[jax.ShapeDtypeStruct((T, D_MODEL), BF16)] * 4,
        compiler_params=_params(("parallel", "parallel")),
    )(dx1_bf, w_out, y_att, y_sgu, w_oa, w_ob, proj, proj)


CONV_ROWS = 256
CONV_TN = 256


def _shift_rows(cur, prev8, k):
    rolled = pltpu.roll(cur, k, axis=0)
    head = jnp.where(lax.broadcasted_iota(jnp.int32, prev8.shape, 0) < k, pltpu.roll(prev8, k, axis=0), rolled[:SUBLANES])
    return jnp.concatenate([head, rolled[SUBLANES:]], axis=0)


def _shift_rows_up(cur, next8, k):
    n = cur.shape[0]
    rolled = pltpu.roll(cur, n - k, axis=0)
    tail = jnp.where(lax.broadcasted_iota(jnp.int32, next8.shape, 0) >= SUBLANES - k,
                     pltpu.roll(next8, SUBLANES - k, axis=0), rolled[n - SUBLANES:])
    return jnp.concatenate([rolled[:n - SUBLANES], tail], axis=0)


def _conv_rows(z_ref, r0, first, w_ref, b_ref, rows):
    cur = z_ref[pl.ds(r0, rows), :]
    rp = pl.multiple_of(jnp.maximum(r0 - SUBLANES, 0), SUBLANES)
    prev8 = jnp.where(first, 0.0, z_ref[pl.ds(rp, SUBLANES), :])
    z1 = _shift_rows(cur, prev8, 1)
    z2 = _shift_rows(cur, prev8, 2)
    return b_ref[...] + w_ref[0:1, :] * z2 + w_ref[1:2, :] * z1 + w_ref[2:3, :] * cur


def _conv_fwd(z_g, z_v, cw_g, cw_v, cb_g, cb_v, *, n_seq, seq, name):
    T = n_seq * seq
    tn, rows = CONV_TN, CONV_ROWS

    def body(zg_ref, zv_ref, wg_ref, wv_ref, bg_ref, bv_ref, a_ref):
        def step(s, carry):
            r0 = pl.multiple_of(s * rows, rows)
            first = s == 0
            g = _conv_rows(zg_ref, r0, first, wg_ref, bg_ref, rows)
            v = _conv_rows(zv_ref, r0, first, wv_ref, bv_ref, rows)
            a_ref[pl.ds(r0, rows), :] = (g * _sigmoid(g) * v).astype(BF16)
            return carry

        lax.fori_loop(0, seq // rows, step, 0)

    zs = pl.BlockSpec((seq, tn), lambda b, j: (b, j))
    ws = pl.BlockSpec((3, tn), lambda b, j: (0, j))
    bs = pl.BlockSpec((1, tn), lambda b, j: (0, j))
    return pl.pallas_call(
        body, name=name, grid=(n_seq, D_FF // tn),
        in_specs=[zs, zs, ws, ws, bs, bs], out_specs=zs,
        out_shape=jax.ShapeDtypeStruct((T, D_FF), BF16),
        compiler_params=_params(("parallel", "parallel")),
    )(z_g, z_v, cw_g, cw_v, cb_g.reshape(1, D_FF), cb_v.reshape(1, D_FF))


def _conv_bwd(z_g, z_v, da, cw_g, cw_v, cb_g, cb_v, *, n_seq, seq, name):
    T = n_seq * seq
    tn, rows = CONV_TN, CONV_ROWS
    n_steps = seq // rows

    def body(zg_ref, zv_ref, da_ref, wg_ref, wv_ref, bg_ref, bv_ref,
             dzg_ref, dzv_ref, dwg_ref, dwv_ref, dbg_ref, dbv_ref, dcg_ref, dcv_ref):
        def grads(s, accs):
            r0 = pl.multiple_of(s * rows, rows)
            first = s == 0
            cur_g = zg_ref[pl.ds(r0, rows), :]
            cur_v = zv_ref[pl.ds(r0, rows), :]
            rp = pl.multiple_of(jnp.maximum(r0 - SUBLANES, 0), SUBLANES)
            pg = jnp.where(first, 0.0, zg_ref[pl.ds(rp, SUBLANES), :])
            pv = jnp.where(first, 0.0, zv_ref[pl.ds(rp, SUBLANES), :])
            g1, g2 = _shift_rows(cur_g, pg, 1), _shift_rows(cur_g, pg, 2)
            v1, v2 = _shift_rows(cur_v, pv, 1), _shift_rows(cur_v, pv, 2)
            g = bg_ref[...] + wg_ref[0:1, :] * g2 + wg_ref[1:2, :] * g1 + wg_ref[2:3, :] * cur_g
            v = bv_ref[...] + wv_ref[0:1, :] * v2 + wv_ref[1:2, :] * v1 + wv_ref[2:3, :] * cur_v
            sg = _sigmoid(g)
            dav = da_ref[pl.ds(r0, rows), :]
            dcg = dav * v * (sg * (1.0 + g * (1.0 - sg)))
            dcv = dav * (g * sg)
            dcg_ref[pl.ds(r0, rows), :] = dcg
            dcv_ref[pl.ds(r0, rows), :] = dcv

            def colsum(x):
                return jnp.sum(x, axis=0, keepdims=True)

            return (accs[0] + colsum(dcg * g2), accs[1] + colsum(dcg * g1), accs[2] + colsum(dcg * cur_g), accs[3] + colsum(dcg),
                    accs[4] + colsum(dcv * v2), accs[5] + colsum(dcv * v1), accs[6] + colsum(dcv * cur_v), accs[7] + colsum(dcv))

        zero = jnp.zeros((1, tn), F32)
        sums = lax.fori_loop(0, n_steps, grads, (zero,) * 8)
        first_seq = pl.program_id(1) == 0

        @pl.when(first_seq)
        def _():
            dwg_ref[...] = jnp.concatenate(sums[0:3], axis=0)
            dbg_ref[...] = sums[3]
            dwv_ref[...] = jnp.concatenate(sums[4:7], axis=0)
            dbv_ref[...] = sums[7]

        @pl.when(jnp.logical_not(first_seq))
        def _():
            dwg_ref[...] += jnp.concatenate(sums[0:3], axis=0)
            dbg_ref[...] += sums[3]
            dwv_ref[...] += jnp.concatenate(sums[4:7], axis=0)
            dbv_ref[...] += sums[7]

        def back(s, carry):
            r0 = pl.multiple_of(s * rows, rows)
            last = s == n_steps - 1
            rn = pl.multiple_of(jnp.minimum(r0 + rows, seq - SUBLANES), SUBLANES)
            for dc_ref, w_ref, dz_ref in ((dcg_ref, wg_ref, dzg_ref), (dcv_ref, wv_ref, dzv_ref)):
                cur = dc_ref[pl.ds(r0, rows), :]
                nxt = jnp.where(last, 0.0, dc_ref[pl.ds(rn, SUBLANES), :])
                u1, u2 = _shift_rows_up(cur, nxt, 1), _shift_rows_up(cur, nxt, 2)
                dz_ref[pl.ds(r0, rows), :] = (w_ref[2:3, :] * cur + w_ref[1:2, :] * u1 + w_ref[0:1, :] * u2).astype(BF16)
            return carry

        lax.fori_loop(0, n_steps, back, 0)

    zs = pl.BlockSpec((seq, tn), lambda j, b: (b, j))
    ws = pl.BlockSpec((3, tn), lambda j, b: (0, j))
    bs = pl.BlockSpec((1, tn), lambda j, b: (0, j))
    outs = pl.pallas_call(
        body, name=name, grid=(D_FF // tn, n_seq),
        in_specs=[zs, zs, zs, ws, ws, bs, bs],
        out_specs=[zs, zs, ws, ws, bs, bs],
        out_shape=[jax.ShapeDtypeStruct((T, D_FF), BF16)] * 2 + [jax.ShapeDtypeStruct((3, D_FF), F32)] * 2
        + [jax.ShapeDtypeStruct((1, D_FF), F32)] * 2,
        scratch_shapes=[pltpu.VMEM((seq, tn), F32), pltpu.VMEM((seq, tn), F32)],
        compiler_params=_params(("parallel", "arbitrary")),
    )(z_g, z_v, da, cw_g, cw_v, cb_g.reshape(1, D_FF), cb_v.reshape(1, D_FF))
    dz_g, dz_v, dw_g, dw_v, db_g, db_v = outs
    return dz_g, dz_v, dw_g, dw_v, db_g.reshape(D_FF), db_v.reshape(D_FF)


def _loss_head(y, target, *, name, tm=512):
    T, D = y.shape

    def body(y_ref, t_ref, dy_ref, dyb_ref, l_ref):
        err = y_ref[...] - t_ref[...]
        dyv = err * (1.0 / D)
        dy_ref[...] = dyv
        dyb_ref[...] = dyv.astype(BF16)
        part = jnp.sum(jnp.sum(err * err, axis=0, keepdims=True), axis=1, keepdims=True) * (0.5 / D)

        @pl.when(pl.program_id(0) == 0)
        def _():
            l_ref[...] = jnp.broadcast_to(part, l_ref.shape)

        @pl.when(pl.program_id(0) > 0)
        def _():
            l_ref[...] += jnp.broadcast_to(part, l_ref.shape)

    row = pl.BlockSpec((tm, D), lambda i: (i, 0))
    dy, dyb, l = pl.pallas_call(
        body, name=name, grid=(T // tm,),
        in_specs=[row, row],
        out_specs=[row, row, pl.BlockSpec((SUBLANES, LANES), lambda i: (0, 0))],
        out_shape=[jax.ShapeDtypeStruct((T, D), F32), jax.ShapeDtypeStruct((T, D), BF16),
                   jax.ShapeDtypeStruct((SUBLANES, LANES), F32)],
        compiler_params=_params(("arbitrary",)),
    )(y, target)
    return l[0, 0], dy, dyb


def _cast_bf16(x, *, name, tm=512):
    T, D = x.shape

    def body(x_ref, o_ref):
        o_ref[...] = x_ref[...].astype(BF16)

    row = pl.BlockSpec((tm, D), lambda i: (i, 0))
    return pl.pallas_call(body, name=name, grid=(T // tm,), in_specs=[row], out_specs=row,
                          out_shape=jax.ShapeDtypeStruct((T, D), BF16), compiler_params=_params(("parallel",)))(x)


def _layer_fwd(x, w, sched, *, n_seq, seq, l):
    tag = f"l{l}"
    deps = sched("fwd_start", l, x)
    h = _rms_fwd(x, w["mix_norm"], name=f"{tag}_mix_norm")
    proj = _mm(h, w["w_in"], mode="nn", out_dtype=F32, tm=512, tn=768, tk=D_MODEL, name=f"{tag}_proj", deps=deps)
    y_att = _attention_fwd(proj, w["q_norm"], w["k_norm"], w["sinks"], n_seq=n_seq, seq=seq, name=f"{tag}_att")
    deps = sched("fwd_att", l, y_att)
    y_sgu = _sgu_fwd(proj, w["sgu_norm"], w["w_s"], w["bias_full"], n_seq=n_seq, seq=seq, name=f"{tag}_sgu")
    merged = _merge_fwd(y_att, y_sgu, w["w_oa"], w["w_ob"], proj, name=f"{tag}_merge", deps=deps)
    x1 = _mm(merged, w["w_out"], mode="nn", out_dtype=F32, tm=512, tn=1024, tk=D_MODEL, name=f"{tag}_out",
             epilogue=_add, extras=(x,))
    deps = sched("fwd_mixer_done", l, x1)
    h2 = _rms_fwd(x1, w["ffn_norm"], name=f"{tag}_ffn_norm")
    z_g = _mm(h2, w["w_up_g"], mode="nn", out_dtype=F32, tm=512, tn=1408, tk=D_MODEL, name=f"{tag}_up_g", deps=deps)
    z_v = _mm(h2, w["w_up_v"], mode="nn", out_dtype=F32, tm=512, tn=1408, tk=D_MODEL, name=f"{tag}_up_v")
    a = _conv_fwd(z_g, z_v, w["cw_g"], w["cw_v"], w["cb_g"], w["cb_v"], n_seq=n_seq, seq=seq, name=f"{tag}_conv")
    deps = sched("fwd_conv", l, a)
    x2 = _mm(a, w["w_down"], mode="nn", out_dtype=F32, tm=512, tn=1024, tk=D_FF, name=f"{tag}_down",
             epilogue=_add, extras=(x1,), deps=deps)
    saved = dict(x=x, h=h, proj=proj, y_att=y_att, y_sgu=y_sgu, merged=merged, x1=x1, h2=h2, z_g=z_g, z_v=z_v, a=a)
    return x2, saved


def _layer_bwd(dx2, dx2_bf, w, s, sched, *, n_seq, seq, l):
    tag = f"l{l}b"
    g = {}
    da = _mm(dx2_bf, w["w_down"], mode="nt", out_dtype=F32, tm=512, tn=1408, tk=D_MODEL, name=f"{tag}_da")
    g["w_down"] = _mm(s["a"], dx2_bf, mode="tn", out_dtype=F32, tm=1408, tn=1024, tk=512, name=f"{tag}_dw_down")
    dz_g, dz_v, g["cw_g"], g["cw_v"], g["cb_g"], g["cb_v"] = _conv_bwd(
        s["z_g"], s["z_v"], da, w["cw_g"], w["cw_v"], w["cb_g"], w["cb_v"], n_seq=n_seq, seq=seq, name=f"{tag}_conv")
    dh2 = _mm(dz_g, w["w_up_g"], mode="nt", out_dtype=F32, tm=512, tn=1024, tk=1408, name=f"{tag}_dh2_g")
    dh2 = _mm(dz_v, w["w_up_v"], mode="nt", out_dtype=F32, tm=512, tn=1024, tk=1408, name=f"{tag}_dh2_v",
              epilogue=_add, extras=(dh2,))
    g["w_up_g"] = _mm(s["h2"], dz_g, mode="tn", out_dtype=F32, tm=1024, tn=1408, tk=512, name=f"{tag}_dw_up_g")
    g["w_up_v"] = _mm(s["h2"], dz_v, mode="tn", out_dtype=F32, tm=1024, tn=1408, tk=512, name=f"{tag}_dw_up_v")
    deps = sched("bwd_ffn_grads", l, dh2, g)
    dx1, g["ffn_norm"] = _rms_bwd(s["x1"], w["ffn_norm"], dh2, dx2, name=f"{tag}_ffn_norm", deps=deps)
    dx1_bf = _cast_bf16(dx1, name=f"{tag}_dx1_bf")
    dpa, dpb, dga, dgb = _merge_bwd(dx1_bf, w["w_out"], s["y_att"], s["y_sgu"], w["w_oa"], w["w_ob"], s["proj"],
                                    name=f"{tag}_merge")
    deps = sched("bwd_merge", l, dpa)
    g["w_out"] = _mm(s["merged"], dx1_bf, mode="tn", out_dtype=F32, tm=1024, tn=1024, tk=512, name=f"{tag}_dw_out",
                     deps=deps)
    dy_att = _mm(dpa, w["w_oa"], mode="nt", out_dtype=BF16, tm=512, tn=512, tk=D_MODEL, name=f"{tag}_dy_att")
    dy_sgu = _mm(dpb, w["w_ob"], mode="nt", out_dtype=F32, tm=512, tn=512, tk=D_MODEL, name=f"{tag}_dy_sgu")
    g["w_oa"] = _mm(s["y_att"], dpa, mode="tn", out_dtype=F32, tm=512, tn=1024, tk=512, name=f"{tag}_dw_oa")
    g["w_ob"] = _mm(s["y_sgu"], dpb, mode="tn", out_dtype=F32, tm=512, tn=1024, tk=512, name=f"{tag}_dw_ob")
    deps = sched("bwd_out_grads", l, dy_att, g)
    dqkv, g["q_norm"], g["k_norm"], g["sinks"] = _attention_bwd(
        s["proj"], dy_att, w["q_norm"], w["k_norm"], w["sinks"], n_seq=n_seq, seq=seq, name=f"{tag}_att", deps=deps)
    deps = sched("bwd_att", l, dqkv)
    dsuv, g["sgu_norm"], g["w_s"], g["b_s"] = _sgu_bwd(
        s["proj"], dy_sgu, w["sgu_norm"], w["w_s"], w["bias_full"], n_seq=n_seq, seq=seq, name=f"{tag}_sgu", deps=deps)
    dproj = jnp.concatenate([dsuv, dga, dgb, dqkv], axis=1)
    g["w_in"] = _mm(s["h"], dproj, mode="tn", out_dtype=F32, tm=1024, tn=768, tk=512, name=f"{tag}_dw_in")
    deps = sched("bwd_w_in_grad", l, dproj, g)
    dh = _mm(dproj, w["w_in"], mode="nt", out_dtype=F32, tm=512, tn=1024, tk=1280, name=f"{tag}_dh", deps=deps)
    deps = sched("bwd_dh", l, dh)
    dx, g["mix_norm"] = _rms_bwd(s["x"], w["mix_norm"], dh, dx1, name=f"{tag}_mix_norm", deps=deps)
    return dx, g


def _local_step(x, target, weights, sched, *, n_seq, seq):
    depth = len(weights)
    saved = []
    h = x
    for l in range(depth):
        h, s = _layer_fwd(h, weights[l], sched, n_seq=n_seq, seq=seq, l=l)
        saved.append(s)
    loss, dy, dy_bf = _loss_head(h, target, name="loss_head")
    grads = [None] * depth
    for l in reversed(range(depth)):
        if l < depth - 1:
            dy_bf = _cast_bf16(dy, name=f"l{l}b_dx2_bf")
        dy, grads[l] = _layer_bwd(dy, dy_bf, weights[l], saved[l], sched, n_seq=n_seq, seq=seq, l=l)
    return loss, dy, grads


W_IN_SHARD = IN_WIDTH // N_DEV
W_UP_SHARD = 2 * D_FF // N_DEV
COL_MOVE_ROWS = 256


def _w_in_moves():
    moves = []
    for j in range(N_DEV):
        a, b = j * W_IN_SHARD, (j + 1) * W_IN_SHARD
        if a < QKV_WIDTH:
            moves.append((j, 0, min(b, QKV_WIDTH) - a, 0, a + REST_WIDTH))
        if b > QKV_WIDTH:
            lo = max(a, QKV_WIDTH)
            moves.append((j, lo - a, b - a, 0, lo - QKV_WIDTH))
    return tuple(moves)


def _w_up_moves():
    half = N_DEV // 2
    return tuple((j, 0, W_UP_SHARD, j // half, (j % half) * W_UP_SHARD) for j in range(N_DEV))


def _w_o_moves():
    return tuple((j, 0, LANES, 0, j * LANES) for j in range(N_DEV))


def _assemble(blocks, widths, moves, *, name):
    _, R, w = blocks.shape
    tr = min(R, COL_MOVE_ROWS)

    def body(b_ref, *o_refs):
        for j, lo, hi, which, at in moves:
            o_refs[which][:, at:at + hi - lo] = b_ref[j, :, lo:hi]

    return pl.pallas_call(
        body, name=name, grid=(R // tr,),
        in_specs=[pl.BlockSpec((N_DEV, tr, w), lambda i: (0, i, 0))],
        out_specs=[pl.BlockSpec((tr, n), lambda i: (i, 0)) for n in widths],
        out_shape=[jax.ShapeDtypeStruct((R, n), blocks.dtype) for n in widths],
        compiler_params=_params(("parallel",)),
    )(blocks)


def _disassemble(mats, w, moves, *, name):
    R = mats[0].shape[0]
    tr = min(R, COL_MOVE_ROWS)
    n = len(mats)

    def body(*refs):
        m_refs, o_ref = refs[:n], refs[n]
        for j, lo, hi, which, at in moves:
            o_ref[j, :, lo:hi] = m_refs[which][:, at:at + hi - lo]

    return pl.pallas_call(
        body, name=name, grid=(R // tr,),
        in_specs=[pl.BlockSpec((tr, m.shape[1]), lambda i: (i, 0)) for m in mats],
        out_specs=pl.BlockSpec((N_DEV, tr, w), lambda i: (0, i, 0)),
        out_shape=jax.ShapeDtypeStruct((N_DEV, R, w), mats[0].dtype),
        compiler_params=_params(("parallel",)),
    )(*mats)


def _my_place():
    return lax.axis_index("x"), lax.axis_index("y"), lax.axis_index("c")


def _gathered_shape(shape, kind):
    r, c = shape
    return {"blocks": (N_DEV, r, c), "rows": (N_DEV * r, c), "cols": (r, N_DEV * c)}[kind]


def _gather_window(ref, kind, shape, j):
    r, c = shape
    if kind == "blocks":
        return ref.at[j]
    if kind == "rows":
        return ref.at[pl.ds(pl.multiple_of(j * r, r), r), :]
    return ref.at[:, pl.ds(pl.multiple_of(j * c, c), c)]


def _gather(srcs, kinds, *, name):
    n = len(srcs)
    shapes = [s.shape for s in srcs]
    per = 7

    def body(*refs):
        src_refs, dst_refs = refs[:n], refs[n:2 * n]
        send_sems, recv_sems, local_sems = refs[2 * n:]
        x, y, c = _my_place()
        me, sibling = (x, y, c), (x, y, 1 - c)
        chips = [(1 - x, y), (x, 1 - y), (1 - x, 1 - y)]

        def at(i, px, py, pc):
            return _gather_window(dst_refs[i], kinds[i], shapes[i], 4 * px + 2 * py + pc)

        def copy(i, k, block, to, src=None):
            return pltpu.make_async_remote_copy(
                src_ref=at(i, *block) if src is None else src, dst_ref=at(i, *block),
                send_sem=send_sems.at[per * i + k], recv_sem=recv_sems.at[per * i + k], device_id=to, device_id_type=MESH)

        mine = [pltpu.make_async_copy(src_refs[i], at(i, *me), local_sems.at[i]) for i in range(n)]
        for cp in mine:
            cp.start()
        started = []
        for i in range(n):
            first = [copy(i, 0, me, sibling, src=src_refs[i])]
            first += [copy(i, 1 + j, me, (*chip, c), src=src_refs[i]) for j, chip in enumerate(chips)]
            for cp in first:
                cp.start()
            started += first
        for i in range(n):
            for j, chip in enumerate(chips):
                copy(i, 1 + j, (*chip, c), me).wait_recv()
                fwd = copy(i, 4 + j, (*chip, c), sibling)
                fwd.start()
                started.append(fwd)
        for i in range(n):
            copy(i, 0, sibling, me).wait_recv()
            for j, chip in enumerate(chips):
                copy(i, 4 + j, (*chip, 1 - c), me).wait_recv()
        for cp in started:
            cp.wait_send()
        for cp in mine:
            cp.wait()

    return pl.pallas_call(
        body, name=name,
        out_shape=[jax.ShapeDtypeStruct(_gathered_shape(s.shape, k), s.dtype) for s, k in zip(srcs, kinds)],
        in_specs=[ANY] * n, out_specs=[ANY] * n,
        scratch_shapes=[pltpu.SemaphoreType.DMA((per * n,)), pltpu.SemaphoreType.DMA((per * n,)),
                        pltpu.SemaphoreType.DMA((n,))],
    )(*srcs)


HBM = pl.BlockSpec(memory_space=pltpu.HBM)
SEM = pl.BlockSpec(memory_space=pltpu.SEMAPHORE)
TOKEN = jax.ShapeDtypeStruct((SUBLANES, LANES), F32)
TOKEN_SPEC = pl.BlockSpec(memory_space=pltpu.VMEM)
SPLIT_PARAMS = pltpu.CompilerParams(has_side_effects=pltpu.SideEffectType.DATAFLOW_SIDE_EFFECTING)


def _in_hbm(x):
    return pltpu.with_memory_space_constraint(x, pltpu.HBM)


def _hbm_like(shape, dtype):
    return pltpu.HBM(shape, dtype)


def _gather_start(srcs, kinds, group_sizes, after=(), *, name):
    n = len(srcs)
    n_groups = len(group_sizes)
    n_after = len(after)
    shapes = [s.shape for s in srcs]
    group_of = [(g, i) for g, size in enumerate(group_sizes) for i in range(size)]

    def body(*refs):
        src_refs, land_refs = refs[:n], refs[n:2 * n]
        sem_refs = refs[2 * n + n_after:2 * n + n_after + 2 * n_groups]
        token, local_sems = refs[-2], refs[-1]
        x, y, c = _my_place()
        me = 4 * x + 2 * y + c
        targets = [(x, y, 1 - c), (1 - x, y, c), (x, 1 - y, c), (1 - x, 1 - y, c)]
        mine = [pltpu.make_async_copy(src_refs[i], _gather_window(land_refs[i], kinds[i], shapes[i], me), local_sems.at[i])
                for i in range(n)]
        for cp in mine:
            cp.start()
        for i in range(n):
            g, li = group_of[i]
            for k, to in enumerate(targets):
                pltpu.make_async_remote_copy(
                    src_ref=src_refs[i], dst_ref=_gather_window(land_refs[i], kinds[i], shapes[i], me),
                    send_sem=sem_refs[2 * g].at[4 * li + k], recv_sem=sem_refs[2 * g + 1].at[4 * li + k],
                    device_id=to, device_id_type=MESH).start()
        for cp in mine:
            cp.wait()
        token[...] = jnp.zeros_like(token)

    lands = [lax.empty(_gathered_shape(s.shape, k), s.dtype) for s, k in zip(srcs, kinds)]
    n_sems = 2 * n_groups
    outs = pl.pallas_call(
        body, name=name,
        out_shape=[pltpu.SemaphoreType.DMA((4 * size,)) for size in group_sizes for _ in range(2)]
        + [_hbm_like(s.shape, s.dtype) for s in srcs] + [_hbm_like(a.shape, a.dtype) for a in lands] + [TOKEN],
        in_specs=[HBM] * (2 * n) + [ANY] * n_after, out_specs=[SEM] * n_sems + [HBM] * (2 * n) + [TOKEN_SPEC],
        input_output_aliases={i: n_sems + i for i in range(2 * n)},
        scratch_shapes=[pltpu.SemaphoreType.DMA((n,))],
        compiler_params=SPLIT_PARAMS,
    )(*[_in_hbm(s) for s in srcs], *[_in_hbm(a) for a in lands], *after)
    sems = [(outs[2 * g], outs[2 * g + 1]) for g in range(n_groups)]
    return sems, outs[n_sems:n_sems + n], outs[n_sems + n:n_sems + 2 * n], outs[-1]


def _gather_forward(recv_sems, lands, kinds, shapes, after, *, name):
    n = len(lands)

    def body(*refs):
        recv_ref, land_refs = refs[0], refs[1:1 + n]
        fwd_send, fwd_recv = refs[2 + n], refs[3 + n]
        token = refs[-1]
        x, y, c = _my_place()
        chips = [(1 - x, y), (x, 1 - y), (1 - x, 1 - y)]
        for i in range(n):
            for j, (px, py) in enumerate(chips):
                block = _gather_window(land_refs[i], kinds[i], shapes[i], 4 * px + 2 * py + c)
                pltpu.make_async_remote_copy(
                    src_ref=block, dst_ref=block, send_sem=fwd_send.at[3 * i + j], recv_sem=recv_ref.at[4 * i + 1 + j],
                    device_id=(px, py, c), device_id_type=MESH).wait_recv()
                pltpu.make_async_remote_copy(
                    src_ref=block, dst_ref=block, send_sem=fwd_send.at[3 * i + j], recv_sem=fwd_recv.at[3 * i + j],
                    device_id=(x, y, 1 - c), device_id_type=MESH).start()
        token[...] = jnp.zeros_like(token)

    outs = pl.pallas_call(
        body, name=name,
        out_shape=[pltpu.SemaphoreType.DMA((3 * n,)), pltpu.SemaphoreType.DMA((3 * n,))]
        + [_hbm_like(a.shape, a.dtype) for a in lands] + [TOKEN],
        in_specs=[SEM] + [HBM] * n + [ANY], out_specs=[SEM, SEM] + [HBM] * n + [TOKEN_SPEC],
        input_output_aliases={1 + i: 2 + i for i in range(n)},
        compiler_params=SPLIT_PARAMS,
    )(recv_sems, *lands, after)
    return outs[0], outs[1], outs[2:2 + n], outs[-1]


def _gather_finish(send_sems, recv_sems, fwd_send, fwd_recv, srcs, lands, kinds, after, *, name):
    n = len(lands)
    shapes = [s.shape for s in srcs]

    def body(*refs):
        send_ref, recv_ref, fsend_ref, frecv_ref = refs[:4]
        src_refs, land_refs = refs[4:4 + n], refs[4 + n:4 + 2 * n]
        x, y, c = _my_place()
        chips = [(1 - x, y), (x, 1 - y), (1 - x, 1 - y)]
        sibling = (x, y, 1 - c)
        for i in range(n):
            def window(j):
                return _gather_window(land_refs[i], kinds[i], shapes[i], j)

            own = window(4 * x + 2 * y + (1 - c))
            pltpu.make_async_remote_copy(src_ref=src_refs[i], dst_ref=own, send_sem=send_ref.at[4 * i],
                                         recv_sem=recv_ref.at[4 * i], device_id=sibling, device_id_type=MESH).wait_recv()
            for j, (px, py) in enumerate(chips):
                theirs = window(4 * px + 2 * py + (1 - c))
                pltpu.make_async_remote_copy(src_ref=theirs, dst_ref=theirs, send_sem=fsend_ref.at[3 * i + j],
                                             recv_sem=frecv_ref.at[3 * i + j], device_id=sibling,
                                             device_id_type=MESH).wait_recv()
            for k in range(4):
                pltpu.make_async_remote_copy(src_ref=src_refs[i], dst_ref=own, send_sem=send_ref.at[4 * i + k],
                                             recv_sem=recv_ref.at[4 * i + k], device_id=sibling,
                                             device_id_type=MESH).wait_send()
            for j, (px, py) in enumerate(chips):
                block = window(4 * px + 2 * py + c)
                pltpu.make_async_remote_copy(src_ref=block, dst_ref=block, send_sem=fsend_ref.at[3 * i + j],
                                             recv_sem=frecv_ref.at[3 * i + j], device_id=sibling,
                                             device_id_type=MESH).wait_send()

    outs = pl.pallas_call(
        body, name=name,
        out_shape=[_hbm_like(s.shape, s.dtype) for s in srcs] + [_hbm_like(a.shape, a.dtype) for a in lands],
        in_specs=[SEM] * 4 + [HBM] * (2 * n) + [ANY], out_specs=[HBM] * (2 * n),
        input_output_aliases={4 + i: i for i in range(2 * n)},
        compiler_params=SPLIT_PARAMS,
    )(send_sems, recv_sems, fwd_send, fwd_recv, *srcs, *lands, after)
    return outs[n:]


def _pair_plan(src_ref, land_ref, x, y, c):
    return [(src_ref.at[2 * k + (1 - c)], land_ref.at[k], (x, y, 1 - c)) for k in range(N_CHIPS)]


def _chip_plan(src_ref, land_ref, x, y, c):
    chips = [(1 - x, y), (x, 1 - y), (1 - x, 1 - y)]
    return [(src_ref.at[2 * px + py], land_ref.at[k], (px, py, c)) for k, (px, py) in enumerate(chips)]


def _exchange_copies(plan, per, src_refs, land_refs, send_sems, recv_sems):
    x, y, c = _my_place()
    copies = []
    for i, (s_ref, l_ref) in enumerate(zip(src_refs, land_refs)):
        for q, (src, dst, to) in enumerate(plan(s_ref, l_ref, x, y, c)):
            copies.append(pltpu.make_async_remote_copy(
                src_ref=src, dst_ref=dst, send_sem=send_sems.at[per * i + q], recv_sem=recv_sems.at[per * i + q],
                device_id=to, device_id_type=MESH))
    return copies


def _exchange_start(srcs, plan, per, *, name):
    n = len(srcs)

    def body(*refs):
        src_refs, land_refs = refs[:n], refs[n:2 * n]
        send_sems, recv_sems = refs[2 * n], refs[2 * n + 1]
        for cp in _exchange_copies(plan, per, src_refs, land_refs, send_sems, recv_sems):
            cp.start()
        refs[-1][...] = jnp.zeros_like(refs[-1])

    lands = [lax.empty((per,) + s.shape[1:], s.dtype) for s in srcs]
    outs = pl.pallas_call(
        body, name=name,
        out_shape=[pltpu.SemaphoreType.DMA((per * n,)), pltpu.SemaphoreType.DMA((per * n,))]
        + [_hbm_like(s.shape, s.dtype) for s in srcs] + [_hbm_like(a.shape, a.dtype) for a in lands] + [TOKEN],
        in_specs=[HBM] * (2 * n), out_specs=[SEM, SEM] + [HBM] * (2 * n) + [TOKEN_SPEC],
        input_output_aliases={i: 2 + i for i in range(2 * n)},
        compiler_params=SPLIT_PARAMS,
    )(*[_in_hbm(s) for s in srcs], *[_in_hbm(a) for a in lands])
    return outs[0], outs[1], outs[2:2 + n], outs[2 + n:2 + 2 * n], outs[-1]


def _exchange_wait(send_sems, recv_sems, srcs, lands, plan, per, after, *, name):
    n = len(srcs)

    def body(*refs):
        send_ref, recv_ref = refs[0], refs[1]
        src_refs, land_refs = refs[2:2 + n], refs[2 + n:2 + 2 * n]
        copies = _exchange_copies(plan, per, src_refs, land_refs, send_ref, recv_ref)
        for cp in copies:
            cp.wait_recv()
        for cp in copies:
            cp.wait_send()

    outs = pl.pallas_call(
        body, name=name,
        out_shape=[_hbm_like(s.shape, s.dtype) for s in srcs] + [_hbm_like(a.shape, a.dtype) for a in lands],
        in_specs=[SEM, SEM] + [HBM] * (2 * n) + [ANY], out_specs=[HBM] * (2 * n),
        input_output_aliases={2 + i: i for i in range(2 * n)},
        compiler_params=SPLIT_PARAMS,
    )(send_sems, recv_sems, *srcs, *lands, after)
    return outs[:n], outs[n:]


REDUCE_BLOCK_BYTES = 1 << 20


def _row_tile(r, c):
    row_bytes = 4 * (-(-c // LANES) * LANES)
    best = r
    for d in range(SUBLANES, r, SUBLANES):
        if r % d == 0 and d * row_bytes <= REDUCE_BLOCK_BYTES:
            best = d
    return best if r * row_bytes > REDUCE_BLOCK_BYTES else r


def _reduce_pair_sum(blocked, recv, place, wire_dtype, *, name):
    _, r, c = blocked.shape
    tr = _row_tile(r, c)

    def body(place_ref, g_ref, r_ref, own_ref, send_ref):
        s = g_ref[...] + r_ref[...]
        send_ref[...] = s.astype(wire_dtype)

        @pl.when(pl.program_id(1) == place_ref[1])
        def _():
            own_ref[...] = s

    return pl.pallas_call(
        body, name=name,
        grid_spec=pltpu.PrefetchScalarGridSpec(
            num_scalar_prefetch=1, grid=(r // tr, N_CHIPS),
            in_specs=[pl.BlockSpec((None, None, tr, c), lambda i, k, place_ref: (k, place_ref[0], i, 0)),
                      pl.BlockSpec((None, tr, c), lambda i, k, place_ref: (k, i, 0))],
            out_specs=[pl.BlockSpec((tr, c), lambda i, k, place_ref: (i, 0)),
                       pl.BlockSpec((None, tr, c), lambda i, k, place_ref: (k, i, 0))]),
        out_shape=[jax.ShapeDtypeStruct((r, c), F32), jax.ShapeDtypeStruct((N_CHIPS, r, c), wire_dtype)],
        compiler_params=_params(("parallel", "arbitrary")),
    )(place, blocked.reshape(N_CHIPS, 2, r, c), recv)


def _chip_sum(own_ref, r_ref):
    return ((own_ref[...] + r_ref[0].astype(F32)) + r_ref[1].astype(F32)) + r_ref[2].astype(F32)


def _reduce_chip_sum(own, recv, *, name):
    r, c = own.shape
    tr = _row_tile(r, c)

    def body(own_ref, r_ref, o_ref):
        o_ref[...] = _chip_sum(own_ref, r_ref)

    return pl.pallas_call(
        body, name=name, grid=(r // tr,),
        in_specs=[pl.BlockSpec((tr, c), lambda i: (i, 0)), pl.BlockSpec((N_CHIPS - 1, tr, c), lambda i: (0, i, 0))],
        out_specs=pl.BlockSpec((tr, c), lambda i: (i, 0)),
        out_shape=jax.ShapeDtypeStruct((r, c), F32),
        compiler_params=_params(("parallel",)),
    )(own, recv)


def _adamw_math(w, g, m, v):
    nm = ADAM_B1 * m + (1.0 - ADAM_B1) * g
    nv = ADAM_B2 * v + (1.0 - ADAM_B2) * (g * g)
    m_hat = nm / (1.0 - ADAM_B1 ** ADAM_STEP)
    v_hat = nv / (1.0 - ADAM_B2 ** ADAM_STEP)
    return -ADAM_LR * (m_hat / (jnp.sqrt(v_hat) + ADAM_EPS) + ADAM_WD * w), nm, nv


def _adamw(w, g, m, v, *, name):
    shape = w.shape
    C = shape[-1]
    R = math.prod(shape[:-1])
    tr = _row_tile(R, C)

    def body(w_ref, g_ref, m_ref, v_ref, d_ref, nm_ref, nv_ref):
        d_ref[...], nm_ref[...], nv_ref[...] = _adamw_math(w_ref[...], g_ref[...], m_ref[...], v_ref[...])

    spec = pl.BlockSpec((tr, C), lambda i: (i, 0))
    outs = pl.pallas_call(
        body, name=name, grid=(R // tr,),
        in_specs=[spec] * 4, out_specs=[spec] * 3,
        out_shape=[jax.ShapeDtypeStruct((R, C), F32)] * 3,
        compiler_params=_params(("parallel",)),
    )(*[a.reshape(R, C) for a in (w, g, m, v)])
    return tuple(o.reshape(shape) for o in outs)


def _reduce_adamw(own, recv, w, m, v, layer, prev, *, name):
    r, c = own.shape
    tr = _row_tile(r, c)
    n_prev = 0 if prev is None else len(prev)

    def body(own_ref, r_ref, w_ref, m_ref, v_ref, *rest):
        g_ref, d_ref, nm_ref, nv_ref = rest[n_prev:]
        g = _chip_sum(own_ref, r_ref)
        g_ref[...] = g
        d_ref[...], nm_ref[...], nv_ref[...] = _adamw_math(w_ref[...], g, m_ref[...], v_ref[...])

    slot = pl.BlockSpec((None, tr, c), lambda i: (layer, i, 0))
    return pl.pallas_call(
        body, name=name, grid=(r // tr,),
        in_specs=[pl.BlockSpec((tr, c), lambda i: (i, 0)), pl.BlockSpec((N_CHIPS - 1, tr, c), lambda i: (0, i, 0)),
                  slot, slot, slot] + [ANY] * n_prev,
        out_specs=[slot] * 4,
        out_shape=[jax.ShapeDtypeStruct((DEPTH, r, c), F32)] * 4,
        input_output_aliases={5 + k: k for k in range(n_prev)},
        compiler_params=_params(("parallel",)),
    )(own, recv, w, m, v, *(prev or ()))


REPLICATED = (("mix_norm", (D_MODEL,)), ("q_norm", (HEAD_DIM,)), ("k_norm", (HEAD_DIM,)), ("sinks", (N_Q_HEADS,)),
              ("sgu_norm", (SGU_WIDTH,)), ("w_s", (SGU_GROUPS, BLOCK, BLOCK)), ("b_s", (SGU_GROUPS, BLOCK)),
              ("ffn_norm", (D_MODEL,)), ("conv_b", (2 * D_FF,)))
SHARDED = (("w_in", "blocks"), ("w_oa", "cols"), ("w_ob", "cols"), ("w_out", "rows"), ("w_up", "blocks"),
           ("conv_w", "blocks"), ("w_down", "rows"))
WEIGHT_ORDER = ("mix_norm", "w_in", "q_norm", "k_norm", "sinks", "sgu_norm", "w_s", "b_s", "w_oa", "w_ob", "w_out",
                "ffn_norm", "w_up", "conv_w", "conv_b", "w_down")
MIXER_WEIGHTS = ["w_in", "w_oa", "w_ob", "w_out"]
FFN_WEIGHTS = ["w_up", "conv_w", "w_down"]


def _small_layout():
    segs, off = {}, 0
    for l in range(DEPTH):
        for name, shape in REPLICATED:
            n = math.prod(shape)
            segs[(l, name)] = (off, n)
            off += n
    per_dev = -(-off // (N_DEV * SUBLANES * LANES)) * SUBLANES * LANES
    return segs, off, per_dev


def _pack_small(grads):
    ssegs, total, per_dev = _small_layout()
    flat = jnp.concatenate([grads[l][name].reshape(-1) for (l, name) in ssegs])
    return jnp.pad(flat, (0, N_DEV * per_dev - total)).reshape(N_DEV, per_dev // LANES, LANES)


def _unpack_small(gathered):
    ssegs, _, _ = _small_layout()
    flat = gathered.reshape(-1)
    shapes = dict(REPLICATED)
    return {name: jnp.stack([flat[ssegs[(l, name)][0]:ssegs[(l, name)][0] + ssegs[(l, name)][1]].reshape(shapes[name])
                             for l in range(DEPTH)]) for name, _ in REPLICATED}


def kernel(x, mix_norm, w_in, q_norm, k_norm, sinks, sgu_norm, w_s, b_s, w_oa, w_ob, w_out, ffn_norm, w_up, conv_w, conv_b, w_down, loss_target, m_mix_norm, m_w_in, m_q_norm, m_k_norm, m_sinks, m_sgu_norm, m_w_s, m_b_s, m_w_oa, m_w_ob, m_w_out, m_ffn_norm, m_w_up, m_conv_w, m_conv_b, m_w_down, v_mix_norm, v_w_in, v_q_norm, v_k_norm, v_sinks, v_sgu_norm, v_w_s, v_b_s, v_w_oa, v_w_ob, v_w_out, v_ffn_norm, v_w_up, v_conv_w, v_conv_b, v_w_down):
    W = dict(mix_norm=mix_norm, w_in=w_in, q_norm=q_norm, k_norm=k_norm, sinks=sinks, sgu_norm=sgu_norm, w_s=w_s, b_s=b_s,
             w_oa=w_oa, w_ob=w_ob, w_out=w_out, ffn_norm=ffn_norm, w_up=w_up, conv_w=conv_w, conv_b=conv_b, w_down=w_down)
    M = dict(mix_norm=m_mix_norm, w_in=m_w_in, q_norm=m_q_norm, k_norm=m_k_norm, sinks=m_sinks, sgu_norm=m_sgu_norm,
             w_s=m_w_s, b_s=m_b_s, w_oa=m_w_oa, w_ob=m_w_ob, w_out=m_w_out, ffn_norm=m_ffn_norm, w_up=m_w_up,
             conv_w=m_conv_w, conv_b=m_conv_b, w_down=m_w_down)
    V = dict(mix_norm=v_mix_norm, w_in=v_w_in, q_norm=v_q_norm, k_norm=v_k_norm, sinks=v_sinks, sgu_norm=v_sgu_norm,
             w_s=v_w_s, b_s=v_b_s, w_oa=v_w_oa, w_ob=v_w_ob, w_out=v_w_out, ffn_norm=v_ffn_norm, w_up=v_w_up,
             conv_w=v_conv_w, conv_b=v_conv_b, w_down=v_w_down)
    n_seq, seq, d_model = x.shape
    tokens = n_seq * seq
    mx, my, mc = _my_place()
    place = jnp.stack([mc, 2 * mx + my]).astype(jnp.int32)
    half = N_DEV // 2
    kind_of = dict(SHARDED)

    gather_groups = [[(l, n) for n in names] for l in range(DEPTH) for names in (MIXER_WEIGHTS, FFN_WEIGHTS)]
    started, in_flight = {}, {}
    weights = []
    for l in range(DEPTH):
        w = {name: W[name][l] for name, _ in REPLICATED}
        w["cb_g"], w["cb_v"] = W["conv_b"][l][:D_FF], W["conv_b"][l][D_FF:]
        w["bias_full"] = jnp.repeat(W["b_s"][l].T, SGU_WIDTH // SGU_GROUPS, axis=1)
        weights.append(w)

    def gather_start(gi, after=()):
        srcs = [W[name][l] if name == "conv_w" else W[name][l].astype(BF16) for l, name in gather_groups[gi]]
        kinds = [kind_of[name] for _, name in gather_groups[gi]]
        sems, srcs_thru, lands, token = _gather_start(srcs, kinds, [len(srcs)], after, name=f"gather_weights_start_{gi}")
        started[gi] = dict(sems=sems[0], srcs=srcs_thru, lands=lands, kinds=kinds, shapes=[s.shape for s in srcs])
        return token

    def gather_forward(gi, after):
        st = started[gi]
        in_flight[gi] = _gather_forward(st["sems"][1], st["lands"], st["kinds"], st["shapes"], after,
                                        name=f"gather_weights_forward_{gi}")
        return in_flight[gi][3]

    def gather_finish(gi, after):
        st = started.pop(gi)
        fwd_send, fwd_recv, lands_g, _ = in_flight.pop(gi)
        whole = _gather_finish(st["sems"][0], st["sems"][1], fwd_send, fwd_recv, st["srcs"], lands_g, st["kinds"], after,
                               name=f"gather_weights_finish_{gi}")
        for (l, name), arr in zip(gather_groups[gi], whole):
            w = weights[l]
            if name == "w_in":
                (w["w_in"],) = _assemble(arr, (IN_WIDTH,), _w_in_moves(), name=f"l{l}_assemble_w_in")
            elif name == "w_up":
                w["w_up_g"], w["w_up_v"] = _assemble(arr, (D_FF, D_FF), _w_up_moves(), name=f"l{l}_assemble_w_up")
            elif name == "conv_w":
                w["cw_g"] = arr[:half].transpose(1, 0, 2).reshape(3, D_FF)
                w["cw_v"] = arr[half:].transpose(1, 0, 2).reshape(3, D_FF)
            else:
                w[name] = arr

    reduce_state, results = {}, {}
    wire = {"conv_w": F32, "small": F32}

    def reduce_begin(key, names, arrays):
        send, recv, srcs_, lands_, token = _exchange_start(arrays, _pair_plan, N_CHIPS, name=f"reduce_pair_start_{key}")
        reduce_state[key] = dict(names=names, pair=(send, recv, srcs_, lands_))
        return [token]

    def reduce_pair(key, after):
        st = reduce_state[key]
        send, recv, srcs_, lands_ = st.pop("pair")
        blocked_, from_sibling = _exchange_wait(send, recv, srcs_, lands_, _pair_plan, N_CHIPS, after,
                                                name=f"reduce_pair_wait_{key}")
        sums = [_reduce_pair_sum(b, r, place, wire.get(n if isinstance(n, str) else n[1], BF16),
                                 name=f"reduce_pair_sum_{key}_{i}")
                for i, (n, b, r) in enumerate(zip(st["names"], blocked_, from_sibling))]
        st["own"] = [s[0] for s in sums]
        *st["chip"], token = _exchange_start([s[1] for s in sums], _chip_plan, N_CHIPS - 1, name=f"reduce_chip_start_{key}")
        return [token]

    def reduce_end(key, after):
        st = reduce_state.pop(key)
        send, recv, srcs_, lands_ = st["chip"]
        _, from_chips = _exchange_wait(send, recv, srcs_, lands_, _chip_plan, N_CHIPS - 1, after,
                                       name=f"reduce_chip_wait_{key}")
        done = []
        for n, own, got in zip(st["names"], st["own"], from_chips):
            if n == "small":
                results["small"] = _reduce_chip_sum(own, got, name="reduce_chip_sum_small")
            else:
                l, name = n
                results[name] = _reduce_adamw(own, got, W[name], M[name], V[name], l, results.get(name),
                                              name=f"l{l}_reduce_adamw_{name}")
                done.append(results[name][0])
        return done

    def sched(point, l, carry, g=None):
        deps = []
        if point == "fwd_start" and l == 0:
            token = gather_forward(0, gather_start(0))
            gather_finish(0, token)
            deps = [gather_start(1, [weights[0]["w_out"]])]
        elif point == "fwd_att" and l == 0:
            deps = [gather_forward(1, carry), gather_start(2, [carry])]
        elif point == "fwd_mixer_done" and l == 0:
            gather_finish(1, carry)
            deps = [gather_start(3, [carry])]
        elif point == "fwd_conv" and l == 0:
            deps = [gather_forward(2, carry)]
        elif point == "fwd_start" and l == 1:
            gather_finish(2, carry)
        elif point == "fwd_att" and l == 1:
            deps = [gather_forward(3, carry)]
        elif point == "fwd_mixer_done" and l == 1:
            gather_finish(3, carry)
        elif point == "bwd_ffn_grads":
            if l + 1 < DEPTH:
                deps += reduce_end(f"l{l + 1}_in", g["w_up_v"])
            conv_w = jnp.concatenate([g[k].reshape(3, half, W_UP_SHARD).transpose(1, 0, 2) for k in ("cw_g", "cw_v")])
            deps += reduce_begin(
                f"l{l}_ffn", [(l, "w_down"), (l, "w_up"), (l, "conv_w")],
                [g["w_down"].reshape(N_DEV, D_FF // N_DEV, D_MODEL),
                 _disassemble((g["w_up_g"], g["w_up_v"]), W_UP_SHARD, _w_up_moves(), name=f"l{l}_split_dw_up"), conv_w])
        elif point == "bwd_merge":
            deps = reduce_pair(f"l{l}_ffn", carry)
        elif point == "bwd_out_grads":
            deps = reduce_begin(
                f"l{l}_out", [(l, "w_out"), (l, "w_oa"), (l, "w_ob")],
                [g["w_out"].reshape(N_DEV, D_MODEL // N_DEV, D_MODEL),
                 _disassemble((g["w_oa"],), LANES, _w_o_moves(), name=f"l{l}_split_dw_oa"),
                 _disassemble((g["w_ob"],), LANES, _w_o_moves(), name=f"l{l}_split_dw_ob")])
        elif point == "bwd_att":
            deps = reduce_pair(f"l{l}_out", carry) + reduce_end(f"l{l}_ffn", carry)
        elif point == "bwd_w_in_grad":
            deps = reduce_begin(f"l{l}_in", [(l, "w_in")],
                                [_disassemble((g["w_in"],), W_IN_SHARD, _w_in_moves(), name=f"l{l}_split_dw_in")])
        elif point == "bwd_dh":
            deps = reduce_pair(f"l{l}_in", carry) + reduce_end(f"l{l}_out", carry)
        return deps

    loss_part, dx, grads = _local_step(x.reshape(tokens, d_model), loss_target.reshape(tokens, d_model), weights, sched,
                                       n_seq=n_seq, seq=seq)
    loss = lax.psum(loss_part, ("x", "y", "c"))

    for g in grads:
        g["conv_b"] = jnp.concatenate([g["cb_g"], g["cb_v"]])
    reduce_begin("small", ["small"], [_pack_small(grads)])
    reduce_end("l0_in", dx)
    reduce_pair("small", results["w_in"][0])
    reduce_end("small", results["w_in"][1])

    G, delta, new_m, new_v = {}, {}, {}, {}
    for name, _ in SHARDED:
        G[name], delta[name], new_m[name], new_v[name] = results[name]
    G.update(_unpack_small(_gather([results["small"]], ["blocks"], name="gather_small_grads")[0]))
    for name, _ in REPLICATED:
        delta[name], new_m[name], new_v[name] = _adamw(W[name], G[name], M[name], V[name], name=f"adamw_{name}")
    return (loss, dx.reshape(n_seq, seq, d_model), *[G[n] for n in WEIGHT_ORDER], *[delta[n] for n in WEIGHT_ORDER],
            *[new_m[n] for n in WEIGHT_ORDER], *[new_v[n] for n in WEIGHT_ORDER])
```

```python
import math

import jax
import jax.numpy as jnp
from jax import lax
from jax.experimental import pallas as pl
from jax.experimental.pallas import tpu as pltpu

F32 = jnp.float32
BF16 = jnp.bfloat16
MESH = pl.DeviceIdType.MESH

DEPTH = 2
D_MODEL = 1024
N_Q_HEADS = 8
HEAD_DIM = 64
ATT_WIDTH = 512
KV_WIDTH = 128
BLOCK = 128
SGU_WIDTH = 512
SGU_GROUPS = 8
IN_WIDTH = 3840
D_FF = 2816
NORM_EPS = 1e-6
NEG_INF = -1e30
ATT_SCALE = HEAD_DIM ** -0.5
ALIBI_SLOPES = tuple(2.0 ** (-(h + 1)) for h in range(N_Q_HEADS))
ADAM_LR, ADAM_B1, ADAM_B2, ADAM_EPS, ADAM_WD, ADAM_STEP = 0.001, 0.9, 0.999, 1e-08, 0.01, 10
N_DEV = 8
N_CHIPS = 4

QKV_WIDTH = ATT_WIDTH + 2 * KV_WIDTH
REST_WIDTH = IN_WIDTH - QKV_WIDTH
COL_SUV, COL_GA, COL_GB, COL_QKV = 0, 1024, 2048, 3072

LANES = 128
SUBLANES = 8
VMEM_LIMIT_V7X = 56 * 1024 * 1024
GELU_C = math.sqrt(2.0 / math.pi)
GELU_K = 0.044715
ANY = pl.BlockSpec(memory_space=pl.ANY)


def _params(sem=None):
    return pltpu.CompilerParams(dimension_semantics=sem, vmem_limit_bytes=VMEM_LIMIT_V7X)


def _sigmoid(x):
    return 1.0 / (1.0 + jnp.exp(-x))


def _gelu(x):
    th = jnp.tanh(GELU_C * (x + GELU_K * x * x * x))
    return 0.5 * x * (1.0 + th)


def _gelu_and_grad(x):
    x2 = x * x
    th = jnp.tanh(GELU_C * (x + GELU_K * x2 * x))
    g = 0.5 * x * (1.0 + th)
    dg = 0.5 * (1.0 + th) + 0.5 * x * (1.0 - th * th) * (GELU_C * (1.0 + 3.0 * GELU_K * x2))
    return g, dg


def _dot(a, b, dims):
    return lax.dot_general(a, b, (dims, ((), ())), preferred_element_type=F32)


def _dot_nn(a, b):
    return _dot(a, b, ((1,), (0,)))


def _dot_nt(a, b):
    return _dot(a, b, ((1,), (1,)))


def _dot_tn(a, b):
    return _dot(a, b, ((0,), (0,)))


def _lo_mask(shape):
    return lax.broadcasted_iota(jnp.int32, shape, len(shape) - 1) < (LANES // 2)


def _half_sums(x, lo):
    s_lo = jnp.sum(jnp.where(lo, x, 0.0), axis=-1, keepdims=True)
    s_all = jnp.sum(x, axis=-1, keepdims=True)
    return jnp.where(lo, s_lo, s_all - s_lo)


def _dup_half(x, half, lo):
    r = pltpu.roll(x, LANES // 2, axis=1)
    return jnp.where(lo, x, r) if half == 0 else jnp.where(lo, r, x)


def _with_deps(body, n_in, deps):
    k = len(deps)
    if not k:
        return body, [], ()

    def skipping(*refs):
        return body(*refs[:n_in], *refs[n_in + k:])

    return skipping, [ANY] * k, tuple(deps)


def _mm(a, b, *, mode, out_dtype, tm, tn, tk, name, epilogue=None, extras=(), deps=()):
    if mode == "nn":
        (M, K), N = a.shape, b.shape[1]
    elif mode == "nt":
        (M, K), N = a.shape, b.shape[0]
    else:
        (K, M), N = a.shape, b.shape[1]
    assert M % tm == 0 and N % tn == 0 and K % tk == 0, (name, M, N, K, tm, tn, tk)
    gm, gn, gk = M // tm, N // tn, K // tk
    if mode == "nn":
        a_spec = pl.BlockSpec((tm, tk), lambda i, j, k: (i, k))
        b_spec = pl.BlockSpec((tk, tn), lambda i, j, k: (k, j))
        contract = ((1,), (0,))
    elif mode == "nt":
        a_spec = pl.BlockSpec((tm, tk), lambda i, j, k: (i, k))
        b_spec = pl.BlockSpec((tn, tk), lambda i, j, k: (j, k))
        contract = ((1,), (1,))
    else:
        a_spec = pl.BlockSpec((tk, tm), lambda i, j, k: (k, i))
        b_spec = pl.BlockSpec((tk, tn), lambda i, j, k: (k, j))
        contract = ((0,), (0,))
    o_spec = pl.BlockSpec((tm, tn), lambda i, j, k: (i, j))
    n_extra = len(extras)

    def finish(acc, extra_refs, o_ref):
        if epilogue is not None:
            acc = epilogue(acc, *[r[...] for r in extra_refs])
        o_ref[...] = acc.astype(out_dtype)

    def body(a_ref, b_ref, *rest):
        extra_refs, o_ref = rest[:n_extra], rest[n_extra]
        part = _dot(a_ref[...].astype(BF16), b_ref[...].astype(BF16), contract)
        if gk == 1:
            finish(part, extra_refs, o_ref)
            return
        acc_ref = rest[n_extra + 1]
        k = pl.program_id(2)

        @pl.when(k == 0)
        def _():
            acc_ref[...] = part

        @pl.when(k > 0)
        def _():
            acc_ref[...] += part

        @pl.when(k == gk - 1)
        def _():
            finish(acc_ref[...], extra_refs, o_ref)

    body, dep_specs, dep_args = _with_deps(body, 2 + n_extra, deps)
    return pl.pallas_call(
        body,
        name=name,
        grid=(gm, gn, gk),
        in_specs=[a_spec, b_spec] + [o_spec] * n_extra + dep_specs,
        out_specs=o_spec,
        out_shape=jax.ShapeDtypeStruct((M, N), out_dtype),
        scratch_shapes=[] if gk == 1 else [pltpu.VMEM((tm, tn), F32)],
        compiler_params=_params(("parallel", "parallel", "arbitrary")),
    )(a, b, *extras, *dep_args)


def _add(acc, r):
    return acc + r


def _rms_fwd(x, gain, *, name, tm=512):
    T, D = x.shape

    def body(x_ref, g_ref, h_ref):
        xv = x_ref[...]
        r = lax.rsqrt(jnp.mean(xv * xv, axis=-1, keepdims=True) + NORM_EPS)
        h_ref[...] = (xv * r * g_ref[...]).astype(BF16)

    return pl.pallas_call(
        body, name=name, grid=(T // tm,),
        in_specs=[pl.BlockSpec((tm, D), lambda i: (i, 0)), pl.BlockSpec((1, D), lambda i: (0, 0))],
        out_specs=pl.BlockSpec((tm, D), lambda i: (i, 0)),
        out_shape=jax.ShapeDtypeStruct((T, D), BF16),
        compiler_params=_params(("parallel",)),
    )(x, gain.reshape(1, D))


def _rms_bwd(x, gain, dh, dres, *, name, tm=512, deps=()):
    T, D = x.shape

    def body(x_ref, g_ref, dh_ref, dres_ref, dx_ref, dg_ref):
        xv = x_ref[...]
        r = lax.rsqrt(jnp.mean(xv * xv, axis=-1, keepdims=True) + NORM_EPS)
        xh = xv * r
        dhv = dh_ref[...]
        dxh = dhv * g_ref[...]
        dx = r * (dxh - xh * jnp.mean(dxh * xh, axis=-1, keepdims=True))
        dx_ref[...] = dres_ref[...] + dx
        part = jnp.sum(dhv * xh, axis=0, keepdims=True)

        @pl.when(pl.program_id(0) == 0)
        def _():
            dg_ref[...] = part

        @pl.when(pl.program_id(0) > 0)
        def _():
            dg_ref[...] += part

    row = pl.BlockSpec((tm, D), lambda i: (i, 0))
    vec = pl.BlockSpec((1, D), lambda i: (0, 0))
    body, dep_specs, dep_args = _with_deps(body, 4, deps)
    dx, dg = pl.pallas_call(
        body, name=name, grid=(T // tm,),
        in_specs=[row, vec, row, row] + dep_specs,
        out_specs=[row, vec],
        out_shape=[jax.ShapeDtypeStruct((T, D), F32), jax.ShapeDtypeStruct((1, D), F32)],
        compiler_params=_params(("arbitrary",)),
    )(x, gain.reshape(1, D), dh, dres, *dep_args)
    return dx, dg.reshape(D)


def _head_norm(x, gain2, lo):
    ms = _half_sums(x * x, lo) * (1.0 / HEAD_DIM)
    r = lax.rsqrt(ms + NORM_EPS)
    xh = x * r
    return xh * gain2, xh, r


def _head_norm_bwd(xh, r, gain2, dy, lo):
    dxh = dy * gain2
    dx = r * (dxh - xh * (_half_sums(dxh * xh, lo) * (1.0 / HEAD_DIM)))
    return dx, dy * xh


def _att_masks():
    qi = lax.broadcasted_iota(jnp.int32, (BLOCK, BLOCK), 0)
    kj = lax.broadcasted_iota(jnp.int32, (BLOCK, BLOCK), 1)
    d_cur = qi - kj
    d_prev = qi - kj + BLOCK
    return d_cur >= 0, d_prev < BLOCK, d_cur.astype(F32), d_prev.astype(F32)


def _att_probs(qm, k2c, k2p, sink, slope, masks, has_prev):
    ok_c, ok_p, d_c, d_p = masks
    s_c = jnp.where(ok_c, _dot_nt(qm, k2c) * ATT_SCALE - slope * d_c, NEG_INF)
    s_p = jnp.where(jnp.logical_and(ok_p, has_prev), _dot_nt(qm, k2p) * ATT_SCALE - slope * d_p, NEG_INF)
    m = jnp.maximum(jnp.maximum(jnp.max(s_c, axis=-1, keepdims=True), jnp.max(s_p, axis=-1, keepdims=True)), sink)
    e_c = jnp.exp(s_c - m)
    e_p = jnp.exp(s_p - m)
    e_s = jnp.exp(sink - m)
    inv = 1.0 / (jnp.sum(e_c, axis=-1, keepdims=True) + jnp.sum(e_p, axis=-1, keepdims=True) + e_s)
    return e_c * inv, e_p * inv, e_s * inv


def _attention_fwd(proj, q_gain, k_gain, sinks, *, n_seq, seq, name):
    T = n_seq * seq
    nb = seq // BLOCK
    qcol, kvcol = COL_QKV // ATT_WIDTH, (COL_QKV + ATT_WIDTH) // (2 * KV_WIDTH)

    def body(q_ref, kv_ref, qg_ref, kg_ref, sink_ref, y_ref):
        lo = _lo_mask((BLOCK, LANES))
        masks = _att_masks()
        qg, kg = qg_ref[...], kg_ref[...]

        def block(i, carry):
            r0 = pl.multiple_of(i * BLOCK, BLOCK)
            rp = pl.multiple_of(jnp.maximum(i - 1, 0) * BLOCK, BLOCK)
            has_prev = i > 0
            kn_c = _head_norm(kv_ref[pl.ds(r0, BLOCK), 0:KV_WIDTH], kg, lo)[0].astype(BF16)
            kn_p = _head_norm(kv_ref[pl.ds(rp, BLOCK), 0:KV_WIDTH], kg, lo)[0].astype(BF16)
            v_c = kv_ref[pl.ds(r0, BLOCK), KV_WIDTH:2 * KV_WIDTH].astype(BF16)
            v_p = kv_ref[pl.ds(rp, BLOCK), KV_WIDTH:2 * KV_WIDTH].astype(BF16)
            for pair in range(N_Q_HEADS // 2):
                kv = pair // 2
                k2c, k2p = _dup_half(kn_c, kv, lo), _dup_half(kn_p, kv, lo)
                v2c, v2p = _dup_half(v_c, kv, lo), _dup_half(v_p, kv, lo)
                qn = _head_norm(q_ref[pl.ds(r0, BLOCK), pair * LANES:(pair + 1) * LANES], qg, lo)[0]
                out = None
                for half in range(2):
                    h = 2 * pair + half
                    mine = lo if half == 0 else jnp.logical_not(lo)
                    qm = jnp.where(mine, qn, 0.0).astype(BF16)
                    p_c, p_p, _ = _att_probs(qm, k2c, k2p, sink_ref[h], ALIBI_SLOPES[h], masks, has_prev)
                    o = _dot_nn(p_c.astype(BF16), v2c) + _dot_nn(p_p.astype(BF16), v2p)
                    out = o if out is None else jnp.where(lo, out, o)
                y_ref[pl.ds(r0, BLOCK), pair * LANES:(pair + 1) * LANES] = out.astype(BF16)
            return carry

        lax.fori_loop(0, nb, block, 0)

    vec = pl.BlockSpec((1, LANES), lambda b: (0, 0))
    return pl.pallas_call(
        body, name=name, grid=(n_seq,),
        in_specs=[pl.BlockSpec((seq, ATT_WIDTH), lambda b: (b, qcol)),
                  pl.BlockSpec((seq, 2 * KV_WIDTH), lambda b: (b, kvcol)),
                  vec, vec, pl.BlockSpec(memory_space=pltpu.SMEM)],
        out_specs=pl.BlockSpec((seq, ATT_WIDTH), lambda b: (b, 0)),
        out_shape=jax.ShapeDtypeStruct((T, ATT_WIDTH), BF16),
        compiler_params=_params(("parallel",)),
    )(proj, proj, jnp.tile(q_gain, 2).reshape(1, LANES), jnp.tile(k_gain, 2).reshape(1, LANES), sinks)


def _attention_bwd(proj, dy, q_gain, k_gain, sinks, *, n_seq, seq, name, deps=()):
    T = n_seq * seq
    nb = seq // BLOCK
    qcol, kvcol = COL_QKV // ATT_WIDTH, (COL_QKV + ATT_WIDTH) // (2 * KV_WIDTH)

    def body(q_ref, kv_ref, dy_ref, qg_ref, kg_ref, sink_ref, dqkv_ref, dqg_ref, dkg_ref, dsink_ref,
             dkn_acc, dv_acc, qg_acc, kg_acc, sink_acc):
        lo = _lo_mask((BLOCK, LANES))
        hi = jnp.logical_not(lo)
        lane = lax.broadcasted_iota(jnp.int32, (BLOCK, LANES), 1)
        masks = _att_masks()
        qg, kg = qg_ref[...], kg_ref[...]
        first = pl.program_id(0) == 0

        @pl.when(first)
        def _():
            qg_acc[...] = jnp.zeros_like(qg_acc)
            kg_acc[...] = jnp.zeros_like(kg_acc)
            sink_acc[...] = jnp.zeros_like(sink_acc)

        dkn_acc[...] = jnp.zeros_like(dkn_acc)
        dv_acc[...] = jnp.zeros_like(dv_acc)

        def block(i, carry):
            r0 = pl.multiple_of(i * BLOCK, BLOCK)
            rp = pl.multiple_of(jnp.maximum(i - 1, 0) * BLOCK, BLOCK)
            has_prev = i > 0
            kn_c = _head_norm(kv_ref[pl.ds(r0, BLOCK), 0:KV_WIDTH], kg, lo)[0].astype(BF16)
            kn_p = _head_norm(kv_ref[pl.ds(rp, BLOCK), 0:KV_WIDTH], kg, lo)[0].astype(BF16)
            v_c = kv_ref[pl.ds(r0, BLOCK), KV_WIDTH:2 * KV_WIDTH].astype(BF16)
            v_p = kv_ref[pl.ds(rp, BLOCK), KV_WIDTH:2 * KV_WIDTH].astype(BF16)
            dk_c = [jnp.zeros((BLOCK, LANES), F32) for _ in range(2)]
            dk_p = [jnp.zeros((BLOCK, LANES), F32) for _ in range(2)]
            dv_c = [jnp.zeros((BLOCK, LANES), F32) for _ in range(2)]
            dv_p = [jnp.zeros((BLOCK, LANES), F32) for _ in range(2)]
            for pair in range(N_Q_HEADS // 2):
                kv = pair // 2
                cols = slice(pair * LANES, (pair + 1) * LANES)
                k2c, k2p = _dup_half(kn_c, kv, lo), _dup_half(kn_p, kv, lo)
                v2c, v2p = _dup_half(v_c, kv, lo), _dup_half(v_p, kv, lo)
                qn, qh, qr = _head_norm(q_ref[pl.ds(r0, BLOCK), cols], qg, lo)
                do_pair = dy_ref[pl.ds(r0, BLOCK), cols]
                dqn = None
                for half in range(2):
                    h = 2 * pair + half
                    mine = lo if half == 0 else hi
                    qm = jnp.where(mine, qn, 0.0).astype(BF16)
                    dom = jnp.where(mine, do_pair, jnp.zeros_like(do_pair))
                    p_c, p_p, p_s = _att_probs(qm, k2c, k2p, sink_ref[h], ALIBI_SLOPES[h], masks, has_prev)
                    dp_c = _dot_nt(dom, v2c)
                    dp_p = _dot_nt(dom, v2p)
                    delta = jnp.sum(p_c * dp_c, axis=-1, keepdims=True) + jnp.sum(p_p * dp_p, axis=-1, keepdims=True)
                    ds_c = (p_c * (dp_c - delta)).astype(BF16)
                    ds_p = (p_p * (dp_p - delta)).astype(BF16)
                    sink_acc[...] += jnp.where(lane == h, -(p_s * delta), 0.0)
                    dq_h = (_dot_nn(ds_c, k2c) + _dot_nn(ds_p, k2p)) * ATT_SCALE
                    dqn = dq_h if dqn is None else jnp.where(lo, dqn, dq_h)
                    dk_c[kv] = dk_c[kv] + _dot_tn(ds_c, qm)
                    dk_p[kv] = dk_p[kv] + _dot_tn(ds_p, qm)
                    dv_c[kv] = dv_c[kv] + _dot_tn(p_c.astype(BF16), dom)
                    dv_p[kv] = dv_p[kv] + _dot_tn(p_p.astype(BF16), dom)
                dq, dg = _head_norm_bwd(qh, qr, qg, dqn, lo)
                dqkv_ref[pl.ds(r0, BLOCK), cols] = dq.astype(BF16)
                qg_acc[...] += dg

            def fold(parts):
                a = parts[0] + pltpu.roll(parts[0], LANES // 2, axis=1)
                b = parts[1] + pltpu.roll(parts[1], LANES // 2, axis=1)
                return jnp.where(lo, a, b)

            dkn_acc[pl.ds(r0, BLOCK), :] += fold(dk_c) * ATT_SCALE
            dkn_acc[pl.ds(rp, BLOCK), :] += fold(dk_p) * ATT_SCALE
            dv_acc[pl.ds(r0, BLOCK), :] += fold(dv_c)
            dv_acc[pl.ds(rp, BLOCK), :] += fold(dv_p)
            return carry

        lax.fori_loop(0, nb, block, 0)

        def finish(i, carry):
            r0 = pl.multiple_of(i * BLOCK, BLOCK)
            _, kh, kr = _head_norm(kv_ref[pl.ds(r0, BLOCK), 0:KV_WIDTH], kg, lo)
            dk, dg = _head_norm_bwd(kh, kr, kg, dkn_acc[pl.ds(r0, BLOCK), :], lo)
            dqkv_ref[pl.ds(r0, BLOCK), ATT_WIDTH:ATT_WIDTH + KV_WIDTH] = dk.astype(BF16)
            dqkv_ref[pl.ds(r0, BLOCK), ATT_WIDTH + KV_WIDTH:QKV_WIDTH] = dv_acc[pl.ds(r0, BLOCK), :].astype(BF16)
            kg_acc[...] += dg
            return carry

        lax.fori_loop(0, nb, finish, 0)

        @pl.when(pl.program_id(0) == n_seq - 1)
        def _():
            dqg_ref[...] = jnp.sum(qg_acc[...], axis=0, keepdims=True)
            dkg_ref[...] = jnp.sum(kg_acc[...], axis=0, keepdims=True)
            dsink_ref[...] = jnp.sum(sink_acc[...], axis=0, keepdims=True)

    vec = pl.BlockSpec((1, LANES), lambda b: (0, 0))
    acc = pltpu.VMEM((BLOCK, LANES), F32)
    body, dep_specs, dep_args = _with_deps(body, 6, deps)
    dqkv, dqg, dkg, dsink = pl.pallas_call(
        body, name=name, grid=(n_seq,),
        in_specs=[pl.BlockSpec((seq, ATT_WIDTH), lambda b: (b, qcol)),
                  pl.BlockSpec((seq, 2 * KV_WIDTH), lambda b: (b, kvcol)),
                  pl.BlockSpec((seq, ATT_WIDTH), lambda b: (b, 0)),
                  vec, vec, pl.BlockSpec(memory_space=pltpu.SMEM)] + dep_specs,
        out_specs=[pl.BlockSpec((seq, QKV_WIDTH), lambda b: (b, 0)), vec, vec, vec],
        out_shape=[jax.ShapeDtypeStruct((T, QKV_WIDTH), BF16)] + [jax.ShapeDtypeStruct((1, LANES), F32)] * 3,
        scratch_shapes=[pltpu.VMEM((seq, KV_WIDTH), F32), pltpu.VMEM((seq, KV_WIDTH), F32), acc, acc, acc],
        compiler_params=_params(("arbitrary",)),
    )(proj, proj, dy, jnp.tile(q_gain, 2).reshape(1, LANES), jnp.tile(k_gain, 2).reshape(1, LANES), sinks, *dep_args)
    half = LANES // 2
    return dqkv, dqg[0, :half] + dqg[0, half:], dkg[0, :half] + dkg[0, half:], dsink[0, :N_Q_HEADS]


def _sgu_weights(w_ref):
    r = lax.broadcasted_iota(jnp.int32, (BLOCK, BLOCK), 0)
    c = lax.broadcasted_iota(jnp.int32, (BLOCK, BLOCK), 1)
    return [jnp.where(r >= c, w_ref[g], 0.0).astype(BF16) for g in range(SGU_GROUPS)]


def _sgu_fwd(proj, gain, w_s, bias_full, *, n_seq, seq, name):
    T = n_seq * seq
    nc = seq // BLOCK

    def body(suv_ref, g_ref, w_ref, b_ref, y_ref):
        lo = _lo_mask((BLOCK, LANES))
        wm = _sgu_weights(w_ref)
        gain_v = g_ref[...]

        def chunk(c, carry):
            r0 = pl.multiple_of(c * BLOCK, BLOCK)
            gv = _gelu(suv_ref[pl.ds(r0, BLOCK), SGU_WIDTH:2 * SGU_WIDTH])
            r = lax.rsqrt(jnp.mean(gv * gv, axis=-1, keepdims=True) + NORM_EPS)
            vn = (gv * r * gain_v).astype(BF16)
            for p in range(SGU_WIDTH // LANES):
                cols = slice(p * LANES, (p + 1) * LANES)
                vp = vn[:, cols]
                mixed = jnp.where(lo, _dot_nn(wm[2 * p], vp), _dot_nn(wm[2 * p + 1], vp)) + b_ref[:, cols]
                u = _gelu(suv_ref[pl.ds(r0, BLOCK), cols])
                y_ref[pl.ds(r0, BLOCK), cols] = (u * mixed).astype(BF16)
            return carry

        lax.fori_loop(0, nc, chunk, 0)

    return pl.pallas_call(
        body, name=name, grid=(n_seq,),
        in_specs=[pl.BlockSpec((seq, 2 * SGU_WIDTH), lambda b: (b, COL_SUV // (2 * SGU_WIDTH))),
                  pl.BlockSpec((1, SGU_WIDTH), lambda b: (0, 0)),
                  pl.BlockSpec((SGU_GROUPS, BLOCK, BLOCK), lambda b: (0, 0, 0)),
                  pl.BlockSpec((BLOCK, SGU_WIDTH), lambda b: (0, 0))],
        out_specs=pl.BlockSpec((seq, SGU_WIDTH), lambda b: (b, 0)),
        out_shape=jax.ShapeDtypeStruct((T, SGU_WIDTH), BF16),
        compiler_params=_params(("parallel",)),
    )(proj, gain.reshape(1, SGU_WIDTH), w_s, bias_full)


def _sgu_bwd(proj, dy, gain, w_s, bias_full, *, n_seq, seq, name, deps=()):
    T = n_seq * seq
    nc = seq // BLOCK
    n_tiles = SGU_WIDTH // LANES

    def body(suv_ref, dy_ref, g_ref, w_ref, b_ref, dsuv_ref, dg_ref, dw_ref, db_ref, dg_acc, dw_acc, db_acc):
        lo = _lo_mask((BLOCK, LANES))
        hi = jnp.logical_not(lo)
        wm = _sgu_weights(w_ref)
        wmt = [jnp.where(lax.broadcasted_iota(jnp.int32, (BLOCK, BLOCK), 1) >= lax.broadcasted_iota(jnp.int32, (BLOCK, BLOCK), 0),
                         w_ref[g].T, 0.0).astype(BF16) for g in range(SGU_GROUPS)]
        gain_v = g_ref[...]

        @pl.when(pl.program_id(0) == 0)
        def _():
            dg_acc[...] = jnp.zeros_like(dg_acc)
            dw_acc[...] = jnp.zeros_like(dw_acc)
            db_acc[...] = jnp.zeros_like(db_acc)

        def chunk(c, carry):
            r0 = pl.multiple_of(c * BLOCK, BLOCK)
            gv, dgelu_v = _gelu_and_grad(suv_ref[pl.ds(r0, BLOCK), SGU_WIDTH:2 * SGU_WIDTH])
            r = lax.rsqrt(jnp.mean(gv * gv, axis=-1, keepdims=True) + NORM_EPS)
            vh = gv * r
            vn = (vh * gain_v).astype(BF16)
            dvn_tiles = []
            for p in range(n_tiles):
                cols = slice(p * LANES, (p + 1) * LANES)
                vp = vn[:, cols]
                mixed = jnp.where(lo, _dot_nn(wm[2 * p], vp), _dot_nn(wm[2 * p + 1], vp)) + b_ref[:, cols]
                u, dgelu_u = _gelu_and_grad(suv_ref[pl.ds(r0, BLOCK), cols])
                dyv = dy_ref[pl.ds(r0, BLOCK), cols]
                dsuv_ref[pl.ds(r0, BLOCK), cols] = (dyv * mixed * dgelu_u).astype(BF16)
                dm = dyv * u
                db_acc[:, cols] += dm
                dm_bf = dm.astype(BF16)
                dvn_tiles.append(jnp.where(lo, _dot_nn(wmt[2 * p], dm_bf), _dot_nn(wmt[2 * p + 1], dm_bf)))
                dw_acc[2 * p] += _dot_nt(jnp.where(lo, dm, 0.0).astype(BF16), vp)
                dw_acc[2 * p + 1] += _dot_nt(jnp.where(hi, dm, 0.0).astype(BF16), vp)
            dvn = jnp.concatenate(dvn_tiles, axis=1)
            dg_acc[...] += dvn * vh
            dvh = dvn * gain_v
            dgv = r * (dvh - vh * jnp.mean(dvh * vh, axis=-1, keepdims=True))
            dsuv_ref[pl.ds(r0, BLOCK), SGU_WIDTH:2 * SGU_WIDTH] = (dgv * dgelu_v).astype(BF16)
            return carry

        lax.fori_loop(0, nc, chunk, 0)

        @pl.when(pl.program_id(0) == n_seq - 1)
        def _():
            dg_ref[...] = jnp.sum(dg_acc[...], axis=0, keepdims=True)
            r = lax.broadcasted_iota(jnp.int32, (BLOCK, BLOCK), 0)
            c = lax.broadcasted_iota(jnp.int32, (BLOCK, BLOCK), 1)
            for g in range(SGU_GROUPS):
                dw_ref[g] = jnp.where(r >= c, dw_acc[g], 0.0)
            lane = lax.broadcasted_iota(jnp.int32, (BLOCK, LANES), 1)
            out = jnp.zeros((BLOCK, LANES), F32)
            for p in range(n_tiles):
                tile = db_acc[:, p * LANES:(p + 1) * LANES]
                s_lo = jnp.sum(jnp.where(lo, tile, 0.0), axis=-1, keepdims=True)
                s_hi = jnp.sum(jnp.where(hi, tile, 0.0), axis=-1, keepdims=True)
                out = jnp.where(lane == 2 * p, s_lo, out)
                out = jnp.where(lane == 2 * p + 1, s_hi, out)
            db_ref[...] = out

    body, dep_specs, dep_args = _with_deps(body, 5, deps)
    dsuv, dg, dw, db = pl.pallas_call(
        body, name=name, grid=(n_seq,),
        in_specs=[pl.BlockSpec((seq, 2 * SGU_WIDTH), lambda b: (b, COL_SUV // (2 * SGU_WIDTH))),
                  pl.BlockSpec((seq, SGU_WIDTH), lambda b: (b, 0)),
                  pl.BlockSpec((1, SGU_WIDTH), lambda b: (0, 0)),
                  pl.BlockSpec((SGU_GROUPS, BLOCK, BLOCK), lambda b: (0, 0, 0)),
                  pl.BlockSpec((BLOCK, SGU_WIDTH), lambda b: (0, 0))] + dep_specs,
        out_specs=[pl.BlockSpec((seq, 2 * SGU_WIDTH), lambda b: (b, 0)),
                   pl.BlockSpec((1, SGU_WIDTH), lambda b: (0, 0)),
                   pl.BlockSpec((SGU_GROUPS, BLOCK, BLOCK), lambda b: (0, 0, 0)),
                   pl.BlockSpec((BLOCK, LANES), lambda b: (0, 0))],
        out_shape=[jax.ShapeDtypeStruct((T, 2 * SGU_WIDTH), BF16), jax.ShapeDtypeStruct((1, SGU_WIDTH), F32),
                   jax.ShapeDtypeStruct((SGU_GROUPS, BLOCK, BLOCK), F32), jax.ShapeDtypeStruct((BLOCK, LANES), F32)],
        scratch_shapes=[pltpu.VMEM((BLOCK, SGU_WIDTH), F32), pltpu.VMEM((SGU_GROUPS, BLOCK, BLOCK), F32),
                        pltpu.VMEM((BLOCK, SGU_WIDTH), F32)],
        compiler_params=_params(("arbitrary",)),
    )(proj, dy, gain.reshape(1, SGU_WIDTH), w_s, bias_full, *dep_args)
    return dsuv, dg.reshape(SGU_WIDTH), dw, db[:, :SGU_GROUPS].T


def _merge_fwd(y_att, y_sgu, w_oa, w_ob, proj, *, name, tm=512, tn=512, deps=()):
    T = y_att.shape[0]

    def body(ya_ref, ys_ref, wa_ref, wb_ref, ga_ref, gb_ref, o_ref):
        pa = _dot_nn(ya_ref[...], wa_ref[...])
        pb = _dot_nn(ys_ref[...], wb_ref[...])
        o_ref[...] = (_sigmoid(ga_ref[...]) * pa + _sigmoid(gb_ref[...]) * pb).astype(BF16)

    act = pl.BlockSpec((tm, ATT_WIDTH), lambda i, j: (i, 0))
    wgt = pl.BlockSpec((ATT_WIDTH, tn), lambda i, j: (0, j))
    body, dep_specs, dep_args = _with_deps(body, 6, deps)
    return pl.pallas_call(
        body, name=name, grid=(T // tm, D_MODEL // tn),
        in_specs=[act, act, wgt, wgt,
                  pl.BlockSpec((tm, tn), lambda i, j: (i, j + COL_GA // tn)),
                  pl.BlockSpec((tm, tn), lambda i, j: (i, j + COL_GB // tn))] + dep_specs,
        out_specs=pl.BlockSpec((tm, tn), lambda i, j: (i, j)),
        out_shape=jax.ShapeDtypeStruct((T, D_MODEL), BF16),
        compiler_params=_params(("parallel", "parallel")),
    )(y_att, y_sgu, w_oa, w_ob, proj, proj, *dep_args)


def _merge_bwd(dx1_bf, w_out, y_att, y_sgu, w_oa, w_ob, proj, *, name, tm=512, tn=512):
    T = y_att.shape[0]

    def body(dx_ref, wo_ref, ya_ref, ys_ref, wa_ref, wb_ref, ga_ref, gb_ref, dpa_ref, dpb_ref, dga_ref, dgb_ref):
        dm = _dot_nt(dx_ref[...], wo_ref[...])
        pa = _dot_nn(ya_ref[...], wa_ref[...])
        pb = _dot_nn(ys_ref[...], wb_ref[...])
        sa = _sigmoid(ga_ref[...])
        sb = _sigmoid(gb_ref[...])
        dpa_ref[...] = (dm * sa).astype(BF16)
        dpb_ref[...] = (dm * sb).astype(BF16)
        dga_ref[...] = (dm * pa * sa * (1.0 - sa)).astype(BF16)
        dgb_ref[...] = (dm * pb * sb * (1.0 - sb)).astype(BF16)

    act = pl.BlockSpec((tm, ATT_WIDTH), lambda i, j: (i, 0))
    wgt = pl.BlockSpec((ATT_WIDTH, tn), lambda i, j: (0, j))
    out = pl.BlockSpec((tm, tn), lambda i, j: (i, j))
    return pl.pallas_call(
        body, name=name, grid=(T // tm, D_MODEL // tn),
        in_specs=[pl.BlockSpec((tm, D_MODEL), lambda i, j: (i, 0)),
                  pl.BlockSpec((tn, D_MODEL), lambda i, j: (j, 0)),
                  act, act, wgt, wgt,
                  pl.BlockSpec((tm, tn), lambda i, j: (i, j + COL_GA // tn)),
                  pl.BlockSpec((tm, tn), lambda i, j: (i, j + COL_GB // tn))],
        out_specs=[out] * 4,
        out_shape=[jax.ShapeDtypeStruct((T, D_MODEL), BF16)] * 4,
        compiler_params=_params(("parallel", "parallel")),
    )(dx1_bf, w_out, y_att, y_sgu, w_oa, w_ob, proj, proj)


CONV_ROWS = 256
CONV_TN = 256


def _shift_rows(cur, prev8, k):
    rolled = pltpu.roll(cur, k, axis=0)
    head = jnp.where(lax.broadcasted_iota(jnp.int32, prev8.shape, 0) < k, pltpu.roll(prev8, k, axis=0), rolled[:SUBLANES])
    return jnp.concatenate([head, rolled[SUBLANES:]], axis=0)


def _shift_rows_up(cur, next8, k):
    n = cur.shape[0]
    rolled = pltpu.roll(cur, n - k, axis=0)
    tail = jnp.where(lax.broadcasted_iota(jnp.int32, next8.shape, 0) >= SUBLANES - k,
                     pltpu.roll(next8, SUBLANES - k, axis=0), rolled[n - SUBLANES:])
    return jnp.concatenate([rolled[:n - SUBLANES], tail], axis=0)


def _conv_rows(z_ref, r0, first, w_ref, b_ref, rows):
    cur = z_ref[pl.ds(r0, rows), :]
    rp = pl.multiple_of(jnp.maximum(r0 - SUBLANES, 0), SUBLANES)
    prev8 = jnp.where(first, 0.0, z_ref[pl.ds(rp, SUBLANES), :])
    z1 = _shift_rows(cur, prev8, 1)
    z2 = _shift_rows(cur, prev8, 2)
    return b_ref[...] + w_ref[0:1, :] * z2 + w_ref[1:2, :] * z1 + w_ref[2:3, :] * cur


def _conv_fwd(z_g, z_v, cw_g, cw_v, cb_g, cb_v, *, n_seq, seq, name):
    T = n_seq * seq
    tn, rows = CONV_TN, CONV_ROWS

    def body(zg_ref, zv_ref, wg_ref, wv_ref, bg_ref, bv_ref, a_ref):
        def step(s, carry):
            r0 = pl.multiple_of(s * rows, rows)
            first = s == 0
            g = _conv_rows(zg_ref, r0, first, wg_ref, bg_ref, rows)
            v = _conv_rows(zv_ref, r0, first, wv_ref, bv_ref, rows)
            a_ref[pl.ds(r0, rows), :] = (g * _sigmoid(g) * v).astype(BF16)
            return carry

        lax.fori_loop(0, seq // rows, step, 0)

    zs = pl.BlockSpec((seq, tn), lambda b, j: (b, j))
    ws = pl.BlockSpec((3, tn), lambda b, j: (0, j))
    bs = pl.BlockSpec((1, tn), lambda b, j: (0, j))
    return pl.pallas_call(
        body, name=name, grid=(n_seq, D_FF // tn),
        in_specs=[zs, zs, ws, ws, bs, bs], out_specs=zs,
        out_shape=jax.ShapeDtypeStruct((T, D_FF), BF16),
        compiler_params=_params(("parallel", "parallel")),
    )(z_g, z_v, cw_g, cw_v, cb_g.reshape(1, D_FF), cb_v.reshape(1, D_FF))


def _conv_bwd(z_g, z_v, da, cw_g, cw_v, cb_g, cb_v, *, n_seq, seq, name):
    T = n_seq * seq
    tn, rows = CONV_TN, CONV_ROWS
    n_steps = seq // rows

    def body(zg_ref, zv_ref, da_ref, wg_ref, wv_ref, bg_ref, bv_ref,
             dzg_ref, dzv_ref, dwg_ref, dwv_ref, dbg_ref, dbv_ref, dcg_ref, dcv_ref):
        def grads(s, accs):
            r0 = pl.multiple_of(s * rows, rows)
            first = s == 0
            cur_g = zg_ref[pl.ds(r0, rows), :]
            cur_v = zv_ref[pl.ds(r0, rows), :]
            rp = pl.multiple_of(jnp.maximum(r0 - SUBLANES, 0), SUBLANES)
            pg = jnp.where(first, 0.0, zg_ref[pl.ds(rp, SUBLANES), :])
            pv = jnp.where(first, 0.0, zv_ref[pl.ds(rp, SUBLANES), :])
            g1, g2 = _shift_rows(cur_g, pg, 1), _shift_rows(cur_g, pg, 2)
            v1, v2 = _shift_rows(cur_v, pv, 1), _shift_rows(cur_v, pv, 2)
            g = bg_ref[...] + wg_ref[0:1, :] * g2 + wg_ref[1:2, :] * g1 + wg_ref[2:3, :] * cur_g
            v = bv_ref[...] + wv_ref[0:1, :] * v2 + wv_ref[1:2, :] * v1 + wv_ref[2:3, :] * cur_v
            sg = _sigmoid(g)
            dav = da_ref[pl.ds(r0, rows), :]
            dcg = dav * v * (sg * (1.0 + g * (1.0 - sg)))
            dcv = dav * (g * sg)
            dcg_ref[pl.ds(r0, rows), :] = dcg
            dcv_ref[pl.ds(r0, rows), :] = dcv

            def colsum(x):
                return jnp.sum(x, axis=0, keepdims=True)

            return (accs[0] + colsum(dcg * g2), accs[1] + colsum(dcg * g1), accs[2] + colsum(dcg * cur_g), accs[3] + colsum(dcg),
                    accs[4] + colsum(dcv * v2), accs[5] + colsum(dcv * v1), accs[6] + colsum(dcv * cur_v), accs[7] + colsum(dcv))

        zero = jnp.zeros((1, tn), F32)
        sums = lax.fori_loop(0, n_steps, grads, (zero,) * 8)
        first_seq = pl.program_id(1) == 0

        @pl.when(first_seq)
        def _():
            dwg_ref[...] = jnp.concatenate(sums[0:3], axis=0)
            dbg_ref[...] = sums[3]
            dwv_ref[...] = jnp.concatenate(sums[4:7], axis=0)
            dbv_ref[...] = sums[7]

        @pl.when(jnp.logical_not(first_seq))
        def _():
            dwg_ref[...] += jnp.concatenate(sums[0:3], axis=0)
            dbg_ref[...] += sums[3]
            dwv_ref[...] += jnp.concatenate(sums[4:7], axis=0)
            dbv_ref[...] += sums[7]

        def back(s, carry):
            r0 = pl.multiple_of(s * rows, rows)
            last = s == n_steps - 1
            rn = pl.multiple_of(jnp.minimum(r0 + rows, seq - SUBLANES), SUBLANES)
            for dc_ref, w_ref, dz_ref in ((dcg_ref, wg_ref, dzg_ref), (dcv_ref, wv_ref, dzv_ref)):
                cur = dc_ref[pl.ds(r0, rows), :]
                nxt = jnp.where(last, 0.0, dc_ref[pl.ds(rn, SUBLANES), :])
                u1, u2 = _shift_rows_up(cur, nxt, 1), _shift_rows_up(cur, nxt, 2)
                dz_ref[pl.ds(r0, rows), :] = (w_ref[2:3, :] * cur + w_ref[1:2, :] * u1 + w_ref[0:1, :] * u2).astype(BF16)
            return carry

        lax.fori_loop(0, n_steps, back, 0)

    zs = pl.BlockSpec((seq, tn), lambda j, b: (b, j))
    ws = pl.BlockSpec((3, tn), lambda j, b: (0, j))
    bs = pl.BlockSpec((1, tn), lambda j, b: (0, j))
    outs = pl.pallas_call(
        body, name=name, grid=(D_FF // tn, n_seq),
        in_specs=[zs, zs, zs, ws, ws, bs, bs],
        out_specs=[zs, zs, ws, ws, bs, bs],
        out_shape=[jax.ShapeDtypeStruct((T, D_FF), BF16)] * 2 + [jax.ShapeDtypeStruct((3, D_FF), F32)] * 2
        + [jax.ShapeDtypeStruct((1, D_FF), F32)] * 2,
        scratch_shapes=[pltpu.VMEM((seq, tn), F32), pltpu.VMEM((seq, tn), F32)],
        compiler_params=_params(("parallel", "arbitrary")),
    )(z_g, z_v, da, cw_g, cw_v, cb_g.reshape(1, D_FF), cb_v.reshape(1, D_FF))
    dz_g, dz_v, dw_g, dw_v, db_g, db_v = outs
    return dz_g, dz_v, dw_g, dw_v, db_g.reshape(D_FF), db_v.reshape(D_FF)


def _loss_head(y, target, *, name, tm=512):
    T, D = y.shape

    def body(y_ref, t_ref, dy_ref, dyb_ref, l_ref):
        err = y_ref[...] - t_ref[...]
        dyv = err * (1.0 / D)
        dy_ref[...] = dyv
        dyb_ref[...] = dyv.astype(BF16)
        part = jnp.sum(jnp.sum(err * err, axis=0, keepdims=True), axis=1, keepdims=True) * (0.5 / D)

        @pl.when(pl.program_id(0) == 0)
        def _():
            l_ref[...] = jnp.broadcast_to(part, l_ref.shape)

        @pl.when(pl.program_id(0) > 0)
        def _():
            l_ref[...] += jnp.broadcast_to(part, l_ref.shape)

    row = pl.BlockSpec((tm, D), lambda i: (i, 0))
    dy, dyb, l = pl.pallas_call(
        body, name=name, grid=(T // tm,),
        in_specs=[row, row],
        out_specs=[row, row, pl.BlockSpec((SUBLANES, LANES), lambda i: (0, 0))],
        out_shape=[jax.ShapeDtypeStruct((T, D), F32), jax.ShapeDtypeStruct((T, D), BF16),
                   jax.ShapeDtypeStruct((SUBLANES, LANES), F32)],
        compiler_params=_params(("arbitrary",)),
    )(y, target)
    return l[0, 0], dy, dyb


def _cast_bf16(x, *, name, tm=512):
    T, D = x.shape

    def body(x_ref, o_ref):
        o_ref[...] = x_ref[...].astype(BF16)

    row = pl.BlockSpec((tm, D), lambda i: (i, 0))
    return pl.pallas_call(body, name=name, grid=(T // tm,), in_specs=[row], out_specs=row,
                          out_shape=jax.ShapeDtypeStruct((T, D), BF16), compiler_params=_params(("parallel",)))(x)


def _layer_fwd(x, w, sched, *, n_seq, seq, l):
    tag = f"l{l}"
    deps = sched("fwd_start", l, x)
    h = _rms_fwd(x, w["mix_norm"], name=f"{tag}_mix_norm")
    proj = _mm(h, w["w_in"], mode="nn", out_dtype=F32, tm=512, tn=768, tk=D_MODEL, name=f"{tag}_proj", deps=deps)
    y_att = _attention_fwd(proj, w["q_norm"], w["k_norm"], w["sinks"], n_seq=n_seq, seq=seq, name=f"{tag}_att")
    deps = sched("fwd_att", l, y_att)
    y_sgu = _sgu_fwd(proj, w["sgu_norm"], w["w_s"], w["bias_full"], n_seq=n_seq, seq=seq, name=f"{tag}_sgu")
    merged = _merge_fwd(y_att, y_sgu, w["w_oa"], w["w_ob"], proj, name=f"{tag}_merge", deps=deps)
    x1 = _mm(merged, w["w_out"], mode="nn", out_dtype=F32, tm=512, tn=1024, tk=D_MODEL, name=f"{tag}_out",
             epilogue=_add, extras=(x,))
    deps = sched("fwd_mixer_done", l, x1)
    h2 = _rms_fwd(x1, w["ffn_norm"], name=f"{tag}_ffn_norm")
    z_g = _mm(h2, w["w_up_g"], mode="nn", out_dtype=F32, tm=512, tn=1408, tk=D_MODEL, name=f"{tag}_up_g", deps=deps)
    z_v = _mm(h2, w["w_up_v"], mode="nn", out_dtype=F32, tm=512, tn=1408, tk=D_MODEL, name=f"{tag}_up_v")
    a = _conv_fwd(z_g, z_v, w["cw_g"], w["cw_v"], w["cb_g"], w["cb_v"], n_seq=n_seq, seq=seq, name=f"{tag}_conv")
    deps = sched("fwd_conv", l, a)
    x2 = _mm(a, w["w_down"], mode="nn", out_dtype=F32, tm=512, tn=1024, tk=D_FF, name=f"{tag}_down",
             epilogue=_add, extras=(x1,), deps=deps)
    saved = dict(x=x, h=h, proj=proj, y_att=y_att, y_sgu=y_sgu, merged=merged, x1=x1, h2=h2, z_g=z_g, z_v=z_v, a=a)
    return x2, saved


def _layer_bwd(dx2, dx2_bf, w, s, sched, *, n_seq, seq, l):
    tag = f"l{l}b"
    g = {}
    da = _mm(dx2_bf, w["w_down"], mode="nt", out_dtype=F32, tm=512, tn=1408, tk=D_MODEL, name=f"{tag}_da")
    g["w_down"] = _mm(s["a"], dx2_bf, mode="tn", out_dtype=F32, tm=1408, tn=1024, tk=512, name=f"{tag}_dw_down")
    dz_g, dz_v, g["cw_g"], g["cw_v"], g["cb_g"], g["cb_v"] = _conv_bwd(
        s["z_g"], s["z_v"], da, w["cw_g"], w["cw_v"], w["cb_g"], w["cb_v"], n_seq=n_seq, seq=seq, name=f"{tag}_conv")
    dh2 = _mm(dz_g, w["w_up_g"], mode="nt", out_dtype=F32, tm=512, tn=1024, tk=1408, name=f"{tag}_dh2_g")
    dh2 = _mm(dz_v, w["w_up_v"], mode="nt", out_dtype=F32, tm=512, tn=1024, tk=1408, name=f"{tag}_dh2_v",
              epilogue=_add, extras=(dh2,))
    g["w_up_g"] = _mm(s["h2"], dz_g, mode="tn", out_dtype=F32, tm=1024, tn=1408, tk=512, name=f"{tag}_dw_up_g")
    g["w_up_v"] = _mm(s["h2"], dz_v, mode="tn", out_dtype=F32, tm=1024, tn=1408, tk=512, name=f"{tag}_dw_up_v")
    deps = sched("bwd_ffn_grads", l, dh2, g)
    dx1, g["ffn_norm"] = _rms_bwd(s["x1"], w["ffn_norm"], dh2, dx2, name=f"{tag}_ffn_norm", deps=deps)
    dx1_bf = _cast_bf16(dx1, name=f"{tag}_dx1_bf")
    dpa, dpb, dga, dgb = _merge_bwd(dx1_bf, w["w_out"], s["y_att"], s["y_sgu"], w["w_oa"], w["w_ob"], s["proj"],
                                    name=f"{tag}_merge")
    deps = sched("bwd_merge", l, dpa)
    g["w_out"] = _mm(s["merged"], dx1_bf, mode="tn", out_dtype=F32, tm=1024, tn=1024, tk=512, name=f"{tag}_dw_out",
                     deps=deps)
    dy_att = _mm(dpa, w["w_oa"], mode="nt", out_dtype=BF16, tm=512, tn=512, tk=D_MODEL, name=f"{tag}_dy_att")
    dy_sgu = _mm(dpb, w["w_ob"], mode="nt", out_dtype=F32, tm=512, tn=512, tk=D_MODEL, name=f"{tag}_dy_sgu")
    g["w_oa"] = _mm(s["y_att"], dpa, mode="tn", out_dtype=F32, tm=512, tn=1024, tk=512, name=f"{tag}_dw_oa")
    g["w_ob"] = _mm(s["y_sgu"], dpb, mode="tn", out_dtype=F32, tm=512, tn=1024, tk=512, name=f"{tag}_dw_ob")
    deps = sched("bwd_out_grads", l, dy_att, g)
    dqkv, g["q_norm"], g["k_norm"], g["sinks"] = _attention_bwd(
        s["proj"], dy_att, w["q_norm"], w["k_norm"], w["sinks"], n_seq=n_seq, seq=seq, name=f"{tag}_att", deps=deps)
    deps = sched("bwd_att", l, dqkv)
    dsuv, g["sgu_norm"], g["w_s"], g["b_s"] = _sgu_bwd(
        s["proj"], dy_sgu, w["sgu_norm"], w["w_s"], w["bias_full"], n_seq=n_seq, seq=seq, name=f"{tag}_sgu", deps=deps)
    dproj = jnp.concatenate([dsuv, dga, dgb, dqkv], axis=1)
    g["w_in"] = _mm(s["h"], dproj, mode="tn", out_dtype=F32, tm=1024, tn=768, tk=512, name=f"{tag}_dw_in")
    deps = sched("bwd_w_in_grad", l, dproj, g)
    dh = _mm(dproj, w["w_in"], mode="nt", out_dtype=F32, tm=512, tn=1024, tk=1280, name=f"{tag}_dh", deps=deps)
    deps = sched("bwd_dh", l, dh)
    dx, g["mix_norm"] = _rms_bwd(s["x"], w["mix_norm"], dh, dx1, name=f"{tag}_mix_norm", deps=deps)
    return dx, g


def _local_step(x, target, weights, sched, *, n_seq, seq):
    depth = len(weights)
    saved = []
    h = x
    for l in range(depth):
        h, s = _layer_fwd(h, weights[l], sched, n_seq=n_seq, seq=seq, l=l)
        saved.append(s)
    loss, dy, dy_bf = _loss_head(h, target, name="loss_head")
    grads = [None] * depth
    for l in reversed(range(depth)):
        if l < depth - 1:
            dy_bf = _cast_bf16(dy, name=f"l{l}b_dx2_bf")
        dy, grads[l] = _layer_bwd(dy, dy_bf, weights[l], saved[l], sched, n_seq=n_seq, seq=seq, l=l)
    return loss, dy, grads


W_IN_SHARD = IN_WIDTH // N_DEV
W_UP_SHARD = 2 * D_FF // N_DEV
COL_MOVE_ROWS = 256


def _w_in_moves():
    moves = []
    for j in range(N_DEV):
        a, b = j * W_IN_SHARD, (j + 1) * W_IN_SHARD
        if a < QKV_WIDTH:
            moves.append((j, 0, min(b, QKV_WIDTH) - a, 0, a + REST_WIDTH))
        if b > QKV_WIDTH:
            lo = max(a, QKV_WIDTH)
            moves.append((j, lo - a, b - a, 0, lo - QKV_WIDTH))
    return tuple(moves)


def _w_up_moves():
    half = N_DEV // 2
    return tuple((j, 0, W_UP_SHARD, j // half, (j % half) * W_UP_SHARD) for j in range(N_DEV))


def _w_o_moves():
    return tuple((j, 0, LANES, 0, j * LANES) for j in range(N_DEV))


def _assemble(blocks, widths, moves, *, name):
    _, R, w = blocks.shape
    tr = min(R, COL_MOVE_ROWS)

    def body(b_ref, *o_refs):
        for j, lo, hi, which, at in moves:
            o_refs[which][:, at:at + hi - lo] = b_ref[j, :, lo:hi]

    return pl.pallas_call(
        body, name=name, grid=(R // tr,),
        in_specs=[pl.BlockSpec((N_DEV, tr, w), lambda i: (0, i, 0))],
        out_specs=[pl.BlockSpec((tr, n), lambda i: (i, 0)) for n in widths],
        out_shape=[jax.ShapeDtypeStruct((R, n), blocks.dtype) for n in widths],
        compiler_params=_params(("parallel",)),
    )(blocks)


def _disassemble(mats, w, moves, *, name):
    R = mats[0].shape[0]
    tr = min(R, COL_MOVE_ROWS)
    n = len(mats)

    def body(*refs):
        m_refs, o_ref = refs[:n], refs[n]
        for j, lo, hi, which, at in moves:
            o_ref[j, :, lo:hi] = m_refs[which][:, at:at + hi - lo]

    return pl.pallas_call(
        body, name=name, grid=(R // tr,),
        in_specs=[pl.BlockSpec((tr, m.shape[1]), lambda i: (i, 0)) for m in mats],
        out_specs=pl.BlockSpec((N_DEV, tr, w), lambda i: (0, i, 0)),
        out_shape=jax.ShapeDtypeStruct((N_DEV, R, w), mats[0].dtype),
        compiler_params=_params(("parallel",)),
    )(*mats)


def _my_place():
    return lax.axis_index("x"), lax.axis_index("y"), lax.axis_index("c")


def _gathered_shape(shape, kind):
    r, c = shape
    return {"blocks": (N_DEV, r, c), "rows": (N_DEV * r, c), "cols": (r, N_DEV * c)}[kind]


def _gather_window(ref, kind, shape, j):
    r, c = shape
    if kind == "blocks":
        return ref.at[j]
    if kind == "rows":
        return ref.at[pl.ds(pl.multiple_of(j * r, r), r), :]
    return ref.at[:, pl.ds(pl.multiple_of(j * c, c), c)]


def _gather(srcs, kinds, *, name):
    n = len(srcs)
    shapes = [s.shape for s in srcs]
    per = 7

    def body(*refs):
        src_refs, dst_refs = refs[:n], refs[n:2 * n]
        send_sems, recv_sems, local_sems = refs[2 * n:]
        x, y, c = _my_place()
        me, sibling = (x, y, c), (x, y, 1 - c)
        chips = [(1 - x, y), (x, 1 - y), (1 - x, 1 - y)]

        def at(i, px, py, pc):
            return _gather_window(dst_refs[i], kinds[i], shapes[i], 4 * px + 2 * py + pc)

        def copy(i, k, block, to, src=None):
            return pltpu.make_async_remote_copy(
                src_ref=at(i, *block) if src is None else src, dst_ref=at(i, *block),
                send_sem=send_sems.at[per * i + k], recv_sem=recv_sems.at[per * i + k], device_id=to, device_id_type=MESH)

        mine = [pltpu.make_async_copy(src_refs[i], at(i, *me), local_sems.at[i]) for i in range(n)]
        for cp in mine:
            cp.start()
        started = []
        for i in range(n):
            first = [copy(i, 0, me, sibling, src=src_refs[i])]
            first += [copy(i, 1 + j, me, (*chip, c), src=src_refs[i]) for j, chip in enumerate(chips)]
            for cp in first:
                cp.start()
            started += first
        for i in range(n):
            for j, chip in enumerate(chips):
                copy(i, 1 + j, (*chip, c), me).wait_recv()
                fwd = copy(i, 4 + j, (*chip, c), sibling)
                fwd.start()
                started.append(fwd)
        for i in range(n):
            copy(i, 0, sibling, me).wait_recv()
            for j, chip in enumerate(chips):
                copy(i, 4 + j, (*chip, 1 - c), me).wait_recv()
        for cp in started:
            cp.wait_send()
        for cp in mine:
            cp.wait()

    return pl.pallas_call(
        body, name=name,
        out_shape=[jax.ShapeDtypeStruct(_gathered_shape(s.shape, k), s.dtype) for s, k in zip(srcs, kinds)],
        in_specs=[ANY] * n, out_specs=[ANY] * n,
        scratch_shapes=[pltpu.SemaphoreType.DMA((per * n,)), pltpu.SemaphoreType.DMA((per * n,)),
                        pltpu.SemaphoreType.DMA((n,))],
    )(*srcs)


HBM = pl.BlockSpec(memory_space=pltpu.HBM)
SEM = pl.BlockSpec(memory_space=pltpu.SEMAPHORE)
TOKEN = jax.ShapeDtypeStruct((SUBLANES, LANES), F32)
TOKEN_SPEC = pl.BlockSpec(memory_space=pltpu.VMEM)
SPLIT_PARAMS = pltpu.CompilerParams(has_side_effects=pltpu.SideEffectType.DATAFLOW_SIDE_EFFECTING)


def _in_hbm(x):
    return pltpu.with_memory_space_constraint(x, pltpu.HBM)


def _hbm_like(shape, dtype):
    return pltpu.HBM(shape, dtype)


def _place_own(shards, kinds, dtypes, *, name):
    n = len(shards)
    shapes = [s.shape for s in shards]

    def body(*refs):
        s_refs, land_refs, bufs, sems = refs[:n], refs[n:2 * n], refs[2 * n:3 * n], refs[3 * n]
        x, y, c = _my_place()
        copies = []
        for i in range(n):
            bufs[i][...] = s_refs[i][...].astype(dtypes[i])
            copies.append(pltpu.make_async_copy(
                bufs[i], _gather_window(land_refs[i], kinds[i], shapes[i], 4 * x + 2 * y + c), sems.at[i]))
        for cp in copies:
            cp.start()
        for cp in copies:
            cp.wait()

    return pl.pallas_call(
        body, name=name,
        out_shape=[jax.ShapeDtypeStruct(_gathered_shape(s, k), d) for s, k, d in zip(shapes, kinds, dtypes)],
        in_specs=[pl.BlockSpec(memory_space=pltpu.VMEM)] * n, out_specs=[ANY] * n,
        scratch_shapes=[pltpu.VMEM(s, d) for s, d in zip(shapes, dtypes)] + [pltpu.SemaphoreType.DMA((n,))],
        compiler_params=_params(),
    )(*shards)


def _gather_start(lands, kinds, shapes, after=(), *, name):
    n = len(lands)
    n_after = len(after)

    def body(*refs):
        land_refs = refs[:n]
        send_sems, recv_sems = refs[n + n_after], refs[n + n_after + 1]
        x, y, c = _my_place()
        targets = [(x, y, 1 - c), (1 - x, y, c), (x, 1 - y, c), (1 - x, 1 - y, c)]
        for i in range(n):
            own = _gather_window(land_refs[i], kinds[i], shapes[i], 4 * x + 2 * y + c)
            for k, to in enumerate(targets):
                pltpu.make_async_remote_copy(
                    src_ref=own, dst_ref=own, send_sem=send_sems.at[4 * i + k], recv_sem=recv_sems.at[4 * i + k],
                    device_id=to, device_id_type=MESH).start()
        refs[-1][...] = jnp.zeros_like(refs[-1])

    outs = pl.pallas_call(
        body, name=name,
        out_shape=[pltpu.SemaphoreType.DMA((4 * n,)), pltpu.SemaphoreType.DMA((4 * n,))]
        + [_hbm_like(a.shape, a.dtype) for a in lands] + [TOKEN],
        in_specs=[HBM] * n + [ANY] * n_after, out_specs=[SEM, SEM] + [HBM] * n + [TOKEN_SPEC],
        input_output_aliases={i: 2 + i for i in range(n)},
        compiler_params=SPLIT_PARAMS,
    )(*[_in_hbm(a) for a in lands], *after)
    return outs[0], outs[1], outs[2:2 + n], outs[-1]


def _gather_forward(recv_sems, lands, kinds, shapes, after, *, name):
    n = len(lands)

    def body(*refs):
        recv_ref, land_refs = refs[0], refs[1:1 + n]
        fwd_send, fwd_recv = refs[2 + n], refs[3 + n]
        token = refs[-1]
        x, y, c = _my_place()
        chips = [(1 - x, y), (x, 1 - y), (1 - x, 1 - y)]
        for i in range(n):
            for j, (px, py) in enumerate(chips):
                block = _gather_window(land_refs[i], kinds[i], shapes[i], 4 * px + 2 * py + c)
                pltpu.make_async_remote_copy(
                    src_ref=block, dst_ref=block, send_sem=fwd_send.at[3 * i + j], recv_sem=recv_ref.at[4 * i + 1 + j],
                    device_id=(px, py, c), device_id_type=MESH).wait_recv()
                pltpu.make_async_remote_copy(
                    src_ref=block, dst_ref=block, send_sem=fwd_send.at[3 * i + j], recv_sem=fwd_recv.at[3 * i + j],
                    device_id=(x, y, 1 - c), device_id_type=MESH).start()
        token[...] = jnp.zeros_like(token)

    outs = pl.pallas_call(
        body, name=name,
        out_shape=[pltpu.SemaphoreType.DMA((3 * n,)), pltpu.SemaphoreType.DMA((3 * n,))]
        + [_hbm_like(a.shape, a.dtype) for a in lands] + [TOKEN],
        in_specs=[SEM] + [HBM] * n + [ANY], out_specs=[SEM, SEM] + [HBM] * n + [TOKEN_SPEC],
        input_output_aliases={1 + i: 2 + i for i in range(n)},
        compiler_params=SPLIT_PARAMS,
    )(recv_sems, *lands, after)
    return outs[0], outs[1], outs[2:2 + n], outs[-1]


def _gather_finish(send_sems, recv_sems, fwd_send, fwd_recv, lands, kinds, shapes, after, *, name):
    n = len(lands)

    def body(*refs):
        send_ref, recv_ref, fsend_ref, frecv_ref = refs[:4]
        land_refs = refs[4:4 + n]
        x, y, c = _my_place()
        chips = [(1 - x, y), (x, 1 - y), (1 - x, 1 - y)]
        sibling = (x, y, 1 - c)
        for i in range(n):
            def window(j):
                return _gather_window(land_refs[i], kinds[i], shapes[i], j)

            mine, theirs = window(4 * x + 2 * y + c), window(4 * x + 2 * y + (1 - c))
            pltpu.make_async_remote_copy(src_ref=mine, dst_ref=theirs, send_sem=send_ref.at[4 * i],
                                         recv_sem=recv_ref.at[4 * i], device_id=sibling, device_id_type=MESH).wait_recv()
            for j, (px, py) in enumerate(chips):
                block = window(4 * px + 2 * py + (1 - c))
                pltpu.make_async_remote_copy(src_ref=block, dst_ref=block, send_sem=fsend_ref.at[3 * i + j],
                                             recv_sem=frecv_ref.at[3 * i + j], device_id=sibling,
                                             device_id_type=MESH).wait_recv()
            for k in range(4):
                pltpu.make_async_remote_copy(src_ref=mine, dst_ref=mine, send_sem=send_ref.at[4 * i + k],
                                             recv_sem=recv_ref.at[4 * i + k], device_id=sibling,
                                             device_id_type=MESH).wait_send()
            for j, (px, py) in enumerate(chips):
                block = window(4 * px + 2 * py + c)
                pltpu.make_async_remote_copy(src_ref=block, dst_ref=block, send_sem=fsend_ref.at[3 * i + j],
                                             recv_sem=frecv_ref.at[3 * i + j], device_id=sibling,
                                             device_id_type=MESH).wait_send()

    return pl.pallas_call(
        body, name=name,
        out_shape=[_hbm_like(a.shape, a.dtype) for a in lands],
        in_specs=[SEM] * 4 + [HBM] * n + [ANY], out_specs=[HBM] * n,
        input_output_aliases={4 + i: i for i in range(n)},
        compiler_params=SPLIT_PARAMS,
    )(send_sems, recv_sems, fwd_send, fwd_recv, *lands, after)


def _pair_plan(src_ref, land_ref, x, y, c):
    return [(src_ref.at[2 * k + (1 - c)], land_ref.at[k], (x, y, 1 - c)) for k in range(N_CHIPS)]


def _chip_plan(src_ref, land_ref, x, y, c):
    chips = [(1 - x, y), (x, 1 - y), (1 - x, 1 - y)]
    return [(src_ref.at[2 * px + py], land_ref.at[k], (px, py, c)) for k, (px, py) in enumerate(chips)]


def _exchange_copies(plan, per, src_refs, land_refs, send_sems, recv_sems):
    x, y, c = _my_place()
    copies = []
    for i, (s_ref, l_ref) in enumerate(zip(src_refs, land_refs)):
        for q, (src, dst, to) in enumerate(plan(s_ref, l_ref, x, y, c)):
            copies.append(pltpu.make_async_remote_copy(
                src_ref=src, dst_ref=dst, send_sem=send_sems.at[per * i + q], recv_sem=recv_sems.at[per * i + q],
                device_id=to, device_id_type=MESH))
    return copies


def _exchange_start(srcs, plan, per, *, name):
    n = len(srcs)

    def body(*refs):
        src_refs, land_refs = refs[:n], refs[n:2 * n]
        send_sems, recv_sems = refs[2 * n], refs[2 * n + 1]
        for cp in _exchange_copies(plan, per, src_refs, land_refs, send_sems, recv_sems):
            cp.start()
        refs[-1][...] = jnp.zeros_like(refs[-1])

    lands = [lax.empty((per,) + s.shape[1:], s.dtype) for s in srcs]
    outs = pl.pallas_call(
        body, name=name,
        out_shape=[pltpu.SemaphoreType.DMA((per * n,)), pltpu.SemaphoreType.DMA((per * n,))]
        + [_hbm_like(s.shape, s.dtype) for s in srcs] + [_hbm_like(a.shape, a.dtype) for a in lands] + [TOKEN],
        in_specs=[HBM] * (2 * n), out_specs=[SEM, SEM] + [HBM] * (2 * n) + [TOKEN_SPEC],
        input_output_aliases={i: 2 + i for i in range(2 * n)},
        compiler_params=SPLIT_PARAMS,
    )(*[_in_hbm(s) for s in srcs], *[_in_hbm(a) for a in lands])
    return outs[0], outs[1], outs[2:2 + n], outs[2 + n:2 + 2 * n], outs[-1]


def _exchange_wait(send_sems, recv_sems, srcs, lands, plan, per, after, *, name):
    n = len(srcs)

    def body(*refs):
        send_ref, recv_ref = refs[0], refs[1]
        src_refs, land_refs = refs[2:2 + n], refs[2 + n:2 + 2 * n]
        copies = _exchange_copies(plan, per, src_refs, land_refs, send_ref, recv_ref)
        for cp in copies:
            cp.wait_recv()
        for cp in copies:
            cp.wait_send()

    outs = pl.pallas_call(
        body, name=name,
        out_shape=[_hbm_like(s.shape, s.dtype) for s in srcs] + [_hbm_like(a.shape, a.dtype) for a in lands],
        in_specs=[SEM, SEM] + [HBM] * (2 * n) + [ANY], out_specs=[HBM] * (2 * n),
        input_output_aliases={2 + i: i for i in range(2 * n)},
        compiler_params=SPLIT_PARAMS,
    )(send_sems, recv_sems, *srcs, *lands, after)
    return outs[:n], outs[n:]


REDUCE_BLOCK_BYTES = 1 << 20


def _row_tile(r, c):
    row_bytes = 4 * (-(-c // LANES) * LANES)
    best = r
    for d in range(SUBLANES, r, SUBLANES):
        if r % d == 0 and d * row_bytes <= REDUCE_BLOCK_BYTES:
            best = d
    return best if r * row_bytes > REDUCE_BLOCK_BYTES else r


def _reduce_pair_sum(blocked, recv, place, wire_dtype, *, name):
    _, r, c = blocked.shape
    tr = _row_tile(r, c)

    def body(place_ref, g_ref, r_ref, own_ref, send_ref):
        s = g_ref[...] + r_ref[...]
        send_ref[...] = s.astype(wire_dtype)

        @pl.when(pl.program_id(1) == place_ref[1])
        def _():
            own_ref[...] = s

    return pl.pallas_call(
        body, name=name,
        grid_spec=pltpu.PrefetchScalarGridSpec(
            num_scalar_prefetch=1, grid=(r // tr, N_CHIPS),
            in_specs=[pl.BlockSpec((None, None, tr, c), lambda i, k, place_ref: (k, place_ref[0], i, 0)),
                      pl.BlockSpec((None, tr, c), lambda i, k, place_ref: (k, i, 0))],
            out_specs=[pl.BlockSpec((tr, c), lambda i, k, place_ref: (i, 0)),
                       pl.BlockSpec((None, tr, c), lambda i, k, place_ref: (k, i, 0))]),
        out_shape=[jax.ShapeDtypeStruct((r, c), F32), jax.ShapeDtypeStruct((N_CHIPS, r, c), wire_dtype)],
        compiler_params=_params(("parallel", "arbitrary")),
    )(place, blocked.reshape(N_CHIPS, 2, r, c), recv)


def _chip_sum(own_ref, r_ref):
    return ((own_ref[...] + r_ref[0].astype(F32)) + r_ref[1].astype(F32)) + r_ref[2].astype(F32)


def _reduce_chip_sum(own, recv, *, name):
    r, c = own.shape
    tr = _row_tile(r, c)

    def body(own_ref, r_ref, o_ref):
        o_ref[...] = _chip_sum(own_ref, r_ref)

    return pl.pallas_call(
        body, name=name, grid=(r // tr,),
        in_specs=[pl.BlockSpec((tr, c), lambda i: (i, 0)), pl.BlockSpec((N_CHIPS - 1, tr, c), lambda i: (0, i, 0))],
        out_specs=pl.BlockSpec((tr, c), lambda i: (i, 0)),
        out_shape=jax.ShapeDtypeStruct((r, c), F32),
        compiler_params=_params(("parallel",)),
    )(own, recv)


def _adamw_math(w, g, m, v):
    nm = ADAM_B1 * m + (1.0 - ADAM_B1) * g
    nv = ADAM_B2 * v + (1.0 - ADAM_B2) * (g * g)
    m_hat = nm / (1.0 - ADAM_B1 ** ADAM_STEP)
    v_hat = nv / (1.0 - ADAM_B2 ** ADAM_STEP)
    return -ADAM_LR * (m_hat / (jnp.sqrt(v_hat) + ADAM_EPS) + ADAM_WD * w), nm, nv


def _adamw(w, g, m, v, *, name):
    shape = w.shape
    C = shape[-1]
    R = math.prod(shape[:-1])
    tr = _row_tile(R, C)

    def body(w_ref, g_ref, m_ref, v_ref, d_ref, nm_ref, nv_ref):
        d_ref[...], nm_ref[...], nv_ref[...] = _adamw_math(w_ref[...], g_ref[...], m_ref[...], v_ref[...])

    spec = pl.BlockSpec((tr, C), lambda i: (i, 0))
    outs = pl.pallas_call(
        body, name=name, grid=(R // tr,),
        in_specs=[spec] * 4, out_specs=[spec] * 3,
        out_shape=[jax.ShapeDtypeStruct((R, C), F32)] * 3,
        compiler_params=_params(("parallel",)),
    )(*[a.reshape(R, C) for a in (w, g, m, v)])
    return tuple(o.reshape(shape) for o in outs)


def _reduce_adamw(own, recv, w, m, v, layer, prev, *, name):
    r, c = own.shape
    tr = _row_tile(r, c)
    n_prev = 0 if prev is None else len(prev)

    def body(own_ref, r_ref, w_ref, m_ref, v_ref, *rest):
        g_ref, d_ref, nm_ref, nv_ref = rest[n_prev:]
        g = _chip_sum(own_ref, r_ref)
        g_ref[...] = g
        d_ref[...], nm_ref[...], nv_ref[...] = _adamw_math(w_ref[...], g, m_ref[...], v_ref[...])

    slot = pl.BlockSpec((None, tr, c), lambda i: (layer, i, 0))
    return pl.pallas_call(
        body, name=name, grid=(r // tr,),
        in_specs=[pl.BlockSpec((tr, c), lambda i: (i, 0)), pl.BlockSpec((N_CHIPS - 1, tr, c), lambda i: (0, i, 0)),
                  slot, slot, slot] + [ANY] * n_prev,
        out_specs=[slot] * 4,
        out_shape=[jax.ShapeDtypeStruct((DEPTH, r, c), F32)] * 4,
        input_output_aliases={5 + k: k for k in range(n_prev)},
        compiler_params=_params(("parallel",)),
    )(own, recv, w, m, v, *(prev or ()))


REPLICATED = (("mix_norm", (D_MODEL,)), ("q_norm", (HEAD_DIM,)), ("k_norm", (HEAD_DIM,)), ("sinks", (N_Q_HEADS,)),
              ("sgu_norm", (SGU_WIDTH,)), ("w_s", (SGU_GROUPS, BLOCK, BLOCK)), ("b_s", (SGU_GROUPS, BLOCK)),
              ("ffn_norm", (D_MODEL,)), ("conv_b", (2 * D_FF,)))
SHARDED = (("w_in", "blocks"), ("w_oa", "cols"), ("w_ob", "cols"), ("w_out", "rows"), ("w_up", "blocks"),
           ("conv_w", "blocks"), ("w_down", "rows"))
WEIGHT_ORDER = ("mix_norm", "w_in", "q_norm", "k_norm", "sinks", "sgu_norm", "w_s", "b_s", "w_oa", "w_ob", "w_out",
                "ffn_norm", "w_up", "conv_w", "conv_b", "w_down")
MIXER_WEIGHTS = ["w_in", "w_oa", "w_ob", "w_out"]
FFN_WEIGHTS = ["w_up", "conv_w", "w_down"]


def _small_layout():
    segs, off = {}, 0
    for l in range(DEPTH):
        for name, shape in REPLICATED:
            n = math.prod(shape)
            segs[(l, name)] = (off, n)
            off += n
    per_dev = -(-off // (N_DEV * SUBLANES * LANES)) * SUBLANES * LANES
    return segs, off, per_dev


def _pack_small(grads):
    ssegs, total, per_dev = _small_layout()
    flat = jnp.concatenate([grads[l][name].reshape(-1) for (l, name) in ssegs])
    return jnp.pad(flat, (0, N_DEV * per_dev - total)).reshape(N_DEV, per_dev // LANES, LANES)


def _unpack_small(gathered):
    ssegs, _, _ = _small_layout()
    flat = gathered.reshape(-1)
    shapes = dict(REPLICATED)
    return {name: jnp.stack([flat[ssegs[(l, name)][0]:ssegs[(l, name)][0] + ssegs[(l, name)][1]].reshape(shapes[name])
                             for l in range(DEPTH)]) for name, _ in REPLICATED}


def kernel(x, mix_norm, w_in, q_norm, k_norm, sinks, sgu_norm, w_s, b_s, w_oa, w_ob, w_out, ffn_norm, w_up, conv_w, conv_b, w_down, loss_target, m_mix_norm, m_w_in, m_q_norm, m_k_norm, m_sinks, m_sgu_norm, m_w_s, m_b_s, m_w_oa, m_w_ob, m_w_out, m_ffn_norm, m_w_up, m_conv_w, m_conv_b, m_w_down, v_mix_norm, v_w_in, v_q_norm, v_k_norm, v_sinks, v_sgu_norm, v_w_s, v_b_s, v_w_oa, v_w_ob, v_w_out, v_ffn_norm, v_w_up, v_conv_w, v_conv_b, v_w_down):
    W = dict(mix_norm=mix_norm, w_in=w_in, q_norm=q_norm, k_norm=k_norm, sinks=sinks, sgu_norm=sgu_norm, w_s=w_s, b_s=b_s,
             w_oa=w_oa, w_ob=w_ob, w_out=w_out, ffn_norm=ffn_norm, w_up=w_up, conv_w=conv_w, conv_b=conv_b, w_down=w_down)
    M = dict(mix_norm=m_mix_norm, w_in=m_w_in, q_norm=m_q_norm, k_norm=m_k_norm, sinks=m_sinks, sgu_norm=m_sgu_norm,
             w_s=m_w_s, b_s=m_b_s, w_oa=m_w_oa, w_ob=m_w_ob, w_out=m_w_out, ffn_norm=m_ffn_norm, w_up=m_w_up,
             conv_w=m_conv_w, conv_b=m_conv_b, w_down=m_w_down)
    V = dict(mix_norm=v_mix_norm, w_in=v_w_in, q_norm=v_q_norm, k_norm=v_k_norm, sinks=v_sinks, sgu_norm=v_sgu_norm,
             w_s=v_w_s, b_s=v_b_s, w_oa=v_w_oa, w_ob=v_w_ob, w_out=v_w_out, ffn_norm=v_ffn_norm, w_up=v_w_up,
             conv_w=v_conv_w, conv_b=v_conv_b, w_down=v_w_down)
    n_seq, seq, d_model = x.shape
    tokens = n_seq * seq
    mx, my, mc = _my_place()
    place = jnp.stack([mc, 2 * mx + my]).astype(jnp.int32)
    half = N_DEV // 2
    kind_of = dict(SHARDED)

    gather_groups = [[(l, n) for n in names] for l in range(DEPTH) for names in (MIXER_WEIGHTS, FFN_WEIGHTS)]
    started, in_flight = {}, {}
    weights = []
    for l in range(DEPTH):
        w = {name: W[name][l] for name, _ in REPLICATED}
        w["cb_g"], w["cb_v"] = W["conv_b"][l][:D_FF], W["conv_b"][l][D_FF:]
        w["bias_full"] = jnp.repeat(W["b_s"][l].T, SGU_WIDTH // SGU_GROUPS, axis=1)
        weights.append(w)

    def gather_start(gi, after=()):
        shards = [W[name][l] for l, name in gather_groups[gi]]
        kinds = [kind_of[name] for _, name in gather_groups[gi]]
        shapes = [s.shape for s in shards]
        lands = _place_own(shards, kinds, [F32 if name == "conv_w" else BF16 for _, name in gather_groups[gi]],
                           name=f"gather_weights_own_{gi}")
        send, recv, lands, token = _gather_start(lands, kinds, shapes, after, name=f"gather_weights_start_{gi}")
        started[gi] = dict(sems=(send, recv), lands=lands, kinds=kinds, shapes=shapes)
        return token

    def gather_forward(gi, after):
        st = started[gi]
        in_flight[gi] = _gather_forward(st["sems"][1], st["lands"], st["kinds"], st["shapes"], after,
                                        name=f"gather_weights_forward_{gi}")
        return in_flight[gi][3]

    def gather_finish(gi, after):
        st = started.pop(gi)
        fwd_send, fwd_recv, lands_g, _ = in_flight.pop(gi)
        whole = _gather_finish(st["sems"][0], st["sems"][1], fwd_send, fwd_recv, lands_g, st["kinds"], st["shapes"], after,
                               name=f"gather_weights_finish_{gi}")
        for (l, name), arr in zip(gather_groups[gi], whole):
            w = weights[l]
            if name == "w_in":
                (w["w_in"],) = _assemble(arr, (IN_WIDTH,), _w_in_moves(), name=f"l{l}_assemble_w_in")
            elif name == "w_up":
                w["w_up_g"], w["w_up_v"] = _assemble(arr, (D_FF, D_FF), _w_up_moves(), name=f"l{l}_assemble_w_up")
            elif name == "conv_w":
                w["cw_g"] = arr[:half].transpose(1, 0, 2).reshape(3, D_FF)
                w["cw_v"] = arr[half:].transpose(1, 0, 2).reshape(3, D_FF)
            else:
                w[name] = arr

    reduce_state, results = {}, {}
    wire = {"conv_w": F32, "small": F32}

    def reduce_begin(key, names, arrays):
        send, recv, srcs_, lands_, token = _exchange_start(arrays, _pair_plan, N_CHIPS, name=f"reduce_pair_start_{key}")
        reduce_state[key] = dict(names=names, pair=(send, recv, srcs_, lands_))
        return [token]

    def reduce_pair(key, after):
        st = reduce_state[key]
        send, recv, srcs_, lands_ = st.pop("pair")
        blocked_, from_sibling = _exchange_wait(send, recv, srcs_, lands_, _pair_plan, N_CHIPS, after,
                                                name=f"reduce_pair_wait_{key}")
        sums = [_reduce_pair_sum(b, r, place, wire.get(n if isinstance(n, str) else n[1], BF16),
                                 name=f"reduce_pair_sum_{key}_{i}")
                for i, (n, b, r) in enumerate(zip(st["names"], blocked_, from_sibling))]
        st["own"] = [s[0] for s in sums]
        *st["chip"], token = _exchange_start([s[1] for s in sums], _chip_plan, N_CHIPS - 1, name=f"reduce_chip_start_{key}")
        return [token]

    def reduce_end(key, after):
        st = reduce_state.pop(key)
        send, recv, srcs_, lands_ = st["chip"]
        _, from_chips = _exchange_wait(send, recv, srcs_, lands_, _chip_plan, N_CHIPS - 1, after,
                                       name=f"reduce_chip_wait_{key}")
        done = []
        for n, own, got in zip(st["names"], st["own"], from_chips):
            if n == "small":
                results["small"] = _reduce_chip_sum(own, got, name="reduce_chip_sum_small")
            else:
                l, name = n
                results[name] = _reduce_adamw(own, got, W[name], M[name], V[name], l, results.get(name),
                                              name=f"l{l}_reduce_adamw_{name}")
                done.append(results[name][0])
        return done

    def sched(point, l, carry, g=None):
        deps = []
        if point == "fwd_start" and l == 0:
            token = gather_forward(0, gather_start(0))
            gather_finish(0, token)
            deps = [gather_start(1, [weights[0]["w_out"]])]
        elif point == "fwd_att" and l == 0:
            deps = [gather_forward(1, carry), gather_start(2, [carry])]
        elif point == "fwd_mixer_done" and l == 0:
            gather_finish(1, carry)
            deps = [gather_start(3, [carry])]
        elif point == "fwd_conv" and l == 0:
            deps = [gather_forward(2, carry)]
        elif point == "fwd_start" and l == 1:
            gather_finish(2, carry)
        elif point == "fwd_att" and l == 1:
            deps = [gather_forward(3, carry)]
        elif point == "fwd_mixer_done" and l == 1:
            gather_finish(3, carry)
        elif point == "bwd_ffn_grads":
            if l + 1 < DEPTH:
                deps += reduce_end(f"l{l + 1}_in", g["w_up_v"])
            conv_w = jnp.concatenate([g[k].reshape(3, half, W_UP_SHARD).transpose(1, 0, 2) for k in ("cw_g", "cw_v")])
            deps += reduce_begin(
                f"l{l}_ffn", [(l, "w_down"), (l, "w_up"), (l, "conv_w")],
                [g["w_down"].reshape(N_DEV, D_FF // N_DEV, D_MODEL),
                 _disassemble((g["w_up_g"], g["w_up_v"]), W_UP_SHARD, _w_up_moves(), name=f"l{l}_split_dw_up"), conv_w])
        elif point == "bwd_merge":
            deps = reduce_pair(f"l{l}_ffn", carry)
        elif point == "bwd_out_grads":
            deps = reduce_begin(
                f"l{l}_out", [(l, "w_out"), (l, "w_oa"), (l, "w_ob")],
                [g["w_out"].reshape(N_DEV, D_MODEL // N_DEV, D_MODEL),
                 _disassemble((g["w_oa"],), LANES, _w_o_moves(), name=f"l{l}_split_dw_oa"),
                 _disassemble((g["w_ob"],), LANES, _w_o_moves(), name=f"l{l}_split_dw_ob")])
        elif point == "bwd_att":
            deps = reduce_pair(f"l{l}_out", carry) + reduce_end(f"l{l}_ffn", carry)
        elif point == "bwd_w_in_grad":
            deps = reduce_begin(f"l{l}_in", [(l, "w_in")],
                                [_disassemble((g["w_in"],), W_IN_SHARD, _w_in_moves(), name=f"l{l}_split_dw_in")])
        elif point == "bwd_dh":
            deps = reduce_pair(f"l{l}_in", carry) + reduce_end(f"l{l}_out", carry)
        return deps

    loss_part, dx, grads = _local_step(x.reshape(tokens, d_model), loss_target.reshape(tokens, d_model), weights, sched,
                                       n_seq=n_seq, seq=seq)
    loss = lax.psum(loss_part, ("x", "y", "c"))

    for g in grads:
        g["conv_b"] = jnp.concatenate([g["cb_g"], g["cb_v"]])
    reduce_begin("small", ["small"], [_pack_small(grads)])
    reduce_end("l0_in", dx)
    reduce_pair("small", results["w_in"][0])
    reduce_end("small", results["w_in"][1])

    G, delta, new_m, new_v = {}, {}, {}, {}
    for name, _ in SHARDED:
        G[name], delta[name], new_m[name], new_v[name] = results[name]
    G.update(_unpack_small(_gather([results["small"]], ["blocks"], name="gather_small_grads")[0]))
    for name, _ in REPLICATED:
        delta[name], new_m[name], new_v[name] = _adamw(W[name], G[name], M[name], V[name], name=f"adamw_{name}")
    return (loss, dx.reshape(n_seq, seq, d_model), *[G[n] for n in WEIGHT_ORDER], *[delta[n] for n in WEIGHT_ORDER],
            *[new_m[n] for n in WEIGHT_ORDER], *[new_v[n] for n in WEIGHT_ORDER])
```

```python
import math

import jax
import jax.numpy as jnp
from jax import lax
from jax.experimental import pallas as pl
from jax.experimental.pallas import tpu as pltpu

F32 = jnp.float32
BF16 = jnp.bfloat16
MESH = pl.DeviceIdType.MESH

DEPTH = 2
D_MODEL = 1024
N_Q_HEADS = 8
HEAD_DIM = 64
ATT_WIDTH = 512
KV_WIDTH = 128
BLOCK = 128
SGU_WIDTH = 512
SGU_GROUPS = 8
IN_WIDTH = 3840
D_FF = 2816
NORM_EPS = 1e-6
NEG_INF = -1e30
ATT_SCALE = HEAD_DIM ** -0.5
ALIBI_SLOPES = tuple(2.0 ** (-(h + 1)) for h in range(N_Q_HEADS))
ADAM_LR, ADAM_B1, ADAM_B2, ADAM_EPS, ADAM_WD, ADAM_STEP = 0.001, 0.9, 0.999, 1e-08, 0.01, 10
N_DEV = 8
N_CHIPS = 4

QKV_WIDTH = ATT_WIDTH + 2 * KV_WIDTH
REST_WIDTH = IN_WIDTH - QKV_WIDTH
COL_SUV, COL_GA, COL_GB, COL_QKV = 0, 1024, 2048, 3072

LANES = 128
SUBLANES = 8
VMEM_LIMIT_V7X = 56 * 1024 * 1024
GELU_C = math.sqrt(2.0 / math.pi)
GELU_K = 0.044715
ANY = pl.BlockSpec(memory_space=pl.ANY)


def _params(sem=None):
    return pltpu.CompilerParams(dimension_semantics=sem, vmem_limit_bytes=VMEM_LIMIT_V7X)


def _sigmoid(x):
    return 1.0 / (1.0 + jnp.exp(-x))


def _gelu(x):
    th = jnp.tanh(GELU_C * (x + GELU_K * x * x * x))
    return 0.5 * x * (1.0 + th)


def _gelu_and_grad(x):
    x2 = x * x
    th = jnp.tanh(GELU_C * (x + GELU_K * x2 * x))
    g = 0.5 * x * (1.0 + th)
    dg = 0.5 * (1.0 + th) + 0.5 * x * (1.0 - th * th) * (GELU_C * (1.0 + 3.0 * GELU_K * x2))
    return g, dg


def _dot(a, b, dims):
    return lax.dot_general(a, b, (dims, ((), ())), preferred_element_type=F32)


def _dot_nn(a, b):
    return _dot(a, b, ((1,), (0,)))


def _dot_nt(a, b):
    return _dot(a, b, ((1,), (1,)))


def _dot_tn(a, b):
    return _dot(a, b, ((0,), (0,)))


def _lo_mask(shape):
    return lax.broadcasted_iota(jnp.int32, shape, len(shape) - 1) < (LANES // 2)


def _half_sums(x, lo):
    s_lo = jnp.sum(jnp.where(lo, x, 0.0), axis=-1, keepdims=True)
    s_all = jnp.sum(x, axis=-1, keepdims=True)
    return jnp.where(lo, s_lo, s_all - s_lo)


def _dup_half(x, half, lo):
    r = pltpu.roll(x, LANES // 2, axis=1)
    return jnp.where(lo, x, r) if half == 0 else jnp.where(lo, r, x)


def _with_deps(body, n_in, deps):
    k = len(deps)
    if not k:
        return body, [], ()

    def skipping(*refs):
        return body(*refs[:n_in], *refs[n_in + k:])

    return skipping, [ANY] * k, tuple(deps)


MM_VMEM_BUDGET = 40 * 1024 * 1024
MM_MAX_TILE = 1408
MM_MAX_TK = 4096
MM_STEP_BYTES = 1 << 20


def _divisors(n, step, cap):
    return [d for d in range(step, min(n, cap) + 1, step) if n % d == 0] or [n]


def _mm_tiles(M, N, K, out_bytes, n_extra):
    best = None
    for tm in _divisors(M, LANES, MM_MAX_TILE):
        for tn in _divisors(N, LANES, MM_MAX_TILE):
            for tk in _divisors(K, 4 * LANES, MM_MAX_TK):
                vmem = 4 * (tm * tk + tk * tn) + 2 * tm * tn * (out_bytes + 4 * n_extra) + (0 if tk == K else 4 * tm * tn)
                if vmem > MM_VMEM_BUDGET:
                    continue
                traffic = 2 * M * K * (N // tn) + 2 * K * N * (M // tm) + M * N * (out_bytes + 4 * n_extra)
                cost = traffic + (K // tk - 1) * 8 * M * N + (M // tm) * (N // tn) * (K // tk) * MM_STEP_BYTES
                if best is None or cost < best[0]:
                    best = (cost, tm, tn, tk)
    assert best is not None, (M, N, K)
    return best[1:]


def _mm(a, b, *, mode, out_dtype, name, epilogue=None, extras=(), deps=()):
    if mode == "nn":
        (M, K), N = a.shape, b.shape[1]
    elif mode == "nt":
        (M, K), N = a.shape, b.shape[0]
    else:
        (K, M), N = a.shape, b.shape[1]
    tm, tn, tk = _mm_tiles(M, N, K, jnp.dtype(out_dtype).itemsize, len(extras))
    gm, gn, gk = M // tm, N // tn, K // tk
    if mode == "nn":
        a_spec = pl.BlockSpec((tm, tk), lambda i, j, k: (i, k))
        b_spec = pl.BlockSpec((tk, tn), lambda i, j, k: (k, j))
        contract = ((1,), (0,))
    elif mode == "nt":
        a_spec = pl.BlockSpec((tm, tk), lambda i, j, k: (i, k))
        b_spec = pl.BlockSpec((tn, tk), lambda i, j, k: (j, k))
        contract = ((1,), (1,))
    else:
        a_spec = pl.BlockSpec((tk, tm), lambda i, j, k: (k, i))
        b_spec = pl.BlockSpec((tk, tn), lambda i, j, k: (k, j))
        contract = ((0,), (0,))
    o_spec = pl.BlockSpec((tm, tn), lambda i, j, k: (i, j))
    n_extra = len(extras)

    def finish(acc, extra_refs, o_ref):
        if epilogue is not None:
            acc = epilogue(acc, *[r[...] for r in extra_refs])
        o_ref[...] = acc.astype(out_dtype)

    def body(a_ref, b_ref, *rest):
        extra_refs, o_ref = rest[:n_extra], rest[n_extra]
        part = _dot(a_ref[...].astype(BF16), b_ref[...].astype(BF16), contract)
        if gk == 1:
            finish(part, extra_refs, o_ref)
            return
        acc_ref = rest[n_extra + 1]
        k = pl.program_id(2)

        @pl.when(k == 0)
        def _():
            acc_ref[...] = part

        @pl.when(k > 0)
        def _():
            acc_ref[...] += part

        @pl.when(k == gk - 1)
        def _():
            finish(acc_ref[...], extra_refs, o_ref)

    body, dep_specs, dep_args = _with_deps(body, 2 + n_extra, deps)
    return pl.pallas_call(
        body,
        name=name,
        grid=(gm, gn, gk),
        in_specs=[a_spec, b_spec] + [o_spec] * n_extra + dep_specs,
        out_specs=o_spec,
        out_shape=jax.ShapeDtypeStruct((M, N), out_dtype),
        scratch_shapes=[] if gk == 1 else [pltpu.VMEM((tm, tn), F32)],
        compiler_params=_params(("parallel", "parallel", "arbitrary")),
    )(a, b, *extras, *dep_args)


def _add(acc, r):
    return acc + r


def _rms_fwd(x, gain, *, name, tm=512):
    T, D = x.shape

    def body(x_ref, g_ref, h_ref):
        xv = x_ref[...]
        r = lax.rsqrt(jnp.mean(xv * xv, axis=-1, keepdims=True) + NORM_EPS)
        h_ref[...] = (xv * r * g_ref[...]).astype(BF16)

    return pl.pallas_call(
        body, name=name, grid=(T // tm,),
        in_specs=[pl.BlockSpec((tm, D), lambda i: (i, 0)), pl.BlockSpec((1, D), lambda i: (0, 0))],
        out_specs=pl.BlockSpec((tm, D), lambda i: (i, 0)),
        out_shape=jax.ShapeDtypeStruct((T, D), BF16),
        compiler_params=_params(("parallel",)),
    )(x, gain.reshape(1, D))


def _rms_bwd(x, gain, dh, dres, *, name, tm=512, deps=()):
    T, D = x.shape

    def body(x_ref, g_ref, dh_ref, dres_ref, dx_ref, dg_ref):
        xv = x_ref[...]
        r = lax.rsqrt(jnp.mean(xv * xv, axis=-1, keepdims=True) + NORM_EPS)
        xh = xv * r
        dhv = dh_ref[...]
        dxh = dhv * g_ref[...]
        dx = r * (dxh - xh * jnp.mean(dxh * xh, axis=-1, keepdims=True))
        dx_ref[...] = dres_ref[...] + dx
        part = jnp.sum(dhv * xh, axis=0, keepdims=True)

        @pl.when(pl.program_id(0) == 0)
        def _():
            dg_ref[...] = part

        @pl.when(pl.program_id(0) > 0)
        def _():
            dg_ref[...] += part

    row = pl.BlockSpec((tm, D), lambda i: (i, 0))
    vec = pl.BlockSpec((1, D), lambda i: (0, 0))
    body, dep_specs, dep_args = _with_deps(body, 4, deps)
    dx, dg = pl.pallas_call(
        body, name=name, grid=(T // tm,),
        in_specs=[row, vec, row, row] + dep_specs,
        out_specs=[row, vec],
        out_shape=[jax.ShapeDtypeStruct((T, D), F32), jax.ShapeDtypeStruct((1, D), F32)],
        compiler_params=_params(("arbitrary",)),
    )(x, gain.reshape(1, D), dh, dres, *dep_args)
    return dx, dg.reshape(D)


def _head_norm(x, gain2, lo):
    ms = _half_sums(x * x, lo) * (1.0 / HEAD_DIM)
    r = lax.rsqrt(ms + NORM_EPS)
    xh = x * r
    return xh * gain2, xh, r


def _head_norm_bwd(xh, r, gain2, dy, lo):
    dxh = dy * gain2
    dx = r * (dxh - xh * (_half_sums(dxh * xh, lo) * (1.0 / HEAD_DIM)))
    return dx, dy * xh


def _att_masks():
    qi = lax.broadcasted_iota(jnp.int32, (BLOCK, BLOCK), 0)
    kj = lax.broadcasted_iota(jnp.int32, (BLOCK, BLOCK), 1)
    d_cur = qi - kj
    d_prev = qi - kj + BLOCK
    return d_cur >= 0, d_prev < BLOCK, d_cur.astype(F32), d_prev.astype(F32)


def _att_probs(qm, k2c, k2p, sink, slope, masks, has_prev):
    ok_c, ok_p, d_c, d_p = masks
    s_c = jnp.where(ok_c, _dot_nt(qm, k2c) * ATT_SCALE - slope * d_c, NEG_INF)
    s_p = jnp.where(jnp.logical_and(ok_p, has_prev), _dot_nt(qm, k2p) * ATT_SCALE - slope * d_p, NEG_INF)
    m = jnp.maximum(jnp.maximum(jnp.max(s_c, axis=-1, keepdims=True), jnp.max(s_p, axis=-1, keepdims=True)), sink)
    e_c = jnp.exp(s_c - m)
    e_p = jnp.exp(s_p - m)
    e_s = jnp.exp(sink - m)
    inv = 1.0 / (jnp.sum(e_c, axis=-1, keepdims=True) + jnp.sum(e_p, axis=-1, keepdims=True) + e_s)
    return e_c * inv, e_p * inv, e_s * inv


def _attention_fwd(proj, q_gain, k_gain, sinks, *, n_seq, seq, name):
    T = n_seq * seq
    nb = seq // BLOCK
    qcol, kvcol = COL_QKV // ATT_WIDTH, (COL_QKV + ATT_WIDTH) // (2 * KV_WIDTH)

    def body(q_ref, kv_ref, qg_ref, kg_ref, sink_ref, y_ref):
        lo = _lo_mask((BLOCK, LANES))
        masks = _att_masks()
        qg, kg = qg_ref[...], kg_ref[...]

        def block(i, carry):
            r0 = pl.multiple_of(i * BLOCK, BLOCK)
            rp = pl.multiple_of(jnp.maximum(i - 1, 0) * BLOCK, BLOCK)
            has_prev = i > 0
            kn_c = _head_norm(kv_ref[pl.ds(r0, BLOCK), 0:KV_WIDTH], kg, lo)[0].astype(BF16)
            kn_p = _head_norm(kv_ref[pl.ds(rp, BLOCK), 0:KV_WIDTH], kg, lo)[0].astype(BF16)
            v_c = kv_ref[pl.ds(r0, BLOCK), KV_WIDTH:2 * KV_WIDTH].astype(BF16)
            v_p = kv_ref[pl.ds(rp, BLOCK), KV_WIDTH:2 * KV_WIDTH].astype(BF16)
            for pair in range(N_Q_HEADS // 2):
                kv = pair // 2
                k2c, k2p = _dup_half(kn_c, kv, lo), _dup_half(kn_p, kv, lo)
                v2c, v2p = _dup_half(v_c, kv, lo), _dup_half(v_p, kv, lo)
                qn = _head_norm(q_ref[pl.ds(r0, BLOCK), pair * LANES:(pair + 1) * LANES], qg, lo)[0]
                out = None
                for half in range(2):
                    h = 2 * pair + half
                    mine = lo if half == 0 else jnp.logical_not(lo)
                    qm = jnp.where(mine, qn, 0.0).astype(BF16)
                    p_c, p_p, _ = _att_probs(qm, k2c, k2p, sink_ref[h], ALIBI_SLOPES[h], masks, has_prev)
                    o = _dot_nn(p_c.astype(BF16), v2c) + _dot_nn(p_p.astype(BF16), v2p)
                    out = o if out is None else jnp.where(lo, out, o)
                y_ref[pl.ds(r0, BLOCK), pair * LANES:(pair + 1) * LANES] = out.astype(BF16)
            return carry

        lax.fori_loop(0, nb, block, 0)

    vec = pl.BlockSpec((1, LANES), lambda b: (0, 0))
    return pl.pallas_call(
        body, name=name, grid=(n_seq,),
        in_specs=[pl.BlockSpec((seq, ATT_WIDTH), lambda b: (b, qcol)),
                  pl.BlockSpec((seq, 2 * KV_WIDTH), lambda b: (b, kvcol)),
                  vec, vec, pl.BlockSpec(memory_space=pltpu.SMEM)],
        out_specs=pl.BlockSpec((seq, ATT_WIDTH), lambda b: (b, 0)),
        out_shape=jax.ShapeDtypeStruct((T, ATT_WIDTH), BF16),
        compiler_params=_params(("parallel",)),
    )(proj, proj, jnp.tile(q_gain, 2).reshape(1, LANES), jnp.tile(k_gain, 2).reshape(1, LANES), sinks)


def _attention_bwd(proj, dy, q_gain, k_gain, sinks, *, n_seq, seq, name, deps=()):
    T = n_seq * seq
    nb = seq // BLOCK
    qcol, kvcol = COL_QKV // ATT_WIDTH, (COL_QKV + ATT_WIDTH) // (2 * KV_WIDTH)

    def body(q_ref, kv_ref, dy_ref, qg_ref, kg_ref, sink_ref, dqkv_ref, dqg_ref, dkg_ref, dsink_ref,
             dkn_acc, dv_acc, qg_acc, kg_acc, sink_acc):
        lo = _lo_mask((BLOCK, LANES))
        hi = jnp.logical_not(lo)
        lane = lax.broadcasted_iota(jnp.int32, (BLOCK, LANES), 1)
        masks = _att_masks()
        qg, kg = qg_ref[...], kg_ref[...]
        first = pl.program_id(0) == 0

        @pl.when(first)
        def _():
            qg_acc[...] = jnp.zeros_like(qg_acc)
            kg_acc[...] = jnp.zeros_like(kg_acc)
            sink_acc[...] = jnp.zeros_like(sink_acc)

        dkn_acc[...] = jnp.zeros_like(dkn_acc)
        dv_acc[...] = jnp.zeros_like(dv_acc)

        def block(i, carry):
            r0 = pl.multiple_of(i * BLOCK, BLOCK)
            rp = pl.multiple_of(jnp.maximum(i - 1, 0) * BLOCK, BLOCK)
            has_prev = i > 0
            kn_c = _head_norm(kv_ref[pl.ds(r0, BLOCK), 0:KV_WIDTH], kg, lo)[0].astype(BF16)
            kn_p = _head_norm(kv_ref[pl.ds(rp, BLOCK), 0:KV_WIDTH], kg, lo)[0].astype(BF16)
            v_c = kv_ref[pl.ds(r0, BLOCK), KV_WIDTH:2 * KV_WIDTH].astype(BF16)
            v_p = kv_ref[pl.ds(rp, BLOCK), KV_WIDTH:2 * KV_WIDTH].astype(BF16)
            dk_c = [jnp.zeros((BLOCK, LANES), F32) for _ in range(2)]
            dk_p = [jnp.zeros((BLOCK, LANES), F32) for _ in range(2)]
            dv_c = [jnp.zeros((BLOCK, LANES), F32) for _ in range(2)]
            dv_p = [jnp.zeros((BLOCK, LANES), F32) for _ in range(2)]
            for pair in range(N_Q_HEADS // 2):
                kv = pair // 2
                cols = slice(pair * LANES, (pair + 1) * LANES)
                k2c, k2p = _dup_half(kn_c, kv, lo), _dup_half(kn_p, kv, lo)
                v2c, v2p = _dup_half(v_c, kv, lo), _dup_half(v_p, kv, lo)
                qn, qh, qr = _head_norm(q_ref[pl.ds(r0, BLOCK), cols], qg, lo)
                do_pair = dy_ref[pl.ds(r0, BLOCK), cols]
                dqn = None
                for half in range(2):
                    h = 2 * pair + half
                    mine = lo if half == 0 else hi
                    qm = jnp.where(mine, qn, 0.0).astype(BF16)
                    dom = jnp.where(mine, do_pair, jnp.zeros_like(do_pair))
                    p_c, p_p, p_s = _att_probs(qm, k2c, k2p, sink_ref[h], ALIBI_SLOPES[h], masks, has_prev)
                    dp_c = _dot_nt(dom, v2c)
                    dp_p = _dot_nt(dom, v2p)
                    delta = jnp.sum(p_c * dp_c, axis=-1, keepdims=True) + jnp.sum(p_p * dp_p, axis=-1, keepdims=True)
                    ds_c = (p_c * (dp_c - delta)).astype(BF16)
                    ds_p = (p_p * (dp_p - delta)).astype(BF16)
                    sink_acc[...] += jnp.where(lane == h, -(p_s * delta), 0.0)
                    dq_h = (_dot_nn(ds_c, k2c) + _dot_nn(ds_p, k2p)) * ATT_SCALE
                    dqn = dq_h if dqn is None else jnp.where(lo, dqn, dq_h)
                    dk_c[kv] = dk_c[kv] + _dot_tn(ds_c, qm)
                    dk_p[kv] = dk_p[kv] + _dot_tn(ds_p, qm)
                    dv_c[kv] = dv_c[kv] + _dot_tn(p_c.astype(BF16), dom)
                    dv_p[kv] = dv_p[kv] + _dot_tn(p_p.astype(BF16), dom)
                dq, dg = _head_norm_bwd(qh, qr, qg, dqn, lo)
                dqkv_ref[pl.ds(r0, BLOCK), cols] = dq.astype(BF16)
                qg_acc[...] += dg

            def fold(parts):
                a = parts[0] + pltpu.roll(parts[0], LANES // 2, axis=1)
                b = parts[1] + pltpu.roll(parts[1], LANES // 2, axis=1)
                return jnp.where(lo, a, b)

            dkn_acc[pl.ds(r0, BLOCK), :] += fold(dk_c) * ATT_SCALE
            dkn_acc[pl.ds(rp, BLOCK), :] += fold(dk_p) * ATT_SCALE
            dv_acc[pl.ds(r0, BLOCK), :] += fold(dv_c)
            dv_acc[pl.ds(rp, BLOCK), :] += fold(dv_p)
            return carry

        lax.fori_loop(0, nb, block, 0)

        def finish(i, carry):
            r0 = pl.multiple_of(i * BLOCK, BLOCK)
            _, kh, kr = _head_norm(kv_ref[pl.ds(r0, BLOCK), 0:KV_WIDTH], kg, lo)
            dk, dg = _head_norm_bwd(kh, kr, kg, dkn_acc[pl.ds(r0, BLOCK), :], lo)
            dqkv_ref[pl.ds(r0, BLOCK), ATT_WIDTH:ATT_WIDTH + KV_WIDTH] = dk.astype(BF16)
            dqkv_ref[pl.ds(r0, BLOCK), ATT_WIDTH + KV_WIDTH:QKV_WIDTH] = dv_acc[pl.ds(r0, BLOCK), :].astype(BF16)
            kg_acc[...] += dg
            return carry

        lax.fori_loop(0, nb, finish, 0)

        @pl.when(pl.program_id(0) == n_seq - 1)
        def _():
            dqg_ref[...] = jnp.sum(qg_acc[...], axis=0, keepdims=True)
            dkg_ref[...] = jnp.sum(kg_acc[...], axis=0, keepdims=True)
            dsink_ref[...] = jnp.sum(sink_acc[...], axis=0, keepdims=True)

    vec = pl.BlockSpec((1, LANES), lambda b: (0, 0))
    acc = pltpu.VMEM((BLOCK, LANES), F32)
    body, dep_specs, dep_args = _with_deps(body, 6, deps)
    dqkv, dqg, dkg, dsink = pl.pallas_call(
        body, name=name, grid=(n_seq,),
        in_specs=[pl.BlockSpec((seq, ATT_WIDTH), lambda b: (b, qcol)),
                  pl.BlockSpec((seq, 2 * KV_WIDTH), lambda b: (b, kvcol)),
                  pl.BlockSpec((seq, ATT_WIDTH), lambda b: (b, 0)),
                  vec, vec, pl.BlockSpec(memory_space=pltpu.SMEM)] + dep_specs,
        out_specs=[pl.BlockSpec((seq, QKV_WIDTH), lambda b: (b, 0)), vec, vec, vec],
        out_shape=[jax.ShapeDtypeStruct((T, QKV_WIDTH), BF16)] + [jax.ShapeDtypeStruct((1, LANES), F32)] * 3,
        scratch_shapes=[pltpu.VMEM((seq, KV_WIDTH), F32), pltpu.VMEM((seq, KV_WIDTH), F32), acc, acc, acc],
        compiler_params=_params(("arbitrary",)),
    )(proj, proj, dy, jnp.tile(q_gain, 2).reshape(1, LANES), jnp.tile(k_gain, 2).reshape(1, LANES), sinks, *dep_args)
    half = LANES // 2
    return dqkv, dqg[0, :half] + dqg[0, half:], dkg[0, :half] + dkg[0, half:], dsink[0, :N_Q_HEADS]


def _sgu_weights(w_ref):
    r = lax.broadcasted_iota(jnp.int32, (BLOCK, BLOCK), 0)
    c = lax.broadcasted_iota(jnp.int32, (BLOCK, BLOCK), 1)
    return [jnp.where(r >= c, w_ref[g], 0.0).astype(BF16) for g in range(SGU_GROUPS)]


def _sgu_fwd(proj, gain, w_s, bias_full, *, n_seq, seq, name):
    T = n_seq * seq
    nc = seq // BLOCK

    def body(suv_ref, g_ref, w_ref, b_ref, y_ref):
        lo = _lo_mask((BLOCK, LANES))
        wm = _sgu_weights(w_ref)
        gain_v = g_ref[...]

        def chunk(c, carry):
            r0 = pl.multiple_of(c * BLOCK, BLOCK)
            gv = _gelu(suv_ref[pl.ds(r0, BLOCK), SGU_WIDTH:2 * SGU_WIDTH])
            r = lax.rsqrt(jnp.mean(gv * gv, axis=-1, keepdims=True) + NORM_EPS)
            vn = (gv * r * gain_v).astype(BF16)
            for p in range(SGU_WIDTH // LANES):
                cols = slice(p * LANES, (p + 1) * LANES)
                vp = vn[:, cols]
                mixed = jnp.where(lo, _dot_nn(wm[2 * p], vp), _dot_nn(wm[2 * p + 1], vp)) + b_ref[:, cols]
                u = _gelu(suv_ref[pl.ds(r0, BLOCK), cols])
                y_ref[pl.ds(r0, BLOCK), cols] = (u * mixed).astype(BF16)
            return carry

        lax.fori_loop(0, nc, chunk, 0)

    return pl.pallas_call(
        body, name=name, grid=(n_seq,),
        in_specs=[pl.BlockSpec((seq, 2 * SGU_WIDTH), lambda b: (b, COL_SUV // (2 * SGU_WIDTH))),
                  pl.BlockSpec((1, SGU_WIDTH), lambda b: (0, 0)),
                  pl.BlockSpec((SGU_GROUPS, BLOCK, BLOCK), lambda b: (0, 0, 0)),
                  pl.BlockSpec((BLOCK, SGU_WIDTH), lambda b: (0, 0))],
        out_specs=pl.BlockSpec((seq, SGU_WIDTH), lambda b: (b, 0)),
        out_shape=jax.ShapeDtypeStruct((T, SGU_WIDTH), BF16),
        compiler_params=_params(("parallel",)),
    )(proj, gain.reshape(1, SGU_WIDTH), w_s, bias_full)


def _sgu_bwd(proj, dy, gain, w_s, bias_full, *, n_seq, seq, name, deps=()):
    T = n_seq * seq
    nc = seq // BLOCK
    n_tiles = SGU_WIDTH // LANES

    def body(suv_ref, dy_ref, g_ref, w_ref, b_ref, dsuv_ref, dg_ref, dw_ref, db_ref, dg_acc, dw_acc, db_acc):
        lo = _lo_mask((BLOCK, LANES))
        hi = jnp.logical_not(lo)
        wm = _sgu_weights(w_ref)
        wmt = [jnp.where(lax.broadcasted_iota(jnp.int32, (BLOCK, BLOCK), 1) >= lax.broadcasted_iota(jnp.int32, (BLOCK, BLOCK), 0),
                         w_ref[g].T, 0.0).astype(BF16) for g in range(SGU_GROUPS)]
        gain_v = g_ref[...]

        @pl.when(pl.program_id(0) == 0)
        def _():
            dg_acc[...] = jnp.zeros_like(dg_acc)
            dw_acc[...] = jnp.zeros_like(dw_acc)
            db_acc[...] = jnp.zeros_like(db_acc)

        def chunk(c, carry):
            r0 = pl.multiple_of(c * BLOCK, BLOCK)
            gv, dgelu_v = _gelu_and_grad(suv_ref[pl.ds(r0, BLOCK), SGU_WIDTH:2 * SGU_WIDTH])
            r = lax.rsqrt(jnp.mean(gv * gv, axis=-1, keepdims=True) + NORM_EPS)
            vh = gv * r
            vn = (vh * gain_v).astype(BF16)
            dvn_tiles = []
            for p in range(n_tiles):
                cols = slice(p * LANES, (p + 1) * LANES)
                vp = vn[:, cols]
                mixed = jnp.where(lo, _dot_nn(wm[2 * p], vp), _dot_nn(wm[2 * p + 1], vp)) + b_ref[:, cols]
                u, dgelu_u = _gelu_and_grad(suv_ref[pl.ds(r0, BLOCK), cols])
                dyv = dy_ref[pl.ds(r0, BLOCK), cols]
                dsuv_ref[pl.ds(r0, BLOCK), cols] = (dyv * mixed * dgelu_u).astype(BF16)
                dm = dyv * u
                db_acc[:, cols] += dm
                dm_bf = dm.astype(BF16)
                dvn_tiles.append(jnp.where(lo, _dot_nn(wmt[2 * p], dm_bf), _dot_nn(wmt[2 * p + 1], dm_bf)))
                dw_acc[2 * p] += _dot_nt(jnp.where(lo, dm, 0.0).astype(BF16), vp)
                dw_acc[2 * p + 1] += _dot_nt(jnp.where(hi, dm, 0.0).astype(BF16), vp)
            dvn = jnp.concatenate(dvn_tiles, axis=1)
            dg_acc[...] += dvn * vh
            dvh = dvn * gain_v
            dgv = r * (dvh - vh * jnp.mean(dvh * vh, axis=-1, keepdims=True))
            dsuv_ref[pl.ds(r0, BLOCK), SGU_WIDTH:2 * SGU_WIDTH] = (dgv * dgelu_v).astype(BF16)
            return carry

        lax.fori_loop(0, nc, chunk, 0)

        @pl.when(pl.program_id(0) == n_seq - 1)
        def _():
            dg_ref[...] = jnp.sum(dg_acc[...], axis=0, keepdims=True)
            r = lax.broadcasted_iota(jnp.int32, (BLOCK, BLOCK), 0)
            c = lax.broadcasted_iota(jnp.int32, (BLOCK, BLOCK), 1)
            for g in range(SGU_GROUPS):
                dw_ref[g] = jnp.where(r >= c, dw_acc[g], 0.0)
            lane = lax.broadcasted_iota(jnp.int32, (BLOCK, LANES), 1)
            out = jnp.zeros((BLOCK, LANES), F32)
            for p in range(n_tiles):
                tile = db_acc[:, p * LANES:(p + 1) * LANES]
                s_lo = jnp.sum(jnp.where(lo, tile, 0.0), axis=-1, keepdims=True)
                s_hi = jnp.sum(jnp.where(hi, tile, 0.0), axis=-1, keepdims=True)
                out = jnp.where(lane == 2 * p, s_lo, out)
                out = jnp.where(lane == 2 * p + 1, s_hi, out)
            db_ref[...] = out

    body, dep_specs, dep_args = _with_deps(body, 5, deps)
    dsuv, dg, dw, db = pl.pallas_call(
        body, name=name, grid=(n_seq,),
        in_specs=[pl.BlockSpec((seq, 2 * SGU_WIDTH), lambda b: (b, COL_SUV // (2 * SGU_WIDTH))),
                  pl.BlockSpec((seq, SGU_WIDTH), lambda b: (b, 0)),
                  pl.BlockSpec((1, SGU_WIDTH), lambda b: (0, 0)),
                  pl.BlockSpec((SGU_GROUPS, BLOCK, BLOCK), lambda b: (0, 0, 0)),
                  pl.BlockSpec((BLOCK, SGU_WIDTH), lambda b: (0, 0))] + dep_specs,
        out_specs=[pl.BlockSpec((seq, 2 * SGU_WIDTH), lambda b: (b, 0)),
                   pl.BlockSpec((1, SGU_WIDTH), lambda b: (0, 0)),
                   pl.BlockSpec((SGU_GROUPS, BLOCK, BLOCK), lambda b: (0, 0, 0)),
                   pl.BlockSpec((BLOCK, LANES), lambda b: (0, 0))],
        out_shape=[jax.ShapeDtypeStruct((T, 2 * SGU_WIDTH), BF16), jax.ShapeDtypeStruct((1, SGU_WIDTH), F32),
                   jax.ShapeDtypeStruct((SGU_GROUPS, BLOCK, BLOCK), F32), jax.ShapeDtypeStruct((BLOCK, LANES), F32)],
        scratch_shapes=[pltpu.VMEM((BLOCK, SGU_WIDTH), F32), pltpu.VMEM((SGU_GROUPS, BLOCK, BLOCK), F32),
                        pltpu.VMEM((BLOCK, SGU_WIDTH), F32)],
        compiler_params=_params(("arbitrary",)),
    )(proj, dy, gain.reshape(1, SGU_WIDTH), w_s, bias_full, *dep_args)
    return dsuv, dg.reshape(SGU_WIDTH), dw, db[:, :SGU_GROUPS].T


def _merge_fwd(y_att, y_sgu, w_oa, w_ob, proj, *, name, tm=1024, tn=512, deps=()):
    T = y_att.shape[0]

    def body(ya_ref, ys_ref, wa_ref, wb_ref, ga_ref, gb_ref, o_ref):
        pa = _dot_nn(ya_ref[...], wa_ref[...])
        pb = _dot_nn(ys_ref[...], wb_ref[...])
        o_ref[...] = (_sigmoid(ga_ref[...]) * pa + _sigmoid(gb_ref[...]) * pb).astype(BF16)

    act = pl.BlockSpec((tm, ATT_WIDTH), lambda i, j: (i, 0))
    wgt = pl.BlockSpec((ATT_WIDTH, tn), lambda i, j: (0, j))
    body, dep_specs, dep_args = _with_deps(body, 6, deps)
    return pl.pallas_call(
        body, name=name, grid=(T // tm, D_MODEL // tn),
        in_specs=[act, act, wgt, wgt,
                  pl.BlockSpec((tm, tn), lambda i, j: (i, j + COL_GA // tn)),
                  pl.BlockSpec((tm, tn), lambda i, j: (i, j + COL_GB // tn))] + dep_specs,
        out_specs=pl.BlockSpec((tm, tn), lambda i, j: (i, j)),
        out_shape=jax.ShapeDtypeStruct((T, D_MODEL), BF16),
        compiler_params=_params(("parallel", "parallel")),
    )(y_att, y_sgu, w_oa, w_ob, proj, proj, *dep_args)


def _merge_bwd(dx1_bf, w_out, y_att, y_sgu, w_oa, w_ob, proj, *, name, tm=1024, tn=512):
    T = y_att.shape[0]

    def body(dx_ref, wo_ref, ya_ref, ys_ref, wa_ref, wb_ref, ga_ref, gb_ref, dpa_ref, dpb_ref, dga_ref, dgb_ref):
        dm = _dot_nt(dx_ref[...], wo_ref[...])
        pa = _dot_nn(ya_ref[...], wa_ref[...])
        pb = _dot_nn(ys_ref[...], wb_ref[...])
        sa = _sigmoid(ga_ref[...])
        sb = _sigmoid(gb_ref[...])
        dpa_ref[...] = (dm * sa).astype(BF16)
        dpb_ref[...] = (dm * sb).astype(BF16)
        dga_ref[...] = (dm * pa * sa * (1.0 - sa)).astype(BF16)
        dgb_ref[...] = (dm * pb * sb * (1.0 - sb)).astype(BF16)

    act = pl.BlockSpec((tm, ATT_WIDTH), lambda i, j: (i, 0))
    wgt = pl.BlockSpec((ATT_WIDTH, tn), lambda i, j: (0, j))
    out = pl.BlockSpec((tm, tn), lambda i, j: (i, j))
    return pl.pallas_call(
        body, name=name, grid=(T // tm, D_MODEL // tn),
        in_specs=[pl.BlockSpec((tm, D_MODEL), lambda i, j: (i, 0)),
                  pl.BlockSpec((tn, D_MODEL), lambda i, j: (j, 0)),
                  act, act, wgt, wgt,
                  pl.BlockSpec((tm, tn), lambda i, j: (i, j + COL_GA // tn)),
                  pl.BlockSpec((tm, tn), lambda i, j: (i, j + COL_GB // tn))],
        out_specs=[out] * 4,
        out_shape=[jax.ShapeDtypeStruct((T, D_MODEL), BF16)] * 4,
        compiler_params=_params(("parallel", "parallel")),
    )(dx1_bf, w_out, y_att, y_sgu, w_oa, w_ob, proj, proj)


CONV_ROWS = 256
CONV_TN = 256


def _shift_rows(cur, prev8, k):
    rolled = pltpu.roll(cur, k, axis=0)
    head = jnp.where(lax.broadcasted_iota(jnp.int32, prev8.shape, 0) < k, pltpu.roll(prev8, k, axis=0), rolled[:SUBLANES])
    return jnp.concatenate([head, rolled[SUBLANES:]], axis=0)


def _shift_rows_up(cur, next8, k):
    n = cur.shape[0]
    rolled = pltpu.roll(cur, n - k, axis=0)
    tail = jnp.where(lax.broadcasted_iota(jnp.int32, next8.shape, 0) >= SUBLANES - k,
                     pltpu.roll(next8, SUBLANES - k, axis=0), rolled[n - SUBLANES:])
    return jnp.concatenate([rolled[:n - SUBLANES], tail], axis=0)


def _conv_rows(z_ref, r0, first, w_ref, b_ref, rows):
    cur = z_ref[pl.ds(r0, rows), :]
    rp = pl.multiple_of(jnp.maximum(r0 - SUBLANES, 0), SUBLANES)
    prev8 = jnp.where(first, 0.0, z_ref[pl.ds(rp, SUBLANES), :])
    z1 = _shift_rows(cur, prev8, 1)
    z2 = _shift_rows(cur, prev8, 2)
    return b_ref[...] + w_ref[0:1, :] * z2 + w_ref[1:2, :] * z1 + w_ref[2:3, :] * cur


def _conv_fwd(z_g, z_v, cw_g, cw_v, cb_g, cb_v, *, n_seq, seq, name):
    T = n_seq * seq
    tn, rows = CONV_TN, CONV_ROWS

    def body(zg_ref, zv_ref, wg_ref, wv_ref, bg_ref, bv_ref, a_ref):
        def step(s, carry):
            r0 = pl.multiple_of(s * rows, rows)
            first = s == 0
            g = _conv_rows(zg_ref, r0, first, wg_ref, bg_ref, rows)
            v = _conv_rows(zv_ref, r0, first, wv_ref, bv_ref, rows)
            a_ref[pl.ds(r0, rows), :] = (g * _sigmoid(g) * v).astype(BF16)
            return carry

        lax.fori_loop(0, seq // rows, step, 0)

    zs = pl.BlockSpec((seq, tn), lambda b, j: (b, j))
    ws = pl.BlockSpec((3, tn), lambda b, j: (0, j))
    bs = pl.BlockSpec((1, tn), lambda b, j: (0, j))
    return pl.pallas_call(
        body, name=name, grid=(n_seq, D_FF // tn),
        in_specs=[zs, zs, ws, ws, bs, bs], out_specs=zs,
        out_shape=jax.ShapeDtypeStruct((T, D_FF), BF16),
        compiler_params=_params(("parallel", "parallel")),
    )(z_g, z_v, cw_g, cw_v, cb_g.reshape(1, D_FF), cb_v.reshape(1, D_FF))


def _conv_bwd(z_g, z_v, da, cw_g, cw_v, cb_g, cb_v, *, n_seq, seq, name):
    T = n_seq * seq
    tn, rows = CONV_TN, CONV_ROWS
    n_steps = seq // rows

    def body(zg_ref, zv_ref, da_ref, wg_ref, wv_ref, bg_ref, bv_ref,
             dzg_ref, dzv_ref, dwg_ref, dwv_ref, dbg_ref, dbv_ref, dcg_ref, dcv_ref):
        def grads(s, accs):
            r0 = pl.multiple_of(s * rows, rows)
            first = s == 0
            cur_g = zg_ref[pl.ds(r0, rows), :]
            cur_v = zv_ref[pl.ds(r0, rows), :]
            rp = pl.multiple_of(jnp.maximum(r0 - SUBLANES, 0), SUBLANES)
            pg = jnp.where(first, 0.0, zg_ref[pl.ds(rp, SUBLANES), :])
            pv = jnp.where(first, 0.0, zv_ref[pl.ds(rp, SUBLANES), :])
            g1, g2 = _shift_rows(cur_g, pg, 1), _shift_rows(cur_g, pg, 2)
            v1, v2 = _shift_rows(cur_v, pv, 1), _shift_rows(cur_v, pv, 2)
            g = bg_ref[...] + wg_ref[0:1, :] * g2 + wg_ref[1:2, :] * g1 + wg_ref[2:3, :] * cur_g
            v = bv_ref[...] + wv_ref[0:1, :] * v2 + wv_ref[1:2, :] * v1 + wv_ref[2:3, :] * cur_v
            sg = _sigmoid(g)
            dav = da_ref[pl.ds(r0, rows), :]
            dcg = dav * v * (sg * (1.0 + g * (1.0 - sg)))
            dcv = dav * (g * sg)
            dcg_ref[pl.ds(r0, rows), :] = dcg
            dcv_ref[pl.ds(r0, rows), :] = dcv

            def colsum(x):
                return jnp.sum(x, axis=0, keepdims=True)

            return (accs[0] + colsum(dcg * g2), accs[1] + colsum(dcg * g1), accs[2] + colsum(dcg * cur_g), accs[3] + colsum(dcg),
                    accs[4] + colsum(dcv * v2), accs[5] + colsum(dcv * v1), accs[6] + colsum(dcv * cur_v), accs[7] + colsum(dcv))

        zero = jnp.zeros((1, tn), F32)
        sums = lax.fori_loop(0, n_steps, grads, (zero,) * 8)
        first_seq = pl.program_id(1) == 0

        @pl.when(first_seq)
        def _():
            dwg_ref[...] = jnp.concatenate(sums[0:3], axis=0)
            dbg_ref[...] = sums[3]
            dwv_ref[...] = jnp.concatenate(sums[4:7], axis=0)
            dbv_ref[...] = sums[7]

        @pl.when(jnp.logical_not(first_seq))
        def _():
            dwg_ref[...] += jnp.concatenate(sums[0:3], axis=0)
            dbg_ref[...] += sums[3]
            dwv_ref[...] += jnp.concatenate(sums[4:7], axis=0)
            dbv_ref[...] += sums[7]

        def back(s, carry):
            r0 = pl.multiple_of(s * rows, rows)
            last = s == n_steps - 1
            rn = pl.multiple_of(jnp.minimum(r0 + rows, seq - SUBLANES), SUBLANES)
            for dc_ref, w_ref, dz_ref in ((dcg_ref, wg_ref, dzg_ref), (dcv_ref, wv_ref, dzv_ref)):
                cur = dc_ref[pl.ds(r0, rows), :]
                nxt = jnp.where(last, 0.0, dc_ref[pl.ds(rn, SUBLANES), :])
                u1, u2 = _shift_rows_up(cur, nxt, 1), _shift_rows_up(cur, nxt, 2)
                dz_ref[pl.ds(r0, rows), :] = (w_ref[2:3, :] * cur + w_ref[1:2, :] * u1 + w_ref[0:1, :] * u2).astype(BF16)
            return carry

        lax.fori_loop(0, n_steps, back, 0)

    zs = pl.BlockSpec((seq, tn), lambda j, b: (b, j))
    ws = pl.BlockSpec((3, tn), lambda j, b: (0, j))
    bs = pl.BlockSpec((1, tn), lambda j, b: (0, j))
    outs = pl.pallas_call(
        body, name=name, grid=(D_FF // tn, n_seq),
        in_specs=[zs, zs, zs, ws, ws, bs, bs],
        out_specs=[zs, zs, ws, ws, bs, bs],
        out_shape=[jax.ShapeDtypeStruct((T, D_FF), BF16)] * 2 + [jax.ShapeDtypeStruct((3, D_FF), F32)] * 2
        + [jax.ShapeDtypeStruct((1, D_FF), F32)] * 2,
        scratch_shapes=[pltpu.VMEM((seq, tn), F32), pltpu.VMEM((seq, tn), F32)],
        compiler_params=_params(("parallel", "arbitrary")),
    )(z_g, z_v, da, cw_g, cw_v, cb_g.reshape(1, D_FF), cb_v.reshape(1, D_FF))
    dz_g, dz_v, dw_g, dw_v, db_g, db_v = outs
    return dz_g, dz_v, dw_g, dw_v, db_g.reshape(D_FF), db_v.reshape(D_FF)


def _loss_head(y, target, *, name, tm=512):
    T, D = y.shape

    def body(y_ref, t_ref, dy_ref, dyb_ref, l_ref):
        err = y_ref[...] - t_ref[...]
        dyv = err * (1.0 / D)
        dy_ref[...] = dyv
        dyb_ref[...] = dyv.astype(BF16)
        part = jnp.sum(jnp.sum(err * err, axis=0, keepdims=True), axis=1, keepdims=True) * (0.5 / D)

        @pl.when(pl.program_id(0) == 0)
        def _():
            l_ref[...] = jnp.broadcast_to(part, l_ref.shape)

        @pl.when(pl.program_id(0) > 0)
        def _():
            l_ref[...] += jnp.broadcast_to(part, l_ref.shape)

    row = pl.BlockSpec((tm, D), lambda i: (i, 0))
    dy, dyb, l = pl.pallas_call(
        body, name=name, grid=(T // tm,),
        in_specs=[row, row],
        out_specs=[row, row, pl.BlockSpec((SUBLANES, LANES), lambda i: (0, 0))],
        out_shape=[jax.ShapeDtypeStruct((T, D), F32), jax.ShapeDtypeStruct((T, D), BF16),
                   jax.ShapeDtypeStruct((SUBLANES, LANES), F32)],
        compiler_params=_params(("arbitrary",)),
    )(y, target)
    return l[0, 0], dy, dyb


def _cast_bf16(x, *, name, tm=512):
    T, D = x.shape

    def body(x_ref, o_ref):
        o_ref[...] = x_ref[...].astype(BF16)

    row = pl.BlockSpec((tm, D), lambda i: (i, 0))
    return pl.pallas_call(body, name=name, grid=(T // tm,), in_specs=[row], out_specs=row,
                          out_shape=jax.ShapeDtypeStruct((T, D), BF16), compiler_params=_params(("parallel",)))(x)


def _layer_fwd(x, w, sched, *, n_seq, seq, l):
    tag = f"l{l}"
    deps = sched("fwd_start", l, x)
    h = _rms_fwd(x, w["mix_norm"], name=f"{tag}_mix_norm")
    proj = _mm(h, w["w_in"], mode="nn", out_dtype=F32, name=f"{tag}_proj", deps=deps)
    y_att = _attention_fwd(proj, w["q_norm"], w["k_norm"], w["sinks"], n_seq=n_seq, seq=seq, name=f"{tag}_att")
    deps = sched("fwd_att", l, y_att)
    y_sgu = _sgu_fwd(proj, w["sgu_norm"], w["w_s"], w["bias_full"], n_seq=n_seq, seq=seq, name=f"{tag}_sgu")
    merged = _merge_fwd(y_att, y_sgu, w["w_oa"], w["w_ob"], proj, name=f"{tag}_merge", deps=deps)
    x1 = _mm(merged, w["w_out"], mode="nn", out_dtype=F32, name=f"{tag}_out",
             epilogue=_add, extras=(x,))
    deps = sched("fwd_mixer_done", l, x1)
    h2 = _rms_fwd(x1, w["ffn_norm"], name=f"{tag}_ffn_norm")
    z_g = _mm(h2, w["w_up_g"], mode="nn", out_dtype=F32, name=f"{tag}_up_g", deps=deps)
    z_v = _mm(h2, w["w_up_v"], mode="nn", out_dtype=F32, name=f"{tag}_up_v")
    a = _conv_fwd(z_g, z_v, w["cw_g"], w["cw_v"], w["cb_g"], w["cb_v"], n_seq=n_seq, seq=seq, name=f"{tag}_conv")
    deps = sched("fwd_conv", l, a)
    x2 = _mm(a, w["w_down"], mode="nn", out_dtype=F32, name=f"{tag}_down",
             epilogue=_add, extras=(x1,), deps=deps)
    saved = dict(x=x, h=h, proj=proj, y_att=y_att, y_sgu=y_sgu, merged=merged, x1=x1, h2=h2, z_g=z_g, z_v=z_v, a=a)
    return x2, saved


def _layer_bwd(dx2, dx2_bf, w, s, sched, *, n_seq, seq, l):
    tag = f"l{l}b"
    g = {}
    da = _mm(dx2_bf, w["w_down"], mode="nt", out_dtype=F32, name=f"{tag}_da")
    g["w_down"] = _mm(s["a"], dx2_bf, mode="tn", out_dtype=F32, name=f"{tag}_dw_down")
    dz_g, dz_v, g["cw_g"], g["cw_v"], g["cb_g"], g["cb_v"] = _conv_bwd(
        s["z_g"], s["z_v"], da, w["cw_g"], w["cw_v"], w["cb_g"], w["cb_v"], n_seq=n_seq, seq=seq, name=f"{tag}_conv")
    dh2 = _mm(dz_g, w["w_up_g"], mode="nt", out_dtype=F32, name=f"{tag}_dh2_g")
    dh2 = _mm(dz_v, w["w_up_v"], mode="nt", out_dtype=F32, name=f"{tag}_dh2_v",
              epilogue=_add, extras=(dh2,))
    g["w_up_g"] = _mm(s["h2"], dz_g, mode="tn", out_dtype=F32, name=f"{tag}_dw_up_g")
    g["w_up_v"] = _mm(s["h2"], dz_v, mode="tn", out_dtype=F32, name=f"{tag}_dw_up_v")
    deps = sched("bwd_ffn_grads", l, dh2, g)
    dx1, g["ffn_norm"] = _rms_bwd(s["x1"], w["ffn_norm"], dh2, dx2, name=f"{tag}_ffn_norm", deps=deps)
    dx1_bf = _cast_bf16(dx1, name=f"{tag}_dx1_bf")
    dpa, dpb, dga, dgb = _merge_bwd(dx1_bf, w["w_out"], s["y_att"], s["y_sgu"], w["w_oa"], w["w_ob"], s["proj"],
                                    name=f"{tag}_merge")
    deps = sched("bwd_merge", l, dpa)
    g["w_out"] = _mm(s["merged"], dx1_bf, mode="tn", out_dtype=F32, name=f"{tag}_dw_out",
                     deps=deps)
    dy_att = _mm(dpa, w["w_oa"], mode="nt", out_dtype=BF16, name=f"{tag}_dy_att")
    dy_sgu = _mm(dpb, w["w_ob"], mode="nt", out_dtype=F32, name=f"{tag}_dy_sgu")
    g["w_oa"] = _mm(s["y_att"], dpa, mode="tn", out_dtype=F32, name=f"{tag}_dw_oa")
    g["w_ob"] = _mm(s["y_sgu"], dpb, mode="tn", out_dtype=F32, name=f"{tag}_dw_ob")
    deps = sched("bwd_out_grads", l, dy_att, g)
    dqkv, g["q_norm"], g["k_norm"], g["sinks"] = _attention_bwd(
        s["proj"], dy_att, w["q_norm"], w["k_norm"], w["sinks"], n_seq=n_seq, seq=seq, name=f"{tag}_att", deps=deps)
    deps = sched("bwd_att", l, dqkv)
    dsuv, g["sgu_norm"], g["w_s"], g["b_s"] = _sgu_bwd(
        s["proj"], dy_sgu, w["sgu_norm"], w["w_s"], w["bias_full"], n_seq=n_seq, seq=seq, name=f"{tag}_sgu", deps=deps)
    dproj = jnp.concatenate([dsuv, dga, dgb, dqkv], axis=1)
    g["w_in"] = _mm(s["h"], dproj, mode="tn", out_dtype=F32, name=f"{tag}_dw_in")
    deps = sched("bwd_w_in_grad", l, dproj, g)
    dh = _mm(dproj, w["w_in"], mode="nt", out_dtype=F32, name=f"{tag}_dh", deps=deps)
    deps = sched("bwd_dh", l, dh)
    dx, g["mix_norm"] = _rms_bwd(s["x"], w["mix_norm"], dh, dx1, name=f"{tag}_mix_norm", deps=deps)
    return dx, g


def _local_step(x, target, weights, sched, *, n_seq, seq):
    depth = len(weights)
    saved = []
    h = x
    for l in range(depth):
        h, s = _layer_fwd(h, weights[l], sched, n_seq=n_seq, seq=seq, l=l)
        saved.append(s)
    loss, dy, dy_bf = _loss_head(h, target, name="loss_head")
    grads = [None] * depth
    for l in reversed(range(depth)):
        if l < depth - 1:
            dy_bf = _cast_bf16(dy, name=f"l{l}b_dx2_bf")
        dy, grads[l] = _layer_bwd(dy, dy_bf, weights[l], saved[l], sched, n_seq=n_seq, seq=seq, l=l)
    return loss, dy, grads


W_IN_SHARD = IN_WIDTH // N_DEV
W_UP_SHARD = 2 * D_FF // N_DEV
COL_MOVE_ROWS = 256


def _w_in_moves():
    moves = []
    for j in range(N_DEV):
        a, b = j * W_IN_SHARD, (j + 1) * W_IN_SHARD
        if a < QKV_WIDTH:
            moves.append((j, 0, min(b, QKV_WIDTH) - a, 0, a + REST_WIDTH))
        if b > QKV_WIDTH:
            lo = max(a, QKV_WIDTH)
            moves.append((j, lo - a, b - a, 0, lo - QKV_WIDTH))
    return tuple(moves)


def _w_up_moves():
    half = N_DEV // 2
    return tuple((j, 0, W_UP_SHARD, j // half, (j % half) * W_UP_SHARD) for j in range(N_DEV))


def _w_o_moves():
    return tuple((j, 0, LANES, 0, j * LANES) for j in range(N_DEV))


def _assemble(blocks, widths, moves, *, name):
    _, R, w = blocks.shape
    tr = min(R, COL_MOVE_ROWS)

    def body(b_ref, *o_refs):
        for j, lo, hi, which, at in moves:
            o_refs[which][:, at:at + hi - lo] = b_ref[j, :, lo:hi]

    return pl.pallas_call(
        body, name=name, grid=(R // tr,),
        in_specs=[pl.BlockSpec((N_DEV, tr, w), lambda i: (0, i, 0))],
        out_specs=[pl.BlockSpec((tr, n), lambda i: (i, 0)) for n in widths],
        out_shape=[jax.ShapeDtypeStruct((R, n), blocks.dtype) for n in widths],
        compiler_params=_params(("parallel",)),
    )(blocks)


def _disassemble(mats, w, moves, *, name):
    R = mats[0].shape[0]
    tr = min(R, COL_MOVE_ROWS)
    n = len(mats)

    def body(*refs):
        m_refs, o_ref = refs[:n], refs[n]
        for j, lo, hi, which, at in moves:
            o_ref[j, :, lo:hi] = m_refs[which][:, at:at + hi - lo]

    return pl.pallas_call(
        body, name=name, grid=(R // tr,),
        in_specs=[pl.BlockSpec((tr, m.shape[1]), lambda i: (i, 0)) for m in mats],
        out_specs=pl.BlockSpec((N_DEV, tr, w), lambda i: (0, i, 0)),
        out_shape=jax.ShapeDtypeStruct((N_DEV, R, w), mats[0].dtype),
        compiler_params=_params(("parallel",)),
    )(*mats)


def _my_place():
    return lax.axis_index("x"), lax.axis_index("y"), lax.axis_index("c")


def _gathered_shape(shape, kind):
    r, c = shape
    return {"blocks": (N_DEV, r, c), "rows": (N_DEV * r, c), "cols": (r, N_DEV * c)}[kind]


def _gather_window(ref, kind, shape, j):
    r, c = shape
    if kind == "blocks":
        return ref.at[j]
    if kind == "rows":
        return ref.at[pl.ds(pl.multiple_of(j * r, r), r), :]
    return ref.at[:, pl.ds(pl.multiple_of(j * c, c), c)]


def _gather(srcs, kinds, *, name):
    n = len(srcs)
    shapes = [s.shape for s in srcs]
    per = 7

    def body(*refs):
        src_refs, dst_refs = refs[:n], refs[n:2 * n]
        send_sems, recv_sems, local_sems = refs[2 * n:]
        x, y, c = _my_place()
        me, sibling = (x, y, c), (x, y, 1 - c)
        chips = [(1 - x, y), (x, 1 - y), (1 - x, 1 - y)]

        def at(i, px, py, pc):
            return _gather_window(dst_refs[i], kinds[i], shapes[i], 4 * px + 2 * py + pc)

        def copy(i, k, block, to, src=None):
            return pltpu.make_async_remote_copy(
                src_ref=at(i, *block) if src is None else src, dst_ref=at(i, *block),
                send_sem=send_sems.at[per * i + k], recv_sem=recv_sems.at[per * i + k], device_id=to, device_id_type=MESH)

        mine = [pltpu.make_async_copy(src_refs[i], at(i, *me), local_sems.at[i]) for i in range(n)]
        for cp in mine:
            cp.start()
        started = []
        for i in range(n):
            first = [copy(i, 0, me, sibling, src=src_refs[i])]
            first += [copy(i, 1 + j, me, (*chip, c), src=src_refs[i]) for j, chip in enumerate(chips)]
            for cp in first:
                cp.start()
            started += first
        for i in range(n):
            for j, chip in enumerate(chips):
                copy(i, 1 + j, (*chip, c), me).wait_recv()
                fwd = copy(i, 4 + j, (*chip, c), sibling)
                fwd.start()
                started.append(fwd)
        for i in range(n):
            copy(i, 0, sibling, me).wait_recv()
            for j, chip in enumerate(chips):
                copy(i, 4 + j, (*chip, 1 - c), me).wait_recv()
        for cp in started:
            cp.wait_send()
        for cp in mine:
            cp.wait()

    return pl.pallas_call(
        body, name=name,
        out_shape=[jax.ShapeDtypeStruct(_gathered_shape(s.shape, k), s.dtype) for s, k in zip(srcs, kinds)],
        in_specs=[ANY] * n, out_specs=[ANY] * n,
        scratch_shapes=[pltpu.SemaphoreType.DMA((per * n,)), pltpu.SemaphoreType.DMA((per * n,)),
                        pltpu.SemaphoreType.DMA((n,))],
    )(*srcs)


HBM = pl.BlockSpec(memory_space=pltpu.HBM)
SEM = pl.BlockSpec(memory_space=pltpu.SEMAPHORE)
TOKEN = jax.ShapeDtypeStruct((SUBLANES, LANES), F32)
TOKEN_SPEC = pl.BlockSpec(memory_space=pltpu.VMEM)
SPLIT_PARAMS = pltpu.CompilerParams(has_side_effects=pltpu.SideEffectType.DATAFLOW_SIDE_EFFECTING)


def _in_hbm(x):
    return pltpu.with_memory_space_constraint(x, pltpu.HBM)


def _hbm_like(shape, dtype):
    return pltpu.HBM(shape, dtype)


def _place_own(shards, kinds, dtypes, *, name):
    n = len(shards)
    shapes = [s.shape for s in shards]

    def body(*refs):
        s_refs, land_refs, bufs, sems = refs[:n], refs[n:2 * n], refs[2 * n:3 * n], refs[3 * n]
        x, y, c = _my_place()
        copies = []
        for i in range(n):
            bufs[i][...] = s_refs[i][...].astype(dtypes[i])
            copies.append(pltpu.make_async_copy(
                bufs[i], _gather_window(land_refs[i], kinds[i], shapes[i], 4 * x + 2 * y + c), sems.at[i]))
        for cp in copies:
            cp.start()
        for cp in copies:
            cp.wait()

    return pl.pallas_call(
        body, name=name,
        out_shape=[jax.ShapeDtypeStruct(_gathered_shape(s, k), d) for s, k, d in zip(shapes, kinds, dtypes)],
        in_specs=[pl.BlockSpec(memory_space=pltpu.VMEM)] * n, out_specs=[ANY] * n,
        scratch_shapes=[pltpu.VMEM(s, d) for s, d in zip(shapes, dtypes)] + [pltpu.SemaphoreType.DMA((n,))],
        compiler_params=_params(),
    )(*shards)


def _gather_start(lands, kinds, shapes, after=(), *, name):
    n = len(lands)
    n_after = len(after)

    def body(*refs):
        land_refs = refs[:n]
        send_sems, recv_sems = refs[n + n_after], refs[n + n_after + 1]
        x, y, c = _my_place()
        targets = [(x, y, 1 - c), (1 - x, y, c), (x, 1 - y, c), (1 - x, 1 - y, c)]
        for i in range(n):
            own = _gather_window(land_refs[i], kinds[i], shapes[i], 4 * x + 2 * y + c)
            for k, to in enumerate(targets):
                pltpu.make_async_remote_copy(
                    src_ref=own, dst_ref=own, send_sem=send_sems.at[4 * i + k], recv_sem=recv_sems.at[4 * i + k],
                    device_id=to, device_id_type=MESH).start()
        refs[-1][...] = jnp.zeros_like(refs[-1])

    outs = pl.pallas_call(
        body, name=name,
        out_shape=[pltpu.SemaphoreType.DMA((4 * n,)), pltpu.SemaphoreType.DMA((4 * n,))]
        + [_hbm_like(a.shape, a.dtype) for a in lands] + [TOKEN],
        in_specs=[HBM] * n + [ANY] * n_after, out_specs=[SEM, SEM] + [HBM] * n + [TOKEN_SPEC],
        input_output_aliases={i: 2 + i for i in range(n)},
        compiler_params=SPLIT_PARAMS,
    )(*[_in_hbm(a) for a in lands], *after)
    return outs[0], outs[1], outs[2:2 + n], outs[-1]


def _gather_forward(recv_sems, lands, kinds, shapes, after, *, name):
    n = len(lands)

    def body(*refs):
        recv_ref, land_refs = refs[0], refs[1:1 + n]
        fwd_send, fwd_recv = refs[2 + n], refs[3 + n]
        token = refs[-1]
        x, y, c = _my_place()
        chips = [(1 - x, y), (x, 1 - y), (1 - x, 1 - y)]
        for i in range(n):
            for j, (px, py) in enumerate(chips):
                block = _gather_window(land_refs[i], kinds[i], shapes[i], 4 * px + 2 * py + c)
                pltpu.make_async_remote_copy(
                    src_ref=block, dst_ref=block, send_sem=fwd_send.at[3 * i + j], recv_sem=recv_ref.at[4 * i + 1 + j],
                    device_id=(px, py, c), device_id_type=MESH).wait_recv()
                pltpu.make_async_remote_copy(
                    src_ref=block, dst_ref=block, send_sem=fwd_send.at[3 * i + j], recv_sem=fwd_recv.at[3 * i + j],
                    device_id=(x, y, 1 - c), device_id_type=MESH).start()
        token[...] = jnp.zeros_like(token)

    outs = pl.pallas_call(
        body, name=name,
        out_shape=[pltpu.SemaphoreType.DMA((3 * n,)), pltpu.SemaphoreType.DMA((3 * n,))]
        + [_hbm_like(a.shape, a.dtype) for a in lands] + [TOKEN],
        in_specs=[SEM] + [HBM] * n + [ANY], out_specs=[SEM, SEM] + [HBM] * n + [TOKEN_SPEC],
        input_output_aliases={1 + i: 2 + i for i in range(n)},
        compiler_params=SPLIT_PARAMS,
    )(recv_sems, *lands, after)
    return outs[0], outs[1], outs[2:2 + n], outs[-1]


def _gather_finish(send_sems, recv_sems, fwd_send, fwd_recv, lands, kinds, shapes, after, *, name):
    n = len(lands)

    def body(*refs):
        send_ref, recv_ref, fsend_ref, frecv_ref = refs[:4]
        land_refs = refs[4:4 + n]
        x, y, c = _my_place()
        chips = [(1 - x, y), (x, 1 - y), (1 - x, 1 - y)]
        sibling = (x, y, 1 - c)
        for i in range(n):
            def window(j):
                return _gather_window(land_refs[i], kinds[i], shapes[i], j)

            mine, theirs = window(4 * x + 2 * y + c), window(4 * x + 2 * y + (1 - c))
            pltpu.make_async_remote_copy(src_ref=mine, dst_ref=theirs, send_sem=send_ref.at[4 * i],
                                         recv_sem=recv_ref.at[4 * i], device_id=sibling, device_id_type=MESH).wait_recv()
            for j, (px, py) in enumerate(chips):
                block = window(4 * px + 2 * py + (1 - c))
                pltpu.make_async_remote_copy(src_ref=block, dst_ref=block, send_sem=fsend_ref.at[3 * i + j],
                                             recv_sem=frecv_ref.at[3 * i + j], device_id=sibling,
                                             device_id_type=MESH).wait_recv()
            for k in range(4):
                pltpu.make_async_remote_copy(src_ref=mine, dst_ref=mine, send_sem=send_ref.at[4 * i + k],
                                             recv_sem=recv_ref.at[4 * i + k], device_id=sibling,
                                             device_id_type=MESH).wait_send()
            for j, (px, py) in enumerate(chips):
                block = window(4 * px + 2 * py + c)
                pltpu.make_async_remote_copy(src_ref=block, dst_ref=block, send_sem=fsend_ref.at[3 * i + j],
                                             recv_sem=frecv_ref.at[3 * i + j], device_id=sibling,
                                             device_id_type=MESH).wait_send()

    return pl.pallas_call(
        body, name=name,
        out_shape=[_hbm_like(a.shape, a.dtype) for a in lands],
        in_specs=[SEM] * 4 + [HBM] * n + [ANY], out_specs=[HBM] * n,
        input_output_aliases={4 + i: i for i in range(n)},
        compiler_params=SPLIT_PARAMS,
    )(send_sems, recv_sems, fwd_send, fwd_recv, *lands, after)


def _pair_plan(src_ref, land_ref, x, y, c):
    return [(src_ref.at[2 * k + (1 - c)], land_ref.at[k], (x, y, 1 - c)) for k in range(N_CHIPS)]


def _chip_plan(src_ref, land_ref, x, y, c):
    chips = [(1 - x, y), (x, 1 - y), (1 - x, 1 - y)]
    return [(src_ref.at[2 * px + py], land_ref.at[k], (px, py, c)) for k, (px, py) in enumerate(chips)]


def _exchange_copies(plan, per, src_refs, land_refs, send_sems, recv_sems):
    x, y, c = _my_place()
    copies = []
    for i, (s_ref, l_ref) in enumerate(zip(src_refs, land_refs)):
        for q, (src, dst, to) in enumerate(plan(s_ref, l_ref, x, y, c)):
            copies.append(pltpu.make_async_remote_copy(
                src_ref=src, dst_ref=dst, send_sem=send_sems.at[per * i + q], recv_sem=recv_sems.at[per * i + q],
                device_id=to, device_id_type=MESH))
    return copies


def _exchange_start(srcs, plan, per, *, name):
    n = len(srcs)

    def body(*refs):
        src_refs, land_refs = refs[:n], refs[n:2 * n]
        send_sems, recv_sems = refs[2 * n], refs[2 * n + 1]
        for cp in _exchange_copies(plan, per, src_refs, land_refs, send_sems, recv_sems):
            cp.start()
        refs[-1][...] = jnp.zeros_like(refs[-1])

    lands = [lax.empty((per,) + s.shape[1:], s.dtype) for s in srcs]
    outs = pl.pallas_call(
        body, name=name,
        out_shape=[pltpu.SemaphoreType.DMA((per * n,)), pltpu.SemaphoreType.DMA((per * n,))]
        + [_hbm_like(s.shape, s.dtype) for s in srcs] + [_hbm_like(a.shape, a.dtype) for a in lands] + [TOKEN],
        in_specs=[HBM] * (2 * n), out_specs=[SEM, SEM] + [HBM] * (2 * n) + [TOKEN_SPEC],
        input_output_aliases={i: 2 + i for i in range(2 * n)},
        compiler_params=SPLIT_PARAMS,
    )(*[_in_hbm(s) for s in srcs], *[_in_hbm(a) for a in lands])
    return outs[0], outs[1], outs[2:2 + n], outs[2 + n:2 + 2 * n], outs[-1]


def _exchange_wait(send_sems, recv_sems, srcs, lands, plan, per, after, *, name):
    n = len(srcs)

    def body(*refs):
        send_ref, recv_ref = refs[0], refs[1]
        src_refs, land_refs = refs[2:2 + n], refs[2 + n:2 + 2 * n]
        copies = _exchange_copies(plan, per, src_refs, land_refs, send_ref, recv_ref)
        for cp in copies:
            cp.wait_recv()
        for cp in copies:
            cp.wait_send()

    outs = pl.pallas_call(
        body, name=name,
        out_shape=[_hbm_like(s.shape, s.dtype) for s in srcs] + [_hbm_like(a.shape, a.dtype) for a in lands],
        in_specs=[SEM, SEM] + [HBM] * (2 * n) + [ANY], out_specs=[HBM] * (2 * n),
        input_output_aliases={2 + i: i for i in range(2 * n)},
        compiler_params=SPLIT_PARAMS,
    )(send_sems, recv_sems, *srcs, *lands, after)
    return outs[:n], outs[n:]


REDUCE_BLOCK_BYTES = 1 << 20


def _row_tile(r, c):
    row_bytes = 4 * (-(-c // LANES) * LANES)
    best = r
    for d in range(SUBLANES, r, SUBLANES):
        if r % d == 0 and d * row_bytes <= REDUCE_BLOCK_BYTES:
            best = d
    return best if r * row_bytes > REDUCE_BLOCK_BYTES else r


def _reduce_pair_sum(blocked, recv, place, wire_dtype, *, name):
    _, r, c = blocked.shape
    tr = _row_tile(r, c)

    def body(place_ref, g_ref, r_ref, own_ref, send_ref):
        s = g_ref[...] + r_ref[...]
        send_ref[...] = s.astype(wire_dtype)

        @pl.when(pl.program_id(1) == place_ref[1])
        def _():
            own_ref[...] = s

    return pl.pallas_call(
        body, name=name,
        grid_spec=pltpu.PrefetchScalarGridSpec(
            num_scalar_prefetch=1, grid=(r // tr, N_CHIPS),
            in_specs=[pl.BlockSpec((None, None, tr, c), lambda i, k, place_ref: (k, place_ref[0], i, 0)),
                      pl.BlockSpec((None, tr, c), lambda i, k, place_ref: (k, i, 0))],
            out_specs=[pl.BlockSpec((tr, c), lambda i, k, place_ref: (i, 0)),
                       pl.BlockSpec((None, tr, c), lambda i, k, place_ref: (k, i, 0))]),
        out_shape=[jax.ShapeDtypeStruct((r, c), F32), jax.ShapeDtypeStruct((N_CHIPS, r, c), wire_dtype)],
        compiler_params=_params(("parallel", "arbitrary")),
    )(place, blocked.reshape(N_CHIPS, 2, r, c), recv)


def _chip_sum(own_ref, r_ref):
    return ((own_ref[...] + r_ref[0].astype(F32)) + r_ref[1].astype(F32)) + r_ref[2].astype(F32)


def _reduce_chip_sum(own, recv, *, name):
    r, c = own.shape
    tr = _row_tile(r, c)

    def body(own_ref, r_ref, o_ref):
        o_ref[...] = _chip_sum(own_ref, r_ref)

    return pl.pallas_call(
        body, name=name, grid=(r // tr,),
        in_specs=[pl.BlockSpec((tr, c), lambda i: (i, 0)), pl.BlockSpec((N_CHIPS - 1, tr, c), lambda i: (0, i, 0))],
        out_specs=pl.BlockSpec((tr, c), lambda i: (i, 0)),
        out_shape=jax.ShapeDtypeStruct((r, c), F32),
        compiler_params=_params(("parallel",)),
    )(own, recv)


def _adamw_math(w, g, m, v):
    nm = ADAM_B1 * m + (1.0 - ADAM_B1) * g
    nv = ADAM_B2 * v + (1.0 - ADAM_B2) * (g * g)
    m_hat = nm / (1.0 - ADAM_B1 ** ADAM_STEP)
    v_hat = nv / (1.0 - ADAM_B2 ** ADAM_STEP)
    return -ADAM_LR * (m_hat / (jnp.sqrt(v_hat) + ADAM_EPS) + ADAM_WD * w), nm, nv


def _adamw(w, g, m, v, *, name):
    shape = w.shape
    C = shape[-1]
    R = math.prod(shape[:-1])
    tr = _row_tile(R, C)

    def body(w_ref, g_ref, m_ref, v_ref, d_ref, nm_ref, nv_ref):
        d_ref[...], nm_ref[...], nv_ref[...] = _adamw_math(w_ref[...], g_ref[...], m_ref[...], v_ref[...])

    spec = pl.BlockSpec((tr, C), lambda i: (i, 0))
    outs = pl.pallas_call(
        body, name=name, grid=(R // tr,),
        in_specs=[spec] * 4, out_specs=[spec] * 3,
        out_shape=[jax.ShapeDtypeStruct((R, C), F32)] * 3,
        compiler_params=_params(("parallel",)),
    )(*[a.reshape(R, C) for a in (w, g, m, v)])
    return tuple(o.reshape(shape) for o in outs)


def _reduce_adamw(own, recv, w, m, v, layer, prev, *, name):
    r, c = own.shape
    tr = _row_tile(r, c)
    n_prev = 0 if prev is None else len(prev)

    def body(own_ref, r_ref, w_ref, m_ref, v_ref, *rest):
        g_ref, d_ref, nm_ref, nv_ref = rest[n_prev:]
        g = _chip_sum(own_ref, r_ref)
        g_ref[...] = g
        d_ref[...], nm_ref[...], nv_ref[...] = _adamw_math(w_ref[...], g, m_ref[...], v_ref[...])

    slot = pl.BlockSpec((None, tr, c), lambda i: (layer, i, 0))
    return pl.pallas_call(
        body, name=name, grid=(r // tr,),
        in_specs=[pl.BlockSpec((tr, c), lambda i: (i, 0)), pl.BlockSpec((N_CHIPS - 1, tr, c), lambda i: (0, i, 0)),
                  slot, slot, slot] + [ANY] * n_prev,
        out_specs=[slot] * 4,
        out_shape=[jax.ShapeDtypeStruct((DEPTH, r, c), F32)] * 4,
        input_output_aliases={5 + k: k for k in range(n_prev)},
        compiler_params=_params(("parallel",)),
    )(own, recv, w, m, v, *(prev or ()))


REPLICATED = (("mix_norm", (D_MODEL,)), ("q_norm", (HEAD_DIM,)), ("k_norm", (HEAD_DIM,)), ("sinks", (N_Q_HEADS,)),
              ("sgu_norm", (SGU_WIDTH,)), ("w_s", (SGU_GROUPS, BLOCK, BLOCK)), ("b_s", (SGU_GROUPS, BLOCK)),
              ("ffn_norm", (D_MODEL,)), ("conv_b", (2 * D_FF,)))
SHARDED = (("w_in", "blocks"), ("w_oa", "cols"), ("w_ob", "cols"), ("w_out", "rows"), ("w_up", "blocks"),
           ("conv_w", "blocks"), ("w_down", "rows"))
WEIGHT_ORDER = ("mix_norm", "w_in", "q_norm", "k_norm", "sinks", "sgu_norm", "w_s", "b_s", "w_oa", "w_ob", "w_out",
                "ffn_norm", "w_up", "conv_w", "conv_b", "w_down")
MIXER_WEIGHTS = ["w_in", "w_oa", "w_ob", "w_out"]
FFN_WEIGHTS = ["w_up", "conv_w", "w_down"]


def _small_layout():
    segs, off = {}, 0
    for l in range(DEPTH):
        for name, shape in REPLICATED:
            n = math.prod(shape)
            segs[(l, name)] = (off, n)
            off += n
    per_dev = -(-off // (N_DEV * SUBLANES * LANES)) * SUBLANES * LANES
    return segs, off, per_dev


def _pack_small(grads):
    ssegs, total, per_dev = _small_layout()
    flat = jnp.concatenate([grads[l][name].reshape(-1) for (l, name) in ssegs])
    return jnp.pad(flat, (0, N_DEV * per_dev - total)).reshape(N_DEV, per_dev // LANES, LANES)


def _unpack_small(gathered):
    ssegs, _, _ = _small_layout()
    flat = gathered.reshape(-1)
    shapes = dict(REPLICATED)
    return {name: jnp.stack([flat[ssegs[(l, name)][0]:ssegs[(l, name)][0] + ssegs[(l, name)][1]].reshape(shapes[name])
                             for l in range(DEPTH)]) for name, _ in REPLICATED}


def kernel(x, mix_norm, w_in, q_norm, k_norm, sinks, sgu_norm, w_s, b_s, w_oa, w_ob, w_out, ffn_norm, w_up, conv_w, conv_b, w_down, loss_target, m_mix_norm, m_w_in, m_q_norm, m_k_norm, m_sinks, m_sgu_norm, m_w_s, m_b_s, m_w_oa, m_w_ob, m_w_out, m_ffn_norm, m_w_up, m_conv_w, m_conv_b, m_w_down, v_mix_norm, v_w_in, v_q_norm, v_k_norm, v_sinks, v_sgu_norm, v_w_s, v_b_s, v_w_oa, v_w_ob, v_w_out, v_ffn_norm, v_w_up, v_conv_w, v_conv_b, v_w_down):
    W = dict(mix_norm=mix_norm, w_in=w_in, q_norm=q_norm, k_norm=k_norm, sinks=sinks, sgu_norm=sgu_norm, w_s=w_s, b_s=b_s,
             w_oa=w_oa, w_ob=w_ob, w_out=w_out, ffn_norm=ffn_norm, w_up=w_up, conv_w=conv_w, conv_b=conv_b, w_down=w_down)
    M = dict(mix_norm=m_mix_norm, w_in=m_w_in, q_norm=m_q_norm, k_norm=m_k_norm, sinks=m_sinks, sgu_norm=m_sgu_norm,
             w_s=m_w_s, b_s=m_b_s, w_oa=m_w_oa, w_ob=m_w_ob, w_out=m_w_out, ffn_norm=m_ffn_norm, w_up=m_w_up,
             conv_w=m_conv_w, conv_b=m_conv_b, w_down=m_w_down)
    V = dict(mix_norm=v_mix_norm, w_in=v_w_in, q_norm=v_q_norm, k_norm=v_k_norm, sinks=v_sinks, sgu_norm=v_sgu_norm,
             w_s=v_w_s, b_s=v_b_s, w_oa=v_w_oa, w_ob=v_w_ob, w_out=v_w_out, ffn_norm=v_ffn_norm, w_up=v_w_up,
             conv_w=v_conv_w, conv_b=v_conv_b, w_down=v_w_down)
    n_seq, seq, d_model = x.shape
    tokens = n_seq * seq
    mx, my, mc = _my_place()
    place = jnp.stack([mc, 2 * mx + my]).astype(jnp.int32)
    half = N_DEV // 2
    kind_of = dict(SHARDED)

    gather_groups = [[(l, n) for n in names] for l in range(DEPTH) for names in (MIXER_WEIGHTS, FFN_WEIGHTS)]
    started, in_flight = {}, {}
    weights = []
    for l in range(DEPTH):
        w = {name: W[name][l] for name, _ in REPLICATED}
        w["cb_g"], w["cb_v"] = W["conv_b"][l][:D_FF], W["conv_b"][l][D_FF:]
        w["bias_full"] = jnp.repeat(W["b_s"][l].T, SGU_WIDTH // SGU_GROUPS, axis=1)
        weights.append(w)

    def gather_start(gi, after=()):
        shards = [W[name][l] for l, name in gather_groups[gi]]
        kinds = [kind_of[name] for _, name in gather_groups[gi]]
        shapes = [s.shape for s in shards]
        lands = _place_own(shards, kinds, [F32 if name == "conv_w" else BF16 for _, name in gather_groups[gi]],
                           name=f"gather_weights_own_{gi}")
        send, recv, lands, token = _gather_start(lands, kinds, shapes, after, name=f"gather_weights_start_{gi}")
        started[gi] = dict(sems=(send, recv), lands=lands, kinds=kinds, shapes=shapes)
        return token

    def gather_forward(gi, after):
        st = started[gi]
        in_flight[gi] = _gather_forward(st["sems"][1], st["lands"], st["kinds"], st["shapes"], after,
                                        name=f"gather_weights_forward_{gi}")
        return in_flight[gi][3]

    def gather_finish(gi, after):
        st = started.pop(gi)
        fwd_send, fwd_recv, lands_g, _ = in_flight.pop(gi)
        whole = _gather_finish(st["sems"][0], st["sems"][1], fwd_send, fwd_recv, lands_g, st["kinds"], st["shapes"], after,
                               name=f"gather_weights_finish_{gi}")
        for (l, name), arr in zip(gather_groups[gi], whole):
            w = weights[l]
            if name == "w_in":
                (w["w_in"],) = _assemble(arr, (IN_WIDTH,), _w_in_moves(), name=f"l{l}_assemble_w_in")
            elif name == "w_up":
                w["w_up_g"], w["w_up_v"] = _assemble(arr, (D_FF, D_FF), _w_up_moves(), name=f"l{l}_assemble_w_up")
            elif name == "conv_w":
                w["cw_g"] = arr[:half].transpose(1, 0, 2).reshape(3, D_FF)
                w["cw_v"] = arr[half:].transpose(1, 0, 2).reshape(3, D_FF)
            else:
                w[name] = arr

    reduce_state, results = {}, {}
    wire = {"conv_w": F32, "small": F32}

    def reduce_begin(key, names, arrays):
        send, recv, srcs_, lands_, token = _exchange_start(arrays, _pair_plan, N_CHIPS, name=f"reduce_pair_start_{key}")
        reduce_state[key] = dict(names=names, pair=(send, recv, srcs_, lands_))
        return [token]

    def reduce_pair(key, after):
        st = reduce_state[key]
        send, recv, srcs_, lands_ = st.pop("pair")
        blocked_, from_sibling = _exchange_wait(send, recv, srcs_, lands_, _pair_plan, N_CHIPS, after,
                                                name=f"reduce_pair_wait_{key}")
        sums = [_reduce_pair_sum(b, r, place, wire.get(n if isinstance(n, str) else n[1], BF16),
                                 name=f"reduce_pair_sum_{key}_{i}")
                for i, (n, b, r) in enumerate(zip(st["names"], blocked_, from_sibling))]
        st["own"] = [s[0] for s in sums]
        *st["chip"], token = _exchange_start([s[1] for s in sums], _chip_plan, N_CHIPS - 1, name=f"reduce_chip_start_{key}")
        return [token]

    def reduce_end(key, after):
        st = reduce_state.pop(key)
        send, recv, srcs_, lands_ = st["chip"]
        _, from_chips = _exchange_wait(send, recv, srcs_, lands_, _chip_plan, N_CHIPS - 1, after,
                                       name=f"reduce_chip_wait_{key}")
        done = []
        for n, own, got in zip(st["names"], st["own"], from_chips):
            if n == "small":
                results["small"] = _reduce_chip_sum(own, got, name="reduce_chip_sum_small")
            else:
                l, name = n
                results[name] = _reduce_adamw(own, got, W[name], M[name], V[name], l, results.get(name),
                                              name=f"l{l}_reduce_adamw_{name}")
                done.append(results[name][0])
        return done

    def sched(point, l, carry, g=None):
        deps = []
        if point == "fwd_start" and l == 0:
            token = gather_forward(0, gather_start(0))
            gather_finish(0, token)
            deps = [gather_start(1, [weights[0]["w_out"]])]
        elif point == "fwd_att" and l == 0:
            deps = [gather_forward(1, carry), gather_start(2, [carry])]
        elif point == "fwd_mixer_done" and l == 0:
            gather_finish(1, carry)
            deps = [gather_start(3, [carry])]
        elif point == "fwd_conv" and l == 0:
            deps = [gather_forward(2, carry)]
        elif point == "fwd_start" and l == 1:
            gather_finish(2, carry)
        elif point == "fwd_att" and l == 1:
            deps = [gather_forward(3, carry)]
        elif point == "fwd_mixer_done" and l == 1:
            gather_finish(3, carry)
        elif point == "bwd_ffn_grads":
            if l + 1 < DEPTH:
                deps += reduce_end(f"l{l + 1}_in", g["w_up_v"])
            conv_w = jnp.concatenate([g[k].reshape(3, half, W_UP_SHARD).transpose(1, 0, 2) for k in ("cw_g", "cw_v")])
            deps += reduce_begin(
                f"l{l}_ffn", [(l, "w_down"), (l, "w_up"), (l, "conv_w")],
                [g["w_down"].reshape(N_DEV, D_FF // N_DEV, D_MODEL),
                 _disassemble((g["w_up_g"], g["w_up_v"]), W_UP_SHARD, _w_up_moves(), name=f"l{l}_split_dw_up"), conv_w])
        elif point == "bwd_merge":
            deps = reduce_pair(f"l{l}_ffn", carry)
        elif point == "bwd_out_grads":
            deps = reduce_begin(
                f"l{l}_out", [(l, "w_out"), (l, "w_oa"), (l, "w_ob")],
                [g["w_out"].reshape(N_DEV, D_MODEL // N_DEV, D_MODEL),
                 _disassemble((g["w_oa"],), LANES, _w_o_moves(), name=f"l{l}_split_dw_oa"),
                 _disassemble((g["w_ob"],), LANES, _w_o_moves(), name=f"l{l}_split_dw_ob")])
        elif point == "bwd_att":
            deps = reduce_pair(f"l{l}_out", carry) + reduce_end(f"l{l}_ffn", carry)
        elif point == "bwd_w_in_grad":
            deps = reduce_begin(f"l{l}_in", [(l, "w_in")],
                                [_disassemble((g["w_in"],), W_IN_SHARD, _w_in_moves(), name=f"l{l}_split_dw_in")])
        elif point == "bwd_dh":
            deps = reduce_pair(f"l{l}_in", carry) + reduce_end(f"l{l}_out", carry)
        return deps

    loss_part, dx, grads = _local_step(x.reshape(tokens, d_model), loss_target.reshape(tokens, d_model), weights, sched,
                                       n_seq=n_seq, seq=seq)
    loss = lax.psum(loss_part, ("x", "y", "c"))

    for g in grads:
        g["conv_b"] = jnp.concatenate([g["cb_g"], g["cb_v"]])
    reduce_begin("small", ["small"], [_pack_small(grads)])
    reduce_end("l0_in", dx)
    reduce_pair("small", results["w_in"][0])
    reduce_end("small", results["w_in"][1])

    G, delta, new_m, new_v = {}, {}, {}, {}
    for name, _ in SHARDED:
        G[name], delta[name], new_m[name], new_v[name] = results[name]
    G.update(_unpack_small(_gather([results["small"]], ["blocks"], name="gather_small_grads")[0]))
    for name, _ in REPLICATED:
        delta[name], new_m[name], new_v[name] = _adamw(W[name], G[name], M[name], V[name], name=f"adamw_{name}")
    return (loss, dx.reshape(n_seq, seq, d_model), *[G[n] for n in WEIGHT_ORDER], *[delta[n] for n in WEIGHT_ORDER],
            *[new_m[n] for n in WEIGHT_ORDER], *[new_v[n] for n in WEIGHT_ORDER])
```

```python
import math

import jax
import jax.numpy as jnp
from jax import lax
from jax.experimental import pallas as pl
from jax.experimental.pallas import tpu as pltpu

F32 = jnp.float32
BF16 = jnp.bfloat16
MESH = pl.DeviceIdType.MESH

DEPTH = 2
D_MODEL = 1024
N_Q_HEADS = 8
HEAD_DIM = 64
ATT_WIDTH = 512
KV_WIDTH = 128
BLOCK = 128
SGU_WIDTH = 512
SGU_GROUPS = 8
IN_WIDTH = 3840
D_FF = 2816
NORM_EPS = 1e-6
NEG_INF = -1e30
ATT_SCALE = HEAD_DIM ** -0.5
ALIBI_SLOPES = tuple(2.0 ** (-(h + 1)) for h in range(N_Q_HEADS))
ADAM_LR, ADAM_B1, ADAM_B2, ADAM_EPS, ADAM_WD, ADAM_STEP = 0.001, 0.9, 0.999, 1e-08, 0.01, 10
N_DEV = 8
N_CHIPS = 4

QKV_WIDTH = ATT_WIDTH + 2 * KV_WIDTH
REST_WIDTH = IN_WIDTH - QKV_WIDTH
COL_SUV, COL_GA, COL_GB, COL_QKV = 0, 1024, 2048, 3072

LANES = 128
SUBLANES = 8
VMEM_LIMIT_V7X = 56 * 1024 * 1024
GELU_C = math.sqrt(2.0 / math.pi)
GELU_K = 0.044715
ANY = pl.BlockSpec(memory_space=pl.ANY)


def _params(sem=None):
    return pltpu.CompilerParams(dimension_semantics=sem, vmem_limit_bytes=VMEM_LIMIT_V7X)


def _sigmoid(x):
    return 1.0 / (1.0 + jnp.exp(-x))


def _gelu(x):
    th = jnp.tanh(GELU_C * (x + GELU_K * x * x * x))
    return 0.5 * x * (1.0 + th)


def _gelu_and_grad(x):
    x2 = x * x
    th = jnp.tanh(GELU_C * (x + GELU_K * x2 * x))
    g = 0.5 * x * (1.0 + th)
    dg = 0.5 * (1.0 + th) + 0.5 * x * (1.0 - th * th) * (GELU_C * (1.0 + 3.0 * GELU_K * x2))
    return g, dg


def _dot(a, b, dims):
    return lax.dot_general(a, b, (dims, ((), ())), preferred_element_type=F32)


def _dot_nn(a, b):
    return _dot(a, b, ((1,), (0,)))


def _dot_nt(a, b):
    return _dot(a, b, ((1,), (1,)))


def _dot_tn(a, b):
    return _dot(a, b, ((0,), (0,)))


def _lo_mask(shape):
    return lax.broadcasted_iota(jnp.int32, shape, len(shape) - 1) < (LANES // 2)


def _half_sums(x, lo):
    s_lo = jnp.sum(jnp.where(lo, x, 0.0), axis=-1, keepdims=True)
    s_all = jnp.sum(x, axis=-1, keepdims=True)
    return jnp.where(lo, s_lo, s_all - s_lo)


def _dup_half(x, half, lo):
    r = pltpu.roll(x, LANES // 2, axis=1)
    return jnp.where(lo, x, r) if half == 0 else jnp.where(lo, r, x)


def _with_deps(body, n_in, deps):
    k = len(deps)
    if not k:
        return body, [], ()

    def skipping(*refs):
        return body(*refs[:n_in], *refs[n_in + k:])

    return skipping, [ANY] * k, tuple(deps)


MM_VMEM_BUDGET = 40 * 1024 * 1024
MM_MAX_TILE = 1408
MM_MAX_TK = 4096
MM_STEP_BYTES = 1 << 20


def _divisors(n, step, cap):
    return [d for d in range(step, min(n, cap) + 1, step) if n % d == 0] or [n]


def _mm_tiles(M, N, K, out_bytes, n_extra):
    best = None
    for tm in _divisors(M, LANES, MM_MAX_TILE):
        for tn in _divisors(N, LANES, MM_MAX_TILE):
            for tk in _divisors(K, 4 * LANES, MM_MAX_TK):
                vmem = 4 * (tm * tk + tk * tn) + 2 * tm * tn * (out_bytes + 4 * n_extra) + (0 if tk == K else 4 * tm * tn)
                if vmem > MM_VMEM_BUDGET:
                    continue
                traffic = 2 * M * K * (N // tn) + 2 * K * N * (M // tm) + M * N * (out_bytes + 4 * n_extra)
                cost = traffic + (K // tk - 1) * 8 * M * N + (M // tm) * (N // tn) * (K // tk) * MM_STEP_BYTES
                if best is None or cost < best[0]:
                    best = (cost, tm, tn, tk)
    assert best is not None, (M, N, K)
    return best[1:]


def _mm(a, b, *, mode, out_dtype, name, epilogue=None, extras=(), deps=()):
    if mode == "nn":
        (M, K), N = a.shape, b.shape[1]
    elif mode == "nt":
        (M, K), N = a.shape, b.shape[0]
    else:
        (K, M), N = a.shape, b.shape[1]
    tm, tn, tk = _mm_tiles(M, N, K, jnp.dtype(out_dtype).itemsize, len(extras))
    gm, gn, gk = M // tm, N // tn, K // tk
    if mode == "nn":
        a_spec = pl.BlockSpec((tm, tk), lambda i, j, k: (i, k))
        b_spec = pl.BlockSpec((tk, tn), lambda i, j, k: (k, j))
        contract = ((1,), (0,))
    elif mode == "nt":
        a_spec = pl.BlockSpec((tm, tk), lambda i, j, k: (i, k))
        b_spec = pl.BlockSpec((tn, tk), lambda i, j, k: (j, k))
        contract = ((1,), (1,))
    else:
        a_spec = pl.BlockSpec((tk, tm), lambda i, j, k: (k, i))
        b_spec = pl.BlockSpec((tk, tn), lambda i, j, k: (k, j))
        contract = ((0,), (0,))
    o_spec = pl.BlockSpec((tm, tn), lambda i, j, k: (i, j))
    n_extra = len(extras)

    def finish(acc, extra_refs, o_ref):
        if epilogue is not None:
            acc = epilogue(acc, *[r[...] for r in extra_refs])
        o_ref[...] = acc.astype(out_dtype)

    def body(a_ref, b_ref, *rest):
        extra_refs, o_ref = rest[:n_extra], rest[n_extra]
        part = _dot(a_ref[...].astype(BF16), b_ref[...].astype(BF16), contract)
        if gk == 1:
            finish(part, extra_refs, o_ref)
            return
        acc_ref = rest[n_extra + 1]
        k = pl.program_id(2)

        @pl.when(k == 0)
        def _():
            acc_ref[...] = part

        @pl.when(k > 0)
        def _():
            acc_ref[...] += part

        @pl.when(k == gk - 1)
        def _():
            finish(acc_ref[...], extra_refs, o_ref)

    body, dep_specs, dep_args = _with_deps(body, 2 + n_extra, deps)
    return pl.pallas_call(
        body,
        name=name,
        grid=(gm, gn, gk),
        in_specs=[a_spec, b_spec] + [o_spec] * n_extra + dep_specs,
        out_specs=o_spec,
        out_shape=jax.ShapeDtypeStruct((M, N), out_dtype),
        scratch_shapes=[] if gk == 1 else [pltpu.VMEM((tm, tn), F32)],
        compiler_params=_params(("parallel", "parallel", "arbitrary")),
    )(a, b, *extras, *dep_args)


def _add(acc, r):
    return acc + r


def _rms_fwd(x, gain, *, name, tm=512):
    T, D = x.shape

    def body(x_ref, g_ref, h_ref):
        xv = x_ref[...]
        r = lax.rsqrt(jnp.mean(xv * xv, axis=-1, keepdims=True) + NORM_EPS)
        h_ref[...] = (xv * r * g_ref[...]).astype(BF16)

    return pl.pallas_call(
        body, name=name, grid=(T // tm,),
        in_specs=[pl.BlockSpec((tm, D), lambda i: (i, 0)), pl.BlockSpec((1, D), lambda i: (0, 0))],
        out_specs=pl.BlockSpec((tm, D), lambda i: (i, 0)),
        out_shape=jax.ShapeDtypeStruct((T, D), BF16),
        compiler_params=_params(("parallel",)),
    )(x, gain.reshape(1, D))


def _rms_bwd(x, gain, dh, dres, *, name, tm=512, deps=()):
    T, D = x.shape

    def body(x_ref, g_ref, dh_ref, dres_ref, dx_ref, dg_ref):
        xv = x_ref[...]
        r = lax.rsqrt(jnp.mean(xv * xv, axis=-1, keepdims=True) + NORM_EPS)
        xh = xv * r
        dhv = dh_ref[...]
        dxh = dhv * g_ref[...]
        dx = r * (dxh - xh * jnp.mean(dxh * xh, axis=-1, keepdims=True))
        dx_ref[...] = dres_ref[...] + dx
        part = jnp.sum(dhv * xh, axis=0, keepdims=True)

        @pl.when(pl.program_id(0) == 0)
        def _():
            dg_ref[...] = part

        @pl.when(pl.program_id(0) > 0)
        def _():
            dg_ref[...] += part

    row = pl.BlockSpec((tm, D), lambda i: (i, 0))
    vec = pl.BlockSpec((1, D), lambda i: (0, 0))
    body, dep_specs, dep_args = _with_deps(body, 4, deps)
    dx, dg = pl.pallas_call(
        body, name=name, grid=(T // tm,),
        in_specs=[row, vec, row, row] + dep_specs,
        out_specs=[row, vec],
        out_shape=[jax.ShapeDtypeStruct((T, D), F32), jax.ShapeDtypeStruct((1, D), F32)],
        compiler_params=_params(("arbitrary",)),
    )(x, gain.reshape(1, D), dh, dres, *dep_args)
    return dx, dg.reshape(D)


def _head_norm(x, gain2, lo):
    ms = _half_sums(x * x, lo) * (1.0 / HEAD_DIM)
    r = lax.rsqrt(ms + NORM_EPS)
    xh = x * r
    return xh * gain2, xh, r


def _head_norm_bwd(xh, r, gain2, dy, lo):
    dxh = dy * gain2
    dx = r * (dxh - xh * (_half_sums(dxh * xh, lo) * (1.0 / HEAD_DIM)))
    return dx, dy * xh


Q_GROUP = N_Q_HEADS // 2
GROUP_ROWS = Q_GROUP * BLOCK
ATT_SCRATCH = (pltpu.VMEM((2, 2, GROUP_ROWS, BLOCK), F32), pltpu.VMEM((2, GROUP_ROWS, 1), F32))


def _att_consts(sink_ref, bias_ref, sinkcol_ref):
    row = lax.broadcasted_iota(jnp.int32, (GROUP_ROWS, BLOCK), 0)
    kj = lax.broadcasted_iota(jnp.int32, (GROUP_ROWS, BLOCK), 1)
    head = row // BLOCK
    head_col = lax.broadcasted_iota(jnp.int32, (GROUP_ROWS, 1), 0) // BLOCK
    d_cur = (row % BLOCK) - kj
    d_prev = d_cur + BLOCK
    for kv in range(2):
        slope = jnp.zeros((GROUP_ROWS, BLOCK), F32)
        sink = jnp.zeros((GROUP_ROWS, 1), F32)
        for r in range(Q_GROUP):
            slope = jnp.where(head == r, ALIBI_SLOPES[Q_GROUP * kv + r], slope)
            sink = jnp.where(head_col == r, sink_ref[Q_GROUP * kv + r], sink)
        bias_ref[kv, 0] = jnp.where(d_cur >= 0, -slope * d_cur.astype(F32), NEG_INF)
        bias_ref[kv, 1] = jnp.where(d_prev < BLOCK, -slope * d_prev.astype(F32), NEG_INF)
        sinkcol_ref[kv] = sink


def _stack_heads(t0, t1, lo):
    z = jnp.zeros_like(t0)
    return jnp.concatenate([jnp.where(lo, t0, z), jnp.where(lo, z, t0), jnp.where(lo, t1, z), jnp.where(lo, z, t1)], axis=0)


def _unstack_heads(x4, lo):
    return (jnp.where(lo, x4[0:BLOCK], x4[BLOCK:2 * BLOCK]), jnp.where(lo, x4[2 * BLOCK:3 * BLOCK], x4[3 * BLOCK:]))


def _att_probs(q4, k2c, k2p, bias_c, bias_p, sink, has_prev):
    s_c = _dot_nt(q4, k2c) * ATT_SCALE + bias_c
    s_p = jnp.where(has_prev, _dot_nt(q4, k2p) * ATT_SCALE + bias_p, NEG_INF)
    m = jnp.maximum(jnp.max(jnp.maximum(s_c, s_p), axis=-1, keepdims=True), sink)
    e_c = jnp.exp(s_c - m)
    e_p = jnp.exp(s_p - m)
    e_s = jnp.exp(sink - m)
    inv = 1.0 / (jnp.sum(e_c + e_p, axis=-1, keepdims=True) + e_s)
    return e_c * inv, e_p * inv, e_s * inv


def _attention_fwd(proj, q_gain, k_gain, sinks, *, n_seq, seq, name):
    T = n_seq * seq
    nb = seq // BLOCK
    qcol, kvcol = COL_QKV // ATT_WIDTH, (COL_QKV + ATT_WIDTH) // (2 * KV_WIDTH)

    def body(q_ref, kv_ref, qg_ref, kg_ref, sink_ref, y_ref, bias_ref, sinkcol_ref):
        lo = _lo_mask((BLOCK, LANES))
        qg, kg = qg_ref[...], kg_ref[...]
        _att_consts(sink_ref, bias_ref, sinkcol_ref)

        def block(i, carry):
            r0 = pl.multiple_of(i * BLOCK, BLOCK)
            rp = pl.multiple_of(jnp.maximum(i - 1, 0) * BLOCK, BLOCK)
            has_prev = i > 0
            kn_c = _head_norm(kv_ref[pl.ds(r0, BLOCK), 0:KV_WIDTH], kg, lo)[0].astype(BF16)
            kn_p = _head_norm(kv_ref[pl.ds(rp, BLOCK), 0:KV_WIDTH], kg, lo)[0].astype(BF16)
            v_c = kv_ref[pl.ds(r0, BLOCK), KV_WIDTH:2 * KV_WIDTH].astype(BF16)
            v_p = kv_ref[pl.ds(rp, BLOCK), KV_WIDTH:2 * KV_WIDTH].astype(BF16)
            for kv in range(2):
                k2c, k2p = _dup_half(kn_c, kv, lo), _dup_half(kn_p, kv, lo)
                v2c, v2p = _dup_half(v_c, kv, lo), _dup_half(v_p, kv, lo)
                cols = [slice((2 * kv + t) * LANES, (2 * kv + t + 1) * LANES) for t in range(2)]
                qn = [_head_norm(q_ref[pl.ds(r0, BLOCK), c], qg, lo)[0] for c in cols]
                q4 = _stack_heads(qn[0], qn[1], lo).astype(BF16)
                p_c, p_p, _ = _att_probs(q4, k2c, k2p, bias_ref[kv, 0], bias_ref[kv, 1], sinkcol_ref[kv], has_prev)
                o4 = _dot_nn(p_c.astype(BF16), v2c) + _dot_nn(p_p.astype(BF16), v2p)
                for c, out in zip(cols, _unstack_heads(o4, lo)):
                    y_ref[pl.ds(r0, BLOCK), c] = out.astype(BF16)
            return carry

        lax.fori_loop(0, nb, block, 0)

    vec = pl.BlockSpec((1, LANES), lambda b: (0, 0))
    return pl.pallas_call(
        body, name=name, grid=(n_seq,),
        in_specs=[pl.BlockSpec((seq, ATT_WIDTH), lambda b: (b, qcol)),
                  pl.BlockSpec((seq, 2 * KV_WIDTH), lambda b: (b, kvcol)),
                  vec, vec, pl.BlockSpec(memory_space=pltpu.SMEM)],
        out_specs=pl.BlockSpec((seq, ATT_WIDTH), lambda b: (b, 0)),
        out_shape=jax.ShapeDtypeStruct((T, ATT_WIDTH), BF16),
        scratch_shapes=list(ATT_SCRATCH),
        compiler_params=_params(("parallel",)),
    )(proj, proj, jnp.tile(q_gain, 2).reshape(1, LANES), jnp.tile(k_gain, 2).reshape(1, LANES), sinks)


def _attention_bwd(proj, dy, q_gain, k_gain, sinks, *, n_seq, seq, name, deps=()):
    T = n_seq * seq
    nb = seq // BLOCK
    qcol, kvcol = COL_QKV // ATT_WIDTH, (COL_QKV + ATT_WIDTH) // (2 * KV_WIDTH)

    def body(q_ref, kv_ref, dy_ref, qg_ref, kg_ref, sink_ref, dqkv_ref, dqg_ref, dkg_ref, dsink_ref,
             dkn_acc, dv_acc, qg_acc, kg_acc, sink_acc, bias_ref, sinkcol_ref):
        lo = _lo_mask((BLOCK, LANES))
        qg, kg = qg_ref[...], kg_ref[...]
        _att_consts(sink_ref, bias_ref, sinkcol_ref)
        first = pl.program_id(0) == 0

        @pl.when(first)
        def _():
            qg_acc[...] = jnp.zeros_like(qg_acc)
            kg_acc[...] = jnp.zeros_like(kg_acc)
            sink_acc[...] = jnp.zeros_like(sink_acc)

        dkn_acc[...] = jnp.zeros_like(dkn_acc)
        dv_acc[...] = jnp.zeros_like(dv_acc)

        def block(i, carry):
            r0 = pl.multiple_of(i * BLOCK, BLOCK)
            rp = pl.multiple_of(jnp.maximum(i - 1, 0) * BLOCK, BLOCK)
            has_prev = i > 0
            kn_c = _head_norm(kv_ref[pl.ds(r0, BLOCK), 0:KV_WIDTH], kg, lo)[0].astype(BF16)
            kn_p = _head_norm(kv_ref[pl.ds(rp, BLOCK), 0:KV_WIDTH], kg, lo)[0].astype(BF16)
            v_c = kv_ref[pl.ds(r0, BLOCK), KV_WIDTH:2 * KV_WIDTH].astype(BF16)
            v_p = kv_ref[pl.ds(rp, BLOCK), KV_WIDTH:2 * KV_WIDTH].astype(BF16)
            dk_c, dk_p, dv_c, dv_p = [], [], [], []
            for kv in range(2):
                k2c, k2p = _dup_half(kn_c, kv, lo), _dup_half(kn_p, kv, lo)
                v2c, v2p = _dup_half(v_c, kv, lo), _dup_half(v_p, kv, lo)
                cols = [slice((2 * kv + t) * LANES, (2 * kv + t + 1) * LANES) for t in range(2)]
                normed = [_head_norm(q_ref[pl.ds(r0, BLOCK), c], qg, lo) for c in cols]
                q4 = _stack_heads(normed[0][0], normed[1][0], lo).astype(BF16)
                do4 = _stack_heads(dy_ref[pl.ds(r0, BLOCK), cols[0]], dy_ref[pl.ds(r0, BLOCK), cols[1]], lo)
                p_c, p_p, p_s = _att_probs(q4, k2c, k2p, bias_ref[kv, 0], bias_ref[kv, 1], sinkcol_ref[kv], has_prev)
                dp_c = _dot_nt(do4, v2c)
                dp_p = _dot_nt(do4, v2p)
                delta = jnp.sum(p_c * dp_c + p_p * dp_p, axis=-1, keepdims=True)
                ds_c = (p_c * (dp_c - delta)).astype(BF16)
                ds_p = (p_p * (dp_p - delta)).astype(BF16)
                sink_acc[kv] += -(p_s * delta)
                dq4 = (_dot_nn(ds_c, k2c) + _dot_nn(ds_p, k2p)) * ATT_SCALE
                for c, (_, qh, qr), dqn in zip(cols, normed, _unstack_heads(dq4, lo)):
                    dq, dg = _head_norm_bwd(qh, qr, qg, dqn, lo)
                    dqkv_ref[pl.ds(r0, BLOCK), c] = dq.astype(BF16)
                    qg_acc[...] += dg
                dk_c.append(_dot_tn(ds_c, q4))
                dk_p.append(_dot_tn(ds_p, q4))
                dv_c.append(_dot_tn(p_c.astype(BF16), do4))
                dv_p.append(_dot_tn(p_p.astype(BF16), do4))

            def fold(parts):
                a = parts[0] + pltpu.roll(parts[0], LANES // 2, axis=1)
                b = parts[1] + pltpu.roll(parts[1], LANES // 2, axis=1)
                return jnp.where(lo, a, b)

            dkn_acc[pl.ds(r0, BLOCK), :] += fold(dk_c) * ATT_SCALE
            dkn_acc[pl.ds(rp, BLOCK), :] += fold(dk_p) * ATT_SCALE
            dv_acc[pl.ds(r0, BLOCK), :] += fold(dv_c)
            dv_acc[pl.ds(rp, BLOCK), :] += fold(dv_p)
            return carry

        lax.fori_loop(0, nb, block, 0)

        def finish(i, carry):
            r0 = pl.multiple_of(i * BLOCK, BLOCK)
            _, kh, kr = _head_norm(kv_ref[pl.ds(r0, BLOCK), 0:KV_WIDTH], kg, lo)
            dk, dg = _head_norm_bwd(kh, kr, kg, dkn_acc[pl.ds(r0, BLOCK), :], lo)
            dqkv_ref[pl.ds(r0, BLOCK), ATT_WIDTH:ATT_WIDTH + KV_WIDTH] = dk.astype(BF16)
            dqkv_ref[pl.ds(r0, BLOCK), ATT_WIDTH + KV_WIDTH:QKV_WIDTH] = dv_acc[pl.ds(r0, BLOCK), :].astype(BF16)
            kg_acc[...] += dg
            return carry

        lax.fori_loop(0, nb, finish, 0)

        @pl.when(pl.program_id(0) == n_seq - 1)
        def _():
            dqg_ref[...] = jnp.sum(qg_acc[...], axis=0, keepdims=True)
            dkg_ref[...] = jnp.sum(kg_acc[...], axis=0, keepdims=True)
            lane = lax.broadcasted_iota(jnp.int32, (1, LANES), 1)
            dsink = jnp.zeros((1, LANES), F32)
            for kv in range(2):
                for r in range(Q_GROUP):
                    total = jnp.sum(sink_acc[kv, r * BLOCK:(r + 1) * BLOCK, :], axis=0, keepdims=True)
                    dsink = jnp.where(lane == Q_GROUP * kv + r, total, dsink)
            dsink_ref[...] = dsink

    vec = pl.BlockSpec((1, LANES), lambda b: (0, 0))
    acc = pltpu.VMEM((BLOCK, LANES), F32)
    body, dep_specs, dep_args = _with_deps(body, 6, deps)
    dqkv, dqg, dkg, dsink = pl.pallas_call(
        body, name=name, grid=(n_seq,),
        in_specs=[pl.BlockSpec((seq, ATT_WIDTH), lambda b: (b, qcol)),
                  pl.BlockSpec((seq, 2 * KV_WIDTH), lambda b: (b, kvcol)),
                  pl.BlockSpec((seq, ATT_WIDTH), lambda b: (b, 0)),
                  vec, vec, pl.BlockSpec(memory_space=pltpu.SMEM)] + dep_specs,
        out_specs=[pl.BlockSpec((seq, QKV_WIDTH), lambda b: (b, 0)), vec, vec, vec],
        out_shape=[jax.ShapeDtypeStruct((T, QKV_WIDTH), BF16)] + [jax.ShapeDtypeStruct((1, LANES), F32)] * 3,
        scratch_shapes=[pltpu.VMEM((seq, KV_WIDTH), F32), pltpu.VMEM((seq, KV_WIDTH), F32), acc, acc,
                        pltpu.VMEM((2, GROUP_ROWS, 1), F32), *ATT_SCRATCH],
        compiler_params=_params(("arbitrary",)),
    )(proj, proj, dy, jnp.tile(q_gain, 2).reshape(1, LANES), jnp.tile(k_gain, 2).reshape(1, LANES), sinks, *dep_args)
    half = LANES // 2
    return dqkv, dqg[0, :half] + dqg[0, half:], dkg[0, :half] + dkg[0, half:], dsink[0, :N_Q_HEADS]


def _sgu_weights(w_ref):
    r = lax.broadcasted_iota(jnp.int32, (BLOCK, BLOCK), 0)
    c = lax.broadcasted_iota(jnp.int32, (BLOCK, BLOCK), 1)
    return [jnp.where(r >= c, w_ref[g], 0.0).astype(BF16) for g in range(SGU_GROUPS)]


def _sgu_fwd(proj, gain, w_s, bias_full, *, n_seq, seq, name):
    T = n_seq * seq
    nc = seq // BLOCK

    def body(suv_ref, g_ref, w_ref, b_ref, y_ref):
        lo = _lo_mask((BLOCK, LANES))
        wm = _sgu_weights(w_ref)
        gain_v = g_ref[...]

        def chunk(c, carry):
            r0 = pl.multiple_of(c * BLOCK, BLOCK)
            gv = _gelu(suv_ref[pl.ds(r0, BLOCK), SGU_WIDTH:2 * SGU_WIDTH])
            r = lax.rsqrt(jnp.mean(gv * gv, axis=-1, keepdims=True) + NORM_EPS)
            vn = (gv * r * gain_v).astype(BF16)
            for p in range(SGU_WIDTH // LANES):
                cols = slice(p * LANES, (p + 1) * LANES)
                vp = vn[:, cols]
                mixed = jnp.where(lo, _dot_nn(wm[2 * p], vp), _dot_nn(wm[2 * p + 1], vp)) + b_ref[:, cols]
                u = _gelu(suv_ref[pl.ds(r0, BLOCK), cols])
                y_ref[pl.ds(r0, BLOCK), cols] = (u * mixed).astype(BF16)
            return carry

        lax.fori_loop(0, nc, chunk, 0)

    return pl.pallas_call(
        body, name=name, grid=(n_seq,),
        in_specs=[pl.BlockSpec((seq, 2 * SGU_WIDTH), lambda b: (b, COL_SUV // (2 * SGU_WIDTH))),
                  pl.BlockSpec((1, SGU_WIDTH), lambda b: (0, 0)),
                  pl.BlockSpec((SGU_GROUPS, BLOCK, BLOCK), lambda b: (0, 0, 0)),
                  pl.BlockSpec((BLOCK, SGU_WIDTH), lambda b: (0, 0))],
        out_specs=pl.BlockSpec((seq, SGU_WIDTH), lambda b: (b, 0)),
        out_shape=jax.ShapeDtypeStruct((T, SGU_WIDTH), BF16),
        compiler_params=_params(("parallel",)),
    )(proj, gain.reshape(1, SGU_WIDTH), w_s, bias_full)


def _sgu_bwd(proj, dy, gain, w_s, bias_full, *, n_seq, seq, name, deps=()):
    T = n_seq * seq
    nc = seq // BLOCK
    n_tiles = SGU_WIDTH // LANES

    def body(suv_ref, dy_ref, g_ref, w_ref, b_ref, dsuv_ref, dg_ref, dw_ref, db_ref, dg_acc, dw_acc, db_acc):
        lo = _lo_mask((BLOCK, LANES))
        hi = jnp.logical_not(lo)
        wm = _sgu_weights(w_ref)
        wmt = [jnp.where(lax.broadcasted_iota(jnp.int32, (BLOCK, BLOCK), 1) >= lax.broadcasted_iota(jnp.int32, (BLOCK, BLOCK), 0),
                         w_ref[g].T, 0.0).astype(BF16) for g in range(SGU_GROUPS)]
        gain_v = g_ref[...]

        @pl.when(pl.program_id(0) == 0)
        def _():
            dg_acc[...] = jnp.zeros_like(dg_acc)
            dw_acc[...] = jnp.zeros_like(dw_acc)
            db_acc[...] = jnp.zeros_like(db_acc)

        def chunk(c, carry):
            r0 = pl.multiple_of(c * BLOCK, BLOCK)
            gv, dgelu_v = _gelu_and_grad(suv_ref[pl.ds(r0, BLOCK), SGU_WIDTH:2 * SGU_WIDTH])
            r = lax.rsqrt(jnp.mean(gv * gv, axis=-1, keepdims=True) + NORM_EPS)
            vh = gv * r
            vn = (vh * gain_v).astype(BF16)
            dvn_tiles = []
            for p in range(n_tiles):
                cols = slice(p * LANES, (p + 1) * LANES)
                vp = vn[:, cols]
                mixed = jnp.where(lo, _dot_nn(wm[2 * p], vp), _dot_nn(wm[2 * p + 1], vp)) + b_ref[:, cols]
                u, dgelu_u = _gelu_and_grad(suv_ref[pl.ds(r0, BLOCK), cols])
                dyv = dy_ref[pl.ds(r0, BLOCK), cols]
                dsuv_ref[pl.ds(r0, BLOCK), cols] = (dyv * mixed * dgelu_u).astype(BF16)
                dm = dyv * u
                db_acc[:, cols] += dm
                dm_bf = dm.astype(BF16)
                dvn_tiles.append(jnp.where(lo, _dot_nn(wmt[2 * p], dm_bf), _dot_nn(wmt[2 * p + 1], dm_bf)))
                dw_acc[2 * p] += _dot_nt(jnp.where(lo, dm, 0.0).astype(BF16), vp)
                dw_acc[2 * p + 1] += _dot_nt(jnp.where(hi, dm, 0.0).astype(BF16), vp)
            dvn = jnp.concatenate(dvn_tiles, axis=1)
            dg_acc[...] += dvn * vh
            dvh = dvn * gain_v
            dgv = r * (dvh - vh * jnp.mean(dvh * vh, axis=-1, keepdims=True))
            dsuv_ref[pl.ds(r0, BLOCK), SGU_WIDTH:2 * SGU_WIDTH] = (dgv * dgelu_v).astype(BF16)
            return carry

        lax.fori_loop(0, nc, chunk, 0)

        @pl.when(pl.program_id(0) == n_seq - 1)
        def _():
            dg_ref[...] = jnp.sum(dg_acc[...], axis=0, keepdims=True)
            r = lax.broadcasted_iota(jnp.int32, (BLOCK, BLOCK), 0)
            c = lax.broadcasted_iota(jnp.int32, (BLOCK, BLOCK), 1)
            for g in range(SGU_GROUPS):
                dw_ref[g] = jnp.where(r >= c, dw_acc[g], 0.0)
            lane = lax.broadcasted_iota(jnp.int32, (BLOCK, LANES), 1)
            out = jnp.zeros((BLOCK, LANES), F32)
            for p in range(n_tiles):
                tile = db_acc[:, p * LANES:(p + 1) * LANES]
                s_lo = jnp.sum(jnp.where(lo, tile, 0.0), axis=-1, keepdims=True)
                s_hi = jnp.sum(jnp.where(hi, tile, 0.0), axis=-1, keepdims=True)
                out = jnp.where(lane == 2 * p, s_lo, out)
                out = jnp.where(lane == 2 * p + 1, s_hi, out)
            db_ref[...] = out

    body, dep_specs, dep_args = _with_deps(body, 5, deps)
    dsuv, dg, dw, db = pl.pallas_call(
        body, name=name, grid=(n_seq,),
        in_specs=[pl.BlockSpec((seq, 2 * SGU_WIDTH), lambda b: (b, COL_SUV // (2 * SGU_WIDTH))),
                  pl.BlockSpec((seq, SGU_WIDTH), lambda b: (b, 0)),
                  pl.BlockSpec((1, SGU_WIDTH), lambda b: (0, 0)),
                  pl.BlockSpec((SGU_GROUPS, BLOCK, BLOCK), lambda b: (0, 0, 0)),
                  pl.BlockSpec((BLOCK, SGU_WIDTH), lambda b: (0, 0))] + dep_specs,
        out_specs=[pl.BlockSpec((seq, 2 * SGU_WIDTH), lambda b: (b, 0)),
                   pl.BlockSpec((1, SGU_WIDTH), lambda b: (0, 0)),
                   pl.BlockSpec((SGU_GROUPS, BLOCK, BLOCK), lambda b: (0, 0, 0)),
                   pl.BlockSpec((BLOCK, LANES), lambda b: (0, 0))],
        out_shape=[jax.ShapeDtypeStruct((T, 2 * SGU_WIDTH), BF16), jax.ShapeDtypeStruct((1, SGU_WIDTH), F32),
                   jax.ShapeDtypeStruct((SGU_GROUPS, BLOCK, BLOCK), F32), jax.ShapeDtypeStruct((BLOCK, LANES), F32)],
        scratch_shapes=[pltpu.VMEM((BLOCK, SGU_WIDTH), F32), pltpu.VMEM((SGU_GROUPS, BLOCK, BLOCK), F32),
                        pltpu.VMEM((BLOCK, SGU_WIDTH), F32)],
        compiler_params=_params(("arbitrary",)),
    )(proj, dy, gain.reshape(1, SGU_WIDTH), w_s, bias_full, *dep_args)
    return dsuv, dg.reshape(SGU_WIDTH), dw, db[:, :SGU_GROUPS].T


def _merge_fwd(y_att, y_sgu, w_oa, w_ob, proj, *, name, tm=1024, tn=512, deps=()):
    T = y_att.shape[0]

    def body(ya_ref, ys_ref, wa_ref, wb_ref, ga_ref, gb_ref, o_ref):
        pa = _dot_nn(ya_ref[...], wa_ref[...])
        pb = _dot_nn(ys_ref[...], wb_ref[...])
        o_ref[...] = (_sigmoid(ga_ref[...]) * pa + _sigmoid(gb_ref[...]) * pb).astype(BF16)

    act = pl.BlockSpec((tm, ATT_WIDTH), lambda i, j: (i, 0))
    wgt = pl.BlockSpec((ATT_WIDTH, tn), lambda i, j: (0, j))
    body, dep_specs, dep_args = _with_deps(body, 6, deps)
    return pl.pallas_call(
        body, name=name, grid=(T // tm, D_MODEL // tn),
        in_specs=[act, act, wgt, wgt,
                  pl.BlockSpec((tm, tn), lambda i, j: (i, j + COL_GA // tn)),
                  pl.BlockSpec((tm, tn), lambda i, j: (i, j + COL_GB // tn))] + dep_specs,
        out_specs=pl.BlockSpec((tm, tn), lambda i, j: (i, j)),
        out_shape=jax.ShapeDtypeStruct((T, D_MODEL), BF16),
        compiler_params=_params(("parallel", "parallel")),
    )(y_att, y_sgu, w_oa, w_ob, proj, proj, *dep_args)


def _merge_bwd(dx1_bf, w_out, y_att, y_sgu, w_oa, w_ob, proj, *, name, tm=1024, tn=512):
    T = y_att.shape[0]

    def body(dx_ref, wo_ref, ya_ref, ys_ref, wa_ref, wb_ref, ga_ref, gb_ref, dpa_ref, dpb_ref, dga_ref, dgb_ref):
        dm = _dot_nt(dx_ref[...], wo_ref[...])
        pa = _dot_nn(ya_ref[...], wa_ref[...])
        pb = _dot_nn(ys_ref[...], wb_ref[...])
        sa = _sigmoid(ga_ref[...])
        sb = _sigmoid(gb_ref[...])
        dpa_ref[...] = (dm * sa).astype(BF16)
        dpb_ref[...] = (dm * sb).astype(BF16)
        dga_ref[...] = (dm * pa * sa * (1.0 - sa)).astype(BF16)
        dgb_ref[...] = (dm * pb * sb * (1.0 - sb)).astype(BF16)

    act = pl.BlockSpec((tm, ATT_WIDTH), lambda i, j: (i, 0))
    wgt = pl.BlockSpec((ATT_WIDTH, tn), lambda i, j: (0, j))
    out = pl.BlockSpec((tm, tn), lambda i, j: (i, j))
    return pl.pallas_call(
        body, name=name, grid=(T // tm, D_MODEL // tn),
        in_specs=[pl.BlockSpec((tm, D_MODEL), lambda i, j: (i, 0)),
                  pl.BlockSpec((tn, D_MODEL), lambda i, j: (j, 0)),
                  act, act, wgt, wgt,
                  pl.BlockSpec((tm, tn), lambda i, j: (i, j + COL_GA // tn)),
                  pl.BlockSpec((tm, tn), lambda i, j: (i, j + COL_GB // tn))],
        out_specs=[out] * 4,
        out_shape=[jax.ShapeDtypeStruct((T, D_MODEL), BF16)] * 4,
        compiler_params=_params(("parallel", "parallel")),
    )(dx1_bf, w_out, y_att, y_sgu, w_oa, w_ob, proj, proj)


CONV_ROWS = 256
CONV_TN = 256


def _shift_rows(cur, prev8, k):
    rolled = pltpu.roll(cur, k, axis=0)
    head = jnp.where(lax.broadcasted_iota(jnp.int32, prev8.shape, 0) < k, pltpu.roll(prev8, k, axis=0), rolled[:SUBLANES])
    return jnp.concatenate([head, rolled[SUBLANES:]], axis=0)


def _shift_rows_up(cur, next8, k):
    n = cur.shape[0]
    rolled = pltpu.roll(cur, n - k, axis=0)
    tail = jnp.where(lax.broadcasted_iota(jnp.int32, next8.shape, 0) >= SUBLANES - k,
                     pltpu.roll(next8, SUBLANES - k, axis=0), rolled[n - SUBLANES:])
    return jnp.concatenate([rolled[:n - SUBLANES], tail], axis=0)


def _conv_rows(z_ref, r0, first, w_ref, b_ref, rows):
    cur = z_ref[pl.ds(r0, rows), :]
    rp = pl.multiple_of(jnp.maximum(r0 - SUBLANES, 0), SUBLANES)
    prev8 = jnp.where(first, 0.0, z_ref[pl.ds(rp, SUBLANES), :])
    z1 = _shift_rows(cur, prev8, 1)
    z2 = _shift_rows(cur, prev8, 2)
    return b_ref[...] + w_ref[0:1, :] * z2 + w_ref[1:2, :] * z1 + w_ref[2:3, :] * cur


def _conv_fwd(z_g, z_v, cw_g, cw_v, cb_g, cb_v, *, n_seq, seq, name):
    T = n_seq * seq
    tn, rows = CONV_TN, CONV_ROWS

    def body(zg_ref, zv_ref, wg_ref, wv_ref, bg_ref, bv_ref, a_ref):
        def step(s, carry):
            r0 = pl.multiple_of(s * rows, rows)
            first = s == 0
            g = _conv_rows(zg_ref, r0, first, wg_ref, bg_ref, rows)
            v = _conv_rows(zv_ref, r0, first, wv_ref, bv_ref, rows)
            a_ref[pl.ds(r0, rows), :] = (g * _sigmoid(g) * v).astype(BF16)
            return carry

        lax.fori_loop(0, seq // rows, step, 0)

    zs = pl.BlockSpec((seq, tn), lambda b, j: (b, j))
    ws = pl.BlockSpec((3, tn), lambda b, j: (0, j))
    bs = pl.BlockSpec((1, tn), lambda b, j: (0, j))
    return pl.pallas_call(
        body, name=name, grid=(n_seq, D_FF // tn),
        in_specs=[zs, zs, ws, ws, bs, bs], out_specs=zs,
        out_shape=jax.ShapeDtypeStruct((T, D_FF), BF16),
        compiler_params=_params(("parallel", "parallel")),
    )(z_g, z_v, cw_g, cw_v, cb_g.reshape(1, D_FF), cb_v.reshape(1, D_FF))


def _conv_bwd(z_g, z_v, da, cw_g, cw_v, cb_g, cb_v, *, n_seq, seq, name):
    T = n_seq * seq
    tn, rows = CONV_TN, CONV_ROWS
    n_steps = seq // rows

    def body(zg_ref, zv_ref, da_ref, wg_ref, wv_ref, bg_ref, bv_ref,
             dzg_ref, dzv_ref, dwg_ref, dwv_ref, dbg_ref, dbv_ref, dcg_ref, dcv_ref):
        def grads(s, accs):
            r0 = pl.multiple_of(s * rows, rows)
            first = s == 0
            cur_g = zg_ref[pl.ds(r0, rows), :]
            cur_v = zv_ref[pl.ds(r0, rows), :]
            rp = pl.multiple_of(jnp.maximum(r0 - SUBLANES, 0), SUBLANES)
            pg = jnp.where(first, 0.0, zg_ref[pl.ds(rp, SUBLANES), :])
            pv = jnp.where(first, 0.0, zv_ref[pl.ds(rp, SUBLANES), :])
            g1, g2 = _shift_rows(cur_g, pg, 1), _shift_rows(cur_g, pg, 2)
            v1, v2 = _shift_rows(cur_v, pv, 1), _shift_rows(cur_v, pv, 2)
            g = bg_ref[...] + wg_ref[0:1, :] * g2 + wg_ref[1:2, :] * g1 + wg_ref[2:3, :] * cur_g
            v = bv_ref[...] + wv_ref[0:1, :] * v2 + wv_ref[1:2, :] * v1 + wv_ref[2:3, :] * cur_v
            sg = _sigmoid(g)
            dav = da_ref[pl.ds(r0, rows), :]
            dcg = dav * v * (sg * (1.0 + g * (1.0 - sg)))
            dcv = dav * (g * sg)
            dcg_ref[pl.ds(r0, rows), :] = dcg
            dcv_ref[pl.ds(r0, rows), :] = dcv

            def colsum(x):
                return jnp.sum(x, axis=0, keepdims=True)

            return (accs[0] + colsum(dcg * g2), accs[1] + colsum(dcg * g1), accs[2] + colsum(dcg * cur_g), accs[3] + colsum(dcg),
                    accs[4] + colsum(dcv * v2), accs[5] + colsum(dcv * v1), accs[6] + colsum(dcv * cur_v), accs[7] + colsum(dcv))

        zero = jnp.zeros((1, tn), F32)
        sums = lax.fori_loop(0, n_steps, grads, (zero,) * 8)
        first_seq = pl.program_id(1) == 0

        @pl.when(first_seq)
        def _():
            dwg_ref[...] = jnp.concatenate(sums[0:3], axis=0)
            dbg_ref[...] = sums[3]
            dwv_ref[...] = jnp.concatenate(sums[4:7], axis=0)
            dbv_ref[...] = sums[7]

        @pl.when(jnp.logical_not(first_seq))
        def _():
            dwg_ref[...] += jnp.concatenate(sums[0:3], axis=0)
            dbg_ref[...] += sums[3]
            dwv_ref[...] += jnp.concatenate(sums[4:7], axis=0)
            dbv_ref[...] += sums[7]

        def back(s, carry):
            r0 = pl.multiple_of(s * rows, rows)
            last = s == n_steps - 1
            rn = pl.multiple_of(jnp.minimum(r0 + rows, seq - SUBLANES), SUBLANES)
            for dc_ref, w_ref, dz_ref in ((dcg_ref, wg_ref, dzg_ref), (dcv_ref, wv_ref, dzv_ref)):
                cur = dc_ref[pl.ds(r0, rows), :]
                nxt = jnp.where(last, 0.0, dc_ref[pl.ds(rn, SUBLANES), :])
                u1, u2 = _shift_rows_up(cur, nxt, 1), _shift_rows_up(cur, nxt, 2)
                dz_ref[pl.ds(r0, rows), :] = (w_ref[2:3, :] * cur + w_ref[1:2, :] * u1 + w_ref[0:1, :] * u2).astype(BF16)
            return carry

        lax.fori_loop(0, n_steps, back, 0)

    zs = pl.BlockSpec((seq, tn), lambda j, b: (b, j))
    ws = pl.BlockSpec((3, tn), lambda j, b: (0, j))
    bs = pl.BlockSpec((1, tn), lambda j, b: (0, j))
    outs = pl.pallas_call(
        body, name=name, grid=(D_FF // tn, n_seq),
        in_specs=[zs, zs, zs, ws, ws, bs, bs],
        out_specs=[zs, zs, ws, ws, bs, bs],
        out_shape=[jax.ShapeDtypeStruct((T, D_FF), BF16)] * 2 + [jax.ShapeDtypeStruct((3, D_FF), F32)] * 2
        + [jax.ShapeDtypeStruct((1, D_FF), F32)] * 2,
        scratch_shapes=[pltpu.VMEM((seq, tn), F32), pltpu.VMEM((seq, tn), F32)],
        compiler_params=_params(("parallel", "arbitrary")),
    )(z_g, z_v, da, cw_g, cw_v, cb_g.reshape(1, D_FF), cb_v.reshape(1, D_FF))
    dz_g, dz_v, dw_g, dw_v, db_g, db_v = outs
    return dz_g, dz_v, dw_g, dw_v, db_g.reshape(D_FF), db_v.reshape(D_FF)


def _loss_head(y, target, *, name, tm=512):
    T, D = y.shape

    def body(y_ref, t_ref, dy_ref, dyb_ref, l_ref):
        err = y_ref[...] - t_ref[...]
        dyv = err * (1.0 / D)
        dy_ref[...] = dyv
        dyb_ref[...] = dyv.astype(BF16)
        part = jnp.sum(jnp.sum(err * err, axis=0, keepdims=True), axis=1, keepdims=True) * (0.5 / D)

        @pl.when(pl.program_id(0) == 0)
        def _():
            l_ref[...] = jnp.broadcast_to(part, l_ref.shape)

        @pl.when(pl.program_id(0) > 0)
        def _():
            l_ref[...] += jnp.broadcast_to(part, l_ref.shape)

    row = pl.BlockSpec((tm, D), lambda i: (i, 0))
    dy, dyb, l = pl.pallas_call(
        body, name=name, grid=(T // tm,),
        in_specs=[row, row],
        out_specs=[row, row, pl.BlockSpec((SUBLANES, LANES), lambda i: (0, 0))],
        out_shape=[jax.ShapeDtypeStruct((T, D), F32), jax.ShapeDtypeStruct((T, D), BF16),
                   jax.ShapeDtypeStruct((SUBLANES, LANES), F32)],
        compiler_params=_params(("arbitrary",)),
    )(y, target)
    return l[0, 0], dy, dyb


def _cast_bf16(x, *, name, tm=512):
    T, D = x.shape

    def body(x_ref, o_ref):
        o_ref[...] = x_ref[...].astype(BF16)

    row = pl.BlockSpec((tm, D), lambda i: (i, 0))
    return pl.pallas_call(body, name=name, grid=(T // tm,), in_specs=[row], out_specs=row,
                          out_shape=jax.ShapeDtypeStruct((T, D), BF16), compiler_params=_params(("parallel",)))(x)


def _layer_fwd(x, w, sched, *, n_seq, seq, l):
    tag = f"l{l}"
    deps = sched("fwd_start", l, x)
    h = _rms_fwd(x, w["mix_norm"], name=f"{tag}_mix_norm")
    proj = _mm(h, w["w_in"], mode="nn", out_dtype=F32, name=f"{tag}_proj", deps=deps)
    y_att = _attention_fwd(proj, w["q_norm"], w["k_norm"], w["sinks"], n_seq=n_seq, seq=seq, name=f"{tag}_att")
    deps = sched("fwd_att", l, y_att)
    y_sgu = _sgu_fwd(proj, w["sgu_norm"], w["w_s"], w["bias_full"], n_seq=n_seq, seq=seq, name=f"{tag}_sgu")
    merged = _merge_fwd(y_att, y_sgu, w["w_oa"], w["w_ob"], proj, name=f"{tag}_merge", deps=deps)
    x1 = _mm(merged, w["w_out"], mode="nn", out_dtype=F32, name=f"{tag}_out",
             epilogue=_add, extras=(x,))
    deps = sched("fwd_mixer_done", l, x1)
    h2 = _rms_fwd(x1, w["ffn_norm"], name=f"{tag}_ffn_norm")
    z_g = _mm(h2, w["w_up_g"], mode="nn", out_dtype=F32, name=f"{tag}_up_g", deps=deps)
    z_v = _mm(h2, w["w_up_v"], mode="nn", out_dtype=F32, name=f"{tag}_up_v")
    a = _conv_fwd(z_g, z_v, w["cw_g"], w["cw_v"], w["cb_g"], w["cb_v"], n_seq=n_seq, seq=seq, name=f"{tag}_conv")
    deps = sched("fwd_conv", l, a)
    x2 = _mm(a, w["w_down"], mode="nn", out_dtype=F32, name=f"{tag}_down",
             epilogue=_add, extras=(x1,), deps=deps)
    saved = dict(x=x, h=h, proj=proj, y_att=y_att, y_sgu=y_sgu, merged=merged, x1=x1, h2=h2, z_g=z_g, z_v=z_v, a=a)
    return x2, saved


def _layer_bwd(dx2, dx2_bf, w, s, sched, *, n_seq, seq, l):
    tag = f"l{l}b"
    g = {}
    da = _mm(dx2_bf, w["w_down"], mode="nt", out_dtype=F32, name=f"{tag}_da")
    g["w_down"] = _mm(s["a"], dx2_bf, mode="tn", out_dtype=F32, name=f"{tag}_dw_down")
    dz_g, dz_v, g["cw_g"], g["cw_v"], g["cb_g"], g["cb_v"] = _conv_bwd(
        s["z_g"], s["z_v"], da, w["cw_g"], w["cw_v"], w["cb_g"], w["cb_v"], n_seq=n_seq, seq=seq, name=f"{tag}_conv")
    dh2 = _mm(dz_g, w["w_up_g"], mode="nt", out_dtype=F32, name=f"{tag}_dh2_g")
    dh2 = _mm(dz_v, w["w_up_v"], mode="nt", out_dtype=F32, name=f"{tag}_dh2_v",
              epilogue=_add, extras=(dh2,))
    g["w_up_g"] = _mm(s["h2"], dz_g, mode="tn", out_dtype=F32, name=f"{tag}_dw_up_g")
    g["w_up_v"] = _mm(s["h2"], dz_v, mode="tn", out_dtype=F32, name=f"{tag}_dw_up_v")
    deps = sched("bwd_ffn_grads", l, dh2, g)
    dx1, g["ffn_norm"] = _rms_bwd(s["x1"], w["ffn_norm"], dh2, dx2, name=f"{tag}_ffn_norm", deps=deps)
    dx1_bf = _cast_bf16(dx1, name=f"{tag}_dx1_bf")
    dpa, dpb, dga, dgb = _merge_bwd(dx1_bf, w["w_out"], s["y_att"], s["y_sgu"], w["w_oa"], w["w_ob"], s["proj"],
                                    name=f"{tag}_merge")
    deps = sched("bwd_merge", l, dpa)
    g["w_out"] = _mm(s["merged"], dx1_bf, mode="tn", out_dtype=F32, name=f"{tag}_dw_out",
                     deps=deps)
    dy_att = _mm(dpa, w["w_oa"], mode="nt", out_dtype=BF16, name=f"{tag}_dy_att")
    dy_sgu = _mm(dpb, w["w_ob"], mode="nt", out_dtype=F32, name=f"{tag}_dy_sgu")
    g["w_oa"] = _mm(s["y_att"], dpa, mode="tn", out_dtype=F32, name=f"{tag}_dw_oa")
    g["w_ob"] = _mm(s["y_sgu"], dpb, mode="tn", out_dtype=F32, name=f"{tag}_dw_ob")
    deps = sched("bwd_out_grads", l, dy_att, g)
    dqkv, g["q_norm"], g["k_norm"], g["sinks"] = _attention_bwd(
        s["proj"], dy_att, w["q_norm"], w["k_norm"], w["sinks"], n_seq=n_seq, seq=seq, name=f"{tag}_att", deps=deps)
    deps = sched("bwd_att", l, dqkv)
    dsuv, g["sgu_norm"], g["w_s"], g["b_s"] = _sgu_bwd(
        s["proj"], dy_sgu, w["sgu_norm"], w["w_s"], w["bias_full"], n_seq=n_seq, seq=seq, name=f"{tag}_sgu", deps=deps)
    dproj = jnp.concatenate([dsuv, dga, dgb, dqkv], axis=1)
    g["w_in"] = _mm(s["h"], dproj, mode="tn", out_dtype=F32, name=f"{tag}_dw_in")
    deps = sched("bwd_w_in_grad", l, dproj, g)
    dh = _mm(dproj, w["w_in"], mode="nt", out_dtype=F32, name=f"{tag}_dh", deps=deps)
    deps = sched("bwd_dh", l, dh)
    dx, g["mix_norm"] = _rms_bwd(s["x"], w["mix_norm"], dh, dx1, name=f"{tag}_mix_norm", deps=deps)
    return dx, g


def _local_step(x, target, weights, sched, *, n_seq, seq):
    depth = len(weights)
    saved = []
    h = x
    for l in range(depth):
        h, s = _layer_fwd(h, weights[l], sched, n_seq=n_seq, seq=seq, l=l)
        saved.append(s)
    loss, dy, dy_bf = _loss_head(h, target, name="loss_head")
    grads = [None] * depth
    for l in reversed(range(depth)):
        if l < depth - 1:
            dy_bf = _cast_bf16(dy, name=f"l{l}b_dx2_bf")
        dy, grads[l] = _layer_bwd(dy, dy_bf, weights[l], saved[l], sched, n_seq=n_seq, seq=seq, l=l)
    return loss, dy, grads


W_IN_SHARD = IN_WIDTH // N_DEV
W_UP_SHARD = 2 * D_FF // N_DEV
COL_MOVE_ROWS = 256


def _w_in_moves():
    moves = []
    for j in range(N_DEV):
        a, b = j * W_IN_SHARD, (j + 1) * W_IN_SHARD
        if a < QKV_WIDTH:
            moves.append((j, 0, min(b, QKV_WIDTH) - a, 0, a + REST_WIDTH))
        if b > QKV_WIDTH:
            lo = max(a, QKV_WIDTH)
            moves.append((j, lo - a, b - a, 0, lo - QKV_WIDTH))
    return tuple(moves)


def _w_up_moves():
    half = N_DEV // 2
    return tuple((j, 0, W_UP_SHARD, j // half, (j % half) * W_UP_SHARD) for j in range(N_DEV))


def _w_o_moves():
    return tuple((j, 0, LANES, 0, j * LANES) for j in range(N_DEV))


def _assemble(blocks, widths, moves, *, name):
    _, R, w = blocks.shape
    tr = min(R, COL_MOVE_ROWS)

    def body(b_ref, *o_refs):
        for j, lo, hi, which, at in moves:
            o_refs[which][:, at:at + hi - lo] = b_ref[j, :, lo:hi]

    return pl.pallas_call(
        body, name=name, grid=(R // tr,),
        in_specs=[pl.BlockSpec((N_DEV, tr, w), lambda i: (0, i, 0))],
        out_specs=[pl.BlockSpec((tr, n), lambda i: (i, 0)) for n in widths],
        out_shape=[jax.ShapeDtypeStruct((R, n), blocks.dtype) for n in widths],
        compiler_params=_params(("parallel",)),
    )(blocks)


def _disassemble(mats, w, moves, *, name):
    R = mats[0].shape[0]
    tr = min(R, COL_MOVE_ROWS)
    n = len(mats)

    def body(*refs):
        m_refs, o_ref = refs[:n], refs[n]
        for j, lo, hi, which, at in moves:
            o_ref[j, :, lo:hi] = m_refs[which][:, at:at + hi - lo]

    return pl.pallas_call(
        body, name=name, grid=(R // tr,),
        in_specs=[pl.BlockSpec((tr, m.shape[1]), lambda i: (i, 0)) for m in mats],
        out_specs=pl.BlockSpec((N_DEV, tr, w), lambda i: (0, i, 0)),
        out_shape=jax.ShapeDtypeStruct((N_DEV, R, w), mats[0].dtype),
        compiler_params=_params(("parallel",)),
    )(*mats)


def _my_place():
    return lax.axis_index("x"), lax.axis_index("y"), lax.axis_index("c")


def _gathered_shape(shape, kind):
    r, c = shape
    return {"blocks": (N_DEV, r, c), "rows": (N_DEV * r, c), "cols": (r, N_DEV * c)}[kind]


def _gather_window(ref, kind, shape, j):
    r, c = shape
    if kind == "blocks":
        return ref.at[j]
    if kind == "rows":
        return ref.at[pl.ds(pl.multiple_of(j * r, r), r), :]
    return ref.at[:, pl.ds(pl.multiple_of(j * c, c), c)]


def _gather(srcs, kinds, *, name):
    n = len(srcs)
    shapes = [s.shape for s in srcs]
    per = 7

    def body(*refs):
        src_refs, dst_refs = refs[:n], refs[n:2 * n]
        send_sems, recv_sems, local_sems = refs[2 * n:]
        x, y, c = _my_place()
        me, sibling = (x, y, c), (x, y, 1 - c)
        chips = [(1 - x, y), (x, 1 - y), (1 - x, 1 - y)]

        def at(i, px, py, pc):
            return _gather_window(dst_refs[i], kinds[i], shapes[i], 4 * px + 2 * py + pc)

        def copy(i, k, block, to, src=None):
            return pltpu.make_async_remote_copy(
                src_ref=at(i, *block) if src is None else src, dst_ref=at(i, *block),
                send_sem=send_sems.at[per * i + k], recv_sem=recv_sems.at[per * i + k], device_id=to, device_id_type=MESH)

        mine = [pltpu.make_async_copy(src_refs[i], at(i, *me), local_sems.at[i]) for i in range(n)]
        for cp in mine:
            cp.start()
        started = []
        for i in range(n):
            first = [copy(i, 0, me, sibling, src=src_refs[i])]
            first += [copy(i, 1 + j, me, (*chip, c), src=src_refs[i]) for j, chip in enumerate(chips)]
            for cp in first:
                cp.start()
            started += first
        for i in range(n):
            for j, chip in enumerate(chips):
                copy(i, 1 + j, (*chip, c), me).wait_recv()
                fwd = copy(i, 4 + j, (*chip, c), sibling)
                fwd.start()
                started.append(fwd)
        for i in range(n):
            copy(i, 0, sibling, me).wait_recv()
            for j, chip in enumerate(chips):
                copy(i, 4 + j, (*chip, 1 - c), me).wait_recv()
        for cp in started:
            cp.wait_send()
        for cp in mine:
            cp.wait()

    return pl.pallas_call(
        body, name=name,
        out_shape=[jax.ShapeDtypeStruct(_gathered_shape(s.shape, k), s.dtype) for s, k in zip(srcs, kinds)],
        in_specs=[ANY] * n, out_specs=[ANY] * n,
        scratch_shapes=[pltpu.SemaphoreType.DMA((per * n,)), pltpu.SemaphoreType.DMA((per * n,)),
                        pltpu.SemaphoreType.DMA((n,))],
    )(*srcs)


HBM = pl.BlockSpec(memory_space=pltpu.HBM)
SEM = pl.BlockSpec(memory_space=pltpu.SEMAPHORE)
TOKEN = jax.ShapeDtypeStruct((SUBLANES, LANES), F32)
TOKEN_SPEC = pl.BlockSpec(memory_space=pltpu.VMEM)
SPLIT_PARAMS = pltpu.CompilerParams(has_side_effects=pltpu.SideEffectType.DATAFLOW_SIDE_EFFECTING)


def _in_hbm(x):
    return pltpu.with_memory_space_constraint(x, pltpu.HBM)


def _hbm_like(shape, dtype):
    return pltpu.HBM(shape, dtype)


def _place_own(shards, kinds, dtypes, *, name):
    n = len(shards)
    shapes = [s.shape for s in shards]

    def body(*refs):
        s_refs, land_refs, bufs, sems = refs[:n], refs[n:2 * n], refs[2 * n:3 * n], refs[3 * n]
        x, y, c = _my_place()
        copies = []
        for i in range(n):
            bufs[i][...] = s_refs[i][...].astype(dtypes[i])
            copies.append(pltpu.make_async_copy(
                bufs[i], _gather_window(land_refs[i], kinds[i], shapes[i], 4 * x + 2 * y + c), sems.at[i]))
        for cp in copies:
            cp.start()
        for cp in copies:
            cp.wait()

    return pl.pallas_call(
        body, name=name,
        out_shape=[jax.ShapeDtypeStruct(_gathered_shape(s, k), d) for s, k, d in zip(shapes, kinds, dtypes)],
        in_specs=[pl.BlockSpec(memory_space=pltpu.VMEM)] * n, out_specs=[ANY] * n,
        scratch_shapes=[pltpu.VMEM(s, d) for s, d in zip(shapes, dtypes)] + [pltpu.SemaphoreType.DMA((n,))],
        compiler_params=_params(),
    )(*shards)


def _gather_start(lands, kinds, shapes, after=(), *, name):
    n = len(lands)
    n_after = len(after)

    def body(*refs):
        land_refs = refs[:n]
        send_sems, recv_sems = refs[n + n_after], refs[n + n_after + 1]
        x, y, c = _my_place()
        targets = [(x, y, 1 - c), (1 - x, y, c), (x, 1 - y, c), (1 - x, 1 - y, c)]
        for i in range(n):
            own = _gather_window(land_refs[i], kinds[i], shapes[i], 4 * x + 2 * y + c)
            for k, to in enumerate(targets):
                pltpu.make_async_remote_copy(
                    src_ref=own, dst_ref=own, send_sem=send_sems.at[4 * i + k], recv_sem=recv_sems.at[4 * i + k],
                    device_id=to, device_id_type=MESH).start()
        refs[-1][...] = jnp.zeros_like(refs[-1])

    outs = pl.pallas_call(
        body, name=name,
        out_shape=[pltpu.SemaphoreType.DMA((4 * n,)), pltpu.SemaphoreType.DMA((4 * n,))]
        + [_hbm_like(a.shape, a.dtype) for a in lands] + [TOKEN],
        in_specs=[HBM] * n + [ANY] * n_after, out_specs=[SEM, SEM] + [HBM] * n + [TOKEN_SPEC],
        input_output_aliases={i: 2 + i for i in range(n)},
        compiler_params=SPLIT_PARAMS,
    )(*[_in_hbm(a) for a in lands], *after)
    return outs[0], outs[1], outs[2:2 + n], outs[-1]


def _gather_forward(recv_sems, lands, kinds, shapes, after, *, name):
    n = len(lands)

    def body(*refs):
        recv_ref, land_refs = refs[0], refs[1:1 + n]
        fwd_send, fwd_recv = refs[2 + n], refs[3 + n]
        token = refs[-1]
        x, y, c = _my_place()
        chips = [(1 - x, y), (x, 1 - y), (1 - x, 1 - y)]
        for i in range(n):
            for j, (px, py) in enumerate(chips):
                block = _gather_window(land_refs[i], kinds[i], shapes[i], 4 * px + 2 * py + c)
                pltpu.make_async_remote_copy(
                    src_ref=block, dst_ref=block, send_sem=fwd_send.at[3 * i + j], recv_sem=recv_ref.at[4 * i + 1 + j],
                    device_id=(px, py, c), device_id_type=MESH).wait_recv()
                pltpu.make_async_remote_copy(
                    src_ref=block, dst_ref=block, send_sem=fwd_send.at[3 * i + j], recv_sem=fwd_recv.at[3 * i + j],
                    device_id=(x, y, 1 - c), device_id_type=MESH).start()
        token[...] = jnp.zeros_like(token)

    outs = pl.pallas_call(
        body, name=name,
        out_shape=[pltpu.SemaphoreType.DMA((3 * n,)), pltpu.SemaphoreType.DMA((3 * n,))]
        + [_hbm_like(a.shape, a.dtype) for a in lands] + [TOKEN],
        in_specs=[SEM] + [HBM] * n + [ANY], out_specs=[SEM, SEM] + [HBM] * n + [TOKEN_SPEC],
        input_output_aliases={1 + i: 2 + i for i in range(n)},
        compiler_params=SPLIT_PARAMS,
    )(recv_sems, *lands, after)
    return outs[0], outs[1], outs[2:2 + n], outs[-1]


def _gather_finish(send_sems, recv_sems, fwd_send, fwd_recv, lands, kinds, shapes, after, *, name):
    n = len(lands)

    def body(*refs):
        send_ref, recv_ref, fsend_ref, frecv_ref = refs[:4]
        land_refs = refs[4:4 + n]
        x, y, c = _my_place()
        chips = [(1 - x, y), (x, 1 - y), (1 - x, 1 - y)]
        sibling = (x, y, 1 - c)
        for i in range(n):
            def window(j):
                return _gather_window(land_refs[i], kinds[i], shapes[i], j)

            mine, theirs = window(4 * x + 2 * y + c), window(4 * x + 2 * y + (1 - c))
            pltpu.make_async_remote_copy(src_ref=mine, dst_ref=theirs, send_sem=send_ref.at[4 * i],
                                         recv_sem=recv_ref.at[4 * i], device_id=sibling, device_id_type=MESH).wait_recv()
            for j, (px, py) in enumerate(chips):
                block = window(4 * px + 2 * py + (1 - c))
                pltpu.make_async_remote_copy(src_ref=block, dst_ref=block, send_sem=fsend_ref.at[3 * i + j],
                                             recv_sem=frecv_ref.at[3 * i + j], device_id=sibling,
                                             device_id_type=MESH).wait_recv()
            for k in range(4):
                pltpu.make_async_remote_copy(src_ref=mine, dst_ref=mine, send_sem=send_ref.at[4 * i + k],
                                             recv_sem=recv_ref.at[4 * i + k], device_id=sibling,
                                             device_id_type=MESH).wait_send()
            for j, (px, py) in enumerate(chips):
                block = window(4 * px + 2 * py + c)
                pltpu.make_async_remote_copy(src_ref=block, dst_ref=block, send_sem=fsend_ref.at[3 * i + j],
                                             recv_sem=frecv_ref.at[3 * i + j], device_id=sibling,
                                             device_id_type=MESH).wait_send()

    return pl.pallas_call(
        body, name=name,
        out_shape=[_hbm_like(a.shape, a.dtype) for a in lands],
        in_specs=[SEM] * 4 + [HBM] * n + [ANY], out_specs=[HBM] * n,
        input_output_aliases={4 + i: i for i in range(n)},
        compiler_params=SPLIT_PARAMS,
    )(send_sems, recv_sems, fwd_send, fwd_recv, *lands, after)


def _pair_plan(src_ref, land_ref, x, y, c):
    return [(src_ref.at[2 * k + (1 - c)], land_ref.at[k], (x, y, 1 - c)) for k in range(N_CHIPS)]


def _chip_plan(src_ref, land_ref, x, y, c):
    chips = [(1 - x, y), (x, 1 - y), (1 - x, 1 - y)]
    return [(src_ref.at[2 * px + py], land_ref.at[k], (px, py, c)) for k, (px, py) in enumerate(chips)]


def _exchange_copies(plan, per, src_refs, land_refs, send_sems, recv_sems):
    x, y, c = _my_place()
    copies = []
    for i, (s_ref, l_ref) in enumerate(zip(src_refs, land_refs)):
        for q, (src, dst, to) in enumerate(plan(s_ref, l_ref, x, y, c)):
            copies.append(pltpu.make_async_remote_copy(
                src_ref=src, dst_ref=dst, send_sem=send_sems.at[per * i + q], recv_sem=recv_sems.at[per * i + q],
                device_id=to, device_id_type=MESH))
    return copies


def _exchange_start(srcs, plan, per, *, name):
    n = len(srcs)

    def body(*refs):
        src_refs, land_refs = refs[:n], refs[n:2 * n]
        send_sems, recv_sems = refs[2 * n], refs[2 * n + 1]
        for cp in _exchange_copies(plan, per, src_refs, land_refs, send_sems, recv_sems):
            cp.start()
        refs[-1][...] = jnp.zeros_like(refs[-1])

    lands = [lax.empty((per,) + s.shape[1:], s.dtype) for s in srcs]
    outs = pl.pallas_call(
        body, name=name,
        out_shape=[pltpu.SemaphoreType.DMA((per * n,)), pltpu.SemaphoreType.DMA((per * n,))]
        + [_hbm_like(s.shape, s.dtype) for s in srcs] + [_hbm_like(a.shape, a.dtype) for a in lands] + [TOKEN],
        in_specs=[HBM] * (2 * n), out_specs=[SEM, SEM] + [HBM] * (2 * n) + [TOKEN_SPEC],
        input_output_aliases={i: 2 + i for i in range(2 * n)},
        compiler_params=SPLIT_PARAMS,
    )(*[_in_hbm(s) for s in srcs], *[_in_hbm(a) for a in lands])
    return outs[0], outs[1], outs[2:2 + n], outs[2 + n:2 + 2 * n], outs[-1]


def _exchange_wait(send_sems, recv_sems, srcs, lands, plan, per, after, *, name):
    n = len(srcs)

    def body(*refs):
        send_ref, recv_ref = refs[0], refs[1]
        src_refs, land_refs = refs[2:2 + n], refs[2 + n:2 + 2 * n]
        copies = _exchange_copies(plan, per, src_refs, land_refs, send_ref, recv_ref)
        for cp in copies:
            cp.wait_recv()
        for cp in copies:
            cp.wait_send()

    outs = pl.pallas_call(
        body, name=name,
        out_shape=[_hbm_like(s.shape, s.dtype) for s in srcs] + [_hbm_like(a.shape, a.dtype) for a in lands],
        in_specs=[SEM, SEM] + [HBM] * (2 * n) + [ANY], out_specs=[HBM] * (2 * n),
        input_output_aliases={2 + i: i for i in range(2 * n)},
        compiler_params=SPLIT_PARAMS,
    )(send_sems, recv_sems, *srcs, *lands, after)
    return outs[:n], outs[n:]


REDUCE_BLOCK_BYTES = 1 << 20


def _row_tile(r, c):
    row_bytes = 4 * (-(-c // LANES) * LANES)
    best = r
    for d in range(SUBLANES, r, SUBLANES):
        if r % d == 0 and d * row_bytes <= REDUCE_BLOCK_BYTES:
            best = d
    return best if r * row_bytes > REDUCE_BLOCK_BYTES else r


def _reduce_pair_sum(blocked, recv, place, wire_dtype, *, name):
    _, r, c = blocked.shape
    tr = _row_tile(r, c)

    def body(place_ref, g_ref, r_ref, own_ref, send_ref):
        s = g_ref[...] + r_ref[...]
        send_ref[...] = s.astype(wire_dtype)

        @pl.when(pl.program_id(1) == place_ref[1])
        def _():
            own_ref[...] = s

    return pl.pallas_call(
        body, name=name,
        grid_spec=pltpu.PrefetchScalarGridSpec(
            num_scalar_prefetch=1, grid=(r // tr, N_CHIPS),
            in_specs=[pl.BlockSpec((None, None, tr, c), lambda i, k, place_ref: (k, place_ref[0], i, 0)),
                      pl.BlockSpec((None, tr, c), lambda i, k, place_ref: (k, i, 0))],
            out_specs=[pl.BlockSpec((tr, c), lambda i, k, place_ref: (i, 0)),
                       pl.BlockSpec((None, tr, c), lambda i, k, place_ref: (k, i, 0))]),
        out_shape=[jax.ShapeDtypeStruct((r, c), F32), jax.ShapeDtypeStruct((N_CHIPS, r, c), wire_dtype)],
        compiler_params=_params(("parallel", "arbitrary")),
    )(place, blocked.reshape(N_CHIPS, 2, r, c), recv)


def _chip_sum(own_ref, r_ref):
    return ((own_ref[...] + r_ref[0].astype(F32)) + r_ref[1].astype(F32)) + r_ref[2].astype(F32)


def _reduce_chip_sum(own, recv, *, name):
    r, c = own.shape
    tr = _row_tile(r, c)

    def body(own_ref, r_ref, o_ref):
        o_ref[...] = _chip_sum(own_ref, r_ref)

    return pl.pallas_call(
        body, name=name, grid=(r // tr,),
        in_specs=[pl.BlockSpec((tr, c), lambda i: (i, 0)), pl.BlockSpec((N_CHIPS - 1, tr, c), lambda i: (0, i, 0))],
        out_specs=pl.BlockSpec((tr, c), lambda i: (i, 0)),
        out_shape=jax.ShapeDtypeStruct((r, c), F32),
        compiler_params=_params(("parallel",)),
    )(own, recv)


def _adamw_math(w, g, m, v):
    nm = ADAM_B1 * m + (1.0 - ADAM_B1) * g
    nv = ADAM_B2 * v + (1.0 - ADAM_B2) * (g * g)
    m_hat = nm / (1.0 - ADAM_B1 ** ADAM_STEP)
    v_hat = nv / (1.0 - ADAM_B2 ** ADAM_STEP)
    return -ADAM_LR * (m_hat / (jnp.sqrt(v_hat) + ADAM_EPS) + ADAM_WD * w), nm, nv


def _adamw(w, g, m, v, *, name):
    shape = w.shape
    C = shape[-1]
    R = math.prod(shape[:-1])
    tr = _row_tile(R, C)

    def body(w_ref, g_ref, m_ref, v_ref, d_ref, nm_ref, nv_ref):
        d_ref[...], nm_ref[...], nv_ref[...] = _adamw_math(w_ref[...], g_ref[...], m_ref[...], v_ref[...])

    spec = pl.BlockSpec((tr, C), lambda i: (i, 0))
    outs = pl.pallas_call(
        body, name=name, grid=(R // tr,),
        in_specs=[spec] * 4, out_specs=[spec] * 3,
        out_shape=[jax.ShapeDtypeStruct((R, C), F32)] * 3,
        compiler_params=_params(("parallel",)),
    )(*[a.reshape(R, C) for a in (w, g, m, v)])
    return tuple(o.reshape(shape) for o in outs)


def _reduce_adamw(own, recv, w, m, v, layer, prev, *, name):
    r, c = own.shape
    tr = _row_tile(r, c)
    n_prev = 0 if prev is None else len(prev)

    def body(own_ref, r_ref, w_ref, m_ref, v_ref, *rest):
        g_ref, d_ref, nm_ref, nv_ref = rest[n_prev:]
        g = _chip_sum(own_ref, r_ref)
        g_ref[...] = g
        d_ref[...], nm_ref[...], nv_ref[...] = _adamw_math(w_ref[...], g, m_ref[...], v_ref[...])

    slot = pl.BlockSpec((None, tr, c), lambda i: (layer, i, 0))
    return pl.pallas_call(
        body, name=name, grid=(r // tr,),
        in_specs=[pl.BlockSpec((tr, c), lambda i: (i, 0)), pl.BlockSpec((N_CHIPS - 1, tr, c), lambda i: (0, i, 0)),
                  slot, slot, slot] + [ANY] * n_prev,
        out_specs=[slot] * 4,
        out_shape=[jax.ShapeDtypeStruct((DEPTH, r, c), F32)] * 4,
        input_output_aliases={5 + k: k for k in range(n_prev)},
        compiler_params=_params(("parallel",)),
    )(own, recv, w, m, v, *(prev or ()))


REPLICATED = (("mix_norm", (D_MODEL,)), ("q_norm", (HEAD_DIM,)), ("k_norm", (HEAD_DIM,)), ("sinks", (N_Q_HEADS,)),
              ("sgu_norm", (SGU_WIDTH,)), ("w_s", (SGU_GROUPS, BLOCK, BLOCK)), ("b_s", (SGU_GROUPS, BLOCK)),
              ("ffn_norm", (D_MODEL,)), ("conv_b", (2 * D_FF,)))
SHARDED = (("w_in", "blocks"), ("w_oa", "cols"), ("w_ob", "cols"), ("w_out", "rows"), ("w_up", "blocks"),
           ("conv_w", "blocks"), ("w_down", "rows"))
WEIGHT_ORDER = ("mix_norm", "w_in", "q_norm", "k_norm", "sinks", "sgu_norm", "w_s", "b_s", "w_oa", "w_ob", "w_out",
                "ffn_norm", "w_up", "conv_w", "conv_b", "w_down")
MIXER_WEIGHTS = ["w_in", "w_oa", "w_ob", "w_out"]
FFN_WEIGHTS = ["w_up", "conv_w", "w_down"]


def _small_layout():
    segs, off = {}, 0
    for l in range(DEPTH):
        for name, shape in REPLICATED:
            n = math.prod(shape)
            segs[(l, name)] = (off, n)
            off += n
    per_dev = -(-off // (N_DEV * SUBLANES * LANES)) * SUBLANES * LANES
    return segs, off, per_dev


def _pack_small(grads):
    ssegs, total, per_dev = _small_layout()
    flat = jnp.concatenate([grads[l][name].reshape(-1) for (l, name) in ssegs])
    return jnp.pad(flat, (0, N_DEV * per_dev - total)).reshape(N_DEV, per_dev // LANES, LANES)


def _unpack_small(gathered):
    ssegs, _, _ = _small_layout()
    flat = gathered.reshape(-1)
    shapes = dict(REPLICATED)
    return {name: jnp.stack([flat[ssegs[(l, name)][0]:ssegs[(l, name)][0] + ssegs[(l, name)][1]].reshape(shapes[name])
                             for l in range(DEPTH)]) for name, _ in REPLICATED}


def kernel(x, mix_norm, w_in, q_norm, k_norm, sinks, sgu_norm, w_s, b_s, w_oa, w_ob, w_out, ffn_norm, w_up, conv_w, conv_b, w_down, loss_target, m_mix_norm, m_w_in, m_q_norm, m_k_norm, m_sinks, m_sgu_norm, m_w_s, m_b_s, m_w_oa, m_w_ob, m_w_out, m_ffn_norm, m_w_up, m_conv_w, m_conv_b, m_w_down, v_mix_norm, v_w_in, v_q_norm, v_k_norm, v_sinks, v_sgu_norm, v_w_s, v_b_s, v_w_oa, v_w_ob, v_w_out, v_ffn_norm, v_w_up, v_conv_w, v_conv_b, v_w_down):
    W = dict(mix_norm=mix_norm, w_in=w_in, q_norm=q_norm, k_norm=k_norm, sinks=sinks, sgu_norm=sgu_norm, w_s=w_s, b_s=b_s,
             w_oa=w_oa, w_ob=w_ob, w_out=w_out, ffn_norm=ffn_norm, w_up=w_up, conv_w=conv_w, conv_b=conv_b, w_down=w_down)
    M = dict(mix_norm=m_mix_norm, w_in=m_w_in, q_norm=m_q_norm, k_norm=m_k_norm, sinks=m_sinks, sgu_norm=m_sgu_norm,
             w_s=m_w_s, b_s=m_b_s, w_oa=m_w_oa, w_ob=m_w_ob, w_out=m_w_out, ffn_norm=m_ffn_norm, w_up=m_w_up,
             conv_w=m_conv_w, conv_b=m_conv_b, w_down=m_w_down)
    V = dict(mix_norm=v_mix_norm, w_in=v_w_in, q_norm=v_q_norm, k_norm=v_k_norm, sinks=v_sinks, sgu_norm=v_sgu_norm,
             w_s=v_w_s, b_s=v_b_s, w_oa=v_w_oa, w_ob=v_w_ob, w_out=v_w_out, ffn_norm=v_ffn_norm, w_up=v_w_up,
             conv_w=v_conv_w, conv_b=v_conv_b, w_down=v_w_down)
    n_seq, seq, d_model = x.shape
    tokens = n_seq * seq
    mx, my, mc = _my_place()
    place = jnp.stack([mc, 2 * mx + my]).astype(jnp.int32)
    half = N_DEV // 2
    kind_of = dict(SHARDED)

    gather_groups = [[(l, n) for n in names] for l in range(DEPTH) for names in (MIXER_WEIGHTS, FFN_WEIGHTS)]
    started, in_flight = {}, {}
    weights = []
    for l in range(DEPTH):
        w = {name: W[name][l] for name, _ in REPLICATED}
        w["cb_g"], w["cb_v"] = W["conv_b"][l][:D_FF], W["conv_b"][l][D_FF:]
        w["bias_full"] = jnp.repeat(W["b_s"][l].T, SGU_WIDTH // SGU_GROUPS, axis=1)
        weights.append(w)

    def gather_start(gi, after=()):
        shards = [W[name][l] for l, name in gather_groups[gi]]
        kinds = [kind_of[name] for _, name in gather_groups[gi]]
        shapes = [s.shape for s in shards]
        lands = _place_own(shards, kinds, [F32 if name == "conv_w" else BF16 for _, name in gather_groups[gi]],
                           name=f"gather_weights_own_{gi}")
        send, recv, lands, token = _gather_start(lands, kinds, shapes, after, name=f"gather_weights_start_{gi}")
        started[gi] = dict(sems=(send, recv), lands=lands, kinds=kinds, shapes=shapes)
        return token

    def gather_forward(gi, after):
        st = started[gi]
        in_flight[gi] = _gather_forward(st["sems"][1], st["lands"], st["kinds"], st["shapes"], after,
                                        name=f"gather_weights_forward_{gi}")
        return in_flight[gi][3]

    def gather_finish(gi, after):
        st = started.pop(gi)
        fwd_send, fwd_recv, lands_g, _ = in_flight.pop(gi)
        whole = _gather_finish(st["sems"][0], st["sems"][1], fwd_send, fwd_recv, lands_g, st["kinds"], st["shapes"], after,
                               name=f"gather_weights_finish_{gi}")
        for (l, name), arr in zip(gather_groups[gi], whole):
            w = weights[l]
            if name == "w_in":
                (w["w_in"],) = _assemble(arr, (IN_WIDTH,), _w_in_moves(), name=f"l{l}_assemble_w_in")
            elif name == "w_up":
                w["w_up_g"], w["w_up_v"] = _assemble(arr, (D_FF, D_FF), _w_up_moves(), name=f"l{l}_assemble_w_up")
            elif name == "conv_w":
                w["cw_g"] = arr[:half].transpose(1, 0, 2).reshape(3, D_FF)
                w["cw_v"] = arr[half:].transpose(1, 0, 2).reshape(3, D_FF)
            else:
                w[name] = arr

    reduce_state, results = {}, {}
    wire = {"conv_w": F32, "small": F32}

    def reduce_begin(key, names, arrays):
        send, recv, srcs_, lands_, token = _exchange_start(arrays, _pair_plan, N_CHIPS, name=f"reduce_pair_start_{key}")
        reduce_state[key] = dict(names=names, pair=(send, recv, srcs_, lands_))
        return [token]

    def reduce_pair(key, after):
        st = reduce_state[key]
        send, recv, srcs_, lands_ = st.pop("pair")
        blocked_, from_sibling = _exchange_wait(send, recv, srcs_, lands_, _pair_plan, N_CHIPS, after,
                                                name=f"reduce_pair_wait_{key}")
        sums = [_reduce_pair_sum(b, r, place, wire.get(n if isinstance(n, str) else n[1], BF16),
                                 name=f"reduce_pair_sum_{key}_{i}")
                for i, (n, b, r) in enumerate(zip(st["names"], blocked_, from_sibling))]
        st["own"] = [s[0] for s in sums]
        *st["chip"], token = _exchange_start([s[1] for s in sums], _chip_plan, N_CHIPS - 1, name=f"reduce_chip_start_{key}")
        return [token]

    def reduce_end(key, after):
        st = reduce_state.pop(key)
        send, recv, srcs_, lands_ = st["chip"]
        _, from_chips = _exchange_wait(send, recv, srcs_, lands_, _chip_plan, N_CHIPS - 1, after,
                                       name=f"reduce_chip_wait_{key}")
        done = []
        for n, own, got in zip(st["names"], st["own"], from_chips):
            if n == "small":
                results["small"] = _reduce_chip_sum(own, got, name="reduce_chip_sum_small")
            else:
                l, name = n
                results[name] = _reduce_adamw(own, got, W[name], M[name], V[name], l, results.get(name),
                                              name=f"l{l}_reduce_adamw_{name}")
                done.append(results[name][0])
        return done

    def sched(point, l, carry, g=None):
        deps = []
        if point == "fwd_start" and l == 0:
            token = gather_forward(0, gather_start(0))
            gather_finish(0, token)
            deps = [gather_start(1, [weights[0]["w_out"]])]
        elif point == "fwd_att" and l == 0:
            deps = [gather_forward(1, carry), gather_start(2, [carry])]
        elif point == "fwd_mixer_done" and l == 0:
            gather_finish(1, carry)
            deps = [gather_start(3, [carry])]
        elif point == "fwd_conv" and l == 0:
            deps = [gather_forward(2, carry)]
        elif point == "fwd_start" and l == 1:
            gather_finish(2, carry)
        elif point == "fwd_att" and l == 1:
            deps = [gather_forward(3, carry)]
        elif point == "fwd_mixer_done" and l == 1:
            gather_finish(3, carry)
        elif point == "bwd_ffn_grads":
            if l + 1 < DEPTH:
                deps += reduce_end(f"l{l + 1}_in", g["w_up_v"])
            conv_w = jnp.concatenate([g[k].reshape(3, half, W_UP_SHARD).transpose(1, 0, 2) for k in ("cw_g", "cw_v")])
            deps += reduce_begin(
                f"l{l}_ffn", [(l, "w_down"), (l, "w_up"), (l, "conv_w")],
                [g["w_down"].reshape(N_DEV, D_FF // N_DEV, D_MODEL),
                 _disassemble((g["w_up_g"], g["w_up_v"]), W_UP_SHARD, _w_up_moves(), name=f"l{l}_split_dw_up"), conv_w])
        elif point == "bwd_merge":
            deps = reduce_pair(f"l{l}_ffn", carry)
        elif point == "bwd_out_grads":
            deps = reduce_begin(
                f"l{l}_out", [(l, "w_out"), (l, "w_oa"), (l, "w_ob")],
                [g["w_out"].reshape(N_DEV, D_MODEL // N_DEV, D_MODEL),
                 _disassemble((g["w_oa"],), LANES, _w_o_moves(), name=f"l{l}_split_dw_oa"),
                 _disassemble((g["w_ob"],), LANES, _w_o_moves(), name=f"l{l}_split_dw_ob")])
        elif point == "bwd_att":
            deps = reduce_pair(f"l{l}_out", carry) + reduce_end(f"l{l}_ffn", carry)
        elif point == "bwd_w_in_grad":
            deps = reduce_begin(f"l{l}_in", [(l, "w_in")],
                                [_disassemble((g["w_in"],), W_IN_SHARD, _w_in_moves(), name=f"l{l}_split_dw_in")])
        elif point == "bwd_dh":
            deps = reduce_pair(f"l{l}_in", carry) + reduce_end(f"l{l}_out", carry)
        return deps

    loss_part, dx, grads = _local_step(x.reshape(tokens, d_model), loss_target.reshape(tokens, d_model), weights, sched,
                                       n_seq=n_seq, seq=seq)
    loss = lax.psum(loss_part, ("x", "y", "c"))

    for g in grads:
        g["conv_b"] = jnp.concatenate([g["cb_g"], g["cb_v"]])
    reduce_begin("small", ["small"], [_pack_small(grads)])
    reduce_end("l0_in", dx)
    reduce_pair("small", results["w_in"][0])
    reduce_end("small", results["w_in"][1])

    G, delta, new_m, new_v = {}, {}, {}, {}
    for name, _ in SHARDED:
        G[name], delta[name], new_m[name], new_v[name] = results[name]
    G.update(_unpack_small(_gather([results["small"]], ["blocks"], name="gather_small_grads")[0]))
    for name, _ in REPLICATED:
        delta[name], new_m[name], new_v[name] = _adamw(W[name], G[name], M[name], V[name], name=f"adamw_{name}")
    return (loss, dx.reshape(n_seq, seq, d_model), *[G[n] for n in WEIGHT_ORDER], *[delta[n] for n in WEIGHT_ORDER],
            *[new_m[n] for n in WEIGHT_ORDER], *[new_v[n] for n in WEIGHT_ORDER])
```

```python
import math

import jax
import jax.numpy as jnp
from jax import lax
from jax.experimental import pallas as pl
from jax.experimental.pallas import tpu as pltpu

F32 = jnp.float32
BF16 = jnp.bfloat16
ACT_DTYPE = BF16
MESH = pl.DeviceIdType.MESH

DEPTH = 2
D_MODEL = 1024
N_Q_HEADS = 8
HEAD_DIM = 64
ATT_WIDTH = 512
KV_WIDTH = 128
BLOCK = 128
SGU_WIDTH = 512
SGU_GROUPS = 8
IN_WIDTH = 3840
D_FF = 2816
NORM_EPS = 1e-6
NEG_INF = -1e30
ATT_SCALE = HEAD_DIM ** -0.5
ALIBI_SLOPES = tuple(2.0 ** (-(h + 1)) for h in range(N_Q_HEADS))
ADAM_LR, ADAM_B1, ADAM_B2, ADAM_EPS, ADAM_WD, ADAM_STEP = 0.001, 0.9, 0.999, 1e-08, 0.01, 10
N_DEV = 8
N_CHIPS = 4

QKV_WIDTH = ATT_WIDTH + 2 * KV_WIDTH
REST_WIDTH = IN_WIDTH - QKV_WIDTH
COL_SUV, COL_GA, COL_GB, COL_QKV = 0, 1024, 2048, 3072

LANES = 128
SUBLANES = 8
VMEM_LIMIT_V7X = 56 * 1024 * 1024
GELU_C = math.sqrt(2.0 / math.pi)
GELU_K = 0.044715
ANY = pl.BlockSpec(memory_space=pl.ANY)


def _params(sem=None):
    return pltpu.CompilerParams(dimension_semantics=sem, vmem_limit_bytes=VMEM_LIMIT_V7X)


def _sigmoid(x):
    return 1.0 / (1.0 + jnp.exp(-x))


def _gelu(x):
    th = jnp.tanh(GELU_C * (x + GELU_K * x * x * x))
    return 0.5 * x * (1.0 + th)


def _gelu_and_grad(x):
    x2 = x * x
    th = jnp.tanh(GELU_C * (x + GELU_K * x2 * x))
    g = 0.5 * x * (1.0 + th)
    dg = 0.5 * (1.0 + th) + 0.5 * x * (1.0 - th * th) * (GELU_C * (1.0 + 3.0 * GELU_K * x2))
    return g, dg


def _dot(a, b, dims):
    return lax.dot_general(a, b, (dims, ((), ())), preferred_element_type=F32)


def _dot_nn(a, b):
    return _dot(a, b, ((1,), (0,)))


def _dot_nt(a, b):
    return _dot(a, b, ((1,), (1,)))


def _dot_tn(a, b):
    return _dot(a, b, ((0,), (0,)))


def _lo_mask(shape):
    return lax.broadcasted_iota(jnp.int32, shape, len(shape) - 1) < (LANES // 2)


def _half_sums(x, lo):
    s_lo = jnp.sum(jnp.where(lo, x, 0.0), axis=-1, keepdims=True)
    s_all = jnp.sum(x, axis=-1, keepdims=True)
    return jnp.where(lo, s_lo, s_all - s_lo)


def _dup_half(x, half, lo):
    r = pltpu.roll(x, LANES // 2, axis=1)
    return jnp.where(lo, x, r) if half == 0 else jnp.where(lo, r, x)


def _with_deps(body, n_in, deps):
    k = len(deps)
    if not k:
        return body, [], ()

    def skipping(*refs):
        return body(*refs[:n_in], *refs[n_in + k:])

    return skipping, [ANY] * k, tuple(deps)


MM_VMEM_BUDGET = 40 * 1024 * 1024
MM_MAX_TILE = 1408
MM_MAX_TK = 4096
MM_STEP_BYTES = 1 << 20


def _divisors(n, step, cap):
    return [d for d in range(step, min(n, cap) + 1, step) if n % d == 0] or [n]


def _mm_tiles(M, N, K, out_bytes, n_extra):
    best = None
    for tm in _divisors(M, LANES, MM_MAX_TILE):
        for tn in _divisors(N, LANES, MM_MAX_TILE):
            for tk in _divisors(K, 4 * LANES, MM_MAX_TK):
                vmem = 4 * (tm * tk + tk * tn) + 2 * tm * tn * (out_bytes + 4 * n_extra) + (0 if tk == K else 4 * tm * tn)
                if vmem > MM_VMEM_BUDGET:
                    continue
                traffic = 2 * M * K * (N // tn) + 2 * K * N * (M // tm) + M * N * (out_bytes + 4 * n_extra)
                cost = traffic + (K // tk - 1) * 8 * M * N + (M // tm) * (N // tn) * (K // tk) * MM_STEP_BYTES
                if best is None or cost < best[0]:
                    best = (cost, tm, tn, tk)
    assert best is not None, (M, N, K)
    return best[1:]


def _mm(a, b, *, mode, out_dtype, name, epilogue=None, extras=(), deps=()):
    if mode == "nn":
        (M, K), N = a.shape, b.shape[1]
    elif mode == "nt":
        (M, K), N = a.shape, b.shape[0]
    else:
        (K, M), N = a.shape, b.shape[1]
    tm, tn, tk = _mm_tiles(M, N, K, jnp.dtype(out_dtype).itemsize, len(extras))
    gm, gn, gk = M // tm, N // tn, K // tk
    if mode == "nn":
        a_spec = pl.BlockSpec((tm, tk), lambda i, j, k: (i, k))
        b_spec = pl.BlockSpec((tk, tn), lambda i, j, k: (k, j))
        contract = ((1,), (0,))
    elif mode == "nt":
        a_spec = pl.BlockSpec((tm, tk), lambda i, j, k: (i, k))
        b_spec = pl.BlockSpec((tn, tk), lambda i, j, k: (j, k))
        contract = ((1,), (1,))
    else:
        a_spec = pl.BlockSpec((tk, tm), lambda i, j, k: (k, i))
        b_spec = pl.BlockSpec((tk, tn), lambda i, j, k: (k, j))
        contract = ((0,), (0,))
    o_spec = pl.BlockSpec((tm, tn), lambda i, j, k: (i, j))
    n_extra = len(extras)

    def finish(acc, extra_refs, o_ref):
        if epilogue is not None:
            acc = epilogue(acc, *[r[...] for r in extra_refs])
        o_ref[...] = acc.astype(out_dtype)

    def body(a_ref, b_ref, *rest):
        extra_refs, o_ref = rest[:n_extra], rest[n_extra]
        part = _dot(a_ref[...].astype(BF16), b_ref[...].astype(BF16), contract)
        if gk == 1:
            finish(part, extra_refs, o_ref)
            return
        acc_ref = rest[n_extra + 1]
        k = pl.program_id(2)

        @pl.when(k == 0)
        def _():
            acc_ref[...] = part

        @pl.when(k > 0)
        def _():
            acc_ref[...] += part

        @pl.when(k == gk - 1)
        def _():
            finish(acc_ref[...], extra_refs, o_ref)

    body, dep_specs, dep_args = _with_deps(body, 2 + n_extra, deps)
    return pl.pallas_call(
        body,
        name=name,
        grid=(gm, gn, gk),
        in_specs=[a_spec, b_spec] + [o_spec] * n_extra + dep_specs,
        out_specs=o_spec,
        out_shape=jax.ShapeDtypeStruct((M, N), out_dtype),
        scratch_shapes=[] if gk == 1 else [pltpu.VMEM((tm, tn), F32)],
        compiler_params=_params(("parallel", "parallel", "arbitrary")),
    )(a, b, *extras, *dep_args)


def _add(acc, r):
    return acc + r


def _rms_fwd(x, gain, *, name, tm=512):
    T, D = x.shape

    def body(x_ref, g_ref, h_ref):
        xv = x_ref[...]
        r = lax.rsqrt(jnp.mean(xv * xv, axis=-1, keepdims=True) + NORM_EPS)
        h_ref[...] = (xv * r * g_ref[...]).astype(BF16)

    return pl.pallas_call(
        body, name=name, grid=(T // tm,),
        in_specs=[pl.BlockSpec((tm, D), lambda i: (i, 0)), pl.BlockSpec((1, D), lambda i: (0, 0))],
        out_specs=pl.BlockSpec((tm, D), lambda i: (i, 0)),
        out_shape=jax.ShapeDtypeStruct((T, D), BF16),
        compiler_params=_params(("parallel",)),
    )(x, gain.reshape(1, D))


def _rms_bwd(x, gain, dh, dres, *, name, tm=512, deps=()):
    T, D = x.shape

    def body(x_ref, g_ref, dh_ref, dres_ref, dx_ref, dg_ref):
        xv = x_ref[...]
        r = lax.rsqrt(jnp.mean(xv * xv, axis=-1, keepdims=True) + NORM_EPS)
        xh = xv * r
        dhv = dh_ref[...]
        dxh = dhv * g_ref[...]
        dx = r * (dxh - xh * jnp.mean(dxh * xh, axis=-1, keepdims=True))
        dx_ref[...] = dres_ref[...] + dx
        part = jnp.sum(dhv * xh, axis=0, keepdims=True)

        @pl.when(pl.program_id(0) == 0)
        def _():
            dg_ref[...] = part

        @pl.when(pl.program_id(0) > 0)
        def _():
            dg_ref[...] += part

    row = pl.BlockSpec((tm, D), lambda i: (i, 0))
    vec = pl.BlockSpec((1, D), lambda i: (0, 0))
    body, dep_specs, dep_args = _with_deps(body, 4, deps)
    dx, dg = pl.pallas_call(
        body, name=name, grid=(T // tm,),
        in_specs=[row, vec, row, row] + dep_specs,
        out_specs=[row, vec],
        out_shape=[jax.ShapeDtypeStruct((T, D), F32), jax.ShapeDtypeStruct((1, D), F32)],
        compiler_params=_params(("arbitrary",)),
    )(x, gain.reshape(1, D), dh, dres, *dep_args)
    return dx, dg.reshape(D)


def _head_norm(x, gain2, lo):
    ms = _half_sums(x * x, lo) * (1.0 / HEAD_DIM)
    r = lax.rsqrt(ms + NORM_EPS)
    xh = x * r
    return xh * gain2, xh, r


def _head_norm_bwd(xh, r, gain2, dy, lo):
    dxh = dy * gain2
    dx = r * (dxh - xh * (_half_sums(dxh * xh, lo) * (1.0 / HEAD_DIM)))
    return dx, dy * xh


Q_GROUP = N_Q_HEADS // 2
GROUP_ROWS = Q_GROUP * BLOCK
ATT_SCRATCH = (pltpu.VMEM((2, 2, GROUP_ROWS, BLOCK), F32), pltpu.VMEM((2, GROUP_ROWS, 1), F32))


def _att_consts(sink_ref, bias_ref, sinkcol_ref):
    row = lax.broadcasted_iota(jnp.int32, (GROUP_ROWS, BLOCK), 0)
    kj = lax.broadcasted_iota(jnp.int32, (GROUP_ROWS, BLOCK), 1)
    head = row // BLOCK
    head_col = lax.broadcasted_iota(jnp.int32, (GROUP_ROWS, 1), 0) // BLOCK
    d_cur = (row % BLOCK) - kj
    d_prev = d_cur + BLOCK
    for kv in range(2):
        slope = jnp.zeros((GROUP_ROWS, BLOCK), F32)
        sink = jnp.zeros((GROUP_ROWS, 1), F32)
        for r in range(Q_GROUP):
            slope = jnp.where(head == r, ALIBI_SLOPES[Q_GROUP * kv + r], slope)
            sink = jnp.where(head_col == r, sink_ref[Q_GROUP * kv + r], sink)
        bias_ref[kv, 0] = jnp.where(d_cur >= 0, -slope * d_cur.astype(F32), NEG_INF)
        bias_ref[kv, 1] = jnp.where(d_prev < BLOCK, -slope * d_prev.astype(F32), NEG_INF)
        sinkcol_ref[kv] = sink


def _stack_heads(t0, t1, lo):
    z = jnp.zeros_like(t0)
    return jnp.concatenate([jnp.where(lo, t0, z), jnp.where(lo, z, t0), jnp.where(lo, t1, z), jnp.where(lo, z, t1)], axis=0)


def _unstack_heads(x4, lo):
    return (jnp.where(lo, x4[0:BLOCK], x4[BLOCK:2 * BLOCK]), jnp.where(lo, x4[2 * BLOCK:3 * BLOCK], x4[3 * BLOCK:]))


def _att_probs(q4, k2c, k2p, bias_c, bias_p, sink, has_prev):
    s_c = _dot_nt(q4, k2c) * ATT_SCALE + bias_c
    s_p = jnp.where(has_prev, _dot_nt(q4, k2p) * ATT_SCALE + bias_p, NEG_INF)
    m = jnp.maximum(jnp.max(jnp.maximum(s_c, s_p), axis=-1, keepdims=True), sink)
    e_c = jnp.exp(s_c - m)
    e_p = jnp.exp(s_p - m)
    e_s = jnp.exp(sink - m)
    inv = 1.0 / (jnp.sum(e_c + e_p, axis=-1, keepdims=True) + e_s)
    return e_c * inv, e_p * inv, e_s * inv


def _attention_fwd(proj, q_gain, k_gain, sinks, *, n_seq, seq, name):
    T = n_seq * seq
    nb = seq // BLOCK
    qcol, kvcol = COL_QKV // ATT_WIDTH, (COL_QKV + ATT_WIDTH) // (2 * KV_WIDTH)

    def body(q_ref, kv_ref, qg_ref, kg_ref, sink_ref, y_ref, bias_ref, sinkcol_ref):
        lo = _lo_mask((BLOCK, LANES))
        qg, kg = qg_ref[...], kg_ref[...]
        _att_consts(sink_ref, bias_ref, sinkcol_ref)

        def block(i, carry):
            r0 = pl.multiple_of(i * BLOCK, BLOCK)
            rp = pl.multiple_of(jnp.maximum(i - 1, 0) * BLOCK, BLOCK)
            has_prev = i > 0
            kn_c = _head_norm(kv_ref[pl.ds(r0, BLOCK), 0:KV_WIDTH].astype(F32), kg, lo)[0].astype(BF16)
            kn_p = _head_norm(kv_ref[pl.ds(rp, BLOCK), 0:KV_WIDTH].astype(F32), kg, lo)[0].astype(BF16)
            v_c = kv_ref[pl.ds(r0, BLOCK), KV_WIDTH:2 * KV_WIDTH].astype(BF16)
            v_p = kv_ref[pl.ds(rp, BLOCK), KV_WIDTH:2 * KV_WIDTH].astype(BF16)
            for kv in range(2):
                k2c, k2p = _dup_half(kn_c, kv, lo), _dup_half(kn_p, kv, lo)
                v2c, v2p = _dup_half(v_c, kv, lo), _dup_half(v_p, kv, lo)
                cols = [slice((2 * kv + t) * LANES, (2 * kv + t + 1) * LANES) for t in range(2)]
                qn = [_head_norm(q_ref[pl.ds(r0, BLOCK), c].astype(F32), qg, lo)[0] for c in cols]
                q4 = _stack_heads(qn[0], qn[1], lo).astype(BF16)
                p_c, p_p, _ = _att_probs(q4, k2c, k2p, bias_ref[kv, 0], bias_ref[kv, 1], sinkcol_ref[kv], has_prev)
                o4 = _dot_nn(p_c.astype(BF16), v2c) + _dot_nn(p_p.astype(BF16), v2p)
                for c, out in zip(cols, _unstack_heads(o4, lo)):
                    y_ref[pl.ds(r0, BLOCK), c] = out.astype(BF16)
            return carry

        lax.fori_loop(0, nb, block, 0)

    vec = pl.BlockSpec((1, LANES), lambda b: (0, 0))
    return pl.pallas_call(
        body, name=name, grid=(n_seq,),
        in_specs=[pl.BlockSpec((seq, ATT_WIDTH), lambda b: (b, qcol)),
                  pl.BlockSpec((seq, 2 * KV_WIDTH), lambda b: (b, kvcol)),
                  vec, vec, pl.BlockSpec(memory_space=pltpu.SMEM)],
        out_specs=pl.BlockSpec((seq, ATT_WIDTH), lambda b: (b, 0)),
        out_shape=jax.ShapeDtypeStruct((T, ATT_WIDTH), BF16),
        scratch_shapes=list(ATT_SCRATCH),
        compiler_params=_params(("parallel",)),
    )(proj, proj, jnp.tile(q_gain, 2).reshape(1, LANES), jnp.tile(k_gain, 2).reshape(1, LANES), sinks)


def _attention_bwd(proj, dy, q_gain, k_gain, sinks, *, n_seq, seq, name, deps=()):
    T = n_seq * seq
    nb = seq // BLOCK
    qcol, kvcol = COL_QKV // ATT_WIDTH, (COL_QKV + ATT_WIDTH) // (2 * KV_WIDTH)

    def body(q_ref, kv_ref, dy_ref, qg_ref, kg_ref, sink_ref, dqkv_ref, dqg_ref, dkg_ref, dsink_ref,
             dkn_acc, dv_acc, qg_acc, kg_acc, sink_acc, bias_ref, sinkcol_ref):
        lo = _lo_mask((BLOCK, LANES))
        qg, kg = qg_ref[...], kg_ref[...]
        _att_consts(sink_ref, bias_ref, sinkcol_ref)
        first = pl.program_id(0) == 0

        @pl.when(first)
        def _():
            qg_acc[...] = jnp.zeros_like(qg_acc)
            kg_acc[...] = jnp.zeros_like(kg_acc)
            sink_acc[...] = jnp.zeros_like(sink_acc)

        dkn_acc[...] = jnp.zeros_like(dkn_acc)
        dv_acc[...] = jnp.zeros_like(dv_acc)

        def block(i, carry):
            r0 = pl.multiple_of(i * BLOCK, BLOCK)
            rp = pl.multiple_of(jnp.maximum(i - 1, 0) * BLOCK, BLOCK)
            has_prev = i > 0
            kn_c = _head_norm(kv_ref[pl.ds(r0, BLOCK), 0:KV_WIDTH].astype(F32), kg, lo)[0].astype(BF16)
            kn_p = _head_norm(kv_ref[pl.ds(rp, BLOCK), 0:KV_WIDTH].astype(F32), kg, lo)[0].astype(BF16)
            v_c = kv_ref[pl.ds(r0, BLOCK), KV_WIDTH:2 * KV_WIDTH].astype(BF16)
            v_p = kv_ref[pl.ds(rp, BLOCK), KV_WIDTH:2 * KV_WIDTH].astype(BF16)
            dk_c, dk_p, dv_c, dv_p = [], [], [], []
            for kv in range(2):
                k2c, k2p = _dup_half(kn_c, kv, lo), _dup_half(kn_p, kv, lo)
                v2c, v2p = _dup_half(v_c, kv, lo), _dup_half(v_p, kv, lo)
                cols = [slice((2 * kv + t) * LANES, (2 * kv + t + 1) * LANES) for t in range(2)]
                normed = [_head_norm(q_ref[pl.ds(r0, BLOCK), c].astype(F32), qg, lo) for c in cols]
                q4 = _stack_heads(normed[0][0], normed[1][0], lo).astype(BF16)
                do4 = _stack_heads(dy_ref[pl.ds(r0, BLOCK), cols[0]], dy_ref[pl.ds(r0, BLOCK), cols[1]], lo)
                p_c, p_p, p_s = _att_probs(q4, k2c, k2p, bias_ref[kv, 0], bias_ref[kv, 1], sinkcol_ref[kv], has_prev)
                dp_c = _dot_nt(do4, v2c)
                dp_p = _dot_nt(do4, v2p)
                delta = jnp.sum(p_c * dp_c + p_p * dp_p, axis=-1, keepdims=True)
                ds_c = (p_c * (dp_c - delta)).astype(BF16)
                ds_p = (p_p * (dp_p - delta)).astype(BF16)
                sink_acc[kv] += -(p_s * delta)
                dq4 = (_dot_nn(ds_c, k2c) + _dot_nn(ds_p, k2p)) * ATT_SCALE
                for c, (_, qh, qr), dqn in zip(cols, normed, _unstack_heads(dq4, lo)):
                    dq, dg = _head_norm_bwd(qh, qr, qg, dqn, lo)
                    dqkv_ref[pl.ds(r0, BLOCK), c] = dq.astype(BF16)
                    qg_acc[...] += dg
                dk_c.append(_dot_tn(ds_c, q4))
                dk_p.append(_dot_tn(ds_p, q4))
                dv_c.append(_dot_tn(p_c.astype(BF16), do4))
                dv_p.append(_dot_tn(p_p.astype(BF16), do4))

            def fold(parts):
                a = parts[0] + pltpu.roll(parts[0], LANES // 2, axis=1)
                b = parts[1] + pltpu.roll(parts[1], LANES // 2, axis=1)
                return jnp.where(lo, a, b)

            dkn_acc[pl.ds(r0, BLOCK), :] += fold(dk_c) * ATT_SCALE
            dkn_acc[pl.ds(rp, BLOCK), :] += fold(dk_p) * ATT_SCALE
            dv_acc[pl.ds(r0, BLOCK), :] += fold(dv_c)
            dv_acc[pl.ds(rp, BLOCK), :] += fold(dv_p)
            return carry

        lax.fori_loop(0, nb, block, 0)

        def finish(i, carry):
            r0 = pl.multiple_of(i * BLOCK, BLOCK)
            _, kh, kr = _head_norm(kv_ref[pl.ds(r0, BLOCK), 0:KV_WIDTH].astype(F32), kg, lo)
            dk, dg = _head_norm_bwd(kh, kr, kg, dkn_acc[pl.ds(r0, BLOCK), :], lo)
            dqkv_ref[pl.ds(r0, BLOCK), ATT_WIDTH:ATT_WIDTH + KV_WIDTH] = dk.astype(BF16)
            dqkv_ref[pl.ds(r0, BLOCK), ATT_WIDTH + KV_WIDTH:QKV_WIDTH] = dv_acc[pl.ds(r0, BLOCK), :].astype(BF16)
            kg_acc[...] += dg
            return carry

        lax.fori_loop(0, nb, finish, 0)

        @pl.when(pl.program_id(0) == n_seq - 1)
        def _():
            dqg_ref[...] = jnp.sum(qg_acc[...], axis=0, keepdims=True)
            dkg_ref[...] = jnp.sum(kg_acc[...], axis=0, keepdims=True)
            lane = lax.broadcasted_iota(jnp.int32, (1, LANES), 1)
            dsink = jnp.zeros((1, LANES), F32)
            for kv in range(2):
                for r in range(Q_GROUP):
                    total = jnp.sum(sink_acc[kv, r * BLOCK:(r + 1) * BLOCK, :], axis=0, keepdims=True)
                    dsink = jnp.where(lane == Q_GROUP * kv + r, total, dsink)
            dsink_ref[...] = dsink

    vec = pl.BlockSpec((1, LANES), lambda b: (0, 0))
    acc = pltpu.VMEM((BLOCK, LANES), F32)
    body, dep_specs, dep_args = _with_deps(body, 6, deps)
    dqkv, dqg, dkg, dsink = pl.pallas_call(
        body, name=name, grid=(n_seq,),
        in_specs=[pl.BlockSpec((seq, ATT_WIDTH), lambda b: (b, qcol)),
                  pl.BlockSpec((seq, 2 * KV_WIDTH), lambda b: (b, kvcol)),
                  pl.BlockSpec((seq, ATT_WIDTH), lambda b: (b, 0)),
                  vec, vec, pl.BlockSpec(memory_space=pltpu.SMEM)] + dep_specs,
        out_specs=[pl.BlockSpec((seq, QKV_WIDTH), lambda b: (b, 0)), vec, vec, vec],
        out_shape=[jax.ShapeDtypeStruct((T, QKV_WIDTH), BF16)] + [jax.ShapeDtypeStruct((1, LANES), F32)] * 3,
        scratch_shapes=[pltpu.VMEM((seq, KV_WIDTH), F32), pltpu.VMEM((seq, KV_WIDTH), F32), acc, acc,
                        pltpu.VMEM((2, GROUP_ROWS, 1), F32), *ATT_SCRATCH],
        compiler_params=_params(("arbitrary",)),
    )(proj, proj, dy, jnp.tile(q_gain, 2).reshape(1, LANES), jnp.tile(k_gain, 2).reshape(1, LANES), sinks, *dep_args)
    half = LANES // 2
    return dqkv, dqg[0, :half] + dqg[0, half:], dkg[0, :half] + dkg[0, half:], dsink[0, :N_Q_HEADS]


def _sgu_weights(w_ref):
    r = lax.broadcasted_iota(jnp.int32, (BLOCK, BLOCK), 0)
    c = lax.broadcasted_iota(jnp.int32, (BLOCK, BLOCK), 1)
    return [jnp.where(r >= c, w_ref[g], 0.0).astype(BF16) for g in range(SGU_GROUPS)]


def _sgu_fwd(proj, gain, w_s, bias_full, *, n_seq, seq, name):
    T = n_seq * seq
    nc = seq // BLOCK

    def body(suv_ref, g_ref, w_ref, b_ref, y_ref):
        lo = _lo_mask((BLOCK, LANES))
        wm = _sgu_weights(w_ref)
        gain_v = g_ref[...]

        def chunk(c, carry):
            r0 = pl.multiple_of(c * BLOCK, BLOCK)
            gv = _gelu(suv_ref[pl.ds(r0, BLOCK), SGU_WIDTH:2 * SGU_WIDTH].astype(F32))
            r = lax.rsqrt(jnp.mean(gv * gv, axis=-1, keepdims=True) + NORM_EPS)
            vn = (gv * r * gain_v).astype(BF16)
            for p in range(SGU_WIDTH // LANES):
                cols = slice(p * LANES, (p + 1) * LANES)
                vp = vn[:, cols]
                mixed = jnp.where(lo, _dot_nn(wm[2 * p], vp), _dot_nn(wm[2 * p + 1], vp)) + b_ref[:, cols]
                u = _gelu(suv_ref[pl.ds(r0, BLOCK), cols].astype(F32))
                y_ref[pl.ds(r0, BLOCK), cols] = (u * mixed).astype(BF16)
            return carry

        lax.fori_loop(0, nc, chunk, 0)

    return pl.pallas_call(
        body, name=name, grid=(n_seq,),
        in_specs=[pl.BlockSpec((seq, 2 * SGU_WIDTH), lambda b: (b, COL_SUV // (2 * SGU_WIDTH))),
                  pl.BlockSpec((1, SGU_WIDTH), lambda b: (0, 0)),
                  pl.BlockSpec((SGU_GROUPS, BLOCK, BLOCK), lambda b: (0, 0, 0)),
                  pl.BlockSpec((BLOCK, SGU_WIDTH), lambda b: (0, 0))],
        out_specs=pl.BlockSpec((seq, SGU_WIDTH), lambda b: (b, 0)),
        out_shape=jax.ShapeDtypeStruct((T, SGU_WIDTH), BF16),
        compiler_params=_params(("parallel",)),
    )(proj, gain.reshape(1, SGU_WIDTH), w_s, bias_full)


def _sgu_bwd(proj, dy, gain, w_s, bias_full, *, n_seq, seq, name, deps=()):
    T = n_seq * seq
    nc = seq // BLOCK
    n_tiles = SGU_WIDTH // LANES

    def body(suv_ref, dy_ref, g_ref, w_ref, b_ref, dsuv_ref, dg_ref, dw_ref, db_ref, dg_acc, dw_acc, db_acc):
        lo = _lo_mask((BLOCK, LANES))
        hi = jnp.logical_not(lo)
        wm = _sgu_weights(w_ref)
        wmt = [jnp.where(lax.broadcasted_iota(jnp.int32, (BLOCK, BLOCK), 1) >= lax.broadcasted_iota(jnp.int32, (BLOCK, BLOCK), 0),
                         w_ref[g].T, 0.0).astype(BF16) for g in range(SGU_GROUPS)]
        gain_v = g_ref[...]

        @pl.when(pl.program_id(0) == 0)
        def _():
            dg_acc[...] = jnp.zeros_like(dg_acc)
            dw_acc[...] = jnp.zeros_like(dw_acc)
            db_acc[...] = jnp.zeros_like(db_acc)

        def chunk(c, carry):
            r0 = pl.multiple_of(c * BLOCK, BLOCK)
            gv, dgelu_v = _gelu_and_grad(suv_ref[pl.ds(r0, BLOCK), SGU_WIDTH:2 * SGU_WIDTH].astype(F32))
            r = lax.rsqrt(jnp.mean(gv * gv, axis=-1, keepdims=True) + NORM_EPS)
            vh = gv * r
            vn = (vh * gain_v).astype(BF16)
            dvn_tiles = []
            for p in range(n_tiles):
                cols = slice(p * LANES, (p + 1) * LANES)
                vp = vn[:, cols]
                mixed = jnp.where(lo, _dot_nn(wm[2 * p], vp), _dot_nn(wm[2 * p + 1], vp)) + b_ref[:, cols]
                u, dgelu_u = _gelu_and_grad(suv_ref[pl.ds(r0, BLOCK), cols].astype(F32))
                dyv = dy_ref[pl.ds(r0, BLOCK), cols]
                dsuv_ref[pl.ds(r0, BLOCK), cols] = (dyv * mixed * dgelu_u).astype(BF16)
                dm = dyv * u
                db_acc[:, cols] += dm
                dm_bf = dm.astype(BF16)
                dvn_tiles.append(jnp.where(lo, _dot_nn(wmt[2 * p], dm_bf), _dot_nn(wmt[2 * p + 1], dm_bf)))
                dw_acc[2 * p] += _dot_nt(jnp.where(lo, dm, 0.0).astype(BF16), vp)
                dw_acc[2 * p + 1] += _dot_nt(jnp.where(hi, dm, 0.0).astype(BF16), vp)
            dvn = jnp.concatenate(dvn_tiles, axis=1)
            dg_acc[...] += dvn * vh
            dvh = dvn * gain_v
            dgv = r * (dvh - vh * jnp.mean(dvh * vh, axis=-1, keepdims=True))
            dsuv_ref[pl.ds(r0, BLOCK), SGU_WIDTH:2 * SGU_WIDTH] = (dgv * dgelu_v).astype(BF16)
            return carry

        lax.fori_loop(0, nc, chunk, 0)

        @pl.when(pl.program_id(0) == n_seq - 1)
        def _():
            dg_ref[...] = jnp.sum(dg_acc[...], axis=0, keepdims=True)
            r = lax.broadcasted_iota(jnp.int32, (BLOCK, BLOCK), 0)
            c = lax.broadcasted_iota(jnp.int32, (BLOCK, BLOCK), 1)
            for g in range(SGU_GROUPS):
                dw_ref[g] = jnp.where(r >= c, dw_acc[g], 0.0)
            lane = lax.broadcasted_iota(jnp.int32, (BLOCK, LANES), 1)
            out = jnp.zeros((BLOCK, LANES), F32)
            for p in range(n_tiles):
                tile = db_acc[:, p * LANES:(p + 1) * LANES]
                s_lo = jnp.sum(jnp.where(lo, tile, 0.0), axis=-1, keepdims=True)
                s_hi = jnp.sum(jnp.where(hi, tile, 0.0), axis=-1, keepdims=True)
                out = jnp.where(lane == 2 * p, s_lo, out)
                out = jnp.where(lane == 2 * p + 1, s_hi, out)
            db_ref[...] = out

    body, dep_specs, dep_args = _with_deps(body, 5, deps)
    dsuv, dg, dw, db = pl.pallas_call(
        body, name=name, grid=(n_seq,),
        in_specs=[pl.BlockSpec((seq, 2 * SGU_WIDTH), lambda b: (b, COL_SUV // (2 * SGU_WIDTH))),
                  pl.BlockSpec((seq, SGU_WIDTH), lambda b: (b, 0)),
                  pl.BlockSpec((1, SGU_WIDTH), lambda b: (0, 0)),
                  pl.BlockSpec((SGU_GROUPS, BLOCK, BLOCK), lambda b: (0, 0, 0)),
                  pl.BlockSpec((BLOCK, SGU_WIDTH), lambda b: (0, 0))] + dep_specs,
        out_specs=[pl.BlockSpec((seq, 2 * SGU_WIDTH), lambda b: (b, 0)),
                   pl.BlockSpec((1, SGU_WIDTH), lambda b: (0, 0)),
                   pl.BlockSpec((SGU_GROUPS, BLOCK, BLOCK), lambda b: (0, 0, 0)),
                   pl.BlockSpec((BLOCK, LANES), lambda b: (0, 0))],
        out_shape=[jax.ShapeDtypeStruct((T, 2 * SGU_WIDTH), BF16), jax.ShapeDtypeStruct((1, SGU_WIDTH), F32),
                   jax.ShapeDtypeStruct((SGU_GROUPS, BLOCK, BLOCK), F32), jax.ShapeDtypeStruct((BLOCK, LANES), F32)],
        scratch_shapes=[pltpu.VMEM((BLOCK, SGU_WIDTH), F32), pltpu.VMEM((SGU_GROUPS, BLOCK, BLOCK), F32),
                        pltpu.VMEM((BLOCK, SGU_WIDTH), F32)],
        compiler_params=_params(("arbitrary",)),
    )(proj, dy, gain.reshape(1, SGU_WIDTH), w_s, bias_full, *dep_args)
    return dsuv, dg.reshape(SGU_WIDTH), dw, db[:, :SGU_GROUPS].T


def _merge_fwd(y_att, y_sgu, w_oa, w_ob, proj, *, name, tm=1024, tn=512, deps=()):
    T = y_att.shape[0]

    def body(ya_ref, ys_ref, wa_ref, wb_ref, ga_ref, gb_ref, o_ref):
        pa = _dot_nn(ya_ref[...], wa_ref[...])
        pb = _dot_nn(ys_ref[...], wb_ref[...])
        o_ref[...] = (_sigmoid(ga_ref[...].astype(F32)) * pa + _sigmoid(gb_ref[...].astype(F32)) * pb).astype(BF16)

    act = pl.BlockSpec((tm, ATT_WIDTH), lambda i, j: (i, 0))
    wgt = pl.BlockSpec((ATT_WIDTH, tn), lambda i, j: (0, j))
    body, dep_specs, dep_args = _with_deps(body, 6, deps)
    return pl.pallas_call(
        body, name=name, grid=(T // tm, D_MODEL // tn),
        in_specs=[act, act, wgt, wgt,
                  pl.BlockSpec((tm, tn), lambda i, j: (i, j + COL_GA // tn)),
                  pl.BlockSpec((tm, tn), lambda i, j: (i, j + COL_GB // tn))] + dep_specs,
        out_specs=pl.BlockSpec((tm, tn), lambda i, j: (i, j)),
        out_shape=jax.ShapeDtypeStruct((T, D_MODEL), BF16),
        compiler_params=_params(("parallel", "parallel")),
    )(y_att, y_sgu, w_oa, w_ob, proj, proj, *dep_args)


def _merge_bwd(dx1_bf, w_out, y_att, y_sgu, w_oa, w_ob, proj, *, name, tm=1024, tn=512):
    T = y_att.shape[0]

    def body(dx_ref, wo_ref, ya_ref, ys_ref, wa_ref, wb_ref, ga_ref, gb_ref, dpa_ref, dpb_ref, dga_ref, dgb_ref):
        dm = _dot_nt(dx_ref[...], wo_ref[...])
        pa = _dot_nn(ya_ref[...], wa_ref[...])
        pb = _dot_nn(ys_ref[...], wb_ref[...])
        sa = _sigmoid(ga_ref[...].astype(F32))
        sb = _sigmoid(gb_ref[...].astype(F32))
        dpa_ref[...] = (dm * sa).astype(BF16)
        dpb_ref[...] = (dm * sb).astype(BF16)
        dga_ref[...] = (dm * pa * sa * (1.0 - sa)).astype(BF16)
        dgb_ref[...] = (dm * pb * sb * (1.0 - sb)).astype(BF16)

    act = pl.BlockSpec((tm, ATT_WIDTH), lambda i, j: (i, 0))
    wgt = pl.BlockSpec((ATT_WIDTH, tn), lambda i, j: (0, j))
    out = pl.BlockSpec((tm, tn), lambda i, j: (i, j))
    return pl.pallas_call(
        body, name=name, grid=(T // tm, D_MODEL // tn),
        in_specs=[pl.BlockSpec((tm, D_MODEL), lambda i, j: (i, 0)),
                  pl.BlockSpec((tn, D_MODEL), lambda i, j: (j, 0)),
                  act, act, wgt, wgt,
                  pl.BlockSpec((tm, tn), lambda i, j: (i, j + COL_GA // tn)),
                  pl.BlockSpec((tm, tn), lambda i, j: (i, j + COL_GB // tn))],
        out_specs=[out] * 4,
        out_shape=[jax.ShapeDtypeStruct((T, D_MODEL), BF16)] * 4,
        compiler_params=_params(("parallel", "parallel")),
    )(dx1_bf, w_out, y_att, y_sgu, w_oa, w_ob, proj, proj)


CONV_ROWS = 256
CONV_TN = 256


def _shift_rows(cur, prev8, k):
    rolled = pltpu.roll(cur, k, axis=0)
    head = jnp.where(lax.broadcasted_iota(jnp.int32, prev8.shape, 0) < k, pltpu.roll(prev8, k, axis=0), rolled[:SUBLANES])
    return jnp.concatenate([head, rolled[SUBLANES:]], axis=0)


def _shift_rows_up(cur, next8, k):
    n = cur.shape[0]
    rolled = pltpu.roll(cur, n - k, axis=0)
    tail = jnp.where(lax.broadcasted_iota(jnp.int32, next8.shape, 0) >= SUBLANES - k,
                     pltpu.roll(next8, SUBLANES - k, axis=0), rolled[n - SUBLANES:])
    return jnp.concatenate([rolled[:n - SUBLANES], tail], axis=0)


HALO_ROWS = 16


def _rows_before(z_ref, r0, first):
    rp = pl.multiple_of(jnp.maximum(r0 - HALO_ROWS, 0), HALO_ROWS)
    halo = z_ref[pl.ds(rp, HALO_ROWS), :].astype(F32)
    return jnp.where(first, 0.0, halo[HALO_ROWS - SUBLANES:])


def _conv_rows(z_ref, r0, first, w_ref, b_ref, rows):
    cur = z_ref[pl.ds(r0, rows), :].astype(F32)
    prev8 = _rows_before(z_ref, r0, first)
    z1 = _shift_rows(cur, prev8, 1)
    z2 = _shift_rows(cur, prev8, 2)
    return b_ref[...] + w_ref[0:1, :] * z2 + w_ref[1:2, :] * z1 + w_ref[2:3, :] * cur


def _conv_fwd(z_g, z_v, cw_g, cw_v, cb_g, cb_v, *, n_seq, seq, name):
    T = n_seq * seq
    tn, rows = CONV_TN, CONV_ROWS

    def body(zg_ref, zv_ref, wg_ref, wv_ref, bg_ref, bv_ref, a_ref):
        def step(s, carry):
            r0 = pl.multiple_of(s * rows, rows)
            first = s == 0
            g = _conv_rows(zg_ref, r0, first, wg_ref, bg_ref, rows)
            v = _conv_rows(zv_ref, r0, first, wv_ref, bv_ref, rows)
            a_ref[pl.ds(r0, rows), :] = (g * _sigmoid(g) * v).astype(BF16)
            return carry

        lax.fori_loop(0, seq // rows, step, 0)

    zs = pl.BlockSpec((seq, tn), lambda b, j: (b, j))
    ws = pl.BlockSpec((3, tn), lambda b, j: (0, j))
    bs = pl.BlockSpec((1, tn), lambda b, j: (0, j))
    return pl.pallas_call(
        body, name=name, grid=(n_seq, D_FF // tn),
        in_specs=[zs, zs, ws, ws, bs, bs], out_specs=zs,
        out_shape=jax.ShapeDtypeStruct((T, D_FF), BF16),
        compiler_params=_params(("parallel", "parallel")),
    )(z_g, z_v, cw_g, cw_v, cb_g.reshape(1, D_FF), cb_v.reshape(1, D_FF))


def _conv_bwd(z_g, z_v, da, cw_g, cw_v, cb_g, cb_v, *, n_seq, seq, name):
    T = n_seq * seq
    tn, rows = CONV_TN, CONV_ROWS
    n_steps = seq // rows

    def body(zg_ref, zv_ref, da_ref, wg_ref, wv_ref, bg_ref, bv_ref,
             dzg_ref, dzv_ref, dwg_ref, dwv_ref, dbg_ref, dbv_ref, dcg_ref, dcv_ref):
        def grads(s, accs):
            r0 = pl.multiple_of(s * rows, rows)
            first = s == 0
            cur_g = zg_ref[pl.ds(r0, rows), :].astype(F32)
            cur_v = zv_ref[pl.ds(r0, rows), :].astype(F32)
            pg = _rows_before(zg_ref, r0, first)
            pv = _rows_before(zv_ref, r0, first)
            g1, g2 = _shift_rows(cur_g, pg, 1), _shift_rows(cur_g, pg, 2)
            v1, v2 = _shift_rows(cur_v, pv, 1), _shift_rows(cur_v, pv, 2)
            g = bg_ref[...] + wg_ref[0:1, :] * g2 + wg_ref[1:2, :] * g1 + wg_ref[2:3, :] * cur_g
            v = bv_ref[...] + wv_ref[0:1, :] * v2 + wv_ref[1:2, :] * v1 + wv_ref[2:3, :] * cur_v
            sg = _sigmoid(g)
            dav = da_ref[pl.ds(r0, rows), :].astype(F32)
            dcg = dav * v * (sg * (1.0 + g * (1.0 - sg)))
            dcv = dav * (g * sg)
            dcg_ref[pl.ds(r0, rows), :] = dcg
            dcv_ref[pl.ds(r0, rows), :] = dcv

            def colsum(x):
                return jnp.sum(x, axis=0, keepdims=True)

            return (accs[0] + colsum(dcg * g2), accs[1] + colsum(dcg * g1), accs[2] + colsum(dcg * cur_g), accs[3] + colsum(dcg),
                    accs[4] + colsum(dcv * v2), accs[5] + colsum(dcv * v1), accs[6] + colsum(dcv * cur_v), accs[7] + colsum(dcv))

        zero = jnp.zeros((1, tn), F32)
        sums = lax.fori_loop(0, n_steps, grads, (zero,) * 8)
        first_seq = pl.program_id(1) == 0

        @pl.when(first_seq)
        def _():
            dwg_ref[...] = jnp.concatenate(sums[0:3], axis=0)
            dbg_ref[...] = sums[3]
            dwv_ref[...] = jnp.concatenate(sums[4:7], axis=0)
            dbv_ref[...] = sums[7]

        @pl.when(jnp.logical_not(first_seq))
        def _():
            dwg_ref[...] += jnp.concatenate(sums[0:3], axis=0)
            dbg_ref[...] += sums[3]
            dwv_ref[...] += jnp.concatenate(sums[4:7], axis=0)
            dbv_ref[...] += sums[7]

        def back(s, carry):
            r0 = pl.multiple_of(s * rows, rows)
            last = s == n_steps - 1
            rn = pl.multiple_of(jnp.minimum(r0 + rows, seq - SUBLANES), SUBLANES)
            for dc_ref, w_ref, dz_ref in ((dcg_ref, wg_ref, dzg_ref), (dcv_ref, wv_ref, dzv_ref)):
                cur = dc_ref[pl.ds(r0, rows), :]
                nxt = jnp.where(last, 0.0, dc_ref[pl.ds(rn, SUBLANES), :])
                u1, u2 = _shift_rows_up(cur, nxt, 1), _shift_rows_up(cur, nxt, 2)
                dz_ref[pl.ds(r0, rows), :] = (w_ref[2:3, :] * cur + w_ref[1:2, :] * u1 + w_ref[0:1, :] * u2).astype(BF16)
            return carry

        lax.fori_loop(0, n_steps, back, 0)

    zs = pl.BlockSpec((seq, tn), lambda j, b: (b, j))
    ws = pl.BlockSpec((3, tn), lambda j, b: (0, j))
    bs = pl.BlockSpec((1, tn), lambda j, b: (0, j))
    outs = pl.pallas_call(
        body, name=name, grid=(D_FF // tn, n_seq),
        in_specs=[zs, zs, zs, ws, ws, bs, bs],
        out_specs=[zs, zs, ws, ws, bs, bs],
        out_shape=[jax.ShapeDtypeStruct((T, D_FF), BF16)] * 2 + [jax.ShapeDtypeStruct((3, D_FF), F32)] * 2
        + [jax.ShapeDtypeStruct((1, D_FF), F32)] * 2,
        scratch_shapes=[pltpu.VMEM((seq, tn), F32), pltpu.VMEM((seq, tn), F32)],
        compiler_params=_params(("parallel", "arbitrary")),
    )(z_g, z_v, da, cw_g, cw_v, cb_g.reshape(1, D_FF), cb_v.reshape(1, D_FF))
    dz_g, dz_v, dw_g, dw_v, db_g, db_v = outs
    return dz_g, dz_v, dw_g, dw_v, db_g.reshape(D_FF), db_v.reshape(D_FF)


def _loss_head(y, target, *, name, tm=512):
    T, D = y.shape

    def body(y_ref, t_ref, dy_ref, dyb_ref, l_ref):
        err = y_ref[...] - t_ref[...]
        dyv = err * (1.0 / D)
        dy_ref[...] = dyv
        dyb_ref[...] = dyv.astype(BF16)
        part = jnp.sum(jnp.sum(err * err, axis=0, keepdims=True), axis=1, keepdims=True) * (0.5 / D)

        @pl.when(pl.program_id(0) == 0)
        def _():
            l_ref[...] = jnp.broadcast_to(part, l_ref.shape)

        @pl.when(pl.program_id(0) > 0)
        def _():
            l_ref[...] += jnp.broadcast_to(part, l_ref.shape)

    row = pl.BlockSpec((tm, D), lambda i: (i, 0))
    dy, dyb, l = pl.pallas_call(
        body, name=name, grid=(T // tm,),
        in_specs=[row, row],
        out_specs=[row, row, pl.BlockSpec((SUBLANES, LANES), lambda i: (0, 0))],
        out_shape=[jax.ShapeDtypeStruct((T, D), F32), jax.ShapeDtypeStruct((T, D), BF16),
                   jax.ShapeDtypeStruct((SUBLANES, LANES), F32)],
        compiler_params=_params(("arbitrary",)),
    )(y, target)
    return l[0, 0], dy, dyb


def _cast_bf16(x, *, name, tm=512):
    T, D = x.shape

    def body(x_ref, o_ref):
        o_ref[...] = x_ref[...].astype(BF16)

    row = pl.BlockSpec((tm, D), lambda i: (i, 0))
    return pl.pallas_call(body, name=name, grid=(T // tm,), in_specs=[row], out_specs=row,
                          out_shape=jax.ShapeDtypeStruct((T, D), BF16), compiler_params=_params(("parallel",)))(x)


def _layer_fwd(x, w, sched, *, n_seq, seq, l):
    tag = f"l{l}"
    deps = sched("fwd_start", l, x)
    h = _rms_fwd(x, w["mix_norm"], name=f"{tag}_mix_norm")
    proj = _mm(h, w["w_in"], mode="nn", out_dtype=ACT_DTYPE, name=f"{tag}_proj", deps=deps)
    y_att = _attention_fwd(proj, w["q_norm"], w["k_norm"], w["sinks"], n_seq=n_seq, seq=seq, name=f"{tag}_att")
    deps = sched("fwd_att", l, y_att)
    y_sgu = _sgu_fwd(proj, w["sgu_norm"], w["w_s"], w["bias_full"], n_seq=n_seq, seq=seq, name=f"{tag}_sgu")
    merged = _merge_fwd(y_att, y_sgu, w["w_oa"], w["w_ob"], proj, name=f"{tag}_merge", deps=deps)
    x1 = _mm(merged, w["w_out"], mode="nn", out_dtype=F32, name=f"{tag}_out",
             epilogue=_add, extras=(x,))
    deps = sched("fwd_mixer_done", l, x1)
    h2 = _rms_fwd(x1, w["ffn_norm"], name=f"{tag}_ffn_norm")
    z_g = _mm(h2, w["w_up_g"], mode="nn", out_dtype=ACT_DTYPE, name=f"{tag}_up_g", deps=deps)
    z_v = _mm(h2, w["w_up_v"], mode="nn", out_dtype=ACT_DTYPE, name=f"{tag}_up_v")
    a = _conv_fwd(z_g, z_v, w["cw_g"], w["cw_v"], w["cb_g"], w["cb_v"], n_seq=n_seq, seq=seq, name=f"{tag}_conv")
    deps = sched("fwd_conv", l, a)
    x2 = _mm(a, w["w_down"], mode="nn", out_dtype=F32, name=f"{tag}_down",
             epilogue=_add, extras=(x1,), deps=deps)
    saved = dict(x=x, h=h, proj=proj, y_att=y_att, y_sgu=y_sgu, merged=merged, x1=x1, h2=h2, z_g=z_g, z_v=z_v, a=a)
    return x2, saved


def _layer_bwd(dx2, dx2_bf, w, s, sched, *, n_seq, seq, l):
    tag = f"l{l}b"
    g = {}
    da = _mm(dx2_bf, w["w_down"], mode="nt", out_dtype=ACT_DTYPE, name=f"{tag}_da")
    g["w_down"] = _mm(s["a"], dx2_bf, mode="tn", out_dtype=F32, name=f"{tag}_dw_down")
    dz_g, dz_v, g["cw_g"], g["cw_v"], g["cb_g"], g["cb_v"] = _conv_bwd(
        s["z_g"], s["z_v"], da, w["cw_g"], w["cw_v"], w["cb_g"], w["cb_v"], n_seq=n_seq, seq=seq, name=f"{tag}_conv")
    dh2 = _mm(dz_g, w["w_up_g"], mode="nt", out_dtype=F32, name=f"{tag}_dh2_g")
    dh2 = _mm(dz_v, w["w_up_v"], mode="nt", out_dtype=F32, name=f"{tag}_dh2_v",
              epilogue=_add, extras=(dh2,))
    g["w_up_g"] = _mm(s["h2"], dz_g, mode="tn", out_dtype=F32, name=f"{tag}_dw_up_g")
    g["w_up_v"] = _mm(s["h2"], dz_v, mode="tn", out_dtype=F32, name=f"{tag}_dw_up_v")
    deps = sched("bwd_ffn_grads", l, dh2, g)
    dx1, g["ffn_norm"] = _rms_bwd(s["x1"], w["ffn_norm"], dh2, dx2, name=f"{tag}_ffn_norm", deps=deps)
    dx1_bf = _cast_bf16(dx1, name=f"{tag}_dx1_bf")
    dpa, dpb, dga, dgb = _merge_bwd(dx1_bf, w["w_out"], s["y_att"], s["y_sgu"], w["w_oa"], w["w_ob"], s["proj"],
                                    name=f"{tag}_merge")
    deps = sched("bwd_merge", l, dpa)
    g["w_out"] = _mm(s["merged"], dx1_bf, mode="tn", out_dtype=F32, name=f"{tag}_dw_out",
                     deps=deps)
    dy_att = _mm(dpa, w["w_oa"], mode="nt", out_dtype=BF16, name=f"{tag}_dy_att")
    dy_sgu = _mm(dpb, w["w_ob"], mode="nt", out_dtype=F32, name=f"{tag}_dy_sgu")
    g["w_oa"] = _mm(s["y_att"], dpa, mode="tn", out_dtype=F32, name=f"{tag}_dw_oa")
    g["w_ob"] = _mm(s["y_sgu"], dpb, mode="tn", out_dtype=F32, name=f"{tag}_dw_ob")
    deps = sched("bwd_out_grads", l, dy_att, g)
    dqkv, g["q_norm"], g["k_norm"], g["sinks"] = _attention_bwd(
        s["proj"], dy_att, w["q_norm"], w["k_norm"], w["sinks"], n_seq=n_seq, seq=seq, name=f"{tag}_att", deps=deps)
    deps = sched("bwd_att", l, dqkv)
    dsuv, g["sgu_norm"], g["w_s"], g["b_s"] = _sgu_bwd(
        s["proj"], dy_sgu, w["sgu_norm"], w["w_s"], w["bias_full"], n_seq=n_seq, seq=seq, name=f"{tag}_sgu", deps=deps)
    dproj = jnp.concatenate([dsuv, dga, dgb, dqkv], axis=1)
    g["w_in"] = _mm(s["h"], dproj, mode="tn", out_dtype=F32, name=f"{tag}_dw_in")
    deps = sched("bwd_w_in_grad", l, dproj, g)
    dh = _mm(dproj, w["w_in"], mode="nt", out_dtype=F32, name=f"{tag}_dh", deps=deps)
    deps = sched("bwd_dh", l, dh)
    dx, g["mix_norm"] = _rms_bwd(s["x"], w["mix_norm"], dh, dx1, name=f"{tag}_mix_norm", deps=deps)
    return dx, g


def _local_step(x, target, weights, sched, *, n_seq, seq):
    depth = len(weights)
    saved = []
    h = x
    for l in range(depth):
        h, s = _layer_fwd(h, weights[l], sched, n_seq=n_seq, seq=seq, l=l)
        saved.append(s)
    loss, dy, dy_bf = _loss_head(h, target, name="loss_head")
    grads = [None] * depth
    for l in reversed(range(depth)):
        if l < depth - 1:
            dy_bf = _cast_bf16(dy, name=f"l{l}b_dx2_bf")
        dy, grads[l] = _layer_bwd(dy, dy_bf, weights[l], saved[l], sched, n_seq=n_seq, seq=seq, l=l)
    return loss, dy, grads


W_IN_SHARD = IN_WIDTH // N_DEV
W_UP_SHARD = 2 * D_FF // N_DEV
COL_MOVE_ROWS = 256


def _w_in_moves():
    moves = []
    for j in range(N_DEV):
        a, b = j * W_IN_SHARD, (j + 1) * W_IN_SHARD
        if a < QKV_WIDTH:
            moves.append((j, 0, min(b, QKV_WIDTH) - a, 0, a + REST_WIDTH))
        if b > QKV_WIDTH:
            lo = max(a, QKV_WIDTH)
            moves.append((j, lo - a, b - a, 0, lo - QKV_WIDTH))
    return tuple(moves)


def _w_up_moves():
    half = N_DEV // 2
    return tuple((j, 0, W_UP_SHARD, j // half, (j % half) * W_UP_SHARD) for j in range(N_DEV))


def _w_o_moves():
    return tuple((j, 0, LANES, 0, j * LANES) for j in range(N_DEV))


def _assemble(blocks, widths, moves, *, name):
    _, R, w = blocks.shape
    tr = min(R, COL_MOVE_ROWS)

    def body(b_ref, *o_refs):
        for j, lo, hi, which, at in moves:
            o_refs[which][:, at:at + hi - lo] = b_ref[j, :, lo:hi]

    return pl.pallas_call(
        body, name=name, grid=(R // tr,),
        in_specs=[pl.BlockSpec((N_DEV, tr, w), lambda i: (0, i, 0))],
        out_specs=[pl.BlockSpec((tr, n), lambda i: (i, 0)) for n in widths],
        out_shape=[jax.ShapeDtypeStruct((R, n), blocks.dtype) for n in widths],
        compiler_params=_params(("parallel",)),
    )(blocks)


def _disassemble(mats, w, moves, *, name):
    R = mats[0].shape[0]
    tr = min(R, COL_MOVE_ROWS)
    n = len(mats)

    def body(*refs):
        m_refs, o_ref = refs[:n], refs[n]
        for j, lo, hi, which, at in moves:
            o_ref[j, :, lo:hi] = m_refs[which][:, at:at + hi - lo]

    return pl.pallas_call(
        body, name=name, grid=(R // tr,),
        in_specs=[pl.BlockSpec((tr, m.shape[1]), lambda i: (i, 0)) for m in mats],
        out_specs=pl.BlockSpec((N_DEV, tr, w), lambda i: (0, i, 0)),
        out_shape=jax.ShapeDtypeStruct((N_DEV, R, w), mats[0].dtype),
        compiler_params=_params(("parallel",)),
    )(*mats)


def _my_place():
    return lax.axis_index("x"), lax.axis_index("y"), lax.axis_index("c")


def _gathered_shape(shape, kind):
    r, c = shape
    return {"blocks": (N_DEV, r, c), "rows": (N_DEV * r, c), "cols": (r, N_DEV * c)}[kind]


def _gather_window(ref, kind, shape, j):
    r, c = shape
    if kind == "blocks":
        return ref.at[j]
    if kind == "rows":
        return ref.at[pl.ds(pl.multiple_of(j * r, r), r), :]
    return ref.at[:, pl.ds(pl.multiple_of(j * c, c), c)]


def _gather(srcs, kinds, *, name):
    n = len(srcs)
    shapes = [s.shape for s in srcs]
    per = 7

    def body(*refs):
        src_refs, dst_refs = refs[:n], refs[n:2 * n]
        send_sems, recv_sems, local_sems = refs[2 * n:]
        x, y, c = _my_place()
        me, sibling = (x, y, c), (x, y, 1 - c)
        chips = [(1 - x, y), (x, 1 - y), (1 - x, 1 - y)]

        def at(i, px, py, pc):
            return _gather_window(dst_refs[i], kinds[i], shapes[i], 4 * px + 2 * py + pc)

        def copy(i, k, block, to, src=None):
            return pltpu.make_async_remote_copy(
                src_ref=at(i, *block) if src is None else src, dst_ref=at(i, *block),
                send_sem=send_sems.at[per * i + k], recv_sem=recv_sems.at[per * i + k], device_id=to, device_id_type=MESH)

        mine = [pltpu.make_async_copy(src_refs[i], at(i, *me), local_sems.at[i]) for i in range(n)]
        for cp in mine:
            cp.start()
        started = []
        for i in range(n):
            first = [copy(i, 0, me, sibling, src=src_refs[i])]
            first += [copy(i, 1 + j, me, (*chip, c), src=src_refs[i]) for j, chip in enumerate(chips)]
            for cp in first:
                cp.start()
            started += first
        for i in range(n):
            for j, chip in enumerate(chips):
                copy(i, 1 + j, (*chip, c), me).wait_recv()
                fwd = copy(i, 4 + j, (*chip, c), sibling)
                fwd.start()
                started.append(fwd)
        for i in range(n):
            copy(i, 0, sibling, me).wait_recv()
            for j, chip in enumerate(chips):
                copy(i, 4 + j, (*chip, 1 - c), me).wait_recv()
        for cp in started:
            cp.wait_send()
        for cp in mine:
            cp.wait()

    return pl.pallas_call(
        body, name=name,
        out_shape=[jax.ShapeDtypeStruct(_gathered_shape(s.shape, k), s.dtype) for s, k in zip(srcs, kinds)],
        in_specs=[ANY] * n, out_specs=[ANY] * n,
        scratch_shapes=[pltpu.SemaphoreType.DMA((per * n,)), pltpu.SemaphoreType.DMA((per * n,)),
                        pltpu.SemaphoreType.DMA((n,))],
    )(*srcs)


HBM = pl.BlockSpec(memory_space=pltpu.HBM)
SEM = pl.BlockSpec(memory_space=pltpu.SEMAPHORE)
TOKEN = jax.ShapeDtypeStruct((SUBLANES, LANES), F32)
TOKEN_SPEC = pl.BlockSpec(memory_space=pltpu.VMEM)
SPLIT_PARAMS = pltpu.CompilerParams(has_side_effects=pltpu.SideEffectType.DATAFLOW_SIDE_EFFECTING)


def _in_hbm(x):
    return pltpu.with_memory_space_constraint(x, pltpu.HBM)


def _hbm_like(shape, dtype):
    return pltpu.HBM(shape, dtype)


def _place_own(shards, kinds, dtypes, *, name):
    n = len(shards)
    shapes = [s.shape for s in shards]

    def body(*refs):
        s_refs, land_refs, bufs, sems = refs[:n], refs[n:2 * n], refs[2 * n:3 * n], refs[3 * n]
        x, y, c = _my_place()
        copies = []
        for i in range(n):
            bufs[i][...] = s_refs[i][...].astype(dtypes[i])
            copies.append(pltpu.make_async_copy(
                bufs[i], _gather_window(land_refs[i], kinds[i], shapes[i], 4 * x + 2 * y + c), sems.at[i]))
        for cp in copies:
            cp.start()
        for cp in copies:
            cp.wait()

    return pl.pallas_call(
        body, name=name,
        out_shape=[jax.ShapeDtypeStruct(_gathered_shape(s, k), d) for s, k, d in zip(shapes, kinds, dtypes)],
        in_specs=[pl.BlockSpec(memory_space=pltpu.VMEM)] * n, out_specs=[ANY] * n,
        scratch_shapes=[pltpu.VMEM(s, d) for s, d in zip(shapes, dtypes)] + [pltpu.SemaphoreType.DMA((n,))],
        compiler_params=_params(),
    )(*shards)


def _gather_start(lands, kinds, shapes, after=(), *, name):
    n = len(lands)
    n_after = len(after)

    def body(*refs):
        land_refs = refs[:n]
        send_sems, recv_sems = refs[n + n_after], refs[n + n_after + 1]
        x, y, c = _my_place()
        targets = [(x, y, 1 - c), (1 - x, y, c), (x, 1 - y, c), (1 - x, 1 - y, c)]
        for i in range(n):
            own = _gather_window(land_refs[i], kinds[i], shapes[i], 4 * x + 2 * y + c)
            for k, to in enumerate(targets):
                pltpu.make_async_remote_copy(
                    src_ref=own, dst_ref=own, send_sem=send_sems.at[4 * i + k], recv_sem=recv_sems.at[4 * i + k],
                    device_id=to, device_id_type=MESH).start()
        refs[-1][...] = jnp.zeros_like(refs[-1])

    outs = pl.pallas_call(
        body, name=name,
        out_shape=[pltpu.SemaphoreType.DMA((4 * n,)), pltpu.SemaphoreType.DMA((4 * n,))]
        + [_hbm_like(a.shape, a.dtype) for a in lands] + [TOKEN],
        in_specs=[HBM] * n + [ANY] * n_after, out_specs=[SEM, SEM] + [HBM] * n + [TOKEN_SPEC],
        input_output_aliases={i: 2 + i for i in range(n)},
        compiler_params=SPLIT_PARAMS,
    )(*[_in_hbm(a) for a in lands], *after)
    return outs[0], outs[1], outs[2:2 + n], outs[-1]


def _gather_forward(recv_sems, lands, kinds, shapes, after, *, name):
    n = len(lands)

    def body(*refs):
        recv_ref, land_refs = refs[0], refs[1:1 + n]
        fwd_send, fwd_recv = refs[2 + n], refs[3 + n]
        token = refs[-1]
        x, y, c = _my_place()
        chips = [(1 - x, y), (x, 1 - y), (1 - x, 1 - y)]
        for i in range(n):
            for j, (px, py) in enumerate(chips):
                block = _gather_window(land_refs[i], kinds[i], shapes[i], 4 * px + 2 * py + c)
                pltpu.make_async_remote_copy(
                    src_ref=block, dst_ref=block, send_sem=fwd_send.at[3 * i + j], recv_sem=recv_ref.at[4 * i + 1 + j],
                    device_id=(px, py, c), device_id_type=MESH).wait_recv()
                pltpu.make_async_remote_copy(
                    src_ref=block, dst_ref=block, send_sem=fwd_send.at[3 * i + j], recv_sem=fwd_recv.at[3 * i + j],
                    device_id=(x, y, 1 - c), device_id_type=MESH).start()
        token[...] = jnp.zeros_like(token)

    outs = pl.pallas_call(
        body, name=name,
        out_shape=[pltpu.SemaphoreType.DMA((3 * n,)), pltpu.SemaphoreType.DMA((3 * n,))]
        + [_hbm_like(a.shape, a.dtype) for a in lands] + [TOKEN],
        in_specs=[SEM] + [HBM] * n + [ANY], out_specs=[SEM, SEM] + [HBM] * n + [TOKEN_SPEC],
        input_output_aliases={1 + i: 2 + i for i in range(n)},
        compiler_params=SPLIT_PARAMS,
    )(recv_sems, *lands, after)
    return outs[0], outs[1], outs[2:2 + n], outs[-1]


def _gather_finish(send_sems, recv_sems, fwd_send, fwd_recv, lands, kinds, shapes, after, *, name):
    n = len(lands)

    def body(*refs):
        send_ref, recv_ref, fsend_ref, frecv_ref = refs[:4]
        land_refs = refs[4:4 + n]
        x, y, c = _my_place()
        chips = [(1 - x, y), (x, 1 - y), (1 - x, 1 - y)]
        sibling = (x, y, 1 - c)
        for i in range(n):
            def window(j):
                return _gather_window(land_refs[i], kinds[i], shapes[i], j)

            mine, theirs = window(4 * x + 2 * y + c), window(4 * x + 2 * y + (1 - c))
            pltpu.make_async_remote_copy(src_ref=mine, dst_ref=theirs, send_sem=send_ref.at[4 * i],
                                         recv_sem=recv_ref.at[4 * i], device_id=sibling, device_id_type=MESH).wait_recv()
            for j, (px, py) in enumerate(chips):
                block = window(4 * px + 2 * py + (1 - c))
                pltpu.make_async_remote_copy(src_ref=block, dst_ref=block, send_sem=fsend_ref.at[3 * i + j],
                                             recv_sem=frecv_ref.at[3 * i + j], device_id=sibling,
                                             device_id_type=MESH).wait_recv()
            for k in range(4):
                pltpu.make_async_remote_copy(src_ref=mine, dst_ref=mine, send_sem=send_ref.at[4 * i + k],
                                             recv_sem=recv_ref.at[4 * i + k], device_id=sibling,
                                             device_id_type=MESH).wait_send()
            for j, (px, py) in enumerate(chips):
                block = window(4 * px + 2 * py + c)
                pltpu.make_async_remote_copy(src_ref=block, dst_ref=block, send_sem=fsend_ref.at[3 * i + j],
                                             recv_sem=frecv_ref.at[3 * i + j], device_id=sibling,
                                             device_id_type=MESH).wait_send()

    return pl.pallas_call(
        body, name=name,
        out_shape=[_hbm_like(a.shape, a.dtype) for a in lands],
        in_specs=[SEM] * 4 + [HBM] * n + [ANY], out_specs=[HBM] * n,
        input_output_aliases={4 + i: i for i in range(n)},
        compiler_params=SPLIT_PARAMS,
    )(send_sems, recv_sems, fwd_send, fwd_recv, *lands, after)


def _pair_plan(src_ref, land_ref, x, y, c):
    return [(src_ref.at[2 * k + (1 - c)], land_ref.at[k], (x, y, 1 - c)) for k in range(N_CHIPS)]


def _chip_plan(src_ref, land_ref, x, y, c):
    chips = [(1 - x, y), (x, 1 - y), (1 - x, 1 - y)]
    return [(src_ref.at[2 * px + py], land_ref.at[k], (px, py, c)) for k, (px, py) in enumerate(chips)]


def _exchange_copies(plan, per, src_refs, land_refs, send_sems, recv_sems):
    x, y, c = _my_place()
    copies = []
    for i, (s_ref, l_ref) in enumerate(zip(src_refs, land_refs)):
        for q, (src, dst, to) in enumerate(plan(s_ref, l_ref, x, y, c)):
            copies.append(pltpu.make_async_remote_copy(
                src_ref=src, dst_ref=dst, send_sem=send_sems.at[per * i + q], recv_sem=recv_sems.at[per * i + q],
                device_id=to, device_id_type=MESH))
    return copies


def _exchange_start(srcs, plan, per, *, name):
    n = len(srcs)

    def body(*refs):
        src_refs, land_refs = refs[:n], refs[n:2 * n]
        send_sems, recv_sems = refs[2 * n], refs[2 * n + 1]
        for cp in _exchange_copies(plan, per, src_refs, land_refs, send_sems, recv_sems):
            cp.start()
        refs[-1][...] = jnp.zeros_like(refs[-1])

    lands = [lax.empty((per,) + s.shape[1:], s.dtype) for s in srcs]
    outs = pl.pallas_call(
        body, name=name,
        out_shape=[pltpu.SemaphoreType.DMA((per * n,)), pltpu.SemaphoreType.DMA((per * n,))]
        + [_hbm_like(s.shape, s.dtype) for s in srcs] + [_hbm_like(a.shape, a.dtype) for a in lands] + [TOKEN],
        in_specs=[HBM] * (2 * n), out_specs=[SEM, SEM] + [HBM] * (2 * n) + [TOKEN_SPEC],
        input_output_aliases={i: 2 + i for i in range(2 * n)},
        compiler_params=SPLIT_PARAMS,
    )(*[_in_hbm(s) for s in srcs], *[_in_hbm(a) for a in lands])
    return outs[0], outs[1], outs[2:2 + n], outs[2 + n:2 + 2 * n], outs[-1]


def _exchange_wait(send_sems, recv_sems, srcs, lands, plan, per, after, *, name):
    n = len(srcs)

    def body(*refs):
        send_ref, recv_ref = refs[0], refs[1]
        src_refs, land_refs = refs[2:2 + n], refs[2 + n:2 + 2 * n]
        copies = _exchange_copies(plan, per, src_refs, land_refs, send_ref, recv_ref)
        for cp in copies:
            cp.wait_recv()
        for cp in copies:
            cp.wait_send()

    outs = pl.pallas_call(
        body, name=name,
        out_shape=[_hbm_like(s.shape, s.dtype) for s in srcs] + [_hbm_like(a.shape, a.dtype) for a in lands],
        in_specs=[SEM, SEM] + [HBM] * (2 * n) + [ANY], out_specs=[HBM] * (2 * n),
        input_output_aliases={2 + i: i for i in range(2 * n)},
        compiler_params=SPLIT_PARAMS,
    )(send_sems, recv_sems, *srcs, *lands, after)
    return outs[:n], outs[n:]


REDUCE_BLOCK_BYTES = 1 << 20


def _row_tile(r, c):
    row_bytes = 4 * (-(-c // LANES) * LANES)
    best = r
    for d in range(SUBLANES, r, SUBLANES):
        if r % d == 0 and d * row_bytes <= REDUCE_BLOCK_BYTES:
            best = d
    return best if r * row_bytes > REDUCE_BLOCK_BYTES else r


def _reduce_pair_sum(blocked, recv, place, wire_dtype, *, name):
    _, r, c = blocked.shape
    tr = _row_tile(r, c)

    def body(place_ref, g_ref, r_ref, own_ref, send_ref):
        s = g_ref[...] + r_ref[...]
        send_ref[...] = s.astype(wire_dtype)

        @pl.when(pl.program_id(1) == place_ref[1])
        def _():
            own_ref[...] = s

    return pl.pallas_call(
        body, name=name,
        grid_spec=pltpu.PrefetchScalarGridSpec(
            num_scalar_prefetch=1, grid=(r // tr, N_CHIPS),
            in_specs=[pl.BlockSpec((None, None, tr, c), lambda i, k, place_ref: (k, place_ref[0], i, 0)),
                      pl.BlockSpec((None, tr, c), lambda i, k, place_ref: (k, i, 0))],
            out_specs=[pl.BlockSpec((tr, c), lambda i, k, place_ref: (i, 0)),
                       pl.BlockSpec((None, tr, c), lambda i, k, place_ref: (k, i, 0))]),
        out_shape=[jax.ShapeDtypeStruct((r, c), F32), jax.ShapeDtypeStruct((N_CHIPS, r, c), wire_dtype)],
        compiler_params=_params(("parallel", "arbitrary")),
    )(place, blocked.reshape(N_CHIPS, 2, r, c), recv)


def _chip_sum(own_ref, r_ref):
    return ((own_ref[...] + r_ref[0].astype(F32)) + r_ref[1].astype(F32)) + r_ref[2].astype(F32)


def _reduce_chip_sum(own, recv, *, name):
    r, c = own.shape
    tr = _row_tile(r, c)

    def body(own_ref, r_ref, o_ref):
        o_ref[...] = _chip_sum(own_ref, r_ref)

    return pl.pallas_call(
        body, name=name, grid=(r // tr,),
        in_specs=[pl.BlockSpec((tr, c), lambda i: (i, 0)), pl.BlockSpec((N_CHIPS - 1, tr, c), lambda i: (0, i, 0))],
        out_specs=pl.BlockSpec((tr, c), lambda i: (i, 0)),
        out_shape=jax.ShapeDtypeStruct((r, c), F32),
        compiler_params=_params(("parallel",)),
    )(own, recv)


def _adamw_math(w, g, m, v):
    nm = ADAM_B1 * m + (1.0 - ADAM_B1) * g
    nv = ADAM_B2 * v + (1.0 - ADAM_B2) * (g * g)
    m_hat = nm / (1.0 - ADAM_B1 ** ADAM_STEP)
    v_hat = nv / (1.0 - ADAM_B2 ** ADAM_STEP)
    return -ADAM_LR * (m_hat / (jnp.sqrt(v_hat) + ADAM_EPS) + ADAM_WD * w), nm, nv


def _adamw(w, g, m, v, *, name):
    shape = w.shape
    C = shape[-1]
    R = math.prod(shape[:-1])
    tr = _row_tile(R, C)

    def body(w_ref, g_ref, m_ref, v_ref, d_ref, nm_ref, nv_ref):
        d_ref[...], nm_ref[...], nv_ref[...] = _adamw_math(w_ref[...], g_ref[...], m_ref[...], v_ref[...])

    spec = pl.BlockSpec((tr, C), lambda i: (i, 0))
    outs = pl.pallas_call(
        body, name=name, grid=(R // tr,),
        in_specs=[spec] * 4, out_specs=[spec] * 3,
        out_shape=[jax.ShapeDtypeStruct((R, C), F32)] * 3,
        compiler_params=_params(("parallel",)),
    )(*[a.reshape(R, C) for a in (w, g, m, v)])
    return tuple(o.reshape(shape) for o in outs)


def _reduce_adamw(own, recv, w, m, v, layer, prev, *, name):
    r, c = own.shape
    tr = _row_tile(r, c)
    n_prev = 0 if prev is None else len(prev)

    def body(own_ref, r_ref, w_ref, m_ref, v_ref, *rest):
        g_ref, d_ref, nm_ref, nv_ref = rest[n_prev:]
        g = _chip_sum(own_ref, r_ref)
        g_ref[...] = g
        d_ref[...], nm_ref[...], nv_ref[...] = _adamw_math(w_ref[...], g, m_ref[...], v_ref[...])

    slot = pl.BlockSpec((None, tr, c), lambda i: (layer, i, 0))
    return pl.pallas_call(
        body, name=name, grid=(r // tr,),
        in_specs=[pl.BlockSpec((tr, c), lambda i: (i, 0)), pl.BlockSpec((N_CHIPS - 1, tr, c), lambda i: (0, i, 0)),
                  slot, slot, slot] + [ANY] * n_prev,
        out_specs=[slot] * 4,
        out_shape=[jax.ShapeDtypeStruct((DEPTH, r, c), F32)] * 4,
        input_output_aliases={5 + k: k for k in range(n_prev)},
        compiler_params=_params(("parallel",)),
    )(own, recv, w, m, v, *(prev or ()))


REPLICATED = (("mix_norm", (D_MODEL,)), ("q_norm", (HEAD_DIM,)), ("k_norm", (HEAD_DIM,)), ("sinks", (N_Q_HEADS,)),
              ("sgu_norm", (SGU_WIDTH,)), ("w_s", (SGU_GROUPS, BLOCK, BLOCK)), ("b_s", (SGU_GROUPS, BLOCK)),
              ("ffn_norm", (D_MODEL,)), ("conv_b", (2 * D_FF,)))
SHARDED = (("w_in", "blocks"), ("w_oa", "cols"), ("w_ob", "cols"), ("w_out", "rows"), ("w_up", "blocks"),
           ("conv_w", "blocks"), ("w_down", "rows"))
WEIGHT_ORDER = ("mix_norm", "w_in", "q_norm", "k_norm", "sinks", "sgu_norm", "w_s", "b_s", "w_oa", "w_ob", "w_out",
                "ffn_norm", "w_up", "conv_w", "conv_b", "w_down")
MIXER_WEIGHTS = ["w_in", "w_oa", "w_ob", "w_out"]
FFN_WEIGHTS = ["w_up", "conv_w", "w_down"]


def _small_layout():
    segs, off = {}, 0
    for l in range(DEPTH):
        for name, shape in REPLICATED:
            n = math.prod(shape)
            segs[(l, name)] = (off, n)
            off += n
    per_dev = -(-off // (N_DEV * SUBLANES * LANES)) * SUBLANES * LANES
    return segs, off, per_dev


def _pack_small(grads):
    ssegs, total, per_dev = _small_layout()
    flat = jnp.concatenate([grads[l][name].reshape(-1) for (l, name) in ssegs])
    return jnp.pad(flat, (0, N_DEV * per_dev - total)).reshape(N_DEV, per_dev // LANES, LANES)


def _unpack_small(gathered):
    ssegs, _, _ = _small_layout()
    flat = gathered.reshape(-1)
    shapes = dict(REPLICATED)
    return {name: jnp.stack([flat[ssegs[(l, name)][0]:ssegs[(l, name)][0] + ssegs[(l, name)][1]].reshape(shapes[name])
                             for l in range(DEPTH)]) for name, _ in REPLICATED}


def kernel(x, mix_norm, w_in, q_norm, k_norm, sinks, sgu_norm, w_s, b_s, w_oa, w_ob, w_out, ffn_norm, w_up, conv_w, conv_b, w_down, loss_target, m_mix_norm, m_w_in, m_q_norm, m_k_norm, m_sinks, m_sgu_norm, m_w_s, m_b_s, m_w_oa, m_w_ob, m_w_out, m_ffn_norm, m_w_up, m_conv_w, m_conv_b, m_w_down, v_mix_norm, v_w_in, v_q_norm, v_k_norm, v_sinks, v_sgu_norm, v_w_s, v_b_s, v_w_oa, v_w_ob, v_w_out, v_ffn_norm, v_w_up, v_conv_w, v_conv_b, v_w_down):
    W = dict(mix_norm=mix_norm, w_in=w_in, q_norm=q_norm, k_norm=k_norm, sinks=sinks, sgu_norm=sgu_norm, w_s=w_s, b_s=b_s,
             w_oa=w_oa, w_ob=w_ob, w_out=w_out, ffn_norm=ffn_norm, w_up=w_up, conv_w=conv_w, conv_b=conv_b, w_down=w_down)
    M = dict(mix_norm=m_mix_norm, w_in=m_w_in, q_norm=m_q_norm, k_norm=m_k_norm, sinks=m_sinks, sgu_norm=m_sgu_norm,
             w_s=m_w_s, b_s=m_b_s, w_oa=m_w_oa, w_ob=m_w_ob, w_out=m_w_out, ffn_norm=m_ffn_norm, w_up=m_w_up,
             conv_w=m_conv_w, conv_b=m_conv_b, w_down=m_w_down)
    V = dict(mix_norm=v_mix_norm, w_in=v_w_in, q_norm=v_q_norm, k_norm=v_k_norm, sinks=v_sinks, sgu_norm=v_sgu_norm,
             w_s=v_w_s, b_s=v_b_s, w_oa=v_w_oa, w_ob=v_w_ob, w_out=v_w_out, ffn_norm=v_ffn_norm, w_up=v_w_up,
             conv_w=v_conv_w, conv_b=v_conv_b, w_down=v_w_down)
    n_seq, seq, d_model = x.shape
    tokens = n_seq * seq
    mx, my, mc = _my_place()
    place = jnp.stack([mc, 2 * mx + my]).astype(jnp.int32)
    half = N_DEV // 2
    kind_of = dict(SHARDED)

    gather_groups = [[(l, n) for n in names] for l in range(DEPTH) for names in (MIXER_WEIGHTS, FFN_WEIGHTS)]
    started, in_flight = {}, {}
    weights = []
    for l in range(DEPTH):
        w = {name: W[name][l] for name, _ in REPLICATED}
        w["cb_g"], w["cb_v"] = W["conv_b"][l][:D_FF], W["conv_b"][l][D_FF:]
        w["bias_full"] = jnp.repeat(W["b_s"][l].T, SGU_WIDTH // SGU_GROUPS, axis=1)
        weights.append(w)

    def gather_start(gi, after=()):
        shards = [W[name][l] for l, name in gather_groups[gi]]
        kinds = [kind_of[name] for _, name in gather_groups[gi]]
        shapes = [s.shape for s in shards]
        lands = _place_own(shards, kinds, [F32 if name == "conv_w" else BF16 for _, name in gather_groups[gi]],
                           name=f"gather_weights_own_{gi}")
        send, recv, lands, token = _gather_start(lands, kinds, shapes, after, name=f"gather_weights_start_{gi}")
        started[gi] = dict(sems=(send, recv), lands=lands, kinds=kinds, shapes=shapes)
        return token

    def gather_forward(gi, after):
        st = started[gi]
        in_flight[gi] = _gather_forward(st["sems"][1], st["lands"], st["kinds"], st["shapes"], after,
                                        name=f"gather_weights_forward_{gi}")
        return in_flight[gi][3]

    def gather_finish(gi, after):
        st = started.pop(gi)
        fwd_send, fwd_recv, lands_g, _ = in_flight.pop(gi)
        whole = _gather_finish(st["sems"][0], st["sems"][1], fwd_send, fwd_recv, lands_g, st["kinds"], st["shapes"], after,
                               name=f"gather_weights_finish_{gi}")
        for (l, name), arr in zip(gather_groups[gi], whole):
            w = weights[l]
            if name == "w_in":
                (w["w_in"],) = _assemble(arr, (IN_WIDTH,), _w_in_moves(), name=f"l{l}_assemble_w_in")
            elif name == "w_up":
                w["w_up_g"], w["w_up_v"] = _assemble(arr, (D_FF, D_FF), _w_up_moves(), name=f"l{l}_assemble_w_up")
            elif name == "conv_w":
                w["cw_g"] = arr[:half].transpose(1, 0, 2).reshape(3, D_FF)
                w["cw_v"] = arr[half:].transpose(1, 0, 2).reshape(3, D_FF)
            else:
                w[name] = arr

    reduce_state, results = {}, {}
    wire = {"conv_w": F32, "small": F32}

    def reduce_begin(key, names, arrays):
        send, recv, srcs_, lands_, token = _exchange_start(arrays, _pair_plan, N_CHIPS, name=f"reduce_pair_start_{key}")
        reduce_state[key] = dict(names=names, pair=(send, recv, srcs_, lands_))
        return [token]

    def reduce_pair(key, after):
        st = reduce_state[key]
        send, recv, srcs_, lands_ = st.pop("pair")
        blocked_, from_sibling = _exchange_wait(send, recv, srcs_, lands_, _pair_plan, N_CHIPS, after,
                                                name=f"reduce_pair_wait_{key}")
        sums = [_reduce_pair_sum(b, r, place, wire.get(n if isinstance(n, str) else n[1], BF16),
                                 name=f"reduce_pair_sum_{key}_{i}")
                for i, (n, b, r) in enumerate(zip(st["names"], blocked_, from_sibling))]
        st["own"] = [s[0] for s in sums]
        *st["chip"], token = _exchange_start([s[1] for s in sums], _chip_plan, N_CHIPS - 1, name=f"reduce_chip_start_{key}")
        return [token]

    def reduce_end(key, after):
        st = reduce_state.pop(key)
        send, recv, srcs_, lands_ = st["chip"]
        _, from_chips = _exchange_wait(send, recv, srcs_, lands_, _chip_plan, N_CHIPS - 1, after,
                                       name=f"reduce_chip_wait_{key}")
        done = []
        for n, own, got in zip(st["names"], st["own"], from_chips):
            if n == "small":
                results["small"] = _reduce_chip_sum(own, got, name="reduce_chip_sum_small")
            else:
                l, name = n
                results[name] = _reduce_adamw(own, got, W[name], M[name], V[name], l, results.get(name),
                                              name=f"l{l}_reduce_adamw_{name}")
                done.append(results[name][0])
        return done

    def sched(point, l, carry, g=None):
        deps = []
        if point == "fwd_start" and l == 0:
            token = gather_forward(0, gather_start(0))
            gather_finish(0, token)
            deps = [gather_start(1, [weights[0]["w_out"]])]
        elif point == "fwd_att" and l == 0:
            deps = [gather_forward(1, carry), gather_start(2, [carry])]
        elif point == "fwd_mixer_done" and l == 0:
            gather_finish(1, carry)
            deps = [gather_start(3, [carry])]
        elif point == "fwd_conv" and l == 0:
            deps = [gather_forward(2, carry)]
        elif point == "fwd_start" and l == 1:
            gather_finish(2, carry)
        elif point == "fwd_att" and l == 1:
            deps = [gather_forward(3, carry)]
        elif point == "fwd_mixer_done" and l == 1:
            gather_finish(3, carry)
        elif point == "bwd_ffn_grads":
            if l + 1 < DEPTH:
                deps += reduce_end(f"l{l + 1}_in", g["w_up_v"])
            conv_w = jnp.concatenate([g[k].reshape(3, half, W_UP_SHARD).transpose(1, 0, 2) for k in ("cw_g", "cw_v")])
            deps += reduce_begin(
                f"l{l}_ffn", [(l, "w_down"), (l, "w_up"), (l, "conv_w")],
                [g["w_down"].reshape(N_DEV, D_FF // N_DEV, D_MODEL),
                 _disassemble((g["w_up_g"], g["w_up_v"]), W_UP_SHARD, _w_up_moves(), name=f"l{l}_split_dw_up"), conv_w])
        elif point == "bwd_merge":
            deps = reduce_pair(f"l{l}_ffn", carry)
        elif point == "bwd_out_grads":
            deps = reduce_begin(
                f"l{l}_out", [(l, "w_out"), (l, "w_oa"), (l, "w_ob")],
                [g["w_out"].reshape(N_DEV, D_MODEL // N_DEV, D_MODEL),
                 _disassemble((g["w_oa"],), LANES, _w_o_moves(), name=f"l{l}_split_dw_oa"),
                 _disassemble((g["w_ob"],), LANES, _w_o_moves(), name=f"l{l}_split_dw_ob")])
        elif point == "bwd_att":
            deps = reduce_pair(f"l{l}_out", carry) + reduce_end(f"l{l}_ffn", carry)
        elif point == "bwd_w_in_grad":
            deps = reduce_begin(f"l{l}_in", [(l, "w_in")],
                                [_disassemble((g["w_in"],), W_IN_SHARD, _w_in_moves(), name=f"l{l}_split_dw_in")])
        elif point == "bwd_dh":
            deps = reduce_pair(f"l{l}_in", carry) + reduce_end(f"l{l}_out", carry)
        return deps

    loss_part, dx, grads = _local_step(x.reshape(tokens, d_model), loss_target.reshape(tokens, d_model), weights, sched,
                                       n_seq=n_seq, seq=seq)
    loss = lax.psum(loss_part, ("x", "y", "c"))

    for g in grads:
        g["conv_b"] = jnp.concatenate([g["cb_g"], g["cb_v"]])
    reduce_begin("small", ["small"], [_pack_small(grads)])
    reduce_end("l0_in", dx)
    reduce_pair("small", results["w_in"][0])
    reduce_end("small", results["w_in"][1])

    G, delta, new_m, new_v = {}, {}, {}, {}
    for name, _ in SHARDED:
        G[name], delta[name], new_m[name], new_v[name] = results[name]
    G.update(_unpack_small(_gather([results["small"]], ["blocks"], name="gather_small_grads")[0]))
    for name, _ in REPLICATED:
        delta[name], new_m[name], new_v[name] = _adamw(W[name], G[name], M[name], V[name], name=f"adamw_{name}")
    return (loss, dx.reshape(n_seq, seq, d_model), *[G[n] for n in WEIGHT_ORDER], *[delta[n] for n in WEIGHT_ORDER],
            *[new_m[n] for n in WEIGHT_ORDER], *[new_v[n] for n in WEIGHT_ORDER])
```

```python
import math

import jax
import jax.numpy as jnp
from jax import lax
from jax.experimental import pallas as pl
from jax.experimental.pallas import tpu as pltpu

F32 = jnp.float32
BF16 = jnp.bfloat16
ACT_DTYPE = BF16
MESH = pl.DeviceIdType.MESH

DEPTH = 2
D_MODEL = 1024
N_Q_HEADS = 8
HEAD_DIM = 64
ATT_WIDTH = 512
KV_WIDTH = 128
BLOCK = 128
SGU_WIDTH = 512
SGU_GROUPS = 8
IN_WIDTH = 3840
D_FF = 2816
NORM_EPS = 1e-6
NEG_INF = -1e30
ATT_SCALE = HEAD_DIM ** -0.5
ALIBI_SLOPES = tuple(2.0 ** (-(h + 1)) for h in range(N_Q_HEADS))
ADAM_LR, ADAM_B1, ADAM_B2, ADAM_EPS, ADAM_WD, ADAM_STEP = 0.001, 0.9, 0.999, 1e-08, 0.01, 10
N_DEV = 8
N_CHIPS = 4

QKV_WIDTH = ATT_WIDTH + 2 * KV_WIDTH
REST_WIDTH = IN_WIDTH - QKV_WIDTH
COL_SUV, COL_GA, COL_GB, COL_QKV = 0, 1024, 2048, 3072

LANES = 128
SUBLANES = 8
VMEM_LIMIT_V7X = 56 * 1024 * 1024
GELU_C = math.sqrt(2.0 / math.pi)
GELU_K = 0.044715
ANY = pl.BlockSpec(memory_space=pl.ANY)


def _params(sem=None):
    return pltpu.CompilerParams(dimension_semantics=sem, vmem_limit_bytes=VMEM_LIMIT_V7X)


def _sigmoid(x):
    return 1.0 / (1.0 + jnp.exp(-x))


def _gelu(x):
    th = jnp.tanh(GELU_C * (x + GELU_K * x * x * x))
    return 0.5 * x * (1.0 + th)


def _gelu_and_grad(x):
    x2 = x * x
    th = jnp.tanh(GELU_C * (x + GELU_K * x2 * x))
    g = 0.5 * x * (1.0 + th)
    dg = 0.5 * (1.0 + th) + 0.5 * x * (1.0 - th * th) * (GELU_C * (1.0 + 3.0 * GELU_K * x2))
    return g, dg


def _dot(a, b, dims):
    return lax.dot_general(a, b, (dims, ((), ())), preferred_element_type=F32)


def _dot_nn(a, b):
    return _dot(a, b, ((1,), (0,)))


def _dot_nt(a, b):
    return _dot(a, b, ((1,), (1,)))


def _dot_tn(a, b):
    return _dot(a, b, ((0,), (0,)))


def _lo_mask(shape):
    return lax.broadcasted_iota(jnp.int32, shape, len(shape) - 1) < (LANES // 2)


def _half_sums(x, lo):
    s_lo = jnp.sum(jnp.where(lo, x, 0.0), axis=-1, keepdims=True)
    s_all = jnp.sum(x, axis=-1, keepdims=True)
    return jnp.where(lo, s_lo, s_all - s_lo)


def _dup_half(x, half, lo):
    r = pltpu.roll(x, LANES // 2, axis=1)
    return jnp.where(lo, x, r) if half == 0 else jnp.where(lo, r, x)


def _with_deps(body, n_in, deps):
    k = len(deps)
    if not k:
        return body, [], ()

    def skipping(*refs):
        return body(*refs[:n_in], *refs[n_in + k:])

    return skipping, [ANY] * k, tuple(deps)


MM_VMEM_BUDGET = 40 * 1024 * 1024
MM_MAX_TILE = 1408
MM_MAX_TK = 4096
MM_STEP_BYTES = 1 << 20


def _divisors(n, step, cap):
    return [d for d in range(step, min(n, cap) + 1, step) if n % d == 0] or [n]


def _mm_tiles(M, N, K, out_bytes, n_extra):
    best = None
    for tm in _divisors(M, LANES, MM_MAX_TILE):
        for tn in _divisors(N, LANES, MM_MAX_TILE):
            for tk in _divisors(K, 4 * LANES, MM_MAX_TK):
                vmem = 4 * (tm * tk + tk * tn) + 2 * tm * tn * (out_bytes + 4 * n_extra) + (0 if tk == K else 4 * tm * tn)
                if vmem > MM_VMEM_BUDGET:
                    continue
                traffic = 2 * M * K * (N // tn) + 2 * K * N * (M // tm) + M * N * (out_bytes + 4 * n_extra)
                cost = traffic + (K // tk - 1) * 8 * M * N + (M // tm) * (N // tn) * (K // tk) * MM_STEP_BYTES
                if best is None or cost < best[0]:
                    best = (cost, tm, tn, tk)
    assert best is not None, (M, N, K)
    return best[1:]


def _mm(a, b, *, mode, out_dtype, name, epilogue=None, extras=(), deps=()):
    if mode == "nn":
        (M, K), N = a.shape, b.shape[1]
    elif mode == "nt":
        (M, K), N = a.shape, b.shape[0]
    else:
        (K, M), N = a.shape, b.shape[1]
    tm, tn, tk = _mm_tiles(M, N, K, jnp.dtype(out_dtype).itemsize, len(extras))
    gm, gn, gk = M // tm, N // tn, K // tk
    if mode == "nn":
        a_spec = pl.BlockSpec((tm, tk), lambda i, j, k: (i, k))
        b_spec = pl.BlockSpec((tk, tn), lambda i, j, k: (k, j))
        contract = ((1,), (0,))
    elif mode == "nt":
        a_spec = pl.BlockSpec((tm, tk), lambda i, j, k: (i, k))
        b_spec = pl.BlockSpec((tn, tk), lambda i, j, k: (j, k))
        contract = ((1,), (1,))
    else:
        a_spec = pl.BlockSpec((tk, tm), lambda i, j, k: (k, i))
        b_spec = pl.BlockSpec((tk, tn), lambda i, j, k: (k, j))
        contract = ((0,), (0,))
    o_spec = pl.BlockSpec((tm, tn), lambda i, j, k: (i, j))
    n_extra = len(extras)

    def finish(acc, extra_refs, o_ref):
        if epilogue is not None:
            acc = epilogue(acc, *[r[...] for r in extra_refs])
        o_ref[...] = acc.astype(out_dtype)

    def body(a_ref, b_ref, *rest):
        extra_refs, o_ref = rest[:n_extra], rest[n_extra]
        part = _dot(a_ref[...].astype(BF16), b_ref[...].astype(BF16), contract)
        if gk == 1:
            finish(part, extra_refs, o_ref)
            return
        acc_ref = rest[n_extra + 1]
        k = pl.program_id(2)

        @pl.when(k == 0)
        def _():
            acc_ref[...] = part

        @pl.when(k > 0)
        def _():
            acc_ref[...] += part

        @pl.when(k == gk - 1)
        def _():
            finish(acc_ref[...], extra_refs, o_ref)

    body, dep_specs, dep_args = _with_deps(body, 2 + n_extra, deps)
    return pl.pallas_call(
        body,
        name=name,
        grid=(gm, gn, gk),
        in_specs=[a_spec, b_spec] + [o_spec] * n_extra + dep_specs,
        out_specs=o_spec,
        out_shape=jax.ShapeDtypeStruct((M, N), out_dtype),
        scratch_shapes=[] if gk == 1 else [pltpu.VMEM((tm, tn), F32)],
        compiler_params=_params(("parallel", "parallel", "arbitrary")),
    )(a, b, *extras, *dep_args)


def _add(acc, r):
    return acc + r


def _mm_rows(a, b, *, mode, fn, out_dtypes, rows=(), vecs=(), reduce=False, name, deps=()):
    M, K = a.shape
    N = b.shape[1] if mode == "nn" else b.shape[0]
    contract = ((1,), (0,)) if mode == "nn" else ((1,), (1,))
    n_rows, n_vecs, n_out = len(rows), len(vecs), len(out_dtypes)
    out_bytes = sum(jnp.dtype(d).itemsize for d in out_dtypes)
    tm = max(t for t in _divisors(M, LANES, MM_MAX_TILE)
             if 4 * t * K + 4 * K * N + 2 * t * N * (4 * n_rows + out_bytes) <= MM_VMEM_BUDGET)

    def body(a_ref, b_ref, *rest):
        row_refs, vec_refs = rest[:n_rows], rest[n_rows:n_rows + n_vecs]
        out_refs = rest[n_rows + n_vecs:]
        acc = _dot(a_ref[...], b_ref[...], contract)
        res = fn(acc, *[r[...] for r in row_refs], *[v[...] for v in vec_refs])
        for o_ref, val in zip(out_refs[:n_out], res):
            o_ref[...] = val.astype(o_ref.dtype)
        if reduce:
            @pl.when(pl.program_id(0) == 0)
            def _():
                out_refs[n_out][...] = res[n_out]

            @pl.when(pl.program_id(0) > 0)
            def _():
                out_refs[n_out][...] += res[n_out]

    row = pl.BlockSpec((tm, N), lambda i: (i, 0))
    vec = pl.BlockSpec((1, N), lambda i: (0, 0))
    body, dep_specs, dep_args = _with_deps(body, 2 + n_rows + n_vecs, deps)
    return pl.pallas_call(
        body, name=name, grid=(M // tm,),
        in_specs=[pl.BlockSpec((tm, K), lambda i: (i, 0)), pl.BlockSpec(b.shape, lambda i: (0, 0))]
        + [row] * n_rows + [vec] * n_vecs + dep_specs,
        out_specs=[row] * n_out + [vec] * reduce,
        out_shape=[jax.ShapeDtypeStruct((M, N), d) for d in out_dtypes] + [jax.ShapeDtypeStruct((1, N), F32)] * reduce,
        compiler_params=_params(("arbitrary",)),
    )(a, b, *rows, *[v.reshape(1, N) for v in vecs], *dep_args)


def _rms(x, gain):
    return x * lax.rsqrt(jnp.mean(x * x, axis=-1, keepdims=True) + NORM_EPS) * gain


def _residual_then_norm(acc, x, gain):
    x_out = x + acc
    return x_out, _rms(x_out, gain)


def _residual_then_loss(acc, x, target):
    err = (x + acc) - target
    dy = err * (1.0 / D_MODEL)
    return dy, dy, jnp.sum(err * err, axis=0, keepdims=True) * (0.5 / D_MODEL)


def _rms_bwd_rows(dh, x, dres, gain):
    r = lax.rsqrt(jnp.mean(x * x, axis=-1, keepdims=True) + NORM_EPS)
    xh = x * r
    dxh = dh * gain
    dx = dres + r * (dxh - xh * jnp.mean(dxh * xh, axis=-1, keepdims=True))
    return dx, dx, jnp.sum(dh * xh, axis=0, keepdims=True)


def _sum_then_rms_bwd(acc, dh_part, x, dres, gain):
    return _rms_bwd_rows(acc + dh_part, x, dres, gain)


def _rms_fwd(x, gain, *, name, tm=512):
    T, D = x.shape

    def body(x_ref, g_ref, h_ref):
        xv = x_ref[...]
        r = lax.rsqrt(jnp.mean(xv * xv, axis=-1, keepdims=True) + NORM_EPS)
        h_ref[...] = (xv * r * g_ref[...]).astype(BF16)

    return pl.pallas_call(
        body, name=name, grid=(T // tm,),
        in_specs=[pl.BlockSpec((tm, D), lambda i: (i, 0)), pl.BlockSpec((1, D), lambda i: (0, 0))],
        out_specs=pl.BlockSpec((tm, D), lambda i: (i, 0)),
        out_shape=jax.ShapeDtypeStruct((T, D), BF16),
        compiler_params=_params(("parallel",)),
    )(x, gain.reshape(1, D))


def _head_norm(x, gain2, lo):
    ms = _half_sums(x * x, lo) * (1.0 / HEAD_DIM)
    r = lax.rsqrt(ms + NORM_EPS)
    xh = x * r
    return xh * gain2, xh, r


def _head_norm_bwd(xh, r, gain2, dy, lo):
    dxh = dy * gain2
    dx = r * (dxh - xh * (_half_sums(dxh * xh, lo) * (1.0 / HEAD_DIM)))
    return dx, dy * xh


Q_GROUP = N_Q_HEADS // 2
GROUP_ROWS = Q_GROUP * BLOCK
ATT_SCRATCH = (pltpu.VMEM((2, 2, GROUP_ROWS, BLOCK), F32), pltpu.VMEM((2, GROUP_ROWS, 1), F32))


def _att_consts(sink_ref, bias_ref, sinkcol_ref):
    row = lax.broadcasted_iota(jnp.int32, (GROUP_ROWS, BLOCK), 0)
    kj = lax.broadcasted_iota(jnp.int32, (GROUP_ROWS, BLOCK), 1)
    head = row // BLOCK
    head_col = lax.broadcasted_iota(jnp.int32, (GROUP_ROWS, 1), 0) // BLOCK
    d_cur = (row % BLOCK) - kj
    d_prev = d_cur + BLOCK
    for kv in range(2):
        slope = jnp.zeros((GROUP_ROWS, BLOCK), F32)
        sink = jnp.zeros((GROUP_ROWS, 1), F32)
        for r in range(Q_GROUP):
            slope = jnp.where(head == r, ALIBI_SLOPES[Q_GROUP * kv + r], slope)
            sink = jnp.where(head_col == r, sink_ref[Q_GROUP * kv + r], sink)
        bias_ref[kv, 0] = jnp.where(d_cur >= 0, -slope * d_cur.astype(F32), NEG_INF)
        bias_ref[kv, 1] = jnp.where(d_prev < BLOCK, -slope * d_prev.astype(F32), NEG_INF)
        sinkcol_ref[kv] = sink


def _stack_heads(t0, t1, lo):
    z = jnp.zeros_like(t0)
    return jnp.concatenate([jnp.where(lo, t0, z), jnp.where(lo, z, t0), jnp.where(lo, t1, z), jnp.where(lo, z, t1)], axis=0)


def _unstack_heads(x4, lo):
    return (jnp.where(lo, x4[0:BLOCK], x4[BLOCK:2 * BLOCK]), jnp.where(lo, x4[2 * BLOCK:3 * BLOCK], x4[3 * BLOCK:]))


def _att_probs(q4, k2c, k2p, bias_c, bias_p, sink, has_prev):
    s_c = _dot_nt(q4, k2c) * ATT_SCALE + bias_c
    s_p = jnp.where(has_prev, _dot_nt(q4, k2p) * ATT_SCALE + bias_p, NEG_INF)
    m = jnp.maximum(jnp.max(jnp.maximum(s_c, s_p), axis=-1, keepdims=True), sink)
    e_c = jnp.exp(s_c - m)
    e_p = jnp.exp(s_p - m)
    e_s = jnp.exp(sink - m)
    inv = 1.0 / (jnp.sum(e_c + e_p, axis=-1, keepdims=True) + e_s)
    return e_c * inv, e_p * inv, e_s * inv


def _attention_fwd(proj, q_gain, k_gain, sinks, *, n_seq, seq, name):
    T = n_seq * seq
    nb = seq // BLOCK
    qcol, kvcol = COL_QKV // ATT_WIDTH, (COL_QKV + ATT_WIDTH) // (2 * KV_WIDTH)

    def body(q_ref, kv_ref, qg_ref, kg_ref, sink_ref, y_ref, bias_ref, sinkcol_ref):
        lo = _lo_mask((BLOCK, LANES))
        qg, kg = qg_ref[...], kg_ref[...]
        _att_consts(sink_ref, bias_ref, sinkcol_ref)

        def block(i, carry):
            r0 = pl.multiple_of(i * BLOCK, BLOCK)
            rp = pl.multiple_of(jnp.maximum(i - 1, 0) * BLOCK, BLOCK)
            has_prev = i > 0
            kn_c = _head_norm(kv_ref[pl.ds(r0, BLOCK), 0:KV_WIDTH].astype(F32), kg, lo)[0].astype(BF16)
            kn_p = _head_norm(kv_ref[pl.ds(rp, BLOCK), 0:KV_WIDTH].astype(F32), kg, lo)[0].astype(BF16)
            v_c = kv_ref[pl.ds(r0, BLOCK), KV_WIDTH:2 * KV_WIDTH].astype(BF16)
            v_p = kv_ref[pl.ds(rp, BLOCK), KV_WIDTH:2 * KV_WIDTH].astype(BF16)
            for kv in range(2):
                k2c, k2p = _dup_half(kn_c, kv, lo), _dup_half(kn_p, kv, lo)
                v2c, v2p = _dup_half(v_c, kv, lo), _dup_half(v_p, kv, lo)
                cols = [slice((2 * kv + t) * LANES, (2 * kv + t + 1) * LANES) for t in range(2)]
                qn = [_head_norm(q_ref[pl.ds(r0, BLOCK), c].astype(F32), qg, lo)[0] for c in cols]
                q4 = _stack_heads(qn[0], qn[1], lo).astype(BF16)
                p_c, p_p, _ = _att_probs(q4, k2c, k2p, bias_ref[kv, 0], bias_ref[kv, 1], sinkcol_ref[kv], has_prev)
                o4 = _dot_nn(p_c.astype(BF16), v2c) + _dot_nn(p_p.astype(BF16), v2p)
                for c, out in zip(cols, _unstack_heads(o4, lo)):
                    y_ref[pl.ds(r0, BLOCK), c] = out.astype(BF16)
            return carry

        lax.fori_loop(0, nb, block, 0)

    vec = pl.BlockSpec((1, LANES), lambda b: (0, 0))
    return pl.pallas_call(
        body, name=name, grid=(n_seq,),
        in_specs=[pl.BlockSpec((seq, ATT_WIDTH), lambda b: (b, qcol)),
                  pl.BlockSpec((seq, 2 * KV_WIDTH), lambda b: (b, kvcol)),
                  vec, vec, pl.BlockSpec(memory_space=pltpu.SMEM)],
        out_specs=pl.BlockSpec((seq, ATT_WIDTH), lambda b: (b, 0)),
        out_shape=jax.ShapeDtypeStruct((T, ATT_WIDTH), BF16),
        scratch_shapes=list(ATT_SCRATCH),
        compiler_params=_params(("parallel",)),
    )(proj, proj, jnp.tile(q_gain, 2).reshape(1, LANES), jnp.tile(k_gain, 2).reshape(1, LANES), sinks)


def _attention_bwd(proj, dy, q_gain, k_gain, sinks, *, n_seq, seq, name, deps=()):
    T = n_seq * seq
    nb = seq // BLOCK
    qcol, kvcol = COL_QKV // ATT_WIDTH, (COL_QKV + ATT_WIDTH) // (2 * KV_WIDTH)

    def body(q_ref, kv_ref, dy_ref, qg_ref, kg_ref, sink_ref, dqkv_ref, dqg_ref, dkg_ref, dsink_ref,
             dkn_acc, dv_acc, qg_acc, kg_acc, sink_acc, bias_ref, sinkcol_ref):
        lo = _lo_mask((BLOCK, LANES))
        qg, kg = qg_ref[...], kg_ref[...]
        _att_consts(sink_ref, bias_ref, sinkcol_ref)
        first = pl.program_id(0) == 0

        @pl.when(first)
        def _():
            qg_acc[...] = jnp.zeros_like(qg_acc)
            kg_acc[...] = jnp.zeros_like(kg_acc)
            sink_acc[...] = jnp.zeros_like(sink_acc)

        dkn_acc[...] = jnp.zeros_like(dkn_acc)
        dv_acc[...] = jnp.zeros_like(dv_acc)

        def block(i, carry):
            r0 = pl.multiple_of(i * BLOCK, BLOCK)
            rp = pl.multiple_of(jnp.maximum(i - 1, 0) * BLOCK, BLOCK)
            has_prev = i > 0
            kn_c = _head_norm(kv_ref[pl.ds(r0, BLOCK), 0:KV_WIDTH].astype(F32), kg, lo)[0].astype(BF16)
            kn_p = _head_norm(kv_ref[pl.ds(rp, BLOCK), 0:KV_WIDTH].astype(F32), kg, lo)[0].astype(BF16)
            v_c = kv_ref[pl.ds(r0, BLOCK), KV_WIDTH:2 * KV_WIDTH].astype(BF16)
            v_p = kv_ref[pl.ds(rp, BLOCK), KV_WIDTH:2 * KV_WIDTH].astype(BF16)
            dk_c, dk_p, dv_c, dv_p = [], [], [], []
            for kv in range(2):
                k2c, k2p = _dup_half(kn_c, kv, lo), _dup_half(kn_p, kv, lo)
                v2c, v2p = _dup_half(v_c, kv, lo), _dup_half(v_p, kv, lo)
                cols = [slice((2 * kv + t) * LANES, (2 * kv + t + 1) * LANES) for t in range(2)]
                normed = [_head_norm(q_ref[pl.ds(r0, BLOCK), c].astype(F32), qg, lo) for c in cols]
                q4 = _stack_heads(normed[0][0], normed[1][0], lo).astype(BF16)
                do4 = _stack_heads(dy_ref[pl.ds(r0, BLOCK), cols[0]], dy_ref[pl.ds(r0, BLOCK), cols[1]], lo)
                p_c, p_p, p_s = _att_probs(q4, k2c, k2p, bias_ref[kv, 0], bias_ref[kv, 1], sinkcol_ref[kv], has_prev)
                dp_c = _dot_nt(do4, v2c)
                dp_p = _dot_nt(do4, v2p)
                delta = jnp.sum(p_c * dp_c + p_p * dp_p, axis=-1, keepdims=True)
                ds_c = (p_c * (dp_c - delta)).astype(BF16)
                ds_p = (p_p * (dp_p - delta)).astype(BF16)
                sink_acc[kv] += -(p_s * delta)
                dq4 = (_dot_nn(ds_c, k2c) + _dot_nn(ds_p, k2p)) * ATT_SCALE
                for c, (_, qh, qr), dqn in zip(cols, normed, _unstack_heads(dq4, lo)):
                    dq, dg = _head_norm_bwd(qh, qr, qg, dqn, lo)
                    dqkv_ref[pl.ds(r0, BLOCK), c] = dq.astype(BF16)
                    qg_acc[...] += dg
                dk_c.append(_dot_tn(ds_c, q4))
                dk_p.append(_dot_tn(ds_p, q4))
                dv_c.append(_dot_tn(p_c.astype(BF16), do4))
                dv_p.append(_dot_tn(p_p.astype(BF16), do4))

            def fold(parts):
                a = parts[0] + pltpu.roll(parts[0], LANES // 2, axis=1)
                b = parts[1] + pltpu.roll(parts[1], LANES // 2, axis=1)
                return jnp.where(lo, a, b)

            dkn_acc[pl.ds(r0, BLOCK), :] += fold(dk_c) * ATT_SCALE
            dkn_acc[pl.ds(rp, BLOCK), :] += fold(dk_p) * ATT_SCALE
            dv_acc[pl.ds(r0, BLOCK), :] += fold(dv_c)
            dv_acc[pl.ds(rp, BLOCK), :] += fold(dv_p)
            return carry

        lax.fori_loop(0, nb, block, 0)

        def finish(i, carry):
            r0 = pl.multiple_of(i * BLOCK, BLOCK)
            _, kh, kr = _head_norm(kv_ref[pl.ds(r0, BLOCK), 0:KV_WIDTH].astype(F32), kg, lo)
            dk, dg = _head_norm_bwd(kh, kr, kg, dkn_acc[pl.ds(r0, BLOCK), :], lo)
            dqkv_ref[pl.ds(r0, BLOCK), ATT_WIDTH:ATT_WIDTH + KV_WIDTH] = dk.astype(BF16)
            dqkv_ref[pl.ds(r0, BLOCK), ATT_WIDTH + KV_WIDTH:QKV_WIDTH] = dv_acc[pl.ds(r0, BLOCK), :].astype(BF16)
            kg_acc[...] += dg
            return carry

        lax.fori_loop(0, nb, finish, 0)

        @pl.when(pl.program_id(0) == n_seq - 1)
        def _():
            dqg_ref[...] = jnp.sum(qg_acc[...], axis=0, keepdims=True)
            dkg_ref[...] = jnp.sum(kg_acc[...], axis=0, keepdims=True)
            lane = lax.broadcasted_iota(jnp.int32, (1, LANES), 1)
            dsink = jnp.zeros((1, LANES), F32)
            for kv in range(2):
                for r in range(Q_GROUP):
                    total = jnp.sum(sink_acc[kv, r * BLOCK:(r + 1) * BLOCK, :], axis=0, keepdims=True)
                    dsink = jnp.where(lane == Q_GROUP * kv + r, total, dsink)
            dsink_ref[...] = dsink

    vec = pl.BlockSpec((1, LANES), lambda b: (0, 0))
    acc = pltpu.VMEM((BLOCK, LANES), F32)
    body, dep_specs, dep_args = _with_deps(body, 6, deps)
    dqkv, dqg, dkg, dsink = pl.pallas_call(
        body, name=name, grid=(n_seq,),
        in_specs=[pl.BlockSpec((seq, ATT_WIDTH), lambda b: (b, qcol)),
                  pl.BlockSpec((seq, 2 * KV_WIDTH), lambda b: (b, kvcol)),
                  pl.BlockSpec((seq, ATT_WIDTH), lambda b: (b, 0)),
                  vec, vec, pl.BlockSpec(memory_space=pltpu.SMEM)] + dep_specs,
        out_specs=[pl.BlockSpec((seq, QKV_WIDTH), lambda b: (b, 0)), vec, vec, vec],
        out_shape=[jax.ShapeDtypeStruct((T, QKV_WIDTH), BF16)] + [jax.ShapeDtypeStruct((1, LANES), F32)] * 3,
        scratch_shapes=[pltpu.VMEM((seq, KV_WIDTH), F32), pltpu.VMEM((seq, KV_WIDTH), F32), acc, acc,
                        pltpu.VMEM((2, GROUP_ROWS, 1), F32), *ATT_SCRATCH],
        compiler_params=_params(("arbitrary",)),
    )(proj, proj, dy, jnp.tile(q_gain, 2).reshape(1, LANES), jnp.tile(k_gain, 2).reshape(1, LANES), sinks, *dep_args)
    half = LANES // 2
    return dqkv, dqg[0, :half] + dqg[0, half:], dkg[0, :half] + dkg[0, half:], dsink[0, :N_Q_HEADS]


def _sgu_weights(w_ref):
    r = lax.broadcasted_iota(jnp.int32, (BLOCK, BLOCK), 0)
    c = lax.broadcasted_iota(jnp.int32, (BLOCK, BLOCK), 1)
    return [jnp.where(r >= c, w_ref[g], 0.0).astype(BF16) for g in range(SGU_GROUPS)]


def _sgu_fwd(proj, gain, w_s, bias_full, *, n_seq, seq, name):
    T = n_seq * seq
    nc = seq // BLOCK

    def body(suv_ref, g_ref, w_ref, b_ref, y_ref):
        lo = _lo_mask((BLOCK, LANES))
        wm = _sgu_weights(w_ref)
        gain_v = g_ref[...]

        def chunk(c, carry):
            r0 = pl.multiple_of(c * BLOCK, BLOCK)
            gv = _gelu(suv_ref[pl.ds(r0, BLOCK), SGU_WIDTH:2 * SGU_WIDTH].astype(F32))
            r = lax.rsqrt(jnp.mean(gv * gv, axis=-1, keepdims=True) + NORM_EPS)
            vn = (gv * r * gain_v).astype(BF16)
            for p in range(SGU_WIDTH // LANES):
                cols = slice(p * LANES, (p + 1) * LANES)
                vp = vn[:, cols]
                mixed = jnp.where(lo, _dot_nn(wm[2 * p], vp), _dot_nn(wm[2 * p + 1], vp)) + b_ref[:, cols]
                u = _gelu(suv_ref[pl.ds(r0, BLOCK), cols].astype(F32))
                y_ref[pl.ds(r0, BLOCK), cols] = (u * mixed).astype(BF16)
            return carry

        lax.fori_loop(0, nc, chunk, 0)

    return pl.pallas_call(
        body, name=name, grid=(n_seq,),
        in_specs=[pl.BlockSpec((seq, 2 * SGU_WIDTH), lambda b: (b, COL_SUV // (2 * SGU_WIDTH))),
                  pl.BlockSpec((1, SGU_WIDTH), lambda b: (0, 0)),
                  pl.BlockSpec((SGU_GROUPS, BLOCK, BLOCK), lambda b: (0, 0, 0)),
                  pl.BlockSpec((BLOCK, SGU_WIDTH), lambda b: (0, 0))],
        out_specs=pl.BlockSpec((seq, SGU_WIDTH), lambda b: (b, 0)),
        out_shape=jax.ShapeDtypeStruct((T, SGU_WIDTH), BF16),
        compiler_params=_params(("parallel",)),
    )(proj, gain.reshape(1, SGU_WIDTH), w_s, bias_full)


def _sgu_bwd(proj, dy, gain, w_s, bias_full, *, n_seq, seq, name, deps=()):
    T = n_seq * seq
    nc = seq // BLOCK
    n_tiles = SGU_WIDTH // LANES

    def body(suv_ref, dy_ref, g_ref, w_ref, b_ref, dsuv_ref, dg_ref, dw_ref, db_ref, dg_acc, dw_acc, db_acc):
        lo = _lo_mask((BLOCK, LANES))
        hi = jnp.logical_not(lo)
        wm = _sgu_weights(w_ref)
        wmt = [jnp.where(lax.broadcasted_iota(jnp.int32, (BLOCK, BLOCK), 1) >= lax.broadcasted_iota(jnp.int32, (BLOCK, BLOCK), 0),
                         w_ref[g].T, 0.0).astype(BF16) for g in range(SGU_GROUPS)]
        gain_v = g_ref[...]

        @pl.when(pl.program_id(0) == 0)
        def _():
            dg_acc[...] = jnp.zeros_like(dg_acc)
            dw_acc[...] = jnp.zeros_like(dw_acc)
            db_acc[...] = jnp.zeros_like(db_acc)

        def chunk(c, carry):
            r0 = pl.multiple_of(c * BLOCK, BLOCK)
            gv, dgelu_v = _gelu_and_grad(suv_ref[pl.ds(r0, BLOCK), SGU_WIDTH:2 * SGU_WIDTH].astype(F32))
            r = lax.rsqrt(jnp.mean(gv * gv, axis=-1, keepdims=True) + NORM_EPS)
            vh = gv * r
            vn = (vh * gain_v).astype(BF16)
            dvn_tiles = []
            for p in range(n_tiles):
                cols = slice(p * LANES, (p + 1) * LANES)
                vp = vn[:, cols]
                mixed = jnp.where(lo, _dot_nn(wm[2 * p], vp), _dot_nn(wm[2 * p + 1], vp)) + b_ref[:, cols]
                u, dgelu_u = _gelu_and_grad(suv_ref[pl.ds(r0, BLOCK), cols].astype(F32))
                dyv = dy_ref[pl.ds(r0, BLOCK), cols]
                dsuv_ref[pl.ds(r0, BLOCK), cols] = (dyv * mixed * dgelu_u).astype(BF16)
                dm = dyv * u
                db_acc[:, cols] += dm
                dm_bf = dm.astype(BF16)
                dvn_tiles.append(jnp.where(lo, _dot_nn(wmt[2 * p], dm_bf), _dot_nn(wmt[2 * p + 1], dm_bf)))
                dw_acc[2 * p] += _dot_nt(jnp.where(lo, dm, 0.0).astype(BF16), vp)
                dw_acc[2 * p + 1] += _dot_nt(jnp.where(hi, dm, 0.0).astype(BF16), vp)
            dvn = jnp.concatenate(dvn_tiles, axis=1)
            dg_acc[...] += dvn * vh
            dvh = dvn * gain_v
            dgv = r * (dvh - vh * jnp.mean(dvh * vh, axis=-1, keepdims=True))
            dsuv_ref[pl.ds(r0, BLOCK), SGU_WIDTH:2 * SGU_WIDTH] = (dgv * dgelu_v).astype(BF16)
            return carry

        lax.fori_loop(0, nc, chunk, 0)

        @pl.when(pl.program_id(0) == n_seq - 1)
        def _():
            dg_ref[...] = jnp.sum(dg_acc[...], axis=0, keepdims=True)
            r = lax.broadcasted_iota(jnp.int32, (BLOCK, BLOCK), 0)
            c = lax.broadcasted_iota(jnp.int32, (BLOCK, BLOCK), 1)
            for g in range(SGU_GROUPS):
                dw_ref[g] = jnp.where(r >= c, dw_acc[g], 0.0)
            lane = lax.broadcasted_iota(jnp.int32, (BLOCK, LANES), 1)
            out = jnp.zeros((BLOCK, LANES), F32)
            for p in range(n_tiles):
                tile = db_acc[:, p * LANES:(p + 1) * LANES]
                s_lo = jnp.sum(jnp.where(lo, tile, 0.0), axis=-1, keepdims=True)
                s_hi = jnp.sum(jnp.where(hi, tile, 0.0), axis=-1, keepdims=True)
                out = jnp.where(lane == 2 * p, s_lo, out)
                out = jnp.where(lane == 2 * p + 1, s_hi, out)
            db_ref[...] = out

    body, dep_specs, dep_args = _with_deps(body, 5, deps)
    dsuv, dg, dw, db = pl.pallas_call(
        body, name=name, grid=(n_seq,),
        in_specs=[pl.BlockSpec((seq, 2 * SGU_WIDTH), lambda b: (b, COL_SUV // (2 * SGU_WIDTH))),
                  pl.BlockSpec((seq, SGU_WIDTH), lambda b: (b, 0)),
                  pl.BlockSpec((1, SGU_WIDTH), lambda b: (0, 0)),
                  pl.BlockSpec((SGU_GROUPS, BLOCK, BLOCK), lambda b: (0, 0, 0)),
                  pl.BlockSpec((BLOCK, SGU_WIDTH), lambda b: (0, 0))] + dep_specs,
        out_specs=[pl.BlockSpec((seq, 2 * SGU_WIDTH), lambda b: (b, 0)),
                   pl.BlockSpec((1, SGU_WIDTH), lambda b: (0, 0)),
                   pl.BlockSpec((SGU_GROUPS, BLOCK, BLOCK), lambda b: (0, 0, 0)),
                   pl.BlockSpec((BLOCK, LANES), lambda b: (0, 0))],
        out_shape=[jax.ShapeDtypeStruct((T, 2 * SGU_WIDTH), BF16), jax.ShapeDtypeStruct((1, SGU_WIDTH), F32),
                   jax.ShapeDtypeStruct((SGU_GROUPS, BLOCK, BLOCK), F32), jax.ShapeDtypeStruct((BLOCK, LANES), F32)],
        scratch_shapes=[pltpu.VMEM((BLOCK, SGU_WIDTH), F32), pltpu.VMEM((SGU_GROUPS, BLOCK, BLOCK), F32),
                        pltpu.VMEM((BLOCK, SGU_WIDTH), F32)],
        compiler_params=_params(("arbitrary",)),
    )(proj, dy, gain.reshape(1, SGU_WIDTH), w_s, bias_full, *dep_args)
    return dsuv, dg.reshape(SGU_WIDTH), dw, db[:, :SGU_GROUPS].T


def _merge_fwd(y_att, y_sgu, w_oa, w_ob, proj, *, name, tm=1024, tn=512, deps=()):
    T = y_att.shape[0]

    def body(ya_ref, ys_ref, wa_ref, wb_ref, ga_ref, gb_ref, o_ref):
        pa = _dot_nn(ya_ref[...], wa_ref[...])
        pb = _dot_nn(ys_ref[...], wb_ref[...])
        o_ref[...] = (_sigmoid(ga_ref[...].astype(F32)) * pa + _sigmoid(gb_ref[...].astype(F32)) * pb).astype(BF16)

    act = pl.BlockSpec((tm, ATT_WIDTH), lambda i, j: (i, 0))
    wgt = pl.BlockSpec((ATT_WIDTH, tn), lambda i, j: (0, j))
    body, dep_specs, dep_args = _with_deps(body, 6, deps)
    return pl.pallas_call(
        body, name=name, grid=(T // tm, D_MODEL // tn),
        in_specs=[act, act, wgt, wgt,
                  pl.BlockSpec((tm, tn), lambda i, j: (i, j + COL_GA // tn)),
                  pl.BlockSpec((tm, tn), lambda i, j: (i, j + COL_GB // tn))] + dep_specs,
        out_specs=pl.BlockSpec((tm, tn), lambda i, j: (i, j)),
        out_shape=jax.ShapeDtypeStruct((T, D_MODEL), BF16),
        compiler_params=_params(("parallel", "parallel")),
    )(y_att, y_sgu, w_oa, w_ob, proj, proj, *dep_args)


def _merge_bwd(dx1_bf, w_out, y_att, y_sgu, w_oa, w_ob, proj, *, name, tm=1024, tn=512):
    T = y_att.shape[0]

    def body(dx_ref, wo_ref, ya_ref, ys_ref, wa_ref, wb_ref, ga_ref, gb_ref, dpa_ref, dpb_ref, dga_ref, dgb_ref):
        dm = _dot_nt(dx_ref[...], wo_ref[...])
        pa = _dot_nn(ya_ref[...], wa_ref[...])
        pb = _dot_nn(ys_ref[...], wb_ref[...])
        sa = _sigmoid(ga_ref[...].astype(F32))
        sb = _sigmoid(gb_ref[...].astype(F32))
        dpa_ref[...] = (dm * sa).astype(BF16)
        dpb_ref[...] = (dm * sb).astype(BF16)
        dga_ref[...] = (dm * pa * sa * (1.0 - sa)).astype(BF16)
        dgb_ref[...] = (dm * pb * sb * (1.0 - sb)).astype(BF16)

    act = pl.BlockSpec((tm, ATT_WIDTH), lambda i, j: (i, 0))
    wgt = pl.BlockSpec((ATT_WIDTH, tn), lambda i, j: (0, j))
    out = pl.BlockSpec((tm, tn), lambda i, j: (i, j))
    return pl.pallas_call(
        body, name=name, grid=(T // tm, D_MODEL // tn),
        in_specs=[pl.BlockSpec((tm, D_MODEL), lambda i, j: (i, 0)),
                  pl.BlockSpec((tn, D_MODEL), lambda i, j: (j, 0)),
                  act, act, wgt, wgt,
                  pl.BlockSpec((tm, tn), lambda i, j: (i, j + COL_GA // tn)),
                  pl.BlockSpec((tm, tn), lambda i, j: (i, j + COL_GB // tn))],
        out_specs=[out] * 4,
        out_shape=[jax.ShapeDtypeStruct((T, D_MODEL), BF16)] * 4,
        compiler_params=_params(("parallel", "parallel")),
    )(dx1_bf, w_out, y_att, y_sgu, w_oa, w_ob, proj, proj)


CONV_ROWS = 256
CONV_TN = 256


def _shift_rows(cur, prev8, k):
    rolled = pltpu.roll(cur, k, axis=0)
    head = jnp.where(lax.broadcasted_iota(jnp.int32, prev8.shape, 0) < k, pltpu.roll(prev8, k, axis=0), rolled[:SUBLANES])
    return jnp.concatenate([head, rolled[SUBLANES:]], axis=0)


def _shift_rows_up(cur, next8, k):
    n = cur.shape[0]
    rolled = pltpu.roll(cur, n - k, axis=0)
    tail = jnp.where(lax.broadcasted_iota(jnp.int32, next8.shape, 0) >= SUBLANES - k,
                     pltpu.roll(next8, SUBLANES - k, axis=0), rolled[n - SUBLANES:])
    return jnp.concatenate([rolled[:n - SUBLANES], tail], axis=0)


HALO_ROWS = 16


def _rows_before(z_ref, r0, first):
    rp = pl.multiple_of(jnp.maximum(r0 - HALO_ROWS, 0), HALO_ROWS)
    halo = z_ref[pl.ds(rp, HALO_ROWS), :].astype(F32)
    return jnp.where(first, 0.0, halo[HALO_ROWS - SUBLANES:])


def _conv_rows(z_ref, r0, first, w_ref, b_ref, rows):
    cur = z_ref[pl.ds(r0, rows), :].astype(F32)
    prev8 = _rows_before(z_ref, r0, first)
    z1 = _shift_rows(cur, prev8, 1)
    z2 = _shift_rows(cur, prev8, 2)
    return b_ref[...] + w_ref[0:1, :] * z2 + w_ref[1:2, :] * z1 + w_ref[2:3, :] * cur


def _conv_fwd(z_g, z_v, cw_g, cw_v, cb_g, cb_v, *, n_seq, seq, name):
    T = n_seq * seq
    tn, rows = CONV_TN, CONV_ROWS

    def body(zg_ref, zv_ref, wg_ref, wv_ref, bg_ref, bv_ref, a_ref):
        def step(s, carry):
            r0 = pl.multiple_of(s * rows, rows)
            first = s == 0
            g = _conv_rows(zg_ref, r0, first, wg_ref, bg_ref, rows)
            v = _conv_rows(zv_ref, r0, first, wv_ref, bv_ref, rows)
            a_ref[pl.ds(r0, rows), :] = (g * _sigmoid(g) * v).astype(BF16)
            return carry

        lax.fori_loop(0, seq // rows, step, 0)

    zs = pl.BlockSpec((seq, tn), lambda b, j: (b, j))
    ws = pl.BlockSpec((3, tn), lambda b, j: (0, j))
    bs = pl.BlockSpec((1, tn), lambda b, j: (0, j))
    return pl.pallas_call(
        body, name=name, grid=(n_seq, D_FF // tn),
        in_specs=[zs, zs, ws, ws, bs, bs], out_specs=zs,
        out_shape=jax.ShapeDtypeStruct((T, D_FF), BF16),
        compiler_params=_params(("parallel", "parallel")),
    )(z_g, z_v, cw_g, cw_v, cb_g.reshape(1, D_FF), cb_v.reshape(1, D_FF))


def _conv_bwd(z_g, z_v, da, cw_g, cw_v, cb_g, cb_v, *, n_seq, seq, name):
    T = n_seq * seq
    tn, rows = CONV_TN, CONV_ROWS
    n_steps = seq // rows

    def body(zg_ref, zv_ref, da_ref, wg_ref, wv_ref, bg_ref, bv_ref,
             dzg_ref, dzv_ref, dwg_ref, dwv_ref, dbg_ref, dbv_ref, dcg_ref, dcv_ref):
        def grads(s, accs):
            r0 = pl.multiple_of(s * rows, rows)
            first = s == 0
            cur_g = zg_ref[pl.ds(r0, rows), :].astype(F32)
            cur_v = zv_ref[pl.ds(r0, rows), :].astype(F32)
            pg = _rows_before(zg_ref, r0, first)
            pv = _rows_before(zv_ref, r0, first)
            g1, g2 = _shift_rows(cur_g, pg, 1), _shift_rows(cur_g, pg, 2)
            v1, v2 = _shift_rows(cur_v, pv, 1), _shift_rows(cur_v, pv, 2)
            g = bg_ref[...] + wg_ref[0:1, :] * g2 + wg_ref[1:2, :] * g1 + wg_ref[2:3, :] * cur_g
            v = bv_ref[...] + wv_ref[0:1, :] * v2 + wv_ref[1:2, :] * v1 + wv_ref[2:3, :] * cur_v
            sg = _sigmoid(g)
            dav = da_ref[pl.ds(r0, rows), :].astype(F32)
            dcg = dav * v * (sg * (1.0 + g * (1.0 - sg)))
            dcv = dav * (g * sg)
            dcg_ref[pl.ds(r0, rows), :] = dcg
            dcv_ref[pl.ds(r0, rows), :] = dcv

            def colsum(x):
                return jnp.sum(x, axis=0, keepdims=True)

            return (accs[0] + colsum(dcg * g2), accs[1] + colsum(dcg * g1), accs[2] + colsum(dcg * cur_g), accs[3] + colsum(dcg),
                    accs[4] + colsum(dcv * v2), accs[5] + colsum(dcv * v1), accs[6] + colsum(dcv * cur_v), accs[7] + colsum(dcv))

        zero = jnp.zeros((1, tn), F32)
        sums = lax.fori_loop(0, n_steps, grads, (zero,) * 8)
        first_seq = pl.program_id(1) == 0

        @pl.when(first_seq)
        def _():
            dwg_ref[...] = jnp.concatenate(sums[0:3], axis=0)
            dbg_ref[...] = sums[3]
            dwv_ref[...] = jnp.concatenate(sums[4:7], axis=0)
            dbv_ref[...] = sums[7]

        @pl.when(jnp.logical_not(first_seq))
        def _():
            dwg_ref[...] += jnp.concatenate(sums[0:3], axis=0)
            dbg_ref[...] += sums[3]
            dwv_ref[...] += jnp.concatenate(sums[4:7], axis=0)
            dbv_ref[...] += sums[7]

        def back(s, carry):
            r0 = pl.multiple_of(s * rows, rows)
            last = s == n_steps - 1
            rn = pl.multiple_of(jnp.minimum(r0 + rows, seq - SUBLANES), SUBLANES)
            for dc_ref, w_ref, dz_ref in ((dcg_ref, wg_ref, dzg_ref), (dcv_ref, wv_ref, dzv_ref)):
                cur = dc_ref[pl.ds(r0, rows), :]
                nxt = jnp.where(last, 0.0, dc_ref[pl.ds(rn, SUBLANES), :])
                u1, u2 = _shift_rows_up(cur, nxt, 1), _shift_rows_up(cur, nxt, 2)
                dz_ref[pl.ds(r0, rows), :] = (w_ref[2:3, :] * cur + w_ref[1:2, :] * u1 + w_ref[0:1, :] * u2).astype(BF16)
            return carry

        lax.fori_loop(0, n_steps, back, 0)

    zs = pl.BlockSpec((seq, tn), lambda j, b: (b, j))
    ws = pl.BlockSpec((3, tn), lambda j, b: (0, j))
    bs = pl.BlockSpec((1, tn), lambda j, b: (0, j))
    outs = pl.pallas_call(
        body, name=name, grid=(D_FF // tn, n_seq),
        in_specs=[zs, zs, zs, ws, ws, bs, bs],
        out_specs=[zs, zs, ws, ws, bs, bs],
        out_shape=[jax.ShapeDtypeStruct((T, D_FF), BF16)] * 2 + [jax.ShapeDtypeStruct((3, D_FF), F32)] * 2
        + [jax.ShapeDtypeStruct((1, D_FF), F32)] * 2,
        scratch_shapes=[pltpu.VMEM((seq, tn), F32), pltpu.VMEM((seq, tn), F32)],
        compiler_params=_params(("parallel", "arbitrary")),
    )(z_g, z_v, da, cw_g, cw_v, cb_g.reshape(1, D_FF), cb_v.reshape(1, D_FF))
    dz_g, dz_v, dw_g, dw_v, db_g, db_v = outs
    return dz_g, dz_v, dw_g, dw_v, db_g.reshape(D_FF), db_v.reshape(D_FF)


def _layer_fwd(x, h, w, sched, tail, *, n_seq, seq, l):
    tag = f"l{l}"
    deps = sched("fwd_start", l, x)
    proj = _mm(h, w["w_in"], mode="nn", out_dtype=ACT_DTYPE, name=f"{tag}_proj", deps=deps)
    y_att = _attention_fwd(proj, w["q_norm"], w["k_norm"], w["sinks"], n_seq=n_seq, seq=seq, name=f"{tag}_att")
    deps = sched("fwd_att", l, y_att)
    y_sgu = _sgu_fwd(proj, w["sgu_norm"], w["w_s"], w["bias_full"], n_seq=n_seq, seq=seq, name=f"{tag}_sgu")
    merged = _merge_fwd(y_att, y_sgu, w["w_oa"], w["w_ob"], proj, name=f"{tag}_merge", deps=deps)
    x1, h2 = _mm_rows(merged, w["w_out"], mode="nn", fn=_residual_then_norm, out_dtypes=(F32, BF16), rows=(x,),
                      vecs=(w["ffn_norm"],), name=f"{tag}_out")
    deps = sched("fwd_mixer_done", l, x1)
    z_g = _mm(h2, w["w_up_g"], mode="nn", out_dtype=ACT_DTYPE, name=f"{tag}_up_g", deps=deps)
    z_v = _mm(h2, w["w_up_v"], mode="nn", out_dtype=ACT_DTYPE, name=f"{tag}_up_v")
    a = _conv_fwd(z_g, z_v, w["cw_g"], w["cw_v"], w["cb_g"], w["cb_v"], n_seq=n_seq, seq=seq, name=f"{tag}_conv")
    deps = sched("fwd_conv", l, a)
    if tail[0] == "norm":
        out = _mm_rows(a, w["w_down"], mode="nn", fn=_residual_then_norm, out_dtypes=(F32, BF16), rows=(x1,),
                       vecs=(tail[1],), name=f"{tag}_down", deps=deps)
    else:
        out = _mm_rows(a, w["w_down"], mode="nn", fn=_residual_then_loss, out_dtypes=(F32, BF16), rows=(x1, tail[1]),
                       reduce=True, name=f"{tag}_down", deps=deps)
    saved = dict(x=x, h=h, proj=proj, y_att=y_att, y_sgu=y_sgu, merged=merged, x1=x1, h2=h2, z_g=z_g, z_v=z_v, a=a)
    return out, saved


def _layer_bwd(dx2, dx2_bf, w, s, sched, deps, *, n_seq, seq, l):
    tag = f"l{l}b"
    g = {}
    da = _mm(dx2_bf, w["w_down"], mode="nt", out_dtype=ACT_DTYPE, name=f"{tag}_da", deps=deps)
    g["w_down"] = _mm(s["a"], dx2_bf, mode="tn", out_dtype=F32, name=f"{tag}_dw_down")
    dz_g, dz_v, g["cw_g"], g["cw_v"], g["cb_g"], g["cb_v"] = _conv_bwd(
        s["z_g"], s["z_v"], da, w["cw_g"], w["cw_v"], w["cb_g"], w["cb_v"], n_seq=n_seq, seq=seq, name=f"{tag}_conv")
    dh2_g = _mm(dz_g, w["w_up_g"], mode="nt", out_dtype=F32, name=f"{tag}_dh2_g")
    g["w_up_g"] = _mm(s["h2"], dz_g, mode="tn", out_dtype=F32, name=f"{tag}_dw_up_g")
    g["w_up_v"] = _mm(s["h2"], dz_v, mode="tn", out_dtype=F32, name=f"{tag}_dw_up_v")
    deps = sched("bwd_ffn_grads", l, dh2_g, g)
    dx1, dx1_bf, dgain = _mm_rows(dz_v, w["w_up_v"], mode="nt", fn=_sum_then_rms_bwd, out_dtypes=(F32, BF16),
                                  rows=(dh2_g, s["x1"], dx2), vecs=(w["ffn_norm"],), reduce=True, name=f"{tag}_dh2_v",
                                  deps=deps)
    g["ffn_norm"] = dgain.reshape(D_MODEL)
    dpa, dpb, dga, dgb = _merge_bwd(dx1_bf, w["w_out"], s["y_att"], s["y_sgu"], w["w_oa"], w["w_ob"], s["proj"],
                                    name=f"{tag}_merge")
    deps = sched("bwd_merge", l, dpa)
    g["w_out"] = _mm(s["merged"], dx1_bf, mode="tn", out_dtype=F32, name=f"{tag}_dw_out",
                     deps=deps)
    dy_att = _mm(dpa, w["w_oa"], mode="nt", out_dtype=BF16, name=f"{tag}_dy_att")
    dy_sgu = _mm(dpb, w["w_ob"], mode="nt", out_dtype=F32, name=f"{tag}_dy_sgu")
    g["w_oa"] = _mm(s["y_att"], dpa, mode="tn", out_dtype=F32, name=f"{tag}_dw_oa")
    g["w_ob"] = _mm(s["y_sgu"], dpb, mode="tn", out_dtype=F32, name=f"{tag}_dw_ob")
    deps = sched("bwd_out_grads", l, dy_att, g)
    dqkv, g["q_norm"], g["k_norm"], g["sinks"] = _attention_bwd(
        s["proj"], dy_att, w["q_norm"], w["k_norm"], w["sinks"], n_seq=n_seq, seq=seq, name=f"{tag}_att", deps=deps)
    deps = sched("bwd_att", l, dqkv)
    dsuv, g["sgu_norm"], g["w_s"], g["b_s"] = _sgu_bwd(
        s["proj"], dy_sgu, w["sgu_norm"], w["w_s"], w["bias_full"], n_seq=n_seq, seq=seq, name=f"{tag}_sgu", deps=deps)
    dproj = jnp.concatenate([dsuv, dga, dgb, dqkv], axis=1)
    g["w_in"] = _mm(s["h"], dproj, mode="tn", out_dtype=F32, name=f"{tag}_dw_in")
    deps = sched("bwd_w_in_grad", l, dproj, g)
    dx, dx_bf, dgain = _mm_rows(dproj, w["w_in"], mode="nt", fn=_rms_bwd_rows, out_dtypes=(F32, BF16),
                                rows=(s["x"], dx1), vecs=(w["mix_norm"],), reduce=True, name=f"{tag}_dh", deps=deps)
    g["mix_norm"] = dgain.reshape(D_MODEL)
    return dx, dx_bf, g, sched("bwd_dh", l, dx)


def _local_step(x, target, weights, sched, *, n_seq, seq):
    depth = len(weights)
    saved = []
    h = _rms_fwd(x, weights[0]["mix_norm"], name="l0_mix_norm")
    for l in range(depth):
        tail = ("norm", weights[l + 1]["mix_norm"]) if l + 1 < depth else ("loss", target)
        out, s = _layer_fwd(x, h, weights[l], sched, tail, n_seq=n_seq, seq=seq, l=l)
        saved.append(s)
        if l + 1 < depth:
            x, h = out
    dy, dy_bf, loss_cols = out
    grads = [None] * depth
    deps = ()
    for l in reversed(range(depth)):
        dy, dy_bf, grads[l], deps = _layer_bwd(dy, dy_bf, weights[l], saved[l], sched, deps, n_seq=n_seq, seq=seq, l=l)
    return jnp.sum(loss_cols), dy, grads


W_IN_SHARD = IN_WIDTH // N_DEV
W_UP_SHARD = 2 * D_FF // N_DEV
COL_MOVE_ROWS = 256


def _w_in_moves():
    moves = []
    for j in range(N_DEV):
        a, b = j * W_IN_SHARD, (j + 1) * W_IN_SHARD
        if a < QKV_WIDTH:
            moves.append((j, 0, min(b, QKV_WIDTH) - a, 0, a + REST_WIDTH))
        if b > QKV_WIDTH:
            lo = max(a, QKV_WIDTH)
            moves.append((j, lo - a, b - a, 0, lo - QKV_WIDTH))
    return tuple(moves)


def _w_up_moves():
    half = N_DEV // 2
    return tuple((j, 0, W_UP_SHARD, j // half, (j % half) * W_UP_SHARD) for j in range(N_DEV))


def _w_o_moves():
    return tuple((j, 0, LANES, 0, j * LANES) for j in range(N_DEV))


def _assemble(blocks, widths, moves, *, name):
    _, R, w = blocks.shape
    tr = min(R, COL_MOVE_ROWS)

    def body(b_ref, *o_refs):
        for j, lo, hi, which, at in moves:
            o_refs[which][:, at:at + hi - lo] = b_ref[j, :, lo:hi]

    return pl.pallas_call(
        body, name=name, grid=(R // tr,),
        in_specs=[pl.BlockSpec((N_DEV, tr, w), lambda i: (0, i, 0))],
        out_specs=[pl.BlockSpec((tr, n), lambda i: (i, 0)) for n in widths],
        out_shape=[jax.ShapeDtypeStruct((R, n), blocks.dtype) for n in widths],
        compiler_params=_params(("parallel",)),
    )(blocks)


def _disassemble(mats, w, moves, *, name):
    R = mats[0].shape[0]
    tr = min(R, COL_MOVE_ROWS)
    n = len(mats)

    def body(*refs):
        m_refs, o_ref = refs[:n], refs[n]
        for j, lo, hi, which, at in moves:
            o_ref[j, :, lo:hi] = m_refs[which][:, at:at + hi - lo]

    return pl.pallas_call(
        body, name=name, grid=(R // tr,),
        in_specs=[pl.BlockSpec((tr, m.shape[1]), lambda i: (i, 0)) for m in mats],
        out_specs=pl.BlockSpec((N_DEV, tr, w), lambda i: (0, i, 0)),
        out_shape=jax.ShapeDtypeStruct((N_DEV, R, w), mats[0].dtype),
        compiler_params=_params(("parallel",)),
    )(*mats)


def _my_place():
    return lax.axis_index("x"), lax.axis_index("y"), lax.axis_index("c")


def _gathered_shape(shape, kind):
    r, c = shape
    return {"blocks": (N_DEV, r, c), "rows": (N_DEV * r, c), "cols": (r, N_DEV * c)}[kind]


def _gather_window(ref, kind, shape, j):
    r, c = shape
    if kind == "blocks":
        return ref.at[j]
    if kind == "rows":
        return ref.at[pl.ds(pl.multiple_of(j * r, r), r), :]
    return ref.at[:, pl.ds(pl.multiple_of(j * c, c), c)]


def _gather(srcs, kinds, *, name):
    n = len(srcs)
    shapes = [s.shape for s in srcs]
    per = 7

    def body(*refs):
        src_refs, dst_refs = refs[:n], refs[n:2 * n]
        send_sems, recv_sems, local_sems = refs[2 * n:]
        x, y, c = _my_place()
        me, sibling = (x, y, c), (x, y, 1 - c)
        chips = [(1 - x, y), (x, 1 - y), (1 - x, 1 - y)]

        def at(i, px, py, pc):
            return _gather_window(dst_refs[i], kinds[i], shapes[i], 4 * px + 2 * py + pc)

        def copy(i, k, block, to, src=None):
            return pltpu.make_async_remote_copy(
                src_ref=at(i, *block) if src is None else src, dst_ref=at(i, *block),
                send_sem=send_sems.at[per * i + k], recv_sem=recv_sems.at[per * i + k], device_id=to, device_id_type=MESH)

        mine = [pltpu.make_async_copy(src_refs[i], at(i, *me), local_sems.at[i]) for i in range(n)]
        for cp in mine:
            cp.start()
        started = []
        for i in range(n):
            first = [copy(i, 0, me, sibling, src=src_refs[i])]
            first += [copy(i, 1 + j, me, (*chip, c), src=src_refs[i]) for j, chip in enumerate(chips)]
            for cp in first:
                cp.start()
            started += first
        for i in range(n):
            for j, chip in enumerate(chips):
                copy(i, 1 + j, (*chip, c), me).wait_recv()
                fwd = copy(i, 4 + j, (*chip, c), sibling)
                fwd.start()
                started.append(fwd)
        for i in range(n):
            copy(i, 0, sibling, me).wait_recv()
            for j, chip in enumerate(chips):
                copy(i, 4 + j, (*chip, 1 - c), me).wait_recv()
        for cp in started:
            cp.wait_send()
        for cp in mine:
            cp.wait()

    return pl.pallas_call(
        body, name=name,
        out_shape=[jax.ShapeDtypeStruct(_gathered_shape(s.shape, k), s.dtype) for s, k in zip(srcs, kinds)],
        in_specs=[ANY] * n, out_specs=[ANY] * n,
        scratch_shapes=[pltpu.SemaphoreType.DMA((per * n,)), pltpu.SemaphoreType.DMA((per * n,)),
                        pltpu.SemaphoreType.DMA((n,))],
    )(*srcs)


HBM = pl.BlockSpec(memory_space=pltpu.HBM)
SEM = pl.BlockSpec(memory_space=pltpu.SEMAPHORE)
TOKEN = jax.ShapeDtypeStruct((SUBLANES, LANES), F32)
TOKEN_SPEC = pl.BlockSpec(memory_space=pltpu.VMEM)
SPLIT_PARAMS = pltpu.CompilerParams(has_side_effects=pltpu.SideEffectType.DATAFLOW_SIDE_EFFECTING)


def _in_hbm(x):
    return pltpu.with_memory_space_constraint(x, pltpu.HBM)


def _hbm_like(shape, dtype):
    return pltpu.HBM(shape, dtype)


def _place_own(shards, kinds, dtypes, *, name):
    n = len(shards)
    shapes = [s.shape for s in shards]

    def body(*refs):
        s_refs, land_refs, bufs, sems = refs[:n], refs[n:2 * n], refs[2 * n:3 * n], refs[3 * n]
        x, y, c = _my_place()
        copies = []
        for i in range(n):
            bufs[i][...] = s_refs[i][...].astype(dtypes[i])
            copies.append(pltpu.make_async_copy(
                bufs[i], _gather_window(land_refs[i], kinds[i], shapes[i], 4 * x + 2 * y + c), sems.at[i]))
        for cp in copies:
            cp.start()
        for cp in copies:
            cp.wait()

    return pl.pallas_call(
        body, name=name,
        out_shape=[jax.ShapeDtypeStruct(_gathered_shape(s, k), d) for s, k, d in zip(shapes, kinds, dtypes)],
        in_specs=[pl.BlockSpec(memory_space=pltpu.VMEM)] * n, out_specs=[ANY] * n,
        scratch_shapes=[pltpu.VMEM(s, d) for s, d in zip(shapes, dtypes)] + [pltpu.SemaphoreType.DMA((n,))],
        compiler_params=_params(),
    )(*shards)


def _gather_start(lands, kinds, shapes, after=(), *, name):
    n = len(lands)
    n_after = len(after)

    def body(*refs):
        land_refs = refs[:n]
        send_sems, recv_sems = refs[n + n_after], refs[n + n_after + 1]
        x, y, c = _my_place()
        targets = [(x, y, 1 - c), (1 - x, y, c), (x, 1 - y, c), (1 - x, 1 - y, c)]
        for i in range(n):
            own = _gather_window(land_refs[i], kinds[i], shapes[i], 4 * x + 2 * y + c)
            for k, to in enumerate(targets):
                pltpu.make_async_remote_copy(
                    src_ref=own, dst_ref=own, send_sem=send_sems.at[4 * i + k], recv_sem=recv_sems.at[4 * i + k],
                    device_id=to, device_id_type=MESH).start()
        refs[-1][...] = jnp.zeros_like(refs[-1])

    outs = pl.pallas_call(
        body, name=name,
        out_shape=[pltpu.SemaphoreType.DMA((4 * n,)), pltpu.SemaphoreType.DMA((4 * n,))]
        + [_hbm_like(a.shape, a.dtype) for a in lands] + [TOKEN],
        in_specs=[HBM] * n + [ANY] * n_after, out_specs=[SEM, SEM] + [HBM] * n + [TOKEN_SPEC],
        input_output_aliases={i: 2 + i for i in range(n)},
        compiler_params=SPLIT_PARAMS,
    )(*[_in_hbm(a) for a in lands], *after)
    return outs[0], outs[1], outs[2:2 + n], outs[-1]


def _gather_forward(recv_sems, lands, kinds, shapes, after, *, name):
    n = len(lands)

    def body(*refs):
        recv_ref, land_refs = refs[0], refs[1:1 + n]
        fwd_send, fwd_recv = refs[2 + n], refs[3 + n]
        token = refs[-1]
        x, y, c = _my_place()
        chips = [(1 - x, y), (x, 1 - y), (1 - x, 1 - y)]
        for i in range(n):
            for j, (px, py) in enumerate(chips):
                block = _gather_window(land_refs[i], kinds[i], shapes[i], 4 * px + 2 * py + c)
                pltpu.make_async_remote_copy(
                    src_ref=block, dst_ref=block, send_sem=fwd_send.at[3 * i + j], recv_sem=recv_ref.at[4 * i + 1 + j],
                    device_id=(px, py, c), device_id_type=MESH).wait_recv()
                pltpu.make_async_remote_copy(
                    src_ref=block, dst_ref=block, send_sem=fwd_send.at[3 * i + j], recv_sem=fwd_recv.at[3 * i + j],
                    device_id=(x, y, 1 - c), device_id_type=MESH).start()
        token[...] = jnp.zeros_like(token)

    outs = pl.pallas_call(
        body, name=name,
        out_shape=[pltpu.SemaphoreType.DMA((3 * n,)), pltpu.SemaphoreType.DMA((3 * n,))]
        + [_hbm_like(a.shape, a.dtype) for a in lands] + [TOKEN],
        in_specs=[SEM] + [HBM] * n + [ANY], out_specs=[SEM, SEM] + [HBM] * n + [TOKEN_SPEC],
        input_output_aliases={1 + i: 2 + i for i in range(n)},
        compiler_params=SPLIT_PARAMS,
    )(recv_sems, *lands, after)
    return outs[0], outs[1], outs[2:2 + n], outs[-1]


def _gather_finish(send_sems, recv_sems, fwd_send, fwd_recv, lands, kinds, shapes, after, *, name):
    n = len(lands)

    def body(*refs):
        send_ref, recv_ref, fsend_ref, frecv_ref = refs[:4]
        land_refs = refs[4:4 + n]
        x, y, c = _my_place()
        chips = [(1 - x, y), (x, 1 - y), (1 - x, 1 - y)]
        sibling = (x, y, 1 - c)
        for i in range(n):
            def window(j):
                return _gather_window(land_refs[i], kinds[i], shapes[i], j)

            mine, theirs = window(4 * x + 2 * y + c), window(4 * x + 2 * y + (1 - c))
            pltpu.make_async_remote_copy(src_ref=mine, dst_ref=theirs, send_sem=send_ref.at[4 * i],
                                         recv_sem=recv_ref.at[4 * i], device_id=sibling, device_id_type=MESH).wait_recv()
            for j, (px, py) in enumerate(chips):
                block = window(4 * px + 2 * py + (1 - c))
                pltpu.make_async_remote_copy(src_ref=block, dst_ref=block, send_sem=fsend_ref.at[3 * i + j],
                                             recv_sem=frecv_ref.at[3 * i + j], device_id=sibling,
                                             device_id_type=MESH).wait_recv()
            for k in range(4):
                pltpu.make_async_remote_copy(src_ref=mine, dst_ref=mine, send_sem=send_ref.at[4 * i + k],
                                             recv_sem=recv_ref.at[4 * i + k], device_id=sibling,
                                             device_id_type=MESH).wait_send()
            for j, (px, py) in enumerate(chips):
                block = window(4 * px + 2 * py + c)
                pltpu.make_async_remote_copy(src_ref=block, dst_ref=block, send_sem=fsend_ref.at[3 * i + j],
                                             recv_sem=frecv_ref.at[3 * i + j], device_id=sibling,
                                             device_id_type=MESH).wait_send()

    return pl.pallas_call(
        body, name=name,
        out_shape=[_hbm_like(a.shape, a.dtype) for a in lands],
        in_specs=[SEM] * 4 + [HBM] * n + [ANY], out_specs=[HBM] * n,
        input_output_aliases={4 + i: i for i in range(n)},
        compiler_params=SPLIT_PARAMS,
    )(send_sems, recv_sems, fwd_send, fwd_recv, *lands, after)


def _pair_plan(src_ref, land_ref, x, y, c):
    return [(src_ref.at[2 * k + (1 - c)], land_ref.at[k], (x, y, 1 - c)) for k in range(N_CHIPS)]


def _chip_plan(src_ref, land_ref, x, y, c):
    chips = [(1 - x, y), (x, 1 - y), (1 - x, 1 - y)]
    return [(src_ref.at[2 * px + py], land_ref.at[k], (px, py, c)) for k, (px, py) in enumerate(chips)]


def _exchange_copies(plan, per, src_refs, land_refs, send_sems, recv_sems):
    x, y, c = _my_place()
    copies = []
    for i, (s_ref, l_ref) in enumerate(zip(src_refs, land_refs)):
        for q, (src, dst, to) in enumerate(plan(s_ref, l_ref, x, y, c)):
            copies.append(pltpu.make_async_remote_copy(
                src_ref=src, dst_ref=dst, send_sem=send_sems.at[per * i + q], recv_sem=recv_sems.at[per * i + q],
                device_id=to, device_id_type=MESH))
    return copies


def _exchange_start(srcs, plan, per, *, name):
    n = len(srcs)

    def body(*refs):
        src_refs, land_refs = refs[:n], refs[n:2 * n]
        send_sems, recv_sems = refs[2 * n], refs[2 * n + 1]
        for cp in _exchange_copies(plan, per, src_refs, land_refs, send_sems, recv_sems):
            cp.start()
        refs[-1][...] = jnp.zeros_like(refs[-1])

    lands = [lax.empty((per,) + s.shape[1:], s.dtype) for s in srcs]
    outs = pl.pallas_call(
        body, name=name,
        out_shape=[pltpu.SemaphoreType.DMA((per * n,)), pltpu.SemaphoreType.DMA((per * n,))]
        + [_hbm_like(s.shape, s.dtype) for s in srcs] + [_hbm_like(a.shape, a.dtype) for a in lands] + [TOKEN],
        in_specs=[HBM] * (2 * n), out_specs=[SEM, SEM] + [HBM] * (2 * n) + [TOKEN_SPEC],
        input_output_aliases={i: 2 + i for i in range(2 * n)},
        compiler_params=SPLIT_PARAMS,
    )(*[_in_hbm(s) for s in srcs], *[_in_hbm(a) for a in lands])
    return outs[0], outs[1], outs[2:2 + n], outs[2 + n:2 + 2 * n], outs[-1]


def _exchange_wait(send_sems, recv_sems, srcs, lands, plan, per, after, *, name):
    n = len(srcs)

    def body(*refs):
        send_ref, recv_ref = refs[0], refs[1]
        src_refs, land_refs = refs[2:2 + n], refs[2 + n:2 + 2 * n]
        copies = _exchange_copies(plan, per, src_refs, land_refs, send_ref, recv_ref)
        for cp in copies:
            cp.wait_recv()
        for cp in copies:
            cp.wait_send()

    outs = pl.pallas_call(
        body, name=name,
        out_shape=[_hbm_like(s.shape, s.dtype) for s in srcs] + [_hbm_like(a.shape, a.dtype) for a in lands],
        in_specs=[SEM, SEM] + [HBM] * (2 * n) + [ANY], out_specs=[HBM] * (2 * n),
        input_output_aliases={2 + i: i for i in range(2 * n)},
        compiler_params=SPLIT_PARAMS,
    )(send_sems, recv_sems, *srcs, *lands, after)
    return outs[:n], outs[n:]


REDUCE_BLOCK_BYTES = 1 << 20


def _row_tile(r, c):
    row_bytes = 4 * (-(-c // LANES) * LANES)
    best = r
    for d in range(SUBLANES, r, SUBLANES):
        if r % d == 0 and d * row_bytes <= REDUCE_BLOCK_BYTES:
            best = d
    return best if r * row_bytes > REDUCE_BLOCK_BYTES else r


def _reduce_pair_sum(blocked, recv, place, wire_dtype, *, name):
    _, r, c = blocked.shape
    tr = _row_tile(r, c)

    def body(place_ref, g_ref, r_ref, own_ref, send_ref):
        s = g_ref[...] + r_ref[...]
        send_ref[...] = s.astype(wire_dtype)

        @pl.when(pl.program_id(1) == place_ref[1])
        def _():
            own_ref[...] = s

    return pl.pallas_call(
        body, name=name,
        grid_spec=pltpu.PrefetchScalarGridSpec(
            num_scalar_prefetch=1, grid=(r // tr, N_CHIPS),
            in_specs=[pl.BlockSpec((None, None, tr, c), lambda i, k, place_ref: (k, place_ref[0], i, 0)),
                      pl.BlockSpec((None, tr, c), lambda i, k, place_ref: (k, i, 0))],
            out_specs=[pl.BlockSpec((tr, c), lambda i, k, place_ref: (i, 0)),
                       pl.BlockSpec((None, tr, c), lambda i, k, place_ref: (k, i, 0))]),
        out_shape=[jax.ShapeDtypeStruct((r, c), F32), jax.ShapeDtypeStruct((N_CHIPS, r, c), wire_dtype)],
        compiler_params=_params(("parallel", "arbitrary")),
    )(place, blocked.reshape(N_CHIPS, 2, r, c), recv)


def _chip_sum(own_ref, r_ref):
    return ((own_ref[...] + r_ref[0].astype(F32)) + r_ref[1].astype(F32)) + r_ref[2].astype(F32)


def _reduce_chip_sum(own, recv, *, name):
    r, c = own.shape
    tr = _row_tile(r, c)

    def body(own_ref, r_ref, o_ref):
        o_ref[...] = _chip_sum(own_ref, r_ref)

    return pl.pallas_call(
        body, name=name, grid=(r // tr,),
        in_specs=[pl.BlockSpec((tr, c), lambda i: (i, 0)), pl.BlockSpec((N_CHIPS - 1, tr, c), lambda i: (0, i, 0))],
        out_specs=pl.BlockSpec((tr, c), lambda i: (i, 0)),
        out_shape=jax.ShapeDtypeStruct((r, c), F32),
        compiler_params=_params(("parallel",)),
    )(own, recv)


def _adamw_math(w, g, m, v):
    nm = ADAM_B1 * m + (1.0 - ADAM_B1) * g
    nv = ADAM_B2 * v + (1.0 - ADAM_B2) * (g * g)
    m_hat = nm / (1.0 - ADAM_B1 ** ADAM_STEP)
    v_hat = nv / (1.0 - ADAM_B2 ** ADAM_STEP)
    return -ADAM_LR * (m_hat / (jnp.sqrt(v_hat) + ADAM_EPS) + ADAM_WD * w), nm, nv


def _adamw(w, g, m, v, *, name):
    shape = w.shape
    C = shape[-1]
    R = math.prod(shape[:-1])
    tr = _row_tile(R, C)

    def body(w_ref, g_ref, m_ref, v_ref, d_ref, nm_ref, nv_ref):
        d_ref[...], nm_ref[...], nv_ref[...] = _adamw_math(w_ref[...], g_ref[...], m_ref[...], v_ref[...])

    spec = pl.BlockSpec((tr, C), lambda i: (i, 0))
    outs = pl.pallas_call(
        body, name=name, grid=(R // tr,),
        in_specs=[spec] * 4, out_specs=[spec] * 3,
        out_shape=[jax.ShapeDtypeStruct((R, C), F32)] * 3,
        compiler_params=_params(("parallel",)),
    )(*[a.reshape(R, C) for a in (w, g, m, v)])
    return tuple(o.reshape(shape) for o in outs)


def _reduce_adamw(own, recv, w, m, v, layer, prev, *, name):
    r, c = own.shape
    tr = _row_tile(r, c)
    n_prev = 0 if prev is None else len(prev)

    def body(own_ref, r_ref, w_ref, m_ref, v_ref, *rest):
        g_ref, d_ref, nm_ref, nv_ref = rest[n_prev:]
        g = _chip_sum(own_ref, r_ref)
        g_ref[...] = g
        d_ref[...], nm_ref[...], nv_ref[...] = _adamw_math(w_ref[...], g, m_ref[...], v_ref[...])

    slot = pl.BlockSpec((None, tr, c), lambda i: (layer, i, 0))
    return pl.pallas_call(
        body, name=name, grid=(r // tr,),
        in_specs=[pl.BlockSpec((tr, c), lambda i: (i, 0)), pl.BlockSpec((N_CHIPS - 1, tr, c), lambda i: (0, i, 0)),
                  slot, slot, slot] + [ANY] * n_prev,
        out_specs=[slot] * 4,
        out_shape=[jax.ShapeDtypeStruct((DEPTH, r, c), F32)] * 4,
        input_output_aliases={5 + k: k for k in range(n_prev)},
        compiler_params=_params(("parallel",)),
    )(own, recv, w, m, v, *(prev or ()))


REPLICATED = (("mix_norm", (D_MODEL,)), ("q_norm", (HEAD_DIM,)), ("k_norm", (HEAD_DIM,)), ("sinks", (N_Q_HEADS,)),
              ("sgu_norm", (SGU_WIDTH,)), ("w_s", (SGU_GROUPS, BLOCK, BLOCK)), ("b_s", (SGU_GROUPS, BLOCK)),
              ("ffn_norm", (D_MODEL,)), ("conv_b", (2 * D_FF,)))
SHARDED = (("w_in", "blocks"), ("w_oa", "cols"), ("w_ob", "cols"), ("w_out", "rows"), ("w_up", "blocks"),
           ("conv_w", "blocks"), ("w_down", "rows"))
WEIGHT_ORDER = ("mix_norm", "w_in", "q_norm", "k_norm", "sinks", "sgu_norm", "w_s", "b_s", "w_oa", "w_ob", "w_out",
                "ffn_norm", "w_up", "conv_w", "conv_b", "w_down")
MIXER_WEIGHTS = ["w_in", "w_oa", "w_ob", "w_out"]
FFN_WEIGHTS = ["w_up", "conv_w", "w_down"]


def _small_layout():
    segs, off = {}, 0
    for l in range(DEPTH):
        for name, shape in REPLICATED:
            n = math.prod(shape)
            segs[(l, name)] = (off, n)
            off += n
    per_dev = -(-off // (N_DEV * SUBLANES * LANES)) * SUBLANES * LANES
    return segs, off, per_dev


def _pack_small(grads):
    ssegs, total, per_dev = _small_layout()
    flat = jnp.concatenate([grads[l][name].reshape(-1) for (l, name) in ssegs])
    return jnp.pad(flat, (0, N_DEV * per_dev - total)).reshape(N_DEV, per_dev // LANES, LANES)


def _unpack_small(gathered):
    ssegs, _, _ = _small_layout()
    flat = gathered.reshape(-1)
    shapes = dict(REPLICATED)
    return {name: jnp.stack([flat[ssegs[(l, name)][0]:ssegs[(l, name)][0] + ssegs[(l, name)][1]].reshape(shapes[name])
                             for l in range(DEPTH)]) for name, _ in REPLICATED}


def kernel(x, mix_norm, w_in, q_norm, k_norm, sinks, sgu_norm, w_s, b_s, w_oa, w_ob, w_out, ffn_norm, w_up, conv_w, conv_b, w_down, loss_target, m_mix_norm, m_w_in, m_q_norm, m_k_norm, m_sinks, m_sgu_norm, m_w_s, m_b_s, m_w_oa, m_w_ob, m_w_out, m_ffn_norm, m_w_up, m_conv_w, m_conv_b, m_w_down, v_mix_norm, v_w_in, v_q_norm, v_k_norm, v_sinks, v_sgu_norm, v_w_s, v_b_s, v_w_oa, v_w_ob, v_w_out, v_ffn_norm, v_w_up, v_conv_w, v_conv_b, v_w_down):
    W = dict(mix_norm=mix_norm, w_in=w_in, q_norm=q_norm, k_norm=k_norm, sinks=sinks, sgu_norm=sgu_norm, w_s=w_s, b_s=b_s,
             w_oa=w_oa, w_ob=w_ob, w_out=w_out, ffn_norm=ffn_norm, w_up=w_up, conv_w=conv_w, conv_b=conv_b, w_down=w_down)
    M = dict(mix_norm=m_mix_norm, w_in=m_w_in, q_norm=m_q_norm, k_norm=m_k_norm, sinks=m_sinks, sgu_norm=m_sgu_norm,
             w_s=m_w_s, b_s=m_b_s, w_oa=m_w_oa, w_ob=m_w_ob, w_out=m_w_out, ffn_norm=m_ffn_norm, w_up=m_w_up,
             conv_w=m_conv_w, conv_b=m_conv_b, w_down=m_w_down)
    V = dict(mix_norm=v_mix_norm, w_in=v_w_in, q_norm=v_q_norm, k_norm=v_k_norm, sinks=v_sinks, sgu_norm=v_sgu_norm,
             w_s=v_w_s, b_s=v_b_s, w_oa=v_w_oa, w_ob=v_w_ob, w_out=v_w_out, ffn_norm=v_ffn_norm, w_up=v_w_up,
             conv_w=v_conv_w, conv_b=v_conv_b, w_down=v_w_down)
    n_seq, seq, d_model = x.shape
    tokens = n_seq * seq
    mx, my, mc = _my_place()
    place = jnp.stack([mc, 2 * mx + my]).astype(jnp.int32)
    half = N_DEV // 2
    kind_of = dict(SHARDED)

    gather_groups = [[(l, n) for n in names] for l in range(DEPTH) for names in (MIXER_WEIGHTS, FFN_WEIGHTS)]
    started, in_flight = {}, {}
    weights = []
    for l in range(DEPTH):
        w = {name: W[name][l] for name, _ in REPLICATED}
        w["cb_g"], w["cb_v"] = W["conv_b"][l][:D_FF], W["conv_b"][l][D_FF:]
        w["bias_full"] = jnp.repeat(W["b_s"][l].T, SGU_WIDTH // SGU_GROUPS, axis=1)
        weights.append(w)

    def gather_start(gi, after=()):
        shards = [W[name][l] for l, name in gather_groups[gi]]
        kinds = [kind_of[name] for _, name in gather_groups[gi]]
        shapes = [s.shape for s in shards]
        lands = _place_own(shards, kinds, [F32 if name == "conv_w" else BF16 for _, name in gather_groups[gi]],
                           name=f"gather_weights_own_{gi}")
        send, recv, lands, token = _gather_start(lands, kinds, shapes, after, name=f"gather_weights_start_{gi}")
        started[gi] = dict(sems=(send, recv), lands=lands, kinds=kinds, shapes=shapes)
        return token

    def gather_forward(gi, after):
        st = started[gi]
        in_flight[gi] = _gather_forward(st["sems"][1], st["lands"], st["kinds"], st["shapes"], after,
                                        name=f"gather_weights_forward_{gi}")
        return in_flight[gi][3]

    def gather_finish(gi, after):
        st = started.pop(gi)
        fwd_send, fwd_recv, lands_g, _ = in_flight.pop(gi)
        whole = _gather_finish(st["sems"][0], st["sems"][1], fwd_send, fwd_recv, lands_g, st["kinds"], st["shapes"], after,
                               name=f"gather_weights_finish_{gi}")
        for (l, name), arr in zip(gather_groups[gi], whole):
            w = weights[l]
            if name == "w_in":
                (w["w_in"],) = _assemble(arr, (IN_WIDTH,), _w_in_moves(), name=f"l{l}_assemble_w_in")
            elif name == "w_up":
                w["w_up_g"], w["w_up_v"] = _assemble(arr, (D_FF, D_FF), _w_up_moves(), name=f"l{l}_assemble_w_up")
            elif name == "conv_w":
                w["cw_g"] = arr[:half].transpose(1, 0, 2).reshape(3, D_FF)
                w["cw_v"] = arr[half:].transpose(1, 0, 2).reshape(3, D_FF)
            else:
                w[name] = arr

    reduce_state, results = {}, {}
    wire = {"conv_w": F32, "small": F32}

    def reduce_begin(key, names, arrays):
        send, recv, srcs_, lands_, token = _exchange_start(arrays, _pair_plan, N_CHIPS, name=f"reduce_pair_start_{key}")
        reduce_state[key] = dict(names=names, pair=(send, recv, srcs_, lands_))
        return [token]

    def reduce_pair(key, after):
        st = reduce_state[key]
        send, recv, srcs_, lands_ = st.pop("pair")
        blocked_, from_sibling = _exchange_wait(send, recv, srcs_, lands_, _pair_plan, N_CHIPS, after,
                                                name=f"reduce_pair_wait_{key}")
        sums = [_reduce_pair_sum(b, r, place, wire.get(n if isinstance(n, str) else n[1], BF16),
                                 name=f"reduce_pair_sum_{key}_{i}")
                for i, (n, b, r) in enumerate(zip(st["names"], blocked_, from_sibling))]
        st["own"] = [s[0] for s in sums]
        *st["chip"], token = _exchange_start([s[1] for s in sums], _chip_plan, N_CHIPS - 1, name=f"reduce_chip_start_{key}")
        return [token]

    def reduce_end(key, after):
        st = reduce_state.pop(key)
        send, recv, srcs_, lands_ = st["chip"]
        _, from_chips = _exchange_wait(send, recv, srcs_, lands_, _chip_plan, N_CHIPS - 1, after,
                                       name=f"reduce_chip_wait_{key}")
        done = []
        for n, own, got in zip(st["names"], st["own"], from_chips):
            if n == "small":
                results["small"] = _reduce_chip_sum(own, got, name="reduce_chip_sum_small")
            else:
                l, name = n
                results[name] = _reduce_adamw(own, got, W[name], M[name], V[name], l, results.get(name),
                                              name=f"l{l}_reduce_adamw_{name}")
                done.append(results[name][0])
        return done

    def sched(point, l, carry, g=None):
        deps = []
        if point == "fwd_start" and l == 0:
            token = gather_forward(0, gather_start(0))
            gather_finish(0, token)
            deps = [gather_start(1, [weights[0]["w_out"]])]
        elif point == "fwd_att" and l == 0:
            deps = [gather_forward(1, carry), gather_start(2, [carry])]
        elif point == "fwd_mixer_done" and l == 0:
            gather_finish(1, carry)
            deps = [gather_start(3, [carry])]
        elif point == "fwd_conv" and l == 0:
            deps = [gather_forward(2, carry)]
        elif point == "fwd_start" and l == 1:
            gather_finish(2, carry)
        elif point == "fwd_att" and l == 1:
            deps = [gather_forward(3, carry)]
        elif point == "fwd_mixer_done" and l == 1:
            gather_finish(3, carry)
        elif point == "bwd_ffn_grads":
            if l + 1 < DEPTH:
                deps += reduce_end(f"l{l + 1}_in", g["w_up_v"])
            conv_w = jnp.concatenate([g[k].reshape(3, half, W_UP_SHARD).transpose(1, 0, 2) for k in ("cw_g", "cw_v")])
            deps += reduce_begin(
                f"l{l}_ffn", [(l, "w_down"), (l, "w_up"), (l, "conv_w")],
                [g["w_down"].reshape(N_DEV, D_FF // N_DEV, D_MODEL),
                 _disassemble((g["w_up_g"], g["w_up_v"]), W_UP_SHARD, _w_up_moves(), name=f"l{l}_split_dw_up"), conv_w])
        elif point == "bwd_merge":
            deps = reduce_pair(f"l{l}_ffn", carry)
        elif point == "bwd_out_grads":
            deps = reduce_begin(
                f"l{l}_out", [(l, "w_out"), (l, "w_oa"), (l, "w_ob")],
                [g["w_out"].reshape(N_DEV, D_MODEL // N_DEV, D_MODEL),
                 _disassemble((g["w_oa"],), LANES, _w_o_moves(), name=f"l{l}_split_dw_oa"),
                 _disassemble((g["w_ob"],), LANES, _w_o_moves(), name=f"l{l}_split_dw_ob")])
        elif point == "bwd_att":
            deps = reduce_pair(f"l{l}_out", carry) + reduce_end(f"l{l}_ffn", carry)
        elif point == "bwd_w_in_grad":
            deps = reduce_begin(f"l{l}_in", [(l, "w_in")],
                                [_disassemble((g["w_in"],), W_IN_SHARD, _w_in_moves(), name=f"l{l}_split_dw_in")])
        elif point == "bwd_dh":
            deps = reduce_pair(f"l{l}_in", carry) + reduce_end(f"l{l}_out", carry)
        return deps

    loss_part, dx, grads = _local_step(x.reshape(tokens, d_model), loss_target.reshape(tokens, d_model), weights, sched,
                                       n_seq=n_seq, seq=seq)
    loss = lax.psum(loss_part, ("x", "y", "c"))

    for g in grads:
        g["conv_b"] = jnp.concatenate([g["cb_g"], g["cb_v"]])
    reduce_begin("small", ["small"], [_pack_small(grads)])
    reduce_end("l0_in", dx)
    reduce_pair("small", results["w_in"][0])
    reduce_end("small", results["w_in"][1])

    G, delta, new_m, new_v = {}, {}, {}, {}
    for name, _ in SHARDED:
        G[name], delta[name], new_m[name], new_v[name] = results[name]
    G.update(_unpack_small(_gather([results["small"]], ["blocks"], name="gather_small_grads")[0]))
    for name, _ in REPLICATED:
        delta[name], new_m[name], new_v[name] = _adamw(W[name], G[name], M[name], V[name], name=f"adamw_{name}")
    return (loss, dx.reshape(n_seq, seq, d_model), *[G[n] for n in WEIGHT_ORDER], *[delta[n] for n in WEIGHT_ORDER],
            *[new_m[n] for n in WEIGHT_ORDER], *[new_v[n] for n in WEIGHT_ORDER])
```

```python
import math

import jax
import jax.numpy as jnp
from jax import lax
from jax.experimental import pallas as pl
from jax.experimental.pallas import tpu as pltpu

F32 = jnp.float32
BF16 = jnp.bfloat16
ACT_DTYPE = BF16
MESH = pl.DeviceIdType.MESH

DEPTH = 2
D_MODEL = 1024
N_Q_HEADS = 8
HEAD_DIM = 64
ATT_WIDTH = 512
KV_WIDTH = 128
BLOCK = 128
SGU_WIDTH = 512
SGU_GROUPS = 8
IN_WIDTH = 3840
D_FF = 2816
NORM_EPS = 1e-6
NEG_INF = -1e30
ATT_SCALE = HEAD_DIM ** -0.5
ALIBI_SLOPES = tuple(2.0 ** (-(h + 1)) for h in range(N_Q_HEADS))
ADAM_LR, ADAM_B1, ADAM_B2, ADAM_EPS, ADAM_WD, ADAM_STEP = 0.001, 0.9, 0.999, 1e-08, 0.01, 10
N_DEV = 8
N_CHIPS = 4

QKV_WIDTH = ATT_WIDTH + 2 * KV_WIDTH
REST_WIDTH = IN_WIDTH - QKV_WIDTH
COL_SUV, COL_GA, COL_GB, COL_QKV = 0, 1024, 2048, 3072
W_IN_ROTATE = (1, IN_WIDTH // QKV_WIDTH)

LANES = 128
SUBLANES = 8
VMEM_LIMIT_V7X = 56 * 1024 * 1024
GELU_C = math.sqrt(2.0 / math.pi)
GELU_K = 0.044715
ANY = pl.BlockSpec(memory_space=pl.ANY)


def _params(sem=None):
    return pltpu.CompilerParams(dimension_semantics=sem, vmem_limit_bytes=VMEM_LIMIT_V7X)


def _sigmoid(x):
    return 1.0 / (1.0 + jnp.exp(-x))


def _gelu(x):
    th = jnp.tanh(GELU_C * (x + GELU_K * x * x * x))
    return 0.5 * x * (1.0 + th)


def _gelu_and_grad(x):
    x2 = x * x
    th = jnp.tanh(GELU_C * (x + GELU_K * x2 * x))
    g = 0.5 * x * (1.0 + th)
    dg = 0.5 * (1.0 + th) + 0.5 * x * (1.0 - th * th) * (GELU_C * (1.0 + 3.0 * GELU_K * x2))
    return g, dg


def _dot(a, b, dims):
    return lax.dot_general(a, b, (dims, ((), ())), preferred_element_type=F32)


def _dot_nn(a, b):
    return _dot(a, b, ((1,), (0,)))


def _dot_nt(a, b):
    return _dot(a, b, ((1,), (1,)))


def _dot_tn(a, b):
    return _dot(a, b, ((0,), (0,)))


def _lo_mask(shape):
    return lax.broadcasted_iota(jnp.int32, shape, len(shape) - 1) < (LANES // 2)


def _half_sums(x, lo):
    s_lo = jnp.sum(jnp.where(lo, x, 0.0), axis=-1, keepdims=True)
    s_all = jnp.sum(x, axis=-1, keepdims=True)
    return jnp.where(lo, s_lo, s_all - s_lo)


def _dup_half(x, half, lo):
    r = pltpu.roll(x, LANES // 2, axis=1)
    return jnp.where(lo, x, r) if half == 0 else jnp.where(lo, r, x)


def _with_deps(body, n_in, deps):
    k = len(deps)
    if not k:
        return body, [], ()

    def skipping(*refs):
        return body(*refs[:n_in], *refs[n_in + k:])

    return skipping, [ANY] * k, tuple(deps)


MM_VMEM_BUDGET = 40 * 1024 * 1024
MM_MAX_TILE = 1408
MM_MAX_TK = 4096
MM_STEP_BYTES = 1 << 20


def _divisors(n, step, cap):
    return [d for d in range(step, min(n, cap) + 1, step) if n % d == 0] or [n]


def _mm_tiles(M, N, K, out_bytes, tm_divides, tn_divides):
    best = None
    for tm in _divisors(M, LANES, MM_MAX_TILE):
        for tn in _divisors(N, LANES, MM_MAX_TILE):
            if tm_divides % tm or tn_divides % tn:
                continue
            for tk in _divisors(K, 4 * LANES, MM_MAX_TK):
                vmem = 4 * (tm * tk + tk * tn) + 2 * tm * tn * out_bytes + (0 if tk == K else 4 * tm * tn)
                if vmem > MM_VMEM_BUDGET:
                    continue
                traffic = 2 * M * K * (N // tn) + 2 * K * N * (M // tm) + M * N * out_bytes
                cost = traffic + (K // tk - 1) * 8 * M * N + (M // tm) * (N // tn) * (K // tk) * MM_STEP_BYTES
                if best is None or cost < best[0]:
                    best = (cost, tm, tn, tk)
    assert best is not None, (M, N, K)
    return best[1:]


def _mm(a, b, *, mode, out_dtype, name, deps=(), n=None, b_rows=(0, None), rotate=None, out_rows=(0, None), out_prev=None):
    b_first, b_count = b_rows
    if mode == "nn":
        (M, K), N = a.shape, b.shape[1]
    elif mode == "nt":
        (M, K), N = a.shape, (b.shape[0] if b_count is None else b_count)
    else:
        (K, M), N = a.shape, b.shape[1]
    shift, period = rotate or (0, 1)
    tm, tn, tk = _mm_tiles(M, N, K, jnp.dtype(out_dtype).itemsize,
                           M // period if mode == "tn" else M, N // period if mode == "nt" else N)
    gm, gn, gk = M // tm, N // tn, K // tk
    out_first, out_total = out_rows[0], (M if out_rows[1] is None else out_rows[1])

    def turned(t, tile, size):
        per = size // period // tile
        return ((t // per + shift) % period) * per + t % per if period > 1 else t

    if mode == "nn":
        a_spec = pl.BlockSpec((tm, tk), lambda i, j, k: (i, k))
        b_spec = pl.BlockSpec((tk, tn), lambda i, j, k: (k + b_first // tk, j))
        contract = ((1,), (0,))
    elif mode == "nt":
        a_spec = pl.BlockSpec((tm, tk), lambda i, j, k: (i, k))
        b_spec = pl.BlockSpec((tn, tk), lambda i, j, k: (turned(j, tn, N) + b_first // tn, k))
        contract = ((1,), (1,))
    else:
        a_spec = pl.BlockSpec((tk, tm), lambda i, j, k: (k, i))
        b_spec = pl.BlockSpec((tk, tn), lambda i, j, k: (k, j))
        contract = ((0,), (0,))
    if mode == "tn":
        o_spec = pl.BlockSpec((tm, tn), lambda i, j, k: (turned(i, tm, M) + out_first // tm, j))
    else:
        o_spec = pl.BlockSpec((tm, tn), lambda i, j, k: (i + out_first // tm, j))
    assert b_first % (tk if mode == "nn" else tn) == 0 and out_first % tm == 0, (name, tm, tn, tk)
    n_prev = 0 if out_prev is None else 1

    def body(a_ref, b_ref, *rest):
        o_ref = rest[n_prev]
        part = _dot(a_ref[...].astype(BF16), b_ref[...].astype(BF16), contract)
        if gk == 1:
            o_ref[...] = part.astype(out_dtype)
            return
        acc_ref = rest[n_prev + 1]
        k = pl.program_id(2)

        @pl.when(k == 0)
        def _():
            acc_ref[...] = part

        @pl.when(k > 0)
        def _():
            acc_ref[...] += part

        @pl.when(k == gk - 1)
        def _():
            o_ref[...] = acc_ref[...].astype(out_dtype)

    body, dep_specs, dep_args = _with_deps(body, 2 + n_prev, deps)
    return pl.pallas_call(
        body,
        name=name,
        grid=(gm, gn, gk),
        in_specs=[a_spec, b_spec] + [ANY] * n_prev + dep_specs,
        out_specs=o_spec,
        out_shape=jax.ShapeDtypeStruct((out_total, N), out_dtype),
        input_output_aliases={2: 0} if n_prev else {},
        scratch_shapes=[] if gk == 1 else [pltpu.VMEM((tm, tn), F32)],
        compiler_params=_params(("parallel", "parallel", "arbitrary")),
    )(a, b, *([out_prev] if n_prev else []), *dep_args)


def _mm_rows(a, b, *, mode, fn, out_dtypes, rows=(), vecs=(), reduce=False, name, deps=(), b_rows=(0, None), k_rotate=0):
    M, K = a.shape
    b_first, b_count = b_rows[0], (b.shape[0] if b_rows[1] is None else b_rows[1])
    N = b.shape[1] if mode == "nn" else b_count
    contract = ((1,), (0,)) if mode == "nn" else ((1,), (1,))
    n_rows, n_vecs, n_out = len(rows), len(vecs), len(out_dtypes)
    out_bytes = sum(jnp.dtype(d).itemsize for d in out_dtypes)
    tm = max(t for t in _divisors(M, LANES, MM_MAX_TILE)
             if 4 * t * K + 4 * K * N + 2 * t * N * (4 * n_rows + out_bytes) <= MM_VMEM_BUDGET)
    assert b_first % b_count == 0 and (k_rotate == 0 or mode == "nn")

    def body(a_ref, b_ref, *rest):
        row_refs, vec_refs = rest[:n_rows], rest[n_rows:n_rows + n_vecs]
        out_refs = rest[n_rows + n_vecs:]
        if k_rotate:
            acc = (_dot(a_ref[:, :K - k_rotate], b_ref[k_rotate:, :], contract)
                   + _dot(a_ref[:, K - k_rotate:], b_ref[:k_rotate, :], contract))
        else:
            acc = _dot(a_ref[...], b_ref[...], contract)
        res = fn(acc, *[r[...] for r in row_refs], *[v[...] for v in vec_refs])
        for o_ref, val in zip(out_refs[:n_out], res):
            o_ref[...] = val.astype(o_ref.dtype)
        if reduce:
            @pl.when(pl.program_id(0) == 0)
            def _():
                out_refs[n_out][...] = res[n_out]

            @pl.when(pl.program_id(0) > 0)
            def _():
                out_refs[n_out][...] += res[n_out]

    row = pl.BlockSpec((tm, N), lambda i: (i, 0))
    vec = pl.BlockSpec((1, N), lambda i: (0, 0))
    body, dep_specs, dep_args = _with_deps(body, 2 + n_rows + n_vecs, deps)
    return pl.pallas_call(
        body, name=name, grid=(M // tm,),
        in_specs=[pl.BlockSpec((tm, K), lambda i: (i, 0)),
                  pl.BlockSpec((b_count, b.shape[1]), lambda i: (b_first // b_count, 0))]
        + [row] * n_rows + [vec] * n_vecs + dep_specs,
        out_specs=[row] * n_out + [vec] * reduce,
        out_shape=[jax.ShapeDtypeStruct((M, N), d) for d in out_dtypes] + [jax.ShapeDtypeStruct((1, N), F32)] * reduce,
        compiler_params=_params(("arbitrary",)),
    )(a, b, *rows, *[v.reshape(1, N) for v in vecs], *dep_args)


def _rms(x, gain):
    return x * lax.rsqrt(jnp.mean(x * x, axis=-1, keepdims=True) + NORM_EPS) * gain


def _residual_then_norm(acc, x, gain):
    x_out = x + acc
    return x_out, _rms(x_out, gain)


def _residual_then_loss(acc, x, target):
    err = (x + acc) - target
    dy = err * (1.0 / D_MODEL)
    return dy, dy, jnp.sum(err * err, axis=0, keepdims=True) * (0.5 / D_MODEL)


def _rms_bwd_rows(dh, x, dres, gain):
    r = lax.rsqrt(jnp.mean(x * x, axis=-1, keepdims=True) + NORM_EPS)
    xh = x * r
    dxh = dh * gain
    dx = dres + r * (dxh - xh * jnp.mean(dxh * xh, axis=-1, keepdims=True))
    return dx, dx, jnp.sum(dh * xh, axis=0, keepdims=True)


def _sum_then_rms_bwd(acc, dh_part, x, dres, gain):
    return _rms_bwd_rows(acc + dh_part, x, dres, gain)


def _rms_fwd(x, gain, *, name, tm=512):
    T, D = x.shape

    def body(x_ref, g_ref, h_ref):
        xv = x_ref[...]
        r = lax.rsqrt(jnp.mean(xv * xv, axis=-1, keepdims=True) + NORM_EPS)
        h_ref[...] = (xv * r * g_ref[...]).astype(BF16)

    return pl.pallas_call(
        body, name=name, grid=(T // tm,),
        in_specs=[pl.BlockSpec((tm, D), lambda i: (i, 0)), pl.BlockSpec((1, D), lambda i: (0, 0))],
        out_specs=pl.BlockSpec((tm, D), lambda i: (i, 0)),
        out_shape=jax.ShapeDtypeStruct((T, D), BF16),
        compiler_params=_params(("parallel",)),
    )(x, gain.reshape(1, D))


def _head_norm(x, gain2, lo):
    ms = _half_sums(x * x, lo) * (1.0 / HEAD_DIM)
    r = lax.rsqrt(ms + NORM_EPS)
    xh = x * r
    return xh * gain2, xh, r


def _head_norm_bwd(xh, r, gain2, dy, lo):
    dxh = dy * gain2
    dx = r * (dxh - xh * (_half_sums(dxh * xh, lo) * (1.0 / HEAD_DIM)))
    return dx, dy * xh


Q_GROUP = N_Q_HEADS // 2
GROUP_ROWS = Q_GROUP * BLOCK
ATT_SCRATCH = (pltpu.VMEM((2, 2, GROUP_ROWS, BLOCK), F32), pltpu.VMEM((2, GROUP_ROWS, 1), F32))


def _att_consts(sink_ref, bias_ref, sinkcol_ref):
    row = lax.broadcasted_iota(jnp.int32, (GROUP_ROWS, BLOCK), 0)
    kj = lax.broadcasted_iota(jnp.int32, (GROUP_ROWS, BLOCK), 1)
    head = row // BLOCK
    head_col = lax.broadcasted_iota(jnp.int32, (GROUP_ROWS, 1), 0) // BLOCK
    d_cur = (row % BLOCK) - kj
    d_prev = d_cur + BLOCK
    for kv in range(2):
        slope = jnp.zeros((GROUP_ROWS, BLOCK), F32)
        sink = jnp.zeros((GROUP_ROWS, 1), F32)
        for r in range(Q_GROUP):
            slope = jnp.where(head == r, ALIBI_SLOPES[Q_GROUP * kv + r], slope)
            sink = jnp.where(head_col == r, sink_ref[Q_GROUP * kv + r], sink)
        bias_ref[kv, 0] = jnp.where(d_cur >= 0, -slope * d_cur.astype(F32), NEG_INF)
        bias_ref[kv, 1] = jnp.where(d_prev < BLOCK, -slope * d_prev.astype(F32), NEG_INF)
        sinkcol_ref[kv] = sink


def _stack_heads(t0, t1, lo):
    z = jnp.zeros_like(t0)
    return jnp.concatenate([jnp.where(lo, t0, z), jnp.where(lo, z, t0), jnp.where(lo, t1, z), jnp.where(lo, z, t1)], axis=0)


def _unstack_heads(x4, lo):
    return (jnp.where(lo, x4[0:BLOCK], x4[BLOCK:2 * BLOCK]), jnp.where(lo, x4[2 * BLOCK:3 * BLOCK], x4[3 * BLOCK:]))


def _att_probs(q4, k2c, k2p, bias_c, bias_p, sink, has_prev):
    s_c = _dot_nt(q4, k2c) * ATT_SCALE + bias_c
    s_p = jnp.where(has_prev, _dot_nt(q4, k2p) * ATT_SCALE + bias_p, NEG_INF)
    m = jnp.maximum(jnp.max(jnp.maximum(s_c, s_p), axis=-1, keepdims=True), sink)
    e_c = jnp.exp(s_c - m)
    e_p = jnp.exp(s_p - m)
    e_s = jnp.exp(sink - m)
    inv = 1.0 / (jnp.sum(e_c + e_p, axis=-1, keepdims=True) + e_s)
    return e_c * inv, e_p * inv, e_s * inv


def _attention_fwd(proj, q_gain, k_gain, sinks, *, n_seq, seq, name):
    T = n_seq * seq
    nb = seq // BLOCK
    qcol, kvcol = COL_QKV // ATT_WIDTH, (COL_QKV + ATT_WIDTH) // (2 * KV_WIDTH)

    def body(q_ref, kv_ref, qg_ref, kg_ref, sink_ref, y_ref, bias_ref, sinkcol_ref):
        lo = _lo_mask((BLOCK, LANES))
        qg, kg = qg_ref[...], kg_ref[...]
        _att_consts(sink_ref, bias_ref, sinkcol_ref)

        def block(i, carry):
            r0 = pl.multiple_of(i * BLOCK, BLOCK)
            rp = pl.multiple_of(jnp.maximum(i - 1, 0) * BLOCK, BLOCK)
            has_prev = i > 0
            kn_c = _head_norm(kv_ref[pl.ds(r0, BLOCK), 0:KV_WIDTH].astype(F32), kg, lo)[0].astype(BF16)
            kn_p = _head_norm(kv_ref[pl.ds(rp, BLOCK), 0:KV_WIDTH].astype(F32), kg, lo)[0].astype(BF16)
            v_c = kv_ref[pl.ds(r0, BLOCK), KV_WIDTH:2 * KV_WIDTH].astype(BF16)
            v_p = kv_ref[pl.ds(rp, BLOCK), KV_WIDTH:2 * KV_WIDTH].astype(BF16)
            for kv in range(2):
                k2c, k2p = _dup_half(kn_c, kv, lo), _dup_half(kn_p, kv, lo)
                v2c, v2p = _dup_half(v_c, kv, lo), _dup_half(v_p, kv, lo)
                cols = [slice((2 * kv + t) * LANES, (2 * kv + t + 1) * LANES) for t in range(2)]
                qn = [_head_norm(q_ref[pl.ds(r0, BLOCK), c].astype(F32), qg, lo)[0] for c in cols]
                q4 = _stack_heads(qn[0], qn[1], lo).astype(BF16)
                p_c, p_p, _ = _att_probs(q4, k2c, k2p, bias_ref[kv, 0], bias_ref[kv, 1], sinkcol_ref[kv], has_prev)
                o4 = _dot_nn(p_c.astype(BF16), v2c) + _dot_nn(p_p.astype(BF16), v2p)
                for c, out in zip(cols, _unstack_heads(o4, lo)):
                    y_ref[pl.ds(r0, BLOCK), c] = out.astype(BF16)
            return carry

        lax.fori_loop(0, nb, block, 0)

    vec = pl.BlockSpec((1, LANES), lambda b: (0, 0))
    return pl.pallas_call(
        body, name=name, grid=(n_seq,),
        in_specs=[pl.BlockSpec((seq, ATT_WIDTH), lambda b: (b, qcol)),
                  pl.BlockSpec((seq, 2 * KV_WIDTH), lambda b: (b, kvcol)),
                  vec, vec, pl.BlockSpec(memory_space=pltpu.SMEM)],
        out_specs=pl.BlockSpec((seq, ATT_WIDTH), lambda b: (b, 0)),
        out_shape=jax.ShapeDtypeStruct((T, ATT_WIDTH), BF16),
        scratch_shapes=list(ATT_SCRATCH),
        compiler_params=_params(("parallel",)),
    )(proj, proj, jnp.tile(q_gain, 2).reshape(1, LANES), jnp.tile(k_gain, 2).reshape(1, LANES), sinks)


def _attention_bwd(proj, dy, q_gain, k_gain, sinks, *, n_seq, seq, name, deps=()):
    T = n_seq * seq
    nb = seq // BLOCK
    qcol, kvcol = COL_QKV // ATT_WIDTH, (COL_QKV + ATT_WIDTH) // (2 * KV_WIDTH)

    def body(q_ref, kv_ref, dy_ref, qg_ref, kg_ref, sink_ref, dqkv_ref, dqg_ref, dkg_ref, dsink_ref,
             dkn_acc, dv_acc, qg_acc, kg_acc, sink_acc, bias_ref, sinkcol_ref):
        lo = _lo_mask((BLOCK, LANES))
        qg, kg = qg_ref[...], kg_ref[...]
        _att_consts(sink_ref, bias_ref, sinkcol_ref)
        first = pl.program_id(0) == 0

        @pl.when(first)
        def _():
            qg_acc[...] = jnp.zeros_like(qg_acc)
            kg_acc[...] = jnp.zeros_like(kg_acc)
            sink_acc[...] = jnp.zeros_like(sink_acc)

        dkn_acc[...] = jnp.zeros_like(dkn_acc)
        dv_acc[...] = jnp.zeros_like(dv_acc)

        def block(i, carry):
            r0 = pl.multiple_of(i * BLOCK, BLOCK)
            rp = pl.multiple_of(jnp.maximum(i - 1, 0) * BLOCK, BLOCK)
            has_prev = i > 0
            kn_c = _head_norm(kv_ref[pl.ds(r0, BLOCK), 0:KV_WIDTH].astype(F32), kg, lo)[0].astype(BF16)
            kn_p = _head_norm(kv_ref[pl.ds(rp, BLOCK), 0:KV_WIDTH].astype(F32), kg, lo)[0].astype(BF16)
            v_c = kv_ref[pl.ds(r0, BLOCK), KV_WIDTH:2 * KV_WIDTH].astype(BF16)
            v_p = kv_ref[pl.ds(rp, BLOCK), KV_WIDTH:2 * KV_WIDTH].astype(BF16)
            dk_c, dk_p, dv_c, dv_p = [], [], [], []
            for kv in range(2):
                k2c, k2p = _dup_half(kn_c, kv, lo), _dup_half(kn_p, kv, lo)
                v2c, v2p = _dup_half(v_c, kv, lo), _dup_half(v_p, kv, lo)
                cols = [slice((2 * kv + t) * LANES, (2 * kv + t + 1) * LANES) for t in range(2)]
                normed = [_head_norm(q_ref[pl.ds(r0, BLOCK), c].astype(F32), qg, lo) for c in cols]
                q4 = _stack_heads(normed[0][0], normed[1][0], lo).astype(BF16)
                do4 = _stack_heads(dy_ref[pl.ds(r0, BLOCK), cols[0]], dy_ref[pl.ds(r0, BLOCK), cols[1]], lo)
                p_c, p_p, p_s = _att_probs(q4, k2c, k2p, bias_ref[kv, 0], bias_ref[kv, 1], sinkcol_ref[kv], has_prev)
                dp_c = _dot_nt(do4, v2c)
                dp_p = _dot_nt(do4, v2p)
                delta = jnp.sum(p_c * dp_c + p_p * dp_p, axis=-1, keepdims=True)
                ds_c = (p_c * (dp_c - delta)).astype(BF16)
                ds_p = (p_p * (dp_p - delta)).astype(BF16)
                sink_acc[kv] += -(p_s * delta)
                dq4 = (_dot_nn(ds_c, k2c) + _dot_nn(ds_p, k2p)) * ATT_SCALE
                for c, (_, qh, qr), dqn in zip(cols, normed, _unstack_heads(dq4, lo)):
                    dq, dg = _head_norm_bwd(qh, qr, qg, dqn, lo)
                    dqkv_ref[pl.ds(r0, BLOCK), c] = dq.astype(BF16)
                    qg_acc[...] += dg
                dk_c.append(_dot_tn(ds_c, q4))
                dk_p.append(_dot_tn(ds_p, q4))
                dv_c.append(_dot_tn(p_c.astype(BF16), do4))
                dv_p.append(_dot_tn(p_p.astype(BF16), do4))

            def fold(parts):
                a = parts[0] + pltpu.roll(parts[0], LANES // 2, axis=1)
                b = parts[1] + pltpu.roll(parts[1], LANES // 2, axis=1)
                return jnp.where(lo, a, b)

            dkn_acc[pl.ds(r0, BLOCK), :] += fold(dk_c) * ATT_SCALE
            dkn_acc[pl.ds(rp, BLOCK), :] += fold(dk_p) * ATT_SCALE
            dv_acc[pl.ds(r0, BLOCK), :] += fold(dv_c)
            dv_acc[pl.ds(rp, BLOCK), :] += fold(dv_p)
            return carry

        lax.fori_loop(0, nb, block, 0)

        def finish(i, carry):
            r0 = pl.multiple_of(i * BLOCK, BLOCK)
            _, kh, kr = _head_norm(kv_ref[pl.ds(r0, BLOCK), 0:KV_WIDTH].astype(F32), kg, lo)
            dk, dg = _head_norm_bwd(kh, kr, kg, dkn_acc[pl.ds(r0, BLOCK), :], lo)
            dqkv_ref[pl.ds(r0, BLOCK), ATT_WIDTH:ATT_WIDTH + KV_WIDTH] = dk.astype(BF16)
            dqkv_ref[pl.ds(r0, BLOCK), ATT_WIDTH + KV_WIDTH:QKV_WIDTH] = dv_acc[pl.ds(r0, BLOCK), :].astype(BF16)
            kg_acc[...] += dg
            return carry

        lax.fori_loop(0, nb, finish, 0)

        @pl.when(pl.program_id(0) == n_seq - 1)
        def _():
            dqg_ref[...] = jnp.sum(qg_acc[...], axis=0, keepdims=True)
            dkg_ref[...] = jnp.sum(kg_acc[...], axis=0, keepdims=True)
            lane = lax.broadcasted_iota(jnp.int32, (1, LANES), 1)
            dsink = jnp.zeros((1, LANES), F32)
            for kv in range(2):
                for r in range(Q_GROUP):
                    total = jnp.sum(sink_acc[kv, r * BLOCK:(r + 1) * BLOCK, :], axis=0, keepdims=True)
                    dsink = jnp.where(lane == Q_GROUP * kv + r, total, dsink)
            dsink_ref[...] = dsink

    vec = pl.BlockSpec((1, LANES), lambda b: (0, 0))
    acc = pltpu.VMEM((BLOCK, LANES), F32)
    body, dep_specs, dep_args = _with_deps(body, 6, deps)
    dqkv, dqg, dkg, dsink = pl.pallas_call(
        body, name=name, grid=(n_seq,),
        in_specs=[pl.BlockSpec((seq, ATT_WIDTH), lambda b: (b, qcol)),
                  pl.BlockSpec((seq, 2 * KV_WIDTH), lambda b: (b, kvcol)),
                  pl.BlockSpec((seq, ATT_WIDTH), lambda b: (b, 0)),
                  vec, vec, pl.BlockSpec(memory_space=pltpu.SMEM)] + dep_specs,
        out_specs=[pl.BlockSpec((seq, QKV_WIDTH), lambda b: (b, 0)), vec, vec, vec],
        out_shape=[jax.ShapeDtypeStruct((T, QKV_WIDTH), BF16)] + [jax.ShapeDtypeStruct((1, LANES), F32)] * 3,
        scratch_shapes=[pltpu.VMEM((seq, KV_WIDTH), F32), pltpu.VMEM((seq, KV_WIDTH), F32), acc, acc,
                        pltpu.VMEM((2, GROUP_ROWS, 1), F32), *ATT_SCRATCH],
        compiler_params=_params(("arbitrary",)),
    )(proj, proj, dy, jnp.tile(q_gain, 2).reshape(1, LANES), jnp.tile(k_gain, 2).reshape(1, LANES), sinks, *dep_args)
    half = LANES // 2
    return dqkv, dqg[0, :half] + dqg[0, half:], dkg[0, :half] + dkg[0, half:], dsink[0, :N_Q_HEADS]


def _sgu_weights(w_ref):
    r = lax.broadcasted_iota(jnp.int32, (BLOCK, BLOCK), 0)
    c = lax.broadcasted_iota(jnp.int32, (BLOCK, BLOCK), 1)
    return [jnp.where(r >= c, w_ref[g], 0.0).astype(BF16) for g in range(SGU_GROUPS)]


def _sgu_fwd(proj, gain, w_s, bias_full, *, n_seq, seq, name):
    T = n_seq * seq
    nc = seq // BLOCK

    def body(suv_ref, g_ref, w_ref, b_ref, y_ref):
        lo = _lo_mask((BLOCK, LANES))
        wm = _sgu_weights(w_ref)
        gain_v = g_ref[...]

        def chunk(c, carry):
            r0 = pl.multiple_of(c * BLOCK, BLOCK)
            gv = _gelu(suv_ref[pl.ds(r0, BLOCK), SGU_WIDTH:2 * SGU_WIDTH].astype(F32))
            r = lax.rsqrt(jnp.mean(gv * gv, axis=-1, keepdims=True) + NORM_EPS)
            vn = (gv * r * gain_v).astype(BF16)
            for p in range(SGU_WIDTH // LANES):
                cols = slice(p * LANES, (p + 1) * LANES)
                vp = vn[:, cols]
                mixed = jnp.where(lo, _dot_nn(wm[2 * p], vp), _dot_nn(wm[2 * p + 1], vp)) + b_ref[:, cols]
                u = _gelu(suv_ref[pl.ds(r0, BLOCK), cols].astype(F32))
                y_ref[pl.ds(r0, BLOCK), cols] = (u * mixed).astype(BF16)
            return carry

        lax.fori_loop(0, nc, chunk, 0)

    return pl.pallas_call(
        body, name=name, grid=(n_seq,),
        in_specs=[pl.BlockSpec((seq, 2 * SGU_WIDTH), lambda b: (b, COL_SUV // (2 * SGU_WIDTH))),
                  pl.BlockSpec((1, SGU_WIDTH), lambda b: (0, 0)),
                  pl.BlockSpec((SGU_GROUPS, BLOCK, BLOCK), lambda b: (0, 0, 0)),
                  pl.BlockSpec((BLOCK, SGU_WIDTH), lambda b: (0, 0))],
        out_specs=pl.BlockSpec((seq, SGU_WIDTH), lambda b: (b, 0)),
        out_shape=jax.ShapeDtypeStruct((T, SGU_WIDTH), BF16),
        compiler_params=_params(("parallel",)),
    )(proj, gain.reshape(1, SGU_WIDTH), w_s, bias_full)


def _sgu_bwd(proj, dy, gain, w_s, bias_full, *, n_seq, seq, name, deps=()):
    T = n_seq * seq
    nc = seq // BLOCK
    n_tiles = SGU_WIDTH // LANES

    def body(suv_ref, dy_ref, g_ref, w_ref, b_ref, dsuv_ref, dg_ref, dw_ref, db_ref, dg_acc, dw_acc, db_acc):
        lo = _lo_mask((BLOCK, LANES))
        hi = jnp.logical_not(lo)
        wm = _sgu_weights(w_ref)
        wmt = [jnp.where(lax.broadcasted_iota(jnp.int32, (BLOCK, BLOCK), 1) >= lax.broadcasted_iota(jnp.int32, (BLOCK, BLOCK), 0),
                         w_ref[g].T, 0.0).astype(BF16) for g in range(SGU_GROUPS)]
        gain_v = g_ref[...]

        @pl.when(pl.program_id(0) == 0)
        def _():
            dg_acc[...] = jnp.zeros_like(dg_acc)
            dw_acc[...] = jnp.zeros_like(dw_acc)
            db_acc[...] = jnp.zeros_like(db_acc)

        def chunk(c, carry):
            r0 = pl.multiple_of(c * BLOCK, BLOCK)
            gv, dgelu_v = _gelu_and_grad(suv_ref[pl.ds(r0, BLOCK), SGU_WIDTH:2 * SGU_WIDTH].astype(F32))
            r = lax.rsqrt(jnp.mean(gv * gv, axis=-1, keepdims=True) + NORM_EPS)
            vh = gv * r
            vn = (vh * gain_v).astype(BF16)
            dvn_tiles = []
            for p in range(n_tiles):
                cols = slice(p * LANES, (p + 1) * LANES)
                vp = vn[:, cols]
                mixed = jnp.where(lo, _dot_nn(wm[2 * p], vp), _dot_nn(wm[2 * p + 1], vp)) + b_ref[:, cols]
                u, dgelu_u = _gelu_and_grad(suv_ref[pl.ds(r0, BLOCK), cols].astype(F32))
                dyv = dy_ref[pl.ds(r0, BLOCK), cols]
                dsuv_ref[pl.ds(r0, BLOCK), cols] = (dyv * mixed * dgelu_u).astype(BF16)
                dm = dyv * u
                db_acc[:, cols] += dm
                dm_bf = dm.astype(BF16)
                dvn_tiles.append(jnp.where(lo, _dot_nn(wmt[2 * p], dm_bf), _dot_nn(wmt[2 * p + 1], dm_bf)))
                dw_acc[2 * p] += _dot_nt(jnp.where(lo, dm, 0.0).astype(BF16), vp)
                dw_acc[2 * p + 1] += _dot_nt(jnp.where(hi, dm, 0.0).astype(BF16), vp)
            dvn = jnp.concatenate(dvn_tiles, axis=1)
            dg_acc[...] += dvn * vh
            dvh = dvn * gain_v
            dgv = r * (dvh - vh * jnp.mean(dvh * vh, axis=-1, keepdims=True))
            dsuv_ref[pl.ds(r0, BLOCK), SGU_WIDTH:2 * SGU_WIDTH] = (dgv * dgelu_v).astype(BF16)
            return carry

        lax.fori_loop(0, nc, chunk, 0)

        @pl.when(pl.program_id(0) == n_seq - 1)
        def _():
            dg_ref[...] = jnp.sum(dg_acc[...], axis=0, keepdims=True)
            r = lax.broadcasted_iota(jnp.int32, (BLOCK, BLOCK), 0)
            c = lax.broadcasted_iota(jnp.int32, (BLOCK, BLOCK), 1)
            for g in range(SGU_GROUPS):
                dw_ref[g] = jnp.where(r >= c, dw_acc[g], 0.0)
            lane = lax.broadcasted_iota(jnp.int32, (BLOCK, LANES), 1)
            out = jnp.zeros((BLOCK, LANES), F32)
            for p in range(n_tiles):
                tile = db_acc[:, p * LANES:(p + 1) * LANES]
                s_lo = jnp.sum(jnp.where(lo, tile, 0.0), axis=-1, keepdims=True)
                s_hi = jnp.sum(jnp.where(hi, tile, 0.0), axis=-1, keepdims=True)
                out = jnp.where(lane == 2 * p, s_lo, out)
                out = jnp.where(lane == 2 * p + 1, s_hi, out)
            db_ref[...] = out

    body, dep_specs, dep_args = _with_deps(body, 5, deps)
    dsuv, dg, dw, db = pl.pallas_call(
        body, name=name, grid=(n_seq,),
        in_specs=[pl.BlockSpec((seq, 2 * SGU_WIDTH), lambda b: (b, COL_SUV // (2 * SGU_WIDTH))),
                  pl.BlockSpec((seq, SGU_WIDTH), lambda b: (b, 0)),
                  pl.BlockSpec((1, SGU_WIDTH), lambda b: (0, 0)),
                  pl.BlockSpec((SGU_GROUPS, BLOCK, BLOCK), lambda b: (0, 0, 0)),
                  pl.BlockSpec((BLOCK, SGU_WIDTH), lambda b: (0, 0))] + dep_specs,
        out_specs=[pl.BlockSpec((seq, 2 * SGU_WIDTH), lambda b: (b, 0)),
                   pl.BlockSpec((1, SGU_WIDTH), lambda b: (0, 0)),
                   pl.BlockSpec((SGU_GROUPS, BLOCK, BLOCK), lambda b: (0, 0, 0)),
                   pl.BlockSpec((BLOCK, LANES), lambda b: (0, 0))],
        out_shape=[jax.ShapeDtypeStruct((T, 2 * SGU_WIDTH), BF16), jax.ShapeDtypeStruct((1, SGU_WIDTH), F32),
                   jax.ShapeDtypeStruct((SGU_GROUPS, BLOCK, BLOCK), F32), jax.ShapeDtypeStruct((BLOCK, LANES), F32)],
        scratch_shapes=[pltpu.VMEM((BLOCK, SGU_WIDTH), F32), pltpu.VMEM((SGU_GROUPS, BLOCK, BLOCK), F32),
                        pltpu.VMEM((BLOCK, SGU_WIDTH), F32)],
        compiler_params=_params(("arbitrary",)),
    )(proj, dy, gain.reshape(1, SGU_WIDTH), w_s, bias_full, *dep_args)
    return dsuv, dg.reshape(SGU_WIDTH), dw, db[:, :SGU_GROUPS].T


def _merge_fwd(y_att, y_sgu, w_oa, w_ob, proj, *, name, tm=1024, tn=512, deps=()):
    T = y_att.shape[0]

    def body(ya_ref, ys_ref, wa_ref, wb_ref, ga_ref, gb_ref, o_ref):
        pa = _dot_nn(ya_ref[...], wa_ref[...])
        pb = _dot_nn(ys_ref[...], wb_ref[...])
        o_ref[...] = (_sigmoid(ga_ref[...].astype(F32)) * pa + _sigmoid(gb_ref[...].astype(F32)) * pb).astype(BF16)

    act = pl.BlockSpec((tm, ATT_WIDTH), lambda i, j: (i, 0))
    wgt = pl.BlockSpec((ATT_WIDTH, tn), lambda i, j: (0, j))
    body, dep_specs, dep_args = _with_deps(body, 6, deps)
    return pl.pallas_call(
        body, name=name, grid=(T // tm, D_MODEL // tn),
        in_specs=[act, act, wgt, wgt,
                  pl.BlockSpec((tm, tn), lambda i, j: (i, j + COL_GA // tn)),
                  pl.BlockSpec((tm, tn), lambda i, j: (i, j + COL_GB // tn))] + dep_specs,
        out_specs=pl.BlockSpec((tm, tn), lambda i, j: (i, j)),
        out_shape=jax.ShapeDtypeStruct((T, D_MODEL), BF16),
        compiler_params=_params(("parallel", "parallel")),
    )(y_att, y_sgu, w_oa, w_ob, proj, proj, *dep_args)


def _merge_bwd(dx1_bf, w_out, y_att, y_sgu, w_oa, w_ob, proj, *, name, tm=1024, tn=512):
    T = y_att.shape[0]

    def body(dx_ref, wo_ref, ya_ref, ys_ref, wa_ref, wb_ref, ga_ref, gb_ref, dpa_ref, dpb_ref, dga_ref, dgb_ref):
        dm = _dot_nt(dx_ref[...], wo_ref[...])
        pa = _dot_nn(ya_ref[...], wa_ref[...])
        pb = _dot_nn(ys_ref[...], wb_ref[...])
        sa = _sigmoid(ga_ref[...].astype(F32))
        sb = _sigmoid(gb_ref[...].astype(F32))
        dpa_ref[...] = (dm * sa).astype(BF16)
        dpb_ref[...] = (dm * sb).astype(BF16)
        dga_ref[...] = (dm * pa * sa * (1.0 - sa)).astype(BF16)
        dgb_ref[...] = (dm * pb * sb * (1.0 - sb)).astype(BF16)

    act = pl.BlockSpec((tm, ATT_WIDTH), lambda i, j: (i, 0))
    wgt = pl.BlockSpec((ATT_WIDTH, tn), lambda i, j: (0, j))
    out = pl.BlockSpec((tm, tn), lambda i, j: (i, j))
    return pl.pallas_call(
        body, name=name, grid=(T // tm, D_MODEL // tn),
        in_specs=[pl.BlockSpec((tm, D_MODEL), lambda i, j: (i, 0)),
                  pl.BlockSpec((tn, D_MODEL), lambda i, j: (j, 0)),
                  act, act, wgt, wgt,
                  pl.BlockSpec((tm, tn), lambda i, j: (i, j + COL_GA // tn)),
                  pl.BlockSpec((tm, tn), lambda i, j: (i, j + COL_GB // tn))],
        out_specs=[out] * 4,
        out_shape=[jax.ShapeDtypeStruct((T, D_MODEL), BF16)] * 4,
        compiler_params=_params(("parallel", "parallel")),
    )(dx1_bf, w_out, y_att, y_sgu, w_oa, w_ob, proj, proj)


CONV_ROWS = 256
CONV_TN = 256


def _shift_rows(cur, prev8, k):
    rolled = pltpu.roll(cur, k, axis=0)
    head = jnp.where(lax.broadcasted_iota(jnp.int32, prev8.shape, 0) < k, pltpu.roll(prev8, k, axis=0), rolled[:SUBLANES])
    return jnp.concatenate([head, rolled[SUBLANES:]], axis=0)


def _shift_rows_up(cur, next8, k):
    n = cur.shape[0]
    rolled = pltpu.roll(cur, n - k, axis=0)
    tail = jnp.where(lax.broadcasted_iota(jnp.int32, next8.shape, 0) >= SUBLANES - k,
                     pltpu.roll(next8, SUBLANES - k, axis=0), rolled[n - SUBLANES:])
    return jnp.concatenate([rolled[:n - SUBLANES], tail], axis=0)


HALO_ROWS = 16


def _rows_before(z_ref, r0, first):
    rp = pl.multiple_of(jnp.maximum(r0 - HALO_ROWS, 0), HALO_ROWS)
    halo = z_ref[pl.ds(rp, HALO_ROWS), :].astype(F32)
    return jnp.where(first, 0.0, halo[HALO_ROWS - SUBLANES:])


def _conv_rows(z_ref, r0, first, w_ref, b_ref, rows):
    cur = z_ref[pl.ds(r0, rows), :].astype(F32)
    prev8 = _rows_before(z_ref, r0, first)
    z1 = _shift_rows(cur, prev8, 1)
    z2 = _shift_rows(cur, prev8, 2)
    return b_ref[...] + w_ref[0:1, :] * z2 + w_ref[1:2, :] * z1 + w_ref[2:3, :] * cur


def _conv_fwd(z_g, z_v, cw_g, cw_v, cb_g, cb_v, *, n_seq, seq, name):
    T = n_seq * seq
    tn, rows = CONV_TN, CONV_ROWS

    def body(zg_ref, zv_ref, wg_ref, wv_ref, bg_ref, bv_ref, a_ref):
        def step(s, carry):
            r0 = pl.multiple_of(s * rows, rows)
            first = s == 0
            g = _conv_rows(zg_ref, r0, first, wg_ref, bg_ref, rows)
            v = _conv_rows(zv_ref, r0, first, wv_ref, bv_ref, rows)
            a_ref[pl.ds(r0, rows), :] = (g * _sigmoid(g) * v).astype(BF16)
            return carry

        lax.fori_loop(0, seq // rows, step, 0)

    zs = pl.BlockSpec((seq, tn), lambda b, j: (b, j))
    ws = pl.BlockSpec((3, tn), lambda b, j: (0, j))
    bs = pl.BlockSpec((1, tn), lambda b, j: (0, j))
    return pl.pallas_call(
        body, name=name, grid=(n_seq, D_FF // tn),
        in_specs=[zs, zs, ws, ws, bs, bs], out_specs=zs,
        out_shape=jax.ShapeDtypeStruct((T, D_FF), BF16),
        compiler_params=_params(("parallel", "parallel")),
    )(z_g, z_v, cw_g, cw_v, cb_g.reshape(1, D_FF), cb_v.reshape(1, D_FF))


def _conv_bwd(z_g, z_v, da, cw_g, cw_v, cb_g, cb_v, *, n_seq, seq, name):
    T = n_seq * seq
    tn, rows = CONV_TN, CONV_ROWS
    n_steps = seq // rows

    def body(zg_ref, zv_ref, da_ref, wg_ref, wv_ref, bg_ref, bv_ref,
             dzg_ref, dzv_ref, dwg_ref, dwv_ref, dbg_ref, dbv_ref, dcg_ref, dcv_ref):
        def grads(s, accs):
            r0 = pl.multiple_of(s * rows, rows)
            first = s == 0
            cur_g = zg_ref[pl.ds(r0, rows), :].astype(F32)
            cur_v = zv_ref[pl.ds(r0, rows), :].astype(F32)
            pg = _rows_before(zg_ref, r0, first)
            pv = _rows_before(zv_ref, r0, first)
            g1, g2 = _shift_rows(cur_g, pg, 1), _shift_rows(cur_g, pg, 2)
            v1, v2 = _shift_rows(cur_v, pv, 1), _shift_rows(cur_v, pv, 2)
            g = bg_ref[...] + wg_ref[0:1, :] * g2 + wg_ref[1:2, :] * g1 + wg_ref[2:3, :] * cur_g
            v = bv_ref[...] + wv_ref[0:1, :] * v2 + wv_ref[1:2, :] * v1 + wv_ref[2:3, :] * cur_v
            sg = _sigmoid(g)
            dav = da_ref[pl.ds(r0, rows), :].astype(F32)
            dcg = dav * v * (sg * (1.0 + g * (1.0 - sg)))
            dcv = dav * (g * sg)
            dcg_ref[pl.ds(r0, rows), :] = dcg
            dcv_ref[pl.ds(r0, rows), :] = dcv

            def colsum(x):
                return jnp.sum(x, axis=0, keepdims=True)

            return (accs[0] + colsum(dcg * g2), accs[1] + colsum(dcg * g1), accs[2] + colsum(dcg * cur_g), accs[3] + colsum(dcg),
                    accs[4] + colsum(dcv * v2), accs[5] + colsum(dcv * v1), accs[6] + colsum(dcv * cur_v), accs[7] + colsum(dcv))

        zero = jnp.zeros((1, tn), F32)
        sums = lax.fori_loop(0, n_steps, grads, (zero,) * 8)
        first_seq = pl.program_id(1) == 0

        @pl.when(first_seq)
        def _():
            dwg_ref[...] = jnp.concatenate(sums[0:3], axis=0)
            dbg_ref[...] = sums[3]
            dwv_ref[...] = jnp.concatenate(sums[4:7], axis=0)
            dbv_ref[...] = sums[7]

        @pl.when(jnp.logical_not(first_seq))
        def _():
            dwg_ref[...] += jnp.concatenate(sums[0:3], axis=0)
            dbg_ref[...] += sums[3]
            dwv_ref[...] += jnp.concatenate(sums[4:7], axis=0)
            dbv_ref[...] += sums[7]

        def back(s, carry):
            r0 = pl.multiple_of(s * rows, rows)
            last = s == n_steps - 1
            rn = pl.multiple_of(jnp.minimum(r0 + rows, seq - SUBLANES), SUBLANES)
            for dc_ref, w_ref, dz_ref in ((dcg_ref, wg_ref, dzg_ref), (dcv_ref, wv_ref, dzv_ref)):
                cur = dc_ref[pl.ds(r0, rows), :]
                nxt = jnp.where(last, 0.0, dc_ref[pl.ds(rn, SUBLANES), :])
                u1, u2 = _shift_rows_up(cur, nxt, 1), _shift_rows_up(cur, nxt, 2)
                dz_ref[pl.ds(r0, rows), :] = (w_ref[2:3, :] * cur + w_ref[1:2, :] * u1 + w_ref[0:1, :] * u2).astype(BF16)
            return carry

        lax.fori_loop(0, n_steps, back, 0)

    zs = pl.BlockSpec((seq, tn), lambda j, b: (b, j))
    ws = pl.BlockSpec((3, tn), lambda j, b: (0, j))
    bs = pl.BlockSpec((1, tn), lambda j, b: (0, j))
    outs = pl.pallas_call(
        body, name=name, grid=(D_FF // tn, n_seq),
        in_specs=[zs, zs, zs, ws, ws, bs, bs],
        out_specs=[zs, zs, ws, ws, bs, bs],
        out_shape=[jax.ShapeDtypeStruct((T, D_FF), BF16)] * 2 + [jax.ShapeDtypeStruct((3, D_FF), F32)] * 2
        + [jax.ShapeDtypeStruct((1, D_FF), F32)] * 2,
        scratch_shapes=[pltpu.VMEM((seq, tn), F32), pltpu.VMEM((seq, tn), F32)],
        compiler_params=_params(("parallel", "arbitrary")),
    )(z_g, z_v, da, cw_g, cw_v, cb_g.reshape(1, D_FF), cb_v.reshape(1, D_FF))
    dz_g, dz_v, dw_g, dw_v, db_g, db_v = outs
    return dz_g, dz_v, dw_g, dw_v, db_g.reshape(D_FF), db_v.reshape(D_FF)


def _layer_fwd(x, h, w, sched, tail, *, n_seq, seq, l):
    tag = f"l{l}"
    deps = sched("fwd_start", l, x)
    proj = _mm(h, w["w_in_t"], mode="nt", out_dtype=ACT_DTYPE, rotate=W_IN_ROTATE, name=f"{tag}_proj", deps=deps)
    y_att = _attention_fwd(proj, w["q_norm"], w["k_norm"], w["sinks"], n_seq=n_seq, seq=seq, name=f"{tag}_att")
    deps = sched("fwd_att", l, y_att)
    y_sgu = _sgu_fwd(proj, w["sgu_norm"], w["w_s"], w["bias_full"], n_seq=n_seq, seq=seq, name=f"{tag}_sgu")
    merged = _merge_fwd(y_att, y_sgu, w["w_oa"], w["w_ob"], proj, name=f"{tag}_merge", deps=deps)
    x1, h2 = _mm_rows(merged, w["w_out"], mode="nn", fn=_residual_then_norm, out_dtypes=(F32, BF16), rows=(x,),
                      vecs=(w["ffn_norm"],), name=f"{tag}_out")
    deps = sched("fwd_mixer_done", l, x1)
    z_g = _mm(h2, w["w_up_t"], mode="nt", out_dtype=ACT_DTYPE, b_rows=(0, D_FF), name=f"{tag}_up_g", deps=deps)
    z_v = _mm(h2, w["w_up_t"], mode="nt", out_dtype=ACT_DTYPE, b_rows=(D_FF, D_FF), name=f"{tag}_up_v")
    a = _conv_fwd(z_g, z_v, w["cw_g"], w["cw_v"], w["cb_g"], w["cb_v"], n_seq=n_seq, seq=seq, name=f"{tag}_conv")
    deps = sched("fwd_conv", l, a)
    if tail[0] == "norm":
        out = _mm_rows(a, w["w_down"], mode="nn", fn=_residual_then_norm, out_dtypes=(F32, BF16), rows=(x1,),
                       vecs=(tail[1],), name=f"{tag}_down", deps=deps)
    else:
        out = _mm_rows(a, w["w_down"], mode="nn", fn=_residual_then_loss, out_dtypes=(F32, BF16), rows=(x1, tail[1]),
                       reduce=True, name=f"{tag}_down", deps=deps)
    saved = dict(x=x, h=h, proj=proj, y_att=y_att, y_sgu=y_sgu, merged=merged, x1=x1, h2=h2, z_g=z_g, z_v=z_v, a=a)
    return out, saved


def _layer_bwd(dx2, dx2_bf, w, s, sched, deps, *, n_seq, seq, l):
    tag = f"l{l}b"
    g = {}
    da = _mm(dx2_bf, w["w_down"], mode="nt", out_dtype=ACT_DTYPE, name=f"{tag}_da", deps=deps)
    g["w_down"] = _mm(s["a"], dx2_bf, mode="tn", out_dtype=F32, name=f"{tag}_dw_down")
    dz_g, dz_v, g["cw_g"], g["cw_v"], g["cb_g"], g["cb_v"] = _conv_bwd(
        s["z_g"], s["z_v"], da, w["cw_g"], w["cw_v"], w["cb_g"], w["cb_v"], n_seq=n_seq, seq=seq, name=f"{tag}_conv")
    dh2_g = _mm(dz_g, w["w_up_t"], mode="nn", out_dtype=F32, b_rows=(0, D_FF), name=f"{tag}_dh2_g")
    dw_up_t = _mm(dz_g, s["h2"], mode="tn", out_dtype=F32, out_rows=(0, 2 * D_FF), name=f"{tag}_dw_up_g")
    g["w_up_t"] = _mm(dz_v, s["h2"], mode="tn", out_dtype=F32, out_rows=(D_FF, 2 * D_FF), out_prev=dw_up_t,
                      name=f"{tag}_dw_up_v")
    deps = sched("bwd_ffn_grads", l, dh2_g, g)
    dx1, dx1_bf, dgain = _mm_rows(dz_v, w["w_up_t"], mode="nn", fn=_sum_then_rms_bwd, out_dtypes=(F32, BF16),
                                  rows=(dh2_g, s["x1"], dx2), vecs=(w["ffn_norm"],), reduce=True, b_rows=(D_FF, D_FF),
                                  name=f"{tag}_dh2_v", deps=deps)
    g["ffn_norm"] = dgain.reshape(D_MODEL)
    dpa, dpb, dga, dgb = _merge_bwd(dx1_bf, w["w_out"], s["y_att"], s["y_sgu"], w["w_oa"], w["w_ob"], s["proj"],
                                    name=f"{tag}_merge")
    deps = sched("bwd_merge", l, dpa)
    g["w_out"] = _mm(s["merged"], dx1_bf, mode="tn", out_dtype=F32, name=f"{tag}_dw_out",
                     deps=deps)
    dy_att = _mm(dpa, w["w_oa"], mode="nt", out_dtype=BF16, name=f"{tag}_dy_att")
    dy_sgu = _mm(dpb, w["w_ob"], mode="nt", out_dtype=F32, name=f"{tag}_dy_sgu")
    g["w_oa"] = _mm(s["y_att"], dpa, mode="tn", out_dtype=F32, name=f"{tag}_dw_oa")
    g["w_ob"] = _mm(s["y_sgu"], dpb, mode="tn", out_dtype=F32, name=f"{tag}_dw_ob")
    deps = sched("bwd_out_grads", l, dy_att, g)
    dqkv, g["q_norm"], g["k_norm"], g["sinks"] = _attention_bwd(
        s["proj"], dy_att, w["q_norm"], w["k_norm"], w["sinks"], n_seq=n_seq, seq=seq, name=f"{tag}_att", deps=deps)
    deps = sched("bwd_att", l, dqkv)
    dsuv, g["sgu_norm"], g["w_s"], g["b_s"] = _sgu_bwd(
        s["proj"], dy_sgu, w["sgu_norm"], w["w_s"], w["bias_full"], n_seq=n_seq, seq=seq, name=f"{tag}_sgu", deps=deps)
    dproj = jnp.concatenate([dsuv, dga, dgb, dqkv], axis=1)
    g["w_in_t"] = _mm(dproj, s["h"], mode="tn", out_dtype=F32, rotate=W_IN_ROTATE, name=f"{tag}_dw_in")
    deps = sched("bwd_w_in_grad", l, dproj, g)
    dx, dx_bf, dgain = _mm_rows(dproj, w["w_in_t"], mode="nn", fn=_rms_bwd_rows, out_dtypes=(F32, BF16),
                                rows=(s["x"], dx1), vecs=(w["mix_norm"],), reduce=True, k_rotate=QKV_WIDTH,
                                name=f"{tag}_dh", deps=deps)
    g["mix_norm"] = dgain.reshape(D_MODEL)
    return dx, dx_bf, g, sched("bwd_dh", l, dx)


def _local_step(x, target, weights, sched, *, n_seq, seq):
    depth = len(weights)
    saved = []
    h = _rms_fwd(x, weights[0]["mix_norm"], name="l0_mix_norm")
    for l in range(depth):
        tail = ("norm", weights[l + 1]["mix_norm"]) if l + 1 < depth else ("loss", target)
        out, s = _layer_fwd(x, h, weights[l], sched, tail, n_seq=n_seq, seq=seq, l=l)
        saved.append(s)
        if l + 1 < depth:
            x, h = out
    dy, dy_bf, loss_cols = out
    grads = [None] * depth
    deps = ()
    for l in reversed(range(depth)):
        dy, dy_bf, grads[l], deps = _layer_bwd(dy, dy_bf, weights[l], saved[l], sched, deps, n_seq=n_seq, seq=seq, l=l)
    return jnp.sum(loss_cols), dy, grads


W_IN_SHARD = IN_WIDTH // N_DEV
W_UP_SHARD = 2 * D_FF // N_DEV
COL_MOVE_ROWS = 256


def _w_o_moves():
    return tuple((j, 0, LANES, 0, j * LANES) for j in range(N_DEV))


def _disassemble(mats, w, moves, *, name):
    R = mats[0].shape[0]
    tr = min(R, COL_MOVE_ROWS)
    n = len(mats)

    def body(*refs):
        m_refs, o_ref = refs[:n], refs[n]
        for j, lo, hi, which, at in moves:
            o_ref[j, :, lo:hi] = m_refs[which][:, at:at + hi - lo]

    return pl.pallas_call(
        body, name=name, grid=(R // tr,),
        in_specs=[pl.BlockSpec((tr, m.shape[1]), lambda i: (i, 0)) for m in mats],
        out_specs=pl.BlockSpec((N_DEV, tr, w), lambda i: (0, i, 0)),
        out_shape=jax.ShapeDtypeStruct((N_DEV, R, w), mats[0].dtype),
        compiler_params=_params(("parallel",)),
    )(*mats)


def _my_place():
    return lax.axis_index("x"), lax.axis_index("y"), lax.axis_index("c")


def _gathered_shape(shape, kind):
    r, c = shape
    return {"blocks": (N_DEV, r, c), "rows": (N_DEV * r, c), "cols": (r, N_DEV * c)}[kind]


def _gather_window(ref, kind, shape, j):
    r, c = shape
    if kind == "blocks":
        return ref.at[j]
    if kind == "rows":
        return ref.at[pl.ds(pl.multiple_of(j * r, r), r), :]
    return ref.at[:, pl.ds(pl.multiple_of(j * c, c), c)]


def _gather(srcs, kinds, *, name):
    n = len(srcs)
    shapes = [s.shape for s in srcs]
    per = 7

    def body(*refs):
        src_refs, dst_refs = refs[:n], refs[n:2 * n]
        send_sems, recv_sems, local_sems = refs[2 * n:]
        x, y, c = _my_place()
        me, sibling = (x, y, c), (x, y, 1 - c)
        chips = [(1 - x, y), (x, 1 - y), (1 - x, 1 - y)]

        def at(i, px, py, pc):
            return _gather_window(dst_refs[i], kinds[i], shapes[i], 4 * px + 2 * py + pc)

        def copy(i, k, block, to, src=None):
            return pltpu.make_async_remote_copy(
                src_ref=at(i, *block) if src is None else src, dst_ref=at(i, *block),
                send_sem=send_sems.at[per * i + k], recv_sem=recv_sems.at[per * i + k], device_id=to, device_id_type=MESH)

        mine = [pltpu.make_async_copy(src_refs[i], at(i, *me), local_sems.at[i]) for i in range(n)]
        for cp in mine:
            cp.start()
        started = []
        for i in range(n):
            first = [copy(i, 0, me, sibling, src=src_refs[i])]
            first += [copy(i, 1 + j, me, (*chip, c), src=src_refs[i]) for j, chip in enumerate(chips)]
            for cp in first:
                cp.start()
            started += first
        for i in range(n):
            for j, chip in enumerate(chips):
                copy(i, 1 + j, (*chip, c), me).wait_recv()
                fwd = copy(i, 4 + j, (*chip, c), sibling)
                fwd.start()
                started.append(fwd)
        for i in range(n):
            copy(i, 0, sibling, me).wait_recv()
            for j, chip in enumerate(chips):
                copy(i, 4 + j, (*chip, 1 - c), me).wait_recv()
        for cp in started:
            cp.wait_send()
        for cp in mine:
            cp.wait()

    return pl.pallas_call(
        body, name=name,
        out_shape=[jax.ShapeDtypeStruct(_gathered_shape(s.shape, k), s.dtype) for s, k in zip(srcs, kinds)],
        in_specs=[ANY] * n, out_specs=[ANY] * n,
        scratch_shapes=[pltpu.SemaphoreType.DMA((per * n,)), pltpu.SemaphoreType.DMA((per * n,)),
                        pltpu.SemaphoreType.DMA((n,))],
    )(*srcs)


HBM = pl.BlockSpec(memory_space=pltpu.HBM)
SEM = pl.BlockSpec(memory_space=pltpu.SEMAPHORE)
TOKEN = jax.ShapeDtypeStruct((SUBLANES, LANES), F32)
TOKEN_SPEC = pl.BlockSpec(memory_space=pltpu.VMEM)
SPLIT_PARAMS = pltpu.CompilerParams(has_side_effects=pltpu.SideEffectType.DATAFLOW_SIDE_EFFECTING)


def _in_hbm(x):
    return pltpu.with_memory_space_constraint(x, pltpu.HBM)


def _hbm_like(shape, dtype):
    return pltpu.HBM(shape, dtype)


def _place_own(shards, kinds, dtypes, *, name):
    n = len(shards)
    shapes = [s.shape for s in shards]

    def body(*refs):
        s_refs, land_refs, bufs, sems = refs[:n], refs[n:2 * n], refs[2 * n:3 * n], refs[3 * n]
        x, y, c = _my_place()
        copies = []
        for i in range(n):
            bufs[i][...] = s_refs[i][...].astype(dtypes[i])
            copies.append(pltpu.make_async_copy(
                bufs[i], _gather_window(land_refs[i], kinds[i], shapes[i], 4 * x + 2 * y + c), sems.at[i]))
        for cp in copies:
            cp.start()
        for cp in copies:
            cp.wait()

    return pl.pallas_call(
        body, name=name,
        out_shape=[jax.ShapeDtypeStruct(_gathered_shape(s, k), d) for s, k, d in zip(shapes, kinds, dtypes)],
        in_specs=[pl.BlockSpec(memory_space=pltpu.VMEM)] * n, out_specs=[ANY] * n,
        scratch_shapes=[pltpu.VMEM(s, d) for s, d in zip(shapes, dtypes)] + [pltpu.SemaphoreType.DMA((n,))],
        compiler_params=_params(),
    )(*shards)


def _gather_start(lands, kinds, shapes, after=(), *, name):
    n = len(lands)
    n_after = len(after)

    def body(*refs):
        land_refs = refs[:n]
        send_sems, recv_sems = refs[n + n_after], refs[n + n_after + 1]
        x, y, c = _my_place()
        targets = [(x, y, 1 - c), (1 - x, y, c), (x, 1 - y, c), (1 - x, 1 - y, c)]
        for i in range(n):
            own = _gather_window(land_refs[i], kinds[i], shapes[i], 4 * x + 2 * y + c)
            for k, to in enumerate(targets):
                pltpu.make_async_remote_copy(
                    src_ref=own, dst_ref=own, send_sem=send_sems.at[4 * i + k], recv_sem=recv_sems.at[4 * i + k],
                    device_id=to, device_id_type=MESH).start()
        refs[-1][...] = jnp.zeros_like(refs[-1])

    outs = pl.pallas_call(
        body, name=name,
        out_shape=[pltpu.SemaphoreType.DMA((4 * n,)), pltpu.SemaphoreType.DMA((4 * n,))]
        + [_hbm_like(a.shape, a.dtype) for a in lands] + [TOKEN],
        in_specs=[HBM] * n + [ANY] * n_after, out_specs=[SEM, SEM] + [HBM] * n + [TOKEN_SPEC],
        input_output_aliases={i: 2 + i for i in range(n)},
        compiler_params=SPLIT_PARAMS,
    )(*[_in_hbm(a) for a in lands], *after)
    return outs[0], outs[1], outs[2:2 + n], outs[-1]


def _gather_forward(recv_sems, lands, kinds, shapes, after, *, name):
    n = len(lands)

    def body(*refs):
        recv_ref, land_refs = refs[0], refs[1:1 + n]
        fwd_send, fwd_recv = refs[2 + n], refs[3 + n]
        token = refs[-1]
        x, y, c = _my_place()
        chips = [(1 - x, y), (x, 1 - y), (1 - x, 1 - y)]
        for i in range(n):
            for j, (px, py) in enumerate(chips):
                block = _gather_window(land_refs[i], kinds[i], shapes[i], 4 * px + 2 * py + c)
                pltpu.make_async_remote_copy(
                    src_ref=block, dst_ref=block, send_sem=fwd_send.at[3 * i + j], recv_sem=recv_ref.at[4 * i + 1 + j],
                    device_id=(px, py, c), device_id_type=MESH).wait_recv()
                pltpu.make_async_remote_copy(
                    src_ref=block, dst_ref=block, send_sem=fwd_send.at[3 * i + j], recv_sem=fwd_recv.at[3 * i + j],
                    device_id=(x, y, 1 - c), device_id_type=MESH).start()
        token[...] = jnp.zeros_like(token)

    outs = pl.pallas_call(
        body, name=name,
        out_shape=[pltpu.SemaphoreType.DMA((3 * n,)), pltpu.SemaphoreType.DMA((3 * n,))]
        + [_hbm_like(a.shape, a.dtype) for a in lands] + [TOKEN],
        in_specs=[SEM] + [HBM] * n + [ANY], out_specs=[SEM, SEM] + [HBM] * n + [TOKEN_SPEC],
        input_output_aliases={1 + i: 2 + i for i in range(n)},
        compiler_params=SPLIT_PARAMS,
    )(recv_sems, *lands, after)
    return outs[0], outs[1], outs[2:2 + n], outs[-1]


def _gather_finish(send_sems, recv_sems, fwd_send, fwd_recv, lands, kinds, shapes, after, *, name):
    n = len(lands)

    def body(*refs):
        send_ref, recv_ref, fsend_ref, frecv_ref = refs[:4]
        land_refs = refs[4:4 + n]
        x, y, c = _my_place()
        chips = [(1 - x, y), (x, 1 - y), (1 - x, 1 - y)]
        sibling = (x, y, 1 - c)
        for i in range(n):
            def window(j):
                return _gather_window(land_refs[i], kinds[i], shapes[i], j)

            mine, theirs = window(4 * x + 2 * y + c), window(4 * x + 2 * y + (1 - c))
            pltpu.make_async_remote_copy(src_ref=mine, dst_ref=theirs, send_sem=send_ref.at[4 * i],
                                         recv_sem=recv_ref.at[4 * i], device_id=sibling, device_id_type=MESH).wait_recv()
            for j, (px, py) in enumerate(chips):
                block = window(4 * px + 2 * py + (1 - c))
                pltpu.make_async_remote_copy(src_ref=block, dst_ref=block, send_sem=fsend_ref.at[3 * i + j],
                                             recv_sem=frecv_ref.at[3 * i + j], device_id=sibling,
                                             device_id_type=MESH).wait_recv()
            for k in range(4):
                pltpu.make_async_remote_copy(src_ref=mine, dst_ref=mine, send_sem=send_ref.at[4 * i + k],
                                             recv_sem=recv_ref.at[4 * i + k], device_id=sibling,
                                             device_id_type=MESH).wait_send()
            for j, (px, py) in enumerate(chips):
                block = window(4 * px + 2 * py + c)
                pltpu.make_async_remote_copy(src_ref=block, dst_ref=block, send_sem=fsend_ref.at[3 * i + j],
                                             recv_sem=frecv_ref.at[3 * i + j], device_id=sibling,
                                             device_id_type=MESH).wait_send()

    return pl.pallas_call(
        body, name=name,
        out_shape=[_hbm_like(a.shape, a.dtype) for a in lands],
        in_specs=[SEM] * 4 + [HBM] * n + [ANY], out_specs=[HBM] * n,
        input_output_aliases={4 + i: i for i in range(n)},
        compiler_params=SPLIT_PARAMS,
    )(send_sems, recv_sems, fwd_send, fwd_recv, *lands, after)


def _pair_plan(src_ref, land_ref, x, y, c):
    return [(src_ref.at[2 * k + (1 - c)], land_ref.at[k], (x, y, 1 - c)) for k in range(N_CHIPS)]


def _chip_plan(src_ref, land_ref, x, y, c):
    chips = [(1 - x, y), (x, 1 - y), (1 - x, 1 - y)]
    return [(src_ref.at[2 * px + py], land_ref.at[k], (px, py, c)) for k, (px, py) in enumerate(chips)]


def _exchange_copies(plan, per, src_refs, land_refs, send_sems, recv_sems):
    x, y, c = _my_place()
    copies = []
    for i, (s_ref, l_ref) in enumerate(zip(src_refs, land_refs)):
        for q, (src, dst, to) in enumerate(plan(s_ref, l_ref, x, y, c)):
            copies.append(pltpu.make_async_remote_copy(
                src_ref=src, dst_ref=dst, send_sem=send_sems.at[per * i + q], recv_sem=recv_sems.at[per * i + q],
                device_id=to, device_id_type=MESH))
    return copies


def _exchange_start(srcs, plan, per, *, name):
    n = len(srcs)

    def body(*refs):
        src_refs, land_refs = refs[:n], refs[n:2 * n]
        send_sems, recv_sems = refs[2 * n], refs[2 * n + 1]
        for cp in _exchange_copies(plan, per, src_refs, land_refs, send_sems, recv_sems):
            cp.start()
        refs[-1][...] = jnp.zeros_like(refs[-1])

    lands = [lax.empty((per,) + s.shape[1:], s.dtype) for s in srcs]
    outs = pl.pallas_call(
        body, name=name,
        out_shape=[pltpu.SemaphoreType.DMA((per * n,)), pltpu.SemaphoreType.DMA((per * n,))]
        + [_hbm_like(s.shape, s.dtype) for s in srcs] + [_hbm_like(a.shape, a.dtype) for a in lands] + [TOKEN],
        in_specs=[HBM] * (2 * n), out_specs=[SEM, SEM] + [HBM] * (2 * n) + [TOKEN_SPEC],
        input_output_aliases={i: 2 + i for i in range(2 * n)},
        compiler_params=SPLIT_PARAMS,
    )(*[_in_hbm(s) for s in srcs], *[_in_hbm(a) for a in lands])
    return outs[0], outs[1], outs[2:2 + n], outs[2 + n:2 + 2 * n], outs[-1]


def _exchange_wait(send_sems, recv_sems, srcs, lands, plan, per, after, *, name):
    n = len(srcs)

    def body(*refs):
        send_ref, recv_ref = refs[0], refs[1]
        src_refs, land_refs = refs[2:2 + n], refs[2 + n:2 + 2 * n]
        copies = _exchange_copies(plan, per, src_refs, land_refs, send_ref, recv_ref)
        for cp in copies:
            cp.wait_recv()
        for cp in copies:
            cp.wait_send()

    outs = pl.pallas_call(
        body, name=name,
        out_shape=[_hbm_like(s.shape, s.dtype) for s in srcs] + [_hbm_like(a.shape, a.dtype) for a in lands],
        in_specs=[SEM, SEM] + [HBM] * (2 * n) + [ANY], out_specs=[HBM] * (2 * n),
        input_output_aliases={2 + i: i for i in range(2 * n)},
        compiler_params=SPLIT_PARAMS,
    )(send_sems, recv_sems, *srcs, *lands, after)
    return outs[:n], outs[n:]


REDUCE_BLOCK_BYTES = 1 << 20


def _row_tile(r, c):
    row_bytes = 4 * (-(-c // LANES) * LANES)
    best = r
    for d in range(SUBLANES, r, SUBLANES):
        if r % d == 0 and d * row_bytes <= REDUCE_BLOCK_BYTES:
            best = d
    return best if r * row_bytes > REDUCE_BLOCK_BYTES else r


def _reduce_pair_sum(blocked, recv, place, wire_dtype, *, name):
    _, r, c = blocked.shape
    tr = _row_tile(r, c)

    def body(place_ref, g_ref, r_ref, own_ref, send_ref):
        s = g_ref[...] + r_ref[...]
        send_ref[...] = s.astype(wire_dtype)

        @pl.when(pl.program_id(1) == place_ref[1])
        def _():
            own_ref[...] = s

    return pl.pallas_call(
        body, name=name,
        grid_spec=pltpu.PrefetchScalarGridSpec(
            num_scalar_prefetch=1, grid=(r // tr, N_CHIPS),
            in_specs=[pl.BlockSpec((None, None, tr, c), lambda i, k, place_ref: (k, place_ref[0], i, 0)),
                      pl.BlockSpec((None, tr, c), lambda i, k, place_ref: (k, i, 0))],
            out_specs=[pl.BlockSpec((tr, c), lambda i, k, place_ref: (i, 0)),
                       pl.BlockSpec((None, tr, c), lambda i, k, place_ref: (k, i, 0))]),
        out_shape=[jax.ShapeDtypeStruct((r, c), F32), jax.ShapeDtypeStruct((N_CHIPS, r, c), wire_dtype)],
        compiler_params=_params(("parallel", "arbitrary")),
    )(place, blocked.reshape(N_CHIPS, 2, r, c), recv)


def _chip_sum(own_ref, r_ref):
    return ((own_ref[...] + r_ref[0].astype(F32)) + r_ref[1].astype(F32)) + r_ref[2].astype(F32)


def _reduce_chip_sum(own, recv, *, name):
    r, c = own.shape
    tr = _row_tile(r, c)

    def body(own_ref, r_ref, o_ref):
        o_ref[...] = _chip_sum(own_ref, r_ref)

    return pl.pallas_call(
        body, name=name, grid=(r // tr,),
        in_specs=[pl.BlockSpec((tr, c), lambda i: (i, 0)), pl.BlockSpec((N_CHIPS - 1, tr, c), lambda i: (0, i, 0))],
        out_specs=pl.BlockSpec((tr, c), lambda i: (i, 0)),
        out_shape=jax.ShapeDtypeStruct((r, c), F32),
        compiler_params=_params(("parallel",)),
    )(own, recv)


def _adamw_math(w, g, m, v):
    nm = ADAM_B1 * m + (1.0 - ADAM_B1) * g
    nv = ADAM_B2 * v + (1.0 - ADAM_B2) * (g * g)
    m_hat = nm / (1.0 - ADAM_B1 ** ADAM_STEP)
    v_hat = nv / (1.0 - ADAM_B2 ** ADAM_STEP)
    return -ADAM_LR * (m_hat / (jnp.sqrt(v_hat) + ADAM_EPS) + ADAM_WD * w), nm, nv


def _adamw(w, g, m, v, *, name):
    shape = w.shape
    C = shape[-1]
    R = math.prod(shape[:-1])
    tr = _row_tile(R, C)

    def body(w_ref, g_ref, m_ref, v_ref, d_ref, nm_ref, nv_ref):
        d_ref[...], nm_ref[...], nv_ref[...] = _adamw_math(w_ref[...], g_ref[...], m_ref[...], v_ref[...])

    spec = pl.BlockSpec((tr, C), lambda i: (i, 0))
    outs = pl.pallas_call(
        body, name=name, grid=(R // tr,),
        in_specs=[spec] * 4, out_specs=[spec] * 3,
        out_shape=[jax.ShapeDtypeStruct((R, C), F32)] * 3,
        compiler_params=_params(("parallel",)),
    )(*[a.reshape(R, C) for a in (w, g, m, v)])
    return tuple(o.reshape(shape) for o in outs)


def _reduce_adamw(own, recv, w, m, v, layer, prev, *, name):
    r, c = own.shape
    tr = _row_tile(r, c)
    n_prev = 0 if prev is None else len(prev)

    def body(own_ref, r_ref, w_ref, m_ref, v_ref, *rest):
        g_ref, d_ref, nm_ref, nv_ref = rest[n_prev:]
        g = _chip_sum(own_ref, r_ref)
        g_ref[...] = g
        d_ref[...], nm_ref[...], nv_ref[...] = _adamw_math(w_ref[...], g, m_ref[...], v_ref[...])

    slot = pl.BlockSpec((None, tr, c), lambda i: (layer, i, 0))
    return pl.pallas_call(
        body, name=name, grid=(r // tr,),
        in_specs=[pl.BlockSpec((tr, c), lambda i: (i, 0)), pl.BlockSpec((N_CHIPS - 1, tr, c), lambda i: (0, i, 0)),
                  slot, slot, slot] + [ANY] * n_prev,
        out_specs=[slot] * 4,
        out_shape=[jax.ShapeDtypeStruct((DEPTH, r, c), F32)] * 4,
        input_output_aliases={5 + k: k for k in range(n_prev)},
        compiler_params=_params(("parallel",)),
    )(own, recv, w, m, v, *(prev or ()))


REPLICATED = (("mix_norm", (D_MODEL,)), ("q_norm", (HEAD_DIM,)), ("k_norm", (HEAD_DIM,)), ("sinks", (N_Q_HEADS,)),
              ("sgu_norm", (SGU_WIDTH,)), ("w_s", (SGU_GROUPS, BLOCK, BLOCK)), ("b_s", (SGU_GROUPS, BLOCK)),
              ("ffn_norm", (D_MODEL,)), ("conv_b", (2 * D_FF,)))
TRANSPOSED = ("w_in", "w_up")
SHARDED = (("w_in", "rows"), ("w_oa", "cols"), ("w_ob", "cols"), ("w_out", "rows"), ("w_up", "rows"),
           ("conv_w", "blocks"), ("w_down", "rows"))
WEIGHT_ORDER = ("mix_norm", "w_in", "q_norm", "k_norm", "sinks", "sgu_norm", "w_s", "b_s", "w_oa", "w_ob", "w_out",
                "ffn_norm", "w_up", "conv_w", "conv_b", "w_down")
MIXER_WEIGHTS = ["w_in", "w_oa", "w_ob", "w_out"]
FFN_WEIGHTS = ["w_up", "conv_w", "w_down"]


def _small_layout():
    segs, off = {}, 0
    for l in range(DEPTH):
        for name, shape in REPLICATED:
            n = math.prod(shape)
            segs[(l, name)] = (off, n)
            off += n
    per_dev = -(-off // (N_DEV * SUBLANES * LANES)) * SUBLANES * LANES
    return segs, off, per_dev


def _pack_small(grads):
    ssegs, total, per_dev = _small_layout()
    flat = jnp.concatenate([grads[l][name].reshape(-1) for (l, name) in ssegs])
    return jnp.pad(flat, (0, N_DEV * per_dev - total)).reshape(N_DEV, per_dev // LANES, LANES)


def _unpack_small(gathered):
    ssegs, _, _ = _small_layout()
    flat = gathered.reshape(-1)
    shapes = dict(REPLICATED)
    return {name: jnp.stack([flat[ssegs[(l, name)][0]:ssegs[(l, name)][0] + ssegs[(l, name)][1]].reshape(shapes[name])
                             for l in range(DEPTH)]) for name, _ in REPLICATED}


def kernel(x, mix_norm, w_in, q_norm, k_norm, sinks, sgu_norm, w_s, b_s, w_oa, w_ob, w_out, ffn_norm, w_up, conv_w, conv_b, w_down, loss_target, m_mix_norm, m_w_in, m_q_norm, m_k_norm, m_sinks, m_sgu_norm, m_w_s, m_b_s, m_w_oa, m_w_ob, m_w_out, m_ffn_norm, m_w_up, m_conv_w, m_conv_b, m_w_down, v_mix_norm, v_w_in, v_q_norm, v_k_norm, v_sinks, v_sgu_norm, v_w_s, v_b_s, v_w_oa, v_w_ob, v_w_out, v_ffn_norm, v_w_up, v_conv_w, v_conv_b, v_w_down):
    W = dict(mix_norm=mix_norm, w_in=w_in, q_norm=q_norm, k_norm=k_norm, sinks=sinks, sgu_norm=sgu_norm, w_s=w_s, b_s=b_s,
             w_oa=w_oa, w_ob=w_ob, w_out=w_out, ffn_norm=ffn_norm, w_up=w_up, conv_w=conv_w, conv_b=conv_b, w_down=w_down)
    M = dict(mix_norm=m_mix_norm, w_in=m_w_in, q_norm=m_q_norm, k_norm=m_k_norm, sinks=m_sinks, sgu_norm=m_sgu_norm,
             w_s=m_w_s, b_s=m_b_s, w_oa=m_w_oa, w_ob=m_w_ob, w_out=m_w_out, ffn_norm=m_ffn_norm, w_up=m_w_up,
             conv_w=m_conv_w, conv_b=m_conv_b, w_down=m_w_down)
    V = dict(mix_norm=v_mix_norm, w_in=v_w_in, q_norm=v_q_norm, k_norm=v_k_norm, sinks=v_sinks, sgu_norm=v_sgu_norm,
             w_s=v_w_s, b_s=v_b_s, w_oa=v_w_oa, w_ob=v_w_ob, w_out=v_w_out, ffn_norm=v_ffn_norm, w_up=v_w_up,
             conv_w=v_conv_w, conv_b=v_conv_b, w_down=v_w_down)
    n_seq, seq, d_model = x.shape
    tokens = n_seq * seq
    mx, my, mc = _my_place()
    place = jnp.stack([mc, 2 * mx + my]).astype(jnp.int32)
    half = N_DEV // 2
    kind_of = dict(SHARDED)
    for name in TRANSPOSED:
        W[name], M[name], V[name] = (jnp.swapaxes(t[name], 1, 2) for t in (W, M, V))

    gather_groups = [[(l, n) for n in names] for l in range(DEPTH) for names in (MIXER_WEIGHTS, FFN_WEIGHTS)]
    started, in_flight = {}, {}
    weights = []
    for l in range(DEPTH):
        w = {name: W[name][l] for name, _ in REPLICATED}
        w["cb_g"], w["cb_v"] = W["conv_b"][l][:D_FF], W["conv_b"][l][D_FF:]
        w["bias_full"] = jnp.repeat(W["b_s"][l].T, SGU_WIDTH // SGU_GROUPS, axis=1)
        weights.append(w)

    def gather_start(gi, after=()):
        shards = [W[name][l] for l, name in gather_groups[gi]]
        kinds = [kind_of[name] for _, name in gather_groups[gi]]
        shapes = [s.shape for s in shards]
        lands = _place_own(shards, kinds, [F32 if name == "conv_w" else BF16 for _, name in gather_groups[gi]],
                           name=f"gather_weights_own_{gi}")
        send, recv, lands, token = _gather_start(lands, kinds, shapes, after, name=f"gather_weights_start_{gi}")
        started[gi] = dict(sems=(send, recv), lands=lands, kinds=kinds, shapes=shapes)
        return token

    def gather_forward(gi, after):
        st = started[gi]
        in_flight[gi] = _gather_forward(st["sems"][1], st["lands"], st["kinds"], st["shapes"], after,
                                        name=f"gather_weights_forward_{gi}")
        return in_flight[gi][3]

    def gather_finish(gi, after):
        st = started.pop(gi)
        fwd_send, fwd_recv, lands_g, _ = in_flight.pop(gi)
        whole = _gather_finish(st["sems"][0], st["sems"][1], fwd_send, fwd_recv, lands_g, st["kinds"], st["shapes"], after,
                               name=f"gather_weights_finish_{gi}")
        for (l, name), arr in zip(gather_groups[gi], whole):
            w = weights[l]
            if name in TRANSPOSED:
                w[name + "_t"] = arr
            elif name == "conv_w":
                w["cw_g"] = arr[:half].transpose(1, 0, 2).reshape(3, D_FF)
                w["cw_v"] = arr[half:].transpose(1, 0, 2).reshape(3, D_FF)
            else:
                w[name] = arr

    reduce_state, results = {}, {}
    wire = {"conv_w": F32, "small": F32}

    def reduce_begin(key, names, arrays):
        send, recv, srcs_, lands_, token = _exchange_start(arrays, _pair_plan, N_CHIPS, name=f"reduce_pair_start_{key}")
        reduce_state[key] = dict(names=names, pair=(send, recv, srcs_, lands_))
        return [token]

    def reduce_pair(key, after):
        st = reduce_state[key]
        send, recv, srcs_, lands_ = st.pop("pair")
        blocked_, from_sibling = _exchange_wait(send, recv, srcs_, lands_, _pair_plan, N_CHIPS, after,
                                                name=f"reduce_pair_wait_{key}")
        sums = [_reduce_pair_sum(b, r, place, wire.get(n if isinstance(n, str) else n[1], BF16),
                                 name=f"reduce_pair_sum_{key}_{i}")
                for i, (n, b, r) in enumerate(zip(st["names"], blocked_, from_sibling))]
        st["own"] = [s[0] for s in sums]
        *st["chip"], token = _exchange_start([s[1] for s in sums], _chip_plan, N_CHIPS - 1, name=f"reduce_chip_start_{key}")
        return [token]

    def reduce_end(key, after):
        st = reduce_state.pop(key)
        send, recv, srcs_, lands_ = st["chip"]
        _, from_chips = _exchange_wait(send, recv, srcs_, lands_, _chip_plan, N_CHIPS - 1, after,
                                       name=f"reduce_chip_wait_{key}")
        done = []
        for n, own, got in zip(st["names"], st["own"], from_chips):
            if n == "small":
                results["small"] = _reduce_chip_sum(own, got, name="reduce_chip_sum_small")
            else:
                l, name = n
                results[name] = _reduce_adamw(own, got, W[name], M[name], V[name], l, results.get(name),
                                              name=f"l{l}_reduce_adamw_{name}")
                done.append(results[name][0])
        return done

    def sched(point, l, carry, g=None):
        deps = []
        if point == "fwd_start" and l == 0:
            token = gather_forward(0, gather_start(0))
            gather_finish(0, token)
            deps = [gather_start(1, [weights[0]["w_out"]])]
        elif point == "fwd_att" and l == 0:
            deps = [gather_forward(1, carry), gather_start(2, [carry])]
        elif point == "fwd_mixer_done" and l == 0:
            gather_finish(1, carry)
            deps = [gather_start(3, [carry])]
        elif point == "fwd_conv" and l == 0:
            deps = [gather_forward(2, carry)]
        elif point == "fwd_start" and l == 1:
            gather_finish(2, carry)
        elif point == "fwd_att" and l == 1:
            deps = [gather_forward(3, carry)]
        elif point == "fwd_mixer_done" and l == 1:
            gather_finish(3, carry)
        elif point == "bwd_ffn_grads":
            if l + 1 < DEPTH:
                deps += reduce_end(f"l{l + 1}_in", g["w_up_t"])
            conv_w = jnp.concatenate([g[k].reshape(3, half, W_UP_SHARD).transpose(1, 0, 2) for k in ("cw_g", "cw_v")])
            deps += reduce_begin(
                f"l{l}_ffn", [(l, "w_down"), (l, "w_up"), (l, "conv_w")],
                [g["w_down"].reshape(N_DEV, D_FF // N_DEV, D_MODEL),
                 g["w_up_t"].reshape(N_DEV, W_UP_SHARD, D_MODEL), conv_w])
        elif point == "bwd_merge":
            deps = reduce_pair(f"l{l}_ffn", carry)
        elif point == "bwd_out_grads":
            deps = reduce_begin(
                f"l{l}_out", [(l, "w_out"), (l, "w_oa"), (l, "w_ob")],
                [g["w_out"].reshape(N_DEV, D_MODEL // N_DEV, D_MODEL),
                 _disassemble((g["w_oa"],), LANES, _w_o_moves(), name=f"l{l}_split_dw_oa"),
                 _disassemble((g["w_ob"],), LANES, _w_o_moves(), name=f"l{l}_split_dw_ob")])
        elif point == "bwd_att":
            deps = reduce_pair(f"l{l}_out", carry) + reduce_end(f"l{l}_ffn", carry)
        elif point == "bwd_w_in_grad":
            deps = reduce_begin(f"l{l}_in", [(l, "w_in")], [g["w_in_t"].reshape(N_DEV, W_IN_SHARD, D_MODEL)])
        elif point == "bwd_dh":
            deps = reduce_pair(f"l{l}_in", carry) + reduce_end(f"l{l}_out", carry)
        return deps

    loss_part, dx, grads = _local_step(x.reshape(tokens, d_model), loss_target.reshape(tokens, d_model), weights, sched,
                                       n_seq=n_seq, seq=seq)
    loss = lax.psum(loss_part, ("x", "y", "c"))

    for g in grads:
        g["conv_b"] = jnp.concatenate([g["cb_g"], g["cb_v"]])
    reduce_begin("small", ["small"], [_pack_small(grads)])
    reduce_end("l0_in", dx)
    reduce_pair("small", results["w_in"][0])
    reduce_end("small", results["w_in"][1])

    G, delta, new_m, new_v = {}, {}, {}, {}
    for name, _ in SHARDED:
        outs = [jnp.swapaxes(o, 1, 2) for o in results[name]] if name in TRANSPOSED else results[name]
        G[name], delta[name], new_m[name], new_v[name] = outs
    G.update(_unpack_small(_gather([results["small"]], ["blocks"], name="gather_small_grads")[0]))
    for name, _ in REPLICATED:
        delta[name], new_m[name], new_v[name] = _adamw(W[name], G[name], M[name], V[name], name=f"adamw_{name}")
    return (loss, dx.reshape(n_seq, seq, d_model), *[G[n] for n in WEIGHT_ORDER], *[delta[n] for n in WEIGHT_ORDER],
            *[new_m[n] for n in WEIGHT_ORDER], *[new_v[n] for n in WEIGHT_ORDER])
```

```python
import math

import jax
import jax.numpy as jnp
from jax import lax
from jax.experimental import pallas as pl
from jax.experimental.pallas import tpu as pltpu

F32 = jnp.float32
BF16 = jnp.bfloat16
ACT_DTYPE = BF16
MESH = pl.DeviceIdType.MESH

DEPTH = 2
D_MODEL = 1024
N_Q_HEADS = 8
HEAD_DIM = 64
ATT_WIDTH = 512
KV_WIDTH = 128
BLOCK = 128
SGU_WIDTH = 512
SGU_GROUPS = 8
IN_WIDTH = 3840
D_FF = 2816
NORM_EPS = 1e-6
NEG_INF = -1e30
ATT_SCALE = HEAD_DIM ** -0.5
ALIBI_SLOPES = tuple(2.0 ** (-(h + 1)) for h in range(N_Q_HEADS))
ADAM_LR, ADAM_B1, ADAM_B2, ADAM_EPS, ADAM_WD, ADAM_STEP = 0.001, 0.9, 0.999, 1e-08, 0.01, 10
N_DEV = 8
N_CHIPS = 4

QKV_WIDTH = ATT_WIDTH + 2 * KV_WIDTH
REST_WIDTH = IN_WIDTH - QKV_WIDTH
COL_SUV, COL_GA, COL_GB, COL_QKV = 0, 1024, 2048, 3072
W_IN_ROTATE = (1, IN_WIDTH // QKV_WIDTH)

LANES = 128
SUBLANES = 8
VMEM_LIMIT_V7X = 56 * 1024 * 1024
GELU_C = math.sqrt(2.0 / math.pi)
GELU_K = 0.044715
ANY = pl.BlockSpec(memory_space=pl.ANY)


def _params(sem=None):
    return pltpu.CompilerParams(dimension_semantics=sem, vmem_limit_bytes=VMEM_LIMIT_V7X)


def _sigmoid(x):
    return 1.0 / (1.0 + jnp.exp(-x))


def _gelu(x):
    th = jnp.tanh(GELU_C * (x + GELU_K * x * x * x))
    return 0.5 * x * (1.0 + th)


def _gelu_and_grad(x):
    x2 = x * x
    th = jnp.tanh(GELU_C * (x + GELU_K * x2 * x))
    g = 0.5 * x * (1.0 + th)
    dg = 0.5 * (1.0 + th) + 0.5 * x * (1.0 - th * th) * (GELU_C * (1.0 + 3.0 * GELU_K * x2))
    return g, dg


def _dot(a, b, dims):
    return lax.dot_general(a, b, (dims, ((), ())), preferred_element_type=F32)


def _dot_nn(a, b):
    return _dot(a, b, ((1,), (0,)))


def _dot_nt(a, b):
    return _dot(a, b, ((1,), (1,)))


def _dot_tn(a, b):
    return _dot(a, b, ((0,), (0,)))


def _lo_mask(shape):
    return lax.broadcasted_iota(jnp.int32, shape, len(shape) - 1) < (LANES // 2)


def _half_sums(x, lo):
    s_lo = jnp.sum(jnp.where(lo, x, 0.0), axis=-1, keepdims=True)
    s_all = jnp.sum(x, axis=-1, keepdims=True)
    return jnp.where(lo, s_lo, s_all - s_lo)


def _dup_half(x, half, lo):
    r = pltpu.roll(x, LANES // 2, axis=1)
    return jnp.where(lo, x, r) if half == 0 else jnp.where(lo, r, x)


def _with_deps(body, n_in, deps):
    k = len(deps)
    if not k:
        return body, [], ()

    def skipping(*refs):
        return body(*refs[:n_in], *refs[n_in + k:])

    return skipping, [ANY] * k, tuple(deps)


MM_VMEM_BUDGET = 40 * 1024 * 1024
MM_MAX_TILE = 1408
MM_MAX_TK = 4096
MM_STEP_BYTES = 1 << 20


def _divisors(n, step, cap):
    return [d for d in range(step, min(n, cap) + 1, step) if n % d == 0] or [n]


def _mm_tiles(M, N, K, out_bytes, tm_divides, tn_divides):
    best = None
    for tm in _divisors(M, LANES, MM_MAX_TILE):
        for tn in _divisors(N, LANES, MM_MAX_TILE):
            if tm_divides % tm or tn_divides % tn:
                continue
            for tk in _divisors(K, 4 * LANES, MM_MAX_TK):
                vmem = 4 * (tm * tk + tk * tn) + 2 * tm * tn * out_bytes + (0 if tk == K else 4 * tm * tn)
                if vmem > MM_VMEM_BUDGET:
                    continue
                traffic = 2 * M * K * (N // tn) + 2 * K * N * (M // tm) + M * N * out_bytes
                cost = traffic + (K // tk - 1) * 8 * M * N + (M // tm) * (N // tn) * (K // tk) * MM_STEP_BYTES
                if best is None or cost < best[0]:
                    best = (cost, tm, tn, tk)
    assert best is not None, (M, N, K)
    return best[1:]


def _mm(a, b, *, mode, out_dtype, name, deps=(), n=None, b_rows=(0, None), rotate=None, out_rows=(0, None), out_prev=None):
    b_first, b_count = b_rows
    if mode == "nn":
        (M, K), N = a.shape, b.shape[1]
    elif mode == "nt":
        (M, K), N = a.shape, (b.shape[0] if b_count is None else b_count)
    else:
        (K, M), N = a.shape, b.shape[1]
    shift, period = rotate or (0, 1)
    out_first, out_total = out_rows[0], (M if out_rows[1] is None else out_rows[1])
    tm, tn, tk = _mm_tiles(M, N, K, jnp.dtype(out_dtype).itemsize,
                           math.gcd(M // period if mode == "tn" else M, out_first),
                           math.gcd(N // period if mode == "nt" else N, b_first if mode == "nt" else 0))
    gm, gn, gk = M // tm, N // tn, K // tk

    def turned(t, tile, size):
        per = size // period // tile
        return ((t // per + shift) % period) * per + t % per if period > 1 else t

    if mode == "nn":
        a_spec = pl.BlockSpec((tm, tk), lambda i, j, k: (i, k))
        b_spec = pl.BlockSpec((tk, tn), lambda i, j, k: (k + b_first // tk, j))
        contract = ((1,), (0,))
    elif mode == "nt":
        a_spec = pl.BlockSpec((tm, tk), lambda i, j, k: (i, k))
        b_spec = pl.BlockSpec((tn, tk), lambda i, j, k: (turned(j, tn, N) + b_first // tn, k))
        contract = ((1,), (1,))
    else:
        a_spec = pl.BlockSpec((tk, tm), lambda i, j, k: (k, i))
        b_spec = pl.BlockSpec((tk, tn), lambda i, j, k: (k, j))
        contract = ((0,), (0,))
    if mode == "tn":
        o_spec = pl.BlockSpec((tm, tn), lambda i, j, k: (turned(i, tm, M) + out_first // tm, j))
    else:
        o_spec = pl.BlockSpec((tm, tn), lambda i, j, k: (i + out_first // tm, j))
    assert b_first % (tk if mode == "nn" else tn) == 0 and out_first % tm == 0, (name, tm, tn, tk)
    n_prev = 0 if out_prev is None else 1

    def body(a_ref, b_ref, *rest):
        o_ref = rest[n_prev]
        part = _dot(a_ref[...].astype(BF16), b_ref[...].astype(BF16), contract)
        if gk == 1:
            o_ref[...] = part.astype(out_dtype)
            return
        acc_ref = rest[n_prev + 1]
        k = pl.program_id(2)

        @pl.when(k == 0)
        def _():
            acc_ref[...] = part

        @pl.when(k > 0)
        def _():
            acc_ref[...] += part

        @pl.when(k == gk - 1)
        def _():
            o_ref[...] = acc_ref[...].astype(out_dtype)

    body, dep_specs, dep_args = _with_deps(body, 2 + n_prev, deps)
    return pl.pallas_call(
        body,
        name=name,
        grid=(gm, gn, gk),
        in_specs=[a_spec, b_spec] + [ANY] * n_prev + dep_specs,
        out_specs=o_spec,
        out_shape=jax.ShapeDtypeStruct((out_total, N), out_dtype),
        input_output_aliases={2: 0} if n_prev else {},
        scratch_shapes=[] if gk == 1 else [pltpu.VMEM((tm, tn), F32)],
        compiler_params=_params(("parallel", "parallel", "arbitrary")),
    )(a, b, *([out_prev] if n_prev else []), *dep_args)


def _mm_rows(a, b, *, mode, fn, out_dtypes, rows=(), vecs=(), reduce=False, name, deps=(), b_rows=(0, None), a_at=None):
    parts = a if a_at is not None else (a,)
    starts = a_at if a_at is not None else (0,)
    n_parts = len(parts)
    M, K = parts[0].shape[0], sum(p.shape[1] for p in parts)
    b_first, b_count = b_rows[0], (b.shape[0] if b_rows[1] is None else b_rows[1])
    N = b.shape[1] if mode == "nn" else b_count
    contract = ((1,), (0,)) if mode == "nn" else ((1,), (1,))
    n_rows, n_vecs, n_out = len(rows), len(vecs), len(out_dtypes)
    out_bytes = sum(jnp.dtype(d).itemsize for d in out_dtypes)
    tm = max(t for t in _divisors(M, LANES, MM_MAX_TILE)
             if 4 * t * K + 4 * K * N + 2 * t * N * (4 * n_rows + out_bytes) <= MM_VMEM_BUDGET)
    assert b_first % b_count == 0 and (a_at is None or mode == "nn")

    def body(*refs):
        a_refs, b_ref, rest = refs[:n_parts], refs[n_parts], refs[n_parts + 1:]
        row_refs, vec_refs = rest[:n_rows], rest[n_rows:n_rows + n_vecs]
        out_refs = rest[n_rows + n_vecs:]
        if a_at is None:
            acc = _dot(a_refs[0][...], b_ref[...], contract)
        else:
            acc = sum(_dot(r[...], b_ref[at:at + r.shape[1], :], contract) for r, at in zip(a_refs, starts))
        res = fn(acc, *[r[...] for r in row_refs], *[v[...] for v in vec_refs])
        for o_ref, val in zip(out_refs[:n_out], res):
            o_ref[...] = val.astype(o_ref.dtype)
        if reduce:
            @pl.when(pl.program_id(0) == 0)
            def _():
                out_refs[n_out][...] = res[n_out]

            @pl.when(pl.program_id(0) > 0)
            def _():
                out_refs[n_out][...] += res[n_out]

    row = pl.BlockSpec((tm, N), lambda i: (i, 0))
    vec = pl.BlockSpec((1, N), lambda i: (0, 0))
    body, dep_specs, dep_args = _with_deps(body, n_parts + 1 + n_rows + n_vecs, deps)
    return pl.pallas_call(
        body, name=name, grid=(M // tm,),
        in_specs=[pl.BlockSpec((tm, p.shape[1]), lambda i: (i, 0)) for p in parts]
        + [pl.BlockSpec((b_count, b.shape[1]), lambda i: (b_first // b_count, 0))]
        + [row] * n_rows + [vec] * n_vecs + dep_specs,
        out_specs=[row] * n_out + [vec] * reduce,
        out_shape=[jax.ShapeDtypeStruct((M, N), d) for d in out_dtypes] + [jax.ShapeDtypeStruct((1, N), F32)] * reduce,
        compiler_params=_params(("arbitrary",)),
    )(*parts, b, *rows, *[v.reshape(1, N) for v in vecs], *dep_args)


def _rms(x, gain):
    return x * lax.rsqrt(jnp.mean(x * x, axis=-1, keepdims=True) + NORM_EPS) * gain


def _residual_then_norm(acc, x, gain):
    x_out = x + acc
    return x_out, _rms(x_out, gain)


def _residual_then_loss(acc, x, target):
    err = (x + acc) - target
    dy = err * (1.0 / D_MODEL)
    return dy, dy, jnp.sum(err * err, axis=0, keepdims=True) * (0.5 / D_MODEL)


def _rms_bwd_rows(dh, x, dres, gain):
    r = lax.rsqrt(jnp.mean(x * x, axis=-1, keepdims=True) + NORM_EPS)
    xh = x * r
    dxh = dh * gain
    dx = dres + r * (dxh - xh * jnp.mean(dxh * xh, axis=-1, keepdims=True))
    return dx, dx, jnp.sum(dh * xh, axis=0, keepdims=True)


def _sum_then_rms_bwd(acc, dh_part, x, dres, gain):
    return _rms_bwd_rows(acc + dh_part, x, dres, gain)


def _rms_fwd(x, gain, *, name, tm=512):
    T, D = x.shape

    def body(x_ref, g_ref, h_ref):
        xv = x_ref[...]
        r = lax.rsqrt(jnp.mean(xv * xv, axis=-1, keepdims=True) + NORM_EPS)
        h_ref[...] = (xv * r * g_ref[...]).astype(BF16)

    return pl.pallas_call(
        body, name=name, grid=(T // tm,),
        in_specs=[pl.BlockSpec((tm, D), lambda i: (i, 0)), pl.BlockSpec((1, D), lambda i: (0, 0))],
        out_specs=pl.BlockSpec((tm, D), lambda i: (i, 0)),
        out_shape=jax.ShapeDtypeStruct((T, D), BF16),
        compiler_params=_params(("parallel",)),
    )(x, gain.reshape(1, D))


def _head_norm(x, gain2, lo):
    ms = _half_sums(x * x, lo) * (1.0 / HEAD_DIM)
    r = lax.rsqrt(ms + NORM_EPS)
    xh = x * r
    return xh * gain2, xh, r


def _head_norm_bwd(xh, r, gain2, dy, lo):
    dxh = dy * gain2
    dx = r * (dxh - xh * (_half_sums(dxh * xh, lo) * (1.0 / HEAD_DIM)))
    return dx, dy * xh


Q_GROUP = N_Q_HEADS // 2
GROUP_ROWS = Q_GROUP * BLOCK
ATT_SCRATCH = (pltpu.VMEM((2, 2, GROUP_ROWS, BLOCK), F32), pltpu.VMEM((2, GROUP_ROWS, 1), F32))


def _att_consts(sink_ref, bias_ref, sinkcol_ref):
    row = lax.broadcasted_iota(jnp.int32, (GROUP_ROWS, BLOCK), 0)
    kj = lax.broadcasted_iota(jnp.int32, (GROUP_ROWS, BLOCK), 1)
    head = row // BLOCK
    head_col = lax.broadcasted_iota(jnp.int32, (GROUP_ROWS, 1), 0) // BLOCK
    d_cur = (row % BLOCK) - kj
    d_prev = d_cur + BLOCK
    for kv in range(2):
        slope = jnp.zeros((GROUP_ROWS, BLOCK), F32)
        sink = jnp.zeros((GROUP_ROWS, 1), F32)
        for r in range(Q_GROUP):
            slope = jnp.where(head == r, ALIBI_SLOPES[Q_GROUP * kv + r], slope)
            sink = jnp.where(head_col == r, sink_ref[Q_GROUP * kv + r], sink)
        bias_ref[kv, 0] = jnp.where(d_cur >= 0, -slope * d_cur.astype(F32), NEG_INF)
        bias_ref[kv, 1] = jnp.where(d_prev < BLOCK, -slope * d_prev.astype(F32), NEG_INF)
        sinkcol_ref[kv] = sink


def _stack_heads(t0, t1, lo):
    z = jnp.zeros_like(t0)
    return jnp.concatenate([jnp.where(lo, t0, z), jnp.where(lo, z, t0), jnp.where(lo, t1, z), jnp.where(lo, z, t1)], axis=0)


def _unstack_heads(x4, lo):
    return (jnp.where(lo, x4[0:BLOCK], x4[BLOCK:2 * BLOCK]), jnp.where(lo, x4[2 * BLOCK:3 * BLOCK], x4[3 * BLOCK:]))


def _att_probs(q4, k2c, k2p, bias_c, bias_p, sink, has_prev):
    s_c = _dot_nt(q4, k2c) * ATT_SCALE + bias_c
    s_p = jnp.where(has_prev, _dot_nt(q4, k2p) * ATT_SCALE + bias_p, NEG_INF)
    m = jnp.maximum(jnp.max(jnp.maximum(s_c, s_p), axis=-1, keepdims=True), sink)
    e_c = jnp.exp(s_c - m)
    e_p = jnp.exp(s_p - m)
    e_s = jnp.exp(sink - m)
    inv = 1.0 / (jnp.sum(e_c + e_p, axis=-1, keepdims=True) + e_s)
    return e_c * inv, e_p * inv, e_s * inv


def _attention_fwd(proj, q_gain, k_gain, sinks, *, n_seq, seq, name):
    T = n_seq * seq
    nb = seq // BLOCK
    qcol, kvcol = COL_QKV // ATT_WIDTH, (COL_QKV + ATT_WIDTH) // (2 * KV_WIDTH)

    def body(q_ref, kv_ref, qg_ref, kg_ref, sink_ref, y_ref, bias_ref, sinkcol_ref):
        lo = _lo_mask((BLOCK, LANES))
        qg, kg = qg_ref[...], kg_ref[...]
        _att_consts(sink_ref, bias_ref, sinkcol_ref)

        def block(i, carry):
            r0 = pl.multiple_of(i * BLOCK, BLOCK)
            rp = pl.multiple_of(jnp.maximum(i - 1, 0) * BLOCK, BLOCK)
            has_prev = i > 0
            kn_c = _head_norm(kv_ref[pl.ds(r0, BLOCK), 0:KV_WIDTH].astype(F32), kg, lo)[0].astype(BF16)
            kn_p = _head_norm(kv_ref[pl.ds(rp, BLOCK), 0:KV_WIDTH].astype(F32), kg, lo)[0].astype(BF16)
            v_c = kv_ref[pl.ds(r0, BLOCK), KV_WIDTH:2 * KV_WIDTH].astype(BF16)
            v_p = kv_ref[pl.ds(rp, BLOCK), KV_WIDTH:2 * KV_WIDTH].astype(BF16)
            for kv in range(2):
                k2c, k2p = _dup_half(kn_c, kv, lo), _dup_half(kn_p, kv, lo)
                v2c, v2p = _dup_half(v_c, kv, lo), _dup_half(v_p, kv, lo)
                cols = [slice((2 * kv + t) * LANES, (2 * kv + t + 1) * LANES) for t in range(2)]
                qn = [_head_norm(q_ref[pl.ds(r0, BLOCK), c].astype(F32), qg, lo)[0] for c in cols]
                q4 = _stack_heads(qn[0], qn[1], lo).astype(BF16)
                p_c, p_p, _ = _att_probs(q4, k2c, k2p, bias_ref[kv, 0], bias_ref[kv, 1], sinkcol_ref[kv], has_prev)
                o4 = _dot_nn(p_c.astype(BF16), v2c) + _dot_nn(p_p.astype(BF16), v2p)
                for c, out in zip(cols, _unstack_heads(o4, lo)):
                    y_ref[pl.ds(r0, BLOCK), c] = out.astype(BF16)
            return carry

        lax.fori_loop(0, nb, block, 0)

    vec = pl.BlockSpec((1, LANES), lambda b: (0, 0))
    return pl.pallas_call(
        body, name=name, grid=(n_seq,),
        in_specs=[pl.BlockSpec((seq, ATT_WIDTH), lambda b: (b, qcol)),
                  pl.BlockSpec((seq, 2 * KV_WIDTH), lambda b: (b, kvcol)),
                  vec, vec, pl.BlockSpec(memory_space=pltpu.SMEM)],
        out_specs=pl.BlockSpec((seq, ATT_WIDTH), lambda b: (b, 0)),
        out_shape=jax.ShapeDtypeStruct((T, ATT_WIDTH), BF16),
        scratch_shapes=list(ATT_SCRATCH),
        compiler_params=_params(("parallel",)),
    )(proj, proj, jnp.tile(q_gain, 2).reshape(1, LANES), jnp.tile(k_gain, 2).reshape(1, LANES), sinks)


def _attention_bwd(proj, dy, q_gain, k_gain, sinks, *, n_seq, seq, name, deps=()):
    T = n_seq * seq
    nb = seq // BLOCK
    qcol, kvcol = COL_QKV // ATT_WIDTH, (COL_QKV + ATT_WIDTH) // (2 * KV_WIDTH)

    def body(q_ref, kv_ref, dy_ref, qg_ref, kg_ref, sink_ref, dqkv_ref, dqg_ref, dkg_ref, dsink_ref,
             dkn_acc, dv_acc, qg_acc, kg_acc, sink_acc, bias_ref, sinkcol_ref):
        lo = _lo_mask((BLOCK, LANES))
        qg, kg = qg_ref[...], kg_ref[...]
        _att_consts(sink_ref, bias_ref, sinkcol_ref)
        first = pl.program_id(0) == 0

        @pl.when(first)
        def _():
            qg_acc[...] = jnp.zeros_like(qg_acc)
            kg_acc[...] = jnp.zeros_like(kg_acc)
            sink_acc[...] = jnp.zeros_like(sink_acc)

        dkn_acc[...] = jnp.zeros_like(dkn_acc)
        dv_acc[...] = jnp.zeros_like(dv_acc)

        def block(i, carry):
            r0 = pl.multiple_of(i * BLOCK, BLOCK)
            rp = pl.multiple_of(jnp.maximum(i - 1, 0) * BLOCK, BLOCK)
            has_prev = i > 0
            kn_c = _head_norm(kv_ref[pl.ds(r0, BLOCK), 0:KV_WIDTH].astype(F32), kg, lo)[0].astype(BF16)
            kn_p = _head_norm(kv_ref[pl.ds(rp, BLOCK), 0:KV_WIDTH].astype(F32), kg, lo)[0].astype(BF16)
            v_c = kv_ref[pl.ds(r0, BLOCK), KV_WIDTH:2 * KV_WIDTH].astype(BF16)
            v_p = kv_ref[pl.ds(rp, BLOCK), KV_WIDTH:2 * KV_WIDTH].astype(BF16)
            dk_c, dk_p, dv_c, dv_p = [], [], [], []
            for kv in range(2):
                k2c, k2p = _dup_half(kn_c, kv, lo), _dup_half(kn_p, kv, lo)
                v2c, v2p = _dup_half(v_c, kv, lo), _dup_half(v_p, kv, lo)
                cols = [slice((2 * kv + t) * LANES, (2 * kv + t + 1) * LANES) for t in range(2)]
                normed = [_head_norm(q_ref[pl.ds(r0, BLOCK), c].astype(F32), qg, lo) for c in cols]
                q4 = _stack_heads(normed[0][0], normed[1][0], lo).astype(BF16)
                do4 = _stack_heads(dy_ref[pl.ds(r0, BLOCK), cols[0]], dy_ref[pl.ds(r0, BLOCK), cols[1]], lo)
                p_c, p_p, p_s = _att_probs(q4, k2c, k2p, bias_ref[kv, 0], bias_ref[kv, 1], sinkcol_ref[kv], has_prev)
                dp_c = _dot_nt(do4, v2c)
                dp_p = _dot_nt(do4, v2p)
                delta = jnp.sum(p_c * dp_c + p_p * dp_p, axis=-1, keepdims=True)
                ds_c = (p_c * (dp_c - delta)).astype(BF16)
                ds_p = (p_p * (dp_p - delta)).astype(BF16)
                sink_acc[kv] += -(p_s * delta)
                dq4 = (_dot_nn(ds_c, k2c) + _dot_nn(ds_p, k2p)) * ATT_SCALE
                for c, (_, qh, qr), dqn in zip(cols, normed, _unstack_heads(dq4, lo)):
                    dq, dg = _head_norm_bwd(qh, qr, qg, dqn, lo)
                    dqkv_ref[pl.ds(r0, BLOCK), c] = dq.astype(BF16)
                    qg_acc[...] += dg
                dk_c.append(_dot_tn(ds_c, q4))
                dk_p.append(_dot_tn(ds_p, q4))
                dv_c.append(_dot_tn(p_c.astype(BF16), do4))
                dv_p.append(_dot_tn(p_p.astype(BF16), do4))

            def fold(parts):
                a = parts[0] + pltpu.roll(parts[0], LANES // 2, axis=1)
                b = parts[1] + pltpu.roll(parts[1], LANES // 2, axis=1)
                return jnp.where(lo, a, b)

            dkn_acc[pl.ds(r0, BLOCK), :] += fold(dk_c) * ATT_SCALE
            dkn_acc[pl.ds(rp, BLOCK), :] += fold(dk_p) * ATT_SCALE
            dv_acc[pl.ds(r0, BLOCK), :] += fold(dv_c)
            dv_acc[pl.ds(rp, BLOCK), :] += fold(dv_p)
            return carry

        lax.fori_loop(0, nb, block, 0)

        def finish(i, carry):
            r0 = pl.multiple_of(i * BLOCK, BLOCK)
            _, kh, kr = _head_norm(kv_ref[pl.ds(r0, BLOCK), 0:KV_WIDTH].astype(F32), kg, lo)
            dk, dg = _head_norm_bwd(kh, kr, kg, dkn_acc[pl.ds(r0, BLOCK), :], lo)
            dqkv_ref[pl.ds(r0, BLOCK), ATT_WIDTH:ATT_WIDTH + KV_WIDTH] = dk.astype(BF16)
            dqkv_ref[pl.ds(r0, BLOCK), ATT_WIDTH + KV_WIDTH:QKV_WIDTH] = dv_acc[pl.ds(r0, BLOCK), :].astype(BF16)
            kg_acc[...] += dg
            return carry

        lax.fori_loop(0, nb, finish, 0)

        @pl.when(pl.program_id(0) == n_seq - 1)
        def _():
            dqg_ref[...] = jnp.sum(qg_acc[...], axis=0, keepdims=True)
            dkg_ref[...] = jnp.sum(kg_acc[...], axis=0, keepdims=True)
            lane = lax.broadcasted_iota(jnp.int32, (1, LANES), 1)
            dsink = jnp.zeros((1, LANES), F32)
            for kv in range(2):
                for r in range(Q_GROUP):
                    total = jnp.sum(sink_acc[kv, r * BLOCK:(r + 1) * BLOCK, :], axis=0, keepdims=True)
                    dsink = jnp.where(lane == Q_GROUP * kv + r, total, dsink)
            dsink_ref[...] = dsink

    vec = pl.BlockSpec((1, LANES), lambda b: (0, 0))
    acc = pltpu.VMEM((BLOCK, LANES), F32)
    body, dep_specs, dep_args = _with_deps(body, 6, deps)
    dqkv, dqg, dkg, dsink = pl.pallas_call(
        body, name=name, grid=(n_seq,),
        in_specs=[pl.BlockSpec((seq, ATT_WIDTH), lambda b: (b, qcol)),
                  pl.BlockSpec((seq, 2 * KV_WIDTH), lambda b: (b, kvcol)),
                  pl.BlockSpec((seq, ATT_WIDTH), lambda b: (b, 0)),
                  vec, vec, pl.BlockSpec(memory_space=pltpu.SMEM)] + dep_specs,
        out_specs=[pl.BlockSpec((seq, QKV_WIDTH), lambda b: (b, 0)), vec, vec, vec],
        out_shape=[jax.ShapeDtypeStruct((T, QKV_WIDTH), BF16)] + [jax.ShapeDtypeStruct((1, LANES), F32)] * 3,
        scratch_shapes=[pltpu.VMEM((seq, KV_WIDTH), F32), pltpu.VMEM((seq, KV_WIDTH), F32), acc, acc,
                        pltpu.VMEM((2, GROUP_ROWS, 1), F32), *ATT_SCRATCH],
        compiler_params=_params(("arbitrary",)),
    )(proj, proj, dy, jnp.tile(q_gain, 2).reshape(1, LANES), jnp.tile(k_gain, 2).reshape(1, LANES), sinks, *dep_args)
    half = LANES // 2
    return dqkv, dqg[0, :half] + dqg[0, half:], dkg[0, :half] + dkg[0, half:], dsink[0, :N_Q_HEADS]


def _sgu_weights(w_ref):
    r = lax.broadcasted_iota(jnp.int32, (BLOCK, BLOCK), 0)
    c = lax.broadcasted_iota(jnp.int32, (BLOCK, BLOCK), 1)
    return [jnp.where(r >= c, w_ref[g], 0.0).astype(BF16) for g in range(SGU_GROUPS)]


def _sgu_fwd(proj, gain, w_s, bias_full, *, n_seq, seq, name):
    T = n_seq * seq
    nc = seq // BLOCK

    def body(suv_ref, g_ref, w_ref, b_ref, y_ref):
        lo = _lo_mask((BLOCK, LANES))
        wm = _sgu_weights(w_ref)
        gain_v = g_ref[...]

        def chunk(c, carry):
            r0 = pl.multiple_of(c * BLOCK, BLOCK)
            gv = _gelu(suv_ref[pl.ds(r0, BLOCK), SGU_WIDTH:2 * SGU_WIDTH].astype(F32))
            r = lax.rsqrt(jnp.mean(gv * gv, axis=-1, keepdims=True) + NORM_EPS)
            vn = (gv * r * gain_v).astype(BF16)
            for p in range(SGU_WIDTH // LANES):
                cols = slice(p * LANES, (p + 1) * LANES)
                vp = vn[:, cols]
                mixed = jnp.where(lo, _dot_nn(wm[2 * p], vp), _dot_nn(wm[2 * p + 1], vp)) + b_ref[:, cols]
                u = _gelu(suv_ref[pl.ds(r0, BLOCK), cols].astype(F32))
                y_ref[pl.ds(r0, BLOCK), cols] = (u * mixed).astype(BF16)
            return carry

        lax.fori_loop(0, nc, chunk, 0)

    return pl.pallas_call(
        body, name=name, grid=(n_seq,),
        in_specs=[pl.BlockSpec((seq, 2 * SGU_WIDTH), lambda b: (b, COL_SUV // (2 * SGU_WIDTH))),
                  pl.BlockSpec((1, SGU_WIDTH), lambda b: (0, 0)),
                  pl.BlockSpec((SGU_GROUPS, BLOCK, BLOCK), lambda b: (0, 0, 0)),
                  pl.BlockSpec((BLOCK, SGU_WIDTH), lambda b: (0, 0))],
        out_specs=pl.BlockSpec((seq, SGU_WIDTH), lambda b: (b, 0)),
        out_shape=jax.ShapeDtypeStruct((T, SGU_WIDTH), BF16),
        compiler_params=_params(("parallel",)),
    )(proj, gain.reshape(1, SGU_WIDTH), w_s, bias_full)


def _sgu_bwd(proj, dy, gain, w_s, bias_full, *, n_seq, seq, name, deps=()):
    T = n_seq * seq
    nc = seq // BLOCK
    n_tiles = SGU_WIDTH // LANES

    def body(suv_ref, dy_ref, g_ref, w_ref, b_ref, dsuv_ref, dg_ref, dw_ref, db_ref, dg_acc, dw_acc, db_acc):
        lo = _lo_mask((BLOCK, LANES))
        hi = jnp.logical_not(lo)
        wm = _sgu_weights(w_ref)
        wmt = [jnp.where(lax.broadcasted_iota(jnp.int32, (BLOCK, BLOCK), 1) >= lax.broadcasted_iota(jnp.int32, (BLOCK, BLOCK), 0),
                         w_ref[g].T, 0.0).astype(BF16) for g in range(SGU_GROUPS)]
        gain_v = g_ref[...]

        @pl.when(pl.program_id(0) == 0)
        def _():
            dg_acc[...] = jnp.zeros_like(dg_acc)
            dw_acc[...] = jnp.zeros_like(dw_acc)
            db_acc[...] = jnp.zeros_like(db_acc)

        def chunk(c, carry):
            r0 = pl.multiple_of(c * BLOCK, BLOCK)
            gv, dgelu_v = _gelu_and_grad(suv_ref[pl.ds(r0, BLOCK), SGU_WIDTH:2 * SGU_WIDTH].astype(F32))
            r = lax.rsqrt(jnp.mean(gv * gv, axis=-1, keepdims=True) + NORM_EPS)
            vh = gv * r
            vn = (vh * gain_v).astype(BF16)
            dvn_tiles = []
            for p in range(n_tiles):
                cols = slice(p * LANES, (p + 1) * LANES)
                vp = vn[:, cols]
                mixed = jnp.where(lo, _dot_nn(wm[2 * p], vp), _dot_nn(wm[2 * p + 1], vp)) + b_ref[:, cols]
                u, dgelu_u = _gelu_and_grad(suv_ref[pl.ds(r0, BLOCK), cols].astype(F32))
                dyv = dy_ref[pl.ds(r0, BLOCK), cols]
                dsuv_ref[pl.ds(r0, BLOCK), cols] = (dyv * mixed * dgelu_u).astype(BF16)
                dm = dyv * u
                db_acc[:, cols] += dm
                dm_bf = dm.astype(BF16)
                dvn_tiles.append(jnp.where(lo, _dot_nn(wmt[2 * p], dm_bf), _dot_nn(wmt[2 * p + 1], dm_bf)))
                dw_acc[2 * p] += _dot_nt(jnp.where(lo, dm, 0.0).astype(BF16), vp)
                dw_acc[2 * p + 1] += _dot_nt(jnp.where(hi, dm, 0.0).astype(BF16), vp)
            dvn = jnp.concatenate(dvn_tiles, axis=1)
            dg_acc[...] += dvn * vh
            dvh = dvn * gain_v
            dgv = r * (dvh - vh * jnp.mean(dvh * vh, axis=-1, keepdims=True))
            dsuv_ref[pl.ds(r0, BLOCK), SGU_WIDTH:2 * SGU_WIDTH] = (dgv * dgelu_v).astype(BF16)
            return carry

        lax.fori_loop(0, nc, chunk, 0)

        @pl.when(pl.program_id(0) == n_seq - 1)
        def _():
            dg_ref[...] = jnp.sum(dg_acc[...], axis=0, keepdims=True)
            r = lax.broadcasted_iota(jnp.int32, (BLOCK, BLOCK), 0)
            c = lax.broadcasted_iota(jnp.int32, (BLOCK, BLOCK), 1)
            for g in range(SGU_GROUPS):
                dw_ref[g] = jnp.where(r >= c, dw_acc[g], 0.0)
            lane = lax.broadcasted_iota(jnp.int32, (BLOCK, LANES), 1)
            out = jnp.zeros((BLOCK, LANES), F32)
            for p in range(n_tiles):
                tile = db_acc[:, p * LANES:(p + 1) * LANES]
                s_lo = jnp.sum(jnp.where(lo, tile, 0.0), axis=-1, keepdims=True)
                s_hi = jnp.sum(jnp.where(hi, tile, 0.0), axis=-1, keepdims=True)
                out = jnp.where(lane == 2 * p, s_lo, out)
                out = jnp.where(lane == 2 * p + 1, s_hi, out)
            db_ref[...] = out

    body, dep_specs, dep_args = _with_deps(body, 5, deps)
    dsuv, dg, dw, db = pl.pallas_call(
        body, name=name, grid=(n_seq,),
        in_specs=[pl.BlockSpec((seq, 2 * SGU_WIDTH), lambda b: (b, COL_SUV // (2 * SGU_WIDTH))),
                  pl.BlockSpec((seq, SGU_WIDTH), lambda b: (b, 0)),
                  pl.BlockSpec((1, SGU_WIDTH), lambda b: (0, 0)),
                  pl.BlockSpec((SGU_GROUPS, BLOCK, BLOCK), lambda b: (0, 0, 0)),
                  pl.BlockSpec((BLOCK, SGU_WIDTH), lambda b: (0, 0))] + dep_specs,
        out_specs=[pl.BlockSpec((seq, 2 * SGU_WIDTH), lambda b: (b, 0)),
                   pl.BlockSpec((1, SGU_WIDTH), lambda b: (0, 0)),
                   pl.BlockSpec((SGU_GROUPS, BLOCK, BLOCK), lambda b: (0, 0, 0)),
                   pl.BlockSpec((BLOCK, LANES), lambda b: (0, 0))],
        out_shape=[jax.ShapeDtypeStruct((T, 2 * SGU_WIDTH), BF16), jax.ShapeDtypeStruct((1, SGU_WIDTH), F32),
                   jax.ShapeDtypeStruct((SGU_GROUPS, BLOCK, BLOCK), F32), jax.ShapeDtypeStruct((BLOCK, LANES), F32)],
        scratch_shapes=[pltpu.VMEM((BLOCK, SGU_WIDTH), F32), pltpu.VMEM((SGU_GROUPS, BLOCK, BLOCK), F32),
                        pltpu.VMEM((BLOCK, SGU_WIDTH), F32)],
        compiler_params=_params(("arbitrary",)),
    )(proj, dy, gain.reshape(1, SGU_WIDTH), w_s, bias_full, *dep_args)
    return dsuv, dg.reshape(SGU_WIDTH), dw, db[:, :SGU_GROUPS].T


def _merge_fwd(y_att, y_sgu, w_oa, w_ob, proj, *, name, tm=1024, tn=512, deps=()):
    T = y_att.shape[0]

    def body(ya_ref, ys_ref, wa_ref, wb_ref, ga_ref, gb_ref, o_ref):
        pa = _dot_nn(ya_ref[...], wa_ref[...])
        pb = _dot_nn(ys_ref[...], wb_ref[...])
        o_ref[...] = (_sigmoid(ga_ref[...].astype(F32)) * pa + _sigmoid(gb_ref[...].astype(F32)) * pb).astype(BF16)

    act = pl.BlockSpec((tm, ATT_WIDTH), lambda i, j: (i, 0))
    wgt = pl.BlockSpec((ATT_WIDTH, tn), lambda i, j: (0, j))
    body, dep_specs, dep_args = _with_deps(body, 6, deps)
    return pl.pallas_call(
        body, name=name, grid=(T // tm, D_MODEL // tn),
        in_specs=[act, act, wgt, wgt,
                  pl.BlockSpec((tm, tn), lambda i, j: (i, j + COL_GA // tn)),
                  pl.BlockSpec((tm, tn), lambda i, j: (i, j + COL_GB // tn))] + dep_specs,
        out_specs=pl.BlockSpec((tm, tn), lambda i, j: (i, j)),
        out_shape=jax.ShapeDtypeStruct((T, D_MODEL), BF16),
        compiler_params=_params(("parallel", "parallel")),
    )(y_att, y_sgu, w_oa, w_ob, proj, proj, *dep_args)


def _merge_bwd(dx1_bf, w_out, y_att, y_sgu, w_oa, w_ob, proj, *, name, tm=1024, tn=512):
    T = y_att.shape[0]

    def body(dx_ref, wo_ref, ya_ref, ys_ref, wa_ref, wb_ref, ga_ref, gb_ref, dpa_ref, dpb_ref, dga_ref, dgb_ref):
        dm = _dot_nt(dx_ref[...], wo_ref[...])
        pa = _dot_nn(ya_ref[...], wa_ref[...])
        pb = _dot_nn(ys_ref[...], wb_ref[...])
        sa = _sigmoid(ga_ref[...].astype(F32))
        sb = _sigmoid(gb_ref[...].astype(F32))
        dpa_ref[...] = (dm * sa).astype(BF16)
        dpb_ref[...] = (dm * sb).astype(BF16)
        dga_ref[...] = (dm * pa * sa * (1.0 - sa)).astype(BF16)
        dgb_ref[...] = (dm * pb * sb * (1.0 - sb)).astype(BF16)

    act = pl.BlockSpec((tm, ATT_WIDTH), lambda i, j: (i, 0))
    wgt = pl.BlockSpec((ATT_WIDTH, tn), lambda i, j: (0, j))
    out = pl.BlockSpec((tm, tn), lambda i, j: (i, j))
    return pl.pallas_call(
        body, name=name, grid=(T // tm, D_MODEL // tn),
        in_specs=[pl.BlockSpec((tm, D_MODEL), lambda i, j: (i, 0)),
                  pl.BlockSpec((tn, D_MODEL), lambda i, j: (j, 0)),
                  act, act, wgt, wgt,
                  pl.BlockSpec((tm, tn), lambda i, j: (i, j + COL_GA // tn)),
                  pl.BlockSpec((tm, tn), lambda i, j: (i, j + COL_GB // tn))],
        out_specs=[out] * 4,
        out_shape=[jax.ShapeDtypeStruct((T, D_MODEL), BF16)] * 4,
        compiler_params=_params(("parallel", "parallel")),
    )(dx1_bf, w_out, y_att, y_sgu, w_oa, w_ob, proj, proj)


CONV_ROWS = 256
CONV_TN = 256


def _shift_rows(cur, prev8, k):
    rolled = pltpu.roll(cur, k, axis=0)
    head = jnp.where(lax.broadcasted_iota(jnp.int32, prev8.shape, 0) < k, pltpu.roll(prev8, k, axis=0), rolled[:SUBLANES])
    return jnp.concatenate([head, rolled[SUBLANES:]], axis=0)


def _shift_rows_up(cur, next8, k):
    n = cur.shape[0]
    rolled = pltpu.roll(cur, n - k, axis=0)
    tail = jnp.where(lax.broadcasted_iota(jnp.int32, next8.shape, 0) >= SUBLANES - k,
                     pltpu.roll(next8, SUBLANES - k, axis=0), rolled[n - SUBLANES:])
    return jnp.concatenate([rolled[:n - SUBLANES], tail], axis=0)


HALO_ROWS = 16


def _rows_before(z_ref, r0, first):
    rp = pl.multiple_of(jnp.maximum(r0 - HALO_ROWS, 0), HALO_ROWS)
    halo = z_ref[pl.ds(rp, HALO_ROWS), :].astype(F32)
    return jnp.where(first, 0.0, halo[HALO_ROWS - SUBLANES:])


def _conv_rows(z_ref, r0, first, w_ref, b_ref, rows):
    cur = z_ref[pl.ds(r0, rows), :].astype(F32)
    prev8 = _rows_before(z_ref, r0, first)
    z1 = _shift_rows(cur, prev8, 1)
    z2 = _shift_rows(cur, prev8, 2)
    return b_ref[...] + w_ref[0:1, :] * z2 + w_ref[1:2, :] * z1 + w_ref[2:3, :] * cur


def _conv_fwd(z_g, z_v, cw_g, cw_v, cb_g, cb_v, *, n_seq, seq, name):
    T = n_seq * seq
    tn, rows = CONV_TN, CONV_ROWS

    def body(zg_ref, zv_ref, wg_ref, wv_ref, bg_ref, bv_ref, a_ref):
        def step(s, carry):
            r0 = pl.multiple_of(s * rows, rows)
            first = s == 0
            g = _conv_rows(zg_ref, r0, first, wg_ref, bg_ref, rows)
            v = _conv_rows(zv_ref, r0, first, wv_ref, bv_ref, rows)
            a_ref[pl.ds(r0, rows), :] = (g * _sigmoid(g) * v).astype(BF16)
            return carry

        lax.fori_loop(0, seq // rows, step, 0)

    zs = pl.BlockSpec((seq, tn), lambda b, j: (b, j))
    ws = pl.BlockSpec((3, tn), lambda b, j: (0, j))
    bs = pl.BlockSpec((1, tn), lambda b, j: (0, j))
    return pl.pallas_call(
        body, name=name, grid=(n_seq, D_FF // tn),
        in_specs=[zs, zs, ws, ws, bs, bs], out_specs=zs,
        out_shape=jax.ShapeDtypeStruct((T, D_FF), BF16),
        compiler_params=_params(("parallel", "parallel")),
    )(z_g, z_v, cw_g, cw_v, cb_g.reshape(1, D_FF), cb_v.reshape(1, D_FF))


def _conv_bwd(z_g, z_v, da, cw_g, cw_v, cb_g, cb_v, *, n_seq, seq, name):
    T = n_seq * seq
    tn, rows = CONV_TN, CONV_ROWS
    n_steps = seq // rows

    def body(zg_ref, zv_ref, da_ref, wg_ref, wv_ref, bg_ref, bv_ref,
             dzg_ref, dzv_ref, dwg_ref, dwv_ref, dbg_ref, dbv_ref, dcg_ref, dcv_ref):
        def grads(s, accs):
            r0 = pl.multiple_of(s * rows, rows)
            first = s == 0
            cur_g = zg_ref[pl.ds(r0, rows), :].astype(F32)
            cur_v = zv_ref[pl.ds(r0, rows), :].astype(F32)
            pg = _rows_before(zg_ref, r0, first)
            pv = _rows_before(zv_ref, r0, first)
            g1, g2 = _shift_rows(cur_g, pg, 1), _shift_rows(cur_g, pg, 2)
            v1, v2 = _shift_rows(cur_v, pv, 1), _shift_rows(cur_v, pv, 2)
            g = bg_ref[...] + wg_ref[0:1, :] * g2 + wg_ref[1:2, :] * g1 + wg_ref[2:3, :] * cur_g
            v = bv_ref[...] + wv_ref[0:1, :] * v2 + wv_ref[1:2, :] * v1 + wv_ref[2:3, :] * cur_v
            sg = _sigmoid(g)
            dav = da_ref[pl.ds(r0, rows), :].astype(F32)
            dcg = dav * v * (sg * (1.0 + g * (1.0 - sg)))
            dcv = dav * (g * sg)
            dcg_ref[pl.ds(r0, rows), :] = dcg
            dcv_ref[pl.ds(r0, rows), :] = dcv

            def colsum(x):
                return jnp.sum(x, axis=0, keepdims=True)

            return (accs[0] + colsum(dcg * g2), accs[1] + colsum(dcg * g1), accs[2] + colsum(dcg * cur_g), accs[3] + colsum(dcg),
                    accs[4] + colsum(dcv * v2), accs[5] + colsum(dcv * v1), accs[6] + colsum(dcv * cur_v), accs[7] + colsum(dcv))

        zero = jnp.zeros((1, tn), F32)
        sums = lax.fori_loop(0, n_steps, grads, (zero,) * 8)
        first_seq = pl.program_id(1) == 0

        @pl.when(first_seq)
        def _():
            dwg_ref[...] = jnp.concatenate(sums[0:3], axis=0)
            dbg_ref[...] = sums[3]
            dwv_ref[...] = jnp.concatenate(sums[4:7], axis=0)
            dbv_ref[...] = sums[7]

        @pl.when(jnp.logical_not(first_seq))
        def _():
            dwg_ref[...] += jnp.concatenate(sums[0:3], axis=0)
            dbg_ref[...] += sums[3]
            dwv_ref[...] += jnp.concatenate(sums[4:7], axis=0)
            dbv_ref[...] += sums[7]

        def back(s, carry):
            r0 = pl.multiple_of(s * rows, rows)
            last = s == n_steps - 1
            rn = pl.multiple_of(jnp.minimum(r0 + rows, seq - SUBLANES), SUBLANES)
            for dc_ref, w_ref, dz_ref in ((dcg_ref, wg_ref, dzg_ref), (dcv_ref, wv_ref, dzv_ref)):
                cur = dc_ref[pl.ds(r0, rows), :]
                nxt = jnp.where(last, 0.0, dc_ref[pl.ds(rn, SUBLANES), :])
                u1, u2 = _shift_rows_up(cur, nxt, 1), _shift_rows_up(cur, nxt, 2)
                dz_ref[pl.ds(r0, rows), :] = (w_ref[2:3, :] * cur + w_ref[1:2, :] * u1 + w_ref[0:1, :] * u2).astype(BF16)
            return carry

        lax.fori_loop(0, n_steps, back, 0)

    zs = pl.BlockSpec((seq, tn), lambda j, b: (b, j))
    ws = pl.BlockSpec((3, tn), lambda j, b: (0, j))
    bs = pl.BlockSpec((1, tn), lambda j, b: (0, j))
    outs = pl.pallas_call(
        body, name=name, grid=(D_FF // tn, n_seq),
        in_specs=[zs, zs, zs, ws, ws, bs, bs],
        out_specs=[zs, zs, ws, ws, bs, bs],
        out_shape=[jax.ShapeDtypeStruct((T, D_FF), BF16)] * 2 + [jax.ShapeDtypeStruct((3, D_FF), F32)] * 2
        + [jax.ShapeDtypeStruct((1, D_FF), F32)] * 2,
        scratch_shapes=[pltpu.VMEM((seq, tn), F32), pltpu.VMEM((seq, tn), F32)],
        compiler_params=_params(("parallel", "arbitrary")),
    )(z_g, z_v, da, cw_g, cw_v, cb_g.reshape(1, D_FF), cb_v.reshape(1, D_FF))
    dz_g, dz_v, dw_g, dw_v, db_g, db_v = outs
    return dz_g, dz_v, dw_g, dw_v, db_g.reshape(D_FF), db_v.reshape(D_FF)


def _layer_fwd(x, h, w, sched, tail, *, n_seq, seq, l):
    tag = f"l{l}"
    deps = sched("fwd_start", l, x)
    proj = _mm(h, w["w_in_t"], mode="nt", out_dtype=ACT_DTYPE, rotate=W_IN_ROTATE, name=f"{tag}_proj", deps=deps)
    y_att = _attention_fwd(proj, w["q_norm"], w["k_norm"], w["sinks"], n_seq=n_seq, seq=seq, name=f"{tag}_att")
    deps = sched("fwd_att", l, y_att)
    y_sgu = _sgu_fwd(proj, w["sgu_norm"], w["w_s"], w["bias_full"], n_seq=n_seq, seq=seq, name=f"{tag}_sgu")
    merged = _merge_fwd(y_att, y_sgu, w["w_oa"], w["w_ob"], proj, name=f"{tag}_merge", deps=deps)
    x1, h2 = _mm_rows(merged, w["w_out"], mode="nn", fn=_residual_then_norm, out_dtypes=(F32, BF16), rows=(x,),
                      vecs=(w["ffn_norm"],), name=f"{tag}_out")
    deps = sched("fwd_mixer_done", l, x1)
    z_g = _mm(h2, w["w_up_t"], mode="nt", out_dtype=ACT_DTYPE, b_rows=(0, D_FF), name=f"{tag}_up_g", deps=deps)
    z_v = _mm(h2, w["w_up_t"], mode="nt", out_dtype=ACT_DTYPE, b_rows=(D_FF, D_FF), name=f"{tag}_up_v")
    a = _conv_fwd(z_g, z_v, w["cw_g"], w["cw_v"], w["cb_g"], w["cb_v"], n_seq=n_seq, seq=seq, name=f"{tag}_conv")
    deps = sched("fwd_conv", l, a)
    if tail[0] == "norm":
        out = _mm_rows(a, w["w_down"], mode="nn", fn=_residual_then_norm, out_dtypes=(F32, BF16), rows=(x1,),
                       vecs=(tail[1],), name=f"{tag}_down", deps=deps)
    else:
        out = _mm_rows(a, w["w_down"], mode="nn", fn=_residual_then_loss, out_dtypes=(F32, BF16), rows=(x1, tail[1]),
                       reduce=True, name=f"{tag}_down", deps=deps)
    saved = dict(x=x, h=h, proj=proj, y_att=y_att, y_sgu=y_sgu, merged=merged, x1=x1, h2=h2, z_g=z_g, z_v=z_v, a=a)
    return out, saved


def _layer_bwd(dx2, dx2_bf, w, s, sched, deps, *, n_seq, seq, l):
    tag = f"l{l}b"
    g = {}
    da = _mm(dx2_bf, w["w_down"], mode="nt", out_dtype=ACT_DTYPE, name=f"{tag}_da", deps=deps)
    g["w_down"] = _mm(s["a"], dx2_bf, mode="tn", out_dtype=F32, name=f"{tag}_dw_down")
    dz_g, dz_v, g["cw_g"], g["cw_v"], g["cb_g"], g["cb_v"] = _conv_bwd(
        s["z_g"], s["z_v"], da, w["cw_g"], w["cw_v"], w["cb_g"], w["cb_v"], n_seq=n_seq, seq=seq, name=f"{tag}_conv")
    dh2_g = _mm(dz_g, w["w_up_t"], mode="nn", out_dtype=F32, b_rows=(0, D_FF), name=f"{tag}_dh2_g")
    dw_up_t = _mm(dz_g, s["h2"], mode="tn", out_dtype=F32, out_rows=(0, 2 * D_FF), name=f"{tag}_dw_up_g")
    g["w_up_t"] = _mm(dz_v, s["h2"], mode="tn", out_dtype=F32, out_rows=(D_FF, 2 * D_FF), out_prev=dw_up_t,
                      name=f"{tag}_dw_up_v")
    deps = sched("bwd_ffn_grads", l, dh2_g, g)
    dx1, dx1_bf, dgain = _mm_rows(dz_v, w["w_up_t"], mode="nn", fn=_sum_then_rms_bwd, out_dtypes=(F32, BF16),
                                  rows=(dh2_g, s["x1"], dx2), vecs=(w["ffn_norm"],), reduce=True, b_rows=(D_FF, D_FF),
                                  name=f"{tag}_dh2_v", deps=deps)
    g["ffn_norm"] = dgain.reshape(D_MODEL)
    dpa, dpb, dga, dgb = _merge_bwd(dx1_bf, w["w_out"], s["y_att"], s["y_sgu"], w["w_oa"], w["w_ob"], s["proj"],
                                    name=f"{tag}_merge")
    deps = sched("bwd_merge", l, dpa)
    g["w_out"] = _mm(s["merged"], dx1_bf, mode="tn", out_dtype=F32, name=f"{tag}_dw_out",
                     deps=deps)
    dy_att = _mm(dpa, w["w_oa"], mode="nt", out_dtype=BF16, name=f"{tag}_dy_att")
    dy_sgu = _mm(dpb, w["w_ob"], mode="nt", out_dtype=F32, name=f"{tag}_dy_sgu")
    g["w_oa"] = _mm(s["y_att"], dpa, mode="tn", out_dtype=F32, name=f"{tag}_dw_oa")
    g["w_ob"] = _mm(s["y_sgu"], dpb, mode="tn", out_dtype=F32, name=f"{tag}_dw_ob")
    deps = sched("bwd_out_grads", l, dy_att, g)
    dqkv, g["q_norm"], g["k_norm"], g["sinks"] = _attention_bwd(
        s["proj"], dy_att, w["q_norm"], w["k_norm"], w["sinks"], n_seq=n_seq, seq=seq, name=f"{tag}_att", deps=deps)
    deps = sched("bwd_att", l, dqkv)
    dsuv, g["sgu_norm"], g["w_s"], g["b_s"] = _sgu_bwd(
        s["proj"], dy_sgu, w["sgu_norm"], w["w_s"], w["bias_full"], n_seq=n_seq, seq=seq, name=f"{tag}_sgu", deps=deps)
    dproj = (dsuv, dga, dgb, dqkv)
    at = (QKV_WIDTH, QKV_WIDTH + 2 * SGU_WIDTH, QKV_WIDTH + 2 * SGU_WIDTH + D_MODEL, 0)
    dw = None
    for i, (piece, first) in enumerate(zip(dproj, at)):
        dw = _mm(piece, s["h"], mode="tn", out_dtype=F32, out_rows=(first, IN_WIDTH), out_prev=dw, name=f"{tag}_dw_in_{i}")
    g["w_in_t"] = dw
    deps = sched("bwd_w_in_grad", l, dqkv, g)
    dx, dx_bf, dgain = _mm_rows(dproj, w["w_in_t"], mode="nn", fn=_rms_bwd_rows, out_dtypes=(F32, BF16),
                                rows=(s["x"], dx1), vecs=(w["mix_norm"],), reduce=True, a_at=at,
                                name=f"{tag}_dh", deps=deps)
    g["mix_norm"] = dgain.reshape(D_MODEL)
    return dx, dx_bf, g, sched("bwd_dh", l, dx)


def _local_step(x, target, weights, sched, *, n_seq, seq):
    depth = len(weights)
    saved = []
    h = _rms_fwd(x, weights[0]["mix_norm"], name="l0_mix_norm")
    for l in range(depth):
        tail = ("norm", weights[l + 1]["mix_norm"]) if l + 1 < depth else ("loss", target)
        out, s = _layer_fwd(x, h, weights[l], sched, tail, n_seq=n_seq, seq=seq, l=l)
        saved.append(s)
        if l + 1 < depth:
            x, h = out
    dy, dy_bf, loss_cols = out
    grads = [None] * depth
    deps = ()
    for l in reversed(range(depth)):
        dy, dy_bf, grads[l], deps = _layer_bwd(dy, dy_bf, weights[l], saved[l], sched, deps, n_seq=n_seq, seq=seq, l=l)
    return jnp.sum(loss_cols), dy, grads


W_IN_SHARD = IN_WIDTH // N_DEV
W_UP_SHARD = 2 * D_FF // N_DEV
COL_MOVE_ROWS = 256


def _w_o_moves():
    return tuple((j, 0, LANES, 0, j * LANES) for j in range(N_DEV))


def _disassemble(mats, w, moves, *, name):
    R = mats[0].shape[0]
    tr = min(R, COL_MOVE_ROWS)
    n = len(mats)

    def body(*refs):
        m_refs, o_ref = refs[:n], refs[n]
        for j, lo, hi, which, at in moves:
            o_ref[j, :, lo:hi] = m_refs[which][:, at:at + hi - lo]

    return pl.pallas_call(
        body, name=name, grid=(R // tr,),
        in_specs=[pl.BlockSpec((tr, m.shape[1]), lambda i: (i, 0)) for m in mats],
        out_specs=pl.BlockSpec((N_DEV, tr, w), lambda i: (0, i, 0)),
        out_shape=jax.ShapeDtypeStruct((N_DEV, R, w), mats[0].dtype),
        compiler_params=_params(("parallel",)),
    )(*mats)


def _my_place():
    return lax.axis_index("x"), lax.axis_index("y"), lax.axis_index("c")


def _gathered_shape(shape, kind):
    r, c = shape
    return {"blocks": (N_DEV, r, c), "rows": (N_DEV * r, c), "cols": (r, N_DEV * c)}[kind]


def _gather_window(ref, kind, shape, j):
    r, c = shape
    if kind == "blocks":
        return ref.at[j]
    if kind == "rows":
        return ref.at[pl.ds(pl.multiple_of(j * r, r), r), :]
    return ref.at[:, pl.ds(pl.multiple_of(j * c, c), c)]


def _gather(srcs, kinds, *, name):
    n = len(srcs)
    shapes = [s.shape for s in srcs]
    per = 7

    def body(*refs):
        src_refs, dst_refs = refs[:n], refs[n:2 * n]
        send_sems, recv_sems, local_sems = refs[2 * n:]
        x, y, c = _my_place()
        me, sibling = (x, y, c), (x, y, 1 - c)
        chips = [(1 - x, y), (x, 1 - y), (1 - x, 1 - y)]

        def at(i, px, py, pc):
            return _gather_window(dst_refs[i], kinds[i], shapes[i], 4 * px + 2 * py + pc)

        def copy(i, k, block, to, src=None):
            return pltpu.make_async_remote_copy(
                src_ref=at(i, *block) if src is None else src, dst_ref=at(i, *block),
                send_sem=send_sems.at[per * i + k], recv_sem=recv_sems.at[per * i + k], device_id=to, device_id_type=MESH)

        mine = [pltpu.make_async_copy(src_refs[i], at(i, *me), local_sems.at[i]) for i in range(n)]
        for cp in mine:
            cp.start()
        started = []
        for i in range(n):
            first = [copy(i, 0, me, sibling, src=src_refs[i])]
            first += [copy(i, 1 + j, me, (*chip, c), src=src_refs[i]) for j, chip in enumerate(chips)]
            for cp in first:
                cp.start()
            started += first
        for i in range(n):
            for j, chip in enumerate(chips):
                copy(i, 1 + j, (*chip, c), me).wait_recv()
                fwd = copy(i, 4 + j, (*chip, c), sibling)
                fwd.start()
                started.append(fwd)
        for i in range(n):
            copy(i, 0, sibling, me).wait_recv()
            for j, chip in enumerate(chips):
                copy(i, 4 + j, (*chip, 1 - c), me).wait_recv()
        for cp in started:
            cp.wait_send()
        for cp in mine:
            cp.wait()

    return pl.pallas_call(
        body, name=name,
        out_shape=[jax.ShapeDtypeStruct(_gathered_shape(s.shape, k), s.dtype) for s, k in zip(srcs, kinds)],
        in_specs=[ANY] * n, out_specs=[ANY] * n,
        scratch_shapes=[pltpu.SemaphoreType.DMA((per * n,)), pltpu.SemaphoreType.DMA((per * n,)),
                        pltpu.SemaphoreType.DMA((n,))],
    )(*srcs)


HBM = pl.BlockSpec(memory_space=pltpu.HBM)
SEM = pl.BlockSpec(memory_space=pltpu.SEMAPHORE)
TOKEN = jax.ShapeDtypeStruct((SUBLANES, LANES), F32)
TOKEN_SPEC = pl.BlockSpec(memory_space=pltpu.VMEM)
SPLIT_PARAMS = pltpu.CompilerParams(has_side_effects=pltpu.SideEffectType.DATAFLOW_SIDE_EFFECTING)


def _in_hbm(x):
    return pltpu.with_memory_space_constraint(x, pltpu.HBM)


def _hbm_like(shape, dtype):
    return pltpu.HBM(shape, dtype)


def _place_own(shards, kinds, dtypes, *, name):
    n = len(shards)
    shapes = [s.shape for s in shards]

    def body(*refs):
        s_refs, land_refs, bufs, sems = refs[:n], refs[n:2 * n], refs[2 * n:3 * n], refs[3 * n]
        x, y, c = _my_place()
        copies = []
        for i in range(n):
            bufs[i][...] = s_refs[i][...].astype(dtypes[i])
            copies.append(pltpu.make_async_copy(
                bufs[i], _gather_window(land_refs[i], kinds[i], shapes[i], 4 * x + 2 * y + c), sems.at[i]))
        for cp in copies:
            cp.start()
        for cp in copies:
            cp.wait()

    return pl.pallas_call(
        body, name=name,
        out_shape=[jax.ShapeDtypeStruct(_gathered_shape(s, k), d) for s, k, d in zip(shapes, kinds, dtypes)],
        in_specs=[pl.BlockSpec(memory_space=pltpu.VMEM)] * n, out_specs=[ANY] * n,
        scratch_shapes=[pltpu.VMEM(s, d) for s, d in zip(shapes, dtypes)] + [pltpu.SemaphoreType.DMA((n,))],
        compiler_params=_params(),
    )(*shards)


def _gather_start(lands, kinds, shapes, after=(), *, name):
    n = len(lands)
    n_after = len(after)

    def body(*refs):
        land_refs = refs[:n]
        send_sems, recv_sems = refs[n + n_after], refs[n + n_after + 1]
        x, y, c = _my_place()
        targets = [(x, y, 1 - c), (1 - x, y, c), (x, 1 - y, c), (1 - x, 1 - y, c)]
        for i in range(n):
            own = _gather_window(land_refs[i], kinds[i], shapes[i], 4 * x + 2 * y + c)
            for k, to in enumerate(targets):
                pltpu.make_async_remote_copy(
                    src_ref=own, dst_ref=own, send_sem=send_sems.at[4 * i + k], recv_sem=recv_sems.at[4 * i + k],
                    device_id=to, device_id_type=MESH).start()
        refs[-1][...] = jnp.zeros_like(refs[-1])

    outs = pl.pallas_call(
        body, name=name,
        out_shape=[pltpu.SemaphoreType.DMA((4 * n,)), pltpu.SemaphoreType.DMA((4 * n,))]
        + [_hbm_like(a.shape, a.dtype) for a in lands] + [TOKEN],
        in_specs=[HBM] * n + [ANY] * n_after, out_specs=[SEM, SEM] + [HBM] * n + [TOKEN_SPEC],
        input_output_aliases={i: 2 + i for i in range(n)},
        compiler_params=SPLIT_PARAMS,
    )(*[_in_hbm(a) for a in lands], *after)
    return outs[0], outs[1], outs[2:2 + n], outs[-1]


def _gather_forward(recv_sems, lands, kinds, shapes, after, *, name):
    n = len(lands)

    def body(*refs):
        recv_ref, land_refs = refs[0], refs[1:1 + n]
        fwd_send, fwd_recv = refs[2 + n], refs[3 + n]
        token = refs[-1]
        x, y, c = _my_place()
        chips = [(1 - x, y), (x, 1 - y), (1 - x, 1 - y)]
        for i in range(n):
            for j, (px, py) in enumerate(chips):
                block = _gather_window(land_refs[i], kinds[i], shapes[i], 4 * px + 2 * py + c)
                pltpu.make_async_remote_copy(
                    src_ref=block, dst_ref=block, send_sem=fwd_send.at[3 * i + j], recv_sem=recv_ref.at[4 * i + 1 + j],
                    device_id=(px, py, c), device_id_type=MESH).wait_recv()
                pltpu.make_async_remote_copy(
                    src_ref=block, dst_ref=block, send_sem=fwd_send.at[3 * i + j], recv_sem=fwd_recv.at[3 * i + j],
                    device_id=(x, y, 1 - c), device_id_type=MESH).start()
        token[...] = jnp.zeros_like(token)

    outs = pl.pallas_call(
        body, name=name,
        out_shape=[pltpu.SemaphoreType.DMA((3 * n,)), pltpu.SemaphoreType.DMA((3 * n,))]
        + [_hbm_like(a.shape, a.dtype) for a in lands] + [TOKEN],
        in_specs=[SEM] + [HBM] * n + [ANY], out_specs=[SEM, SEM] + [HBM] * n + [TOKEN_SPEC],
        input_output_aliases={1 + i: 2 + i for i in range(n)},
        compiler_params=SPLIT_PARAMS,
    )(recv_sems, *lands, after)
    return outs[0], outs[1], outs[2:2 + n], outs[-1]


def _gather_finish(send_sems, recv_sems, fwd_send, fwd_recv, lands, kinds, shapes, after, *, name):
    n = len(lands)

    def body(*refs):
        send_ref, recv_ref, fsend_ref, frecv_ref = refs[:4]
        land_refs = refs[4:4 + n]
        x, y, c = _my_place()
        chips = [(1 - x, y), (x, 1 - y), (1 - x, 1 - y)]
        sibling = (x, y, 1 - c)
        for i in range(n):
            def window(j):
                return _gather_window(land_refs[i], kinds[i], shapes[i], j)

            mine, theirs = window(4 * x + 2 * y + c), window(4 * x + 2 * y + (1 - c))
            pltpu.make_async_remote_copy(src_ref=mine, dst_ref=theirs, send_sem=send_ref.at[4 * i],
                                         recv_sem=recv_ref.at[4 * i], device_id=sibling, device_id_type=MESH).wait_recv()
            for j, (px, py) in enumerate(chips):
                block = window(4 * px + 2 * py + (1 - c))
                pltpu.make_async_remote_copy(src_ref=block, dst_ref=block, send_sem=fsend_ref.at[3 * i + j],
                                             recv_sem=frecv_ref.at[3 * i + j], device_id=sibling,
                                             device_id_type=MESH).wait_recv()
            for k in range(4):
                pltpu.make_async_remote_copy(src_ref=mine, dst_ref=mine, send_sem=send_ref.at[4 * i + k],
                                             recv_sem=recv_ref.at[4 * i + k], device_id=sibling,
                                             device_id_type=MESH).wait_send()
            for j, (px, py) in enumerate(chips):
                block = window(4 * px + 2 * py + c)
                pltpu.make_async_remote_copy(src_ref=block, dst_ref=block, send_sem=fsend_ref.at[3 * i + j],
                                             recv_sem=frecv_ref.at[3 * i + j], device_id=sibling,
                                             device_id_type=MESH).wait_send()

    return pl.pallas_call(
        body, name=name,
        out_shape=[_hbm_like(a.shape, a.dtype) for a in lands],
        in_specs=[SEM] * 4 + [HBM] * n + [ANY], out_specs=[HBM] * n,
        input_output_aliases={4 + i: i for i in range(n)},
        compiler_params=SPLIT_PARAMS,
    )(send_sems, recv_sems, fwd_send, fwd_recv, *lands, after)


def _pair_plan(src_ref, land_ref, x, y, c):
    return [(src_ref.at[2 * k + (1 - c)], land_ref.at[k], (x, y, 1 - c)) for k in range(N_CHIPS)]


def _chip_plan(src_ref, land_ref, x, y, c):
    chips = [(1 - x, y), (x, 1 - y), (1 - x, 1 - y)]
    return [(src_ref.at[2 * px + py], land_ref.at[k], (px, py, c)) for k, (px, py) in enumerate(chips)]


def _exchange_copies(plan, per, src_refs, land_refs, send_sems, recv_sems):
    x, y, c = _my_place()
    copies = []
    for i, (s_ref, l_ref) in enumerate(zip(src_refs, land_refs)):
        for q, (src, dst, to) in enumerate(plan(s_ref, l_ref, x, y, c)):
            copies.append(pltpu.make_async_remote_copy(
                src_ref=src, dst_ref=dst, send_sem=send_sems.at[per * i + q], recv_sem=recv_sems.at[per * i + q],
                device_id=to, device_id_type=MESH))
    return copies


def _exchange_start(srcs, plan, per, *, name):
    n = len(srcs)

    def body(*refs):
        src_refs, land_refs = refs[:n], refs[n:2 * n]
        send_sems, recv_sems = refs[2 * n], refs[2 * n + 1]
        for cp in _exchange_copies(plan, per, src_refs, land_refs, send_sems, recv_sems):
            cp.start()
        refs[-1][...] = jnp.zeros_like(refs[-1])

    lands = [lax.empty((per,) + s.shape[1:], s.dtype) for s in srcs]
    outs = pl.pallas_call(
        body, name=name,
        out_shape=[pltpu.SemaphoreType.DMA((per * n,)), pltpu.SemaphoreType.DMA((per * n,))]
        + [_hbm_like(s.shape, s.dtype) for s in srcs] + [_hbm_like(a.shape, a.dtype) for a in lands] + [TOKEN],
        in_specs=[HBM] * (2 * n), out_specs=[SEM, SEM] + [HBM] * (2 * n) + [TOKEN_SPEC],
        input_output_aliases={i: 2 + i for i in range(2 * n)},
        compiler_params=SPLIT_PARAMS,
    )(*[_in_hbm(s) for s in srcs], *[_in_hbm(a) for a in lands])
    return outs[0], outs[1], outs[2:2 + n], outs[2 + n:2 + 2 * n], outs[-1]


def _exchange_wait(send_sems, recv_sems, srcs, lands, plan, per, after, *, name):
    n = len(srcs)

    def body(*refs):
        send_ref, recv_ref = refs[0], refs[1]
        src_refs, land_refs = refs[2:2 + n], refs[2 + n:2 + 2 * n]
        copies = _exchange_copies(plan, per, src_refs, land_refs, send_ref, recv_ref)
        for cp in copies:
            cp.wait_recv()
        for cp in copies:
            cp.wait_send()

    outs = pl.pallas_call(
        body, name=name,
        out_shape=[_hbm_like(s.shape, s.dtype) for s in srcs] + [_hbm_like(a.shape, a.dtype) for a in lands],
        in_specs=[SEM, SEM] + [HBM] * (2 * n) + [ANY], out_specs=[HBM] * (2 * n),
        input_output_aliases={2 + i: i for i in range(2 * n)},
        compiler_params=SPLIT_PARAMS,
    )(send_sems, recv_sems, *srcs, *lands, after)
    return outs[:n], outs[n:]


REDUCE_BLOCK_BYTES = 1 << 20


def _row_tile(r, c):
    row_bytes = 4 * (-(-c // LANES) * LANES)
    best = r
    for d in range(SUBLANES, r, SUBLANES):
        if r % d == 0 and d * row_bytes <= REDUCE_BLOCK_BYTES:
            best = d
    return best if r * row_bytes > REDUCE_BLOCK_BYTES else r


def _reduce_pair_sum(blocked, recv, place, wire_dtype, *, name):
    _, r, c = blocked.shape
    tr = _row_tile(r, c)

    def body(place_ref, g_ref, r_ref, own_ref, send_ref):
        s = g_ref[...] + r_ref[...]
        send_ref[...] = s.astype(wire_dtype)

        @pl.when(pl.program_id(1) == place_ref[1])
        def _():
            own_ref[...] = s

    return pl.pallas_call(
        body, name=name,
        grid_spec=pltpu.PrefetchScalarGridSpec(
            num_scalar_prefetch=1, grid=(r // tr, N_CHIPS),
            in_specs=[pl.BlockSpec((None, None, tr, c), lambda i, k, place_ref: (k, place_ref[0], i, 0)),
                      pl.BlockSpec((None, tr, c), lambda i, k, place_ref: (k, i, 0))],
            out_specs=[pl.BlockSpec((tr, c), lambda i, k, place_ref: (i, 0)),
                       pl.BlockSpec((None, tr, c), lambda i, k, place_ref: (k, i, 0))]),
        out_shape=[jax.ShapeDtypeStruct((r, c), F32), jax.ShapeDtypeStruct((N_CHIPS, r, c), wire_dtype)],
        compiler_params=_params(("parallel", "arbitrary")),
    )(place, blocked.reshape(N_CHIPS, 2, r, c), recv)


def _chip_sum(own_ref, r_ref):
    return ((own_ref[...] + r_ref[0].astype(F32)) + r_ref[1].astype(F32)) + r_ref[2].astype(F32)


def _reduce_chip_sum(own, recv, *, name):
    r, c = own.shape
    tr = _row_tile(r, c)

    def body(own_ref, r_ref, o_ref):
        o_ref[...] = _chip_sum(own_ref, r_ref)

    return pl.pallas_call(
        body, name=name, grid=(r // tr,),
        in_specs=[pl.BlockSpec((tr, c), lambda i: (i, 0)), pl.BlockSpec((N_CHIPS - 1, tr, c), lambda i: (0, i, 0))],
        out_specs=pl.BlockSpec((tr, c), lambda i: (i, 0)),
        out_shape=jax.ShapeDtypeStruct((r, c), F32),
        compiler_params=_params(("parallel",)),
    )(own, recv)


def _adamw_math(w, g, m, v):
    nm = ADAM_B1 * m + (1.0 - ADAM_B1) * g
    nv = ADAM_B2 * v + (1.0 - ADAM_B2) * (g * g)
    m_hat = nm / (1.0 - ADAM_B1 ** ADAM_STEP)
    v_hat = nv / (1.0 - ADAM_B2 ** ADAM_STEP)
    return -ADAM_LR * (m_hat / (jnp.sqrt(v_hat) + ADAM_EPS) + ADAM_WD * w), nm, nv


def _adamw(w, g, m, v, *, name):
    shape = w.shape
    C = shape[-1]
    R = math.prod(shape[:-1])
    tr = _row_tile(R, C)

    def body(w_ref, g_ref, m_ref, v_ref, d_ref, nm_ref, nv_ref):
        d_ref[...], nm_ref[...], nv_ref[...] = _adamw_math(w_ref[...], g_ref[...], m_ref[...], v_ref[...])

    spec = pl.BlockSpec((tr, C), lambda i: (i, 0))
    outs = pl.pallas_call(
        body, name=name, grid=(R // tr,),
        in_specs=[spec] * 4, out_specs=[spec] * 3,
        out_shape=[jax.ShapeDtypeStruct((R, C), F32)] * 3,
        compiler_params=_params(("parallel",)),
    )(*[a.reshape(R, C) for a in (w, g, m, v)])
    return tuple(o.reshape(shape) for o in outs)


def _reduce_adamw(own, recv, w, m, v, layer, prev, *, name):
    r, c = own.shape
    tr = _row_tile(r, c)
    n_prev = 0 if prev is None else len(prev)

    def body(own_ref, r_ref, w_ref, m_ref, v_ref, *rest):
        g_ref, d_ref, nm_ref, nv_ref = rest[n_prev:]
        g = _chip_sum(own_ref, r_ref)
        g_ref[...] = g
        d_ref[...], nm_ref[...], nv_ref[...] = _adamw_math(w_ref[...], g, m_ref[...], v_ref[...])

    slot = pl.BlockSpec((None, tr, c), lambda i: (layer, i, 0))
    return pl.pallas_call(
        body, name=name, grid=(r // tr,),
        in_specs=[pl.BlockSpec((tr, c), lambda i: (i, 0)), pl.BlockSpec((N_CHIPS - 1, tr, c), lambda i: (0, i, 0)),
                  slot, slot, slot] + [ANY] * n_prev,
        out_specs=[slot] * 4,
        out_shape=[jax.ShapeDtypeStruct((DEPTH, r, c), F32)] * 4,
        input_output_aliases={5 + k: k for k in range(n_prev)},
        compiler_params=_params(("parallel",)),
    )(own, recv, w, m, v, *(prev or ()))


REPLICATED = (("mix_norm", (D_MODEL,)), ("q_norm", (HEAD_DIM,)), ("k_norm", (HEAD_DIM,)), ("sinks", (N_Q_HEADS,)),
              ("sgu_norm", (SGU_WIDTH,)), ("w_s", (SGU_GROUPS, BLOCK, BLOCK)), ("b_s", (SGU_GROUPS, BLOCK)),
              ("ffn_norm", (D_MODEL,)), ("conv_b", (2 * D_FF,)))
TRANSPOSED = ("w_in", "w_up")
SHARDED = (("w_in", "rows"), ("w_oa", "cols"), ("w_ob", "cols"), ("w_out", "rows"), ("w_up", "rows"),
           ("conv_w", "blocks"), ("w_down", "rows"))
WEIGHT_ORDER = ("mix_norm", "w_in", "q_norm", "k_norm", "sinks", "sgu_norm", "w_s", "b_s", "w_oa", "w_ob", "w_out",
                "ffn_norm", "w_up", "conv_w", "conv_b", "w_down")
MIXER_WEIGHTS = ["w_in", "w_oa", "w_ob", "w_out"]
FFN_WEIGHTS = ["w_up", "conv_w", "w_down"]


def _small_layout():
    segs, off = {}, 0
    for l in range(DEPTH):
        for name, shape in REPLICATED:
            n = math.prod(shape)
            segs[(l, name)] = (off, n)
            off += n
    per_dev = -(-off // (N_DEV * SUBLANES * LANES)) * SUBLANES * LANES
    return segs, off, per_dev


def _pack_small(grads):
    ssegs, total, per_dev = _small_layout()
    flat = jnp.concatenate([grads[l][name].reshape(-1) for (l, name) in ssegs])
    return jnp.pad(flat, (0, N_DEV * per_dev - total)).reshape(N_DEV, per_dev // LANES, LANES)


def _unpack_small(gathered):
    ssegs, _, _ = _small_layout()
    flat = gathered.reshape(-1)
    shapes = dict(REPLICATED)
    return {name: jnp.stack([flat[ssegs[(l, name)][0]:ssegs[(l, name)][0] + ssegs[(l, name)][1]].reshape(shapes[name])
                             for l in range(DEPTH)]) for name, _ in REPLICATED}


def kernel(x, mix_norm, w_in, q_norm, k_norm, sinks, sgu_norm, w_s, b_s, w_oa, w_ob, w_out, ffn_norm, w_up, conv_w, conv_b, w_down, loss_target, m_mix_norm, m_w_in, m_q_norm, m_k_norm, m_sinks, m_sgu_norm, m_w_s, m_b_s, m_w_oa, m_w_ob, m_w_out, m_ffn_norm, m_w_up, m_conv_w, m_conv_b, m_w_down, v_mix_norm, v_w_in, v_q_norm, v_k_norm, v_sinks, v_sgu_norm, v_w_s, v_b_s, v_w_oa, v_w_ob, v_w_out, v_ffn_norm, v_w_up, v_conv_w, v_conv_b, v_w_down):
    W = dict(mix_norm=mix_norm, w_in=w_in, q_norm=q_norm, k_norm=k_norm, sinks=sinks, sgu_norm=sgu_norm, w_s=w_s, b_s=b_s,
             w_oa=w_oa, w_ob=w_ob, w_out=w_out, ffn_norm=ffn_norm, w_up=w_up, conv_w=conv_w, conv_b=conv_b, w_down=w_down)
    M = dict(mix_norm=m_mix_norm, w_in=m_w_in, q_norm=m_q_norm, k_norm=m_k_norm, sinks=m_sinks, sgu_norm=m_sgu_norm,
             w_s=m_w_s, b_s=m_b_s, w_oa=m_w_oa, w_ob=m_w_ob, w_out=m_w_out, ffn_norm=m_ffn_norm, w_up=m_w_up,
             conv_w=m_conv_w, conv_b=m_conv_b, w_down=m_w_down)
    V = dict(mix_norm=v_mix_norm, w_in=v_w_in, q_norm=v_q_norm, k_norm=v_k_norm, sinks=v_sinks, sgu_norm=v_sgu_norm,
             w_s=v_w_s, b_s=v_b_s, w_oa=v_w_oa, w_ob=v_w_ob, w_out=v_w_out, ffn_norm=v_ffn_norm, w_up=v_w_up,
             conv_w=v_conv_w, conv_b=v_conv_b, w_down=v_w_down)
    n_seq, seq, d_model = x.shape
    tokens = n_seq * seq
    mx, my, mc = _my_place()
    place = jnp.stack([mc, 2 * mx + my]).astype(jnp.int32)
    half = N_DEV // 2
    kind_of = dict(SHARDED)
    for name in TRANSPOSED:
        W[name], M[name], V[name] = (jnp.swapaxes(t[name], 1, 2) for t in (W, M, V))

    gather_groups = [[(l, n) for n in names] for l in range(DEPTH) for names in (MIXER_WEIGHTS, FFN_WEIGHTS)]
    started, in_flight = {}, {}
    weights = []
    for l in range(DEPTH):
        w = {name: W[name][l] for name, _ in REPLICATED}
        w["cb_g"], w["cb_v"] = W["conv_b"][l][:D_FF], W["conv_b"][l][D_FF:]
        w["bias_full"] = jnp.repeat(W["b_s"][l].T, SGU_WIDTH // SGU_GROUPS, axis=1)
        weights.append(w)

    def gather_start(gi, after=()):
        shards = [W[name][l] for l, name in gather_groups[gi]]
        kinds = [kind_of[name] for _, name in gather_groups[gi]]
        shapes = [s.shape for s in shards]
        lands = _place_own(shards, kinds, [F32 if name == "conv_w" else BF16 for _, name in gather_groups[gi]],
                           name=f"gather_weights_own_{gi}")
        send, recv, lands, token = _gather_start(lands, kinds, shapes, after, name=f"gather_weights_start_{gi}")
        started[gi] = dict(sems=(send, recv), lands=lands, kinds=kinds, shapes=shapes)
        return token

    def gather_forward(gi, after):
        st = started[gi]
        in_flight[gi] = _gather_forward(st["sems"][1], st["lands"], st["kinds"], st["shapes"], after,
                                        name=f"gather_weights_forward_{gi}")
        return in_flight[gi][3]

    def gather_finish(gi, after):
        st = started.pop(gi)
        fwd_send, fwd_recv, lands_g, _ = in_flight.pop(gi)
        whole = _gather_finish(st["sems"][0], st["sems"][1], fwd_send, fwd_recv, lands_g, st["kinds"], st["shapes"], after,
                               name=f"gather_weights_finish_{gi}")
        for (l, name), arr in zip(gather_groups[gi], whole):
            w = weights[l]
            if name in TRANSPOSED:
                w[name + "_t"] = arr
            elif name == "conv_w":
                w["cw_g"] = arr[:half].transpose(1, 0, 2).reshape(3, D_FF)
                w["cw_v"] = arr[half:].transpose(1, 0, 2).reshape(3, D_FF)
            else:
                w[name] = arr

    reduce_state, results = {}, {}
    wire = {"conv_w": F32, "small": F32}

    def reduce_begin(key, names, arrays):
        send, recv, srcs_, lands_, token = _exchange_start(arrays, _pair_plan, N_CHIPS, name=f"reduce_pair_start_{key}")
        reduce_state[key] = dict(names=names, pair=(send, recv, srcs_, lands_))
        return [token]

    def reduce_pair(key, after):
        st = reduce_state[key]
        send, recv, srcs_, lands_ = st.pop("pair")
        blocked_, from_sibling = _exchange_wait(send, recv, srcs_, lands_, _pair_plan, N_CHIPS, after,
                                                name=f"reduce_pair_wait_{key}")
        sums = [_reduce_pair_sum(b, r, place, wire.get(n if isinstance(n, str) else n[1], BF16),
                                 name=f"reduce_pair_sum_{key}_{i}")
                for i, (n, b, r) in enumerate(zip(st["names"], blocked_, from_sibling))]
        st["own"] = [s[0] for s in sums]
        *st["chip"], token = _exchange_start([s[1] for s in sums], _chip_plan, N_CHIPS - 1, name=f"reduce_chip_start_{key}")
        return [token]

    def reduce_end(key, after):
        st = reduce_state.pop(key)
        send, recv, srcs_, lands_ = st["chip"]
        _, from_chips = _exchange_wait(send, recv, srcs_, lands_, _chip_plan, N_CHIPS - 1, after,
                                       name=f"reduce_chip_wait_{key}")
        done = []
        for n, own, got in zip(st["names"], st["own"], from_chips):
            if n == "small":
                results["small"] = _reduce_chip_sum(own, got, name="reduce_chip_sum_small")
            else:
                l, name = n
                results[name] = _reduce_adamw(own, got, W[name], M[name], V[name], l, results.get(name),
                                              name=f"l{l}_reduce_adamw_{name}")
                done.append(results[name][0])
        return done

    def sched(point, l, carry, g=None):
        deps = []
        if point == "fwd_start" and l == 0:
            token = gather_forward(0, gather_start(0))
            gather_finish(0, token)
            deps = [gather_start(1, [weights[0]["w_out"]])]
        elif point == "fwd_att" and l == 0:
            deps = [gather_forward(1, carry), gather_start(2, [carry])]
        elif point == "fwd_mixer_done" and l == 0:
            gather_finish(1, carry)
            deps = [gather_start(3, [carry])]
        elif point == "fwd_conv" and l == 0:
            deps = [gather_forward(2, carry)]
        elif point == "fwd_start" and l == 1:
            gather_finish(2, carry)
        elif point == "fwd_att" and l == 1:
            deps = [gather_forward(3, carry)]
        elif point == "fwd_mixer_done" and l == 1:
            gather_finish(3, carry)
        elif point == "bwd_ffn_grads":
            if l + 1 < DEPTH:
                deps += reduce_end(f"l{l + 1}_in", g["w_up_t"])
            conv_w = jnp.concatenate([g[k].reshape(3, half, W_UP_SHARD).transpose(1, 0, 2) for k in ("cw_g", "cw_v")])
            deps += reduce_begin(
                f"l{l}_ffn", [(l, "w_down"), (l, "w_up"), (l, "conv_w")],
                [g["w_down"].reshape(N_DEV, D_FF // N_DEV, D_MODEL),
                 g["w_up_t"].reshape(N_DEV, W_UP_SHARD, D_MODEL), conv_w])
        elif point == "bwd_merge":
            deps = reduce_pair(f"l{l}_ffn", carry)
        elif point == "bwd_out_grads":
            deps = reduce_begin(
                f"l{l}_out", [(l, "w_out"), (l, "w_oa"), (l, "w_ob")],
                [g["w_out"].reshape(N_DEV, D_MODEL // N_DEV, D_MODEL),
                 _disassemble((g["w_oa"],), LANES, _w_o_moves(), name=f"l{l}_split_dw_oa"),
                 _disassemble((g["w_ob"],), LANES, _w_o_moves(), name=f"l{l}_split_dw_ob")])
        elif point == "bwd_att":
            deps = reduce_pair(f"l{l}_out", carry) + reduce_end(f"l{l}_ffn", carry)
        elif point == "bwd_w_in_grad":
            deps = reduce_begin(f"l{l}_in", [(l, "w_in")], [g["w_in_t"].reshape(N_DEV, W_IN_SHARD, D_MODEL)])
        elif point == "bwd_dh":
            deps = reduce_pair(f"l{l}_in", carry) + reduce_end(f"l{l}_out", carry)
        return deps

    loss_part, dx, grads = _local_step(x.reshape(tokens, d_model), loss_target.reshape(tokens, d_model), weights, sched,
                                       n_seq=n_seq, seq=seq)
    loss = lax.psum(loss_part, ("x", "y", "c"))

    for g in grads:
        g["conv_b"] = jnp.concatenate([g["cb_g"], g["cb_v"]])
    reduce_begin("small", ["small"], [_pack_small(grads)])
    reduce_end("l0_in", dx)
    reduce_pair("small", results["w_in"][0])
    reduce_end("small", results["w_in"][1])

    G, delta, new_m, new_v = {}, {}, {}, {}
    for name, _ in SHARDED:
        outs = [jnp.swapaxes(o, 1, 2) for o in results[name]] if name in TRANSPOSED else results[name]
        G[name], delta[name], new_m[name], new_v[name] = outs
    G.update(_unpack_small(_gather([results["small"]], ["blocks"], name="gather_small_grads")[0]))
    for name, _ in REPLICATED:
        delta[name], new_m[name], new_v[name] = _adamw(W[name], G[name], M[name], V[name], name=f"adamw_{name}")
    return (loss, dx.reshape(n_seq, seq, d_model), *[G[n] for n in WEIGHT_ORDER], *[delta[n] for n in WEIGHT_ORDER],
            *[new_m[n] for n in WEIGHT_ORDER], *[new_v[n] for n in WEIGHT_ORDER])
```

```python
import math

import jax
import jax.numpy as jnp
from jax import lax
from jax.experimental import pallas as pl
from jax.experimental.pallas import tpu as pltpu

F32 = jnp.float32
BF16 = jnp.bfloat16
ACT_DTYPE = BF16
MESH = pl.DeviceIdType.MESH

DEPTH = 2
D_MODEL = 1024
N_Q_HEADS = 8
HEAD_DIM = 64
ATT_WIDTH = 512
KV_WIDTH = 128
BLOCK = 128
SGU_WIDTH = 512
SGU_GROUPS = 8
IN_WIDTH = 3840
D_FF = 2816
NORM_EPS = 1e-6
NEG_INF = -1e30
ATT_SCALE = HEAD_DIM ** -0.5
ALIBI_SLOPES = tuple(2.0 ** (-(h + 1)) for h in range(N_Q_HEADS))
ADAM_LR, ADAM_B1, ADAM_B2, ADAM_EPS, ADAM_WD, ADAM_STEP = 0.001, 0.9, 0.999, 1e-08, 0.01, 10
N_DEV = 8
N_CHIPS = 4

QKV_WIDTH = ATT_WIDTH + 2 * KV_WIDTH
REST_WIDTH = IN_WIDTH - QKV_WIDTH
COL_SUV, COL_GA, COL_GB, COL_QKV = 0, 1024, 2048, 3072
W_IN_ROTATE = (1, IN_WIDTH // QKV_WIDTH)

LANES = 128
SUBLANES = 8
VMEM_LIMIT_V7X = 56 * 1024 * 1024
GELU_C = math.sqrt(2.0 / math.pi)
GELU_K = 0.044715
ANY = pl.BlockSpec(memory_space=pl.ANY)


def _params(sem=None):
    return pltpu.CompilerParams(dimension_semantics=sem, vmem_limit_bytes=VMEM_LIMIT_V7X)


def _sigmoid(x):
    return 1.0 / (1.0 + jnp.exp(-x))


def _gelu(x):
    th = jnp.tanh(GELU_C * (x + GELU_K * x * x * x))
    return 0.5 * x * (1.0 + th)


def _gelu_and_grad(x):
    x2 = x * x
    th = jnp.tanh(GELU_C * (x + GELU_K * x2 * x))
    g = 0.5 * x * (1.0 + th)
    dg = 0.5 * (1.0 + th) + 0.5 * x * (1.0 - th * th) * (GELU_C * (1.0 + 3.0 * GELU_K * x2))
    return g, dg


def _dot(a, b, dims):
    return lax.dot_general(a, b, (dims, ((), ())), preferred_element_type=F32)


def _dot_nn(a, b):
    return _dot(a, b, ((1,), (0,)))


def _dot_nt(a, b):
    return _dot(a, b, ((1,), (1,)))


def _dot_tn(a, b):
    return _dot(a, b, ((0,), (0,)))


def _lo_mask(shape):
    return lax.broadcasted_iota(jnp.int32, shape, len(shape) - 1) < (LANES // 2)


def _half_sums(x, lo):
    s_lo = jnp.sum(jnp.where(lo, x, 0.0), axis=-1, keepdims=True)
    s_all = jnp.sum(x, axis=-1, keepdims=True)
    return jnp.where(lo, s_lo, s_all - s_lo)


def _dup_half(x, half, lo):
    r = pltpu.roll(x, LANES // 2, axis=1)
    return jnp.where(lo, x, r) if half == 0 else jnp.where(lo, r, x)


def _with_deps(body, n_in, deps):
    k = len(deps)
    if not k:
        return body, [], ()

    def skipping(*refs):
        return body(*refs[:n_in], *refs[n_in + k:])

    return skipping, [ANY] * k, tuple(deps)


MM_VMEM_BUDGET = 40 * 1024 * 1024
MM_MAX_TILE = 1408
MM_MAX_TK = 4096
MM_STEP_BYTES = 1 << 20


def _divisors(n, step, cap):
    return [d for d in range(step, min(n, cap) + 1, step) if n % d == 0] or [n]


def _mm_tiles(M, N, K, out_bytes, tm_divides, tn_divides):
    best = None
    for tm in _divisors(M, LANES, MM_MAX_TILE):
        for tn in _divisors(N, LANES, MM_MAX_TILE):
            if tm_divides % tm or tn_divides % tn:
                continue
            for tk in _divisors(K, 4 * LANES, MM_MAX_TK):
                vmem = 4 * (tm * tk + tk * tn) + 2 * tm * tn * out_bytes + (0 if tk == K else 4 * tm * tn)
                if vmem > MM_VMEM_BUDGET:
                    continue
                traffic = 2 * M * K * (N // tn) + 2 * K * N * (M // tm) + M * N * out_bytes
                cost = traffic + (K // tk - 1) * 8 * M * N + (M // tm) * (N // tn) * (K // tk) * MM_STEP_BYTES
                if best is None or cost < best[0]:
                    best = (cost, tm, tn, tk)
    assert best is not None, (M, N, K)
    return best[1:]


def _mm(a, b, *, mode, out_dtype, name, deps=(), n=None, b_rows=(0, None), rotate=None, out_rows=(0, None), out_prev=None):
    b_first, b_count = b_rows
    if mode == "nn":
        (M, K), N = a.shape, b.shape[1]
    elif mode == "nt":
        (M, K), N = a.shape, (b.shape[0] if b_count is None else b_count)
    else:
        (K, M), N = a.shape, b.shape[1]
    shift, period = rotate or (0, 1)
    out_first, out_total = out_rows[0], (M if out_rows[1] is None else out_rows[1])
    tm, tn, tk = _mm_tiles(M, N, K, jnp.dtype(out_dtype).itemsize,
                           math.gcd(M // period if mode == "tn" else M, out_first),
                           math.gcd(N // period if mode == "nt" else N, b_first if mode == "nt" else 0))
    gm, gn, gk = M // tm, N // tn, K // tk

    def turned(t, tile, size):
        per = size // period // tile
        return ((t // per + shift) % period) * per + t % per if period > 1 else t

    if mode == "nn":
        a_spec = pl.BlockSpec((tm, tk), lambda i, j, k: (i, k))
        b_spec = pl.BlockSpec((tk, tn), lambda i, j, k: (k + b_first // tk, j))
        contract = ((1,), (0,))
    elif mode == "nt":
        a_spec = pl.BlockSpec((tm, tk), lambda i, j, k: (i, k))
        b_spec = pl.BlockSpec((tn, tk), lambda i, j, k: (turned(j, tn, N) + b_first // tn, k))
        contract = ((1,), (1,))
    else:
        a_spec = pl.BlockSpec((tk, tm), lambda i, j, k: (k, i))
        b_spec = pl.BlockSpec((tk, tn), lambda i, j, k: (k, j))
        contract = ((0,), (0,))
    if mode == "tn":
        o_spec = pl.BlockSpec((tm, tn), lambda i, j, k: (turned(i, tm, M) + out_first // tm, j))
    else:
        o_spec = pl.BlockSpec((tm, tn), lambda i, j, k: (i + out_first // tm, j))
    assert b_first % (tk if mode == "nn" else tn) == 0 and out_first % tm == 0, (name, tm, tn, tk)
    n_prev = 0 if out_prev is None else 1

    def body(a_ref, b_ref, *rest):
        o_ref = rest[n_prev]
        part = _dot(a_ref[...].astype(BF16), b_ref[...].astype(BF16), contract)
        if gk == 1:
            o_ref[...] = part.astype(out_dtype)
            return
        acc_ref = rest[n_prev + 1]
        k = pl.program_id(2)

        @pl.when(k == 0)
        def _():
            acc_ref[...] = part

        @pl.when(k > 0)
        def _():
            acc_ref[...] += part

        @pl.when(k == gk - 1)
        def _():
            o_ref[...] = acc_ref[...].astype(out_dtype)

    body, dep_specs, dep_args = _with_deps(body, 2 + n_prev, deps)
    return pl.pallas_call(
        body,
        name=name,
        grid=(gm, gn, gk),
        in_specs=[a_spec, b_spec] + [ANY] * n_prev + dep_specs,
        out_specs=o_spec,
        out_shape=jax.ShapeDtypeStruct((out_total, N), out_dtype),
        input_output_aliases={2: 0} if n_prev else {},
        scratch_shapes=[] if gk == 1 else [pltpu.VMEM((tm, tn), F32)],
        compiler_params=_params(("parallel", "parallel", "arbitrary")),
    )(a, b, *([out_prev] if n_prev else []), *dep_args)


def _mm_rows(a, b, *, mode, fn, out_dtypes, rows=(), vecs=(), reduce=False, name, deps=(), b_rows=(0, None), a_at=None):
    parts = a if a_at is not None else (a,)
    starts = a_at if a_at is not None else (0,)
    n_parts = len(parts)
    M, K = parts[0].shape[0], sum(p.shape[1] for p in parts)
    b_first, b_count = b_rows[0], (b.shape[0] if b_rows[1] is None else b_rows[1])
    N = b.shape[1] if mode == "nn" else b_count
    contract = ((1,), (0,)) if mode == "nn" else ((1,), (1,))
    n_rows, n_vecs, n_out = len(rows), len(vecs), len(out_dtypes)
    out_bytes = sum(jnp.dtype(d).itemsize for d in out_dtypes)
    tm = max(t for t in _divisors(M, LANES, MM_MAX_TILE)
             if 4 * t * K + 4 * K * N + 2 * t * N * (4 * n_rows + out_bytes) <= MM_VMEM_BUDGET)
    assert b_first % b_count == 0 and (a_at is None or mode == "nn")

    def body(*refs):
        a_refs, b_ref, rest = refs[:n_parts], refs[n_parts], refs[n_parts + 1:]
        row_refs, vec_refs = rest[:n_rows], rest[n_rows:n_rows + n_vecs]
        out_refs = rest[n_rows + n_vecs:]
        if a_at is None:
            acc = _dot(a_refs[0][...], b_ref[...], contract)
        else:
            acc = sum(_dot(r[...], b_ref[at:at + r.shape[1], :], contract) for r, at in zip(a_refs, starts))
        res = fn(acc, *[r[...] for r in row_refs], *[v[...] for v in vec_refs])
        for o_ref, val in zip(out_refs[:n_out], res):
            o_ref[...] = val.astype(o_ref.dtype)
        if reduce:
            @pl.when(pl.program_id(0) == 0)
            def _():
                out_refs[n_out][...] = res[n_out]

            @pl.when(pl.program_id(0) > 0)
            def _():
                out_refs[n_out][...] += res[n_out]

    row = pl.BlockSpec((tm, N), lambda i: (i, 0))
    vec = pl.BlockSpec((1, N), lambda i: (0, 0))
    body, dep_specs, dep_args = _with_deps(body, n_parts + 1 + n_rows + n_vecs, deps)
    return pl.pallas_call(
        body, name=name, grid=(M // tm,),
        in_specs=[pl.BlockSpec((tm, p.shape[1]), lambda i: (i, 0)) for p in parts]
        + [pl.BlockSpec((b_count, b.shape[1]), lambda i: (b_first // b_count, 0))]
        + [row] * n_rows + [vec] * n_vecs + dep_specs,
        out_specs=[row] * n_out + [vec] * reduce,
        out_shape=[jax.ShapeDtypeStruct((M, N), d) for d in out_dtypes] + [jax.ShapeDtypeStruct((1, N), F32)] * reduce,
        compiler_params=_params(("arbitrary",)),
    )(*parts, b, *rows, *[v.reshape(1, N) for v in vecs], *dep_args)


def _rms(x, gain):
    return x * lax.rsqrt(jnp.mean(x * x, axis=-1, keepdims=True) + NORM_EPS) * gain


def _residual_then_norm(acc, x, gain):
    x_out = x + acc
    return x_out, _rms(x_out, gain)


def _residual_then_loss(acc, x, target):
    err = (x + acc) - target
    dy = err * (1.0 / D_MODEL)
    return dy, dy, jnp.sum(err * err, axis=0, keepdims=True) * (0.5 / D_MODEL)


def _rms_bwd_rows(dh, x, dres, gain):
    r = lax.rsqrt(jnp.mean(x * x, axis=-1, keepdims=True) + NORM_EPS)
    xh = x * r
    dxh = dh * gain
    dx = dres + r * (dxh - xh * jnp.mean(dxh * xh, axis=-1, keepdims=True))
    return dx, dx, jnp.sum(dh * xh, axis=0, keepdims=True)


def _rms_fwd(x, gain, *, name, tm=512):
    T, D = x.shape

    def body(x_ref, g_ref, h_ref):
        xv = x_ref[...]
        r = lax.rsqrt(jnp.mean(xv * xv, axis=-1, keepdims=True) + NORM_EPS)
        h_ref[...] = (xv * r * g_ref[...]).astype(BF16)

    return pl.pallas_call(
        body, name=name, grid=(T // tm,),
        in_specs=[pl.BlockSpec((tm, D), lambda i: (i, 0)), pl.BlockSpec((1, D), lambda i: (0, 0))],
        out_specs=pl.BlockSpec((tm, D), lambda i: (i, 0)),
        out_shape=jax.ShapeDtypeStruct((T, D), BF16),
        compiler_params=_params(("parallel",)),
    )(x, gain.reshape(1, D))


def _head_norm(x, gain2, lo):
    ms = _half_sums(x * x, lo) * (1.0 / HEAD_DIM)
    r = lax.rsqrt(ms + NORM_EPS)
    xh = x * r
    return xh * gain2, xh, r


def _head_norm_bwd(xh, r, gain2, dy, lo):
    dxh = dy * gain2
    dx = r * (dxh - xh * (_half_sums(dxh * xh, lo) * (1.0 / HEAD_DIM)))
    return dx, dy * xh


Q_GROUP = N_Q_HEADS // 2
GROUP_ROWS = Q_GROUP * BLOCK
ATT_SCRATCH = (pltpu.VMEM((2, 2, GROUP_ROWS, BLOCK), F32), pltpu.VMEM((2, GROUP_ROWS, 1), F32))


def _att_consts(sink_ref, bias_ref, sinkcol_ref):
    row = lax.broadcasted_iota(jnp.int32, (GROUP_ROWS, BLOCK), 0)
    kj = lax.broadcasted_iota(jnp.int32, (GROUP_ROWS, BLOCK), 1)
    head = row // BLOCK
    head_col = lax.broadcasted_iota(jnp.int32, (GROUP_ROWS, 1), 0) // BLOCK
    d_cur = (row % BLOCK) - kj
    d_prev = d_cur + BLOCK
    for kv in range(2):
        slope = jnp.zeros((GROUP_ROWS, BLOCK), F32)
        sink = jnp.zeros((GROUP_ROWS, 1), F32)
        for r in range(Q_GROUP):
            slope = jnp.where(head == r, ALIBI_SLOPES[Q_GROUP * kv + r], slope)
            sink = jnp.where(head_col == r, sink_ref[Q_GROUP * kv + r], sink)
        bias_ref[kv, 0] = jnp.where(d_cur >= 0, -slope * d_cur.astype(F32), NEG_INF)
        bias_ref[kv, 1] = jnp.where(d_prev < BLOCK, -slope * d_prev.astype(F32), NEG_INF)
        sinkcol_ref[kv] = sink


def _stack_heads(t0, t1, lo):
    z = jnp.zeros_like(t0)
    return jnp.concatenate([jnp.where(lo, t0, z), jnp.where(lo, z, t0), jnp.where(lo, t1, z), jnp.where(lo, z, t1)], axis=0)


def _unstack_heads(x4, lo):
    return (jnp.where(lo, x4[0:BLOCK], x4[BLOCK:2 * BLOCK]), jnp.where(lo, x4[2 * BLOCK:3 * BLOCK], x4[3 * BLOCK:]))


def _att_probs(q4, k2c, k2p, bias_c, bias_p, sink, has_prev):
    s_c = _dot_nt(q4, k2c) * ATT_SCALE + bias_c
    s_p = jnp.where(has_prev, _dot_nt(q4, k2p) * ATT_SCALE + bias_p, NEG_INF)
    m = jnp.maximum(jnp.max(jnp.maximum(s_c, s_p), axis=-1, keepdims=True), sink)
    e_c = jnp.exp(s_c - m)
    e_p = jnp.exp(s_p - m)
    e_s = jnp.exp(sink - m)
    inv = 1.0 / (jnp.sum(e_c + e_p, axis=-1, keepdims=True) + e_s)
    return e_c * inv, e_p * inv, e_s * inv


def _attention_fwd(proj, q_gain, k_gain, sinks, *, n_seq, seq, name):
    T = n_seq * seq
    nb = seq // BLOCK
    qcol, kvcol = COL_QKV // ATT_WIDTH, (COL_QKV + ATT_WIDTH) // (2 * KV_WIDTH)

    def body(q_ref, kv_ref, qg_ref, kg_ref, sink_ref, y_ref, bias_ref, sinkcol_ref):
        lo = _lo_mask((BLOCK, LANES))
        qg, kg = qg_ref[...], kg_ref[...]
        _att_consts(sink_ref, bias_ref, sinkcol_ref)

        def block(i, carry):
            r0 = pl.multiple_of(i * BLOCK, BLOCK)
            rp = pl.multiple_of(jnp.maximum(i - 1, 0) * BLOCK, BLOCK)
            has_prev = i > 0
            kn_c = _head_norm(kv_ref[pl.ds(r0, BLOCK), 0:KV_WIDTH].astype(F32), kg, lo)[0].astype(BF16)
            kn_p = _head_norm(kv_ref[pl.ds(rp, BLOCK), 0:KV_WIDTH].astype(F32), kg, lo)[0].astype(BF16)
            v_c = kv_ref[pl.ds(r0, BLOCK), KV_WIDTH:2 * KV_WIDTH].astype(BF16)
            v_p = kv_ref[pl.ds(rp, BLOCK), KV_WIDTH:2 * KV_WIDTH].astype(BF16)
            for kv in range(2):
                k2c, k2p = _dup_half(kn_c, kv, lo), _dup_half(kn_p, kv, lo)
                v2c, v2p = _dup_half(v_c, kv, lo), _dup_half(v_p, kv, lo)
                cols = [slice((2 * kv + t) * LANES, (2 * kv + t + 1) * LANES) for t in range(2)]
                qn = [_head_norm(q_ref[pl.ds(r0, BLOCK), c].astype(F32), qg, lo)[0] for c in cols]
                q4 = _stack_heads(qn[0], qn[1], lo).astype(BF16)
                p_c, p_p, _ = _att_probs(q4, k2c, k2p, bias_ref[kv, 0], bias_ref[kv, 1], sinkcol_ref[kv], has_prev)
                o4 = _dot_nn(p_c.astype(BF16), v2c) + _dot_nn(p_p.astype(BF16), v2p)
                for c, out in zip(cols, _unstack_heads(o4, lo)):
                    y_ref[pl.ds(r0, BLOCK), c] = out.astype(BF16)
            return carry

        lax.fori_loop(0, nb, block, 0)

    vec = pl.BlockSpec((1, LANES), lambda b: (0, 0))
    return pl.pallas_call(
        body, name=name, grid=(n_seq,),
        in_specs=[pl.BlockSpec((seq, ATT_WIDTH), lambda b: (b, qcol)),
                  pl.BlockSpec((seq, 2 * KV_WIDTH), lambda b: (b, kvcol)),
                  vec, vec, pl.BlockSpec(memory_space=pltpu.SMEM)],
        out_specs=pl.BlockSpec((seq, ATT_WIDTH), lambda b: (b, 0)),
        out_shape=jax.ShapeDtypeStruct((T, ATT_WIDTH), BF16),
        scratch_shapes=list(ATT_SCRATCH),
        compiler_params=_params(("parallel",)),
    )(proj, proj, jnp.tile(q_gain, 2).reshape(1, LANES), jnp.tile(k_gain, 2).reshape(1, LANES), sinks)


def _attention_bwd(proj, dy, q_gain, k_gain, sinks, *, n_seq, seq, name, deps=()):
    T = n_seq * seq
    nb = seq // BLOCK
    qcol, kvcol = COL_QKV // ATT_WIDTH, (COL_QKV + ATT_WIDTH) // (2 * KV_WIDTH)

    def body(q_ref, kv_ref, dy_ref, qg_ref, kg_ref, sink_ref, dqkv_ref, dqg_ref, dkg_ref, dsink_ref,
             dkn_acc, dv_acc, qg_acc, kg_acc, sink_acc, bias_ref, sinkcol_ref):
        lo = _lo_mask((BLOCK, LANES))
        qg, kg = qg_ref[...], kg_ref[...]
        _att_consts(sink_ref, bias_ref, sinkcol_ref)
        first = pl.program_id(0) == 0

        @pl.when(first)
        def _():
            qg_acc[...] = jnp.zeros_like(qg_acc)
            kg_acc[...] = jnp.zeros_like(kg_acc)
            sink_acc[...] = jnp.zeros_like(sink_acc)

        dkn_acc[...] = jnp.zeros_like(dkn_acc)
        dv_acc[...] = jnp.zeros_like(dv_acc)

        def block(i, carry):
            r0 = pl.multiple_of(i * BLOCK, BLOCK)
            rp = pl.multiple_of(jnp.maximum(i - 1, 0) * BLOCK, BLOCK)
            has_prev = i > 0
            kn_c = _head_norm(kv_ref[pl.ds(r0, BLOCK), 0:KV_WIDTH].astype(F32), kg, lo)[0].astype(BF16)
            kn_p = _head_norm(kv_ref[pl.ds(rp, BLOCK), 0:KV_WIDTH].astype(F32), kg, lo)[0].astype(BF16)
            v_c = kv_ref[pl.ds(r0, BLOCK), KV_WIDTH:2 * KV_WIDTH].astype(BF16)
            v_p = kv_ref[pl.ds(rp, BLOCK), KV_WIDTH:2 * KV_WIDTH].astype(BF16)
            dk_c, dk_p, dv_c, dv_p = [], [], [], []
            for kv in range(2):
                k2c, k2p = _dup_half(kn_c, kv, lo), _dup_half(kn_p, kv, lo)
                v2c, v2p = _dup_half(v_c, kv, lo), _dup_half(v_p, kv, lo)
                cols = [slice((2 * kv + t) * LANES, (2 * kv + t + 1) * LANES) for t in range(2)]
                normed = [_head_norm(q_ref[pl.ds(r0, BLOCK), c].astype(F32), qg, lo) for c in cols]
                q4 = _stack_heads(normed[0][0], normed[1][0], lo).astype(BF16)
                do4 = _stack_heads(dy_ref[pl.ds(r0, BLOCK), cols[0]], dy_ref[pl.ds(r0, BLOCK), cols[1]], lo)
                p_c, p_p, p_s = _att_probs(q4, k2c, k2p, bias_ref[kv, 0], bias_ref[kv, 1], sinkcol_ref[kv], has_prev)
                dp_c = _dot_nt(do4, v2c)
                dp_p = _dot_nt(do4, v2p)
                delta = jnp.sum(p_c * dp_c + p_p * dp_p, axis=-1, keepdims=True)
                ds_c = (p_c * (dp_c - delta)).astype(BF16)
                ds_p = (p_p * (dp_p - delta)).astype(BF16)
                sink_acc[kv] += -(p_s * delta)
                dq4 = (_dot_nn(ds_c, k2c) + _dot_nn(ds_p, k2p)) * ATT_SCALE
                for c, (_, qh, qr), dqn in zip(cols, normed, _unstack_heads(dq4, lo)):
                    dq, dg = _head_norm_bwd(qh, qr, qg, dqn, lo)
                    dqkv_ref[pl.ds(r0, BLOCK), c] = dq.astype(BF16)
                    qg_acc[...] += dg
                dk_c.append(_dot_tn(ds_c, q4))
                dk_p.append(_dot_tn(ds_p, q4))
                dv_c.append(_dot_tn(p_c.astype(BF16), do4))
                dv_p.append(_dot_tn(p_p.astype(BF16), do4))

            def fold(parts):
                a = parts[0] + pltpu.roll(parts[0], LANES // 2, axis=1)
                b = parts[1] + pltpu.roll(parts[1], LANES // 2, axis=1)
                return jnp.where(lo, a, b)

            dkn_acc[pl.ds(r0, BLOCK), :] += fold(dk_c) * ATT_SCALE
            dkn_acc[pl.ds(rp, BLOCK), :] += fold(dk_p) * ATT_SCALE
            dv_acc[pl.ds(r0, BLOCK), :] += fold(dv_c)
            dv_acc[pl.ds(rp, BLOCK), :] += fold(dv_p)
            return carry

        lax.fori_loop(0, nb, block, 0)

        def finish(i, carry):
            r0 = pl.multiple_of(i * BLOCK, BLOCK)
            _, kh, kr = _head_norm(kv_ref[pl.ds(r0, BLOCK), 0:KV_WIDTH].astype(F32), kg, lo)
            dk, dg = _head_norm_bwd(kh, kr, kg, dkn_acc[pl.ds(r0, BLOCK), :], lo)
            dqkv_ref[pl.ds(r0, BLOCK), ATT_WIDTH:ATT_WIDTH + KV_WIDTH] = dk.astype(BF16)
            dqkv_ref[pl.ds(r0, BLOCK), ATT_WIDTH + KV_WIDTH:QKV_WIDTH] = dv_acc[pl.ds(r0, BLOCK), :].astype(BF16)
            kg_acc[...] += dg
            return carry

        lax.fori_loop(0, nb, finish, 0)

        @pl.when(pl.program_id(0) == n_seq - 1)
        def _():
            dqg_ref[...] = jnp.sum(qg_acc[...], axis=0, keepdims=True)
            dkg_ref[...] = jnp.sum(kg_acc[...], axis=0, keepdims=True)
            lane = lax.broadcasted_iota(jnp.int32, (1, LANES), 1)
            dsink = jnp.zeros((1, LANES), F32)
            for kv in range(2):
                for r in range(Q_GROUP):
                    total = jnp.sum(sink_acc[kv, r * BLOCK:(r + 1) * BLOCK, :], axis=0, keepdims=True)
                    dsink = jnp.where(lane == Q_GROUP * kv + r, total, dsink)
            dsink_ref[...] = dsink

    vec = pl.BlockSpec((1, LANES), lambda b: (0, 0))
    acc = pltpu.VMEM((BLOCK, LANES), F32)
    body, dep_specs, dep_args = _with_deps(body, 6, deps)
    dqkv, dqg, dkg, dsink = pl.pallas_call(
        body, name=name, grid=(n_seq,),
        in_specs=[pl.BlockSpec((seq, ATT_WIDTH), lambda b: (b, qcol)),
                  pl.BlockSpec((seq, 2 * KV_WIDTH), lambda b: (b, kvcol)),
                  pl.BlockSpec((seq, ATT_WIDTH), lambda b: (b, 0)),
                  vec, vec, pl.BlockSpec(memory_space=pltpu.SMEM)] + dep_specs,
        out_specs=[pl.BlockSpec((seq, QKV_WIDTH), lambda b: (b, 0)), vec, vec, vec],
        out_shape=[jax.ShapeDtypeStruct((T, QKV_WIDTH), BF16)] + [jax.ShapeDtypeStruct((1, LANES), F32)] * 3,
        scratch_shapes=[pltpu.VMEM((seq, KV_WIDTH), F32), pltpu.VMEM((seq, KV_WIDTH), F32), acc, acc,
                        pltpu.VMEM((2, GROUP_ROWS, 1), F32), *ATT_SCRATCH],
        compiler_params=_params(("arbitrary",)),
    )(proj, proj, dy, jnp.tile(q_gain, 2).reshape(1, LANES), jnp.tile(k_gain, 2).reshape(1, LANES), sinks, *dep_args)
    half = LANES // 2
    return dqkv, dqg[0, :half] + dqg[0, half:], dkg[0, :half] + dkg[0, half:], dsink[0, :N_Q_HEADS]


def _sgu_weights(w_ref):
    r = lax.broadcasted_iota(jnp.int32, (BLOCK, BLOCK), 0)
    c = lax.broadcasted_iota(jnp.int32, (BLOCK, BLOCK), 1)
    return [jnp.where(r >= c, w_ref[g], 0.0).astype(BF16) for g in range(SGU_GROUPS)]


def _sgu_fwd(proj, gain, w_s, bias_full, *, n_seq, seq, name):
    T = n_seq * seq
    nc = seq // BLOCK

    def body(suv_ref, g_ref, w_ref, b_ref, y_ref):
        lo = _lo_mask((BLOCK, LANES))
        wm = _sgu_weights(w_ref)
        gain_v = g_ref[...]

        def chunk(c, carry):
            r0 = pl.multiple_of(c * BLOCK, BLOCK)
            gv = _gelu(suv_ref[pl.ds(r0, BLOCK), SGU_WIDTH:2 * SGU_WIDTH].astype(F32))
            r = lax.rsqrt(jnp.mean(gv * gv, axis=-1, keepdims=True) + NORM_EPS)
            vn = (gv * r * gain_v).astype(BF16)
            for p in range(SGU_WIDTH // LANES):
                cols = slice(p * LANES, (p + 1) * LANES)
                vp = vn[:, cols]
                mixed = jnp.where(lo, _dot_nn(wm[2 * p], vp), _dot_nn(wm[2 * p + 1], vp)) + b_ref[:, cols]
                u = _gelu(suv_ref[pl.ds(r0, BLOCK), cols].astype(F32))
                y_ref[pl.ds(r0, BLOCK), cols] = (u * mixed).astype(BF16)
            return carry

        lax.fori_loop(0, nc, chunk, 0)

    return pl.pallas_call(
        body, name=name, grid=(n_seq,),
        in_specs=[pl.BlockSpec((seq, 2 * SGU_WIDTH), lambda b: (b, COL_SUV // (2 * SGU_WIDTH))),
                  pl.BlockSpec((1, SGU_WIDTH), lambda b: (0, 0)),
                  pl.BlockSpec((SGU_GROUPS, BLOCK, BLOCK), lambda b: (0, 0, 0)),
                  pl.BlockSpec((BLOCK, SGU_WIDTH), lambda b: (0, 0))],
        out_specs=pl.BlockSpec((seq, SGU_WIDTH), lambda b: (b, 0)),
        out_shape=jax.ShapeDtypeStruct((T, SGU_WIDTH), BF16),
        compiler_params=_params(("parallel",)),
    )(proj, gain.reshape(1, SGU_WIDTH), w_s, bias_full)


def _sgu_bwd(proj, dy, gain, w_s, bias_full, *, n_seq, seq, name, deps=()):
    T = n_seq * seq
    nc = seq // BLOCK
    n_tiles = SGU_WIDTH // LANES

    def body(suv_ref, dy_ref, g_ref, w_ref, b_ref, dsuv_ref, dg_ref, dw_ref, db_ref, dg_acc, dw_acc, db_acc):
        lo = _lo_mask((BLOCK, LANES))
        hi = jnp.logical_not(lo)
        wm = _sgu_weights(w_ref)
        wmt = [jnp.where(lax.broadcasted_iota(jnp.int32, (BLOCK, BLOCK), 1) >= lax.broadcasted_iota(jnp.int32, (BLOCK, BLOCK), 0),
                         w_ref[g].T, 0.0).astype(BF16) for g in range(SGU_GROUPS)]
        gain_v = g_ref[...]

        @pl.when(pl.program_id(0) == 0)
        def _():
            dg_acc[...] = jnp.zeros_like(dg_acc)
            dw_acc[...] = jnp.zeros_like(dw_acc)
            db_acc[...] = jnp.zeros_like(db_acc)

        def chunk(c, carry):
            r0 = pl.multiple_of(c * BLOCK, BLOCK)
            gv, dgelu_v = _gelu_and_grad(suv_ref[pl.ds(r0, BLOCK), SGU_WIDTH:2 * SGU_WIDTH].astype(F32))
            r = lax.rsqrt(jnp.mean(gv * gv, axis=-1, keepdims=True) + NORM_EPS)
            vh = gv * r
            vn = (vh * gain_v).astype(BF16)
            dvn_tiles = []
            for p in range(n_tiles):
                cols = slice(p * LANES, (p + 1) * LANES)
                vp = vn[:, cols]
                mixed = jnp.where(lo, _dot_nn(wm[2 * p], vp), _dot_nn(wm[2 * p + 1], vp)) + b_ref[:, cols]
                u, dgelu_u = _gelu_and_grad(suv_ref[pl.ds(r0, BLOCK), cols].astype(F32))
                dyv = dy_ref[pl.ds(r0, BLOCK), cols]
                dsuv_ref[pl.ds(r0, BLOCK), cols] = (dyv * mixed * dgelu_u).astype(BF16)
                dm = dyv * u
                db_acc[:, cols] += dm
                dm_bf = dm.astype(BF16)
                dvn_tiles.append(jnp.where(lo, _dot_nn(wmt[2 * p], dm_bf), _dot_nn(wmt[2 * p + 1], dm_bf)))
                dw_acc[2 * p] += _dot_nt(jnp.where(lo, dm, 0.0).astype(BF16), vp)
                dw_acc[2 * p + 1] += _dot_nt(jnp.where(hi, dm, 0.0).astype(BF16), vp)
            dvn = jnp.concatenate(dvn_tiles, axis=1)
            dg_acc[...] += dvn * vh
            dvh = dvn * gain_v
            dgv = r * (dvh - vh * jnp.mean(dvh * vh, axis=-1, keepdims=True))
            dsuv_ref[pl.ds(r0, BLOCK), SGU_WIDTH:2 * SGU_WIDTH] = (dgv * dgelu_v).astype(BF16)
            return carry

        lax.fori_loop(0, nc, chunk, 0)

        @pl.when(pl.program_id(0) == n_seq - 1)
        def _():
            dg_ref[...] = jnp.sum(dg_acc[...], axis=0, keepdims=True)
            r = lax.broadcasted_iota(jnp.int32, (BLOCK, BLOCK), 0)
            c = lax.broadcasted_iota(jnp.int32, (BLOCK, BLOCK), 1)
            for g in range(SGU_GROUPS):
                dw_ref[g] = jnp.where(r >= c, dw_acc[g], 0.0)
            lane = lax.broadcasted_iota(jnp.int32, (BLOCK, LANES), 1)
            out = jnp.zeros((BLOCK, LANES), F32)
            for p in range(n_tiles):
                tile = db_acc[:, p * LANES:(p + 1) * LANES]
                s_lo = jnp.sum(jnp.where(lo, tile, 0.0), axis=-1, keepdims=True)
                s_hi = jnp.sum(jnp.where(hi, tile, 0.0), axis=-1, keepdims=True)
                out = jnp.where(lane == 2 * p, s_lo, out)
                out = jnp.where(lane == 2 * p + 1, s_hi, out)
            db_ref[...] = out

    body, dep_specs, dep_args = _with_deps(body, 5, deps)
    dsuv, dg, dw, db = pl.pallas_call(
        body, name=name, grid=(n_seq,),
        in_specs=[pl.BlockSpec((seq, 2 * SGU_WIDTH), lambda b: (b, COL_SUV // (2 * SGU_WIDTH))),
                  pl.BlockSpec((seq, SGU_WIDTH), lambda b: (b, 0)),
                  pl.BlockSpec((1, SGU_WIDTH), lambda b: (0, 0)),
                  pl.BlockSpec((SGU_GROUPS, BLOCK, BLOCK), lambda b: (0, 0, 0)),
                  pl.BlockSpec((BLOCK, SGU_WIDTH), lambda b: (0, 0))] + dep_specs,
        out_specs=[pl.BlockSpec((seq, 2 * SGU_WIDTH), lambda b: (b, 0)),
                   pl.BlockSpec((1, SGU_WIDTH), lambda b: (0, 0)),
                   pl.BlockSpec((SGU_GROUPS, BLOCK, BLOCK), lambda b: (0, 0, 0)),
                   pl.BlockSpec((BLOCK, LANES), lambda b: (0, 0))],
        out_shape=[jax.ShapeDtypeStruct((T, 2 * SGU_WIDTH), BF16), jax.ShapeDtypeStruct((1, SGU_WIDTH), F32),
                   jax.ShapeDtypeStruct((SGU_GROUPS, BLOCK, BLOCK), F32), jax.ShapeDtypeStruct((BLOCK, LANES), F32)],
        scratch_shapes=[pltpu.VMEM((BLOCK, SGU_WIDTH), F32), pltpu.VMEM((SGU_GROUPS, BLOCK, BLOCK), F32),
                        pltpu.VMEM((BLOCK, SGU_WIDTH), F32)],
        compiler_params=_params(("arbitrary",)),
    )(proj, dy, gain.reshape(1, SGU_WIDTH), w_s, bias_full, *dep_args)
    return dsuv, dg.reshape(SGU_WIDTH), dw, db[:, :SGU_GROUPS].T


def _merge_fwd(y_att, y_sgu, w_oa, w_ob, proj, *, name, tm=1024, tn=512, deps=()):
    T = y_att.shape[0]

    def body(ya_ref, ys_ref, wa_ref, wb_ref, ga_ref, gb_ref, o_ref):
        pa = _dot_nn(ya_ref[...], wa_ref[...])
        pb = _dot_nn(ys_ref[...], wb_ref[...])
        o_ref[...] = (_sigmoid(ga_ref[...].astype(F32)) * pa + _sigmoid(gb_ref[...].astype(F32)) * pb).astype(BF16)

    act = pl.BlockSpec((tm, ATT_WIDTH), lambda i, j: (i, 0))
    wgt = pl.BlockSpec((ATT_WIDTH, tn), lambda i, j: (0, j))
    body, dep_specs, dep_args = _with_deps(body, 6, deps)
    return pl.pallas_call(
        body, name=name, grid=(T // tm, D_MODEL // tn),
        in_specs=[act, act, wgt, wgt,
                  pl.BlockSpec((tm, tn), lambda i, j: (i, j + COL_GA // tn)),
                  pl.BlockSpec((tm, tn), lambda i, j: (i, j + COL_GB // tn))] + dep_specs,
        out_specs=pl.BlockSpec((tm, tn), lambda i, j: (i, j)),
        out_shape=jax.ShapeDtypeStruct((T, D_MODEL), BF16),
        compiler_params=_params(("parallel", "parallel")),
    )(y_att, y_sgu, w_oa, w_ob, proj, proj, *dep_args)


def _merge_bwd(dx1_bf, w_out, y_att, y_sgu, w_oa, w_ob, proj, *, name, tm=1024, tn=512):
    T = y_att.shape[0]

    def body(dx_ref, wo_ref, ya_ref, ys_ref, wa_ref, wb_ref, ga_ref, gb_ref, dpa_ref, dpb_ref, dga_ref, dgb_ref):
        dm = _dot_nt(dx_ref[...], wo_ref[...])
        pa = _dot_nn(ya_ref[...], wa_ref[...])
        pb = _dot_nn(ys_ref[...], wb_ref[...])
        sa = _sigmoid(ga_ref[...].astype(F32))
        sb = _sigmoid(gb_ref[...].astype(F32))
        dpa_ref[...] = (dm * sa).astype(BF16)
        dpb_ref[...] = (dm * sb).astype(BF16)
        dga_ref[...] = (dm * pa * sa * (1.0 - sa)).astype(BF16)
        dgb_ref[...] = (dm * pb * sb * (1.0 - sb)).astype(BF16)

    act = pl.BlockSpec((tm, ATT_WIDTH), lambda i, j: (i, 0))
    wgt = pl.BlockSpec((ATT_WIDTH, tn), lambda i, j: (0, j))
    out = pl.BlockSpec((tm, tn), lambda i, j: (i, j))
    return pl.pallas_call(
        body, name=name, grid=(T // tm, D_MODEL // tn),
        in_specs=[pl.BlockSpec((tm, D_MODEL), lambda i, j: (i, 0)),
                  pl.BlockSpec((tn, D_MODEL), lambda i, j: (j, 0)),
                  act, act, wgt, wgt,
                  pl.BlockSpec((tm, tn), lambda i, j: (i, j + COL_GA // tn)),
                  pl.BlockSpec((tm, tn), lambda i, j: (i, j + COL_GB // tn))],
        out_specs=[out] * 4,
        out_shape=[jax.ShapeDtypeStruct((T, D_MODEL), BF16)] * 4,
        compiler_params=_params(("parallel", "parallel")),
    )(dx1_bf, w_out, y_att, y_sgu, w_oa, w_ob, proj, proj)


CONV_ROWS = 256
CONV_TN = 256


def _shift_rows(cur, prev8, k):
    rolled = pltpu.roll(cur, k, axis=0)
    head = jnp.where(lax.broadcasted_iota(jnp.int32, prev8.shape, 0) < k, pltpu.roll(prev8, k, axis=0), rolled[:SUBLANES])
    return jnp.concatenate([head, rolled[SUBLANES:]], axis=0)


def _shift_rows_up(cur, next8, k):
    n = cur.shape[0]
    rolled = pltpu.roll(cur, n - k, axis=0)
    tail = jnp.where(lax.broadcasted_iota(jnp.int32, next8.shape, 0) >= SUBLANES - k,
                     pltpu.roll(next8, SUBLANES - k, axis=0), rolled[n - SUBLANES:])
    return jnp.concatenate([rolled[:n - SUBLANES], tail], axis=0)


HALO_ROWS = 16


def _rows_before(z_ref, r0, first):
    rp = pl.multiple_of(jnp.maximum(r0 - HALO_ROWS, 0), HALO_ROWS)
    halo = z_ref[pl.ds(rp, HALO_ROWS), :].astype(F32)
    return jnp.where(first, 0.0, halo[HALO_ROWS - SUBLANES:])


def _conv_rows(z_ref, r0, first, w_ref, b_ref, rows):
    cur = z_ref[pl.ds(r0, rows), :].astype(F32)
    prev8 = _rows_before(z_ref, r0, first)
    z1 = _shift_rows(cur, prev8, 1)
    z2 = _shift_rows(cur, prev8, 2)
    return b_ref[...] + w_ref[0:1, :] * z2 + w_ref[1:2, :] * z1 + w_ref[2:3, :] * cur


def _conv_fwd(z_g, z_v, cw_g, cw_v, cb_g, cb_v, *, n_seq, seq, name):
    T = n_seq * seq
    tn, rows = CONV_TN, CONV_ROWS

    def body(zg_ref, zv_ref, wg_ref, wv_ref, bg_ref, bv_ref, a_ref, cg_ref, cv_ref):
        def step(s, carry):
            r0 = pl.multiple_of(s * rows, rows)
            first = s == 0
            g = _conv_rows(zg_ref, r0, first, wg_ref, bg_ref, rows)
            v = _conv_rows(zv_ref, r0, first, wv_ref, bv_ref, rows)
            a_ref[pl.ds(r0, rows), :] = (g * _sigmoid(g) * v).astype(BF16)
            cg_ref[pl.ds(r0, rows), :] = g.astype(ACT_DTYPE)
            cv_ref[pl.ds(r0, rows), :] = v.astype(ACT_DTYPE)
            return carry

        lax.fori_loop(0, seq // rows, step, 0)

    zs = pl.BlockSpec((seq, tn), lambda b, j: (b, j))
    ws = pl.BlockSpec((3, tn), lambda b, j: (0, j))
    bs = pl.BlockSpec((1, tn), lambda b, j: (0, j))
    return pl.pallas_call(
        body, name=name, grid=(n_seq, D_FF // tn),
        in_specs=[zs, zs, ws, ws, bs, bs], out_specs=[zs] * 3,
        out_shape=[jax.ShapeDtypeStruct((T, D_FF), BF16)] + [jax.ShapeDtypeStruct((T, D_FF), ACT_DTYPE)] * 2,
        compiler_params=_params(("parallel", "parallel")),
    )(z_g, z_v, cw_g, cw_v, cb_g.reshape(1, D_FF), cb_v.reshape(1, D_FF))


def _conv_bwd(z_g, z_v, c_g, c_v, da, cw_g, cw_v, *, n_seq, seq, name):
    T = n_seq * seq
    tn, rows = CONV_TN, CONV_ROWS
    n_steps = seq // rows

    def body(zg_ref, zv_ref, cg_ref, cv_ref, da_ref, wg_ref, wv_ref,
             dzg_ref, dzv_ref, dwg_ref, dwv_ref, dbg_ref, dbv_ref, dcg_ref, dcv_ref):
        def colsum(x):
            return jnp.sum(x, axis=0, keepdims=True)

        def grads(s, accs):
            r0 = pl.multiple_of(s * rows, rows)
            g = cg_ref[pl.ds(r0, rows), :].astype(F32)
            v = cv_ref[pl.ds(r0, rows), :].astype(F32)
            sg = _sigmoid(g)
            dav = da_ref[pl.ds(r0, rows), :].astype(F32)
            dcg = dav * v * (sg * (1.0 + g * (1.0 - sg)))
            dcv = dav * (g * sg)
            dcg_ref[pl.ds(r0, rows), :] = dcg
            dcv_ref[pl.ds(r0, rows), :] = dcv
            return accs[0] + colsum(dcg), accs[1] + colsum(dcv)

        zero = jnp.zeros((1, tn), F32)
        db = lax.fori_loop(0, n_steps, grads, (zero, zero))

        def back(s, accs):
            r0 = pl.multiple_of(s * rows, rows)
            last = s == n_steps - 1
            rn = pl.multiple_of(jnp.minimum(r0 + rows, seq - SUBLANES), SUBLANES)
            new = []
            for half, (dc_ref, w_ref, dz_ref, z_ref) in enumerate(((dcg_ref, wg_ref, dzg_ref, zg_ref),
                                                                   (dcv_ref, wv_ref, dzv_ref, zv_ref))):
                cur = dc_ref[pl.ds(r0, rows), :]
                nxt = jnp.where(last, 0.0, dc_ref[pl.ds(rn, SUBLANES), :])
                u1, u2 = _shift_rows_up(cur, nxt, 1), _shift_rows_up(cur, nxt, 2)
                dz_ref[pl.ds(r0, rows), :] = (w_ref[2:3, :] * cur + w_ref[1:2, :] * u1 + w_ref[0:1, :] * u2).astype(BF16)
                z = z_ref[pl.ds(r0, rows), :].astype(F32)
                new += [accs[3 * half] + colsum(u2 * z), accs[3 * half + 1] + colsum(u1 * z),
                        accs[3 * half + 2] + colsum(cur * z)]
            return tuple(new)

        dw = lax.fori_loop(0, n_steps, back, (zero,) * 6)
        first_seq = pl.program_id(1) == 0

        @pl.when(first_seq)
        def _():
            dwg_ref[...] = jnp.concatenate(dw[0:3], axis=0)
            dwv_ref[...] = jnp.concatenate(dw[3:6], axis=0)
            dbg_ref[...], dbv_ref[...] = db

        @pl.when(jnp.logical_not(first_seq))
        def _():
            dwg_ref[...] += jnp.concatenate(dw[0:3], axis=0)
            dwv_ref[...] += jnp.concatenate(dw[3:6], axis=0)
            dbg_ref[...] += db[0]
            dbv_ref[...] += db[1]

    zs = pl.BlockSpec((seq, tn), lambda j, b: (b, j))
    ws = pl.BlockSpec((3, tn), lambda j, b: (0, j))
    bs = pl.BlockSpec((1, tn), lambda j, b: (0, j))
    outs = pl.pallas_call(
        body, name=name, grid=(D_FF // tn, n_seq),
        in_specs=[zs] * 5 + [ws, ws],
        out_specs=[zs, zs, ws, ws, bs, bs],
        out_shape=[jax.ShapeDtypeStruct((T, D_FF), BF16)] * 2 + [jax.ShapeDtypeStruct((3, D_FF), F32)] * 2
        + [jax.ShapeDtypeStruct((1, D_FF), F32)] * 2,
        scratch_shapes=[pltpu.VMEM((seq, tn), F32), pltpu.VMEM((seq, tn), F32)],
        compiler_params=_params(("parallel", "arbitrary")),
    )(z_g, z_v, c_g, c_v, da, cw_g, cw_v)
    dz_g, dz_v, dw_g, dw_v, db_g, db_v = outs
    return dz_g, dz_v, dw_g, dw_v, db_g.reshape(D_FF), db_v.reshape(D_FF)


def _layer_fwd(x, h, w, sched, tail, *, n_seq, seq, l):
    tag = f"l{l}"
    deps = sched("fwd_start", l, x)
    proj = _mm(h, w["w_in_t"], mode="nt", out_dtype=ACT_DTYPE, rotate=W_IN_ROTATE, name=f"{tag}_proj", deps=deps)
    y_att = _attention_fwd(proj, w["q_norm"], w["k_norm"], w["sinks"], n_seq=n_seq, seq=seq, name=f"{tag}_att")
    deps = sched("fwd_att", l, y_att)
    y_sgu = _sgu_fwd(proj, w["sgu_norm"], w["w_s"], w["bias_full"], n_seq=n_seq, seq=seq, name=f"{tag}_sgu")
    merged = _merge_fwd(y_att, y_sgu, w["w_oa"], w["w_ob"], proj, name=f"{tag}_merge", deps=deps)
    x1, h2 = _mm_rows(merged, w["w_out"], mode="nn", fn=_residual_then_norm, out_dtypes=(F32, BF16), rows=(x,),
                      vecs=(w["ffn_norm"],), name=f"{tag}_out")
    deps = sched("fwd_mixer_done", l, x1)
    z_g = _mm(h2, w["w_up_t"], mode="nt", out_dtype=ACT_DTYPE, b_rows=(0, D_FF), name=f"{tag}_up_g", deps=deps)
    z_v = _mm(h2, w["w_up_t"], mode="nt", out_dtype=ACT_DTYPE, b_rows=(D_FF, D_FF), name=f"{tag}_up_v")
    a, c_g, c_v = _conv_fwd(z_g, z_v, w["cw_g"], w["cw_v"], w["cb_g"], w["cb_v"], n_seq=n_seq, seq=seq,
                            name=f"{tag}_conv")
    deps = sched("fwd_conv", l, a)
    if tail[0] == "norm":
        out = _mm_rows(a, w["w_down"], mode="nn", fn=_residual_then_norm, out_dtypes=(F32, BF16), rows=(x1,),
                       vecs=(tail[1],), name=f"{tag}_down", deps=deps)
    else:
        out = _mm_rows(a, w["w_down"], mode="nn", fn=_residual_then_loss, out_dtypes=(F32, BF16), rows=(x1, tail[1]),
                       reduce=True, name=f"{tag}_down", deps=deps)
    saved = dict(x=x, h=h, proj=proj, y_att=y_att, y_sgu=y_sgu, merged=merged, x1=x1, h2=h2, z_g=z_g, z_v=z_v,
                 c_g=c_g, c_v=c_v, a=a)
    return out, saved


def _layer_bwd(dx2, dx2_bf, w, s, sched, deps, *, n_seq, seq, l):
    tag = f"l{l}b"
    g = {}
    da = _mm(dx2_bf, w["w_down"], mode="nt", out_dtype=ACT_DTYPE, name=f"{tag}_da", deps=deps)
    g["w_down"] = _mm(s["a"], dx2_bf, mode="tn", out_dtype=F32, name=f"{tag}_dw_down")
    dz_g, dz_v, g["cw_g"], g["cw_v"], g["cb_g"], g["cb_v"] = _conv_bwd(
        s["z_g"], s["z_v"], s["c_g"], s["c_v"], da, w["cw_g"], w["cw_v"], n_seq=n_seq, seq=seq, name=f"{tag}_conv")
    dw_up_t = _mm(dz_g, s["h2"], mode="tn", out_dtype=F32, out_rows=(0, 2 * D_FF), name=f"{tag}_dw_up_g")
    g["w_up_t"] = _mm(dz_v, s["h2"], mode="tn", out_dtype=F32, out_rows=(D_FF, 2 * D_FF), out_prev=dw_up_t,
                      name=f"{tag}_dw_up_v")
    deps = sched("bwd_ffn_grads", l, dz_v, g)
    dx1, dx1_bf, dgain = _mm_rows((dz_g, dz_v), w["w_up_t"], mode="nn", fn=_rms_bwd_rows, out_dtypes=(F32, BF16),
                                  rows=(s["x1"], dx2), vecs=(w["ffn_norm"],), reduce=True, a_at=(0, D_FF),
                                  name=f"{tag}_dh2", deps=deps)
    g["ffn_norm"] = dgain.reshape(D_MODEL)
    dpa, dpb, dga, dgb = _merge_bwd(dx1_bf, w["w_out"], s["y_att"], s["y_sgu"], w["w_oa"], w["w_ob"], s["proj"],
                                    name=f"{tag}_merge")
    deps = sched("bwd_merge", l, dpa)
    g["w_out"] = _mm(s["merged"], dx1_bf, mode="tn", out_dtype=F32, name=f"{tag}_dw_out",
                     deps=deps)
    dy_att = _mm(dpa, w["w_oa"], mode="nt", out_dtype=BF16, name=f"{tag}_dy_att")
    dy_sgu = _mm(dpb, w["w_ob"], mode="nt", out_dtype=F32, name=f"{tag}_dy_sgu")
    g["w_oa"] = _mm(s["y_att"], dpa, mode="tn", out_dtype=F32, name=f"{tag}_dw_oa")
    g["w_ob"] = _mm(s["y_sgu"], dpb, mode="tn", out_dtype=F32, name=f"{tag}_dw_ob")
    deps = sched("bwd_out_grads", l, dy_att, g)
    dqkv, g["q_norm"], g["k_norm"], g["sinks"] = _attention_bwd(
        s["proj"], dy_att, w["q_norm"], w["k_norm"], w["sinks"], n_seq=n_seq, seq=seq, name=f"{tag}_att", deps=deps)
    deps = sched("bwd_att", l, dqkv)
    dsuv, g["sgu_norm"], g["w_s"], g["b_s"] = _sgu_bwd(
        s["proj"], dy_sgu, w["sgu_norm"], w["w_s"], w["bias_full"], n_seq=n_seq, seq=seq, name=f"{tag}_sgu", deps=deps)
    dproj = (dsuv, dga, dgb, dqkv)
    at = (QKV_WIDTH, QKV_WIDTH + 2 * SGU_WIDTH, QKV_WIDTH + 2 * SGU_WIDTH + D_MODEL, 0)
    dw = None
    for i, (piece, first) in enumerate(zip(dproj, at)):
        dw = _mm(piece, s["h"], mode="tn", out_dtype=F32, out_rows=(first, IN_WIDTH), out_prev=dw, name=f"{tag}_dw_in_{i}")
    g["w_in_t"] = dw
    deps = sched("bwd_w_in_grad", l, dqkv, g)
    dx, dx_bf, dgain = _mm_rows(dproj, w["w_in_t"], mode="nn", fn=_rms_bwd_rows, out_dtypes=(F32, BF16),
                                rows=(s["x"], dx1), vecs=(w["mix_norm"],), reduce=True, a_at=at,
                                name=f"{tag}_dh", deps=deps)
    g["mix_norm"] = dgain.reshape(D_MODEL)
    return dx, dx_bf, g, sched("bwd_dh", l, dx)


def _local_step(x, target, weights, sched, *, n_seq, seq):
    depth = len(weights)
    saved = []
    h = _rms_fwd(x, weights[0]["mix_norm"], name="l0_mix_norm")
    for l in range(depth):
        tail = ("norm", weights[l + 1]["mix_norm"]) if l + 1 < depth else ("loss", target)
        out, s = _layer_fwd(x, h, weights[l], sched, tail, n_seq=n_seq, seq=seq, l=l)
        saved.append(s)
        if l + 1 < depth:
            x, h = out
    dy, dy_bf, loss_cols = out
    grads = [None] * depth
    deps = ()
    for l in reversed(range(depth)):
        dy, dy_bf, grads[l], deps = _layer_bwd(dy, dy_bf, weights[l], saved[l], sched, deps, n_seq=n_seq, seq=seq, l=l)
    return jnp.sum(loss_cols), dy, grads


W_IN_SHARD = IN_WIDTH // N_DEV
W_UP_SHARD = 2 * D_FF // N_DEV
COL_MOVE_ROWS = 256


def _w_o_moves():
    return tuple((j, 0, LANES, 0, j * LANES) for j in range(N_DEV))


def _disassemble(mats, w, moves, *, name):
    R = mats[0].shape[0]
    tr = min(R, COL_MOVE_ROWS)
    n = len(mats)

    def body(*refs):
        m_refs, o_ref = refs[:n], refs[n]
        for j, lo, hi, which, at in moves:
            o_ref[j, :, lo:hi] = m_refs[which][:, at:at + hi - lo]

    return pl.pallas_call(
        body, name=name, grid=(R // tr,),
        in_specs=[pl.BlockSpec((tr, m.shape[1]), lambda i: (i, 0)) for m in mats],
        out_specs=pl.BlockSpec((N_DEV, tr, w), lambda i: (0, i, 0)),
        out_shape=jax.ShapeDtypeStruct((N_DEV, R, w), mats[0].dtype),
        compiler_params=_params(("parallel",)),
    )(*mats)


def _my_place():
    return lax.axis_index("x"), lax.axis_index("y"), lax.axis_index("c")


def _gathered_shape(shape, kind):
    r, c = shape
    return {"blocks": (N_DEV, r, c), "rows": (N_DEV * r, c), "cols": (r, N_DEV * c)}[kind]


def _gather_window(ref, kind, shape, j):
    r, c = shape
    if kind == "blocks":
        return ref.at[j]
    if kind == "rows":
        return ref.at[pl.ds(pl.multiple_of(j * r, r), r), :]
    return ref.at[:, pl.ds(pl.multiple_of(j * c, c), c)]


def _gather(srcs, kinds, *, name):
    n = len(srcs)
    shapes = [s.shape for s in srcs]
    per = 7

    def body(*refs):
        src_refs, dst_refs = refs[:n], refs[n:2 * n]
        send_sems, recv_sems, local_sems = refs[2 * n:]
        x, y, c = _my_place()
        me, sibling = (x, y, c), (x, y, 1 - c)
        chips = [(1 - x, y), (x, 1 - y), (1 - x, 1 - y)]

        def at(i, px, py, pc):
            return _gather_window(dst_refs[i], kinds[i], shapes[i], 4 * px + 2 * py + pc)

        def copy(i, k, block, to, src=None):
            return pltpu.make_async_remote_copy(
                src_ref=at(i, *block) if src is None else src, dst_ref=at(i, *block),
                send_sem=send_sems.at[per * i + k], recv_sem=recv_sems.at[per * i + k], device_id=to, device_id_type=MESH)

        mine = [pltpu.make_async_copy(src_refs[i], at(i, *me), local_sems.at[i]) for i in range(n)]
        for cp in mine:
            cp.start()
        started = []
        for i in range(n):
            first = [copy(i, 0, me, sibling, src=src_refs[i])]
            first += [copy(i, 1 + j, me, (*chip, c), src=src_refs[i]) for j, chip in enumerate(chips)]
            for cp in first:
                cp.start()
            started += first
        for i in range(n):
            for j, chip in enumerate(chips):
                copy(i, 1 + j, (*chip, c), me).wait_recv()
                fwd = copy(i, 4 + j, (*chip, c), sibling)
                fwd.start()
                started.append(fwd)
        for i in range(n):
            copy(i, 0, sibling, me).wait_recv()
            for j, chip in enumerate(chips):
                copy(i, 4 + j, (*chip, 1 - c), me).wait_recv()
        for cp in started:
            cp.wait_send()
        for cp in mine:
            cp.wait()

    return pl.pallas_call(
        body, name=name,
        out_shape=[jax.ShapeDtypeStruct(_gathered_shape(s.shape, k), s.dtype) for s, k in zip(srcs, kinds)],
        in_specs=[ANY] * n, out_specs=[ANY] * n,
        scratch_shapes=[pltpu.SemaphoreType.DMA((per * n,)), pltpu.SemaphoreType.DMA((per * n,)),
                        pltpu.SemaphoreType.DMA((n,))],
    )(*srcs)


HBM = pl.BlockSpec(memory_space=pltpu.HBM)
SEM = pl.BlockSpec(memory_space=pltpu.SEMAPHORE)
TOKEN = jax.ShapeDtypeStruct((SUBLANES, LANES), F32)
TOKEN_SPEC = pl.BlockSpec(memory_space=pltpu.VMEM)
SPLIT_PARAMS = pltpu.CompilerParams(has_side_effects=pltpu.SideEffectType.DATAFLOW_SIDE_EFFECTING)


def _in_hbm(x):
    return pltpu.with_memory_space_constraint(x, pltpu.HBM)


def _hbm_like(shape, dtype):
    return pltpu.HBM(shape, dtype)


def _place_own(shards, kinds, dtypes, *, name):
    n = len(shards)
    shapes = [s.shape for s in shards]

    def body(*refs):
        s_refs, land_refs, bufs, sems = refs[:n], refs[n:2 * n], refs[2 * n:3 * n], refs[3 * n]
        x, y, c = _my_place()
        copies = []
        for i in range(n):
            bufs[i][...] = s_refs[i][...].astype(dtypes[i])
            copies.append(pltpu.make_async_copy(
                bufs[i], _gather_window(land_refs[i], kinds[i], shapes[i], 4 * x + 2 * y + c), sems.at[i]))
        for cp in copies:
            cp.start()
        for cp in copies:
            cp.wait()

    return pl.pallas_call(
        body, name=name,
        out_shape=[jax.ShapeDtypeStruct(_gathered_shape(s, k), d) for s, k, d in zip(shapes, kinds, dtypes)],
        in_specs=[pl.BlockSpec(memory_space=pltpu.VMEM)] * n, out_specs=[ANY] * n,
        scratch_shapes=[pltpu.VMEM(s, d) for s, d in zip(shapes, dtypes)] + [pltpu.SemaphoreType.DMA((n,))],
        compiler_params=_params(),
    )(*shards)


def _gather_start(lands, kinds, shapes, after=(), *, name):
    n = len(lands)
    n_after = len(after)

    def body(*refs):
        land_refs = refs[:n]
        send_sems, recv_sems = refs[n + n_after], refs[n + n_after + 1]
        x, y, c = _my_place()
        targets = [(x, y, 1 - c), (1 - x, y, c), (x, 1 - y, c), (1 - x, 1 - y, c)]
        for i in range(n):
            own = _gather_window(land_refs[i], kinds[i], shapes[i], 4 * x + 2 * y + c)
            for k, to in enumerate(targets):
                pltpu.make_async_remote_copy(
                    src_ref=own, dst_ref=own, send_sem=send_sems.at[4 * i + k], recv_sem=recv_sems.at[4 * i + k],
                    device_id=to, device_id_type=MESH).start()
        refs[-1][...] = jnp.zeros_like(refs[-1])

    outs = pl.pallas_call(
        body, name=name,
        out_shape=[pltpu.SemaphoreType.DMA((4 * n,)), pltpu.SemaphoreType.DMA((4 * n,))]
        + [_hbm_like(a.shape, a.dtype) for a in lands] + [TOKEN],
        in_specs=[HBM] * n + [ANY] * n_after, out_specs=[SEM, SEM] + [HBM] * n + [TOKEN_SPEC],
        input_output_aliases={i: 2 + i for i in range(n)},
        compiler_params=SPLIT_PARAMS,
    )(*[_in_hbm(a) for a in lands], *after)
    return outs[0], outs[1], outs[2:2 + n], outs[-1]


def _gather_forward(recv_sems, lands, kinds, shapes, after, *, name):
    n = len(lands)

    def body(*refs):
        recv_ref, land_refs = refs[0], refs[1:1 + n]
        fwd_send, fwd_recv = refs[2 + n], refs[3 + n]
        token = refs[-1]
        x, y, c = _my_place()
        chips = [(1 - x, y), (x, 1 - y), (1 - x, 1 - y)]
        for i in range(n):
            for j, (px, py) in enumerate(chips):
                block = _gather_window(land_refs[i], kinds[i], shapes[i], 4 * px + 2 * py + c)
                pltpu.make_async_remote_copy(
                    src_ref=block, dst_ref=block, send_sem=fwd_send.at[3 * i + j], recv_sem=recv_ref.at[4 * i + 1 + j],
                    device_id=(px, py, c), device_id_type=MESH).wait_recv()
                pltpu.make_async_remote_copy(
                    src_ref=block, dst_ref=block, send_sem=fwd_send.at[3 * i + j], recv_sem=fwd_recv.at[3 * i + j],
                    device_id=(x, y, 1 - c), device_id_type=MESH).start()
        token[...] = jnp.zeros_like(token)

    outs = pl.pallas_call(
        body, name=name,
        out_shape=[pltpu.SemaphoreType.DMA((3 * n,)), pltpu.SemaphoreType.DMA((3 * n,))]
        + [_hbm_like(a.shape, a.dtype) for a in lands] + [TOKEN],
        in_specs=[SEM] + [HBM] * n + [ANY], out_specs=[SEM, SEM] + [HBM] * n + [TOKEN_SPEC],
        input_output_aliases={1 + i: 2 + i for i in range(n)},
        compiler_params=SPLIT_PARAMS,
    )(recv_sems, *lands, after)
    return outs[0], outs[1], outs[2:2 + n], outs[-1]


def _gather_finish(send_sems, recv_sems, fwd_send, fwd_recv, lands, kinds, shapes, after, *, name):
    n = len(lands)

    def body(*refs):
        send_ref, recv_ref, fsend_ref, frecv_ref = refs[:4]
        land_refs = refs[4:4 + n]
        x, y, c = _my_place()
        chips = [(1 - x, y), (x, 1 - y), (1 - x, 1 - y)]
        sibling = (x, y, 1 - c)
        for i in range(n):
            def window(j):
                return _gather_window(land_refs[i], kinds[i], shapes[i], j)

            mine, theirs = window(4 * x + 2 * y + c), window(4 * x + 2 * y + (1 - c))
            pltpu.make_async_remote_copy(src_ref=mine, dst_ref=theirs, send_sem=send_ref.at[4 * i],
                                         recv_sem=recv_ref.at[4 * i], device_id=sibling, device_id_type=MESH).wait_recv()
            for j, (px, py) in enumerate(chips):
                block = window(4 * px + 2 * py + (1 - c))
                pltpu.make_async_remote_copy(src_ref=block, dst_ref=block, send_sem=fsend_ref.at[3 * i + j],
                                             recv_sem=frecv_ref.at[3 * i + j], device_id=sibling,
                                             device_id_type=MESH).wait_recv()
            for k in range(4):
                pltpu.make_async_remote_copy(src_ref=mine, dst_ref=mine, send_sem=send_ref.at[4 * i + k],
                                             recv_sem=recv_ref.at[4 * i + k], device_id=sibling,
                                             device_id_type=MESH).wait_send()
            for j, (px, py) in enumerate(chips):
                block = window(4 * px + 2 * py + c)
                pltpu.make_async_remote_copy(src_ref=block, dst_ref=block, send_sem=fsend_ref.at[3 * i + j],
                                             recv_sem=frecv_ref.at[3 * i + j], device_id=sibling,
                                             device_id_type=MESH).wait_send()

    return pl.pallas_call(
        body, name=name,
        out_shape=[_hbm_like(a.shape, a.dtype) for a in lands],
        in_specs=[SEM] * 4 + [HBM] * n + [ANY], out_specs=[HBM] * n,
        input_output_aliases={4 + i: i for i in range(n)},
        compiler_params=SPLIT_PARAMS,
    )(send_sems, recv_sems, fwd_send, fwd_recv, *lands, after)


def _pair_plan(src_ref, land_ref, x, y, c):
    return [(src_ref.at[2 * k + (1 - c)], land_ref.at[k], (x, y, 1 - c)) for k in range(N_CHIPS)]


def _chip_plan(src_ref, land_ref, x, y, c):
    chips = [(1 - x, y), (x, 1 - y), (1 - x, 1 - y)]
    return [(src_ref.at[2 * px + py], land_ref.at[k], (px, py, c)) for k, (px, py) in enumerate(chips)]


def _exchange_copies(plan, per, src_refs, land_refs, send_sems, recv_sems):
    x, y, c = _my_place()
    copies = []
    for i, (s_ref, l_ref) in enumerate(zip(src_refs, land_refs)):
        for q, (src, dst, to) in enumerate(plan(s_ref, l_ref, x, y, c)):
            copies.append(pltpu.make_async_remote_copy(
                src_ref=src, dst_ref=dst, send_sem=send_sems.at[per * i + q], recv_sem=recv_sems.at[per * i + q],
                device_id=to, device_id_type=MESH))
    return copies


def _exchange_start(srcs, plan, per, *, name):
    n = len(srcs)

    def body(*refs):
        src_refs, land_refs = refs[:n], refs[n:2 * n]
        send_sems, recv_sems = refs[2 * n], refs[2 * n + 1]
        for cp in _exchange_copies(plan, per, src_refs, land_refs, send_sems, recv_sems):
            cp.start()
        refs[-1][...] = jnp.zeros_like(refs[-1])

    lands = [lax.empty((per,) + s.shape[1:], s.dtype) for s in srcs]
    outs = pl.pallas_call(
        body, name=name,
        out_shape=[pltpu.SemaphoreType.DMA((per * n,)), pltpu.SemaphoreType.DMA((per * n,))]
        + [_hbm_like(s.shape, s.dtype) for s in srcs] + [_hbm_like(a.shape, a.dtype) for a in lands] + [TOKEN],
        in_specs=[HBM] * (2 * n), out_specs=[SEM, SEM] + [HBM] * (2 * n) + [TOKEN_SPEC],
        input_output_aliases={i: 2 + i for i in range(2 * n)},
        compiler_params=SPLIT_PARAMS,
    )(*[_in_hbm(s) for s in srcs], *[_in_hbm(a) for a in lands])
    return outs[0], outs[1], outs[2:2 + n], outs[2 + n:2 + 2 * n], outs[-1]


def _exchange_wait(send_sems, recv_sems, srcs, lands, plan, per, after, *, name):
    n = len(srcs)

    def body(*refs):
        send_ref, recv_ref = refs[0], refs[1]
        src_refs, land_refs = refs[2:2 + n], refs[2 + n:2 + 2 * n]
        copies = _exchange_copies(plan, per, src_refs, land_refs, send_ref, recv_ref)
        for cp in copies:
            cp.wait_recv()
        for cp in copies:
            cp.wait_send()

    outs = pl.pallas_call(
        body, name=name,
        out_shape=[_hbm_like(s.shape, s.dtype) for s in srcs] + [_hbm_like(a.shape, a.dtype) for a in lands],
        in_specs=[SEM, SEM] + [HBM] * (2 * n) + [ANY], out_specs=[HBM] * (2 * n),
        input_output_aliases={2 + i: i for i in range(2 * n)},
        compiler_params=SPLIT_PARAMS,
    )(send_sems, recv_sems, *srcs, *lands, after)
    return outs[:n], outs[n:]


REDUCE_BLOCK_BYTES = 1 << 20


def _row_tile(r, c):
    row_bytes = 4 * (-(-c // LANES) * LANES)
    best = r
    for d in range(SUBLANES, r, SUBLANES):
        if r % d == 0 and d * row_bytes <= REDUCE_BLOCK_BYTES:
            best = d
    return best if r * row_bytes > REDUCE_BLOCK_BYTES else r


def _reduce_pair_sum(blocked, recv, place, wire_dtype, *, name):
    _, r, c = blocked.shape
    tr = _row_tile(r, c)

    def body(place_ref, g_ref, r_ref, own_ref, send_ref):
        s = g_ref[...] + r_ref[...]
        send_ref[...] = s.astype(wire_dtype)

        @pl.when(pl.program_id(1) == place_ref[1])
        def _():
            own_ref[...] = s

    return pl.pallas_call(
        body, name=name,
        grid_spec=pltpu.PrefetchScalarGridSpec(
            num_scalar_prefetch=1, grid=(r // tr, N_CHIPS),
            in_specs=[pl.BlockSpec((None, None, tr, c), lambda i, k, place_ref: (k, place_ref[0], i, 0)),
                      pl.BlockSpec((None, tr, c), lambda i, k, place_ref: (k, i, 0))],
            out_specs=[pl.BlockSpec((tr, c), lambda i, k, place_ref: (i, 0)),
                       pl.BlockSpec((None, tr, c), lambda i, k, place_ref: (k, i, 0))]),
        out_shape=[jax.ShapeDtypeStruct((r, c), F32), jax.ShapeDtypeStruct((N_CHIPS, r, c), wire_dtype)],
        compiler_params=_params(("parallel", "arbitrary")),
    )(place, blocked.reshape(N_CHIPS, 2, r, c), recv)


def _chip_sum(own_ref, r_ref):
    return ((own_ref[...] + r_ref[0].astype(F32)) + r_ref[1].astype(F32)) + r_ref[2].astype(F32)


def _reduce_chip_sum(own, recv, *, name):
    r, c = own.shape
    tr = _row_tile(r, c)

    def body(own_ref, r_ref, o_ref):
        o_ref[...] = _chip_sum(own_ref, r_ref)

    return pl.pallas_call(
        body, name=name, grid=(r // tr,),
        in_specs=[pl.BlockSpec((tr, c), lambda i: (i, 0)), pl.BlockSpec((N_CHIPS - 1, tr, c), lambda i: (0, i, 0))],
        out_specs=pl.BlockSpec((tr, c), lambda i: (i, 0)),
        out_shape=jax.ShapeDtypeStruct((r, c), F32),
        compiler_params=_params(("parallel",)),
    )(own, recv)


def _adamw_math(w, g, m, v):
    nm = ADAM_B1 * m + (1.0 - ADAM_B1) * g
    nv = ADAM_B2 * v + (1.0 - ADAM_B2) * (g * g)
    m_hat = nm / (1.0 - ADAM_B1 ** ADAM_STEP)
    v_hat = nv / (1.0 - ADAM_B2 ** ADAM_STEP)
    return -ADAM_LR * (m_hat / (jnp.sqrt(v_hat) + ADAM_EPS) + ADAM_WD * w), nm, nv


def _adamw(w, g, m, v, *, name):
    shape = w.shape
    C = shape[-1]
    R = math.prod(shape[:-1])
    tr = _row_tile(R, C)

    def body(w_ref, g_ref, m_ref, v_ref, d_ref, nm_ref, nv_ref):
        d_ref[...], nm_ref[...], nv_ref[...] = _adamw_math(w_ref[...], g_ref[...], m_ref[...], v_ref[...])

    spec = pl.BlockSpec((tr, C), lambda i: (i, 0))
    outs = pl.pallas_call(
        body, name=name, grid=(R // tr,),
        in_specs=[spec] * 4, out_specs=[spec] * 3,
        out_shape=[jax.ShapeDtypeStruct((R, C), F32)] * 3,
        compiler_params=_params(("parallel",)),
    )(*[a.reshape(R, C) for a in (w, g, m, v)])
    return tuple(o.reshape(shape) for o in outs)


def _reduce_adamw(own, recv, w, m, v, layer, prev, *, name):
    r, c = own.shape
    tr = _row_tile(r, c)
    n_prev = 0 if prev is None else len(prev)

    def body(own_ref, r_ref, w_ref, m_ref, v_ref, *rest):
        g_ref, d_ref, nm_ref, nv_ref = rest[n_prev:]
        g = _chip_sum(own_ref, r_ref)
        g_ref[...] = g
        d_ref[...], nm_ref[...], nv_ref[...] = _adamw_math(w_ref[...], g, m_ref[...], v_ref[...])

    slot = pl.BlockSpec((None, tr, c), lambda i: (layer, i, 0))
    return pl.pallas_call(
        body, name=name, grid=(r // tr,),
        in_specs=[pl.BlockSpec((tr, c), lambda i: (i, 0)), pl.BlockSpec((N_CHIPS - 1, tr, c), lambda i: (0, i, 0)),
                  slot, slot, slot] + [ANY] * n_prev,
        out_specs=[slot] * 4,
        out_shape=[jax.ShapeDtypeStruct((DEPTH, r, c), F32)] * 4,
        input_output_aliases={5 + k: k for k in range(n_prev)},
        compiler_params=_params(("parallel",)),
    )(own, recv, w, m, v, *(prev or ()))


REPLICATED = (("mix_norm", (D_MODEL,)), ("q_norm", (HEAD_DIM,)), ("k_norm", (HEAD_DIM,)), ("sinks", (N_Q_HEADS,)),
              ("sgu_norm", (SGU_WIDTH,)), ("w_s", (SGU_GROUPS, BLOCK, BLOCK)), ("b_s", (SGU_GROUPS, BLOCK)),
              ("ffn_norm", (D_MODEL,)), ("conv_b", (2 * D_FF,)))
TRANSPOSED = ("w_in", "w_up")
SHARDED = (("w_in", "rows"), ("w_oa", "cols"), ("w_ob", "cols"), ("w_out", "rows"), ("w_up", "rows"),
           ("conv_w", "blocks"), ("w_down", "rows"))
WEIGHT_ORDER = ("mix_norm", "w_in", "q_norm", "k_norm", "sinks", "sgu_norm", "w_s", "b_s", "w_oa", "w_ob", "w_out",
                "ffn_norm", "w_up", "conv_w", "conv_b", "w_down")
MIXER_WEIGHTS = ["w_in", "w_oa", "w_ob", "w_out"]
FFN_WEIGHTS = ["w_up", "conv_w", "w_down"]


def _small_layout():
    segs, off = {}, 0
    for l in range(DEPTH):
        for name, shape in REPLICATED:
            n = math.prod(shape)
            segs[(l, name)] = (off, n)
            off += n
    per_dev = -(-off // (N_DEV * SUBLANES * LANES)) * SUBLANES * LANES
    return segs, off, per_dev


def _pack_small(grads):
    ssegs, total, per_dev = _small_layout()
    flat = jnp.concatenate([grads[l][name].reshape(-1) for (l, name) in ssegs])
    return jnp.pad(flat, (0, N_DEV * per_dev - total)).reshape(N_DEV, per_dev // LANES, LANES)


def _unpack_small(gathered):
    ssegs, _, _ = _small_layout()
    flat = gathered.reshape(-1)
    shapes = dict(REPLICATED)
    return {name: jnp.stack([flat[ssegs[(l, name)][0]:ssegs[(l, name)][0] + ssegs[(l, name)][1]].reshape(shapes[name])
                             for l in range(DEPTH)]) for name, _ in REPLICATED}


def kernel(x, mix_norm, w_in, q_norm, k_norm, sinks, sgu_norm, w_s, b_s, w_oa, w_ob, w_out, ffn_norm, w_up, conv_w, conv_b, w_down, loss_target, m_mix_norm, m_w_in, m_q_norm, m_k_norm, m_sinks, m_sgu_norm, m_w_s, m_b_s, m_w_oa, m_w_ob, m_w_out, m_ffn_norm, m_w_up, m_conv_w, m_conv_b, m_w_down, v_mix_norm, v_w_in, v_q_norm, v_k_norm, v_sinks, v_sgu_norm, v_w_s, v_b_s, v_w_oa, v_w_ob, v_w_out, v_ffn_norm, v_w_up, v_conv_w, v_conv_b, v_w_down):
    W = dict(mix_norm=mix_norm, w_in=w_in, q_norm=q_norm, k_norm=k_norm, sinks=sinks, sgu_norm=sgu_norm, w_s=w_s, b_s=b_s,
             w_oa=w_oa, w_ob=w_ob, w_out=w_out, ffn_norm=ffn_norm, w_up=w_up, conv_w=conv_w, conv_b=conv_b, w_down=w_down)
    M = dict(mix_norm=m_mix_norm, w_in=m_w_in, q_norm=m_q_norm, k_norm=m_k_norm, sinks=m_sinks, sgu_norm=m_sgu_norm,
             w_s=m_w_s, b_s=m_b_s, w_oa=m_w_oa, w_ob=m_w_ob, w_out=m_w_out, ffn_norm=m_ffn_norm, w_up=m_w_up,
             conv_w=m_conv_w, conv_b=m_conv_b, w_down=m_w_down)
    V = dict(mix_norm=v_mix_norm, w_in=v_w_in, q_norm=v_q_norm, k_norm=v_k_norm, sinks=v_sinks, sgu_norm=v_sgu_norm,
             w_s=v_w_s, b_s=v_b_s, w_oa=v_w_oa, w_ob=v_w_ob, w_out=v_w_out, ffn_norm=v_ffn_norm, w_up=v_w_up,
             conv_w=v_conv_w, conv_b=v_conv_b, w_down=v_w_down)
    n_seq, seq, d_model = x.shape
    tokens = n_seq * seq
    mx, my, mc = _my_place()
    place = jnp.stack([mc, 2 * mx + my]).astype(jnp.int32)
    half = N_DEV // 2
    kind_of = dict(SHARDED)
    for name in TRANSPOSED:
        W[name], M[name], V[name] = (jnp.swapaxes(t[name], 1, 2) for t in (W, M, V))

    gather_groups = [[(l, n) for n in names] for l in range(DEPTH) for names in (MIXER_WEIGHTS, FFN_WEIGHTS)]
    started, in_flight = {}, {}
    weights = []
    for l in range(DEPTH):
        w = {name: W[name][l] for name, _ in REPLICATED}
        w["cb_g"], w["cb_v"] = W["conv_b"][l][:D_FF], W["conv_b"][l][D_FF:]
        w["bias_full"] = jnp.repeat(W["b_s"][l].T, SGU_WIDTH // SGU_GROUPS, axis=1)
        weights.append(w)

    def gather_start(gi, after=()):
        shards = [W[name][l] for l, name in gather_groups[gi]]
        kinds = [kind_of[name] for _, name in gather_groups[gi]]
        shapes = [s.shape for s in shards]
        lands = _place_own(shards, kinds, [F32 if name == "conv_w" else BF16 for _, name in gather_groups[gi]],
                           name=f"gather_weights_own_{gi}")
        send, recv, lands, token = _gather_start(lands, kinds, shapes, after, name=f"gather_weights_start_{gi}")
        started[gi] = dict(sems=(send, recv), lands=lands, kinds=kinds, shapes=shapes)
        return token

    def gather_forward(gi, after):
        st = started[gi]
        in_flight[gi] = _gather_forward(st["sems"][1], st["lands"], st["kinds"], st["shapes"], after,
                                        name=f"gather_weights_forward_{gi}")
        return in_flight[gi][3]

    def gather_finish(gi, after):
        st = started.pop(gi)
        fwd_send, fwd_recv, lands_g, _ = in_flight.pop(gi)
        whole = _gather_finish(st["sems"][0], st["sems"][1], fwd_send, fwd_recv, lands_g, st["kinds"], st["shapes"], after,
                               name=f"gather_weights_finish_{gi}")
        for (l, name), arr in zip(gather_groups[gi], whole):
            w = weights[l]
            if name in TRANSPOSED:
                w[name + "_t"] = arr
            elif name == "conv_w":
                w["cw_g"] = arr[:half].transpose(1, 0, 2).reshape(3, D_FF)
                w["cw_v"] = arr[half:].transpose(1, 0, 2).reshape(3, D_FF)
            else:
                w[name] = arr

    reduce_state, results = {}, {}
    wire = {"conv_w": F32, "small": F32}

    def reduce_begin(key, names, arrays):
        send, recv, srcs_, lands_, token = _exchange_start(arrays, _pair_plan, N_CHIPS, name=f"reduce_pair_start_{key}")
        reduce_state[key] = dict(names=names, pair=(send, recv, srcs_, lands_))
        return [token]

    def reduce_pair(key, after):
        st = reduce_state[key]
        send, recv, srcs_, lands_ = st.pop("pair")
        blocked_, from_sibling = _exchange_wait(send, recv, srcs_, lands_, _pair_plan, N_CHIPS, after,
                                                name=f"reduce_pair_wait_{key}")
        sums = [_reduce_pair_sum(b, r, place, wire.get(n if isinstance(n, str) else n[1], BF16),
                                 name=f"reduce_pair_sum_{key}_{i}")
                for i, (n, b, r) in enumerate(zip(st["names"], blocked_, from_sibling))]
        st["own"] = [s[0] for s in sums]
        *st["chip"], token = _exchange_start([s[1] for s in sums], _chip_plan, N_CHIPS - 1, name=f"reduce_chip_start_{key}")
        return [token]

    def reduce_end(key, after):
        st = reduce_state.pop(key)
        send, recv, srcs_, lands_ = st["chip"]
        _, from_chips = _exchange_wait(send, recv, srcs_, lands_, _chip_plan, N_CHIPS - 1, after,
                                       name=f"reduce_chip_wait_{key}")
        done = []
        for n, own, got in zip(st["names"], st["own"], from_chips):
            if n == "small":
                results["small"] = _reduce_chip_sum(own, got, name="reduce_chip_sum_small")
            else:
                l, name = n
                results[name] = _reduce_adamw(own, got, W[name], M[name], V[name], l, results.get(name),
                                              name=f"l{l}_reduce_adamw_{name}")
                done.append(results[name][0])
        return done

    def sched(point, l, carry, g=None):
        deps = []
        if point == "fwd_start" and l == 0:
            token = gather_forward(0, gather_start(0))
            gather_finish(0, token)
            deps = [gather_start(1, [weights[0]["w_out"]])]
        elif point == "fwd_att" and l == 0:
            deps = [gather_forward(1, carry), gather_start(2, [carry])]
        elif point == "fwd_mixer_done" and l == 0:
            gather_finish(1, carry)
            deps = [gather_start(3, [carry])]
        elif point == "fwd_conv" and l == 0:
            deps = [gather_forward(2, carry)]
        elif point == "fwd_start" and l == 1:
            gather_finish(2, carry)
        elif point == "fwd_att" and l == 1:
            deps = [gather_forward(3, carry)]
        elif point == "fwd_mixer_done" and l == 1:
            gather_finish(3, carry)
        elif point == "bwd_ffn_grads":
            if l + 1 < DEPTH:
                deps += reduce_end(f"l{l + 1}_in", g["w_up_t"])
            conv_w = jnp.concatenate([g[k].reshape(3, half, W_UP_SHARD).transpose(1, 0, 2) for k in ("cw_g", "cw_v")])
            deps += reduce_begin(
                f"l{l}_ffn", [(l, "w_down"), (l, "w_up"), (l, "conv_w")],
                [g["w_down"].reshape(N_DEV, D_FF // N_DEV, D_MODEL),
                 g["w_up_t"].reshape(N_DEV, W_UP_SHARD, D_MODEL), conv_w])
        elif point == "bwd_merge":
            deps = reduce_pair(f"l{l}_ffn", carry)
        elif point == "bwd_out_grads":
            deps = reduce_begin(
                f"l{l}_out", [(l, "w_out"), (l, "w_oa"), (l, "w_ob")],
                [g["w_out"].reshape(N_DEV, D_MODEL // N_DEV, D_MODEL),
                 _disassemble((g["w_oa"],), LANES, _w_o_moves(), name=f"l{l}_split_dw_oa"),
                 _disassemble((g["w_ob"],), LANES, _w_o_moves(), name=f"l{l}_split_dw_ob")])
        elif point == "bwd_att":
            deps = reduce_pair(f"l{l}_out", carry) + reduce_end(f"l{l}_ffn", carry)
        elif point == "bwd_w_in_grad":
            deps = reduce_begin(f"l{l}_in", [(l, "w_in")], [g["w_in_t"].reshape(N_DEV, W_IN_SHARD, D_MODEL)])
        elif point == "bwd_dh":
            deps = reduce_pair(f"l{l}_in", carry) + reduce_end(f"l{l}_out", carry)
        return deps

    loss_part, dx, grads = _local_step(x.reshape(tokens, d_model), loss_target.reshape(tokens, d_model), weights, sched,
                                       n_seq=n_seq, seq=seq)
    loss = lax.psum(loss_part, ("x", "y", "c"))

    for g in grads:
        g["conv_b"] = jnp.concatenate([g["cb_g"], g["cb_v"]])
    reduce_begin("small", ["small"], [_pack_small(grads)])
    reduce_end("l0_in", dx)
    reduce_pair("small", results["w_in"][0])
    reduce_end("small", results["w_in"][1])

    G, delta, new_m, new_v = {}, {}, {}, {}
    for name, _ in SHARDED:
        outs = [jnp.swapaxes(o, 1, 2) for o in results[name]] if name in TRANSPOSED else results[name]
        G[name], delta[name], new_m[name], new_v[name] = outs
    G.update(_unpack_small(_gather([results["small"]], ["blocks"], name="gather_small_grads")[0]))
    for name, _ in REPLICATED:
        delta[name], new_m[name], new_v[name] = _adamw(W[name], G[name], M[name], V[name], name=f"adamw_{name}")
    return (loss, dx.reshape(n_seq, seq, d_model), *[G[n] for n in WEIGHT_ORDER], *[delta[n] for n in WEIGHT_ORDER],
            *[new_m[n] for n in WEIGHT_ORDER], *[new_v[n] for n in WEIGHT_ORDER])
```

```python
import math

import jax
import jax.numpy as jnp
from jax import lax
from jax.experimental import pallas as pl
from jax.experimental.pallas import tpu as pltpu

F32 = jnp.float32
BF16 = jnp.bfloat16
ACT_DTYPE = BF16
MESH = pl.DeviceIdType.MESH

DEPTH = 2
D_MODEL = 1024
N_Q_HEADS = 8
HEAD_DIM = 64
ATT_WIDTH = 512
KV_WIDTH = 128
BLOCK = 128
SGU_WIDTH = 512
SGU_GROUPS = 8
IN_WIDTH = 3840
D_FF = 2816
NORM_EPS = 1e-6
NEG_INF = -1e30
ATT_SCALE = HEAD_DIM ** -0.5
ALIBI_SLOPES = tuple(2.0 ** (-(h + 1)) for h in range(N_Q_HEADS))
ADAM_LR, ADAM_B1, ADAM_B2, ADAM_EPS, ADAM_WD, ADAM_STEP = 0.001, 0.9, 0.999, 1e-08, 0.01, 10
N_DEV = 8
N_CHIPS = 4

QKV_WIDTH = ATT_WIDTH + 2 * KV_WIDTH
REST_WIDTH = IN_WIDTH - QKV_WIDTH
COL_SUV, COL_GA, COL_GB, COL_QKV = 0, 1024, 2048, 3072
W_IN_ROTATE = (1, IN_WIDTH // QKV_WIDTH)

LANES = 128
SUBLANES = 8
VMEM_LIMIT_V7X = 56 * 1024 * 1024
GELU_C = math.sqrt(2.0 / math.pi)
GELU_K = 0.044715
ANY = pl.BlockSpec(memory_space=pl.ANY)


def _params(sem=None):
    return pltpu.CompilerParams(dimension_semantics=sem, vmem_limit_bytes=VMEM_LIMIT_V7X)


def _sigmoid(x):
    return 1.0 / (1.0 + jnp.exp(-x))


def _gelu(x):
    th = jnp.tanh(GELU_C * (x + GELU_K * x * x * x))
    return 0.5 * x * (1.0 + th)


def _gelu_and_grad(x):
    x2 = x * x
    th = jnp.tanh(GELU_C * (x + GELU_K * x2 * x))
    g = 0.5 * x * (1.0 + th)
    dg = 0.5 * (1.0 + th) + 0.5 * x * (1.0 - th * th) * (GELU_C * (1.0 + 3.0 * GELU_K * x2))
    return g, dg


def _dot(a, b, dims):
    return lax.dot_general(a, b, (dims, ((), ())), preferred_element_type=F32)


def _dot_nn(a, b):
    return _dot(a, b, ((1,), (0,)))


def _dot_nt(a, b):
    return _dot(a, b, ((1,), (1,)))


def _dot_tn(a, b):
    return _dot(a, b, ((0,), (0,)))


def _lo_mask(shape):
    return lax.broadcasted_iota(jnp.int32, shape, len(shape) - 1) < (LANES // 2)


def _half_sums(x, lo):
    s_lo = jnp.sum(jnp.where(lo, x, 0.0), axis=-1, keepdims=True)
    s_all = jnp.sum(x, axis=-1, keepdims=True)
    return jnp.where(lo, s_lo, s_all - s_lo)


def _dup_half(x, half, lo):
    r = pltpu.roll(x, LANES // 2, axis=1)
    return jnp.where(lo, x, r) if half == 0 else jnp.where(lo, r, x)


def _with_deps(body, n_in, deps):
    k = len(deps)
    if not k:
        return body, [], ()

    def skipping(*refs):
        return body(*refs[:n_in], *refs[n_in + k:])

    return skipping, [ANY] * k, tuple(deps)


MM_VMEM_BUDGET = 40 * 1024 * 1024
MM_MAX_TILE = 1408
MM_MAX_TK = 4096
MM_STEP_BYTES = 1 << 20


def _divisors(n, step, cap):
    return [d for d in range(step, min(n, cap) + 1, step) if n % d == 0] or [n]


def _mm_tiles(M, N, K, out_bytes, tm_divides, tn_divides):
    best = None
    for tm in _divisors(M, LANES, MM_MAX_TILE):
        for tn in _divisors(N, LANES, MM_MAX_TILE):
            if tm_divides % tm or tn_divides % tn:
                continue
            for tk in _divisors(K, 4 * LANES, MM_MAX_TK):
                vmem = 4 * (tm * tk + tk * tn) + 2 * tm * tn * out_bytes + (0 if tk == K else 4 * tm * tn)
                if vmem > MM_VMEM_BUDGET:
                    continue
                traffic = 2 * M * K * (N // tn) + 2 * K * N * (M // tm) + M * N * out_bytes
                cost = traffic + (K // tk - 1) * 8 * M * N + (M // tm) * (N // tn) * (K // tk) * MM_STEP_BYTES
                if best is None or cost < best[0]:
                    best = (cost, tm, tn, tk)
    assert best is not None, (M, N, K)
    return best[1:]


def _mm(a, b, *, mode, out_dtype, name, deps=(), n=None, b_rows=(0, None), rotate=None, out_rows=(0, None), out_prev=None):
    b_first, b_count = b_rows
    if mode == "nn":
        (M, K), N = a.shape, b.shape[1]
    elif mode == "nt":
        (M, K), N = a.shape, (b.shape[0] if b_count is None else b_count)
    else:
        (K, M), N = a.shape, b.shape[1]
    shift, period = rotate or (0, 1)
    out_first, out_total = out_rows[0], (M if out_rows[1] is None else out_rows[1])
    tm, tn, tk = _mm_tiles(M, N, K, jnp.dtype(out_dtype).itemsize,
                           math.gcd(M // period if mode == "tn" else M, out_first),
                           math.gcd(N // period if mode == "nt" else N, b_first if mode == "nt" else 0))
    gm, gn, gk = M // tm, N // tn, K // tk

    def turned(t, tile, size):
        per = size // period // tile
        return ((t // per + shift) % period) * per + t % per if period > 1 else t

    if mode == "nn":
        a_spec = pl.BlockSpec((tm, tk), lambda i, j, k: (i, k))
        b_spec = pl.BlockSpec((tk, tn), lambda i, j, k: (k + b_first // tk, j))
        contract = ((1,), (0,))
    elif mode == "nt":
        a_spec = pl.BlockSpec((tm, tk), lambda i, j, k: (i, k))
        b_spec = pl.BlockSpec((tn, tk), lambda i, j, k: (turned(j, tn, N) + b_first // tn, k))
        contract = ((1,), (1,))
    else:
        a_spec = pl.BlockSpec((tk, tm), lambda i, j, k: (k, i))
        b_spec = pl.BlockSpec((tk, tn), lambda i, j, k: (k, j))
        contract = ((0,), (0,))
    if mode == "tn":
        o_spec = pl.BlockSpec((tm, tn), lambda i, j, k: (turned(i, tm, M) + out_first // tm, j))
    else:
        o_spec = pl.BlockSpec((tm, tn), lambda i, j, k: (i + out_first // tm, j))
    assert b_first % (tk if mode == "nn" else tn) == 0 and out_first % tm == 0, (name, tm, tn, tk)
    n_prev = 0 if out_prev is None else 1

    def body(a_ref, b_ref, *rest):
        o_ref = rest[n_prev]
        part = _dot(a_ref[...].astype(BF16), b_ref[...].astype(BF16), contract)
        if gk == 1:
            o_ref[...] = part.astype(out_dtype)
            return
        acc_ref = rest[n_prev + 1]
        k = pl.program_id(2)

        @pl.when(k == 0)
        def _():
            acc_ref[...] = part

        @pl.when(k > 0)
        def _():
            acc_ref[...] += part

        @pl.when(k == gk - 1)
        def _():
            o_ref[...] = acc_ref[...].astype(out_dtype)

    body, dep_specs, dep_args = _with_deps(body, 2 + n_prev, deps)
    return pl.pallas_call(
        body,
        name=name,
        grid=(gm, gn, gk),
        in_specs=[a_spec, b_spec] + [ANY] * n_prev + dep_specs,
        out_specs=o_spec,
        out_shape=jax.ShapeDtypeStruct((out_total, N), out_dtype),
        input_output_aliases={2: 0} if n_prev else {},
        scratch_shapes=[] if gk == 1 else [pltpu.VMEM((tm, tn), F32)],
        compiler_params=_params(("parallel", "parallel", "arbitrary")),
    )(a, b, *([out_prev] if n_prev else []), *dep_args)


def _mm_rows(a, b, *, mode, fn, out_dtypes, rows=(), vecs=(), reduce=False, name, deps=(), b_rows=(0, None), a_at=None):
    parts = a if a_at is not None else (a,)
    starts = a_at if a_at is not None else (0,)
    n_parts = len(parts)
    M, K = parts[0].shape[0], sum(p.shape[1] for p in parts)
    b_first, b_count = b_rows[0], (b.shape[0] if b_rows[1] is None else b_rows[1])
    N = b.shape[1] if mode == "nn" else b_count
    contract = ((1,), (0,)) if mode == "nn" else ((1,), (1,))
    n_rows, n_vecs, n_out = len(rows), len(vecs), len(out_dtypes)
    out_bytes = sum(jnp.dtype(d).itemsize for d in out_dtypes)
    tm = max(t for t in _divisors(M, LANES, MM_MAX_TILE)
             if 4 * t * K + 4 * K * N + 2 * t * N * (4 * n_rows + out_bytes) <= MM_VMEM_BUDGET)
    assert b_first % b_count == 0 and (a_at is None or mode == "nn")

    def body(*refs):
        a_refs, b_ref, rest = refs[:n_parts], refs[n_parts], refs[n_parts + 1:]
        row_refs, vec_refs = rest[:n_rows], rest[n_rows:n_rows + n_vecs]
        out_refs = rest[n_rows + n_vecs:]
        if a_at is None:
            acc = _dot(a_refs[0][...], b_ref[...], contract)
        else:
            acc = sum(_dot(r[...], b_ref[at:at + r.shape[1], :], contract) for r, at in zip(a_refs, starts))
        res = fn(acc, *[r[...] for r in row_refs], *[v[...] for v in vec_refs])
        for o_ref, val in zip(out_refs[:n_out], res):
            o_ref[...] = val.astype(o_ref.dtype)
        if reduce:
            @pl.when(pl.program_id(0) == 0)
            def _():
                out_refs[n_out][...] = res[n_out]

            @pl.when(pl.program_id(0) > 0)
            def _():
                out_refs[n_out][...] += res[n_out]

    row = pl.BlockSpec((tm, N), lambda i: (i, 0))
    vec = pl.BlockSpec((1, N), lambda i: (0, 0))
    body, dep_specs, dep_args = _with_deps(body, n_parts + 1 + n_rows + n_vecs, deps)
    return pl.pallas_call(
        body, name=name, grid=(M // tm,),
        in_specs=[pl.BlockSpec((tm, p.shape[1]), lambda i: (i, 0)) for p in parts]
        + [pl.BlockSpec((b_count, b.shape[1]), lambda i: (b_first // b_count, 0))]
        + [row] * n_rows + [vec] * n_vecs + dep_specs,
        out_specs=[row] * n_out + [vec] * reduce,
        out_shape=[jax.ShapeDtypeStruct((M, N), d) for d in out_dtypes] + [jax.ShapeDtypeStruct((1, N), F32)] * reduce,
        compiler_params=_params(("arbitrary",)),
    )(*parts, b, *rows, *[v.reshape(1, N) for v in vecs], *dep_args)


def _rms(x, gain):
    return x * lax.rsqrt(jnp.mean(x * x, axis=-1, keepdims=True) + NORM_EPS) * gain


def _residual_then_norm(acc, x, gain):
    x_out = x + acc
    return x_out, _rms(x_out, gain)


def _residual_then_loss(acc, x, target):
    err = (x + acc) - target
    dy = err * (1.0 / D_MODEL)
    return dy, dy, jnp.sum(err * err, axis=0, keepdims=True) * (0.5 / D_MODEL)


def _rms_bwd_rows(dh, x, dres, gain):
    r = lax.rsqrt(jnp.mean(x * x, axis=-1, keepdims=True) + NORM_EPS)
    xh = x * r
    dxh = dh * gain
    dx = dres + r * (dxh - xh * jnp.mean(dxh * xh, axis=-1, keepdims=True))
    return dx, dx, jnp.sum(dh * xh, axis=0, keepdims=True)


def _rms_fwd(x, gain, *, name, tm=512):
    T, D = x.shape

    def body(x_ref, g_ref, h_ref):
        xv = x_ref[...]
        r = lax.rsqrt(jnp.mean(xv * xv, axis=-1, keepdims=True) + NORM_EPS)
        h_ref[...] = (xv * r * g_ref[...]).astype(BF16)

    return pl.pallas_call(
        body, name=name, grid=(T // tm,),
        in_specs=[pl.BlockSpec((tm, D), lambda i: (i, 0)), pl.BlockSpec((1, D), lambda i: (0, 0))],
        out_specs=pl.BlockSpec((tm, D), lambda i: (i, 0)),
        out_shape=jax.ShapeDtypeStruct((T, D), BF16),
        compiler_params=_params(("parallel",)),
    )(x, gain.reshape(1, D))


def _head_norm(x, gain2, lo):
    ms = _half_sums(x * x, lo) * (1.0 / HEAD_DIM)
    r = lax.rsqrt(ms + NORM_EPS)
    xh = x * r
    return xh * gain2, xh, r


def _head_norm_bwd(xh, r, gain2, dy, lo):
    dxh = dy * gain2
    dx = r * (dxh - xh * (_half_sums(dxh * xh, lo) * (1.0 / HEAD_DIM)))
    return dx, dy * xh


Q_GROUP = N_Q_HEADS // 2
GROUP_ROWS = Q_GROUP * BLOCK
ATT_SCRATCH = (pltpu.VMEM((2, 2, GROUP_ROWS, BLOCK), F32), pltpu.VMEM((2, GROUP_ROWS, 1), F32))


def _att_consts(sink_ref, bias_ref, sinkcol_ref):
    row = lax.broadcasted_iota(jnp.int32, (GROUP_ROWS, BLOCK), 0)
    kj = lax.broadcasted_iota(jnp.int32, (GROUP_ROWS, BLOCK), 1)
    head = row // BLOCK
    head_col = lax.broadcasted_iota(jnp.int32, (GROUP_ROWS, 1), 0) // BLOCK
    d_cur = (row % BLOCK) - kj
    d_prev = d_cur + BLOCK
    for kv in range(2):
        slope = jnp.zeros((GROUP_ROWS, BLOCK), F32)
        sink = jnp.zeros((GROUP_ROWS, 1), F32)
        for r in range(Q_GROUP):
            slope = jnp.where(head == r, ALIBI_SLOPES[Q_GROUP * kv + r], slope)
            sink = jnp.where(head_col == r, sink_ref[Q_GROUP * kv + r], sink)
        bias_ref[kv, 0] = jnp.where(d_cur >= 0, -slope * d_cur.astype(F32), NEG_INF)
        bias_ref[kv, 1] = jnp.where(d_prev < BLOCK, -slope * d_prev.astype(F32), NEG_INF)
        sinkcol_ref[kv] = sink


def _stack_heads(t0, t1, lo):
    z = jnp.zeros_like(t0)
    return jnp.concatenate([jnp.where(lo, t0, z), jnp.where(lo, z, t0), jnp.where(lo, t1, z), jnp.where(lo, z, t1)], axis=0)


def _unstack_heads(x4, lo):
    return (jnp.where(lo, x4[0:BLOCK], x4[BLOCK:2 * BLOCK]), jnp.where(lo, x4[2 * BLOCK:3 * BLOCK], x4[3 * BLOCK:]))


def _att_probs(q4, k2c, k2p, bias_c, bias_p, sink, has_prev):
    s_c = _dot_nt(q4, k2c) * ATT_SCALE + bias_c
    s_p = jnp.where(has_prev, _dot_nt(q4, k2p) * ATT_SCALE + bias_p, NEG_INF)
    m = jnp.maximum(jnp.max(jnp.maximum(s_c, s_p), axis=-1, keepdims=True), sink)
    e_c = jnp.exp(s_c - m)
    e_p = jnp.exp(s_p - m)
    e_s = jnp.exp(sink - m)
    inv = 1.0 / (jnp.sum(e_c + e_p, axis=-1, keepdims=True) + e_s)
    return e_c * inv, e_p * inv, e_s * inv


def _attention_fwd(proj, q_gain, k_gain, sinks, *, n_seq, seq, name):
    T = n_seq * seq
    nb = seq // BLOCK
    qcol, kvcol = COL_QKV // ATT_WIDTH, (COL_QKV + ATT_WIDTH) // (2 * KV_WIDTH)

    def body(q_ref, kv_ref, qg_ref, kg_ref, sink_ref, y_ref, bias_ref, sinkcol_ref):
        lo = _lo_mask((BLOCK, LANES))
        qg, kg = qg_ref[...], kg_ref[...]
        _att_consts(sink_ref, bias_ref, sinkcol_ref)

        def block(i, carry):
            r0 = pl.multiple_of(i * BLOCK, BLOCK)
            rp = pl.multiple_of(jnp.maximum(i - 1, 0) * BLOCK, BLOCK)
            has_prev = i > 0
            kn_c = _head_norm(kv_ref[pl.ds(r0, BLOCK), 0:KV_WIDTH].astype(F32), kg, lo)[0].astype(BF16)
            kn_p = _head_norm(kv_ref[pl.ds(rp, BLOCK), 0:KV_WIDTH].astype(F32), kg, lo)[0].astype(BF16)
            v_c = kv_ref[pl.ds(r0, BLOCK), KV_WIDTH:2 * KV_WIDTH].astype(BF16)
            v_p = kv_ref[pl.ds(rp, BLOCK), KV_WIDTH:2 * KV_WIDTH].astype(BF16)
            for kv in range(2):
                k2c, k2p = _dup_half(kn_c, kv, lo), _dup_half(kn_p, kv, lo)
                v2c, v2p = _dup_half(v_c, kv, lo), _dup_half(v_p, kv, lo)
                cols = [slice((2 * kv + t) * LANES, (2 * kv + t + 1) * LANES) for t in range(2)]
                qn = [_head_norm(q_ref[pl.ds(r0, BLOCK), c].astype(F32), qg, lo)[0] for c in cols]
                q4 = _stack_heads(qn[0], qn[1], lo).astype(BF16)
                p_c, p_p, _ = _att_probs(q4, k2c, k2p, bias_ref[kv, 0], bias_ref[kv, 1], sinkcol_ref[kv], has_prev)
                o4 = _dot_nn(p_c.astype(BF16), v2c) + _dot_nn(p_p.astype(BF16), v2p)
                for c, out in zip(cols, _unstack_heads(o4, lo)):
                    y_ref[pl.ds(r0, BLOCK), c] = out.astype(BF16)
            return carry

        lax.fori_loop(0, nb, block, 0)

    vec = pl.BlockSpec((1, LANES), lambda b: (0, 0))
    return pl.pallas_call(
        body, name=name, grid=(n_seq,),
        in_specs=[pl.BlockSpec((seq, ATT_WIDTH), lambda b: (b, qcol)),
                  pl.BlockSpec((seq, 2 * KV_WIDTH), lambda b: (b, kvcol)),
                  vec, vec, pl.BlockSpec(memory_space=pltpu.SMEM)],
        out_specs=pl.BlockSpec((seq, ATT_WIDTH), lambda b: (b, 0)),
        out_shape=jax.ShapeDtypeStruct((T, ATT_WIDTH), BF16),
        scratch_shapes=list(ATT_SCRATCH),
        compiler_params=_params(("parallel",)),
    )(proj, proj, jnp.tile(q_gain, 2).reshape(1, LANES), jnp.tile(k_gain, 2).reshape(1, LANES), sinks)


def _attention_bwd(proj, dy, q_gain, k_gain, sinks, *, n_seq, seq, name, deps=()):
    T = n_seq * seq
    nb = seq // BLOCK
    qcol, kvcol = COL_QKV // ATT_WIDTH, (COL_QKV + ATT_WIDTH) // (2 * KV_WIDTH)

    def body(q_ref, kv_ref, dy_ref, qg_ref, kg_ref, sink_ref, dqkv_ref, dqg_ref, dkg_ref, dsink_ref,
             dkn_acc, dv_acc, qg_acc, kg_acc, sink_acc, bias_ref, sinkcol_ref):
        lo = _lo_mask((BLOCK, LANES))
        qg, kg = qg_ref[...], kg_ref[...]
        _att_consts(sink_ref, bias_ref, sinkcol_ref)
        first = pl.program_id(0) == 0

        @pl.when(first)
        def _():
            qg_acc[...] = jnp.zeros_like(qg_acc)
            kg_acc[...] = jnp.zeros_like(kg_acc)
            sink_acc[...] = jnp.zeros_like(sink_acc)

        dkn_acc[...] = jnp.zeros_like(dkn_acc)
        dv_acc[...] = jnp.zeros_like(dv_acc)

        def block(i, carry):
            r0 = pl.multiple_of(i * BLOCK, BLOCK)
            rp = pl.multiple_of(jnp.maximum(i - 1, 0) * BLOCK, BLOCK)
            has_prev = i > 0
            kn_c = _head_norm(kv_ref[pl.ds(r0, BLOCK), 0:KV_WIDTH].astype(F32), kg, lo)[0].astype(BF16)
            kn_p = _head_norm(kv_ref[pl.ds(rp, BLOCK), 0:KV_WIDTH].astype(F32), kg, lo)[0].astype(BF16)
            v_c = kv_ref[pl.ds(r0, BLOCK), KV_WIDTH:2 * KV_WIDTH].astype(BF16)
            v_p = kv_ref[pl.ds(rp, BLOCK), KV_WIDTH:2 * KV_WIDTH].astype(BF16)
            dk_c, dk_p, dv_c, dv_p = [], [], [], []
            for kv in range(2):
                k2c, k2p = _dup_half(kn_c, kv, lo), _dup_half(kn_p, kv, lo)
                v2c, v2p = _dup_half(v_c, kv, lo), _dup_half(v_p, kv, lo)
                cols = [slice((2 * kv + t) * LANES, (2 * kv + t + 1) * LANES) for t in range(2)]
                normed = [_head_norm(q_ref[pl.ds(r0, BLOCK), c].astype(F32), qg, lo) for c in cols]
                q4 = _stack_heads(normed[0][0], normed[1][0], lo).astype(BF16)
                do4 = _stack_heads(dy_ref[pl.ds(r0, BLOCK), cols[0]], dy_ref[pl.ds(r0, BLOCK), cols[1]], lo)
                p_c, p_p, p_s = _att_probs(q4, k2c, k2p, bias_ref[kv, 0], bias_ref[kv, 1], sinkcol_ref[kv], has_prev)
                dp_c = _dot_nt(do4, v2c)
                dp_p = _dot_nt(do4, v2p)
                delta = jnp.sum(p_c * dp_c + p_p * dp_p, axis=-1, keepdims=True)
                ds_c = (p_c * (dp_c - delta)).astype(BF16)
                ds_p = (p_p * (dp_p - delta)).astype(BF16)
                sink_acc[kv] += -(p_s * delta)
                dq4 = (_dot_nn(ds_c, k2c) + _dot_nn(ds_p, k2p)) * ATT_SCALE
                for c, (_, qh, qr), dqn in zip(cols, normed, _unstack_heads(dq4, lo)):
                    dq, dg = _head_norm_bwd(qh, qr, qg, dqn, lo)
                    dqkv_ref[pl.ds(r0, BLOCK), c] = dq.astype(BF16)
                    qg_acc[...] += dg
                dk_c.append(_dot_tn(ds_c, q4))
                dk_p.append(_dot_tn(ds_p, q4))
                dv_c.append(_dot_tn(p_c.astype(BF16), do4))
                dv_p.append(_dot_tn(p_p.astype(BF16), do4))

            def fold(parts):
                a = parts[0] + pltpu.roll(parts[0], LANES // 2, axis=1)
                b = parts[1] + pltpu.roll(parts[1], LANES // 2, axis=1)
                return jnp.where(lo, a, b)

            dkn_acc[pl.ds(r0, BLOCK), :] += fold(dk_c) * ATT_SCALE
            dkn_acc[pl.ds(rp, BLOCK), :] += fold(dk_p) * ATT_SCALE
            dv_acc[pl.ds(r0, BLOCK), :] += fold(dv_c)
            dv_acc[pl.ds(rp, BLOCK), :] += fold(dv_p)
            return carry

        lax.fori_loop(0, nb, block, 0)

        def finish(i, carry):
            r0 = pl.multiple_of(i * BLOCK, BLOCK)
            _, kh, kr = _head_norm(kv_ref[pl.ds(r0, BLOCK), 0:KV_WIDTH].astype(F32), kg, lo)
            dk, dg = _head_norm_bwd(kh, kr, kg, dkn_acc[pl.ds(r0, BLOCK), :], lo)
            dqkv_ref[pl.ds(r0, BLOCK), ATT_WIDTH:ATT_WIDTH + KV_WIDTH] = dk.astype(BF16)
            dqkv_ref[pl.ds(r0, BLOCK), ATT_WIDTH + KV_WIDTH:QKV_WIDTH] = dv_acc[pl.ds(r0, BLOCK), :].astype(BF16)
            kg_acc[...] += dg
            return carry

        lax.fori_loop(0, nb, finish, 0)

        @pl.when(pl.program_id(0) == n_seq - 1)
        def _():
            dqg_ref[...] = jnp.sum(qg_acc[...], axis=0, keepdims=True)
            dkg_ref[...] = jnp.sum(kg_acc[...], axis=0, keepdims=True)
            lane = lax.broadcasted_iota(jnp.int32, (1, LANES), 1)
            dsink = jnp.zeros((1, LANES), F32)
            for kv in range(2):
                for r in range(Q_GROUP):
                    total = jnp.sum(sink_acc[kv, r * BLOCK:(r + 1) * BLOCK, :], axis=0, keepdims=True)
                    dsink = jnp.where(lane == Q_GROUP * kv + r, total, dsink)
            dsink_ref[...] = dsink

    vec = pl.BlockSpec((1, LANES), lambda b: (0, 0))
    acc = pltpu.VMEM((BLOCK, LANES), F32)
    body, dep_specs, dep_args = _with_deps(body, 6, deps)
    dqkv, dqg, dkg, dsink = pl.pallas_call(
        body, name=name, grid=(n_seq,),
        in_specs=[pl.BlockSpec((seq, ATT_WIDTH), lambda b: (b, qcol)),
                  pl.BlockSpec((seq, 2 * KV_WIDTH), lambda b: (b, kvcol)),
                  pl.BlockSpec((seq, ATT_WIDTH), lambda b: (b, 0)),
                  vec, vec, pl.BlockSpec(memory_space=pltpu.SMEM)] + dep_specs,
        out_specs=[pl.BlockSpec((seq, QKV_WIDTH), lambda b: (b, 0)), vec, vec, vec],
        out_shape=[jax.ShapeDtypeStruct((T, QKV_WIDTH), BF16)] + [jax.ShapeDtypeStruct((1, LANES), F32)] * 3,
        scratch_shapes=[pltpu.VMEM((seq, KV_WIDTH), F32), pltpu.VMEM((seq, KV_WIDTH), F32), acc, acc,
                        pltpu.VMEM((2, GROUP_ROWS, 1), F32), *ATT_SCRATCH],
        compiler_params=_params(("arbitrary",)),
    )(proj, proj, dy, jnp.tile(q_gain, 2).reshape(1, LANES), jnp.tile(k_gain, 2).reshape(1, LANES), sinks, *dep_args)
    half = LANES // 2
    return dqkv, dqg[0, :half] + dqg[0, half:], dkg[0, :half] + dkg[0, half:], dsink[0, :N_Q_HEADS]


def _sgu_weights(w_ref):
    r = lax.broadcasted_iota(jnp.int32, (BLOCK, BLOCK), 0)
    c = lax.broadcasted_iota(jnp.int32, (BLOCK, BLOCK), 1)
    return [jnp.where(r >= c, w_ref[g], 0.0).astype(BF16) for g in range(SGU_GROUPS)]


def _sgu_fwd(proj, gain, w_s, bias_full, *, n_seq, seq, name):
    T = n_seq * seq
    nc = seq // BLOCK

    def body(suv_ref, g_ref, w_ref, b_ref, y_ref):
        lo = _lo_mask((BLOCK, LANES))
        wm = _sgu_weights(w_ref)
        gain_v = g_ref[...]

        def chunk(c, carry):
            r0 = pl.multiple_of(c * BLOCK, BLOCK)
            gv = _gelu(suv_ref[pl.ds(r0, BLOCK), SGU_WIDTH:2 * SGU_WIDTH].astype(F32))
            r = lax.rsqrt(jnp.mean(gv * gv, axis=-1, keepdims=True) + NORM_EPS)
            vn = (gv * r * gain_v).astype(BF16)
            for p in range(SGU_WIDTH // LANES):
                cols = slice(p * LANES, (p + 1) * LANES)
                vp = vn[:, cols]
                mixed = jnp.where(lo, _dot_nn(wm[2 * p], vp), _dot_nn(wm[2 * p + 1], vp)) + b_ref[:, cols]
                u = _gelu(suv_ref[pl.ds(r0, BLOCK), cols].astype(F32))
                y_ref[pl.ds(r0, BLOCK), cols] = (u * mixed).astype(BF16)
            return carry

        lax.fori_loop(0, nc, chunk, 0)

    return pl.pallas_call(
        body, name=name, grid=(n_seq,),
        in_specs=[pl.BlockSpec((seq, 2 * SGU_WIDTH), lambda b: (b, COL_SUV // (2 * SGU_WIDTH))),
                  pl.BlockSpec((1, SGU_WIDTH), lambda b: (0, 0)),
                  pl.BlockSpec((SGU_GROUPS, BLOCK, BLOCK), lambda b: (0, 0, 0)),
                  pl.BlockSpec((BLOCK, SGU_WIDTH), lambda b: (0, 0))],
        out_specs=pl.BlockSpec((seq, SGU_WIDTH), lambda b: (b, 0)),
        out_shape=jax.ShapeDtypeStruct((T, SGU_WIDTH), BF16),
        compiler_params=_params(("parallel",)),
    )(proj, gain.reshape(1, SGU_WIDTH), w_s, bias_full)


def _sgu_bwd(proj, dy, gain, w_s, bias_full, *, n_seq, seq, name, deps=()):
    T = n_seq * seq
    nc = seq // BLOCK
    n_tiles = SGU_WIDTH // LANES

    def body(suv_ref, dy_ref, g_ref, w_ref, b_ref, dsuv_ref, dg_ref, dw_ref, db_ref, dg_acc, dw_acc, db_acc):
        lo = _lo_mask((BLOCK, LANES))
        hi = jnp.logical_not(lo)
        wm = _sgu_weights(w_ref)
        wmt = [jnp.where(lax.broadcasted_iota(jnp.int32, (BLOCK, BLOCK), 1) >= lax.broadcasted_iota(jnp.int32, (BLOCK, BLOCK), 0),
                         w_ref[g].T, 0.0).astype(BF16) for g in range(SGU_GROUPS)]
        gain_v = g_ref[...]

        @pl.when(pl.program_id(0) == 0)
        def _():
            dg_acc[...] = jnp.zeros_like(dg_acc)
            dw_acc[...] = jnp.zeros_like(dw_acc)
            db_acc[...] = jnp.zeros_like(db_acc)

        def chunk(c, carry):
            r0 = pl.multiple_of(c * BLOCK, BLOCK)
            gv, dgelu_v = _gelu_and_grad(suv_ref[pl.ds(r0, BLOCK), SGU_WIDTH:2 * SGU_WIDTH].astype(F32))
            r = lax.rsqrt(jnp.mean(gv * gv, axis=-1, keepdims=True) + NORM_EPS)
            vh = gv * r
            vn = (vh * gain_v).astype(BF16)
            dvn_tiles = []
            for p in range(n_tiles):
                cols = slice(p * LANES, (p + 1) * LANES)
                vp = vn[:, cols]
                mixed = jnp.where(lo, _dot_nn(wm[2 * p], vp), _dot_nn(wm[2 * p + 1], vp)) + b_ref[:, cols]
                u, dgelu_u = _gelu_and_grad(suv_ref[pl.ds(r0, BLOCK), cols].astype(F32))
                dyv = dy_ref[pl.ds(r0, BLOCK), cols]
                dsuv_ref[pl.ds(r0, BLOCK), cols] = (dyv * mixed * dgelu_u).astype(BF16)
                dm = dyv * u
                db_acc[:, cols] += dm
                dm_bf = dm.astype(BF16)
                dvn_tiles.append(jnp.where(lo, _dot_nn(wmt[2 * p], dm_bf), _dot_nn(wmt[2 * p + 1], dm_bf)))
                dw_acc[2 * p] += _dot_nt(jnp.where(lo, dm, 0.0).astype(BF16), vp)
                dw_acc[2 * p + 1] += _dot_nt(jnp.where(hi, dm, 0.0).astype(BF16), vp)
            dvn = jnp.concatenate(dvn_tiles, axis=1)
            dg_acc[...] += dvn * vh
            dvh = dvn * gain_v
            dgv = r * (dvh - vh * jnp.mean(dvh * vh, axis=-1, keepdims=True))
            dsuv_ref[pl.ds(r0, BLOCK), SGU_WIDTH:2 * SGU_WIDTH] = (dgv * dgelu_v).astype(BF16)
            return carry

        lax.fori_loop(0, nc, chunk, 0)

        @pl.when(pl.program_id(0) == n_seq - 1)
        def _():
            dg_ref[...] = jnp.sum(dg_acc[...], axis=0, keepdims=True)
            r = lax.broadcasted_iota(jnp.int32, (BLOCK, BLOCK), 0)
            c = lax.broadcasted_iota(jnp.int32, (BLOCK, BLOCK), 1)
            for g in range(SGU_GROUPS):
                dw_ref[g] = jnp.where(r >= c, dw_acc[g], 0.0)
            lane = lax.broadcasted_iota(jnp.int32, (BLOCK, LANES), 1)
            out = jnp.zeros((BLOCK, LANES), F32)
            for p in range(n_tiles):
                tile = db_acc[:, p * LANES:(p + 1) * LANES]
                s_lo = jnp.sum(jnp.where(lo, tile, 0.0), axis=-1, keepdims=True)
                s_hi = jnp.sum(jnp.where(hi, tile, 0.0), axis=-1, keepdims=True)
                out = jnp.where(lane == 2 * p, s_lo, out)
                out = jnp.where(lane == 2 * p + 1, s_hi, out)
            db_ref[...] = out

    body, dep_specs, dep_args = _with_deps(body, 5, deps)
    dsuv, dg, dw, db = pl.pallas_call(
        body, name=name, grid=(n_seq,),
        in_specs=[pl.BlockSpec((seq, 2 * SGU_WIDTH), lambda b: (b, COL_SUV // (2 * SGU_WIDTH))),
                  pl.BlockSpec((seq, SGU_WIDTH), lambda b: (b, 0)),
                  pl.BlockSpec((1, SGU_WIDTH), lambda b: (0, 0)),
                  pl.BlockSpec((SGU_GROUPS, BLOCK, BLOCK), lambda b: (0, 0, 0)),
                  pl.BlockSpec((BLOCK, SGU_WIDTH), lambda b: (0, 0))] + dep_specs,
        out_specs=[pl.BlockSpec((seq, 2 * SGU_WIDTH), lambda b: (b, 0)),
                   pl.BlockSpec((1, SGU_WIDTH), lambda b: (0, 0)),
                   pl.BlockSpec((SGU_GROUPS, BLOCK, BLOCK), lambda b: (0, 0, 0)),
                   pl.BlockSpec((BLOCK, LANES), lambda b: (0, 0))],
        out_shape=[jax.ShapeDtypeStruct((T, 2 * SGU_WIDTH), BF16), jax.ShapeDtypeStruct((1, SGU_WIDTH), F32),
                   jax.ShapeDtypeStruct((SGU_GROUPS, BLOCK, BLOCK), F32), jax.ShapeDtypeStruct((BLOCK, LANES), F32)],
        scratch_shapes=[pltpu.VMEM((BLOCK, SGU_WIDTH), F32), pltpu.VMEM((SGU_GROUPS, BLOCK, BLOCK), F32),
                        pltpu.VMEM((BLOCK, SGU_WIDTH), F32)],
        compiler_params=_params(("arbitrary",)),
    )(proj, dy, gain.reshape(1, SGU_WIDTH), w_s, bias_full, *dep_args)
    return dsuv, dg.reshape(SGU_WIDTH), dw, db[:, :SGU_GROUPS].T


def _merge_fwd(y_att, y_sgu, w_oa, w_ob, proj, *, name, tm=1024, tn=512, deps=()):
    T = y_att.shape[0]

    def body(ya_ref, ys_ref, wa_ref, wb_ref, ga_ref, gb_ref, o_ref):
        pa = _dot_nn(ya_ref[...], wa_ref[...])
        pb = _dot_nn(ys_ref[...], wb_ref[...])
        o_ref[...] = (_sigmoid(ga_ref[...].astype(F32)) * pa + _sigmoid(gb_ref[...].astype(F32)) * pb).astype(BF16)

    act = pl.BlockSpec((tm, ATT_WIDTH), lambda i, j: (i, 0))
    wgt = pl.BlockSpec((ATT_WIDTH, tn), lambda i, j: (0, j))
    body, dep_specs, dep_args = _with_deps(body, 6, deps)
    return pl.pallas_call(
        body, name=name, grid=(T // tm, D_MODEL // tn),
        in_specs=[act, act, wgt, wgt,
                  pl.BlockSpec((tm, tn), lambda i, j: (i, j + COL_GA // tn)),
                  pl.BlockSpec((tm, tn), lambda i, j: (i, j + COL_GB // tn))] + dep_specs,
        out_specs=pl.BlockSpec((tm, tn), lambda i, j: (i, j)),
        out_shape=jax.ShapeDtypeStruct((T, D_MODEL), BF16),
        compiler_params=_params(("parallel", "parallel")),
    )(y_att, y_sgu, w_oa, w_ob, proj, proj, *dep_args)


def _merge_bwd(dx1_bf, w_out, y_att, y_sgu, w_oa, w_ob, proj, *, name, tm=1024, tn=512):
    T = y_att.shape[0]

    def body(dx_ref, wo_ref, ya_ref, ys_ref, wa_ref, wb_ref, ga_ref, gb_ref, dpa_ref, dpb_ref, dga_ref, dgb_ref):
        dm = _dot_nt(dx_ref[...], wo_ref[...])
        pa = _dot_nn(ya_ref[...], wa_ref[...])
        pb = _dot_nn(ys_ref[...], wb_ref[...])
        sa = _sigmoid(ga_ref[...].astype(F32))
        sb = _sigmoid(gb_ref[...].astype(F32))
        dpa_ref[...] = (dm * sa).astype(BF16)
        dpb_ref[...] = (dm * sb).astype(BF16)
        dga_ref[...] = (dm * pa * sa * (1.0 - sa)).astype(BF16)
        dgb_ref[...] = (dm * pb * sb * (1.0 - sb)).astype(BF16)

    act = pl.BlockSpec((tm, ATT_WIDTH), lambda i, j: (i, 0))
    wgt = pl.BlockSpec((ATT_WIDTH, tn), lambda i, j: (0, j))
    out = pl.BlockSpec((tm, tn), lambda i, j: (i, j))
    return pl.pallas_call(
        body, name=name, grid=(T // tm, D_MODEL // tn),
        in_specs=[pl.BlockSpec((tm, D_MODEL), lambda i, j: (i, 0)),
                  pl.BlockSpec((tn, D_MODEL), lambda i, j: (j, 0)),
                  act, act, wgt, wgt,
                  pl.BlockSpec((tm, tn), lambda i, j: (i, j + COL_GA // tn)),
                  pl.BlockSpec((tm, tn), lambda i, j: (i, j + COL_GB // tn))],
        out_specs=[out] * 4,
        out_shape=[jax.ShapeDtypeStruct((T, D_MODEL), BF16)] * 4,
        compiler_params=_params(("parallel", "parallel")),
    )(dx1_bf, w_out, y_att, y_sgu, w_oa, w_ob, proj, proj)


CONV_ROWS = 256
CONV_TN = 256


def _shift_rows(cur, prev8, k):
    rolled = pltpu.roll(cur, k, axis=0)
    head = jnp.where(lax.broadcasted_iota(jnp.int32, prev8.shape, 0) < k, pltpu.roll(prev8, k, axis=0), rolled[:SUBLANES])
    return jnp.concatenate([head, rolled[SUBLANES:]], axis=0)


def _shift_rows_up(cur, next8, k):
    n = cur.shape[0]
    rolled = pltpu.roll(cur, n - k, axis=0)
    tail = jnp.where(lax.broadcasted_iota(jnp.int32, next8.shape, 0) >= SUBLANES - k,
                     pltpu.roll(next8, SUBLANES - k, axis=0), rolled[n - SUBLANES:])
    return jnp.concatenate([rolled[:n - SUBLANES], tail], axis=0)


HALO_ROWS = 16


def _rows_before(z_ref, r0, first):
    rp = pl.multiple_of(jnp.maximum(r0 - HALO_ROWS, 0), HALO_ROWS)
    halo = z_ref[pl.ds(rp, HALO_ROWS), :].astype(F32)
    return jnp.where(first, 0.0, halo[HALO_ROWS - SUBLANES:])


def _conv_rows(z_ref, r0, first, w_ref, b_ref, rows):
    cur = z_ref[pl.ds(r0, rows), :].astype(F32)
    prev8 = _rows_before(z_ref, r0, first)
    z1 = _shift_rows(cur, prev8, 1)
    z2 = _shift_rows(cur, prev8, 2)
    return b_ref[...] + w_ref[0:1, :] * z2 + w_ref[1:2, :] * z1 + w_ref[2:3, :] * cur


def _conv_fwd(z_g, z_v, cw_g, cw_v, cb_g, cb_v, *, n_seq, seq, name):
    T = n_seq * seq
    tn, rows = CONV_TN, CONV_ROWS

    def body(zg_ref, zv_ref, wg_ref, wv_ref, bg_ref, bv_ref, a_ref, cg_ref, cv_ref):
        def step(s, carry):
            r0 = pl.multiple_of(s * rows, rows)
            first = s == 0
            g = _conv_rows(zg_ref, r0, first, wg_ref, bg_ref, rows)
            v = _conv_rows(zv_ref, r0, first, wv_ref, bv_ref, rows)
            a_ref[pl.ds(r0, rows), :] = (g * _sigmoid(g) * v).astype(BF16)
            cg_ref[pl.ds(r0, rows), :] = g.astype(ACT_DTYPE)
            cv_ref[pl.ds(r0, rows), :] = v.astype(ACT_DTYPE)
            return carry

        lax.fori_loop(0, seq // rows, step, 0)

    zs = pl.BlockSpec((seq, tn), lambda b, j: (b, j))
    ws = pl.BlockSpec((3, tn), lambda b, j: (0, j))
    bs = pl.BlockSpec((1, tn), lambda b, j: (0, j))
    return pl.pallas_call(
        body, name=name, grid=(n_seq, D_FF // tn),
        in_specs=[zs, zs, ws, ws, bs, bs], out_specs=[zs] * 3,
        out_shape=[jax.ShapeDtypeStruct((T, D_FF), BF16)] + [jax.ShapeDtypeStruct((T, D_FF), ACT_DTYPE)] * 2,
        compiler_params=_params(("parallel", "parallel")),
    )(z_g, z_v, cw_g, cw_v, cb_g.reshape(1, D_FF), cb_v.reshape(1, D_FF))


def _conv_bwd(z_g, z_v, c_g, c_v, da, cw_g, cw_v, *, n_seq, seq, name):
    T = n_seq * seq
    tn, rows = CONV_TN, CONV_ROWS
    n_steps = seq // rows

    def body(zg_ref, zv_ref, cg_ref, cv_ref, da_ref, wg_ref, wv_ref,
             dzg_ref, dzv_ref, dwg_ref, dwv_ref, dbg_ref, dbv_ref, dcg_ref, dcv_ref):
        def colsum(x):
            return jnp.sum(x, axis=0, keepdims=True)

        def grads(s, accs):
            r0 = pl.multiple_of(s * rows, rows)
            g = cg_ref[pl.ds(r0, rows), :].astype(F32)
            v = cv_ref[pl.ds(r0, rows), :].astype(F32)
            sg = _sigmoid(g)
            dav = da_ref[pl.ds(r0, rows), :].astype(F32)
            dcg = dav * v * (sg * (1.0 + g * (1.0 - sg)))
            dcv = dav * (g * sg)
            dcg_ref[pl.ds(r0, rows), :] = dcg
            dcv_ref[pl.ds(r0, rows), :] = dcv
            return accs[0] + colsum(dcg), accs[1] + colsum(dcv)

        zero = jnp.zeros((1, tn), F32)
        db = lax.fori_loop(0, n_steps, grads, (zero, zero))

        def back(s, accs):
            r0 = pl.multiple_of(s * rows, rows)
            last = s == n_steps - 1
            rn = pl.multiple_of(jnp.minimum(r0 + rows, seq - SUBLANES), SUBLANES)
            new = []
            for half, (dc_ref, w_ref, dz_ref, z_ref) in enumerate(((dcg_ref, wg_ref, dzg_ref, zg_ref),
                                                                   (dcv_ref, wv_ref, dzv_ref, zv_ref))):
                cur = dc_ref[pl.ds(r0, rows), :]
                nxt = jnp.where(last, 0.0, dc_ref[pl.ds(rn, SUBLANES), :])
                u1, u2 = _shift_rows_up(cur, nxt, 1), _shift_rows_up(cur, nxt, 2)
                dz_ref[pl.ds(r0, rows), :] = (w_ref[2:3, :] * cur + w_ref[1:2, :] * u1 + w_ref[0:1, :] * u2).astype(BF16)
                z = z_ref[pl.ds(r0, rows), :].astype(F32)
                new += [accs[3 * half] + colsum(u2 * z), accs[3 * half + 1] + colsum(u1 * z),
                        accs[3 * half + 2] + colsum(cur * z)]
            return tuple(new)

        dw = lax.fori_loop(0, n_steps, back, (zero,) * 6)
        first_seq = pl.program_id(1) == 0

        @pl.when(first_seq)
        def _():
            dwg_ref[...] = jnp.concatenate(dw[0:3], axis=0)
            dwv_ref[...] = jnp.concatenate(dw[3:6], axis=0)
            dbg_ref[...], dbv_ref[...] = db

        @pl.when(jnp.logical_not(first_seq))
        def _():
            dwg_ref[...] += jnp.concatenate(dw[0:3], axis=0)
            dwv_ref[...] += jnp.concatenate(dw[3:6], axis=0)
            dbg_ref[...] += db[0]
            dbv_ref[...] += db[1]

    zs = pl.BlockSpec((seq, tn), lambda j, b: (b, j))
    ws = pl.BlockSpec((3, tn), lambda j, b: (0, j))
    bs = pl.BlockSpec((1, tn), lambda j, b: (0, j))
    outs = pl.pallas_call(
        body, name=name, grid=(D_FF // tn, n_seq),
        in_specs=[zs] * 5 + [ws, ws],
        out_specs=[zs, zs, ws, ws, bs, bs],
        out_shape=[jax.ShapeDtypeStruct((T, D_FF), BF16)] * 2 + [jax.ShapeDtypeStruct((3, D_FF), F32)] * 2
        + [jax.ShapeDtypeStruct((1, D_FF), F32)] * 2,
        scratch_shapes=[pltpu.VMEM((seq, tn), F32), pltpu.VMEM((seq, tn), F32)],
        compiler_params=_params(("parallel", "arbitrary")),
    )(z_g, z_v, c_g, c_v, da, cw_g, cw_v)
    dz_g, dz_v, dw_g, dw_v, db_g, db_v = outs
    return dz_g, dz_v, dw_g, dw_v, db_g.reshape(D_FF), db_v.reshape(D_FF)


def _layer_fwd(x, h, w, sched, tail, *, n_seq, seq, l):
    tag = f"l{l}"
    deps = sched("fwd_start", l, x)
    proj = _mm(h, w["w_in_t"], mode="nt", out_dtype=ACT_DTYPE, rotate=W_IN_ROTATE, name=f"{tag}_proj", deps=deps)
    y_att = _attention_fwd(proj, w["q_norm"], w["k_norm"], w["sinks"], n_seq=n_seq, seq=seq, name=f"{tag}_att")
    deps = sched("fwd_att", l, y_att)
    y_sgu = _sgu_fwd(proj, w["sgu_norm"], w["w_s"], w["bias_full"], n_seq=n_seq, seq=seq, name=f"{tag}_sgu")
    merged = _merge_fwd(y_att, y_sgu, w["w_oa"], w["w_ob"], proj, name=f"{tag}_merge", deps=deps)
    x1, h2 = _mm_rows(merged, w["w_out"], mode="nn", fn=_residual_then_norm, out_dtypes=(F32, BF16), rows=(x,),
                      vecs=(w["ffn_norm"],), name=f"{tag}_out")
    deps = sched("fwd_mixer_done", l, x1)
    z_g = _mm(h2, w["w_up_t"], mode="nt", out_dtype=ACT_DTYPE, b_rows=(0, D_FF), name=f"{tag}_up_g", deps=deps)
    z_v = _mm(h2, w["w_up_t"], mode="nt", out_dtype=ACT_DTYPE, b_rows=(D_FF, D_FF), name=f"{tag}_up_v")
    a, c_g, c_v = _conv_fwd(z_g, z_v, w["cw_g"], w["cw_v"], w["cb_g"], w["cb_v"], n_seq=n_seq, seq=seq,
                            name=f"{tag}_conv")
    deps = sched("fwd_conv", l, a)
    if tail[0] == "norm":
        out = _mm_rows(a, w["w_down"], mode="nn", fn=_residual_then_norm, out_dtypes=(F32, BF16), rows=(x1,),
                       vecs=(tail[1],), name=f"{tag}_down", deps=deps)
    else:
        out = _mm_rows(a, w["w_down"], mode="nn", fn=_residual_then_loss, out_dtypes=(F32, BF16), rows=(x1, tail[1]),
                       reduce=True, name=f"{tag}_down", deps=deps)
    saved = dict(x=x, h=h, proj=proj, y_att=y_att, y_sgu=y_sgu, merged=merged, x1=x1, h2=h2, z_g=z_g, z_v=z_v,
                 c_g=c_g, c_v=c_v, a=a)
    return out, saved


def _layer_bwd(dx2, dx2_bf, w, s, sched, deps, *, n_seq, seq, l):
    tag = f"l{l}b"
    g = {}
    da = _mm(dx2_bf, w["w_down"], mode="nt", out_dtype=ACT_DTYPE, name=f"{tag}_da", deps=deps)
    g["w_down"] = _mm(s["a"], dx2_bf, mode="tn", out_dtype=F32, name=f"{tag}_dw_down")
    dz_g, dz_v, g["cw_g"], g["cw_v"], g["cb_g"], g["cb_v"] = _conv_bwd(
        s["z_g"], s["z_v"], s["c_g"], s["c_v"], da, w["cw_g"], w["cw_v"], n_seq=n_seq, seq=seq, name=f"{tag}_conv")
    dw_up_t = _mm(dz_g, s["h2"], mode="tn", out_dtype=F32, out_rows=(0, 2 * D_FF), name=f"{tag}_dw_up_g")
    g["w_up_t"] = _mm(dz_v, s["h2"], mode="tn", out_dtype=F32, out_rows=(D_FF, 2 * D_FF), out_prev=dw_up_t,
                      name=f"{tag}_dw_up_v")
    deps = sched("bwd_ffn_grads", l, dz_v, g)
    dx1, dx1_bf, dgain = _mm_rows((dz_g, dz_v), w["w_up_t"], mode="nn", fn=_rms_bwd_rows, out_dtypes=(F32, BF16),
                                  rows=(s["x1"], dx2), vecs=(w["ffn_norm"],), reduce=True, a_at=(0, D_FF),
                                  name=f"{tag}_dh2", deps=deps)
    g["ffn_norm"] = dgain.reshape(D_MODEL)
    dpa, dpb, dga, dgb = _merge_bwd(dx1_bf, w["w_out"], s["y_att"], s["y_sgu"], w["w_oa"], w["w_ob"], s["proj"],
                                    name=f"{tag}_merge")
    deps = sched("bwd_merge", l, dpa)
    g["w_out"] = _mm(s["merged"], dx1_bf, mode="tn", out_dtype=F32, name=f"{tag}_dw_out",
                     deps=deps)
    dy_att = _mm(dpa, w["w_oa"], mode="nt", out_dtype=BF16, name=f"{tag}_dy_att")
    dy_sgu = _mm(dpb, w["w_ob"], mode="nt", out_dtype=F32, name=f"{tag}_dy_sgu")
    g["w_oa"] = _mm(s["y_att"], dpa, mode="tn", out_dtype=F32, name=f"{tag}_dw_oa")
    g["w_ob"] = _mm(s["y_sgu"], dpb, mode="tn", out_dtype=F32, name=f"{tag}_dw_ob")
    deps = sched("bwd_out_grads", l, dy_att, g)
    dqkv, g["q_norm"], g["k_norm"], g["sinks"] = _attention_bwd(
        s["proj"], dy_att, w["q_norm"], w["k_norm"], w["sinks"], n_seq=n_seq, seq=seq, name=f"{tag}_att", deps=deps)
    deps = sched("bwd_att", l, dqkv)
    dsuv, g["sgu_norm"], g["w_s"], g["b_s"] = _sgu_bwd(
        s["proj"], dy_sgu, w["sgu_norm"], w["w_s"], w["bias_full"], n_seq=n_seq, seq=seq, name=f"{tag}_sgu", deps=deps)
    dproj = (dsuv, dga, dgb, dqkv)
    at = (QKV_WIDTH, QKV_WIDTH + 2 * SGU_WIDTH, QKV_WIDTH + 2 * SGU_WIDTH + D_MODEL, 0)
    dw = None
    for i, (piece, first) in enumerate(zip(dproj, at)):
        dw = _mm(piece, s["h"], mode="tn", out_dtype=F32, out_rows=(first, IN_WIDTH), out_prev=dw, name=f"{tag}_dw_in_{i}")
    g["w_in_t"] = dw
    deps = sched("bwd_w_in_grad", l, dqkv, g)
    dx, dx_bf, dgain = _mm_rows(dproj, w["w_in_t"], mode="nn", fn=_rms_bwd_rows, out_dtypes=(F32, BF16),
                                rows=(s["x"], dx1), vecs=(w["mix_norm"],), reduce=True, a_at=at,
                                name=f"{tag}_dh", deps=deps)
    g["mix_norm"] = dgain.reshape(D_MODEL)
    return dx, dx_bf, g, sched("bwd_dh", l, dx)


def _local_step(x, target, weights, sched, *, n_seq, seq):
    depth = len(weights)
    saved = []
    h = _rms_fwd(x, weights[0]["mix_norm"], name="l0_mix_norm")
    for l in range(depth):
        tail = ("norm", weights[l + 1]["mix_norm"]) if l + 1 < depth else ("loss", target)
        out, s = _layer_fwd(x, h, weights[l], sched, tail, n_seq=n_seq, seq=seq, l=l)
        saved.append(s)
        if l + 1 < depth:
            x, h = out
    dy, dy_bf, loss_cols = out
    grads = [None] * depth
    deps = ()
    for l in reversed(range(depth)):
        dy, dy_bf, grads[l], deps = _layer_bwd(dy, dy_bf, weights[l], saved[l], sched, deps, n_seq=n_seq, seq=seq, l=l)
    return jnp.sum(loss_cols), dy, grads, deps


W_IN_SHARD = IN_WIDTH // N_DEV
W_UP_SHARD = 2 * D_FF // N_DEV
COL_MOVE_ROWS = 256


def _w_o_moves():
    return tuple((j, 0, LANES, 0, j * LANES) for j in range(N_DEV))


def _disassemble(mats, w, moves, *, name):
    R = mats[0].shape[0]
    tr = min(R, COL_MOVE_ROWS)
    n = len(mats)

    def body(*refs):
        m_refs, o_ref = refs[:n], refs[n]
        for j, lo, hi, which, at in moves:
            o_ref[j, :, lo:hi] = m_refs[which][:, at:at + hi - lo]

    return pl.pallas_call(
        body, name=name, grid=(R // tr,),
        in_specs=[pl.BlockSpec((tr, m.shape[1]), lambda i: (i, 0)) for m in mats],
        out_specs=pl.BlockSpec((N_DEV, tr, w), lambda i: (0, i, 0)),
        out_shape=jax.ShapeDtypeStruct((N_DEV, R, w), mats[0].dtype),
        compiler_params=_params(("parallel",)),
    )(*mats)


def _my_place():
    return lax.axis_index("x"), lax.axis_index("y"), lax.axis_index("c")


def _gathered_shape(shape, kind):
    r, c = shape
    return {"blocks": (N_DEV, r, c), "rows": (N_DEV * r, c), "cols": (r, N_DEV * c)}[kind]


def _gather_window(ref, kind, shape, j):
    r, c = shape
    if kind == "blocks":
        return ref.at[j]
    if kind == "rows":
        return ref.at[pl.ds(pl.multiple_of(j * r, r), r), :]
    return ref.at[:, pl.ds(pl.multiple_of(j * c, c), c)]


def _gather(srcs, kinds, *, name):
    n = len(srcs)
    shapes = [s.shape for s in srcs]
    per = 7

    def body(*refs):
        src_refs, dst_refs = refs[:n], refs[n:2 * n]
        send_sems, recv_sems, local_sems = refs[2 * n:]
        x, y, c = _my_place()
        me, sibling = (x, y, c), (x, y, 1 - c)
        chips = [(1 - x, y), (x, 1 - y), (1 - x, 1 - y)]

        def at(i, px, py, pc):
            return _gather_window(dst_refs[i], kinds[i], shapes[i], 4 * px + 2 * py + pc)

        def copy(i, k, block, to, src=None):
            return pltpu.make_async_remote_copy(
                src_ref=at(i, *block) if src is None else src, dst_ref=at(i, *block),
                send_sem=send_sems.at[per * i + k], recv_sem=recv_sems.at[per * i + k], device_id=to, device_id_type=MESH)

        mine = [pltpu.make_async_copy(src_refs[i], at(i, *me), local_sems.at[i]) for i in range(n)]
        for cp in mine:
            cp.start()
        started = []
        for i in range(n):
            first = [copy(i, 0, me, sibling, src=src_refs[i])]
            first += [copy(i, 1 + j, me, (*chip, c), src=src_refs[i]) for j, chip in enumerate(chips)]
            for cp in first:
                cp.start()
            started += first
        for i in range(n):
            for j, chip in enumerate(chips):
                copy(i, 1 + j, (*chip, c), me).wait_recv()
                fwd = copy(i, 4 + j, (*chip, c), sibling)
                fwd.start()
                started.append(fwd)
        for i in range(n):
            copy(i, 0, sibling, me).wait_recv()
            for j, chip in enumerate(chips):
                copy(i, 4 + j, (*chip, 1 - c), me).wait_recv()
        for cp in started:
            cp.wait_send()
        for cp in mine:
            cp.wait()

    return pl.pallas_call(
        body, name=name,
        out_shape=[jax.ShapeDtypeStruct(_gathered_shape(s.shape, k), s.dtype) for s, k in zip(srcs, kinds)],
        in_specs=[ANY] * n, out_specs=[ANY] * n,
        scratch_shapes=[pltpu.SemaphoreType.DMA((per * n,)), pltpu.SemaphoreType.DMA((per * n,)),
                        pltpu.SemaphoreType.DMA((n,))],
    )(*srcs)


HBM = pl.BlockSpec(memory_space=pltpu.HBM)
SEM = pl.BlockSpec(memory_space=pltpu.SEMAPHORE)
TOKEN = jax.ShapeDtypeStruct((SUBLANES, LANES), F32)
TOKEN_SPEC = pl.BlockSpec(memory_space=pltpu.VMEM)
SPLIT_PARAMS = pltpu.CompilerParams(has_side_effects=pltpu.SideEffectType.DATAFLOW_SIDE_EFFECTING)


def _in_hbm(x):
    return pltpu.with_memory_space_constraint(x, pltpu.HBM)


def _hbm_like(shape, dtype):
    return pltpu.HBM(shape, dtype)


def _place_own(shards, kinds, dtypes, *, name):
    n = len(shards)
    shapes = [s.shape for s in shards]

    def body(*refs):
        s_refs, land_refs, bufs, sems = refs[:n], refs[n:2 * n], refs[2 * n:3 * n], refs[3 * n]
        x, y, c = _my_place()
        copies = []
        for i in range(n):
            bufs[i][...] = s_refs[i][...].astype(dtypes[i])
            copies.append(pltpu.make_async_copy(
                bufs[i], _gather_window(land_refs[i], kinds[i], shapes[i], 4 * x + 2 * y + c), sems.at[i]))
        for cp in copies:
            cp.start()
        for cp in copies:
            cp.wait()

    return pl.pallas_call(
        body, name=name,
        out_shape=[jax.ShapeDtypeStruct(_gathered_shape(s, k), d) for s, k, d in zip(shapes, kinds, dtypes)],
        in_specs=[pl.BlockSpec(memory_space=pltpu.VMEM)] * n, out_specs=[ANY] * n,
        scratch_shapes=[pltpu.VMEM(s, d) for s, d in zip(shapes, dtypes)] + [pltpu.SemaphoreType.DMA((n,))],
        compiler_params=_params(),
    )(*shards)


def _gather_start(lands, kinds, shapes, after=(), *, name):
    n = len(lands)
    n_after = len(after)

    def body(*refs):
        land_refs = refs[:n]
        send_sems, recv_sems = refs[n + n_after], refs[n + n_after + 1]
        x, y, c = _my_place()
        targets = [(x, y, 1 - c), (1 - x, y, c), (x, 1 - y, c), (1 - x, 1 - y, c)]
        for i in range(n):
            own = _gather_window(land_refs[i], kinds[i], shapes[i], 4 * x + 2 * y + c)
            for k, to in enumerate(targets):
                pltpu.make_async_remote_copy(
                    src_ref=own, dst_ref=own, send_sem=send_sems.at[4 * i + k], recv_sem=recv_sems.at[4 * i + k],
                    device_id=to, device_id_type=MESH).start()
        refs[-1][...] = jnp.zeros_like(refs[-1])

    outs = pl.pallas_call(
        body, name=name,
        out_shape=[pltpu.SemaphoreType.DMA((4 * n,)), pltpu.SemaphoreType.DMA((4 * n,))]
        + [_hbm_like(a.shape, a.dtype) for a in lands] + [TOKEN],
        in_specs=[HBM] * n + [ANY] * n_after, out_specs=[SEM, SEM] + [HBM] * n + [TOKEN_SPEC],
        input_output_aliases={i: 2 + i for i in range(n)},
        compiler_params=SPLIT_PARAMS,
    )(*[_in_hbm(a) for a in lands], *after)
    return outs[0], outs[1], outs[2:2 + n], outs[-1]


def _gather_forward(recv_sems, lands, kinds, shapes, after, *, name):
    n = len(lands)

    def body(*refs):
        recv_ref, land_refs = refs[0], refs[1:1 + n]
        fwd_send, fwd_recv = refs[2 + n], refs[3 + n]
        token = refs[-1]
        x, y, c = _my_place()
        chips = [(1 - x, y), (x, 1 - y), (1 - x, 1 - y)]
        for i in range(n):
            for j, (px, py) in enumerate(chips):
                block = _gather_window(land_refs[i], kinds[i], shapes[i], 4 * px + 2 * py + c)
                pltpu.make_async_remote_copy(
                    src_ref=block, dst_ref=block, send_sem=fwd_send.at[3 * i + j], recv_sem=recv_ref.at[4 * i + 1 + j],
                    device_id=(px, py, c), device_id_type=MESH).wait_recv()
                pltpu.make_async_remote_copy(
                    src_ref=block, dst_ref=block, send_sem=fwd_send.at[3 * i + j], recv_sem=fwd_recv.at[3 * i + j],
                    device_id=(x, y, 1 - c), device_id_type=MESH).start()
        token[...] = jnp.zeros_like(token)

    outs = pl.pallas_call(
        body, name=name,
        out_shape=[pltpu.SemaphoreType.DMA((3 * n,)), pltpu.SemaphoreType.DMA((3 * n,))]
        + [_hbm_like(a.shape, a.dtype) for a in lands] + [TOKEN],
        in_specs=[SEM] + [HBM] * n + [ANY], out_specs=[SEM, SEM] + [HBM] * n + [TOKEN_SPEC],
        input_output_aliases={1 + i: 2 + i for i in range(n)},
        compiler_params=SPLIT_PARAMS,
    )(recv_sems, *lands, after)
    return outs[0], outs[1], outs[2:2 + n], outs[-1]


def _gather_finish(send_sems, recv_sems, fwd_send, fwd_recv, lands, kinds, shapes, after, *, name):
    n = len(lands)

    def body(*refs):
        send_ref, recv_ref, fsend_ref, frecv_ref = refs[:4]
        land_refs = refs[4:4 + n]
        x, y, c = _my_place()
        chips = [(1 - x, y), (x, 1 - y), (1 - x, 1 - y)]
        sibling = (x, y, 1 - c)
        for i in range(n):
            def window(j):
                return _gather_window(land_refs[i], kinds[i], shapes[i], j)

            mine, theirs = window(4 * x + 2 * y + c), window(4 * x + 2 * y + (1 - c))
            pltpu.make_async_remote_copy(src_ref=mine, dst_ref=theirs, send_sem=send_ref.at[4 * i],
                                         recv_sem=recv_ref.at[4 * i], device_id=sibling, device_id_type=MESH).wait_recv()
            for j, (px, py) in enumerate(chips):
                block = window(4 * px + 2 * py + (1 - c))
                pltpu.make_async_remote_copy(src_ref=block, dst_ref=block, send_sem=fsend_ref.at[3 * i + j],
                                             recv_sem=frecv_ref.at[3 * i + j], device_id=sibling,
                                             device_id_type=MESH).wait_recv()
            for k in range(4):
                pltpu.make_async_remote_copy(src_ref=mine, dst_ref=mine, send_sem=send_ref.at[4 * i + k],
                                             recv_sem=recv_ref.at[4 * i + k], device_id=sibling,
                                             device_id_type=MESH).wait_send()
            for j, (px, py) in enumerate(chips):
                block = window(4 * px + 2 * py + c)
                pltpu.make_async_remote_copy(src_ref=block, dst_ref=block, send_sem=fsend_ref.at[3 * i + j],
                                             recv_sem=frecv_ref.at[3 * i + j], device_id=sibling,
                                             device_id_type=MESH).wait_send()

    return pl.pallas_call(
        body, name=name,
        out_shape=[_hbm_like(a.shape, a.dtype) for a in lands],
        in_specs=[SEM] * 4 + [HBM] * n + [ANY], out_specs=[HBM] * n,
        input_output_aliases={4 + i: i for i in range(n)},
        compiler_params=SPLIT_PARAMS,
    )(send_sems, recv_sems, fwd_send, fwd_recv, *lands, after)


def _pair_plan(src_ref, land_ref, x, y, c):
    return [(src_ref.at[2 * k + (1 - c)], land_ref.at[k], (x, y, 1 - c)) for k in range(N_CHIPS)]


def _chip_plan(src_ref, land_ref, x, y, c):
    chips = [(1 - x, y), (x, 1 - y), (1 - x, 1 - y)]
    return [(src_ref.at[2 * px + py], land_ref.at[k], (px, py, c)) for k, (px, py) in enumerate(chips)]


def _exchange_copies(plan, per, src_refs, land_refs, send_sems, recv_sems):
    x, y, c = _my_place()
    copies = []
    for i, (s_ref, l_ref) in enumerate(zip(src_refs, land_refs)):
        for q, (src, dst, to) in enumerate(plan(s_ref, l_ref, x, y, c)):
            copies.append(pltpu.make_async_remote_copy(
                src_ref=src, dst_ref=dst, send_sem=send_sems.at[per * i + q], recv_sem=recv_sems.at[per * i + q],
                device_id=to, device_id_type=MESH))
    return copies


def _exchange_start(srcs, plan, per, *, name):
    n = len(srcs)

    def body(*refs):
        src_refs, land_refs = refs[:n], refs[n:2 * n]
        send_sems, recv_sems = refs[2 * n], refs[2 * n + 1]
        for cp in _exchange_copies(plan, per, src_refs, land_refs, send_sems, recv_sems):
            cp.start()
        refs[-1][...] = jnp.zeros_like(refs[-1])

    lands = [lax.empty((per,) + s.shape[1:], s.dtype) for s in srcs]
    outs = pl.pallas_call(
        body, name=name,
        out_shape=[pltpu.SemaphoreType.DMA((per * n,)), pltpu.SemaphoreType.DMA((per * n,))]
        + [_hbm_like(s.shape, s.dtype) for s in srcs] + [_hbm_like(a.shape, a.dtype) for a in lands] + [TOKEN],
        in_specs=[HBM] * (2 * n), out_specs=[SEM, SEM] + [HBM] * (2 * n) + [TOKEN_SPEC],
        input_output_aliases={i: 2 + i for i in range(2 * n)},
        compiler_params=SPLIT_PARAMS,
    )(*[_in_hbm(s) for s in srcs], *[_in_hbm(a) for a in lands])
    return outs[0], outs[1], outs[2:2 + n], outs[2 + n:2 + 2 * n], outs[-1]


def _exchange_wait(send_sems, recv_sems, srcs, lands, plan, per, after, *, name):
    n = len(srcs)
    after = list(after) if isinstance(after, (list, tuple)) else [after]

    def body(*refs):
        send_ref, recv_ref = refs[0], refs[1]
        src_refs, land_refs = refs[2:2 + n], refs[2 + n:2 + 2 * n]
        copies = _exchange_copies(plan, per, src_refs, land_refs, send_ref, recv_ref)
        for cp in copies:
            cp.wait_recv()
        for cp in copies:
            cp.wait_send()

    outs = pl.pallas_call(
        body, name=name,
        out_shape=[_hbm_like(s.shape, s.dtype) for s in srcs] + [_hbm_like(a.shape, a.dtype) for a in lands],
        in_specs=[SEM, SEM] + [HBM] * (2 * n) + [ANY] * len(after), out_specs=[HBM] * (2 * n),
        input_output_aliases={2 + i: i for i in range(2 * n)},
        compiler_params=SPLIT_PARAMS,
    )(send_sems, recv_sems, *srcs, *lands, *after)
    return outs[:n], outs[n:]


REDUCE_BLOCK_BYTES = 1 << 20


def _row_tile(r, c):
    row_bytes = 4 * (-(-c // LANES) * LANES)
    best = r
    for d in range(SUBLANES, r, SUBLANES):
        if r % d == 0 and d * row_bytes <= REDUCE_BLOCK_BYTES:
            best = d
    return best if r * row_bytes > REDUCE_BLOCK_BYTES else r


def _reduce_pair_sum(blocked, recv, place, wire_dtype, *, name):
    _, r, c = blocked.shape
    tr = _row_tile(r, c)

    def body(place_ref, g_ref, r_ref, own_ref, send_ref):
        s = g_ref[...] + r_ref[...]
        send_ref[...] = s.astype(wire_dtype)

        @pl.when(pl.program_id(1) == place_ref[1])
        def _():
            own_ref[...] = s

    return pl.pallas_call(
        body, name=name,
        grid_spec=pltpu.PrefetchScalarGridSpec(
            num_scalar_prefetch=1, grid=(r // tr, N_CHIPS),
            in_specs=[pl.BlockSpec((None, None, tr, c), lambda i, k, place_ref: (k, place_ref[0], i, 0)),
                      pl.BlockSpec((None, tr, c), lambda i, k, place_ref: (k, i, 0))],
            out_specs=[pl.BlockSpec((tr, c), lambda i, k, place_ref: (i, 0)),
                       pl.BlockSpec((None, tr, c), lambda i, k, place_ref: (k, i, 0))]),
        out_shape=[jax.ShapeDtypeStruct((r, c), F32), jax.ShapeDtypeStruct((N_CHIPS, r, c), wire_dtype)],
        compiler_params=_params(("parallel", "arbitrary")),
    )(place, blocked.reshape(N_CHIPS, 2, r, c), recv)


def _chip_sum(own_ref, r_ref):
    return ((own_ref[...] + r_ref[0].astype(F32)) + r_ref[1].astype(F32)) + r_ref[2].astype(F32)


def _reduce_chip_sum(own, recv, *, name):
    r, c = own.shape
    tr = _row_tile(r, c)

    def body(own_ref, r_ref, o_ref):
        o_ref[...] = _chip_sum(own_ref, r_ref)

    return pl.pallas_call(
        body, name=name, grid=(r // tr,),
        in_specs=[pl.BlockSpec((tr, c), lambda i: (i, 0)), pl.BlockSpec((N_CHIPS - 1, tr, c), lambda i: (0, i, 0))],
        out_specs=pl.BlockSpec((tr, c), lambda i: (i, 0)),
        out_shape=jax.ShapeDtypeStruct((r, c), F32),
        compiler_params=_params(("parallel",)),
    )(own, recv)


def _adamw_math(w, g, m, v):
    nm = ADAM_B1 * m + (1.0 - ADAM_B1) * g
    nv = ADAM_B2 * v + (1.0 - ADAM_B2) * (g * g)
    m_hat = nm / (1.0 - ADAM_B1 ** ADAM_STEP)
    v_hat = nv / (1.0 - ADAM_B2 ** ADAM_STEP)
    return -ADAM_LR * (m_hat / (jnp.sqrt(v_hat) + ADAM_EPS) + ADAM_WD * w), nm, nv


def _adamw(w, g, m, v, *, name):
    shape = w.shape
    C = shape[-1]
    R = math.prod(shape[:-1])
    tr = _row_tile(R, C)

    def body(w_ref, g_ref, m_ref, v_ref, d_ref, nm_ref, nv_ref):
        d_ref[...], nm_ref[...], nv_ref[...] = _adamw_math(w_ref[...], g_ref[...], m_ref[...], v_ref[...])

    spec = pl.BlockSpec((tr, C), lambda i: (i, 0))
    outs = pl.pallas_call(
        body, name=name, grid=(R // tr,),
        in_specs=[spec] * 4, out_specs=[spec] * 3,
        out_shape=[jax.ShapeDtypeStruct((R, C), F32)] * 3,
        compiler_params=_params(("parallel",)),
    )(*[a.reshape(R, C) for a in (w, g, m, v)])
    return tuple(o.reshape(shape) for o in outs)


def _reduce_adamw(own, recv, w, m, v, layer, prev, *, name):
    r, c = own.shape
    tr = _row_tile(r, c)
    n_prev = 0 if prev is None else len(prev)

    def body(own_ref, r_ref, w_ref, m_ref, v_ref, *rest):
        g_ref, d_ref, nm_ref, nv_ref = rest[n_prev:]
        g = _chip_sum(own_ref, r_ref)
        g_ref[...] = g
        d_ref[...], nm_ref[...], nv_ref[...] = _adamw_math(w_ref[...], g, m_ref[...], v_ref[...])

    slot = pl.BlockSpec((None, tr, c), lambda i: (layer, i, 0))
    return pl.pallas_call(
        body, name=name, grid=(r // tr,),
        in_specs=[pl.BlockSpec((tr, c), lambda i: (i, 0)), pl.BlockSpec((N_CHIPS - 1, tr, c), lambda i: (0, i, 0)),
                  slot, slot, slot] + [ANY] * n_prev,
        out_specs=[slot] * 4,
        out_shape=[jax.ShapeDtypeStruct((DEPTH, r, c), F32)] * 4,
        input_output_aliases={5 + k: k for k in range(n_prev)},
        compiler_params=_params(("parallel",)),
    )(own, recv, w, m, v, *(prev or ()))


REPLICATED = (("mix_norm", (D_MODEL,)), ("q_norm", (HEAD_DIM,)), ("k_norm", (HEAD_DIM,)), ("sinks", (N_Q_HEADS,)),
              ("sgu_norm", (SGU_WIDTH,)), ("w_s", (SGU_GROUPS, BLOCK, BLOCK)), ("b_s", (SGU_GROUPS, BLOCK)),
              ("ffn_norm", (D_MODEL,)), ("conv_b", (2 * D_FF,)))
TRANSPOSED = ("w_in", "w_up")
SHARDED = (("w_in", "rows"), ("w_oa", "cols"), ("w_ob", "cols"), ("w_out", "rows"), ("w_up", "rows"),
           ("conv_w", "blocks"), ("w_down", "rows"))
WEIGHT_ORDER = ("mix_norm", "w_in", "q_norm", "k_norm", "sinks", "sgu_norm", "w_s", "b_s", "w_oa", "w_ob", "w_out",
                "ffn_norm", "w_up", "conv_w", "conv_b", "w_down")
MIXER_WEIGHTS = ["w_in", "w_oa", "w_ob", "w_out"]
FFN_WEIGHTS = ["w_up", "conv_w", "w_down"]


def _small_layout():
    segs, off = {}, 0
    for l in range(DEPTH):
        for name, shape in REPLICATED:
            n = math.prod(shape)
            segs[(l, name)] = (off, n)
            off += n
    per_dev = -(-off // (N_DEV * SUBLANES * LANES)) * SUBLANES * LANES
    return segs, off, per_dev


def _pack_small(grads):
    ssegs, total, per_dev = _small_layout()
    flat = jnp.concatenate([grads[l][name].reshape(-1) for (l, name) in ssegs])
    return jnp.pad(flat, (0, N_DEV * per_dev - total)).reshape(N_DEV, per_dev // LANES, LANES)


def _unpack_small(gathered):
    ssegs, _, _ = _small_layout()
    flat = gathered.reshape(-1)
    shapes = dict(REPLICATED)
    return {name: jnp.stack([flat[ssegs[(l, name)][0]:ssegs[(l, name)][0] + ssegs[(l, name)][1]].reshape(shapes[name])
                             for l in range(DEPTH)]) for name, _ in REPLICATED}


def kernel(x, mix_norm, w_in, q_norm, k_norm, sinks, sgu_norm, w_s, b_s, w_oa, w_ob, w_out, ffn_norm, w_up, conv_w, conv_b, w_down, loss_target, m_mix_norm, m_w_in, m_q_norm, m_k_norm, m_sinks, m_sgu_norm, m_w_s, m_b_s, m_w_oa, m_w_ob, m_w_out, m_ffn_norm, m_w_up, m_conv_w, m_conv_b, m_w_down, v_mix_norm, v_w_in, v_q_norm, v_k_norm, v_sinks, v_sgu_norm, v_w_s, v_b_s, v_w_oa, v_w_ob, v_w_out, v_ffn_norm, v_w_up, v_conv_w, v_conv_b, v_w_down):
    W = dict(mix_norm=mix_norm, w_in=w_in, q_norm=q_norm, k_norm=k_norm, sinks=sinks, sgu_norm=sgu_norm, w_s=w_s, b_s=b_s,
             w_oa=w_oa, w_ob=w_ob, w_out=w_out, ffn_norm=ffn_norm, w_up=w_up, conv_w=conv_w, conv_b=conv_b, w_down=w_down)
    M = dict(mix_norm=m_mix_norm, w_in=m_w_in, q_norm=m_q_norm, k_norm=m_k_norm, sinks=m_sinks, sgu_norm=m_sgu_norm,
             w_s=m_w_s, b_s=m_b_s, w_oa=m_w_oa, w_ob=m_w_ob, w_out=m_w_out, ffn_norm=m_ffn_norm, w_up=m_w_up,
             conv_w=m_conv_w, conv_b=m_conv_b, w_down=m_w_down)
    V = dict(mix_norm=v_mix_norm, w_in=v_w_in, q_norm=v_q_norm, k_norm=v_k_norm, sinks=v_sinks, sgu_norm=v_sgu_norm,
             w_s=v_w_s, b_s=v_b_s, w_oa=v_w_oa, w_ob=v_w_ob, w_out=v_w_out, ffn_norm=v_ffn_norm, w_up=v_w_up,
             conv_w=v_conv_w, conv_b=v_conv_b, w_down=v_w_down)
    n_seq, seq, d_model = x.shape
    tokens = n_seq * seq
    mx, my, mc = _my_place()
    place = jnp.stack([mc, 2 * mx + my]).astype(jnp.int32)
    half = N_DEV // 2
    kind_of = dict(SHARDED)
    for name in TRANSPOSED:
        W[name], M[name], V[name] = (jnp.swapaxes(t[name], 1, 2) for t in (W, M, V))

    gather_groups = [[(0, MIXER_WEIGHTS[0])], [(0, n) for n in MIXER_WEIGHTS[1:]], [(0, n) for n in FFN_WEIGHTS],
                     [(1, n) for n in MIXER_WEIGHTS], [(1, n) for n in FFN_WEIGHTS]]
    started, in_flight = {}, {}
    weights = []
    for l in range(DEPTH):
        w = {name: W[name][l] for name, _ in REPLICATED}
        w["cb_g"], w["cb_v"] = W["conv_b"][l][:D_FF], W["conv_b"][l][D_FF:]
        w["bias_full"] = jnp.repeat(W["b_s"][l].T, SGU_WIDTH // SGU_GROUPS, axis=1)
        weights.append(w)

    def gather_start(gi, after=()):
        shards = [W[name][l] for l, name in gather_groups[gi]]
        kinds = [kind_of[name] for _, name in gather_groups[gi]]
        shapes = [s.shape for s in shards]
        lands = _place_own(shards, kinds, [F32 if name == "conv_w" else BF16 for _, name in gather_groups[gi]],
                           name=f"gather_weights_own_{gi}")
        send, recv, lands, token = _gather_start(lands, kinds, shapes, after, name=f"gather_weights_start_{gi}")
        started[gi] = dict(sems=(send, recv), lands=lands, kinds=kinds, shapes=shapes)
        return token

    def gather_forward(gi, after):
        st = started[gi]
        in_flight[gi] = _gather_forward(st["sems"][1], st["lands"], st["kinds"], st["shapes"], after,
                                        name=f"gather_weights_forward_{gi}")
        return in_flight[gi][3]

    def gather_finish(gi, after):
        st = started.pop(gi)
        fwd_send, fwd_recv, lands_g, _ = in_flight.pop(gi)
        whole = _gather_finish(st["sems"][0], st["sems"][1], fwd_send, fwd_recv, lands_g, st["kinds"], st["shapes"], after,
                               name=f"gather_weights_finish_{gi}")
        for (l, name), arr in zip(gather_groups[gi], whole):
            w = weights[l]
            if name in TRANSPOSED:
                w[name + "_t"] = arr
            elif name == "conv_w":
                w["cw_g"] = arr[:half].transpose(1, 0, 2).reshape(3, D_FF)
                w["cw_v"] = arr[half:].transpose(1, 0, 2).reshape(3, D_FF)
            else:
                w[name] = arr

    reduce_state, results = {}, {}
    wire = {"conv_w": F32, "small": F32}

    def reduce_begin(key, names, arrays):
        send, recv, srcs_, lands_, token = _exchange_start(arrays, _pair_plan, N_CHIPS, name=f"reduce_pair_start_{key}")
        reduce_state[key] = dict(names=names, pair=(send, recv, srcs_, lands_))
        return [token]

    def reduce_pair(key, after):
        st = reduce_state[key]
        send, recv, srcs_, lands_ = st.pop("pair")
        blocked_, from_sibling = _exchange_wait(send, recv, srcs_, lands_, _pair_plan, N_CHIPS, after,
                                                name=f"reduce_pair_wait_{key}")
        sums = [_reduce_pair_sum(b, r, place, wire.get(n if isinstance(n, str) else n[1], BF16),
                                 name=f"reduce_pair_sum_{key}_{i}")
                for i, (n, b, r) in enumerate(zip(st["names"], blocked_, from_sibling))]
        st["own"] = [s[0] for s in sums]
        *st["chip"], token = _exchange_start([s[1] for s in sums], _chip_plan, N_CHIPS - 1, name=f"reduce_chip_start_{key}")
        return [token]

    def reduce_end(key, after):
        st = reduce_state.pop(key)
        send, recv, srcs_, lands_ = st["chip"]
        _, from_chips = _exchange_wait(send, recv, srcs_, lands_, _chip_plan, N_CHIPS - 1, after,
                                       name=f"reduce_chip_wait_{key}")
        done = []
        for n, own, got in zip(st["names"], st["own"], from_chips):
            if n == "small":
                results["small"] = _reduce_chip_sum(own, got, name="reduce_chip_sum_small")
            else:
                l, name = n
                results[name] = _reduce_adamw(own, got, W[name], M[name], V[name], l, results.get(name),
                                              name=f"l{l}_reduce_adamw_{name}")
                done.append(results[name][0])
        return done

    def sched(point, l, carry, g=None):
        deps = []
        if point == "fwd_start" and l == 0:
            token = gather_start(1, [gather_start(0)])
            gather_finish(0, gather_forward(0, token))
            deps = [gather_start(2, [weights[0]["w_in_t"]])]
        elif point == "fwd_att" and l == 0:
            gather_finish(1, gather_forward(1, carry))
            deps = [gather_forward(2, carry), gather_start(3, [carry])]
        elif point == "fwd_mixer_done" and l == 0:
            gather_finish(2, carry)
            deps = [gather_start(4, [carry])]
        elif point == "fwd_conv" and l == 0:
            deps = [gather_forward(3, carry)]
        elif point == "fwd_start" and l == 1:
            gather_finish(3, carry)
        elif point == "fwd_att" and l == 1:
            deps = [gather_forward(4, carry)]
        elif point == "fwd_mixer_done" and l == 1:
            gather_finish(4, carry)
        elif point == "bwd_ffn_grads":
            conv_w = jnp.concatenate([g[k].reshape(3, half, W_UP_SHARD).transpose(1, 0, 2) for k in ("cw_g", "cw_v")])
            deps = reduce_begin(
                f"l{l}_ffn", [(l, "w_down"), (l, "w_up"), (l, "conv_w")],
                [g["w_down"].reshape(N_DEV, D_FF // N_DEV, D_MODEL),
                 g["w_up_t"].reshape(N_DEV, W_UP_SHARD, D_MODEL), conv_w])
        elif point == "bwd_merge":
            deps = reduce_pair(f"l{l}_ffn", carry)
        elif point == "bwd_out_grads":
            deps = reduce_begin(
                f"l{l}_out", [(l, "w_out"), (l, "w_oa"), (l, "w_ob")],
                [g["w_out"].reshape(N_DEV, D_MODEL // N_DEV, D_MODEL),
                 _disassemble((g["w_oa"],), LANES, _w_o_moves(), name=f"l{l}_split_dw_oa"),
                 _disassemble((g["w_ob"],), LANES, _w_o_moves(), name=f"l{l}_split_dw_ob")])
        elif point == "bwd_att":
            deps = reduce_pair(f"l{l}_out", carry)
        elif point == "bwd_w_in_grad":
            deps = reduce_begin(f"l{l}_in", [(l, "w_in")], [g["w_in_t"].reshape(N_DEV, W_IN_SHARD, D_MODEL)])
        elif point == "bwd_dh":
            deps = reduce_pair(f"l{l}_in", carry)
        return deps

    loss_part, dx, grads, last_deps = _local_step(x.reshape(tokens, d_model), loss_target.reshape(tokens, d_model),
                                                  weights, sched, n_seq=n_seq, seq=seq)
    loss = lax.psum(loss_part, ("x", "y", "c"))

    for g in grads:
        g["conv_b"] = jnp.concatenate([g["cb_g"], g["cb_v"]])
    after = [dx, *last_deps, *reduce_begin("small", ["small"], [_pack_small(grads)])]
    for key in [f"l{l}_{part}" for l in reversed(range(DEPTH)) for part in ("ffn", "out", "in")][:-1]:
        after = reduce_end(key, after)
    after = reduce_end("l0_in", after + reduce_pair("small", after))
    reduce_end("small", after)

    G, delta, new_m, new_v = {}, {}, {}, {}
    for name, _ in SHARDED:
        outs = [jnp.swapaxes(o, 1, 2) for o in results[name]] if name in TRANSPOSED else results[name]
        G[name], delta[name], new_m[name], new_v[name] = outs
    G.update(_unpack_small(_gather([results["small"]], ["blocks"], name="gather_small_grads")[0]))
    for name, _ in REPLICATED:
        delta[name], new_m[name], new_v[name] = _adamw(W[name], G[name], M[name], V[name], name=f"adamw_{name}")
    return (loss, dx.reshape(n_seq, seq, d_model), *[G[n] for n in WEIGHT_ORDER], *[delta[n] for n in WEIGHT_ORDER],
            *[new_m[n] for n in WEIGHT_ORDER], *[new_v[n] for n in WEIGHT_ORDER])
```

```python
import math

import jax
import jax.numpy as jnp
from jax import lax
from jax.experimental import pallas as pl
from jax.experimental.pallas import tpu as pltpu

F32 = jnp.float32
BF16 = jnp.bfloat16
ACT_DTYPE = BF16
MESH = pl.DeviceIdType.MESH

DEPTH = 2
D_MODEL = 1024
N_Q_HEADS = 8
HEAD_DIM = 64
ATT_WIDTH = 512
KV_WIDTH = 128
BLOCK = 128
SGU_WIDTH = 512
SGU_GROUPS = 8
IN_WIDTH = 3840
D_FF = 2816
NORM_EPS = 1e-6
NEG_INF = -1e30
ATT_SCALE = HEAD_DIM ** -0.5
ALIBI_SLOPES = tuple(2.0 ** (-(h + 1)) for h in range(N_Q_HEADS))
ADAM_LR, ADAM_B1, ADAM_B2, ADAM_EPS, ADAM_WD, ADAM_STEP = 0.001, 0.9, 0.999, 1e-08, 0.01, 10
N_DEV = 8
N_CHIPS = 4

QKV_WIDTH = ATT_WIDTH + 2 * KV_WIDTH
COL_SUV, COL_GA, COL_GB, COL_QKV = 0, 1024, 2048, 3072
W_IN_ROTATE = (1, IN_WIDTH // QKV_WIDTH)

LANES = 128
SUBLANES = 8
VMEM_LIMIT_V7X = 56 * 1024 * 1024
GELU_C = math.sqrt(2.0 / math.pi)
GELU_K = 0.044715
ANY = pl.BlockSpec(memory_space=pl.ANY)


def _params(sem=None):
    return pltpu.CompilerParams(dimension_semantics=sem, vmem_limit_bytes=VMEM_LIMIT_V7X)


def _sigmoid(x):
    return 1.0 / (1.0 + jnp.exp(-x))


def _gelu(x):
    th = jnp.tanh(GELU_C * (x + GELU_K * x * x * x))
    return 0.5 * x * (1.0 + th)


def _gelu_and_grad(x):
    x2 = x * x
    th = jnp.tanh(GELU_C * (x + GELU_K * x2 * x))
    g = 0.5 * x * (1.0 + th)
    dg = 0.5 * (1.0 + th) + 0.5 * x * (1.0 - th * th) * (GELU_C * (1.0 + 3.0 * GELU_K * x2))
    return g, dg


def _dot(a, b, dims):
    return lax.dot_general(a, b, (dims, ((), ())), preferred_element_type=F32)


def _dot_nn(a, b):
    return _dot(a, b, ((1,), (0,)))


def _dot_nt(a, b):
    return _dot(a, b, ((1,), (1,)))


def _dot_tn(a, b):
    return _dot(a, b, ((0,), (0,)))


def _lo_mask(shape):
    return lax.broadcasted_iota(jnp.int32, shape, len(shape) - 1) < (LANES // 2)


def _half_sums(x, lo):
    s_lo = jnp.sum(jnp.where(lo, x, 0.0), axis=-1, keepdims=True)
    s_all = jnp.sum(x, axis=-1, keepdims=True)
    return jnp.where(lo, s_lo, s_all - s_lo)


def _dup_half(x, half, lo):
    r = pltpu.roll(x, LANES // 2, axis=1)
    return jnp.where(lo, x, r) if half == 0 else jnp.where(lo, r, x)


def _with_deps(body, n_in, deps):
    k = len(deps)
    if not k:
        return body, [], ()

    def skipping(*refs):
        return body(*refs[:n_in], *refs[n_in + k:])

    return skipping, [ANY] * k, tuple(deps)


MM_VMEM_BUDGET = 40 * 1024 * 1024
MM_MAX_TILE = 1408
MM_MAX_TK = 4096
MM_STEP_BYTES = 1 << 20


def _divisors(n, step, cap):
    return [d for d in range(step, min(n, cap) + 1, step) if n % d == 0] or [n]


def _mm_tiles(M, N, K, out_bytes, tm_divides, tn_divides):
    best = None
    for tm in _divisors(M, LANES, MM_MAX_TILE):
        for tn in _divisors(N, LANES, MM_MAX_TILE):
            if tm_divides % tm or tn_divides % tn:
                continue
            for tk in _divisors(K, 4 * LANES, MM_MAX_TK):
                vmem = 4 * (tm * tk + tk * tn) + 2 * tm * tn * out_bytes + (0 if tk == K else 4 * tm * tn)
                if vmem > MM_VMEM_BUDGET:
                    continue
                traffic = 2 * M * K * (N // tn) + 2 * K * N * (M // tm) + M * N * out_bytes
                cost = traffic + (K // tk - 1) * 8 * M * N + (M // tm) * (N // tn) * (K // tk) * MM_STEP_BYTES
                if best is None or cost < best[0]:
                    best = (cost, tm, tn, tk)
    assert best is not None, (M, N, K)
    return best[1:]


def _mm(a, b, *, mode, out_dtype, name, deps=(), b_rows=(0, None), rotate=None, out_rows=(0, None), out_prev=None):
    b_first, b_count = b_rows
    if mode == "nn":
        (M, K), N = a.shape, b.shape[1]
    elif mode == "nt":
        (M, K), N = a.shape, (b.shape[0] if b_count is None else b_count)
    else:
        (K, M), N = a.shape, b.shape[1]
    shift, period = rotate or (0, 1)
    assert period == 1 or mode == "nt"
    out_first, out_total = out_rows[0], (M if out_rows[1] is None else out_rows[1])
    tm, tn, tk = _mm_tiles(M, N, K, jnp.dtype(out_dtype).itemsize, math.gcd(M, out_first),
                           math.gcd(N // period, b_first if mode == "nt" else 0))
    gm, gn, gk = M // tm, N // tn, K // tk

    def turned(j):
        per = N // period // tn
        return ((j // per + shift) % period) * per + j % per if period > 1 else j

    if mode == "nn":
        a_spec = pl.BlockSpec((tm, tk), lambda i, j, k: (i, k))
        b_spec = pl.BlockSpec((tk, tn), lambda i, j, k: (k + b_first // tk, j))
        contract = ((1,), (0,))
    elif mode == "nt":
        a_spec = pl.BlockSpec((tm, tk), lambda i, j, k: (i, k))
        b_spec = pl.BlockSpec((tn, tk), lambda i, j, k: (turned(j) + b_first // tn, k))
        contract = ((1,), (1,))
    else:
        a_spec = pl.BlockSpec((tk, tm), lambda i, j, k: (k, i))
        b_spec = pl.BlockSpec((tk, tn), lambda i, j, k: (k, j))
        contract = ((0,), (0,))
    o_spec = pl.BlockSpec((tm, tn), lambda i, j, k: (i + out_first // tm, j))
    assert b_first % (tk if mode == "nn" else tn) == 0 and out_first % tm == 0, (name, tm, tn, tk)
    n_prev = 0 if out_prev is None else 1

    def body(a_ref, b_ref, *rest):
        o_ref = rest[n_prev]
        part = _dot(a_ref[...].astype(BF16), b_ref[...].astype(BF16), contract)
        if gk == 1:
            o_ref[...] = part.astype(out_dtype)
            return
        acc_ref = rest[n_prev + 1]
        k = pl.program_id(2)

        @pl.when(k == 0)
        def _():
            acc_ref[...] = part

        @pl.when(k > 0)
        def _():
            acc_ref[...] += part

        @pl.when(k == gk - 1)
        def _():
            o_ref[...] = acc_ref[...].astype(out_dtype)

    body, dep_specs, dep_args = _with_deps(body, 2 + n_prev, deps)
    return pl.pallas_call(
        body,
        name=name,
        grid=(gm, gn, gk),
        in_specs=[a_spec, b_spec] + [ANY] * n_prev + dep_specs,
        out_specs=o_spec,
        out_shape=jax.ShapeDtypeStruct((out_total, N), out_dtype),
        input_output_aliases={2: 0} if n_prev else {},
        scratch_shapes=[] if gk == 1 else [pltpu.VMEM((tm, tn), F32)],
        compiler_params=_params(("parallel", "parallel", "arbitrary")),
    )(a, b, *([out_prev] if n_prev else []), *dep_args)


def _mm_tn_parts(parts, at, b, *, name):
    K, N = b.shape
    n = len(parts)
    tm = math.gcd(*[p.shape[1] for p in parts], *at)
    tiles = [p.shape[1] // tm for p in parts]
    first = [sum(tiles[:p]) for p in range(n)]

    def mine(i, p):
        return jnp.logical_and(i >= first[p], i < first[p] + tiles[p])

    def out_tile(i):
        t = 0
        for p in range(n):
            t = jnp.where(mine(i, p), at[p] // tm + i - first[p], t)
        return t

    def body(*refs):
        a_refs, b_ref, o_ref = refs[:n], refs[n], refs[n + 1]
        for p in range(n):
            @pl.when(mine(pl.program_id(0), p))
            def _(p=p):
                o_ref[...] = _dot_tn(a_refs[p][...], b_ref[...])

    return pl.pallas_call(
        body, name=name, grid=(sum(tiles),),
        in_specs=[pl.BlockSpec((K, tm), lambda i, p=p: (0, jnp.clip(i - first[p], 0, tiles[p] - 1))) for p in range(n)]
        + [pl.BlockSpec((K, N), lambda i: (0, 0))],
        out_specs=pl.BlockSpec((tm, N), lambda i: (out_tile(i), 0)),
        out_shape=jax.ShapeDtypeStruct((sum(p.shape[1] for p in parts), N), F32),
        compiler_params=_params(("arbitrary",)),
    )(*parts, b)


def _mm_rows(a, b, *, mode, fn, out_dtypes, rows=(), vecs=(), reduce=False, name, deps=(), b_rows=(0, None), a_at=None):
    parts = a if a_at is not None else (a,)
    starts = a_at if a_at is not None else (0,)
    n_parts = len(parts)
    M, K = parts[0].shape[0], sum(p.shape[1] for p in parts)
    b_first, b_count = b_rows[0], (b.shape[0] if b_rows[1] is None else b_rows[1])
    N = b.shape[1] if mode == "nn" else b_count
    contract = ((1,), (0,)) if mode == "nn" else ((1,), (1,))
    n_rows, n_vecs, n_out = len(rows), len(vecs), len(out_dtypes)
    out_bytes = sum(jnp.dtype(d).itemsize for d in out_dtypes)
    tm = max(t for t in _divisors(M, LANES, MM_MAX_TILE)
             if 4 * t * K + 4 * K * N + 2 * t * N * (4 * n_rows + out_bytes) <= MM_VMEM_BUDGET)
    assert b_first % b_count == 0 and (a_at is None or mode == "nn")

    def body(*refs):
        a_refs, b_ref, rest = refs[:n_parts], refs[n_parts], refs[n_parts + 1:]
        row_refs, vec_refs = rest[:n_rows], rest[n_rows:n_rows + n_vecs]
        out_refs = rest[n_rows + n_vecs:]
        if a_at is None:
            acc = _dot(a_refs[0][...], b_ref[...], contract)
        else:
            acc = sum(_dot(r[...], b_ref[at:at + r.shape[1], :], contract) for r, at in zip(a_refs, starts))
        res = fn(acc, *[r[...] for r in row_refs], *[v[...] for v in vec_refs])
        for o_ref, val in zip(out_refs[:n_out], res):
            o_ref[...] = val.astype(o_ref.dtype)
        if reduce:
            @pl.when(pl.program_id(0) == 0)
            def _():
                out_refs[n_out][...] = res[n_out]

            @pl.when(pl.program_id(0) > 0)
            def _():
                out_refs[n_out][...] += res[n_out]

    row = pl.BlockSpec((tm, N), lambda i: (i, 0))
    vec = pl.BlockSpec((1, N), lambda i: (0, 0))
    body, dep_specs, dep_args = _with_deps(body, n_parts + 1 + n_rows + n_vecs, deps)
    return pl.pallas_call(
        body, name=name, grid=(M // tm,),
        in_specs=[pl.BlockSpec((tm, p.shape[1]), lambda i: (i, 0)) for p in parts]
        + [pl.BlockSpec((b_count, b.shape[1]), lambda i: (b_first // b_count, 0))]
        + [row] * n_rows + [vec] * n_vecs + dep_specs,
        out_specs=[row] * n_out + [vec] * reduce,
        out_shape=[jax.ShapeDtypeStruct((M, N), d) for d in out_dtypes] + [jax.ShapeDtypeStruct((1, N), F32)] * reduce,
        compiler_params=_params(("arbitrary",)),
    )(*parts, b, *rows, *[v.reshape(1, N) for v in vecs], *dep_args)


def _rms(x, gain):
    return x * lax.rsqrt(jnp.mean(x * x, axis=-1, keepdims=True) + NORM_EPS) * gain


def _residual_then_norm(acc, x, gain):
    x_out = x + acc
    return x_out, _rms(x_out, gain)


def _residual_then_loss(acc, x, target):
    err = (x + acc) - target
    dy = err * (1.0 / D_MODEL)
    return dy, dy, jnp.sum(err * err, axis=0, keepdims=True) * (0.5 / D_MODEL)


def _rms_bwd_rows(dh, x, dres, gain):
    r = lax.rsqrt(jnp.mean(x * x, axis=-1, keepdims=True) + NORM_EPS)
    xh = x * r
    dxh = dh * gain
    dx = dres + r * (dxh - xh * jnp.mean(dxh * xh, axis=-1, keepdims=True))
    return dx, dx, jnp.sum(dh * xh, axis=0, keepdims=True)


def _rms_fwd(x, gain, *, name, tm=512):
    T, D = x.shape

    def body(x_ref, g_ref, h_ref):
        xv = x_ref[...]
        r = lax.rsqrt(jnp.mean(xv * xv, axis=-1, keepdims=True) + NORM_EPS)
        h_ref[...] = (xv * r * g_ref[...]).astype(BF16)

    return pl.pallas_call(
        body, name=name, grid=(T // tm,),
        in_specs=[pl.BlockSpec((tm, D), lambda i: (i, 0)), pl.BlockSpec((1, D), lambda i: (0, 0))],
        out_specs=pl.BlockSpec((tm, D), lambda i: (i, 0)),
        out_shape=jax.ShapeDtypeStruct((T, D), BF16),
        compiler_params=_params(("parallel",)),
    )(x, gain.reshape(1, D))


def _head_norm(x, gain2, lo):
    ms = _half_sums(x * x, lo) * (1.0 / HEAD_DIM)
    r = lax.rsqrt(ms + NORM_EPS)
    xh = x * r
    return xh * gain2, xh, r


def _head_norm_bwd(xh, r, gain2, dy, lo):
    dxh = dy * gain2
    dx = r * (dxh - xh * (_half_sums(dxh * xh, lo) * (1.0 / HEAD_DIM)))
    return dx, dy * xh


Q_GROUP = N_Q_HEADS // 2
GROUP_ROWS = Q_GROUP * BLOCK
ATT_SCRATCH = (pltpu.VMEM((2, 2, GROUP_ROWS, BLOCK), F32), pltpu.VMEM((2, GROUP_ROWS, 1), F32))


def _att_consts(sink_ref, bias_ref, sinkcol_ref):
    row = lax.broadcasted_iota(jnp.int32, (GROUP_ROWS, BLOCK), 0)
    kj = lax.broadcasted_iota(jnp.int32, (GROUP_ROWS, BLOCK), 1)
    head = row // BLOCK
    head_col = lax.broadcasted_iota(jnp.int32, (GROUP_ROWS, 1), 0) // BLOCK
    d_cur = (row % BLOCK) - kj
    d_prev = d_cur + BLOCK
    for kv in range(2):
        slope = jnp.zeros((GROUP_ROWS, BLOCK), F32)
        sink = jnp.zeros((GROUP_ROWS, 1), F32)
        for r in range(Q_GROUP):
            slope = jnp.where(head == r, ALIBI_SLOPES[Q_GROUP * kv + r], slope)
            sink = jnp.where(head_col == r, sink_ref[Q_GROUP * kv + r], sink)
        bias_ref[kv, 0] = jnp.where(d_cur >= 0, -slope * d_cur.astype(F32), NEG_INF)
        bias_ref[kv, 1] = jnp.where(d_prev < BLOCK, -slope * d_prev.astype(F32), NEG_INF)
        sinkcol_ref[kv] = sink


def _stack_heads(t0, t1, lo):
    z = jnp.zeros_like(t0)
    return jnp.concatenate([jnp.where(lo, t0, z), jnp.where(lo, z, t0), jnp.where(lo, t1, z), jnp.where(lo, z, t1)], axis=0)


def _unstack_heads(x4, lo):
    return (jnp.where(lo, x4[0:BLOCK], x4[BLOCK:2 * BLOCK]), jnp.where(lo, x4[2 * BLOCK:3 * BLOCK], x4[3 * BLOCK:]))


def _att_probs(q4, k2c, k2p, bias_c, bias_p, sink, has_prev):
    s_c = _dot_nt(q4, k2c) * ATT_SCALE + bias_c
    s_p = jnp.where(has_prev, _dot_nt(q4, k2p) * ATT_SCALE + bias_p, NEG_INF)
    m = jnp.maximum(jnp.max(jnp.maximum(s_c, s_p), axis=-1, keepdims=True), sink)
    e_c = jnp.exp(s_c - m)
    e_p = jnp.exp(s_p - m)
    e_s = jnp.exp(sink - m)
    inv = 1.0 / (jnp.sum(e_c + e_p, axis=-1, keepdims=True) + e_s)
    return e_c * inv, e_p * inv, e_s * inv


def _attention_fwd(proj, q_gain, k_gain, sinks, *, n_seq, seq, name):
    T = n_seq * seq
    nb = seq // BLOCK
    qcol, kvcol = COL_QKV // ATT_WIDTH, (COL_QKV + ATT_WIDTH) // (2 * KV_WIDTH)

    def body(q_ref, kv_ref, qg_ref, kg_ref, sink_ref, y_ref, bias_ref, sinkcol_ref):
        lo = _lo_mask((BLOCK, LANES))
        qg, kg = qg_ref[...], kg_ref[...]
        _att_consts(sink_ref, bias_ref, sinkcol_ref)

        def block(i, carry):
            r0 = pl.multiple_of(i * BLOCK, BLOCK)
            rp = pl.multiple_of(jnp.maximum(i - 1, 0) * BLOCK, BLOCK)
            has_prev = i > 0
            kn_c = _head_norm(kv_ref[pl.ds(r0, BLOCK), 0:KV_WIDTH].astype(F32), kg, lo)[0].astype(BF16)
            kn_p = _head_norm(kv_ref[pl.ds(rp, BLOCK), 0:KV_WIDTH].astype(F32), kg, lo)[0].astype(BF16)
            v_c = kv_ref[pl.ds(r0, BLOCK), KV_WIDTH:2 * KV_WIDTH].astype(BF16)
            v_p = kv_ref[pl.ds(rp, BLOCK), KV_WIDTH:2 * KV_WIDTH].astype(BF16)
            for kv in range(2):
                k2c, k2p = _dup_half(kn_c, kv, lo), _dup_half(kn_p, kv, lo)
                v2c, v2p = _dup_half(v_c, kv, lo), _dup_half(v_p, kv, lo)
                cols = [slice((2 * kv + t) * LANES, (2 * kv + t + 1) * LANES) for t in range(2)]
                qn = [_head_norm(q_ref[pl.ds(r0, BLOCK), c].astype(F32), qg, lo)[0] for c in cols]
                q4 = _stack_heads(qn[0], qn[1], lo).astype(BF16)
                p_c, p_p, _ = _att_probs(q4, k2c, k2p, bias_ref[kv, 0], bias_ref[kv, 1], sinkcol_ref[kv], has_prev)
                o4 = _dot_nn(p_c.astype(BF16), v2c) + _dot_nn(p_p.astype(BF16), v2p)
                for c, out in zip(cols, _unstack_heads(o4, lo)):
                    y_ref[pl.ds(r0, BLOCK), c] = out.astype(BF16)
            return carry

        lax.fori_loop(0, nb, block, 0)

    vec = pl.BlockSpec((1, LANES), lambda b: (0, 0))
    return pl.pallas_call(
        body, name=name, grid=(n_seq,),
        in_specs=[pl.BlockSpec((seq, ATT_WIDTH), lambda b: (b, qcol)),
                  pl.BlockSpec((seq, 2 * KV_WIDTH), lambda b: (b, kvcol)),
                  vec, vec, pl.BlockSpec(memory_space=pltpu.SMEM)],
        out_specs=pl.BlockSpec((seq, ATT_WIDTH), lambda b: (b, 0)),
        out_shape=jax.ShapeDtypeStruct((T, ATT_WIDTH), BF16),
        scratch_shapes=list(ATT_SCRATCH),
        compiler_params=_params(("parallel",)),
    )(proj, proj, jnp.tile(q_gain, 2).reshape(1, LANES), jnp.tile(k_gain, 2).reshape(1, LANES), sinks)


def _attention_bwd(proj, dy, q_gain, k_gain, sinks, *, n_seq, seq, name, deps=()):
    T = n_seq * seq
    nb = seq // BLOCK
    qcol, kvcol = COL_QKV // ATT_WIDTH, (COL_QKV + ATT_WIDTH) // (2 * KV_WIDTH)

    def body(q_ref, kv_ref, dy_ref, qg_ref, kg_ref, sink_ref, dqkv_ref, dqg_ref, dkg_ref, dsink_ref,
             dkn_acc, dv_acc, qg_acc, kg_acc, sink_acc, bias_ref, sinkcol_ref):
        lo = _lo_mask((BLOCK, LANES))
        qg, kg = qg_ref[...], kg_ref[...]
        _att_consts(sink_ref, bias_ref, sinkcol_ref)
        first = pl.program_id(0) == 0

        @pl.when(first)
        def _():
            qg_acc[...] = jnp.zeros_like(qg_acc)
            kg_acc[...] = jnp.zeros_like(kg_acc)
            sink_acc[...] = jnp.zeros_like(sink_acc)

        dkn_acc[...] = jnp.zeros_like(dkn_acc)
        dv_acc[...] = jnp.zeros_like(dv_acc)

        def block(i, carry):
            r0 = pl.multiple_of(i * BLOCK, BLOCK)
            rp = pl.multiple_of(jnp.maximum(i - 1, 0) * BLOCK, BLOCK)
            has_prev = i > 0
            kn_c = _head_norm(kv_ref[pl.ds(r0, BLOCK), 0:KV_WIDTH].astype(F32), kg, lo)[0].astype(BF16)
            kn_p = _head_norm(kv_ref[pl.ds(rp, BLOCK), 0:KV_WIDTH].astype(F32), kg, lo)[0].astype(BF16)
            v_c = kv_ref[pl.ds(r0, BLOCK), KV_WIDTH:2 * KV_WIDTH].astype(BF16)
            v_p = kv_ref[pl.ds(rp, BLOCK), KV_WIDTH:2 * KV_WIDTH].astype(BF16)
            dk_c, dk_p, dv_c, dv_p = [], [], [], []
            for kv in range(2):
                k2c, k2p = _dup_half(kn_c, kv, lo), _dup_half(kn_p, kv, lo)
                v2c, v2p = _dup_half(v_c, kv, lo), _dup_half(v_p, kv, lo)
                cols = [slice((2 * kv + t) * LANES, (2 * kv + t + 1) * LANES) for t in range(2)]
                normed = [_head_norm(q_ref[pl.ds(r0, BLOCK), c].astype(F32), qg, lo) for c in cols]
                q4 = _stack_heads(normed[0][0], normed[1][0], lo).astype(BF16)
                do4 = _stack_heads(dy_ref[pl.ds(r0, BLOCK), cols[0]], dy_ref[pl.ds(r0, BLOCK), cols[1]], lo)
                p_c, p_p, p_s = _att_probs(q4, k2c, k2p, bias_ref[kv, 0], bias_ref[kv, 1], sinkcol_ref[kv], has_prev)
                dp_c = _dot_nt(do4, v2c)
                dp_p = _dot_nt(do4, v2p)
                delta = jnp.sum(p_c * dp_c + p_p * dp_p, axis=-1, keepdims=True)
                ds_c = (p_c * (dp_c - delta)).astype(BF16)
                ds_p = (p_p * (dp_p - delta)).astype(BF16)
                sink_acc[kv] += -(p_s * delta)
                dq4 = (_dot_nn(ds_c, k2c) + _dot_nn(ds_p, k2p)) * ATT_SCALE
                for c, (_, qh, qr), dqn in zip(cols, normed, _unstack_heads(dq4, lo)):
                    dq, dg = _head_norm_bwd(qh, qr, qg, dqn, lo)
                    dqkv_ref[pl.ds(r0, BLOCK), c] = dq.astype(BF16)
                    qg_acc[...] += dg
                dk_c.append(_dot_tn(ds_c, q4))
                dk_p.append(_dot_tn(ds_p, q4))
                dv_c.append(_dot_tn(p_c.astype(BF16), do4))
                dv_p.append(_dot_tn(p_p.astype(BF16), do4))

            def fold(parts):
                a = parts[0] + pltpu.roll(parts[0], LANES // 2, axis=1)
                b = parts[1] + pltpu.roll(parts[1], LANES // 2, axis=1)
                return jnp.where(lo, a, b)

            dkn_acc[pl.ds(r0, BLOCK), :] += fold(dk_c) * ATT_SCALE
            dkn_acc[pl.ds(rp, BLOCK), :] += fold(dk_p) * ATT_SCALE
            dv_acc[pl.ds(r0, BLOCK), :] += fold(dv_c)
            dv_acc[pl.ds(rp, BLOCK), :] += fold(dv_p)
            return carry

        lax.fori_loop(0, nb, block, 0)

        def finish(i, carry):
            r0 = pl.multiple_of(i * BLOCK, BLOCK)
            _, kh, kr = _head_norm(kv_ref[pl.ds(r0, BLOCK), 0:KV_WIDTH].astype(F32), kg, lo)
            dk, dg = _head_norm_bwd(kh, kr, kg, dkn_acc[pl.ds(r0, BLOCK), :], lo)
            dqkv_ref[pl.ds(r0, BLOCK), ATT_WIDTH:ATT_WIDTH + KV_WIDTH] = dk.astype(BF16)
            dqkv_ref[pl.ds(r0, BLOCK), ATT_WIDTH + KV_WIDTH:QKV_WIDTH] = dv_acc[pl.ds(r0, BLOCK), :].astype(BF16)
            kg_acc[...] += dg
            return carry

        lax.fori_loop(0, nb, finish, 0)

        @pl.when(pl.program_id(0) == n_seq - 1)
        def _():
            dqg_ref[...] = jnp.sum(qg_acc[...], axis=0, keepdims=True)
            dkg_ref[...] = jnp.sum(kg_acc[...], axis=0, keepdims=True)
            lane = lax.broadcasted_iota(jnp.int32, (1, LANES), 1)
            dsink = jnp.zeros((1, LANES), F32)
            for kv in range(2):
                for r in range(Q_GROUP):
                    total = jnp.sum(sink_acc[kv, r * BLOCK:(r + 1) * BLOCK, :], axis=0, keepdims=True)
                    dsink = jnp.where(lane == Q_GROUP * kv + r, total, dsink)
            dsink_ref[...] = dsink

    vec = pl.BlockSpec((1, LANES), lambda b: (0, 0))
    acc = pltpu.VMEM((BLOCK, LANES), F32)
    body, dep_specs, dep_args = _with_deps(body, 6, deps)
    dqkv, dqg, dkg, dsink = pl.pallas_call(
        body, name=name, grid=(n_seq,),
        in_specs=[pl.BlockSpec((seq, ATT_WIDTH), lambda b: (b, qcol)),
                  pl.BlockSpec((seq, 2 * KV_WIDTH), lambda b: (b, kvcol)),
                  pl.BlockSpec((seq, ATT_WIDTH), lambda b: (b, 0)),
                  vec, vec, pl.BlockSpec(memory_space=pltpu.SMEM)] + dep_specs,
        out_specs=[pl.BlockSpec((seq, QKV_WIDTH), lambda b: (b, 0)), vec, vec, vec],
        out_shape=[jax.ShapeDtypeStruct((T, QKV_WIDTH), BF16)] + [jax.ShapeDtypeStruct((1, LANES), F32)] * 3,
        scratch_shapes=[pltpu.VMEM((seq, KV_WIDTH), F32), pltpu.VMEM((seq, KV_WIDTH), F32), acc, acc,
                        pltpu.VMEM((2, GROUP_ROWS, 1), F32), *ATT_SCRATCH],
        compiler_params=_params(("arbitrary",)),
    )(proj, proj, dy, jnp.tile(q_gain, 2).reshape(1, LANES), jnp.tile(k_gain, 2).reshape(1, LANES), sinks, *dep_args)
    half = LANES // 2
    return dqkv, dqg[0, :half] + dqg[0, half:], dkg[0, :half] + dkg[0, half:], dsink[0, :N_Q_HEADS]


def _sgu_weights(w_ref):
    r = lax.broadcasted_iota(jnp.int32, (BLOCK, BLOCK), 0)
    c = lax.broadcasted_iota(jnp.int32, (BLOCK, BLOCK), 1)
    return [jnp.where(r >= c, w_ref[g], 0.0).astype(BF16) for g in range(SGU_GROUPS)]


def _sgu_fwd(proj, gain, w_s, bias_full, *, n_seq, seq, name):
    T = n_seq * seq
    nc = seq // BLOCK

    def body(suv_ref, g_ref, w_ref, b_ref, y_ref):
        lo = _lo_mask((BLOCK, LANES))
        wm = _sgu_weights(w_ref)
        gain_v = g_ref[...]

        def chunk(c, carry):
            r0 = pl.multiple_of(c * BLOCK, BLOCK)
            gv = _gelu(suv_ref[pl.ds(r0, BLOCK), SGU_WIDTH:2 * SGU_WIDTH].astype(F32))
            r = lax.rsqrt(jnp.mean(gv * gv, axis=-1, keepdims=True) + NORM_EPS)
            vn = (gv * r * gain_v).astype(BF16)
            for p in range(SGU_WIDTH // LANES):
                cols = slice(p * LANES, (p + 1) * LANES)
                vp = vn[:, cols]
                mixed = jnp.where(lo, _dot_nn(wm[2 * p], vp), _dot_nn(wm[2 * p + 1], vp)) + b_ref[:, cols]
                u = _gelu(suv_ref[pl.ds(r0, BLOCK), cols].astype(F32))
                y_ref[pl.ds(r0, BLOCK), cols] = (u * mixed).astype(BF16)
            return carry

        lax.fori_loop(0, nc, chunk, 0)

    return pl.pallas_call(
        body, name=name, grid=(n_seq,),
        in_specs=[pl.BlockSpec((seq, 2 * SGU_WIDTH), lambda b: (b, COL_SUV // (2 * SGU_WIDTH))),
                  pl.BlockSpec((1, SGU_WIDTH), lambda b: (0, 0)),
                  pl.BlockSpec((SGU_GROUPS, BLOCK, BLOCK), lambda b: (0, 0, 0)),
                  pl.BlockSpec((BLOCK, SGU_WIDTH), lambda b: (0, 0))],
        out_specs=pl.BlockSpec((seq, SGU_WIDTH), lambda b: (b, 0)),
        out_shape=jax.ShapeDtypeStruct((T, SGU_WIDTH), BF16),
        compiler_params=_params(("parallel",)),
    )(proj, gain.reshape(1, SGU_WIDTH), w_s, bias_full)


def _sgu_bwd(proj, dy, gain, w_s, bias_full, *, n_seq, seq, name, deps=()):
    T = n_seq * seq
    nc = seq // BLOCK
    n_tiles = SGU_WIDTH // LANES

    def body(suv_ref, dy_ref, g_ref, w_ref, b_ref, dsuv_ref, dg_ref, dw_ref, db_ref, dg_acc, dw_acc, db_acc):
        lo = _lo_mask((BLOCK, LANES))
        hi = jnp.logical_not(lo)
        wm = _sgu_weights(w_ref)
        wmt = [jnp.where(lax.broadcasted_iota(jnp.int32, (BLOCK, BLOCK), 1) >= lax.broadcasted_iota(jnp.int32, (BLOCK, BLOCK), 0),
                         w_ref[g].T, 0.0).astype(BF16) for g in range(SGU_GROUPS)]
        gain_v = g_ref[...]

        @pl.when(pl.program_id(0) == 0)
        def _():
            dg_acc[...] = jnp.zeros_like(dg_acc)
            dw_acc[...] = jnp.zeros_like(dw_acc)
            db_acc[...] = jnp.zeros_like(db_acc)

        def chunk(c, carry):
            r0 = pl.multiple_of(c * BLOCK, BLOCK)
            gv, dgelu_v = _gelu_and_grad(suv_ref[pl.ds(r0, BLOCK), SGU_WIDTH:2 * SGU_WIDTH].astype(F32))
            r = lax.rsqrt(jnp.mean(gv * gv, axis=-1, keepdims=True) + NORM_EPS)
            vh = gv * r
            vn = (vh * gain_v).astype(BF16)
            dvn_tiles = []
            for p in range(n_tiles):
                cols = slice(p * LANES, (p + 1) * LANES)
                vp = vn[:, cols]
                mixed = jnp.where(lo, _dot_nn(wm[2 * p], vp), _dot_nn(wm[2 * p + 1], vp)) + b_ref[:, cols]
                u, dgelu_u = _gelu_and_grad(suv_ref[pl.ds(r0, BLOCK), cols].astype(F32))
                dyv = dy_ref[pl.ds(r0, BLOCK), cols]
                dsuv_ref[pl.ds(r0, BLOCK), cols] = (dyv * mixed * dgelu_u).astype(BF16)
                dm = dyv * u
                db_acc[:, cols] += dm
                dm_bf = dm.astype(BF16)
                dvn_tiles.append(jnp.where(lo, _dot_nn(wmt[2 * p], dm_bf), _dot_nn(wmt[2 * p + 1], dm_bf)))
                dw_acc[2 * p] += _dot_nt(jnp.where(lo, dm, 0.0).astype(BF16), vp)
                dw_acc[2 * p + 1] += _dot_nt(jnp.where(hi, dm, 0.0).astype(BF16), vp)
            dvn = jnp.concatenate(dvn_tiles, axis=1)
            dg_acc[...] += dvn * vh
            dvh = dvn * gain_v
            dgv = r * (dvh - vh * jnp.mean(dvh * vh, axis=-1, keepdims=True))
            dsuv_ref[pl.ds(r0, BLOCK), SGU_WIDTH:2 * SGU_WIDTH] = (dgv * dgelu_v).astype(BF16)
            return carry

        lax.fori_loop(0, nc, chunk, 0)

        @pl.when(pl.program_id(0) == n_seq - 1)
        def _():
            dg_ref[...] = jnp.sum(dg_acc[...], axis=0, keepdims=True)
            r = lax.broadcasted_iota(jnp.int32, (BLOCK, BLOCK), 0)
            c = lax.broadcasted_iota(jnp.int32, (BLOCK, BLOCK), 1)
            for g in range(SGU_GROUPS):
                dw_ref[g] = jnp.where(r >= c, dw_acc[g], 0.0)
            lane = lax.broadcasted_iota(jnp.int32, (BLOCK, LANES), 1)
            out = jnp.zeros((BLOCK, LANES), F32)
            for p in range(n_tiles):
                tile = db_acc[:, p * LANES:(p + 1) * LANES]
                s_lo = jnp.sum(jnp.where(lo, tile, 0.0), axis=-1, keepdims=True)
                s_hi = jnp.sum(jnp.where(hi, tile, 0.0), axis=-1, keepdims=True)
                out = jnp.where(lane == 2 * p, s_lo, out)
                out = jnp.where(lane == 2 * p + 1, s_hi, out)
            db_ref[...] = out

    body, dep_specs, dep_args = _with_deps(body, 5, deps)
    dsuv, dg, dw, db = pl.pallas_call(
        body, name=name, grid=(n_seq,),
        in_specs=[pl.BlockSpec((seq, 2 * SGU_WIDTH), lambda b: (b, COL_SUV // (2 * SGU_WIDTH))),
                  pl.BlockSpec((seq, SGU_WIDTH), lambda b: (b, 0)),
                  pl.BlockSpec((1, SGU_WIDTH), lambda b: (0, 0)),
                  pl.BlockSpec((SGU_GROUPS, BLOCK, BLOCK), lambda b: (0, 0, 0)),
                  pl.BlockSpec((BLOCK, SGU_WIDTH), lambda b: (0, 0))] + dep_specs,
        out_specs=[pl.BlockSpec((seq, 2 * SGU_WIDTH), lambda b: (b, 0)),
                   pl.BlockSpec((1, SGU_WIDTH), lambda b: (0, 0)),
                   pl.BlockSpec((SGU_GROUPS, BLOCK, BLOCK), lambda b: (0, 0, 0)),
                   pl.BlockSpec((BLOCK, LANES), lambda b: (0, 0))],
        out_shape=[jax.ShapeDtypeStruct((T, 2 * SGU_WIDTH), BF16), jax.ShapeDtypeStruct((1, SGU_WIDTH), F32),
                   jax.ShapeDtypeStruct((SGU_GROUPS, BLOCK, BLOCK), F32), jax.ShapeDtypeStruct((BLOCK, LANES), F32)],
        scratch_shapes=[pltpu.VMEM((BLOCK, SGU_WIDTH), F32), pltpu.VMEM((SGU_GROUPS, BLOCK, BLOCK), F32),
                        pltpu.VMEM((BLOCK, SGU_WIDTH), F32)],
        compiler_params=_params(("arbitrary",)),
    )(proj, dy, gain.reshape(1, SGU_WIDTH), w_s, bias_full, *dep_args)
    return dsuv, dg.reshape(SGU_WIDTH), dw, db[:, :SGU_GROUPS].T


def _merge_fwd(y_att, y_sgu, w_oa, w_ob, proj, *, name, tm=1024, tn=512, deps=()):
    T = y_att.shape[0]

    def body(ya_ref, ys_ref, wa_ref, wb_ref, ga_ref, gb_ref, o_ref):
        pa = _dot_nn(ya_ref[...], wa_ref[...])
        pb = _dot_nn(ys_ref[...], wb_ref[...])
        o_ref[...] = (_sigmoid(ga_ref[...].astype(F32)) * pa + _sigmoid(gb_ref[...].astype(F32)) * pb).astype(BF16)

    act = pl.BlockSpec((tm, ATT_WIDTH), lambda i, j: (i, 0))
    wgt = pl.BlockSpec((ATT_WIDTH, tn), lambda i, j: (0, j))
    body, dep_specs, dep_args = _with_deps(body, 6, deps)
    return pl.pallas_call(
        body, name=name, grid=(T // tm, D_MODEL // tn),
        in_specs=[act, act, wgt, wgt,
                  pl.BlockSpec((tm, tn), lambda i, j: (i, j + COL_GA // tn)),
                  pl.BlockSpec((tm, tn), lambda i, j: (i, j + COL_GB // tn))] + dep_specs,
        out_specs=pl.BlockSpec((tm, tn), lambda i, j: (i, j)),
        out_shape=jax.ShapeDtypeStruct((T, D_MODEL), BF16),
        compiler_params=_params(("parallel", "parallel")),
    )(y_att, y_sgu, w_oa, w_ob, proj, proj, *dep_args)


def _merge_bwd(dx1_bf, w_out, y_att, y_sgu, w_oa, w_ob, proj, *, name, tm=1024, tn=512):
    T = y_att.shape[0]

    def body(dx_ref, wo_ref, ya_ref, ys_ref, wa_ref, wb_ref, ga_ref, gb_ref, dpa_ref, dpb_ref, dga_ref, dgb_ref):
        dm = _dot_nt(dx_ref[...], wo_ref[...])
        pa = _dot_nn(ya_ref[...], wa_ref[...])
        pb = _dot_nn(ys_ref[...], wb_ref[...])
        sa = _sigmoid(ga_ref[...].astype(F32))
        sb = _sigmoid(gb_ref[...].astype(F32))
        dpa_ref[...] = (dm * sa).astype(BF16)
        dpb_ref[...] = (dm * sb).astype(BF16)
        dga_ref[...] = (dm * pa * sa * (1.0 - sa)).astype(BF16)
        dgb_ref[...] = (dm * pb * sb * (1.0 - sb)).astype(BF16)

    act = pl.BlockSpec((tm, ATT_WIDTH), lambda i, j: (i, 0))
    wgt = pl.BlockSpec((ATT_WIDTH, tn), lambda i, j: (0, j))
    out = pl.BlockSpec((tm, tn), lambda i, j: (i, j))
    return pl.pallas_call(
        body, name=name, grid=(T // tm, D_MODEL // tn),
        in_specs=[pl.BlockSpec((tm, D_MODEL), lambda i, j: (i, 0)),
                  pl.BlockSpec((tn, D_MODEL), lambda i, j: (j, 0)),
                  act, act, wgt, wgt,
                  pl.BlockSpec((tm, tn), lambda i, j: (i, j + COL_GA // tn)),
                  pl.BlockSpec((tm, tn), lambda i, j: (i, j + COL_GB // tn))],
        out_specs=[out] * 4,
        out_shape=[jax.ShapeDtypeStruct((T, D_MODEL), BF16)] * 4,
        compiler_params=_params(("parallel", "parallel")),
    )(dx1_bf, w_out, y_att, y_sgu, w_oa, w_ob, proj, proj)


CONV_ROWS = 256
CONV_TN = 256


def _shift_rows(cur, prev8, k):
    rolled = pltpu.roll(cur, k, axis=0)
    head = jnp.where(lax.broadcasted_iota(jnp.int32, prev8.shape, 0) < k, pltpu.roll(prev8, k, axis=0), rolled[:SUBLANES])
    return jnp.concatenate([head, rolled[SUBLANES:]], axis=0)


def _shift_rows_up(cur, next8, k):
    n = cur.shape[0]
    rolled = pltpu.roll(cur, n - k, axis=0)
    tail = jnp.where(lax.broadcasted_iota(jnp.int32, next8.shape, 0) >= SUBLANES - k,
                     pltpu.roll(next8, SUBLANES - k, axis=0), rolled[n - SUBLANES:])
    return jnp.concatenate([rolled[:n - SUBLANES], tail], axis=0)


HALO_ROWS = 16


def _rows_before(z_ref, r0, first):
    rp = pl.multiple_of(jnp.maximum(r0 - HALO_ROWS, 0), HALO_ROWS)
    halo = z_ref[pl.ds(rp, HALO_ROWS), :].astype(F32)
    return jnp.where(first, 0.0, halo[HALO_ROWS - SUBLANES:])


def _conv_rows(z_ref, r0, first, w_ref, b_ref, rows):
    cur = z_ref[pl.ds(r0, rows), :].astype(F32)
    prev8 = _rows_before(z_ref, r0, first)
    z1 = _shift_rows(cur, prev8, 1)
    z2 = _shift_rows(cur, prev8, 2)
    return b_ref[...] + w_ref[0:1, :] * z2 + w_ref[1:2, :] * z1 + w_ref[2:3, :] * cur


def _conv_fwd(z_g, z_v, cw_g, cw_v, cb_g, cb_v, *, n_seq, seq, name):
    T = n_seq * seq
    tn, rows = CONV_TN, CONV_ROWS

    def body(zg_ref, zv_ref, wg_ref, wv_ref, bg_ref, bv_ref, a_ref, cg_ref, cv_ref):
        def step(s, carry):
            r0 = pl.multiple_of(s * rows, rows)
            first = s == 0
            g = _conv_rows(zg_ref, r0, first, wg_ref, bg_ref, rows)
            v = _conv_rows(zv_ref, r0, first, wv_ref, bv_ref, rows)
            a_ref[pl.ds(r0, rows), :] = (g * _sigmoid(g) * v).astype(BF16)
            cg_ref[pl.ds(r0, rows), :] = g.astype(ACT_DTYPE)
            cv_ref[pl.ds(r0, rows), :] = v.astype(ACT_DTYPE)
            return carry

        lax.fori_loop(0, seq // rows, step, 0)

    zs = pl.BlockSpec((seq, tn), lambda b, j: (b, j))
    ws = pl.BlockSpec((3, tn), lambda b, j: (0, j))
    bs = pl.BlockSpec((1, tn), lambda b, j: (0, j))
    return pl.pallas_call(
        body, name=name, grid=(n_seq, D_FF // tn),
        in_specs=[zs, zs, ws, ws, bs, bs], out_specs=[zs] * 3,
        out_shape=[jax.ShapeDtypeStruct((T, D_FF), BF16)] + [jax.ShapeDtypeStruct((T, D_FF), ACT_DTYPE)] * 2,
        compiler_params=_params(("parallel", "parallel")),
    )(z_g, z_v, cw_g, cw_v, cb_g.reshape(1, D_FF), cb_v.reshape(1, D_FF))


def _conv_bwd(z_g, z_v, c_g, c_v, da, cw_g, cw_v, *, n_seq, seq, name):
    T = n_seq * seq
    tn, rows = CONV_TN, CONV_ROWS
    n_steps = seq // rows

    def body(zg_ref, zv_ref, cg_ref, cv_ref, da_ref, wg_ref, wv_ref,
             dzg_ref, dzv_ref, dwg_ref, dwv_ref, dbg_ref, dbv_ref, dcg_ref, dcv_ref):
        def colsum(x):
            return jnp.sum(x, axis=0, keepdims=True)

        def grads(s, accs):
            r0 = pl.multiple_of(s * rows, rows)
            g = cg_ref[pl.ds(r0, rows), :].astype(F32)
            v = cv_ref[pl.ds(r0, rows), :].astype(F32)
            sg = _sigmoid(g)
            dav = da_ref[pl.ds(r0, rows), :].astype(F32)
            dcg = dav * v * (sg * (1.0 + g * (1.0 - sg)))
            dcv = dav * (g * sg)
            dcg_ref[pl.ds(r0, rows), :] = dcg
            dcv_ref[pl.ds(r0, rows), :] = dcv
            return accs[0] + colsum(dcg), accs[1] + colsum(dcv)

        zero = jnp.zeros((1, tn), F32)
        db = lax.fori_loop(0, n_steps, grads, (zero, zero))

        def back(s, accs):
            r0 = pl.multiple_of(s * rows, rows)
            last = s == n_steps - 1
            rn = pl.multiple_of(jnp.minimum(r0 + rows, seq - SUBLANES), SUBLANES)
            new = []
            for half, (dc_ref, w_ref, dz_ref, z_ref) in enumerate(((dcg_ref, wg_ref, dzg_ref, zg_ref),
                                                                   (dcv_ref, wv_ref, dzv_ref, zv_ref))):
                cur = dc_ref[pl.ds(r0, rows), :]
                nxt = jnp.where(last, 0.0, dc_ref[pl.ds(rn, SUBLANES), :])
                u1, u2 = _shift_rows_up(cur, nxt, 1), _shift_rows_up(cur, nxt, 2)
                dz_ref[pl.ds(r0, rows), :] = (w_ref[2:3, :] * cur + w_ref[1:2, :] * u1 + w_ref[0:1, :] * u2).astype(BF16)
                z = z_ref[pl.ds(r0, rows), :].astype(F32)
                new += [accs[3 * half] + colsum(u2 * z), accs[3 * half + 1] + colsum(u1 * z),
                        accs[3 * half + 2] + colsum(cur * z)]
            return tuple(new)

        dw = lax.fori_loop(0, n_steps, back, (zero,) * 6)
        first_seq = pl.program_id(1) == 0

        @pl.when(first_seq)
        def _():
            dwg_ref[...] = jnp.concatenate(dw[0:3], axis=0)
            dwv_ref[...] = jnp.concatenate(dw[3:6], axis=0)
            dbg_ref[...], dbv_ref[...] = db

        @pl.when(jnp.logical_not(first_seq))
        def _():
            dwg_ref[...] += jnp.concatenate(dw[0:3], axis=0)
            dwv_ref[...] += jnp.concatenate(dw[3:6], axis=0)
            dbg_ref[...] += db[0]
            dbv_ref[...] += db[1]

    zs = pl.BlockSpec((seq, tn), lambda j, b: (b, j))
    ws = pl.BlockSpec((3, tn), lambda j, b: (0, j))
    bs = pl.BlockSpec((1, tn), lambda j, b: (0, j))
    outs = pl.pallas_call(
        body, name=name, grid=(D_FF // tn, n_seq),
        in_specs=[zs] * 5 + [ws, ws],
        out_specs=[zs, zs, ws, ws, bs, bs],
        out_shape=[jax.ShapeDtypeStruct((T, D_FF), BF16)] * 2 + [jax.ShapeDtypeStruct((3, D_FF), F32)] * 2
        + [jax.ShapeDtypeStruct((1, D_FF), F32)] * 2,
        scratch_shapes=[pltpu.VMEM((seq, tn), F32), pltpu.VMEM((seq, tn), F32)],
        compiler_params=_params(("parallel", "arbitrary")),
    )(z_g, z_v, c_g, c_v, da, cw_g, cw_v)
    dz_g, dz_v, dw_g, dw_v, db_g, db_v = outs
    return dz_g, dz_v, dw_g, dw_v, db_g.reshape(D_FF), db_v.reshape(D_FF)


def _layer_fwd(x, h, w, sched, tail, *, n_seq, seq, l):
    tag = f"l{l}"
    deps = sched("fwd_start", l, x)
    proj = _mm(h, w["w_in_t"], mode="nt", out_dtype=ACT_DTYPE, rotate=W_IN_ROTATE, name=f"{tag}_proj", deps=deps)
    y_att = _attention_fwd(proj, w["q_norm"], w["k_norm"], w["sinks"], n_seq=n_seq, seq=seq, name=f"{tag}_att")
    deps = sched("fwd_att", l, y_att)
    y_sgu = _sgu_fwd(proj, w["sgu_norm"], w["w_s"], w["bias_full"], n_seq=n_seq, seq=seq, name=f"{tag}_sgu")
    merged = _merge_fwd(y_att, y_sgu, w["w_oa"], w["w_ob"], proj, name=f"{tag}_merge", deps=deps)
    x1, h2 = _mm_rows(merged, w["w_out"], mode="nn", fn=_residual_then_norm, out_dtypes=(F32, BF16), rows=(x,),
                      vecs=(w["ffn_norm"],), name=f"{tag}_out")
    deps = sched("fwd_mixer_done", l, x1)
    z_g = _mm(h2, w["w_up_t"], mode="nt", out_dtype=ACT_DTYPE, b_rows=(0, D_FF), name=f"{tag}_up_g", deps=deps)
    z_v = _mm(h2, w["w_up_t"], mode="nt", out_dtype=ACT_DTYPE, b_rows=(D_FF, D_FF), name=f"{tag}_up_v")
    a, c_g, c_v = _conv_fwd(z_g, z_v, w["cw_g"], w["cw_v"], w["cb_g"], w["cb_v"], n_seq=n_seq, seq=seq,
                            name=f"{tag}_conv")
    deps = sched("fwd_conv", l, a)
    if tail[0] == "norm":
        out = _mm_rows(a, w["w_down"], mode="nn", fn=_residual_then_norm, out_dtypes=(F32, BF16), rows=(x1,),
                       vecs=(tail[1],), name=f"{tag}_down", deps=deps)
    else:
        out = _mm_rows(a, w["w_down"], mode="nn", fn=_residual_then_loss, out_dtypes=(F32, BF16), rows=(x1, tail[1]),
                       reduce=True, name=f"{tag}_down", deps=deps)
    saved = dict(x=x, h=h, proj=proj, y_att=y_att, y_sgu=y_sgu, merged=merged, x1=x1, h2=h2, z_g=z_g, z_v=z_v,
                 c_g=c_g, c_v=c_v, a=a)
    return out, saved


def _layer_bwd(dx2, dx2_bf, w, s, sched, deps, *, n_seq, seq, l):
    tag = f"l{l}b"
    g = {}
    da = _mm(dx2_bf, w["w_down"], mode="nt", out_dtype=ACT_DTYPE, name=f"{tag}_da", deps=deps)
    g["w_down"] = _mm(s["a"], dx2_bf, mode="tn", out_dtype=F32, name=f"{tag}_dw_down")
    dz_g, dz_v, g["cw_g"], g["cw_v"], g["cb_g"], g["cb_v"] = _conv_bwd(
        s["z_g"], s["z_v"], s["c_g"], s["c_v"], da, w["cw_g"], w["cw_v"], n_seq=n_seq, seq=seq, name=f"{tag}_conv")
    dw_up_t = _mm(dz_g, s["h2"], mode="tn", out_dtype=F32, out_rows=(0, 2 * D_FF), name=f"{tag}_dw_up_g")
    g["w_up_t"] = _mm(dz_v, s["h2"], mode="tn", out_dtype=F32, out_rows=(D_FF, 2 * D_FF), out_prev=dw_up_t,
                      name=f"{tag}_dw_up_v")
    deps = sched("bwd_ffn_grads", l, dz_v, g)
    dx1, dx1_bf, dgain = _mm_rows((dz_g, dz_v), w["w_up_t"], mode="nn", fn=_rms_bwd_rows, out_dtypes=(F32, BF16),
                                  rows=(s["x1"], dx2), vecs=(w["ffn_norm"],), reduce=True, a_at=(0, D_FF),
                                  name=f"{tag}_dh2", deps=deps)
    g["ffn_norm"] = dgain.reshape(D_MODEL)
    dpa, dpb, dga, dgb = _merge_bwd(dx1_bf, w["w_out"], s["y_att"], s["y_sgu"], w["w_oa"], w["w_ob"], s["proj"],
                                    name=f"{tag}_merge")
    deps = sched("bwd_merge", l, dpa)
    g["w_out"] = _mm(s["merged"], dx1_bf, mode="tn", out_dtype=F32, name=f"{tag}_dw_out",
                     deps=deps)
    dy_att = _mm(dpa, w["w_oa"], mode="nt", out_dtype=BF16, name=f"{tag}_dy_att")
    dy_sgu = _mm(dpb, w["w_ob"], mode="nt", out_dtype=F32, name=f"{tag}_dy_sgu")
    g["w_oa"] = _mm(s["y_att"], dpa, mode="tn", out_dtype=F32, name=f"{tag}_dw_oa")
    g["w_ob"] = _mm(s["y_sgu"], dpb, mode="tn", out_dtype=F32, name=f"{tag}_dw_ob")
    deps = sched("bwd_out_grads", l, dy_att, g)
    dqkv, g["q_norm"], g["k_norm"], g["sinks"] = _attention_bwd(
        s["proj"], dy_att, w["q_norm"], w["k_norm"], w["sinks"], n_seq=n_seq, seq=seq, name=f"{tag}_att", deps=deps)
    deps = sched("bwd_att", l, dqkv)
    dsuv, g["sgu_norm"], g["w_s"], g["b_s"] = _sgu_bwd(
        s["proj"], dy_sgu, w["sgu_norm"], w["w_s"], w["bias_full"], n_seq=n_seq, seq=seq, name=f"{tag}_sgu", deps=deps)
    dproj = (dsuv, dga, dgb, dqkv)
    at = (QKV_WIDTH, QKV_WIDTH + 2 * SGU_WIDTH, QKV_WIDTH + 2 * SGU_WIDTH + D_MODEL, 0)
    g["w_in_t"] = _mm_tn_parts(dproj, at, s["h"], name=f"{tag}_dw_in")
    deps = sched("bwd_w_in_grad", l, dqkv, g)
    dx, dx_bf, dgain = _mm_rows(dproj, w["w_in_t"], mode="nn", fn=_rms_bwd_rows, out_dtypes=(F32, BF16),
                                rows=(s["x"], dx1), vecs=(w["mix_norm"],), reduce=True, a_at=at,
                                name=f"{tag}_dh", deps=deps)
    g["mix_norm"] = dgain.reshape(D_MODEL)
    return dx, dx_bf, g, sched("bwd_dh", l, dx)


def _local_step(x, target, weights, sched, *, n_seq, seq):
    depth = len(weights)
    saved = []
    h = _rms_fwd(x, weights[0]["mix_norm"], name="l0_mix_norm")
    for l in range(depth):
        tail = ("norm", weights[l + 1]["mix_norm"]) if l + 1 < depth else ("loss", target)
        out, s = _layer_fwd(x, h, weights[l], sched, tail, n_seq=n_seq, seq=seq, l=l)
        saved.append(s)
        if l + 1 < depth:
            x, h = out
    dy, dy_bf, loss_cols = out
    grads = [None] * depth
    deps = ()
    for l in reversed(range(depth)):
        dy, dy_bf, grads[l], deps = _layer_bwd(dy, dy_bf, weights[l], saved[l], sched, deps, n_seq=n_seq, seq=seq, l=l)
    return jnp.sum(loss_cols), dy, grads, deps


W_IN_SHARD = IN_WIDTH // N_DEV
W_UP_SHARD = 2 * D_FF // N_DEV
COL_MOVE_ROWS = 256


def _w_o_moves():
    return tuple((j, 0, LANES, 0, j * LANES) for j in range(N_DEV))


def _disassemble(mats, w, moves, *, name):
    R = mats[0].shape[0]
    tr = min(R, COL_MOVE_ROWS)
    n = len(mats)

    def body(*refs):
        m_refs, o_ref = refs[:n], refs[n]
        for j, lo, hi, which, at in moves:
            o_ref[j, :, lo:hi] = m_refs[which][:, at:at + hi - lo]

    return pl.pallas_call(
        body, name=name, grid=(R // tr,),
        in_specs=[pl.BlockSpec((tr, m.shape[1]), lambda i: (i, 0)) for m in mats],
        out_specs=pl.BlockSpec((N_DEV, tr, w), lambda i: (0, i, 0)),
        out_shape=jax.ShapeDtypeStruct((N_DEV, R, w), mats[0].dtype),
        compiler_params=_params(("parallel",)),
    )(*mats)


def _my_place():
    return lax.axis_index("x"), lax.axis_index("y"), lax.axis_index("c")


def _gathered_shape(shape, kind):
    r, c = shape
    return {"blocks": (N_DEV, r, c), "rows": (N_DEV * r, c), "cols": (r, N_DEV * c)}[kind]


def _gather_window(ref, kind, shape, j):
    r, c = shape
    if kind == "blocks":
        return ref.at[j]
    if kind == "rows":
        return ref.at[pl.ds(pl.multiple_of(j * r, r), r), :]
    return ref.at[:, pl.ds(pl.multiple_of(j * c, c), c)]


def _gather(srcs, kinds, *, name):
    n = len(srcs)
    shapes = [s.shape for s in srcs]
    per = 7

    def body(*refs):
        src_refs, dst_refs = refs[:n], refs[n:2 * n]
        send_sems, recv_sems, local_sems = refs[2 * n:]
        x, y, c = _my_place()
        me, sibling = (x, y, c), (x, y, 1 - c)
        chips = [(1 - x, y), (x, 1 - y), (1 - x, 1 - y)]

        def at(i, px, py, pc):
            return _gather_window(dst_refs[i], kinds[i], shapes[i], 4 * px + 2 * py + pc)

        def copy(i, k, block, to, src=None):
            return pltpu.make_async_remote_copy(
                src_ref=at(i, *block) if src is None else src, dst_ref=at(i, *block),
                send_sem=send_sems.at[per * i + k], recv_sem=recv_sems.at[per * i + k], device_id=to, device_id_type=MESH)

        mine = [pltpu.make_async_copy(src_refs[i], at(i, *me), local_sems.at[i]) for i in range(n)]
        for cp in mine:
            cp.start()
        started = []
        for i in range(n):
            first = [copy(i, 0, me, sibling, src=src_refs[i])]
            first += [copy(i, 1 + j, me, (*chip, c), src=src_refs[i]) for j, chip in enumerate(chips)]
            for cp in first:
                cp.start()
            started += first
        for i in range(n):
            for j, chip in enumerate(chips):
                copy(i, 1 + j, (*chip, c), me).wait_recv()
                fwd = copy(i, 4 + j, (*chip, c), sibling)
                fwd.start()
                started.append(fwd)
        for i in range(n):
            copy(i, 0, sibling, me).wait_recv()
            for j, chip in enumerate(chips):
                copy(i, 4 + j, (*chip, 1 - c), me).wait_recv()
        for cp in started:
            cp.wait_send()
        for cp in mine:
            cp.wait()

    return pl.pallas_call(
        body, name=name,
        out_shape=[jax.ShapeDtypeStruct(_gathered_shape(s.shape, k), s.dtype) for s, k in zip(srcs, kinds)],
        in_specs=[ANY] * n, out_specs=[ANY] * n,
        scratch_shapes=[pltpu.SemaphoreType.DMA((per * n,)), pltpu.SemaphoreType.DMA((per * n,)),
                        pltpu.SemaphoreType.DMA((n,))],
    )(*srcs)


HBM = pl.BlockSpec(memory_space=pltpu.HBM)
SEM = pl.BlockSpec(memory_space=pltpu.SEMAPHORE)
TOKEN = jax.ShapeDtypeStruct((SUBLANES, LANES), F32)
TOKEN_SPEC = pl.BlockSpec(memory_space=pltpu.VMEM)
SPLIT_PARAMS = pltpu.CompilerParams(has_side_effects=pltpu.SideEffectType.DATAFLOW_SIDE_EFFECTING)


def _in_hbm(x):
    return pltpu.with_memory_space_constraint(x, pltpu.HBM)


def _hbm_like(shape, dtype):
    return pltpu.HBM(shape, dtype)


def _place_own(shards, kinds, dtypes, *, name):
    n = len(shards)
    shapes = [s.shape for s in shards]

    def body(*refs):
        s_refs, land_refs, bufs, sems = refs[:n], refs[n:2 * n], refs[2 * n:3 * n], refs[3 * n]
        x, y, c = _my_place()
        copies = []
        for i in range(n):
            bufs[i][...] = s_refs[i][...].astype(dtypes[i])
            copies.append(pltpu.make_async_copy(
                bufs[i], _gather_window(land_refs[i], kinds[i], shapes[i], 4 * x + 2 * y + c), sems.at[i]))
        for cp in copies:
            cp.start()
        for cp in copies:
            cp.wait()

    return pl.pallas_call(
        body, name=name,
        out_shape=[jax.ShapeDtypeStruct(_gathered_shape(s, k), d) for s, k, d in zip(shapes, kinds, dtypes)],
        in_specs=[pl.BlockSpec(memory_space=pltpu.VMEM)] * n, out_specs=[ANY] * n,
        scratch_shapes=[pltpu.VMEM(s, d) for s, d in zip(shapes, dtypes)] + [pltpu.SemaphoreType.DMA((n,))],
        compiler_params=_params(),
    )(*shards)


def _gather_start(lands, kinds, shapes, after=(), *, name):
    n = len(lands)
    n_after = len(after)

    def body(*refs):
        land_refs = refs[:n]
        send_sems, recv_sems = refs[n + n_after], refs[n + n_after + 1]
        x, y, c = _my_place()
        targets = [(x, y, 1 - c), (1 - x, y, c), (x, 1 - y, c), (1 - x, 1 - y, c)]
        for i in range(n):
            own = _gather_window(land_refs[i], kinds[i], shapes[i], 4 * x + 2 * y + c)
            for k, to in enumerate(targets):
                pltpu.make_async_remote_copy(
                    src_ref=own, dst_ref=own, send_sem=send_sems.at[4 * i + k], recv_sem=recv_sems.at[4 * i + k],
                    device_id=to, device_id_type=MESH).start()
        refs[-1][...] = jnp.zeros_like(refs[-1])

    outs = pl.pallas_call(
        body, name=name,
        out_shape=[pltpu.SemaphoreType.DMA((4 * n,)), pltpu.SemaphoreType.DMA((4 * n,))]
        + [_hbm_like(a.shape, a.dtype) for a in lands] + [TOKEN],
        in_specs=[HBM] * n + [ANY] * n_after, out_specs=[SEM, SEM] + [HBM] * n + [TOKEN_SPEC],
        input_output_aliases={i: 2 + i for i in range(n)},
        compiler_params=SPLIT_PARAMS,
    )(*[_in_hbm(a) for a in lands], *after)
    return outs[0], outs[1], outs[2:2 + n], outs[-1]


def _gather_forward(recv_sems, lands, kinds, shapes, after, *, name):
    n = len(lands)

    def body(*refs):
        recv_ref, land_refs = refs[0], refs[1:1 + n]
        fwd_send, fwd_recv = refs[2 + n], refs[3 + n]
        token = refs[-1]
        x, y, c = _my_place()
        chips = [(1 - x, y), (x, 1 - y), (1 - x, 1 - y)]
        for i in range(n):
            for j, (px, py) in enumerate(chips):
                block = _gather_window(land_refs[i], kinds[i], shapes[i], 4 * px + 2 * py + c)
                pltpu.make_async_remote_copy(
                    src_ref=block, dst_ref=block, send_sem=fwd_send.at[3 * i + j], recv_sem=recv_ref.at[4 * i + 1 + j],
                    device_id=(px, py, c), device_id_type=MESH).wait_recv()
                pltpu.make_async_remote_copy(
                    src_ref=block, dst_ref=block, send_sem=fwd_send.at[3 * i + j], recv_sem=fwd_recv.at[3 * i + j],
                    device_id=(x, y, 1 - c), device_id_type=MESH).start()
        token[...] = jnp.zeros_like(token)

    outs = pl.pallas_call(
        body, name=name,
        out_shape=[pltpu.SemaphoreType.DMA((3 * n,)), pltpu.SemaphoreType.DMA((3 * n,))]
        + [_hbm_like(a.shape, a.dtype) for a in lands] + [TOKEN],
        in_specs=[SEM] + [HBM] * n + [ANY], out_specs=[SEM, SEM] + [HBM] * n + [TOKEN_SPEC],
        input_output_aliases={1 + i: 2 + i for i in range(n)},
        compiler_params=SPLIT_PARAMS,
    )(recv_sems, *lands, after)
    return outs[0], outs[1], outs[2:2 + n], outs[-1]


def _gather_finish(send_sems, recv_sems, fwd_send, fwd_recv, lands, kinds, shapes, after, *, name):
    n = len(lands)

    def body(*refs):
        send_ref, recv_ref, fsend_ref, frecv_ref = refs[:4]
        land_refs = refs[4:4 + n]
        x, y, c = _my_place()
        chips = [(1 - x, y), (x, 1 - y), (1 - x, 1 - y)]
        sibling = (x, y, 1 - c)
        for i in range(n):
            def window(j):
                return _gather_window(land_refs[i], kinds[i], shapes[i], j)

            mine, theirs = window(4 * x + 2 * y + c), window(4 * x + 2 * y + (1 - c))
            pltpu.make_async_remote_copy(src_ref=mine, dst_ref=theirs, send_sem=send_ref.at[4 * i],
                                         recv_sem=recv_ref.at[4 * i], device_id=sibling, device_id_type=MESH).wait_recv()
            for j, (px, py) in enumerate(chips):
                block = window(4 * px + 2 * py + (1 - c))
                pltpu.make_async_remote_copy(src_ref=block, dst_ref=block, send_sem=fsend_ref.at[3 * i + j],
                                             recv_sem=frecv_ref.at[3 * i + j], device_id=sibling,
                                             device_id_type=MESH).wait_recv()
            for k in range(4):
                pltpu.make_async_remote_copy(src_ref=mine, dst_ref=mine, send_sem=send_ref.at[4 * i + k],
                                             recv_sem=recv_ref.at[4 * i + k], device_id=sibling,
                                             device_id_type=MESH).wait_send()
            for j, (px, py) in enumerate(chips):
                block = window(4 * px + 2 * py + c)
                pltpu.make_async_remote_copy(src_ref=block, dst_ref=block, send_sem=fsend_ref.at[3 * i + j],
                                             recv_sem=frecv_ref.at[3 * i + j], device_id=sibling,
                                             device_id_type=MESH).wait_send()

    return pl.pallas_call(
        body, name=name,
        out_shape=[_hbm_like(a.shape, a.dtype) for a in lands],
        in_specs=[SEM] * 4 + [HBM] * n + [ANY], out_specs=[HBM] * n,
        input_output_aliases={4 + i: i for i in range(n)},
        compiler_params=SPLIT_PARAMS,
    )(send_sems, recv_sems, fwd_send, fwd_recv, *lands, after)


def _pair_plan(src_ref, land_ref, x, y, c):
    return [(src_ref.at[2 * k + (1 - c)], land_ref.at[k], (x, y, 1 - c)) for k in range(N_CHIPS)]


def _chip_plan(src_ref, land_ref, x, y, c):
    chips = [(1 - x, y), (x, 1 - y), (1 - x, 1 - y)]
    return [(src_ref.at[2 * px + py], land_ref.at[k], (px, py, c)) for k, (px, py) in enumerate(chips)]


def _exchange_copies(plan, per, src_refs, land_refs, send_sems, recv_sems):
    x, y, c = _my_place()
    copies = []
    for i, (s_ref, l_ref) in enumerate(zip(src_refs, land_refs)):
        for q, (src, dst, to) in enumerate(plan(s_ref, l_ref, x, y, c)):
            copies.append(pltpu.make_async_remote_copy(
                src_ref=src, dst_ref=dst, send_sem=send_sems.at[per * i + q], recv_sem=recv_sems.at[per * i + q],
                device_id=to, device_id_type=MESH))
    return copies


def _exchange_start(srcs, plan, per, *, name):
    n = len(srcs)

    def body(*refs):
        src_refs, land_refs = refs[:n], refs[n:2 * n]
        send_sems, recv_sems = refs[2 * n], refs[2 * n + 1]
        for cp in _exchange_copies(plan, per, src_refs, land_refs, send_sems, recv_sems):
            cp.start()
        refs[-1][...] = jnp.zeros_like(refs[-1])

    lands = [lax.empty((per,) + s.shape[1:], s.dtype) for s in srcs]
    outs = pl.pallas_call(
        body, name=name,
        out_shape=[pltpu.SemaphoreType.DMA((per * n,)), pltpu.SemaphoreType.DMA((per * n,))]
        + [_hbm_like(s.shape, s.dtype) for s in srcs] + [_hbm_like(a.shape, a.dtype) for a in lands] + [TOKEN],
        in_specs=[HBM] * (2 * n), out_specs=[SEM, SEM] + [HBM] * (2 * n) + [TOKEN_SPEC],
        input_output_aliases={i: 2 + i for i in range(2 * n)},
        compiler_params=SPLIT_PARAMS,
    )(*[_in_hbm(s) for s in srcs], *[_in_hbm(a) for a in lands])
    return outs[0], outs[1], outs[2:2 + n], outs[2 + n:2 + 2 * n], outs[-1]


def _exchange_wait(send_sems, recv_sems, srcs, lands, plan, per, after, *, name):
    n = len(srcs)
    after = list(after) if isinstance(after, (list, tuple)) else [after]

    def body(*refs):
        send_ref, recv_ref = refs[0], refs[1]
        src_refs, land_refs = refs[2:2 + n], refs[2 + n:2 + 2 * n]
        copies = _exchange_copies(plan, per, src_refs, land_refs, send_ref, recv_ref)
        for cp in copies:
            cp.wait_recv()
        for cp in copies:
            cp.wait_send()

    outs = pl.pallas_call(
        body, name=name,
        out_shape=[_hbm_like(s.shape, s.dtype) for s in srcs] + [_hbm_like(a.shape, a.dtype) for a in lands],
        in_specs=[SEM, SEM] + [HBM] * (2 * n) + [ANY] * len(after), out_specs=[HBM] * (2 * n),
        input_output_aliases={2 + i: i for i in range(2 * n)},
        compiler_params=SPLIT_PARAMS,
    )(send_sems, recv_sems, *srcs, *lands, *after)
    return outs[:n], outs[n:]


REDUCE_BLOCK_BYTES = 1 << 20


def _row_tile(r, c):
    row_bytes = 4 * (-(-c // LANES) * LANES)
    best = r
    for d in range(SUBLANES, r, SUBLANES):
        if r % d == 0 and d * row_bytes <= REDUCE_BLOCK_BYTES:
            best = d
    return best if r * row_bytes > REDUCE_BLOCK_BYTES else r


def _reduce_pair_sum(blocked, recv, place, wire_dtype, *, name):
    _, r, c = blocked.shape
    tr = _row_tile(r, c)

    def body(place_ref, g_ref, r_ref, own_ref, send_ref):
        s = g_ref[...] + r_ref[...]
        send_ref[...] = s.astype(wire_dtype)

        @pl.when(pl.program_id(1) == place_ref[1])
        def _():
            own_ref[...] = s

    return pl.pallas_call(
        body, name=name,
        grid_spec=pltpu.PrefetchScalarGridSpec(
            num_scalar_prefetch=1, grid=(r // tr, N_CHIPS),
            in_specs=[pl.BlockSpec((None, None, tr, c), lambda i, k, place_ref: (k, place_ref[0], i, 0)),
                      pl.BlockSpec((None, tr, c), lambda i, k, place_ref: (k, i, 0))],
            out_specs=[pl.BlockSpec((tr, c), lambda i, k, place_ref: (i, 0)),
                       pl.BlockSpec((None, tr, c), lambda i, k, place_ref: (k, i, 0))]),
        out_shape=[jax.ShapeDtypeStruct((r, c), F32), jax.ShapeDtypeStruct((N_CHIPS, r, c), wire_dtype)],
        compiler_params=_params(("parallel", "arbitrary")),
    )(place, blocked.reshape(N_CHIPS, 2, r, c), recv)


def _chip_sum(own_ref, r_ref):
    return ((own_ref[...] + r_ref[0].astype(F32)) + r_ref[1].astype(F32)) + r_ref[2].astype(F32)


def _reduce_chip_sum(own, recv, *, name):
    r, c = own.shape
    tr = _row_tile(r, c)

    def body(own_ref, r_ref, o_ref):
        o_ref[...] = _chip_sum(own_ref, r_ref)

    return pl.pallas_call(
        body, name=name, grid=(r // tr,),
        in_specs=[pl.BlockSpec((tr, c), lambda i: (i, 0)), pl.BlockSpec((N_CHIPS - 1, tr, c), lambda i: (0, i, 0))],
        out_specs=pl.BlockSpec((tr, c), lambda i: (i, 0)),
        out_shape=jax.ShapeDtypeStruct((r, c), F32),
        compiler_params=_params(("parallel",)),
    )(own, recv)


def _adamw_math(w, g, m, v):
    nm = ADAM_B1 * m + (1.0 - ADAM_B1) * g
    nv = ADAM_B2 * v + (1.0 - ADAM_B2) * (g * g)
    m_hat = nm / (1.0 - ADAM_B1 ** ADAM_STEP)
    v_hat = nv / (1.0 - ADAM_B2 ** ADAM_STEP)
    return -ADAM_LR * (m_hat / (jnp.sqrt(v_hat) + ADAM_EPS) + ADAM_WD * w), nm, nv


def _adamw(w, g, m, v, *, name):
    shape = w.shape
    C = shape[-1]
    R = math.prod(shape[:-1])
    tr = _row_tile(R, C)

    def body(w_ref, g_ref, m_ref, v_ref, d_ref, nm_ref, nv_ref):
        d_ref[...], nm_ref[...], nv_ref[...] = _adamw_math(w_ref[...], g_ref[...], m_ref[...], v_ref[...])

    spec = pl.BlockSpec((tr, C), lambda i: (i, 0))
    outs = pl.pallas_call(
        body, name=name, grid=(R // tr,),
        in_specs=[spec] * 4, out_specs=[spec] * 3,
        out_shape=[jax.ShapeDtypeStruct((R, C), F32)] * 3,
        compiler_params=_params(("parallel",)),
    )(*[a.reshape(R, C) for a in (w, g, m, v)])
    return tuple(o.reshape(shape) for o in outs)


def _reduce_adamw(own, recv, w, m, v, layer, prev, *, name):
    r, c = own.shape
    tr = _row_tile(r, c)
    n_prev = 0 if prev is None else len(prev)

    def body(own_ref, r_ref, w_ref, m_ref, v_ref, *rest):
        g_ref, d_ref, nm_ref, nv_ref = rest[n_prev:]
        g = _chip_sum(own_ref, r_ref)
        g_ref[...] = g
        d_ref[...], nm_ref[...], nv_ref[...] = _adamw_math(w_ref[...], g, m_ref[...], v_ref[...])

    slot = pl.BlockSpec((None, tr, c), lambda i: (layer, i, 0))
    return pl.pallas_call(
        body, name=name, grid=(r // tr,),
        in_specs=[pl.BlockSpec((tr, c), lambda i: (i, 0)), pl.BlockSpec((N_CHIPS - 1, tr, c), lambda i: (0, i, 0)),
                  slot, slot, slot] + [ANY] * n_prev,
        out_specs=[slot] * 4,
        out_shape=[jax.ShapeDtypeStruct((DEPTH, r, c), F32)] * 4,
        input_output_aliases={5 + k: k for k in range(n_prev)},
        compiler_params=_params(("parallel",)),
    )(own, recv, w, m, v, *(prev or ()))


REPLICATED = (("mix_norm", (D_MODEL,)), ("q_norm", (HEAD_DIM,)), ("k_norm", (HEAD_DIM,)), ("sinks", (N_Q_HEADS,)),
              ("sgu_norm", (SGU_WIDTH,)), ("w_s", (SGU_GROUPS, BLOCK, BLOCK)), ("b_s", (SGU_GROUPS, BLOCK)),
              ("ffn_norm", (D_MODEL,)), ("conv_b", (2 * D_FF,)))
TRANSPOSED = ("w_in", "w_up")
SHARDED = (("w_in", "rows"), ("w_oa", "cols"), ("w_ob", "cols"), ("w_out", "rows"), ("w_up", "rows"),
           ("conv_w", "blocks"), ("w_down", "rows"))
WEIGHT_ORDER = ("mix_norm", "w_in", "q_norm", "k_norm", "sinks", "sgu_norm", "w_s", "b_s", "w_oa", "w_ob", "w_out",
                "ffn_norm", "w_up", "conv_w", "conv_b", "w_down")
MIXER_WEIGHTS = ["w_in", "w_oa", "w_ob", "w_out"]
FFN_WEIGHTS = ["w_up", "conv_w", "w_down"]


def _small_layout():
    segs, off = {}, 0
    for l in range(DEPTH):
        for name, shape in REPLICATED:
            n = math.prod(shape)
            segs[(l, name)] = (off, n)
            off += n
    per_dev = -(-off // (N_DEV * SUBLANES * LANES)) * SUBLANES * LANES
    return segs, off, per_dev


def _pack_small(grads):
    ssegs, total, per_dev = _small_layout()
    flat = jnp.concatenate([grads[l][name].reshape(-1) for (l, name) in ssegs])
    return jnp.pad(flat, (0, N_DEV * per_dev - total)).reshape(N_DEV, per_dev // LANES, LANES)


def _unpack_small(gathered):
    ssegs, _, _ = _small_layout()
    flat = gathered.reshape(-1)
    shapes = dict(REPLICATED)
    return {name: jnp.stack([flat[ssegs[(l, name)][0]:ssegs[(l, name)][0] + ssegs[(l, name)][1]].reshape(shapes[name])
                             for l in range(DEPTH)]) for name, _ in REPLICATED}


def kernel(x, mix_norm, w_in, q_norm, k_norm, sinks, sgu_norm, w_s, b_s, w_oa, w_ob, w_out, ffn_norm, w_up, conv_w, conv_b, w_down, loss_target, m_mix_norm, m_w_in, m_q_norm, m_k_norm, m_sinks, m_sgu_norm, m_w_s, m_b_s, m_w_oa, m_w_ob, m_w_out, m_ffn_norm, m_w_up, m_conv_w, m_conv_b, m_w_down, v_mix_norm, v_w_in, v_q_norm, v_k_norm, v_sinks, v_sgu_norm, v_w_s, v_b_s, v_w_oa, v_w_ob, v_w_out, v_ffn_norm, v_w_up, v_conv_w, v_conv_b, v_w_down):
    W = dict(mix_norm=mix_norm, w_in=w_in, q_norm=q_norm, k_norm=k_norm, sinks=sinks, sgu_norm=sgu_norm, w_s=w_s, b_s=b_s,
             w_oa=w_oa, w_ob=w_ob, w_out=w_out, ffn_norm=ffn_norm, w_up=w_up, conv_w=conv_w, conv_b=conv_b, w_down=w_down)
    M = dict(mix_norm=m_mix_norm, w_in=m_w_in, q_norm=m_q_norm, k_norm=m_k_norm, sinks=m_sinks, sgu_norm=m_sgu_norm,
             w_s=m_w_s, b_s=m_b_s, w_oa=m_w_oa, w_ob=m_w_ob, w_out=m_w_out, ffn_norm=m_ffn_norm, w_up=m_w_up,
             conv_w=m_conv_w, conv_b=m_conv_b, w_down=m_w_down)
    V = dict(mix_norm=v_mix_norm, w_in=v_w_in, q_norm=v_q_norm, k_norm=v_k_norm, sinks=v_sinks, sgu_norm=v_sgu_norm,
             w_s=v_w_s, b_s=v_b_s, w_oa=v_w_oa, w_ob=v_w_ob, w_out=v_w_out, ffn_norm=v_ffn_norm, w_up=v_w_up,
             conv_w=v_conv_w, conv_b=v_conv_b, w_down=v_w_down)
    n_seq, seq, d_model = x.shape
    tokens = n_seq * seq
    mx, my, mc = _my_place()
    place = jnp.stack([mc, 2 * mx + my]).astype(jnp.int32)
    half = N_DEV // 2
    kind_of = dict(SHARDED)
    for name in TRANSPOSED:
        W[name], M[name], V[name] = (jnp.swapaxes(t[name], 1, 2) for t in (W, M, V))

    gather_groups = [[(0, MIXER_WEIGHTS[0])], [(0, n) for n in MIXER_WEIGHTS[1:]], [(0, n) for n in FFN_WEIGHTS],
                     [(1, n) for n in MIXER_WEIGHTS], [(1, n) for n in FFN_WEIGHTS]]
    started, in_flight = {}, {}
    weights = []
    for l in range(DEPTH):
        w = {name: W[name][l] for name, _ in REPLICATED}
        w["cb_g"], w["cb_v"] = W["conv_b"][l][:D_FF], W["conv_b"][l][D_FF:]
        w["bias_full"] = jnp.repeat(W["b_s"][l].T, SGU_WIDTH // SGU_GROUPS, axis=1)
        weights.append(w)

    def gather_start(gi, after=()):
        shards = [W[name][l] for l, name in gather_groups[gi]]
        kinds = [kind_of[name] for _, name in gather_groups[gi]]
        shapes = [s.shape for s in shards]
        lands = _place_own(shards, kinds, [F32 if name == "conv_w" else BF16 for _, name in gather_groups[gi]],
                           name=f"gather_weights_own_{gi}")
        send, recv, lands, token = _gather_start(lands, kinds, shapes, after, name=f"gather_weights_start_{gi}")
        started[gi] = dict(sems=(send, recv), lands=lands, kinds=kinds, shapes=shapes)
        return token

    def gather_forward(gi, after):
        st = started[gi]
        in_flight[gi] = _gather_forward(st["sems"][1], st["lands"], st["kinds"], st["shapes"], after,
                                        name=f"gather_weights_forward_{gi}")
        return in_flight[gi][3]

    def gather_finish(gi, after):
        st = started.pop(gi)
        fwd_send, fwd_recv, lands_g, _ = in_flight.pop(gi)
        whole = _gather_finish(st["sems"][0], st["sems"][1], fwd_send, fwd_recv, lands_g, st["kinds"], st["shapes"], after,
                               name=f"gather_weights_finish_{gi}")
        for (l, name), arr in zip(gather_groups[gi], whole):
            w = weights[l]
            if name in TRANSPOSED:
                w[name + "_t"] = arr
            elif name == "conv_w":
                w["cw_g"] = arr[:half].transpose(1, 0, 2).reshape(3, D_FF)
                w["cw_v"] = arr[half:].transpose(1, 0, 2).reshape(3, D_FF)
            else:
                w[name] = arr

    reduce_state, results = {}, {}
    wire = {"conv_w": F32, "small": F32}

    def reduce_begin(key, names, arrays):
        send, recv, srcs_, lands_, token = _exchange_start(arrays, _pair_plan, N_CHIPS, name=f"reduce_pair_start_{key}")
        reduce_state[key] = dict(names=names, pair=(send, recv, srcs_, lands_))
        return [token]

    def reduce_pair(key, after):
        st = reduce_state[key]
        send, recv, srcs_, lands_ = st.pop("pair")
        blocked_, from_sibling = _exchange_wait(send, recv, srcs_, lands_, _pair_plan, N_CHIPS, after,
                                                name=f"reduce_pair_wait_{key}")
        sums = [_reduce_pair_sum(b, r, place, wire.get(n if isinstance(n, str) else n[1], BF16),
                                 name=f"reduce_pair_sum_{key}_{i}")
                for i, (n, b, r) in enumerate(zip(st["names"], blocked_, from_sibling))]
        st["own"] = [s[0] for s in sums]
        *st["chip"], token = _exchange_start([s[1] for s in sums], _chip_plan, N_CHIPS - 1, name=f"reduce_chip_start_{key}")
        return [token]

    def reduce_end(key, after):
        st = reduce_state.pop(key)
        send, recv, srcs_, lands_ = st["chip"]
        _, from_chips = _exchange_wait(send, recv, srcs_, lands_, _chip_plan, N_CHIPS - 1, after,
                                       name=f"reduce_chip_wait_{key}")
        done = []
        for n, own, got in zip(st["names"], st["own"], from_chips):
            if n == "small":
                results["small"] = _reduce_chip_sum(own, got, name="reduce_chip_sum_small")
            else:
                l, name = n
                results[name] = _reduce_adamw(own, got, W[name], M[name], V[name], l, results.get(name),
                                              name=f"l{l}_reduce_adamw_{name}")
                done.append(results[name][0])
        return done

    def sched(point, l, carry, g=None):
        deps = []
        if point == "fwd_start" and l == 0:
            token = gather_start(1, [gather_start(0)])
            gather_finish(0, gather_forward(0, token))
            deps = [gather_start(2, [weights[0]["w_in_t"]])]
        elif point == "fwd_att" and l == 0:
            gather_finish(1, gather_forward(1, carry))
            deps = [gather_forward(2, carry), gather_start(3, [carry])]
        elif point == "fwd_mixer_done" and l == 0:
            gather_finish(2, carry)
            deps = [gather_start(4, [carry])]
        elif point == "fwd_conv" and l == 0:
            deps = [gather_forward(3, carry)]
        elif point == "fwd_start" and l == 1:
            gather_finish(3, carry)
        elif point == "fwd_att" and l == 1:
            deps = [gather_forward(4, carry)]
        elif point == "fwd_mixer_done" and l == 1:
            gather_finish(4, carry)
        elif point == "bwd_ffn_grads":
            conv_w = jnp.concatenate([g[k].reshape(3, half, W_UP_SHARD).transpose(1, 0, 2) for k in ("cw_g", "cw_v")])
            deps = reduce_begin(
                f"l{l}_ffn", [(l, "w_down"), (l, "w_up"), (l, "conv_w")],
                [g["w_down"].reshape(N_DEV, D_FF // N_DEV, D_MODEL),
                 g["w_up_t"].reshape(N_DEV, W_UP_SHARD, D_MODEL), conv_w])
        elif point == "bwd_merge":
            deps = reduce_pair(f"l{l}_ffn", carry)
        elif point == "bwd_out_grads":
            deps = reduce_begin(
                f"l{l}_out", [(l, "w_out"), (l, "w_oa"), (l, "w_ob")],
                [g["w_out"].reshape(N_DEV, D_MODEL // N_DEV, D_MODEL),
                 _disassemble((g["w_oa"],), LANES, _w_o_moves(), name=f"l{l}_split_dw_oa"),
                 _disassemble((g["w_ob"],), LANES, _w_o_moves(), name=f"l{l}_split_dw_ob")])
        elif point == "bwd_att":
            deps = reduce_pair(f"l{l}_out", carry)
        elif point == "bwd_w_in_grad":
            deps = reduce_begin(f"l{l}_in", [(l, "w_in")], [g["w_in_t"].reshape(N_DEV, W_IN_SHARD, D_MODEL)])
        elif point == "bwd_dh":
            deps = reduce_pair(f"l{l}_in", carry)
        return deps

    loss_part, dx, grads, last_deps = _local_step(x.reshape(tokens, d_model), loss_target.reshape(tokens, d_model),
                                                  weights, sched, n_seq=n_seq, seq=seq)
    loss = lax.psum(loss_part, ("x", "y", "c"))

    for g in grads:
        g["conv_b"] = jnp.concatenate([g["cb_g"], g["cb_v"]])
    after = [dx, *last_deps, *reduce_begin("small", ["small"], [_pack_small(grads)])]
    for key in [f"l{l}_{part}" for l in reversed(range(DEPTH)) for part in ("ffn", "out", "in")][:-1]:
        after = reduce_end(key, after)
    after = reduce_end("l0_in", after + reduce_pair("small", after))
    reduce_end("small", after)

    G, delta, new_m, new_v = {}, {}, {}, {}
    for name, _ in SHARDED:
        outs = [jnp.swapaxes(o, 1, 2) for o in results[name]] if name in TRANSPOSED else results[name]
        G[name], delta[name], new_m[name], new_v[name] = outs
    G.update(_unpack_small(_gather([results["small"]], ["blocks"], name="gather_small_grads")[0]))
    for name, _ in REPLICATED:
        delta[name], new_m[name], new_v[name] = _adamw(W[name], G[name], M[name], V[name], name=f"adamw_{name}")
    return (loss, dx.reshape(n_seq, seq, d_model), *[G[n] for n in WEIGHT_ORDER], *[delta[n] for n in WEIGHT_ORDER],
            *[new_m[n] for n in WEIGHT_ORDER], *[new_v[n] for n in WEIGHT_ORDER])
```

```python
import math

import jax
import jax.numpy as jnp
from jax import lax
from jax.experimental import pallas as pl
from jax.experimental.pallas import tpu as pltpu

F32 = jnp.float32
BF16 = jnp.bfloat16
ACT_DTYPE = BF16
MESH = pl.DeviceIdType.MESH

DEPTH = 2
D_MODEL = 1024
N_Q_HEADS = 8
HEAD_DIM = 64
ATT_WIDTH = 512
KV_WIDTH = 128
BLOCK = 128
SGU_WIDTH = 512
SGU_GROUPS = 8
IN_WIDTH = 3840
D_FF = 2816
NORM_EPS = 1e-6
NEG_INF = -1e30
ATT_SCALE = HEAD_DIM ** -0.5
ALIBI_SLOPES = tuple(2.0 ** (-(h + 1)) for h in range(N_Q_HEADS))
ADAM_LR, ADAM_B1, ADAM_B2, ADAM_EPS, ADAM_WD, ADAM_STEP = 0.001, 0.9, 0.999, 1e-08, 0.01, 10
N_DEV = 8
N_CHIPS = 4

QKV_WIDTH = ATT_WIDTH + 2 * KV_WIDTH
COL_SUV, COL_GA, COL_GB, COL_QKV = 0, 1024, 2048, 3072
W_IN_ROTATE = (1, IN_WIDTH // QKV_WIDTH)

LANES = 128
SUBLANES = 8
VMEM_LIMIT_V7X = 56 * 1024 * 1024
GELU_C = math.sqrt(2.0 / math.pi)
GELU_K = 0.044715
ANY = pl.BlockSpec(memory_space=pl.ANY)


def _params(sem=None):
    return pltpu.CompilerParams(dimension_semantics=sem, vmem_limit_bytes=VMEM_LIMIT_V7X)


def _sigmoid(x):
    return 1.0 / (1.0 + jnp.exp(-x))


def _gelu(x):
    th = jnp.tanh(GELU_C * (x + GELU_K * x * x * x))
    return 0.5 * x * (1.0 + th)


def _gelu_and_grad(x):
    x2 = x * x
    th = jnp.tanh(GELU_C * (x + GELU_K * x2 * x))
    g = 0.5 * x * (1.0 + th)
    dg = 0.5 * (1.0 + th) + 0.5 * x * (1.0 - th * th) * (GELU_C * (1.0 + 3.0 * GELU_K * x2))
    return g, dg


def _dot(a, b, dims):
    return lax.dot_general(a, b, (dims, ((), ())), preferred_element_type=F32)


def _dot_nn(a, b):
    return _dot(a, b, ((1,), (0,)))


def _dot_nt(a, b):
    return _dot(a, b, ((1,), (1,)))


def _dot_tn(a, b):
    return _dot(a, b, ((0,), (0,)))


def _lo_mask(shape):
    return lax.broadcasted_iota(jnp.int32, shape, len(shape) - 1) < (LANES // 2)


def _half_sums(x, lo):
    s_lo = jnp.sum(jnp.where(lo, x, 0.0), axis=-1, keepdims=True)
    s_all = jnp.sum(x, axis=-1, keepdims=True)
    return jnp.where(lo, s_lo, s_all - s_lo)


def _dup_half(x, half, lo):
    r = pltpu.roll(x, LANES // 2, axis=1)
    return jnp.where(lo, x, r) if half == 0 else jnp.where(lo, r, x)


def _with_deps(body, n_in, deps):
    k = len(deps)
    if not k:
        return body, [], ()

    def skipping(*refs):
        return body(*refs[:n_in], *refs[n_in + k:])

    return skipping, [ANY] * k, tuple(deps)


MM_VMEM_BUDGET = 40 * 1024 * 1024
MM_MAX_TILE = 1408
MM_MAX_TK = 4096
MM_STEP_BYTES = 1 << 20


def _divisors(n, step, cap):
    return [d for d in range(step, min(n, cap) + 1, step) if n % d == 0] or [n]


def _mm_tiles(M, N, K, out_bytes, tm_divides, tn_divides):
    best = None
    for tm in _divisors(M, LANES, MM_MAX_TILE):
        for tn in _divisors(N, LANES, MM_MAX_TILE):
            if tm_divides % tm or tn_divides % tn:
                continue
            for tk in _divisors(K, 4 * LANES, MM_MAX_TK):
                vmem = 4 * (tm * tk + tk * tn) + 2 * tm * tn * out_bytes + (0 if tk == K else 4 * tm * tn)
                if vmem > MM_VMEM_BUDGET:
                    continue
                traffic = 2 * M * K * (N // tn) + 2 * K * N * (M // tm) + M * N * out_bytes
                cost = traffic + (K // tk - 1) * 8 * M * N + (M // tm) * (N // tn) * (K // tk) * MM_STEP_BYTES
                if best is None or cost < best[0]:
                    best = (cost, tm, tn, tk)
    assert best is not None, (M, N, K)
    return best[1:]


def _mm(a, b, *, mode, out_dtype, name, deps=(), b_rows=(0, None), rotate=None, out_rows=(0, None), out_prev=None):
    b_first, b_count = b_rows
    if mode == "nn":
        (M, K), N = a.shape, b.shape[1]
    elif mode == "nt":
        (M, K), N = a.shape, (b.shape[0] if b_count is None else b_count)
    else:
        (K, M), N = a.shape, b.shape[1]
    shift, period = rotate or (0, 1)
    assert period == 1 or mode == "nt"
    out_first, out_total = out_rows[0], (M if out_rows[1] is None else out_rows[1])
    tm, tn, tk = _mm_tiles(M, N, K, jnp.dtype(out_dtype).itemsize, math.gcd(M, out_first),
                           math.gcd(N // period, b_first if mode == "nt" else 0))
    gm, gn, gk = M // tm, N // tn, K // tk

    def turned(j):
        per = N // period // tn
        return ((j // per + shift) % period) * per + j % per if period > 1 else j

    if mode == "nn":
        a_spec = pl.BlockSpec((tm, tk), lambda i, j, k: (i, k))
        b_spec = pl.BlockSpec((tk, tn), lambda i, j, k: (k + b_first // tk, j))
        contract = ((1,), (0,))
    elif mode == "nt":
        a_spec = pl.BlockSpec((tm, tk), lambda i, j, k: (i, k))
        b_spec = pl.BlockSpec((tn, tk), lambda i, j, k: (turned(j) + b_first // tn, k))
        contract = ((1,), (1,))
    else:
        a_spec = pl.BlockSpec((tk, tm), lambda i, j, k: (k, i))
        b_spec = pl.BlockSpec((tk, tn), lambda i, j, k: (k, j))
        contract = ((0,), (0,))
    o_spec = pl.BlockSpec((tm, tn), lambda i, j, k: (i + out_first // tm, j))
    assert b_first % (tk if mode == "nn" else tn) == 0 and out_first % tm == 0, (name, tm, tn, tk)
    n_prev = 0 if out_prev is None else 1

    def body(a_ref, b_ref, *rest):
        o_ref = rest[n_prev]
        part = _dot(a_ref[...].astype(BF16), b_ref[...].astype(BF16), contract)
        if gk == 1:
            o_ref[...] = part.astype(out_dtype)
            return
        acc_ref = rest[n_prev + 1]
        k = pl.program_id(2)

        @pl.when(k == 0)
        def _():
            acc_ref[...] = part

        @pl.when(k > 0)
        def _():
            acc_ref[...] += part

        @pl.when(k == gk - 1)
        def _():
            o_ref[...] = acc_ref[...].astype(out_dtype)

    body, dep_specs, dep_args = _with_deps(body, 2 + n_prev, deps)
    return pl.pallas_call(
        body,
        name=name,
        grid=(gm, gn, gk),
        in_specs=[a_spec, b_spec] + [ANY] * n_prev + dep_specs,
        out_specs=o_spec,
        out_shape=jax.ShapeDtypeStruct((out_total, N), out_dtype),
        input_output_aliases={2: 0} if n_prev else {},
        scratch_shapes=[] if gk == 1 else [pltpu.VMEM((tm, tn), F32)],
        compiler_params=_params(("parallel", "parallel", "arbitrary")),
    )(a, b, *([out_prev] if n_prev else []), *dep_args)


def _mm_tn_parts(parts, at, b, *, name):
    K, N = b.shape
    n = len(parts)
    tm = math.gcd(*[p.shape[1] for p in parts], *at)
    tiles = [p.shape[1] // tm for p in parts]
    first = [sum(tiles[:p]) for p in range(n)]

    def mine(i, p):
        return jnp.logical_and(i >= first[p], i < first[p] + tiles[p])

    def out_tile(i):
        t = 0
        for p in range(n):
            t = jnp.where(mine(i, p), at[p] // tm + i - first[p], t)
        return t

    def body(*refs):
        a_refs, b_ref, o_ref = refs[:n], refs[n], refs[n + 1]
        for p in range(n):
            @pl.when(mine(pl.program_id(0), p))
            def _(p=p):
                o_ref[...] = _dot_tn(a_refs[p][...], b_ref[...])

    return pl.pallas_call(
        body, name=name, grid=(sum(tiles),),
        in_specs=[pl.BlockSpec((K, tm), lambda i, p=p: (0, jnp.clip(i - first[p], 0, tiles[p] - 1))) for p in range(n)]
        + [pl.BlockSpec((K, N), lambda i: (0, 0))],
        out_specs=pl.BlockSpec((tm, N), lambda i: (out_tile(i), 0)),
        out_shape=jax.ShapeDtypeStruct((sum(p.shape[1] for p in parts), N), F32),
        compiler_params=_params(("arbitrary",)),
    )(*parts, b)


def _mm_rows(a, b, *, mode, fn, out_dtypes, rows=(), vecs=(), reduce=False, name, deps=(), b_rows=(0, None), a_at=None):
    parts = a if a_at is not None else (a,)
    starts = a_at if a_at is not None else (0,)
    n_parts = len(parts)
    M, K = parts[0].shape[0], sum(p.shape[1] for p in parts)
    b_first, b_count = b_rows[0], (b.shape[0] if b_rows[1] is None else b_rows[1])
    N = b.shape[1] if mode == "nn" else b_count
    contract = ((1,), (0,)) if mode == "nn" else ((1,), (1,))
    n_rows, n_vecs, n_out = len(rows), len(vecs), len(out_dtypes)
    out_bytes = sum(jnp.dtype(d).itemsize for d in out_dtypes)
    tm = max(t for t in _divisors(M, LANES, MM_MAX_TILE)
             if 4 * t * K + 4 * K * N + 2 * t * N * (4 * n_rows + out_bytes) <= MM_VMEM_BUDGET)
    assert b_first % b_count == 0 and (a_at is None or mode == "nn")

    def body(*refs):
        a_refs, b_ref, rest = refs[:n_parts], refs[n_parts], refs[n_parts + 1:]
        row_refs, vec_refs = rest[:n_rows], rest[n_rows:n_rows + n_vecs]
        out_refs = rest[n_rows + n_vecs:]
        if a_at is None:
            acc = _dot(a_refs[0][...], b_ref[...], contract)
        else:
            acc = sum(_dot(r[...], b_ref[at:at + r.shape[1], :], contract) for r, at in zip(a_refs, starts))
        res = fn(acc, *[r[...] for r in row_refs], *[v[...] for v in vec_refs])
        for o_ref, val in zip(out_refs[:n_out], res):
            o_ref[...] = val.astype(o_ref.dtype)
        if reduce:
            @pl.when(pl.program_id(0) == 0)
            def _():
                out_refs[n_out][...] = res[n_out]

            @pl.when(pl.program_id(0) > 0)
            def _():
                out_refs[n_out][...] += res[n_out]

    row = pl.BlockSpec((tm, N), lambda i: (i, 0))
    vec = pl.BlockSpec((1, N), lambda i: (0, 0))
    body, dep_specs, dep_args = _with_deps(body, n_parts + 1 + n_rows + n_vecs, deps)
    return pl.pallas_call(
        body, name=name, grid=(M // tm,),
        in_specs=[pl.BlockSpec((tm, p.shape[1]), lambda i: (i, 0)) for p in parts]
        + [pl.BlockSpec((b_count, b.shape[1]), lambda i: (b_first // b_count, 0))]
        + [row] * n_rows + [vec] * n_vecs + dep_specs,
        out_specs=[row] * n_out + [vec] * reduce,
        out_shape=[jax.ShapeDtypeStruct((M, N), d) for d in out_dtypes] + [jax.ShapeDtypeStruct((1, N), F32)] * reduce,
        compiler_params=_params(("arbitrary",)),
    )(*parts, b, *rows, *[v.reshape(1, N) for v in vecs], *dep_args)


def _rms(x, gain):
    return x * lax.rsqrt(jnp.mean(x * x, axis=-1, keepdims=True) + NORM_EPS) * gain


def _residual_then_norm(acc, x, gain):
    x_out = x + acc
    return x_out, _rms(x_out, gain)


def _residual_then_loss(acc, x, target):
    err = (x + acc) - target
    dy = err * (1.0 / D_MODEL)
    return dy, dy, jnp.sum(err * err, axis=0, keepdims=True) * (0.5 / D_MODEL)


def _rms_bwd_rows(dh, x, dres, gain):
    r = lax.rsqrt(jnp.mean(x * x, axis=-1, keepdims=True) + NORM_EPS)
    xh = x * r
    dxh = dh * gain
    dx = dres + r * (dxh - xh * jnp.mean(dxh * xh, axis=-1, keepdims=True))
    return dx, dx, jnp.sum(dh * xh, axis=0, keepdims=True)


def _rms_fwd(x, gain, *, name, tm=512, deps=()):
    T, D = x.shape

    def body(x_ref, g_ref, h_ref):
        xv = x_ref[...]
        r = lax.rsqrt(jnp.mean(xv * xv, axis=-1, keepdims=True) + NORM_EPS)
        h_ref[...] = (xv * r * g_ref[...]).astype(BF16)

    body, dep_specs, dep_args = _with_deps(body, 2, deps)
    return pl.pallas_call(
        body, name=name, grid=(T // tm,),
        in_specs=[pl.BlockSpec((tm, D), lambda i: (i, 0)), pl.BlockSpec((1, D), lambda i: (0, 0))] + dep_specs,
        out_specs=pl.BlockSpec((tm, D), lambda i: (i, 0)),
        out_shape=jax.ShapeDtypeStruct((T, D), BF16),
        compiler_params=_params(("parallel",)),
    )(x, gain.reshape(1, D), *dep_args)


def _head_norm(x, gain2, lo):
    ms = _half_sums(x * x, lo) * (1.0 / HEAD_DIM)
    r = lax.rsqrt(ms + NORM_EPS)
    xh = x * r
    return xh * gain2, xh, r


def _head_norm_bwd(xh, r, gain2, dy, lo):
    dxh = dy * gain2
    dx = r * (dxh - xh * (_half_sums(dxh * xh, lo) * (1.0 / HEAD_DIM)))
    return dx, dy * xh


Q_GROUP = N_Q_HEADS // 2
GROUP_ROWS = Q_GROUP * BLOCK
ATT_SCRATCH = (pltpu.VMEM((2, 2, GROUP_ROWS, BLOCK), F32), pltpu.VMEM((2, GROUP_ROWS, 1), F32))


def _att_consts(sink_ref, bias_ref, sinkcol_ref):
    row = lax.broadcasted_iota(jnp.int32, (GROUP_ROWS, BLOCK), 0)
    kj = lax.broadcasted_iota(jnp.int32, (GROUP_ROWS, BLOCK), 1)
    head = row // BLOCK
    head_col = lax.broadcasted_iota(jnp.int32, (GROUP_ROWS, 1), 0) // BLOCK
    d_cur = (row % BLOCK) - kj
    d_prev = d_cur + BLOCK
    for kv in range(2):
        slope = jnp.zeros((GROUP_ROWS, BLOCK), F32)
        sink = jnp.zeros((GROUP_ROWS, 1), F32)
        for r in range(Q_GROUP):
            slope = jnp.where(head == r, ALIBI_SLOPES[Q_GROUP * kv + r], slope)
            sink = jnp.where(head_col == r, sink_ref[Q_GROUP * kv + r], sink)
        bias_ref[kv, 0] = jnp.where(d_cur >= 0, -slope * d_cur.astype(F32), NEG_INF)
        bias_ref[kv, 1] = jnp.where(d_prev < BLOCK, -slope * d_prev.astype(F32), NEG_INF)
        sinkcol_ref[kv] = sink


def _stack_heads(t0, t1, lo):
    z = jnp.zeros_like(t0)
    return jnp.concatenate([jnp.where(lo, t0, z), jnp.where(lo, z, t0), jnp.where(lo, t1, z), jnp.where(lo, z, t1)], axis=0)


def _unstack_heads(x4, lo):
    return (jnp.where(lo, x4[0:BLOCK], x4[BLOCK:2 * BLOCK]), jnp.where(lo, x4[2 * BLOCK:3 * BLOCK], x4[3 * BLOCK:]))


def _att_probs(q4, k2c, k2p, bias_c, bias_p, sink, has_prev):
    s_c = _dot_nt(q4, k2c) * ATT_SCALE + bias_c
    s_p = jnp.where(has_prev, _dot_nt(q4, k2p) * ATT_SCALE + bias_p, NEG_INF)
    m = jnp.maximum(jnp.max(jnp.maximum(s_c, s_p), axis=-1, keepdims=True), sink)
    e_c = jnp.exp(s_c - m)
    e_p = jnp.exp(s_p - m)
    e_s = jnp.exp(sink - m)
    inv = 1.0 / (jnp.sum(e_c + e_p, axis=-1, keepdims=True) + e_s)
    return e_c * inv, e_p * inv, e_s * inv


def _attention_fwd(proj, q_gain, k_gain, sinks, *, n_seq, seq, name):
    T = n_seq * seq
    nb = seq // BLOCK
    qcol, kvcol = COL_QKV // ATT_WIDTH, (COL_QKV + ATT_WIDTH) // (2 * KV_WIDTH)

    def body(q_ref, kv_ref, qg_ref, kg_ref, sink_ref, y_ref, bias_ref, sinkcol_ref):
        lo = _lo_mask((BLOCK, LANES))
        qg, kg = qg_ref[...], kg_ref[...]
        _att_consts(sink_ref, bias_ref, sinkcol_ref)

        def block(i, carry):
            r0 = pl.multiple_of(i * BLOCK, BLOCK)
            rp = pl.multiple_of(jnp.maximum(i - 1, 0) * BLOCK, BLOCK)
            has_prev = i > 0
            kn_c = _head_norm(kv_ref[pl.ds(r0, BLOCK), 0:KV_WIDTH].astype(F32), kg, lo)[0].astype(BF16)
            kn_p = _head_norm(kv_ref[pl.ds(rp, BLOCK), 0:KV_WIDTH].astype(F32), kg, lo)[0].astype(BF16)
            v_c = kv_ref[pl.ds(r0, BLOCK), KV_WIDTH:2 * KV_WIDTH].astype(BF16)
            v_p = kv_ref[pl.ds(rp, BLOCK), KV_WIDTH:2 * KV_WIDTH].astype(BF16)
            for kv in range(2):
                k2c, k2p = _dup_half(kn_c, kv, lo), _dup_half(kn_p, kv, lo)
                v2c, v2p = _dup_half(v_c, kv, lo), _dup_half(v_p, kv, lo)
                cols = [slice((2 * kv + t) * LANES, (2 * kv + t + 1) * LANES) for t in range(2)]
                qn = [_head_norm(q_ref[pl.ds(r0, BLOCK), c].astype(F32), qg, lo)[0] for c in cols]
                q4 = _stack_heads(qn[0], qn[1], lo).astype(BF16)
                p_c, p_p, _ = _att_probs(q4, k2c, k2p, bias_ref[kv, 0], bias_ref[kv, 1], sinkcol_ref[kv], has_prev)
                o4 = _dot_nn(p_c.astype(BF16), v2c) + _dot_nn(p_p.astype(BF16), v2p)
                for c, out in zip(cols, _unstack_heads(o4, lo)):
                    y_ref[pl.ds(r0, BLOCK), c] = out.astype(BF16)
            return carry

        lax.fori_loop(0, nb, block, 0)

    vec = pl.BlockSpec((1, LANES), lambda b: (0, 0))
    return pl.pallas_call(
        body, name=name, grid=(n_seq,),
        in_specs=[pl.BlockSpec((seq, ATT_WIDTH), lambda b: (b, qcol)),
                  pl.BlockSpec((seq, 2 * KV_WIDTH), lambda b: (b, kvcol)),
                  vec, vec, pl.BlockSpec(memory_space=pltpu.SMEM)],
        out_specs=pl.BlockSpec((seq, ATT_WIDTH), lambda b: (b, 0)),
        out_shape=jax.ShapeDtypeStruct((T, ATT_WIDTH), BF16),
        scratch_shapes=list(ATT_SCRATCH),
        compiler_params=_params(("parallel",)),
    )(proj, proj, jnp.tile(q_gain, 2).reshape(1, LANES), jnp.tile(k_gain, 2).reshape(1, LANES), sinks)


def _attention_bwd(proj, dy, q_gain, k_gain, sinks, *, n_seq, seq, name, deps=()):
    T = n_seq * seq
    nb = seq // BLOCK
    qcol, kvcol = COL_QKV // ATT_WIDTH, (COL_QKV + ATT_WIDTH) // (2 * KV_WIDTH)

    def body(q_ref, kv_ref, dy_ref, qg_ref, kg_ref, sink_ref, dqkv_ref, dqg_ref, dkg_ref, dsink_ref,
             dkn_acc, dv_acc, qg_acc, kg_acc, sink_acc, bias_ref, sinkcol_ref):
        lo = _lo_mask((BLOCK, LANES))
        qg, kg = qg_ref[...], kg_ref[...]
        _att_consts(sink_ref, bias_ref, sinkcol_ref)
        first = pl.program_id(0) == 0

        @pl.when(first)
        def _():
            qg_acc[...] = jnp.zeros_like(qg_acc)
            kg_acc[...] = jnp.zeros_like(kg_acc)
            sink_acc[...] = jnp.zeros_like(sink_acc)

        dkn_acc[...] = jnp.zeros_like(dkn_acc)
        dv_acc[...] = jnp.zeros_like(dv_acc)

        def block(i, carry):
            r0 = pl.multiple_of(i * BLOCK, BLOCK)
            rp = pl.multiple_of(jnp.maximum(i - 1, 0) * BLOCK, BLOCK)
            has_prev = i > 0
            kn_c = _head_norm(kv_ref[pl.ds(r0, BLOCK), 0:KV_WIDTH].astype(F32), kg, lo)[0].astype(BF16)
            kn_p = _head_norm(kv_ref[pl.ds(rp, BLOCK), 0:KV_WIDTH].astype(F32), kg, lo)[0].astype(BF16)
            v_c = kv_ref[pl.ds(r0, BLOCK), KV_WIDTH:2 * KV_WIDTH].astype(BF16)
            v_p = kv_ref[pl.ds(rp, BLOCK), KV_WIDTH:2 * KV_WIDTH].astype(BF16)
            dk_c, dk_p, dv_c, dv_p = [], [], [], []
            for kv in range(2):
                k2c, k2p = _dup_half(kn_c, kv, lo), _dup_half(kn_p, kv, lo)
                v2c, v2p = _dup_half(v_c, kv, lo), _dup_half(v_p, kv, lo)
                cols = [slice((2 * kv + t) * LANES, (2 * kv + t + 1) * LANES) for t in range(2)]
                normed = [_head_norm(q_ref[pl.ds(r0, BLOCK), c].astype(F32), qg, lo) for c in cols]
                q4 = _stack_heads(normed[0][0], normed[1][0], lo).astype(BF16)
                do4 = _stack_heads(dy_ref[pl.ds(r0, BLOCK), cols[0]], dy_ref[pl.ds(r0, BLOCK), cols[1]], lo)
                p_c, p_p, p_s = _att_probs(q4, k2c, k2p, bias_ref[kv, 0], bias_ref[kv, 1], sinkcol_ref[kv], has_prev)
                dp_c = _dot_nt(do4, v2c)
                dp_p = _dot_nt(do4, v2p)
                delta = jnp.sum(p_c * dp_c + p_p * dp_p, axis=-1, keepdims=True)
                ds_c = (p_c * (dp_c - delta)).astype(BF16)
                ds_p = (p_p * (dp_p - delta)).astype(BF16)
                sink_acc[kv] += -(p_s * delta)
                dq4 = (_dot_nn(ds_c, k2c) + _dot_nn(ds_p, k2p)) * ATT_SCALE
                for c, (_, qh, qr), dqn in zip(cols, normed, _unstack_heads(dq4, lo)):
                    dq, dg = _head_norm_bwd(qh, qr, qg, dqn, lo)
                    dqkv_ref[pl.ds(r0, BLOCK), c] = dq.astype(BF16)
                    qg_acc[...] += dg
                dk_c.append(_dot_tn(ds_c, q4))
                dk_p.append(_dot_tn(ds_p, q4))
                dv_c.append(_dot_tn(p_c.astype(BF16), do4))
                dv_p.append(_dot_tn(p_p.astype(BF16), do4))

            def fold(parts):
                a = parts[0] + pltpu.roll(parts[0], LANES // 2, axis=1)
                b = parts[1] + pltpu.roll(parts[1], LANES // 2, axis=1)
                return jnp.where(lo, a, b)

            dkn_acc[pl.ds(r0, BLOCK), :] += fold(dk_c) * ATT_SCALE
            dkn_acc[pl.ds(rp, BLOCK), :] += fold(dk_p) * ATT_SCALE
            dv_acc[pl.ds(r0, BLOCK), :] += fold(dv_c)
            dv_acc[pl.ds(rp, BLOCK), :] += fold(dv_p)
            return carry

        lax.fori_loop(0, nb, block, 0)

        def finish(i, carry):
            r0 = pl.multiple_of(i * BLOCK, BLOCK)
            _, kh, kr = _head_norm(kv_ref[pl.ds(r0, BLOCK), 0:KV_WIDTH].astype(F32), kg, lo)
            dk, dg = _head_norm_bwd(kh, kr, kg, dkn_acc[pl.ds(r0, BLOCK), :], lo)
            dqkv_ref[pl.ds(r0, BLOCK), ATT_WIDTH:ATT_WIDTH + KV_WIDTH] = dk.astype(BF16)
            dqkv_ref[pl.ds(r0, BLOCK), ATT_WIDTH + KV_WIDTH:QKV_WIDTH] = dv_acc[pl.ds(r0, BLOCK), :].astype(BF16)
            kg_acc[...] += dg
            return carry

        lax.fori_loop(0, nb, finish, 0)

        @pl.when(pl.program_id(0) == n_seq - 1)
        def _():
            dqg_ref[...] = jnp.sum(qg_acc[...], axis=0, keepdims=True)
            dkg_ref[...] = jnp.sum(kg_acc[...], axis=0, keepdims=True)
            lane = lax.broadcasted_iota(jnp.int32, (1, LANES), 1)
            dsink = jnp.zeros((1, LANES), F32)
            for kv in range(2):
                for r in range(Q_GROUP):
                    total = jnp.sum(sink_acc[kv, r * BLOCK:(r + 1) * BLOCK, :], axis=0, keepdims=True)
                    dsink = jnp.where(lane == Q_GROUP * kv + r, total, dsink)
            dsink_ref[...] = dsink

    vec = pl.BlockSpec((1, LANES), lambda b: (0, 0))
    acc = pltpu.VMEM((BLOCK, LANES), F32)
    body, dep_specs, dep_args = _with_deps(body, 6, deps)
    dqkv, dqg, dkg, dsink = pl.pallas_call(
        body, name=name, grid=(n_seq,),
        in_specs=[pl.BlockSpec((seq, ATT_WIDTH), lambda b: (b, qcol)),
                  pl.BlockSpec((seq, 2 * KV_WIDTH), lambda b: (b, kvcol)),
                  pl.BlockSpec((seq, ATT_WIDTH), lambda b: (b, 0)),
                  vec, vec, pl.BlockSpec(memory_space=pltpu.SMEM)] + dep_specs,
        out_specs=[pl.BlockSpec((seq, QKV_WIDTH), lambda b: (b, 0)), vec, vec, vec],
        out_shape=[jax.ShapeDtypeStruct((T, QKV_WIDTH), BF16)] + [jax.ShapeDtypeStruct((1, LANES), F32)] * 3,
        scratch_shapes=[pltpu.VMEM((seq, KV_WIDTH), F32), pltpu.VMEM((seq, KV_WIDTH), F32), acc, acc,
                        pltpu.VMEM((2, GROUP_ROWS, 1), F32), *ATT_SCRATCH],
        compiler_params=_params(("arbitrary",)),
    )(proj, proj, dy, jnp.tile(q_gain, 2).reshape(1, LANES), jnp.tile(k_gain, 2).reshape(1, LANES), sinks, *dep_args)
    half = LANES // 2
    return dqkv, dqg[0, :half] + dqg[0, half:], dkg[0, :half] + dkg[0, half:], dsink[0, :N_Q_HEADS]


def _sgu_weights(w_ref):
    r = lax.broadcasted_iota(jnp.int32, (BLOCK, BLOCK), 0)
    c = lax.broadcasted_iota(jnp.int32, (BLOCK, BLOCK), 1)
    return [jnp.where(r >= c, w_ref[g], 0.0).astype(BF16) for g in range(SGU_GROUPS)]


def _sgu_fwd(proj, gain, w_s, bias_full, *, n_seq, seq, name):
    T = n_seq * seq
    nc = seq // BLOCK

    def body(suv_ref, g_ref, w_ref, b_ref, y_ref):
        lo = _lo_mask((BLOCK, LANES))
        wm = _sgu_weights(w_ref)
        gain_v = g_ref[...]

        def chunk(c, carry):
            r0 = pl.multiple_of(c * BLOCK, BLOCK)
            gv = _gelu(suv_ref[pl.ds(r0, BLOCK), SGU_WIDTH:2 * SGU_WIDTH].astype(F32))
            r = lax.rsqrt(jnp.mean(gv * gv, axis=-1, keepdims=True) + NORM_EPS)
            vn = (gv * r * gain_v).astype(BF16)
            for p in range(SGU_WIDTH // LANES):
                cols = slice(p * LANES, (p + 1) * LANES)
                vp = vn[:, cols]
                mixed = jnp.where(lo, _dot_nn(wm[2 * p], vp), _dot_nn(wm[2 * p + 1], vp)) + b_ref[:, cols]
                u = _gelu(suv_ref[pl.ds(r0, BLOCK), cols].astype(F32))
                y_ref[pl.ds(r0, BLOCK), cols] = (u * mixed).astype(BF16)
            return carry

        lax.fori_loop(0, nc, chunk, 0)

    return pl.pallas_call(
        body, name=name, grid=(n_seq,),
        in_specs=[pl.BlockSpec((seq, 2 * SGU_WIDTH), lambda b: (b, COL_SUV // (2 * SGU_WIDTH))),
                  pl.BlockSpec((1, SGU_WIDTH), lambda b: (0, 0)),
                  pl.BlockSpec((SGU_GROUPS, BLOCK, BLOCK), lambda b: (0, 0, 0)),
                  pl.BlockSpec((BLOCK, SGU_WIDTH), lambda b: (0, 0))],
        out_specs=pl.BlockSpec((seq, SGU_WIDTH), lambda b: (b, 0)),
        out_shape=jax.ShapeDtypeStruct((T, SGU_WIDTH), BF16),
        compiler_params=_params(("parallel",)),
    )(proj, gain.reshape(1, SGU_WIDTH), w_s, bias_full)


def _sgu_bwd(proj, dy, gain, w_s, bias_full, *, n_seq, seq, name, deps=()):
    T = n_seq * seq
    nc = seq // BLOCK
    n_tiles = SGU_WIDTH // LANES

    def body(suv_ref, dy_ref, g_ref, w_ref, b_ref, dsuv_ref, dg_ref, dw_ref, db_ref, dg_acc, dw_acc, db_acc):
        lo = _lo_mask((BLOCK, LANES))
        hi = jnp.logical_not(lo)
        wm = _sgu_weights(w_ref)
        wmt = [jnp.where(lax.broadcasted_iota(jnp.int32, (BLOCK, BLOCK), 1) >= lax.broadcasted_iota(jnp.int32, (BLOCK, BLOCK), 0),
                         w_ref[g].T, 0.0).astype(BF16) for g in range(SGU_GROUPS)]
        gain_v = g_ref[...]

        @pl.when(pl.program_id(0) == 0)
        def _():
            dg_acc[...] = jnp.zeros_like(dg_acc)
            dw_acc[...] = jnp.zeros_like(dw_acc)
            db_acc[...] = jnp.zeros_like(db_acc)

        def chunk(c, carry):
            r0 = pl.multiple_of(c * BLOCK, BLOCK)
            gv, dgelu_v = _gelu_and_grad(suv_ref[pl.ds(r0, BLOCK), SGU_WIDTH:2 * SGU_WIDTH].astype(F32))
            r = lax.rsqrt(jnp.mean(gv * gv, axis=-1, keepdims=True) + NORM_EPS)
            vh = gv * r
            vn = (vh * gain_v).astype(BF16)
            dvn_tiles = []
            for p in range(n_tiles):
                cols = slice(p * LANES, (p + 1) * LANES)
                vp = vn[:, cols]
                mixed = jnp.where(lo, _dot_nn(wm[2 * p], vp), _dot_nn(wm[2 * p + 1], vp)) + b_ref[:, cols]
                u, dgelu_u = _gelu_and_grad(suv_ref[pl.ds(r0, BLOCK), cols].astype(F32))
                dyv = dy_ref[pl.ds(r0, BLOCK), cols]
                dsuv_ref[pl.ds(r0, BLOCK), cols] = (dyv * mixed * dgelu_u).astype(BF16)
                dm = dyv * u
                db_acc[:, cols] += dm
                dm_bf = dm.astype(BF16)
                dvn_tiles.append(jnp.where(lo, _dot_nn(wmt[2 * p], dm_bf), _dot_nn(wmt[2 * p + 1], dm_bf)))
                dw_acc[2 * p] += _dot_nt(jnp.where(lo, dm, 0.0).astype(BF16), vp)
                dw_acc[2 * p + 1] += _dot_nt(jnp.where(hi, dm, 0.0).astype(BF16), vp)
            dvn = jnp.concatenate(dvn_tiles, axis=1)
            dg_acc[...] += dvn * vh
            dvh = dvn * gain_v
            dgv = r * (dvh - vh * jnp.mean(dvh * vh, axis=-1, keepdims=True))
            dsuv_ref[pl.ds(r0, BLOCK), SGU_WIDTH:2 * SGU_WIDTH] = (dgv * dgelu_v).astype(BF16)
            return carry

        lax.fori_loop(0, nc, chunk, 0)

        @pl.when(pl.program_id(0) == n_seq - 1)
        def _():
            dg_ref[...] = jnp.sum(dg_acc[...], axis=0, keepdims=True)
            r = lax.broadcasted_iota(jnp.int32, (BLOCK, BLOCK), 0)
            c = lax.broadcasted_iota(jnp.int32, (BLOCK, BLOCK), 1)
            for g in range(SGU_GROUPS):
                dw_ref[g] = jnp.where(r >= c, dw_acc[g], 0.0)
            lane = lax.broadcasted_iota(jnp.int32, (BLOCK, LANES), 1)
            out = jnp.zeros((BLOCK, LANES), F32)
            for p in range(n_tiles):
                tile = db_acc[:, p * LANES:(p + 1) * LANES]
                s_lo = jnp.sum(jnp.where(lo, tile, 0.0), axis=-1, keepdims=True)
                s_hi = jnp.sum(jnp.where(hi, tile, 0.0), axis=-1, keepdims=True)
                out = jnp.where(lane == 2 * p, s_lo, out)
                out = jnp.where(lane == 2 * p + 1, s_hi, out)
            db_ref[...] = out

    body, dep_specs, dep_args = _with_deps(body, 5, deps)
    dsuv, dg, dw, db = pl.pallas_call(
        body, name=name, grid=(n_seq,),
        in_specs=[pl.BlockSpec((seq, 2 * SGU_WIDTH), lambda b: (b, COL_SUV // (2 * SGU_WIDTH))),
                  pl.BlockSpec((seq, SGU_WIDTH), lambda b: (b, 0)),
                  pl.BlockSpec((1, SGU_WIDTH), lambda b: (0, 0)),
                  pl.BlockSpec((SGU_GROUPS, BLOCK, BLOCK), lambda b: (0, 0, 0)),
                  pl.BlockSpec((BLOCK, SGU_WIDTH), lambda b: (0, 0))] + dep_specs,
        out_specs=[pl.BlockSpec((seq, 2 * SGU_WIDTH), lambda b: (b, 0)),
                   pl.BlockSpec((1, SGU_WIDTH), lambda b: (0, 0)),
                   pl.BlockSpec((SGU_GROUPS, BLOCK, BLOCK), lambda b: (0, 0, 0)),
                   pl.BlockSpec((BLOCK, LANES), lambda b: (0, 0))],
        out_shape=[jax.ShapeDtypeStruct((T, 2 * SGU_WIDTH), BF16), jax.ShapeDtypeStruct((1, SGU_WIDTH), F32),
                   jax.ShapeDtypeStruct((SGU_GROUPS, BLOCK, BLOCK), F32), jax.ShapeDtypeStruct((BLOCK, LANES), F32)],
        scratch_shapes=[pltpu.VMEM((BLOCK, SGU_WIDTH), F32), pltpu.VMEM((SGU_GROUPS, BLOCK, BLOCK), F32),
                        pltpu.VMEM((BLOCK, SGU_WIDTH), F32)],
        compiler_params=_params(("arbitrary",)),
    )(proj, dy, gain.reshape(1, SGU_WIDTH), w_s, bias_full, *dep_args)
    return dsuv, dg.reshape(SGU_WIDTH), dw, db[:, :SGU_GROUPS].T


def _merge_fwd(y_att, y_sgu, w_oa, w_ob, proj, *, name, tm=1024, tn=512, deps=()):
    T = y_att.shape[0]

    def body(ya_ref, ys_ref, wa_ref, wb_ref, ga_ref, gb_ref, o_ref):
        pa = _dot_nn(ya_ref[...], wa_ref[...])
        pb = _dot_nn(ys_ref[...], wb_ref[...])
        o_ref[...] = (_sigmoid(ga_ref[...].astype(F32)) * pa + _sigmoid(gb_ref[...].astype(F32)) * pb).astype(BF16)

    act = pl.BlockSpec((tm, ATT_WIDTH), lambda i, j: (i, 0))
    wgt = pl.BlockSpec((ATT_WIDTH, tn), lambda i, j: (0, j))
    body, dep_specs, dep_args = _with_deps(body, 6, deps)
    return pl.pallas_call(
        body, name=name, grid=(T // tm, D_MODEL // tn),
        in_specs=[act, act, wgt, wgt,
                  pl.BlockSpec((tm, tn), lambda i, j: (i, j + COL_GA // tn)),
                  pl.BlockSpec((tm, tn), lambda i, j: (i, j + COL_GB // tn))] + dep_specs,
        out_specs=pl.BlockSpec((tm, tn), lambda i, j: (i, j)),
        out_shape=jax.ShapeDtypeStruct((T, D_MODEL), BF16),
        compiler_params=_params(("parallel", "parallel")),
    )(y_att, y_sgu, w_oa, w_ob, proj, proj, *dep_args)


def _merge_bwd(dx1_bf, w_out, y_att, y_sgu, w_oa, w_ob, proj, *, name, tm=1024, tn=512):
    T = y_att.shape[0]

    def body(dx_ref, wo_ref, ya_ref, ys_ref, wa_ref, wb_ref, ga_ref, gb_ref, dpa_ref, dpb_ref, dga_ref, dgb_ref):
        dm = _dot_nt(dx_ref[...], wo_ref[...])
        pa = _dot_nn(ya_ref[...], wa_ref[...])
        pb = _dot_nn(ys_ref[...], wb_ref[...])
        sa = _sigmoid(ga_ref[...].astype(F32))
        sb = _sigmoid(gb_ref[...].astype(F32))
        dpa_ref[...] = (dm * sa).astype(BF16)
        dpb_ref[...] = (dm * sb).astype(BF16)
        dga_ref[...] = (dm * pa * sa * (1.0 - sa)).astype(BF16)
        dgb_ref[...] = (dm * pb * sb * (1.0 - sb)).astype(BF16)

    act = pl.BlockSpec((tm, ATT_WIDTH), lambda i, j: (i, 0))
    wgt = pl.BlockSpec((ATT_WIDTH, tn), lambda i, j: (0, j))
    out = pl.BlockSpec((tm, tn), lambda i, j: (i, j))
    return pl.pallas_call(
        body, name=name, grid=(T // tm, D_MODEL // tn),
        in_specs=[pl.BlockSpec((tm, D_MODEL), lambda i, j: (i, 0)),
                  pl.BlockSpec((tn, D_MODEL), lambda i, j: (j, 0)),
                  act, act, wgt, wgt,
                  pl.BlockSpec((tm, tn), lambda i, j: (i, j + COL_GA // tn)),
                  pl.BlockSpec((tm, tn), lambda i, j: (i, j + COL_GB // tn))],
        out_specs=[out] * 4,
        out_shape=[jax.ShapeDtypeStruct((T, D_MODEL), BF16)] * 4,
        compiler_params=_params(("parallel", "parallel")),
    )(dx1_bf, w_out, y_att, y_sgu, w_oa, w_ob, proj, proj)


CONV_ROWS = 256
CONV_TN = 256


def _shift_rows(cur, prev8, k):
    rolled = pltpu.roll(cur, k, axis=0)
    head = jnp.where(lax.broadcasted_iota(jnp.int32, prev8.shape, 0) < k, pltpu.roll(prev8, k, axis=0), rolled[:SUBLANES])
    return jnp.concatenate([head, rolled[SUBLANES:]], axis=0)


def _shift_rows_up(cur, next8, k):
    n = cur.shape[0]
    rolled = pltpu.roll(cur, n - k, axis=0)
    tail = jnp.where(lax.broadcasted_iota(jnp.int32, next8.shape, 0) >= SUBLANES - k,
                     pltpu.roll(next8, SUBLANES - k, axis=0), rolled[n - SUBLANES:])
    return jnp.concatenate([rolled[:n - SUBLANES], tail], axis=0)


HALO_ROWS = 16


def _rows_before(z_ref, r0, first):
    rp = pl.multiple_of(jnp.maximum(r0 - HALO_ROWS, 0), HALO_ROWS)
    halo = z_ref[pl.ds(rp, HALO_ROWS), :].astype(F32)
    return jnp.where(first, 0.0, halo[HALO_ROWS - SUBLANES:])


def _conv_rows(z_ref, r0, first, w_ref, b_ref, rows):
    cur = z_ref[pl.ds(r0, rows), :].astype(F32)
    prev8 = _rows_before(z_ref, r0, first)
    z1 = _shift_rows(cur, prev8, 1)
    z2 = _shift_rows(cur, prev8, 2)
    return b_ref[...] + w_ref[0:1, :] * z2 + w_ref[1:2, :] * z1 + w_ref[2:3, :] * cur


def _conv_fwd(z_g, z_v, cw_g, cw_v, cb_g, cb_v, *, n_seq, seq, name):
    T = n_seq * seq
    tn, rows = CONV_TN, CONV_ROWS

    def body(zg_ref, zv_ref, wg_ref, wv_ref, bg_ref, bv_ref, a_ref, cg_ref, cv_ref):
        def step(s, carry):
            r0 = pl.multiple_of(s * rows, rows)
            first = s == 0
            g = _conv_rows(zg_ref, r0, first, wg_ref, bg_ref, rows)
            v = _conv_rows(zv_ref, r0, first, wv_ref, bv_ref, rows)
            a_ref[pl.ds(r0, rows), :] = (g * _sigmoid(g) * v).astype(BF16)
            cg_ref[pl.ds(r0, rows), :] = g.astype(ACT_DTYPE)
            cv_ref[pl.ds(r0, rows), :] = v.astype(ACT_DTYPE)
            return carry

        lax.fori_loop(0, seq // rows, step, 0)

    zs = pl.BlockSpec((seq, tn), lambda b, j: (b, j))
    ws = pl.BlockSpec((3, tn), lambda b, j: (0, j))
    bs = pl.BlockSpec((1, tn), lambda b, j: (0, j))
    return pl.pallas_call(
        body, name=name, grid=(n_seq, D_FF // tn),
        in_specs=[zs, zs, ws, ws, bs, bs], out_specs=[zs] * 3,
        out_shape=[jax.ShapeDtypeStruct((T, D_FF), BF16)] + [jax.ShapeDtypeStruct((T, D_FF), ACT_DTYPE)] * 2,
        compiler_params=_params(("parallel", "parallel")),
    )(z_g, z_v, cw_g, cw_v, cb_g.reshape(1, D_FF), cb_v.reshape(1, D_FF))


def _conv_bwd(z_g, z_v, c_g, c_v, da, cw_g, cw_v, *, n_seq, seq, name):
    T = n_seq * seq
    tn, rows = CONV_TN, CONV_ROWS
    n_steps = seq // rows

    def body(zg_ref, zv_ref, cg_ref, cv_ref, da_ref, wg_ref, wv_ref,
             dzg_ref, dzv_ref, dwg_ref, dwv_ref, dbg_ref, dbv_ref, dcg_ref, dcv_ref):
        def colsum(x):
            return jnp.sum(x, axis=0, keepdims=True)

        def grads(s, accs):
            r0 = pl.multiple_of(s * rows, rows)
            g = cg_ref[pl.ds(r0, rows), :].astype(F32)
            v = cv_ref[pl.ds(r0, rows), :].astype(F32)
            sg = _sigmoid(g)
            dav = da_ref[pl.ds(r0, rows), :].astype(F32)
            dcg = dav * v * (sg * (1.0 + g * (1.0 - sg)))
            dcv = dav * (g * sg)
            dcg_ref[pl.ds(r0, rows), :] = dcg
            dcv_ref[pl.ds(r0, rows), :] = dcv
            return accs[0] + colsum(dcg), accs[1] + colsum(dcv)

        zero = jnp.zeros((1, tn), F32)
        db = lax.fori_loop(0, n_steps, grads, (zero, zero))

        def back(s, accs):
            r0 = pl.multiple_of(s * rows, rows)
            last = s == n_steps - 1
            rn = pl.multiple_of(jnp.minimum(r0 + rows, seq - SUBLANES), SUBLANES)
            new = []
            for half, (dc_ref, w_ref, dz_ref, z_ref) in enumerate(((dcg_ref, wg_ref, dzg_ref, zg_ref),
                                                                   (dcv_ref, wv_ref, dzv_ref, zv_ref))):
                cur = dc_ref[pl.ds(r0, rows), :]
                nxt = jnp.where(last, 0.0, dc_ref[pl.ds(rn, SUBLANES), :])
                u1, u2 = _shift_rows_up(cur, nxt, 1), _shift_rows_up(cur, nxt, 2)
                dz_ref[pl.ds(r0, rows), :] = (w_ref[2:3, :] * cur + w_ref[1:2, :] * u1 + w_ref[0:1, :] * u2).astype(BF16)
                z = z_ref[pl.ds(r0, rows), :].astype(F32)
                new += [accs[3 * half] + colsum(u2 * z), accs[3 * half + 1] + colsum(u1 * z),
                        accs[3 * half + 2] + colsum(cur * z)]
            return tuple(new)

        dw = lax.fori_loop(0, n_steps, back, (zero,) * 6)
        first_seq = pl.program_id(1) == 0

        @pl.when(first_seq)
        def _():
            dwg_ref[...] = jnp.concatenate(dw[0:3], axis=0)
            dwv_ref[...] = jnp.concatenate(dw[3:6], axis=0)
            dbg_ref[...], dbv_ref[...] = db

        @pl.when(jnp.logical_not(first_seq))
        def _():
            dwg_ref[...] += jnp.concatenate(dw[0:3], axis=0)
            dwv_ref[...] += jnp.concatenate(dw[3:6], axis=0)
            dbg_ref[...] += db[0]
            dbv_ref[...] += db[1]

    zs = pl.BlockSpec((seq, tn), lambda j, b: (b, j))
    ws = pl.BlockSpec((3, tn), lambda j, b: (0, j))
    bs = pl.BlockSpec((1, tn), lambda j, b: (0, j))
    outs = pl.pallas_call(
        body, name=name, grid=(D_FF // tn, n_seq),
        in_specs=[zs] * 5 + [ws, ws],
        out_specs=[zs, zs, ws, ws, bs, bs],
        out_shape=[jax.ShapeDtypeStruct((T, D_FF), BF16)] * 2 + [jax.ShapeDtypeStruct((3, D_FF), F32)] * 2
        + [jax.ShapeDtypeStruct((1, D_FF), F32)] * 2,
        scratch_shapes=[pltpu.VMEM((seq, tn), F32), pltpu.VMEM((seq, tn), F32)],
        compiler_params=_params(("parallel", "arbitrary")),
    )(z_g, z_v, c_g, c_v, da, cw_g, cw_v)
    dz_g, dz_v, dw_g, dw_v, db_g, db_v = outs
    return dz_g, dz_v, dw_g, dw_v, db_g.reshape(D_FF), db_v.reshape(D_FF)


def _layer_fwd(x, h, w, sched, tail, *, n_seq, seq, l):
    tag = f"l{l}"
    deps = sched("fwd_start", l, h)
    proj = _mm(h, w["w_in_t"], mode="nt", out_dtype=ACT_DTYPE, rotate=W_IN_ROTATE, name=f"{tag}_proj", deps=deps)
    y_att = _attention_fwd(proj, w["q_norm"], w["k_norm"], w["sinks"], n_seq=n_seq, seq=seq, name=f"{tag}_att")
    deps = sched("fwd_att", l, y_att)
    y_sgu = _sgu_fwd(proj, w["sgu_norm"], w["w_s"], w["bias_full"], n_seq=n_seq, seq=seq, name=f"{tag}_sgu")
    merged = _merge_fwd(y_att, y_sgu, w["w_oa"], w["w_ob"], proj, name=f"{tag}_merge", deps=deps)
    x1, h2 = _mm_rows(merged, w["w_out"], mode="nn", fn=_residual_then_norm, out_dtypes=(F32, BF16), rows=(x,),
                      vecs=(w["ffn_norm"],), name=f"{tag}_out")
    deps = sched("fwd_mixer_done", l, x1)
    z_g = _mm(h2, w["w_up_t"], mode="nt", out_dtype=ACT_DTYPE, b_rows=(0, D_FF), name=f"{tag}_up_g", deps=deps)
    z_v = _mm(h2, w["w_up_t"], mode="nt", out_dtype=ACT_DTYPE, b_rows=(D_FF, D_FF), name=f"{tag}_up_v")
    a, c_g, c_v = _conv_fwd(z_g, z_v, w["cw_g"], w["cw_v"], w["cb_g"], w["cb_v"], n_seq=n_seq, seq=seq,
                            name=f"{tag}_conv")
    deps = sched("fwd_conv", l, a)
    if tail[0] == "norm":
        out = _mm_rows(a, w["w_down"], mode="nn", fn=_residual_then_norm, out_dtypes=(F32, BF16), rows=(x1,),
                       vecs=(tail[1],), name=f"{tag}_down", deps=deps)
    else:
        out = _mm_rows(a, w["w_down"], mode="nn", fn=_residual_then_loss, out_dtypes=(F32, BF16), rows=(x1, tail[1]),
                       reduce=True, name=f"{tag}_down", deps=deps)
    saved = dict(x=x, h=h, proj=proj, y_att=y_att, y_sgu=y_sgu, merged=merged, x1=x1, h2=h2, z_g=z_g, z_v=z_v,
                 c_g=c_g, c_v=c_v, a=a)
    return out, saved


def _layer_bwd(dx2, dx2_bf, w, s, sched, deps, *, n_seq, seq, l):
    tag = f"l{l}b"
    g = {}
    da = _mm(dx2_bf, w["w_down"], mode="nt", out_dtype=ACT_DTYPE, name=f"{tag}_da", deps=deps)
    g["w_down"] = _mm(s["a"], dx2_bf, mode="tn", out_dtype=F32, name=f"{tag}_dw_down")
    dz_g, dz_v, g["cw_g"], g["cw_v"], g["cb_g"], g["cb_v"] = _conv_bwd(
        s["z_g"], s["z_v"], s["c_g"], s["c_v"], da, w["cw_g"], w["cw_v"], n_seq=n_seq, seq=seq, name=f"{tag}_conv")
    dw_up_t = _mm(dz_g, s["h2"], mode="tn", out_dtype=F32, out_rows=(0, 2 * D_FF), name=f"{tag}_dw_up_g")
    g["w_up_t"] = _mm(dz_v, s["h2"], mode="tn", out_dtype=F32, out_rows=(D_FF, 2 * D_FF), out_prev=dw_up_t,
                      name=f"{tag}_dw_up_v")
    deps = sched("bwd_ffn_grads", l, dz_v, g)
    dx1, dx1_bf, dgain = _mm_rows((dz_g, dz_v), w["w_up_t"], mode="nn", fn=_rms_bwd_rows, out_dtypes=(F32, BF16),
                                  rows=(s["x1"], dx2), vecs=(w["ffn_norm"],), reduce=True, a_at=(0, D_FF),
                                  name=f"{tag}_dh2", deps=deps)
    g["ffn_norm"] = dgain.reshape(D_MODEL)
    dpa, dpb, dga, dgb = _merge_bwd(dx1_bf, w["w_out"], s["y_att"], s["y_sgu"], w["w_oa"], w["w_ob"], s["proj"],
                                    name=f"{tag}_merge")
    deps = sched("bwd_merge", l, dpa)
    g["w_out"] = _mm(s["merged"], dx1_bf, mode="tn", out_dtype=F32, name=f"{tag}_dw_out",
                     deps=deps)
    dy_att = _mm(dpa, w["w_oa"], mode="nt", out_dtype=BF16, name=f"{tag}_dy_att")
    dy_sgu = _mm(dpb, w["w_ob"], mode="nt", out_dtype=F32, name=f"{tag}_dy_sgu")
    g["w_oa"] = _mm(s["y_att"], dpa, mode="tn", out_dtype=F32, name=f"{tag}_dw_oa")
    g["w_ob"] = _mm(s["y_sgu"], dpb, mode="tn", out_dtype=F32, name=f"{tag}_dw_ob")
    deps = sched("bwd_out_grads", l, dy_att, g)
    dqkv, g["q_norm"], g["k_norm"], g["sinks"] = _attention_bwd(
        s["proj"], dy_att, w["q_norm"], w["k_norm"], w["sinks"], n_seq=n_seq, seq=seq, name=f"{tag}_att", deps=deps)
    deps = sched("bwd_att", l, dqkv)
    dsuv, g["sgu_norm"], g["w_s"], g["b_s"] = _sgu_bwd(
        s["proj"], dy_sgu, w["sgu_norm"], w["w_s"], w["bias_full"], n_seq=n_seq, seq=seq, name=f"{tag}_sgu", deps=deps)
    dproj = (dsuv, dga, dgb, dqkv)
    at = (QKV_WIDTH, QKV_WIDTH + 2 * SGU_WIDTH, QKV_WIDTH + 2 * SGU_WIDTH + D_MODEL, 0)
    g["w_in_t"] = _mm_tn_parts(dproj, at, s["h"], name=f"{tag}_dw_in")
    deps = sched("bwd_w_in_grad", l, dqkv, g)
    dx, dx_bf, dgain = _mm_rows(dproj, w["w_in_t"], mode="nn", fn=_rms_bwd_rows, out_dtypes=(F32, BF16),
                                rows=(s["x"], dx1), vecs=(w["mix_norm"],), reduce=True, a_at=at,
                                name=f"{tag}_dh", deps=deps)
    g["mix_norm"] = dgain.reshape(D_MODEL)
    return dx, dx_bf, g, sched("bwd_dh", l, dx)


def _local_step(x, target, weights, sched, *, n_seq, seq):
    depth = len(weights)
    saved = []
    h = _rms_fwd(x, weights[0]["mix_norm"], name="l0_mix_norm", deps=sched("begin", 0, x))
    for l in range(depth):
        tail = ("norm", weights[l + 1]["mix_norm"]) if l + 1 < depth else ("loss", target)
        out, s = _layer_fwd(x, h, weights[l], sched, tail, n_seq=n_seq, seq=seq, l=l)
        saved.append(s)
        if l + 1 < depth:
            x, h = out
    dy, dy_bf, loss_cols = out
    grads = [None] * depth
    deps = ()
    for l in reversed(range(depth)):
        dy, dy_bf, grads[l], deps = _layer_bwd(dy, dy_bf, weights[l], saved[l], sched, deps, n_seq=n_seq, seq=seq, l=l)
    return jnp.sum(loss_cols), dy, grads, deps


W_IN_SHARD = IN_WIDTH // N_DEV
W_UP_SHARD = 2 * D_FF // N_DEV
COL_MOVE_ROWS = 256


def _w_o_moves():
    return tuple((j, 0, LANES, 0, j * LANES) for j in range(N_DEV))


def _disassemble(mats, w, moves, *, name):
    R = mats[0].shape[0]
    tr = min(R, COL_MOVE_ROWS)
    n = len(mats)

    def body(*refs):
        m_refs, o_ref = refs[:n], refs[n]
        for j, lo, hi, which, at in moves:
            o_ref[j, :, lo:hi] = m_refs[which][:, at:at + hi - lo]

    return pl.pallas_call(
        body, name=name, grid=(R // tr,),
        in_specs=[pl.BlockSpec((tr, m.shape[1]), lambda i: (i, 0)) for m in mats],
        out_specs=pl.BlockSpec((N_DEV, tr, w), lambda i: (0, i, 0)),
        out_shape=jax.ShapeDtypeStruct((N_DEV, R, w), mats[0].dtype),
        compiler_params=_params(("parallel",)),
    )(*mats)


def _my_place():
    return lax.axis_index("x"), lax.axis_index("y"), lax.axis_index("c")


def _gathered_shape(shape, kind):
    r, c = shape
    return {"blocks": (N_DEV, r, c), "rows": (N_DEV * r, c), "cols": (r, N_DEV * c)}[kind]


def _gather_window(ref, kind, shape, j):
    r, c = shape
    if kind == "blocks":
        return ref.at[j]
    if kind == "rows":
        return ref.at[pl.ds(pl.multiple_of(j * r, r), r), :]
    return ref.at[:, pl.ds(pl.multiple_of(j * c, c), c)]


def _gather(srcs, kinds, *, name):
    n = len(srcs)
    shapes = [s.shape for s in srcs]
    per = 7

    def body(*refs):
        src_refs, dst_refs = refs[:n], refs[n:2 * n]
        send_sems, recv_sems, local_sems = refs[2 * n:]
        x, y, c = _my_place()
        me, sibling = (x, y, c), (x, y, 1 - c)
        chips = [(1 - x, y), (x, 1 - y), (1 - x, 1 - y)]

        def at(i, px, py, pc):
            return _gather_window(dst_refs[i], kinds[i], shapes[i], 4 * px + 2 * py + pc)

        def copy(i, k, block, to, src=None):
            return pltpu.make_async_remote_copy(
                src_ref=at(i, *block) if src is None else src, dst_ref=at(i, *block),
                send_sem=send_sems.at[per * i + k], recv_sem=recv_sems.at[per * i + k], device_id=to, device_id_type=MESH)

        mine = [pltpu.make_async_copy(src_refs[i], at(i, *me), local_sems.at[i]) for i in range(n)]
        for cp in mine:
            cp.start()
        started = []
        for i in range(n):
            first = [copy(i, 0, me, sibling, src=src_refs[i])]
            first += [copy(i, 1 + j, me, (*chip, c), src=src_refs[i]) for j, chip in enumerate(chips)]
            for cp in first:
                cp.start()
            started += first
        for i in range(n):
            for j, chip in enumerate(chips):
                copy(i, 1 + j, (*chip, c), me).wait_recv()
                fwd = copy(i, 4 + j, (*chip, c), sibling)
                fwd.start()
                started.append(fwd)
        for i in range(n):
            copy(i, 0, sibling, me).wait_recv()
            for j, chip in enumerate(chips):
                copy(i, 4 + j, (*chip, 1 - c), me).wait_recv()
        for cp in started:
            cp.wait_send()
        for cp in mine:
            cp.wait()

    return pl.pallas_call(
        body, name=name,
        out_shape=[jax.ShapeDtypeStruct(_gathered_shape(s.shape, k), s.dtype) for s, k in zip(srcs, kinds)],
        in_specs=[ANY] * n, out_specs=[ANY] * n,
        scratch_shapes=[pltpu.SemaphoreType.DMA((per * n,)), pltpu.SemaphoreType.DMA((per * n,)),
                        pltpu.SemaphoreType.DMA((n,))],
    )(*srcs)


HBM = pl.BlockSpec(memory_space=pltpu.HBM)
SEM = pl.BlockSpec(memory_space=pltpu.SEMAPHORE)
TOKEN = jax.ShapeDtypeStruct((SUBLANES, LANES), F32)
TOKEN_SPEC = pl.BlockSpec(memory_space=pltpu.VMEM)
SPLIT_PARAMS = pltpu.CompilerParams(has_side_effects=pltpu.SideEffectType.DATAFLOW_SIDE_EFFECTING)


def _in_hbm(x):
    return pltpu.with_memory_space_constraint(x, pltpu.HBM)


def _hbm_like(shape, dtype):
    return pltpu.HBM(shape, dtype)


def _place_own(shards, kinds, dtypes, *, name, deps=()):
    n = len(shards)
    shapes = [s.shape for s in shards]

    def body(*refs):
        s_refs, land_refs, bufs, sems = refs[:n], refs[n:2 * n], refs[2 * n:3 * n], refs[3 * n]
        x, y, c = _my_place()
        copies = []
        for i in range(n):
            bufs[i][...] = s_refs[i][...].astype(dtypes[i])
            copies.append(pltpu.make_async_copy(
                bufs[i], _gather_window(land_refs[i], kinds[i], shapes[i], 4 * x + 2 * y + c), sems.at[i]))
        for cp in copies:
            cp.start()
        for cp in copies:
            cp.wait()

    body, dep_specs, dep_args = _with_deps(body, n, deps)
    return pl.pallas_call(
        body, name=name,
        out_shape=[jax.ShapeDtypeStruct(_gathered_shape(s, k), d) for s, k, d in zip(shapes, kinds, dtypes)],
        in_specs=[pl.BlockSpec(memory_space=pltpu.VMEM)] * n + dep_specs, out_specs=[ANY] * n,
        scratch_shapes=[pltpu.VMEM(s, d) for s, d in zip(shapes, dtypes)] + [pltpu.SemaphoreType.DMA((n,))],
        compiler_params=_params(),
    )(*shards, *dep_args)


def _gather_start(lands, kinds, shapes, after=(), *, name):
    n = len(lands)
    n_after = len(after)

    def body(*refs):
        land_refs = refs[:n]
        send_sems, recv_sems = refs[n + n_after], refs[n + n_after + 1]
        x, y, c = _my_place()
        targets = [(x, y, 1 - c), (1 - x, y, c), (x, 1 - y, c), (1 - x, 1 - y, c)]
        for i in range(n):
            own = _gather_window(land_refs[i], kinds[i], shapes[i], 4 * x + 2 * y + c)
            for k, to in enumerate(targets):
                pltpu.make_async_remote_copy(
                    src_ref=own, dst_ref=own, send_sem=send_sems.at[4 * i + k], recv_sem=recv_sems.at[4 * i + k],
                    device_id=to, device_id_type=MESH).start()
        refs[-1][...] = jnp.zeros_like(refs[-1])

    outs = pl.pallas_call(
        body, name=name,
        out_shape=[pltpu.SemaphoreType.DMA((4 * n,)), pltpu.SemaphoreType.DMA((4 * n,))]
        + [_hbm_like(a.shape, a.dtype) for a in lands] + [TOKEN],
        in_specs=[HBM] * n + [ANY] * n_after, out_specs=[SEM, SEM] + [HBM] * n + [TOKEN_SPEC],
        input_output_aliases={i: 2 + i for i in range(n)},
        compiler_params=SPLIT_PARAMS,
    )(*[_in_hbm(a) for a in lands], *after)
    return outs[0], outs[1], outs[2:2 + n], outs[-1]


def _gather_forward(recv_sems, lands, kinds, shapes, after, *, name):
    n = len(lands)

    def body(*refs):
        recv_ref, land_refs = refs[0], refs[1:1 + n]
        fwd_send, fwd_recv = refs[2 + n], refs[3 + n]
        token = refs[-1]
        x, y, c = _my_place()
        chips = [(1 - x, y), (x, 1 - y), (1 - x, 1 - y)]
        for i in range(n):
            for j, (px, py) in enumerate(chips):
                block = _gather_window(land_refs[i], kinds[i], shapes[i], 4 * px + 2 * py + c)
                pltpu.make_async_remote_copy(
                    src_ref=block, dst_ref=block, send_sem=fwd_send.at[3 * i + j], recv_sem=recv_ref.at[4 * i + 1 + j],
                    device_id=(px, py, c), device_id_type=MESH).wait_recv()
                pltpu.make_async_remote_copy(
                    src_ref=block, dst_ref=block, send_sem=fwd_send.at[3 * i + j], recv_sem=fwd_recv.at[3 * i + j],
                    device_id=(x, y, 1 - c), device_id_type=MESH).start()
        token[...] = jnp.zeros_like(token)

    outs = pl.pallas_call(
        body, name=name,
        out_shape=[pltpu.SemaphoreType.DMA((3 * n,)), pltpu.SemaphoreType.DMA((3 * n,))]
        + [_hbm_like(a.shape, a.dtype) for a in lands] + [TOKEN],
        in_specs=[SEM] + [HBM] * n + [ANY], out_specs=[SEM, SEM] + [HBM] * n + [TOKEN_SPEC],
        input_output_aliases={1 + i: 2 + i for i in range(n)},
        compiler_params=SPLIT_PARAMS,
    )(recv_sems, *lands, after)
    return outs[0], outs[1], outs[2:2 + n], outs[-1]


def _gather_finish(send_sems, recv_sems, fwd_send, fwd_recv, lands, kinds, shapes, after, *, name):
    n = len(lands)

    def body(*refs):
        send_ref, recv_ref, fsend_ref, frecv_ref = refs[:4]
        land_refs = refs[4:4 + n]
        x, y, c = _my_place()
        chips = [(1 - x, y), (x, 1 - y), (1 - x, 1 - y)]
        sibling = (x, y, 1 - c)
        for i in range(n):
            def window(j):
                return _gather_window(land_refs[i], kinds[i], shapes[i], j)

            mine, theirs = window(4 * x + 2 * y + c), window(4 * x + 2 * y + (1 - c))
            pltpu.make_async_remote_copy(src_ref=mine, dst_ref=theirs, send_sem=send_ref.at[4 * i],
                                         recv_sem=recv_ref.at[4 * i], device_id=sibling, device_id_type=MESH).wait_recv()
            for j, (px, py) in enumerate(chips):
                block = window(4 * px + 2 * py + (1 - c))
                pltpu.make_async_remote_copy(src_ref=block, dst_ref=block, send_sem=fsend_ref.at[3 * i + j],
                                             recv_sem=frecv_ref.at[3 * i + j], device_id=sibling,
                                             device_id_type=MESH).wait_recv()
            for k in range(4):
                pltpu.make_async_remote_copy(src_ref=mine, dst_ref=mine, send_sem=send_ref.at[4 * i + k],
                                             recv_sem=recv_ref.at[4 * i + k], device_id=sibling,
                                             device_id_type=MESH).wait_send()
            for j, (px, py) in enumerate(chips):
                block = window(4 * px + 2 * py + c)
                pltpu.make_async_remote_copy(src_ref=block, dst_ref=block, send_sem=fsend_ref.at[3 * i + j],
                                             recv_sem=frecv_ref.at[3 * i + j], device_id=sibling,
                                             device_id_type=MESH).wait_send()

    return pl.pallas_call(
        body, name=name,
        out_shape=[_hbm_like(a.shape, a.dtype) for a in lands],
        in_specs=[SEM] * 4 + [HBM] * n + [ANY], out_specs=[HBM] * n,
        input_output_aliases={4 + i: i for i in range(n)},
        compiler_params=SPLIT_PARAMS,
    )(send_sems, recv_sems, fwd_send, fwd_recv, *lands, after)


def _pair_plan(src_ref, land_ref, x, y, c):
    return [(src_ref.at[2 * k + (1 - c)], land_ref.at[k], (x, y, 1 - c)) for k in range(N_CHIPS)]


def _chip_plan(src_ref, land_ref, x, y, c):
    chips = [(1 - x, y), (x, 1 - y), (1 - x, 1 - y)]
    return [(src_ref.at[2 * px + py], land_ref.at[k], (px, py, c)) for k, (px, py) in enumerate(chips)]


def _exchange_copies(plan, per, src_refs, land_refs, send_sems, recv_sems):
    x, y, c = _my_place()
    copies = []
    for i, (s_ref, l_ref) in enumerate(zip(src_refs, land_refs)):
        for q, (src, dst, to) in enumerate(plan(s_ref, l_ref, x, y, c)):
            copies.append(pltpu.make_async_remote_copy(
                src_ref=src, dst_ref=dst, send_sem=send_sems.at[per * i + q], recv_sem=recv_sems.at[per * i + q],
                device_id=to, device_id_type=MESH))
    return copies


def _exchange_start(srcs, plan, per, *, name):
    n = len(srcs)

    def body(*refs):
        src_refs, land_refs = refs[:n], refs[n:2 * n]
        send_sems, recv_sems = refs[2 * n], refs[2 * n + 1]
        for cp in _exchange_copies(plan, per, src_refs, land_refs, send_sems, recv_sems):
            cp.start()
        refs[-1][...] = jnp.zeros_like(refs[-1])

    lands = [lax.empty((per,) + s.shape[1:], s.dtype) for s in srcs]
    outs = pl.pallas_call(
        body, name=name,
        out_shape=[pltpu.SemaphoreType.DMA((per * n,)), pltpu.SemaphoreType.DMA((per * n,))]
        + [_hbm_like(s.shape, s.dtype) for s in srcs] + [_hbm_like(a.shape, a.dtype) for a in lands] + [TOKEN],
        in_specs=[HBM] * (2 * n), out_specs=[SEM, SEM] + [HBM] * (2 * n) + [TOKEN_SPEC],
        input_output_aliases={i: 2 + i for i in range(2 * n)},
        compiler_params=SPLIT_PARAMS,
    )(*[_in_hbm(s) for s in srcs], *[_in_hbm(a) for a in lands])
    return outs[0], outs[1], outs[2:2 + n], outs[2 + n:2 + 2 * n], outs[-1]


def _exchange_wait(send_sems, recv_sems, srcs, lands, plan, per, after, *, name):
    n = len(srcs)
    after = list(after) if isinstance(after, (list, tuple)) else [after]

    def body(*refs):
        send_ref, recv_ref = refs[0], refs[1]
        src_refs, land_refs = refs[2:2 + n], refs[2 + n:2 + 2 * n]
        copies = _exchange_copies(plan, per, src_refs, land_refs, send_ref, recv_ref)
        for cp in copies:
            cp.wait_recv()
        for cp in copies:
            cp.wait_send()

    outs = pl.pallas_call(
        body, name=name,
        out_shape=[_hbm_like(s.shape, s.dtype) for s in srcs] + [_hbm_like(a.shape, a.dtype) for a in lands],
        in_specs=[SEM, SEM] + [HBM] * (2 * n) + [ANY] * len(after), out_specs=[HBM] * (2 * n),
        input_output_aliases={2 + i: i for i in range(2 * n)},
        compiler_params=SPLIT_PARAMS,
    )(send_sems, recv_sems, *srcs, *lands, *after)
    return outs[:n], outs[n:]


REDUCE_BLOCK_BYTES = 1 << 20


def _row_tile(r, c):
    row_bytes = 4 * (-(-c // LANES) * LANES)
    best = r
    for d in range(SUBLANES, r, SUBLANES):
        if r % d == 0 and d * row_bytes <= REDUCE_BLOCK_BYTES:
            best = d
    return best if r * row_bytes > REDUCE_BLOCK_BYTES else r


def _reduce_pair_sum(blocked, recv, place, wire_dtype, *, name):
    _, r, c = blocked.shape
    tr = _row_tile(r, c)

    def body(place_ref, g_ref, r_ref, own_ref, send_ref):
        s = g_ref[...] + r_ref[...]
        send_ref[...] = s.astype(wire_dtype)

        @pl.when(pl.program_id(1) == place_ref[1])
        def _():
            own_ref[...] = s

    return pl.pallas_call(
        body, name=name,
        grid_spec=pltpu.PrefetchScalarGridSpec(
            num_scalar_prefetch=1, grid=(r // tr, N_CHIPS),
            in_specs=[pl.BlockSpec((None, None, tr, c), lambda i, k, place_ref: (k, place_ref[0], i, 0)),
                      pl.BlockSpec((None, tr, c), lambda i, k, place_ref: (k, i, 0))],
            out_specs=[pl.BlockSpec((tr, c), lambda i, k, place_ref: (i, 0)),
                       pl.BlockSpec((None, tr, c), lambda i, k, place_ref: (k, i, 0))]),
        out_shape=[jax.ShapeDtypeStruct((r, c), F32), jax.ShapeDtypeStruct((N_CHIPS, r, c), wire_dtype)],
        compiler_params=_params(("parallel", "arbitrary")),
    )(place, blocked.reshape(N_CHIPS, 2, r, c), recv)


def _chip_sum(own_ref, r_ref):
    return ((own_ref[...] + r_ref[0].astype(F32)) + r_ref[1].astype(F32)) + r_ref[2].astype(F32)


def _reduce_chip_sum(own, recv, *, name):
    r, c = own.shape
    tr = _row_tile(r, c)

    def body(own_ref, r_ref, o_ref):
        o_ref[...] = _chip_sum(own_ref, r_ref)

    return pl.pallas_call(
        body, name=name, grid=(r // tr,),
        in_specs=[pl.BlockSpec((tr, c), lambda i: (i, 0)), pl.BlockSpec((N_CHIPS - 1, tr, c), lambda i: (0, i, 0))],
        out_specs=pl.BlockSpec((tr, c), lambda i: (i, 0)),
        out_shape=jax.ShapeDtypeStruct((r, c), F32),
        compiler_params=_params(("parallel",)),
    )(own, recv)


def _adamw_math(w, g, m, v):
    nm = ADAM_B1 * m + (1.0 - ADAM_B1) * g
    nv = ADAM_B2 * v + (1.0 - ADAM_B2) * (g * g)
    m_hat = nm / (1.0 - ADAM_B1 ** ADAM_STEP)
    v_hat = nv / (1.0 - ADAM_B2 ** ADAM_STEP)
    return -ADAM_LR * (m_hat / (jnp.sqrt(v_hat) + ADAM_EPS) + ADAM_WD * w), nm, nv


def _adamw(w, g, m, v, *, name):
    shape = w.shape
    C = shape[-1]
    R = math.prod(shape[:-1])
    tr = _row_tile(R, C)

    def body(w_ref, g_ref, m_ref, v_ref, d_ref, nm_ref, nv_ref):
        d_ref[...], nm_ref[...], nv_ref[...] = _adamw_math(w_ref[...], g_ref[...], m_ref[...], v_ref[...])

    spec = pl.BlockSpec((tr, C), lambda i: (i, 0))
    outs = pl.pallas_call(
        body, name=name, grid=(R // tr,),
        in_specs=[spec] * 4, out_specs=[spec] * 3,
        out_shape=[jax.ShapeDtypeStruct((R, C), F32)] * 3,
        compiler_params=_params(("parallel",)),
    )(*[a.reshape(R, C) for a in (w, g, m, v)])
    return tuple(o.reshape(shape) for o in outs)


def _reduce_adamw(own, recv, w, m, v, layer, prev, *, name):
    r, c = own.shape
    tr = _row_tile(r, c)
    n_prev = 0 if prev is None else len(prev)

    def body(own_ref, r_ref, w_ref, m_ref, v_ref, *rest):
        g_ref, d_ref, nm_ref, nv_ref = rest[n_prev:]
        g = _chip_sum(own_ref, r_ref)
        g_ref[...] = g
        d_ref[...], nm_ref[...], nv_ref[...] = _adamw_math(w_ref[...], g, m_ref[...], v_ref[...])

    slot = pl.BlockSpec((None, tr, c), lambda i: (layer, i, 0))
    return pl.pallas_call(
        body, name=name, grid=(r // tr,),
        in_specs=[pl.BlockSpec((tr, c), lambda i: (i, 0)), pl.BlockSpec((N_CHIPS - 1, tr, c), lambda i: (0, i, 0)),
                  slot, slot, slot] + [ANY] * n_prev,
        out_specs=[slot] * 4,
        out_shape=[jax.ShapeDtypeStruct((DEPTH, r, c), F32)] * 4,
        input_output_aliases={5 + k: k for k in range(n_prev)},
        compiler_params=_params(("parallel",)),
    )(own, recv, w, m, v, *(prev or ()))


REPLICATED = (("mix_norm", (D_MODEL,)), ("q_norm", (HEAD_DIM,)), ("k_norm", (HEAD_DIM,)), ("sinks", (N_Q_HEADS,)),
              ("sgu_norm", (SGU_WIDTH,)), ("w_s", (SGU_GROUPS, BLOCK, BLOCK)), ("b_s", (SGU_GROUPS, BLOCK)),
              ("ffn_norm", (D_MODEL,)), ("conv_b", (2 * D_FF,)))
TRANSPOSED = ("w_in", "w_up")
SHARDED = (("w_in", "rows"), ("w_oa", "cols"), ("w_ob", "cols"), ("w_out", "rows"), ("w_up", "rows"),
           ("conv_w", "blocks"), ("w_down", "rows"))
WEIGHT_ORDER = ("mix_norm", "w_in", "q_norm", "k_norm", "sinks", "sgu_norm", "w_s", "b_s", "w_oa", "w_ob", "w_out",
                "ffn_norm", "w_up", "conv_w", "conv_b", "w_down")
MIXER_WEIGHTS = ["w_in", "w_oa", "w_ob", "w_out"]
FFN_WEIGHTS = ["w_up", "conv_w", "w_down"]


def _small_layout():
    segs, off = {}, 0
    for l in range(DEPTH):
        for name, shape in REPLICATED:
            n = math.prod(shape)
            segs[(l, name)] = (off, n)
            off += n
    per_dev = -(-off // (N_DEV * SUBLANES * LANES)) * SUBLANES * LANES
    return segs, off, per_dev


def _pack_small(grads):
    ssegs, total, per_dev = _small_layout()
    flat = jnp.concatenate([grads[l][name].reshape(-1) for (l, name) in ssegs])
    return jnp.pad(flat, (0, N_DEV * per_dev - total)).reshape(N_DEV, per_dev // LANES, LANES)


def _unpack_small(gathered):
    ssegs, _, _ = _small_layout()
    flat = gathered.reshape(-1)
    shapes = dict(REPLICATED)
    return {name: jnp.stack([flat[ssegs[(l, name)][0]:ssegs[(l, name)][0] + ssegs[(l, name)][1]].reshape(shapes[name])
                             for l in range(DEPTH)]) for name, _ in REPLICATED}


def kernel(x, mix_norm, w_in, q_norm, k_norm, sinks, sgu_norm, w_s, b_s, w_oa, w_ob, w_out, ffn_norm, w_up, conv_w, conv_b, w_down, loss_target, m_mix_norm, m_w_in, m_q_norm, m_k_norm, m_sinks, m_sgu_norm, m_w_s, m_b_s, m_w_oa, m_w_ob, m_w_out, m_ffn_norm, m_w_up, m_conv_w, m_conv_b, m_w_down, v_mix_norm, v_w_in, v_q_norm, v_k_norm, v_sinks, v_sgu_norm, v_w_s, v_b_s, v_w_oa, v_w_ob, v_w_out, v_ffn_norm, v_w_up, v_conv_w, v_conv_b, v_w_down):
    W = dict(mix_norm=mix_norm, w_in=w_in, q_norm=q_norm, k_norm=k_norm, sinks=sinks, sgu_norm=sgu_norm, w_s=w_s, b_s=b_s,
             w_oa=w_oa, w_ob=w_ob, w_out=w_out, ffn_norm=ffn_norm, w_up=w_up, conv_w=conv_w, conv_b=conv_b, w_down=w_down)
    M = dict(mix_norm=m_mix_norm, w_in=m_w_in, q_norm=m_q_norm, k_norm=m_k_norm, sinks=m_sinks, sgu_norm=m_sgu_norm,
             w_s=m_w_s, b_s=m_b_s, w_oa=m_w_oa, w_ob=m_w_ob, w_out=m_w_out, ffn_norm=m_ffn_norm, w_up=m_w_up,
             conv_w=m_conv_w, conv_b=m_conv_b, w_down=m_w_down)
    V = dict(mix_norm=v_mix_norm, w_in=v_w_in, q_norm=v_q_norm, k_norm=v_k_norm, sinks=v_sinks, sgu_norm=v_sgu_norm,
             w_s=v_w_s, b_s=v_b_s, w_oa=v_w_oa, w_ob=v_w_ob, w_out=v_w_out, ffn_norm=v_ffn_norm, w_up=v_w_up,
             conv_w=v_conv_w, conv_b=v_conv_b, w_down=v_w_down)
    n_seq, seq, d_model = x.shape
    tokens = n_seq * seq
    mx, my, mc = _my_place()
    place = jnp.stack([mc, 2 * mx + my]).astype(jnp.int32)
    half = N_DEV // 2
    kind_of = dict(SHARDED)
    for name in TRANSPOSED:
        W[name], M[name], V[name] = (jnp.swapaxes(t[name], 1, 2) for t in (W, M, V))

    gather_groups = [[(0, MIXER_WEIGHTS[0])], [(0, n) for n in MIXER_WEIGHTS[1:]], [(0, n) for n in FFN_WEIGHTS],
                     [(1, n) for n in MIXER_WEIGHTS], [(1, n) for n in FFN_WEIGHTS]]
    started, in_flight = {}, {}
    weights = []
    for l in range(DEPTH):
        w = {name: W[name][l] for name, _ in REPLICATED}
        w["cb_g"], w["cb_v"] = W["conv_b"][l][:D_FF], W["conv_b"][l][D_FF:]
        w["bias_full"] = jnp.repeat(W["b_s"][l].T, SGU_WIDTH // SGU_GROUPS, axis=1)
        weights.append(w)

    def gather_start(gi, after=()):
        shards = [W[name][l] for l, name in gather_groups[gi]]
        kinds = [kind_of[name] for _, name in gather_groups[gi]]
        shapes = [s.shape for s in shards]
        lands = _place_own(shards, kinds, [F32 if name == "conv_w" else BF16 for _, name in gather_groups[gi]],
                           name=f"gather_weights_own_{gi}", deps=after)
        send, recv, lands, token = _gather_start(lands, kinds, shapes, after, name=f"gather_weights_start_{gi}")
        started[gi] = dict(sems=(send, recv), lands=lands, kinds=kinds, shapes=shapes)
        return token

    def gather_forward(gi, after):
        st = started[gi]
        in_flight[gi] = _gather_forward(st["sems"][1], st["lands"], st["kinds"], st["shapes"], after,
                                        name=f"gather_weights_forward_{gi}")
        return in_flight[gi][3]

    def gather_finish(gi, after):
        st = started.pop(gi)
        fwd_send, fwd_recv, lands_g, _ = in_flight.pop(gi)
        whole = _gather_finish(st["sems"][0], st["sems"][1], fwd_send, fwd_recv, lands_g, st["kinds"], st["shapes"], after,
                               name=f"gather_weights_finish_{gi}")
        for (l, name), arr in zip(gather_groups[gi], whole):
            w = weights[l]
            if name in TRANSPOSED:
                w[name + "_t"] = arr
            elif name == "conv_w":
                w["cw_g"] = arr[:half].transpose(1, 0, 2).reshape(3, D_FF)
                w["cw_v"] = arr[half:].transpose(1, 0, 2).reshape(3, D_FF)
            else:
                w[name] = arr

    reduce_state, results = {}, {}
    wire = {"conv_w": F32, "small": F32}

    def reduce_begin(key, names, arrays):
        send, recv, srcs_, lands_, token = _exchange_start(arrays, _pair_plan, N_CHIPS, name=f"reduce_pair_start_{key}")
        reduce_state[key] = dict(names=names, pair=(send, recv, srcs_, lands_))
        return [token]

    def reduce_pair(key, after):
        st = reduce_state[key]
        send, recv, srcs_, lands_ = st.pop("pair")
        blocked_, from_sibling = _exchange_wait(send, recv, srcs_, lands_, _pair_plan, N_CHIPS, after,
                                                name=f"reduce_pair_wait_{key}")
        sums = [_reduce_pair_sum(b, r, place, wire.get(n if isinstance(n, str) else n[1], BF16),
                                 name=f"reduce_pair_sum_{key}_{i}")
                for i, (n, b, r) in enumerate(zip(st["names"], blocked_, from_sibling))]
        st["own"] = [s[0] for s in sums]
        *st["chip"], token = _exchange_start([s[1] for s in sums], _chip_plan, N_CHIPS - 1, name=f"reduce_chip_start_{key}")
        return [token]

    def reduce_end(key, after):
        st = reduce_state.pop(key)
        send, recv, srcs_, lands_ = st["chip"]
        _, from_chips = _exchange_wait(send, recv, srcs_, lands_, _chip_plan, N_CHIPS - 1, after,
                                       name=f"reduce_chip_wait_{key}")
        done = []
        for n, own, got in zip(st["names"], st["own"], from_chips):
            if n == "small":
                results["small"] = _reduce_chip_sum(own, got, name="reduce_chip_sum_small")
            else:
                l, name = n
                results[name] = _reduce_adamw(own, got, W[name], M[name], V[name], l, results.get(name),
                                              name=f"l{l}_reduce_adamw_{name}")
                done.append(results[name][0])
        return done

    def sched(point, l, carry, g=None):
        deps = []
        if point == "begin":
            token = ()
            for gi in range(len(gather_groups)):
                token = [gather_start(gi, token)]
            deps = token
        elif point == "fwd_start" and l == 0:
            gather_finish(0, gather_forward(0, carry))
        elif point == "fwd_att" and l == 0:
            gather_finish(1, gather_forward(1, carry))
            deps = [gather_forward(2, carry)]
        elif point == "fwd_mixer_done" and l == 0:
            gather_finish(2, carry)
        elif point == "fwd_conv" and l == 0:
            deps = [gather_forward(3, carry)]
        elif point == "fwd_start" and l == 1:
            gather_finish(3, carry)
        elif point == "fwd_att" and l == 1:
            deps = [gather_forward(4, carry)]
        elif point == "fwd_mixer_done" and l == 1:
            gather_finish(4, carry)
        elif point == "bwd_ffn_grads":
            conv_w = jnp.concatenate([g[k].reshape(3, half, W_UP_SHARD).transpose(1, 0, 2) for k in ("cw_g", "cw_v")])
            deps = reduce_begin(
                f"l{l}_ffn", [(l, "w_down"), (l, "w_up"), (l, "conv_w")],
                [g["w_down"].reshape(N_DEV, D_FF // N_DEV, D_MODEL),
                 g["w_up_t"].reshape(N_DEV, W_UP_SHARD, D_MODEL), conv_w])
        elif point == "bwd_merge":
            deps = reduce_pair(f"l{l}_ffn", carry)
        elif point == "bwd_out_grads":
            deps = reduce_begin(
                f"l{l}_out", [(l, "w_out"), (l, "w_oa"), (l, "w_ob")],
                [g["w_out"].reshape(N_DEV, D_MODEL // N_DEV, D_MODEL),
                 _disassemble((g["w_oa"],), LANES, _w_o_moves(), name=f"l{l}_split_dw_oa"),
                 _disassemble((g["w_ob"],), LANES, _w_o_moves(), name=f"l{l}_split_dw_ob")])
        elif point == "bwd_att":
            deps = reduce_pair(f"l{l}_out", carry)
        elif point == "bwd_w_in_grad":
            deps = reduce_begin(f"l{l}_in", [(l, "w_in")], [g["w_in_t"].reshape(N_DEV, W_IN_SHARD, D_MODEL)])
        elif point == "bwd_dh":
            deps = reduce_pair(f"l{l}_in", carry)
        return deps

    loss_part, dx, grads, last_deps = _local_step(x.reshape(tokens, d_model), loss_target.reshape(tokens, d_model),
                                                  weights, sched, n_seq=n_seq, seq=seq)
    loss = lax.psum(loss_part, ("x", "y", "c"))

    for g in grads:
        g["conv_b"] = jnp.concatenate([g["cb_g"], g["cb_v"]])
    after = [dx, *last_deps, *reduce_begin("small", ["small"], [_pack_small(grads)])]
    for key in [f"l{l}_{part}" for l in reversed(range(DEPTH)) for part in ("ffn", "out", "in")][:-1]:
        after = reduce_end(key, after)
    after = reduce_end("l0_in", after + reduce_pair("small", after))
    reduce_end("small", after)

    G, delta, new_m, new_v = {}, {}, {}, {}
    for name, _ in SHARDED:
        outs = [jnp.swapaxes(o, 1, 2) for o in results[name]] if name in TRANSPOSED else results[name]
        G[name], delta[name], new_m[name], new_v[name] = outs
    G.update(_unpack_small(_gather([results["small"]], ["blocks"], name="gather_small_grads")[0]))
    for name, _ in REPLICATED:
        delta[name], new_m[name], new_v[name] = _adamw(W[name], G[name], M[name], V[name], name=f"adamw_{name}")
    return (loss, dx.reshape(n_seq, seq, d_model), *[G[n] for n in WEIGHT_ORDER], *[delta[n] for n in WEIGHT_ORDER],
            *[new_m[n] for n in WEIGHT_ORDER], *[new_v[n] for n in WEIGHT_ORDER])
```

```python
import math

import jax
import jax.numpy as jnp
from jax import lax
from jax.experimental import pallas as pl
from jax.experimental.pallas import tpu as pltpu

F32 = jnp.float32
BF16 = jnp.bfloat16
ACT_DTYPE = BF16
MESH = pl.DeviceIdType.MESH

DEPTH = 2
D_MODEL = 1024
N_Q_HEADS = 8
HEAD_DIM = 64
ATT_WIDTH = 512
KV_WIDTH = 128
BLOCK = 128
SGU_WIDTH = 512
SGU_GROUPS = 8
IN_WIDTH = 3840
D_FF = 2816
NORM_EPS = 1e-6
NEG_INF = -1e30
ATT_SCALE = HEAD_DIM ** -0.5
ALIBI_SLOPES = tuple(2.0 ** (-(h + 1)) for h in range(N_Q_HEADS))
ADAM_LR, ADAM_B1, ADAM_B2, ADAM_EPS, ADAM_WD, ADAM_STEP = 0.001, 0.9, 0.999, 1e-08, 0.01, 10
N_DEV = 8
N_CHIPS = 4

QKV_WIDTH = ATT_WIDTH + 2 * KV_WIDTH
COL_SUV, COL_GA, COL_GB, COL_QKV = 0, 1024, 2048, 3072
W_IN_ROTATE = (1, IN_WIDTH // QKV_WIDTH)

LANES = 128
SUBLANES = 8
VMEM_LIMIT_V7X = 56 * 1024 * 1024
GELU_C = math.sqrt(2.0 / math.pi)
GELU_K = 0.044715
ANY = pl.BlockSpec(memory_space=pl.ANY)


def _params(sem=None):
    return pltpu.CompilerParams(dimension_semantics=sem, vmem_limit_bytes=VMEM_LIMIT_V7X)


def _sigmoid(x):
    return 1.0 / (1.0 + jnp.exp(-x))


def _gelu(x):
    th = jnp.tanh(GELU_C * (x + GELU_K * x * x * x))
    return 0.5 * x * (1.0 + th)


def _gelu_and_grad(x):
    x2 = x * x
    th = jnp.tanh(GELU_C * (x + GELU_K * x2 * x))
    g = 0.5 * x * (1.0 + th)
    dg = 0.5 * (1.0 + th) + 0.5 * x * (1.0 - th * th) * (GELU_C * (1.0 + 3.0 * GELU_K * x2))
    return g, dg


def _dot(a, b, dims):
    return lax.dot_general(a, b, (dims, ((), ())), preferred_element_type=F32)


def _dot_nn(a, b):
    return _dot(a, b, ((1,), (0,)))


def _dot_nt(a, b):
    return _dot(a, b, ((1,), (1,)))


def _dot_tn(a, b):
    return _dot(a, b, ((0,), (0,)))


def _lo_mask(shape):
    return lax.broadcasted_iota(jnp.int32, shape, len(shape) - 1) < (LANES // 2)


def _half_sums(x, lo):
    s_lo = jnp.sum(jnp.where(lo, x, 0.0), axis=-1, keepdims=True)
    s_all = jnp.sum(x, axis=-1, keepdims=True)
    return jnp.where(lo, s_lo, s_all - s_lo)


def _dup_half(x, half, lo):
    r = pltpu.roll(x, LANES // 2, axis=1)
    return jnp.where(lo, x, r) if half == 0 else jnp.where(lo, r, x)


def _with_deps(body, n_in, deps):
    k = len(deps)
    if not k:
        return body, [], ()

    def skipping(*refs):
        return body(*refs[:n_in], *refs[n_in + k:])

    return skipping, [ANY] * k, tuple(deps)


MM_VMEM_BUDGET = 40 * 1024 * 1024
MM_MAX_TILE = 1408
MM_MAX_TK = 4096
MM_STEP_BYTES = 1 << 20


def _divisors(n, step, cap):
    return [d for d in range(step, min(n, cap) + 1, step) if n % d == 0] or [n]


def _mm_tiles(M, N, K, out_bytes, tm_divides, tn_divides):
    best = None
    for tm in _divisors(M, LANES, MM_MAX_TILE):
        for tn in _divisors(N, LANES, MM_MAX_TILE):
            if tm_divides % tm or tn_divides % tn:
                continue
            for tk in _divisors(K, 4 * LANES, MM_MAX_TK):
                vmem = 4 * (tm * tk + tk * tn) + 2 * tm * tn * out_bytes + (0 if tk == K else 4 * tm * tn)
                if vmem > MM_VMEM_BUDGET:
                    continue
                traffic = 2 * M * K * (N // tn) + 2 * K * N * (M // tm) + M * N * out_bytes
                cost = traffic + (K // tk - 1) * 8 * M * N + (M // tm) * (N // tn) * (K // tk) * MM_STEP_BYTES
                if best is None or cost < best[0]:
                    best = (cost, tm, tn, tk)
    assert best is not None, (M, N, K)
    return best[1:]


def _mm(a, b, *, mode, out_dtype, name, deps=(), b_rows=(0, None), rotate=None, out_rows=(0, None), out_prev=None):
    b_first, b_count = b_rows
    if mode == "nn":
        (M, K), N = a.shape, b.shape[1]
    elif mode == "nt":
        (M, K), N = a.shape, (b.shape[0] if b_count is None else b_count)
    else:
        (K, M), N = a.shape, b.shape[1]
    shift, period = rotate or (0, 1)
    assert period == 1 or mode == "nt"
    out_first, out_total = out_rows[0], (M if out_rows[1] is None else out_rows[1])
    tm, tn, tk = _mm_tiles(M, N, K, jnp.dtype(out_dtype).itemsize, math.gcd(M, out_first),
                           math.gcd(N // period, b_first if mode == "nt" else 0))
    gm, gn, gk = M // tm, N // tn, K // tk

    def turned(j):
        per = N // period // tn
        return ((j // per + shift) % period) * per + j % per if period > 1 else j

    if mode == "nn":
        a_spec = pl.BlockSpec((tm, tk), lambda i, j, k: (i, k))
        b_spec = pl.BlockSpec((tk, tn), lambda i, j, k: (k + b_first // tk, j))
        contract = ((1,), (0,))
    elif mode == "nt":
        a_spec = pl.BlockSpec((tm, tk), lambda i, j, k: (i, k))
        b_spec = pl.BlockSpec((tn, tk), lambda i, j, k: (turned(j) + b_first // tn, k))
        contract = ((1,), (1,))
    else:
        a_spec = pl.BlockSpec((tk, tm), lambda i, j, k: (k, i))
        b_spec = pl.BlockSpec((tk, tn), lambda i, j, k: (k, j))
        contract = ((0,), (0,))
    o_spec = pl.BlockSpec((tm, tn), lambda i, j, k: (i + out_first // tm, j))
    assert b_first % (tk if mode == "nn" else tn) == 0 and out_first % tm == 0, (name, tm, tn, tk)
    n_prev = 0 if out_prev is None else 1

    def body(a_ref, b_ref, *rest):
        o_ref = rest[n_prev]
        part = _dot(a_ref[...].astype(BF16), b_ref[...].astype(BF16), contract)
        if gk == 1:
            o_ref[...] = part.astype(out_dtype)
            return
        acc_ref = rest[n_prev + 1]
        k = pl.program_id(2)

        @pl.when(k == 0)
        def _():
            acc_ref[...] = part

        @pl.when(k > 0)
        def _():
            acc_ref[...] += part

        @pl.when(k == gk - 1)
        def _():
            o_ref[...] = acc_ref[...].astype(out_dtype)

    body, dep_specs, dep_args = _with_deps(body, 2 + n_prev, deps)
    return pl.pallas_call(
        body,
        name=name,
        grid=(gm, gn, gk),
        in_specs=[a_spec, b_spec] + [ANY] * n_prev + dep_specs,
        out_specs=o_spec,
        out_shape=jax.ShapeDtypeStruct((out_total, N), out_dtype),
        input_output_aliases={2: 0} if n_prev else {},
        scratch_shapes=[] if gk == 1 else [pltpu.VMEM((tm, tn), F32)],
        compiler_params=_params(("parallel", "parallel", "arbitrary")),
    )(a, b, *([out_prev] if n_prev else []), *dep_args)


def _mm_tn_parts(parts, at, b, *, name):
    K, N = b.shape
    n = len(parts)
    tm = math.gcd(*[p.shape[1] for p in parts], *at)
    tiles = [p.shape[1] // tm for p in parts]
    first = [sum(tiles[:p]) for p in range(n)]

    def mine(i, p):
        return jnp.logical_and(i >= first[p], i < first[p] + tiles[p])

    def out_tile(i):
        t = 0
        for p in range(n):
            t = jnp.where(mine(i, p), at[p] // tm + i - first[p], t)
        return t

    def body(*refs):
        a_refs, b_ref, o_ref = refs[:n], refs[n], refs[n + 1]
        for p in range(n):
            @pl.when(mine(pl.program_id(0), p))
            def _(p=p):
                o_ref[...] = _dot_tn(a_refs[p][...], b_ref[...])

    return pl.pallas_call(
        body, name=name, grid=(sum(tiles),),
        in_specs=[pl.BlockSpec((K, tm), lambda i, p=p: (0, jnp.clip(i - first[p], 0, tiles[p] - 1))) for p in range(n)]
        + [pl.BlockSpec((K, N), lambda i: (0, 0), pipeline_mode=pl.Buffered(1))],
        out_specs=pl.BlockSpec((tm, N), lambda i: (out_tile(i), 0)),
        out_shape=jax.ShapeDtypeStruct((sum(p.shape[1] for p in parts), N), F32),
        compiler_params=_params(("arbitrary",)),
    )(*parts, b)


def _mm_rows(a, b, *, mode, fn, out_dtypes, rows=(), vecs=(), reduce=False, name, deps=(), b_rows=(0, None), a_at=None):
    parts = a if a_at is not None else (a,)
    starts = a_at if a_at is not None else (0,)
    n_parts = len(parts)
    M, K = parts[0].shape[0], sum(p.shape[1] for p in parts)
    b_first, b_count = b_rows[0], (b.shape[0] if b_rows[1] is None else b_rows[1])
    N = b.shape[1] if mode == "nn" else b_count
    contract = ((1,), (0,)) if mode == "nn" else ((1,), (1,))
    n_rows, n_vecs, n_out = len(rows), len(vecs), len(out_dtypes)
    out_bytes = sum(jnp.dtype(d).itemsize for d in out_dtypes)
    tm = max(t for t in _divisors(M, LANES, MM_MAX_TILE)
             if 4 * t * K + 2 * K * N + 2 * t * N * (4 * n_rows + out_bytes) <= MM_VMEM_BUDGET)
    assert b_first % b_count == 0 and (a_at is None or mode == "nn")

    def body(*refs):
        a_refs, b_ref, rest = refs[:n_parts], refs[n_parts], refs[n_parts + 1:]
        row_refs, vec_refs = rest[:n_rows], rest[n_rows:n_rows + n_vecs]
        out_refs = rest[n_rows + n_vecs:]
        if a_at is None:
            acc = _dot(a_refs[0][...], b_ref[...], contract)
        else:
            acc = sum(_dot(r[...], b_ref[at:at + r.shape[1], :], contract) for r, at in zip(a_refs, starts))
        res = fn(acc, *[r[...] for r in row_refs], *[v[...] for v in vec_refs])
        for o_ref, val in zip(out_refs[:n_out], res):
            o_ref[...] = val.astype(o_ref.dtype)
        if reduce:
            @pl.when(pl.program_id(0) == 0)
            def _():
                out_refs[n_out][...] = res[n_out]

            @pl.when(pl.program_id(0) > 0)
            def _():
                out_refs[n_out][...] += res[n_out]

    row = pl.BlockSpec((tm, N), lambda i: (i, 0))
    vec = pl.BlockSpec((1, N), lambda i: (0, 0))
    body, dep_specs, dep_args = _with_deps(body, n_parts + 1 + n_rows + n_vecs, deps)
    return pl.pallas_call(
        body, name=name, grid=(M // tm,),
        in_specs=[pl.BlockSpec((tm, p.shape[1]), lambda i: (i, 0)) for p in parts]
        + [pl.BlockSpec((b_count, b.shape[1]), lambda i: (b_first // b_count, 0), pipeline_mode=pl.Buffered(1))]
        + [row] * n_rows + [vec] * n_vecs + dep_specs,
        out_specs=[row] * n_out + [vec] * reduce,
        out_shape=[jax.ShapeDtypeStruct((M, N), d) for d in out_dtypes] + [jax.ShapeDtypeStruct((1, N), F32)] * reduce,
        compiler_params=_params(("arbitrary",)),
    )(*parts, b, *rows, *[v.reshape(1, N) for v in vecs], *dep_args)


def _rms(x, gain):
    return x * lax.rsqrt(jnp.mean(x * x, axis=-1, keepdims=True) + NORM_EPS) * gain


def _residual_then_norm(acc, x, gain):
    x_out = x + acc
    return x_out, _rms(x_out, gain)


def _residual_then_loss(acc, x, target):
    err = (x + acc) - target
    dy = err * (1.0 / D_MODEL)
    return dy, dy, jnp.sum(err * err, axis=0, keepdims=True) * (0.5 / D_MODEL)


def _rms_bwd_rows(dh, x, dres, gain):
    r = lax.rsqrt(jnp.mean(x * x, axis=-1, keepdims=True) + NORM_EPS)
    xh = x * r
    dxh = dh * gain
    dx = dres + r * (dxh - xh * jnp.mean(dxh * xh, axis=-1, keepdims=True))
    return dx, dx, jnp.sum(dh * xh, axis=0, keepdims=True)


def _rms_fwd(x, gain, *, name, tm=512, deps=()):
    T, D = x.shape

    def body(x_ref, g_ref, h_ref):
        xv = x_ref[...]
        r = lax.rsqrt(jnp.mean(xv * xv, axis=-1, keepdims=True) + NORM_EPS)
        h_ref[...] = (xv * r * g_ref[...]).astype(BF16)

    body, dep_specs, dep_args = _with_deps(body, 2, deps)
    return pl.pallas_call(
        body, name=name, grid=(T // tm,),
        in_specs=[pl.BlockSpec((tm, D), lambda i: (i, 0)), pl.BlockSpec((1, D), lambda i: (0, 0))] + dep_specs,
        out_specs=pl.BlockSpec((tm, D), lambda i: (i, 0)),
        out_shape=jax.ShapeDtypeStruct((T, D), BF16),
        compiler_params=_params(("parallel",)),
    )(x, gain.reshape(1, D), *dep_args)


def _head_norm(x, gain2, lo):
    ms = _half_sums(x * x, lo) * (1.0 / HEAD_DIM)
    r = lax.rsqrt(ms + NORM_EPS)
    xh = x * r
    return xh * gain2, xh, r


def _head_norm_bwd(xh, r, gain2, dy, lo):
    dxh = dy * gain2
    dx = r * (dxh - xh * (_half_sums(dxh * xh, lo) * (1.0 / HEAD_DIM)))
    return dx, dy * xh


Q_GROUP = N_Q_HEADS // 2
GROUP_ROWS = Q_GROUP * BLOCK
ATT_SCRATCH = (pltpu.VMEM((2, 2, GROUP_ROWS, BLOCK), F32), pltpu.VMEM((2, GROUP_ROWS, 1), F32))


def _att_consts(sink_ref, bias_ref, sinkcol_ref):
    row = lax.broadcasted_iota(jnp.int32, (GROUP_ROWS, BLOCK), 0)
    kj = lax.broadcasted_iota(jnp.int32, (GROUP_ROWS, BLOCK), 1)
    head = row // BLOCK
    head_col = lax.broadcasted_iota(jnp.int32, (GROUP_ROWS, 1), 0) // BLOCK
    d_cur = (row % BLOCK) - kj
    d_prev = d_cur + BLOCK
    for kv in range(2):
        slope = jnp.zeros((GROUP_ROWS, BLOCK), F32)
        sink = jnp.zeros((GROUP_ROWS, 1), F32)
        for r in range(Q_GROUP):
            slope = jnp.where(head == r, ALIBI_SLOPES[Q_GROUP * kv + r], slope)
            sink = jnp.where(head_col == r, sink_ref[Q_GROUP * kv + r], sink)
        bias_ref[kv, 0] = jnp.where(d_cur >= 0, -slope * d_cur.astype(F32), NEG_INF)
        bias_ref[kv, 1] = jnp.where(d_prev < BLOCK, -slope * d_prev.astype(F32), NEG_INF)
        sinkcol_ref[kv] = sink


def _stack_heads(t0, t1, lo):
    z = jnp.zeros_like(t0)
    return jnp.concatenate([jnp.where(lo, t0, z), jnp.where(lo, z, t0), jnp.where(lo, t1, z), jnp.where(lo, z, t1)], axis=0)


def _unstack_heads(x4, lo):
    return (jnp.where(lo, x4[0:BLOCK], x4[BLOCK:2 * BLOCK]), jnp.where(lo, x4[2 * BLOCK:3 * BLOCK], x4[3 * BLOCK:]))


def _att_probs(q4, k2c, k2p, bias_c, bias_p, sink, has_prev):
    s_c = _dot_nt(q4, k2c) * ATT_SCALE + bias_c
    s_p = jnp.where(has_prev, _dot_nt(q4, k2p) * ATT_SCALE + bias_p, NEG_INF)
    m = jnp.maximum(jnp.max(jnp.maximum(s_c, s_p), axis=-1, keepdims=True), sink)
    e_c = jnp.exp(s_c - m)
    e_p = jnp.exp(s_p - m)
    e_s = jnp.exp(sink - m)
    inv = 1.0 / (jnp.sum(e_c + e_p, axis=-1, keepdims=True) + e_s)
    return e_c * inv, e_p * inv, e_s * inv


def _attention_fwd(proj, q_gain, k_gain, sinks, *, n_seq, seq, name):
    T = n_seq * seq
    nb = seq // BLOCK
    qcol, kvcol = COL_QKV // ATT_WIDTH, (COL_QKV + ATT_WIDTH) // (2 * KV_WIDTH)

    def body(q_ref, kv_ref, qg_ref, kg_ref, sink_ref, y_ref, bias_ref, sinkcol_ref):
        lo = _lo_mask((BLOCK, LANES))
        qg, kg = qg_ref[...], kg_ref[...]
        _att_consts(sink_ref, bias_ref, sinkcol_ref)

        def block(i, carry):
            r0 = pl.multiple_of(i * BLOCK, BLOCK)
            rp = pl.multiple_of(jnp.maximum(i - 1, 0) * BLOCK, BLOCK)
            has_prev = i > 0
            kn_c = _head_norm(kv_ref[pl.ds(r0, BLOCK), 0:KV_WIDTH].astype(F32), kg, lo)[0].astype(BF16)
            kn_p = _head_norm(kv_ref[pl.ds(rp, BLOCK), 0:KV_WIDTH].astype(F32), kg, lo)[0].astype(BF16)
            v_c = kv_ref[pl.ds(r0, BLOCK), KV_WIDTH:2 * KV_WIDTH].astype(BF16)
            v_p = kv_ref[pl.ds(rp, BLOCK), KV_WIDTH:2 * KV_WIDTH].astype(BF16)
            for kv in range(2):
                k2c, k2p = _dup_half(kn_c, kv, lo), _dup_half(kn_p, kv, lo)
                v2c, v2p = _dup_half(v_c, kv, lo), _dup_half(v_p, kv, lo)
                cols = [slice((2 * kv + t) * LANES, (2 * kv + t + 1) * LANES) for t in range(2)]
                qn = [_head_norm(q_ref[pl.ds(r0, BLOCK), c].astype(F32), qg, lo)[0] for c in cols]
                q4 = _stack_heads(qn[0], qn[1], lo).astype(BF16)
                p_c, p_p, _ = _att_probs(q4, k2c, k2p, bias_ref[kv, 0], bias_ref[kv, 1], sinkcol_ref[kv], has_prev)
                o4 = _dot_nn(p_c.astype(BF16), v2c) + _dot_nn(p_p.astype(BF16), v2p)
                for c, out in zip(cols, _unstack_heads(o4, lo)):
                    y_ref[pl.ds(r0, BLOCK), c] = out.astype(BF16)
            return carry

        lax.fori_loop(0, nb, block, 0)

    vec = pl.BlockSpec((1, LANES), lambda b: (0, 0))
    return pl.pallas_call(
        body, name=name, grid=(n_seq,),
        in_specs=[pl.BlockSpec((seq, ATT_WIDTH), lambda b: (b, qcol)),
                  pl.BlockSpec((seq, 2 * KV_WIDTH), lambda b: (b, kvcol)),
                  vec, vec, pl.BlockSpec(memory_space=pltpu.SMEM)],
        out_specs=pl.BlockSpec((seq, ATT_WIDTH), lambda b: (b, 0)),
        out_shape=jax.ShapeDtypeStruct((T, ATT_WIDTH), BF16),
        scratch_shapes=list(ATT_SCRATCH),
        compiler_params=_params(("parallel",)),
    )(proj, proj, jnp.tile(q_gain, 2).reshape(1, LANES), jnp.tile(k_gain, 2).reshape(1, LANES), sinks)


def _attention_bwd(proj, dy, q_gain, k_gain, sinks, *, n_seq, seq, name, deps=()):
    T = n_seq * seq
    nb = seq // BLOCK
    qcol, kvcol = COL_QKV // ATT_WIDTH, (COL_QKV + ATT_WIDTH) // (2 * KV_WIDTH)

    def body(q_ref, kv_ref, dy_ref, qg_ref, kg_ref, sink_ref, dqkv_ref, dqg_ref, dkg_ref, dsink_ref,
             dkn_acc, dv_acc, qg_acc, kg_acc, sink_acc, bias_ref, sinkcol_ref):
        lo = _lo_mask((BLOCK, LANES))
        qg, kg = qg_ref[...], kg_ref[...]
        _att_consts(sink_ref, bias_ref, sinkcol_ref)
        first = pl.program_id(0) == 0

        @pl.when(first)
        def _():
            qg_acc[...] = jnp.zeros_like(qg_acc)
            kg_acc[...] = jnp.zeros_like(kg_acc)
            sink_acc[...] = jnp.zeros_like(sink_acc)

        dkn_acc[...] = jnp.zeros_like(dkn_acc)
        dv_acc[...] = jnp.zeros_like(dv_acc)

        def block(i, carry):
            r0 = pl.multiple_of(i * BLOCK, BLOCK)
            rp = pl.multiple_of(jnp.maximum(i - 1, 0) * BLOCK, BLOCK)
            has_prev = i > 0
            kn_c = _head_norm(kv_ref[pl.ds(r0, BLOCK), 0:KV_WIDTH].astype(F32), kg, lo)[0].astype(BF16)
            kn_p = _head_norm(kv_ref[pl.ds(rp, BLOCK), 0:KV_WIDTH].astype(F32), kg, lo)[0].astype(BF16)
            v_c = kv_ref[pl.ds(r0, BLOCK), KV_WIDTH:2 * KV_WIDTH].astype(BF16)
            v_p = kv_ref[pl.ds(rp, BLOCK), KV_WIDTH:2 * KV_WIDTH].astype(BF16)
            dk_c, dk_p, dv_c, dv_p = [], [], [], []
            for kv in range(2):
                k2c, k2p = _dup_half(kn_c, kv, lo), _dup_half(kn_p, kv, lo)
                v2c, v2p = _dup_half(v_c, kv, lo), _dup_half(v_p, kv, lo)
                cols = [slice((2 * kv + t) * LANES, (2 * kv + t + 1) * LANES) for t in range(2)]
                normed = [_head_norm(q_ref[pl.ds(r0, BLOCK), c].astype(F32), qg, lo) for c in cols]
                q4 = _stack_heads(normed[0][0], normed[1][0], lo).astype(BF16)
                do4 = _stack_heads(dy_ref[pl.ds(r0, BLOCK), cols[0]], dy_ref[pl.ds(r0, BLOCK), cols[1]], lo)
                p_c, p_p, p_s = _att_probs(q4, k2c, k2p, bias_ref[kv, 0], bias_ref[kv, 1], sinkcol_ref[kv], has_prev)
                dp_c = _dot_nt(do4, v2c)
                dp_p = _dot_nt(do4, v2p)
                delta = jnp.sum(p_c * dp_c + p_p * dp_p, axis=-1, keepdims=True)
                ds_c = (p_c * (dp_c - delta)).astype(BF16)
                ds_p = (p_p * (dp_p - delta)).astype(BF16)
                sink_acc[kv] += -(p_s * delta)
                dq4 = (_dot_nn(ds_c, k2c) + _dot_nn(ds_p, k2p)) * ATT_SCALE
                for c, (_, qh, qr), dqn in zip(cols, normed, _unstack_heads(dq4, lo)):
                    dq, dg = _head_norm_bwd(qh, qr, qg, dqn, lo)
                    dqkv_ref[pl.ds(r0, BLOCK), c] = dq.astype(BF16)
                    qg_acc[...] += dg
                dk_c.append(_dot_tn(ds_c, q4))
                dk_p.append(_dot_tn(ds_p, q4))
                dv_c.append(_dot_tn(p_c.astype(BF16), do4))
                dv_p.append(_dot_tn(p_p.astype(BF16), do4))

            def fold(parts):
                a = parts[0] + pltpu.roll(parts[0], LANES // 2, axis=1)
                b = parts[1] + pltpu.roll(parts[1], LANES // 2, axis=1)
                return jnp.where(lo, a, b)

            dkn_acc[pl.ds(r0, BLOCK), :] += fold(dk_c) * ATT_SCALE
            dkn_acc[pl.ds(rp, BLOCK), :] += fold(dk_p) * ATT_SCALE
            dv_acc[pl.ds(r0, BLOCK), :] += fold(dv_c)
            dv_acc[pl.ds(rp, BLOCK), :] += fold(dv_p)
            return carry

        lax.fori_loop(0, nb, block, 0)

        def finish(i, carry):
            r0 = pl.multiple_of(i * BLOCK, BLOCK)
            _, kh, kr = _head_norm(kv_ref[pl.ds(r0, BLOCK), 0:KV_WIDTH].astype(F32), kg, lo)
            dk, dg = _head_norm_bwd(kh, kr, kg, dkn_acc[pl.ds(r0, BLOCK), :], lo)
            dqkv_ref[pl.ds(r0, BLOCK), ATT_WIDTH:ATT_WIDTH + KV_WIDTH] = dk.astype(BF16)
            dqkv_ref[pl.ds(r0, BLOCK), ATT_WIDTH + KV_WIDTH:QKV_WIDTH] = dv_acc[pl.ds(r0, BLOCK), :].astype(BF16)
            kg_acc[...] += dg
            return carry

        lax.fori_loop(0, nb, finish, 0)

        @pl.when(pl.program_id(0) == n_seq - 1)
        def _():
            dqg_ref[...] = jnp.sum(qg_acc[...], axis=0, keepdims=True)
            dkg_ref[...] = jnp.sum(kg_acc[...], axis=0, keepdims=True)
            lane = lax.broadcasted_iota(jnp.int32, (1, LANES), 1)
            dsink = jnp.zeros((1, LANES), F32)
            for kv in range(2):
                for r in range(Q_GROUP):
                    total = jnp.sum(sink_acc[kv, r * BLOCK:(r + 1) * BLOCK, :], axis=0, keepdims=True)
                    dsink = jnp.where(lane == Q_GROUP * kv + r, total, dsink)
            dsink_ref[...] = dsink

    vec = pl.BlockSpec((1, LANES), lambda b: (0, 0))
    acc = pltpu.VMEM((BLOCK, LANES), F32)
    body, dep_specs, dep_args = _with_deps(body, 6, deps)
    dqkv, dqg, dkg, dsink = pl.pallas_call(
        body, name=name, grid=(n_seq,),
        in_specs=[pl.BlockSpec((seq, ATT_WIDTH), lambda b: (b, qcol)),
                  pl.BlockSpec((seq, 2 * KV_WIDTH), lambda b: (b, kvcol)),
                  pl.BlockSpec((seq, ATT_WIDTH), lambda b: (b, 0)),
                  vec, vec, pl.BlockSpec(memory_space=pltpu.SMEM)] + dep_specs,
        out_specs=[pl.BlockSpec((seq, QKV_WIDTH), lambda b: (b, 0)), vec, vec, vec],
        out_shape=[jax.ShapeDtypeStruct((T, QKV_WIDTH), BF16)] + [jax.ShapeDtypeStruct((1, LANES), F32)] * 3,
        scratch_shapes=[pltpu.VMEM((seq, KV_WIDTH), F32), pltpu.VMEM((seq, KV_WIDTH), F32), acc, acc,
                        pltpu.VMEM((2, GROUP_ROWS, 1), F32), *ATT_SCRATCH],
        compiler_params=_params(("arbitrary",)),
    )(proj, proj, dy, jnp.tile(q_gain, 2).reshape(1, LANES), jnp.tile(k_gain, 2).reshape(1, LANES), sinks, *dep_args)
    half = LANES // 2
    return dqkv, dqg[0, :half] + dqg[0, half:], dkg[0, :half] + dkg[0, half:], dsink[0, :N_Q_HEADS]


def _sgu_weights(w_ref):
    r = lax.broadcasted_iota(jnp.int32, (BLOCK, BLOCK), 0)
    c = lax.broadcasted_iota(jnp.int32, (BLOCK, BLOCK), 1)
    return [jnp.where(r >= c, w_ref[g], 0.0).astype(BF16) for g in range(SGU_GROUPS)]


def _sgu_fwd(proj, gain, w_s, bias_full, *, n_seq, seq, name):
    T = n_seq * seq
    nc = seq // BLOCK

    def body(suv_ref, g_ref, w_ref, b_ref, y_ref):
        lo = _lo_mask((BLOCK, LANES))
        wm = _sgu_weights(w_ref)
        gain_v = g_ref[...]

        def chunk(c, carry):
            r0 = pl.multiple_of(c * BLOCK, BLOCK)
            gv = _gelu(suv_ref[pl.ds(r0, BLOCK), SGU_WIDTH:2 * SGU_WIDTH].astype(F32))
            r = lax.rsqrt(jnp.mean(gv * gv, axis=-1, keepdims=True) + NORM_EPS)
            vn = (gv * r * gain_v).astype(BF16)
            for p in range(SGU_WIDTH // LANES):
                cols = slice(p * LANES, (p + 1) * LANES)
                vp = vn[:, cols]
                mixed = jnp.where(lo, _dot_nn(wm[2 * p], vp), _dot_nn(wm[2 * p + 1], vp)) + b_ref[:, cols]
                u = _gelu(suv_ref[pl.ds(r0, BLOCK), cols].astype(F32))
                y_ref[pl.ds(r0, BLOCK), cols] = (u * mixed).astype(BF16)
            return carry

        lax.fori_loop(0, nc, chunk, 0)

    return pl.pallas_call(
        body, name=name, grid=(n_seq,),
        in_specs=[pl.BlockSpec((seq, 2 * SGU_WIDTH), lambda b: (b, COL_SUV // (2 * SGU_WIDTH))),
                  pl.BlockSpec((1, SGU_WIDTH), lambda b: (0, 0)),
                  pl.BlockSpec((SGU_GROUPS, BLOCK, BLOCK), lambda b: (0, 0, 0)),
                  pl.BlockSpec((BLOCK, SGU_WIDTH), lambda b: (0, 0))],
        out_specs=pl.BlockSpec((seq, SGU_WIDTH), lambda b: (b, 0)),
        out_shape=jax.ShapeDtypeStruct((T, SGU_WIDTH), BF16),
        compiler_params=_params(("parallel",)),
    )(proj, gain.reshape(1, SGU_WIDTH), w_s, bias_full)


def _sgu_bwd(proj, dy, gain, w_s, bias_full, *, n_seq, seq, name, deps=()):
    T = n_seq * seq
    nc = seq // BLOCK
    n_tiles = SGU_WIDTH // LANES

    def body(suv_ref, dy_ref, g_ref, w_ref, b_ref, dsuv_ref, dg_ref, dw_ref, db_ref, dg_acc, dw_acc, db_acc):
        lo = _lo_mask((BLOCK, LANES))
        hi = jnp.logical_not(lo)
        wm = _sgu_weights(w_ref)
        wmt = [jnp.where(lax.broadcasted_iota(jnp.int32, (BLOCK, BLOCK), 1) >= lax.broadcasted_iota(jnp.int32, (BLOCK, BLOCK), 0),
                         w_ref[g].T, 0.0).astype(BF16) for g in range(SGU_GROUPS)]
        gain_v = g_ref[...]

        @pl.when(pl.program_id(0) == 0)
        def _():
            dg_acc[...] = jnp.zeros_like(dg_acc)
            dw_acc[...] = jnp.zeros_like(dw_acc)
            db_acc[...] = jnp.zeros_like(db_acc)

        def chunk(c, carry):
            r0 = pl.multiple_of(c * BLOCK, BLOCK)
            gv, dgelu_v = _gelu_and_grad(suv_ref[pl.ds(r0, BLOCK), SGU_WIDTH:2 * SGU_WIDTH].astype(F32))
            r = lax.rsqrt(jnp.mean(gv * gv, axis=-1, keepdims=True) + NORM_EPS)
            vh = gv * r
            vn = (vh * gain_v).astype(BF16)
            dvn_tiles = []
            for p in range(n_tiles):
                cols = slice(p * LANES, (p + 1) * LANES)
                vp = vn[:, cols]
                mixed = jnp.where(lo, _dot_nn(wm[2 * p], vp), _dot_nn(wm[2 * p + 1], vp)) + b_ref[:, cols]
                u, dgelu_u = _gelu_and_grad(suv_ref[pl.ds(r0, BLOCK), cols].astype(F32))
                dyv = dy_ref[pl.ds(r0, BLOCK), cols]
                dsuv_ref[pl.ds(r0, BLOCK), cols] = (dyv * mixed * dgelu_u).astype(BF16)
                dm = dyv * u
                db_acc[:, cols] += dm
                dm_bf = dm.astype(BF16)
                dvn_tiles.append(jnp.where(lo, _dot_nn(wmt[2 * p], dm_bf), _dot_nn(wmt[2 * p + 1], dm_bf)))
                dw_acc[2 * p] += _dot_nt(jnp.where(lo, dm, 0.0).astype(BF16), vp)
                dw_acc[2 * p + 1] += _dot_nt(jnp.where(hi, dm, 0.0).astype(BF16), vp)
            dvn = jnp.concatenate(dvn_tiles, axis=1)
            dg_acc[...] += dvn * vh
            dvh = dvn * gain_v
            dgv = r * (dvh - vh * jnp.mean(dvh * vh, axis=-1, keepdims=True))
            dsuv_ref[pl.ds(r0, BLOCK), SGU_WIDTH:2 * SGU_WIDTH] = (dgv * dgelu_v).astype(BF16)
            return carry

        lax.fori_loop(0, nc, chunk, 0)

        @pl.when(pl.program_id(0) == n_seq - 1)
        def _():
            dg_ref[...] = jnp.sum(dg_acc[...], axis=0, keepdims=True)
            r = lax.broadcasted_iota(jnp.int32, (BLOCK, BLOCK), 0)
            c = lax.broadcasted_iota(jnp.int32, (BLOCK, BLOCK), 1)
            for g in range(SGU_GROUPS):
                dw_ref[g] = jnp.where(r >= c, dw_acc[g], 0.0)
            lane = lax.broadcasted_iota(jnp.int32, (BLOCK, LANES), 1)
            out = jnp.zeros((BLOCK, LANES), F32)
            for p in range(n_tiles):
                tile = db_acc[:, p * LANES:(p + 1) * LANES]
                s_lo = jnp.sum(jnp.where(lo, tile, 0.0), axis=-1, keepdims=True)
                s_hi = jnp.sum(jnp.where(hi, tile, 0.0), axis=-1, keepdims=True)
                out = jnp.where(lane == 2 * p, s_lo, out)
                out = jnp.where(lane == 2 * p + 1, s_hi, out)
            db_ref[...] = out

    body, dep_specs, dep_args = _with_deps(body, 5, deps)
    dsuv, dg, dw, db = pl.pallas_call(
        body, name=name, grid=(n_seq,),
        in_specs=[pl.BlockSpec((seq, 2 * SGU_WIDTH), lambda b: (b, COL_SUV // (2 * SGU_WIDTH))),
                  pl.BlockSpec((seq, SGU_WIDTH), lambda b: (b, 0)),
                  pl.BlockSpec((1, SGU_WIDTH), lambda b: (0, 0)),
                  pl.BlockSpec((SGU_GROUPS, BLOCK, BLOCK), lambda b: (0, 0, 0)),
                  pl.BlockSpec((BLOCK, SGU_WIDTH), lambda b: (0, 0))] + dep_specs,
        out_specs=[pl.BlockSpec((seq, 2 * SGU_WIDTH), lambda b: (b, 0)),
                   pl.BlockSpec((1, SGU_WIDTH), lambda b: (0, 0)),
                   pl.BlockSpec((SGU_GROUPS, BLOCK, BLOCK), lambda b: (0, 0, 0)),
                   pl.BlockSpec((BLOCK, LANES), lambda b: (0, 0))],
        out_shape=[jax.ShapeDtypeStruct((T, 2 * SGU_WIDTH), BF16), jax.ShapeDtypeStruct((1, SGU_WIDTH), F32),
                   jax.ShapeDtypeStruct((SGU_GROUPS, BLOCK, BLOCK), F32), jax.ShapeDtypeStruct((BLOCK, LANES), F32)],
        scratch_shapes=[pltpu.VMEM((BLOCK, SGU_WIDTH), F32), pltpu.VMEM((SGU_GROUPS, BLOCK, BLOCK), F32),
                        pltpu.VMEM((BLOCK, SGU_WIDTH), F32)],
        compiler_params=_params(("arbitrary",)),
    )(proj, dy, gain.reshape(1, SGU_WIDTH), w_s, bias_full, *dep_args)
    return dsuv, dg.reshape(SGU_WIDTH), dw, db[:, :SGU_GROUPS].T


def _merge_fwd(y_att, y_sgu, w_oa, w_ob, proj, *, name, tm=1024, tn=512, deps=()):
    T = y_att.shape[0]

    def body(ya_ref, ys_ref, wa_ref, wb_ref, ga_ref, gb_ref, o_ref):
        pa = _dot_nn(ya_ref[...], wa_ref[...])
        pb = _dot_nn(ys_ref[...], wb_ref[...])
        o_ref[...] = (_sigmoid(ga_ref[...].astype(F32)) * pa + _sigmoid(gb_ref[...].astype(F32)) * pb).astype(BF16)

    act = pl.BlockSpec((tm, ATT_WIDTH), lambda i, j: (i, 0))
    wgt = pl.BlockSpec((ATT_WIDTH, tn), lambda i, j: (0, j))
    body, dep_specs, dep_args = _with_deps(body, 6, deps)
    return pl.pallas_call(
        body, name=name, grid=(T // tm, D_MODEL // tn),
        in_specs=[act, act, wgt, wgt,
                  pl.BlockSpec((tm, tn), lambda i, j: (i, j + COL_GA // tn)),
                  pl.BlockSpec((tm, tn), lambda i, j: (i, j + COL_GB // tn))] + dep_specs,
        out_specs=pl.BlockSpec((tm, tn), lambda i, j: (i, j)),
        out_shape=jax.ShapeDtypeStruct((T, D_MODEL), BF16),
        compiler_params=_params(("parallel", "parallel")),
    )(y_att, y_sgu, w_oa, w_ob, proj, proj, *dep_args)


def _merge_bwd(dx1_bf, w_out, y_att, y_sgu, w_oa, w_ob, proj, *, name, tm=1024, tn=512):
    T = y_att.shape[0]

    def body(dx_ref, wo_ref, ya_ref, ys_ref, wa_ref, wb_ref, ga_ref, gb_ref, dpa_ref, dpb_ref, dga_ref, dgb_ref):
        dm = _dot_nt(dx_ref[...], wo_ref[...])
        pa = _dot_nn(ya_ref[...], wa_ref[...])
        pb = _dot_nn(ys_ref[...], wb_ref[...])
        sa = _sigmoid(ga_ref[...].astype(F32))
        sb = _sigmoid(gb_ref[...].astype(F32))
        dpa_ref[...] = (dm * sa).astype(BF16)
        dpb_ref[...] = (dm * sb).astype(BF16)
        dga_ref[...] = (dm * pa * sa * (1.0 - sa)).astype(BF16)
        dgb_ref[...] = (dm * pb * sb * (1.0 - sb)).astype(BF16)

    act = pl.BlockSpec((tm, ATT_WIDTH), lambda i, j: (i, 0))
    wgt = pl.BlockSpec((ATT_WIDTH, tn), lambda i, j: (0, j))
    out = pl.BlockSpec((tm, tn), lambda i, j: (i, j))
    return pl.pallas_call(
        body, name=name, grid=(T // tm, D_MODEL // tn),
        in_specs=[pl.BlockSpec((tm, D_MODEL), lambda i, j: (i, 0)),
                  pl.BlockSpec((tn, D_MODEL), lambda i, j: (j, 0)),
                  act, act, wgt, wgt,
                  pl.BlockSpec((tm, tn), lambda i, j: (i, j + COL_GA // tn)),
                  pl.BlockSpec((tm, tn), lambda i, j: (i, j + COL_GB // tn))],
        out_specs=[out] * 4,
        out_shape=[jax.ShapeDtypeStruct((T, D_MODEL), BF16)] * 4,
        compiler_params=_params(("parallel", "parallel")),
    )(dx1_bf, w_out, y_att, y_sgu, w_oa, w_ob, proj, proj)


CONV_ROWS = 256
CONV_TN = 256


def _shift_rows(cur, prev8, k):
    rolled = pltpu.roll(cur, k, axis=0)
    head = jnp.where(lax.broadcasted_iota(jnp.int32, prev8.shape, 0) < k, pltpu.roll(prev8, k, axis=0), rolled[:SUBLANES])
    return jnp.concatenate([head, rolled[SUBLANES:]], axis=0)


def _shift_rows_up(cur, next8, k):
    n = cur.shape[0]
    rolled = pltpu.roll(cur, n - k, axis=0)
    tail = jnp.where(lax.broadcasted_iota(jnp.int32, next8.shape, 0) >= SUBLANES - k,
                     pltpu.roll(next8, SUBLANES - k, axis=0), rolled[n - SUBLANES:])
    return jnp.concatenate([rolled[:n - SUBLANES], tail], axis=0)


HALO_ROWS = 16


def _rows_before(z_ref, r0, first):
    rp = pl.multiple_of(jnp.maximum(r0 - HALO_ROWS, 0), HALO_ROWS)
    halo = z_ref[pl.ds(rp, HALO_ROWS), :].astype(F32)
    return jnp.where(first, 0.0, halo[HALO_ROWS - SUBLANES:])


def _conv_rows(z_ref, r0, first, w_ref, b_ref, rows):
    cur = z_ref[pl.ds(r0, rows), :].astype(F32)
    prev8 = _rows_before(z_ref, r0, first)
    z1 = _shift_rows(cur, prev8, 1)
    z2 = _shift_rows(cur, prev8, 2)
    return b_ref[...] + w_ref[0:1, :] * z2 + w_ref[1:2, :] * z1 + w_ref[2:3, :] * cur


def _conv_fwd(z_g, z_v, cw_g, cw_v, cb_g, cb_v, *, n_seq, seq, name):
    T = n_seq * seq
    tn, rows = CONV_TN, CONV_ROWS

    def body(zg_ref, zv_ref, wg_ref, wv_ref, bg_ref, bv_ref, a_ref, cg_ref, cv_ref):
        def step(s, carry):
            r0 = pl.multiple_of(s * rows, rows)
            first = s == 0
            g = _conv_rows(zg_ref, r0, first, wg_ref, bg_ref, rows)
            v = _conv_rows(zv_ref, r0, first, wv_ref, bv_ref, rows)
            a_ref[pl.ds(r0, rows), :] = (g * _sigmoid(g) * v).astype(BF16)
            cg_ref[pl.ds(r0, rows), :] = g.astype(ACT_DTYPE)
            cv_ref[pl.ds(r0, rows), :] = v.astype(ACT_DTYPE)
            return carry

        lax.fori_loop(0, seq // rows, step, 0)

    zs = pl.BlockSpec((seq, tn), lambda b, j: (b, j))
    ws = pl.BlockSpec((3, tn), lambda b, j: (0, j))
    bs = pl.BlockSpec((1, tn), lambda b, j: (0, j))
    return pl.pallas_call(
        body, name=name, grid=(n_seq, D_FF // tn),
        in_specs=[zs, zs, ws, ws, bs, bs], out_specs=[zs] * 3,
        out_shape=[jax.ShapeDtypeStruct((T, D_FF), BF16)] + [jax.ShapeDtypeStruct((T, D_FF), ACT_DTYPE)] * 2,
        compiler_params=_params(("parallel", "parallel")),
    )(z_g, z_v, cw_g, cw_v, cb_g.reshape(1, D_FF), cb_v.reshape(1, D_FF))


def _conv_bwd(z_g, z_v, c_g, c_v, da, cw_g, cw_v, *, n_seq, seq, name):
    T = n_seq * seq
    tn, rows = CONV_TN, CONV_ROWS
    n_steps = seq // rows

    def body(zg_ref, zv_ref, cg_ref, cv_ref, da_ref, wg_ref, wv_ref,
             dzg_ref, dzv_ref, dwg_ref, dwv_ref, dbg_ref, dbv_ref, dcg_ref, dcv_ref):
        def colsum(x):
            return jnp.sum(x, axis=0, keepdims=True)

        def grads(s, accs):
            r0 = pl.multiple_of(s * rows, rows)
            g = cg_ref[pl.ds(r0, rows), :].astype(F32)
            v = cv_ref[pl.ds(r0, rows), :].astype(F32)
            sg = _sigmoid(g)
            dav = da_ref[pl.ds(r0, rows), :].astype(F32)
            dcg = dav * v * (sg * (1.0 + g * (1.0 - sg)))
            dcv = dav * (g * sg)
            dcg_ref[pl.ds(r0, rows), :] = dcg
            dcv_ref[pl.ds(r0, rows), :] = dcv
            return accs[0] + colsum(dcg), accs[1] + colsum(dcv)

        zero = jnp.zeros((1, tn), F32)
        db = lax.fori_loop(0, n_steps, grads, (zero, zero))

        def back(s, accs):
            r0 = pl.multiple_of(s * rows, rows)
            last = s == n_steps - 1
            rn = pl.multiple_of(jnp.minimum(r0 + rows, seq - SUBLANES), SUBLANES)
            new = []
            for half, (dc_ref, w_ref, dz_ref, z_ref) in enumerate(((dcg_ref, wg_ref, dzg_ref, zg_ref),
                                                                   (dcv_ref, wv_ref, dzv_ref, zv_ref))):
                cur = dc_ref[pl.ds(r0, rows), :]
                nxt = jnp.where(last, 0.0, dc_ref[pl.ds(rn, SUBLANES), :])
                u1, u2 = _shift_rows_up(cur, nxt, 1), _shift_rows_up(cur, nxt, 2)
                dz_ref[pl.ds(r0, rows), :] = (w_ref[2:3, :] * cur + w_ref[1:2, :] * u1 + w_ref[0:1, :] * u2).astype(BF16)
                z = z_ref[pl.ds(r0, rows), :].astype(F32)
                new += [accs[3 * half] + colsum(u2 * z), accs[3 * half + 1] + colsum(u1 * z),
                        accs[3 * half + 2] + colsum(cur * z)]
            return tuple(new)

        dw = lax.fori_loop(0, n_steps, back, (zero,) * 6)
        first_seq = pl.program_id(1) == 0

        @pl.when(first_seq)
        def _():
            dwg_ref[...] = jnp.concatenate(dw[0:3], axis=0)
            dwv_ref[...] = jnp.concatenate(dw[3:6], axis=0)
            dbg_ref[...], dbv_ref[...] = db

        @pl.when(jnp.logical_not(first_seq))
        def _():
            dwg_ref[...] += jnp.concatenate(dw[0:3], axis=0)
            dwv_ref[...] += jnp.concatenate(dw[3:6], axis=0)
            dbg_ref[...] += db[0]
            dbv_ref[...] += db[1]

    zs = pl.BlockSpec((seq, tn), lambda j, b: (b, j))
    ws = pl.BlockSpec((3, tn), lambda j, b: (0, j))
    bs = pl.BlockSpec((1, tn), lambda j, b: (0, j))
    outs = pl.pallas_call(
        body, name=name, grid=(D_FF // tn, n_seq),
        in_specs=[zs] * 5 + [ws, ws],
        out_specs=[zs, zs, ws, ws, bs, bs],
        out_shape=[jax.ShapeDtypeStruct((T, D_FF), BF16)] * 2 + [jax.ShapeDtypeStruct((3, D_FF), F32)] * 2
        + [jax.ShapeDtypeStruct((1, D_FF), F32)] * 2,
        scratch_shapes=[pltpu.VMEM((seq, tn), F32), pltpu.VMEM((seq, tn), F32)],
        compiler_params=_params(("parallel", "arbitrary")),
    )(z_g, z_v, c_g, c_v, da, cw_g, cw_v)
    dz_g, dz_v, dw_g, dw_v, db_g, db_v = outs
    return dz_g, dz_v, dw_g, dw_v, db_g.reshape(D_FF), db_v.reshape(D_FF)


def _layer_fwd(x, h, w, sched, tail, *, n_seq, seq, l):
    tag = f"l{l}"
    deps = sched("fwd_start", l, h)
    proj = _mm(h, w["w_in_t"], mode="nt", out_dtype=ACT_DTYPE, rotate=W_IN_ROTATE, name=f"{tag}_proj", deps=deps)
    y_att = _attention_fwd(proj, w["q_norm"], w["k_norm"], w["sinks"], n_seq=n_seq, seq=seq, name=f"{tag}_att")
    deps = sched("fwd_att", l, y_att)
    y_sgu = _sgu_fwd(proj, w["sgu_norm"], w["w_s"], w["bias_full"], n_seq=n_seq, seq=seq, name=f"{tag}_sgu")
    merged = _merge_fwd(y_att, y_sgu, w["w_oa"], w["w_ob"], proj, name=f"{tag}_merge", deps=deps)
    x1, h2 = _mm_rows(merged, w["w_out"], mode="nn", fn=_residual_then_norm, out_dtypes=(F32, BF16), rows=(x,),
                      vecs=(w["ffn_norm"],), name=f"{tag}_out")
    deps = sched("fwd_mixer_done", l, x1)
    z_g = _mm(h2, w["w_up_t"], mode="nt", out_dtype=ACT_DTYPE, b_rows=(0, D_FF), name=f"{tag}_up_g", deps=deps)
    z_v = _mm(h2, w["w_up_t"], mode="nt", out_dtype=ACT_DTYPE, b_rows=(D_FF, D_FF), name=f"{tag}_up_v")
    a, c_g, c_v = _conv_fwd(z_g, z_v, w["cw_g"], w["cw_v"], w["cb_g"], w["cb_v"], n_seq=n_seq, seq=seq,
                            name=f"{tag}_conv")
    deps = sched("fwd_conv", l, a)
    if tail[0] == "norm":
        out = _mm_rows(a, w["w_down"], mode="nn", fn=_residual_then_norm, out_dtypes=(F32, BF16), rows=(x1,),
                       vecs=(tail[1],), name=f"{tag}_down", deps=deps)
    else:
        out = _mm_rows(a, w["w_down"], mode="nn", fn=_residual_then_loss, out_dtypes=(F32, BF16), rows=(x1, tail[1]),
                       reduce=True, name=f"{tag}_down", deps=deps)
    saved = dict(x=x, h=h, proj=proj, y_att=y_att, y_sgu=y_sgu, merged=merged, x1=x1, h2=h2, z_g=z_g, z_v=z_v,
                 c_g=c_g, c_v=c_v, a=a)
    return out, saved


def _layer_bwd(dx2, dx2_bf, w, s, sched, deps, *, n_seq, seq, l):
    tag = f"l{l}b"
    g = {}
    da = _mm(dx2_bf, w["w_down"], mode="nt", out_dtype=ACT_DTYPE, name=f"{tag}_da", deps=deps)
    g["w_down"] = _mm(s["a"], dx2_bf, mode="tn", out_dtype=F32, name=f"{tag}_dw_down")
    dz_g, dz_v, g["cw_g"], g["cw_v"], g["cb_g"], g["cb_v"] = _conv_bwd(
        s["z_g"], s["z_v"], s["c_g"], s["c_v"], da, w["cw_g"], w["cw_v"], n_seq=n_seq, seq=seq, name=f"{tag}_conv")
    dw_up_t = _mm(dz_g, s["h2"], mode="tn", out_dtype=F32, out_rows=(0, 2 * D_FF), name=f"{tag}_dw_up_g")
    g["w_up_t"] = _mm(dz_v, s["h2"], mode="tn", out_dtype=F32, out_rows=(D_FF, 2 * D_FF), out_prev=dw_up_t,
                      name=f"{tag}_dw_up_v")
    deps = sched("bwd_ffn_grads", l, dz_v, g)
    dx1, dx1_bf, dgain = _mm_rows((dz_g, dz_v), w["w_up_t"], mode="nn", fn=_rms_bwd_rows, out_dtypes=(F32, BF16),
                                  rows=(s["x1"], dx2), vecs=(w["ffn_norm"],), reduce=True, a_at=(0, D_FF),
                                  name=f"{tag}_dh2", deps=deps)
    g["ffn_norm"] = dgain.reshape(D_MODEL)
    dpa, dpb, dga, dgb = _merge_bwd(dx1_bf, w["w_out"], s["y_att"], s["y_sgu"], w["w_oa"], w["w_ob"], s["proj"],
                                    name=f"{tag}_merge")
    deps = sched("bwd_merge", l, dpa)
    g["w_out"] = _mm(s["merged"], dx1_bf, mode="tn", out_dtype=F32, name=f"{tag}_dw_out",
                     deps=deps)
    dy_att = _mm(dpa, w["w_oa"], mode="nt", out_dtype=BF16, name=f"{tag}_dy_att")
    dy_sgu = _mm(dpb, w["w_ob"], mode="nt", out_dtype=F32, name=f"{tag}_dy_sgu")
    g["w_oa"] = _mm(s["y_att"], dpa, mode="tn", out_dtype=F32, name=f"{tag}_dw_oa")
    g["w_ob"] = _mm(s["y_sgu"], dpb, mode="tn", out_dtype=F32, name=f"{tag}_dw_ob")
    deps = sched("bwd_out_grads", l, dy_att, g)
    dqkv, g["q_norm"], g["k_norm"], g["sinks"] = _attention_bwd(
        s["proj"], dy_att, w["q_norm"], w["k_norm"], w["sinks"], n_seq=n_seq, seq=seq, name=f"{tag}_att", deps=deps)
    deps = sched("bwd_att", l, dqkv)
    dsuv, g["sgu_norm"], g["w_s"], g["b_s"] = _sgu_bwd(
        s["proj"], dy_sgu, w["sgu_norm"], w["w_s"], w["bias_full"], n_seq=n_seq, seq=seq, name=f"{tag}_sgu", deps=deps)
    dproj = (dsuv, dga, dgb, dqkv)
    at = (QKV_WIDTH, QKV_WIDTH + 2 * SGU_WIDTH, QKV_WIDTH + 2 * SGU_WIDTH + D_MODEL, 0)
    g["w_in_t"] = _mm_tn_parts(dproj, at, s["h"], name=f"{tag}_dw_in")
    deps = sched("bwd_w_in_grad", l, dqkv, g)
    dx, dx_bf, dgain = _mm_rows(dproj, w["w_in_t"], mode="nn", fn=_rms_bwd_rows, out_dtypes=(F32, BF16),
                                rows=(s["x"], dx1), vecs=(w["mix_norm"],), reduce=True, a_at=at,
                                name=f"{tag}_dh", deps=deps)
    g["mix_norm"] = dgain.reshape(D_MODEL)
    return dx, dx_bf, g, sched("bwd_dh", l, dx)


def _local_step(x, target, weights, sched, *, n_seq, seq):
    depth = len(weights)
    saved = []
    h = _rms_fwd(x, weights[0]["mix_norm"], name="l0_mix_norm", deps=sched("begin", 0, x))
    for l in range(depth):
        tail = ("norm", weights[l + 1]["mix_norm"]) if l + 1 < depth else ("loss", target)
        out, s = _layer_fwd(x, h, weights[l], sched, tail, n_seq=n_seq, seq=seq, l=l)
        saved.append(s)
        if l + 1 < depth:
            x, h = out
    dy, dy_bf, loss_cols = out
    grads = [None] * depth
    deps = ()
    for l in reversed(range(depth)):
        dy, dy_bf, grads[l], deps = _layer_bwd(dy, dy_bf, weights[l], saved[l], sched, deps, n_seq=n_seq, seq=seq, l=l)
    return jnp.sum(loss_cols), dy, grads, deps


W_IN_SHARD = IN_WIDTH // N_DEV
W_UP_SHARD = 2 * D_FF // N_DEV
COL_MOVE_ROWS = 256


def _w_o_moves():
    return tuple((j, 0, LANES, 0, j * LANES) for j in range(N_DEV))


def _disassemble(mats, w, moves, *, name):
    R = mats[0].shape[0]
    tr = min(R, COL_MOVE_ROWS)
    n = len(mats)

    def body(*refs):
        m_refs, o_ref = refs[:n], refs[n]
        for j, lo, hi, which, at in moves:
            o_ref[j, :, lo:hi] = m_refs[which][:, at:at + hi - lo]

    return pl.pallas_call(
        body, name=name, grid=(R // tr,),
        in_specs=[pl.BlockSpec((tr, m.shape[1]), lambda i: (i, 0)) for m in mats],
        out_specs=pl.BlockSpec((N_DEV, tr, w), lambda i: (0, i, 0)),
        out_shape=jax.ShapeDtypeStruct((N_DEV, R, w), mats[0].dtype),
        compiler_params=_params(("parallel",)),
    )(*mats)


def _my_place():
    return lax.axis_index("x"), lax.axis_index("y"), lax.axis_index("c")


def _gathered_shape(shape, kind):
    r, c = shape
    return {"blocks": (N_DEV, r, c), "rows": (N_DEV * r, c), "cols": (r, N_DEV * c)}[kind]


def _gather_window(ref, kind, shape, j):
    r, c = shape
    if kind == "blocks":
        return ref.at[j]
    if kind == "rows":
        return ref.at[pl.ds(pl.multiple_of(j * r, r), r), :]
    return ref.at[:, pl.ds(pl.multiple_of(j * c, c), c)]


def _gather(srcs, kinds, *, name):
    n = len(srcs)
    shapes = [s.shape for s in srcs]
    per = 7

    def body(*refs):
        src_refs, dst_refs = refs[:n], refs[n:2 * n]
        send_sems, recv_sems, local_sems = refs[2 * n:]
        x, y, c = _my_place()
        me, sibling = (x, y, c), (x, y, 1 - c)
        chips = [(1 - x, y), (x, 1 - y), (1 - x, 1 - y)]

        def at(i, px, py, pc):
            return _gather_window(dst_refs[i], kinds[i], shapes[i], 4 * px + 2 * py + pc)

        def copy(i, k, block, to, src=None):
            return pltpu.make_async_remote_copy(
                src_ref=at(i, *block) if src is None else src, dst_ref=at(i, *block),
                send_sem=send_sems.at[per * i + k], recv_sem=recv_sems.at[per * i + k], device_id=to, device_id_type=MESH)

        mine = [pltpu.make_async_copy(src_refs[i], at(i, *me), local_sems.at[i]) for i in range(n)]
        for cp in mine:
            cp.start()
        started = []
        for i in range(n):
            first = [copy(i, 0, me, sibling, src=src_refs[i])]
            first += [copy(i, 1 + j, me, (*chip, c), src=src_refs[i]) for j, chip in enumerate(chips)]
            for cp in first:
                cp.start()
            started += first
        for i in range(n):
            for j, chip in enumerate(chips):
                copy(i, 1 + j, (*chip, c), me).wait_recv()
                fwd = copy(i, 4 + j, (*chip, c), sibling)
                fwd.start()
                started.append(fwd)
        for i in range(n):
            copy(i, 0, sibling, me).wait_recv()
            for j, chip in enumerate(chips):
                copy(i, 4 + j, (*chip, 1 - c), me).wait_recv()
        for cp in started:
            cp.wait_send()
        for cp in mine:
            cp.wait()

    return pl.pallas_call(
        body, name=name,
        out_shape=[jax.ShapeDtypeStruct(_gathered_shape(s.shape, k), s.dtype) for s, k in zip(srcs, kinds)],
        in_specs=[ANY] * n, out_specs=[ANY] * n,
        scratch_shapes=[pltpu.SemaphoreType.DMA((per * n,)), pltpu.SemaphoreType.DMA((per * n,)),
                        pltpu.SemaphoreType.DMA((n,))],
    )(*srcs)


HBM = pl.BlockSpec(memory_space=pltpu.HBM)
SEM = pl.BlockSpec(memory_space=pltpu.SEMAPHORE)
TOKEN = jax.ShapeDtypeStruct((SUBLANES, LANES), F32)
TOKEN_SPEC = pl.BlockSpec(memory_space=pltpu.VMEM)
SPLIT_PARAMS = pltpu.CompilerParams(has_side_effects=pltpu.SideEffectType.DATAFLOW_SIDE_EFFECTING)


def _in_hbm(x):
    return pltpu.with_memory_space_constraint(x, pltpu.HBM)


def _hbm_like(shape, dtype):
    return pltpu.HBM(shape, dtype)


def _place_own(shards, kinds, dtypes, *, name, deps=()):
    n = len(shards)
    shapes = [s.shape for s in shards]

    def body(*refs):
        s_refs, land_refs, bufs, sems = refs[:n], refs[n:2 * n], refs[2 * n:3 * n], refs[3 * n]
        x, y, c = _my_place()
        copies = []
        for i in range(n):
            bufs[i][...] = s_refs[i][...].astype(dtypes[i])
            copies.append(pltpu.make_async_copy(
                bufs[i], _gather_window(land_refs[i], kinds[i], shapes[i], 4 * x + 2 * y + c), sems.at[i]))
        for cp in copies:
            cp.start()
        for cp in copies:
            cp.wait()

    body, dep_specs, dep_args = _with_deps(body, n, deps)
    return pl.pallas_call(
        body, name=name,
        out_shape=[jax.ShapeDtypeStruct(_gathered_shape(s, k), d) for s, k, d in zip(shapes, kinds, dtypes)],
        in_specs=[pl.BlockSpec(memory_space=pltpu.VMEM)] * n + dep_specs, out_specs=[ANY] * n,
        scratch_shapes=[pltpu.VMEM(s, d) for s, d in zip(shapes, dtypes)] + [pltpu.SemaphoreType.DMA((n,))],
        compiler_params=_params(),
    )(*shards, *dep_args)


def _gather_start(lands, kinds, shapes, after=(), *, name):
    n = len(lands)
    n_after = len(after)

    def body(*refs):
        land_refs = refs[:n]
        send_sems, recv_sems = refs[n + n_after], refs[n + n_after + 1]
        x, y, c = _my_place()
        targets = [(x, y, 1 - c), (1 - x, y, c), (x, 1 - y, c), (1 - x, 1 - y, c)]
        for i in range(n):
            own = _gather_window(land_refs[i], kinds[i], shapes[i], 4 * x + 2 * y + c)
            for k, to in enumerate(targets):
                pltpu.make_async_remote_copy(
                    src_ref=own, dst_ref=own, send_sem=send_sems.at[4 * i + k], recv_sem=recv_sems.at[4 * i + k],
                    device_id=to, device_id_type=MESH).start()
        refs[-1][...] = jnp.zeros_like(refs[-1])

    outs = pl.pallas_call(
        body, name=name,
        out_shape=[pltpu.SemaphoreType.DMA((4 * n,)), pltpu.SemaphoreType.DMA((4 * n,))]
        + [_hbm_like(a.shape, a.dtype) for a in lands] + [TOKEN],
        in_specs=[HBM] * n + [ANY] * n_after, out_specs=[SEM, SEM] + [HBM] * n + [TOKEN_SPEC],
        input_output_aliases={i: 2 + i for i in range(n)},
        compiler_params=SPLIT_PARAMS,
    )(*[_in_hbm(a) for a in lands], *after)
    return outs[0], outs[1], outs[2:2 + n], outs[-1]


def _gather_forward(recv_sems, lands, kinds, shapes, after, *, name):
    n = len(lands)

    def body(*refs):
        recv_ref, land_refs = refs[0], refs[1:1 + n]
        fwd_send, fwd_recv = refs[2 + n], refs[3 + n]
        token = refs[-1]
        x, y, c = _my_place()
        chips = [(1 - x, y), (x, 1 - y), (1 - x, 1 - y)]
        for i in range(n):
            for j, (px, py) in enumerate(chips):
                block = _gather_window(land_refs[i], kinds[i], shapes[i], 4 * px + 2 * py + c)
                pltpu.make_async_remote_copy(
                    src_ref=block, dst_ref=block, send_sem=fwd_send.at[3 * i + j], recv_sem=recv_ref.at[4 * i + 1 + j],
                    device_id=(px, py, c), device_id_type=MESH).wait_recv()
                pltpu.make_async_remote_copy(
                    src_ref=block, dst_ref=block, send_sem=fwd_send.at[3 * i + j], recv_sem=fwd_recv.at[3 * i + j],
                    device_id=(x, y, 1 - c), device_id_type=MESH).start()
        token[...] = jnp.zeros_like(token)

    outs = pl.pallas_call(
        body, name=name,
        out_shape=[pltpu.SemaphoreType.DMA((3 * n,)), pltpu.SemaphoreType.DMA((3 * n,))]
        + [_hbm_like(a.shape, a.dtype) for a in lands] + [TOKEN],
        in_specs=[SEM] + [HBM] * n + [ANY], out_specs=[SEM, SEM] + [HBM] * n + [TOKEN_SPEC],
        input_output_aliases={1 + i: 2 + i for i in range(n)},
        compiler_params=SPLIT_PARAMS,
    )(recv_sems, *lands, after)
    return outs[0], outs[1], outs[2:2 + n], outs[-1]


def _gather_finish(send_sems, recv_sems, fwd_send, fwd_recv, lands, kinds, shapes, after, *, name):
    n = len(lands)

    def body(*refs):
        send_ref, recv_ref, fsend_ref, frecv_ref = refs[:4]
        land_refs = refs[4:4 + n]
        x, y, c = _my_place()
        chips = [(1 - x, y), (x, 1 - y), (1 - x, 1 - y)]
        sibling = (x, y, 1 - c)
        for i in range(n):
            def window(j):
                return _gather_window(land_refs[i], kinds[i], shapes[i], j)

            mine, theirs = window(4 * x + 2 * y + c), window(4 * x + 2 * y + (1 - c))
            pltpu.make_async_remote_copy(src_ref=mine, dst_ref=theirs, send_sem=send_ref.at[4 * i],
                                         recv_sem=recv_ref.at[4 * i], device_id=sibling, device_id_type=MESH).wait_recv()
            for j, (px, py) in enumerate(chips):
                block = window(4 * px + 2 * py + (1 - c))
                pltpu.make_async_remote_copy(src_ref=block, dst_ref=block, send_sem=fsend_ref.at[3 * i + j],
                                             recv_sem=frecv_ref.at[3 * i + j], device_id=sibling,
                                             device_id_type=MESH).wait_recv()
            for k in range(4):
                pltpu.make_async_remote_copy(src_ref=mine, dst_ref=mine, send_sem=send_ref.at[4 * i + k],
                                             recv_sem=recv_ref.at[4 * i + k], device_id=sibling,
                                             device_id_type=MESH).wait_send()
            for j, (px, py) in enumerate(chips):
                block = window(4 * px + 2 * py + c)
                pltpu.make_async_remote_copy(src_ref=block, dst_ref=block, send_sem=fsend_ref.at[3 * i + j],
                                             recv_sem=frecv_ref.at[3 * i + j], device_id=sibling,
                                             device_id_type=MESH).wait_send()

    return pl.pallas_call(
        body, name=name,
        out_shape=[_hbm_like(a.shape, a.dtype) for a in lands],
        in_specs=[SEM] * 4 + [HBM] * n + [ANY], out_specs=[HBM] * n,
        input_output_aliases={4 + i: i for i in range(n)},
        compiler_params=SPLIT_PARAMS,
    )(send_sems, recv_sems, fwd_send, fwd_recv, *lands, after)


def _pair_plan(src_ref, land_ref, x, y, c):
    return [(src_ref.at[2 * k + (1 - c)], land_ref.at[k], (x, y, 1 - c)) for k in range(N_CHIPS)]


def _chip_plan(src_ref, land_ref, x, y, c):
    chips = [(1 - x, y), (x, 1 - y), (1 - x, 1 - y)]
    return [(src_ref.at[2 * px + py], land_ref.at[k], (px, py, c)) for k, (px, py) in enumerate(chips)]


def _exchange_copies(plan, per, src_refs, land_refs, send_sems, recv_sems):
    x, y, c = _my_place()
    copies = []
    for i, (s_ref, l_ref) in enumerate(zip(src_refs, land_refs)):
        for q, (src, dst, to) in enumerate(plan(s_ref, l_ref, x, y, c)):
            copies.append(pltpu.make_async_remote_copy(
                src_ref=src, dst_ref=dst, send_sem=send_sems.at[per * i + q], recv_sem=recv_sems.at[per * i + q],
                device_id=to, device_id_type=MESH))
    return copies


def _exchange_start(srcs, plan, per, *, name):
    n = len(srcs)

    def body(*refs):
        src_refs, land_refs = refs[:n], refs[n:2 * n]
        send_sems, recv_sems = refs[2 * n], refs[2 * n + 1]
        for cp in _exchange_copies(plan, per, src_refs, land_refs, send_sems, recv_sems):
            cp.start()
        refs[-1][...] = jnp.zeros_like(refs[-1])

    lands = [lax.empty((per,) + s.shape[1:], s.dtype) for s in srcs]
    outs = pl.pallas_call(
        body, name=name,
        out_shape=[pltpu.SemaphoreType.DMA((per * n,)), pltpu.SemaphoreType.DMA((per * n,))]
        + [_hbm_like(s.shape, s.dtype) for s in srcs] + [_hbm_like(a.shape, a.dtype) for a in lands] + [TOKEN],
        in_specs=[HBM] * (2 * n), out_specs=[SEM, SEM] + [HBM] * (2 * n) + [TOKEN_SPEC],
        input_output_aliases={i: 2 + i for i in range(2 * n)},
        compiler_params=SPLIT_PARAMS,
    )(*[_in_hbm(s) for s in srcs], *[_in_hbm(a) for a in lands])
    return outs[0], outs[1], outs[2:2 + n], outs[2 + n:2 + 2 * n], outs[-1]


def _exchange_wait(send_sems, recv_sems, srcs, lands, plan, per, after, *, name):
    n = len(srcs)
    after = list(after) if isinstance(after, (list, tuple)) else [after]

    def body(*refs):
        send_ref, recv_ref = refs[0], refs[1]
        src_refs, land_refs = refs[2:2 + n], refs[2 + n:2 + 2 * n]
        copies = _exchange_copies(plan, per, src_refs, land_refs, send_ref, recv_ref)
        for cp in copies:
            cp.wait_recv()
        for cp in copies:
            cp.wait_send()

    outs = pl.pallas_call(
        body, name=name,
        out_shape=[_hbm_like(s.shape, s.dtype) for s in srcs] + [_hbm_like(a.shape, a.dtype) for a in lands],
        in_specs=[SEM, SEM] + [HBM] * (2 * n) + [ANY] * len(after), out_specs=[HBM] * (2 * n),
        input_output_aliases={2 + i: i for i in range(2 * n)},
        compiler_params=SPLIT_PARAMS,
    )(send_sems, recv_sems, *srcs, *lands, *after)
    return outs[:n], outs[n:]


REDUCE_BLOCK_BYTES = 2 << 20


def _row_tile(r, c):
    row_bytes = 4 * (-(-c // LANES) * LANES)
    best = r
    for d in range(SUBLANES, r, SUBLANES):
        if r % d == 0 and d * row_bytes <= REDUCE_BLOCK_BYTES:
            best = d
    return best if r * row_bytes > REDUCE_BLOCK_BYTES else r


def _reduce_pair_sum(blocked, recv, place, wire_dtype, *, name):
    _, r, c = blocked.shape
    tr = _row_tile(r, c)

    def body(place_ref, g_ref, r_ref, own_ref, send_ref):
        s = g_ref[...] + r_ref[...]
        send_ref[...] = s.astype(wire_dtype)

        @pl.when(pl.program_id(1) == place_ref[1])
        def _():
            own_ref[...] = s

    return pl.pallas_call(
        body, name=name,
        grid_spec=pltpu.PrefetchScalarGridSpec(
            num_scalar_prefetch=1, grid=(r // tr, N_CHIPS),
            in_specs=[pl.BlockSpec((None, None, tr, c), lambda i, k, place_ref: (k, place_ref[0], i, 0)),
                      pl.BlockSpec((None, tr, c), lambda i, k, place_ref: (k, i, 0))],
            out_specs=[pl.BlockSpec((tr, c), lambda i, k, place_ref: (i, 0)),
                       pl.BlockSpec((None, tr, c), lambda i, k, place_ref: (k, i, 0))]),
        out_shape=[jax.ShapeDtypeStruct((r, c), F32), jax.ShapeDtypeStruct((N_CHIPS, r, c), wire_dtype)],
        compiler_params=_params(("parallel", "arbitrary")),
    )(place, blocked.reshape(N_CHIPS, 2, r, c), recv)


def _chip_sum(own_ref, r_ref):
    return ((own_ref[...] + r_ref[0].astype(F32)) + r_ref[1].astype(F32)) + r_ref[2].astype(F32)


def _reduce_chip_sum(own, recv, *, name):
    r, c = own.shape
    tr = _row_tile(r, c)

    def body(own_ref, r_ref, o_ref):
        o_ref[...] = _chip_sum(own_ref, r_ref)

    return pl.pallas_call(
        body, name=name, grid=(r // tr,),
        in_specs=[pl.BlockSpec((tr, c), lambda i: (i, 0)), pl.BlockSpec((N_CHIPS - 1, tr, c), lambda i: (0, i, 0))],
        out_specs=pl.BlockSpec((tr, c), lambda i: (i, 0)),
        out_shape=jax.ShapeDtypeStruct((r, c), F32),
        compiler_params=_params(("parallel",)),
    )(own, recv)


def _adamw_math(w, g, m, v):
    nm = ADAM_B1 * m + (1.0 - ADAM_B1) * g
    nv = ADAM_B2 * v + (1.0 - ADAM_B2) * (g * g)
    m_hat = nm / (1.0 - ADAM_B1 ** ADAM_STEP)
    v_hat = nv / (1.0 - ADAM_B2 ** ADAM_STEP)
    return -ADAM_LR * (m_hat / (jnp.sqrt(v_hat) + ADAM_EPS) + ADAM_WD * w), nm, nv


def _adamw(w, g, m, v, *, name):
    shape = w.shape
    C = shape[-1]
    R = math.prod(shape[:-1])
    tr = _row_tile(R, C)

    def body(w_ref, g_ref, m_ref, v_ref, d_ref, nm_ref, nv_ref):
        d_ref[...], nm_ref[...], nv_ref[...] = _adamw_math(w_ref[...], g_ref[...], m_ref[...], v_ref[...])

    spec = pl.BlockSpec((tr, C), lambda i: (i, 0))
    outs = pl.pallas_call(
        body, name=name, grid=(R // tr,),
        in_specs=[spec] * 4, out_specs=[spec] * 3,
        out_shape=[jax.ShapeDtypeStruct((R, C), F32)] * 3,
        compiler_params=_params(("parallel",)),
    )(*[a.reshape(R, C) for a in (w, g, m, v)])
    return tuple(o.reshape(shape) for o in outs)


def _reduce_adamw(own, recv, w, m, v, layer, prev, *, name):
    r, c = own.shape
    tr = _row_tile(r, c)
    n_prev = 0 if prev is None else len(prev)

    def body(own_ref, r_ref, w_ref, m_ref, v_ref, *rest):
        g_ref, d_ref, nm_ref, nv_ref = rest[n_prev:]
        g = _chip_sum(own_ref, r_ref)
        g_ref[...] = g
        d_ref[...], nm_ref[...], nv_ref[...] = _adamw_math(w_ref[...], g, m_ref[...], v_ref[...])

    slot = pl.BlockSpec((None, tr, c), lambda i: (layer, i, 0))
    return pl.pallas_call(
        body, name=name, grid=(r // tr,),
        in_specs=[pl.BlockSpec((tr, c), lambda i: (i, 0)), pl.BlockSpec((N_CHIPS - 1, tr, c), lambda i: (0, i, 0)),
                  slot, slot, slot] + [ANY] * n_prev,
        out_specs=[slot] * 4,
        out_shape=[jax.ShapeDtypeStruct((DEPTH, r, c), F32)] * 4,
        input_output_aliases={5 + k: k for k in range(n_prev)},
        compiler_params=_params(("parallel",)),
    )(own, recv, w, m, v, *(prev or ()))


REPLICATED = (("mix_norm", (D_MODEL,)), ("q_norm", (HEAD_DIM,)), ("k_norm", (HEAD_DIM,)), ("sinks", (N_Q_HEADS,)),
              ("sgu_norm", (SGU_WIDTH,)), ("w_s", (SGU_GROUPS, BLOCK, BLOCK)), ("b_s", (SGU_GROUPS, BLOCK)),
              ("ffn_norm", (D_MODEL,)), ("conv_b", (2 * D_FF,)))
TRANSPOSED = ("w_in", "w_up")
SHARDED = (("w_in", "rows"), ("w_oa", "cols"), ("w_ob", "cols"), ("w_out", "rows"), ("w_up", "rows"),
           ("conv_w", "blocks"), ("w_down", "rows"))
WEIGHT_ORDER = ("mix_norm", "w_in", "q_norm", "k_norm", "sinks", "sgu_norm", "w_s", "b_s", "w_oa", "w_ob", "w_out",
                "ffn_norm", "w_up", "conv_w", "conv_b", "w_down")
MIXER_WEIGHTS = ["w_in", "w_oa", "w_ob", "w_out"]
FFN_WEIGHTS = ["w_up", "conv_w", "w_down"]


def _small_layout():
    segs, off = {}, 0
    for l in range(DEPTH):
        for name, shape in REPLICATED:
            n = math.prod(shape)
            segs[(l, name)] = (off, n)
            off += n
    per_dev = -(-off // (N_DEV * SUBLANES * LANES)) * SUBLANES * LANES
    return segs, off, per_dev


def _pack_small(grads, loss_part):
    ssegs, total, per_dev = _small_layout()
    flat = jnp.concatenate([grads[l][name].reshape(-1) for (l, name) in ssegs] + [loss_part.reshape(1)])
    return jnp.pad(flat, (0, N_DEV * per_dev - total - 1)).reshape(N_DEV, per_dev // LANES, LANES)


def _unpack_small(gathered):
    ssegs, total, _ = _small_layout()
    flat = gathered.reshape(-1)
    shapes = dict(REPLICATED)
    small = {name: jnp.stack([flat[ssegs[(l, name)][0]:ssegs[(l, name)][0] + ssegs[(l, name)][1]].reshape(shapes[name])
                              for l in range(DEPTH)]) for name, _ in REPLICATED}
    return small, flat[total]


def kernel(x, mix_norm, w_in, q_norm, k_norm, sinks, sgu_norm, w_s, b_s, w_oa, w_ob, w_out, ffn_norm, w_up, conv_w, conv_b, w_down, loss_target, m_mix_norm, m_w_in, m_q_norm, m_k_norm, m_sinks, m_sgu_norm, m_w_s, m_b_s, m_w_oa, m_w_ob, m_w_out, m_ffn_norm, m_w_up, m_conv_w, m_conv_b, m_w_down, v_mix_norm, v_w_in, v_q_norm, v_k_norm, v_sinks, v_sgu_norm, v_w_s, v_b_s, v_w_oa, v_w_ob, v_w_out, v_ffn_norm, v_w_up, v_conv_w, v_conv_b, v_w_down):
    W = dict(mix_norm=mix_norm, w_in=w_in, q_norm=q_norm, k_norm=k_norm, sinks=sinks, sgu_norm=sgu_norm, w_s=w_s, b_s=b_s,
             w_oa=w_oa, w_ob=w_ob, w_out=w_out, ffn_norm=ffn_norm, w_up=w_up, conv_w=conv_w, conv_b=conv_b, w_down=w_down)
    M = dict(mix_norm=m_mix_norm, w_in=m_w_in, q_norm=m_q_norm, k_norm=m_k_norm, sinks=m_sinks, sgu_norm=m_sgu_norm,
             w_s=m_w_s, b_s=m_b_s, w_oa=m_w_oa, w_ob=m_w_ob, w_out=m_w_out, ffn_norm=m_ffn_norm, w_up=m_w_up,
             conv_w=m_conv_w, conv_b=m_conv_b, w_down=m_w_down)
    V = dict(mix_norm=v_mix_norm, w_in=v_w_in, q_norm=v_q_norm, k_norm=v_k_norm, sinks=v_sinks, sgu_norm=v_sgu_norm,
             w_s=v_w_s, b_s=v_b_s, w_oa=v_w_oa, w_ob=v_w_ob, w_out=v_w_out, ffn_norm=v_ffn_norm, w_up=v_w_up,
             conv_w=v_conv_w, conv_b=v_conv_b, w_down=v_w_down)
    n_seq, seq, d_model = x.shape
    tokens = n_seq * seq
    mx, my, mc = _my_place()
    place = jnp.stack([mc, 2 * mx + my]).astype(jnp.int32)
    half = N_DEV // 2
    kind_of = dict(SHARDED)
    for name in TRANSPOSED:
        W[name], M[name], V[name] = (jnp.swapaxes(t[name], 1, 2) for t in (W, M, V))

    gather_groups = [[(0, MIXER_WEIGHTS[0])], [(0, n) for n in MIXER_WEIGHTS[1:]], [(0, n) for n in FFN_WEIGHTS],
                     [(1, n) for n in MIXER_WEIGHTS], [(1, n) for n in FFN_WEIGHTS]]
    started, in_flight = {}, {}
    weights = []
    for l in range(DEPTH):
        w = {name: W[name][l] for name, _ in REPLICATED}
        w["cb_g"], w["cb_v"] = W["conv_b"][l][:D_FF], W["conv_b"][l][D_FF:]
        w["bias_full"] = jnp.repeat(W["b_s"][l].T, SGU_WIDTH // SGU_GROUPS, axis=1)
        weights.append(w)

    def gather_start(gi, after=()):
        shards = [W[name][l] for l, name in gather_groups[gi]]
        kinds = [kind_of[name] for _, name in gather_groups[gi]]
        shapes = [s.shape for s in shards]
        lands = _place_own(shards, kinds, [F32 if name == "conv_w" else BF16 for _, name in gather_groups[gi]],
                           name=f"gather_weights_own_{gi}", deps=after)
        send, recv, lands, token = _gather_start(lands, kinds, shapes, after, name=f"gather_weights_start_{gi}")
        started[gi] = dict(sems=(send, recv), lands=lands, kinds=kinds, shapes=shapes)
        return token

    def gather_forward(gi, after):
        st = started[gi]
        in_flight[gi] = _gather_forward(st["sems"][1], st["lands"], st["kinds"], st["shapes"], after,
                                        name=f"gather_weights_forward_{gi}")
        return in_flight[gi][3]

    def gather_finish(gi, after):
        st = started.pop(gi)
        fwd_send, fwd_recv, lands_g, _ = in_flight.pop(gi)
        whole = _gather_finish(st["sems"][0], st["sems"][1], fwd_send, fwd_recv, lands_g, st["kinds"], st["shapes"], after,
                               name=f"gather_weights_finish_{gi}")
        for (l, name), arr in zip(gather_groups[gi], whole):
            w = weights[l]
            if name in TRANSPOSED:
                w[name + "_t"] = arr
            elif name == "conv_w":
                w["cw_g"] = arr[:half].transpose(1, 0, 2).reshape(3, D_FF)
                w["cw_v"] = arr[half:].transpose(1, 0, 2).reshape(3, D_FF)
            else:
                w[name] = arr

    reduce_state, results = {}, {}
    wire = {"conv_w": F32, "small": F32}

    def reduce_begin(key, names, arrays):
        send, recv, srcs_, lands_, token = _exchange_start(arrays, _pair_plan, N_CHIPS, name=f"reduce_pair_start_{key}")
        reduce_state[key] = dict(names=names, pair=(send, recv, srcs_, lands_))
        return [token]

    def reduce_pair(key, after):
        st = reduce_state[key]
        send, recv, srcs_, lands_ = st.pop("pair")
        blocked_, from_sibling = _exchange_wait(send, recv, srcs_, lands_, _pair_plan, N_CHIPS, after,
                                                name=f"reduce_pair_wait_{key}")
        sums = [_reduce_pair_sum(b, r, place, wire.get(n if isinstance(n, str) else n[1], BF16),
                                 name=f"reduce_pair_sum_{key}_{i}")
                for i, (n, b, r) in enumerate(zip(st["names"], blocked_, from_sibling))]
        st["own"] = [s[0] for s in sums]
        *st["chip"], token = _exchange_start([s[1] for s in sums], _chip_plan, N_CHIPS - 1, name=f"reduce_chip_start_{key}")
        return [token]

    def reduce_end(key, after):
        st = reduce_state.pop(key)
        send, recv, srcs_, lands_ = st["chip"]
        _, from_chips = _exchange_wait(send, recv, srcs_, lands_, _chip_plan, N_CHIPS - 1, after,
                                       name=f"reduce_chip_wait_{key}")
        done = []
        for n, own, got in zip(st["names"], st["own"], from_chips):
            if n == "small":
                results["small"] = _reduce_chip_sum(own, got, name="reduce_chip_sum_small")
            else:
                l, name = n
                results[name] = _reduce_adamw(own, got, W[name], M[name], V[name], l, results.get(name),
                                              name=f"l{l}_reduce_adamw_{name}")
                done.append(results[name][0])
        return done

    def sched(point, l, carry, g=None):
        deps = []
        if point == "begin":
            token = ()
            for gi in range(len(gather_groups)):
                token = [gather_start(gi, token)]
            deps = token
        elif point == "fwd_start" and l == 0:
            gather_finish(0, gather_forward(0, carry))
        elif point == "fwd_att" and l == 0:
            gather_finish(1, gather_forward(1, carry))
            deps = [gather_forward(2, carry)]
        elif point == "fwd_mixer_done" and l == 0:
            gather_finish(2, carry)
        elif point == "fwd_conv" and l == 0:
            deps = [gather_forward(3, carry)]
        elif point == "fwd_start" and l == 1:
            gather_finish(3, carry)
        elif point == "fwd_att" and l == 1:
            deps = [gather_forward(4, carry)]
        elif point == "fwd_mixer_done" and l == 1:
            gather_finish(4, carry)
        elif point == "bwd_ffn_grads":
            conv_w = jnp.concatenate([g[k].reshape(3, half, W_UP_SHARD).transpose(1, 0, 2) for k in ("cw_g", "cw_v")])
            deps = reduce_begin(
                f"l{l}_ffn", [(l, "w_down"), (l, "w_up"), (l, "conv_w")],
                [g["w_down"].reshape(N_DEV, D_FF // N_DEV, D_MODEL),
                 g["w_up_t"].reshape(N_DEV, W_UP_SHARD, D_MODEL), conv_w])
        elif point == "bwd_merge":
            deps = reduce_pair(f"l{l}_ffn", carry)
        elif point == "bwd_out_grads":
            deps = reduce_begin(
                f"l{l}_out", [(l, "w_out"), (l, "w_oa"), (l, "w_ob")],
                [g["w_out"].reshape(N_DEV, D_MODEL // N_DEV, D_MODEL),
                 _disassemble((g["w_oa"],), LANES, _w_o_moves(), name=f"l{l}_split_dw_oa"),
                 _disassemble((g["w_ob"],), LANES, _w_o_moves(), name=f"l{l}_split_dw_ob")])
        elif point == "bwd_att":
            deps = reduce_pair(f"l{l}_out", carry)
        elif point == "bwd_w_in_grad":
            deps = reduce_begin(f"l{l}_in", [(l, "w_in")], [g["w_in_t"].reshape(N_DEV, W_IN_SHARD, D_MODEL)])
        elif point == "bwd_dh":
            deps = reduce_pair(f"l{l}_in", carry)
        return deps

    loss_part, dx, grads, last_deps = _local_step(x.reshape(tokens, d_model), loss_target.reshape(tokens, d_model),
                                                  weights, sched, n_seq=n_seq, seq=seq)
    for g in grads:
        g["conv_b"] = jnp.concatenate([g["cb_g"], g["cb_v"]])
    after = [dx, *last_deps, *reduce_begin("small", ["small"], [_pack_small(grads, loss_part)])]
    for key in [f"l{l}_{part}" for l in reversed(range(DEPTH)) for part in ("ffn", "out", "in")][:-1]:
        after = reduce_end(key, after)
    after = reduce_end("l0_in", after + reduce_pair("small", after))
    reduce_end("small", after)

    G, delta, new_m, new_v = {}, {}, {}, {}
    for name, _ in SHARDED:
        outs = [jnp.swapaxes(o, 1, 2) for o in results[name]] if name in TRANSPOSED else results[name]
        G[name], delta[name], new_m[name], new_v[name] = outs
    small, loss = _unpack_small(_gather([results["small"]], ["blocks"], name="gather_small_grads")[0])
    G.update(small)
    for name, _ in REPLICATED:
        delta[name], new_m[name], new_v[name] = _adamw(W[name], G[name], M[name], V[name], name=f"adamw_{name}")
    return (loss, dx.reshape(n_seq, seq, d_model), *[G[n] for n in WEIGHT_ORDER], *[delta[n] for n in WEIGHT_ORDER],
            *[new_m[n] for n in WEIGHT_ORDER], *[new_v[n] for n in WEIGHT_ORDER])
```

```python
import math

import jax
import jax.numpy as jnp
from jax import lax
from jax.experimental import pallas as pl
from jax.experimental.pallas import tpu as pltpu

F32 = jnp.float32
BF16 = jnp.bfloat16
ACT_DTYPE = BF16
MESH = pl.DeviceIdType.MESH

DEPTH = 2
D_MODEL = 1024
N_Q_HEADS = 8
HEAD_DIM = 64
ATT_WIDTH = 512
KV_WIDTH = 128
BLOCK = 128
SGU_WIDTH = 512
SGU_GROUPS = 8
IN_WIDTH = 3840
D_FF = 2816
NORM_EPS = 1e-6
NEG_INF = -1e30
ATT_SCALE = HEAD_DIM ** -0.5
ALIBI_SLOPES = tuple(2.0 ** (-(h + 1)) for h in range(N_Q_HEADS))
ADAM_LR, ADAM_B1, ADAM_B2, ADAM_EPS, ADAM_WD, ADAM_STEP = 0.001, 0.9, 0.999, 1e-08, 0.01, 10
N_DEV = 8
N_CHIPS = 4

QKV_WIDTH = ATT_WIDTH + 2 * KV_WIDTH
COL_SUV, COL_GA, COL_GB, COL_QKV = 0, 1024, 2048, 3072
W_IN_ROTATE = (1, IN_WIDTH // QKV_WIDTH)

LANES = 128
SUBLANES = 8
VMEM_LIMIT_V7X = 56 * 1024 * 1024
GELU_C = math.sqrt(2.0 / math.pi)
GELU_K = 0.044715
ANY = pl.BlockSpec(memory_space=pl.ANY)


def _params(sem=None):
    return pltpu.CompilerParams(dimension_semantics=sem, vmem_limit_bytes=VMEM_LIMIT_V7X)


def _sigmoid(x):
    return 1.0 / (1.0 + jnp.exp(-x))


def _gelu(x):
    th = jnp.tanh(GELU_C * (x + GELU_K * x * x * x))
    return 0.5 * x * (1.0 + th)


def _gelu_and_grad(x):
    x2 = x * x
    th = jnp.tanh(GELU_C * (x + GELU_K * x2 * x))
    g = 0.5 * x * (1.0 + th)
    dg = 0.5 * (1.0 + th) + 0.5 * x * (1.0 - th * th) * (GELU_C * (1.0 + 3.0 * GELU_K * x2))
    return g, dg


def _dot(a, b, dims):
    return lax.dot_general(a, b, (dims, ((), ())), preferred_element_type=F32)


def _dot_nn(a, b):
    return _dot(a, b, ((1,), (0,)))


def _dot_nt(a, b):
    return _dot(a, b, ((1,), (1,)))


def _dot_tn(a, b):
    return _dot(a, b, ((0,), (0,)))


def _lo_mask(shape):
    return lax.broadcasted_iota(jnp.int32, shape, len(shape) - 1) < (LANES // 2)


def _half_sums(x, lo):
    s_lo = jnp.sum(jnp.where(lo, x, 0.0), axis=-1, keepdims=True)
    s_all = jnp.sum(x, axis=-1, keepdims=True)
    return jnp.where(lo, s_lo, s_all - s_lo)


def _dup_half(x, half, lo):
    r = pltpu.roll(x, LANES // 2, axis=1)
    return jnp.where(lo, x, r) if half == 0 else jnp.where(lo, r, x)


def _with_deps(body, n_in, deps):
    k = len(deps)
    if not k:
        return body, [], ()

    def skipping(*refs):
        return body(*refs[:n_in], *refs[n_in + k:])

    return skipping, [ANY] * k, tuple(deps)


MM_VMEM_BUDGET = 40 * 1024 * 1024
MM_MAX_TILE = 1408
MM_MAX_TK = 4096
MM_STEP_BYTES = 1 << 20
MM_ROWS_SUB = 256


def _divisors(n, step, cap):
    return [d for d in range(step, min(n, cap) + 1, step) if n % d == 0] or [n]


def _mm_tiles(M, N, K, out_bytes, tm_divides, tn_divides):
    best = None
    for tm in _divisors(M, LANES, MM_MAX_TILE):
        for tn in _divisors(N, LANES, MM_MAX_TILE):
            if tm_divides % tm or tn_divides % tn:
                continue
            for tk in _divisors(K, 4 * LANES, MM_MAX_TK):
                vmem = 4 * (tm * tk + tk * tn) + 2 * tm * tn * out_bytes + (0 if tk == K else 4 * tm * tn)
                if vmem > MM_VMEM_BUDGET:
                    continue
                traffic = 2 * M * K * (N // tn) + 2 * K * N * (M // tm) + M * N * out_bytes
                cost = traffic + (K // tk - 1) * 8 * M * N + (M // tm) * (N // tn) * (K // tk) * MM_STEP_BYTES
                if best is None or cost < best[0]:
                    best = (cost, tm, tn, tk)
    assert best is not None, (M, N, K)
    return best[1:]


def _mm(a, b, *, mode, out_dtype, name, deps=(), b_rows=(0, None), rotate=None, out_rows=(0, None), out_prev=None):
    b_first, b_count = b_rows
    if mode == "nn":
        (M, K), N = a.shape, b.shape[1]
    elif mode == "nt":
        (M, K), N = a.shape, (b.shape[0] if b_count is None else b_count)
    else:
        (K, M), N = a.shape, b.shape[1]
    shift, period = rotate or (0, 1)
    assert period == 1 or mode == "nt"
    out_first, out_total = out_rows[0], (M if out_rows[1] is None else out_rows[1])
    tm, tn, tk = _mm_tiles(M, N, K, jnp.dtype(out_dtype).itemsize, math.gcd(M, out_first),
                           math.gcd(N // period, b_first if mode == "nt" else 0))
    gm, gn, gk = M // tm, N // tn, K // tk

    def turned(j):
        per = N // period // tn
        return ((j // per + shift) % period) * per + j % per if period > 1 else j

    if mode == "nn":
        a_spec = pl.BlockSpec((tm, tk), lambda i, j, k: (i, k))
        b_spec = pl.BlockSpec((tk, tn), lambda i, j, k: (k + b_first // tk, j))
        contract = ((1,), (0,))
    elif mode == "nt":
        a_spec = pl.BlockSpec((tm, tk), lambda i, j, k: (i, k))
        b_spec = pl.BlockSpec((tn, tk), lambda i, j, k: (turned(j) + b_first // tn, k))
        contract = ((1,), (1,))
    else:
        a_spec = pl.BlockSpec((tk, tm), lambda i, j, k: (k, i))
        b_spec = pl.BlockSpec((tk, tn), lambda i, j, k: (k, j))
        contract = ((0,), (0,))
    o_spec = pl.BlockSpec((tm, tn), lambda i, j, k: (i + out_first // tm, j))
    assert b_first % (tk if mode == "nn" else tn) == 0 and out_first % tm == 0, (name, tm, tn, tk)
    n_prev = 0 if out_prev is None else 1

    def body(a_ref, b_ref, *rest):
        o_ref = rest[n_prev]
        part = _dot(a_ref[...].astype(BF16), b_ref[...].astype(BF16), contract)
        if gk == 1:
            o_ref[...] = part.astype(out_dtype)
            return
        acc_ref = rest[n_prev + 1]
        k = pl.program_id(2)

        @pl.when(k == 0)
        def _():
            acc_ref[...] = part

        @pl.when(k > 0)
        def _():
            acc_ref[...] += part

        @pl.when(k == gk - 1)
        def _():
            o_ref[...] = acc_ref[...].astype(out_dtype)

    body, dep_specs, dep_args = _with_deps(body, 2 + n_prev, deps)
    return pl.pallas_call(
        body,
        name=name,
        grid=(gm, gn, gk),
        in_specs=[a_spec, b_spec] + [ANY] * n_prev + dep_specs,
        out_specs=o_spec,
        out_shape=jax.ShapeDtypeStruct((out_total, N), out_dtype),
        input_output_aliases={2: 0} if n_prev else {},
        scratch_shapes=[] if gk == 1 else [pltpu.VMEM((tm, tn), F32)],
        compiler_params=_params(("parallel", "parallel", "arbitrary")),
    )(a, b, *([out_prev] if n_prev else []), *dep_args)


def _mm_tn_parts(parts, at, b, *, name):
    K, N = b.shape
    n = len(parts)
    tm = math.gcd(*[p.shape[1] for p in parts], *at)
    tiles = [p.shape[1] // tm for p in parts]
    first = [sum(tiles[:p]) for p in range(n)]

    def mine(i, p):
        return jnp.logical_and(i >= first[p], i < first[p] + tiles[p])

    def out_tile(i):
        t = 0
        for p in range(n):
            t = jnp.where(mine(i, p), at[p] // tm + i - first[p], t)
        return t

    def body(*refs):
        a_refs, b_ref, o_ref = refs[:n], refs[n], refs[n + 1]
        for p in range(n):
            @pl.when(mine(pl.program_id(0), p))
            def _(p=p):
                o_ref[...] = _dot_tn(a_refs[p][...], b_ref[...])

    return pl.pallas_call(
        body, name=name, grid=(sum(tiles),),
        in_specs=[pl.BlockSpec((K, tm), lambda i, p=p: (0, jnp.clip(i - first[p], 0, tiles[p] - 1))) for p in range(n)]
        + [pl.BlockSpec((K, N), lambda i: (0, 0), pipeline_mode=pl.Buffered(1))],
        out_specs=pl.BlockSpec((tm, N), lambda i: (out_tile(i), 0)),
        out_shape=jax.ShapeDtypeStruct((sum(p.shape[1] for p in parts), N), F32),
        compiler_params=_params(("arbitrary",)),
    )(*parts, b)


def _mm_rows(a, b, *, mode, fn, out_dtypes, rows=(), vecs=(), reduce=False, name, deps=(), b_rows=(0, None), a_at=None):
    parts = a if a_at is not None else (a,)
    starts = a_at if a_at is not None else (0,)
    n_parts = len(parts)
    M, K = parts[0].shape[0], sum(p.shape[1] for p in parts)
    b_first, b_count = b_rows[0], (b.shape[0] if b_rows[1] is None else b_rows[1])
    N = b.shape[1] if mode == "nn" else b_count
    contract = ((1,), (0,)) if mode == "nn" else ((1,), (1,))
    n_rows, n_vecs, n_out = len(rows), len(vecs), len(out_dtypes)
    out_bytes = sum(jnp.dtype(d).itemsize for d in out_dtypes)
    tm = max(t for t in _divisors(M, LANES, MM_MAX_TILE)
             if 4 * t * K + 2 * K * N + 2 * t * N * (4 * n_rows + out_bytes) <= MM_VMEM_BUDGET)
    assert b_first % b_count == 0 and (a_at is None or mode == "nn")
    sub = min(tm, MM_ROWS_SUB)

    def body(*refs):
        a_refs, b_ref, rest = refs[:n_parts], refs[n_parts], refs[n_parts + 1:]
        row_refs, vec_refs = rest[:n_rows], rest[n_rows:n_rows + n_vecs]
        out_refs = rest[n_rows + n_vecs:]
        vecs_v = [v[...] for v in vec_refs]
        partial = None
        for q in range(0, tm, sub):
            rq = slice(q, q + sub)
            if a_at is None:
                acc = _dot(a_refs[0][rq, :], b_ref[...], contract)
            else:
                acc = sum(_dot(r[rq, :], b_ref[at:at + r.shape[1], :], contract) for r, at in zip(a_refs, starts))
            res = fn(acc, *[r[rq, :] for r in row_refs], *vecs_v)
            for o_ref, val in zip(out_refs[:n_out], res):
                o_ref[rq, :] = val.astype(o_ref.dtype)
            if reduce:
                partial = res[n_out] if partial is None else partial + res[n_out]
        if reduce:
            @pl.when(pl.program_id(0) == 0)
            def _():
                out_refs[n_out][...] = partial

            @pl.when(pl.program_id(0) > 0)
            def _():
                out_refs[n_out][...] += partial

    row = pl.BlockSpec((tm, N), lambda i: (i, 0))
    vec = pl.BlockSpec((1, N), lambda i: (0, 0))
    body, dep_specs, dep_args = _with_deps(body, n_parts + 1 + n_rows + n_vecs, deps)
    return pl.pallas_call(
        body, name=name, grid=(M // tm,),
        in_specs=[pl.BlockSpec((tm, p.shape[1]), lambda i: (i, 0)) for p in parts]
        + [pl.BlockSpec((b_count, b.shape[1]), lambda i: (b_first // b_count, 0), pipeline_mode=pl.Buffered(1))]
        + [row] * n_rows + [vec] * n_vecs + dep_specs,
        out_specs=[row] * n_out + [vec] * reduce,
        out_shape=[jax.ShapeDtypeStruct((M, N), d) for d in out_dtypes] + [jax.ShapeDtypeStruct((1, N), F32)] * reduce,
        compiler_params=_params(("arbitrary",)),
    )(*parts, b, *rows, *[v.reshape(1, N) for v in vecs], *dep_args)


def _rms(x, gain):
    return x * lax.rsqrt(jnp.mean(x * x, axis=-1, keepdims=True) + NORM_EPS) * gain


def _residual_then_norm(acc, x, gain):
    x_out = x + acc
    return x_out, _rms(x_out, gain)


def _residual_then_loss(acc, x, target):
    err = (x + acc) - target
    dy = err * (1.0 / D_MODEL)
    return dy, dy, jnp.sum(err * err, axis=0, keepdims=True) * (0.5 / D_MODEL)


def _rms_bwd_rows(dh, x, dres, gain):
    r = lax.rsqrt(jnp.mean(x * x, axis=-1, keepdims=True) + NORM_EPS)
    xh = x * r
    dxh = dh * gain
    dx = dres + r * (dxh - xh * jnp.mean(dxh * xh, axis=-1, keepdims=True))
    return dx, dx, jnp.sum(dh * xh, axis=0, keepdims=True)


def _rms_fwd(x, gain, *, name, tm=512, deps=()):
    T, D = x.shape

    def body(x_ref, g_ref, h_ref):
        xv = x_ref[...]
        r = lax.rsqrt(jnp.mean(xv * xv, axis=-1, keepdims=True) + NORM_EPS)
        h_ref[...] = (xv * r * g_ref[...]).astype(BF16)

    body, dep_specs, dep_args = _with_deps(body, 2, deps)
    return pl.pallas_call(
        body, name=name, grid=(T // tm,),
        in_specs=[pl.BlockSpec((tm, D), lambda i: (i, 0)), pl.BlockSpec((1, D), lambda i: (0, 0))] + dep_specs,
        out_specs=pl.BlockSpec((tm, D), lambda i: (i, 0)),
        out_shape=jax.ShapeDtypeStruct((T, D), BF16),
        compiler_params=_params(("parallel",)),
    )(x, gain.reshape(1, D), *dep_args)


def _head_norm(x, gain2, lo):
    ms = _half_sums(x * x, lo) * (1.0 / HEAD_DIM)
    r = lax.rsqrt(ms + NORM_EPS)
    xh = x * r
    return xh * gain2, xh, r


def _head_norm_bwd(xh, r, gain2, dy, lo):
    dxh = dy * gain2
    dx = r * (dxh - xh * (_half_sums(dxh * xh, lo) * (1.0 / HEAD_DIM)))
    return dx, dy * xh


Q_GROUP = N_Q_HEADS // 2
GROUP_ROWS = Q_GROUP * BLOCK
ATT_SCRATCH = (pltpu.VMEM((2, 2, GROUP_ROWS, BLOCK), F32), pltpu.VMEM((2, GROUP_ROWS, 1), F32))


def _att_consts(sink_ref, bias_ref, sinkcol_ref):
    row = lax.broadcasted_iota(jnp.int32, (GROUP_ROWS, BLOCK), 0)
    kj = lax.broadcasted_iota(jnp.int32, (GROUP_ROWS, BLOCK), 1)
    head = row // BLOCK
    head_col = lax.broadcasted_iota(jnp.int32, (GROUP_ROWS, 1), 0) // BLOCK
    d_cur = (row % BLOCK) - kj
    d_prev = d_cur + BLOCK
    for kv in range(2):
        slope = jnp.zeros((GROUP_ROWS, BLOCK), F32)
        sink = jnp.zeros((GROUP_ROWS, 1), F32)
        for r in range(Q_GROUP):
            slope = jnp.where(head == r, ALIBI_SLOPES[Q_GROUP * kv + r], slope)
            sink = jnp.where(head_col == r, sink_ref[Q_GROUP * kv + r], sink)
        bias_ref[kv, 0] = jnp.where(d_cur >= 0, -slope * d_cur.astype(F32), NEG_INF)
        bias_ref[kv, 1] = jnp.where(d_prev < BLOCK, -slope * d_prev.astype(F32), NEG_INF)
        sinkcol_ref[kv] = sink


def _stack_heads(t0, t1, lo):
    z = jnp.zeros_like(t0)
    return jnp.concatenate([jnp.where(lo, t0, z), jnp.where(lo, z, t0), jnp.where(lo, t1, z), jnp.where(lo, z, t1)], axis=0)


def _unstack_heads(x4, lo):
    return (jnp.where(lo, x4[0:BLOCK], x4[BLOCK:2 * BLOCK]), jnp.where(lo, x4[2 * BLOCK:3 * BLOCK], x4[3 * BLOCK:]))


def _att_probs(q4, k2c, k2p, bias_c, bias_p, sink, has_prev):
    s_c = _dot_nt(q4, k2c) * ATT_SCALE + bias_c
    s_p = jnp.where(has_prev, _dot_nt(q4, k2p) * ATT_SCALE + bias_p, NEG_INF)
    m = jnp.maximum(jnp.max(jnp.maximum(s_c, s_p), axis=-1, keepdims=True), sink)
    e_c = jnp.exp(s_c - m)
    e_p = jnp.exp(s_p - m)
    e_s = jnp.exp(sink - m)
    inv = 1.0 / (jnp.sum(e_c + e_p, axis=-1, keepdims=True) + e_s)
    return e_c * inv, e_p * inv, e_s * inv


def _attention_fwd(proj, q_gain, k_gain, sinks, *, n_seq, seq, name):
    T = n_seq * seq
    nb = seq // BLOCK
    qcol, kvcol = COL_QKV // ATT_WIDTH, (COL_QKV + ATT_WIDTH) // (2 * KV_WIDTH)

    def body(q_ref, kv_ref, qg_ref, kg_ref, sink_ref, y_ref, bias_ref, sinkcol_ref):
        lo = _lo_mask((BLOCK, LANES))
        qg, kg = qg_ref[...], kg_ref[...]
        _att_consts(sink_ref, bias_ref, sinkcol_ref)

        def block(i, carry):
            r0 = pl.multiple_of(i * BLOCK, BLOCK)
            rp = pl.multiple_of(jnp.maximum(i - 1, 0) * BLOCK, BLOCK)
            has_prev = i > 0
            kn_c = _head_norm(kv_ref[pl.ds(r0, BLOCK), 0:KV_WIDTH].astype(F32), kg, lo)[0].astype(BF16)
            kn_p = _head_norm(kv_ref[pl.ds(rp, BLOCK), 0:KV_WIDTH].astype(F32), kg, lo)[0].astype(BF16)
            v_c = kv_ref[pl.ds(r0, BLOCK), KV_WIDTH:2 * KV_WIDTH].astype(BF16)
            v_p = kv_ref[pl.ds(rp, BLOCK), KV_WIDTH:2 * KV_WIDTH].astype(BF16)
            for kv in range(2):
                k2c, k2p = _dup_half(kn_c, kv, lo), _dup_half(kn_p, kv, lo)
                v2c, v2p = _dup_half(v_c, kv, lo), _dup_half(v_p, kv, lo)
                cols = [slice((2 * kv + t) * LANES, (2 * kv + t + 1) * LANES) for t in range(2)]
                qn = [_head_norm(q_ref[pl.ds(r0, BLOCK), c].astype(F32), qg, lo)[0] for c in cols]
                q4 = _stack_heads(qn[0], qn[1], lo).astype(BF16)
                p_c, p_p, _ = _att_probs(q4, k2c, k2p, bias_ref[kv, 0], bias_ref[kv, 1], sinkcol_ref[kv], has_prev)
                o4 = _dot_nn(p_c.astype(BF16), v2c) + _dot_nn(p_p.astype(BF16), v2p)
                for c, out in zip(cols, _unstack_heads(o4, lo)):
                    y_ref[pl.ds(r0, BLOCK), c] = out.astype(BF16)
            return carry

        lax.fori_loop(0, nb, block, 0)

    vec = pl.BlockSpec((1, LANES), lambda b: (0, 0))
    return pl.pallas_call(
        body, name=name, grid=(n_seq,),
        in_specs=[pl.BlockSpec((seq, ATT_WIDTH), lambda b: (b, qcol)),
                  pl.BlockSpec((seq, 2 * KV_WIDTH), lambda b: (b, kvcol)),
                  vec, vec, pl.BlockSpec(memory_space=pltpu.SMEM)],
        out_specs=pl.BlockSpec((seq, ATT_WIDTH), lambda b: (b, 0)),
        out_shape=jax.ShapeDtypeStruct((T, ATT_WIDTH), BF16),
        scratch_shapes=list(ATT_SCRATCH),
        compiler_params=_params(("parallel",)),
    )(proj, proj, jnp.tile(q_gain, 2).reshape(1, LANES), jnp.tile(k_gain, 2).reshape(1, LANES), sinks)


def _attention_bwd(proj, dy, q_gain, k_gain, sinks, *, n_seq, seq, name, deps=()):
    T = n_seq * seq
    nb = seq // BLOCK
    qcol, kvcol = COL_QKV // ATT_WIDTH, (COL_QKV + ATT_WIDTH) // (2 * KV_WIDTH)

    def body(q_ref, kv_ref, dy_ref, qg_ref, kg_ref, sink_ref, dqkv_ref, dqg_ref, dkg_ref, dsink_ref,
             dkn_acc, dv_acc, qg_acc, kg_acc, sink_acc, bias_ref, sinkcol_ref):
        lo = _lo_mask((BLOCK, LANES))
        qg, kg = qg_ref[...], kg_ref[...]
        _att_consts(sink_ref, bias_ref, sinkcol_ref)
        first = pl.program_id(0) == 0

        @pl.when(first)
        def _():
            qg_acc[...] = jnp.zeros_like(qg_acc)
            kg_acc[...] = jnp.zeros_like(kg_acc)
            sink_acc[...] = jnp.zeros_like(sink_acc)

        dkn_acc[...] = jnp.zeros_like(dkn_acc)
        dv_acc[...] = jnp.zeros_like(dv_acc)

        def block(i, carry):
            r0 = pl.multiple_of(i * BLOCK, BLOCK)
            rp = pl.multiple_of(jnp.maximum(i - 1, 0) * BLOCK, BLOCK)
            has_prev = i > 0
            kn_c = _head_norm(kv_ref[pl.ds(r0, BLOCK), 0:KV_WIDTH].astype(F32), kg, lo)[0].astype(BF16)
            kn_p = _head_norm(kv_ref[pl.ds(rp, BLOCK), 0:KV_WIDTH].astype(F32), kg, lo)[0].astype(BF16)
            v_c = kv_ref[pl.ds(r0, BLOCK), KV_WIDTH:2 * KV_WIDTH].astype(BF16)
            v_p = kv_ref[pl.ds(rp, BLOCK), KV_WIDTH:2 * KV_WIDTH].astype(BF16)
            dk_c, dk_p, dv_c, dv_p = [], [], [], []
            for kv in range(2):
                k2c, k2p = _dup_half(kn_c, kv, lo), _dup_half(kn_p, kv, lo)
                v2c, v2p = _dup_half(v_c, kv, lo), _dup_half(v_p, kv, lo)
                cols = [slice((2 * kv + t) * LANES, (2 * kv + t + 1) * LANES) for t in range(2)]
                normed = [_head_norm(q_ref[pl.ds(r0, BLOCK), c].astype(F32), qg, lo) for c in cols]
                q4 = _stack_heads(normed[0][0], normed[1][0], lo).astype(BF16)
                do4 = _stack_heads(dy_ref[pl.ds(r0, BLOCK), cols[0]], dy_ref[pl.ds(r0, BLOCK), cols[1]], lo)
                p_c, p_p, p_s = _att_probs(q4, k2c, k2p, bias_ref[kv, 0], bias_ref[kv, 1], sinkcol_ref[kv], has_prev)
                dp_c = _dot_nt(do4, v2c)
                dp_p = _dot_nt(do4, v2p)
                delta = jnp.sum(p_c * dp_c + p_p * dp_p, axis=-1, keepdims=True)
                ds_c = (p_c * (dp_c - delta)).astype(BF16)
                ds_p = (p_p * (dp_p - delta)).astype(BF16)
                sink_acc[kv] += -(p_s * delta)
                dq4 = (_dot_nn(ds_c, k2c) + _dot_nn(ds_p, k2p)) * ATT_SCALE
                for c, (_, qh, qr), dqn in zip(cols, normed, _unstack_heads(dq4, lo)):
                    dq, dg = _head_norm_bwd(qh, qr, qg, dqn, lo)
                    dqkv_ref[pl.ds(r0, BLOCK), c] = dq.astype(BF16)
                    qg_acc[...] += dg
                dk_c.append(_dot_tn(ds_c, q4))
                dk_p.append(_dot_tn(ds_p, q4))
                dv_c.append(_dot_tn(p_c.astype(BF16), do4))
                dv_p.append(_dot_tn(p_p.astype(BF16), do4))

            def fold(parts):
                a = parts[0] + pltpu.roll(parts[0], LANES // 2, axis=1)
                b = parts[1] + pltpu.roll(parts[1], LANES // 2, axis=1)
                return jnp.where(lo, a, b)

            dkn_acc[pl.ds(r0, BLOCK), :] += fold(dk_c) * ATT_SCALE
            dkn_acc[pl.ds(rp, BLOCK), :] += fold(dk_p) * ATT_SCALE
            dv_acc[pl.ds(r0, BLOCK), :] += fold(dv_c)
            dv_acc[pl.ds(rp, BLOCK), :] += fold(dv_p)
            return carry

        lax.fori_loop(0, nb, block, 0)

        def finish(i, carry):
            r0 = pl.multiple_of(i * BLOCK, BLOCK)
            _, kh, kr = _head_norm(kv_ref[pl.ds(r0, BLOCK), 0:KV_WIDTH].astype(F32), kg, lo)
            dk, dg = _head_norm_bwd(kh, kr, kg, dkn_acc[pl.ds(r0, BLOCK), :], lo)
            dqkv_ref[pl.ds(r0, BLOCK), ATT_WIDTH:ATT_WIDTH + KV_WIDTH] = dk.astype(BF16)
            dqkv_ref[pl.ds(r0, BLOCK), ATT_WIDTH + KV_WIDTH:QKV_WIDTH] = dv_acc[pl.ds(r0, BLOCK), :].astype(BF16)
            kg_acc[...] += dg
            return carry

        lax.fori_loop(0, nb, finish, 0)

        @pl.when(pl.program_id(0) == n_seq - 1)
        def _():
            dqg_ref[...] = jnp.sum(qg_acc[...], axis=0, keepdims=True)
            dkg_ref[...] = jnp.sum(kg_acc[...], axis=0, keepdims=True)
            lane = lax.broadcasted_iota(jnp.int32, (1, LANES), 1)
            dsink = jnp.zeros((1, LANES), F32)
            for kv in range(2):
                for r in range(Q_GROUP):
                    total = jnp.sum(sink_acc[kv, r * BLOCK:(r + 1) * BLOCK, :], axis=0, keepdims=True)
                    dsink = jnp.where(lane == Q_GROUP * kv + r, total, dsink)
            dsink_ref[...] = dsink

    vec = pl.BlockSpec((1, LANES), lambda b: (0, 0))
    acc = pltpu.VMEM((BLOCK, LANES), F32)
    body, dep_specs, dep_args = _with_deps(body, 6, deps)
    dqkv, dqg, dkg, dsink = pl.pallas_call(
        body, name=name, grid=(n_seq,),
        in_specs=[pl.BlockSpec((seq, ATT_WIDTH), lambda b: (b, qcol)),
                  pl.BlockSpec((seq, 2 * KV_WIDTH), lambda b: (b, kvcol)),
                  pl.BlockSpec((seq, ATT_WIDTH), lambda b: (b, 0)),
                  vec, vec, pl.BlockSpec(memory_space=pltpu.SMEM)] + dep_specs,
        out_specs=[pl.BlockSpec((seq, QKV_WIDTH), lambda b: (b, 0)), vec, vec, vec],
        out_shape=[jax.ShapeDtypeStruct((T, QKV_WIDTH), BF16)] + [jax.ShapeDtypeStruct((1, LANES), F32)] * 3,
        scratch_shapes=[pltpu.VMEM((seq, KV_WIDTH), F32), pltpu.VMEM((seq, KV_WIDTH), F32), acc, acc,
                        pltpu.VMEM((2, GROUP_ROWS, 1), F32), *ATT_SCRATCH],
        compiler_params=_params(("arbitrary",)),
    )(proj, proj, dy, jnp.tile(q_gain, 2).reshape(1, LANES), jnp.tile(k_gain, 2).reshape(1, LANES), sinks, *dep_args)
    half = LANES // 2
    return dqkv, dqg[0, :half] + dqg[0, half:], dkg[0, :half] + dkg[0, half:], dsink[0, :N_Q_HEADS]


def _sgu_weights(w_ref):
    r = lax.broadcasted_iota(jnp.int32, (BLOCK, BLOCK), 0)
    c = lax.broadcasted_iota(jnp.int32, (BLOCK, BLOCK), 1)
    return [jnp.where(r >= c, w_ref[g], 0.0).astype(BF16) for g in range(SGU_GROUPS)]


def _sgu_fwd(proj, gain, w_s, bias_full, *, n_seq, seq, name):
    T = n_seq * seq
    nc = seq // BLOCK

    def body(suv_ref, g_ref, w_ref, b_ref, y_ref):
        lo = _lo_mask((BLOCK, LANES))
        wm = _sgu_weights(w_ref)
        gain_v = g_ref[...]

        def chunk(c, carry):
            r0 = pl.multiple_of(c * BLOCK, BLOCK)
            gv = _gelu(suv_ref[pl.ds(r0, BLOCK), SGU_WIDTH:2 * SGU_WIDTH].astype(F32))
            r = lax.rsqrt(jnp.mean(gv * gv, axis=-1, keepdims=True) + NORM_EPS)
            vn = (gv * r * gain_v).astype(BF16)
            for p in range(SGU_WIDTH // LANES):
                cols = slice(p * LANES, (p + 1) * LANES)
                vp = vn[:, cols]
                mixed = jnp.where(lo, _dot_nn(wm[2 * p], vp), _dot_nn(wm[2 * p + 1], vp)) + b_ref[:, cols]
                u = _gelu(suv_ref[pl.ds(r0, BLOCK), cols].astype(F32))
                y_ref[pl.ds(r0, BLOCK), cols] = (u * mixed).astype(BF16)
            return carry

        lax.fori_loop(0, nc, chunk, 0)

    return pl.pallas_call(
        body, name=name, grid=(n_seq,),
        in_specs=[pl.BlockSpec((seq, 2 * SGU_WIDTH), lambda b: (b, COL_SUV // (2 * SGU_WIDTH))),
                  pl.BlockSpec((1, SGU_WIDTH), lambda b: (0, 0)),
                  pl.BlockSpec((SGU_GROUPS, BLOCK, BLOCK), lambda b: (0, 0, 0)),
                  pl.BlockSpec((BLOCK, SGU_WIDTH), lambda b: (0, 0))],
        out_specs=pl.BlockSpec((seq, SGU_WIDTH), lambda b: (b, 0)),
        out_shape=jax.ShapeDtypeStruct((T, SGU_WIDTH), BF16),
        compiler_params=_params(("parallel",)),
    )(proj, gain.reshape(1, SGU_WIDTH), w_s, bias_full)


def _sgu_bwd(proj, dy, gain, w_s, bias_full, *, n_seq, seq, name, deps=()):
    T = n_seq * seq
    nc = seq // BLOCK
    n_tiles = SGU_WIDTH // LANES

    def body(suv_ref, dy_ref, g_ref, w_ref, b_ref, dsuv_ref, dg_ref, dw_ref, db_ref, dg_acc, dw_acc, db_acc):
        lo = _lo_mask((BLOCK, LANES))
        hi = jnp.logical_not(lo)
        wm = _sgu_weights(w_ref)
        wmt = [jnp.where(lax.broadcasted_iota(jnp.int32, (BLOCK, BLOCK), 1) >= lax.broadcasted_iota(jnp.int32, (BLOCK, BLOCK), 0),
                         w_ref[g].T, 0.0).astype(BF16) for g in range(SGU_GROUPS)]
        gain_v = g_ref[...]

        @pl.when(pl.program_id(0) == 0)
        def _():
            dg_acc[...] = jnp.zeros_like(dg_acc)
            dw_acc[...] = jnp.zeros_like(dw_acc)
            db_acc[...] = jnp.zeros_like(db_acc)

        def chunk(c, carry):
            r0 = pl.multiple_of(c * BLOCK, BLOCK)
            gv, dgelu_v = _gelu_and_grad(suv_ref[pl.ds(r0, BLOCK), SGU_WIDTH:2 * SGU_WIDTH].astype(F32))
            r = lax.rsqrt(jnp.mean(gv * gv, axis=-1, keepdims=True) + NORM_EPS)
            vh = gv * r
            vn = (vh * gain_v).astype(BF16)
            dvn_tiles = []
            for p in range(n_tiles):
                cols = slice(p * LANES, (p + 1) * LANES)
                vp = vn[:, cols]
                mixed = jnp.where(lo, _dot_nn(wm[2 * p], vp), _dot_nn(wm[2 * p + 1], vp)) + b_ref[:, cols]
                u, dgelu_u = _gelu_and_grad(suv_ref[pl.ds(r0, BLOCK), cols].astype(F32))
                dyv = dy_ref[pl.ds(r0, BLOCK), cols]
                dsuv_ref[pl.ds(r0, BLOCK), cols] = (dyv * mixed * dgelu_u).astype(BF16)
                dm = dyv * u
                db_acc[:, cols] += dm
                dm_bf = dm.astype(BF16)
                dvn_tiles.append(jnp.where(lo, _dot_nn(wmt[2 * p], dm_bf), _dot_nn(wmt[2 * p + 1], dm_bf)))
                dw_acc[2 * p] += _dot_nt(jnp.where(lo, dm, 0.0).astype(BF16), vp)
                dw_acc[2 * p + 1] += _dot_nt(jnp.where(hi, dm, 0.0).astype(BF16), vp)
            dvn = jnp.concatenate(dvn_tiles, axis=1)
            dg_acc[...] += dvn * vh
            dvh = dvn * gain_v
            dgv = r * (dvh - vh * jnp.mean(dvh * vh, axis=-1, keepdims=True))
            dsuv_ref[pl.ds(r0, BLOCK), SGU_WIDTH:2 * SGU_WIDTH] = (dgv * dgelu_v).astype(BF16)
            return carry

        lax.fori_loop(0, nc, chunk, 0)

        @pl.when(pl.program_id(0) == n_seq - 1)
        def _():
            dg_ref[...] = jnp.sum(dg_acc[...], axis=0, keepdims=True)
            r = lax.broadcasted_iota(jnp.int32, (BLOCK, BLOCK), 0)
            c = lax.broadcasted_iota(jnp.int32, (BLOCK, BLOCK), 1)
            for g in range(SGU_GROUPS):
                dw_ref[g] = jnp.where(r >= c, dw_acc[g], 0.0)
            lane = lax.broadcasted_iota(jnp.int32, (BLOCK, LANES), 1)
            out = jnp.zeros((BLOCK, LANES), F32)
            for p in range(n_tiles):
                tile = db_acc[:, p * LANES:(p + 1) * LANES]
                s_lo = jnp.sum(jnp.where(lo, tile, 0.0), axis=-1, keepdims=True)
                s_hi = jnp.sum(jnp.where(hi, tile, 0.0), axis=-1, keepdims=True)
                out = jnp.where(lane == 2 * p, s_lo, out)
                out = jnp.where(lane == 2 * p + 1, s_hi, out)
            db_ref[...] = out

    body, dep_specs, dep_args = _with_deps(body, 5, deps)
    dsuv, dg, dw, db = pl.pallas_call(
        body, name=name, grid=(n_seq,),
        in_specs=[pl.BlockSpec((seq, 2 * SGU_WIDTH), lambda b: (b, COL_SUV // (2 * SGU_WIDTH))),
                  pl.BlockSpec((seq, SGU_WIDTH), lambda b: (b, 0)),
                  pl.BlockSpec((1, SGU_WIDTH), lambda b: (0, 0)),
                  pl.BlockSpec((SGU_GROUPS, BLOCK, BLOCK), lambda b: (0, 0, 0)),
                  pl.BlockSpec((BLOCK, SGU_WIDTH), lambda b: (0, 0))] + dep_specs,
        out_specs=[pl.BlockSpec((seq, 2 * SGU_WIDTH), lambda b: (b, 0)),
                   pl.BlockSpec((1, SGU_WIDTH), lambda b: (0, 0)),
                   pl.BlockSpec((SGU_GROUPS, BLOCK, BLOCK), lambda b: (0, 0, 0)),
                   pl.BlockSpec((BLOCK, LANES), lambda b: (0, 0))],
        out_shape=[jax.ShapeDtypeStruct((T, 2 * SGU_WIDTH), BF16), jax.ShapeDtypeStruct((1, SGU_WIDTH), F32),
                   jax.ShapeDtypeStruct((SGU_GROUPS, BLOCK, BLOCK), F32), jax.ShapeDtypeStruct((BLOCK, LANES), F32)],
        scratch_shapes=[pltpu.VMEM((BLOCK, SGU_WIDTH), F32), pltpu.VMEM((SGU_GROUPS, BLOCK, BLOCK), F32),
                        pltpu.VMEM((BLOCK, SGU_WIDTH), F32)],
        compiler_params=_params(("arbitrary",)),
    )(proj, dy, gain.reshape(1, SGU_WIDTH), w_s, bias_full, *dep_args)
    return dsuv, dg.reshape(SGU_WIDTH), dw, db[:, :SGU_GROUPS].T


def _merge_fwd(y_att, y_sgu, w_oa, w_ob, proj, *, name, tm=1024, tn=512, deps=()):
    T = y_att.shape[0]

    def body(ya_ref, ys_ref, wa_ref, wb_ref, ga_ref, gb_ref, o_ref):
        pa = _dot_nn(ya_ref[...], wa_ref[...])
        pb = _dot_nn(ys_ref[...], wb_ref[...])
        o_ref[...] = (_sigmoid(ga_ref[...].astype(F32)) * pa + _sigmoid(gb_ref[...].astype(F32)) * pb).astype(BF16)

    act = pl.BlockSpec((tm, ATT_WIDTH), lambda i, j: (i, 0))
    wgt = pl.BlockSpec((ATT_WIDTH, tn), lambda i, j: (0, j))
    body, dep_specs, dep_args = _with_deps(body, 6, deps)
    return pl.pallas_call(
        body, name=name, grid=(T // tm, D_MODEL // tn),
        in_specs=[act, act, wgt, wgt,
                  pl.BlockSpec((tm, tn), lambda i, j: (i, j + COL_GA // tn)),
                  pl.BlockSpec((tm, tn), lambda i, j: (i, j + COL_GB // tn))] + dep_specs,
        out_specs=pl.BlockSpec((tm, tn), lambda i, j: (i, j)),
        out_shape=jax.ShapeDtypeStruct((T, D_MODEL), BF16),
        compiler_params=_params(("parallel", "parallel")),
    )(y_att, y_sgu, w_oa, w_ob, proj, proj, *dep_args)


def _merge_bwd(dx1_bf, w_out, y_att, y_sgu, w_oa, w_ob, proj, *, name, tm=1024, tn=512):
    T = y_att.shape[0]

    def body(dx_ref, wo_ref, ya_ref, ys_ref, wa_ref, wb_ref, ga_ref, gb_ref, dpa_ref, dpb_ref, dga_ref, dgb_ref):
        dm = _dot_nt(dx_ref[...], wo_ref[...])
        pa = _dot_nn(ya_ref[...], wa_ref[...])
        pb = _dot_nn(ys_ref[...], wb_ref[...])
        sa = _sigmoid(ga_ref[...].astype(F32))
        sb = _sigmoid(gb_ref[...].astype(F32))
        dpa_ref[...] = (dm * sa).astype(BF16)
        dpb_ref[...] = (dm * sb).astype(BF16)
        dga_ref[...] = (dm * pa * sa * (1.0 - sa)).astype(BF16)
        dgb_ref[...] = (dm * pb * sb * (1.0 - sb)).astype(BF16)

    act = pl.BlockSpec((tm, ATT_WIDTH), lambda i, j: (i, 0))
    wgt = pl.BlockSpec((ATT_WIDTH, tn), lambda i, j: (0, j))
    out = pl.BlockSpec((tm, tn), lambda i, j: (i, j))
    return pl.pallas_call(
        body, name=name, grid=(T // tm, D_MODEL // tn),
        in_specs=[pl.BlockSpec((tm, D_MODEL), lambda i, j: (i, 0)),
                  pl.BlockSpec((tn, D_MODEL), lambda i, j: (j, 0)),
                  act, act, wgt, wgt,
                  pl.BlockSpec((tm, tn), lambda i, j: (i, j + COL_GA // tn)),
                  pl.BlockSpec((tm, tn), lambda i, j: (i, j + COL_GB // tn))],
        out_specs=[out] * 4,
        out_shape=[jax.ShapeDtypeStruct((T, D_MODEL), BF16)] * 4,
        compiler_params=_params(("parallel", "parallel")),
    )(dx1_bf, w_out, y_att, y_sgu, w_oa, w_ob, proj, proj)


CONV_ROWS = 256
CONV_TN = 256


def _shift_rows(cur, prev8, k):
    rolled = pltpu.roll(cur, k, axis=0)
    head = jnp.where(lax.broadcasted_iota(jnp.int32, prev8.shape, 0) < k, pltpu.roll(prev8, k, axis=0), rolled[:SUBLANES])
    return jnp.concatenate([head, rolled[SUBLANES:]], axis=0)


def _shift_rows_up(cur, next8, k):
    n = cur.shape[0]
    rolled = pltpu.roll(cur, n - k, axis=0)
    tail = jnp.where(lax.broadcasted_iota(jnp.int32, next8.shape, 0) >= SUBLANES - k,
                     pltpu.roll(next8, SUBLANES - k, axis=0), rolled[n - SUBLANES:])
    return jnp.concatenate([rolled[:n - SUBLANES], tail], axis=0)


HALO_ROWS = 16


def _rows_before(z_ref, r0, first):
    rp = pl.multiple_of(jnp.maximum(r0 - HALO_ROWS, 0), HALO_ROWS)
    halo = z_ref[pl.ds(rp, HALO_ROWS), :].astype(F32)
    return jnp.where(first, 0.0, halo[HALO_ROWS - SUBLANES:])


def _conv_rows(z_ref, r0, first, w_ref, b_ref, rows):
    cur = z_ref[pl.ds(r0, rows), :].astype(F32)
    prev8 = _rows_before(z_ref, r0, first)
    z1 = _shift_rows(cur, prev8, 1)
    z2 = _shift_rows(cur, prev8, 2)
    return b_ref[...] + w_ref[0:1, :] * z2 + w_ref[1:2, :] * z1 + w_ref[2:3, :] * cur


def _conv_fwd(z_g, z_v, cw_g, cw_v, cb_g, cb_v, *, n_seq, seq, name):
    T = n_seq * seq
    tn, rows = CONV_TN, CONV_ROWS

    def body(zg_ref, zv_ref, wg_ref, wv_ref, bg_ref, bv_ref, a_ref, cg_ref, cv_ref):
        def step(s, carry):
            r0 = pl.multiple_of(s * rows, rows)
            first = s == 0
            g = _conv_rows(zg_ref, r0, first, wg_ref, bg_ref, rows)
            v = _conv_rows(zv_ref, r0, first, wv_ref, bv_ref, rows)
            a_ref[pl.ds(r0, rows), :] = (g * _sigmoid(g) * v).astype(BF16)
            cg_ref[pl.ds(r0, rows), :] = g.astype(ACT_DTYPE)
            cv_ref[pl.ds(r0, rows), :] = v.astype(ACT_DTYPE)
            return carry

        lax.fori_loop(0, seq // rows, step, 0)

    zs = pl.BlockSpec((seq, tn), lambda b, j: (b, j))
    ws = pl.BlockSpec((3, tn), lambda b, j: (0, j))
    bs = pl.BlockSpec((1, tn), lambda b, j: (0, j))
    return pl.pallas_call(
        body, name=name, grid=(n_seq, D_FF // tn),
        in_specs=[zs, zs, ws, ws, bs, bs], out_specs=[zs] * 3,
        out_shape=[jax.ShapeDtypeStruct((T, D_FF), BF16)] + [jax.ShapeDtypeStruct((T, D_FF), ACT_DTYPE)] * 2,
        compiler_params=_params(("parallel", "parallel")),
    )(z_g, z_v, cw_g, cw_v, cb_g.reshape(1, D_FF), cb_v.reshape(1, D_FF))


def _conv_bwd(z_g, z_v, c_g, c_v, da, cw_g, cw_v, *, n_seq, seq, name):
    T = n_seq * seq
    tn, rows = CONV_TN, CONV_ROWS
    n_steps = seq // rows

    def body(zg_ref, zv_ref, cg_ref, cv_ref, da_ref, wg_ref, wv_ref,
             dzg_ref, dzv_ref, dwg_ref, dwv_ref, dbg_ref, dbv_ref, dcg_ref, dcv_ref):
        def colsum(x):
            return jnp.sum(x, axis=0, keepdims=True)

        def grads(s, accs):
            r0 = pl.multiple_of(s * rows, rows)
            g = cg_ref[pl.ds(r0, rows), :].astype(F32)
            v = cv_ref[pl.ds(r0, rows), :].astype(F32)
            sg = _sigmoid(g)
            dav = da_ref[pl.ds(r0, rows), :].astype(F32)
            dcg = dav * v * (sg * (1.0 + g * (1.0 - sg)))
            dcv = dav * (g * sg)
            dcg_ref[pl.ds(r0, rows), :] = dcg
            dcv_ref[pl.ds(r0, rows), :] = dcv
            return accs[0] + colsum(dcg), accs[1] + colsum(dcv)

        zero = jnp.zeros((1, tn), F32)
        db = lax.fori_loop(0, n_steps, grads, (zero, zero))

        def back(s, accs):
            r0 = pl.multiple_of(s * rows, rows)
            last = s == n_steps - 1
            rn = pl.multiple_of(jnp.minimum(r0 + rows, seq - SUBLANES), SUBLANES)
            new = []
            for half, (dc_ref, w_ref, dz_ref, z_ref) in enumerate(((dcg_ref, wg_ref, dzg_ref, zg_ref),
                                                                   (dcv_ref, wv_ref, dzv_ref, zv_ref))):
                cur = dc_ref[pl.ds(r0, rows), :]
                nxt = jnp.where(last, 0.0, dc_ref[pl.ds(rn, SUBLANES), :])
                u1, u2 = _shift_rows_up(cur, nxt, 1), _shift_rows_up(cur, nxt, 2)
                dz_ref[pl.ds(r0, rows), :] = (w_ref[2:3, :] * cur + w_ref[1:2, :] * u1 + w_ref[0:1, :] * u2).astype(BF16)
                z = z_ref[pl.ds(r0, rows), :].astype(F32)
                new += [accs[3 * half] + colsum(u2 * z), accs[3 * half + 1] + colsum(u1 * z),
                        accs[3 * half + 2] + colsum(cur * z)]
            return tuple(new)

        dw = lax.fori_loop(0, n_steps, back, (zero,) * 6)
        first_seq = pl.program_id(1) == 0

        @pl.when(first_seq)
        def _():
            dwg_ref[...] = jnp.concatenate(dw[0:3], axis=0)
            dwv_ref[...] = jnp.concatenate(dw[3:6], axis=0)
            dbg_ref[...], dbv_ref[...] = db

        @pl.when(jnp.logical_not(first_seq))
        def _():
            dwg_ref[...] += jnp.concatenate(dw[0:3], axis=0)
            dwv_ref[...] += jnp.concatenate(dw[3:6], axis=0)
            dbg_ref[...] += db[0]
            dbv_ref[...] += db[1]

    zs = pl.BlockSpec((seq, tn), lambda j, b: (b, j))
    ws = pl.BlockSpec((3, tn), lambda j, b: (0, j))
    bs = pl.BlockSpec((1, tn), lambda j, b: (0, j))
    outs = pl.pallas_call(
        body, name=name, grid=(D_FF // tn, n_seq),
        in_specs=[zs] * 5 + [ws, ws],
        out_specs=[zs, zs, ws, ws, bs, bs],
        out_shape=[jax.ShapeDtypeStruct((T, D_FF), BF16)] * 2 + [jax.ShapeDtypeStruct((3, D_FF), F32)] * 2
        + [jax.ShapeDtypeStruct((1, D_FF), F32)] * 2,
        scratch_shapes=[pltpu.VMEM((seq, tn), F32), pltpu.VMEM((seq, tn), F32)],
        compiler_params=_params(("parallel", "arbitrary")),
    )(z_g, z_v, c_g, c_v, da, cw_g, cw_v)
    dz_g, dz_v, dw_g, dw_v, db_g, db_v = outs
    return dz_g, dz_v, dw_g, dw_v, db_g.reshape(D_FF), db_v.reshape(D_FF)


def _layer_fwd(x, h, w, sched, tail, *, n_seq, seq, l):
    tag = f"l{l}"
    deps = sched("fwd_start", l, h)
    proj = _mm(h, w["w_in_t"], mode="nt", out_dtype=ACT_DTYPE, rotate=W_IN_ROTATE, name=f"{tag}_proj", deps=deps)
    y_att = _attention_fwd(proj, w["q_norm"], w["k_norm"], w["sinks"], n_seq=n_seq, seq=seq, name=f"{tag}_att")
    deps = sched("fwd_att", l, y_att)
    y_sgu = _sgu_fwd(proj, w["sgu_norm"], w["w_s"], w["bias_full"], n_seq=n_seq, seq=seq, name=f"{tag}_sgu")
    merged = _merge_fwd(y_att, y_sgu, w["w_oa"], w["w_ob"], proj, name=f"{tag}_merge", deps=deps)
    x1, h2 = _mm_rows(merged, w["w_out"], mode="nn", fn=_residual_then_norm, out_dtypes=(F32, BF16), rows=(x,),
                      vecs=(w["ffn_norm"],), name=f"{tag}_out")
    deps = sched("fwd_mixer_done", l, x1)
    z_g = _mm(h2, w["w_up_t"], mode="nt", out_dtype=ACT_DTYPE, b_rows=(0, D_FF), name=f"{tag}_up_g", deps=deps)
    z_v = _mm(h2, w["w_up_t"], mode="nt", out_dtype=ACT_DTYPE, b_rows=(D_FF, D_FF), name=f"{tag}_up_v")
    a, c_g, c_v = _conv_fwd(z_g, z_v, w["cw_g"], w["cw_v"], w["cb_g"], w["cb_v"], n_seq=n_seq, seq=seq,
                            name=f"{tag}_conv")
    deps = sched("fwd_conv", l, a)
    if tail[0] == "norm":
        out = _mm_rows(a, w["w_down"], mode="nn", fn=_residual_then_norm, out_dtypes=(F32, BF16), rows=(x1,),
                       vecs=(tail[1],), name=f"{tag}_down", deps=deps)
    else:
        out = _mm_rows(a, w["w_down"], mode="nn", fn=_residual_then_loss, out_dtypes=(F32, BF16), rows=(x1, tail[1]),
                       reduce=True, name=f"{tag}_down", deps=deps)
    saved = dict(x=x, h=h, proj=proj, y_att=y_att, y_sgu=y_sgu, merged=merged, x1=x1, h2=h2, z_g=z_g, z_v=z_v,
                 c_g=c_g, c_v=c_v, a=a)
    return out, saved


def _layer_bwd(dx2, dx2_bf, w, s, sched, deps, *, n_seq, seq, l):
    tag = f"l{l}b"
    g = {}
    da = _mm(dx2_bf, w["w_down"], mode="nt", out_dtype=ACT_DTYPE, name=f"{tag}_da", deps=deps)
    g["w_down"] = _mm(s["a"], dx2_bf, mode="tn", out_dtype=F32, name=f"{tag}_dw_down")
    dz_g, dz_v, g["cw_g"], g["cw_v"], g["cb_g"], g["cb_v"] = _conv_bwd(
        s["z_g"], s["z_v"], s["c_g"], s["c_v"], da, w["cw_g"], w["cw_v"], n_seq=n_seq, seq=seq, name=f"{tag}_conv")
    dw_up_t = _mm(dz_g, s["h2"], mode="tn", out_dtype=F32, out_rows=(0, 2 * D_FF), name=f"{tag}_dw_up_g")
    g["w_up_t"] = _mm(dz_v, s["h2"], mode="tn", out_dtype=F32, out_rows=(D_FF, 2 * D_FF), out_prev=dw_up_t,
                      name=f"{tag}_dw_up_v")
    deps = sched("bwd_ffn_grads", l, dz_v, g)
    dx1, dx1_bf, dgain = _mm_rows((dz_g, dz_v), w["w_up_t"], mode="nn", fn=_rms_bwd_rows, out_dtypes=(F32, BF16),
                                  rows=(s["x1"], dx2), vecs=(w["ffn_norm"],), reduce=True, a_at=(0, D_FF),
                                  name=f"{tag}_dh2", deps=deps)
    g["ffn_norm"] = dgain.reshape(D_MODEL)
    dpa, dpb, dga, dgb = _merge_bwd(dx1_bf, w["w_out"], s["y_att"], s["y_sgu"], w["w_oa"], w["w_ob"], s["proj"],
                                    name=f"{tag}_merge")
    deps = sched("bwd_merge", l, dpa)
    g["w_out"] = _mm(s["merged"], dx1_bf, mode="tn", out_dtype=F32, name=f"{tag}_dw_out",
                     deps=deps)
    dy_att = _mm(dpa, w["w_oa"], mode="nt", out_dtype=BF16, name=f"{tag}_dy_att")
    dy_sgu = _mm(dpb, w["w_ob"], mode="nt", out_dtype=F32, name=f"{tag}_dy_sgu")
    g["w_oa"] = _mm(s["y_att"], dpa, mode="tn", out_dtype=F32, name=f"{tag}_dw_oa")
    g["w_ob"] = _mm(s["y_sgu"], dpb, mode="tn", out_dtype=F32, name=f"{tag}_dw_ob")
    deps = sched("bwd_out_grads", l, dy_att, g)
    dqkv, g["q_norm"], g["k_norm"], g["sinks"] = _attention_bwd(
        s["proj"], dy_att, w["q_norm"], w["k_norm"], w["sinks"], n_seq=n_seq, seq=seq, name=f"{tag}_att", deps=deps)
    deps = sched("bwd_att", l, dqkv)
    dsuv, g["sgu_norm"], g["w_s"], g["b_s"] = _sgu_bwd(
        s["proj"], dy_sgu, w["sgu_norm"], w["w_s"], w["bias_full"], n_seq=n_seq, seq=seq, name=f"{tag}_sgu", deps=deps)
    dproj = (dsuv, dga, dgb, dqkv)
    at = (QKV_WIDTH, QKV_WIDTH + 2 * SGU_WIDTH, QKV_WIDTH + 2 * SGU_WIDTH + D_MODEL, 0)
    g["w_in_t"] = _mm_tn_parts(dproj, at, s["h"], name=f"{tag}_dw_in")
    deps = sched("bwd_w_in_grad", l, dqkv, g)
    dx, dx_bf, dgain = _mm_rows(dproj, w["w_in_t"], mode="nn", fn=_rms_bwd_rows, out_dtypes=(F32, BF16),
                                rows=(s["x"], dx1), vecs=(w["mix_norm"],), reduce=True, a_at=at,
                                name=f"{tag}_dh", deps=deps)
    g["mix_norm"] = dgain.reshape(D_MODEL)
    return dx, dx_bf, g, sched("bwd_dh", l, dx)


def _local_step(x, target, weights, sched, *, n_seq, seq):
    depth = len(weights)
    saved = []
    h = _rms_fwd(x, weights[0]["mix_norm"], name="l0_mix_norm", deps=sched("begin", 0, x))
    for l in range(depth):
        tail = ("norm", weights[l + 1]["mix_norm"]) if l + 1 < depth else ("loss", target)
        out, s = _layer_fwd(x, h, weights[l], sched, tail, n_seq=n_seq, seq=seq, l=l)
        saved.append(s)
        if l + 1 < depth:
            x, h = out
    dy, dy_bf, loss_cols = out
    grads = [None] * depth
    deps = ()
    for l in reversed(range(depth)):
        dy, dy_bf, grads[l], deps = _layer_bwd(dy, dy_bf, weights[l], saved[l], sched, deps, n_seq=n_seq, seq=seq, l=l)
    return jnp.sum(loss_cols), dy, grads, deps


W_IN_SHARD = IN_WIDTH // N_DEV
W_UP_SHARD = 2 * D_FF // N_DEV
COL_MOVE_ROWS = 256


def _w_o_moves():
    return tuple((j, 0, LANES, 0, j * LANES) for j in range(N_DEV))


def _disassemble(mats, w, moves, *, name):
    R = mats[0].shape[0]
    tr = min(R, COL_MOVE_ROWS)
    n = len(mats)

    def body(*refs):
        m_refs, o_ref = refs[:n], refs[n]
        for j, lo, hi, which, at in moves:
            o_ref[j, :, lo:hi] = m_refs[which][:, at:at + hi - lo]

    return pl.pallas_call(
        body, name=name, grid=(R // tr,),
        in_specs=[pl.BlockSpec((tr, m.shape[1]), lambda i: (i, 0)) for m in mats],
        out_specs=pl.BlockSpec((N_DEV, tr, w), lambda i: (0, i, 0)),
        out_shape=jax.ShapeDtypeStruct((N_DEV, R, w), mats[0].dtype),
        compiler_params=_params(("parallel",)),
    )(*mats)


def _my_place():
    return lax.axis_index("x"), lax.axis_index("y"), lax.axis_index("c")


def _gathered_shape(shape, kind):
    r, c = shape
    return {"blocks": (N_DEV, r, c), "rows": (N_DEV * r, c), "cols": (r, N_DEV * c)}[kind]


def _gather_window(ref, kind, shape, j):
    r, c = shape
    if kind == "blocks":
        return ref.at[j]
    if kind == "rows":
        return ref.at[pl.ds(pl.multiple_of(j * r, r), r), :]
    return ref.at[:, pl.ds(pl.multiple_of(j * c, c), c)]


def _gather(srcs, kinds, *, name):
    n = len(srcs)
    shapes = [s.shape for s in srcs]
    per = 7

    def body(*refs):
        src_refs, dst_refs = refs[:n], refs[n:2 * n]
        send_sems, recv_sems, local_sems = refs[2 * n:]
        x, y, c = _my_place()
        me, sibling = (x, y, c), (x, y, 1 - c)
        chips = [(1 - x, y), (x, 1 - y), (1 - x, 1 - y)]

        def at(i, px, py, pc):
            return _gather_window(dst_refs[i], kinds[i], shapes[i], 4 * px + 2 * py + pc)

        def copy(i, k, block, to, src=None):
            return pltpu.make_async_remote_copy(
                src_ref=at(i, *block) if src is None else src, dst_ref=at(i, *block),
                send_sem=send_sems.at[per * i + k], recv_sem=recv_sems.at[per * i + k], device_id=to, device_id_type=MESH)

        mine = [pltpu.make_async_copy(src_refs[i], at(i, *me), local_sems.at[i]) for i in range(n)]
        for cp in mine:
            cp.start()
        started = []
        for i in range(n):
            first = [copy(i, 0, me, sibling, src=src_refs[i])]
            first += [copy(i, 1 + j, me, (*chip, c), src=src_refs[i]) for j, chip in enumerate(chips)]
            for cp in first:
                cp.start()
            started += first
        for i in range(n):
            for j, chip in enumerate(chips):
                copy(i, 1 + j, (*chip, c), me).wait_recv()
                fwd = copy(i, 4 + j, (*chip, c), sibling)
                fwd.start()
                started.append(fwd)
        for i in range(n):
            copy(i, 0, sibling, me).wait_recv()
            for j, chip in enumerate(chips):
                copy(i, 4 + j, (*chip, 1 - c), me).wait_recv()
        for cp in started:
            cp.wait_send()
        for cp in mine:
            cp.wait()

    return pl.pallas_call(
        body, name=name,
        out_shape=[jax.ShapeDtypeStruct(_gathered_shape(s.shape, k), s.dtype) for s, k in zip(srcs, kinds)],
        in_specs=[ANY] * n, out_specs=[ANY] * n,
        scratch_shapes=[pltpu.SemaphoreType.DMA((per * n,)), pltpu.SemaphoreType.DMA((per * n,)),
                        pltpu.SemaphoreType.DMA((n,))],
    )(*srcs)


HBM = pl.BlockSpec(memory_space=pltpu.HBM)
SEM = pl.BlockSpec(memory_space=pltpu.SEMAPHORE)
TOKEN = jax.ShapeDtypeStruct((SUBLANES, LANES), F32)
TOKEN_SPEC = pl.BlockSpec(memory_space=pltpu.VMEM)
SPLIT_PARAMS = pltpu.CompilerParams(has_side_effects=pltpu.SideEffectType.DATAFLOW_SIDE_EFFECTING)


def _in_hbm(x):
    return pltpu.with_memory_space_constraint(x, pltpu.HBM)


def _hbm_like(shape, dtype):
    return pltpu.HBM(shape, dtype)


def _place_own(shards, kinds, dtypes, *, name, deps=()):
    n = len(shards)
    shapes = [s.shape for s in shards]

    def body(*refs):
        s_refs, land_refs, bufs, sems = refs[:n], refs[n:2 * n], refs[2 * n:3 * n], refs[3 * n]
        x, y, c = _my_place()
        copies = []
        for i in range(n):
            bufs[i][...] = s_refs[i][...].astype(dtypes[i])
            copies.append(pltpu.make_async_copy(
                bufs[i], _gather_window(land_refs[i], kinds[i], shapes[i], 4 * x + 2 * y + c), sems.at[i]))
        for cp in copies:
            cp.start()
        for cp in copies:
            cp.wait()

    body, dep_specs, dep_args = _with_deps(body, n, deps)
    return pl.pallas_call(
        body, name=name,
        out_shape=[jax.ShapeDtypeStruct(_gathered_shape(s, k), d) for s, k, d in zip(shapes, kinds, dtypes)],
        in_specs=[pl.BlockSpec(memory_space=pltpu.VMEM)] * n + dep_specs, out_specs=[ANY] * n,
        scratch_shapes=[pltpu.VMEM(s, d) for s, d in zip(shapes, dtypes)] + [pltpu.SemaphoreType.DMA((n,))],
        compiler_params=_params(),
    )(*shards, *dep_args)


def _gather_start(lands, kinds, shapes, after=(), *, name):
    n = len(lands)
    n_after = len(after)

    def body(*refs):
        land_refs = refs[:n]
        send_sems, recv_sems = refs[n + n_after], refs[n + n_after + 1]
        x, y, c = _my_place()
        targets = [(x, y, 1 - c), (1 - x, y, c), (x, 1 - y, c), (1 - x, 1 - y, c)]
        for i in range(n):
            own = _gather_window(land_refs[i], kinds[i], shapes[i], 4 * x + 2 * y + c)
            for k, to in enumerate(targets):
                pltpu.make_async_remote_copy(
                    src_ref=own, dst_ref=own, send_sem=send_sems.at[4 * i + k], recv_sem=recv_sems.at[4 * i + k],
                    device_id=to, device_id_type=MESH).start()
        refs[-1][...] = jnp.zeros_like(refs[-1])

    outs = pl.pallas_call(
        body, name=name,
        out_shape=[pltpu.SemaphoreType.DMA((4 * n,)), pltpu.SemaphoreType.DMA((4 * n,))]
        + [_hbm_like(a.shape, a.dtype) for a in lands] + [TOKEN],
        in_specs=[HBM] * n + [ANY] * n_after, out_specs=[SEM, SEM] + [HBM] * n + [TOKEN_SPEC],
        input_output_aliases={i: 2 + i for i in range(n)},
        compiler_params=SPLIT_PARAMS,
    )(*[_in_hbm(a) for a in lands], *after)
    return outs[0], outs[1], outs[2:2 + n], outs[-1]


def _gather_forward(recv_sems, lands, kinds, shapes, after, *, name):
    n = len(lands)

    def body(*refs):
        recv_ref, land_refs = refs[0], refs[1:1 + n]
        fwd_send, fwd_recv = refs[2 + n], refs[3 + n]
        token = refs[-1]
        x, y, c = _my_place()
        chips = [(1 - x, y), (x, 1 - y), (1 - x, 1 - y)]
        for i in range(n):
            for j, (px, py) in enumerate(chips):
                block = _gather_window(land_refs[i], kinds[i], shapes[i], 4 * px + 2 * py + c)
                pltpu.make_async_remote_copy(
                    src_ref=block, dst_ref=block, send_sem=fwd_send.at[3 * i + j], recv_sem=recv_ref.at[4 * i + 1 + j],
                    device_id=(px, py, c), device_id_type=MESH).wait_recv()
                pltpu.make_async_remote_copy(
                    src_ref=block, dst_ref=block, send_sem=fwd_send.at[3 * i + j], recv_sem=fwd_recv.at[3 * i + j],
                    device_id=(x, y, 1 - c), device_id_type=MESH).start()
        token[...] = jnp.zeros_like(token)

    outs = pl.pallas_call(
        body, name=name,
        out_shape=[pltpu.SemaphoreType.DMA((3 * n,)), pltpu.SemaphoreType.DMA((3 * n,))]
        + [_hbm_like(a.shape, a.dtype) for a in lands] + [TOKEN],
        in_specs=[SEM] + [HBM] * n + [ANY], out_specs=[SEM, SEM] + [HBM] * n + [TOKEN_SPEC],
        input_output_aliases={1 + i: 2 + i for i in range(n)},
        compiler_params=SPLIT_PARAMS,
    )(recv_sems, *lands, after)
    return outs[0], outs[1], outs[2:2 + n], outs[-1]


def _gather_finish(send_sems, recv_sems, fwd_send, fwd_recv, lands, kinds, shapes, after, *, name):
    n = len(lands)

    def body(*refs):
        send_ref, recv_ref, fsend_ref, frecv_ref = refs[:4]
        land_refs = refs[4:4 + n]
        x, y, c = _my_place()
        chips = [(1 - x, y), (x, 1 - y), (1 - x, 1 - y)]
        sibling = (x, y, 1 - c)
        for i in range(n):
            def window(j):
                return _gather_window(land_refs[i], kinds[i], shapes[i], j)

            mine, theirs = window(4 * x + 2 * y + c), window(4 * x + 2 * y + (1 - c))
            pltpu.make_async_remote_copy(src_ref=mine, dst_ref=theirs, send_sem=send_ref.at[4 * i],
                                         recv_sem=recv_ref.at[4 * i], device_id=sibling, device_id_type=MESH).wait_recv()
            for j, (px, py) in enumerate(chips):
                block = window(4 * px + 2 * py + (1 - c))
                pltpu.make_async_remote_copy(src_ref=block, dst_ref=block, send_sem=fsend_ref.at[3 * i + j],
                                             recv_sem=frecv_ref.at[3 * i + j], device_id=sibling,
                                             device_id_type=MESH).wait_recv()
            for k in range(4):
                pltpu.make_async_remote_copy(src_ref=mine, dst_ref=mine, send_sem=send_ref.at[4 * i + k],
                                             recv_sem=recv_ref.at[4 * i + k], device_id=sibling,
                                             device_id_type=MESH).wait_send()
            for j, (px, py) in enumerate(chips):
                block = window(4 * px + 2 * py + c)
                pltpu.make_async_remote_copy(src_ref=block, dst_ref=block, send_sem=fsend_ref.at[3 * i + j],
                                             recv_sem=frecv_ref.at[3 * i + j], device_id=sibling,
                                             device_id_type=MESH).wait_send()

    return pl.pallas_call(
        body, name=name,
        out_shape=[_hbm_like(a.shape, a.dtype) for a in lands],
        in_specs=[SEM] * 4 + [HBM] * n + [ANY], out_specs=[HBM] * n,
        input_output_aliases={4 + i: i for i in range(n)},
        compiler_params=SPLIT_PARAMS,
    )(send_sems, recv_sems, fwd_send, fwd_recv, *lands, after)


def _pair_plan(src_ref, land_ref, x, y, c):
    return [(src_ref.at[2 * k + (1 - c)], land_ref.at[k], (x, y, 1 - c)) for k in range(N_CHIPS)]


def _chip_plan(src_ref, land_ref, x, y, c):
    chips = [(1 - x, y), (x, 1 - y), (1 - x, 1 - y)]
    return [(src_ref.at[2 * px + py], land_ref.at[k], (px, py, c)) for k, (px, py) in enumerate(chips)]


def _exchange_copies(plan, per, src_refs, land_refs, send_sems, recv_sems):
    x, y, c = _my_place()
    copies = []
    for i, (s_ref, l_ref) in enumerate(zip(src_refs, land_refs)):
        for q, (src, dst, to) in enumerate(plan(s_ref, l_ref, x, y, c)):
            copies.append(pltpu.make_async_remote_copy(
                src_ref=src, dst_ref=dst, send_sem=send_sems.at[per * i + q], recv_sem=recv_sems.at[per * i + q],
                device_id=to, device_id_type=MESH))
    return copies


def _exchange_start(srcs, plan, per, *, name):
    n = len(srcs)

    def body(*refs):
        src_refs, land_refs = refs[:n], refs[n:2 * n]
        send_sems, recv_sems = refs[2 * n], refs[2 * n + 1]
        for cp in _exchange_copies(plan, per, src_refs, land_refs, send_sems, recv_sems):
            cp.start()
        refs[-1][...] = jnp.zeros_like(refs[-1])

    lands = [lax.empty((per,) + s.shape[1:], s.dtype) for s in srcs]
    outs = pl.pallas_call(
        body, name=name,
        out_shape=[pltpu.SemaphoreType.DMA((per * n,)), pltpu.SemaphoreType.DMA((per * n,))]
        + [_hbm_like(s.shape, s.dtype) for s in srcs] + [_hbm_like(a.shape, a.dtype) for a in lands] + [TOKEN],
        in_specs=[HBM] * (2 * n), out_specs=[SEM, SEM] + [HBM] * (2 * n) + [TOKEN_SPEC],
        input_output_aliases={i: 2 + i for i in range(2 * n)},
        compiler_params=SPLIT_PARAMS,
    )(*[_in_hbm(s) for s in srcs], *[_in_hbm(a) for a in lands])
    return outs[0], outs[1], outs[2:2 + n], outs[2 + n:2 + 2 * n], outs[-1]


def _exchange_wait(send_sems, recv_sems, srcs, lands, plan, per, after, *, name):
    n = len(srcs)
    after = list(after) if isinstance(after, (list, tuple)) else [after]

    def body(*refs):
        send_ref, recv_ref = refs[0], refs[1]
        src_refs, land_refs = refs[2:2 + n], refs[2 + n:2 + 2 * n]
        copies = _exchange_copies(plan, per, src_refs, land_refs, send_ref, recv_ref)
        for cp in copies:
            cp.wait_recv()
        for cp in copies:
            cp.wait_send()

    outs = pl.pallas_call(
        body, name=name,
        out_shape=[_hbm_like(s.shape, s.dtype) for s in srcs] + [_hbm_like(a.shape, a.dtype) for a in lands],
        in_specs=[SEM, SEM] + [HBM] * (2 * n) + [ANY] * len(after), out_specs=[HBM] * (2 * n),
        input_output_aliases={2 + i: i for i in range(2 * n)},
        compiler_params=SPLIT_PARAMS,
    )(send_sems, recv_sems, *srcs, *lands, *after)
    return outs[:n], outs[n:]


REDUCE_BLOCK_BYTES = 2 << 20


def _row_tile(r, c):
    row_bytes = 4 * (-(-c // LANES) * LANES)
    best = r
    for d in range(SUBLANES, r, SUBLANES):
        if r % d == 0 and d * row_bytes <= REDUCE_BLOCK_BYTES:
            best = d
    return best if r * row_bytes > REDUCE_BLOCK_BYTES else r


def _reduce_pair_sum(blocked, recv, place, wire_dtype, *, name):
    _, r, c = blocked.shape
    tr = _row_tile(r, c)

    def body(place_ref, g_ref, r_ref, own_ref, send_ref):
        s = g_ref[...] + r_ref[...]
        send_ref[...] = s.astype(wire_dtype)

        @pl.when(pl.program_id(1) == place_ref[1])
        def _():
            own_ref[...] = s

    return pl.pallas_call(
        body, name=name,
        grid_spec=pltpu.PrefetchScalarGridSpec(
            num_scalar_prefetch=1, grid=(r // tr, N_CHIPS),
            in_specs=[pl.BlockSpec((None, None, tr, c), lambda i, k, place_ref: (k, place_ref[0], i, 0)),
                      pl.BlockSpec((None, tr, c), lambda i, k, place_ref: (k, i, 0))],
            out_specs=[pl.BlockSpec((tr, c), lambda i, k, place_ref: (i, 0)),
                       pl.BlockSpec((None, tr, c), lambda i, k, place_ref: (k, i, 0))]),
        out_shape=[jax.ShapeDtypeStruct((r, c), F32), jax.ShapeDtypeStruct((N_CHIPS, r, c), wire_dtype)],
        compiler_params=_params(("parallel", "arbitrary")),
    )(place, blocked.reshape(N_CHIPS, 2, r, c), recv)


def _chip_sum(own_ref, r_ref):
    return ((own_ref[...] + r_ref[0].astype(F32)) + r_ref[1].astype(F32)) + r_ref[2].astype(F32)


def _reduce_chip_sum(own, recv, *, name):
    r, c = own.shape
    tr = _row_tile(r, c)

    def body(own_ref, r_ref, o_ref):
        o_ref[...] = _chip_sum(own_ref, r_ref)

    return pl.pallas_call(
        body, name=name, grid=(r // tr,),
        in_specs=[pl.BlockSpec((tr, c), lambda i: (i, 0)), pl.BlockSpec((N_CHIPS - 1, tr, c), lambda i: (0, i, 0))],
        out_specs=pl.BlockSpec((tr, c), lambda i: (i, 0)),
        out_shape=jax.ShapeDtypeStruct((r, c), F32),
        compiler_params=_params(("parallel",)),
    )(own, recv)


def _adamw_math(w, g, m, v):
    nm = ADAM_B1 * m + (1.0 - ADAM_B1) * g
    nv = ADAM_B2 * v + (1.0 - ADAM_B2) * (g * g)
    m_hat = nm / (1.0 - ADAM_B1 ** ADAM_STEP)
    v_hat = nv / (1.0 - ADAM_B2 ** ADAM_STEP)
    return -ADAM_LR * (m_hat / (jnp.sqrt(v_hat) + ADAM_EPS) + ADAM_WD * w), nm, nv


def _adamw(w, g, m, v, *, name):
    shape = w.shape
    C = shape[-1]
    R = math.prod(shape[:-1])
    tr = _row_tile(R, C)

    def body(w_ref, g_ref, m_ref, v_ref, d_ref, nm_ref, nv_ref):
        d_ref[...], nm_ref[...], nv_ref[...] = _adamw_math(w_ref[...], g_ref[...], m_ref[...], v_ref[...])

    spec = pl.BlockSpec((tr, C), lambda i: (i, 0))
    outs = pl.pallas_call(
        body, name=name, grid=(R // tr,),
        in_specs=[spec] * 4, out_specs=[spec] * 3,
        out_shape=[jax.ShapeDtypeStruct((R, C), F32)] * 3,
        compiler_params=_params(("parallel",)),
    )(*[a.reshape(R, C) for a in (w, g, m, v)])
    return tuple(o.reshape(shape) for o in outs)


def _reduce_adamw(own, recv, w, m, v, layer, prev, *, name):
    r, c = own.shape
    tr = _row_tile(r, c)
    n_prev = 0 if prev is None else len(prev)

    def body(own_ref, r_ref, w_ref, m_ref, v_ref, *rest):
        g_ref, d_ref, nm_ref, nv_ref = rest[n_prev:]
        g = _chip_sum(own_ref, r_ref)
        g_ref[...] = g
        d_ref[...], nm_ref[...], nv_ref[...] = _adamw_math(w_ref[...], g, m_ref[...], v_ref[...])

    slot = pl.BlockSpec((None, tr, c), lambda i: (layer, i, 0))
    return pl.pallas_call(
        body, name=name, grid=(r // tr,),
        in_specs=[pl.BlockSpec((tr, c), lambda i: (i, 0)), pl.BlockSpec((N_CHIPS - 1, tr, c), lambda i: (0, i, 0)),
                  slot, slot, slot] + [ANY] * n_prev,
        out_specs=[slot] * 4,
        out_shape=[jax.ShapeDtypeStruct((DEPTH, r, c), F32)] * 4,
        input_output_aliases={5 + k: k for k in range(n_prev)},
        compiler_params=_params(("parallel",)),
    )(own, recv, w, m, v, *(prev or ()))


REPLICATED = (("mix_norm", (D_MODEL,)), ("q_norm", (HEAD_DIM,)), ("k_norm", (HEAD_DIM,)), ("sinks", (N_Q_HEADS,)),
              ("sgu_norm", (SGU_WIDTH,)), ("w_s", (SGU_GROUPS, BLOCK, BLOCK)), ("b_s", (SGU_GROUPS, BLOCK)),
              ("ffn_norm", (D_MODEL,)), ("conv_b", (2 * D_FF,)))
TRANSPOSED = ("w_in", "w_up")
SHARDED = (("w_in", "rows"), ("w_oa", "cols"), ("w_ob", "cols"), ("w_out", "rows"), ("w_up", "rows"),
           ("conv_w", "blocks"), ("w_down", "rows"))
WEIGHT_ORDER = ("mix_norm", "w_in", "q_norm", "k_norm", "sinks", "sgu_norm", "w_s", "b_s", "w_oa", "w_ob", "w_out",
                "ffn_norm", "w_up", "conv_w", "conv_b", "w_down")
MIXER_WEIGHTS = ["w_in", "w_oa", "w_ob", "w_out"]
FFN_WEIGHTS = ["w_up", "conv_w", "w_down"]


def _small_layout():
    segs, off = {}, 0
    for l in range(DEPTH):
        for name, shape in REPLICATED:
            n = math.prod(shape)
            segs[(l, name)] = (off, n)
            off += n
    per_dev = -(-off // (N_DEV * SUBLANES * LANES)) * SUBLANES * LANES
    return segs, off, per_dev


def _pack_small(grads, loss_part):
    ssegs, total, per_dev = _small_layout()
    flat = jnp.concatenate([grads[l][name].reshape(-1) for (l, name) in ssegs] + [loss_part.reshape(1)])
    return jnp.pad(flat, (0, N_DEV * per_dev - total - 1)).reshape(N_DEV, per_dev // LANES, LANES)


def _unpack_small(gathered):
    ssegs, total, _ = _small_layout()
    flat = gathered.reshape(-1)
    shapes = dict(REPLICATED)
    small = {name: jnp.stack([flat[ssegs[(l, name)][0]:ssegs[(l, name)][0] + ssegs[(l, name)][1]].reshape(shapes[name])
                              for l in range(DEPTH)]) for name, _ in REPLICATED}
    return small, flat[total]


def kernel(x, mix_norm, w_in, q_norm, k_norm, sinks, sgu_norm, w_s, b_s, w_oa, w_ob, w_out, ffn_norm, w_up, conv_w, conv_b, w_down, loss_target, m_mix_norm, m_w_in, m_q_norm, m_k_norm, m_sinks, m_sgu_norm, m_w_s, m_b_s, m_w_oa, m_w_ob, m_w_out, m_ffn_norm, m_w_up, m_conv_w, m_conv_b, m_w_down, v_mix_norm, v_w_in, v_q_norm, v_k_norm, v_sinks, v_sgu_norm, v_w_s, v_b_s, v_w_oa, v_w_ob, v_w_out, v_ffn_norm, v_w_up, v_conv_w, v_conv_b, v_w_down):
    W = dict(mix_norm=mix_norm, w_in=w_in, q_norm=q_norm, k_norm=k_norm, sinks=sinks, sgu_norm=sgu_norm, w_s=w_s, b_s=b_s,
             w_oa=w_oa, w_ob=w_ob, w_out=w_out, ffn_norm=ffn_norm, w_up=w_up, conv_w=conv_w, conv_b=conv_b, w_down=w_down)
    M = dict(mix_norm=m_mix_norm, w_in=m_w_in, q_norm=m_q_norm, k_norm=m_k_norm, sinks=m_sinks, sgu_norm=m_sgu_norm,
             w_s=m_w_s, b_s=m_b_s, w_oa=m_w_oa, w_ob=m_w_ob, w_out=m_w_out, ffn_norm=m_ffn_norm, w_up=m_w_up,
             conv_w=m_conv_w, conv_b=m_conv_b, w_down=m_w_down)
    V = dict(mix_norm=v_mix_norm, w_in=v_w_in, q_norm=v_q_norm, k_norm=v_k_norm, sinks=v_sinks, sgu_norm=v_sgu_norm,
             w_s=v_w_s, b_s=v_b_s, w_oa=v_w_oa, w_ob=v_w_ob, w_out=v_w_out, ffn_norm=v_ffn_norm, w_up=v_w_up,
             conv_w=v_conv_w, conv_b=v_conv_b, w_down=v_w_down)
    n_seq, seq, d_model = x.shape
    tokens = n_seq * seq
    mx, my, mc = _my_place()
    place = jnp.stack([mc, 2 * mx + my]).astype(jnp.int32)
    half = N_DEV // 2
    kind_of = dict(SHARDED)
    for name in TRANSPOSED:
        W[name], M[name], V[name] = (jnp.swapaxes(t[name], 1, 2) for t in (W, M, V))

    gather_groups = [[(0, MIXER_WEIGHTS[0])], [(0, n) for n in MIXER_WEIGHTS[1:]], [(0, n) for n in FFN_WEIGHTS],
                     [(1, n) for n in MIXER_WEIGHTS], [(1, n) for n in FFN_WEIGHTS]]
    started, in_flight = {}, {}
    weights = []
    for l in range(DEPTH):
        w = {name: W[name][l] for name, _ in REPLICATED}
        w["cb_g"], w["cb_v"] = W["conv_b"][l][:D_FF], W["conv_b"][l][D_FF:]
        w["bias_full"] = jnp.repeat(W["b_s"][l].T, SGU_WIDTH // SGU_GROUPS, axis=1)
        weights.append(w)

    def gather_start(gi, after=()):
        shards = [W[name][l] for l, name in gather_groups[gi]]
        kinds = [kind_of[name] for _, name in gather_groups[gi]]
        shapes = [s.shape for s in shards]
        lands = _place_own(shards, kinds, [F32 if name == "conv_w" else BF16 for _, name in gather_groups[gi]],
                           name=f"gather_weights_own_{gi}", deps=after)
        send, recv, lands, token = _gather_start(lands, kinds, shapes, after, name=f"gather_weights_start_{gi}")
        started[gi] = dict(sems=(send, recv), lands=lands, kinds=kinds, shapes=shapes)
        return token

    def gather_forward(gi, after):
        st = started[gi]
        in_flight[gi] = _gather_forward(st["sems"][1], st["lands"], st["kinds"], st["shapes"], after,
                                        name=f"gather_weights_forward_{gi}")
        return in_flight[gi][3]

    def gather_finish(gi, after):
        st = started.pop(gi)
        fwd_send, fwd_recv, lands_g, _ = in_flight.pop(gi)
        whole = _gather_finish(st["sems"][0], st["sems"][1], fwd_send, fwd_recv, lands_g, st["kinds"], st["shapes"], after,
                               name=f"gather_weights_finish_{gi}")
        for (l, name), arr in zip(gather_groups[gi], whole):
            w = weights[l]
            if name in TRANSPOSED:
                w[name + "_t"] = arr
            elif name == "conv_w":
                w["cw_g"] = arr[:half].transpose(1, 0, 2).reshape(3, D_FF)
                w["cw_v"] = arr[half:].transpose(1, 0, 2).reshape(3, D_FF)
            else:
                w[name] = arr

    reduce_state, results = {}, {}
    wire = {"conv_w": F32, "small": F32}

    def reduce_begin(key, names, arrays):
        send, recv, srcs_, lands_, token = _exchange_start(arrays, _pair_plan, N_CHIPS, name=f"reduce_pair_start_{key}")
        reduce_state[key] = dict(names=names, pair=(send, recv, srcs_, lands_))
        return [token]

    def reduce_pair(key, after):
        st = reduce_state[key]
        send, recv, srcs_, lands_ = st.pop("pair")
        blocked_, from_sibling = _exchange_wait(send, recv, srcs_, lands_, _pair_plan, N_CHIPS, after,
                                                name=f"reduce_pair_wait_{key}")
        sums = [_reduce_pair_sum(b, r, place, wire.get(n if isinstance(n, str) else n[1], BF16),
                                 name=f"reduce_pair_sum_{key}_{i}")
                for i, (n, b, r) in enumerate(zip(st["names"], blocked_, from_sibling))]
        st["own"] = [s[0] for s in sums]
        *st["chip"], token = _exchange_start([s[1] for s in sums], _chip_plan, N_CHIPS - 1, name=f"reduce_chip_start_{key}")
        return [token]

    def reduce_end(key, after):
        st = reduce_state.pop(key)
        send, recv, srcs_, lands_ = st["chip"]
        _, from_chips = _exchange_wait(send, recv, srcs_, lands_, _chip_plan, N_CHIPS - 1, after,
                                       name=f"reduce_chip_wait_{key}")
        done = []
        for n, own, got in zip(st["names"], st["own"], from_chips):
            if n == "small":
                results["small"] = _reduce_chip_sum(own, got, name="reduce_chip_sum_small")
            else:
                l, name = n
                results[name] = _reduce_adamw(own, got, W[name], M[name], V[name], l, results.get(name),
                                              name=f"l{l}_reduce_adamw_{name}")
                done.append(results[name][0])
        return done

    def sched(point, l, carry, g=None):
        deps = []
        if point == "begin":
            token = ()
            for gi in range(len(gather_groups)):
                token = [gather_start(gi, token)]
            deps = token
        elif point == "fwd_start" and l == 0:
            gather_finish(0, gather_forward(0, carry))
        elif point == "fwd_att" and l == 0:
            gather_finish(1, gather_forward(1, carry))
            deps = [gather_forward(2, carry)]
        elif point == "fwd_mixer_done" and l == 0:
            gather_finish(2, carry)
        elif point == "fwd_conv" and l == 0:
            deps = [gather_forward(3, carry)]
        elif point == "fwd_start" and l == 1:
            gather_finish(3, carry)
        elif point == "fwd_att" and l == 1:
            deps = [gather_forward(4, carry)]
        elif point == "fwd_mixer_done" and l == 1:
            gather_finish(4, carry)
        elif point == "bwd_ffn_grads":
            conv_w = jnp.concatenate([g[k].reshape(3, half, W_UP_SHARD).transpose(1, 0, 2) for k in ("cw_g", "cw_v")])
            deps = reduce_begin(
                f"l{l}_ffn", [(l, "w_down"), (l, "w_up"), (l, "conv_w")],
                [g["w_down"].reshape(N_DEV, D_FF // N_DEV, D_MODEL),
                 g["w_up_t"].reshape(N_DEV, W_UP_SHARD, D_MODEL), conv_w])
        elif point == "bwd_merge":
            deps = reduce_pair(f"l{l}_ffn", carry)
        elif point == "bwd_out_grads":
            deps = reduce_begin(
                f"l{l}_out", [(l, "w_out"), (l, "w_oa"), (l, "w_ob")],
                [g["w_out"].reshape(N_DEV, D_MODEL // N_DEV, D_MODEL),
                 _disassemble((g["w_oa"],), LANES, _w_o_moves(), name=f"l{l}_split_dw_oa"),
                 _disassemble((g["w_ob"],), LANES, _w_o_moves(), name=f"l{l}_split_dw_ob")])
        elif point == "bwd_att":
            deps = reduce_pair(f"l{l}_out", carry)
        elif point == "bwd_w_in_grad":
            deps = reduce_begin(f"l{l}_in", [(l, "w_in")], [g["w_in_t"].reshape(N_DEV, W_IN_SHARD, D_MODEL)])
        elif point == "bwd_dh":
            deps = reduce_pair(f"l{l}_in", carry)
        return deps

    loss_part, dx, grads, last_deps = _local_step(x.reshape(tokens, d_model), loss_target.reshape(tokens, d_model),
                                                  weights, sched, n_seq=n_seq, seq=seq)
    for g in grads:
        g["conv_b"] = jnp.concatenate([g["cb_g"], g["cb_v"]])
    after = [dx, *last_deps, *reduce_begin("small", ["small"], [_pack_small(grads, loss_part)])]
    for key in [f"l{l}_{part}" for l in reversed(range(DEPTH)) for part in ("ffn", "out", "in")][:-1]:
        after = reduce_end(key, after)
    after = reduce_end("l0_in", after + reduce_pair("small", after))
    reduce_end("small", after)

    G, delta, new_m, new_v = {}, {}, {}, {}
    for name, _ in SHARDED:
        outs = [jnp.swapaxes(o, 1, 2) for o in results[name]] if name in TRANSPOSED else results[name]
        G[name], delta[name], new_m[name], new_v[name] = outs
    small, loss = _unpack_small(_gather([results["small"]], ["blocks"], name="gather_small_grads")[0])
    G.update(small)
    for name, _ in REPLICATED:
        delta[name], new_m[name], new_v[name] = _adamw(W[name], G[name], M[name], V[name], name=f"adamw_{name}")
    return (loss, dx.reshape(n_seq, seq, d_model), *[G[n] for n in WEIGHT_ORDER], *[delta[n] for n in WEIGHT_ORDER],
            *[new_m[n] for n in WEIGHT_ORDER], *[new_v[n] for n in WEIGHT_ORDER])
```

```python
import math

import jax
import jax.numpy as jnp
from jax import lax
from jax.experimental import pallas as pl
from jax.experimental.pallas import tpu as pltpu

F32 = jnp.float32
BF16 = jnp.bfloat16
ACT_DTYPE = BF16
MESH = pl.DeviceIdType.MESH

DEPTH = 2
D_MODEL = 1024
N_Q_HEADS = 8
HEAD_DIM = 64
ATT_WIDTH = 512
KV_WIDTH = 128
BLOCK = 128
SGU_WIDTH = 512
SGU_GROUPS = 8
IN_WIDTH = 3840
D_FF = 2816
NORM_EPS = 1e-6
NEG_INF = -1e30
ATT_SCALE = HEAD_DIM ** -0.5
ALIBI_SLOPES = tuple(2.0 ** (-(h + 1)) for h in range(N_Q_HEADS))
ADAM_LR, ADAM_B1, ADAM_B2, ADAM_EPS, ADAM_WD, ADAM_STEP = 0.001, 0.9, 0.999, 1e-08, 0.01, 10
N_DEV = 8
N_CHIPS = 4

QKV_WIDTH = ATT_WIDTH + 2 * KV_WIDTH
COL_SUV, COL_GA, COL_GB, COL_QKV = 0, 1024, 2048, 3072
W_IN_ROTATE = (1, IN_WIDTH // QKV_WIDTH)

LANES = 128
SUBLANES = 8
VMEM_LIMIT_V7X = 56 * 1024 * 1024
GELU_C = math.sqrt(2.0 / math.pi)
GELU_K = 0.044715
ANY = pl.BlockSpec(memory_space=pl.ANY)


def _params(sem=None):
    return pltpu.CompilerParams(dimension_semantics=sem, vmem_limit_bytes=VMEM_LIMIT_V7X)


def _sigmoid(x):
    return 1.0 / (1.0 + jnp.exp(-x))


def _gelu(x):
    th = jnp.tanh(GELU_C * (x + GELU_K * x * x * x))
    return 0.5 * x * (1.0 + th)


def _gelu_and_grad(x):
    x2 = x * x
    th = jnp.tanh(GELU_C * (x + GELU_K * x2 * x))
    g = 0.5 * x * (1.0 + th)
    dg = 0.5 * (1.0 + th) + 0.5 * x * (1.0 - th * th) * (GELU_C * (1.0 + 3.0 * GELU_K * x2))
    return g, dg


def _dot(a, b, dims):
    return lax.dot_general(a, b, (dims, ((), ())), preferred_element_type=F32)


def _dot_nn(a, b):
    return _dot(a, b, ((1,), (0,)))


def _dot_nt(a, b):
    return _dot(a, b, ((1,), (1,)))


def _dot_tn(a, b):
    return _dot(a, b, ((0,), (0,)))


def _lo_mask(shape):
    return lax.broadcasted_iota(jnp.int32, shape, len(shape) - 1) < (LANES // 2)


def _half_sums(x, lo):
    s_lo = jnp.sum(jnp.where(lo, x, 0.0), axis=-1, keepdims=True)
    s_all = jnp.sum(x, axis=-1, keepdims=True)
    return jnp.where(lo, s_lo, s_all - s_lo)


def _dup_half(x, half, lo):
    r = pltpu.roll(x, LANES // 2, axis=1)
    return jnp.where(lo, x, r) if half == 0 else jnp.where(lo, r, x)


def _with_deps(body, n_in, deps):
    k = len(deps)
    if not k:
        return body, [], ()

    def skipping(*refs):
        return body(*refs[:n_in], *refs[n_in + k:])

    return skipping, [ANY] * k, tuple(deps)


MM_VMEM_BUDGET = 40 * 1024 * 1024
MM_MAX_TILE = 1408
MM_MAX_TK = 4096
MM_STEP_BYTES = 1 << 20


def _divisors(n, step, cap):
    return [d for d in range(step, min(n, cap) + 1, step) if n % d == 0] or [n]


def _mm_tiles(M, N, K, out_bytes, tm_divides, tn_divides):
    best = None
    for tm in _divisors(M, LANES, MM_MAX_TILE):
        for tn in _divisors(N, LANES, MM_MAX_TILE):
            if tm_divides % tm or tn_divides % tn:
                continue
            for tk in _divisors(K, 4 * LANES, MM_MAX_TK):
                vmem = 4 * (tm * tk + tk * tn) + 2 * tm * tn * out_bytes + (0 if tk == K else 4 * tm * tn)
                if vmem > MM_VMEM_BUDGET:
                    continue
                traffic = 2 * M * K * (N // tn) + 2 * K * N * (M // tm) + M * N * out_bytes
                cost = traffic + (K // tk - 1) * 8 * M * N + (M // tm) * (N // tn) * (K // tk) * MM_STEP_BYTES
                if best is None or cost < best[0]:
                    best = (cost, tm, tn, tk)
    assert best is not None, (M, N, K)
    return best[1:]


def _mm(a, b, *, mode, out_dtype, name, deps=(), b_rows=(0, None), rotate=None, out_rows=(0, None), out_prev=None):
    b_first, b_count = b_rows
    if mode == "nn":
        (M, K), N = a.shape, b.shape[1]
    elif mode == "nt":
        (M, K), N = a.shape, (b.shape[0] if b_count is None else b_count)
    else:
        (K, M), N = a.shape, b.shape[1]
    shift, period = rotate or (0, 1)
    assert period == 1 or mode == "nt"
    out_first, out_total = out_rows[0], (M if out_rows[1] is None else out_rows[1])
    tm, tn, tk = _mm_tiles(M, N, K, jnp.dtype(out_dtype).itemsize, math.gcd(M, out_first),
                           math.gcd(N // period, b_first if mode == "nt" else 0))
    gm, gn, gk = M // tm, N // tn, K // tk

    def turned(j):
        per = N // period // tn
        return ((j // per + shift) % period) * per + j % per if period > 1 else j

    if mode == "nn":
        a_spec = pl.BlockSpec((tm, tk), lambda i, j, k: (i, k))
        b_spec = pl.BlockSpec((tk, tn), lambda i, j, k: (k + b_first // tk, j))
        contract = ((1,), (0,))
    elif mode == "nt":
        a_spec = pl.BlockSpec((tm, tk), lambda i, j, k: (i, k))
        b_spec = pl.BlockSpec((tn, tk), lambda i, j, k: (turned(j) + b_first // tn, k))
        contract = ((1,), (1,))
    else:
        a_spec = pl.BlockSpec((tk, tm), lambda i, j, k: (k, i))
        b_spec = pl.BlockSpec((tk, tn), lambda i, j, k: (k, j))
        contract = ((0,), (0,))
    o_spec = pl.BlockSpec((tm, tn), lambda i, j, k: (i + out_first // tm, j))
    assert b_first % (tk if mode == "nn" else tn) == 0 and out_first % tm == 0, (name, tm, tn, tk)
    n_prev = 0 if out_prev is None else 1

    def body(a_ref, b_ref, *rest):
        o_ref = rest[n_prev]
        part = _dot(a_ref[...].astype(BF16), b_ref[...].astype(BF16), contract)
        if gk == 1:
            o_ref[...] = part.astype(out_dtype)
            return
        acc_ref = rest[n_prev + 1]
        k = pl.program_id(2)

        @pl.when(k == 0)
        def _():
            acc_ref[...] = part

        @pl.when(k > 0)
        def _():
            acc_ref[...] += part

        @pl.when(k == gk - 1)
        def _():
            o_ref[...] = acc_ref[...].astype(out_dtype)

    body, dep_specs, dep_args = _with_deps(body, 2 + n_prev, deps)
    return pl.pallas_call(
        body,
        name=name,
        grid=(gm, gn, gk),
        in_specs=[a_spec, b_spec] + [ANY] * n_prev + dep_specs,
        out_specs=o_spec,
        out_shape=jax.ShapeDtypeStruct((out_total, N), out_dtype),
        input_output_aliases={2: 0} if n_prev else {},
        scratch_shapes=[] if gk == 1 else [pltpu.VMEM((tm, tn), F32)],
        compiler_params=_params(("parallel", "parallel", "arbitrary")),
    )(a, b, *([out_prev] if n_prev else []), *dep_args)


def _mm_tn_parts(parts, at, b, *, name):
    K, N = b.shape
    n = len(parts)
    tm = math.gcd(*[p.shape[1] for p in parts], *at)
    tiles = [p.shape[1] // tm for p in parts]
    first = [sum(tiles[:p]) for p in range(n)]

    def mine(i, p):
        return jnp.logical_and(i >= first[p], i < first[p] + tiles[p])

    def out_tile(i):
        t = 0
        for p in range(n):
            t = jnp.where(mine(i, p), at[p] // tm + i - first[p], t)
        return t

    def body(*refs):
        a_refs, b_ref, o_ref = refs[:n], refs[n], refs[n + 1]
        for p in range(n):
            @pl.when(mine(pl.program_id(0), p))
            def _(p=p):
                o_ref[...] = _dot_tn(a_refs[p][...], b_ref[...])

    return pl.pallas_call(
        body, name=name, grid=(sum(tiles),),
        in_specs=[pl.BlockSpec((K, tm), lambda i, p=p: (0, jnp.clip(i - first[p], 0, tiles[p] - 1))) for p in range(n)]
        + [pl.BlockSpec((K, N), lambda i: (0, 0), pipeline_mode=pl.Buffered(1))],
        out_specs=pl.BlockSpec((tm, N), lambda i: (out_tile(i), 0)),
        out_shape=jax.ShapeDtypeStruct((sum(p.shape[1] for p in parts), N), F32),
        compiler_params=_params(("arbitrary",)),
    )(*parts, b)


def _mm_rows(a, b, *, mode, fn, out_dtypes, rows=(), vecs=(), reduce=False, name, deps=(), b_rows=(0, None), a_at=None):
    parts = a if a_at is not None else (a,)
    starts = a_at if a_at is not None else (0,)
    n_parts = len(parts)
    M, K = parts[0].shape[0], sum(p.shape[1] for p in parts)
    b_first, b_count = b_rows[0], (b.shape[0] if b_rows[1] is None else b_rows[1])
    N = b.shape[1] if mode == "nn" else b_count
    contract = ((1,), (0,)) if mode == "nn" else ((1,), (1,))
    n_rows, n_vecs, n_out = len(rows), len(vecs), len(out_dtypes)
    out_bytes = sum(jnp.dtype(d).itemsize for d in out_dtypes)
    tm = max(t for t in _divisors(M, LANES, MM_MAX_TILE)
             if 4 * t * K + 2 * K * N + 2 * t * N * (4 * n_rows + out_bytes) <= MM_VMEM_BUDGET)
    assert b_first % b_count == 0 and (a_at is None or mode == "nn")

    def body(*refs):
        a_refs, b_ref, rest = refs[:n_parts], refs[n_parts], refs[n_parts + 1:]
        row_refs, vec_refs = rest[:n_rows], rest[n_rows:n_rows + n_vecs]
        out_refs = rest[n_rows + n_vecs:]
        if a_at is None:
            acc = _dot(a_refs[0][...], b_ref[...], contract)
        else:
            acc = sum(_dot(r[...], b_ref[at:at + r.shape[1], :], contract) for r, at in zip(a_refs, starts))
        res = fn(acc, *[r[...] for r in row_refs], *[v[...] for v in vec_refs])
        for o_ref, val in zip(out_refs[:n_out], res):
            o_ref[...] = val.astype(o_ref.dtype)
        if reduce:
            @pl.when(pl.program_id(0) == 0)
            def _():
                out_refs[n_out][...] = res[n_out]

            @pl.when(pl.program_id(0) > 0)
            def _():
                out_refs[n_out][...] += res[n_out]

    row = pl.BlockSpec((tm, N), lambda i: (i, 0))
    vec = pl.BlockSpec((1, N), lambda i: (0, 0))
    body, dep_specs, dep_args = _with_deps(body, n_parts + 1 + n_rows + n_vecs, deps)
    return pl.pallas_call(
        body, name=name, grid=(M // tm,),
        in_specs=[pl.BlockSpec((tm, p.shape[1]), lambda i: (i, 0)) for p in parts]
        + [pl.BlockSpec((b_count, b.shape[1]), lambda i: (b_first // b_count, 0), pipeline_mode=pl.Buffered(1))]
        + [row] * n_rows + [vec] * n_vecs + dep_specs,
        out_specs=[row] * n_out + [vec] * reduce,
        out_shape=[jax.ShapeDtypeStruct((M, N), d) for d in out_dtypes] + [jax.ShapeDtypeStruct((1, N), F32)] * reduce,
        compiler_params=_params(("arbitrary",)),
    )(*parts, b, *rows, *[v.reshape(1, N) for v in vecs], *dep_args)


def _rms(x, gain):
    return x * lax.rsqrt(jnp.mean(x * x, axis=-1, keepdims=True) + NORM_EPS) * gain


def _residual_then_norm(acc, x, gain):
    x_out = x + acc
    return x_out, _rms(x_out, gain)


def _residual_then_loss(acc, x, target):
    err = (x + acc) - target
    dy = err * (1.0 / D_MODEL)
    return dy, dy, jnp.sum(err * err, axis=0, keepdims=True) * (0.5 / D_MODEL)


def _rms_bwd_rows(dh, x, dres, gain):
    r = lax.rsqrt(jnp.mean(x * x, axis=-1, keepdims=True) + NORM_EPS)
    xh = x * r
    dxh = dh * gain
    dx = dres + r * (dxh - xh * jnp.mean(dxh * xh, axis=-1, keepdims=True))
    return dx, dx, jnp.sum(dh * xh, axis=0, keepdims=True)


def _rms_fwd(x, gain, *, name, tm=512, deps=()):
    T, D = x.shape

    def body(x_ref, g_ref, h_ref):
        xv = x_ref[...]
        r = lax.rsqrt(jnp.mean(xv * xv, axis=-1, keepdims=True) + NORM_EPS)
        h_ref[...] = (xv * r * g_ref[...]).astype(BF16)

    body, dep_specs, dep_args = _with_deps(body, 2, deps)
    return pl.pallas_call(
        body, name=name, grid=(T // tm,),
        in_specs=[pl.BlockSpec((tm, D), lambda i: (i, 0)), pl.BlockSpec((1, D), lambda i: (0, 0))] + dep_specs,
        out_specs=pl.BlockSpec((tm, D), lambda i: (i, 0)),
        out_shape=jax.ShapeDtypeStruct((T, D), BF16),
        compiler_params=_params(("parallel",)),
    )(x, gain.reshape(1, D), *dep_args)


def _head_norm(x, gain2, lo):
    ms = _half_sums(x * x, lo) * (1.0 / HEAD_DIM)
    r = lax.rsqrt(ms + NORM_EPS)
    xh = x * r
    return xh * gain2, xh, r


def _head_norm_bwd(xh, r, gain2, dy, lo):
    dxh = dy * gain2
    dx = r * (dxh - xh * (_half_sums(dxh * xh, lo) * (1.0 / HEAD_DIM)))
    return dx, dy * xh


Q_GROUP = N_Q_HEADS // 2
GROUP_ROWS = Q_GROUP * BLOCK
ATT_SCRATCH = (pltpu.VMEM((2, 2, GROUP_ROWS, BLOCK), F32), pltpu.VMEM((2, GROUP_ROWS, 1), F32))


def _att_consts(sink_ref, bias_ref, sinkcol_ref):
    row = lax.broadcasted_iota(jnp.int32, (GROUP_ROWS, BLOCK), 0)
    kj = lax.broadcasted_iota(jnp.int32, (GROUP_ROWS, BLOCK), 1)
    head = row // BLOCK
    head_col = lax.broadcasted_iota(jnp.int32, (GROUP_ROWS, 1), 0) // BLOCK
    d_cur = (row % BLOCK) - kj
    d_prev = d_cur + BLOCK
    for kv in range(2):
        slope = jnp.zeros((GROUP_ROWS, BLOCK), F32)
        sink = jnp.zeros((GROUP_ROWS, 1), F32)
        for r in range(Q_GROUP):
            slope = jnp.where(head == r, ALIBI_SLOPES[Q_GROUP * kv + r], slope)
            sink = jnp.where(head_col == r, sink_ref[Q_GROUP * kv + r], sink)
        bias_ref[kv, 0] = jnp.where(d_cur >= 0, -slope * d_cur.astype(F32), NEG_INF)
        bias_ref[kv, 1] = jnp.where(d_prev < BLOCK, -slope * d_prev.astype(F32), NEG_INF)
        sinkcol_ref[kv] = sink


def _stack_heads(t0, t1, lo):
    z = jnp.zeros_like(t0)
    return jnp.concatenate([jnp.where(lo, t0, z), jnp.where(lo, z, t0), jnp.where(lo, t1, z), jnp.where(lo, z, t1)], axis=0)


def _unstack_heads(x4, lo):
    return (jnp.where(lo, x4[0:BLOCK], x4[BLOCK:2 * BLOCK]), jnp.where(lo, x4[2 * BLOCK:3 * BLOCK], x4[3 * BLOCK:]))


def _att_probs(q4, k2c, k2p, bias_c, bias_p, sink, has_prev):
    s_c = _dot_nt(q4, k2c) * ATT_SCALE + bias_c
    s_p = jnp.where(has_prev, _dot_nt(q4, k2p) * ATT_SCALE + bias_p, NEG_INF)
    m = jnp.maximum(jnp.max(jnp.maximum(s_c, s_p), axis=-1, keepdims=True), sink)
    e_c = jnp.exp(s_c - m)
    e_p = jnp.exp(s_p - m)
    e_s = jnp.exp(sink - m)
    inv = 1.0 / (jnp.sum(e_c + e_p, axis=-1, keepdims=True) + e_s)
    return e_c * inv, e_p * inv, e_s * inv


def _attention_fwd(proj, q_gain, k_gain, sinks, *, n_seq, seq, name):
    T = n_seq * seq
    nb = seq // BLOCK
    qcol, kvcol = COL_QKV // ATT_WIDTH, (COL_QKV + ATT_WIDTH) // (2 * KV_WIDTH)

    def body(q_ref, kv_ref, qg_ref, kg_ref, sink_ref, y_ref, bias_ref, sinkcol_ref):
        lo = _lo_mask((BLOCK, LANES))
        qg, kg = qg_ref[...], kg_ref[...]
        _att_consts(sink_ref, bias_ref, sinkcol_ref)

        def block(i, carry):
            r0 = pl.multiple_of(i * BLOCK, BLOCK)
            rp = pl.multiple_of(jnp.maximum(i - 1, 0) * BLOCK, BLOCK)
            has_prev = i > 0
            kn_c = _head_norm(kv_ref[pl.ds(r0, BLOCK), 0:KV_WIDTH].astype(F32), kg, lo)[0].astype(BF16)
            kn_p = _head_norm(kv_ref[pl.ds(rp, BLOCK), 0:KV_WIDTH].astype(F32), kg, lo)[0].astype(BF16)
            v_c = kv_ref[pl.ds(r0, BLOCK), KV_WIDTH:2 * KV_WIDTH].astype(BF16)
            v_p = kv_ref[pl.ds(rp, BLOCK), KV_WIDTH:2 * KV_WIDTH].astype(BF16)
            for kv in range(2):
                k2c, k2p = _dup_half(kn_c, kv, lo), _dup_half(kn_p, kv, lo)
                v2c, v2p = _dup_half(v_c, kv, lo), _dup_half(v_p, kv, lo)
                cols = [slice((2 * kv + t) * LANES, (2 * kv + t + 1) * LANES) for t in range(2)]
                qn = [_head_norm(q_ref[pl.ds(r0, BLOCK), c].astype(F32), qg, lo)[0] for c in cols]
                q4 = _stack_heads(qn[0], qn[1], lo).astype(BF16)
                p_c, p_p, _ = _att_probs(q4, k2c, k2p, bias_ref[kv, 0], bias_ref[kv, 1], sinkcol_ref[kv], has_prev)
                o4 = _dot_nn(p_c.astype(BF16), v2c) + _dot_nn(p_p.astype(BF16), v2p)
                for c, out in zip(cols, _unstack_heads(o4, lo)):
                    y_ref[pl.ds(r0, BLOCK), c] = out.astype(BF16)
            return carry

        lax.fori_loop(0, nb, block, 0)

    vec = pl.BlockSpec((1, LANES), lambda b: (0, 0))
    return pl.pallas_call(
        body, name=name, grid=(n_seq,),
        in_specs=[pl.BlockSpec((seq, ATT_WIDTH), lambda b: (b, qcol)),
                  pl.BlockSpec((seq, 2 * KV_WIDTH), lambda b: (b, kvcol)),
                  vec, vec, pl.BlockSpec(memory_space=pltpu.SMEM)],
        out_specs=pl.BlockSpec((seq, ATT_WIDTH), lambda b: (b, 0)),
        out_shape=jax.ShapeDtypeStruct((T, ATT_WIDTH), BF16),
        scratch_shapes=list(ATT_SCRATCH),
        compiler_params=_params(("parallel",)),
    )(proj, proj, jnp.tile(q_gain, 2).reshape(1, LANES), jnp.tile(k_gain, 2).reshape(1, LANES), sinks)


def _attention_bwd(proj, dy, q_gain, k_gain, sinks, *, n_seq, seq, name, deps=()):
    T = n_seq * seq
    nb = seq // BLOCK
    qcol, kvcol = COL_QKV // ATT_WIDTH, (COL_QKV + ATT_WIDTH) // (2 * KV_WIDTH)

    def body(q_ref, kv_ref, dy_ref, qg_ref, kg_ref, sink_ref, dqkv_ref, dqg_ref, dkg_ref, dsink_ref,
             dkn_acc, dv_acc, qg_acc, kg_acc, sink_acc, bias_ref, sinkcol_ref):
        lo = _lo_mask((BLOCK, LANES))
        qg, kg = qg_ref[...], kg_ref[...]
        _att_consts(sink_ref, bias_ref, sinkcol_ref)
        first = pl.program_id(0) == 0

        @pl.when(first)
        def _():
            qg_acc[...] = jnp.zeros_like(qg_acc)
            kg_acc[...] = jnp.zeros_like(kg_acc)
            sink_acc[...] = jnp.zeros_like(sink_acc)

        dkn_acc[...] = jnp.zeros_like(dkn_acc)
        dv_acc[...] = jnp.zeros_like(dv_acc)

        def block(i, carry):
            r0 = pl.multiple_of(i * BLOCK, BLOCK)
            rp = pl.multiple_of(jnp.maximum(i - 1, 0) * BLOCK, BLOCK)
            has_prev = i > 0
            kn_c = _head_norm(kv_ref[pl.ds(r0, BLOCK), 0:KV_WIDTH].astype(F32), kg, lo)[0].astype(BF16)
            kn_p = _head_norm(kv_ref[pl.ds(rp, BLOCK), 0:KV_WIDTH].astype(F32), kg, lo)[0].astype(BF16)
            v_c = kv_ref[pl.ds(r0, BLOCK), KV_WIDTH:2 * KV_WIDTH].astype(BF16)
            v_p = kv_ref[pl.ds(rp, BLOCK), KV_WIDTH:2 * KV_WIDTH].astype(BF16)
            dk_c, dk_p, dv_c, dv_p = [], [], [], []
            for kv in range(2):
                k2c, k2p = _dup_half(kn_c, kv, lo), _dup_half(kn_p, kv, lo)
                v2c, v2p = _dup_half(v_c, kv, lo), _dup_half(v_p, kv, lo)
                cols = [slice((2 * kv + t) * LANES, (2 * kv + t + 1) * LANES) for t in range(2)]
                normed = [_head_norm(q_ref[pl.ds(r0, BLOCK), c].astype(F32), qg, lo) for c in cols]
                q4 = _stack_heads(normed[0][0], normed[1][0], lo).astype(BF16)
                do4 = _stack_heads(dy_ref[pl.ds(r0, BLOCK), cols[0]], dy_ref[pl.ds(r0, BLOCK), cols[1]], lo)
                p_c, p_p, p_s = _att_probs(q4, k2c, k2p, bias_ref[kv, 0], bias_ref[kv, 1], sinkcol_ref[kv], has_prev)
                dp_c = _dot_nt(do4, v2c)
                dp_p = _dot_nt(do4, v2p)
                delta = jnp.sum(p_c * dp_c + p_p * dp_p, axis=-1, keepdims=True)
                ds_c = (p_c * (dp_c - delta)).astype(BF16)
                ds_p = (p_p * (dp_p - delta)).astype(BF16)
                sink_acc[kv] += -(p_s * delta)
                dq4 = (_dot_nn(ds_c, k2c) + _dot_nn(ds_p, k2p)) * ATT_SCALE
                for c, (_, qh, qr), dqn in zip(cols, normed, _unstack_heads(dq4, lo)):
                    dq, dg = _head_norm_bwd(qh, qr, qg, dqn, lo)
                    dqkv_ref[pl.ds(r0, BLOCK), c] = dq.astype(BF16)
                    qg_acc[...] += dg
                dk_c.append(_dot_tn(ds_c, q4))
                dk_p.append(_dot_tn(ds_p, q4))
                dv_c.append(_dot_tn(p_c.astype(BF16), do4))
                dv_p.append(_dot_tn(p_p.astype(BF16), do4))

            def fold(parts):
                a = parts[0] + pltpu.roll(parts[0], LANES // 2, axis=1)
                b = parts[1] + pltpu.roll(parts[1], LANES // 2, axis=1)
                return jnp.where(lo, a, b)

            dkn_acc[pl.ds(r0, BLOCK), :] += fold(dk_c) * ATT_SCALE
            dkn_acc[pl.ds(rp, BLOCK), :] += fold(dk_p) * ATT_SCALE
            dv_acc[pl.ds(r0, BLOCK), :] += fold(dv_c)
            dv_acc[pl.ds(rp, BLOCK), :] += fold(dv_p)
            return carry

        lax.fori_loop(0, nb, block, 0)

        def finish(i, carry):
            r0 = pl.multiple_of(i * BLOCK, BLOCK)
            _, kh, kr = _head_norm(kv_ref[pl.ds(r0, BLOCK), 0:KV_WIDTH].astype(F32), kg, lo)
            dk, dg = _head_norm_bwd(kh, kr, kg, dkn_acc[pl.ds(r0, BLOCK), :], lo)
            dqkv_ref[pl.ds(r0, BLOCK), ATT_WIDTH:ATT_WIDTH + KV_WIDTH] = dk.astype(BF16)
            dqkv_ref[pl.ds(r0, BLOCK), ATT_WIDTH + KV_WIDTH:QKV_WIDTH] = dv_acc[pl.ds(r0, BLOCK), :].astype(BF16)
            kg_acc[...] += dg
            return carry

        lax.fori_loop(0, nb, finish, 0)

        @pl.when(pl.program_id(0) == n_seq - 1)
        def _():
            dqg_ref[...] = jnp.sum(qg_acc[...], axis=0, keepdims=True)
            dkg_ref[...] = jnp.sum(kg_acc[...], axis=0, keepdims=True)
            lane = lax.broadcasted_iota(jnp.int32, (1, LANES), 1)
            dsink = jnp.zeros((1, LANES), F32)
            for kv in range(2):
                for r in range(Q_GROUP):
                    total = jnp.sum(sink_acc[kv, r * BLOCK:(r + 1) * BLOCK, :], axis=0, keepdims=True)
                    dsink = jnp.where(lane == Q_GROUP * kv + r, total, dsink)
            dsink_ref[...] = dsink

    vec = pl.BlockSpec((1, LANES), lambda b: (0, 0))
    acc = pltpu.VMEM((BLOCK, LANES), F32)
    body, dep_specs, dep_args = _with_deps(body, 6, deps)
    dqkv, dqg, dkg, dsink = pl.pallas_call(
        body, name=name, grid=(n_seq,),
        in_specs=[pl.BlockSpec((seq, ATT_WIDTH), lambda b: (b, qcol)),
                  pl.BlockSpec((seq, 2 * KV_WIDTH), lambda b: (b, kvcol)),
                  pl.BlockSpec((seq, ATT_WIDTH), lambda b: (b, 0)),
                  vec, vec, pl.BlockSpec(memory_space=pltpu.SMEM)] + dep_specs,
        out_specs=[pl.BlockSpec((seq, QKV_WIDTH), lambda b: (b, 0)), vec, vec, vec],
        out_shape=[jax.ShapeDtypeStruct((T, QKV_WIDTH), BF16)] + [jax.ShapeDtypeStruct((1, LANES), F32)] * 3,
        scratch_shapes=[pltpu.VMEM((seq, KV_WIDTH), F32), pltpu.VMEM((seq, KV_WIDTH), F32), acc, acc,
                        pltpu.VMEM((2, GROUP_ROWS, 1), F32), *ATT_SCRATCH],
        compiler_params=_params(("arbitrary",)),
    )(proj, proj, dy, jnp.tile(q_gain, 2).reshape(1, LANES), jnp.tile(k_gain, 2).reshape(1, LANES), sinks, *dep_args)
    half = LANES // 2
    return dqkv, dqg[0, :half] + dqg[0, half:], dkg[0, :half] + dkg[0, half:], dsink[0, :N_Q_HEADS]


def _sgu_weights(w_ref):
    r = lax.broadcasted_iota(jnp.int32, (BLOCK, BLOCK), 0)
    c = lax.broadcasted_iota(jnp.int32, (BLOCK, BLOCK), 1)
    return [jnp.where(r >= c, w_ref[g], 0.0).astype(BF16) for g in range(SGU_GROUPS)]


def _sgu_fwd(proj, gain, w_s, bias_full, *, n_seq, seq, name):
    T = n_seq * seq
    nc = seq // BLOCK

    def body(suv_ref, g_ref, w_ref, b_ref, y_ref):
        lo = _lo_mask((BLOCK, LANES))
        wm = _sgu_weights(w_ref)
        gain_v = g_ref[...]

        def chunk(c, carry):
            r0 = pl.multiple_of(c * BLOCK, BLOCK)
            gv = _gelu(suv_ref[pl.ds(r0, BLOCK), SGU_WIDTH:2 * SGU_WIDTH].astype(F32))
            r = lax.rsqrt(jnp.mean(gv * gv, axis=-1, keepdims=True) + NORM_EPS)
            vn = (gv * r * gain_v).astype(BF16)
            for p in range(SGU_WIDTH // LANES):
                cols = slice(p * LANES, (p + 1) * LANES)
                vp = vn[:, cols]
                mixed = jnp.where(lo, _dot_nn(wm[2 * p], vp), _dot_nn(wm[2 * p + 1], vp)) + b_ref[:, cols]
                u = _gelu(suv_ref[pl.ds(r0, BLOCK), cols].astype(F32))
                y_ref[pl.ds(r0, BLOCK), cols] = (u * mixed).astype(BF16)
            return carry

        lax.fori_loop(0, nc, chunk, 0)

    return pl.pallas_call(
        body, name=name, grid=(n_seq,),
        in_specs=[pl.BlockSpec((seq, 2 * SGU_WIDTH), lambda b: (b, COL_SUV // (2 * SGU_WIDTH))),
                  pl.BlockSpec((1, SGU_WIDTH), lambda b: (0, 0)),
                  pl.BlockSpec((SGU_GROUPS, BLOCK, BLOCK), lambda b: (0, 0, 0)),
                  pl.BlockSpec((BLOCK, SGU_WIDTH), lambda b: (0, 0))],
        out_specs=pl.BlockSpec((seq, SGU_WIDTH), lambda b: (b, 0)),
        out_shape=jax.ShapeDtypeStruct((T, SGU_WIDTH), BF16),
        compiler_params=_params(("parallel",)),
    )(proj, gain.reshape(1, SGU_WIDTH), w_s, bias_full)


def _sgu_bwd(proj, dy, gain, w_s, bias_full, *, n_seq, seq, name, deps=()):
    T = n_seq * seq
    nc = seq // BLOCK
    n_tiles = SGU_WIDTH // LANES

    def body(suv_ref, dy_ref, g_ref, w_ref, b_ref, dsuv_ref, dg_ref, dw_ref, db_ref, dg_acc, dw_acc, db_acc):
        lo = _lo_mask((BLOCK, LANES))
        hi = jnp.logical_not(lo)
        wm = _sgu_weights(w_ref)
        wmt = [jnp.where(lax.broadcasted_iota(jnp.int32, (BLOCK, BLOCK), 1) >= lax.broadcasted_iota(jnp.int32, (BLOCK, BLOCK), 0),
                         w_ref[g].T, 0.0).astype(BF16) for g in range(SGU_GROUPS)]
        gain_v = g_ref[...]

        @pl.when(pl.program_id(0) == 0)
        def _():
            dg_acc[...] = jnp.zeros_like(dg_acc)
            dw_acc[...] = jnp.zeros_like(dw_acc)
            db_acc[...] = jnp.zeros_like(db_acc)

        def chunk(c, carry):
            r0 = pl.multiple_of(c * BLOCK, BLOCK)
            gv, dgelu_v = _gelu_and_grad(suv_ref[pl.ds(r0, BLOCK), SGU_WIDTH:2 * SGU_WIDTH].astype(F32))
            r = lax.rsqrt(jnp.mean(gv * gv, axis=-1, keepdims=True) + NORM_EPS)
            vh = gv * r
            vn = (vh * gain_v).astype(BF16)
            dvn_tiles = []
            for p in range(n_tiles):
                cols = slice(p * LANES, (p + 1) * LANES)
                vp = vn[:, cols]
                mixed = jnp.where(lo, _dot_nn(wm[2 * p], vp), _dot_nn(wm[2 * p + 1], vp)) + b_ref[:, cols]
                u, dgelu_u = _gelu_and_grad(suv_ref[pl.ds(r0, BLOCK), cols].astype(F32))
                dyv = dy_ref[pl.ds(r0, BLOCK), cols]
                dsuv_ref[pl.ds(r0, BLOCK), cols] = (dyv * mixed * dgelu_u).astype(BF16)
                dm = dyv * u
                db_acc[:, cols] += dm
                dm_bf = dm.astype(BF16)
                dvn_tiles.append(jnp.where(lo, _dot_nn(wmt[2 * p], dm_bf), _dot_nn(wmt[2 * p + 1], dm_bf)))
                dw_acc[2 * p] += _dot_nt(jnp.where(lo, dm, 0.0).astype(BF16), vp)
                dw_acc[2 * p + 1] += _dot_nt(jnp.where(hi, dm, 0.0).astype(BF16), vp)
            dvn = jnp.concatenate(dvn_tiles, axis=1)
            dg_acc[...] += dvn * vh
            dvh = dvn * gain_v
            dgv = r * (dvh - vh * jnp.mean(dvh * vh, axis=-1, keepdims=True))
            dsuv_ref[pl.ds(r0, BLOCK), SGU_WIDTH:2 * SGU_WIDTH] = (dgv * dgelu_v).astype(BF16)
            return carry

        lax.fori_loop(0, nc, chunk, 0)

        @pl.when(pl.program_id(0) == n_seq - 1)
        def _():
            dg_ref[...] = jnp.sum(dg_acc[...], axis=0, keepdims=True)
            r = lax.broadcasted_iota(jnp.int32, (BLOCK, BLOCK), 0)
            c = lax.broadcasted_iota(jnp.int32, (BLOCK, BLOCK), 1)
            for g in range(SGU_GROUPS):
                dw_ref[g] = jnp.where(r >= c, dw_acc[g], 0.0)
            lane = lax.broadcasted_iota(jnp.int32, (BLOCK, LANES), 1)
            out = jnp.zeros((BLOCK, LANES), F32)
            for p in range(n_tiles):
                tile = db_acc[:, p * LANES:(p + 1) * LANES]
                s_lo = jnp.sum(jnp.where(lo, tile, 0.0), axis=-1, keepdims=True)
                s_hi = jnp.sum(jnp.where(hi, tile, 0.0), axis=-1, keepdims=True)
                out = jnp.where(lane == 2 * p, s_lo, out)
                out = jnp.where(lane == 2 * p + 1, s_hi, out)
            db_ref[...] = out

    body, dep_specs, dep_args = _with_deps(body, 5, deps)
    dsuv, dg, dw, db = pl.pallas_call(
        body, name=name, grid=(n_seq,),
        in_specs=[pl.BlockSpec((seq, 2 * SGU_WIDTH), lambda b: (b, COL_SUV // (2 * SGU_WIDTH))),
                  pl.BlockSpec((seq, SGU_WIDTH), lambda b: (b, 0)),
                  pl.BlockSpec((1, SGU_WIDTH), lambda b: (0, 0)),
                  pl.BlockSpec((SGU_GROUPS, BLOCK, BLOCK), lambda b: (0, 0, 0)),
                  pl.BlockSpec((BLOCK, SGU_WIDTH), lambda b: (0, 0))] + dep_specs,
        out_specs=[pl.BlockSpec((seq, 2 * SGU_WIDTH), lambda b: (b, 0)),
                   pl.BlockSpec((1, SGU_WIDTH), lambda b: (0, 0)),
                   pl.BlockSpec((SGU_GROUPS, BLOCK, BLOCK), lambda b: (0, 0, 0)),
                   pl.BlockSpec((BLOCK, LANES), lambda b: (0, 0))],
        out_shape=[jax.ShapeDtypeStruct((T, 2 * SGU_WIDTH), BF16), jax.ShapeDtypeStruct((1, SGU_WIDTH), F32),
                   jax.ShapeDtypeStruct((SGU_GROUPS, BLOCK, BLOCK), F32), jax.ShapeDtypeStruct((BLOCK, LANES), F32)],
        scratch_shapes=[pltpu.VMEM((BLOCK, SGU_WIDTH), F32), pltpu.VMEM((SGU_GROUPS, BLOCK, BLOCK), F32),
                        pltpu.VMEM((BLOCK, SGU_WIDTH), F32)],
        compiler_params=_params(("arbitrary",)),
    )(proj, dy, gain.reshape(1, SGU_WIDTH), w_s, bias_full, *dep_args)
    return dsuv, dg.reshape(SGU_WIDTH), dw, db[:, :SGU_GROUPS].T


def _merge_fwd(y_att, y_sgu, w_oa, w_ob, proj, *, name, tm=1024, tn=512, deps=()):
    T = y_att.shape[0]

    def body(ya_ref, ys_ref, wa_ref, wb_ref, ga_ref, gb_ref, o_ref):
        pa = _dot_nn(ya_ref[...], wa_ref[...])
        pb = _dot_nn(ys_ref[...], wb_ref[...])
        o_ref[...] = (_sigmoid(ga_ref[...].astype(F32)) * pa + _sigmoid(gb_ref[...].astype(F32)) * pb).astype(BF16)

    act = pl.BlockSpec((tm, ATT_WIDTH), lambda i, j: (i, 0))
    wgt = pl.BlockSpec((ATT_WIDTH, tn), lambda i, j: (0, j))
    body, dep_specs, dep_args = _with_deps(body, 6, deps)
    return pl.pallas_call(
        body, name=name, grid=(T // tm, D_MODEL // tn),
        in_specs=[act, act, wgt, wgt,
                  pl.BlockSpec((tm, tn), lambda i, j: (i, j + COL_GA // tn)),
                  pl.BlockSpec((tm, tn), lambda i, j: (i, j + COL_GB // tn))] + dep_specs,
        out_specs=pl.BlockSpec((tm, tn), lambda i, j: (i, j)),
        out_shape=jax.ShapeDtypeStruct((T, D_MODEL), BF16),
        compiler_params=_params(("parallel", "parallel")),
    )(y_att, y_sgu, w_oa, w_ob, proj, proj, *dep_args)


def _merge_bwd(dx1_bf, w_out, y_att, y_sgu, w_oa, w_ob, proj, *, name, tm=1024, tn=512):
    T = y_att.shape[0]

    def body(dx_ref, wo_ref, ya_ref, ys_ref, wa_ref, wb_ref, ga_ref, gb_ref, dpa_ref, dpb_ref, dga_ref, dgb_ref):
        dm = _dot_nt(dx_ref[...], wo_ref[...])
        pa = _dot_nn(ya_ref[...], wa_ref[...])
        pb = _dot_nn(ys_ref[...], wb_ref[...])
        sa = _sigmoid(ga_ref[...].astype(F32))
        sb = _sigmoid(gb_ref[...].astype(F32))
        dpa_ref[...] = (dm * sa).astype(BF16)
        dpb_ref[...] = (dm * sb).astype(BF16)
        dga_ref[...] = (dm * pa * sa * (1.0 - sa)).astype(BF16)
        dgb_ref[...] = (dm * pb * sb * (1.0 - sb)).astype(BF16)

    act = pl.BlockSpec((tm, ATT_WIDTH), lambda i, j: (i, 0))
    wgt = pl.BlockSpec((ATT_WIDTH, tn), lambda i, j: (0, j))
    out = pl.BlockSpec((tm, tn), lambda i, j: (i, j))
    return pl.pallas_call(
        body, name=name, grid=(T // tm, D_MODEL // tn),
        in_specs=[pl.BlockSpec((tm, D_MODEL), lambda i, j: (i, 0)),
                  pl.BlockSpec((tn, D_MODEL), lambda i, j: (j, 0)),
                  act, act, wgt, wgt,
                  pl.BlockSpec((tm, tn), lambda i, j: (i, j + COL_GA // tn)),
                  pl.BlockSpec((tm, tn), lambda i, j: (i, j + COL_GB // tn))],
        out_specs=[out] * 4,
        out_shape=[jax.ShapeDtypeStruct((T, D_MODEL), BF16)] * 4,
        compiler_params=_params(("parallel", "parallel")),
    )(dx1_bf, w_out, y_att, y_sgu, w_oa, w_ob, proj, proj)


CONV_ROWS = 256
CONV_TN = 256


def _shift_rows(cur, prev8, k):
    rolled = pltpu.roll(cur, k, axis=0)
    head = jnp.where(lax.broadcasted_iota(jnp.int32, prev8.shape, 0) < k, pltpu.roll(prev8, k, axis=0), rolled[:SUBLANES])
    return jnp.concatenate([head, rolled[SUBLANES:]], axis=0)


def _shift_rows_up(cur, next8, k):
    n = cur.shape[0]
    rolled = pltpu.roll(cur, n - k, axis=0)
    tail = jnp.where(lax.broadcasted_iota(jnp.int32, next8.shape, 0) >= SUBLANES - k,
                     pltpu.roll(next8, SUBLANES - k, axis=0), rolled[n - SUBLANES:])
    return jnp.concatenate([rolled[:n - SUBLANES], tail], axis=0)


def _up_conv_fwd(h2, w_up_t, cw_g, cw_v, cb_g, cb_v, *, n_seq, seq, name, deps=()):
    T = n_seq * seq
    tn, rows = CONV_TN, CONV_ROWS

    def body(h_ref, ug_ref, uv_ref, wg_ref, wv_ref, bg_ref, bv_ref, a_ref, zg_ref, zv_ref, cg_ref, cv_ref):
        def conv(cur, prev8, w_ref, b_ref):
            z1 = _shift_rows(cur, prev8, 1)
            z2 = _shift_rows(cur, prev8, 2)
            return b_ref[...] + w_ref[0:1, :] * z2 + w_ref[1:2, :] * z1 + w_ref[2:3, :] * cur

        def step(s, prev):
            r0 = pl.multiple_of(s * rows, rows)
            h = h_ref[pl.ds(r0, rows), :]
            zg = _dot_nt(h, ug_ref[...])
            zv = _dot_nt(h, uv_ref[...])
            zg_ref[pl.ds(r0, rows), :] = zg.astype(ACT_DTYPE)
            zv_ref[pl.ds(r0, rows), :] = zv.astype(ACT_DTYPE)
            g = conv(zg, prev[0], wg_ref, bg_ref)
            v = conv(zv, prev[1], wv_ref, bv_ref)
            a_ref[pl.ds(r0, rows), :] = (g * _sigmoid(g) * v).astype(BF16)
            cg_ref[pl.ds(r0, rows), :] = g.astype(ACT_DTYPE)
            cv_ref[pl.ds(r0, rows), :] = v.astype(ACT_DTYPE)
            return zg[rows - SUBLANES:], zv[rows - SUBLANES:]

        start = jnp.zeros((SUBLANES, tn), F32)
        lax.fori_loop(0, seq // rows, step, (start, start))

    zs = pl.BlockSpec((seq, tn), lambda b, j: (b, j))
    ws = pl.BlockSpec((3, tn), lambda b, j: (0, j))
    bs = pl.BlockSpec((1, tn), lambda b, j: (0, j))
    body, dep_specs, dep_args = _with_deps(body, 7, deps)
    return pl.pallas_call(
        body, name=name, grid=(n_seq, D_FF // tn),
        in_specs=[pl.BlockSpec((seq, D_MODEL), lambda b, j: (b, 0)),
                  pl.BlockSpec((tn, D_MODEL), lambda b, j: (j, 0)),
                  pl.BlockSpec((tn, D_MODEL), lambda b, j: (j + D_FF // tn, 0)), ws, ws, bs, bs] + dep_specs,
        out_specs=[zs] * 5,
        out_shape=[jax.ShapeDtypeStruct((T, D_FF), BF16)] + [jax.ShapeDtypeStruct((T, D_FF), ACT_DTYPE)] * 4,
        compiler_params=_params(("parallel", "parallel")),
    )(h2, w_up_t, w_up_t, cw_g, cw_v, cb_g.reshape(1, D_FF), cb_v.reshape(1, D_FF), *dep_args)


def _conv_bwd(z_g, z_v, c_g, c_v, dx2_bf, w_down, cw_g, cw_v, *, n_seq, seq, name):
    T = n_seq * seq
    tn, rows = CONV_TN, CONV_ROWS
    n_steps = seq // rows

    def body(zg_ref, zv_ref, cg_ref, cv_ref, dx_ref, wd_ref, wg_ref, wv_ref,
             dzg_ref, dzv_ref, dwg_ref, dwv_ref, dbg_ref, dbv_ref, dcg_ref, dcv_ref):
        def colsum(x):
            return jnp.sum(x, axis=0, keepdims=True)

        def grads(s, accs):
            r0 = pl.multiple_of(s * rows, rows)
            g = cg_ref[pl.ds(r0, rows), :].astype(F32)
            v = cv_ref[pl.ds(r0, rows), :].astype(F32)
            sg = _sigmoid(g)
            dav = _dot_nt(dx_ref[pl.ds(r0, rows), :], wd_ref[...])
            dcg = dav * v * (sg * (1.0 + g * (1.0 - sg)))
            dcv = dav * (g * sg)
            dcg_ref[pl.ds(r0, rows), :] = dcg
            dcv_ref[pl.ds(r0, rows), :] = dcv
            return accs[0] + colsum(dcg), accs[1] + colsum(dcv)

        zero = jnp.zeros((1, tn), F32)
        db = lax.fori_loop(0, n_steps, grads, (zero, zero))

        def back(s, accs):
            r0 = pl.multiple_of(s * rows, rows)
            last = s == n_steps - 1
            rn = pl.multiple_of(jnp.minimum(r0 + rows, seq - SUBLANES), SUBLANES)
            new = []
            for half, (dc_ref, w_ref, dz_ref, z_ref) in enumerate(((dcg_ref, wg_ref, dzg_ref, zg_ref),
                                                                   (dcv_ref, wv_ref, dzv_ref, zv_ref))):
                cur = dc_ref[pl.ds(r0, rows), :]
                nxt = jnp.where(last, 0.0, dc_ref[pl.ds(rn, SUBLANES), :])
                u1, u2 = _shift_rows_up(cur, nxt, 1), _shift_rows_up(cur, nxt, 2)
                dz_ref[pl.ds(r0, rows), :] = (w_ref[2:3, :] * cur + w_ref[1:2, :] * u1 + w_ref[0:1, :] * u2).astype(BF16)
                z = z_ref[pl.ds(r0, rows), :].astype(F32)
                new += [accs[3 * half] + colsum(u2 * z), accs[3 * half + 1] + colsum(u1 * z),
                        accs[3 * half + 2] + colsum(cur * z)]
            return tuple(new)

        dw = lax.fori_loop(0, n_steps, back, (zero,) * 6)
        first_seq = pl.program_id(1) == 0

        @pl.when(first_seq)
        def _():
            dwg_ref[...] = jnp.concatenate(dw[0:3], axis=0)
            dwv_ref[...] = jnp.concatenate(dw[3:6], axis=0)
            dbg_ref[...], dbv_ref[...] = db

        @pl.when(jnp.logical_not(first_seq))
        def _():
            dwg_ref[...] += jnp.concatenate(dw[0:3], axis=0)
            dwv_ref[...] += jnp.concatenate(dw[3:6], axis=0)
            dbg_ref[...] += db[0]
            dbv_ref[...] += db[1]

    zs = pl.BlockSpec((seq, tn), lambda j, b: (b, j))
    ws = pl.BlockSpec((3, tn), lambda j, b: (0, j))
    bs = pl.BlockSpec((1, tn), lambda j, b: (0, j))
    outs = pl.pallas_call(
        body, name=name, grid=(D_FF // tn, n_seq),
        in_specs=[zs] * 4 + [pl.BlockSpec((seq, D_MODEL), lambda j, b: (b, 0)),
                             pl.BlockSpec((tn, D_MODEL), lambda j, b: (j, 0)), ws, ws],
        out_specs=[zs, zs, ws, ws, bs, bs],
        out_shape=[jax.ShapeDtypeStruct((T, D_FF), BF16)] * 2 + [jax.ShapeDtypeStruct((3, D_FF), F32)] * 2
        + [jax.ShapeDtypeStruct((1, D_FF), F32)] * 2,
        scratch_shapes=[pltpu.VMEM((seq, tn), F32), pltpu.VMEM((seq, tn), F32)],
        compiler_params=_params(("parallel", "arbitrary")),
    )(z_g, z_v, c_g, c_v, dx2_bf, w_down, cw_g, cw_v)
    dz_g, dz_v, dw_g, dw_v, db_g, db_v = outs
    return dz_g, dz_v, dw_g, dw_v, db_g.reshape(D_FF), db_v.reshape(D_FF)


def _layer_fwd(x, h, w, sched, tail, *, n_seq, seq, l):
    tag = f"l{l}"
    deps = sched("fwd_start", l, h)
    proj = _mm(h, w["w_in_t"], mode="nt", out_dtype=ACT_DTYPE, rotate=W_IN_ROTATE, name=f"{tag}_proj", deps=deps)
    y_att = _attention_fwd(proj, w["q_norm"], w["k_norm"], w["sinks"], n_seq=n_seq, seq=seq, name=f"{tag}_att")
    deps = sched("fwd_att", l, y_att)
    y_sgu = _sgu_fwd(proj, w["sgu_norm"], w["w_s"], w["bias_full"], n_seq=n_seq, seq=seq, name=f"{tag}_sgu")
    merged = _merge_fwd(y_att, y_sgu, w["w_oa"], w["w_ob"], proj, name=f"{tag}_merge", deps=deps)
    x1, h2 = _mm_rows(merged, w["w_out"], mode="nn", fn=_residual_then_norm, out_dtypes=(F32, BF16), rows=(x,),
                      vecs=(w["ffn_norm"],), name=f"{tag}_out")
    deps = sched("fwd_mixer_done", l, x1)
    a, z_g, z_v, c_g, c_v = _up_conv_fwd(h2, w["w_up_t"], w["cw_g"], w["cw_v"], w["cb_g"], w["cb_v"], n_seq=n_seq,
                                         seq=seq, name=f"{tag}_up_conv", deps=deps)
    deps = sched("fwd_conv", l, a)
    if tail[0] == "norm":
        out = _mm_rows(a, w["w_down"], mode="nn", fn=_residual_then_norm, out_dtypes=(F32, BF16), rows=(x1,),
                       vecs=(tail[1],), name=f"{tag}_down", deps=deps)
    else:
        out = _mm_rows(a, w["w_down"], mode="nn", fn=_residual_then_loss, out_dtypes=(F32, BF16), rows=(x1, tail[1]),
                       reduce=True, name=f"{tag}_down", deps=deps)
    saved = dict(x=x, h=h, proj=proj, y_att=y_att, y_sgu=y_sgu, merged=merged, x1=x1, h2=h2, z_g=z_g, z_v=z_v,
                 c_g=c_g, c_v=c_v, a=a)
    return out, saved


def _layer_bwd(dx2, dx2_bf, w, s, sched, deps, *, n_seq, seq, l):
    tag = f"l{l}b"
    g = {}
    g["w_down"] = _mm(s["a"], dx2_bf, mode="tn", out_dtype=F32, name=f"{tag}_dw_down", deps=deps)
    dz_g, dz_v, g["cw_g"], g["cw_v"], g["cb_g"], g["cb_v"] = _conv_bwd(
        s["z_g"], s["z_v"], s["c_g"], s["c_v"], dx2_bf, w["w_down"], w["cw_g"], w["cw_v"], n_seq=n_seq, seq=seq,
        name=f"{tag}_conv")
    dw_up_t = _mm(dz_g, s["h2"], mode="tn", out_dtype=F32, out_rows=(0, 2 * D_FF), name=f"{tag}_dw_up_g")
    g["w_up_t"] = _mm(dz_v, s["h2"], mode="tn", out_dtype=F32, out_rows=(D_FF, 2 * D_FF), out_prev=dw_up_t,
                      name=f"{tag}_dw_up_v")
    deps = sched("bwd_ffn_grads", l, dz_v, g)
    dx1, dx1_bf, dgain = _mm_rows((dz_g, dz_v), w["w_up_t"], mode="nn", fn=_rms_bwd_rows, out_dtypes=(F32, BF16),
                                  rows=(s["x1"], dx2), vecs=(w["ffn_norm"],), reduce=True, a_at=(0, D_FF),
                                  name=f"{tag}_dh2", deps=deps)
    g["ffn_norm"] = dgain.reshape(D_MODEL)
    dpa, dpb, dga, dgb = _merge_bwd(dx1_bf, w["w_out"], s["y_att"], s["y_sgu"], w["w_oa"], w["w_ob"], s["proj"],
                                    name=f"{tag}_merge")
    deps = sched("bwd_merge", l, dpa)
    g["w_out"] = _mm(s["merged"], dx1_bf, mode="tn", out_dtype=F32, name=f"{tag}_dw_out",
                     deps=deps)
    dy_att = _mm(dpa, w["w_oa"], mode="nt", out_dtype=BF16, name=f"{tag}_dy_att")
    dy_sgu = _mm(dpb, w["w_ob"], mode="nt", out_dtype=F32, name=f"{tag}_dy_sgu")
    g["w_oa"] = _mm(s["y_att"], dpa, mode="tn", out_dtype=F32, name=f"{tag}_dw_oa")
    g["w_ob"] = _mm(s["y_sgu"], dpb, mode="tn", out_dtype=F32, name=f"{tag}_dw_ob")
    deps = sched("bwd_out_grads", l, dy_att, g)
    dqkv, g["q_norm"], g["k_norm"], g["sinks"] = _attention_bwd(
        s["proj"], dy_att, w["q_norm"], w["k_norm"], w["sinks"], n_seq=n_seq, seq=seq, name=f"{tag}_att", deps=deps)
    deps = sched("bwd_att", l, dqkv)
    dsuv, g["sgu_norm"], g["w_s"], g["b_s"] = _sgu_bwd(
        s["proj"], dy_sgu, w["sgu_norm"], w["w_s"], w["bias_full"], n_seq=n_seq, seq=seq, name=f"{tag}_sgu", deps=deps)
    dproj = (dsuv, dga, dgb, dqkv)
    at = (QKV_WIDTH, QKV_WIDTH + 2 * SGU_WIDTH, QKV_WIDTH + 2 * SGU_WIDTH + D_MODEL, 0)
    g["w_in_t"] = _mm_tn_parts(dproj, at, s["h"], name=f"{tag}_dw_in")
    deps = sched("bwd_w_in_grad", l, dqkv, g)
    dx, dx_bf, dgain = _mm_rows(dproj, w["w_in_t"], mode="nn", fn=_rms_bwd_rows, out_dtypes=(F32, BF16),
                                rows=(s["x"], dx1), vecs=(w["mix_norm"],), reduce=True, a_at=at,
                                name=f"{tag}_dh", deps=deps)
    g["mix_norm"] = dgain.reshape(D_MODEL)
    return dx, dx_bf, g, sched("bwd_dh", l, dx)


def _local_step(x, target, weights, sched, *, n_seq, seq):
    depth = len(weights)
    saved = []
    h = _rms_fwd(x, weights[0]["mix_norm"], name="l0_mix_norm", deps=sched("begin", 0, x))
    for l in range(depth):
        tail = ("norm", weights[l + 1]["mix_norm"]) if l + 1 < depth else ("loss", target)
        out, s = _layer_fwd(x, h, weights[l], sched, tail, n_seq=n_seq, seq=seq, l=l)
        saved.append(s)
        if l + 1 < depth:
            x, h = out
    dy, dy_bf, loss_cols = out
    grads = [None] * depth
    deps = ()
    for l in reversed(range(depth)):
        dy, dy_bf, grads[l], deps = _layer_bwd(dy, dy_bf, weights[l], saved[l], sched, deps, n_seq=n_seq, seq=seq, l=l)
    return jnp.sum(loss_cols), dy, grads, deps


W_IN_SHARD = IN_WIDTH // N_DEV
W_UP_SHARD = 2 * D_FF // N_DEV
COL_MOVE_ROWS = 256


def _w_o_moves():
    return tuple((j, 0, LANES, 0, j * LANES) for j in range(N_DEV))


def _disassemble(mats, w, moves, *, name):
    R = mats[0].shape[0]
    tr = min(R, COL_MOVE_ROWS)
    n = len(mats)

    def body(*refs):
        m_refs, o_ref = refs[:n], refs[n]
        for j, lo, hi, which, at in moves:
            o_ref[j, :, lo:hi] = m_refs[which][:, at:at + hi - lo]

    return pl.pallas_call(
        body, name=name, grid=(R // tr,),
        in_specs=[pl.BlockSpec((tr, m.shape[1]), lambda i: (i, 0)) for m in mats],
        out_specs=pl.BlockSpec((N_DEV, tr, w), lambda i: (0, i, 0)),
        out_shape=jax.ShapeDtypeStruct((N_DEV, R, w), mats[0].dtype),
        compiler_params=_params(("parallel",)),
    )(*mats)


def _my_place():
    return lax.axis_index("x"), lax.axis_index("y"), lax.axis_index("c")


def _gathered_shape(shape, kind):
    r, c = shape
    return {"blocks": (N_DEV, r, c), "rows": (N_DEV * r, c), "cols": (r, N_DEV * c)}[kind]


def _gather_window(ref, kind, shape, j):
    r, c = shape
    if kind == "blocks":
        return ref.at[j]
    if kind == "rows":
        return ref.at[pl.ds(pl.multiple_of(j * r, r), r), :]
    return ref.at[:, pl.ds(pl.multiple_of(j * c, c), c)]


def _gather(srcs, kinds, *, name):
    n = len(srcs)
    shapes = [s.shape for s in srcs]
    per = 7

    def body(*refs):
        src_refs, dst_refs = refs[:n], refs[n:2 * n]
        send_sems, recv_sems, local_sems = refs[2 * n:]
        x, y, c = _my_place()
        me, sibling = (x, y, c), (x, y, 1 - c)
        chips = [(1 - x, y), (x, 1 - y), (1 - x, 1 - y)]

        def at(i, px, py, pc):
            return _gather_window(dst_refs[i], kinds[i], shapes[i], 4 * px + 2 * py + pc)

        def copy(i, k, block, to, src=None):
            return pltpu.make_async_remote_copy(
                src_ref=at(i, *block) if src is None else src, dst_ref=at(i, *block),
                send_sem=send_sems.at[per * i + k], recv_sem=recv_sems.at[per * i + k], device_id=to, device_id_type=MESH)

        mine = [pltpu.make_async_copy(src_refs[i], at(i, *me), local_sems.at[i]) for i in range(n)]
        for cp in mine:
            cp.start()
        started = []
        for i in range(n):
            first = [copy(i, 0, me, sibling, src=src_refs[i])]
            first += [copy(i, 1 + j, me, (*chip, c), src=src_refs[i]) for j, chip in enumerate(chips)]
            for cp in first:
                cp.start()
            started += first
        for i in range(n):
            for j, chip in enumerate(chips):
                copy(i, 1 + j, (*chip, c), me).wait_recv()
                fwd = copy(i, 4 + j, (*chip, c), sibling)
                fwd.start()
                started.append(fwd)
        for i in range(n):
            copy(i, 0, sibling, me).wait_recv()
            for j, chip in enumerate(chips):
                copy(i, 4 + j, (*chip, 1 - c), me).wait_recv()
        for cp in started:
            cp.wait_send()
        for cp in mine:
            cp.wait()

    return pl.pallas_call(
        body, name=name,
        out_shape=[jax.ShapeDtypeStruct(_gathered_shape(s.shape, k), s.dtype) for s, k in zip(srcs, kinds)],
        in_specs=[ANY] * n, out_specs=[ANY] * n,
        scratch_shapes=[pltpu.SemaphoreType.DMA((per * n,)), pltpu.SemaphoreType.DMA((per * n,)),
                        pltpu.SemaphoreType.DMA((n,))],
    )(*srcs)


HBM = pl.BlockSpec(memory_space=pltpu.HBM)
SEM = pl.BlockSpec(memory_space=pltpu.SEMAPHORE)
TOKEN = jax.ShapeDtypeStruct((SUBLANES, LANES), F32)
TOKEN_SPEC = pl.BlockSpec(memory_space=pltpu.VMEM)
SPLIT_PARAMS = pltpu.CompilerParams(has_side_effects=pltpu.SideEffectType.DATAFLOW_SIDE_EFFECTING)


def _in_hbm(x):
    return pltpu.with_memory_space_constraint(x, pltpu.HBM)


def _hbm_like(shape, dtype):
    return pltpu.HBM(shape, dtype)


def _place_own(shards, kinds, dtypes, *, name, deps=()):
    n = len(shards)
    shapes = [s.shape for s in shards]

    def body(*refs):
        s_refs, land_refs, bufs, sems = refs[:n], refs[n:2 * n], refs[2 * n:3 * n], refs[3 * n]
        x, y, c = _my_place()
        copies = []
        for i in range(n):
            bufs[i][...] = s_refs[i][...].astype(dtypes[i])
            copies.append(pltpu.make_async_copy(
                bufs[i], _gather_window(land_refs[i], kinds[i], shapes[i], 4 * x + 2 * y + c), sems.at[i]))
        for cp in copies:
            cp.start()
        for cp in copies:
            cp.wait()

    body, dep_specs, dep_args = _with_deps(body, n, deps)
    return pl.pallas_call(
        body, name=name,
        out_shape=[jax.ShapeDtypeStruct(_gathered_shape(s, k), d) for s, k, d in zip(shapes, kinds, dtypes)],
        in_specs=[pl.BlockSpec(memory_space=pltpu.VMEM)] * n + dep_specs, out_specs=[ANY] * n,
        scratch_shapes=[pltpu.VMEM(s, d) for s, d in zip(shapes, dtypes)] + [pltpu.SemaphoreType.DMA((n,))],
        compiler_params=_params(),
    )(*shards, *dep_args)


def _gather_start(lands, kinds, shapes, after=(), *, name):
    n = len(lands)
    n_after = len(after)

    def body(*refs):
        land_refs = refs[:n]
        send_sems, recv_sems = refs[n + n_after], refs[n + n_after + 1]
        x, y, c = _my_place()
        targets = [(x, y, 1 - c), (1 - x, y, c), (x, 1 - y, c), (1 - x, 1 - y, c)]
        for i in range(n):
            own = _gather_window(land_refs[i], kinds[i], shapes[i], 4 * x + 2 * y + c)
            for k, to in enumerate(targets):
                pltpu.make_async_remote_copy(
                    src_ref=own, dst_ref=own, send_sem=send_sems.at[4 * i + k], recv_sem=recv_sems.at[4 * i + k],
                    device_id=to, device_id_type=MESH).start()
        refs[-1][...] = jnp.zeros_like(refs[-1])

    outs = pl.pallas_call(
        body, name=name,
        out_shape=[pltpu.SemaphoreType.DMA((4 * n,)), pltpu.SemaphoreType.DMA((4 * n,))]
        + [_hbm_like(a.shape, a.dtype) for a in lands] + [TOKEN],
        in_specs=[HBM] * n + [ANY] * n_after, out_specs=[SEM, SEM] + [HBM] * n + [TOKEN_SPEC],
        input_output_aliases={i: 2 + i for i in range(n)},
        compiler_params=SPLIT_PARAMS,
    )(*[_in_hbm(a) for a in lands], *after)
    return outs[0], outs[1], outs[2:2 + n], outs[-1]


def _gather_forward(recv_sems, lands, kinds, shapes, after, *, name):
    n = len(lands)

    def body(*refs):
        recv_ref, land_refs = refs[0], refs[1:1 + n]
        fwd_send, fwd_recv = refs[2 + n], refs[3 + n]
        token = refs[-1]
        x, y, c = _my_place()
        chips = [(1 - x, y), (x, 1 - y), (1 - x, 1 - y)]
        for i in range(n):
            for j, (px, py) in enumerate(chips):
                block = _gather_window(land_refs[i], kinds[i], shapes[i], 4 * px + 2 * py + c)
                pltpu.make_async_remote_copy(
                    src_ref=block, dst_ref=block, send_sem=fwd_send.at[3 * i + j], recv_sem=recv_ref.at[4 * i + 1 + j],
                    device_id=(px, py, c), device_id_type=MESH).wait_recv()
                pltpu.make_async_remote_copy(
                    src_ref=block, dst_ref=block, send_sem=fwd_send.at[3 * i + j], recv_sem=fwd_recv.at[3 * i + j],
                    device_id=(x, y, 1 - c), device_id_type=MESH).start()
        token[...] = jnp.zeros_like(token)

    outs = pl.pallas_call(
        body, name=name,
        out_shape=[pltpu.SemaphoreType.DMA((3 * n,)), pltpu.SemaphoreType.DMA((3 * n,))]
        + [_hbm_like(a.shape, a.dtype) for a in lands] + [TOKEN],
        in_specs=[SEM] + [HBM] * n + [ANY], out_specs=[SEM, SEM] + [HBM] * n + [TOKEN_SPEC],
        input_output_aliases={1 + i: 2 + i for i in range(n)},
        compiler_params=SPLIT_PARAMS,
    )(recv_sems, *lands, after)
    return outs[0], outs[1], outs[2:2 + n], outs[-1]


def _gather_finish(send_sems, recv_sems, fwd_send, fwd_recv, lands, kinds, shapes, after, *, name):
    n = len(lands)

    def body(*refs):
        send_ref, recv_ref, fsend_ref, frecv_ref = refs[:4]
        land_refs = refs[4:4 + n]
        x, y, c = _my_place()
        chips = [(1 - x, y), (x, 1 - y), (1 - x, 1 - y)]
        sibling = (x, y, 1 - c)
        for i in range(n):
            def window(j):
                return _gather_window(land_refs[i], kinds[i], shapes[i], j)

            mine, theirs = window(4 * x + 2 * y + c), window(4 * x + 2 * y + (1 - c))
            pltpu.make_async_remote_copy(src_ref=mine, dst_ref=theirs, send_sem=send_ref.at[4 * i],
                                         recv_sem=recv_ref.at[4 * i], device_id=sibling, device_id_type=MESH).wait_recv()
            for j, (px, py) in enumerate(chips):
                block = window(4 * px + 2 * py + (1 - c))
                pltpu.make_async_remote_copy(src_ref=block, dst_ref=block, send_sem=fsend_ref.at[3 * i + j],
                                             recv_sem=frecv_ref.at[3 * i + j], device_id=sibling,
                                             device_id_type=MESH).wait_recv()
            for k in range(4):
                pltpu.make_async_remote_copy(src_ref=mine, dst_ref=mine, send_sem=send_ref.at[4 * i + k],
                                             recv_sem=recv_ref.at[4 * i + k], device_id=sibling,
                                             device_id_type=MESH).wait_send()
            for j, (px, py) in enumerate(chips):
                block = window(4 * px + 2 * py + c)
                pltpu.make_async_remote_copy(src_ref=block, dst_ref=block, send_sem=fsend_ref.at[3 * i + j],
                                             recv_sem=frecv_ref.at[3 * i + j], device_id=sibling,
                                             device_id_type=MESH).wait_send()

    return pl.pallas_call(
        body, name=name,
        out_shape=[_hbm_like(a.shape, a.dtype) for a in lands],
        in_specs=[SEM] * 4 + [HBM] * n + [ANY], out_specs=[HBM] * n,
        input_output_aliases={4 + i: i for i in range(n)},
        compiler_params=SPLIT_PARAMS,
    )(send_sems, recv_sems, fwd_send, fwd_recv, *lands, after)


def _pair_plan(src_ref, land_ref, x, y, c):
    return [(src_ref.at[2 * k + (1 - c)], land_ref.at[k], (x, y, 1 - c)) for k in range(N_CHIPS)]


def _chip_plan(src_ref, land_ref, x, y, c):
    chips = [(1 - x, y), (x, 1 - y), (1 - x, 1 - y)]
    return [(src_ref.at[2 * px + py], land_ref.at[k], (px, py, c)) for k, (px, py) in enumerate(chips)]


def _exchange_copies(plan, per, src_refs, land_refs, send_sems, recv_sems):
    x, y, c = _my_place()
    copies = []
    for i, (s_ref, l_ref) in enumerate(zip(src_refs, land_refs)):
        for q, (src, dst, to) in enumerate(plan(s_ref, l_ref, x, y, c)):
            copies.append(pltpu.make_async_remote_copy(
                src_ref=src, dst_ref=dst, send_sem=send_sems.at[per * i + q], recv_sem=recv_sems.at[per * i + q],
                device_id=to, device_id_type=MESH))
    return copies


def _exchange_start(srcs, plan, per, *, name):
    n = len(srcs)

    def body(*refs):
        src_refs, land_refs = refs[:n], refs[n:2 * n]
        send_sems, recv_sems = refs[2 * n], refs[2 * n + 1]
        for cp in _exchange_copies(plan, per, src_refs, land_refs, send_sems, recv_sems):
            cp.start()
        refs[-1][...] = jnp.zeros_like(refs[-1])

    lands = [lax.empty((per,) + s.shape[1:], s.dtype) for s in srcs]
    outs = pl.pallas_call(
        body, name=name,
        out_shape=[pltpu.SemaphoreType.DMA((per * n,)), pltpu.SemaphoreType.DMA((per * n,))]
        + [_hbm_like(s.shape, s.dtype) for s in srcs] + [_hbm_like(a.shape, a.dtype) for a in lands] + [TOKEN],
        in_specs=[HBM] * (2 * n), out_specs=[SEM, SEM] + [HBM] * (2 * n) + [TOKEN_SPEC],
        input_output_aliases={i: 2 + i for i in range(2 * n)},
        compiler_params=SPLIT_PARAMS,
    )(*[_in_hbm(s) for s in srcs], *[_in_hbm(a) for a in lands])
    return outs[0], outs[1], outs[2:2 + n], outs[2 + n:2 + 2 * n], outs[-1]


def _exchange_wait(send_sems, recv_sems, srcs, lands, plan, per, after, *, name):
    n = len(srcs)
    after = list(after) if isinstance(after, (list, tuple)) else [after]

    def body(*refs):
        send_ref, recv_ref = refs[0], refs[1]
        src_refs, land_refs = refs[2:2 + n], refs[2 + n:2 + 2 * n]
        copies = _exchange_copies(plan, per, src_refs, land_refs, send_ref, recv_ref)
        for cp in copies:
            cp.wait_recv()
        for cp in copies:
            cp.wait_send()

    outs = pl.pallas_call(
        body, name=name,
        out_shape=[_hbm_like(s.shape, s.dtype) for s in srcs] + [_hbm_like(a.shape, a.dtype) for a in lands],
        in_specs=[SEM, SEM] + [HBM] * (2 * n) + [ANY] * len(after), out_specs=[HBM] * (2 * n),
        input_output_aliases={2 + i: i for i in range(2 * n)},
        compiler_params=SPLIT_PARAMS,
    )(send_sems, recv_sems, *srcs, *lands, *after)
    return outs[:n], outs[n:]


REDUCE_BLOCK_BYTES = 2 << 20


def _row_tile(r, c):
    row_bytes = 4 * (-(-c // LANES) * LANES)
    best = r
    for d in range(SUBLANES, r, SUBLANES):
        if r % d == 0 and d * row_bytes <= REDUCE_BLOCK_BYTES:
            best = d
    return best if r * row_bytes > REDUCE_BLOCK_BYTES else r


def _reduce_pair_sum(blocked, recv, place, wire_dtype, *, name):
    _, r, c = blocked.shape
    tr = _row_tile(r, c)

    def body(place_ref, g_ref, r_ref, own_ref, send_ref):
        s = g_ref[...] + r_ref[...]
        send_ref[...] = s.astype(wire_dtype)

        @pl.when(pl.program_id(1) == place_ref[1])
        def _():
            own_ref[...] = s

    return pl.pallas_call(
        body, name=name,
        grid_spec=pltpu.PrefetchScalarGridSpec(
            num_scalar_prefetch=1, grid=(r // tr, N_CHIPS),
            in_specs=[pl.BlockSpec((None, None, tr, c), lambda i, k, place_ref: (k, place_ref[0], i, 0)),
                      pl.BlockSpec((None, tr, c), lambda i, k, place_ref: (k, i, 0))],
            out_specs=[pl.BlockSpec((tr, c), lambda i, k, place_ref: (i, 0)),
                       pl.BlockSpec((None, tr, c), lambda i, k, place_ref: (k, i, 0))]),
        out_shape=[jax.ShapeDtypeStruct((r, c), F32), jax.ShapeDtypeStruct((N_CHIPS, r, c), wire_dtype)],
        compiler_params=_params(("parallel", "arbitrary")),
    )(place, blocked.reshape(N_CHIPS, 2, r, c), recv)


def _chip_sum(own_ref, r_ref):
    return ((own_ref[...] + r_ref[0].astype(F32)) + r_ref[1].astype(F32)) + r_ref[2].astype(F32)


def _reduce_chip_sum(own, recv, *, name):
    r, c = own.shape
    tr = _row_tile(r, c)

    def body(own_ref, r_ref, o_ref):
        o_ref[...] = _chip_sum(own_ref, r_ref)

    return pl.pallas_call(
        body, name=name, grid=(r // tr,),
        in_specs=[pl.BlockSpec((tr, c), lambda i: (i, 0)), pl.BlockSpec((N_CHIPS - 1, tr, c), lambda i: (0, i, 0))],
        out_specs=pl.BlockSpec((tr, c), lambda i: (i, 0)),
        out_shape=jax.ShapeDtypeStruct((r, c), F32),
        compiler_params=_params(("parallel",)),
    )(own, recv)


def _adamw_math(w, g, m, v):
    nm = ADAM_B1 * m + (1.0 - ADAM_B1) * g
    nv = ADAM_B2 * v + (1.0 - ADAM_B2) * (g * g)
    m_hat = nm / (1.0 - ADAM_B1 ** ADAM_STEP)
    v_hat = nv / (1.0 - ADAM_B2 ** ADAM_STEP)
    return -ADAM_LR * (m_hat / (jnp.sqrt(v_hat) + ADAM_EPS) + ADAM_WD * w), nm, nv


def _adamw(w, g, m, v, *, name):
    shape = w.shape
    C = shape[-1]
    R = math.prod(shape[:-1])
    tr = _row_tile(R, C)

    def body(w_ref, g_ref, m_ref, v_ref, d_ref, nm_ref, nv_ref):
        d_ref[...], nm_ref[...], nv_ref[...] = _adamw_math(w_ref[...], g_ref[...], m_ref[...], v_ref[...])

    spec = pl.BlockSpec((tr, C), lambda i: (i, 0))
    outs = pl.pallas_call(
        body, name=name, grid=(R // tr,),
        in_specs=[spec] * 4, out_specs=[spec] * 3,
        out_shape=[jax.ShapeDtypeStruct((R, C), F32)] * 3,
        compiler_params=_params(("parallel",)),
    )(*[a.reshape(R, C) for a in (w, g, m, v)])
    return tuple(o.reshape(shape) for o in outs)


def _reduce_adamw(own, recv, w, m, v, layer, prev, *, name):
    r, c = own.shape
    tr = _row_tile(r, c)
    n_prev = 0 if prev is None else len(prev)

    def body(own_ref, r_ref, w_ref, m_ref, v_ref, *rest):
        g_ref, d_ref, nm_ref, nv_ref = rest[n_prev:]
        g = _chip_sum(own_ref, r_ref)
        g_ref[...] = g
        d_ref[...], nm_ref[...], nv_ref[...] = _adamw_math(w_ref[...], g, m_ref[...], v_ref[...])

    slot = pl.BlockSpec((None, tr, c), lambda i: (layer, i, 0))
    return pl.pallas_call(
        body, name=name, grid=(r // tr,),
        in_specs=[pl.BlockSpec((tr, c), lambda i: (i, 0)), pl.BlockSpec((N_CHIPS - 1, tr, c), lambda i: (0, i, 0)),
                  slot, slot, slot] + [ANY] * n_prev,
        out_specs=[slot] * 4,
        out_shape=[jax.ShapeDtypeStruct((DEPTH, r, c), F32)] * 4,
        input_output_aliases={5 + k: k for k in range(n_prev)},
        compiler_params=_params(("parallel",)),
    )(own, recv, w, m, v, *(prev or ()))


REPLICATED = (("mix_norm", (D_MODEL,)), ("q_norm", (HEAD_DIM,)), ("k_norm", (HEAD_DIM,)), ("sinks", (N_Q_HEADS,)),
              ("sgu_norm", (SGU_WIDTH,)), ("w_s", (SGU_GROUPS, BLOCK, BLOCK)), ("b_s", (SGU_GROUPS, BLOCK)),
              ("ffn_norm", (D_MODEL,)), ("conv_b", (2 * D_FF,)))
TRANSPOSED = ("w_in", "w_up")
SHARDED = (("w_in", "rows"), ("w_oa", "cols"), ("w_ob", "cols"), ("w_out", "rows"), ("w_up", "rows"),
           ("conv_w", "blocks"), ("w_down", "rows"))
WEIGHT_ORDER = ("mix_norm", "w_in", "q_norm", "k_norm", "sinks", "sgu_norm", "w_s", "b_s", "w_oa", "w_ob", "w_out",
                "ffn_norm", "w_up", "conv_w", "conv_b", "w_down")
MIXER_WEIGHTS = ["w_in", "w_oa", "w_ob", "w_out"]
FFN_WEIGHTS = ["w_up", "conv_w", "w_down"]


def _small_layout():
    segs, off = {}, 0
    for l in range(DEPTH):
        for name, shape in REPLICATED:
            n = math.prod(shape)
            segs[(l, name)] = (off, n)
            off += n
    per_dev = -(-off // (N_DEV * SUBLANES * LANES)) * SUBLANES * LANES
    return segs, off, per_dev


def _pack_small(grads, loss_part):
    ssegs, total, per_dev = _small_layout()
    flat = jnp.concatenate([grads[l][name].reshape(-1) for (l, name) in ssegs] + [loss_part.reshape(1)])
    return jnp.pad(flat, (0, N_DEV * per_dev - total - 1)).reshape(N_DEV, per_dev // LANES, LANES)


def _unpack_small(gathered):
    ssegs, total, _ = _small_layout()
    flat = gathered.reshape(-1)
    shapes = dict(REPLICATED)
    small = {name: jnp.stack([flat[ssegs[(l, name)][0]:ssegs[(l, name)][0] + ssegs[(l, name)][1]].reshape(shapes[name])
                              for l in range(DEPTH)]) for name, _ in REPLICATED}
    return small, flat[total]


def kernel(x, mix_norm, w_in, q_norm, k_norm, sinks, sgu_norm, w_s, b_s, w_oa, w_ob, w_out, ffn_norm, w_up, conv_w, conv_b, w_down, loss_target, m_mix_norm, m_w_in, m_q_norm, m_k_norm, m_sinks, m_sgu_norm, m_w_s, m_b_s, m_w_oa, m_w_ob, m_w_out, m_ffn_norm, m_w_up, m_conv_w, m_conv_b, m_w_down, v_mix_norm, v_w_in, v_q_norm, v_k_norm, v_sinks, v_sgu_norm, v_w_s, v_b_s, v_w_oa, v_w_ob, v_w_out, v_ffn_norm, v_w_up, v_conv_w, v_conv_b, v_w_down):
    W = dict(mix_norm=mix_norm, w_in=w_in, q_norm=q_norm, k_norm=k_norm, sinks=sinks, sgu_norm=sgu_norm, w_s=w_s, b_s=b_s,
             w_oa=w_oa, w_ob=w_ob, w_out=w_out, ffn_norm=ffn_norm, w_up=w_up, conv_w=conv_w, conv_b=conv_b, w_down=w_down)
    M = dict(mix_norm=m_mix_norm, w_in=m_w_in, q_norm=m_q_norm, k_norm=m_k_norm, sinks=m_sinks, sgu_norm=m_sgu_norm,
             w_s=m_w_s, b_s=m_b_s, w_oa=m_w_oa, w_ob=m_w_ob, w_out=m_w_out, ffn_norm=m_ffn_norm, w_up=m_w_up,
             conv_w=m_conv_w, conv_b=m_conv_b, w_down=m_w_down)
    V = dict(mix_norm=v_mix_norm, w_in=v_w_in, q_norm=v_q_norm, k_norm=v_k_norm, sinks=v_sinks, sgu_norm=v_sgu_norm,
             w_s=v_w_s, b_s=v_b_s, w_oa=v_w_oa, w_ob=v_w_ob, w_out=v_w_out, ffn_norm=v_ffn_norm, w_up=v_w_up,
             conv_w=v_conv_w, conv_b=v_conv_b, w_down=v_w_down)
    n_seq, seq, d_model = x.shape
    tokens = n_seq * seq
    mx, my, mc = _my_place()
    place = jnp.stack([mc, 2 * mx + my]).astype(jnp.int32)
    half = N_DEV // 2
    kind_of = dict(SHARDED)
    for name in TRANSPOSED:
        W[name], M[name], V[name] = (jnp.swapaxes(t[name], 1, 2) for t in (W, M, V))

    gather_groups = [[(0, MIXER_WEIGHTS[0])], [(0, n) for n in MIXER_WEIGHTS[1:]], [(0, n) for n in FFN_WEIGHTS],
                     [(1, n) for n in MIXER_WEIGHTS], [(1, n) for n in FFN_WEIGHTS]]
    started, in_flight = {}, {}
    weights = []
    for l in range(DEPTH):
        w = {name: W[name][l] for name, _ in REPLICATED}
        w["cb_g"], w["cb_v"] = W["conv_b"][l][:D_FF], W["conv_b"][l][D_FF:]
        w["bias_full"] = jnp.repeat(W["b_s"][l].T, SGU_WIDTH // SGU_GROUPS, axis=1)
        weights.append(w)

    def gather_start(gi, after=()):
        shards = [W[name][l] for l, name in gather_groups[gi]]
        kinds = [kind_of[name] for _, name in gather_groups[gi]]
        shapes = [s.shape for s in shards]
        lands = _place_own(shards, kinds, [F32 if name == "conv_w" else BF16 for _, name in gather_groups[gi]],
                           name=f"gather_weights_own_{gi}", deps=after)
        send, recv, lands, token = _gather_start(lands, kinds, shapes, after, name=f"gather_weights_start_{gi}")
        started[gi] = dict(sems=(send, recv), lands=lands, kinds=kinds, shapes=shapes)
        return token

    def gather_forward(gi, after):
        st = started[gi]
        in_flight[gi] = _gather_forward(st["sems"][1], st["lands"], st["kinds"], st["shapes"], after,
                                        name=f"gather_weights_forward_{gi}")
        return in_flight[gi][3]

    def gather_finish(gi, after):
        st = started.pop(gi)
        fwd_send, fwd_recv, lands_g, _ = in_flight.pop(gi)
        whole = _gather_finish(st["sems"][0], st["sems"][1], fwd_send, fwd_recv, lands_g, st["kinds"], st["shapes"], after,
                               name=f"gather_weights_finish_{gi}")
        for (l, name), arr in zip(gather_groups[gi], whole):
            w = weights[l]
            if name in TRANSPOSED:
                w[name + "_t"] = arr
            elif name == "conv_w":
                w["cw_g"] = arr[:half].transpose(1, 0, 2).reshape(3, D_FF)
                w["cw_v"] = arr[half:].transpose(1, 0, 2).reshape(3, D_FF)
            else:
                w[name] = arr

    reduce_state, results = {}, {}
    wire = {"conv_w": F32, "small": F32}

    def reduce_begin(key, names, arrays):
        send, recv, srcs_, lands_, token = _exchange_start(arrays, _pair_plan, N_CHIPS, name=f"reduce_pair_start_{key}")
        reduce_state[key] = dict(names=names, pair=(send, recv, srcs_, lands_))
        return [token]

    def reduce_pair(key, after):
        st = reduce_state[key]
        send, recv, srcs_, lands_ = st.pop("pair")
        blocked_, from_sibling = _exchange_wait(send, recv, srcs_, lands_, _pair_plan, N_CHIPS, after,
                                                name=f"reduce_pair_wait_{key}")
        sums = [_reduce_pair_sum(b, r, place, wire.get(n if isinstance(n, str) else n[1], BF16),
                                 name=f"reduce_pair_sum_{key}_{i}")
                for i, (n, b, r) in enumerate(zip(st["names"], blocked_, from_sibling))]
        st["own"] = [s[0] for s in sums]
        *st["chip"], token = _exchange_start([s[1] for s in sums], _chip_plan, N_CHIPS - 1, name=f"reduce_chip_start_{key}")
        return [token]

    def reduce_end(key, after):
        st = reduce_state.pop(key)
        send, recv, srcs_, lands_ = st["chip"]
        _, from_chips = _exchange_wait(send, recv, srcs_, lands_, _chip_plan, N_CHIPS - 1, after,
                                       name=f"reduce_chip_wait_{key}")
        done = []
        for n, own, got in zip(st["names"], st["own"], from_chips):
            if n == "small":
                results["small"] = _reduce_chip_sum(own, got, name="reduce_chip_sum_small")
            else:
                l, name = n
                results[name] = _reduce_adamw(own, got, W[name], M[name], V[name], l, results.get(name),
                                              name=f"l{l}_reduce_adamw_{name}")
                done.append(results[name][0])
        return done

    def sched(point, l, carry, g=None):
        deps = []
        if point == "begin":
            token = ()
            for gi in range(len(gather_groups)):
                token = [gather_start(gi, token)]
            deps = token
        elif point == "fwd_start" and l == 0:
            gather_finish(0, gather_forward(0, carry))
        elif point == "fwd_att" and l == 0:
            gather_finish(1, gather_forward(1, carry))
            deps = [gather_forward(2, carry)]
        elif point == "fwd_mixer_done" and l == 0:
            gather_finish(2, carry)
        elif point == "fwd_conv" and l == 0:
            deps = [gather_forward(3, carry)]
        elif point == "fwd_start" and l == 1:
            gather_finish(3, carry)
        elif point == "fwd_att" and l == 1:
            deps = [gather_forward(4, carry)]
        elif point == "fwd_mixer_done" and l == 1:
            gather_finish(4, carry)
        elif point == "bwd_ffn_grads":
            conv_w = jnp.concatenate([g[k].reshape(3, half, W_UP_SHARD).transpose(1, 0, 2) for k in ("cw_g", "cw_v")])
            deps = reduce_begin(
                f"l{l}_ffn", [(l, "w_down"), (l, "w_up"), (l, "conv_w")],
                [g["w_down"].reshape(N_DEV, D_FF // N_DEV, D_MODEL),
                 g["w_up_t"].reshape(N_DEV, W_UP_SHARD, D_MODEL), conv_w])
        elif point == "bwd_merge":
            deps = reduce_pair(f"l{l}_ffn", carry)
        elif point == "bwd_out_grads":
            deps = reduce_begin(
                f"l{l}_out", [(l, "w_out"), (l, "w_oa"), (l, "w_ob")],
                [g["w_out"].reshape(N_DEV, D_MODEL // N_DEV, D_MODEL),
                 _disassemble((g["w_oa"],), LANES, _w_o_moves(), name=f"l{l}_split_dw_oa"),
                 _disassemble((g["w_ob"],), LANES, _w_o_moves(), name=f"l{l}_split_dw_ob")])
        elif point == "bwd_att":
            deps = reduce_pair(f"l{l}_out", carry)
        elif point == "bwd_w_in_grad":
            deps = reduce_begin(f"l{l}_in", [(l, "w_in")], [g["w_in_t"].reshape(N_DEV, W_IN_SHARD, D_MODEL)])
        elif point == "bwd_dh":
            deps = reduce_pair(f"l{l}_in", carry)
        return deps

    loss_part, dx, grads, last_deps = _local_step(x.reshape(tokens, d_model), loss_target.reshape(tokens, d_model),
                                                  weights, sched, n_seq=n_seq, seq=seq)
    for g in grads:
        g["conv_b"] = jnp.concatenate([g["cb_g"], g["cb_v"]])
    after = [dx, *last_deps, *reduce_begin("small", ["small"], [_pack_small(grads, loss_part)])]
    for key in [f"l{l}_{part}" for l in reversed(range(DEPTH)) for part in ("ffn", "out", "in")][:-1]:
        after = reduce_end(key, after)
    after = reduce_end("l0_in", after + reduce_pair("small", after))
    reduce_end("small", after)

    G, delta, new_m, new_v = {}, {}, {}, {}
    for name, _ in SHARDED:
        outs = [jnp.swapaxes(o, 1, 2) for o in results[name]] if name in TRANSPOSED else results[name]
        G[name], delta[name], new_m[name], new_v[name] = outs
    small, loss = _unpack_small(_gather([results["small"]], ["blocks"], name="gather_small_grads")[0])
    G.update(small)
    for name, _ in REPLICATED:
        delta[name], new_m[name], new_v[name] = _adamw(W[name], G[name], M[name], V[name], name=f"adamw_{name}")
    return (loss, dx.reshape(n_seq, seq, d_model), *[G[n] for n in WEIGHT_ORDER], *[delta[n] for n in WEIGHT_ORDER],
            *[new_m[n] for n in WEIGHT_ORDER], *[new_v[n] for n in WEIGHT_ORDER])
```

```python
import math

import jax
import jax.numpy as jnp
from jax import lax
from jax.experimental import pallas as pl
from jax.experimental.pallas import tpu as pltpu

F32 = jnp.float32
BF16 = jnp.bfloat16
ACT_DTYPE = BF16
MESH = pl.DeviceIdType.MESH

DEPTH = 2
D_MODEL = 1024
N_Q_HEADS = 8
HEAD_DIM = 64
ATT_WIDTH = 512
KV_WIDTH = 128
BLOCK = 128
SGU_WIDTH = 512
SGU_GROUPS = 8
IN_WIDTH = 3840
D_FF = 2816
NORM_EPS = 1e-6
NEG_INF = -1e30
ATT_SCALE = HEAD_DIM ** -0.5
ALIBI_SLOPES = tuple(2.0 ** (-(h + 1)) for h in range(N_Q_HEADS))
ADAM_LR, ADAM_B1, ADAM_B2, ADAM_EPS, ADAM_WD, ADAM_STEP = 0.001, 0.9, 0.999, 1e-08, 0.01, 10
N_DEV = 8
N_CHIPS = 4

QKV_WIDTH = ATT_WIDTH + 2 * KV_WIDTH
COL_SUV, COL_GA, COL_GB, COL_QKV = 0, 1024, 2048, 3072
W_IN_ROTATE = (1, IN_WIDTH // QKV_WIDTH)

LANES = 128
SUBLANES = 8
VMEM_LIMIT_V7X = 56 * 1024 * 1024
GELU_C = math.sqrt(2.0 / math.pi)
GELU_K = 0.044715
ANY = pl.BlockSpec(memory_space=pl.ANY)


def _params(sem=None):
    return pltpu.CompilerParams(dimension_semantics=sem, vmem_limit_bytes=VMEM_LIMIT_V7X)


def _sigmoid(x):
    return 1.0 / (1.0 + jnp.exp(-x))


def _gelu(x):
    th = jnp.tanh(GELU_C * (x + GELU_K * x * x * x))
    return 0.5 * x * (1.0 + th)


def _gelu_and_grad(x):
    x2 = x * x
    th = jnp.tanh(GELU_C * (x + GELU_K * x2 * x))
    g = 0.5 * x * (1.0 + th)
    dg = 0.5 * (1.0 + th) + 0.5 * x * (1.0 - th * th) * (GELU_C * (1.0 + 3.0 * GELU_K * x2))
    return g, dg


def _dot(a, b, dims):
    return lax.dot_general(a, b, (dims, ((), ())), preferred_element_type=F32)


def _dot_nn(a, b):
    return _dot(a, b, ((1,), (0,)))


def _dot_nt(a, b):
    return _dot(a, b, ((1,), (1,)))


def _dot_tn(a, b):
    return _dot(a, b, ((0,), (0,)))


def _lo_mask(shape):
    return lax.broadcasted_iota(jnp.int32, shape, len(shape) - 1) < (LANES // 2)


def _half_sums(x, lo):
    s_lo = jnp.sum(jnp.where(lo, x, 0.0), axis=-1, keepdims=True)
    s_all = jnp.sum(x, axis=-1, keepdims=True)
    return jnp.where(lo, s_lo, s_all - s_lo)


def _dup_half(x, half, lo):
    r = pltpu.roll(x, LANES // 2, axis=1)
    return jnp.where(lo, x, r) if half == 0 else jnp.where(lo, r, x)


def _with_deps(body, n_in, deps):
    k = len(deps)
    if not k:
        return body, [], ()

    def skipping(*refs):
        return body(*refs[:n_in], *refs[n_in + k:])

    return skipping, [ANY] * k, tuple(deps)


MM_VMEM_BUDGET = 40 * 1024 * 1024
MM_MAX_TILE = 1408
MM_MAX_TK = 4096
MM_STEP_BYTES = 1 << 20


def _divisors(n, step, cap):
    return [d for d in range(step, min(n, cap) + 1, step) if n % d == 0] or [n]


def _mm_tiles(M, N, K, out_bytes, tm_divides, tn_divides):
    best = None
    for tm in _divisors(M, LANES, MM_MAX_TILE):
        for tn in _divisors(N, LANES, MM_MAX_TILE):
            if tm_divides % tm or tn_divides % tn:
                continue
            for tk in _divisors(K, 4 * LANES, MM_MAX_TK):
                vmem = 4 * (tm * tk + tk * tn) + 2 * tm * tn * out_bytes + (0 if tk == K else 4 * tm * tn)
                if vmem > MM_VMEM_BUDGET:
                    continue
                traffic = 2 * M * K * (N // tn) + 2 * K * N * (M // tm) + M * N * out_bytes
                cost = traffic + (K // tk - 1) * 8 * M * N + (M // tm) * (N // tn) * (K // tk) * MM_STEP_BYTES
                if best is None or cost < best[0]:
                    best = (cost, tm, tn, tk)
    assert best is not None, (M, N, K)
    return best[1:]


def _mm(a, b, *, mode, out_dtype, name, deps=(), b_rows=(0, None), rotate=None, out_rows=(0, None), out_prev=None):
    b_first, b_count = b_rows
    if mode == "nn":
        (M, K), N = a.shape, b.shape[1]
    elif mode == "nt":
        (M, K), N = a.shape, (b.shape[0] if b_count is None else b_count)
    else:
        (K, M), N = a.shape, b.shape[1]
    shift, period = rotate or (0, 1)
    assert period == 1 or mode == "nt"
    out_first, out_total = out_rows[0], (M if out_rows[1] is None else out_rows[1])
    tm, tn, tk = _mm_tiles(M, N, K, jnp.dtype(out_dtype).itemsize, math.gcd(M, out_first),
                           math.gcd(N // period, b_first if mode == "nt" else 0))
    gm, gn, gk = M // tm, N // tn, K // tk

    def turned(j):
        per = N // period // tn
        return ((j // per + shift) % period) * per + j % per if period > 1 else j

    if mode == "nn":
        a_spec = pl.BlockSpec((tm, tk), lambda i, j, k: (i, k))
        b_spec = pl.BlockSpec((tk, tn), lambda i, j, k: (k + b_first // tk, j))
        contract = ((1,), (0,))
    elif mode == "nt":
        a_spec = pl.BlockSpec((tm, tk), lambda i, j, k: (i, k))
        b_spec = pl.BlockSpec((tn, tk), lambda i, j, k: (turned(j) + b_first // tn, k))
        contract = ((1,), (1,))
    else:
        a_spec = pl.BlockSpec((tk, tm), lambda i, j, k: (k, i))
        b_spec = pl.BlockSpec((tk, tn), lambda i, j, k: (k, j))
        contract = ((0,), (0,))
    o_spec = pl.BlockSpec((tm, tn), lambda i, j, k: (i + out_first // tm, j))
    assert b_first % (tk if mode == "nn" else tn) == 0 and out_first % tm == 0, (name, tm, tn, tk)
    n_prev = 0 if out_prev is None else 1

    def body(a_ref, b_ref, *rest):
        o_ref = rest[n_prev]
        part = _dot(a_ref[...].astype(BF16), b_ref[...].astype(BF16), contract)
        if gk == 1:
            o_ref[...] = part.astype(out_dtype)
            return
        acc_ref = rest[n_prev + 1]
        k = pl.program_id(2)

        @pl.when(k == 0)
        def _():
            acc_ref[...] = part

        @pl.when(k > 0)
        def _():
            acc_ref[...] += part

        @pl.when(k == gk - 1)
        def _():
            o_ref[...] = acc_ref[...].astype(out_dtype)

    body, dep_specs, dep_args = _with_deps(body, 2 + n_prev, deps)
    return pl.pallas_call(
        body,
        name=name,
        grid=(gm, gn, gk),
        in_specs=[a_spec, b_spec] + [ANY] * n_prev + dep_specs,
        out_specs=o_spec,
        out_shape=jax.ShapeDtypeStruct((out_total, N), out_dtype),
        input_output_aliases={2: 0} if n_prev else {},
        scratch_shapes=[] if gk == 1 else [pltpu.VMEM((tm, tn), F32)],
        compiler_params=_params(("parallel", "parallel", "arbitrary")),
    )(a, b, *([out_prev] if n_prev else []), *dep_args)


def _mm_tn_parts(parts, at, b, *, name):
    K, N = b.shape
    n = len(parts)
    tm = math.gcd(*[p.shape[1] for p in parts], *at)
    tiles = [p.shape[1] // tm for p in parts]
    first = [sum(tiles[:p]) for p in range(n)]

    def mine(i, p):
        return jnp.logical_and(i >= first[p], i < first[p] + tiles[p])

    def out_tile(i):
        t = 0
        for p in range(n):
            t = jnp.where(mine(i, p), at[p] // tm + i - first[p], t)
        return t

    def body(*refs):
        a_refs, b_ref, o_ref = refs[:n], refs[n], refs[n + 1]
        for p in range(n):
            @pl.when(mine(pl.program_id(0), p))
            def _(p=p):
                o_ref[...] = _dot_tn(a_refs[p][...], b_ref[...])

    return pl.pallas_call(
        body, name=name, grid=(sum(tiles),),
        in_specs=[pl.BlockSpec((K, tm), lambda i, p=p: (0, jnp.clip(i - first[p], 0, tiles[p] - 1))) for p in range(n)]
        + [pl.BlockSpec((K, N), lambda i: (0, 0), pipeline_mode=pl.Buffered(1))],
        out_specs=pl.BlockSpec((tm, N), lambda i: (out_tile(i), 0)),
        out_shape=jax.ShapeDtypeStruct((sum(p.shape[1] for p in parts), N), F32),
        compiler_params=_params(("arbitrary",)),
    )(*parts, b)


def _mm_rows(a, b, *, mode, fn, out_dtypes, rows=(), vecs=(), reduce=False, name, deps=(), b_rows=(0, None), a_at=None):
    parts = a if a_at is not None else (a,)
    starts = a_at if a_at is not None else (0,)
    n_parts = len(parts)
    M, K = parts[0].shape[0], sum(p.shape[1] for p in parts)
    b_first, b_count = b_rows[0], (b.shape[0] if b_rows[1] is None else b_rows[1])
    N = b.shape[1] if mode == "nn" else b_count
    contract = ((1,), (0,)) if mode == "nn" else ((1,), (1,))
    n_rows, n_vecs, n_out = len(rows), len(vecs), len(out_dtypes)
    out_bytes = sum(jnp.dtype(d).itemsize for d in out_dtypes)
    tm = max(t for t in _divisors(M, LANES, MM_MAX_TILE)
             if 4 * t * K + 2 * K * N + 2 * t * N * (4 * n_rows + out_bytes) <= MM_VMEM_BUDGET)
    assert b_first % b_count == 0 and (a_at is None or mode == "nn")

    def body(*refs):
        a_refs, b_ref, rest = refs[:n_parts], refs[n_parts], refs[n_parts + 1:]
        row_refs, vec_refs = rest[:n_rows], rest[n_rows:n_rows + n_vecs]
        out_refs = rest[n_rows + n_vecs:]
        if a_at is None:
            acc = _dot(a_refs[0][...], b_ref[...], contract)
        else:
            acc = sum(_dot(r[...], b_ref[at:at + r.shape[1], :], contract) for r, at in zip(a_refs, starts))
        res = fn(acc, *[r[...] for r in row_refs], *[v[...] for v in vec_refs])
        for o_ref, val in zip(out_refs[:n_out], res):
            o_ref[...] = val.astype(o_ref.dtype)
        if reduce:
            @pl.when(pl.program_id(0) == 0)
            def _():
                out_refs[n_out][...] = res[n_out]

            @pl.when(pl.program_id(0) > 0)
            def _():
                out_refs[n_out][...] += res[n_out]

    row = pl.BlockSpec((tm, N), lambda i: (i, 0))
    vec = pl.BlockSpec((1, N), lambda i: (0, 0))
    body, dep_specs, dep_args = _with_deps(body, n_parts + 1 + n_rows + n_vecs, deps)
    return pl.pallas_call(
        body, name=name, grid=(M // tm,),
        in_specs=[pl.BlockSpec((tm, p.shape[1]), lambda i: (i, 0)) for p in parts]
        + [pl.BlockSpec((b_count, b.shape[1]), lambda i: (b_first // b_count, 0), pipeline_mode=pl.Buffered(1))]
        + [row] * n_rows + [vec] * n_vecs + dep_specs,
        out_specs=[row] * n_out + [vec] * reduce,
        out_shape=[jax.ShapeDtypeStruct((M, N), d) for d in out_dtypes] + [jax.ShapeDtypeStruct((1, N), F32)] * reduce,
        compiler_params=_params(("arbitrary",)),
    )(*parts, b, *rows, *[v.reshape(1, N) for v in vecs], *dep_args)


def _rms(x, gain):
    return x * lax.rsqrt(jnp.mean(x * x, axis=-1, keepdims=True) + NORM_EPS) * gain


def _residual_then_norm(acc, x, gain):
    x_out = x + acc
    return x_out, _rms(x_out, gain)


def _residual_then_loss(acc, x, target):
    err = (x + acc) - target
    dy = err * (1.0 / D_MODEL)
    return dy, dy, jnp.sum(err * err, axis=0, keepdims=True) * (0.5 / D_MODEL)


def _rms_bwd_rows(dh, x, dres, gain):
    r = lax.rsqrt(jnp.mean(x * x, axis=-1, keepdims=True) + NORM_EPS)
    xh = x * r
    dxh = dh * gain
    dx = dres + r * (dxh - xh * jnp.mean(dxh * xh, axis=-1, keepdims=True))
    return dx, dx, jnp.sum(dh * xh, axis=0, keepdims=True)


def _rms_fwd(x, gain, *, name, tm=512, deps=()):
    T, D = x.shape

    def body(x_ref, g_ref, h_ref):
        xv = x_ref[...]
        r = lax.rsqrt(jnp.mean(xv * xv, axis=-1, keepdims=True) + NORM_EPS)
        h_ref[...] = (xv * r * g_ref[...]).astype(BF16)

    body, dep_specs, dep_args = _with_deps(body, 2, deps)
    return pl.pallas_call(
        body, name=name, grid=(T // tm,),
        in_specs=[pl.BlockSpec((tm, D), lambda i: (i, 0)), pl.BlockSpec((1, D), lambda i: (0, 0))] + dep_specs,
        out_specs=pl.BlockSpec((tm, D), lambda i: (i, 0)),
        out_shape=jax.ShapeDtypeStruct((T, D), BF16),
        compiler_params=_params(("parallel",)),
    )(x, gain.reshape(1, D), *dep_args)


def _head_norm(x, gain2, lo):
    ms = _half_sums(x * x, lo) * (1.0 / HEAD_DIM)
    r = lax.rsqrt(ms + NORM_EPS)
    xh = x * r
    return xh * gain2, xh, r


def _head_norm_bwd(xh, r, gain2, dy, lo):
    dxh = dy * gain2
    dx = r * (dxh - xh * (_half_sums(dxh * xh, lo) * (1.0 / HEAD_DIM)))
    return dx, dy * xh


Q_GROUP = N_Q_HEADS // 2
GROUP_ROWS = Q_GROUP * BLOCK
ATT_SCRATCH = (pltpu.VMEM((2, 2, GROUP_ROWS, BLOCK), F32), pltpu.VMEM((2, GROUP_ROWS, 1), F32))


def _att_consts(sink_ref, bias_ref, sinkcol_ref):
    row = lax.broadcasted_iota(jnp.int32, (GROUP_ROWS, BLOCK), 0)
    kj = lax.broadcasted_iota(jnp.int32, (GROUP_ROWS, BLOCK), 1)
    head = row // BLOCK
    head_col = lax.broadcasted_iota(jnp.int32, (GROUP_ROWS, 1), 0) // BLOCK
    d_cur = (row % BLOCK) - kj
    d_prev = d_cur + BLOCK
    for kv in range(2):
        slope = jnp.zeros((GROUP_ROWS, BLOCK), F32)
        sink = jnp.zeros((GROUP_ROWS, 1), F32)
        for r in range(Q_GROUP):
            slope = jnp.where(head == r, ALIBI_SLOPES[Q_GROUP * kv + r], slope)
            sink = jnp.where(head_col == r, sink_ref[Q_GROUP * kv + r], sink)
        bias_ref[kv, 0] = jnp.where(d_cur >= 0, -slope * d_cur.astype(F32), NEG_INF)
        bias_ref[kv, 1] = jnp.where(d_prev < BLOCK, -slope * d_prev.astype(F32), NEG_INF)
        sinkcol_ref[kv] = sink


def _stack_heads(t0, t1, lo):
    z = jnp.zeros_like(t0)
    return jnp.concatenate([jnp.where(lo, t0, z), jnp.where(lo, z, t0), jnp.where(lo, t1, z), jnp.where(lo, z, t1)], axis=0)


def _unstack_heads(x4, lo):
    return (jnp.where(lo, x4[0:BLOCK], x4[BLOCK:2 * BLOCK]), jnp.where(lo, x4[2 * BLOCK:3 * BLOCK], x4[3 * BLOCK:]))


def _att_probs(q4, k2c, k2p, bias_c, bias_p, sink, has_prev):
    s_c = _dot_nt(q4, k2c) * ATT_SCALE + bias_c
    s_p = jnp.where(has_prev, _dot_nt(q4, k2p) * ATT_SCALE + bias_p, NEG_INF)
    m = jnp.maximum(jnp.max(jnp.maximum(s_c, s_p), axis=-1, keepdims=True), sink)
    e_c = jnp.exp(s_c - m)
    e_p = jnp.exp(s_p - m)
    e_s = jnp.exp(sink - m)
    inv = 1.0 / (jnp.sum(e_c + e_p, axis=-1, keepdims=True) + e_s)
    return e_c * inv, e_p * inv, e_s * inv


def _attention_fwd(proj, q_gain, k_gain, sinks, *, n_seq, seq, name):
    T = n_seq * seq
    nb = seq // BLOCK
    qcol, kvcol = COL_QKV // ATT_WIDTH, (COL_QKV + ATT_WIDTH) // (2 * KV_WIDTH)

    def body(q_ref, kv_ref, qg_ref, kg_ref, sink_ref, y_ref, bias_ref, sinkcol_ref):
        lo = _lo_mask((BLOCK, LANES))
        qg, kg = qg_ref[...], kg_ref[...]
        _att_consts(sink_ref, bias_ref, sinkcol_ref)

        def block(i, carry):
            r0 = pl.multiple_of(i * BLOCK, BLOCK)
            rp = pl.multiple_of(jnp.maximum(i - 1, 0) * BLOCK, BLOCK)
            has_prev = i > 0
            kn_c = _head_norm(kv_ref[pl.ds(r0, BLOCK), 0:KV_WIDTH].astype(F32), kg, lo)[0].astype(BF16)
            kn_p = _head_norm(kv_ref[pl.ds(rp, BLOCK), 0:KV_WIDTH].astype(F32), kg, lo)[0].astype(BF16)
            v_c = kv_ref[pl.ds(r0, BLOCK), KV_WIDTH:2 * KV_WIDTH].astype(BF16)
            v_p = kv_ref[pl.ds(rp, BLOCK), KV_WIDTH:2 * KV_WIDTH].astype(BF16)
            for kv in range(2):
                k2c, k2p = _dup_half(kn_c, kv, lo), _dup_half(kn_p, kv, lo)
                v2c, v2p = _dup_half(v_c, kv, lo), _dup_half(v_p, kv, lo)
                cols = [slice((2 * kv + t) * LANES, (2 * kv + t + 1) * LANES) for t in range(2)]
                qn = [_head_norm(q_ref[pl.ds(r0, BLOCK), c].astype(F32), qg, lo)[0] for c in cols]
                q4 = _stack_heads(qn[0], qn[1], lo).astype(BF16)
                p_c, p_p, _ = _att_probs(q4, k2c, k2p, bias_ref[kv, 0], bias_ref[kv, 1], sinkcol_ref[kv], has_prev)
                o4 = _dot_nn(p_c.astype(BF16), v2c) + _dot_nn(p_p.astype(BF16), v2p)
                for c, out in zip(cols, _unstack_heads(o4, lo)):
                    y_ref[pl.ds(r0, BLOCK), c] = out.astype(BF16)
            return carry

        lax.fori_loop(0, nb, block, 0)

    vec = pl.BlockSpec((1, LANES), lambda b: (0, 0))
    return pl.pallas_call(
        body, name=name, grid=(n_seq,),
        in_specs=[pl.BlockSpec((seq, ATT_WIDTH), lambda b: (b, qcol)),
                  pl.BlockSpec((seq, 2 * KV_WIDTH), lambda b: (b, kvcol)),
                  vec, vec, pl.BlockSpec(memory_space=pltpu.SMEM)],
        out_specs=pl.BlockSpec((seq, ATT_WIDTH), lambda b: (b, 0)),
        out_shape=jax.ShapeDtypeStruct((T, ATT_WIDTH), BF16),
        scratch_shapes=list(ATT_SCRATCH),
        compiler_params=_params(("parallel",)),
    )(proj, proj, jnp.tile(q_gain, 2).reshape(1, LANES), jnp.tile(k_gain, 2).reshape(1, LANES), sinks)


def _attention_bwd(proj, dy, q_gain, k_gain, sinks, *, n_seq, seq, name, deps=()):
    T = n_seq * seq
    nb = seq // BLOCK
    qcol, kvcol = COL_QKV // ATT_WIDTH, (COL_QKV + ATT_WIDTH) // (2 * KV_WIDTH)

    def body(q_ref, kv_ref, dy_ref, qg_ref, kg_ref, sink_ref, dqkv_ref, dqg_ref, dkg_ref, dsink_ref,
             dkn_acc, dv_acc, qg_acc, kg_acc, sink_acc, bias_ref, sinkcol_ref):
        lo = _lo_mask((BLOCK, LANES))
        qg, kg = qg_ref[...], kg_ref[...]
        _att_consts(sink_ref, bias_ref, sinkcol_ref)
        first = pl.program_id(0) == 0

        @pl.when(first)
        def _():
            qg_acc[...] = jnp.zeros_like(qg_acc)
            kg_acc[...] = jnp.zeros_like(kg_acc)
            sink_acc[...] = jnp.zeros_like(sink_acc)

        dkn_acc[...] = jnp.zeros_like(dkn_acc)
        dv_acc[...] = jnp.zeros_like(dv_acc)

        def block(i, carry):
            r0 = pl.multiple_of(i * BLOCK, BLOCK)
            rp = pl.multiple_of(jnp.maximum(i - 1, 0) * BLOCK, BLOCK)
            has_prev = i > 0
            kn_c = _head_norm(kv_ref[pl.ds(r0, BLOCK), 0:KV_WIDTH].astype(F32), kg, lo)[0].astype(BF16)
            kn_p = _head_norm(kv_ref[pl.ds(rp, BLOCK), 0:KV_WIDTH].astype(F32), kg, lo)[0].astype(BF16)
            v_c = kv_ref[pl.ds(r0, BLOCK), KV_WIDTH:2 * KV_WIDTH].astype(BF16)
            v_p = kv_ref[pl.ds(rp, BLOCK), KV_WIDTH:2 * KV_WIDTH].astype(BF16)
            dk_c, dk_p, dv_c, dv_p = [], [], [], []
            for kv in range(2):
                k2c, k2p = _dup_half(kn_c, kv, lo), _dup_half(kn_p, kv, lo)
                v2c, v2p = _dup_half(v_c, kv, lo), _dup_half(v_p, kv, lo)
                cols = [slice((2 * kv + t) * LANES, (2 * kv + t + 1) * LANES) for t in range(2)]
                normed = [_head_norm(q_ref[pl.ds(r0, BLOCK), c].astype(F32), qg, lo) for c in cols]
                q4 = _stack_heads(normed[0][0], normed[1][0], lo).astype(BF16)
                do4 = _stack_heads(dy_ref[pl.ds(r0, BLOCK), cols[0]], dy_ref[pl.ds(r0, BLOCK), cols[1]], lo)
                p_c, p_p, p_s = _att_probs(q4, k2c, k2p, bias_ref[kv, 0], bias_ref[kv, 1], sinkcol_ref[kv], has_prev)
                dp_c = _dot_nt(do4, v2c)
                dp_p = _dot_nt(do4, v2p)
                delta = jnp.sum(p_c * dp_c + p_p * dp_p, axis=-1, keepdims=True)
                ds_c = (p_c * (dp_c - delta)).astype(BF16)
                ds_p = (p_p * (dp_p - delta)).astype(BF16)
                sink_acc[kv] += -(p_s * delta)
                dq4 = (_dot_nn(ds_c, k2c) + _dot_nn(ds_p, k2p)) * ATT_SCALE
                for c, (_, qh, qr), dqn in zip(cols, normed, _unstack_heads(dq4, lo)):
                    dq, dg = _head_norm_bwd(qh, qr, qg, dqn, lo)
                    dqkv_ref[pl.ds(r0, BLOCK), c] = dq.astype(BF16)
                    qg_acc[...] += dg
                dk_c.append(_dot_tn(ds_c, q4))
                dk_p.append(_dot_tn(ds_p, q4))
                dv_c.append(_dot_tn(p_c.astype(BF16), do4))
                dv_p.append(_dot_tn(p_p.astype(BF16), do4))

            def fold(parts):
                a = parts[0] + pltpu.roll(parts[0], LANES // 2, axis=1)
                b = parts[1] + pltpu.roll(parts[1], LANES // 2, axis=1)
                return jnp.where(lo, a, b)

            dkn_acc[pl.ds(r0, BLOCK), :] += fold(dk_c) * ATT_SCALE
            dkn_acc[pl.ds(rp, BLOCK), :] += fold(dk_p) * ATT_SCALE
            dv_acc[pl.ds(r0, BLOCK), :] += fold(dv_c)
            dv_acc[pl.ds(rp, BLOCK), :] += fold(dv_p)
            return carry

        lax.fori_loop(0, nb, block, 0)

        def finish(i, carry):
            r0 = pl.multiple_of(i * BLOCK, BLOCK)
            _, kh, kr = _head_norm(kv_ref[pl.ds(r0, BLOCK), 0:KV_WIDTH].astype(F32), kg, lo)
            dk, dg = _head_norm_bwd(kh, kr, kg, dkn_acc[pl.ds(r0, BLOCK), :], lo)
            dqkv_ref[pl.ds(r0, BLOCK), ATT_WIDTH:ATT_WIDTH + KV_WIDTH] = dk.astype(BF16)
            dqkv_ref[pl.ds(r0, BLOCK), ATT_WIDTH + KV_WIDTH:QKV_WIDTH] = dv_acc[pl.ds(r0, BLOCK), :].astype(BF16)
            kg_acc[...] += dg
            return carry

        lax.fori_loop(0, nb, finish, 0)

        @pl.when(pl.program_id(0) == n_seq - 1)
        def _():
            dqg_ref[...] = jnp.sum(qg_acc[...], axis=0, keepdims=True)
            dkg_ref[...] = jnp.sum(kg_acc[...], axis=0, keepdims=True)
            lane = lax.broadcasted_iota(jnp.int32, (1, LANES), 1)
            dsink = jnp.zeros((1, LANES), F32)
            for kv in range(2):
                for r in range(Q_GROUP):
                    total = jnp.sum(sink_acc[kv, r * BLOCK:(r + 1) * BLOCK, :], axis=0, keepdims=True)
                    dsink = jnp.where(lane == Q_GROUP * kv + r, total, dsink)
            dsink_ref[...] = dsink

    vec = pl.BlockSpec((1, LANES), lambda b: (0, 0))
    acc = pltpu.VMEM((BLOCK, LANES), F32)
    body, dep_specs, dep_args = _with_deps(body, 6, deps)
    dqkv, dqg, dkg, dsink = pl.pallas_call(
        body, name=name, grid=(n_seq,),
        in_specs=[pl.BlockSpec((seq, ATT_WIDTH), lambda b: (b, qcol)),
                  pl.BlockSpec((seq, 2 * KV_WIDTH), lambda b: (b, kvcol)),
                  pl.BlockSpec((seq, ATT_WIDTH), lambda b: (b, 0)),
                  vec, vec, pl.BlockSpec(memory_space=pltpu.SMEM)] + dep_specs,
        out_specs=[pl.BlockSpec((seq, QKV_WIDTH), lambda b: (b, 0)), vec, vec, vec],
        out_shape=[jax.ShapeDtypeStruct((T, QKV_WIDTH), BF16)] + [jax.ShapeDtypeStruct((1, LANES), F32)] * 3,
        scratch_shapes=[pltpu.VMEM((seq, KV_WIDTH), F32), pltpu.VMEM((seq, KV_WIDTH), F32), acc, acc,
                        pltpu.VMEM((2, GROUP_ROWS, 1), F32), *ATT_SCRATCH],
        compiler_params=_params(("arbitrary",)),
    )(proj, proj, dy, jnp.tile(q_gain, 2).reshape(1, LANES), jnp.tile(k_gain, 2).reshape(1, LANES), sinks, *dep_args)
    half = LANES // 2
    return dqkv, dqg[0, :half] + dqg[0, half:], dkg[0, :half] + dkg[0, half:], dsink[0, :N_Q_HEADS]


def _sgu_weights(w_ref):
    r = lax.broadcasted_iota(jnp.int32, (BLOCK, BLOCK), 0)
    c = lax.broadcasted_iota(jnp.int32, (BLOCK, BLOCK), 1)
    return [jnp.where(r >= c, w_ref[g], 0.0).astype(BF16) for g in range(SGU_GROUPS)]


def _sgu_fwd(proj, gain, w_s, bias_full, *, n_seq, seq, name):
    T = n_seq * seq
    nc = seq // BLOCK

    def body(suv_ref, g_ref, w_ref, b_ref, y_ref):
        lo = _lo_mask((BLOCK, LANES))
        wm = _sgu_weights(w_ref)
        gain_v = g_ref[...]

        def chunk(c, carry):
            r0 = pl.multiple_of(c * BLOCK, BLOCK)
            gv = _gelu(suv_ref[pl.ds(r0, BLOCK), SGU_WIDTH:2 * SGU_WIDTH].astype(F32))
            r = lax.rsqrt(jnp.mean(gv * gv, axis=-1, keepdims=True) + NORM_EPS)
            vn = (gv * r * gain_v).astype(BF16)
            for p in range(SGU_WIDTH // LANES):
                cols = slice(p * LANES, (p + 1) * LANES)
                vp = vn[:, cols]
                mixed = jnp.where(lo, _dot_nn(wm[2 * p], vp), _dot_nn(wm[2 * p + 1], vp)) + b_ref[:, cols]
                u = _gelu(suv_ref[pl.ds(r0, BLOCK), cols].astype(F32))
                y_ref[pl.ds(r0, BLOCK), cols] = (u * mixed).astype(BF16)
            return carry

        lax.fori_loop(0, nc, chunk, 0)

    return pl.pallas_call(
        body, name=name, grid=(n_seq,),
        in_specs=[pl.BlockSpec((seq, 2 * SGU_WIDTH), lambda b: (b, COL_SUV // (2 * SGU_WIDTH))),
                  pl.BlockSpec((1, SGU_WIDTH), lambda b: (0, 0)),
                  pl.BlockSpec((SGU_GROUPS, BLOCK, BLOCK), lambda b: (0, 0, 0)),
                  pl.BlockSpec((BLOCK, SGU_WIDTH), lambda b: (0, 0))],
        out_specs=pl.BlockSpec((seq, SGU_WIDTH), lambda b: (b, 0)),
        out_shape=jax.ShapeDtypeStruct((T, SGU_WIDTH), BF16),
        compiler_params=_params(("parallel",)),
    )(proj, gain.reshape(1, SGU_WIDTH), w_s, bias_full)


def _sgu_bwd(proj, dy, gain, w_s, bias_full, *, n_seq, seq, name, deps=()):
    T = n_seq * seq
    nc = seq // BLOCK
    n_tiles = SGU_WIDTH // LANES

    def body(suv_ref, dy_ref, g_ref, w_ref, b_ref, dsuv_ref, dg_ref, dw_ref, db_ref, dg_acc, dw_acc, db_acc):
        lo = _lo_mask((BLOCK, LANES))
        hi = jnp.logical_not(lo)
        wm = _sgu_weights(w_ref)
        wmt = [jnp.where(lax.broadcasted_iota(jnp.int32, (BLOCK, BLOCK), 1) >= lax.broadcasted_iota(jnp.int32, (BLOCK, BLOCK), 0),
                         w_ref[g].T, 0.0).astype(BF16) for g in range(SGU_GROUPS)]
        gain_v = g_ref[...]

        @pl.when(pl.program_id(0) == 0)
        def _():
            dg_acc[...] = jnp.zeros_like(dg_acc)
            dw_acc[...] = jnp.zeros_like(dw_acc)
            db_acc[...] = jnp.zeros_like(db_acc)

        def chunk(c, carry):
            r0 = pl.multiple_of(c * BLOCK, BLOCK)
            gv, dgelu_v = _gelu_and_grad(suv_ref[pl.ds(r0, BLOCK), SGU_WIDTH:2 * SGU_WIDTH].astype(F32))
            r = lax.rsqrt(jnp.mean(gv * gv, axis=-1, keepdims=True) + NORM_EPS)
            vh = gv * r
            vn = (vh * gain_v).astype(BF16)
            dvn_tiles = []
            for p in range(n_tiles):
                cols = slice(p * LANES, (p + 1) * LANES)
                vp = vn[:, cols]
                mixed = jnp.where(lo, _dot_nn(wm[2 * p], vp), _dot_nn(wm[2 * p + 1], vp)) + b_ref[:, cols]
                u, dgelu_u = _gelu_and_grad(suv_ref[pl.ds(r0, BLOCK), cols].astype(F32))
                dyv = dy_ref[pl.ds(r0, BLOCK), cols]
                dsuv_ref[pl.ds(r0, BLOCK), cols] = (dyv * mixed * dgelu_u).astype(BF16)
                dm = dyv * u
                db_acc[:, cols] += dm
                dm_bf = dm.astype(BF16)
                dvn_tiles.append(jnp.where(lo, _dot_nn(wmt[2 * p], dm_bf), _dot_nn(wmt[2 * p + 1], dm_bf)))
                dw_acc[2 * p] += _dot_nt(jnp.where(lo, dm, 0.0).astype(BF16), vp)
                dw_acc[2 * p + 1] += _dot_nt(jnp.where(hi, dm, 0.0).astype(BF16), vp)
            dvn = jnp.concatenate(dvn_tiles, axis=1)
            dg_acc[...] += dvn * vh
            dvh = dvn * gain_v
            dgv = r * (dvh - vh * jnp.mean(dvh * vh, axis=-1, keepdims=True))
            dsuv_ref[pl.ds(r0, BLOCK), SGU_WIDTH:2 * SGU_WIDTH] = (dgv * dgelu_v).astype(BF16)
            return carry

        lax.fori_loop(0, nc, chunk, 0)

        @pl.when(pl.program_id(0) == n_seq - 1)
        def _():
            dg_ref[...] = jnp.sum(dg_acc[...], axis=0, keepdims=True)
            r = lax.broadcasted_iota(jnp.int32, (BLOCK, BLOCK), 0)
            c = lax.broadcasted_iota(jnp.int32, (BLOCK, BLOCK), 1)
            for g in range(SGU_GROUPS):
                dw_ref[g] = jnp.where(r >= c, dw_acc[g], 0.0)
            lane = lax.broadcasted_iota(jnp.int32, (BLOCK, LANES), 1)
            out = jnp.zeros((BLOCK, LANES), F32)
            for p in range(n_tiles):
                tile = db_acc[:, p * LANES:(p + 1) * LANES]
                s_lo = jnp.sum(jnp.where(lo, tile, 0.0), axis=-1, keepdims=True)
                s_hi = jnp.sum(jnp.where(hi, tile, 0.0), axis=-1, keepdims=True)
                out = jnp.where(lane == 2 * p, s_lo, out)
                out = jnp.where(lane == 2 * p + 1, s_hi, out)
            db_ref[...] = out

    body, dep_specs, dep_args = _with_deps(body, 5, deps)
    dsuv, dg, dw, db = pl.pallas_call(
        body, name=name, grid=(n_seq,),
        in_specs=[pl.BlockSpec((seq, 2 * SGU_WIDTH), lambda b: (b, COL_SUV // (2 * SGU_WIDTH))),
                  pl.BlockSpec((seq, SGU_WIDTH), lambda b: (b, 0)),
                  pl.BlockSpec((1, SGU_WIDTH), lambda b: (0, 0)),
                  pl.BlockSpec((SGU_GROUPS, BLOCK, BLOCK), lambda b: (0, 0, 0)),
                  pl.BlockSpec((BLOCK, SGU_WIDTH), lambda b: (0, 0))] + dep_specs,
        out_specs=[pl.BlockSpec((seq, 2 * SGU_WIDTH), lambda b: (b, 0)),
                   pl.BlockSpec((1, SGU_WIDTH), lambda b: (0, 0)),
                   pl.BlockSpec((SGU_GROUPS, BLOCK, BLOCK), lambda b: (0, 0, 0)),
                   pl.BlockSpec((BLOCK, LANES), lambda b: (0, 0))],
        out_shape=[jax.ShapeDtypeStruct((T, 2 * SGU_WIDTH), BF16), jax.ShapeDtypeStruct((1, SGU_WIDTH), F32),
                   jax.ShapeDtypeStruct((SGU_GROUPS, BLOCK, BLOCK), F32), jax.ShapeDtypeStruct((BLOCK, LANES), F32)],
        scratch_shapes=[pltpu.VMEM((BLOCK, SGU_WIDTH), F32), pltpu.VMEM((SGU_GROUPS, BLOCK, BLOCK), F32),
                        pltpu.VMEM((BLOCK, SGU_WIDTH), F32)],
        compiler_params=_params(("arbitrary",)),
    )(proj, dy, gain.reshape(1, SGU_WIDTH), w_s, bias_full, *dep_args)
    return dsuv, dg.reshape(SGU_WIDTH), dw, db[:, :SGU_GROUPS].T


def _merge_fwd(y_att, y_sgu, w_oa, w_ob, proj, *, name, tm=1024, tn=512, deps=()):
    T = y_att.shape[0]

    def body(ya_ref, ys_ref, wa_ref, wb_ref, ga_ref, gb_ref, o_ref):
        pa = _dot_nn(ya_ref[...], wa_ref[...])
        pb = _dot_nn(ys_ref[...], wb_ref[...])
        o_ref[...] = (_sigmoid(ga_ref[...].astype(F32)) * pa + _sigmoid(gb_ref[...].astype(F32)) * pb).astype(BF16)

    act = pl.BlockSpec((tm, ATT_WIDTH), lambda i, j: (i, 0))
    wgt = pl.BlockSpec((ATT_WIDTH, tn), lambda i, j: (0, j))
    body, dep_specs, dep_args = _with_deps(body, 6, deps)
    return pl.pallas_call(
        body, name=name, grid=(T // tm, D_MODEL // tn),
        in_specs=[act, act, wgt, wgt,
                  pl.BlockSpec((tm, tn), lambda i, j: (i, j + COL_GA // tn)),
                  pl.BlockSpec((tm, tn), lambda i, j: (i, j + COL_GB // tn))] + dep_specs,
        out_specs=pl.BlockSpec((tm, tn), lambda i, j: (i, j)),
        out_shape=jax.ShapeDtypeStruct((T, D_MODEL), BF16),
        compiler_params=_params(("parallel", "parallel")),
    )(y_att, y_sgu, w_oa, w_ob, proj, proj, *dep_args)


def _merge_bwd(dx1_bf, w_out, y_att, y_sgu, w_oa, w_ob, proj, *, name, tm=1024, tn=512):
    T = y_att.shape[0]

    def body(dx_ref, wo_ref, ya_ref, ys_ref, wa_ref, wb_ref, ga_ref, gb_ref, dpa_ref, dpb_ref, dga_ref, dgb_ref):
        dm = _dot_nt(dx_ref[...], wo_ref[...])
        pa = _dot_nn(ya_ref[...], wa_ref[...])
        pb = _dot_nn(ys_ref[...], wb_ref[...])
        sa = _sigmoid(ga_ref[...].astype(F32))
        sb = _sigmoid(gb_ref[...].astype(F32))
        dpa_ref[...] = (dm * sa).astype(BF16)
        dpb_ref[...] = (dm * sb).astype(BF16)
        dga_ref[...] = (dm * pa * sa * (1.0 - sa)).astype(BF16)
        dgb_ref[...] = (dm * pb * sb * (1.0 - sb)).astype(BF16)

    act = pl.BlockSpec((tm, ATT_WIDTH), lambda i, j: (i, 0))
    wgt = pl.BlockSpec((ATT_WIDTH, tn), lambda i, j: (0, j))
    out = pl.BlockSpec((tm, tn), lambda i, j: (i, j))
    return pl.pallas_call(
        body, name=name, grid=(T // tm, D_MODEL // tn),
        in_specs=[pl.BlockSpec((tm, D_MODEL), lambda i, j: (i, 0)),
                  pl.BlockSpec((tn, D_MODEL), lambda i, j: (j, 0)),
                  act, act, wgt, wgt,
                  pl.BlockSpec((tm, tn), lambda i, j: (i, j + COL_GA // tn)),
                  pl.BlockSpec((tm, tn), lambda i, j: (i, j + COL_GB // tn))],
        out_specs=[out] * 4,
        out_shape=[jax.ShapeDtypeStruct((T, D_MODEL), BF16)] * 4,
        compiler_params=_params(("parallel", "parallel")),
    )(dx1_bf, w_out, y_att, y_sgu, w_oa, w_ob, proj, proj)


CONV_ROWS = 256
CONV_TN = 256


def _shift_rows(cur, prev8, k):
    rolled = pltpu.roll(cur, k, axis=0)
    head = jnp.where(lax.broadcasted_iota(jnp.int32, prev8.shape, 0) < k, pltpu.roll(prev8, k, axis=0), rolled[:SUBLANES])
    return jnp.concatenate([head, rolled[SUBLANES:]], axis=0)


def _shift_rows_up(cur, next8, k):
    n = cur.shape[0]
    rolled = pltpu.roll(cur, n - k, axis=0)
    tail = jnp.where(lax.broadcasted_iota(jnp.int32, next8.shape, 0) >= SUBLANES - k,
                     pltpu.roll(next8, SUBLANES - k, axis=0), rolled[n - SUBLANES:])
    return jnp.concatenate([rolled[:n - SUBLANES], tail], axis=0)


def _up_conv_fwd(h2, w_up_t, cw_g, cw_v, cb_g, cb_v, *, n_seq, seq, name, deps=()):
    T = n_seq * seq
    tn, rows = CONV_TN, CONV_ROWS

    def body(h_ref, ug_ref, uv_ref, wg_ref, wv_ref, bg_ref, bv_ref, a_ref, zg_ref, zv_ref, cg_ref, cv_ref, zg_s, zv_s):
        def conv(cur, prev8, w_ref, b_ref):
            z1 = _shift_rows(cur, prev8, 1)
            z2 = _shift_rows(cur, prev8, 2)
            return b_ref[...] + w_ref[0:1, :] * z2 + w_ref[1:2, :] * z1 + w_ref[2:3, :] * cur

        zg_s[...] = _dot_nt(h_ref[...], ug_ref[...])
        zv_s[...] = _dot_nt(h_ref[...], uv_ref[...])

        def step(s, prev):
            r0 = pl.multiple_of(s * rows, rows)
            zg = zg_s[pl.ds(r0, rows), :]
            zv = zv_s[pl.ds(r0, rows), :]
            zg_ref[pl.ds(r0, rows), :] = zg.astype(ACT_DTYPE)
            zv_ref[pl.ds(r0, rows), :] = zv.astype(ACT_DTYPE)
            g = conv(zg, prev[0], wg_ref, bg_ref)
            v = conv(zv, prev[1], wv_ref, bv_ref)
            a_ref[pl.ds(r0, rows), :] = (g * _sigmoid(g) * v).astype(BF16)
            cg_ref[pl.ds(r0, rows), :] = g.astype(ACT_DTYPE)
            cv_ref[pl.ds(r0, rows), :] = v.astype(ACT_DTYPE)
            return zg[rows - SUBLANES:], zv[rows - SUBLANES:]

        start = jnp.zeros((SUBLANES, tn), F32)
        lax.fori_loop(0, seq // rows, step, (start, start))

    zs = pl.BlockSpec((seq, tn), lambda b, j: (b, j))
    ws = pl.BlockSpec((3, tn), lambda b, j: (0, j))
    bs = pl.BlockSpec((1, tn), lambda b, j: (0, j))
    body, dep_specs, dep_args = _with_deps(body, 7, deps)
    return pl.pallas_call(
        body, name=name, grid=(n_seq, D_FF // tn),
        in_specs=[pl.BlockSpec((seq, D_MODEL), lambda b, j: (b, 0)),
                  pl.BlockSpec((tn, D_MODEL), lambda b, j: (j, 0)),
                  pl.BlockSpec((tn, D_MODEL), lambda b, j: (j + D_FF // tn, 0)), ws, ws, bs, bs] + dep_specs,
        out_specs=[zs] * 5,
        out_shape=[jax.ShapeDtypeStruct((T, D_FF), BF16)] + [jax.ShapeDtypeStruct((T, D_FF), ACT_DTYPE)] * 4,
        scratch_shapes=[pltpu.VMEM((seq, tn), F32)] * 2,
        compiler_params=_params(("parallel", "parallel")),
    )(h2, w_up_t, w_up_t, cw_g, cw_v, cb_g.reshape(1, D_FF), cb_v.reshape(1, D_FF), *dep_args)


def _conv_bwd(z_g, z_v, c_g, c_v, dx2_bf, w_down, cw_g, cw_v, *, n_seq, seq, name):
    T = n_seq * seq
    tn, rows = CONV_TN, CONV_ROWS
    n_steps = seq // rows

    def body(zg_ref, zv_ref, cg_ref, cv_ref, dx_ref, wd_ref, wg_ref, wv_ref,
             dzg_ref, dzv_ref, dwg_ref, dwv_ref, dbg_ref, dbv_ref, dcg_ref, dcv_ref):
        def colsum(x):
            return jnp.sum(x, axis=0, keepdims=True)

        dcv_ref[...] = _dot_nt(dx_ref[...], wd_ref[...])

        def grads(s, accs):
            r0 = pl.multiple_of(s * rows, rows)
            g = cg_ref[pl.ds(r0, rows), :].astype(F32)
            v = cv_ref[pl.ds(r0, rows), :].astype(F32)
            sg = _sigmoid(g)
            dav = dcv_ref[pl.ds(r0, rows), :]
            dcg = dav * v * (sg * (1.0 + g * (1.0 - sg)))
            dcv = dav * (g * sg)
            dcg_ref[pl.ds(r0, rows), :] = dcg
            dcv_ref[pl.ds(r0, rows), :] = dcv
            return accs[0] + colsum(dcg), accs[1] + colsum(dcv)

        zero = jnp.zeros((1, tn), F32)
        db = lax.fori_loop(0, n_steps, grads, (zero, zero))

        def back(s, accs):
            r0 = pl.multiple_of(s * rows, rows)
            last = s == n_steps - 1
            rn = pl.multiple_of(jnp.minimum(r0 + rows, seq - SUBLANES), SUBLANES)
            new = []
            for half, (dc_ref, w_ref, dz_ref, z_ref) in enumerate(((dcg_ref, wg_ref, dzg_ref, zg_ref),
                                                                   (dcv_ref, wv_ref, dzv_ref, zv_ref))):
                cur = dc_ref[pl.ds(r0, rows), :]
                nxt = jnp.where(last, 0.0, dc_ref[pl.ds(rn, SUBLANES), :])
                u1, u2 = _shift_rows_up(cur, nxt, 1), _shift_rows_up(cur, nxt, 2)
                dz_ref[pl.ds(r0, rows), :] = (w_ref[2:3, :] * cur + w_ref[1:2, :] * u1 + w_ref[0:1, :] * u2).astype(BF16)
                z = z_ref[pl.ds(r0, rows), :].astype(F32)
                new += [accs[3 * half] + colsum(u2 * z), accs[3 * half + 1] + colsum(u1 * z),
                        accs[3 * half + 2] + colsum(cur * z)]
            return tuple(new)

        dw = lax.fori_loop(0, n_steps, back, (zero,) * 6)
        first_seq = pl.program_id(1) == 0

        @pl.when(first_seq)
        def _():
            dwg_ref[...] = jnp.concatenate(dw[0:3], axis=0)
            dwv_ref[...] = jnp.concatenate(dw[3:6], axis=0)
            dbg_ref[...], dbv_ref[...] = db

        @pl.when(jnp.logical_not(first_seq))
        def _():
            dwg_ref[...] += jnp.concatenate(dw[0:3], axis=0)
            dwv_ref[...] += jnp.concatenate(dw[3:6], axis=0)
            dbg_ref[...] += db[0]
            dbv_ref[...] += db[1]

    zs = pl.BlockSpec((seq, tn), lambda j, b: (b, j))
    ws = pl.BlockSpec((3, tn), lambda j, b: (0, j))
    bs = pl.BlockSpec((1, tn), lambda j, b: (0, j))
    outs = pl.pallas_call(
        body, name=name, grid=(D_FF // tn, n_seq),
        in_specs=[zs] * 4 + [pl.BlockSpec((seq, D_MODEL), lambda j, b: (b, 0)),
                             pl.BlockSpec((tn, D_MODEL), lambda j, b: (j, 0)), ws, ws],
        out_specs=[zs, zs, ws, ws, bs, bs],
        out_shape=[jax.ShapeDtypeStruct((T, D_FF), BF16)] * 2 + [jax.ShapeDtypeStruct((3, D_FF), F32)] * 2
        + [jax.ShapeDtypeStruct((1, D_FF), F32)] * 2,
        scratch_shapes=[pltpu.VMEM((seq, tn), F32), pltpu.VMEM((seq, tn), F32)],
        compiler_params=_params(("parallel", "arbitrary")),
    )(z_g, z_v, c_g, c_v, dx2_bf, w_down, cw_g, cw_v)
    dz_g, dz_v, dw_g, dw_v, db_g, db_v = outs
    return dz_g, dz_v, dw_g, dw_v, db_g.reshape(D_FF), db_v.reshape(D_FF)


def _layer_fwd(x, h, w, sched, tail, *, n_seq, seq, l):
    tag = f"l{l}"
    deps = sched("fwd_start", l, h)
    proj = _mm(h, w["w_in_t"], mode="nt", out_dtype=ACT_DTYPE, rotate=W_IN_ROTATE, name=f"{tag}_proj", deps=deps)
    y_att = _attention_fwd(proj, w["q_norm"], w["k_norm"], w["sinks"], n_seq=n_seq, seq=seq, name=f"{tag}_att")
    deps = sched("fwd_att", l, y_att)
    y_sgu = _sgu_fwd(proj, w["sgu_norm"], w["w_s"], w["bias_full"], n_seq=n_seq, seq=seq, name=f"{tag}_sgu")
    merged = _merge_fwd(y_att, y_sgu, w["w_oa"], w["w_ob"], proj, name=f"{tag}_merge", deps=deps)
    x1, h2 = _mm_rows(merged, w["w_out"], mode="nn", fn=_residual_then_norm, out_dtypes=(F32, BF16), rows=(x,),
                      vecs=(w["ffn_norm"],), name=f"{tag}_out")
    deps = sched("fwd_mixer_done", l, x1)
    a, z_g, z_v, c_g, c_v = _up_conv_fwd(h2, w["w_up_t"], w["cw_g"], w["cw_v"], w["cb_g"], w["cb_v"], n_seq=n_seq,
                                         seq=seq, name=f"{tag}_up_conv", deps=deps)
    deps = sched("fwd_conv", l, a)
    if tail[0] == "norm":
        out = _mm_rows(a, w["w_down"], mode="nn", fn=_residual_then_norm, out_dtypes=(F32, BF16), rows=(x1,),
                       vecs=(tail[1],), name=f"{tag}_down", deps=deps)
    else:
        out = _mm_rows(a, w["w_down"], mode="nn", fn=_residual_then_loss, out_dtypes=(F32, BF16), rows=(x1, tail[1]),
                       reduce=True, name=f"{tag}_down", deps=deps)
    saved = dict(x=x, h=h, proj=proj, y_att=y_att, y_sgu=y_sgu, merged=merged, x1=x1, h2=h2, z_g=z_g, z_v=z_v,
                 c_g=c_g, c_v=c_v, a=a)
    return out, saved


def _layer_bwd(dx2, dx2_bf, w, s, sched, deps, *, n_seq, seq, l):
    tag = f"l{l}b"
    g = {}
    g["w_down"] = _mm(s["a"], dx2_bf, mode="tn", out_dtype=F32, name=f"{tag}_dw_down", deps=deps)
    dz_g, dz_v, g["cw_g"], g["cw_v"], g["cb_g"], g["cb_v"] = _conv_bwd(
        s["z_g"], s["z_v"], s["c_g"], s["c_v"], dx2_bf, w["w_down"], w["cw_g"], w["cw_v"], n_seq=n_seq, seq=seq,
        name=f"{tag}_conv")
    dw_up_t = _mm(dz_g, s["h2"], mode="tn", out_dtype=F32, out_rows=(0, 2 * D_FF), name=f"{tag}_dw_up_g")
    g["w_up_t"] = _mm(dz_v, s["h2"], mode="tn", out_dtype=F32, out_rows=(D_FF, 2 * D_FF), out_prev=dw_up_t,
                      name=f"{tag}_dw_up_v")
    deps = sched("bwd_ffn_grads", l, dz_v, g)
    dx1, dx1_bf, dgain = _mm_rows((dz_g, dz_v), w["w_up_t"], mode="nn", fn=_rms_bwd_rows, out_dtypes=(F32, BF16),
                                  rows=(s["x1"], dx2), vecs=(w["ffn_norm"],), reduce=True, a_at=(0, D_FF),
                                  name=f"{tag}_dh2", deps=deps)
    g["ffn_norm"] = dgain.reshape(D_MODEL)
    dpa, dpb, dga, dgb = _merge_bwd(dx1_bf, w["w_out"], s["y_att"], s["y_sgu"], w["w_oa"], w["w_ob"], s["proj"],
                                    name=f"{tag}_merge")
    deps = sched("bwd_merge", l, dpa)
    g["w_out"] = _mm(s["merged"], dx1_bf, mode="tn", out_dtype=F32, name=f"{tag}_dw_out",
                     deps=deps)
    dy_att = _mm(dpa, w["w_oa"], mode="nt", out_dtype=BF16, name=f"{tag}_dy_att")
    dy_sgu = _mm(dpb, w["w_ob"], mode="nt", out_dtype=F32, name=f"{tag}_dy_sgu")
    g["w_oa"] = _mm(s["y_att"], dpa, mode="tn", out_dtype=F32, name=f"{tag}_dw_oa")
    g["w_ob"] = _mm(s["y_sgu"], dpb, mode="tn", out_dtype=F32, name=f"{tag}_dw_ob")
    deps = sched("bwd_out_grads", l, dy_att, g)
    dqkv, g["q_norm"], g["k_norm"], g["sinks"] = _attention_bwd(
        s["proj"], dy_att, w["q_norm"], w["k_norm"], w["sinks"], n_seq=n_seq, seq=seq, name=f"{tag}_att", deps=deps)
    deps = sched("bwd_att", l, dqkv)
    dsuv, g["sgu_norm"], g["w_s"], g["b_s"] = _sgu_bwd(
        s["proj"], dy_sgu, w["sgu_norm"], w["w_s"], w["bias_full"], n_seq=n_seq, seq=seq, name=f"{tag}_sgu", deps=deps)
    dproj = (dsuv, dga, dgb, dqkv)
    at = (QKV_WIDTH, QKV_WIDTH + 2 * SGU_WIDTH, QKV_WIDTH + 2 * SGU_WIDTH + D_MODEL, 0)
    g["w_in_t"] = _mm_tn_parts(dproj, at, s["h"], name=f"{tag}_dw_in")
    deps = sched("bwd_w_in_grad", l, dqkv, g)
    dx, dx_bf, dgain = _mm_rows(dproj, w["w_in_t"], mode="nn", fn=_rms_bwd_rows, out_dtypes=(F32, BF16),
                                rows=(s["x"], dx1), vecs=(w["mix_norm"],), reduce=True, a_at=at,
                                name=f"{tag}_dh", deps=deps)
    g["mix_norm"] = dgain.reshape(D_MODEL)
    return dx, dx_bf, g, sched("bwd_dh", l, dx)


def _local_step(x, target, weights, sched, *, n_seq, seq):
    depth = len(weights)
    saved = []
    h = _rms_fwd(x, weights[0]["mix_norm"], name="l0_mix_norm", deps=sched("begin", 0, x))
    for l in range(depth):
        tail = ("norm", weights[l + 1]["mix_norm"]) if l + 1 < depth else ("loss", target)
        out, s = _layer_fwd(x, h, weights[l], sched, tail, n_seq=n_seq, seq=seq, l=l)
        saved.append(s)
        if l + 1 < depth:
            x, h = out
    dy, dy_bf, loss_cols = out
    grads = [None] * depth
    deps = ()
    for l in reversed(range(depth)):
        dy, dy_bf, grads[l], deps = _layer_bwd(dy, dy_bf, weights[l], saved[l], sched, deps, n_seq=n_seq, seq=seq, l=l)
    return jnp.sum(loss_cols), dy, grads, deps


W_IN_SHARD = IN_WIDTH // N_DEV
W_UP_SHARD = 2 * D_FF // N_DEV
COL_MOVE_ROWS = 256


def _w_o_moves():
    return tuple((j, 0, LANES, 0, j * LANES) for j in range(N_DEV))


def _disassemble(mats, w, moves, *, name):
    R = mats[0].shape[0]
    tr = min(R, COL_MOVE_ROWS)
    n = len(mats)

    def body(*refs):
        m_refs, o_ref = refs[:n], refs[n]
        for j, lo, hi, which, at in moves:
            o_ref[j, :, lo:hi] = m_refs[which][:, at:at + hi - lo]

    return pl.pallas_call(
        body, name=name, grid=(R // tr,),
        in_specs=[pl.BlockSpec((tr, m.shape[1]), lambda i: (i, 0)) for m in mats],
        out_specs=pl.BlockSpec((N_DEV, tr, w), lambda i: (0, i, 0)),
        out_shape=jax.ShapeDtypeStruct((N_DEV, R, w), mats[0].dtype),
        compiler_params=_params(("parallel",)),
    )(*mats)


def _my_place():
    return lax.axis_index("x"), lax.axis_index("y"), lax.axis_index("c")


def _gathered_shape(shape, kind):
    r, c = shape
    return {"blocks": (N_DEV, r, c), "rows": (N_DEV * r, c), "cols": (r, N_DEV * c)}[kind]


def _gather_window(ref, kind, shape, j):
    r, c = shape
    if kind == "blocks":
        return ref.at[j]
    if kind == "rows":
        return ref.at[pl.ds(pl.multiple_of(j * r, r), r), :]
    return ref.at[:, pl.ds(pl.multiple_of(j * c, c), c)]


def _gather(srcs, kinds, *, name):
    n = len(srcs)
    shapes = [s.shape for s in srcs]
    per = 7

    def body(*refs):
        src_refs, dst_refs = refs[:n], refs[n:2 * n]
        send_sems, recv_sems, local_sems = refs[2 * n:]
        x, y, c = _my_place()
        me, sibling = (x, y, c), (x, y, 1 - c)
        chips = [(1 - x, y), (x, 1 - y), (1 - x, 1 - y)]

        def at(i, px, py, pc):
            return _gather_window(dst_refs[i], kinds[i], shapes[i], 4 * px + 2 * py + pc)

        def copy(i, k, block, to, src=None):
            return pltpu.make_async_remote_copy(
                src_ref=at(i, *block) if src is None else src, dst_ref=at(i, *block),
                send_sem=send_sems.at[per * i + k], recv_sem=recv_sems.at[per * i + k], device_id=to, device_id_type=MESH)

        mine = [pltpu.make_async_copy(src_refs[i], at(i, *me), local_sems.at[i]) for i in range(n)]
        for cp in mine:
            cp.start()
        started = []
        for i in range(n):
            first = [copy(i, 0, me, sibling, src=src_refs[i])]
            first += [copy(i, 1 + j, me, (*chip, c), src=src_refs[i]) for j, chip in enumerate(chips)]
            for cp in first:
                cp.start()
            started += first
        for i in range(n):
            for j, chip in enumerate(chips):
                copy(i, 1 + j, (*chip, c), me).wait_recv()
                fwd = copy(i, 4 + j, (*chip, c), sibling)
                fwd.start()
                started.append(fwd)
        for i in range(n):
            copy(i, 0, sibling, me).wait_recv()
            for j, chip in enumerate(chips):
                copy(i, 4 + j, (*chip, 1 - c), me).wait_recv()
        for cp in started:
            cp.wait_send()
        for cp in mine:
            cp.wait()

    return pl.pallas_call(
        body, name=name,
        out_shape=[jax.ShapeDtypeStruct(_gathered_shape(s.shape, k), s.dtype) for s, k in zip(srcs, kinds)],
        in_specs=[ANY] * n, out_specs=[ANY] * n,
        scratch_shapes=[pltpu.SemaphoreType.DMA((per * n,)), pltpu.SemaphoreType.DMA((per * n,)),
                        pltpu.SemaphoreType.DMA((n,))],
    )(*srcs)


HBM = pl.BlockSpec(memory_space=pltpu.HBM)
SEM = pl.BlockSpec(memory_space=pltpu.SEMAPHORE)
TOKEN = jax.ShapeDtypeStruct((SUBLANES, LANES), F32)
TOKEN_SPEC = pl.BlockSpec(memory_space=pltpu.VMEM)
SPLIT_PARAMS = pltpu.CompilerParams(has_side_effects=pltpu.SideEffectType.DATAFLOW_SIDE_EFFECTING)


def _in_hbm(x):
    return pltpu.with_memory_space_constraint(x, pltpu.HBM)


def _hbm_like(shape, dtype):
    return pltpu.HBM(shape, dtype)


def _place_own(shards, kinds, dtypes, *, name, deps=()):
    n = len(shards)
    shapes = [s.shape for s in shards]

    def body(*refs):
        s_refs, land_refs, bufs, sems = refs[:n], refs[n:2 * n], refs[2 * n:3 * n], refs[3 * n]
        x, y, c = _my_place()
        copies = []
        for i in range(n):
            bufs[i][...] = s_refs[i][...].astype(dtypes[i])
            copies.append(pltpu.make_async_copy(
                bufs[i], _gather_window(land_refs[i], kinds[i], shapes[i], 4 * x + 2 * y + c), sems.at[i]))
        for cp in copies:
            cp.start()
        for cp in copies:
            cp.wait()

    body, dep_specs, dep_args = _with_deps(body, n, deps)
    return pl.pallas_call(
        body, name=name,
        out_shape=[jax.ShapeDtypeStruct(_gathered_shape(s, k), d) for s, k, d in zip(shapes, kinds, dtypes)],
        in_specs=[pl.BlockSpec(memory_space=pltpu.VMEM)] * n + dep_specs, out_specs=[ANY] * n,
        scratch_shapes=[pltpu.VMEM(s, d) for s, d in zip(shapes, dtypes)] + [pltpu.SemaphoreType.DMA((n,))],
        compiler_params=_params(),
    )(*shards, *dep_args)


def _gather_start(lands, kinds, shapes, after=(), *, name):
    n = len(lands)
    n_after = len(after)

    def body(*refs):
        land_refs = refs[:n]
        send_sems, recv_sems = refs[n + n_after], refs[n + n_after + 1]
        x, y, c = _my_place()
        targets = [(x, y, 1 - c), (1 - x, y, c), (x, 1 - y, c), (1 - x, 1 - y, c)]
        for i in range(n):
            own = _gather_window(land_refs[i], kinds[i], shapes[i], 4 * x + 2 * y + c)
            for k, to in enumerate(targets):
                pltpu.make_async_remote_copy(
                    src_ref=own, dst_ref=own, send_sem=send_sems.at[4 * i + k], recv_sem=recv_sems.at[4 * i + k],
                    device_id=to, device_id_type=MESH).start()
        refs[-1][...] = jnp.zeros_like(refs[-1])

    outs = pl.pallas_call(
        body, name=name,
        out_shape=[pltpu.SemaphoreType.DMA((4 * n,)), pltpu.SemaphoreType.DMA((4 * n,))]
        + [_hbm_like(a.shape, a.dtype) for a in lands] + [TOKEN],
        in_specs=[HBM] * n + [ANY] * n_after, out_specs=[SEM, SEM] + [HBM] * n + [TOKEN_SPEC],
        input_output_aliases={i: 2 + i for i in range(n)},
        compiler_params=SPLIT_PARAMS,
    )(*[_in_hbm(a) for a in lands], *after)
    return outs[0], outs[1], outs[2:2 + n], outs[-1]


def _gather_forward(recv_sems, lands, kinds, shapes, after, *, name):
    n = len(lands)

    def body(*refs):
        recv_ref, land_refs = refs[0], refs[1:1 + n]
        fwd_send, fwd_recv = refs[2 + n], refs[3 + n]
        token = refs[-1]
        x, y, c = _my_place()
        chips = [(1 - x, y), (x, 1 - y), (1 - x, 1 - y)]
        for i in range(n):
            for j, (px, py) in enumerate(chips):
                block = _gather_window(land_refs[i], kinds[i], shapes[i], 4 * px + 2 * py + c)
                pltpu.make_async_remote_copy(
                    src_ref=block, dst_ref=block, send_sem=fwd_send.at[3 * i + j], recv_sem=recv_ref.at[4 * i + 1 + j],
                    device_id=(px, py, c), device_id_type=MESH).wait_recv()
                pltpu.make_async_remote_copy(
                    src_ref=block, dst_ref=block, send_sem=fwd_send.at[3 * i + j], recv_sem=fwd_recv.at[3 * i + j],
                    device_id=(x, y, 1 - c), device_id_type=MESH).start()
        token[...] = jnp.zeros_like(token)

    outs = pl.pallas_call(
        body, name=name,
        out_shape=[pltpu.SemaphoreType.DMA((3 * n,)), pltpu.SemaphoreType.DMA((3 * n,))]
        + [_hbm_like(a.shape, a.dtype) for a in lands] + [TOKEN],
        in_specs=[SEM] + [HBM] * n + [ANY], out_specs=[SEM, SEM] + [HBM] * n + [TOKEN_SPEC],
        input_output_aliases={1 + i: 2 + i for i in range(n)},
        compiler_params=SPLIT_PARAMS,
    )(recv_sems, *lands, after)
    return outs[0], outs[1], outs[2:2 + n], outs[-1]


def _gather_finish(send_sems, recv_sems, fwd_send, fwd_recv, lands, kinds, shapes, after, *, name):
    n = len(lands)

    def body(*refs):
        send_ref, recv_ref, fsend_ref, frecv_ref = refs[:4]
        land_refs = refs[4:4 + n]
        x, y, c = _my_place()
        chips = [(1 - x, y), (x, 1 - y), (1 - x, 1 - y)]
        sibling = (x, y, 1 - c)
        for i in range(n):
            def window(j):
                return _gather_window(land_refs[i], kinds[i], shapes[i], j)

            mine, theirs = window(4 * x + 2 * y + c), window(4 * x + 2 * y + (1 - c))
            pltpu.make_async_remote_copy(src_ref=mine, dst_ref=theirs, send_sem=send_ref.at[4 * i],
                                         recv_sem=recv_ref.at[4 * i], device_id=sibling, device_id_type=MESH).wait_recv()
            for j, (px, py) in enumerate(chips):
                block = window(4 * px + 2 * py + (1 - c))
                pltpu.make_async_remote_copy(src_ref=block, dst_ref=block, send_sem=fsend_ref.at[3 * i + j],
                                             recv_sem=frecv_ref.at[3 * i + j], device_id=sibling,
                                             device_id_type=MESH).wait_recv()
            for k in range(4):
                pltpu.make_async_remote_copy(src_ref=mine, dst_ref=mine, send_sem=send_ref.at[4 * i + k],
                                             recv_sem=recv_ref.at[4 * i + k], device_id=sibling,
                                             device_id_type=MESH).wait_send()
            for j, (px, py) in enumerate(chips):
                block = window(4 * px + 2 * py + c)
                pltpu.make_async_remote_copy(src_ref=block, dst_ref=block, send_sem=fsend_ref.at[3 * i + j],
                                             recv_sem=frecv_ref.at[3 * i + j], device_id=sibling,
                                             device_id_type=MESH).wait_send()

    return pl.pallas_call(
        body, name=name,
        out_shape=[_hbm_like(a.shape, a.dtype) for a in lands],
        in_specs=[SEM] * 4 + [HBM] * n + [ANY], out_specs=[HBM] * n,
        input_output_aliases={4 + i: i for i in range(n)},
        compiler_params=SPLIT_PARAMS,
    )(send_sems, recv_sems, fwd_send, fwd_recv, *lands, after)


def _pair_plan(src_ref, land_ref, x, y, c):
    return [(src_ref.at[2 * k + (1 - c)], land_ref.at[k], (x, y, 1 - c)) for k in range(N_CHIPS)]


def _chip_plan(src_ref, land_ref, x, y, c):
    chips = [(1 - x, y), (x, 1 - y), (1 - x, 1 - y)]
    return [(src_ref.at[2 * px + py], land_ref.at[k], (px, py, c)) for k, (px, py) in enumerate(chips)]


def _exchange_copies(plan, per, src_refs, land_refs, send_sems, recv_sems):
    x, y, c = _my_place()
    copies = []
    for i, (s_ref, l_ref) in enumerate(zip(src_refs, land_refs)):
        for q, (src, dst, to) in enumerate(plan(s_ref, l_ref, x, y, c)):
            copies.append(pltpu.make_async_remote_copy(
                src_ref=src, dst_ref=dst, send_sem=send_sems.at[per * i + q], recv_sem=recv_sems.at[per * i + q],
                device_id=to, device_id_type=MESH))
    return copies


def _exchange_start(srcs, plan, per, *, name):
    n = len(srcs)

    def body(*refs):
        src_refs, land_refs = refs[:n], refs[n:2 * n]
        send_sems, recv_sems = refs[2 * n], refs[2 * n + 1]
        for cp in _exchange_copies(plan, per, src_refs, land_refs, send_sems, recv_sems):
            cp.start()
        refs[-1][...] = jnp.zeros_like(refs[-1])

    lands = [lax.empty((per,) + s.shape[1:], s.dtype) for s in srcs]
    outs = pl.pallas_call(
        body, name=name,
        out_shape=[pltpu.SemaphoreType.DMA((per * n,)), pltpu.SemaphoreType.DMA((per * n,))]
        + [_hbm_like(s.shape, s.dtype) for s in srcs] + [_hbm_like(a.shape, a.dtype) for a in lands] + [TOKEN],
        in_specs=[HBM] * (2 * n), out_specs=[SEM, SEM] + [HBM] * (2 * n) + [TOKEN_SPEC],
        input_output_aliases={i: 2 + i for i in range(2 * n)},
        compiler_params=SPLIT_PARAMS,
    )(*[_in_hbm(s) for s in srcs], *[_in_hbm(a) for a in lands])
    return outs[0], outs[1], outs[2:2 + n], outs[2 + n:2 + 2 * n], outs[-1]


def _exchange_wait(send_sems, recv_sems, srcs, lands, plan, per, after, *, name):
    n = len(srcs)
    after = list(after) if isinstance(after, (list, tuple)) else [after]

    def body(*refs):
        send_ref, recv_ref = refs[0], refs[1]
        src_refs, land_refs = refs[2:2 + n], refs[2 + n:2 + 2 * n]
        copies = _exchange_copies(plan, per, src_refs, land_refs, send_ref, recv_ref)
        for cp in copies:
            cp.wait_recv()
        for cp in copies:
            cp.wait_send()

    outs = pl.pallas_call(
        body, name=name,
        out_shape=[_hbm_like(s.shape, s.dtype) for s in srcs] + [_hbm_like(a.shape, a.dtype) for a in lands],
        in_specs=[SEM, SEM] + [HBM] * (2 * n) + [ANY] * len(after), out_specs=[HBM] * (2 * n),
        input_output_aliases={2 + i: i for i in range(2 * n)},
        compiler_params=SPLIT_PARAMS,
    )(send_sems, recv_sems, *srcs, *lands, *after)
    return outs[:n], outs[n:]


REDUCE_BLOCK_BYTES = 2 << 20


def _row_tile(r, c):
    row_bytes = 4 * (-(-c // LANES) * LANES)
    best = r
    for d in range(SUBLANES, r, SUBLANES):
        if r % d == 0 and d * row_bytes <= REDUCE_BLOCK_BYTES:
            best = d
    return best if r * row_bytes > REDUCE_BLOCK_BYTES else r


def _reduce_pair_sum(blocked, recv, place, wire_dtype, *, name):
    _, r, c = blocked.shape
    tr = _row_tile(r, c)

    def body(place_ref, g_ref, r_ref, own_ref, send_ref):
        s = g_ref[...] + r_ref[...]
        send_ref[...] = s.astype(wire_dtype)

        @pl.when(pl.program_id(1) == place_ref[1])
        def _():
            own_ref[...] = s

    return pl.pallas_call(
        body, name=name,
        grid_spec=pltpu.PrefetchScalarGridSpec(
            num_scalar_prefetch=1, grid=(r // tr, N_CHIPS),
            in_specs=[pl.BlockSpec((None, None, tr, c), lambda i, k, place_ref: (k, place_ref[0], i, 0)),
                      pl.BlockSpec((None, tr, c), lambda i, k, place_ref: (k, i, 0))],
            out_specs=[pl.BlockSpec((tr, c), lambda i, k, place_ref: (i, 0)),
                       pl.BlockSpec((None, tr, c), lambda i, k, place_ref: (k, i, 0))]),
        out_shape=[jax.ShapeDtypeStruct((r, c), F32), jax.ShapeDtypeStruct((N_CHIPS, r, c), wire_dtype)],
        compiler_params=_params(("parallel", "arbitrary")),
    )(place, blocked.reshape(N_CHIPS, 2, r, c), recv)


def _chip_sum(own_ref, r_ref):
    return ((own_ref[...] + r_ref[0].astype(F32)) + r_ref[1].astype(F32)) + r_ref[2].astype(F32)


def _reduce_chip_sum(own, recv, *, name):
    r, c = own.shape
    tr = _row_tile(r, c)

    def body(own_ref, r_ref, o_ref):
        o_ref[...] = _chip_sum(own_ref, r_ref)

    return pl.pallas_call(
        body, name=name, grid=(r // tr,),
        in_specs=[pl.BlockSpec((tr, c), lambda i: (i, 0)), pl.BlockSpec((N_CHIPS - 1, tr, c), lambda i: (0, i, 0))],
        out_specs=pl.BlockSpec((tr, c), lambda i: (i, 0)),
        out_shape=jax.ShapeDtypeStruct((r, c), F32),
        compiler_params=_params(("parallel",)),
    )(own, recv)


def _adamw_math(w, g, m, v):
    nm = ADAM_B1 * m + (1.0 - ADAM_B1) * g
    nv = ADAM_B2 * v + (1.0 - ADAM_B2) * (g * g)
    m_hat = nm / (1.0 - ADAM_B1 ** ADAM_STEP)
    v_hat = nv / (1.0 - ADAM_B2 ** ADAM_STEP)
    return -ADAM_LR * (m_hat / (jnp.sqrt(v_hat) + ADAM_EPS) + ADAM_WD * w), nm, nv


def _adamw(w, g, m, v, *, name):
    shape = w.shape
    C = shape[-1]
    R = math.prod(shape[:-1])
    tr = _row_tile(R, C)

    def body(w_ref, g_ref, m_ref, v_ref, d_ref, nm_ref, nv_ref):
        d_ref[...], nm_ref[...], nv_ref[...] = _adamw_math(w_ref[...], g_ref[...], m_ref[...], v_ref[...])

    spec = pl.BlockSpec((tr, C), lambda i: (i, 0))
    outs = pl.pallas_call(
        body, name=name, grid=(R // tr,),
        in_specs=[spec] * 4, out_specs=[spec] * 3,
        out_shape=[jax.ShapeDtypeStruct((R, C), F32)] * 3,
        compiler_params=_params(("parallel",)),
    )(*[a.reshape(R, C) for a in (w, g, m, v)])
    return tuple(o.reshape(shape) for o in outs)


def _reduce_adamw(own, recv, w, m, v, layer, prev, *, name):
    r, c = own.shape
    tr = _row_tile(r, c)
    n_prev = 0 if prev is None else len(prev)

    def body(own_ref, r_ref, w_ref, m_ref, v_ref, *rest):
        g_ref, d_ref, nm_ref, nv_ref = rest[n_prev:]
        g = _chip_sum(own_ref, r_ref)
        g_ref[...] = g
        d_ref[...], nm_ref[...], nv_ref[...] = _adamw_math(w_ref[...], g, m_ref[...], v_ref[...])

    slot = pl.BlockSpec((None, tr, c), lambda i: (layer, i, 0))
    return pl.pallas_call(
        body, name=name, grid=(r // tr,),
        in_specs=[pl.BlockSpec((tr, c), lambda i: (i, 0)), pl.BlockSpec((N_CHIPS - 1, tr, c), lambda i: (0, i, 0)),
                  slot, slot, slot] + [ANY] * n_prev,
        out_specs=[slot] * 4,
        out_shape=[jax.ShapeDtypeStruct((DEPTH, r, c), F32)] * 4,
        input_output_aliases={5 + k: k for k in range(n_prev)},
        compiler_params=_params(("parallel",)),
    )(own, recv, w, m, v, *(prev or ()))


REPLICATED = (("mix_norm", (D_MODEL,)), ("q_norm", (HEAD_DIM,)), ("k_norm", (HEAD_DIM,)), ("sinks", (N_Q_HEADS,)),
              ("sgu_norm", (SGU_WIDTH,)), ("w_s", (SGU_GROUPS, BLOCK, BLOCK)), ("b_s", (SGU_GROUPS, BLOCK)),
              ("ffn_norm", (D_MODEL,)), ("conv_b", (2 * D_FF,)))
TRANSPOSED = ("w_in", "w_up")
SHARDED = (("w_in", "rows"), ("w_oa", "cols"), ("w_ob", "cols"), ("w_out", "rows"), ("w_up", "rows"),
           ("conv_w", "blocks"), ("w_down", "rows"))
WEIGHT_ORDER = ("mix_norm", "w_in", "q_norm", "k_norm", "sinks", "sgu_norm", "w_s", "b_s", "w_oa", "w_ob", "w_out",
                "ffn_norm", "w_up", "conv_w", "conv_b", "w_down")
MIXER_WEIGHTS = ["w_in", "w_oa", "w_ob", "w_out"]
FFN_WEIGHTS = ["w_up", "conv_w", "w_down"]


def _small_layout():
    segs, off = {}, 0
    for l in range(DEPTH):
        for name, shape in REPLICATED:
            n = math.prod(shape)
            segs[(l, name)] = (off, n)
            off += n
    per_dev = -(-off // (N_DEV * SUBLANES * LANES)) * SUBLANES * LANES
    return segs, off, per_dev


def _pack_small(grads, loss_part):
    ssegs, total, per_dev = _small_layout()
    flat = jnp.concatenate([grads[l][name].reshape(-1) for (l, name) in ssegs] + [loss_part.reshape(1)])
    return jnp.pad(flat, (0, N_DEV * per_dev - total - 1)).reshape(N_DEV, per_dev // LANES, LANES)


def _unpack_small(gathered):
    ssegs, total, _ = _small_layout()
    flat = gathered.reshape(-1)
    shapes = dict(REPLICATED)
    small = {name: jnp.stack([flat[ssegs[(l, name)][0]:ssegs[(l, name)][0] + ssegs[(l, name)][1]].reshape(shapes[name])
                              for l in range(DEPTH)]) for name, _ in REPLICATED}
    return small, flat[total]


def kernel(x, mix_norm, w_in, q_norm, k_norm, sinks, sgu_norm, w_s, b_s, w_oa, w_ob, w_out, ffn_norm, w_up, conv_w, conv_b, w_down, loss_target, m_mix_norm, m_w_in, m_q_norm, m_k_norm, m_sinks, m_sgu_norm, m_w_s, m_b_s, m_w_oa, m_w_ob, m_w_out, m_ffn_norm, m_w_up, m_conv_w, m_conv_b, m_w_down, v_mix_norm, v_w_in, v_q_norm, v_k_norm, v_sinks, v_sgu_norm, v_w_s, v_b_s, v_w_oa, v_w_ob, v_w_out, v_ffn_norm, v_w_up, v_conv_w, v_conv_b, v_w_down):
    W = dict(mix_norm=mix_norm, w_in=w_in, q_norm=q_norm, k_norm=k_norm, sinks=sinks, sgu_norm=sgu_norm, w_s=w_s, b_s=b_s,
             w_oa=w_oa, w_ob=w_ob, w_out=w_out, ffn_norm=ffn_norm, w_up=w_up, conv_w=conv_w, conv_b=conv_b, w_down=w_down)
    M = dict(mix_norm=m_mix_norm, w_in=m_w_in, q_norm=m_q_norm, k_norm=m_k_norm, sinks=m_sinks, sgu_norm=m_sgu_norm,
             w_s=m_w_s, b_s=m_b_s, w_oa=m_w_oa, w_ob=m_w_ob, w_out=m_w_out, ffn_norm=m_ffn_norm, w_up=m_w_up,
             conv_w=m_conv_w, conv_b=m_conv_b, w_down=m_w_down)
    V = dict(mix_norm=v_mix_norm, w_in=v_w_in, q_norm=v_q_norm, k_norm=v_k_norm, sinks=v_sinks, sgu_norm=v_sgu_norm,
             w_s=v_w_s, b_s=v_b_s, w_oa=v_w_oa, w_ob=v_w_ob, w_out=v_w_out, ffn_norm=v_ffn_norm, w_up=v_w_up,
             conv_w=v_conv_w, conv_b=v_conv_b, w_down=v_w_down)
    n_seq, seq, d_model = x.shape
    tokens = n_seq * seq
    mx, my, mc = _my_place()
    place = jnp.stack([mc, 2 * mx + my]).astype(jnp.int32)
    half = N_DEV // 2
    kind_of = dict(SHARDED)
    for name in TRANSPOSED:
        W[name], M[name], V[name] = (jnp.swapaxes(t[name], 1, 2) for t in (W, M, V))

    gather_groups = [[(0, MIXER_WEIGHTS[0])], [(0, n) for n in MIXER_WEIGHTS[1:]], [(0, n) for n in FFN_WEIGHTS],
                     [(1, n) for n in MIXER_WEIGHTS], [(1, n) for n in FFN_WEIGHTS]]
    started, in_flight = {}, {}
    weights = []
    for l in range(DEPTH):
        w = {name: W[name][l] for name, _ in REPLICATED}
        w["cb_g"], w["cb_v"] = W["conv_b"][l][:D_FF], W["conv_b"][l][D_FF:]
        w["bias_full"] = jnp.repeat(W["b_s"][l].T, SGU_WIDTH // SGU_GROUPS, axis=1)
        weights.append(w)

    def gather_start(gi, after=()):
        shards = [W[name][l] for l, name in gather_groups[gi]]
        kinds = [kind_of[name] for _, name in gather_groups[gi]]
        shapes = [s.shape for s in shards]
        lands = _place_own(shards, kinds, [F32 if name == "conv_w" else BF16 for _, name in gather_groups[gi]],
                           name=f"gather_weights_own_{gi}", deps=after)
        send, recv, lands, token = _gather_start(lands, kinds, shapes, after, name=f"gather_weights_start_{gi}")
        started[gi] = dict(sems=(send, recv), lands=lands, kinds=kinds, shapes=shapes)
        return token

    def gather_forward(gi, after):
        st = started[gi]
        in_flight[gi] = _gather_forward(st["sems"][1], st["lands"], st["kinds"], st["shapes"], after,
                                        name=f"gather_weights_forward_{gi}")
        return in_flight[gi][3]

    def gather_finish(gi, after):
        st = started.pop(gi)
        fwd_send, fwd_recv, lands_g, _ = in_flight.pop(gi)
        whole = _gather_finish(st["sems"][0], st["sems"][1], fwd_send, fwd_recv, lands_g, st["kinds"], st["shapes"], after,
                               name=f"gather_weights_finish_{gi}")
        for (l, name), arr in zip(gather_groups[gi], whole):
            w = weights[l]
            if name in TRANSPOSED:
                w[name + "_t"] = arr
            elif name == "conv_w":
                w["cw_g"] = arr[:half].transpose(1, 0, 2).reshape(3, D_FF)
                w["cw_v"] = arr[half:].transpose(1, 0, 2).reshape(3, D_FF)
            else:
                w[name] = arr

    reduce_state, results = {}, {}
    wire = {"conv_w": F32, "small": F32}

    def reduce_begin(key, names, arrays):
        send, recv, srcs_, lands_, token = _exchange_start(arrays, _pair_plan, N_CHIPS, name=f"reduce_pair_start_{key}")
        reduce_state[key] = dict(names=names, pair=(send, recv, srcs_, lands_))
        return [token]

    def reduce_pair(key, after):
        st = reduce_state[key]
        send, recv, srcs_, lands_ = st.pop("pair")
        blocked_, from_sibling = _exchange_wait(send, recv, srcs_, lands_, _pair_plan, N_CHIPS, after,
                                                name=f"reduce_pair_wait_{key}")
        sums = [_reduce_pair_sum(b, r, place, wire.get(n if isinstance(n, str) else n[1], BF16),
                                 name=f"reduce_pair_sum_{key}_{i}")
                for i, (n, b, r) in enumerate(zip(st["names"], blocked_, from_sibling))]
        st["own"] = [s[0] for s in sums]
        *st["chip"], token = _exchange_start([s[1] for s in sums], _chip_plan, N_CHIPS - 1, name=f"reduce_chip_start_{key}")
        return [token]

    def reduce_end(key, after):
        st = reduce_state.pop(key)
        send, recv, srcs_, lands_ = st["chip"]
        _, from_chips = _exchange_wait(send, recv, srcs_, lands_, _chip_plan, N_CHIPS - 1, after,
                                       name=f"reduce_chip_wait_{key}")
        done = []
        for n, own, got in zip(st["names"], st["own"], from_chips):
            if n == "small":
                results["small"] = _reduce_chip_sum(own, got, name="reduce_chip_sum_small")
            else:
                l, name = n
                results[name] = _reduce_adamw(own, got, W[name], M[name], V[name], l, results.get(name),
                                              name=f"l{l}_reduce_adamw_{name}")
                done.append(results[name][0])
        return done

    def sched(point, l, carry, g=None):
        deps = []
        if point == "begin":
            token = ()
            for gi in range(len(gather_groups)):
                token = [gather_start(gi, token)]
            deps = token
        elif point == "fwd_start" and l == 0:
            gather_finish(0, gather_forward(0, carry))
        elif point == "fwd_att" and l == 0:
            gather_finish(1, gather_forward(1, carry))
            deps = [gather_forward(2, carry)]
        elif point == "fwd_mixer_done" and l == 0:
            gather_finish(2, carry)
        elif point == "fwd_conv" and l == 0:
            deps = [gather_forward(3, carry)]
        elif point == "fwd_start" and l == 1:
            gather_finish(3, carry)
        elif point == "fwd_att" and l == 1:
            deps = [gather_forward(4, carry)]
        elif point == "fwd_mixer_done" and l == 1:
            gather_finish(4, carry)
        elif point == "bwd_ffn_grads":
            conv_w = jnp.concatenate([g[k].reshape(3, half, W_UP_SHARD).transpose(1, 0, 2) for k in ("cw_g", "cw_v")])
            deps = reduce_begin(
                f"l{l}_ffn", [(l, "w_down"), (l, "w_up"), (l, "conv_w")],
                [g["w_down"].reshape(N_DEV, D_FF // N_DEV, D_MODEL),
                 g["w_up_t"].reshape(N_DEV, W_UP_SHARD, D_MODEL), conv_w])
        elif point == "bwd_merge":
            deps = reduce_pair(f"l{l}_ffn", carry)
        elif point == "bwd_out_grads":
            deps = reduce_begin(
                f"l{l}_out", [(l, "w_out"), (l, "w_oa"), (l, "w_ob")],
                [g["w_out"].reshape(N_DEV, D_MODEL // N_DEV, D_MODEL),
                 _disassemble((g["w_oa"],), LANES, _w_o_moves(), name=f"l{l}_split_dw_oa"),
                 _disassemble((g["w_ob"],), LANES, _w_o_moves(), name=f"l{l}_split_dw_ob")])
        elif point == "bwd_att":
            deps = reduce_pair(f"l{l}_out", carry)
        elif point == "bwd_w_in_grad":
            deps = reduce_begin(f"l{l}_in", [(l, "w_in")], [g["w_in_t"].reshape(N_DEV, W_IN_SHARD, D_MODEL)])
        elif point == "bwd_dh":
            deps = reduce_pair(f"l{l}_in", carry)
        return deps

    loss_part, dx, grads, last_deps = _local_step(x.reshape(tokens, d_model), loss_target.reshape(tokens, d_model),
                                                  weights, sched, n_seq=n_seq, seq=seq)
    for g in grads:
        g["conv_b"] = jnp.concatenate([g["cb_g"], g["cb_v"]])
    after = [dx, *last_deps, *reduce_begin("small", ["small"], [_pack_small(grads, loss_part)])]
    for key in [f"l{l}_{part}" for l in reversed(range(DEPTH)) for part in ("ffn", "out", "in")][:-1]:
        after = reduce_end(key, after)
    after = reduce_end("l0_in", after + reduce_pair("small", after))
    reduce_end("small", after)

    G, delta, new_m, new_v = {}, {}, {}, {}
    for name, _ in SHARDED:
        outs = [jnp.swapaxes(o, 1, 2) for o in results[name]] if name in TRANSPOSED else results[name]
        G[name], delta[name], new_m[name], new_v[name] = outs
    small, loss = _unpack_small(_gather([results["small"]], ["blocks"], name="gather_small_grads")[0])
    G.update(small)
    for name, _ in REPLICATED:
        delta[name], new_m[name], new_v[name] = _adamw(W[name], G[name], M[name], V[name], name=f"adamw_{name}")
    return (loss, dx.reshape(n_seq, seq, d_model), *[G[n] for n in WEIGHT_ORDER], *[delta[n] for n in WEIGHT_ORDER],
            *[new_m[n] for n in WEIGHT_ORDER], *[new_v[n] for n in WEIGHT_ORDER])
```

```python
import math

import jax
import jax.numpy as jnp
from jax import lax
from jax.experimental import pallas as pl
from jax.experimental.pallas import tpu as pltpu

F32 = jnp.float32
BF16 = jnp.bfloat16
ACT_DTYPE = BF16
MESH = pl.DeviceIdType.MESH

DEPTH = 2
D_MODEL = 1024
N_Q_HEADS = 8
HEAD_DIM = 64
ATT_WIDTH = 512
KV_WIDTH = 128
BLOCK = 128
SGU_WIDTH = 512
SGU_GROUPS = 8
IN_WIDTH = 3840
D_FF = 2816
NORM_EPS = 1e-6
NEG_INF = -1e30
ATT_SCALE = HEAD_DIM ** -0.5
ALIBI_SLOPES = tuple(2.0 ** (-(h + 1)) for h in range(N_Q_HEADS))
ADAM_LR, ADAM_B1, ADAM_B2, ADAM_EPS, ADAM_WD, ADAM_STEP = 0.001, 0.9, 0.999, 1e-08, 0.01, 10
N_DEV = 8
N_CHIPS = 4

QKV_WIDTH = ATT_WIDTH + 2 * KV_WIDTH
COL_SUV, COL_GA, COL_GB, COL_QKV = 0, 1024, 2048, 3072
W_IN_ROTATE = (1, IN_WIDTH // QKV_WIDTH)

LANES = 128
SUBLANES = 8
VMEM_LIMIT_V7X = 56 * 1024 * 1024
GELU_C = math.sqrt(2.0 / math.pi)
GELU_K = 0.044715
ANY = pl.BlockSpec(memory_space=pl.ANY)


def _params(sem=None):
    return pltpu.CompilerParams(dimension_semantics=sem, vmem_limit_bytes=VMEM_LIMIT_V7X)


def _sigmoid(x):
    return 1.0 / (1.0 + jnp.exp(-x))


def _gelu(x):
    th = jnp.tanh(GELU_C * (x + GELU_K * x * x * x))
    return 0.5 * x * (1.0 + th)


def _gelu_and_grad(x):
    x2 = x * x
    th = jnp.tanh(GELU_C * (x + GELU_K * x2 * x))
    g = 0.5 * x * (1.0 + th)
    dg = 0.5 * (1.0 + th) + 0.5 * x * (1.0 - th * th) * (GELU_C * (1.0 + 3.0 * GELU_K * x2))
    return g, dg


def _dot(a, b, dims):
    return lax.dot_general(a, b, (dims, ((), ())), preferred_element_type=F32)


def _dot_nn(a, b):
    return _dot(a, b, ((1,), (0,)))


def _dot_nt(a, b):
    return _dot(a, b, ((1,), (1,)))


def _dot_tn(a, b):
    return _dot(a, b, ((0,), (0,)))


def _lo_mask(shape):
    return lax.broadcasted_iota(jnp.int32, shape, len(shape) - 1) < (LANES // 2)


def _half_sums(x, lo):
    s_lo = jnp.sum(jnp.where(lo, x, 0.0), axis=-1, keepdims=True)
    s_all = jnp.sum(x, axis=-1, keepdims=True)
    return jnp.where(lo, s_lo, s_all - s_lo)


def _dup_half(x, half, lo):
    r = pltpu.roll(x, LANES // 2, axis=1)
    return jnp.where(lo, x, r) if half == 0 else jnp.where(lo, r, x)


def _with_deps(body, n_in, deps):
    k = len(deps)
    if not k:
        return body, [], ()

    def skipping(*refs):
        return body(*refs[:n_in], *refs[n_in + k:])

    return skipping, [ANY] * k, tuple(deps)


MM_VMEM_BUDGET = 40 * 1024 * 1024
MM_MAX_TILE = 1408
MM_MAX_TK = 4096
MM_STEP_BYTES = 1 << 20


def _divisors(n, step, cap):
    return [d for d in range(step, min(n, cap) + 1, step) if n % d == 0] or [n]


def _mm_tiles(M, N, K, out_bytes, tm_divides, tn_divides):
    best = None
    for tm in _divisors(M, LANES, MM_MAX_TILE):
        for tn in _divisors(N, LANES, MM_MAX_TILE):
            if tm_divides % tm or tn_divides % tn:
                continue
            for tk in _divisors(K, 4 * LANES, MM_MAX_TK):
                vmem = 4 * (tm * tk + tk * tn) + 2 * tm * tn * out_bytes + (0 if tk == K else 4 * tm * tn)
                if vmem > MM_VMEM_BUDGET:
                    continue
                traffic = 2 * M * K * (N // tn) + 2 * K * N * (M // tm) + M * N * out_bytes
                cost = traffic + (K // tk - 1) * 8 * M * N + (M // tm) * (N // tn) * (K // tk) * MM_STEP_BYTES
                if best is None or cost < best[0]:
                    best = (cost, tm, tn, tk)
    assert best is not None, (M, N, K)
    return best[1:]


def _mm(a, b, *, mode, out_dtype, name, deps=(), b_rows=(0, None), rotate=None, out_rows=(0, None), out_prev=None):
    b_first, b_count = b_rows
    if mode == "nn":
        (M, K), N = a.shape, b.shape[1]
    elif mode == "nt":
        (M, K), N = a.shape, (b.shape[0] if b_count is None else b_count)
    else:
        (K, M), N = a.shape, b.shape[1]
    shift, period = rotate or (0, 1)
    assert period == 1 or mode == "nt"
    out_first, out_total = out_rows[0], (M if out_rows[1] is None else out_rows[1])
    tm, tn, tk = _mm_tiles(M, N, K, jnp.dtype(out_dtype).itemsize, math.gcd(M, out_first),
                           math.gcd(N // period, b_first if mode == "nt" else 0))
    gm, gn, gk = M // tm, N // tn, K // tk

    def turned(j):
        per = N // period // tn
        return ((j // per + shift) % period) * per + j % per if period > 1 else j

    if mode == "nn":
        a_spec = pl.BlockSpec((tm, tk), lambda i, j, k: (i, k))
        b_spec = pl.BlockSpec((tk, tn), lambda i, j, k: (k + b_first // tk, j))
        contract = ((1,), (0,))
    elif mode == "nt":
        a_spec = pl.BlockSpec((tm, tk), lambda i, j, k: (i, k))
        b_spec = pl.BlockSpec((tn, tk), lambda i, j, k: (turned(j) + b_first // tn, k))
        contract = ((1,), (1,))
    else:
        a_spec = pl.BlockSpec((tk, tm), lambda i, j, k: (k, i))
        b_spec = pl.BlockSpec((tk, tn), lambda i, j, k: (k, j))
        contract = ((0,), (0,))
    o_spec = pl.BlockSpec((tm, tn), lambda i, j, k: (i + out_first // tm, j))
    assert b_first % (tk if mode == "nn" else tn) == 0 and out_first % tm == 0, (name, tm, tn, tk)
    n_prev = 0 if out_prev is None else 1

    def body(a_ref, b_ref, *rest):
        o_ref = rest[n_prev]
        part = _dot(a_ref[...].astype(BF16), b_ref[...].astype(BF16), contract)
        if gk == 1:
            o_ref[...] = part.astype(out_dtype)
            return
        acc_ref = rest[n_prev + 1]
        k = pl.program_id(2)

        @pl.when(k == 0)
        def _():
            acc_ref[...] = part

        @pl.when(k > 0)
        def _():
            acc_ref[...] += part

        @pl.when(k == gk - 1)
        def _():
            o_ref[...] = acc_ref[...].astype(out_dtype)

    body, dep_specs, dep_args = _with_deps(body, 2 + n_prev, deps)
    return pl.pallas_call(
        body,
        name=name,
        grid=(gm, gn, gk),
        in_specs=[a_spec, b_spec] + [ANY] * n_prev + dep_specs,
        out_specs=o_spec,
        out_shape=jax.ShapeDtypeStruct((out_total, N), out_dtype),
        input_output_aliases={2: 0} if n_prev else {},
        scratch_shapes=[] if gk == 1 else [pltpu.VMEM((tm, tn), F32)],
        compiler_params=_params(("parallel", "parallel", "arbitrary")),
    )(a, b, *([out_prev] if n_prev else []), *dep_args)


def _mm_tn_parts(parts, at, b, *, name):
    K, N = b.shape
    n = len(parts)
    tm = math.gcd(*[p.shape[1] for p in parts], *at)
    tiles = [p.shape[1] // tm for p in parts]
    first = [sum(tiles[:p]) for p in range(n)]

    def mine(i, p):
        return jnp.logical_and(i >= first[p], i < first[p] + tiles[p])

    def out_tile(i):
        t = 0
        for p in range(n):
            t = jnp.where(mine(i, p), at[p] // tm + i - first[p], t)
        return t

    def body(*refs):
        a_refs, b_ref, o_ref = refs[:n], refs[n], refs[n + 1]
        for p in range(n):
            @pl.when(mine(pl.program_id(0), p))
            def _(p=p):
                o_ref[...] = _dot_tn(a_refs[p][...], b_ref[...])

    return pl.pallas_call(
        body, name=name, grid=(sum(tiles),),
        in_specs=[pl.BlockSpec((K, tm), lambda i, p=p: (0, jnp.clip(i - first[p], 0, tiles[p] - 1))) for p in range(n)]
        + [pl.BlockSpec((K, N), lambda i: (0, 0), pipeline_mode=pl.Buffered(1))],
        out_specs=pl.BlockSpec((tm, N), lambda i: (out_tile(i), 0)),
        out_shape=jax.ShapeDtypeStruct((sum(p.shape[1] for p in parts), N), F32),
        compiler_params=_params(("arbitrary",)),
    )(*parts, b)


def _mm_rows(a, b, *, mode, fn, out_dtypes, rows=(), vecs=(), reduce=False, name, deps=(), b_rows=(0, None), a_at=None):
    parts = a if a_at is not None else (a,)
    starts = a_at if a_at is not None else (0,)
    n_parts = len(parts)
    M, K = parts[0].shape[0], sum(p.shape[1] for p in parts)
    b_first, b_count = b_rows[0], (b.shape[0] if b_rows[1] is None else b_rows[1])
    N = b.shape[1] if mode == "nn" else b_count
    contract = ((1,), (0,)) if mode == "nn" else ((1,), (1,))
    n_rows, n_vecs, n_out = len(rows), len(vecs), len(out_dtypes)
    out_bytes = sum(jnp.dtype(d).itemsize for d in out_dtypes)
    tm = max(t for t in _divisors(M, LANES, MM_MAX_TILE)
             if 4 * t * K + 2 * K * N + 2 * t * N * (4 * n_rows + out_bytes) <= MM_VMEM_BUDGET)
    assert b_first % b_count == 0 and (a_at is None or mode == "nn")

    def body(*refs):
        a_refs, b_ref, rest = refs[:n_parts], refs[n_parts], refs[n_parts + 1:]
        row_refs, vec_refs = rest[:n_rows], rest[n_rows:n_rows + n_vecs]
        out_refs = rest[n_rows + n_vecs:]
        if a_at is None:
            acc = _dot(a_refs[0][...], b_ref[...], contract)
        else:
            acc = sum(_dot(r[...], b_ref[at:at + r.shape[1], :], contract) for r, at in zip(a_refs, starts))
        res = fn(acc, *[r[...] for r in row_refs], *[v[...] for v in vec_refs])
        for o_ref, val in zip(out_refs[:n_out], res):
            o_ref[...] = val.astype(o_ref.dtype)
        if reduce:
            @pl.when(pl.program_id(0) == 0)
            def _():
                out_refs[n_out][...] = res[n_out]

            @pl.when(pl.program_id(0) > 0)
            def _():
                out_refs[n_out][...] += res[n_out]

    row = pl.BlockSpec((tm, N), lambda i: (i, 0))
    vec = pl.BlockSpec((1, N), lambda i: (0, 0))
    body, dep_specs, dep_args = _with_deps(body, n_parts + 1 + n_rows + n_vecs, deps)
    return pl.pallas_call(
        body, name=name, grid=(M // tm,),
        in_specs=[pl.BlockSpec((tm, p.shape[1]), lambda i: (i, 0)) for p in parts]
        + [pl.BlockSpec((b_count, b.shape[1]), lambda i: (b_first // b_count, 0), pipeline_mode=pl.Buffered(1))]
        + [row] * n_rows + [vec] * n_vecs + dep_specs,
        out_specs=[row] * n_out + [vec] * reduce,
        out_shape=[jax.ShapeDtypeStruct((M, N), d) for d in out_dtypes] + [jax.ShapeDtypeStruct((1, N), F32)] * reduce,
        compiler_params=_params(("arbitrary",)),
    )(*parts, b, *rows, *[v.reshape(1, N) for v in vecs], *dep_args)


def _rms(x, gain):
    return x * lax.rsqrt(jnp.mean(x * x, axis=-1, keepdims=True) + NORM_EPS) * gain


def _residual_then_norm(acc, x, gain):
    x_out = x + acc
    return x_out, _rms(x_out, gain)


def _residual_then_loss(acc, x, target):
    err = (x + acc) - target
    dy = err * (1.0 / D_MODEL)
    return dy, dy, jnp.sum(err * err, axis=0, keepdims=True) * (0.5 / D_MODEL)


def _rms_bwd_rows(dh, x, dres, gain):
    r = lax.rsqrt(jnp.mean(x * x, axis=-1, keepdims=True) + NORM_EPS)
    xh = x * r
    dxh = dh * gain
    dx = dres + r * (dxh - xh * jnp.mean(dxh * xh, axis=-1, keepdims=True))
    return dx, dx, jnp.sum(dh * xh, axis=0, keepdims=True)


def _rms_fwd(x, gain, *, name, tm=512, deps=()):
    T, D = x.shape

    def body(x_ref, g_ref, h_ref):
        xv = x_ref[...]
        r = lax.rsqrt(jnp.mean(xv * xv, axis=-1, keepdims=True) + NORM_EPS)
        h_ref[...] = (xv * r * g_ref[...]).astype(BF16)

    body, dep_specs, dep_args = _with_deps(body, 2, deps)
    return pl.pallas_call(
        body, name=name, grid=(T // tm,),
        in_specs=[pl.BlockSpec((tm, D), lambda i: (i, 0)), pl.BlockSpec((1, D), lambda i: (0, 0))] + dep_specs,
        out_specs=pl.BlockSpec((tm, D), lambda i: (i, 0)),
        out_shape=jax.ShapeDtypeStruct((T, D), BF16),
        compiler_params=_params(("parallel",)),
    )(x, gain.reshape(1, D), *dep_args)


def _head_norm(x, gain2, lo):
    ms = _half_sums(x * x, lo) * (1.0 / HEAD_DIM)
    r = lax.rsqrt(ms + NORM_EPS)
    xh = x * r
    return xh * gain2, xh, r


def _head_norm_bwd(xh, r, gain2, dy, lo):
    dxh = dy * gain2
    dx = r * (dxh - xh * (_half_sums(dxh * xh, lo) * (1.0 / HEAD_DIM)))
    return dx, dy * xh


Q_GROUP = N_Q_HEADS // 2
GROUP_ROWS = Q_GROUP * BLOCK
ATT_SCRATCH = (pltpu.VMEM((2, 2, GROUP_ROWS, BLOCK), F32), pltpu.VMEM((2, GROUP_ROWS, 1), F32))


def _att_consts(sink_ref, bias_ref, sinkcol_ref):
    row = lax.broadcasted_iota(jnp.int32, (GROUP_ROWS, BLOCK), 0)
    kj = lax.broadcasted_iota(jnp.int32, (GROUP_ROWS, BLOCK), 1)
    head = row // BLOCK
    head_col = lax.broadcasted_iota(jnp.int32, (GROUP_ROWS, 1), 0) // BLOCK
    d_cur = (row % BLOCK) - kj
    d_prev = d_cur + BLOCK
    for kv in range(2):
        slope = jnp.zeros((GROUP_ROWS, BLOCK), F32)
        sink = jnp.zeros((GROUP_ROWS, 1), F32)
        for r in range(Q_GROUP):
            slope = jnp.where(head == r, ALIBI_SLOPES[Q_GROUP * kv + r], slope)
            sink = jnp.where(head_col == r, sink_ref[Q_GROUP * kv + r], sink)
        bias_ref[kv, 0] = jnp.where(d_cur >= 0, -slope * d_cur.astype(F32), NEG_INF)
        bias_ref[kv, 1] = jnp.where(d_prev < BLOCK, -slope * d_prev.astype(F32), NEG_INF)
        sinkcol_ref[kv] = sink


def _stack_heads(t0, t1, lo):
    z = jnp.zeros_like(t0)
    return jnp.concatenate([jnp.where(lo, t0, z), jnp.where(lo, z, t0), jnp.where(lo, t1, z), jnp.where(lo, z, t1)], axis=0)


def _unstack_heads(x4, lo):
    return (jnp.where(lo, x4[0:BLOCK], x4[BLOCK:2 * BLOCK]), jnp.where(lo, x4[2 * BLOCK:3 * BLOCK], x4[3 * BLOCK:]))


def _att_probs(q4, k2c, k2p, bias_c, bias_p, sink, has_prev):
    s_c = _dot_nt(q4, k2c) * ATT_SCALE + bias_c
    s_p = jnp.where(has_prev, _dot_nt(q4, k2p) * ATT_SCALE + bias_p, NEG_INF)
    m = jnp.maximum(jnp.max(jnp.maximum(s_c, s_p), axis=-1, keepdims=True), sink)
    e_c = jnp.exp(s_c - m)
    e_p = jnp.exp(s_p - m)
    e_s = jnp.exp(sink - m)
    inv = 1.0 / (jnp.sum(e_c + e_p, axis=-1, keepdims=True) + e_s)
    return e_c * inv, e_p * inv, e_s * inv


def _attention_fwd(proj, q_gain, k_gain, sinks, *, n_seq, seq, name):
    T = n_seq * seq
    nb = seq // BLOCK
    qcol, kvcol = COL_QKV // ATT_WIDTH, (COL_QKV + ATT_WIDTH) // (2 * KV_WIDTH)

    def body(q_ref, kv_ref, qg_ref, kg_ref, sink_ref, y_ref, bias_ref, sinkcol_ref):
        lo = _lo_mask((BLOCK, LANES))
        qg, kg = qg_ref[...], kg_ref[...]
        _att_consts(sink_ref, bias_ref, sinkcol_ref)

        def block(i, carry):
            r0 = pl.multiple_of(i * BLOCK, BLOCK)
            rp = pl.multiple_of(jnp.maximum(i - 1, 0) * BLOCK, BLOCK)
            has_prev = i > 0
            kn_c = _head_norm(kv_ref[pl.ds(r0, BLOCK), 0:KV_WIDTH].astype(F32), kg, lo)[0].astype(BF16)
            kn_p = _head_norm(kv_ref[pl.ds(rp, BLOCK), 0:KV_WIDTH].astype(F32), kg, lo)[0].astype(BF16)
            v_c = kv_ref[pl.ds(r0, BLOCK), KV_WIDTH:2 * KV_WIDTH].astype(BF16)
            v_p = kv_ref[pl.ds(rp, BLOCK), KV_WIDTH:2 * KV_WIDTH].astype(BF16)
            for kv in range(2):
                k2c, k2p = _dup_half(kn_c, kv, lo), _dup_half(kn_p, kv, lo)
                v2c, v2p = _dup_half(v_c, kv, lo), _dup_half(v_p, kv, lo)
                cols = [slice((2 * kv + t) * LANES, (2 * kv + t + 1) * LANES) for t in range(2)]
                qn = [_head_norm(q_ref[pl.ds(r0, BLOCK), c].astype(F32), qg, lo)[0] for c in cols]
                q4 = _stack_heads(qn[0], qn[1], lo).astype(BF16)
                p_c, p_p, _ = _att_probs(q4, k2c, k2p, bias_ref[kv, 0], bias_ref[kv, 1], sinkcol_ref[kv], has_prev)
                o4 = _dot_nn(p_c.astype(BF16), v2c) + _dot_nn(p_p.astype(BF16), v2p)
                for c, out in zip(cols, _unstack_heads(o4, lo)):
                    y_ref[pl.ds(r0, BLOCK), c] = out.astype(BF16)
            return carry

        lax.fori_loop(0, nb, block, 0)

    vec = pl.BlockSpec((1, LANES), lambda b: (0, 0))
    return pl.pallas_call(
        body, name=name, grid=(n_seq,),
        in_specs=[pl.BlockSpec((seq, ATT_WIDTH), lambda b: (b, qcol)),
                  pl.BlockSpec((seq, 2 * KV_WIDTH), lambda b: (b, kvcol)),
                  vec, vec, pl.BlockSpec(memory_space=pltpu.SMEM)],
        out_specs=pl.BlockSpec((seq, ATT_WIDTH), lambda b: (b, 0)),
        out_shape=jax.ShapeDtypeStruct((T, ATT_WIDTH), BF16),
        scratch_shapes=list(ATT_SCRATCH),
        compiler_params=_params(("parallel",)),
    )(proj, proj, jnp.tile(q_gain, 2).reshape(1, LANES), jnp.tile(k_gain, 2).reshape(1, LANES), sinks)


def _attention_bwd(proj, dy, q_gain, k_gain, sinks, *, n_seq, seq, name, deps=()):
    T = n_seq * seq
    nb = seq // BLOCK
    qcol, kvcol = COL_QKV // ATT_WIDTH, (COL_QKV + ATT_WIDTH) // (2 * KV_WIDTH)

    def body(q_ref, kv_ref, dy_ref, qg_ref, kg_ref, sink_ref, dqkv_ref, dqg_ref, dkg_ref, dsink_ref,
             dkn_acc, dv_acc, qg_acc, kg_acc, sink_acc, bias_ref, sinkcol_ref):
        lo = _lo_mask((BLOCK, LANES))
        qg, kg = qg_ref[...], kg_ref[...]
        _att_consts(sink_ref, bias_ref, sinkcol_ref)
        first = pl.program_id(0) == 0

        @pl.when(first)
        def _():
            qg_acc[...] = jnp.zeros_like(qg_acc)
            kg_acc[...] = jnp.zeros_like(kg_acc)
            sink_acc[...] = jnp.zeros_like(sink_acc)

        dkn_acc[...] = jnp.zeros_like(dkn_acc)
        dv_acc[...] = jnp.zeros_like(dv_acc)

        def block(i, carry):
            r0 = pl.multiple_of(i * BLOCK, BLOCK)
            rp = pl.multiple_of(jnp.maximum(i - 1, 0) * BLOCK, BLOCK)
            has_prev = i > 0
            kn_c = _head_norm(kv_ref[pl.ds(r0, BLOCK), 0:KV_WIDTH].astype(F32), kg, lo)[0].astype(BF16)
            kn_p = _head_norm(kv_ref[pl.ds(rp, BLOCK), 0:KV_WIDTH].astype(F32), kg, lo)[0].astype(BF16)
            v_c = kv_ref[pl.ds(r0, BLOCK), KV_WIDTH:2 * KV_WIDTH].astype(BF16)
            v_p = kv_ref[pl.ds(rp, BLOCK), KV_WIDTH:2 * KV_WIDTH].astype(BF16)
            dk_c, dk_p, dv_c, dv_p = [], [], [], []
            for kv in range(2):
                k2c, k2p = _dup_half(kn_c, kv, lo), _dup_half(kn_p, kv, lo)
                v2c, v2p = _dup_half(v_c, kv, lo), _dup_half(v_p, kv, lo)
                cols = [slice((2 * kv + t) * LANES, (2 * kv + t + 1) * LANES) for t in range(2)]
                normed = [_head_norm(q_ref[pl.ds(r0, BLOCK), c].astype(F32), qg, lo) for c in cols]
                q4 = _stack_heads(normed[0][0], normed[1][0], lo).astype(BF16)
                do4 = _stack_heads(dy_ref[pl.ds(r0, BLOCK), cols[0]], dy_ref[pl.ds(r0, BLOCK), cols[1]], lo)
                p_c, p_p, p_s = _att_probs(q4, k2c, k2p, bias_ref[kv, 0], bias_ref[kv, 1], sinkcol_ref[kv], has_prev)
                dp_c = _dot_nt(do4, v2c)
                dp_p = _dot_nt(do4, v2p)
                delta = jnp.sum(p_c * dp_c + p_p * dp_p, axis=-1, keepdims=True)
                ds_c = (p_c * (dp_c - delta)).astype(BF16)
                ds_p = (p_p * (dp_p - delta)).astype(BF16)
                sink_acc[kv] += -(p_s * delta)
                dq4 = (_dot_nn(ds_c, k2c) + _dot_nn(ds_p, k2p)) * ATT_SCALE
                for c, (_, qh, qr), dqn in zip(cols, normed, _unstack_heads(dq4, lo)):
                    dq, dg = _head_norm_bwd(qh, qr, qg, dqn, lo)
                    dqkv_ref[pl.ds(r0, BLOCK), c] = dq.astype(BF16)
                    qg_acc[...] += dg
                dk_c.append(_dot_tn(ds_c, q4))
                dk_p.append(_dot_tn(ds_p, q4))
                dv_c.append(_dot_tn(p_c.astype(BF16), do4))
                dv_p.append(_dot_tn(p_p.astype(BF16), do4))

            def fold(parts):
                a = parts[0] + pltpu.roll(parts[0], LANES // 2, axis=1)
                b = parts[1] + pltpu.roll(parts[1], LANES // 2, axis=1)
                return jnp.where(lo, a, b)

            dkn_acc[pl.ds(r0, BLOCK), :] += fold(dk_c) * ATT_SCALE
            dkn_acc[pl.ds(rp, BLOCK), :] += fold(dk_p) * ATT_SCALE
            dv_acc[pl.ds(r0, BLOCK), :] += fold(dv_c)
            dv_acc[pl.ds(rp, BLOCK), :] += fold(dv_p)
            return carry

        lax.fori_loop(0, nb, block, 0)

        def finish(i, carry):
            r0 = pl.multiple_of(i * BLOCK, BLOCK)
            _, kh, kr = _head_norm(kv_ref[pl.ds(r0, BLOCK), 0:KV_WIDTH].astype(F32), kg, lo)
            dk, dg = _head_norm_bwd(kh, kr, kg, dkn_acc[pl.ds(r0, BLOCK), :], lo)
            dqkv_ref[pl.ds(r0, BLOCK), ATT_WIDTH:ATT_WIDTH + KV_WIDTH] = dk.astype(BF16)
            dqkv_ref[pl.ds(r0, BLOCK), ATT_WIDTH + KV_WIDTH:QKV_WIDTH] = dv_acc[pl.ds(r0, BLOCK), :].astype(BF16)
            kg_acc[...] += dg
            return carry

        lax.fori_loop(0, nb, finish, 0)

        @pl.when(pl.program_id(0) == n_seq - 1)
        def _():
            dqg_ref[...] = jnp.sum(qg_acc[...], axis=0, keepdims=True)
            dkg_ref[...] = jnp.sum(kg_acc[...], axis=0, keepdims=True)
            lane = lax.broadcasted_iota(jnp.int32, (1, LANES), 1)
            dsink = jnp.zeros((1, LANES), F32)
            for kv in range(2):
                for r in range(Q_GROUP):
                    total = jnp.sum(sink_acc[kv, r * BLOCK:(r + 1) * BLOCK, :], axis=0, keepdims=True)
                    dsink = jnp.where(lane == Q_GROUP * kv + r, total, dsink)
            dsink_ref[...] = dsink

    vec = pl.BlockSpec((1, LANES), lambda b: (0, 0))
    acc = pltpu.VMEM((BLOCK, LANES), F32)
    body, dep_specs, dep_args = _with_deps(body, 6, deps)
    dqkv, dqg, dkg, dsink = pl.pallas_call(
        body, name=name, grid=(n_seq,),
        in_specs=[pl.BlockSpec((seq, ATT_WIDTH), lambda b: (b, qcol)),
                  pl.BlockSpec((seq, 2 * KV_WIDTH), lambda b: (b, kvcol)),
                  pl.BlockSpec((seq, ATT_WIDTH), lambda b: (b, 0)),
                  vec, vec, pl.BlockSpec(memory_space=pltpu.SMEM)] + dep_specs,
        out_specs=[pl.BlockSpec((seq, QKV_WIDTH), lambda b: (b, 0)), vec, vec, vec],
        out_shape=[jax.ShapeDtypeStruct((T, QKV_WIDTH), BF16)] + [jax.ShapeDtypeStruct((1, LANES), F32)] * 3,
        scratch_shapes=[pltpu.VMEM((seq, KV_WIDTH), F32), pltpu.VMEM((seq, KV_WIDTH), F32), acc, acc,
                        pltpu.VMEM((2, GROUP_ROWS, 1), F32), *ATT_SCRATCH],
        compiler_params=_params(("arbitrary",)),
    )(proj, proj, dy, jnp.tile(q_gain, 2).reshape(1, LANES), jnp.tile(k_gain, 2).reshape(1, LANES), sinks, *dep_args)
    half = LANES // 2
    return dqkv, dqg[0, :half] + dqg[0, half:], dkg[0, :half] + dkg[0, half:], dsink[0, :N_Q_HEADS]


def _sgu_weights(w_ref):
    r = lax.broadcasted_iota(jnp.int32, (BLOCK, BLOCK), 0)
    c = lax.broadcasted_iota(jnp.int32, (BLOCK, BLOCK), 1)
    return [jnp.where(r >= c, w_ref[g], 0.0).astype(BF16) for g in range(SGU_GROUPS)]


def _sgu_fwd(proj, gain, w_s, bias_full, *, n_seq, seq, name):
    T = n_seq * seq
    nc = seq // BLOCK

    def body(suv_ref, g_ref, w_ref, b_ref, y_ref):
        lo = _lo_mask((BLOCK, LANES))
        wm = _sgu_weights(w_ref)
        gain_v = g_ref[...]

        def chunk(c, carry):
            r0 = pl.multiple_of(c * BLOCK, BLOCK)
            gv = _gelu(suv_ref[pl.ds(r0, BLOCK), SGU_WIDTH:2 * SGU_WIDTH].astype(F32))
            r = lax.rsqrt(jnp.mean(gv * gv, axis=-1, keepdims=True) + NORM_EPS)
            vn = (gv * r * gain_v).astype(BF16)
            for p in range(SGU_WIDTH // LANES):
                cols = slice(p * LANES, (p + 1) * LANES)
                vp = vn[:, cols]
                mixed = jnp.where(lo, _dot_nn(wm[2 * p], vp), _dot_nn(wm[2 * p + 1], vp)) + b_ref[:, cols]
                u = _gelu(suv_ref[pl.ds(r0, BLOCK), cols].astype(F32))
                y_ref[pl.ds(r0, BLOCK), cols] = (u * mixed).astype(BF16)
            return carry

        lax.fori_loop(0, nc, chunk, 0)

    return pl.pallas_call(
        body, name=name, grid=(n_seq,),
        in_specs=[pl.BlockSpec((seq, 2 * SGU_WIDTH), lambda b: (b, COL_SUV // (2 * SGU_WIDTH))),
                  pl.BlockSpec((1, SGU_WIDTH), lambda b: (0, 0)),
                  pl.BlockSpec((SGU_GROUPS, BLOCK, BLOCK), lambda b: (0, 0, 0)),
                  pl.BlockSpec((BLOCK, SGU_WIDTH), lambda b: (0, 0))],
        out_specs=pl.BlockSpec((seq, SGU_WIDTH), lambda b: (b, 0)),
        out_shape=jax.ShapeDtypeStruct((T, SGU_WIDTH), BF16),
        compiler_params=_params(("parallel",)),
    )(proj, gain.reshape(1, SGU_WIDTH), w_s, bias_full)


def _sgu_bwd(proj, dy, gain, w_s, bias_full, *, n_seq, seq, name, deps=()):
    T = n_seq * seq
    nc = seq // BLOCK
    n_tiles = SGU_WIDTH // LANES

    def body(suv_ref, dy_ref, g_ref, w_ref, b_ref, dsuv_ref, dg_ref, dw_ref, db_ref, dg_acc, dw_acc, db_acc):
        lo = _lo_mask((BLOCK, LANES))
        hi = jnp.logical_not(lo)
        wm = _sgu_weights(w_ref)
        wmt = [jnp.where(lax.broadcasted_iota(jnp.int32, (BLOCK, BLOCK), 1) >= lax.broadcasted_iota(jnp.int32, (BLOCK, BLOCK), 0),
                         w_ref[g].T, 0.0).astype(BF16) for g in range(SGU_GROUPS)]
        gain_v = g_ref[...]

        @pl.when(pl.program_id(0) == 0)
        def _():
            dg_acc[...] = jnp.zeros_like(dg_acc)
            dw_acc[...] = jnp.zeros_like(dw_acc)
            db_acc[...] = jnp.zeros_like(db_acc)

        def chunk(c, carry):
            r0 = pl.multiple_of(c * BLOCK, BLOCK)
            gv, dgelu_v = _gelu_and_grad(suv_ref[pl.ds(r0, BLOCK), SGU_WIDTH:2 * SGU_WIDTH].astype(F32))
            r = lax.rsqrt(jnp.mean(gv * gv, axis=-1, keepdims=True) + NORM_EPS)
            vh = gv * r
            vn = (vh * gain_v).astype(BF16)
            dvn_tiles = []
            for p in range(n_tiles):
                cols = slice(p * LANES, (p + 1) * LANES)
                vp = vn[:, cols]
                mixed = jnp.where(lo, _dot_nn(wm[2 * p], vp), _dot_nn(wm[2 * p + 1], vp)) + b_ref[:, cols]
                u, dgelu_u = _gelu_and_grad(suv_ref[pl.ds(r0, BLOCK), cols].astype(F32))
                dyv = dy_ref[pl.ds(r0, BLOCK), cols]
                dsuv_ref[pl.ds(r0, BLOCK), cols] = (dyv * mixed * dgelu_u).astype(BF16)
                dm = dyv * u
                db_acc[:, cols] += dm
                dm_bf = dm.astype(BF16)
                dvn_tiles.append(jnp.where(lo, _dot_nn(wmt[2 * p], dm_bf), _dot_nn(wmt[2 * p + 1], dm_bf)))
                dw_acc[2 * p] += _dot_nt(jnp.where(lo, dm, 0.0).astype(BF16), vp)
                dw_acc[2 * p + 1] += _dot_nt(jnp.where(hi, dm, 0.0).astype(BF16), vp)
            dvn = jnp.concatenate(dvn_tiles, axis=1)
            dg_acc[...] += dvn * vh
            dvh = dvn * gain_v
            dgv = r * (dvh - vh * jnp.mean(dvh * vh, axis=-1, keepdims=True))
            dsuv_ref[pl.ds(r0, BLOCK), SGU_WIDTH:2 * SGU_WIDTH] = (dgv * dgelu_v).astype(BF16)
            return carry

        lax.fori_loop(0, nc, chunk, 0)

        @pl.when(pl.program_id(0) == n_seq - 1)
        def _():
            dg_ref[...] = jnp.sum(dg_acc[...], axis=0, keepdims=True)
            r = lax.broadcasted_iota(jnp.int32, (BLOCK, BLOCK), 0)
            c = lax.broadcasted_iota(jnp.int32, (BLOCK, BLOCK), 1)
            for g in range(SGU_GROUPS):
                dw_ref[g] = jnp.where(r >= c, dw_acc[g], 0.0)
            lane = lax.broadcasted_iota(jnp.int32, (BLOCK, LANES), 1)
            out = jnp.zeros((BLOCK, LANES), F32)
            for p in range(n_tiles):
                tile = db_acc[:, p * LANES:(p + 1) * LANES]
                s_lo = jnp.sum(jnp.where(lo, tile, 0.0), axis=-1, keepdims=True)
                s_hi = jnp.sum(jnp.where(hi, tile, 0.0), axis=-1, keepdims=True)
                out = jnp.where(lane == 2 * p, s_lo, out)
                out = jnp.where(lane == 2 * p + 1, s_hi, out)
            db_ref[...] = out

    body, dep_specs, dep_args = _with_deps(body, 5, deps)
    dsuv, dg, dw, db = pl.pallas_call(
        body, name=name, grid=(n_seq,),
        in_specs=[pl.BlockSpec((seq, 2 * SGU_WIDTH), lambda b: (b, COL_SUV // (2 * SGU_WIDTH))),
                  pl.BlockSpec((seq, SGU_WIDTH), lambda b: (b, 0)),
                  pl.BlockSpec((1, SGU_WIDTH), lambda b: (0, 0)),
                  pl.BlockSpec((SGU_GROUPS, BLOCK, BLOCK), lambda b: (0, 0, 0)),
                  pl.BlockSpec((BLOCK, SGU_WIDTH), lambda b: (0, 0))] + dep_specs,
        out_specs=[pl.BlockSpec((seq, 2 * SGU_WIDTH), lambda b: (b, 0)),
                   pl.BlockSpec((1, SGU_WIDTH), lambda b: (0, 0)),
                   pl.BlockSpec((SGU_GROUPS, BLOCK, BLOCK), lambda b: (0, 0, 0)),
                   pl.BlockSpec((BLOCK, LANES), lambda b: (0, 0))],
        out_shape=[jax.ShapeDtypeStruct((T, 2 * SGU_WIDTH), BF16), jax.ShapeDtypeStruct((1, SGU_WIDTH), F32),
                   jax.ShapeDtypeStruct((SGU_GROUPS, BLOCK, BLOCK), F32), jax.ShapeDtypeStruct((BLOCK, LANES), F32)],
        scratch_shapes=[pltpu.VMEM((BLOCK, SGU_WIDTH), F32), pltpu.VMEM((SGU_GROUPS, BLOCK, BLOCK), F32),
                        pltpu.VMEM((BLOCK, SGU_WIDTH), F32)],
        compiler_params=_params(("arbitrary",)),
    )(proj, dy, gain.reshape(1, SGU_WIDTH), w_s, bias_full, *dep_args)
    return dsuv, dg.reshape(SGU_WIDTH), dw, db[:, :SGU_GROUPS].T


def _merge_fwd(y_att, y_sgu, w_oa, w_ob, proj, *, name, tm=1024, tn=512, deps=()):
    T = y_att.shape[0]

    def body(ya_ref, ys_ref, wa_ref, wb_ref, ga_ref, gb_ref, o_ref):
        pa = _dot_nn(ya_ref[...], wa_ref[...])
        pb = _dot_nn(ys_ref[...], wb_ref[...])
        o_ref[...] = (_sigmoid(ga_ref[...].astype(F32)) * pa + _sigmoid(gb_ref[...].astype(F32)) * pb).astype(BF16)

    act = pl.BlockSpec((tm, ATT_WIDTH), lambda i, j: (i, 0))
    wgt = pl.BlockSpec((ATT_WIDTH, tn), lambda i, j: (0, j))
    body, dep_specs, dep_args = _with_deps(body, 6, deps)
    return pl.pallas_call(
        body, name=name, grid=(T // tm, D_MODEL // tn),
        in_specs=[act, act, wgt, wgt,
                  pl.BlockSpec((tm, tn), lambda i, j: (i, j + COL_GA // tn)),
                  pl.BlockSpec((tm, tn), lambda i, j: (i, j + COL_GB // tn))] + dep_specs,
        out_specs=pl.BlockSpec((tm, tn), lambda i, j: (i, j)),
        out_shape=jax.ShapeDtypeStruct((T, D_MODEL), BF16),
        compiler_params=_params(("parallel", "parallel")),
    )(y_att, y_sgu, w_oa, w_ob, proj, proj, *dep_args)


def _merge_bwd(dx1_bf, w_out, y_att, y_sgu, w_oa, w_ob, proj, *, name, tm=1024, tn=512):
    T = y_att.shape[0]

    def body(dx_ref, wo_ref, ya_ref, ys_ref, wa_ref, wb_ref, ga_ref, gb_ref, dpa_ref, dpb_ref, dga_ref, dgb_ref):
        dm = _dot_nt(dx_ref[...], wo_ref[...])
        pa = _dot_nn(ya_ref[...], wa_ref[...])
        pb = _dot_nn(ys_ref[...], wb_ref[...])
        sa = _sigmoid(ga_ref[...].astype(F32))
        sb = _sigmoid(gb_ref[...].astype(F32))
        dpa_ref[...] = (dm * sa).astype(BF16)
        dpb_ref[...] = (dm * sb).astype(BF16)
        dga_ref[...] = (dm * pa * sa * (1.0 - sa)).astype(BF16)
        dgb_ref[...] = (dm * pb * sb * (1.0 - sb)).astype(BF16)

    act = pl.BlockSpec((tm, ATT_WIDTH), lambda i, j: (i, 0))
    wgt = pl.BlockSpec((ATT_WIDTH, tn), lambda i, j: (0, j))
    out = pl.BlockSpec((tm, tn), lambda i, j: (i, j))
    return pl.pallas_call(
        body, name=name, grid=(T // tm, D_MODEL // tn),
        in_specs=[pl.BlockSpec((tm, D_MODEL), lambda i, j: (i, 0)),
                  pl.BlockSpec((tn, D_MODEL), lambda i, j: (j, 0)),
                  act, act, wgt, wgt,
                  pl.BlockSpec((tm, tn), lambda i, j: (i, j + COL_GA // tn)),
                  pl.BlockSpec((tm, tn), lambda i, j: (i, j + COL_GB // tn))],
        out_specs=[out] * 4,
        out_shape=[jax.ShapeDtypeStruct((T, D_MODEL), BF16)] * 4,
        compiler_params=_params(("parallel", "parallel")),
    )(dx1_bf, w_out, y_att, y_sgu, w_oa, w_ob, proj, proj)


CONV_ROWS = 256
CONV_TN = 256
UP_CONV_ROWS = 256


def _shift_rows(cur, prev8, k):
    rolled = pltpu.roll(cur, k, axis=0)
    head = jnp.where(lax.broadcasted_iota(jnp.int32, prev8.shape, 0) < k, pltpu.roll(prev8, k, axis=0), rolled[:SUBLANES])
    return jnp.concatenate([head, rolled[SUBLANES:]], axis=0)


def _shift_rows_up(cur, next8, k):
    n = cur.shape[0]
    rolled = pltpu.roll(cur, n - k, axis=0)
    tail = jnp.where(lax.broadcasted_iota(jnp.int32, next8.shape, 0) >= SUBLANES - k,
                     pltpu.roll(next8, SUBLANES - k, axis=0), rolled[n - SUBLANES:])
    return jnp.concatenate([rolled[:n - SUBLANES], tail], axis=0)


def _up_conv_fwd(h2, w_up_t, cw_g, cw_v, cb_g, cb_v, *, n_seq, seq, name, deps=()):
    T = n_seq * seq
    tn, rows = CONV_TN, UP_CONV_ROWS

    def body(h_ref, ug_ref, uv_ref, wg_ref, wv_ref, bg_ref, bv_ref, a_ref, zg_ref, zv_ref, cg_ref, cv_ref):
        def conv(cur, prev8, w_ref, b_ref):
            z1 = _shift_rows(cur, prev8, 1)
            z2 = _shift_rows(cur, prev8, 2)
            return b_ref[...] + w_ref[0:1, :] * z2 + w_ref[1:2, :] * z1 + w_ref[2:3, :] * cur

        start = jnp.zeros((SUBLANES, tn), F32)
        prev = (start, start)
        for s in range(seq // rows):
            r = pl.ds(s * rows, rows)
            h = h_ref[r, :]
            zg = _dot_nt(h, ug_ref[...])
            zv = _dot_nt(h, uv_ref[...])
            zg_ref[r, :] = zg.astype(ACT_DTYPE)
            zv_ref[r, :] = zv.astype(ACT_DTYPE)
            g = conv(zg, prev[0], wg_ref, bg_ref)
            v = conv(zv, prev[1], wv_ref, bv_ref)
            a_ref[r, :] = (g * _sigmoid(g) * v).astype(BF16)
            cg_ref[r, :] = g.astype(ACT_DTYPE)
            cv_ref[r, :] = v.astype(ACT_DTYPE)
            prev = (zg[rows - SUBLANES:], zv[rows - SUBLANES:])

    zs = pl.BlockSpec((seq, tn), lambda b, j: (b, j))
    ws = pl.BlockSpec((3, tn), lambda b, j: (0, j))
    bs = pl.BlockSpec((1, tn), lambda b, j: (0, j))
    body, dep_specs, dep_args = _with_deps(body, 7, deps)
    return pl.pallas_call(
        body, name=name, grid=(n_seq, D_FF // tn),
        in_specs=[pl.BlockSpec((seq, D_MODEL), lambda b, j: (b, 0)),
                  pl.BlockSpec((tn, D_MODEL), lambda b, j: (j, 0)),
                  pl.BlockSpec((tn, D_MODEL), lambda b, j: (j + D_FF // tn, 0)), ws, ws, bs, bs] + dep_specs,
        out_specs=[zs] * 5,
        out_shape=[jax.ShapeDtypeStruct((T, D_FF), BF16)] + [jax.ShapeDtypeStruct((T, D_FF), ACT_DTYPE)] * 4,
        compiler_params=_params(("parallel", "parallel")),
    )(h2, w_up_t, w_up_t, cw_g, cw_v, cb_g.reshape(1, D_FF), cb_v.reshape(1, D_FF), *dep_args)


def _conv_bwd(z_g, z_v, c_g, c_v, dx2_bf, w_down, cw_g, cw_v, *, n_seq, seq, name):
    T = n_seq * seq
    tn, rows = CONV_TN, CONV_ROWS
    n_steps = seq // rows

    def body(zg_ref, zv_ref, cg_ref, cv_ref, dx_ref, wd_ref, wg_ref, wv_ref,
             dzg_ref, dzv_ref, dwg_ref, dwv_ref, dbg_ref, dbv_ref, dcg_ref, dcv_ref):
        def colsum(x):
            return jnp.sum(x, axis=0, keepdims=True)

        zero = jnp.zeros((1, tn), F32)
        db = (zero, zero)
        for s in range(n_steps):
            r = pl.ds(s * rows, rows)
            g = cg_ref[r, :].astype(F32)
            v = cv_ref[r, :].astype(F32)
            sg = _sigmoid(g)
            dav = _dot_nt(dx_ref[r, :], wd_ref[...])
            dcg = dav * v * (sg * (1.0 + g * (1.0 - sg)))
            dcv = dav * (g * sg)
            dcg_ref[r, :] = dcg
            dcv_ref[r, :] = dcv
            db = (db[0] + colsum(dcg), db[1] + colsum(dcv))

        def back(s, accs):
            r0 = pl.multiple_of(s * rows, rows)
            last = s == n_steps - 1
            rn = pl.multiple_of(jnp.minimum(r0 + rows, seq - SUBLANES), SUBLANES)
            new = []
            for half, (dc_ref, w_ref, dz_ref, z_ref) in enumerate(((dcg_ref, wg_ref, dzg_ref, zg_ref),
                                                                   (dcv_ref, wv_ref, dzv_ref, zv_ref))):
                cur = dc_ref[pl.ds(r0, rows), :]
                nxt = jnp.where(last, 0.0, dc_ref[pl.ds(rn, SUBLANES), :])
                u1, u2 = _shift_rows_up(cur, nxt, 1), _shift_rows_up(cur, nxt, 2)
                dz_ref[pl.ds(r0, rows), :] = (w_ref[2:3, :] * cur + w_ref[1:2, :] * u1 + w_ref[0:1, :] * u2).astype(BF16)
                z = z_ref[pl.ds(r0, rows), :].astype(F32)
                new += [accs[3 * half] + colsum(u2 * z), accs[3 * half + 1] + colsum(u1 * z),
                        accs[3 * half + 2] + colsum(cur * z)]
            return tuple(new)

        dw = lax.fori_loop(0, n_steps, back, (zero,) * 6)
        first_seq = pl.program_id(1) == 0

        @pl.when(first_seq)
        def _():
            dwg_ref[...] = jnp.concatenate(dw[0:3], axis=0)
            dwv_ref[...] = jnp.concatenate(dw[3:6], axis=0)
            dbg_ref[...], dbv_ref[...] = db

        @pl.when(jnp.logical_not(first_seq))
        def _():
            dwg_ref[...] += jnp.concatenate(dw[0:3], axis=0)
            dwv_ref[...] += jnp.concatenate(dw[3:6], axis=0)
            dbg_ref[...] += db[0]
            dbv_ref[...] += db[1]

    zs = pl.BlockSpec((seq, tn), lambda j, b: (b, j))
    ws = pl.BlockSpec((3, tn), lambda j, b: (0, j))
    bs = pl.BlockSpec((1, tn), lambda j, b: (0, j))
    outs = pl.pallas_call(
        body, name=name, grid=(D_FF // tn, n_seq),
        in_specs=[zs] * 4 + [pl.BlockSpec((seq, D_MODEL), lambda j, b: (b, 0)),
                             pl.BlockSpec((tn, D_MODEL), lambda j, b: (j, 0)), ws, ws],
        out_specs=[zs, zs, ws, ws, bs, bs],
        out_shape=[jax.ShapeDtypeStruct((T, D_FF), BF16)] * 2 + [jax.ShapeDtypeStruct((3, D_FF), F32)] * 2
        + [jax.ShapeDtypeStruct((1, D_FF), F32)] * 2,
        scratch_shapes=[pltpu.VMEM((seq, tn), F32), pltpu.VMEM((seq, tn), F32)],
        compiler_params=_params(("parallel", "arbitrary")),
    )(z_g, z_v, c_g, c_v, dx2_bf, w_down, cw_g, cw_v)
    dz_g, dz_v, dw_g, dw_v, db_g, db_v = outs
    return dz_g, dz_v, dw_g, dw_v, db_g.reshape(D_FF), db_v.reshape(D_FF)


def _layer_fwd(x, h, w, sched, tail, *, n_seq, seq, l):
    tag = f"l{l}"
    deps = sched("fwd_start", l, h)
    proj = _mm(h, w["w_in_t"], mode="nt", out_dtype=ACT_DTYPE, rotate=W_IN_ROTATE, name=f"{tag}_proj", deps=deps)
    y_att = _attention_fwd(proj, w["q_norm"], w["k_norm"], w["sinks"], n_seq=n_seq, seq=seq, name=f"{tag}_att")
    deps = sched("fwd_att", l, y_att)
    y_sgu = _sgu_fwd(proj, w["sgu_norm"], w["w_s"], w["bias_full"], n_seq=n_seq, seq=seq, name=f"{tag}_sgu")
    merged = _merge_fwd(y_att, y_sgu, w["w_oa"], w["w_ob"], proj, name=f"{tag}_merge", deps=deps)
    x1, h2 = _mm_rows(merged, w["w_out"], mode="nn", fn=_residual_then_norm, out_dtypes=(F32, BF16), rows=(x,),
                      vecs=(w["ffn_norm"],), name=f"{tag}_out")
    deps = sched("fwd_mixer_done", l, x1)
    a, z_g, z_v, c_g, c_v = _up_conv_fwd(h2, w["w_up_t"], w["cw_g"], w["cw_v"], w["cb_g"], w["cb_v"], n_seq=n_seq,
                                         seq=seq, name=f"{tag}_up_conv", deps=deps)
    deps = sched("fwd_conv", l, a)
    if tail[0] == "norm":
        out = _mm_rows(a, w["w_down"], mode="nn", fn=_residual_then_norm, out_dtypes=(F32, BF16), rows=(x1,),
                       vecs=(tail[1],), name=f"{tag}_down", deps=deps)
    else:
        out = _mm_rows(a, w["w_down"], mode="nn", fn=_residual_then_loss, out_dtypes=(F32, BF16), rows=(x1, tail[1]),
                       reduce=True, name=f"{tag}_down", deps=deps)
    saved = dict(x=x, h=h, proj=proj, y_att=y_att, y_sgu=y_sgu, merged=merged, x1=x1, h2=h2, z_g=z_g, z_v=z_v,
                 c_g=c_g, c_v=c_v, a=a)
    return out, saved


def _layer_bwd(dx2, dx2_bf, w, s, sched, deps, *, n_seq, seq, l):
    tag = f"l{l}b"
    g = {}
    g["w_down"] = _mm(s["a"], dx2_bf, mode="tn", out_dtype=F32, name=f"{tag}_dw_down", deps=deps)
    dz_g, dz_v, g["cw_g"], g["cw_v"], g["cb_g"], g["cb_v"] = _conv_bwd(
        s["z_g"], s["z_v"], s["c_g"], s["c_v"], dx2_bf, w["w_down"], w["cw_g"], w["cw_v"], n_seq=n_seq, seq=seq,
        name=f"{tag}_conv")
    dw_up_t = _mm(dz_g, s["h2"], mode="tn", out_dtype=F32, out_rows=(0, 2 * D_FF), name=f"{tag}_dw_up_g")
    g["w_up_t"] = _mm(dz_v, s["h2"], mode="tn", out_dtype=F32, out_rows=(D_FF, 2 * D_FF), out_prev=dw_up_t,
                      name=f"{tag}_dw_up_v")
    deps = sched("bwd_ffn_grads", l, dz_v, g)
    dx1, dx1_bf, dgain = _mm_rows((dz_g, dz_v), w["w_up_t"], mode="nn", fn=_rms_bwd_rows, out_dtypes=(F32, BF16),
                                  rows=(s["x1"], dx2), vecs=(w["ffn_norm"],), reduce=True, a_at=(0, D_FF),
                                  name=f"{tag}_dh2", deps=deps)
    g["ffn_norm"] = dgain.reshape(D_MODEL)
    dpa, dpb, dga, dgb = _merge_bwd(dx1_bf, w["w_out"], s["y_att"], s["y_sgu"], w["w_oa"], w["w_ob"], s["proj"],
                                    name=f"{tag}_merge")
    deps = sched("bwd_merge", l, dpa)
    g["w_out"] = _mm(s["merged"], dx1_bf, mode="tn", out_dtype=F32, name=f"{tag}_dw_out",
                     deps=deps)
    dy_att = _mm(dpa, w["w_oa"], mode="nt", out_dtype=BF16, name=f"{tag}_dy_att")
    dy_sgu = _mm(dpb, w["w_ob"], mode="nt", out_dtype=F32, name=f"{tag}_dy_sgu")
    g["w_oa"] = _mm(s["y_att"], dpa, mode="tn", out_dtype=F32, name=f"{tag}_dw_oa")
    g["w_ob"] = _mm(s["y_sgu"], dpb, mode="tn", out_dtype=F32, name=f"{tag}_dw_ob")
    deps = sched("bwd_out_grads", l, dy_att, g)
    dqkv, g["q_norm"], g["k_norm"], g["sinks"] = _attention_bwd(
        s["proj"], dy_att, w["q_norm"], w["k_norm"], w["sinks"], n_seq=n_seq, seq=seq, name=f"{tag}_att", deps=deps)
    deps = sched("bwd_att", l, dqkv)
    dsuv, g["sgu_norm"], g["w_s"], g["b_s"] = _sgu_bwd(
        s["proj"], dy_sgu, w["sgu_norm"], w["w_s"], w["bias_full"], n_seq=n_seq, seq=seq, name=f"{tag}_sgu", deps=deps)
    dproj = (dsuv, dga, dgb, dqkv)
    at = (QKV_WIDTH, QKV_WIDTH + 2 * SGU_WIDTH, QKV_WIDTH + 2 * SGU_WIDTH + D_MODEL, 0)
    g["w_in_t"] = _mm_tn_parts(dproj, at, s["h"], name=f"{tag}_dw_in")
    deps = sched("bwd_w_in_grad", l, dqkv, g)
    dx, dx_bf, dgain = _mm_rows(dproj, w["w_in_t"], mode="nn", fn=_rms_bwd_rows, out_dtypes=(F32, BF16),
                                rows=(s["x"], dx1), vecs=(w["mix_norm"],), reduce=True, a_at=at,
                                name=f"{tag}_dh", deps=deps)
    g["mix_norm"] = dgain.reshape(D_MODEL)
    return dx, dx_bf, g, sched("bwd_dh", l, dx)


def _local_step(x, target, weights, sched, *, n_seq, seq):
    depth = len(weights)
    saved = []
    h = _rms_fwd(x, weights[0]["mix_norm"], name="l0_mix_norm", deps=sched("begin", 0, x))
    for l in range(depth):
        tail = ("norm", weights[l + 1]["mix_norm"]) if l + 1 < depth else ("loss", target)
        out, s = _layer_fwd(x, h, weights[l], sched, tail, n_seq=n_seq, seq=seq, l=l)
        saved.append(s)
        if l + 1 < depth:
            x, h = out
    dy, dy_bf, loss_cols = out
    grads = [None] * depth
    deps = ()
    for l in reversed(range(depth)):
        dy, dy_bf, grads[l], deps = _layer_bwd(dy, dy_bf, weights[l], saved[l], sched, deps, n_seq=n_seq, seq=seq, l=l)
    return jnp.sum(loss_cols), dy, grads, deps


W_IN_SHARD = IN_WIDTH // N_DEV
W_UP_SHARD = 2 * D_FF // N_DEV
COL_MOVE_ROWS = 256


def _w_o_moves():
    return tuple((j, 0, LANES, 0, j * LANES) for j in range(N_DEV))


def _disassemble(mats, w, moves, *, name):
    R = mats[0].shape[0]
    tr = min(R, COL_MOVE_ROWS)
    n = len(mats)

    def body(*refs):
        m_refs, o_ref = refs[:n], refs[n]
        for j, lo, hi, which, at in moves:
            o_ref[j, :, lo:hi] = m_refs[which][:, at:at + hi - lo]

    return pl.pallas_call(
        body, name=name, grid=(R // tr,),
        in_specs=[pl.BlockSpec((tr, m.shape[1]), lambda i: (i, 0)) for m in mats],
        out_specs=pl.BlockSpec((N_DEV, tr, w), lambda i: (0, i, 0)),
        out_shape=jax.ShapeDtypeStruct((N_DEV, R, w), mats[0].dtype),
        compiler_params=_params(("parallel",)),
    )(*mats)


def _my_place():
    return lax.axis_index("x"), lax.axis_index("y"), lax.axis_index("c")


def _gathered_shape(shape, kind):
    r, c = shape
    return {"blocks": (N_DEV, r, c), "rows": (N_DEV * r, c), "cols": (r, N_DEV * c)}[kind]


def _gather_window(ref, kind, shape, j):
    r, c = shape
    if kind == "blocks":
        return ref.at[j]
    if kind == "rows":
        return ref.at[pl.ds(pl.multiple_of(j * r, r), r), :]
    return ref.at[:, pl.ds(pl.multiple_of(j * c, c), c)]


def _gather(srcs, kinds, *, name):
    n = len(srcs)
    shapes = [s.shape for s in srcs]
    per = 7

    def body(*refs):
        src_refs, dst_refs = refs[:n], refs[n:2 * n]
        send_sems, recv_sems, local_sems = refs[2 * n:]
        x, y, c = _my_place()
        me, sibling = (x, y, c), (x, y, 1 - c)
        chips = [(1 - x, y), (x, 1 - y), (1 - x, 1 - y)]

        def at(i, px, py, pc):
            return _gather_window(dst_refs[i], kinds[i], shapes[i], 4 * px + 2 * py + pc)

        def copy(i, k, block, to, src=None):
            return pltpu.make_async_remote_copy(
                src_ref=at(i, *block) if src is None else src, dst_ref=at(i, *block),
                send_sem=send_sems.at[per * i + k], recv_sem=recv_sems.at[per * i + k], device_id=to, device_id_type=MESH)

        mine = [pltpu.make_async_copy(src_refs[i], at(i, *me), local_sems.at[i]) for i in range(n)]
        for cp in mine:
            cp.start()
        started = []
        for i in range(n):
            first = [copy(i, 0, me, sibling, src=src_refs[i])]
            first += [copy(i, 1 + j, me, (*chip, c), src=src_refs[i]) for j, chip in enumerate(chips)]
            for cp in first:
                cp.start()
            started += first
        for i in range(n):
            for j, chip in enumerate(chips):
                copy(i, 1 + j, (*chip, c), me).wait_recv()
                fwd = copy(i, 4 + j, (*chip, c), sibling)
                fwd.start()
                started.append(fwd)
        for i in range(n):
            copy(i, 0, sibling, me).wait_recv()
            for j, chip in enumerate(chips):
                copy(i, 4 + j, (*chip, 1 - c), me).wait_recv()
        for cp in started:
            cp.wait_send()
        for cp in mine:
            cp.wait()

    return pl.pallas_call(
        body, name=name,
        out_shape=[jax.ShapeDtypeStruct(_gathered_shape(s.shape, k), s.dtype) for s, k in zip(srcs, kinds)],
        in_specs=[ANY] * n, out_specs=[ANY] * n,
        scratch_shapes=[pltpu.SemaphoreType.DMA((per * n,)), pltpu.SemaphoreType.DMA((per * n,)),
                        pltpu.SemaphoreType.DMA((n,))],
    )(*srcs)


HBM = pl.BlockSpec(memory_space=pltpu.HBM)
SEM = pl.BlockSpec(memory_space=pltpu.SEMAPHORE)
TOKEN = jax.ShapeDtypeStruct((SUBLANES, LANES), F32)
TOKEN_SPEC = pl.BlockSpec(memory_space=pltpu.VMEM)
SPLIT_PARAMS = pltpu.CompilerParams(has_side_effects=pltpu.SideEffectType.DATAFLOW_SIDE_EFFECTING)


def _in_hbm(x):
    return pltpu.with_memory_space_constraint(x, pltpu.HBM)


def _hbm_like(shape, dtype):
    return pltpu.HBM(shape, dtype)


def _place_own(shards, kinds, dtypes, *, name, deps=()):
    n = len(shards)
    shapes = [s.shape for s in shards]

    def body(*refs):
        s_refs, land_refs, bufs, sems = refs[:n], refs[n:2 * n], refs[2 * n:3 * n], refs[3 * n]
        x, y, c = _my_place()
        copies = []
        for i in range(n):
            bufs[i][...] = s_refs[i][...].astype(dtypes[i])
            copies.append(pltpu.make_async_copy(
                bufs[i], _gather_window(land_refs[i], kinds[i], shapes[i], 4 * x + 2 * y + c), sems.at[i]))
        for cp in copies:
            cp.start()
        for cp in copies:
            cp.wait()

    body, dep_specs, dep_args = _with_deps(body, n, deps)
    return pl.pallas_call(
        body, name=name,
        out_shape=[jax.ShapeDtypeStruct(_gathered_shape(s, k), d) for s, k, d in zip(shapes, kinds, dtypes)],
        in_specs=[pl.BlockSpec(memory_space=pltpu.VMEM)] * n + dep_specs, out_specs=[ANY] * n,
        scratch_shapes=[pltpu.VMEM(s, d) for s, d in zip(shapes, dtypes)] + [pltpu.SemaphoreType.DMA((n,))],
        compiler_params=_params(),
    )(*shards, *dep_args)


def _gather_start(lands, kinds, shapes, after=(), *, name):
    n = len(lands)
    n_after = len(after)

    def body(*refs):
        land_refs = refs[:n]
        send_sems, recv_sems = refs[n + n_after], refs[n + n_after + 1]
        x, y, c = _my_place()
        targets = [(x, y, 1 - c), (1 - x, y, c), (x, 1 - y, c), (1 - x, 1 - y, c)]
        for i in range(n):
            own = _gather_window(land_refs[i], kinds[i], shapes[i], 4 * x + 2 * y + c)
            for k, to in enumerate(targets):
                pltpu.make_async_remote_copy(
                    src_ref=own, dst_ref=own, send_sem=send_sems.at[4 * i + k], recv_sem=recv_sems.at[4 * i + k],
                    device_id=to, device_id_type=MESH).start()
        refs[-1][...] = jnp.zeros_like(refs[-1])

    outs = pl.pallas_call(
        body, name=name,
        out_shape=[pltpu.SemaphoreType.DMA((4 * n,)), pltpu.SemaphoreType.DMA((4 * n,))]
        + [_hbm_like(a.shape, a.dtype) for a in lands] + [TOKEN],
        in_specs=[HBM] * n + [ANY] * n_after, out_specs=[SEM, SEM] + [HBM] * n + [TOKEN_SPEC],
        input_output_aliases={i: 2 + i for i in range(n)},
        compiler_params=SPLIT_PARAMS,
    )(*[_in_hbm(a) for a in lands], *after)
    return outs[0], outs[1], outs[2:2 + n], outs[-1]


def _gather_forward(recv_sems, lands, kinds, shapes, after, *, name):
    n = len(lands)

    def body(*refs):
        recv_ref, land_refs = refs[0], refs[1:1 + n]
        fwd_send, fwd_recv = refs[2 + n], refs[3 + n]
        token = refs[-1]
        x, y, c = _my_place()
        chips = [(1 - x, y), (x, 1 - y), (1 - x, 1 - y)]
        for i in range(n):
            for j, (px, py) in enumerate(chips):
                block = _gather_window(land_refs[i], kinds[i], shapes[i], 4 * px + 2 * py + c)
                pltpu.make_async_remote_copy(
                    src_ref=block, dst_ref=block, send_sem=fwd_send.at[3 * i + j], recv_sem=recv_ref.at[4 * i + 1 + j],
                    device_id=(px, py, c), device_id_type=MESH).wait_recv()
                pltpu.make_async_remote_copy(
                    src_ref=block, dst_ref=block, send_sem=fwd_send.at[3 * i + j], recv_sem=fwd_recv.at[3 * i + j],
                    device_id=(x, y, 1 - c), device_id_type=MESH).start()
        token[...] = jnp.zeros_like(token)

    outs = pl.pallas_call(
        body, name=name,
        out_shape=[pltpu.SemaphoreType.DMA((3 * n,)), pltpu.SemaphoreType.DMA((3 * n,))]
        + [_hbm_like(a.shape, a.dtype) for a in lands] + [TOKEN],
        in_specs=[SEM] + [HBM] * n + [ANY], out_specs=[SEM, SEM] + [HBM] * n + [TOKEN_SPEC],
        input_output_aliases={1 + i: 2 + i for i in range(n)},
        compiler_params=SPLIT_PARAMS,
    )(recv_sems, *lands, after)
    return outs[0], outs[1], outs[2:2 + n], outs[-1]


def _gather_finish(send_sems, recv_sems, fwd_send, fwd_recv, lands, kinds, shapes, after, *, name):
    n = len(lands)

    def body(*refs):
        send_ref, recv_ref, fsend_ref, frecv_ref = refs[:4]
        land_refs = refs[4:4 + n]
        x, y, c = _my_place()
        chips = [(1 - x, y), (x, 1 - y), (1 - x, 1 - y)]
        sibling = (x, y, 1 - c)
        for i in range(n):
            def window(j):
                return _gather_window(land_refs[i], kinds[i], shapes[i], j)

            mine, theirs = window(4 * x + 2 * y + c), window(4 * x + 2 * y + (1 - c))
            pltpu.make_async_remote_copy(src_ref=mine, dst_ref=theirs, send_sem=send_ref.at[4 * i],
                                         recv_sem=recv_ref.at[4 * i], device_id=sibling, device_id_type=MESH).wait_recv()
            for j, (px, py) in enumerate(chips):
                block = window(4 * px + 2 * py + (1 - c))
                pltpu.make_async_remote_copy(src_ref=block, dst_ref=block, send_sem=fsend_ref.at[3 * i + j],
                                             recv_sem=frecv_ref.at[3 * i + j], device_id=sibling,
                                             device_id_type=MESH).wait_recv()
            for k in range(4):
                pltpu.make_async_remote_copy(src_ref=mine, dst_ref=mine, send_sem=send_ref.at[4 * i + k],
                                             recv_sem=recv_ref.at[4 * i + k], device_id=sibling,
                                             device_id_type=MESH).wait_send()
            for j, (px, py) in enumerate(chips):
                block = window(4 * px + 2 * py + c)
                pltpu.make_async_remote_copy(src_ref=block, dst_ref=block, send_sem=fsend_ref.at[3 * i + j],
                                             recv_sem=frecv_ref.at[3 * i + j], device_id=sibling,
                                             device_id_type=MESH).wait_send()

    return pl.pallas_call(
        body, name=name,
        out_shape=[_hbm_like(a.shape, a.dtype) for a in lands],
        in_specs=[SEM] * 4 + [HBM] * n + [ANY], out_specs=[HBM] * n,
        input_output_aliases={4 + i: i for i in range(n)},
        compiler_params=SPLIT_PARAMS,
    )(send_sems, recv_sems, fwd_send, fwd_recv, *lands, after)


def _pair_plan(src_ref, land_ref, x, y, c):
    return [(src_ref.at[2 * k + (1 - c)], land_ref.at[k], (x, y, 1 - c)) for k in range(N_CHIPS)]


def _chip_plan(src_ref, land_ref, x, y, c):
    chips = [(1 - x, y), (x, 1 - y), (1 - x, 1 - y)]
    return [(src_ref.at[2 * px + py], land_ref.at[k], (px, py, c)) for k, (px, py) in enumerate(chips)]


def _exchange_copies(plan, per, src_refs, land_refs, send_sems, recv_sems):
    x, y, c = _my_place()
    copies = []
    for i, (s_ref, l_ref) in enumerate(zip(src_refs, land_refs)):
        for q, (src, dst, to) in enumerate(plan(s_ref, l_ref, x, y, c)):
            copies.append(pltpu.make_async_remote_copy(
                src_ref=src, dst_ref=dst, send_sem=send_sems.at[per * i + q], recv_sem=recv_sems.at[per * i + q],
                device_id=to, device_id_type=MESH))
    return copies


def _exchange_start(srcs, plan, per, *, name):
    n = len(srcs)

    def body(*refs):
        src_refs, land_refs = refs[:n], refs[n:2 * n]
        send_sems, recv_sems = refs[2 * n], refs[2 * n + 1]
        for cp in _exchange_copies(plan, per, src_refs, land_refs, send_sems, recv_sems):
            cp.start()
        refs[-1][...] = jnp.zeros_like(refs[-1])

    lands = [lax.empty((per,) + s.shape[1:], s.dtype) for s in srcs]
    outs = pl.pallas_call(
        body, name=name,
        out_shape=[pltpu.SemaphoreType.DMA((per * n,)), pltpu.SemaphoreType.DMA((per * n,))]
        + [_hbm_like(s.shape, s.dtype) for s in srcs] + [_hbm_like(a.shape, a.dtype) for a in lands] + [TOKEN],
        in_specs=[HBM] * (2 * n), out_specs=[SEM, SEM] + [HBM] * (2 * n) + [TOKEN_SPEC],
        input_output_aliases={i: 2 + i for i in range(2 * n)},
        compiler_params=SPLIT_PARAMS,
    )(*[_in_hbm(s) for s in srcs], *[_in_hbm(a) for a in lands])
    return outs[0], outs[1], outs[2:2 + n], outs[2 + n:2 + 2 * n], outs[-1]


def _exchange_wait(send_sems, recv_sems, srcs, lands, plan, per, after, *, name):
    n = len(srcs)
    after = list(after) if isinstance(after, (list, tuple)) else [after]

    def body(*refs):
        send_ref, recv_ref = refs[0], refs[1]
        src_refs, land_refs = refs[2:2 + n], refs[2 + n:2 + 2 * n]
        copies = _exchange_copies(plan, per, src_refs, land_refs, send_ref, recv_ref)
        for cp in copies:
            cp.wait_recv()
        for cp in copies:
            cp.wait_send()

    outs = pl.pallas_call(
        body, name=name,
        out_shape=[_hbm_like(s.shape, s.dtype) for s in srcs] + [_hbm_like(a.shape, a.dtype) for a in lands],
        in_specs=[SEM, SEM] + [HBM] * (2 * n) + [ANY] * len(after), out_specs=[HBM] * (2 * n),
        input_output_aliases={2 + i: i for i in range(2 * n)},
        compiler_params=SPLIT_PARAMS,
    )(send_sems, recv_sems, *srcs, *lands, *after)
    return outs[:n], outs[n:]


REDUCE_BLOCK_BYTES = 2 << 20


def _row_tile(r, c):
    row_bytes = 4 * (-(-c // LANES) * LANES)
    best = r
    for d in range(SUBLANES, r, SUBLANES):
        if r % d == 0 and d * row_bytes <= REDUCE_BLOCK_BYTES:
            best = d
    return best if r * row_bytes > REDUCE_BLOCK_BYTES else r


def _reduce_pair_sum(blocked, recv, place, wire_dtype, *, name):
    _, r, c = blocked.shape
    tr = _row_tile(r, c)

    def body(place_ref, g_ref, r_ref, own_ref, send_ref):
        s = g_ref[...] + r_ref[...]
        send_ref[...] = s.astype(wire_dtype)

        @pl.when(pl.program_id(1) == place_ref[1])
        def _():
            own_ref[...] = s

    return pl.pallas_call(
        body, name=name,
        grid_spec=pltpu.PrefetchScalarGridSpec(
            num_scalar_prefetch=1, grid=(r // tr, N_CHIPS),
            in_specs=[pl.BlockSpec((None, None, tr, c), lambda i, k, place_ref: (k, place_ref[0], i, 0)),
                      pl.BlockSpec((None, tr, c), lambda i, k, place_ref: (k, i, 0))],
            out_specs=[pl.BlockSpec((tr, c), lambda i, k, place_ref: (i, 0)),
                       pl.BlockSpec((None, tr, c), lambda i, k, place_ref: (k, i, 0))]),
        out_shape=[jax.ShapeDtypeStruct((r, c), F32), jax.ShapeDtypeStruct((N_CHIPS, r, c), wire_dtype)],
        compiler_params=_params(("parallel", "arbitrary")),
    )(place, blocked.reshape(N_CHIPS, 2, r, c), recv)


def _chip_sum(own_ref, r_ref):
    return ((own_ref[...] + r_ref[0].astype(F32)) + r_ref[1].astype(F32)) + r_ref[2].astype(F32)


def _reduce_chip_sum(own, recv, *, name):
    r, c = own.shape
    tr = _row_tile(r, c)

    def body(own_ref, r_ref, o_ref):
        o_ref[...] = _chip_sum(own_ref, r_ref)

    return pl.pallas_call(
        body, name=name, grid=(r // tr,),
        in_specs=[pl.BlockSpec((tr, c), lambda i: (i, 0)), pl.BlockSpec((N_CHIPS - 1, tr, c), lambda i: (0, i, 0))],
        out_specs=pl.BlockSpec((tr, c), lambda i: (i, 0)),
        out_shape=jax.ShapeDtypeStruct((r, c), F32),
        compiler_params=_params(("parallel",)),
    )(own, recv)


def _adamw_math(w, g, m, v):
    nm = ADAM_B1 * m + (1.0 - ADAM_B1) * g
    nv = ADAM_B2 * v + (1.0 - ADAM_B2) * (g * g)
    m_hat = nm / (1.0 - ADAM_B1 ** ADAM_STEP)
    v_hat = nv / (1.0 - ADAM_B2 ** ADAM_STEP)
    return -ADAM_LR * (m_hat / (jnp.sqrt(v_hat) + ADAM_EPS) + ADAM_WD * w), nm, nv


def _adamw(w, g, m, v, *, name):
    shape = w.shape
    C = shape[-1]
    R = math.prod(shape[:-1])
    tr = _row_tile(R, C)

    def body(w_ref, g_ref, m_ref, v_ref, d_ref, nm_ref, nv_ref):
        d_ref[...], nm_ref[...], nv_ref[...] = _adamw_math(w_ref[...], g_ref[...], m_ref[...], v_ref[...])

    spec = pl.BlockSpec((tr, C), lambda i: (i, 0))
    outs = pl.pallas_call(
        body, name=name, grid=(R // tr,),
        in_specs=[spec] * 4, out_specs=[spec] * 3,
        out_shape=[jax.ShapeDtypeStruct((R, C), F32)] * 3,
        compiler_params=_params(("parallel",)),
    )(*[a.reshape(R, C) for a in (w, g, m, v)])
    return tuple(o.reshape(shape) for o in outs)


def _reduce_adamw(own, recv, w, m, v, layer, prev, *, name):
    r, c = own.shape
    tr = _row_tile(r, c)
    n_prev = 0 if prev is None else len(prev)

    def body(own_ref, r_ref, w_ref, m_ref, v_ref, *rest):
        g_ref, d_ref, nm_ref, nv_ref = rest[n_prev:]
        g = _chip_sum(own_ref, r_ref)
        g_ref[...] = g
        d_ref[...], nm_ref[...], nv_ref[...] = _adamw_math(w_ref[...], g, m_ref[...], v_ref[...])

    slot = pl.BlockSpec((None, tr, c), lambda i: (layer, i, 0))
    return pl.pallas_call(
        body, name=name, grid=(r // tr,),
        in_specs=[pl.BlockSpec((tr, c), lambda i: (i, 0)), pl.BlockSpec((N_CHIPS - 1, tr, c), lambda i: (0, i, 0)),
                  slot, slot, slot] + [ANY] * n_prev,
        out_specs=[slot] * 4,
        out_shape=[jax.ShapeDtypeStruct((DEPTH, r, c), F32)] * 4,
        input_output_aliases={5 + k: k for k in range(n_prev)},
        compiler_params=_params(("parallel",)),
    )(own, recv, w, m, v, *(prev or ()))


REPLICATED = (("mix_norm", (D_MODEL,)), ("q_norm", (HEAD_DIM,)), ("k_norm", (HEAD_DIM,)), ("sinks", (N_Q_HEADS,)),
              ("sgu_norm", (SGU_WIDTH,)), ("w_s", (SGU_GROUPS, BLOCK, BLOCK)), ("b_s", (SGU_GROUPS, BLOCK)),
              ("ffn_norm", (D_MODEL,)), ("conv_b", (2 * D_FF,)))
TRANSPOSED = ("w_in", "w_up")
SHARDED = (("w_in", "rows"), ("w_oa", "cols"), ("w_ob", "cols"), ("w_out", "rows"), ("w_up", "rows"),
           ("conv_w", "blocks"), ("w_down", "rows"))
WEIGHT_ORDER = ("mix_norm", "w_in", "q_norm", "k_norm", "sinks", "sgu_norm", "w_s", "b_s", "w_oa", "w_ob", "w_out",
                "ffn_norm", "w_up", "conv_w", "conv_b", "w_down")
MIXER_WEIGHTS = ["w_in", "w_oa", "w_ob", "w_out"]
FFN_WEIGHTS = ["w_up", "conv_w", "w_down"]


def _small_layout():
    segs, off = {}, 0
    for l in range(DEPTH):
        for name, shape in REPLICATED:
            n = math.prod(shape)
            segs[(l, name)] = (off, n)
            off += n
    per_dev = -(-off // (N_DEV * SUBLANES * LANES)) * SUBLANES * LANES
    return segs, off, per_dev


def _pack_small(grads, loss_part):
    ssegs, total, per_dev = _small_layout()
    flat = jnp.concatenate([grads[l][name].reshape(-1) for (l, name) in ssegs] + [loss_part.reshape(1)])
    return jnp.pad(flat, (0, N_DEV * per_dev - total - 1)).reshape(N_DEV, per_dev // LANES, LANES)


def _unpack_small(gathered):
    ssegs, total, _ = _small_layout()
    flat = gathered.reshape(-1)
    shapes = dict(REPLICATED)
    small = {name: jnp.stack([flat[ssegs[(l, name)][0]:ssegs[(l, name)][0] + ssegs[(l, name)][1]].reshape(shapes[name])
                              for l in range(DEPTH)]) for name, _ in REPLICATED}
    return small, flat[total]


def kernel(x, mix_norm, w_in, q_norm, k_norm, sinks, sgu_norm, w_s, b_s, w_oa, w_ob, w_out, ffn_norm, w_up, conv_w, conv_b, w_down, loss_target, m_mix_norm, m_w_in, m_q_norm, m_k_norm, m_sinks, m_sgu_norm, m_w_s, m_b_s, m_w_oa, m_w_ob, m_w_out, m_ffn_norm, m_w_up, m_conv_w, m_conv_b, m_w_down, v_mix_norm, v_w_in, v_q_norm, v_k_norm, v_sinks, v_sgu_norm, v_w_s, v_b_s, v_w_oa, v_w_ob, v_w_out, v_ffn_norm, v_w_up, v_conv_w, v_conv_b, v_w_down):
    W = dict(mix_norm=mix_norm, w_in=w_in, q_norm=q_norm, k_norm=k_norm, sinks=sinks, sgu_norm=sgu_norm, w_s=w_s, b_s=b_s,
             w_oa=w_oa, w_ob=w_ob, w_out=w_out, ffn_norm=ffn_norm, w_up=w_up, conv_w=conv_w, conv_b=conv_b, w_down=w_down)
    M = dict(mix_norm=m_mix_norm, w_in=m_w_in, q_norm=m_q_norm, k_norm=m_k_norm, sinks=m_sinks, sgu_norm=m_sgu_norm,
             w_s=m_w_s, b_s=m_b_s, w_oa=m_w_oa, w_ob=m_w_ob, w_out=m_w_out, ffn_norm=m_ffn_norm, w_up=m_w_up,
             conv_w=m_conv_w, conv_b=m_conv_b, w_down=m_w_down)
    V = dict(mix_norm=v_mix_norm, w_in=v_w_in, q_norm=v_q_norm, k_norm=v_k_norm, sinks=v_sinks, sgu_norm=v_sgu_norm,
             w_s=v_w_s, b_s=v_b_s, w_oa=v_w_oa, w_ob=v_w_ob, w_out=v_w_out, ffn_norm=v_ffn_norm, w_up=v_w_up,
             conv_w=v_conv_w, conv_b=v_conv_b, w_down=v_w_down)
    n_seq, seq, d_model = x.shape
    tokens = n_seq * seq
    mx, my, mc = _my_place()
    place = jnp.stack([mc, 2 * mx + my]).astype(jnp.int32)
    half = N_DEV // 2
    kind_of = dict(SHARDED)
    for name in TRANSPOSED:
        W[name], M[name], V[name] = (jnp.swapaxes(t[name], 1, 2) for t in (W, M, V))

    gather_groups = [[(0, MIXER_WEIGHTS[0])], [(0, n) for n in MIXER_WEIGHTS[1:]], [(0, n) for n in FFN_WEIGHTS],
                     [(1, n) for n in MIXER_WEIGHTS], [(1, n) for n in FFN_WEIGHTS]]
    started, in_flight = {}, {}
    weights = []
    for l in range(DEPTH):
        w = {name: W[name][l] for name, _ in REPLICATED}
        w["cb_g"], w["cb_v"] = W["conv_b"][l][:D_FF], W["conv_b"][l][D_FF:]
        w["bias_full"] = jnp.repeat(W["b_s"][l].T, SGU_WIDTH // SGU_GROUPS, axis=1)
        weights.append(w)

    def gather_start(gi, after=()):
        shards = [W[name][l] for l, name in gather_groups[gi]]
        kinds = [kind_of[name] for _, name in gather_groups[gi]]
        shapes = [s.shape for s in shards]
        lands = _place_own(shards, kinds, [F32 if name == "conv_w" else BF16 for _, name in gather_groups[gi]],
                           name=f"gather_weights_own_{gi}", deps=after)
        send, recv, lands, token = _gather_start(lands, kinds, shapes, after, name=f"gather_weights_start_{gi}")
        started[gi] = dict(sems=(send, recv), lands=lands, kinds=kinds, shapes=shapes)
        return token

    def gather_forward(gi, after):
        st = started[gi]
        in_flight[gi] = _gather_forward(st["sems"][1], st["lands"], st["kinds"], st["shapes"], after,
                                        name=f"gather_weights_forward_{gi}")
        return in_flight[gi][3]

    def gather_finish(gi, after):
        st = started.pop(gi)
        fwd_send, fwd_recv, lands_g, _ = in_flight.pop(gi)
        whole = _gather_finish(st["sems"][0], st["sems"][1], fwd_send, fwd_recv, lands_g, st["kinds"], st["shapes"], after,
                               name=f"gather_weights_finish_{gi}")
        for (l, name), arr in zip(gather_groups[gi], whole):
            w = weights[l]
            if name in TRANSPOSED:
                w[name + "_t"] = arr
            elif name == "conv_w":
                w["cw_g"] = arr[:half].transpose(1, 0, 2).reshape(3, D_FF)
                w["cw_v"] = arr[half:].transpose(1, 0, 2).reshape(3, D_FF)
            else:
                w[name] = arr

    reduce_state, results = {}, {}
    wire = {"conv_w": F32, "small": F32}

    def reduce_begin(key, names, arrays):
        send, recv, srcs_, lands_, token = _exchange_start(arrays, _pair_plan, N_CHIPS, name=f"reduce_pair_start_{key}")
        reduce_state[key] = dict(names=names, pair=(send, recv, srcs_, lands_))
        return [token]

    def reduce_pair(key, after):
        st = reduce_state[key]
        send, recv, srcs_, lands_ = st.pop("pair")
        blocked_, from_sibling = _exchange_wait(send, recv, srcs_, lands_, _pair_plan, N_CHIPS, after,
                                                name=f"reduce_pair_wait_{key}")
        sums = [_reduce_pair_sum(b, r, place, wire.get(n if isinstance(n, str) else n[1], BF16),
                                 name=f"reduce_pair_sum_{key}_{i}")
                for i, (n, b, r) in enumerate(zip(st["names"], blocked_, from_sibling))]
        st["own"] = [s[0] for s in sums]
        *st["chip"], token = _exchange_start([s[1] for s in sums], _chip_plan, N_CHIPS - 1, name=f"reduce_chip_start_{key}")
        return [token]

    def reduce_end(key, after):
        st = reduce_state.pop(key)
        send, recv, srcs_, lands_ = st["chip"]
        _, from_chips = _exchange_wait(send, recv, srcs_, lands_, _chip_plan, N_CHIPS - 1, after,
                                       name=f"reduce_chip_wait_{key}")
        done = []
        for n, own, got in zip(st["names"], st["own"], from_chips):
            if n == "small":
                results["small"] = _reduce_chip_sum(own, got, name="reduce_chip_sum_small")
            else:
                l, name = n
                results[name] = _reduce_adamw(own, got, W[name], M[name], V[name], l, results.get(name),
                                              name=f"l{l}_reduce_adamw_{name}")
                done.append(results[name][0])
        return done

    def sched(point, l, carry, g=None):
        deps = []
        if point == "begin":
            token = ()
            for gi in range(len(gather_groups)):
                token = [gather_start(gi, token)]
            deps = token
        elif point == "fwd_start" and l == 0:
            gather_finish(0, gather_forward(0, carry))
        elif point == "fwd_att" and l == 0:
            gather_finish(1, gather_forward(1, carry))
            deps = [gather_forward(2, carry)]
        elif point == "fwd_mixer_done" and l == 0:
            gather_finish(2, carry)
        elif point == "fwd_conv" and l == 0:
            deps = [gather_forward(3, carry)]
        elif point == "fwd_start" and l == 1:
            gather_finish(3, carry)
        elif point == "fwd_att" and l == 1:
            deps = [gather_forward(4, carry)]
        elif point == "fwd_mixer_done" and l == 1:
            gather_finish(4, carry)
        elif point == "bwd_ffn_grads":
            conv_w = jnp.concatenate([g[k].reshape(3, half, W_UP_SHARD).transpose(1, 0, 2) for k in ("cw_g", "cw_v")])
            deps = reduce_begin(
                f"l{l}_ffn", [(l, "w_down"), (l, "w_up"), (l, "conv_w")],
                [g["w_down"].reshape(N_DEV, D_FF // N_DEV, D_MODEL),
                 g["w_up_t"].reshape(N_DEV, W_UP_SHARD, D_MODEL), conv_w])
        elif point == "bwd_merge":
            deps = reduce_pair(f"l{l}_ffn", carry)
        elif point == "bwd_out_grads":
            deps = reduce_begin(
                f"l{l}_out", [(l, "w_out"), (l, "w_oa"), (l, "w_ob")],
                [g["w_out"].reshape(N_DEV, D_MODEL // N_DEV, D_MODEL),
                 _disassemble((g["w_oa"],), LANES, _w_o_moves(), name=f"l{l}_split_dw_oa"),
                 _disassemble((g["w_ob"],), LANES, _w_o_moves(), name=f"l{l}_split_dw_ob")])
        elif point == "bwd_att":
            deps = reduce_pair(f"l{l}_out", carry)
        elif point == "bwd_w_in_grad":
            deps = reduce_begin(f"l{l}_in", [(l, "w_in")], [g["w_in_t"].reshape(N_DEV, W_IN_SHARD, D_MODEL)])
        elif point == "bwd_dh":
            deps = reduce_pair(f"l{l}_in", carry)
        return deps

    loss_part, dx, grads, last_deps = _local_step(x.reshape(tokens, d_model), loss_target.reshape(tokens, d_model),
                                                  weights, sched, n_seq=n_seq, seq=seq)
    for g in grads:
        g["conv_b"] = jnp.concatenate([g["cb_g"], g["cb_v"]])
    after = [dx, *last_deps, *reduce_begin("small", ["small"], [_pack_small(grads, loss_part)])]
    for key in [f"l{l}_{part}" for l in reversed(range(DEPTH)) for part in ("ffn", "out", "in")][:-1]:
        after = reduce_end(key, after)
    after = reduce_end("l0_in", after + reduce_pair("small", after))
    reduce_end("small", after)

    G, delta, new_m, new_v = {}, {}, {}, {}
    for name, _ in SHARDED:
        outs = [jnp.swapaxes(o, 1, 2) for o in results[name]] if name in TRANSPOSED else results[name]
        G[name], delta[name], new_m[name], new_v[name] = outs
    small, loss = _unpack_small(_gather([results["small"]], ["blocks"], name="gather_small_grads")[0])
    G.update(small)
    for name, _ in REPLICATED:
        delta[name], new_m[name], new_v[name] = _adamw(W[name], G[name], M[name], V[name], name=f"adamw_{name}")
    return (loss, dx.reshape(n_seq, seq, d_model), *[G[n] for n in WEIGHT_ORDER], *[delta[n] for n in WEIGHT_ORDER],
            *[new_m[n] for n in WEIGHT_ORDER], *[new_v[n] for n in WEIGHT_ORDER])
```

```python
import math

import jax
import jax.numpy as jnp
from jax import lax
from jax.experimental import pallas as pl
from jax.experimental.pallas import tpu as pltpu

F32 = jnp.float32
BF16 = jnp.bfloat16
ACT_DTYPE = BF16
MESH = pl.DeviceIdType.MESH

DEPTH = 2
D_MODEL = 1024
N_Q_HEADS = 8
HEAD_DIM = 64
ATT_WIDTH = 512
KV_WIDTH = 128
BLOCK = 128
SGU_WIDTH = 512
SGU_GROUPS = 8
IN_WIDTH = 3840
D_FF = 2816
NORM_EPS = 1e-6
NEG_INF = -1e30
ATT_SCALE = HEAD_DIM ** -0.5
ALIBI_SLOPES = tuple(2.0 ** (-(h + 1)) for h in range(N_Q_HEADS))
ADAM_LR, ADAM_B1, ADAM_B2, ADAM_EPS, ADAM_WD, ADAM_STEP = 0.001, 0.9, 0.999, 1e-08, 0.01, 10
N_DEV = 8
N_CHIPS = 4

QKV_WIDTH = ATT_WIDTH + 2 * KV_WIDTH
COL_SUV, COL_GA, COL_GB, COL_QKV = 0, 1024, 2048, 3072
W_IN_ROTATE = (1, IN_WIDTH // QKV_WIDTH)

LANES = 128
SUBLANES = 8
VMEM_LIMIT_V7X = 56 * 1024 * 1024
GELU_C = math.sqrt(2.0 / math.pi)
GELU_K = 0.044715
ANY = pl.BlockSpec(memory_space=pl.ANY)


def _params(sem=None):
    return pltpu.CompilerParams(dimension_semantics=sem, vmem_limit_bytes=VMEM_LIMIT_V7X)


def _sigmoid(x):
    return 1.0 / (1.0 + jnp.exp(-x))


def _gelu(x):
    th = jnp.tanh(GELU_C * (x + GELU_K * x * x * x))
    return 0.5 * x * (1.0 + th)


def _gelu_and_grad(x):
    x2 = x * x
    th = jnp.tanh(GELU_C * (x + GELU_K * x2 * x))
    g = 0.5 * x * (1.0 + th)
    dg = 0.5 * (1.0 + th) + 0.5 * x * (1.0 - th * th) * (GELU_C * (1.0 + 3.0 * GELU_K * x2))
    return g, dg


def _dot(a, b, dims):
    return lax.dot_general(a, b, (dims, ((), ())), preferred_element_type=F32)


def _dot_nn(a, b):
    return _dot(a, b, ((1,), (0,)))


def _dot_nt(a, b):
    return _dot(a, b, ((1,), (1,)))


def _dot_tn(a, b):
    return _dot(a, b, ((0,), (0,)))


def _lo_mask(shape):
    return lax.broadcasted_iota(jnp.int32, shape, len(shape) - 1) < (LANES // 2)


def _half_sums(x, lo):
    s_lo = jnp.sum(jnp.where(lo, x, 0.0), axis=-1, keepdims=True)
    s_all = jnp.sum(x, axis=-1, keepdims=True)
    return jnp.where(lo, s_lo, s_all - s_lo)


def _dup_half(x, half, lo):
    r = pltpu.roll(x, LANES // 2, axis=1)
    return jnp.where(lo, x, r) if half == 0 else jnp.where(lo, r, x)


def _with_deps(body, n_in, deps):
    k = len(deps)
    if not k:
        return body, [], ()

    def skipping(*refs):
        return body(*refs[:n_in], *refs[n_in + k:])

    return skipping, [ANY] * k, tuple(deps)


MM_VMEM_BUDGET = 40 * 1024 * 1024
MM_MAX_TILE = 1408
MM_MAX_TK = 4096
MM_STEP_BYTES = 1 << 20


def _divisors(n, step, cap):
    return [d for d in range(step, min(n, cap) + 1, step) if n % d == 0] or [n]


def _mm_tiles(M, N, K, out_bytes, tm_divides, tn_divides):
    best = None
    for tm in _divisors(M, LANES, MM_MAX_TILE):
        for tn in _divisors(N, LANES, MM_MAX_TILE):
            if tm_divides % tm or tn_divides % tn:
                continue
            for tk in _divisors(K, 4 * LANES, MM_MAX_TK):
                vmem = 4 * (tm * tk + tk * tn) + 2 * tm * tn * out_bytes + (0 if tk == K else 4 * tm * tn)
                if vmem > MM_VMEM_BUDGET:
                    continue
                traffic = 2 * M * K * (N // tn) + 2 * K * N * (M // tm) + M * N * out_bytes
                cost = traffic + (K // tk - 1) * 8 * M * N + (M // tm) * (N // tn) * (K // tk) * MM_STEP_BYTES
                if best is None or cost < best[0]:
                    best = (cost, tm, tn, tk)
    assert best is not None, (M, N, K)
    return best[1:]


def _mm(a, b, *, mode, out_dtype, name, deps=(), b_rows=(0, None), rotate=None, out_rows=(0, None), out_prev=None):
    b_first, b_count = b_rows
    if mode == "nn":
        (M, K), N = a.shape, b.shape[1]
    elif mode == "nt":
        (M, K), N = a.shape, (b.shape[0] if b_count is None else b_count)
    else:
        (K, M), N = a.shape, b.shape[1]
    shift, period = rotate or (0, 1)
    assert period == 1 or mode == "nt"
    out_first, out_total = out_rows[0], (M if out_rows[1] is None else out_rows[1])
    tm, tn, tk = _mm_tiles(M, N, K, jnp.dtype(out_dtype).itemsize, math.gcd(M, out_first),
                           math.gcd(N // period, b_first if mode == "nt" else 0))
    gm, gn, gk = M // tm, N // tn, K // tk

    def turned(j):
        per = N // period // tn
        return ((j // per + shift) % period) * per + j % per if period > 1 else j

    if mode == "nn":
        a_spec = pl.BlockSpec((tm, tk), lambda i, j, k: (i, k))
        b_spec = pl.BlockSpec((tk, tn), lambda i, j, k: (k + b_first // tk, j))
        contract = ((1,), (0,))
    elif mode == "nt":
        a_spec = pl.BlockSpec((tm, tk), lambda i, j, k: (i, k))
        b_spec = pl.BlockSpec((tn, tk), lambda i, j, k: (turned(j) + b_first // tn, k))
        contract = ((1,), (1,))
    else:
        a_spec = pl.BlockSpec((tk, tm), lambda i, j, k: (k, i))
        b_spec = pl.BlockSpec((tk, tn), lambda i, j, k: (k, j))
        contract = ((0,), (0,))
    o_spec = pl.BlockSpec((tm, tn), lambda i, j, k: (i + out_first // tm, j))
    assert b_first % (tk if mode == "nn" else tn) == 0 and out_first % tm == 0, (name, tm, tn, tk)
    n_prev = 0 if out_prev is None else 1

    def body(a_ref, b_ref, *rest):
        o_ref = rest[n_prev]
        part = _dot(a_ref[...].astype(BF16), b_ref[...].astype(BF16), contract)
        if gk == 1:
            o_ref[...] = part.astype(out_dtype)
            return
        acc_ref = rest[n_prev + 1]
        k = pl.program_id(2)

        @pl.when(k == 0)
        def _():
            acc_ref[...] = part

        @pl.when(k > 0)
        def _():
            acc_ref[...] += part

        @pl.when(k == gk - 1)
        def _():
            o_ref[...] = acc_ref[...].astype(out_dtype)

    body, dep_specs, dep_args = _with_deps(body, 2 + n_prev, deps)
    return pl.pallas_call(
        body,
        name=name,
        grid=(gm, gn, gk),
        in_specs=[a_spec, b_spec] + [ANY] * n_prev + dep_specs,
        out_specs=o_spec,
        out_shape=jax.ShapeDtypeStruct((out_total, N), out_dtype),
        input_output_aliases={2: 0} if n_prev else {},
        scratch_shapes=[] if gk == 1 else [pltpu.VMEM((tm, tn), F32)],
        compiler_params=_params(("parallel", "parallel", "arbitrary")),
    )(a, b, *([out_prev] if n_prev else []), *dep_args)


def _mm_tn_parts(parts, at, b, *, name):
    K, N = b.shape
    n = len(parts)
    tm = math.gcd(*[p.shape[1] for p in parts], *at)
    tiles = [p.shape[1] // tm for p in parts]
    first = [sum(tiles[:p]) for p in range(n)]

    def mine(i, p):
        return jnp.logical_and(i >= first[p], i < first[p] + tiles[p])

    def out_tile(i):
        t = 0
        for p in range(n):
            t = jnp.where(mine(i, p), at[p] // tm + i - first[p], t)
        return t

    def body(*refs):
        a_refs, b_ref, o_ref = refs[:n], refs[n], refs[n + 1]
        for p in range(n):
            @pl.when(mine(pl.program_id(0), p))
            def _(p=p):
                o_ref[...] = _dot_tn(a_refs[p][...], b_ref[...])

    return pl.pallas_call(
        body, name=name, grid=(sum(tiles),),
        in_specs=[pl.BlockSpec((K, tm), lambda i, p=p: (0, jnp.clip(i - first[p], 0, tiles[p] - 1))) for p in range(n)]
        + [pl.BlockSpec((K, N), lambda i: (0, 0), pipeline_mode=pl.Buffered(1))],
        out_specs=pl.BlockSpec((tm, N), lambda i: (out_tile(i), 0)),
        out_shape=jax.ShapeDtypeStruct((sum(p.shape[1] for p in parts), N), F32),
        compiler_params=_params(("arbitrary",)),
    )(*parts, b)


def _mm_rows(a, b, *, mode, fn, out_dtypes, rows=(), vecs=(), reduce=False, name, deps=(), b_rows=(0, None), a_at=None):
    parts = a if a_at is not None else (a,)
    starts = a_at if a_at is not None else (0,)
    n_parts = len(parts)
    M, K = parts[0].shape[0], sum(p.shape[1] for p in parts)
    b_first, b_count = b_rows[0], (b.shape[0] if b_rows[1] is None else b_rows[1])
    N = b.shape[1] if mode == "nn" else b_count
    contract = ((1,), (0,)) if mode == "nn" else ((1,), (1,))
    n_rows, n_vecs, n_out = len(rows), len(vecs), len(out_dtypes)
    out_bytes = sum(jnp.dtype(d).itemsize for d in out_dtypes)
    tm = max(t for t in _divisors(M, LANES, MM_MAX_TILE)
             if 4 * t * K + 2 * K * N + 2 * t * N * (4 * n_rows + out_bytes) <= MM_VMEM_BUDGET)
    assert b_first % b_count == 0 and (a_at is None or mode == "nn")

    def body(*refs):
        a_refs, b_ref, rest = refs[:n_parts], refs[n_parts], refs[n_parts + 1:]
        row_refs, vec_refs = rest[:n_rows], rest[n_rows:n_rows + n_vecs]
        out_refs = rest[n_rows + n_vecs:]
        if a_at is None:
            acc = _dot(a_refs[0][...], b_ref[...], contract)
        else:
            acc = sum(_dot(r[...], b_ref[at:at + r.shape[1], :], contract) for r, at in zip(a_refs, starts))
        res = fn(acc, *[r[...] for r in row_refs], *[v[...] for v in vec_refs])
        for o_ref, val in zip(out_refs[:n_out], res):
            o_ref[...] = val.astype(o_ref.dtype)
        if reduce:
            @pl.when(pl.program_id(0) == 0)
            def _():
                out_refs[n_out][...] = res[n_out]

            @pl.when(pl.program_id(0) > 0)
            def _():
                out_refs[n_out][...] += res[n_out]

    row = pl.BlockSpec((tm, N), lambda i: (i, 0))
    vec = pl.BlockSpec((1, N), lambda i: (0, 0))
    body, dep_specs, dep_args = _with_deps(body, n_parts + 1 + n_rows + n_vecs, deps)
    return pl.pallas_call(
        body, name=name, grid=(M // tm,),
        in_specs=[pl.BlockSpec((tm, p.shape[1]), lambda i: (i, 0)) for p in parts]
        + [pl.BlockSpec((b_count, b.shape[1]), lambda i: (b_first // b_count, 0), pipeline_mode=pl.Buffered(1))]
        + [row] * n_rows + [vec] * n_vecs + dep_specs,
        out_specs=[row] * n_out + [vec] * reduce,
        out_shape=[jax.ShapeDtypeStruct((M, N), d) for d in out_dtypes] + [jax.ShapeDtypeStruct((1, N), F32)] * reduce,
        compiler_params=_params(("arbitrary",)),
    )(*parts, b, *rows, *[v.reshape(1, N) for v in vecs], *dep_args)


def _rms(x, gain):
    return x * lax.rsqrt(jnp.mean(x * x, axis=-1, keepdims=True) + NORM_EPS) * gain


def _residual_then_norm(acc, x, gain):
    x_out = x + acc
    return x_out, _rms(x_out, gain)


def _residual_then_loss(acc, x, target):
    err = (x + acc) - target
    dy = err * (1.0 / D_MODEL)
    return dy, dy, jnp.sum(err * err, axis=0, keepdims=True) * (0.5 / D_MODEL)


def _rms_bwd_rows(dh, x, dres, gain):
    r = lax.rsqrt(jnp.mean(x * x, axis=-1, keepdims=True) + NORM_EPS)
    xh = x * r
    dxh = dh * gain
    dx = dres + r * (dxh - xh * jnp.mean(dxh * xh, axis=-1, keepdims=True))
    return dx, dx, jnp.sum(dh * xh, axis=0, keepdims=True)


def _rms_fwd(x, gain, *, name, tm=512, deps=()):
    T, D = x.shape

    def body(x_ref, g_ref, h_ref):
        xv = x_ref[...]
        r = lax.rsqrt(jnp.mean(xv * xv, axis=-1, keepdims=True) + NORM_EPS)
        h_ref[...] = (xv * r * g_ref[...]).astype(BF16)

    body, dep_specs, dep_args = _with_deps(body, 2, deps)
    return pl.pallas_call(
        body, name=name, grid=(T // tm,),
        in_specs=[pl.BlockSpec((tm, D), lambda i: (i, 0)), pl.BlockSpec((1, D), lambda i: (0, 0))] + dep_specs,
        out_specs=pl.BlockSpec((tm, D), lambda i: (i, 0)),
        out_shape=jax.ShapeDtypeStruct((T, D), BF16),
        compiler_params=_params(("parallel",)),
    )(x, gain.reshape(1, D), *dep_args)


def _head_norm(x, gain2, lo):
    ms = _half_sums(x * x, lo) * (1.0 / HEAD_DIM)
    r = lax.rsqrt(ms + NORM_EPS)
    xh = x * r
    return xh * gain2, xh, r


def _head_norm_bwd(xh, r, gain2, dy, lo):
    dxh = dy * gain2
    dx = r * (dxh - xh * (_half_sums(dxh * xh, lo) * (1.0 / HEAD_DIM)))
    return dx, dy * xh


Q_GROUP = N_Q_HEADS // 2
GROUP_ROWS = Q_GROUP * BLOCK
ATT_SCRATCH = (pltpu.VMEM((2, 2, GROUP_ROWS, BLOCK), F32), pltpu.VMEM((2, GROUP_ROWS, 1), F32))


def _att_consts(sink_ref, bias_ref, sinkcol_ref):
    row = lax.broadcasted_iota(jnp.int32, (GROUP_ROWS, BLOCK), 0)
    kj = lax.broadcasted_iota(jnp.int32, (GROUP_ROWS, BLOCK), 1)
    head = row // BLOCK
    head_col = lax.broadcasted_iota(jnp.int32, (GROUP_ROWS, 1), 0) // BLOCK
    d_cur = (row % BLOCK) - kj
    d_prev = d_cur + BLOCK
    for kv in range(2):
        slope = jnp.zeros((GROUP_ROWS, BLOCK), F32)
        sink = jnp.zeros((GROUP_ROWS, 1), F32)
        for r in range(Q_GROUP):
            slope = jnp.where(head == r, ALIBI_SLOPES[Q_GROUP * kv + r], slope)
            sink = jnp.where(head_col == r, sink_ref[Q_GROUP * kv + r], sink)
        bias_ref[kv, 0] = jnp.where(d_cur >= 0, -slope * d_cur.astype(F32), NEG_INF)
        bias_ref[kv, 1] = jnp.where(d_prev < BLOCK, -slope * d_prev.astype(F32), NEG_INF)
        sinkcol_ref[kv] = sink


def _stack_heads(t0, t1, lo):
    z = jnp.zeros_like(t0)
    return jnp.concatenate([jnp.where(lo, t0, z), jnp.where(lo, z, t0), jnp.where(lo, t1, z), jnp.where(lo, z, t1)], axis=0)


def _unstack_heads(x4, lo):
    return (jnp.where(lo, x4[0:BLOCK], x4[BLOCK:2 * BLOCK]), jnp.where(lo, x4[2 * BLOCK:3 * BLOCK], x4[3 * BLOCK:]))


def _att_probs(q4, k2c, k2p, bias_c, bias_p, sink, has_prev):
    s_c = _dot_nt(q4, k2c) * ATT_SCALE + bias_c
    s_p = jnp.where(has_prev, _dot_nt(q4, k2p) * ATT_SCALE + bias_p, NEG_INF)
    m = jnp.maximum(jnp.max(jnp.maximum(s_c, s_p), axis=-1, keepdims=True), sink)
    e_c = jnp.exp(s_c - m)
    e_p = jnp.exp(s_p - m)
    e_s = jnp.exp(sink - m)
    inv = 1.0 / (jnp.sum(e_c + e_p, axis=-1, keepdims=True) + e_s)
    return e_c * inv, e_p * inv, e_s * inv


def _attention_fwd(proj, q_gain, k_gain, sinks, *, n_seq, seq, name):
    T = n_seq * seq
    nb = seq // BLOCK
    qcol, kvcol = COL_QKV // ATT_WIDTH, (COL_QKV + ATT_WIDTH) // (2 * KV_WIDTH)

    def body(q_ref, kv_ref, qg_ref, kg_ref, sink_ref, y_ref, bias_ref, sinkcol_ref):
        lo = _lo_mask((BLOCK, LANES))
        qg, kg = qg_ref[...], kg_ref[...]
        _att_consts(sink_ref, bias_ref, sinkcol_ref)

        def block(i, carry):
            r0 = pl.multiple_of(i * BLOCK, BLOCK)
            rp = pl.multiple_of(jnp.maximum(i - 1, 0) * BLOCK, BLOCK)
            has_prev = i > 0
            kn_c = _head_norm(kv_ref[pl.ds(r0, BLOCK), 0:KV_WIDTH].astype(F32), kg, lo)[0].astype(BF16)
            kn_p = _head_norm(kv_ref[pl.ds(rp, BLOCK), 0:KV_WIDTH].astype(F32), kg, lo)[0].astype(BF16)
            v_c = kv_ref[pl.ds(r0, BLOCK), KV_WIDTH:2 * KV_WIDTH].astype(BF16)
            v_p = kv_ref[pl.ds(rp, BLOCK), KV_WIDTH:2 * KV_WIDTH].astype(BF16)
            for kv in range(2):
                k2c, k2p = _dup_half(kn_c, kv, lo), _dup_half(kn_p, kv, lo)
                v2c, v2p = _dup_half(v_c, kv, lo), _dup_half(v_p, kv, lo)
                cols = [slice((2 * kv + t) * LANES, (2 * kv + t + 1) * LANES) for t in range(2)]
                qn = [_head_norm(q_ref[pl.ds(r0, BLOCK), c].astype(F32), qg, lo)[0] for c in cols]
                q4 = _stack_heads(qn[0], qn[1], lo).astype(BF16)
                p_c, p_p, _ = _att_probs(q4, k2c, k2p, bias_ref[kv, 0], bias_ref[kv, 1], sinkcol_ref[kv], has_prev)
                o4 = _dot_nn(p_c.astype(BF16), v2c) + _dot_nn(p_p.astype(BF16), v2p)
                for c, out in zip(cols, _unstack_heads(o4, lo)):
                    y_ref[pl.ds(r0, BLOCK), c] = out.astype(BF16)
            return carry

        lax.fori_loop(0, nb, block, 0)

    vec = pl.BlockSpec((1, LANES), lambda b: (0, 0))
    return pl.pallas_call(
        body, name=name, grid=(n_seq,),
        in_specs=[pl.BlockSpec((seq, ATT_WIDTH), lambda b: (b, qcol)),
                  pl.BlockSpec((seq, 2 * KV_WIDTH), lambda b: (b, kvcol)),
                  vec, vec, pl.BlockSpec(memory_space=pltpu.SMEM)],
        out_specs=pl.BlockSpec((seq, ATT_WIDTH), lambda b: (b, 0)),
        out_shape=jax.ShapeDtypeStruct((T, ATT_WIDTH), BF16),
        scratch_shapes=list(ATT_SCRATCH),
        compiler_params=_params(("parallel",)),
    )(proj, proj, jnp.tile(q_gain, 2).reshape(1, LANES), jnp.tile(k_gain, 2).reshape(1, LANES), sinks)


def _attention_bwd(proj, dy, q_gain, k_gain, sinks, *, n_seq, seq, name, deps=()):
    T = n_seq * seq
    nb = seq // BLOCK
    qcol, kvcol = COL_QKV // ATT_WIDTH, (COL_QKV + ATT_WIDTH) // (2 * KV_WIDTH)

    def body(q_ref, kv_ref, dy_ref, qg_ref, kg_ref, sink_ref, dqkv_ref, dqg_ref, dkg_ref, dsink_ref,
             dkn_acc, dv_acc, qg_acc, kg_acc, sink_acc, bias_ref, sinkcol_ref):
        lo = _lo_mask((BLOCK, LANES))
        qg, kg = qg_ref[...], kg_ref[...]
        _att_consts(sink_ref, bias_ref, sinkcol_ref)
        first = pl.program_id(0) == 0

        @pl.when(first)
        def _():
            qg_acc[...] = jnp.zeros_like(qg_acc)
            kg_acc[...] = jnp.zeros_like(kg_acc)
            sink_acc[...] = jnp.zeros_like(sink_acc)

        dkn_acc[...] = jnp.zeros_like(dkn_acc)
        dv_acc[...] = jnp.zeros_like(dv_acc)

        def block(i, carry):
            r0 = pl.multiple_of(i * BLOCK, BLOCK)
            rp = pl.multiple_of(jnp.maximum(i - 1, 0) * BLOCK, BLOCK)
            has_prev = i > 0
            kn_c = _head_norm(kv_ref[pl.ds(r0, BLOCK), 0:KV_WIDTH].astype(F32), kg, lo)[0].astype(BF16)
            kn_p = _head_norm(kv_ref[pl.ds(rp, BLOCK), 0:KV_WIDTH].astype(F32), kg, lo)[0].astype(BF16)
            v_c = kv_ref[pl.ds(r0, BLOCK), KV_WIDTH:2 * KV_WIDTH].astype(BF16)
            v_p = kv_ref[pl.ds(rp, BLOCK), KV_WIDTH:2 * KV_WIDTH].astype(BF16)
            dk_c, dk_p, dv_c, dv_p = [], [], [], []
            for kv in range(2):
                k2c, k2p = _dup_half(kn_c, kv, lo), _dup_half(kn_p, kv, lo)
                v2c, v2p = _dup_half(v_c, kv, lo), _dup_half(v_p, kv, lo)
                cols = [slice((2 * kv + t) * LANES, (2 * kv + t + 1) * LANES) for t in range(2)]
                normed = [_head_norm(q_ref[pl.ds(r0, BLOCK), c].astype(F32), qg, lo) for c in cols]
                q4 = _stack_heads(normed[0][0], normed[1][0], lo).astype(BF16)
                do4 = _stack_heads(dy_ref[pl.ds(r0, BLOCK), cols[0]], dy_ref[pl.ds(r0, BLOCK), cols[1]], lo)
                p_c, p_p, p_s = _att_probs(q4, k2c, k2p, bias_ref[kv, 0], bias_ref[kv, 1], sinkcol_ref[kv], has_prev)
                dp_c = _dot_nt(do4, v2c)
                dp_p = _dot_nt(do4, v2p)
                delta = jnp.sum(p_c * dp_c + p_p * dp_p, axis=-1, keepdims=True)
                ds_c = (p_c * (dp_c - delta)).astype(BF16)
                ds_p = (p_p * (dp_p - delta)).astype(BF16)
                sink_acc[kv] += -(p_s * delta)
                dq4 = (_dot_nn(ds_c, k2c) + _dot_nn(ds_p, k2p)) * ATT_SCALE
                for c, (_, qh, qr), dqn in zip(cols, normed, _unstack_heads(dq4, lo)):
                    dq, dg = _head_norm_bwd(qh, qr, qg, dqn, lo)
                    dqkv_ref[pl.ds(r0, BLOCK), c] = dq.astype(BF16)
                    qg_acc[...] += dg
                dk_c.append(_dot_tn(ds_c, q4))
                dk_p.append(_dot_tn(ds_p, q4))
                dv_c.append(_dot_tn(p_c.astype(BF16), do4))
                dv_p.append(_dot_tn(p_p.astype(BF16), do4))

            def fold(parts):
                a = parts[0] + pltpu.roll(parts[0], LANES // 2, axis=1)
                b = parts[1] + pltpu.roll(parts[1], LANES // 2, axis=1)
                return jnp.where(lo, a, b)

            dkn_acc[pl.ds(r0, BLOCK), :] += fold(dk_c) * ATT_SCALE
            dkn_acc[pl.ds(rp, BLOCK), :] += fold(dk_p) * ATT_SCALE
            dv_acc[pl.ds(r0, BLOCK), :] += fold(dv_c)
            dv_acc[pl.ds(rp, BLOCK), :] += fold(dv_p)
            return carry

        lax.fori_loop(0, nb, block, 0)

        def finish(i, carry):
            r0 = pl.multiple_of(i * BLOCK, BLOCK)
            _, kh, kr = _head_norm(kv_ref[pl.ds(r0, BLOCK), 0:KV_WIDTH].astype(F32), kg, lo)
            dk, dg = _head_norm_bwd(kh, kr, kg, dkn_acc[pl.ds(r0, BLOCK), :], lo)
            dqkv_ref[pl.ds(r0, BLOCK), ATT_WIDTH:ATT_WIDTH + KV_WIDTH] = dk.astype(BF16)
            dqkv_ref[pl.ds(r0, BLOCK), ATT_WIDTH + KV_WIDTH:QKV_WIDTH] = dv_acc[pl.ds(r0, BLOCK), :].astype(BF16)
            kg_acc[...] += dg
            return carry

        lax.fori_loop(0, nb, finish, 0)

        @pl.when(pl.program_id(0) == n_seq - 1)
        def _():
            dqg_ref[...] = jnp.sum(qg_acc[...], axis=0, keepdims=True)
            dkg_ref[...] = jnp.sum(kg_acc[...], axis=0, keepdims=True)
            lane = lax.broadcasted_iota(jnp.int32, (1, LANES), 1)
            dsink = jnp.zeros((1, LANES), F32)
            for kv in range(2):
                for r in range(Q_GROUP):
                    total = jnp.sum(sink_acc[kv, r * BLOCK:(r + 1) * BLOCK, :], axis=0, keepdims=True)
                    dsink = jnp.where(lane == Q_GROUP * kv + r, total, dsink)
            dsink_ref[...] = dsink

    vec = pl.BlockSpec((1, LANES), lambda b: (0, 0))
    acc = pltpu.VMEM((BLOCK, LANES), F32)
    body, dep_specs, dep_args = _with_deps(body, 6, deps)
    dqkv, dqg, dkg, dsink = pl.pallas_call(
        body, name=name, grid=(n_seq,),
        in_specs=[pl.BlockSpec((seq, ATT_WIDTH), lambda b: (b, qcol)),
                  pl.BlockSpec((seq, 2 * KV_WIDTH), lambda b: (b, kvcol)),
                  pl.BlockSpec((seq, ATT_WIDTH), lambda b: (b, 0)),
                  vec, vec, pl.BlockSpec(memory_space=pltpu.SMEM)] + dep_specs,
        out_specs=[pl.BlockSpec((seq, QKV_WIDTH), lambda b: (b, 0)), vec, vec, vec],
        out_shape=[jax.ShapeDtypeStruct((T, QKV_WIDTH), BF16)] + [jax.ShapeDtypeStruct((1, LANES), F32)] * 3,
        scratch_shapes=[pltpu.VMEM((seq, KV_WIDTH), F32), pltpu.VMEM((seq, KV_WIDTH), F32), acc, acc,
                        pltpu.VMEM((2, GROUP_ROWS, 1), F32), *ATT_SCRATCH],
        compiler_params=_params(("arbitrary",)),
    )(proj, proj, dy, jnp.tile(q_gain, 2).reshape(1, LANES), jnp.tile(k_gain, 2).reshape(1, LANES), sinks, *dep_args)
    half = LANES // 2
    return dqkv, dqg[0, :half] + dqg[0, half:], dkg[0, :half] + dkg[0, half:], dsink[0, :N_Q_HEADS]


def _sgu_weights(w_ref):
    r = lax.broadcasted_iota(jnp.int32, (BLOCK, BLOCK), 0)
    c = lax.broadcasted_iota(jnp.int32, (BLOCK, BLOCK), 1)
    return [jnp.where(r >= c, w_ref[g], 0.0).astype(BF16) for g in range(SGU_GROUPS)]


def _sgu_fwd(proj, gain, w_s, bias_full, *, n_seq, seq, name):
    T = n_seq * seq
    nc = seq // BLOCK

    def body(suv_ref, g_ref, w_ref, b_ref, y_ref):
        lo = _lo_mask((BLOCK, LANES))
        wm = _sgu_weights(w_ref)
        gain_v = g_ref[...]

        def chunk(c, carry):
            r0 = pl.multiple_of(c * BLOCK, BLOCK)
            gv = _gelu(suv_ref[pl.ds(r0, BLOCK), SGU_WIDTH:2 * SGU_WIDTH].astype(F32))
            r = lax.rsqrt(jnp.mean(gv * gv, axis=-1, keepdims=True) + NORM_EPS)
            vn = (gv * r * gain_v).astype(BF16)
            for p in range(SGU_WIDTH // LANES):
                cols = slice(p * LANES, (p + 1) * LANES)
                vp = vn[:, cols]
                mixed = jnp.where(lo, _dot_nn(wm[2 * p], vp), _dot_nn(wm[2 * p + 1], vp)) + b_ref[:, cols]
                u = _gelu(suv_ref[pl.ds(r0, BLOCK), cols].astype(F32))
                y_ref[pl.ds(r0, BLOCK), cols] = (u * mixed).astype(BF16)
            return carry

        lax.fori_loop(0, nc, chunk, 0)

    return pl.pallas_call(
        body, name=name, grid=(n_seq,),
        in_specs=[pl.BlockSpec((seq, 2 * SGU_WIDTH), lambda b: (b, COL_SUV // (2 * SGU_WIDTH))),
                  pl.BlockSpec((1, SGU_WIDTH), lambda b: (0, 0)),
                  pl.BlockSpec((SGU_GROUPS, BLOCK, BLOCK), lambda b: (0, 0, 0)),
                  pl.BlockSpec((BLOCK, SGU_WIDTH), lambda b: (0, 0))],
        out_specs=pl.BlockSpec((seq, SGU_WIDTH), lambda b: (b, 0)),
        out_shape=jax.ShapeDtypeStruct((T, SGU_WIDTH), BF16),
        compiler_params=_params(("parallel",)),
    )(proj, gain.reshape(1, SGU_WIDTH), w_s, bias_full)


def _sgu_bwd(proj, dy, gain, w_s, bias_full, *, n_seq, seq, name, deps=()):
    T = n_seq * seq
    nc = seq // BLOCK
    n_tiles = SGU_WIDTH // LANES

    def body(suv_ref, dy_ref, g_ref, w_ref, b_ref, dsuv_ref, dg_ref, dw_ref, db_ref, dg_acc, dw_acc, db_acc):
        lo = _lo_mask((BLOCK, LANES))
        hi = jnp.logical_not(lo)
        wm = _sgu_weights(w_ref)
        wmt = [jnp.where(lax.broadcasted_iota(jnp.int32, (BLOCK, BLOCK), 1) >= lax.broadcasted_iota(jnp.int32, (BLOCK, BLOCK), 0),
                         w_ref[g].T, 0.0).astype(BF16) for g in range(SGU_GROUPS)]
        gain_v = g_ref[...]

        @pl.when(pl.program_id(0) == 0)
        def _():
            dg_acc[...] = jnp.zeros_like(dg_acc)
            dw_acc[...] = jnp.zeros_like(dw_acc)
            db_acc[...] = jnp.zeros_like(db_acc)

        def chunk(c, carry):
            r0 = pl.multiple_of(c * BLOCK, BLOCK)
            gv, dgelu_v = _gelu_and_grad(suv_ref[pl.ds(r0, BLOCK), SGU_WIDTH:2 * SGU_WIDTH].astype(F32))
            r = lax.rsqrt(jnp.mean(gv * gv, axis=-1, keepdims=True) + NORM_EPS)
            vh = gv * r
            vn = (vh * gain_v).astype(BF16)
            dvn_tiles = []
            for p in range(n_tiles):
                cols = slice(p * LANES, (p + 1) * LANES)
                vp = vn[:, cols]
                mixed = jnp.where(lo, _dot_nn(wm[2 * p], vp), _dot_nn(wm[2 * p + 1], vp)) + b_ref[:, cols]
                u, dgelu_u = _gelu_and_grad(suv_ref[pl.ds(r0, BLOCK), cols].astype(F32))
                dyv = dy_ref[pl.ds(r0, BLOCK), cols]
                dsuv_ref[pl.ds(r0, BLOCK), cols] = (dyv * mixed * dgelu_u).astype(BF16)
                dm = dyv * u
                db_acc[:, cols] += dm
                dm_bf = dm.astype(BF16)
                dvn_tiles.append(jnp.where(lo, _dot_nn(wmt[2 * p], dm_bf), _dot_nn(wmt[2 * p + 1], dm_bf)))
                dw_acc[2 * p] += _dot_nt(jnp.where(lo, dm, 0.0).astype(BF16), vp)
                dw_acc[2 * p + 1] += _dot_nt(jnp.where(hi, dm, 0.0).astype(BF16), vp)
            dvn = jnp.concatenate(dvn_tiles, axis=1)
            dg_acc[...] += dvn * vh
            dvh = dvn * gain_v
            dgv = r * (dvh - vh * jnp.mean(dvh * vh, axis=-1, keepdims=True))
            dsuv_ref[pl.ds(r0, BLOCK), SGU_WIDTH:2 * SGU_WIDTH] = (dgv * dgelu_v).astype(BF16)
            return carry

        lax.fori_loop(0, nc, chunk, 0)

        @pl.when(pl.program_id(0) == n_seq - 1)
        def _():
            dg_ref[...] = jnp.sum(dg_acc[...], axis=0, keepdims=True)
            r = lax.broadcasted_iota(jnp.int32, (BLOCK, BLOCK), 0)
            c = lax.broadcasted_iota(jnp.int32, (BLOCK, BLOCK), 1)
            for g in range(SGU_GROUPS):
                dw_ref[g] = jnp.where(r >= c, dw_acc[g], 0.0)
            lane = lax.broadcasted_iota(jnp.int32, (BLOCK, LANES), 1)
            out = jnp.zeros((BLOCK, LANES), F32)
            for p in range(n_tiles):
                tile = db_acc[:, p * LANES:(p + 1) * LANES]
                s_lo = jnp.sum(jnp.where(lo, tile, 0.0), axis=-1, keepdims=True)
                s_hi = jnp.sum(jnp.where(hi, tile, 0.0), axis=-1, keepdims=True)
                out = jnp.where(lane == 2 * p, s_lo, out)
                out = jnp.where(lane == 2 * p + 1, s_hi, out)
            db_ref[...] = out

    body, dep_specs, dep_args = _with_deps(body, 5, deps)
    dsuv, dg, dw, db = pl.pallas_call(
        body, name=name, grid=(n_seq,),
        in_specs=[pl.BlockSpec((seq, 2 * SGU_WIDTH), lambda b: (b, COL_SUV // (2 * SGU_WIDTH))),
                  pl.BlockSpec((seq, SGU_WIDTH), lambda b: (b, 0)),
                  pl.BlockSpec((1, SGU_WIDTH), lambda b: (0, 0)),
                  pl.BlockSpec((SGU_GROUPS, BLOCK, BLOCK), lambda b: (0, 0, 0)),
                  pl.BlockSpec((BLOCK, SGU_WIDTH), lambda b: (0, 0))] + dep_specs,
        out_specs=[pl.BlockSpec((seq, 2 * SGU_WIDTH), lambda b: (b, 0)),
                   pl.BlockSpec((1, SGU_WIDTH), lambda b: (0, 0)),
                   pl.BlockSpec((SGU_GROUPS, BLOCK, BLOCK), lambda b: (0, 0, 0)),
                   pl.BlockSpec((BLOCK, LANES), lambda b: (0, 0))],
        out_shape=[jax.ShapeDtypeStruct((T, 2 * SGU_WIDTH), BF16), jax.ShapeDtypeStruct((1, SGU_WIDTH), F32),
                   jax.ShapeDtypeStruct((SGU_GROUPS, BLOCK, BLOCK), F32), jax.ShapeDtypeStruct((BLOCK, LANES), F32)],
        scratch_shapes=[pltpu.VMEM((BLOCK, SGU_WIDTH), F32), pltpu.VMEM((SGU_GROUPS, BLOCK, BLOCK), F32),
                        pltpu.VMEM((BLOCK, SGU_WIDTH), F32)],
        compiler_params=_params(("arbitrary",)),
    )(proj, dy, gain.reshape(1, SGU_WIDTH), w_s, bias_full, *dep_args)
    return dsuv, dg.reshape(SGU_WIDTH), dw, db[:, :SGU_GROUPS].T


def _merge_fwd(y_att, y_sgu, w_oa, w_ob, proj, *, name, tm=1024, tn=512, deps=()):
    T = y_att.shape[0]

    def body(ya_ref, ys_ref, wa_ref, wb_ref, ga_ref, gb_ref, o_ref):
        pa = _dot_nn(ya_ref[...], wa_ref[...])
        pb = _dot_nn(ys_ref[...], wb_ref[...])
        o_ref[...] = (_sigmoid(ga_ref[...].astype(F32)) * pa + _sigmoid(gb_ref[...].astype(F32)) * pb).astype(BF16)

    act = pl.BlockSpec((tm, ATT_WIDTH), lambda i, j: (i, 0))
    wgt = pl.BlockSpec((ATT_WIDTH, tn), lambda i, j: (0, j))
    body, dep_specs, dep_args = _with_deps(body, 6, deps)
    return pl.pallas_call(
        body, name=name, grid=(T // tm, D_MODEL // tn),
        in_specs=[act, act, wgt, wgt,
                  pl.BlockSpec((tm, tn), lambda i, j: (i, j + COL_GA // tn)),
                  pl.BlockSpec((tm, tn), lambda i, j: (i, j + COL_GB // tn))] + dep_specs,
        out_specs=pl.BlockSpec((tm, tn), lambda i, j: (i, j)),
        out_shape=jax.ShapeDtypeStruct((T, D_MODEL), BF16),
        compiler_params=_params(("parallel", "parallel")),
    )(y_att, y_sgu, w_oa, w_ob, proj, proj, *dep_args)


def _merge_bwd(dx1_bf, w_out, y_att, y_sgu, w_oa, w_ob, proj, *, name, tm=1024, tn=512):
    T = y_att.shape[0]

    def body(dx_ref, wo_ref, ya_ref, ys_ref, wa_ref, wb_ref, ga_ref, gb_ref, dpa_ref, dpb_ref, dga_ref, dgb_ref):
        dm = _dot_nt(dx_ref[...], wo_ref[...])
        pa = _dot_nn(ya_ref[...], wa_ref[...])
        pb = _dot_nn(ys_ref[...], wb_ref[...])
        sa = _sigmoid(ga_ref[...].astype(F32))
        sb = _sigmoid(gb_ref[...].astype(F32))
        dpa_ref[...] = (dm * sa).astype(BF16)
        dpb_ref[...] = (dm * sb).astype(BF16)
        dga_ref[...] = (dm * pa * sa * (1.0 - sa)).astype(BF16)
        dgb_ref[...] = (dm * pb * sb * (1.0 - sb)).astype(BF16)

    act = pl.BlockSpec((tm, ATT_WIDTH), lambda i, j: (i, 0))
    wgt = pl.BlockSpec((ATT_WIDTH, tn), lambda i, j: (0, j))
    out = pl.BlockSpec((tm, tn), lambda i, j: (i, j))
    return pl.pallas_call(
        body, name=name, grid=(T // tm, D_MODEL // tn),
        in_specs=[pl.BlockSpec((tm, D_MODEL), lambda i, j: (i, 0)),
                  pl.BlockSpec((tn, D_MODEL), lambda i, j: (j, 0)),
                  act, act, wgt, wgt,
                  pl.BlockSpec((tm, tn), lambda i, j: (i, j + COL_GA // tn)),
                  pl.BlockSpec((tm, tn), lambda i, j: (i, j + COL_GB // tn))],
        out_specs=[out] * 4,
        out_shape=[jax.ShapeDtypeStruct((T, D_MODEL), BF16)] * 4,
        compiler_params=_params(("parallel", "parallel")),
    )(dx1_bf, w_out, y_att, y_sgu, w_oa, w_ob, proj, proj)


CONV_ROWS = 256
CONV_TN = 256
UP_CONV_ROWS = 256


def _shift_rows(cur, prev8, k):
    rolled = pltpu.roll(cur, k, axis=0)
    head = jnp.where(lax.broadcasted_iota(jnp.int32, prev8.shape, 0) < k, pltpu.roll(prev8, k, axis=0), rolled[:SUBLANES])
    return jnp.concatenate([head, rolled[SUBLANES:]], axis=0)


def _shift_rows_up(cur, next8, k):
    n = cur.shape[0]
    rolled = pltpu.roll(cur, n - k, axis=0)
    tail = jnp.where(lax.broadcasted_iota(jnp.int32, next8.shape, 0) >= SUBLANES - k,
                     pltpu.roll(next8, SUBLANES - k, axis=0), rolled[n - SUBLANES:])
    return jnp.concatenate([rolled[:n - SUBLANES], tail], axis=0)


def _up_conv_fwd(h2, w_up_t, cw_g, cw_v, cb_g, cb_v, *, n_seq, seq, name, deps=()):
    T = n_seq * seq
    tn, rows = CONV_TN, UP_CONV_ROWS

    def body(h_ref, ug_ref, uv_ref, wg_ref, wv_ref, bg_ref, bv_ref, a_ref, zg_ref, zv_ref, cg_ref, cv_ref):
        def conv(cur, prev8, w_ref, b_ref):
            z1 = _shift_rows(cur, prev8, 1)
            z2 = _shift_rows(cur, prev8, 2)
            return b_ref[...] + w_ref[0:1, :] * z2 + w_ref[1:2, :] * z1 + w_ref[2:3, :] * cur

        start = jnp.zeros((SUBLANES, tn), F32)
        prev = (start, start)
        for s in range(seq // rows):
            r = pl.ds(s * rows, rows)
            h = h_ref[r, :]
            zg = _dot_nt(h, ug_ref[...])
            zv = _dot_nt(h, uv_ref[...])
            zg_ref[r, :] = zg.astype(ACT_DTYPE)
            zv_ref[r, :] = zv.astype(ACT_DTYPE)
            g = conv(zg, prev[0], wg_ref, bg_ref)
            v = conv(zv, prev[1], wv_ref, bv_ref)
            a_ref[r, :] = (g * _sigmoid(g) * v).astype(BF16)
            cg_ref[r, :] = g.astype(ACT_DTYPE)
            cv_ref[r, :] = v.astype(ACT_DTYPE)
            prev = (zg[rows - SUBLANES:], zv[rows - SUBLANES:])

    zs = pl.BlockSpec((seq, tn), lambda b, j: (b, j))
    ws = pl.BlockSpec((3, tn), lambda b, j: (0, j))
    bs = pl.BlockSpec((1, tn), lambda b, j: (0, j))
    body, dep_specs, dep_args = _with_deps(body, 7, deps)
    return pl.pallas_call(
        body, name=name, grid=(n_seq, D_FF // tn),
        in_specs=[pl.BlockSpec((seq, D_MODEL), lambda b, j: (b, 0)),
                  pl.BlockSpec((tn, D_MODEL), lambda b, j: (j, 0)),
                  pl.BlockSpec((tn, D_MODEL), lambda b, j: (j + D_FF // tn, 0)), ws, ws, bs, bs] + dep_specs,
        out_specs=[zs] * 5,
        out_shape=[jax.ShapeDtypeStruct((T, D_FF), BF16)] + [jax.ShapeDtypeStruct((T, D_FF), ACT_DTYPE)] * 4,
        compiler_params=_params(("parallel", "parallel")),
    )(h2, w_up_t, w_up_t, cw_g, cw_v, cb_g.reshape(1, D_FF), cb_v.reshape(1, D_FF), *dep_args)


def _conv_bwd(z_g, z_v, c_g, c_v, dx2_bf, w_down, cw_g, cw_v, *, n_seq, seq, name):
    T = n_seq * seq
    tn, rows = CONV_TN, CONV_ROWS
    n_steps = seq // rows

    def body(zg_ref, zv_ref, cg_ref, cv_ref, dx_ref, wd_ref, wg_ref, wv_ref,
             dzg_ref, dzv_ref, dwg_ref, dwv_ref, dbg_ref, dbv_ref, dcg_ref, dcv_ref):
        def colsum(x):
            return jnp.sum(x, axis=0, keepdims=True)

        zero = jnp.zeros((1, tn), F32)
        db = (zero, zero)
        for s in range(n_steps):
            r = pl.ds(s * rows, rows)
            g = cg_ref[r, :].astype(F32)
            v = cv_ref[r, :].astype(F32)
            sg = _sigmoid(g)
            dav = _dot_nt(dx_ref[r, :], wd_ref[...])
            dcg = dav * v * (sg * (1.0 + g * (1.0 - sg)))
            dcv = dav * (g * sg)
            dcg_ref[r, :] = dcg
            dcv_ref[r, :] = dcv
            db = (db[0] + colsum(dcg), db[1] + colsum(dcv))

        def back(s, accs):
            r0 = pl.multiple_of(s * rows, rows)
            last = s == n_steps - 1
            rn = pl.multiple_of(jnp.minimum(r0 + rows, seq - SUBLANES), SUBLANES)
            new = []
            for half, (dc_ref, w_ref, dz_ref, z_ref) in enumerate(((dcg_ref, wg_ref, dzg_ref, zg_ref),
                                                                   (dcv_ref, wv_ref, dzv_ref, zv_ref))):
                cur = dc_ref[pl.ds(r0, rows), :]
                nxt = jnp.where(last, 0.0, dc_ref[pl.ds(rn, SUBLANES), :])
                u1, u2 = _shift_rows_up(cur, nxt, 1), _shift_rows_up(cur, nxt, 2)
                dz_ref[pl.ds(r0, rows), :] = (w_ref[2:3, :] * cur + w_ref[1:2, :] * u1 + w_ref[0:1, :] * u2).astype(BF16)
                z = z_ref[pl.ds(r0, rows), :].astype(F32)
                new += [accs[3 * half] + colsum(u2 * z), accs[3 * half + 1] + colsum(u1 * z),
                        accs[3 * half + 2] + colsum(cur * z)]
            return tuple(new)

        dw = lax.fori_loop(0, n_steps, back, (zero,) * 6)
        first_seq = pl.program_id(1) == 0

        @pl.when(first_seq)
        def _():
            dwg_ref[...] = jnp.concatenate(dw[0:3], axis=0)
            dwv_ref[...] = jnp.concatenate(dw[3:6], axis=0)
            dbg_ref[...], dbv_ref[...] = db

        @pl.when(jnp.logical_not(first_seq))
        def _():
            dwg_ref[...] += jnp.concatenate(dw[0:3], axis=0)
            dwv_ref[...] += jnp.concatenate(dw[3:6], axis=0)
            dbg_ref[...] += db[0]
            dbv_ref[...] += db[1]

    zs = pl.BlockSpec((seq, tn), lambda j, b: (b, j))
    ws = pl.BlockSpec((3, tn), lambda j, b: (0, j))
    bs = pl.BlockSpec((1, tn), lambda j, b: (0, j))
    outs = pl.pallas_call(
        body, name=name, grid=(D_FF // tn, n_seq),
        in_specs=[zs] * 4 + [pl.BlockSpec((seq, D_MODEL), lambda j, b: (b, 0)),
                             pl.BlockSpec((tn, D_MODEL), lambda j, b: (j, 0)), ws, ws],
        out_specs=[zs, zs, ws, ws, bs, bs],
        out_shape=[jax.ShapeDtypeStruct((T, D_FF), BF16)] * 2 + [jax.ShapeDtypeStruct((3, D_FF), F32)] * 2
        + [jax.ShapeDtypeStruct((1, D_FF), F32)] * 2,
        scratch_shapes=[pltpu.VMEM((seq, tn), F32), pltpu.VMEM((seq, tn), F32)],
        compiler_params=_params(("parallel", "arbitrary")),
    )(z_g, z_v, c_g, c_v, dx2_bf, w_down, cw_g, cw_v)
    dz_g, dz_v, dw_g, dw_v, db_g, db_v = outs
    return dz_g, dz_v, dw_g, dw_v, db_g.reshape(D_FF), db_v.reshape(D_FF)


def _layer_fwd(x, h, w, sched, tail, *, n_seq, seq, l):
    tag = f"l{l}"
    deps = sched("fwd_start", l, h)
    proj = _mm(h, w["w_in_t"], mode="nt", out_dtype=ACT_DTYPE, rotate=W_IN_ROTATE, name=f"{tag}_proj", deps=deps)
    y_att = _attention_fwd(proj, w["q_norm"], w["k_norm"], w["sinks"], n_seq=n_seq, seq=seq, name=f"{tag}_att")
    deps = sched("fwd_att", l, y_att)
    y_sgu = _sgu_fwd(proj, w["sgu_norm"], w["w_s"], w["bias_full"], n_seq=n_seq, seq=seq, name=f"{tag}_sgu")
    merged = _merge_fwd(y_att, y_sgu, w["w_oa"], w["w_ob"], proj, name=f"{tag}_merge", deps=deps)
    x1, h2 = _mm_rows(merged, w["w_out"], mode="nn", fn=_residual_then_norm, out_dtypes=(F32, BF16), rows=(x,),
                      vecs=(w["ffn_norm"],), name=f"{tag}_out")
    deps = sched("fwd_mixer_done", l, x1)
    a, z_g, z_v, c_g, c_v = _up_conv_fwd(h2, w["w_up_t"], w["cw_g"], w["cw_v"], w["cb_g"], w["cb_v"], n_seq=n_seq,
                                         seq=seq, name=f"{tag}_up_conv", deps=deps)
    deps = sched("fwd_conv", l, a)
    if tail[0] == "norm":
        out = _mm_rows(a, w["w_down"], mode="nn", fn=_residual_then_norm, out_dtypes=(F32, BF16), rows=(x1,),
                       vecs=(tail[1],), name=f"{tag}_down", deps=deps)
    else:
        out = _mm_rows(a, w["w_down"], mode="nn", fn=_residual_then_loss, out_dtypes=(F32, BF16), rows=(x1, tail[1]),
                       reduce=True, name=f"{tag}_down", deps=deps)
    saved = dict(x=x, h=h, proj=proj, y_att=y_att, y_sgu=y_sgu, merged=merged, x1=x1, h2=h2, z_g=z_g, z_v=z_v,
                 c_g=c_g, c_v=c_v, a=a)
    return out, saved


def _layer_bwd(dx2, dx2_bf, w, s, sched, deps, *, n_seq, seq, l):
    tag = f"l{l}b"
    g = {}
    g["w_down"] = _mm(s["a"], dx2_bf, mode="tn", out_dtype=F32, name=f"{tag}_dw_down", deps=deps)
    dz_g, dz_v, g["cw_g"], g["cw_v"], g["cb_g"], g["cb_v"] = _conv_bwd(
        s["z_g"], s["z_v"], s["c_g"], s["c_v"], dx2_bf, w["w_down"], w["cw_g"], w["cw_v"], n_seq=n_seq, seq=seq,
        name=f"{tag}_conv")
    dw_up_t = _mm(dz_g, s["h2"], mode="tn", out_dtype=F32, out_rows=(0, 2 * D_FF), name=f"{tag}_dw_up_g")
    g["w_up_t"] = _mm(dz_v, s["h2"], mode="tn", out_dtype=F32, out_rows=(D_FF, 2 * D_FF), out_prev=dw_up_t,
                      name=f"{tag}_dw_up_v")
    deps = sched("bwd_ffn_grads", l, dz_v, g)
    dx1, dx1_bf, dgain = _mm_rows((dz_g, dz_v), w["w_up_t"], mode="nn", fn=_rms_bwd_rows, out_dtypes=(F32, BF16),
                                  rows=(s["x1"], dx2), vecs=(w["ffn_norm"],), reduce=True, a_at=(0, D_FF),
                                  name=f"{tag}_dh2", deps=deps)
    g["ffn_norm"] = dgain.reshape(D_MODEL)
    dpa, dpb, dga, dgb = _merge_bwd(dx1_bf, w["w_out"], s["y_att"], s["y_sgu"], w["w_oa"], w["w_ob"], s["proj"],
                                    name=f"{tag}_merge")
    deps = sched("bwd_merge", l, dpa)
    g["w_out"] = _mm(s["merged"], dx1_bf, mode="tn", out_dtype=F32, name=f"{tag}_dw_out",
                     deps=deps)
    dy_att = _mm(dpa, w["w_oa"], mode="nt", out_dtype=BF16, name=f"{tag}_dy_att")
    dy_sgu = _mm(dpb, w["w_ob"], mode="nt", out_dtype=F32, name=f"{tag}_dy_sgu")
    g["w_oa"] = _mm(s["y_att"], dpa, mode="tn", out_dtype=F32, name=f"{tag}_dw_oa")
    g["w_ob"] = _mm(s["y_sgu"], dpb, mode="tn", out_dtype=F32, name=f"{tag}_dw_ob")
    deps = sched("bwd_out_grads", l, dy_att, g)
    dqkv, g["q_norm"], g["k_norm"], g["sinks"] = _attention_bwd(
        s["proj"], dy_att, w["q_norm"], w["k_norm"], w["sinks"], n_seq=n_seq, seq=seq, name=f"{tag}_att", deps=deps)
    deps = sched("bwd_att", l, dqkv)
    dsuv, g["sgu_norm"], g["w_s"], g["b_s"] = _sgu_bwd(
        s["proj"], dy_sgu, w["sgu_norm"], w["w_s"], w["bias_full"], n_seq=n_seq, seq=seq, name=f"{tag}_sgu", deps=deps)
    dproj = (dsuv, dga, dgb, dqkv)
    at = (QKV_WIDTH, QKV_WIDTH + 2 * SGU_WIDTH, QKV_WIDTH + 2 * SGU_WIDTH + D_MODEL, 0)
    g["w_in_t"] = _mm_tn_parts(dproj, at, s["h"], name=f"{tag}_dw_in")
    deps = sched("bwd_w_in_grad", l, dqkv, g)
    dx, dx_bf, dgain = _mm_rows(dproj, w["w_in_t"], mode="nn", fn=_rms_bwd_rows, out_dtypes=(F32, BF16),
                                rows=(s["x"], dx1), vecs=(w["mix_norm"],), reduce=True, a_at=at,
                                name=f"{tag}_dh", deps=deps)
    g["mix_norm"] = dgain.reshape(D_MODEL)
    return dx, dx_bf, g, sched("bwd_dh", l, dx)


def _local_step(x, target, weights, sched, *, n_seq, seq):
    depth = len(weights)
    saved = []
    h = _rms_fwd(x, weights[0]["mix_norm"], name="l0_mix_norm", deps=sched("begin", 0, x))
    for l in range(depth):
        tail = ("norm", weights[l + 1]["mix_norm"]) if l + 1 < depth else ("loss", target)
        out, s = _layer_fwd(x, h, weights[l], sched, tail, n_seq=n_seq, seq=seq, l=l)
        saved.append(s)
        if l + 1 < depth:
            x, h = out
    dy, dy_bf, loss_cols = out
    grads = [None] * depth
    deps = ()
    for l in reversed(range(depth)):
        dy, dy_bf, grads[l], deps = _layer_bwd(dy, dy_bf, weights[l], saved[l], sched, deps, n_seq=n_seq, seq=seq, l=l)
    return jnp.sum(loss_cols), dy, grads, deps


W_IN_SHARD = IN_WIDTH // N_DEV
W_UP_SHARD = 2 * D_FF // N_DEV
COL_MOVE_ROWS = 256


def _w_o_moves():
    return tuple((j, 0, LANES, 0, j * LANES) for j in range(N_DEV))


def _disassemble(mats, w, moves, *, name):
    R = mats[0].shape[0]
    tr = min(R, COL_MOVE_ROWS)
    n = len(mats)

    def body(*refs):
        m_refs, o_ref = refs[:n], refs[n]
        for j, lo, hi, which, at in moves:
            o_ref[j, :, lo:hi] = m_refs[which][:, at:at + hi - lo]

    return pl.pallas_call(
        body, name=name, grid=(R // tr,),
        in_specs=[pl.BlockSpec((tr, m.shape[1]), lambda i: (i, 0)) for m in mats],
        out_specs=pl.BlockSpec((N_DEV, tr, w), lambda i: (0, i, 0)),
        out_shape=jax.ShapeDtypeStruct((N_DEV, R, w), mats[0].dtype),
        compiler_params=_params(("parallel",)),
    )(*mats)


def _my_place():
    return lax.axis_index("x"), lax.axis_index("y"), lax.axis_index("c")


def _gathered_shape(shape, kind):
    r, c = shape
    return {"blocks": (N_DEV, r, c), "rows": (N_DEV * r, c), "cols": (r, N_DEV * c)}[kind]


def _gather_window(ref, kind, shape, j):
    r, c = shape
    if kind == "blocks":
        return ref.at[j]
    if kind == "rows":
        return ref.at[pl.ds(pl.multiple_of(j * r, r), r), :]
    return ref.at[:, pl.ds(pl.multiple_of(j * c, c), c)]


def _gather(srcs, kinds, *, name):
    n = len(srcs)
    shapes = [s.shape for s in srcs]
    per = 7

    def body(*refs):
        src_refs, dst_refs = refs[:n], refs[n:2 * n]
        send_sems, recv_sems, local_sems = refs[2 * n:]
        x, y, c = _my_place()
        me, sibling = (x, y, c), (x, y, 1 - c)
        chips = [(1 - x, y), (x, 1 - y), (1 - x, 1 - y)]

        def at(i, px, py, pc):
            return _gather_window(dst_refs[i], kinds[i], shapes[i], 4 * px + 2 * py + pc)

        def copy(i, k, block, to, src=None):
            return pltpu.make_async_remote_copy(
                src_ref=at(i, *block) if src is None else src, dst_ref=at(i, *block),
                send_sem=send_sems.at[per * i + k], recv_sem=recv_sems.at[per * i + k], device_id=to, device_id_type=MESH)

        mine = [pltpu.make_async_copy(src_refs[i], at(i, *me), local_sems.at[i]) for i in range(n)]
        for cp in mine:
            cp.start()
        started = []
        for i in range(n):
            first = [copy(i, 0, me, sibling, src=src_refs[i])]
            first += [copy(i, 1 + j, me, (*chip, c), src=src_refs[i]) for j, chip in enumerate(chips)]
            for cp in first:
                cp.start()
            started += first
        for i in range(n):
            for j, chip in enumerate(chips):
                copy(i, 1 + j, (*chip, c), me).wait_recv()
                fwd = copy(i, 4 + j, (*chip, c), sibling)
                fwd.start()
                started.append(fwd)
        for i in range(n):
            copy(i, 0, sibling, me).wait_recv()
            for j, chip in enumerate(chips):
                copy(i, 4 + j, (*chip, 1 - c), me).wait_recv()
        for cp in started:
            cp.wait_send()
        for cp in mine:
            cp.wait()

    return pl.pallas_call(
        body, name=name,
        out_shape=[jax.ShapeDtypeStruct(_gathered_shape(s.shape, k), s.dtype) for s, k in zip(srcs, kinds)],
        in_specs=[ANY] * n, out_specs=[ANY] * n,
        scratch_shapes=[pltpu.SemaphoreType.DMA((per * n,)), pltpu.SemaphoreType.DMA((per * n,)),
                        pltpu.SemaphoreType.DMA((n,))],
    )(*srcs)


HBM = pl.BlockSpec(memory_space=pltpu.HBM)
SEM = pl.BlockSpec(memory_space=pltpu.SEMAPHORE)
TOKEN = jax.ShapeDtypeStruct((SUBLANES, LANES), F32)
TOKEN_SPEC = pl.BlockSpec(memory_space=pltpu.VMEM)
SPLIT_PARAMS = pltpu.CompilerParams(has_side_effects=pltpu.SideEffectType.DATAFLOW_SIDE_EFFECTING)


def _in_hbm(x):
    return pltpu.with_memory_space_constraint(x, pltpu.HBM)


def _hbm_like(shape, dtype):
    return pltpu.HBM(shape, dtype)


def _place_own(shards, kinds, dtypes, *, name, deps=()):
    n = len(shards)
    shapes = [s.shape for s in shards]

    def body(*refs):
        s_refs, land_refs, bufs, sems = refs[:n], refs[n:2 * n], refs[2 * n:3 * n], refs[3 * n]
        x, y, c = _my_place()
        copies = []
        for i in range(n):
            bufs[i][...] = s_refs[i][...].astype(dtypes[i])
            copies.append(pltpu.make_async_copy(
                bufs[i], _gather_window(land_refs[i], kinds[i], shapes[i], 4 * x + 2 * y + c), sems.at[i]))
        for cp in copies:
            cp.start()
        for cp in copies:
            cp.wait()

    body, dep_specs, dep_args = _with_deps(body, n, deps)
    return pl.pallas_call(
        body, name=name,
        out_shape=[jax.ShapeDtypeStruct(_gathered_shape(s, k), d) for s, k, d in zip(shapes, kinds, dtypes)],
        in_specs=[pl.BlockSpec(memory_space=pltpu.VMEM)] * n + dep_specs, out_specs=[ANY] * n,
        scratch_shapes=[pltpu.VMEM(s, d) for s, d in zip(shapes, dtypes)] + [pltpu.SemaphoreType.DMA((n,))],
        compiler_params=_params(),
    )(*shards, *dep_args)


def _gather_start(lands, kinds, shapes, after=(), *, name):
    n = len(lands)
    n_after = len(after)

    def body(*refs):
        land_refs = refs[:n]
        send_sems, recv_sems = refs[n + n_after], refs[n + n_after + 1]
        x, y, c = _my_place()
        targets = [(x, y, 1 - c), (1 - x, y, c), (x, 1 - y, c), (1 - x, 1 - y, c)]
        for i in range(n):
            own = _gather_window(land_refs[i], kinds[i], shapes[i], 4 * x + 2 * y + c)
            for k, to in enumerate(targets):
                pltpu.make_async_remote_copy(
                    src_ref=own, dst_ref=own, send_sem=send_sems.at[4 * i + k], recv_sem=recv_sems.at[4 * i + k],
                    device_id=to, device_id_type=MESH).start()
        refs[-1][...] = jnp.zeros_like(refs[-1])

    outs = pl.pallas_call(
        body, name=name,
        out_shape=[pltpu.SemaphoreType.DMA((4 * n,)), pltpu.SemaphoreType.DMA((4 * n,))]
        + [_hbm_like(a.shape, a.dtype) for a in lands] + [TOKEN],
        in_specs=[HBM] * n + [ANY] * n_after, out_specs=[SEM, SEM] + [HBM] * n + [TOKEN_SPEC],
        input_output_aliases={i: 2 + i for i in range(n)},
        compiler_params=SPLIT_PARAMS,
    )(*[_in_hbm(a) for a in lands], *after)
    return outs[0], outs[1], outs[2:2 + n], outs[-1]


def _gather_forward(recv_sems, lands, kinds, shapes, after, *, name):
    n = len(lands)

    def body(*refs):
        recv_ref, land_refs = refs[0], refs[1:1 + n]
        fwd_send, fwd_recv = refs[2 + n], refs[3 + n]
        token = refs[-1]
        x, y, c = _my_place()
        chips = [(1 - x, y), (x, 1 - y), (1 - x, 1 - y)]
        for i in range(n):
            for j, (px, py) in enumerate(chips):
                block = _gather_window(land_refs[i], kinds[i], shapes[i], 4 * px + 2 * py + c)
                pltpu.make_async_remote_copy(
                    src_ref=block, dst_ref=block, send_sem=fwd_send.at[3 * i + j], recv_sem=recv_ref.at[4 * i + 1 + j],
                    device_id=(px, py, c), device_id_type=MESH).wait_recv()
                pltpu.make_async_remote_copy(
                    src_ref=block, dst_ref=block, send_sem=fwd_send.at[3 * i + j], recv_sem=fwd_recv.at[3 * i + j],
                    device_id=(x, y, 1 - c), device_id_type=MESH).start()
        token[...] = jnp.zeros_like(token)

    outs = pl.pallas_call(
        body, name=name,
        out_shape=[pltpu.SemaphoreType.DMA((3 * n,)), pltpu.SemaphoreType.DMA((3 * n,))]
        + [_hbm_like(a.shape, a.dtype) for a in lands] + [TOKEN],
        in_specs=[SEM] + [HBM] * n + [ANY], out_specs=[SEM, SEM] + [HBM] * n + [TOKEN_SPEC],
        input_output_aliases={1 + i: 2 + i for i in range(n)},
        compiler_params=SPLIT_PARAMS,
    )(recv_sems, *lands, after)
    return outs[0], outs[1], outs[2:2 + n], outs[-1]


def _gather_finish(send_sems, recv_sems, fwd_send, fwd_recv, lands, kinds, shapes, after, *, name):
    n = len(lands)

    def body(*refs):
        send_ref, recv_ref, fsend_ref, frecv_ref = refs[:4]
        land_refs = refs[4:4 + n]
        x, y, c = _my_place()
        chips = [(1 - x, y), (x, 1 - y), (1 - x, 1 - y)]
        sibling = (x, y, 1 - c)
        for i in range(n):
            def window(j):
                return _gather_window(land_refs[i], kinds[i], shapes[i], j)

            mine, theirs = window(4 * x + 2 * y + c), window(4 * x + 2 * y + (1 - c))
            pltpu.make_async_remote_copy(src_ref=mine, dst_ref=theirs, send_sem=send_ref.at[4 * i],
                                         recv_sem=recv_ref.at[4 * i], device_id=sibling, device_id_type=MESH).wait_recv()
            for j, (px, py) in enumerate(chips):
                block = window(4 * px + 2 * py + (1 - c))
                pltpu.make_async_remote_copy(src_ref=block, dst_ref=block, send_sem=fsend_ref.at[3 * i + j],
                                             recv_sem=frecv_ref.at[3 * i + j], device_id=sibling,
                                             device_id_type=MESH).wait_recv()
            for k in range(4):
                pltpu.make_async_remote_copy(src_ref=mine, dst_ref=mine, send_sem=send_ref.at[4 * i + k],
                                             recv_sem=recv_ref.at[4 * i + k], device_id=sibling,
                                             device_id_type=MESH).wait_send()
            for j, (px, py) in enumerate(chips):
                block = window(4 * px + 2 * py + c)
                pltpu.make_async_remote_copy(src_ref=block, dst_ref=block, send_sem=fsend_ref.at[3 * i + j],
                                             recv_sem=frecv_ref.at[3 * i + j], device_id=sibling,
                                             device_id_type=MESH).wait_send()

    return pl.pallas_call(
        body, name=name,
        out_shape=[_hbm_like(a.shape, a.dtype) for a in lands],
        in_specs=[SEM] * 4 + [HBM] * n + [ANY], out_specs=[HBM] * n,
        input_output_aliases={4 + i: i for i in range(n)},
        compiler_params=SPLIT_PARAMS,
    )(send_sems, recv_sems, fwd_send, fwd_recv, *lands, after)


def _pair_plan(src_ref, land_ref, x, y, c):
    return [(src_ref.at[2 * k + (1 - c)], land_ref.at[k], (x, y, 1 - c)) for k in range(N_CHIPS)]


def _chip_plan(src_ref, land_ref, x, y, c):
    chips = [(1 - x, y), (x, 1 - y), (1 - x, 1 - y)]
    return [(src_ref.at[2 * px + py], land_ref.at[k], (px, py, c)) for k, (px, py) in enumerate(chips)]


def _exchange_copies(plan, per, src_refs, land_refs, send_sems, recv_sems):
    x, y, c = _my_place()
    copies = []
    for i, (s_ref, l_ref) in enumerate(zip(src_refs, land_refs)):
        for q, (src, dst, to) in enumerate(plan(s_ref, l_ref, x, y, c)):
            copies.append(pltpu.make_async_remote_copy(
                src_ref=src, dst_ref=dst, send_sem=send_sems.at[per * i + q], recv_sem=recv_sems.at[per * i + q],
                device_id=to, device_id_type=MESH))
    return copies


def _exchange_start(srcs, plan, per, *, name):
    n = len(srcs)

    def body(*refs):
        src_refs, land_refs = refs[:n], refs[n:2 * n]
        send_sems, recv_sems = refs[2 * n], refs[2 * n + 1]
        for cp in _exchange_copies(plan, per, src_refs, land_refs, send_sems, recv_sems):
            cp.start()
        refs[-1][...] = jnp.zeros_like(refs[-1])

    lands = [lax.empty((per,) + s.shape[1:], s.dtype) for s in srcs]
    outs = pl.pallas_call(
        body, name=name,
        out_shape=[pltpu.SemaphoreType.DMA((per * n,)), pltpu.SemaphoreType.DMA((per * n,))]
        + [_hbm_like(s.shape, s.dtype) for s in srcs] + [_hbm_like(a.shape, a.dtype) for a in lands] + [TOKEN],
        in_specs=[HBM] * (2 * n), out_specs=[SEM, SEM] + [HBM] * (2 * n) + [TOKEN_SPEC],
        input_output_aliases={i: 2 + i for i in range(2 * n)},
        compiler_params=SPLIT_PARAMS,
    )(*[_in_hbm(s) for s in srcs], *[_in_hbm(a) for a in lands])
    return outs[0], outs[1], outs[2:2 + n], outs[2 + n:2 + 2 * n], outs[-1]


def _exchange_wait(send_sems, recv_sems, srcs, lands, plan, per, after, *, name):
    n = len(srcs)
    after = list(after) if isinstance(after, (list, tuple)) else [after]

    def body(*refs):
        send_ref, recv_ref = refs[0], refs[1]
        src_refs, land_refs = refs[2:2 + n], refs[2 + n:2 + 2 * n]
        copies = _exchange_copies(plan, per, src_refs, land_refs, send_ref, recv_ref)
        for cp in copies:
            cp.wait_recv()
        for cp in copies:
            cp.wait_send()

    outs = pl.pallas_call(
        body, name=name,
        out_shape=[_hbm_like(s.shape, s.dtype) for s in srcs] + [_hbm_like(a.shape, a.dtype) for a in lands],
        in_specs=[SEM, SEM] + [HBM] * (2 * n) + [ANY] * len(after), out_specs=[HBM] * (2 * n),
        input_output_aliases={2 + i: i for i in range(2 * n)},
        compiler_params=SPLIT_PARAMS,
    )(send_sems, recv_sems, *srcs, *lands, *after)
    return outs[:n], outs[n:]


REDUCE_BLOCK_BYTES = 2 << 20


def _row_tile(r, c):
    row_bytes = 4 * (-(-c // LANES) * LANES)
    best = r
    for d in range(SUBLANES, r, SUBLANES):
        if r % d == 0 and d * row_bytes <= REDUCE_BLOCK_BYTES:
            best = d
    return best if r * row_bytes > REDUCE_BLOCK_BYTES else r


def _reduce_pair_sum(blocked, recv, place, wire_dtype, *, name):
    _, r, c = blocked.shape
    tr = _row_tile(r, c)

    def body(place_ref, g_ref, r_ref, own_ref, send_ref):
        s = g_ref[...] + r_ref[...]
        send_ref[...] = s.astype(wire_dtype)

        @pl.when(pl.program_id(1) == place_ref[1])
        def _():
            own_ref[...] = s

    return pl.pallas_call(
        body, name=name,
        grid_spec=pltpu.PrefetchScalarGridSpec(
            num_scalar_prefetch=1, grid=(r // tr, N_CHIPS),
            in_specs=[pl.BlockSpec((None, None, tr, c), lambda i, k, place_ref: (k, place_ref[0], i, 0)),
                      pl.BlockSpec((None, tr, c), lambda i, k, place_ref: (k, i, 0))],
            out_specs=[pl.BlockSpec((tr, c), lambda i, k, place_ref: (i, 0)),
                       pl.BlockSpec((None, tr, c), lambda i, k, place_ref: (k, i, 0))]),
        out_shape=[jax.ShapeDtypeStruct((r, c), F32), jax.ShapeDtypeStruct((N_CHIPS, r, c), wire_dtype)],
        compiler_params=_params(("parallel", "arbitrary")),
    )(place, blocked.reshape(N_CHIPS, 2, r, c), recv)


def _chip_sum(own_ref, r_ref):
    return ((own_ref[...] + r_ref[0].astype(F32)) + r_ref[1].astype(F32)) + r_ref[2].astype(F32)


def _reduce_chip_sum(own, recv, *, name):
    r, c = own.shape
    tr = _row_tile(r, c)

    def body(own_ref, r_ref, o_ref):
        o_ref[...] = _chip_sum(own_ref, r_ref)

    return pl.pallas_call(
        body, name=name, grid=(r // tr,),
        in_specs=[pl.BlockSpec((tr, c), lambda i: (i, 0)), pl.BlockSpec((N_CHIPS - 1, tr, c), lambda i: (0, i, 0))],
        out_specs=pl.BlockSpec((tr, c), lambda i: (i, 0)),
        out_shape=jax.ShapeDtypeStruct((r, c), F32),
        compiler_params=_params(("parallel",)),
    )(own, recv)


def _adamw_math(w, g, m, v):
    nm = ADAM_B1 * m + (1.0 - ADAM_B1) * g
    nv = ADAM_B2 * v + (1.0 - ADAM_B2) * (g * g)
    m_hat = nm / (1.0 - ADAM_B1 ** ADAM_STEP)
    v_hat = nv / (1.0 - ADAM_B2 ** ADAM_STEP)
    return -ADAM_LR * (m_hat / (jnp.sqrt(v_hat) + ADAM_EPS) + ADAM_WD * w), nm, nv


def _adamw(w, g, m, v, *, name):
    shape = w.shape
    C = shape[-1]
    R = math.prod(shape[:-1])
    tr = _row_tile(R, C)

    def body(w_ref, g_ref, m_ref, v_ref, d_ref, nm_ref, nv_ref):
        d_ref[...], nm_ref[...], nv_ref[...] = _adamw_math(w_ref[...], g_ref[...], m_ref[...], v_ref[...])

    spec = pl.BlockSpec((tr, C), lambda i: (i, 0))
    outs = pl.pallas_call(
        body, name=name, grid=(R // tr,),
        in_specs=[spec] * 4, out_specs=[spec] * 3,
        out_shape=[jax.ShapeDtypeStruct((R, C), F32)] * 3,
        compiler_params=_params(("parallel",)),
    )(*[a.reshape(R, C) for a in (w, g, m, v)])
    return tuple(o.reshape(shape) for o in outs)


def _reduce_adamw(own, recv, w, m, v, layer, prev, *, name):
    r, c = own.shape
    tr = _row_tile(r, c)
    n_prev = 0 if prev is None else len(prev)

    def body(own_ref, r_ref, w_ref, m_ref, v_ref, *rest):
        g_ref, d_ref, nm_ref, nv_ref = rest[n_prev:]
        g = _chip_sum(own_ref, r_ref)
        g_ref[...] = g
        d_ref[...], nm_ref[...], nv_ref[...] = _adamw_math(w_ref[...], g, m_ref[...], v_ref[...])

    slot = pl.BlockSpec((None, tr, c), lambda i: (layer, i, 0))
    return pl.pallas_call(
        body, name=name, grid=(r // tr,),
        in_specs=[pl.BlockSpec((tr, c), lambda i: (i, 0)), pl.BlockSpec((N_CHIPS - 1, tr, c), lambda i: (0, i, 0)),
                  slot, slot, slot] + [ANY] * n_prev,
        out_specs=[slot] * 4,
        out_shape=[jax.ShapeDtypeStruct((DEPTH, r, c), F32)] * 4,
        input_output_aliases={5 + k: k for k in range(n_prev)},
        compiler_params=_params(("parallel",)),
    )(own, recv, w, m, v, *(prev or ()))


REPLICATED = (("mix_norm", (D_MODEL,)), ("q_norm", (HEAD_DIM,)), ("k_norm", (HEAD_DIM,)), ("sinks", (N_Q_HEADS,)),
              ("sgu_norm", (SGU_WIDTH,)), ("w_s", (SGU_GROUPS, BLOCK, BLOCK)), ("b_s", (SGU_GROUPS, BLOCK)),
              ("ffn_norm", (D_MODEL,)), ("conv_b", (2 * D_FF,)))
TRANSPOSED = ("w_in", "w_up")
SHARDED = (("w_in", "rows"), ("w_oa", "cols"), ("w_ob", "cols"), ("w_out", "rows"), ("w_up", "rows"),
           ("conv_w", "blocks"), ("w_down", "rows"))
WEIGHT_ORDER = ("mix_norm", "w_in", "q_norm", "k_norm", "sinks", "sgu_norm", "w_s", "b_s", "w_oa", "w_ob", "w_out",
                "ffn_norm", "w_up", "conv_w", "conv_b", "w_down")
MIXER_WEIGHTS = ["w_in", "w_oa", "w_ob", "w_out"]
FFN_WEIGHTS = ["w_up", "conv_w", "w_down"]


def _small_layout():
    segs, off = {}, 0
    for l in range(DEPTH):
        for name, shape in REPLICATED:
            n = math.prod(shape)
            segs[(l, name)] = (off, n)
            off += n
    per_dev = -(-off // (N_DEV * SUBLANES * LANES)) * SUBLANES * LANES
    return segs, off, per_dev


def _pack_small(grads, loss_part):
    ssegs, total, per_dev = _small_layout()
    flat = jnp.concatenate([grads[l][name].reshape(-1) for (l, name) in ssegs] + [loss_part.reshape(1)])
    return jnp.pad(flat, (0, N_DEV * per_dev - total - 1)).reshape(N_DEV, per_dev // LANES, LANES)


def _unpack_small(gathered):
    ssegs, total, _ = _small_layout()
    flat = gathered.reshape(-1)
    shapes = dict(REPLICATED)
    small = {name: jnp.stack([flat[ssegs[(l, name)][0]:ssegs[(l, name)][0] + ssegs[(l, name)][1]].reshape(shapes[name])
                              for l in range(DEPTH)]) for name, _ in REPLICATED}
    return small, flat[total]


def kernel(x, mix_norm, w_in, q_norm, k_norm, sinks, sgu_norm, w_s, b_s, w_oa, w_ob, w_out, ffn_norm, w_up, conv_w, conv_b, w_down, loss_target, m_mix_norm, m_w_in, m_q_norm, m_k_norm, m_sinks, m_sgu_norm, m_w_s, m_b_s, m_w_oa, m_w_ob, m_w_out, m_ffn_norm, m_w_up, m_conv_w, m_conv_b, m_w_down, v_mix_norm, v_w_in, v_q_norm, v_k_norm, v_sinks, v_sgu_norm, v_w_s, v_b_s, v_w_oa, v_w_ob, v_w_out, v_ffn_norm, v_w_up, v_conv_w, v_conv_b, v_w_down):
    W = dict(mix_norm=mix_norm, w_in=w_in, q_norm=q_norm, k_norm=k_norm, sinks=sinks, sgu_norm=sgu_norm, w_s=w_s, b_s=b_s,
             w_oa=w_oa, w_ob=w_ob, w_out=w_out, ffn_norm=ffn_norm, w_up=w_up, conv_w=conv_w, conv_b=conv_b, w_down=w_down)
    M = dict(mix_norm=m_mix_norm, w_in=m_w_in, q_norm=m_q_norm, k_norm=m_k_norm, sinks=m_sinks, sgu_norm=m_sgu_norm,
             w_s=m_w_s, b_s=m_b_s, w_oa=m_w_oa, w_ob=m_w_ob, w_out=m_w_out, ffn_norm=m_ffn_norm, w_up=m_w_up,
             conv_w=m_conv_w, conv_b=m_conv_b, w_down=m_w_down)
    V = dict(mix_norm=v_mix_norm, w_in=v_w_in, q_norm=v_q_norm, k_norm=v_k_norm, sinks=v_sinks, sgu_norm=v_sgu_norm,
             w_s=v_w_s, b_s=v_b_s, w_oa=v_w_oa, w_ob=v_w_ob, w_out=v_w_out, ffn_norm=v_ffn_norm, w_up=v_w_up,
             conv_w=v_conv_w, conv_b=v_conv_b, w_down=v_w_down)
    n_seq, seq, d_model = x.shape
    tokens = n_seq * seq
    mx, my, mc = _my_place()
    place = jnp.stack([mc, 2 * mx + my]).astype(jnp.int32)
    half = N_DEV // 2
    kind_of = dict(SHARDED)
    for name in TRANSPOSED:
        W[name], M[name], V[name] = (jnp.swapaxes(t[name], 1, 2) for t in (W, M, V))

    gather_groups = [[(0, MIXER_WEIGHTS[0])], [(0, n) for n in MIXER_WEIGHTS[1:]], [(0, n) for n in FFN_WEIGHTS],
                     [(1, n) for n in MIXER_WEIGHTS], [(1, n) for n in FFN_WEIGHTS]]
    started, in_flight = {}, {}
    weights = []
    for l in range(DEPTH):
        w = {name: W[name][l] for name, _ in REPLICATED}
        w["cb_g"], w["cb_v"] = W["conv_b"][l][:D_FF], W["conv_b"][l][D_FF:]
        w["bias_full"] = jnp.repeat(W["b_s"][l].T, SGU_WIDTH // SGU_GROUPS, axis=1)
        weights.append(w)

    def gather_start(gi, after=()):
        shards = [W[name][l] for l, name in gather_groups[gi]]
        kinds = [kind_of[name] for _, name in gather_groups[gi]]
        shapes = [s.shape for s in shards]
        lands = _place_own(shards, kinds, [F32 if name == "conv_w" else BF16 for _, name in gather_groups[gi]],
                           name=f"gather_weights_own_{gi}", deps=after)
        send, recv, lands, token = _gather_start(lands, kinds, shapes, after, name=f"gather_weights_start_{gi}")
        started[gi] = dict(sems=(send, recv), lands=lands, kinds=kinds, shapes=shapes)
        return token

    def gather_forward(gi, after):
        st = started[gi]
        in_flight[gi] = _gather_forward(st["sems"][1], st["lands"], st["kinds"], st["shapes"], after,
                                        name=f"gather_weights_forward_{gi}")
        return in_flight[gi][3]

    def gather_finish(gi, after):
        st = started.pop(gi)
        fwd_send, fwd_recv, lands_g, _ = in_flight.pop(gi)
        whole = _gather_finish(st["sems"][0], st["sems"][1], fwd_send, fwd_recv, lands_g, st["kinds"], st["shapes"], after,
                               name=f"gather_weights_finish_{gi}")
        for (l, name), arr in zip(gather_groups[gi], whole):
            w = weights[l]
            if name in TRANSPOSED:
                w[name + "_t"] = arr
            elif name == "conv_w":
                w["cw_g"] = arr[:half].transpose(1, 0, 2).reshape(3, D_FF)
                w["cw_v"] = arr[half:].transpose(1, 0, 2).reshape(3, D_FF)
            else:
                w[name] = arr

    reduce_state, results = {}, {}
    wire = {"conv_w": F32, "small": F32}

    def reduce_begin(key, names, arrays):
        send, recv, srcs_, lands_, token = _exchange_start(arrays, _pair_plan, N_CHIPS, name=f"reduce_pair_start_{key}")
        reduce_state[key] = dict(names=names, pair=(send, recv, srcs_, lands_))
        return [token]

    def reduce_pair(key, after):
        st = reduce_state[key]
        send, recv, srcs_, lands_ = st.pop("pair")
        blocked_, from_sibling = _exchange_wait(send, recv, srcs_, lands_, _pair_plan, N_CHIPS, after,
                                                name=f"reduce_pair_wait_{key}")
        sums = [_reduce_pair_sum(b, r, place, wire.get(n if isinstance(n, str) else n[1], BF16),
                                 name=f"reduce_pair_sum_{key}_{i}")
                for i, (n, b, r) in enumerate(zip(st["names"], blocked_, from_sibling))]
        st["own"] = [s[0] for s in sums]
        *st["chip"], token = _exchange_start([s[1] for s in sums], _chip_plan, N_CHIPS - 1, name=f"reduce_chip_start_{key}")
        return [token]

    def reduce_end(key, after):
        st = reduce_state.pop(key)
        send, recv, srcs_, lands_ = st["chip"]
        _, from_chips = _exchange_wait(send, recv, srcs_, lands_, _chip_plan, N_CHIPS - 1, after,
                                       name=f"reduce_chip_wait_{key}")
        done = []
        for n, own, got in zip(st["names"], st["own"], from_chips):
            if n == "small":
                results["small"] = _reduce_chip_sum(own, got, name="reduce_chip_sum_small")
            else:
                l, name = n
                results[name] = _reduce_adamw(own, got, W[name], M[name], V[name], l, results.get(name),
                                              name=f"l{l}_reduce_adamw_{name}")
                done.append(results[name][0])
        return done

    def sched(point, l, carry, g=None):
        deps = []
        if point == "begin":
            token = ()
            for gi in range(len(gather_groups)):
                token = [gather_start(gi, token)]
            deps = [gather_forward(0, token[0])]
        elif point == "fwd_start" and l == 0:
            gather_finish(0, carry)
            deps = [gather_forward(1, weights[0]["w_in_t"])]
        elif point == "fwd_att" and l == 0:
            gather_finish(1, carry)
            deps = [gather_forward(2, carry)]
        elif point == "fwd_mixer_done" and l == 0:
            gather_finish(2, carry)
        elif point == "fwd_conv" and l == 0:
            deps = [gather_forward(3, carry)]
        elif point == "fwd_start" and l == 1:
            gather_finish(3, carry)
        elif point == "fwd_att" and l == 1:
            deps = [gather_forward(4, carry)]
        elif point == "fwd_mixer_done" and l == 1:
            gather_finish(4, carry)
        elif point == "bwd_ffn_grads":
            conv_w = jnp.concatenate([g[k].reshape(3, half, W_UP_SHARD).transpose(1, 0, 2) for k in ("cw_g", "cw_v")])
            deps = reduce_begin(
                f"l{l}_ffn", [(l, "w_down"), (l, "w_up"), (l, "conv_w")],
                [g["w_down"].reshape(N_DEV, D_FF // N_DEV, D_MODEL),
                 g["w_up_t"].reshape(N_DEV, W_UP_SHARD, D_MODEL), conv_w])
        elif point == "bwd_merge":
            deps = reduce_pair(f"l{l}_ffn", carry)
        elif point == "bwd_out_grads":
            deps = reduce_begin(
                f"l{l}_out", [(l, "w_out"), (l, "w_oa"), (l, "w_ob")],
                [g["w_out"].reshape(N_DEV, D_MODEL // N_DEV, D_MODEL),
                 _disassemble((g["w_oa"],), LANES, _w_o_moves(), name=f"l{l}_split_dw_oa"),
                 _disassemble((g["w_ob"],), LANES, _w_o_moves(), name=f"l{l}_split_dw_ob")])
        elif point == "bwd_att":
            deps = reduce_pair(f"l{l}_out", carry)
        elif point == "bwd_w_in_grad":
            deps = reduce_begin(f"l{l}_in", [(l, "w_in")], [g["w_in_t"].reshape(N_DEV, W_IN_SHARD, D_MODEL)])
        elif point == "bwd_dh":
            deps = reduce_pair(f"l{l}_in", carry)
        return deps

    loss_part, dx, grads, last_deps = _local_step(x.reshape(tokens, d_model), loss_target.reshape(tokens, d_model),
                                                  weights, sched, n_seq=n_seq, seq=seq)
    for g in grads:
        g["conv_b"] = jnp.concatenate([g["cb_g"], g["cb_v"]])
    after = [dx, *last_deps, *reduce_begin("small", ["small"], [_pack_small(grads, loss_part)])]
    for key in [f"l{l}_{part}" for l in reversed(range(DEPTH)) for part in ("ffn", "out", "in")][:-1]:
        after = reduce_end(key, after)
    after = reduce_end("l0_in", after + reduce_pair("small", after))
    reduce_end("small", after)

    G, delta, new_m, new_v = {}, {}, {}, {}
    for name, _ in SHARDED:
        outs = [jnp.swapaxes(o, 1, 2) for o in results[name]] if name in TRANSPOSED else results[name]
        G[name], delta[name], new_m[name], new_v[name] = outs
    small, loss = _unpack_small(_gather([results["small"]], ["blocks"], name="gather_small_grads")[0])
    G.update(small)
    for name, _ in REPLICATED:
        delta[name], new_m[name], new_v[name] = _adamw(W[name], G[name], M[name], V[name], name=f"adamw_{name}")
    return (loss, dx.reshape(n_seq, seq, d_model), *[G[n] for n in WEIGHT_ORDER], *[delta[n] for n in WEIGHT_ORDER],
            *[new_m[n] for n in WEIGHT_ORDER], *[new_v[n] for n in WEIGHT_ORDER])
```

```python
import math

import jax
import jax.numpy as jnp
from jax import lax
from jax.experimental import pallas as pl
from jax.experimental.pallas import tpu as pltpu

F32 = jnp.float32
BF16 = jnp.bfloat16
ACT_DTYPE = BF16
MESH = pl.DeviceIdType.MESH

DEPTH = 2
D_MODEL = 1024
N_Q_HEADS = 8
HEAD_DIM = 64
ATT_WIDTH = 512
KV_WIDTH = 128
BLOCK = 128
SGU_WIDTH = 512
SGU_GROUPS = 8
IN_WIDTH = 3840
D_FF = 2816
NORM_EPS = 1e-6
NEG_INF = -1e30
ATT_SCALE = HEAD_DIM ** -0.5
ALIBI_SLOPES = tuple(2.0 ** (-(h + 1)) for h in range(N_Q_HEADS))
ADAM_LR, ADAM_B1, ADAM_B2, ADAM_EPS, ADAM_WD, ADAM_STEP = 0.001, 0.9, 0.999, 1e-08, 0.01, 10
N_DEV = 8
N_CHIPS = 4

QKV_WIDTH = ATT_WIDTH + 2 * KV_WIDTH
COL_SUV, COL_GA, COL_GB, COL_QKV = 0, 1024, 2048, 3072
W_IN_ROTATE = (1, IN_WIDTH // QKV_WIDTH)

LANES = 128
SUBLANES = 8
VMEM_LIMIT_V7X = 56 * 1024 * 1024
GELU_C = math.sqrt(2.0 / math.pi)
GELU_K = 0.044715
ANY = pl.BlockSpec(memory_space=pl.ANY)


def _params(sem=None):
    return pltpu.CompilerParams(dimension_semantics=sem, vmem_limit_bytes=VMEM_LIMIT_V7X)


def _sigmoid(x):
    return 1.0 / (1.0 + jnp.exp(-x))


def _gelu(x):
    th = jnp.tanh(GELU_C * (x + GELU_K * x * x * x))
    return 0.5 * x * (1.0 + th)


def _gelu_and_grad(x):
    x2 = x * x
    th = jnp.tanh(GELU_C * (x + GELU_K * x2 * x))
    g = 0.5 * x * (1.0 + th)
    dg = 0.5 * (1.0 + th) + 0.5 * x * (1.0 - th * th) * (GELU_C * (1.0 + 3.0 * GELU_K * x2))
    return g, dg


def _dot(a, b, dims):
    return lax.dot_general(a, b, (dims, ((), ())), preferred_element_type=F32)


def _dot_nn(a, b):
    return _dot(a, b, ((1,), (0,)))


def _dot_nt(a, b):
    return _dot(a, b, ((1,), (1,)))


def _dot_tn(a, b):
    return _dot(a, b, ((0,), (0,)))


def _lo_mask(shape):
    return lax.broadcasted_iota(jnp.int32, shape, len(shape) - 1) < (LANES // 2)


def _half_sums(x, lo):
    s_lo = jnp.sum(jnp.where(lo, x, 0.0), axis=-1, keepdims=True)
    s_all = jnp.sum(x, axis=-1, keepdims=True)
    return jnp.where(lo, s_lo, s_all - s_lo)


def _dup_half(x, half, lo):
    r = pltpu.roll(x, LANES // 2, axis=1)
    return jnp.where(lo, x, r) if half == 0 else jnp.where(lo, r, x)


def _with_deps(body, n_in, deps):
    k = len(deps)
    if not k:
        return body, [], ()

    def skipping(*refs):
        return body(*refs[:n_in], *refs[n_in + k:])

    return skipping, [ANY] * k, tuple(deps)


MM_VMEM_BUDGET = 40 * 1024 * 1024
MM_MAX_TILE = 1408
MM_MAX_TK = 4096
MM_STEP_BYTES = 1 << 20


def _divisors(n, step, cap):
    return [d for d in range(step, min(n, cap) + 1, step) if n % d == 0] or [n]


def _mm_tiles(M, N, K, out_bytes, tm_divides, tn_divides):
    best = None
    for tm in _divisors(M, LANES, MM_MAX_TILE):
        for tn in _divisors(N, LANES, MM_MAX_TILE):
            if tm_divides % tm or tn_divides % tn:
                continue
            for tk in _divisors(K, 4 * LANES, MM_MAX_TK):
                vmem = 4 * (tm * tk + tk * tn) + 2 * tm * tn * out_bytes + (0 if tk == K else 4 * tm * tn)
                if vmem > MM_VMEM_BUDGET:
                    continue
                traffic = 2 * M * K * (N // tn) + 2 * K * N * (M // tm) + M * N * out_bytes
                cost = traffic + (K // tk - 1) * 8 * M * N + (M // tm) * (N // tn) * (K // tk) * MM_STEP_BYTES
                if best is None or cost < best[0]:
                    best = (cost, tm, tn, tk)
    assert best is not None, (M, N, K)
    return best[1:]


def _mm(a, b, *, mode, out_dtype, name, deps=(), b_rows=(0, None), rotate=None, out_rows=(0, None), out_prev=None):
    b_first, b_count = b_rows
    if mode == "nn":
        (M, K), N = a.shape, b.shape[1]
    elif mode == "nt":
        (M, K), N = a.shape, (b.shape[0] if b_count is None else b_count)
    else:
        (K, M), N = a.shape, b.shape[1]
    shift, period = rotate or (0, 1)
    assert period == 1 or mode == "nt"
    out_first, out_total = out_rows[0], (M if out_rows[1] is None else out_rows[1])
    tm, tn, tk = _mm_tiles(M, N, K, jnp.dtype(out_dtype).itemsize, math.gcd(M, out_first),
                           math.gcd(N // period, b_first if mode == "nt" else 0))
    gm, gn, gk = M // tm, N // tn, K // tk

    def turned(j):
        per = N // period // tn
        return ((j // per + shift) % period) * per + j % per if period > 1 else j

    if mode == "nn":
        a_spec = pl.BlockSpec((tm, tk), lambda i, j, k: (i, k))
        b_spec = pl.BlockSpec((tk, tn), lambda i, j, k: (k + b_first // tk, j))
        contract = ((1,), (0,))
    elif mode == "nt":
        a_spec = pl.BlockSpec((tm, tk), lambda i, j, k: (i, k))
        b_spec = pl.BlockSpec((tn, tk), lambda i, j, k: (turned(j) + b_first // tn, k))
        contract = ((1,), (1,))
    else:
        a_spec = pl.BlockSpec((tk, tm), lambda i, j, k: (k, i))
        b_spec = pl.BlockSpec((tk, tn), lambda i, j, k: (k, j))
        contract = ((0,), (0,))
    o_spec = pl.BlockSpec((tm, tn), lambda i, j, k: (i + out_first // tm, j))
    assert b_first % (tk if mode == "nn" else tn) == 0 and out_first % tm == 0, (name, tm, tn, tk)
    n_prev = 0 if out_prev is None else 1

    def body(a_ref, b_ref, *rest):
        o_ref = rest[n_prev]
        part = _dot(a_ref[...].astype(BF16), b_ref[...].astype(BF16), contract)
        if gk == 1:
            o_ref[...] = part.astype(out_dtype)
            return
        acc_ref = rest[n_prev + 1]
        k = pl.program_id(2)

        @pl.when(k == 0)
        def _():
            acc_ref[...] = part

        @pl.when(k > 0)
        def _():
            acc_ref[...] += part

        @pl.when(k == gk - 1)
        def _():
            o_ref[...] = acc_ref[...].astype(out_dtype)

    body, dep_specs, dep_args = _with_deps(body, 2 + n_prev, deps)
    return pl.pallas_call(
        body,
        name=name,
        grid=(gm, gn, gk),
        in_specs=[a_spec, b_spec] + [ANY] * n_prev + dep_specs,
        out_specs=o_spec,
        out_shape=jax.ShapeDtypeStruct((out_total, N), out_dtype),
        input_output_aliases={2: 0} if n_prev else {},
        scratch_shapes=[] if gk == 1 else [pltpu.VMEM((tm, tn), F32)],
        compiler_params=_params(("parallel", "parallel", "arbitrary")),
    )(a, b, *([out_prev] if n_prev else []), *dep_args)


def _mm_tn_parts(parts, at, b, *, name):
    K, N = b.shape
    n = len(parts)
    tm = math.gcd(*[p.shape[1] for p in parts], *at)
    tiles = [p.shape[1] // tm for p in parts]
    first = [sum(tiles[:p]) for p in range(n)]

    def mine(i, p):
        return jnp.logical_and(i >= first[p], i < first[p] + tiles[p])

    def out_tile(i):
        t = 0
        for p in range(n):
            t = jnp.where(mine(i, p), at[p] // tm + i - first[p], t)
        return t

    def body(*refs):
        a_refs, b_ref, o_ref = refs[:n], refs[n], refs[n + 1]
        for p in range(n):
            @pl.when(mine(pl.program_id(0), p))
            def _(p=p):
                o_ref[...] = _dot_tn(a_refs[p][...], b_ref[...])

    return pl.pallas_call(
        body, name=name, grid=(sum(tiles),),
        in_specs=[pl.BlockSpec((K, tm), lambda i, p=p: (0, jnp.clip(i - first[p], 0, tiles[p] - 1))) for p in range(n)]
        + [pl.BlockSpec((K, N), lambda i: (0, 0), pipeline_mode=pl.Buffered(1))],
        out_specs=pl.BlockSpec((tm, N), lambda i: (out_tile(i), 0)),
        out_shape=jax.ShapeDtypeStruct((sum(p.shape[1] for p in parts), N), F32),
        compiler_params=_params(("arbitrary",)),
    )(*parts, b)


def _mm_rows(a, b, *, mode, fn, out_dtypes, rows=(), vecs=(), reduce=False, name, deps=(), b_rows=(0, None), a_at=None):
    parts = a if a_at is not None else (a,)
    starts = a_at if a_at is not None else (0,)
    n_parts = len(parts)
    M, K = parts[0].shape[0], sum(p.shape[1] for p in parts)
    b_first, b_count = b_rows[0], (b.shape[0] if b_rows[1] is None else b_rows[1])
    N = b.shape[1] if mode == "nn" else b_count
    contract = ((1,), (0,)) if mode == "nn" else ((1,), (1,))
    n_rows, n_vecs, n_out = len(rows), len(vecs), len(out_dtypes)
    out_bytes = sum(jnp.dtype(d).itemsize for d in out_dtypes)
    tm = max(t for t in _divisors(M, LANES, MM_MAX_TILE)
             if 4 * t * K + 2 * K * N + 2 * t * N * (4 * n_rows + out_bytes) <= MM_VMEM_BUDGET)
    assert b_first % b_count == 0 and (a_at is None or mode == "nn")

    def body(*refs):
        a_refs, b_ref, rest = refs[:n_parts], refs[n_parts], refs[n_parts + 1:]
        row_refs, vec_refs = rest[:n_rows], rest[n_rows:n_rows + n_vecs]
        out_refs = rest[n_rows + n_vecs:]
        if a_at is None:
            acc = _dot(a_refs[0][...], b_ref[...], contract)
        else:
            acc = sum(_dot(r[...], b_ref[at:at + r.shape[1], :], contract) for r, at in zip(a_refs, starts))
        res = fn(acc, *[r[...] for r in row_refs], *[v[...] for v in vec_refs])
        for o_ref, val in zip(out_refs[:n_out], res):
            o_ref[...] = val.astype(o_ref.dtype)
        if reduce:
            @pl.when(pl.program_id(0) == 0)
            def _():
                out_refs[n_out][...] = res[n_out]

            @pl.when(pl.program_id(0) > 0)
            def _():
                out_refs[n_out][...] += res[n_out]

    row = pl.BlockSpec((tm, N), lambda i: (i, 0))
    vec = pl.BlockSpec((1, N), lambda i: (0, 0))
    body, dep_specs, dep_args = _with_deps(body, n_parts + 1 + n_rows + n_vecs, deps)
    return pl.pallas_call(
        body, name=name, grid=(M // tm,),
        in_specs=[pl.BlockSpec((tm, p.shape[1]), lambda i: (i, 0)) for p in parts]
        + [pl.BlockSpec((b_count, b.shape[1]), lambda i: (b_first // b_count, 0), pipeline_mode=pl.Buffered(1))]
        + [row] * n_rows + [vec] * n_vecs + dep_specs,
        out_specs=[row] * n_out + [vec] * reduce,
        out_shape=[jax.ShapeDtypeStruct((M, N), d) for d in out_dtypes] + [jax.ShapeDtypeStruct((1, N), F32)] * reduce,
        compiler_params=_params(("arbitrary",)),
    )(*parts, b, *rows, *[v.reshape(1, N) for v in vecs], *dep_args)


def _rms(x, gain):
    return x * lax.rsqrt(jnp.mean(x * x, axis=-1, keepdims=True) + NORM_EPS) * gain


def _residual_then_norm(acc, x, gain):
    x_out = x + acc
    return x_out, _rms(x_out, gain)


def _residual_then_loss(acc, x, target):
    err = (x + acc) - target
    dy = err * (1.0 / D_MODEL)
    return dy, dy, jnp.sum(err * err, axis=0, keepdims=True) * (0.5 / D_MODEL)


def _rms_bwd_rows(dh, x, dres, gain):
    r = lax.rsqrt(jnp.mean(x * x, axis=-1, keepdims=True) + NORM_EPS)
    xh = x * r
    dxh = dh * gain
    dx = dres + r * (dxh - xh * jnp.mean(dxh * xh, axis=-1, keepdims=True))
    return dx, dx, jnp.sum(dh * xh, axis=0, keepdims=True)


def _rms_fwd(x, gain, *, name, tm=512, deps=()):
    T, D = x.shape

    def body(x_ref, g_ref, h_ref):
        xv = x_ref[...]
        r = lax.rsqrt(jnp.mean(xv * xv, axis=-1, keepdims=True) + NORM_EPS)
        h_ref[...] = (xv * r * g_ref[...]).astype(BF16)

    body, dep_specs, dep_args = _with_deps(body, 2, deps)
    return pl.pallas_call(
        body, name=name, grid=(T // tm,),
        in_specs=[pl.BlockSpec((tm, D), lambda i: (i, 0)), pl.BlockSpec((1, D), lambda i: (0, 0))] + dep_specs,
        out_specs=pl.BlockSpec((tm, D), lambda i: (i, 0)),
        out_shape=jax.ShapeDtypeStruct((T, D), BF16),
        compiler_params=_params(("parallel",)),
    )(x, gain.reshape(1, D), *dep_args)


def _head_norm(x, gain2, lo):
    ms = _half_sums(x * x, lo) * (1.0 / HEAD_DIM)
    r = lax.rsqrt(ms + NORM_EPS)
    xh = x * r
    return xh * gain2, xh, r


def _head_norm_bwd(xh, r, gain2, dy, lo):
    dxh = dy * gain2
    dx = r * (dxh - xh * (_half_sums(dxh * xh, lo) * (1.0 / HEAD_DIM)))
    return dx, dy * xh


Q_GROUP = N_Q_HEADS // 2
GROUP_ROWS = Q_GROUP * BLOCK
ATT_SCRATCH = (pltpu.VMEM((2, 2, GROUP_ROWS, BLOCK), F32), pltpu.VMEM((2, GROUP_ROWS, 1), F32))


def _att_consts(sink_ref, bias_ref, sinkcol_ref):
    row = lax.broadcasted_iota(jnp.int32, (GROUP_ROWS, BLOCK), 0)
    kj = lax.broadcasted_iota(jnp.int32, (GROUP_ROWS, BLOCK), 1)
    head = row // BLOCK
    head_col = lax.broadcasted_iota(jnp.int32, (GROUP_ROWS, 1), 0) // BLOCK
    d_cur = (row % BLOCK) - kj
    d_prev = d_cur + BLOCK
    for kv in range(2):
        slope = jnp.zeros((GROUP_ROWS, BLOCK), F32)
        sink = jnp.zeros((GROUP_ROWS, 1), F32)
        for r in range(Q_GROUP):
            slope = jnp.where(head == r, ALIBI_SLOPES[Q_GROUP * kv + r], slope)
            sink = jnp.where(head_col == r, sink_ref[Q_GROUP * kv + r], sink)
        bias_ref[kv, 0] = jnp.where(d_cur >= 0, -slope * d_cur.astype(F32), NEG_INF)
        bias_ref[kv, 1] = jnp.where(d_prev < BLOCK, -slope * d_prev.astype(F32), NEG_INF)
        sinkcol_ref[kv] = sink


def _stack_heads(t0, t1, lo):
    z = jnp.zeros_like(t0)
    return jnp.concatenate([jnp.where(lo, t0, z), jnp.where(lo, z, t0), jnp.where(lo, t1, z), jnp.where(lo, z, t1)], axis=0)


def _unstack_heads(x4, lo):
    return (jnp.where(lo, x4[0:BLOCK], x4[BLOCK:2 * BLOCK]), jnp.where(lo, x4[2 * BLOCK:3 * BLOCK], x4[3 * BLOCK:]))


def _att_probs(q4, k2c, k2p, bias_c, bias_p, sink, has_prev):
    s_c = _dot_nt(q4, k2c) * ATT_SCALE + bias_c
    s_p = jnp.where(has_prev, _dot_nt(q4, k2p) * ATT_SCALE + bias_p, NEG_INF)
    m = jnp.maximum(jnp.max(jnp.maximum(s_c, s_p), axis=-1, keepdims=True), sink)
    e_c = jnp.exp(s_c - m)
    e_p = jnp.exp(s_p - m)
    e_s = jnp.exp(sink - m)
    inv = 1.0 / (jnp.sum(e_c + e_p, axis=-1, keepdims=True) + e_s)
    return e_c * inv, e_p * inv, e_s * inv


def _attention_fwd(proj, q_gain, k_gain, sinks, *, n_seq, seq, name):
    T = n_seq * seq
    nb = seq // BLOCK
    qcol, kvcol = COL_QKV // ATT_WIDTH, (COL_QKV + ATT_WIDTH) // (2 * KV_WIDTH)

    def body(q_ref, kv_ref, qg_ref, kg_ref, sink_ref, y_ref, bias_ref, sinkcol_ref):
        lo = _lo_mask((BLOCK, LANES))
        qg, kg = qg_ref[...], kg_ref[...]
        _att_consts(sink_ref, bias_ref, sinkcol_ref)

        def block(i, carry):
            r0 = pl.multiple_of(i * BLOCK, BLOCK)
            rp = pl.multiple_of(jnp.maximum(i - 1, 0) * BLOCK, BLOCK)
            has_prev = i > 0
            kn_c = _head_norm(kv_ref[pl.ds(r0, BLOCK), 0:KV_WIDTH].astype(F32), kg, lo)[0].astype(BF16)
            kn_p = _head_norm(kv_ref[pl.ds(rp, BLOCK), 0:KV_WIDTH].astype(F32), kg, lo)[0].astype(BF16)
            v_c = kv_ref[pl.ds(r0, BLOCK), KV_WIDTH:2 * KV_WIDTH].astype(BF16)
            v_p = kv_ref[pl.ds(rp, BLOCK), KV_WIDTH:2 * KV_WIDTH].astype(BF16)
            for kv in range(2):
                k2c, k2p = _dup_half(kn_c, kv, lo), _dup_half(kn_p, kv, lo)
                v2c, v2p = _dup_half(v_c, kv, lo), _dup_half(v_p, kv, lo)
                cols = [slice((2 * kv + t) * LANES, (2 * kv + t + 1) * LANES) for t in range(2)]
                qn = [_head_norm(q_ref[pl.ds(r0, BLOCK), c].astype(F32), qg, lo)[0] for c in cols]
                q4 = _stack_heads(qn[0], qn[1], lo).astype(BF16)
                p_c, p_p, _ = _att_probs(q4, k2c, k2p, bias_ref[kv, 0], bias_ref[kv, 1], sinkcol_ref[kv], has_prev)
                o4 = _dot_nn(p_c.astype(BF16), v2c) + _dot_nn(p_p.astype(BF16), v2p)
                for c, out in zip(cols, _unstack_heads(o4, lo)):
                    y_ref[pl.ds(r0, BLOCK), c] = out.astype(BF16)
            return carry

        lax.fori_loop(0, nb, block, 0)

    vec = pl.BlockSpec((1, LANES), lambda b: (0, 0))
    return pl.pallas_call(
        body, name=name, grid=(n_seq,),
        in_specs=[pl.BlockSpec((seq, ATT_WIDTH), lambda b: (b, qcol)),
                  pl.BlockSpec((seq, 2 * KV_WIDTH), lambda b: (b, kvcol)),
                  vec, vec, pl.BlockSpec(memory_space=pltpu.SMEM)],
        out_specs=pl.BlockSpec((seq, ATT_WIDTH), lambda b: (b, 0)),
        out_shape=jax.ShapeDtypeStruct((T, ATT_WIDTH), BF16),
        scratch_shapes=list(ATT_SCRATCH),
        compiler_params=_params(("parallel",)),
    )(proj, proj, jnp.tile(q_gain, 2).reshape(1, LANES), jnp.tile(k_gain, 2).reshape(1, LANES), sinks)


def _attention_bwd(proj, dy, q_gain, k_gain, sinks, *, n_seq, seq, name, deps=()):
    T = n_seq * seq
    nb = seq // BLOCK
    qcol, kvcol = COL_QKV // ATT_WIDTH, (COL_QKV + ATT_WIDTH) // (2 * KV_WIDTH)

    def body(q_ref, kv_ref, dy_ref, qg_ref, kg_ref, sink_ref, dqkv_ref, dqg_ref, dkg_ref, dsink_ref,
             dkn_acc, dv_acc, qg_acc, kg_acc, sink_acc, bias_ref, sinkcol_ref):
        lo = _lo_mask((BLOCK, LANES))
        qg, kg = qg_ref[...], kg_ref[...]
        _att_consts(sink_ref, bias_ref, sinkcol_ref)
        first = pl.program_id(0) == 0

        @pl.when(first)
        def _():
            qg_acc[...] = jnp.zeros_like(qg_acc)
            kg_acc[...] = jnp.zeros_like(kg_acc)
            sink_acc[...] = jnp.zeros_like(sink_acc)

        dkn_acc[...] = jnp.zeros_like(dkn_acc)
        dv_acc[...] = jnp.zeros_like(dv_acc)

        def block(i, carry):
            r0 = pl.multiple_of(i * BLOCK, BLOCK)
            rp = pl.multiple_of(jnp.maximum(i - 1, 0) * BLOCK, BLOCK)
            has_prev = i > 0
            kn_c = _head_norm(kv_ref[pl.ds(r0, BLOCK), 0:KV_WIDTH].astype(F32), kg, lo)[0].astype(BF16)
            kn_p = _head_norm(kv_ref[pl.ds(rp, BLOCK), 0:KV_WIDTH].astype(F32), kg, lo)[0].astype(BF16)
            v_c = kv_ref[pl.ds(r0, BLOCK), KV_WIDTH:2 * KV_WIDTH].astype(BF16)
            v_p = kv_ref[pl.ds(rp, BLOCK), KV_WIDTH:2 * KV_WIDTH].astype(BF16)
            dk_c, dk_p, dv_c, dv_p = [], [], [], []
            for kv in range(2):
                k2c, k2p = _dup_half(kn_c, kv, lo), _dup_half(kn_p, kv, lo)
                v2c, v2p = _dup_half(v_c, kv, lo), _dup_half(v_p, kv, lo)
                cols = [slice((2 * kv + t) * LANES, (2 * kv + t + 1) * LANES) for t in range(2)]
                normed = [_head_norm(q_ref[pl.ds(r0, BLOCK), c].astype(F32), qg, lo) for c in cols]
                q4 = _stack_heads(normed[0][0], normed[1][0], lo).astype(BF16)
                do4 = _stack_heads(dy_ref[pl.ds(r0, BLOCK), cols[0]], dy_ref[pl.ds(r0, BLOCK), cols[1]], lo)
                p_c, p_p, p_s = _att_probs(q4, k2c, k2p, bias_ref[kv, 0], bias_ref[kv, 1], sinkcol_ref[kv], has_prev)
                dp_c = _dot_nt(do4, v2c)
                dp_p = _dot_nt(do4, v2p)
                delta = jnp.sum(p_c * dp_c + p_p * dp_p, axis=-1, keepdims=True)
                ds_c = (p_c * (dp_c - delta)).astype(BF16)
                ds_p = (p_p * (dp_p - delta)).astype(BF16)
                sink_acc[kv] += -(p_s * delta)
                dq4 = (_dot_nn(ds_c, k2c) + _dot_nn(ds_p, k2p)) * ATT_SCALE
                for c, (_, qh, qr), dqn in zip(cols, normed, _unstack_heads(dq4, lo)):
                    dq, dg = _head_norm_bwd(qh, qr, qg, dqn, lo)
                    dqkv_ref[pl.ds(r0, BLOCK), c] = dq.astype(BF16)
                    qg_acc[...] += dg
                dk_c.append(_dot_tn(ds_c, q4))
                dk_p.append(_dot_tn(ds_p, q4))
                dv_c.append(_dot_tn(p_c.astype(BF16), do4))
                dv_p.append(_dot_tn(p_p.astype(BF16), do4))

            def fold(parts):
                a = parts[0] + pltpu.roll(parts[0], LANES // 2, axis=1)
                b = parts[1] + pltpu.roll(parts[1], LANES // 2, axis=1)
                return jnp.where(lo, a, b)

            dkn_acc[pl.ds(r0, BLOCK), :] += fold(dk_c) * ATT_SCALE
            dkn_acc[pl.ds(rp, BLOCK), :] += fold(dk_p) * ATT_SCALE
            dv_acc[pl.ds(r0, BLOCK), :] += fold(dv_c)
            dv_acc[pl.ds(rp, BLOCK), :] += fold(dv_p)
            return carry

        lax.fori_loop(0, nb, block, 0)

        def finish(i, carry):
            r0 = pl.multiple_of(i * BLOCK, BLOCK)
            _, kh, kr = _head_norm(kv_ref[pl.ds(r0, BLOCK), 0:KV_WIDTH].astype(F32), kg, lo)
            dk, dg = _head_norm_bwd(kh, kr, kg, dkn_acc[pl.ds(r0, BLOCK), :], lo)
            dqkv_ref[pl.ds(r0, BLOCK), ATT_WIDTH:ATT_WIDTH + KV_WIDTH] = dk.astype(BF16)
            dqkv_ref[pl.ds(r0, BLOCK), ATT_WIDTH + KV_WIDTH:QKV_WIDTH] = dv_acc[pl.ds(r0, BLOCK), :].astype(BF16)
            kg_acc[...] += dg
            return carry

        lax.fori_loop(0, nb, finish, 0)

        @pl.when(pl.program_id(0) == n_seq - 1)
        def _():
            dqg_ref[...] = jnp.sum(qg_acc[...], axis=0, keepdims=True)
            dkg_ref[...] = jnp.sum(kg_acc[...], axis=0, keepdims=True)
            lane = lax.broadcasted_iota(jnp.int32, (1, LANES), 1)
            dsink = jnp.zeros((1, LANES), F32)
            for kv in range(2):
                for r in range(Q_GROUP):
                    total = jnp.sum(sink_acc[kv, r * BLOCK:(r + 1) * BLOCK, :], axis=0, keepdims=True)
                    dsink = jnp.where(lane == Q_GROUP * kv + r, total, dsink)
            dsink_ref[...] = dsink

    vec = pl.BlockSpec((1, LANES), lambda b: (0, 0))
    acc = pltpu.VMEM((BLOCK, LANES), F32)
    body, dep_specs, dep_args = _with_deps(body, 6, deps)
    dqkv, dqg, dkg, dsink = pl.pallas_call(
        body, name=name, grid=(n_seq,),
        in_specs=[pl.BlockSpec((seq, ATT_WIDTH), lambda b: (b, qcol)),
                  pl.BlockSpec((seq, 2 * KV_WIDTH), lambda b: (b, kvcol)),
                  pl.BlockSpec((seq, ATT_WIDTH), lambda b: (b, 0)),
                  vec, vec, pl.BlockSpec(memory_space=pltpu.SMEM)] + dep_specs,
        out_specs=[pl.BlockSpec((seq, QKV_WIDTH), lambda b: (b, 0)), vec, vec, vec],
        out_shape=[jax.ShapeDtypeStruct((T, QKV_WIDTH), BF16)] + [jax.ShapeDtypeStruct((1, LANES), F32)] * 3,
        scratch_shapes=[pltpu.VMEM((seq, KV_WIDTH), F32), pltpu.VMEM((seq, KV_WIDTH), F32), acc, acc,
                        pltpu.VMEM((2, GROUP_ROWS, 1), F32), *ATT_SCRATCH],
        compiler_params=_params(("arbitrary",)),
    )(proj, proj, dy, jnp.tile(q_gain, 2).reshape(1, LANES), jnp.tile(k_gain, 2).reshape(1, LANES), sinks, *dep_args)
    half = LANES // 2
    return dqkv, dqg[0, :half] + dqg[0, half:], dkg[0, :half] + dkg[0, half:], dsink[0, :N_Q_HEADS]


def _sgu_weights(w_ref):
    r = lax.broadcasted_iota(jnp.int32, (BLOCK, BLOCK), 0)
    c = lax.broadcasted_iota(jnp.int32, (BLOCK, BLOCK), 1)
    return [jnp.where(r >= c, w_ref[g], 0.0).astype(BF16) for g in range(SGU_GROUPS)]


def _sgu_fwd(proj, gain, w_s, bias_full, *, n_seq, seq, name):
    T = n_seq * seq
    nc = seq // BLOCK

    def body(suv_ref, g_ref, w_ref, b_ref, y_ref):
        lo = _lo_mask((BLOCK, LANES))
        wm = _sgu_weights(w_ref)
        gain_v = g_ref[...]

        def chunk(c, carry):
            r0 = pl.multiple_of(c * BLOCK, BLOCK)
            gv = _gelu(suv_ref[pl.ds(r0, BLOCK), SGU_WIDTH:2 * SGU_WIDTH].astype(F32))
            r = lax.rsqrt(jnp.mean(gv * gv, axis=-1, keepdims=True) + NORM_EPS)
            vn = (gv * r * gain_v).astype(BF16)
            for p in range(SGU_WIDTH // LANES):
                cols = slice(p * LANES, (p + 1) * LANES)
                vp = vn[:, cols]
                mixed = jnp.where(lo, _dot_nn(wm[2 * p], vp), _dot_nn(wm[2 * p + 1], vp)) + b_ref[:, cols]
                u = _gelu(suv_ref[pl.ds(r0, BLOCK), cols].astype(F32))
                y_ref[pl.ds(r0, BLOCK), cols] = (u * mixed).astype(BF16)
            return carry

        lax.fori_loop(0, nc, chunk, 0)

    return pl.pallas_call(
        body, name=name, grid=(n_seq,),
        in_specs=[pl.BlockSpec((seq, 2 * SGU_WIDTH), lambda b: (b, COL_SUV // (2 * SGU_WIDTH))),
                  pl.BlockSpec((1, SGU_WIDTH), lambda b: (0, 0)),
                  pl.BlockSpec((SGU_GROUPS, BLOCK, BLOCK), lambda b: (0, 0, 0)),
                  pl.BlockSpec((BLOCK, SGU_WIDTH), lambda b: (0, 0))],
        out_specs=pl.BlockSpec((seq, SGU_WIDTH), lambda b: (b, 0)),
        out_shape=jax.ShapeDtypeStruct((T, SGU_WIDTH), BF16),
        compiler_params=_params(("parallel",)),
    )(proj, gain.reshape(1, SGU_WIDTH), w_s, bias_full)


def _sgu_bwd(proj, dy, gain, w_s, bias_full, *, n_seq, seq, name, deps=()):
    T = n_seq * seq
    nc = seq // BLOCK
    n_tiles = SGU_WIDTH // LANES

    def body(suv_ref, dy_ref, g_ref, w_ref, b_ref, dsuv_ref, dg_ref, dw_ref, db_ref, dg_acc, dw_acc, db_acc):
        lo = _lo_mask((BLOCK, LANES))
        hi = jnp.logical_not(lo)
        wm = _sgu_weights(w_ref)
        wmt = [jnp.where(lax.broadcasted_iota(jnp.int32, (BLOCK, BLOCK), 1) >= lax.broadcasted_iota(jnp.int32, (BLOCK, BLOCK), 0),
                         w_ref[g].T, 0.0).astype(BF16) for g in range(SGU_GROUPS)]
        gain_v = g_ref[...]

        @pl.when(pl.program_id(0) == 0)
        def _():
            dg_acc[...] = jnp.zeros_like(dg_acc)
            dw_acc[...] = jnp.zeros_like(dw_acc)
            db_acc[...] = jnp.zeros_like(db_acc)

        def chunk(c, carry):
            r0 = pl.multiple_of(c * BLOCK, BLOCK)
            gv, dgelu_v = _gelu_and_grad(suv_ref[pl.ds(r0, BLOCK), SGU_WIDTH:2 * SGU_WIDTH].astype(F32))
            r = lax.rsqrt(jnp.mean(gv * gv, axis=-1, keepdims=True) + NORM_EPS)
            vh = gv * r
            vn = (vh * gain_v).astype(BF16)
            dvn_tiles = []
            for p in range(n_tiles):
                cols = slice(p * LANES, (p + 1) * LANES)
                vp = vn[:, cols]
                mixed = jnp.where(lo, _dot_nn(wm[2 * p], vp), _dot_nn(wm[2 * p + 1], vp)) + b_ref[:, cols]
                u, dgelu_u = _gelu_and_grad(suv_ref[pl.ds(r0, BLOCK), cols].astype(F32))
                dyv = dy_ref[pl.ds(r0, BLOCK), cols]
                dsuv_ref[pl.ds(r0, BLOCK), cols] = (dyv * mixed * dgelu_u).astype(BF16)
                dm = dyv * u
                db_acc[:, cols] += dm
                dm_bf = dm.astype(BF16)
                dvn_tiles.append(jnp.where(lo, _dot_nn(wmt[2 * p], dm_bf), _dot_nn(wmt[2 * p + 1], dm_bf)))
                dw_acc[2 * p] += _dot_nt(jnp.where(lo, dm, 0.0).astype(BF16), vp)
                dw_acc[2 * p + 1] += _dot_nt(jnp.where(hi, dm, 0.0).astype(BF16), vp)
            dvn = jnp.concatenate(dvn_tiles, axis=1)
            dg_acc[...] += dvn * vh
            dvh = dvn * gain_v
            dgv = r * (dvh - vh * jnp.mean(dvh * vh, axis=-1, keepdims=True))
            dsuv_ref[pl.ds(r0, BLOCK), SGU_WIDTH:2 * SGU_WIDTH] = (dgv * dgelu_v).astype(BF16)
            return carry

        lax.fori_loop(0, nc, chunk, 0)

        @pl.when(pl.program_id(0) == n_seq - 1)
        def _():
            dg_ref[...] = jnp.sum(dg_acc[...], axis=0, keepdims=True)
            r = lax.broadcasted_iota(jnp.int32, (BLOCK, BLOCK), 0)
            c = lax.broadcasted_iota(jnp.int32, (BLOCK, BLOCK), 1)
            for g in range(SGU_GROUPS):
                dw_ref[g] = jnp.where(r >= c, dw_acc[g], 0.0)
            lane = lax.broadcasted_iota(jnp.int32, (BLOCK, LANES), 1)
            out = jnp.zeros((BLOCK, LANES), F32)
            for p in range(n_tiles):
                tile = db_acc[:, p * LANES:(p + 1) * LANES]
                s_lo = jnp.sum(jnp.where(lo, tile, 0.0), axis=-1, keepdims=True)
                s_hi = jnp.sum(jnp.where(hi, tile, 0.0), axis=-1, keepdims=True)
                out = jnp.where(lane == 2 * p, s_lo, out)
                out = jnp.where(lane == 2 * p + 1, s_hi, out)
            db_ref[...] = out

    body, dep_specs, dep_args = _with_deps(body, 5, deps)
    dsuv, dg, dw, db = pl.pallas_call(
        body, name=name, grid=(n_seq,),
        in_specs=[pl.BlockSpec((seq, 2 * SGU_WIDTH), lambda b: (b, COL_SUV // (2 * SGU_WIDTH))),
                  pl.BlockSpec((seq, SGU_WIDTH), lambda b: (b, 0)),
                  pl.BlockSpec((1, SGU_WIDTH), lambda b: (0, 0)),
                  pl.BlockSpec((SGU_GROUPS, BLOCK, BLOCK), lambda b: (0, 0, 0)),
                  pl.BlockSpec((BLOCK, SGU_WIDTH), lambda b: (0, 0))] + dep_specs,
        out_specs=[pl.BlockSpec((seq, 2 * SGU_WIDTH), lambda b: (b, 0)),
                   pl.BlockSpec((1, SGU_WIDTH), lambda b: (0, 0)),
                   pl.BlockSpec((SGU_GROUPS, BLOCK, BLOCK), lambda b: (0, 0, 0)),
                   pl.BlockSpec((BLOCK, LANES), lambda b: (0, 0))],
        out_shape=[jax.ShapeDtypeStruct((T, 2 * SGU_WIDTH), BF16), jax.ShapeDtypeStruct((1, SGU_WIDTH), F32),
                   jax.ShapeDtypeStruct((SGU_GROUPS, BLOCK, BLOCK), F32), jax.ShapeDtypeStruct((BLOCK, LANES), F32)],
        scratch_shapes=[pltpu.VMEM((BLOCK, SGU_WIDTH), F32), pltpu.VMEM((SGU_GROUPS, BLOCK, BLOCK), F32),
                        pltpu.VMEM((BLOCK, SGU_WIDTH), F32)],
        compiler_params=_params(("arbitrary",)),
    )(proj, dy, gain.reshape(1, SGU_WIDTH), w_s, bias_full, *dep_args)
    return dsuv, dg.reshape(SGU_WIDTH), dw, db[:, :SGU_GROUPS].T


def _merge_fwd(y_att, y_sgu, w_oa, w_ob, proj, *, name, tm=1024, tn=512, deps=()):
    T = y_att.shape[0]

    def body(ya_ref, ys_ref, wa_ref, wb_ref, ga_ref, gb_ref, o_ref):
        pa = _dot_nn(ya_ref[...], wa_ref[...])
        pb = _dot_nn(ys_ref[...], wb_ref[...])
        o_ref[...] = (_sigmoid(ga_ref[...].astype(F32)) * pa + _sigmoid(gb_ref[...].astype(F32)) * pb).astype(BF16)

    act = pl.BlockSpec((tm, ATT_WIDTH), lambda i, j: (i, 0))
    wgt = pl.BlockSpec((ATT_WIDTH, tn), lambda i, j: (0, j))
    body, dep_specs, dep_args = _with_deps(body, 6, deps)
    return pl.pallas_call(
        body, name=name, grid=(T // tm, D_MODEL // tn),
        in_specs=[act, act, wgt, wgt,
                  pl.BlockSpec((tm, tn), lambda i, j: (i, j + COL_GA // tn)),
                  pl.BlockSpec((tm, tn), lambda i, j: (i, j + COL_GB // tn))] + dep_specs,
        out_specs=pl.BlockSpec((tm, tn), lambda i, j: (i, j)),
        out_shape=jax.ShapeDtypeStruct((T, D_MODEL), BF16),
        compiler_params=_params(("parallel", "parallel")),
    )(y_att, y_sgu, w_oa, w_ob, proj, proj, *dep_args)


def _merge_bwd(dx1_bf, w_out, y_att, y_sgu, w_oa, w_ob, proj, *, name, tm=1024, tn=512):
    T = y_att.shape[0]

    def body(dx_ref, wo_ref, ya_ref, ys_ref, wa_ref, wb_ref, ga_ref, gb_ref, dpa_ref, dpb_ref, dga_ref, dgb_ref):
        dm = _dot_nt(dx_ref[...], wo_ref[...])
        pa = _dot_nn(ya_ref[...], wa_ref[...])
        pb = _dot_nn(ys_ref[...], wb_ref[...])
        sa = _sigmoid(ga_ref[...].astype(F32))
        sb = _sigmoid(gb_ref[...].astype(F32))
        dpa_ref[...] = (dm * sa).astype(BF16)
        dpb_ref[...] = (dm * sb).astype(BF16)
        dga_ref[...] = (dm * pa * sa * (1.0 - sa)).astype(BF16)
        dgb_ref[...] = (dm * pb * sb * (1.0 - sb)).astype(BF16)

    act = pl.BlockSpec((tm, ATT_WIDTH), lambda i, j: (i, 0))
    wgt = pl.BlockSpec((ATT_WIDTH, tn), lambda i, j: (0, j))
    out = pl.BlockSpec((tm, tn), lambda i, j: (i, j))
    return pl.pallas_call(
        body, name=name, grid=(T // tm, D_MODEL // tn),
        in_specs=[pl.BlockSpec((tm, D_MODEL), lambda i, j: (i, 0)),
                  pl.BlockSpec((tn, D_MODEL), lambda i, j: (j, 0)),
                  act, act, wgt, wgt,
                  pl.BlockSpec((tm, tn), lambda i, j: (i, j + COL_GA // tn)),
                  pl.BlockSpec((tm, tn), lambda i, j: (i, j + COL_GB // tn))],
        out_specs=[out] * 4,
        out_shape=[jax.ShapeDtypeStruct((T, D_MODEL), BF16)] * 4,
        compiler_params=_params(("parallel", "parallel")),
    )(dx1_bf, w_out, y_att, y_sgu, w_oa, w_ob, proj, proj)


CONV_ROWS = 256
CONV_TN = 256
UP_CONV_ROWS = 256


def _shift_rows(cur, prev8, k):
    rolled = pltpu.roll(cur, k, axis=0)
    head = jnp.where(lax.broadcasted_iota(jnp.int32, prev8.shape, 0) < k, pltpu.roll(prev8, k, axis=0), rolled[:SUBLANES])
    return jnp.concatenate([head, rolled[SUBLANES:]], axis=0)


def _shift_rows_up(cur, next8, k):
    n = cur.shape[0]
    rolled = pltpu.roll(cur, n - k, axis=0)
    tail = jnp.where(lax.broadcasted_iota(jnp.int32, next8.shape, 0) >= SUBLANES - k,
                     pltpu.roll(next8, SUBLANES - k, axis=0), rolled[n - SUBLANES:])
    return jnp.concatenate([rolled[:n - SUBLANES], tail], axis=0)


def _up_conv_fwd(h2, w_up_t, cw_g, cw_v, cb_g, cb_v, *, n_seq, seq, name, deps=()):
    T = n_seq * seq
    tn, rows = CONV_TN, UP_CONV_ROWS

    def body(h_ref, ug_ref, uv_ref, wg_ref, wv_ref, bg_ref, bv_ref, a_ref, zg_ref, zv_ref, cg_ref, cv_ref):
        def conv(cur, prev8, w_ref, b_ref):
            z1 = _shift_rows(cur, prev8, 1)
            z2 = _shift_rows(cur, prev8, 2)
            return b_ref[...] + w_ref[0:1, :] * z2 + w_ref[1:2, :] * z1 + w_ref[2:3, :] * cur

        start = jnp.zeros((SUBLANES, tn), F32)
        prev = (start, start)
        for s in range(seq // rows):
            r = pl.ds(s * rows, rows)
            h = h_ref[r, :]
            zg = _dot_nt(h, ug_ref[...])
            zv = _dot_nt(h, uv_ref[...])
            zg_ref[r, :] = zg.astype(ACT_DTYPE)
            zv_ref[r, :] = zv.astype(ACT_DTYPE)
            g = conv(zg, prev[0], wg_ref, bg_ref)
            v = conv(zv, prev[1], wv_ref, bv_ref)
            a_ref[r, :] = (g * _sigmoid(g) * v).astype(BF16)
            cg_ref[r, :] = g.astype(ACT_DTYPE)
            cv_ref[r, :] = v.astype(ACT_DTYPE)
            prev = (zg[rows - SUBLANES:], zv[rows - SUBLANES:])

    zs = pl.BlockSpec((seq, tn), lambda b, j: (b, j))
    ws = pl.BlockSpec((3, tn), lambda b, j: (0, j))
    bs = pl.BlockSpec((1, tn), lambda b, j: (0, j))
    body, dep_specs, dep_args = _with_deps(body, 7, deps)
    return pl.pallas_call(
        body, name=name, grid=(n_seq, D_FF // tn),
        in_specs=[pl.BlockSpec((seq, D_MODEL), lambda b, j: (b, 0)),
                  pl.BlockSpec((tn, D_MODEL), lambda b, j: (j, 0)),
                  pl.BlockSpec((tn, D_MODEL), lambda b, j: (j + D_FF // tn, 0)), ws, ws, bs, bs] + dep_specs,
        out_specs=[zs] * 5,
        out_shape=[jax.ShapeDtypeStruct((T, D_FF), BF16)] + [jax.ShapeDtypeStruct((T, D_FF), ACT_DTYPE)] * 4,
        compiler_params=_params(("parallel", "parallel")),
    )(h2, w_up_t, w_up_t, cw_g, cw_v, cb_g.reshape(1, D_FF), cb_v.reshape(1, D_FF), *dep_args)


def _conv_bwd(z_g, z_v, c_g, c_v, dx2_bf, w_down, cw_g, cw_v, *, n_seq, seq, name):
    T = n_seq * seq
    tn, rows = CONV_TN, CONV_ROWS
    n_steps = seq // rows

    def body(zg_ref, zv_ref, cg_ref, cv_ref, dx_ref, wd_ref, wg_ref, wv_ref,
             dzg_ref, dzv_ref, dwg_ref, dwv_ref, dbg_ref, dbv_ref, dcg_ref, dcv_ref):
        def colsum(x):
            return jnp.sum(x, axis=0, keepdims=True)

        zero = jnp.zeros((1, tn), F32)
        db = (zero, zero)
        for s in range(n_steps):
            r = pl.ds(s * rows, rows)
            g = cg_ref[r, :].astype(F32)
            v = cv_ref[r, :].astype(F32)
            sg = _sigmoid(g)
            dav = _dot_nt(dx_ref[r, :], wd_ref[...])
            dcg = dav * v * (sg * (1.0 + g * (1.0 - sg)))
            dcv = dav * (g * sg)
            dcg_ref[r, :] = dcg
            dcv_ref[r, :] = dcv
            db = (db[0] + colsum(dcg), db[1] + colsum(dcv))

        def back(s, accs):
            r0 = pl.multiple_of(s * rows, rows)
            last = s == n_steps - 1
            rn = pl.multiple_of(jnp.minimum(r0 + rows, seq - SUBLANES), SUBLANES)
            new = []
            for half, (dc_ref, w_ref, dz_ref, z_ref) in enumerate(((dcg_ref, wg_ref, dzg_ref, zg_ref),
                                                                   (dcv_ref, wv_ref, dzv_ref, zv_ref))):
                cur = dc_ref[pl.ds(r0, rows), :]
                nxt = jnp.where(last, 0.0, dc_ref[pl.ds(rn, SUBLANES), :])
                u1, u2 = _shift_rows_up(cur, nxt, 1), _shift_rows_up(cur, nxt, 2)
                dz_ref[pl.ds(r0, rows), :] = (w_ref[2:3, :] * cur + w_ref[1:2, :] * u1 + w_ref[0:1, :] * u2).astype(BF16)
                z = z_ref[pl.ds(r0, rows), :].astype(F32)
                new += [accs[3 * half] + colsum(u2 * z), accs[3 * half + 1] + colsum(u1 * z),
                        accs[3 * half + 2] + colsum(cur * z)]
            return tuple(new)

        dw = lax.fori_loop(0, n_steps, back, (zero,) * 6)
        first_seq = pl.program_id(1) == 0

        @pl.when(first_seq)
        def _():
            dwg_ref[...] = jnp.concatenate(dw[0:3], axis=0)
            dwv_ref[...] = jnp.concatenate(dw[3:6], axis=0)
            dbg_ref[...], dbv_ref[...] = db

        @pl.when(jnp.logical_not(first_seq))
        def _():
            dwg_ref[...] += jnp.concatenate(dw[0:3], axis=0)
            dwv_ref[...] += jnp.concatenate(dw[3:6], axis=0)
            dbg_ref[...] += db[0]
            dbv_ref[...] += db[1]

    zs = pl.BlockSpec((seq, tn), lambda j, b: (b, j))
    ws = pl.BlockSpec((3, tn), lambda j, b: (0, j))
    bs = pl.BlockSpec((1, tn), lambda j, b: (0, j))
    outs = pl.pallas_call(
        body, name=name, grid=(D_FF // tn, n_seq),
        in_specs=[zs] * 4 + [pl.BlockSpec((seq, D_MODEL), lambda j, b: (b, 0)),
                             pl.BlockSpec((tn, D_MODEL), lambda j, b: (j, 0)), ws, ws],
        out_specs=[zs, zs, ws, ws, bs, bs],
        out_shape=[jax.ShapeDtypeStruct((T, D_FF), BF16)] * 2 + [jax.ShapeDtypeStruct((3, D_FF), F32)] * 2
        + [jax.ShapeDtypeStruct((1, D_FF), F32)] * 2,
        scratch_shapes=[pltpu.VMEM((seq, tn), F32), pltpu.VMEM((seq, tn), F32)],
        compiler_params=_params(("parallel", "arbitrary")),
    )(z_g, z_v, c_g, c_v, dx2_bf, w_down, cw_g, cw_v)
    dz_g, dz_v, dw_g, dw_v, db_g, db_v = outs
    return dz_g, dz_v, dw_g, dw_v, db_g.reshape(D_FF), db_v.reshape(D_FF)


def _layer_fwd(x, h, w, sched, tail, *, n_seq, seq, l):
    tag = f"l{l}"
    deps = sched("fwd_start", l, h)
    proj = _mm(h, w["w_in_t"], mode="nt", out_dtype=ACT_DTYPE, rotate=W_IN_ROTATE, name=f"{tag}_proj", deps=deps)
    y_att = _attention_fwd(proj, w["q_norm"], w["k_norm"], w["sinks"], n_seq=n_seq, seq=seq, name=f"{tag}_att")
    deps = sched("fwd_att", l, y_att)
    y_sgu = _sgu_fwd(proj, w["sgu_norm"], w["w_s"], w["bias_full"], n_seq=n_seq, seq=seq, name=f"{tag}_sgu")
    merged = _merge_fwd(y_att, y_sgu, w["w_oa"], w["w_ob"], proj, name=f"{tag}_merge", deps=deps)
    x1, h2 = _mm_rows(merged, w["w_out"], mode="nn", fn=_residual_then_norm, out_dtypes=(F32, BF16), rows=(x,),
                      vecs=(w["ffn_norm"],), name=f"{tag}_out")
    deps = sched("fwd_mixer_done", l, x1)
    a, z_g, z_v, c_g, c_v = _up_conv_fwd(h2, w["w_up_t"], w["cw_g"], w["cw_v"], w["cb_g"], w["cb_v"], n_seq=n_seq,
                                         seq=seq, name=f"{tag}_up_conv", deps=deps)
    deps = sched("fwd_conv", l, a)
    if tail[0] == "norm":
        out = _mm_rows(a, w["w_down"], mode="nn", fn=_residual_then_norm, out_dtypes=(F32, BF16), rows=(x1,),
                       vecs=(tail[1],), name=f"{tag}_down", deps=deps)
    else:
        out = _mm_rows(a, w["w_down"], mode="nn", fn=_residual_then_loss, out_dtypes=(F32, BF16), rows=(x1, tail[1]),
                       reduce=True, name=f"{tag}_down", deps=deps)
    saved = dict(x=x, h=h, proj=proj, y_att=y_att, y_sgu=y_sgu, merged=merged, x1=x1, h2=h2, z_g=z_g, z_v=z_v,
                 c_g=c_g, c_v=c_v, a=a)
    return out, saved


def _layer_bwd(dx2, dx2_bf, w, s, sched, deps, *, n_seq, seq, l):
    tag = f"l{l}b"
    g = {}
    g["w_down"] = _mm(s["a"], dx2_bf, mode="tn", out_dtype=F32, name=f"{tag}_dw_down", deps=deps)
    dz_g, dz_v, g["cw_g"], g["cw_v"], g["cb_g"], g["cb_v"] = _conv_bwd(
        s["z_g"], s["z_v"], s["c_g"], s["c_v"], dx2_bf, w["w_down"], w["cw_g"], w["cw_v"], n_seq=n_seq, seq=seq,
        name=f"{tag}_conv")
    dw_up_t = _mm(dz_g, s["h2"], mode="tn", out_dtype=F32, out_rows=(0, 2 * D_FF), name=f"{tag}_dw_up_g")
    g["w_up_t"] = _mm(dz_v, s["h2"], mode="tn", out_dtype=F32, out_rows=(D_FF, 2 * D_FF), out_prev=dw_up_t,
                      name=f"{tag}_dw_up_v")
    deps = sched("bwd_ffn_grads", l, dz_v, g)
    dx1, dx1_bf, dgain = _mm_rows((dz_g, dz_v), w["w_up_t"], mode="nn", fn=_rms_bwd_rows, out_dtypes=(F32, BF16),
                                  rows=(s["x1"], dx2), vecs=(w["ffn_norm"],), reduce=True, a_at=(0, D_FF),
                                  name=f"{tag}_dh2", deps=deps)
    g["ffn_norm"] = dgain.reshape(D_MODEL)
    dpa, dpb, dga, dgb = _merge_bwd(dx1_bf, w["w_out"], s["y_att"], s["y_sgu"], w["w_oa"], w["w_ob"], s["proj"],
                                    name=f"{tag}_merge")
    deps = sched("bwd_merge", l, dpa)
    g["w_out"] = _mm(s["merged"], dx1_bf, mode="tn", out_dtype=F32, name=f"{tag}_dw_out",
                     deps=deps)
    dy_att = _mm(dpa, w["w_oa"], mode="nt", out_dtype=BF16, name=f"{tag}_dy_att")
    dy_sgu = _mm(dpb, w["w_ob"], mode="nt", out_dtype=F32, name=f"{tag}_dy_sgu")
    g["w_oa"] = _mm(s["y_att"], dpa, mode="tn", out_dtype=F32, name=f"{tag}_dw_oa")
    g["w_ob"] = _mm(s["y_sgu"], dpb, mode="tn", out_dtype=F32, name=f"{tag}_dw_ob")
    deps = sched("bwd_out_grads", l, dy_att, g)
    dqkv, g["q_norm"], g["k_norm"], g["sinks"] = _attention_bwd(
        s["proj"], dy_att, w["q_norm"], w["k_norm"], w["sinks"], n_seq=n_seq, seq=seq, name=f"{tag}_att", deps=deps)
    deps = sched("bwd_att", l, dqkv)
    dsuv, g["sgu_norm"], g["w_s"], g["b_s"] = _sgu_bwd(
        s["proj"], dy_sgu, w["sgu_norm"], w["w_s"], w["bias_full"], n_seq=n_seq, seq=seq, name=f"{tag}_sgu", deps=deps)
    dproj = (dsuv, dga, dgb, dqkv)
    at = (QKV_WIDTH, QKV_WIDTH + 2 * SGU_WIDTH, QKV_WIDTH + 2 * SGU_WIDTH + D_MODEL, 0)
    g["w_in_t"] = _mm_tn_parts(dproj, at, s["h"], name=f"{tag}_dw_in")
    deps = sched("bwd_w_in_grad", l, dqkv, g)
    dx, dx_bf, dgain = _mm_rows(dproj, w["w_in_t"], mode="nn", fn=_rms_bwd_rows, out_dtypes=(F32, BF16),
                                rows=(s["x"], dx1), vecs=(w["mix_norm"],), reduce=True, a_at=at,
                                name=f"{tag}_dh", deps=deps)
    g["mix_norm"] = dgain.reshape(D_MODEL)
    return dx, dx_bf, g, sched("bwd_dh", l, dx)


def _local_step(x, target, weights, sched, *, n_seq, seq):
    depth = len(weights)
    saved = []
    h = _rms_fwd(x, weights[0]["mix_norm"], name="l0_mix_norm", deps=sched("begin", 0, x))
    for l in range(depth):
        tail = ("norm", weights[l + 1]["mix_norm"]) if l + 1 < depth else ("loss", target)
        out, s = _layer_fwd(x, h, weights[l], sched, tail, n_seq=n_seq, seq=seq, l=l)
        saved.append(s)
        if l + 1 < depth:
            x, h = out
    dy, dy_bf, loss_cols = out
    grads = [None] * depth
    deps = ()
    for l in reversed(range(depth)):
        dy, dy_bf, grads[l], deps = _layer_bwd(dy, dy_bf, weights[l], saved[l], sched, deps, n_seq=n_seq, seq=seq, l=l)
    return jnp.sum(loss_cols), dy, grads, deps


W_IN_SHARD = IN_WIDTH // N_DEV
W_UP_SHARD = 2 * D_FF // N_DEV
COL_MOVE_ROWS = 256


def _w_o_moves():
    return tuple((j, 0, LANES, 0, j * LANES) for j in range(N_DEV))


def _disassemble(mats, w, moves, *, name):
    R = mats[0].shape[0]
    tr = min(R, COL_MOVE_ROWS)
    n = len(mats)

    def body(*refs):
        m_refs, o_ref = refs[:n], refs[n]
        for j, lo, hi, which, at in moves:
            o_ref[j, :, lo:hi] = m_refs[which][:, at:at + hi - lo]

    return pl.pallas_call(
        body, name=name, grid=(R // tr,),
        in_specs=[pl.BlockSpec((tr, m.shape[1]), lambda i: (i, 0)) for m in mats],
        out_specs=pl.BlockSpec((N_DEV, tr, w), lambda i: (0, i, 0)),
        out_shape=jax.ShapeDtypeStruct((N_DEV, R, w), mats[0].dtype),
        compiler_params=_params(("parallel",)),
    )(*mats)


def _my_place():
    return lax.axis_index("x"), lax.axis_index("y"), lax.axis_index("c")


def _gathered_shape(shape, kind):
    r, c = shape
    return {"blocks": (N_DEV, r, c), "rows": (N_DEV * r, c), "cols": (r, N_DEV * c)}[kind]


def _gather_window(ref, kind, shape, j):
    r, c = shape
    if kind == "blocks":
        return ref.at[j]
    if kind == "rows":
        return ref.at[pl.ds(pl.multiple_of(j * r, r), r), :]
    return ref.at[:, pl.ds(pl.multiple_of(j * c, c), c)]


def _gather(srcs, kinds, *, name):
    n = len(srcs)
    shapes = [s.shape for s in srcs]
    per = 7

    def body(*refs):
        src_refs, dst_refs = refs[:n], refs[n:2 * n]
        send_sems, recv_sems, local_sems = refs[2 * n:]
        x, y, c = _my_place()
        me, sibling = (x, y, c), (x, y, 1 - c)
        chips = [(1 - x, y), (x, 1 - y), (1 - x, 1 - y)]

        def at(i, px, py, pc):
            return _gather_window(dst_refs[i], kinds[i], shapes[i], 4 * px + 2 * py + pc)

        def copy(i, k, block, to, src=None):
            return pltpu.make_async_remote_copy(
                src_ref=at(i, *block) if src is None else src, dst_ref=at(i, *block),
                send_sem=send_sems.at[per * i + k], recv_sem=recv_sems.at[per * i + k], device_id=to, device_id_type=MESH)

        mine = [pltpu.make_async_copy(src_refs[i], at(i, *me), local_sems.at[i]) for i in range(n)]
        for cp in mine:
            cp.start()
        started = []
        for i in range(n):
            first = [copy(i, 0, me, sibling, src=src_refs[i])]
            first += [copy(i, 1 + j, me, (*chip, c), src=src_refs[i]) for j, chip in enumerate(chips)]
            for cp in first:
                cp.start()
            started += first
        for i in range(n):
            for j, chip in enumerate(chips):
                copy(i, 1 + j, (*chip, c), me).wait_recv()
                fwd = copy(i, 4 + j, (*chip, c), sibling)
                fwd.start()
                started.append(fwd)
        for i in range(n):
            copy(i, 0, sibling, me).wait_recv()
            for j, chip in enumerate(chips):
                copy(i, 4 + j, (*chip, 1 - c), me).wait_recv()
        for cp in started:
            cp.wait_send()
        for cp in mine:
            cp.wait()

    return pl.pallas_call(
        body, name=name,
        out_shape=[jax.ShapeDtypeStruct(_gathered_shape(s.shape, k), s.dtype) for s, k in zip(srcs, kinds)],
        in_specs=[ANY] * n, out_specs=[ANY] * n,
        scratch_shapes=[pltpu.SemaphoreType.DMA((per * n,)), pltpu.SemaphoreType.DMA((per * n,)),
                        pltpu.SemaphoreType.DMA((n,))],
    )(*srcs)


HBM = pl.BlockSpec(memory_space=pltpu.HBM)
SEM = pl.BlockSpec(memory_space=pltpu.SEMAPHORE)
TOKEN = jax.ShapeDtypeStruct((SUBLANES, LANES), F32)
TOKEN_SPEC = pl.BlockSpec(memory_space=pltpu.VMEM)
SPLIT_PARAMS = pltpu.CompilerParams(has_side_effects=pltpu.SideEffectType.DATAFLOW_SIDE_EFFECTING)


def _in_hbm(x):
    return pltpu.with_memory_space_constraint(x, pltpu.HBM)


def _hbm_like(shape, dtype):
    return pltpu.HBM(shape, dtype)


def _place_own(stacks, layers, kinds, dtypes, *, name, deps=()):
    n = len(stacks)
    shapes = [s.shape[1:] for s in stacks]

    def body(*refs):
        s_refs, land_refs, bufs, sems = refs[:n], refs[n:2 * n], refs[2 * n:3 * n], refs[3 * n]
        x, y, c = _my_place()
        copies = []
        for i in range(n):
            bufs[i][...] = s_refs[i][...].astype(dtypes[i])
            copies.append(pltpu.make_async_copy(
                bufs[i], _gather_window(land_refs[i], kinds[i], shapes[i], 4 * x + 2 * y + c), sems.at[i]))
        for cp in copies:
            cp.start()
        for cp in copies:
            cp.wait()

    def layer_of(shape, l):
        return pl.BlockSpec((None,) + shape, lambda i: (l,) + (0,) * len(shape))

    body, dep_specs, dep_args = _with_deps(body, n, deps)
    return pl.pallas_call(
        body, name=name, grid=(1,),
        out_shape=[jax.ShapeDtypeStruct(_gathered_shape(s, k), d) for s, k, d in zip(shapes, kinds, dtypes)],
        in_specs=[layer_of(s, l) for s, l in zip(shapes, layers)] + dep_specs, out_specs=[ANY] * n,
        scratch_shapes=[pltpu.VMEM(s, d) for s, d in zip(shapes, dtypes)] + [pltpu.SemaphoreType.DMA((n,))],
        compiler_params=_params(("arbitrary",)),
    )(*stacks, *dep_args)


def _gather_start(lands, kinds, shapes, after=(), *, name):
    n = len(lands)
    n_after = len(after)

    def body(*refs):
        land_refs = refs[:n]
        send_sems, recv_sems = refs[n + n_after], refs[n + n_after + 1]
        x, y, c = _my_place()
        targets = [(x, y, 1 - c), (1 - x, y, c), (x, 1 - y, c), (1 - x, 1 - y, c)]
        for i in range(n):
            own = _gather_window(land_refs[i], kinds[i], shapes[i], 4 * x + 2 * y + c)
            for k, to in enumerate(targets):
                pltpu.make_async_remote_copy(
                    src_ref=own, dst_ref=own, send_sem=send_sems.at[4 * i + k], recv_sem=recv_sems.at[4 * i + k],
                    device_id=to, device_id_type=MESH).start()
        refs[-1][...] = jnp.zeros_like(refs[-1])

    outs = pl.pallas_call(
        body, name=name,
        out_shape=[pltpu.SemaphoreType.DMA((4 * n,)), pltpu.SemaphoreType.DMA((4 * n,))]
        + [_hbm_like(a.shape, a.dtype) for a in lands] + [TOKEN],
        in_specs=[HBM] * n + [ANY] * n_after, out_specs=[SEM, SEM] + [HBM] * n + [TOKEN_SPEC],
        input_output_aliases={i: 2 + i for i in range(n)},
        compiler_params=SPLIT_PARAMS,
    )(*[_in_hbm(a) for a in lands], *after)
    return outs[0], outs[1], outs[2:2 + n], outs[-1]


def _gather_forward(recv_sems, lands, kinds, shapes, after, *, name):
    n = len(lands)

    def body(*refs):
        recv_ref, land_refs = refs[0], refs[1:1 + n]
        fwd_send, fwd_recv = refs[2 + n], refs[3 + n]
        token = refs[-1]
        x, y, c = _my_place()
        chips = [(1 - x, y), (x, 1 - y), (1 - x, 1 - y)]
        for i in range(n):
            for j, (px, py) in enumerate(chips):
                block = _gather_window(land_refs[i], kinds[i], shapes[i], 4 * px + 2 * py + c)
                pltpu.make_async_remote_copy(
                    src_ref=block, dst_ref=block, send_sem=fwd_send.at[3 * i + j], recv_sem=recv_ref.at[4 * i + 1 + j],
                    device_id=(px, py, c), device_id_type=MESH).wait_recv()
                pltpu.make_async_remote_copy(
                    src_ref=block, dst_ref=block, send_sem=fwd_send.at[3 * i + j], recv_sem=fwd_recv.at[3 * i + j],
                    device_id=(x, y, 1 - c), device_id_type=MESH).start()
        token[...] = jnp.zeros_like(token)

    outs = pl.pallas_call(
        body, name=name,
        out_shape=[pltpu.SemaphoreType.DMA((3 * n,)), pltpu.SemaphoreType.DMA((3 * n,))]
        + [_hbm_like(a.shape, a.dtype) for a in lands] + [TOKEN],
        in_specs=[SEM] + [HBM] * n + [ANY], out_specs=[SEM, SEM] + [HBM] * n + [TOKEN_SPEC],
        input_output_aliases={1 + i: 2 + i for i in range(n)},
        compiler_params=SPLIT_PARAMS,
    )(recv_sems, *lands, after)
    return outs[0], outs[1], outs[2:2 + n], outs[-1]


def _gather_finish(send_sems, recv_sems, fwd_send, fwd_recv, lands, kinds, shapes, after, *, name):
    n = len(lands)

    def body(*refs):
        send_ref, recv_ref, fsend_ref, frecv_ref = refs[:4]
        land_refs = refs[4:4 + n]
        x, y, c = _my_place()
        chips = [(1 - x, y), (x, 1 - y), (1 - x, 1 - y)]
        sibling = (x, y, 1 - c)
        for i in range(n):
            def window(j):
                return _gather_window(land_refs[i], kinds[i], shapes[i], j)

            mine, theirs = window(4 * x + 2 * y + c), window(4 * x + 2 * y + (1 - c))
            pltpu.make_async_remote_copy(src_ref=mine, dst_ref=theirs, send_sem=send_ref.at[4 * i],
                                         recv_sem=recv_ref.at[4 * i], device_id=sibling, device_id_type=MESH).wait_recv()
            for j, (px, py) in enumerate(chips):
                block = window(4 * px + 2 * py + (1 - c))
                pltpu.make_async_remote_copy(src_ref=block, dst_ref=block, send_sem=fsend_ref.at[3 * i + j],
                                             recv_sem=frecv_ref.at[3 * i + j], device_id=sibling,
                                             device_id_type=MESH).wait_recv()
            for k in range(4):
                pltpu.make_async_remote_copy(src_ref=mine, dst_ref=mine, send_sem=send_ref.at[4 * i + k],
                                             recv_sem=recv_ref.at[4 * i + k], device_id=sibling,
                                             device_id_type=MESH).wait_send()
            for j, (px, py) in enumerate(chips):
                block = window(4 * px + 2 * py + c)
                pltpu.make_async_remote_copy(src_ref=block, dst_ref=block, send_sem=fsend_ref.at[3 * i + j],
                                             recv_sem=frecv_ref.at[3 * i + j], device_id=sibling,
                                             device_id_type=MESH).wait_send()

    return pl.pallas_call(
        body, name=name,
        out_shape=[_hbm_like(a.shape, a.dtype) for a in lands],
        in_specs=[SEM] * 4 + [HBM] * n + [ANY], out_specs=[HBM] * n,
        input_output_aliases={4 + i: i for i in range(n)},
        compiler_params=SPLIT_PARAMS,
    )(send_sems, recv_sems, fwd_send, fwd_recv, *lands, after)


def _pair_plan(src_ref, land_ref, x, y, c):
    return [(src_ref.at[2 * k + (1 - c)], land_ref.at[k], (x, y, 1 - c)) for k in range(N_CHIPS)]


def _chip_plan(src_ref, land_ref, x, y, c):
    chips = [(1 - x, y), (x, 1 - y), (1 - x, 1 - y)]
    return [(src_ref.at[2 * px + py], land_ref.at[k], (px, py, c)) for k, (px, py) in enumerate(chips)]


def _exchange_copies(plan, per, src_refs, land_refs, send_sems, recv_sems):
    x, y, c = _my_place()
    copies = []
    for i, (s_ref, l_ref) in enumerate(zip(src_refs, land_refs)):
        for q, (src, dst, to) in enumerate(plan(s_ref, l_ref, x, y, c)):
            copies.append(pltpu.make_async_remote_copy(
                src_ref=src, dst_ref=dst, send_sem=send_sems.at[per * i + q], recv_sem=recv_sems.at[per * i + q],
                device_id=to, device_id_type=MESH))
    return copies


def _exchange_start(srcs, plan, per, *, name):
    n = len(srcs)

    def body(*refs):
        src_refs, land_refs = refs[:n], refs[n:2 * n]
        send_sems, recv_sems = refs[2 * n], refs[2 * n + 1]
        for cp in _exchange_copies(plan, per, src_refs, land_refs, send_sems, recv_sems):
            cp.start()
        refs[-1][...] = jnp.zeros_like(refs[-1])

    lands = [lax.empty((per,) + s.shape[1:], s.dtype) for s in srcs]
    outs = pl.pallas_call(
        body, name=name,
        out_shape=[pltpu.SemaphoreType.DMA((per * n,)), pltpu.SemaphoreType.DMA((per * n,))]
        + [_hbm_like(s.shape, s.dtype) for s in srcs] + [_hbm_like(a.shape, a.dtype) for a in lands] + [TOKEN],
        in_specs=[HBM] * (2 * n), out_specs=[SEM, SEM] + [HBM] * (2 * n) + [TOKEN_SPEC],
        input_output_aliases={i: 2 + i for i in range(2 * n)},
        compiler_params=SPLIT_PARAMS,
    )(*[_in_hbm(s) for s in srcs], *[_in_hbm(a) for a in lands])
    return outs[0], outs[1], outs[2:2 + n], outs[2 + n:2 + 2 * n], outs[-1]


def _exchange_wait(send_sems, recv_sems, srcs, lands, plan, per, after, *, name):
    n = len(srcs)
    after = list(after) if isinstance(after, (list, tuple)) else [after]

    def body(*refs):
        send_ref, recv_ref = refs[0], refs[1]
        src_refs, land_refs = refs[2:2 + n], refs[2 + n:2 + 2 * n]
        copies = _exchange_copies(plan, per, src_refs, land_refs, send_ref, recv_ref)
        for cp in copies:
            cp.wait_recv()
        for cp in copies:
            cp.wait_send()

    outs = pl.pallas_call(
        body, name=name,
        out_shape=[_hbm_like(s.shape, s.dtype) for s in srcs] + [_hbm_like(a.shape, a.dtype) for a in lands],
        in_specs=[SEM, SEM] + [HBM] * (2 * n) + [ANY] * len(after), out_specs=[HBM] * (2 * n),
        input_output_aliases={2 + i: i for i in range(2 * n)},
        compiler_params=SPLIT_PARAMS,
    )(send_sems, recv_sems, *srcs, *lands, *after)
    return outs[:n], outs[n:]


REDUCE_BLOCK_BYTES = 2 << 20


def _row_tile(r, c):
    row_bytes = 4 * (-(-c // LANES) * LANES)
    best = r
    for d in range(SUBLANES, r, SUBLANES):
        if r % d == 0 and d * row_bytes <= REDUCE_BLOCK_BYTES:
            best = d
    return best if r * row_bytes > REDUCE_BLOCK_BYTES else r


def _reduce_pair_sum(blocked, recv, place, wire_dtype, *, name):
    _, r, c = blocked.shape
    tr = _row_tile(r, c)

    def body(place_ref, g_ref, r_ref, own_ref, send_ref):
        s = g_ref[...] + r_ref[...]
        send_ref[...] = s.astype(wire_dtype)

        @pl.when(pl.program_id(1) == place_ref[1])
        def _():
            own_ref[...] = s

    return pl.pallas_call(
        body, name=name,
        grid_spec=pltpu.PrefetchScalarGridSpec(
            num_scalar_prefetch=1, grid=(r // tr, N_CHIPS),
            in_specs=[pl.BlockSpec((None, None, tr, c), lambda i, k, place_ref: (k, place_ref[0], i, 0)),
                      pl.BlockSpec((None, tr, c), lambda i, k, place_ref: (k, i, 0))],
            out_specs=[pl.BlockSpec((tr, c), lambda i, k, place_ref: (i, 0)),
                       pl.BlockSpec((None, tr, c), lambda i, k, place_ref: (k, i, 0))]),
        out_shape=[jax.ShapeDtypeStruct((r, c), F32), jax.ShapeDtypeStruct((N_CHIPS, r, c), wire_dtype)],
        compiler_params=_params(("parallel", "arbitrary")),
    )(place, blocked.reshape(N_CHIPS, 2, r, c), recv)


def _chip_sum(own_ref, r_ref):
    return ((own_ref[...] + r_ref[0].astype(F32)) + r_ref[1].astype(F32)) + r_ref[2].astype(F32)


def _reduce_chip_sum(own, recv, *, name):
    r, c = own.shape
    tr = _row_tile(r, c)

    def body(own_ref, r_ref, o_ref):
        o_ref[...] = _chip_sum(own_ref, r_ref)

    return pl.pallas_call(
        body, name=name, grid=(r // tr,),
        in_specs=[pl.BlockSpec((tr, c), lambda i: (i, 0)), pl.BlockSpec((N_CHIPS - 1, tr, c), lambda i: (0, i, 0))],
        out_specs=pl.BlockSpec((tr, c), lambda i: (i, 0)),
        out_shape=jax.ShapeDtypeStruct((r, c), F32),
        compiler_params=_params(("parallel",)),
    )(own, recv)


def _adamw_math(w, g, m, v):
    nm = ADAM_B1 * m + (1.0 - ADAM_B1) * g
    nv = ADAM_B2 * v + (1.0 - ADAM_B2) * (g * g)
    m_hat = nm / (1.0 - ADAM_B1 ** ADAM_STEP)
    v_hat = nv / (1.0 - ADAM_B2 ** ADAM_STEP)
    return -ADAM_LR * (m_hat / (jnp.sqrt(v_hat) + ADAM_EPS) + ADAM_WD * w), nm, nv


def _adamw(w, g, m, v, *, name):
    shape = w.shape
    C = shape[-1]
    R = math.prod(shape[:-1])
    tr = _row_tile(R, C)

    def body(w_ref, g_ref, m_ref, v_ref, d_ref, nm_ref, nv_ref):
        d_ref[...], nm_ref[...], nv_ref[...] = _adamw_math(w_ref[...], g_ref[...], m_ref[...], v_ref[...])

    spec = pl.BlockSpec((tr, C), lambda i: (i, 0))
    outs = pl.pallas_call(
        body, name=name, grid=(R // tr,),
        in_specs=[spec] * 4, out_specs=[spec] * 3,
        out_shape=[jax.ShapeDtypeStruct((R, C), F32)] * 3,
        compiler_params=_params(("parallel",)),
    )(*[a.reshape(R, C) for a in (w, g, m, v)])
    return tuple(o.reshape(shape) for o in outs)


def _reduce_adamw(own, recv, w, m, v, layer, prev, *, name):
    r, c = own.shape
    tr = _row_tile(r, c)
    n_prev = 0 if prev is None else len(prev)

    def body(own_ref, r_ref, w_ref, m_ref, v_ref, *rest):
        g_ref, d_ref, nm_ref, nv_ref = rest[n_prev:]
        g = _chip_sum(own_ref, r_ref)
        g_ref[...] = g
        d_ref[...], nm_ref[...], nv_ref[...] = _adamw_math(w_ref[...], g, m_ref[...], v_ref[...])

    slot = pl.BlockSpec((None, tr, c), lambda i: (layer, i, 0))
    return pl.pallas_call(
        body, name=name, grid=(r // tr,),
        in_specs=[pl.BlockSpec((tr, c), lambda i: (i, 0)), pl.BlockSpec((N_CHIPS - 1, tr, c), lambda i: (0, i, 0)),
                  slot, slot, slot] + [ANY] * n_prev,
        out_specs=[slot] * 4,
        out_shape=[jax.ShapeDtypeStruct((DEPTH, r, c), F32)] * 4,
        input_output_aliases={5 + k: k for k in range(n_prev)},
        compiler_params=_params(("parallel",)),
    )(own, recv, w, m, v, *(prev or ()))


REPLICATED = (("mix_norm", (D_MODEL,)), ("q_norm", (HEAD_DIM,)), ("k_norm", (HEAD_DIM,)), ("sinks", (N_Q_HEADS,)),
              ("sgu_norm", (SGU_WIDTH,)), ("w_s", (SGU_GROUPS, BLOCK, BLOCK)), ("b_s", (SGU_GROUPS, BLOCK)),
              ("ffn_norm", (D_MODEL,)), ("conv_b", (2 * D_FF,)))
TRANSPOSED = ("w_in", "w_up")
SHARDED = (("w_in", "rows"), ("w_oa", "cols"), ("w_ob", "cols"), ("w_out", "rows"), ("w_up", "rows"),
           ("conv_w", "blocks"), ("w_down", "rows"))
WEIGHT_ORDER = ("mix_norm", "w_in", "q_norm", "k_norm", "sinks", "sgu_norm", "w_s", "b_s", "w_oa", "w_ob", "w_out",
                "ffn_norm", "w_up", "conv_w", "conv_b", "w_down")
MIXER_WEIGHTS = ["w_in", "w_oa", "w_ob", "w_out"]
FFN_WEIGHTS = ["w_up", "conv_w", "w_down"]


def _small_layout():
    segs, off = {}, 0
    for l in range(DEPTH):
        for name, shape in REPLICATED:
            n = math.prod(shape)
            segs[(l, name)] = (off, n)
            off += n
    per_dev = -(-off // (N_DEV * SUBLANES * LANES)) * SUBLANES * LANES
    return segs, off, per_dev


def _pack_small(grads, loss_part):
    ssegs, total, per_dev = _small_layout()
    flat = jnp.concatenate([grads[l][name].reshape(-1) for (l, name) in ssegs] + [loss_part.reshape(1)])
    return jnp.pad(flat, (0, N_DEV * per_dev - total - 1)).reshape(N_DEV, per_dev // LANES, LANES)


def _unpack_small(gathered):
    ssegs, total, _ = _small_layout()
    flat = gathered.reshape(-1)
    shapes = dict(REPLICATED)
    small = {name: jnp.stack([flat[ssegs[(l, name)][0]:ssegs[(l, name)][0] + ssegs[(l, name)][1]].reshape(shapes[name])
                              for l in range(DEPTH)]) for name, _ in REPLICATED}
    return small, flat[total]


def kernel(x, mix_norm, w_in, q_norm, k_norm, sinks, sgu_norm, w_s, b_s, w_oa, w_ob, w_out, ffn_norm, w_up, conv_w, conv_b, w_down, loss_target, m_mix_norm, m_w_in, m_q_norm, m_k_norm, m_sinks, m_sgu_norm, m_w_s, m_b_s, m_w_oa, m_w_ob, m_w_out, m_ffn_norm, m_w_up, m_conv_w, m_conv_b, m_w_down, v_mix_norm, v_w_in, v_q_norm, v_k_norm, v_sinks, v_sgu_norm, v_w_s, v_b_s, v_w_oa, v_w_ob, v_w_out, v_ffn_norm, v_w_up, v_conv_w, v_conv_b, v_w_down):
    W = dict(mix_norm=mix_norm, w_in=w_in, q_norm=q_norm, k_norm=k_norm, sinks=sinks, sgu_norm=sgu_norm, w_s=w_s, b_s=b_s,
             w_oa=w_oa, w_ob=w_ob, w_out=w_out, ffn_norm=ffn_norm, w_up=w_up, conv_w=conv_w, conv_b=conv_b, w_down=w_down)
    M = dict(mix_norm=m_mix_norm, w_in=m_w_in, q_norm=m_q_norm, k_norm=m_k_norm, sinks=m_sinks, sgu_norm=m_sgu_norm,
             w_s=m_w_s, b_s=m_b_s, w_oa=m_w_oa, w_ob=m_w_ob, w_out=m_w_out, ffn_norm=m_ffn_norm, w_up=m_w_up,
             conv_w=m_conv_w, conv_b=m_conv_b, w_down=m_w_down)
    V = dict(mix_norm=v_mix_norm, w_in=v_w_in, q_norm=v_q_norm, k_norm=v_k_norm, sinks=v_sinks, sgu_norm=v_sgu_norm,
             w_s=v_w_s, b_s=v_b_s, w_oa=v_w_oa, w_ob=v_w_ob, w_out=v_w_out, ffn_norm=v_ffn_norm, w_up=v_w_up,
             conv_w=v_conv_w, conv_b=v_conv_b, w_down=v_w_down)
    n_seq, seq, d_model = x.shape
    tokens = n_seq * seq
    mx, my, mc = _my_place()
    place = jnp.stack([mc, 2 * mx + my]).astype(jnp.int32)
    half = N_DEV // 2
    kind_of = dict(SHARDED)
    for name in TRANSPOSED:
        W[name], M[name], V[name] = (jnp.swapaxes(t[name], 1, 2) for t in (W, M, V))

    gather_groups = [[(0, MIXER_WEIGHTS[0])], [(0, n) for n in MIXER_WEIGHTS[1:]], [(0, n) for n in FFN_WEIGHTS],
                     [(1, n) for n in MIXER_WEIGHTS], [(1, n) for n in FFN_WEIGHTS]]
    started, in_flight = {}, {}
    weights = []
    for l in range(DEPTH):
        w = {name: W[name][l] for name, _ in REPLICATED}
        w["cb_g"], w["cb_v"] = W["conv_b"][l][:D_FF], W["conv_b"][l][D_FF:]
        w["bias_full"] = jnp.repeat(W["b_s"][l].T, SGU_WIDTH // SGU_GROUPS, axis=1)
        weights.append(w)

    def gather_start(gi, after=()):
        stacks = [W[name] for _, name in gather_groups[gi]]
        kinds = [kind_of[name] for _, name in gather_groups[gi]]
        shapes = [s.shape[1:] for s in stacks]
        lands = _place_own(stacks, [l for l, _ in gather_groups[gi]], kinds,
                           [F32 if name == "conv_w" else BF16 for _, name in gather_groups[gi]],
                           name=f"gather_weights_own_{gi}", deps=after)
        send, recv, lands, token = _gather_start(lands, kinds, shapes, after, name=f"gather_weights_start_{gi}")
        started[gi] = dict(sems=(send, recv), lands=lands, kinds=kinds, shapes=shapes)
        return token

    def gather_forward(gi, after):
        st = started[gi]
        in_flight[gi] = _gather_forward(st["sems"][1], st["lands"], st["kinds"], st["shapes"], after,
                                        name=f"gather_weights_forward_{gi}")
        return in_flight[gi][3]

    def gather_finish(gi, after):
        st = started.pop(gi)
        fwd_send, fwd_recv, lands_g, _ = in_flight.pop(gi)
        whole = _gather_finish(st["sems"][0], st["sems"][1], fwd_send, fwd_recv, lands_g, st["kinds"], st["shapes"], after,
                               name=f"gather_weights_finish_{gi}")
        for (l, name), arr in zip(gather_groups[gi], whole):
            w = weights[l]
            if name in TRANSPOSED:
                w[name + "_t"] = arr
            elif name == "conv_w":
                w["cw_g"] = arr[:half].transpose(1, 0, 2).reshape(3, D_FF)
                w["cw_v"] = arr[half:].transpose(1, 0, 2).reshape(3, D_FF)
            else:
                w[name] = arr

    reduce_state, results = {}, {}
    wire = {"conv_w": F32, "small": F32}

    def reduce_begin(key, names, arrays):
        send, recv, srcs_, lands_, token = _exchange_start(arrays, _pair_plan, N_CHIPS, name=f"reduce_pair_start_{key}")
        reduce_state[key] = dict(names=names, pair=(send, recv, srcs_, lands_))
        return [token]

    def reduce_pair(key, after):
        st = reduce_state[key]
        send, recv, srcs_, lands_ = st.pop("pair")
        blocked_, from_sibling = _exchange_wait(send, recv, srcs_, lands_, _pair_plan, N_CHIPS, after,
                                                name=f"reduce_pair_wait_{key}")
        sums = [_reduce_pair_sum(b, r, place, wire.get(n if isinstance(n, str) else n[1], BF16),
                                 name=f"reduce_pair_sum_{key}_{i}")
                for i, (n, b, r) in enumerate(zip(st["names"], blocked_, from_sibling))]
        st["own"] = [s[0] for s in sums]
        *st["chip"], token = _exchange_start([s[1] for s in sums], _chip_plan, N_CHIPS - 1, name=f"reduce_chip_start_{key}")
        return [token]

    def reduce_end(key, after):
        st = reduce_state.pop(key)
        send, recv, srcs_, lands_ = st["chip"]
        _, from_chips = _exchange_wait(send, recv, srcs_, lands_, _chip_plan, N_CHIPS - 1, after,
                                       name=f"reduce_chip_wait_{key}")
        done = []
        for n, own, got in zip(st["names"], st["own"], from_chips):
            if n == "small":
                results["small"] = _reduce_chip_sum(own, got, name="reduce_chip_sum_small")
            else:
                l, name = n
                results[name] = _reduce_adamw(own, got, W[name], M[name], V[name], l, results.get(name),
                                              name=f"l{l}_reduce_adamw_{name}")
                done.append(results[name][0])
        return done

    def sched(point, l, carry, g=None):
        deps = []
        if point == "begin":
            token = ()
            for gi in range(len(gather_groups)):
                token = [gather_start(gi, token)]
            deps = [gather_forward(0, token[0])]
        elif point == "fwd_start" and l == 0:
            gather_finish(0, carry)
            deps = [gather_forward(1, weights[0]["w_in_t"])]
        elif point == "fwd_att" and l == 0:
            gather_finish(1, carry)
            deps = [gather_forward(2, carry)]
        elif point == "fwd_mixer_done" and l == 0:
            gather_finish(2, carry)
        elif point == "fwd_conv" and l == 0:
            deps = [gather_forward(3, carry)]
        elif point == "fwd_start" and l == 1:
            gather_finish(3, carry)
        elif point == "fwd_att" and l == 1:
            deps = [gather_forward(4, carry)]
        elif point == "fwd_mixer_done" and l == 1:
            gather_finish(4, carry)
        elif point == "bwd_ffn_grads":
            conv_w = jnp.concatenate([g[k].reshape(3, half, W_UP_SHARD).transpose(1, 0, 2) for k in ("cw_g", "cw_v")])
            deps = reduce_begin(
                f"l{l}_ffn", [(l, "w_down"), (l, "w_up"), (l, "conv_w")],
                [g["w_down"].reshape(N_DEV, D_FF // N_DEV, D_MODEL),
                 g["w_up_t"].reshape(N_DEV, W_UP_SHARD, D_MODEL), conv_w])
        elif point == "bwd_merge":
            deps = reduce_pair(f"l{l}_ffn", carry)
        elif point == "bwd_out_grads":
            deps = reduce_begin(
                f"l{l}_out", [(l, "w_out"), (l, "w_oa"), (l, "w_ob")],
                [g["w_out"].reshape(N_DEV, D_MODEL // N_DEV, D_MODEL),
                 _disassemble((g["w_oa"],), LANES, _w_o_moves(), name=f"l{l}_split_dw_oa"),
                 _disassemble((g["w_ob"],), LANES, _w_o_moves(), name=f"l{l}_split_dw_ob")])
        elif point == "bwd_att":
            deps = reduce_pair(f"l{l}_out", carry)
        elif point == "bwd_w_in_grad":
            deps = reduce_begin(f"l{l}_in", [(l, "w_in")], [g["w_in_t"].reshape(N_DEV, W_IN_SHARD, D_MODEL)])
        elif point == "bwd_dh":
            deps = reduce_pair(f"l{l}_in", carry)
        return deps

    loss_part, dx, grads, last_deps = _local_step(x.reshape(tokens, d_model), loss_target.reshape(tokens, d_model),
                                                  weights, sched, n_seq=n_seq, seq=seq)
    for g in grads:
        g["conv_b"] = jnp.concatenate([g["cb_g"], g["cb_v"]])
    after = [dx, *last_deps, *reduce_begin("small", ["small"], [_pack_small(grads, loss_part)])]
    for key in [f"l{l}_{part}" for l in reversed(range(DEPTH)) for part in ("ffn", "out", "in")][:-1]:
        after = reduce_end(key, after)
    after = reduce_end("l0_in", after + reduce_pair("small", after))
    reduce_end("small", after)

    G, delta, new_m, new_v = {}, {}, {}, {}
    for name, _ in SHARDED:
        outs = [jnp.swapaxes(o, 1, 2) for o in results[name]] if name in TRANSPOSED else results[name]
        G[name], delta[name], new_m[name], new_v[name] = outs
    small, loss = _unpack_small(_gather([results["small"]], ["blocks"], name="gather_small_grads")[0])
    G.update(small)
    for name, _ in REPLICATED:
        delta[name], new_m[name], new_v[name] = _adamw(W[name], G[name], M[name], V[name], name=f"adamw_{name}")
    return (loss, dx.reshape(n_seq, seq, d_model), *[G[n] for n in WEIGHT_ORDER], *[delta[n] for n in WEIGHT_ORDER],
            *[new_m[n] for n in WEIGHT_ORDER], *[new_v[n] for n in WEIGHT_ORDER])
```

```python
import math

import jax
import jax.numpy as jnp
from jax import lax
from jax.experimental import pallas as pl
from jax.experimental.pallas import tpu as pltpu

F32 = jnp.float32
BF16 = jnp.bfloat16
ACT_DTYPE = BF16
MESH = pl.DeviceIdType.MESH

DEPTH = 2
D_MODEL = 1024
N_Q_HEADS = 8
HEAD_DIM = 64
ATT_WIDTH = 512
KV_WIDTH = 128
BLOCK = 128
SGU_WIDTH = 512
SGU_GROUPS = 8
IN_WIDTH = 3840
D_FF = 2816
NORM_EPS = 1e-6
NEG_INF = -1e30
ATT_SCALE = HEAD_DIM ** -0.5
ALIBI_SLOPES = tuple(2.0 ** (-(h + 1)) for h in range(N_Q_HEADS))
ADAM_LR, ADAM_B1, ADAM_B2, ADAM_EPS, ADAM_WD, ADAM_STEP = 0.001, 0.9, 0.999, 1e-08, 0.01, 10
N_DEV = 8
N_CHIPS = 4

QKV_WIDTH = ATT_WIDTH + 2 * KV_WIDTH
COL_SUV, COL_GA, COL_GB, COL_QKV = 0, 1024, 2048, 3072
W_IN_ROTATE = (1, IN_WIDTH // QKV_WIDTH)

LANES = 128
SUBLANES = 8
VMEM_LIMIT_V7X = 56 * 1024 * 1024
GELU_C = math.sqrt(2.0 / math.pi)
GELU_K = 0.044715
ANY = pl.BlockSpec(memory_space=pl.ANY)


def _params(sem=None):
    return pltpu.CompilerParams(dimension_semantics=sem, vmem_limit_bytes=VMEM_LIMIT_V7X)


def _sigmoid(x):
    return 1.0 / (1.0 + jnp.exp(-x))


def _gelu(x):
    th = jnp.tanh(GELU_C * (x + GELU_K * x * x * x))
    return 0.5 * x * (1.0 + th)


def _gelu_and_grad(x):
    x2 = x * x
    th = jnp.tanh(GELU_C * (x + GELU_K * x2 * x))
    g = 0.5 * x * (1.0 + th)
    dg = 0.5 * (1.0 + th) + 0.5 * x * (1.0 - th * th) * (GELU_C * (1.0 + 3.0 * GELU_K * x2))
    return g, dg


def _dot(a, b, dims):
    return lax.dot_general(a, b, (dims, ((), ())), preferred_element_type=F32)


def _dot_nn(a, b):
    return _dot(a, b, ((1,), (0,)))


def _dot_nt(a, b):
    return _dot(a, b, ((1,), (1,)))


def _dot_tn(a, b):
    return _dot(a, b, ((0,), (0,)))


def _lo_mask(shape):
    return lax.broadcasted_iota(jnp.int32, shape, len(shape) - 1) < (LANES // 2)


def _half_sums(x, lo):
    s_lo = jnp.sum(jnp.where(lo, x, 0.0), axis=-1, keepdims=True)
    s_all = jnp.sum(x, axis=-1, keepdims=True)
    return jnp.where(lo, s_lo, s_all - s_lo)


def _dup_half(x, half, lo):
    r = pltpu.roll(x, LANES // 2, axis=1)
    return jnp.where(lo, x, r) if half == 0 else jnp.where(lo, r, x)


def _with_deps(body, n_in, deps):
    k = len(deps)
    if not k:
        return body, [], ()

    def skipping(*refs):
        return body(*refs[:n_in], *refs[n_in + k:])

    return skipping, [ANY] * k, tuple(deps)


MM_VMEM_BUDGET = 40 * 1024 * 1024
MM_MAX_TILE = 1408
MM_MAX_TK = 4096
MM_STEP_BYTES = 1 << 20


def _divisors(n, step, cap):
    return [d for d in range(step, min(n, cap) + 1, step) if n % d == 0] or [n]


def _mm_tiles(M, N, K, out_bytes, tm_divides, tn_divides):
    best = None
    for tm in _divisors(M, LANES, MM_MAX_TILE):
        for tn in _divisors(N, LANES, MM_MAX_TILE):
            if tm_divides % tm or tn_divides % tn:
                continue
            for tk in _divisors(K, 4 * LANES, MM_MAX_TK):
                vmem = 4 * (tm * tk + tk * tn) + 2 * tm * tn * out_bytes + (0 if tk == K else 4 * tm * tn)
                if vmem > MM_VMEM_BUDGET:
                    continue
                traffic = 2 * M * K * (N // tn) + 2 * K * N * (M // tm) + M * N * out_bytes
                cost = traffic + (K // tk - 1) * 8 * M * N + (M // tm) * (N // tn) * (K // tk) * MM_STEP_BYTES
                if best is None or cost < best[0]:
                    best = (cost, tm, tn, tk)
    assert best is not None, (M, N, K)
    return best[1:]


def _mm(a, b, *, mode, out_dtype, name, deps=(), b_rows=(0, None), rotate=None, out_rows=(0, None), out_prev=None):
    b_first, b_count = b_rows
    if mode == "nn":
        (M, K), N = a.shape, b.shape[1]
    elif mode == "nt":
        (M, K), N = a.shape, (b.shape[0] if b_count is None else b_count)
    else:
        (K, M), N = a.shape, b.shape[1]
    shift, period = rotate or (0, 1)
    assert period == 1 or mode == "nt"
    out_first, out_total = out_rows[0], (M if out_rows[1] is None else out_rows[1])
    tm, tn, tk = _mm_tiles(M, N, K, jnp.dtype(out_dtype).itemsize, math.gcd(M, out_first),
                           math.gcd(N // period, b_first if mode == "nt" else 0))
    gm, gn, gk = M // tm, N // tn, K // tk

    def turned(j):
        per = N // period // tn
        return ((j // per + shift) % period) * per + j % per if period > 1 else j

    if mode == "nn":
        a_spec = pl.BlockSpec((tm, tk), lambda i, j, k: (i, k))
        b_spec = pl.BlockSpec((tk, tn), lambda i, j, k: (k + b_first // tk, j))
        contract = ((1,), (0,))
    elif mode == "nt":
        a_spec = pl.BlockSpec((tm, tk), lambda i, j, k: (i, k))
        b_spec = pl.BlockSpec((tn, tk), lambda i, j, k: (turned(j) + b_first // tn, k))
        contract = ((1,), (1,))
    else:
        a_spec = pl.BlockSpec((tk, tm), lambda i, j, k: (k, i))
        b_spec = pl.BlockSpec((tk, tn), lambda i, j, k: (k, j))
        contract = ((0,), (0,))
    o_spec = pl.BlockSpec((tm, tn), lambda i, j, k: (i + out_first // tm, j))
    assert b_first % (tk if mode == "nn" else tn) == 0 and out_first % tm == 0, (name, tm, tn, tk)
    n_prev = 0 if out_prev is None else 1

    def body(a_ref, b_ref, *rest):
        o_ref = rest[n_prev]
        part = _dot(a_ref[...].astype(BF16), b_ref[...].astype(BF16), contract)
        if gk == 1:
            o_ref[...] = part.astype(out_dtype)
            return
        acc_ref = rest[n_prev + 1]
        k = pl.program_id(2)

        @pl.when(k == 0)
        def _():
            acc_ref[...] = part

        @pl.when(k > 0)
        def _():
            acc_ref[...] += part

        @pl.when(k == gk - 1)
        def _():
            o_ref[...] = acc_ref[...].astype(out_dtype)

    body, dep_specs, dep_args = _with_deps(body, 2 + n_prev, deps)
    return pl.pallas_call(
        body,
        name=name,
        grid=(gm, gn, gk),
        in_specs=[a_spec, b_spec] + [ANY] * n_prev + dep_specs,
        out_specs=o_spec,
        out_shape=jax.ShapeDtypeStruct((out_total, N), out_dtype),
        input_output_aliases={2: 0} if n_prev else {},
        scratch_shapes=[] if gk == 1 else [pltpu.VMEM((tm, tn), F32)],
        compiler_params=_params(("parallel", "parallel", "arbitrary")),
    )(a, b, *([out_prev] if n_prev else []), *dep_args)


def _mm_tn_parts(parts, at, b, *, name):
    K, N = b.shape
    n = len(parts)
    tm = math.gcd(*[p.shape[1] for p in parts], *at)
    tiles = [p.shape[1] // tm for p in parts]
    first = [sum(tiles[:p]) for p in range(n)]

    def mine(i, p):
        return jnp.logical_and(i >= first[p], i < first[p] + tiles[p])

    def out_tile(i):
        t = 0
        for p in range(n):
            t = jnp.where(mine(i, p), at[p] // tm + i - first[p], t)
        return t

    def body(*refs):
        a_refs, b_ref, o_ref = refs[:n], refs[n], refs[n + 1]
        for p in range(n):
            @pl.when(mine(pl.program_id(0), p))
            def _(p=p):
                o_ref[...] = _dot_tn(a_refs[p][...], b_ref[...])

    return pl.pallas_call(
        body, name=name, grid=(sum(tiles),),
        in_specs=[pl.BlockSpec((K, tm), lambda i, p=p: (0, jnp.clip(i - first[p], 0, tiles[p] - 1))) for p in range(n)]
        + [pl.BlockSpec((K, N), lambda i: (0, 0), pipeline_mode=pl.Buffered(1))],
        out_specs=pl.BlockSpec((tm, N), lambda i: (out_tile(i), 0)),
        out_shape=jax.ShapeDtypeStruct((sum(p.shape[1] for p in parts), N), F32),
        compiler_params=_params(("arbitrary",)),
    )(*parts, b)


def _mm_rows(a, b, *, mode, fn, out_dtypes, rows=(), vecs=(), reduce=False, name, deps=(), b_rows=(0, None), a_at=None):
    parts = a if a_at is not None else (a,)
    starts = a_at if a_at is not None else (0,)
    n_parts = len(parts)
    M, K = parts[0].shape[0], sum(p.shape[1] for p in parts)
    b_first, b_count = b_rows[0], (b.shape[0] if b_rows[1] is None else b_rows[1])
    N = b.shape[1] if mode == "nn" else b_count
    contract = ((1,), (0,)) if mode == "nn" else ((1,), (1,))
    n_rows, n_vecs, n_out = len(rows), len(vecs), len(out_dtypes)
    out_bytes = sum(jnp.dtype(d).itemsize for d in out_dtypes)
    tm = max(t for t in _divisors(M, LANES, MM_MAX_TILE)
             if 4 * t * K + 2 * K * N + 2 * t * N * (4 * n_rows + out_bytes) <= MM_VMEM_BUDGET)
    assert b_first % b_count == 0 and (a_at is None or mode == "nn")

    def body(*refs):
        a_refs, b_ref, rest = refs[:n_parts], refs[n_parts], refs[n_parts + 1:]
        row_refs, vec_refs = rest[:n_rows], rest[n_rows:n_rows + n_vecs]
        out_refs = rest[n_rows + n_vecs:]
        if a_at is None:
            acc = _dot(a_refs[0][...], b_ref[...], contract)
        else:
            acc = sum(_dot(r[...], b_ref[at:at + r.shape[1], :], contract) for r, at in zip(a_refs, starts))
        res = fn(acc, *[r[...] for r in row_refs], *[v[...] for v in vec_refs])
        for o_ref, val in zip(out_refs[:n_out], res):
            o_ref[...] = val.astype(o_ref.dtype)
        if reduce:
            @pl.when(pl.program_id(0) == 0)
            def _():
                out_refs[n_out][...] = res[n_out]

            @pl.when(pl.program_id(0) > 0)
            def _():
                out_refs[n_out][...] += res[n_out]

    row = pl.BlockSpec((tm, N), lambda i: (i, 0))
    vec = pl.BlockSpec((1, N), lambda i: (0, 0))
    body, dep_specs, dep_args = _with_deps(body, n_parts + 1 + n_rows + n_vecs, deps)
    return pl.pallas_call(
        body, name=name, grid=(M // tm,),
        in_specs=[pl.BlockSpec((tm, p.shape[1]), lambda i: (i, 0)) for p in parts]
        + [pl.BlockSpec((b_count, b.shape[1]), lambda i: (b_first // b_count, 0), pipeline_mode=pl.Buffered(1))]
        + [row] * n_rows + [vec] * n_vecs + dep_specs,
        out_specs=[row] * n_out + [vec] * reduce,
        out_shape=[jax.ShapeDtypeStruct((M, N), d) for d in out_dtypes] + [jax.ShapeDtypeStruct((1, N), F32)] * reduce,
        compiler_params=_params(("arbitrary",)),
    )(*parts, b, *rows, *[v.reshape(1, N) for v in vecs], *dep_args)


def _rms(x, gain):
    return x * lax.rsqrt(jnp.mean(x * x, axis=-1, keepdims=True) + NORM_EPS) * gain


def _residual_then_norm(acc, x, gain):
    x_out = x + acc
    return x_out, _rms(x_out, gain)


def _residual_then_loss(acc, x, target):
    err = (x + acc) - target
    dy = err * (1.0 / D_MODEL)
    return dy, dy, jnp.sum(err * err, axis=0, keepdims=True) * (0.5 / D_MODEL)


def _rms_bwd_rows(dh, x, dres, gain):
    r = lax.rsqrt(jnp.mean(x * x, axis=-1, keepdims=True) + NORM_EPS)
    xh = x * r
    dxh = dh * gain
    dx = dres + r * (dxh - xh * jnp.mean(dxh * xh, axis=-1, keepdims=True))
    return dx, dx, jnp.sum(dh * xh, axis=0, keepdims=True)


def _rms_fwd(x, gain, *, name, tm=512, deps=()):
    T, D = x.shape

    def body(x_ref, g_ref, h_ref):
        xv = x_ref[...]
        r = lax.rsqrt(jnp.mean(xv * xv, axis=-1, keepdims=True) + NORM_EPS)
        h_ref[...] = (xv * r * g_ref[...]).astype(BF16)

    body, dep_specs, dep_args = _with_deps(body, 2, deps)
    return pl.pallas_call(
        body, name=name, grid=(T // tm,),
        in_specs=[pl.BlockSpec((tm, D), lambda i: (i, 0)), pl.BlockSpec((1, D), lambda i: (0, 0))] + dep_specs,
        out_specs=pl.BlockSpec((tm, D), lambda i: (i, 0)),
        out_shape=jax.ShapeDtypeStruct((T, D), BF16),
        compiler_params=_params(("parallel",)),
    )(x, gain.reshape(1, D), *dep_args)


def _head_norm(x, gain2, lo):
    ms = _half_sums(x * x, lo) * (1.0 / HEAD_DIM)
    r = lax.rsqrt(ms + NORM_EPS)
    xh = x * r
    return xh * gain2, xh, r


def _head_norm_bwd(xh, r, gain2, dy, lo):
    dxh = dy * gain2
    dx = r * (dxh - xh * (_half_sums(dxh * xh, lo) * (1.0 / HEAD_DIM)))
    return dx, dy * xh


Q_GROUP = N_Q_HEADS // 2
GROUP_ROWS = Q_GROUP * BLOCK
ATT_SCRATCH = (pltpu.VMEM((2, 2, GROUP_ROWS, BLOCK), F32), pltpu.VMEM((2, GROUP_ROWS, 1), F32))


def _att_consts(sink_ref, bias_ref, sinkcol_ref):
    row = lax.broadcasted_iota(jnp.int32, (GROUP_ROWS, BLOCK), 0)
    kj = lax.broadcasted_iota(jnp.int32, (GROUP_ROWS, BLOCK), 1)
    head = row // BLOCK
    head_col = lax.broadcasted_iota(jnp.int32, (GROUP_ROWS, 1), 0) // BLOCK
    d_cur = (row % BLOCK) - kj
    d_prev = d_cur + BLOCK
    for kv in range(2):
        slope = jnp.zeros((GROUP_ROWS, BLOCK), F32)
        sink = jnp.zeros((GROUP_ROWS, 1), F32)
        for r in range(Q_GROUP):
            slope = jnp.where(head == r, ALIBI_SLOPES[Q_GROUP * kv + r], slope)
            sink = jnp.where(head_col == r, sink_ref[Q_GROUP * kv + r], sink)
        bias_ref[kv, 0] = jnp.where(d_cur >= 0, -slope * d_cur.astype(F32), NEG_INF)
        bias_ref[kv, 1] = jnp.where(d_prev < BLOCK, -slope * d_prev.astype(F32), NEG_INF)
        sinkcol_ref[kv] = sink


def _stack_heads(t0, t1, lo):
    z = jnp.zeros_like(t0)
    return jnp.concatenate([jnp.where(lo, t0, z), jnp.where(lo, z, t0), jnp.where(lo, t1, z), jnp.where(lo, z, t1)], axis=0)


def _unstack_heads(x4, lo):
    return (jnp.where(lo, x4[0:BLOCK], x4[BLOCK:2 * BLOCK]), jnp.where(lo, x4[2 * BLOCK:3 * BLOCK], x4[3 * BLOCK:]))


def _att_probs(q4, k2c, k2p, bias_c, bias_p, sink, has_prev):
    s_c = _dot_nt(q4, k2c) * ATT_SCALE + bias_c
    s_p = jnp.where(has_prev, _dot_nt(q4, k2p) * ATT_SCALE + bias_p, NEG_INF)
    m = jnp.maximum(jnp.max(jnp.maximum(s_c, s_p), axis=-1, keepdims=True), sink)
    e_c = jnp.exp(s_c - m)
    e_p = jnp.exp(s_p - m)
    e_s = jnp.exp(sink - m)
    inv = 1.0 / (jnp.sum(e_c + e_p, axis=-1, keepdims=True) + e_s)
    return e_c * inv, e_p * inv, e_s * inv


def _attention_fwd(proj, q_gain, k_gain, sinks, *, n_seq, seq, name):
    T = n_seq * seq
    nb = seq // BLOCK
    qcol, kvcol = COL_QKV // ATT_WIDTH, (COL_QKV + ATT_WIDTH) // (2 * KV_WIDTH)

    def body(q_ref, kv_ref, qg_ref, kg_ref, sink_ref, y_ref, bias_ref, sinkcol_ref):
        lo = _lo_mask((BLOCK, LANES))
        qg, kg = qg_ref[...], kg_ref[...]
        _att_consts(sink_ref, bias_ref, sinkcol_ref)

        def block(i, carry):
            r0 = pl.multiple_of(i * BLOCK, BLOCK)
            rp = pl.multiple_of(jnp.maximum(i - 1, 0) * BLOCK, BLOCK)
            has_prev = i > 0
            kn_c = _head_norm(kv_ref[pl.ds(r0, BLOCK), 0:KV_WIDTH].astype(F32), kg, lo)[0].astype(BF16)
            kn_p = _head_norm(kv_ref[pl.ds(rp, BLOCK), 0:KV_WIDTH].astype(F32), kg, lo)[0].astype(BF16)
            v_c = kv_ref[pl.ds(r0, BLOCK), KV_WIDTH:2 * KV_WIDTH].astype(BF16)
            v_p = kv_ref[pl.ds(rp, BLOCK), KV_WIDTH:2 * KV_WIDTH].astype(BF16)
            for kv in range(2):
                k2c, k2p = _dup_half(kn_c, kv, lo), _dup_half(kn_p, kv, lo)
                v2c, v2p = _dup_half(v_c, kv, lo), _dup_half(v_p, kv, lo)
                cols = [slice((2 * kv + t) * LANES, (2 * kv + t + 1) * LANES) for t in range(2)]
                qn = [_head_norm(q_ref[pl.ds(r0, BLOCK), c].astype(F32), qg, lo)[0] for c in cols]
                q4 = _stack_heads(qn[0], qn[1], lo).astype(BF16)
                p_c, p_p, _ = _att_probs(q4, k2c, k2p, bias_ref[kv, 0], bias_ref[kv, 1], sinkcol_ref[kv], has_prev)
                o4 = _dot_nn(p_c.astype(BF16), v2c) + _dot_nn(p_p.astype(BF16), v2p)
                for c, out in zip(cols, _unstack_heads(o4, lo)):
                    y_ref[pl.ds(r0, BLOCK), c] = out.astype(BF16)
            return carry

        lax.fori_loop(0, nb, block, 0)

    vec = pl.BlockSpec((1, LANES), lambda b: (0, 0))
    return pl.pallas_call(
        body, name=name, grid=(n_seq,),
        in_specs=[pl.BlockSpec((seq, ATT_WIDTH), lambda b: (b, qcol)),
                  pl.BlockSpec((seq, 2 * KV_WIDTH), lambda b: (b, kvcol)),
                  vec, vec, pl.BlockSpec(memory_space=pltpu.SMEM)],
        out_specs=pl.BlockSpec((seq, ATT_WIDTH), lambda b: (b, 0)),
        out_shape=jax.ShapeDtypeStruct((T, ATT_WIDTH), BF16),
        scratch_shapes=list(ATT_SCRATCH),
        compiler_params=_params(("parallel",)),
    )(proj, proj, jnp.tile(q_gain, 2).reshape(1, LANES), jnp.tile(k_gain, 2).reshape(1, LANES), sinks)


def _attention_bwd(proj, dy, q_gain, k_gain, sinks, *, n_seq, seq, name, deps=()):
    T = n_seq * seq
    nb = seq // BLOCK
    qcol, kvcol = COL_QKV // ATT_WIDTH, (COL_QKV + ATT_WIDTH) // (2 * KV_WIDTH)

    def body(q_ref, kv_ref, dy_ref, qg_ref, kg_ref, sink_ref, dqkv_ref, dqg_ref, dkg_ref, dsink_ref,
             dkn_acc, dv_acc, qg_acc, kg_acc, sink_acc, bias_ref, sinkcol_ref):
        lo = _lo_mask((BLOCK, LANES))
        qg, kg = qg_ref[...], kg_ref[...]
        _att_consts(sink_ref, bias_ref, sinkcol_ref)
        first = pl.program_id(0) == 0

        @pl.when(first)
        def _():
            qg_acc[...] = jnp.zeros_like(qg_acc)
            kg_acc[...] = jnp.zeros_like(kg_acc)
            sink_acc[...] = jnp.zeros_like(sink_acc)

        dkn_acc[...] = jnp.zeros_like(dkn_acc)
        dv_acc[...] = jnp.zeros_like(dv_acc)

        def block(i, carry):
            r0 = pl.multiple_of(i * BLOCK, BLOCK)
            rp = pl.multiple_of(jnp.maximum(i - 1, 0) * BLOCK, BLOCK)
            has_prev = i > 0
            kn_c = _head_norm(kv_ref[pl.ds(r0, BLOCK), 0:KV_WIDTH].astype(F32), kg, lo)[0].astype(BF16)
            kn_p = _head_norm(kv_ref[pl.ds(rp, BLOCK), 0:KV_WIDTH].astype(F32), kg, lo)[0].astype(BF16)
            v_c = kv_ref[pl.ds(r0, BLOCK), KV_WIDTH:2 * KV_WIDTH].astype(BF16)
            v_p = kv_ref[pl.ds(rp, BLOCK), KV_WIDTH:2 * KV_WIDTH].astype(BF16)
            dk_c, dk_p, dv_c, dv_p = [], [], [], []
            for kv in range(2):
                k2c, k2p = _dup_half(kn_c, kv, lo), _dup_half(kn_p, kv, lo)
                v2c, v2p = _dup_half(v_c, kv, lo), _dup_half(v_p, kv, lo)
                cols = [slice((2 * kv + t) * LANES, (2 * kv + t + 1) * LANES) for t in range(2)]
                normed = [_head_norm(q_ref[pl.ds(r0, BLOCK), c].astype(F32), qg, lo) for c in cols]
                q4 = _stack_heads(normed[0][0], normed[1][0], lo).astype(BF16)
                do4 = _stack_heads(dy_ref[pl.ds(r0, BLOCK), cols[0]], dy_ref[pl.ds(r0, BLOCK), cols[1]], lo)
                p_c, p_p, p_s = _att_probs(q4, k2c, k2p, bias_ref[kv, 0], bias_ref[kv, 1], sinkcol_ref[kv], has_prev)
                dp_c = _dot_nt(do4, v2c)
                dp_p = _dot_nt(do4, v2p)
                delta = jnp.sum(p_c * dp_c + p_p * dp_p, axis=-1, keepdims=True)
                ds_c = (p_c * (dp_c - delta)).astype(BF16)
                ds_p = (p_p * (dp_p - delta)).astype(BF16)
                sink_acc[kv] += -(p_s * delta)
                dq4 = (_dot_nn(ds_c, k2c) + _dot_nn(ds_p, k2p)) * ATT_SCALE
                for c, (_, qh, qr), dqn in zip(cols, normed, _unstack_heads(dq4, lo)):
                    dq, dg = _head_norm_bwd(qh, qr, qg, dqn, lo)
                    dqkv_ref[pl.ds(r0, BLOCK), c] = dq.astype(BF16)
                    qg_acc[...] += dg
                dk_c.append(_dot_tn(ds_c, q4))
                dk_p.append(_dot_tn(ds_p, q4))
                dv_c.append(_dot_tn(p_c.astype(BF16), do4))
                dv_p.append(_dot_tn(p_p.astype(BF16), do4))

            def fold(parts):
                a = parts[0] + pltpu.roll(parts[0], LANES // 2, axis=1)
                b = parts[1] + pltpu.roll(parts[1], LANES // 2, axis=1)
                return jnp.where(lo, a, b)

            dkn_acc[pl.ds(r0, BLOCK), :] += fold(dk_c) * ATT_SCALE
            dkn_acc[pl.ds(rp, BLOCK), :] += fold(dk_p) * ATT_SCALE
            dv_acc[pl.ds(r0, BLOCK), :] += fold(dv_c)
            dv_acc[pl.ds(rp, BLOCK), :] += fold(dv_p)
            return carry

        lax.fori_loop(0, nb, block, 0)

        def finish(i, carry):
            r0 = pl.multiple_of(i * BLOCK, BLOCK)
            _, kh, kr = _head_norm(kv_ref[pl.ds(r0, BLOCK), 0:KV_WIDTH].astype(F32), kg, lo)
            dk, dg = _head_norm_bwd(kh, kr, kg, dkn_acc[pl.ds(r0, BLOCK), :], lo)
            dqkv_ref[pl.ds(r0, BLOCK), ATT_WIDTH:ATT_WIDTH + KV_WIDTH] = dk.astype(BF16)
            dqkv_ref[pl.ds(r0, BLOCK), ATT_WIDTH + KV_WIDTH:QKV_WIDTH] = dv_acc[pl.ds(r0, BLOCK), :].astype(BF16)
            kg_acc[...] += dg
            return carry

        lax.fori_loop(0, nb, finish, 0)

        @pl.when(pl.program_id(0) == n_seq - 1)
        def _():
            dqg_ref[...] = jnp.sum(qg_acc[...], axis=0, keepdims=True)
            dkg_ref[...] = jnp.sum(kg_acc[...], axis=0, keepdims=True)
            lane = lax.broadcasted_iota(jnp.int32, (1, LANES), 1)
            dsink = jnp.zeros((1, LANES), F32)
            for kv in range(2):
                for r in range(Q_GROUP):
                    total = jnp.sum(sink_acc[kv, r * BLOCK:(r + 1) * BLOCK, :], axis=0, keepdims=True)
                    dsink = jnp.where(lane == Q_GROUP * kv + r, total, dsink)
            dsink_ref[...] = dsink

    vec = pl.BlockSpec((1, LANES), lambda b: (0, 0))
    acc = pltpu.VMEM((BLOCK, LANES), F32)
    body, dep_specs, dep_args = _with_deps(body, 6, deps)
    dqkv, dqg, dkg, dsink = pl.pallas_call(
        body, name=name, grid=(n_seq,),
        in_specs=[pl.BlockSpec((seq, ATT_WIDTH), lambda b: (b, qcol)),
                  pl.BlockSpec((seq, 2 * KV_WIDTH), lambda b: (b, kvcol)),
                  pl.BlockSpec((seq, ATT_WIDTH), lambda b: (b, 0)),
                  vec, vec, pl.BlockSpec(memory_space=pltpu.SMEM)] + dep_specs,
        out_specs=[pl.BlockSpec((seq, QKV_WIDTH), lambda b: (b, 0)), vec, vec, vec],
        out_shape=[jax.ShapeDtypeStruct((T, QKV_WIDTH), BF16)] + [jax.ShapeDtypeStruct((1, LANES), F32)] * 3,
        scratch_shapes=[pltpu.VMEM((seq, KV_WIDTH), F32), pltpu.VMEM((seq, KV_WIDTH), F32), acc, acc,
                        pltpu.VMEM((2, GROUP_ROWS, 1), F32), *ATT_SCRATCH],
        compiler_params=_params(("arbitrary",)),
    )(proj, proj, dy, jnp.tile(q_gain, 2).reshape(1, LANES), jnp.tile(k_gain, 2).reshape(1, LANES), sinks, *dep_args)
    half = LANES // 2
    return dqkv, dqg[0, :half] + dqg[0, half:], dkg[0, :half] + dkg[0, half:], dsink[0, :N_Q_HEADS]


def _sgu_weights(w_ref):
    r = lax.broadcasted_iota(jnp.int32, (BLOCK, BLOCK), 0)
    c = lax.broadcasted_iota(jnp.int32, (BLOCK, BLOCK), 1)
    return [jnp.where(r >= c, w_ref[g], 0.0).astype(BF16) for g in range(SGU_GROUPS)]


def _sgu_fwd(proj, gain, w_s, bias_full, *, n_seq, seq, name):
    T = n_seq * seq
    nc = seq // BLOCK

    def body(suv_ref, g_ref, w_ref, b_ref, y_ref):
        lo = _lo_mask((BLOCK, LANES))
        wm = _sgu_weights(w_ref)
        gain_v = g_ref[...]

        def chunk(c, carry):
            r0 = pl.multiple_of(c * BLOCK, BLOCK)
            gv = _gelu(suv_ref[pl.ds(r0, BLOCK), SGU_WIDTH:2 * SGU_WIDTH].astype(F32))
            r = lax.rsqrt(jnp.mean(gv * gv, axis=-1, keepdims=True) + NORM_EPS)
            vn = (gv * r * gain_v).astype(BF16)
            for p in range(SGU_WIDTH // LANES):
                cols = slice(p * LANES, (p + 1) * LANES)
                vp = vn[:, cols]
                mixed = jnp.where(lo, _dot_nn(wm[2 * p], vp), _dot_nn(wm[2 * p + 1], vp)) + b_ref[:, cols]
                u = _gelu(suv_ref[pl.ds(r0, BLOCK), cols].astype(F32))
                y_ref[pl.ds(r0, BLOCK), cols] = (u * mixed).astype(BF16)
            return carry

        lax.fori_loop(0, nc, chunk, 0)

    return pl.pallas_call(
        body, name=name, grid=(n_seq,),
        in_specs=[pl.BlockSpec((seq, 2 * SGU_WIDTH), lambda b: (b, COL_SUV // (2 * SGU_WIDTH))),
                  pl.BlockSpec((1, SGU_WIDTH), lambda b: (0, 0)),
                  pl.BlockSpec((SGU_GROUPS, BLOCK, BLOCK), lambda b: (0, 0, 0)),
                  pl.BlockSpec((BLOCK, SGU_WIDTH), lambda b: (0, 0))],
        out_specs=pl.BlockSpec((seq, SGU_WIDTH), lambda b: (b, 0)),
        out_shape=jax.ShapeDtypeStruct((T, SGU_WIDTH), BF16),
        compiler_params=_params(("parallel",)),
    )(proj, gain.reshape(1, SGU_WIDTH), w_s, bias_full)


def _sgu_bwd(proj, dy, gain, w_s, bias_full, *, n_seq, seq, name, deps=()):
    T = n_seq * seq
    nc = seq // BLOCK
    n_tiles = SGU_WIDTH // LANES

    def body(suv_ref, dy_ref, g_ref, w_ref, b_ref, dsuv_ref, dg_ref, dw_ref, db_ref, dg_acc, dw_acc, db_acc):
        lo = _lo_mask((BLOCK, LANES))
        hi = jnp.logical_not(lo)
        wm = _sgu_weights(w_ref)
        wmt = [jnp.where(lax.broadcasted_iota(jnp.int32, (BLOCK, BLOCK), 1) >= lax.broadcasted_iota(jnp.int32, (BLOCK, BLOCK), 0),
                         w_ref[g].T, 0.0).astype(BF16) for g in range(SGU_GROUPS)]
        gain_v = g_ref[...]

        @pl.when(pl.program_id(0) == 0)
        def _():
            dg_acc[...] = jnp.zeros_like(dg_acc)
            dw_acc[...] = jnp.zeros_like(dw_acc)
            db_acc[...] = jnp.zeros_like(db_acc)

        def chunk(c, carry):
            r0 = pl.multiple_of(c * BLOCK, BLOCK)
            gv, dgelu_v = _gelu_and_grad(suv_ref[pl.ds(r0, BLOCK), SGU_WIDTH:2 * SGU_WIDTH].astype(F32))
            r = lax.rsqrt(jnp.mean(gv * gv, axis=-1, keepdims=True) + NORM_EPS)
            vh = gv * r
            vn = (vh * gain_v).astype(BF16)
            dvn_tiles = []
            for p in range(n_tiles):
                cols = slice(p * LANES, (p + 1) * LANES)
                vp = vn[:, cols]
                mixed = jnp.where(lo, _dot_nn(wm[2 * p], vp), _dot_nn(wm[2 * p + 1], vp)) + b_ref[:, cols]
                u, dgelu_u = _gelu_and_grad(suv_ref[pl.ds(r0, BLOCK), cols].astype(F32))
                dyv = dy_ref[pl.ds(r0, BLOCK), cols]
                dsuv_ref[pl.ds(r0, BLOCK), cols] = (dyv * mixed * dgelu_u).astype(BF16)
                dm = dyv * u
                db_acc[:, cols] += dm
                dm_bf = dm.astype(BF16)
                dvn_tiles.append(jnp.where(lo, _dot_nn(wmt[2 * p], dm_bf), _dot_nn(wmt[2 * p + 1], dm_bf)))
                dw_acc[2 * p] += _dot_nt(jnp.where(lo, dm, 0.0).astype(BF16), vp)
                dw_acc[2 * p + 1] += _dot_nt(jnp.where(hi, dm, 0.0).astype(BF16), vp)
            dvn = jnp.concatenate(dvn_tiles, axis=1)
            dg_acc[...] += dvn * vh
            dvh = dvn * gain_v
            dgv = r * (dvh - vh * jnp.mean(dvh * vh, axis=-1, keepdims=True))
            dsuv_ref[pl.ds(r0, BLOCK), SGU_WIDTH:2 * SGU_WIDTH] = (dgv * dgelu_v).astype(BF16)
            return carry

        lax.fori_loop(0, nc, chunk, 0)

        @pl.when(pl.program_id(0) == n_seq - 1)
        def _():
            dg_ref[...] = jnp.sum(dg_acc[...], axis=0, keepdims=True)
            r = lax.broadcasted_iota(jnp.int32, (BLOCK, BLOCK), 0)
            c = lax.broadcasted_iota(jnp.int32, (BLOCK, BLOCK), 1)
            for g in range(SGU_GROUPS):
                dw_ref[g] = jnp.where(r >= c, dw_acc[g], 0.0)
            lane = lax.broadcasted_iota(jnp.int32, (BLOCK, LANES), 1)
            out = jnp.zeros((BLOCK, LANES), F32)
            for p in range(n_tiles):
                tile = db_acc[:, p * LANES:(p + 1) * LANES]
                s_lo = jnp.sum(jnp.where(lo, tile, 0.0), axis=-1, keepdims=True)
                s_hi = jnp.sum(jnp.where(hi, tile, 0.0), axis=-1, keepdims=True)
                out = jnp.where(lane == 2 * p, s_lo, out)
                out = jnp.where(lane == 2 * p + 1, s_hi, out)
            db_ref[...] = out

    body, dep_specs, dep_args = _with_deps(body, 5, deps)
    dsuv, dg, dw, db = pl.pallas_call(
        body, name=name, grid=(n_seq,),
        in_specs=[pl.BlockSpec((seq, 2 * SGU_WIDTH), lambda b: (b, COL_SUV // (2 * SGU_WIDTH))),
                  pl.BlockSpec((seq, SGU_WIDTH), lambda b: (b, 0)),
                  pl.BlockSpec((1, SGU_WIDTH), lambda b: (0, 0)),
                  pl.BlockSpec((SGU_GROUPS, BLOCK, BLOCK), lambda b: (0, 0, 0)),
                  pl.BlockSpec((BLOCK, SGU_WIDTH), lambda b: (0, 0))] + dep_specs,
        out_specs=[pl.BlockSpec((seq, 2 * SGU_WIDTH), lambda b: (b, 0)),
                   pl.BlockSpec((1, SGU_WIDTH), lambda b: (0, 0)),
                   pl.BlockSpec((SGU_GROUPS, BLOCK, BLOCK), lambda b: (0, 0, 0)),
                   pl.BlockSpec((BLOCK, LANES), lambda b: (0, 0))],
        out_shape=[jax.ShapeDtypeStruct((T, 2 * SGU_WIDTH), BF16), jax.ShapeDtypeStruct((1, SGU_WIDTH), F32),
                   jax.ShapeDtypeStruct((SGU_GROUPS, BLOCK, BLOCK), F32), jax.ShapeDtypeStruct((BLOCK, LANES), F32)],
        scratch_shapes=[pltpu.VMEM((BLOCK, SGU_WIDTH), F32), pltpu.VMEM((SGU_GROUPS, BLOCK, BLOCK), F32),
                        pltpu.VMEM((BLOCK, SGU_WIDTH), F32)],
        compiler_params=_params(("arbitrary",)),
    )(proj, dy, gain.reshape(1, SGU_WIDTH), w_s, bias_full, *dep_args)
    return dsuv, dg.reshape(SGU_WIDTH), dw, db[:, :SGU_GROUPS].T


def _merge_fwd(y_att, y_sgu, w_oa, w_ob, proj, *, name, tm=1024, tn=512, deps=()):
    T = y_att.shape[0]

    def body(ya_ref, ys_ref, wa_ref, wb_ref, ga_ref, gb_ref, o_ref):
        pa = _dot_nn(ya_ref[...], wa_ref[...])
        pb = _dot_nn(ys_ref[...], wb_ref[...])
        o_ref[...] = (_sigmoid(ga_ref[...].astype(F32)) * pa + _sigmoid(gb_ref[...].astype(F32)) * pb).astype(BF16)

    act = pl.BlockSpec((tm, ATT_WIDTH), lambda i, j: (i, 0))
    wgt = pl.BlockSpec((ATT_WIDTH, tn), lambda i, j: (0, j))
    body, dep_specs, dep_args = _with_deps(body, 6, deps)
    return pl.pallas_call(
        body, name=name, grid=(T // tm, D_MODEL // tn),
        in_specs=[act, act, wgt, wgt,
                  pl.BlockSpec((tm, tn), lambda i, j: (i, j + COL_GA // tn)),
                  pl.BlockSpec((tm, tn), lambda i, j: (i, j + COL_GB // tn))] + dep_specs,
        out_specs=pl.BlockSpec((tm, tn), lambda i, j: (i, j)),
        out_shape=jax.ShapeDtypeStruct((T, D_MODEL), BF16),
        compiler_params=_params(("parallel", "parallel")),
    )(y_att, y_sgu, w_oa, w_ob, proj, proj, *dep_args)


def _merge_bwd(dx1_bf, w_out, y_att, y_sgu, w_oa, w_ob, proj, *, name, tm=1024, tn=512):
    T = y_att.shape[0]

    def body(dx_ref, wo_ref, ya_ref, ys_ref, wa_ref, wb_ref, ga_ref, gb_ref, dpa_ref, dpb_ref, dga_ref, dgb_ref):
        dm = _dot_nt(dx_ref[...], wo_ref[...])
        pa = _dot_nn(ya_ref[...], wa_ref[...])
        pb = _dot_nn(ys_ref[...], wb_ref[...])
        sa = _sigmoid(ga_ref[...].astype(F32))
        sb = _sigmoid(gb_ref[...].astype(F32))
        dpa_ref[...] = (dm * sa).astype(BF16)
        dpb_ref[...] = (dm * sb).astype(BF16)
        dga_ref[...] = (dm * pa * sa * (1.0 - sa)).astype(BF16)
        dgb_ref[...] = (dm * pb * sb * (1.0 - sb)).astype(BF16)

    act = pl.BlockSpec((tm, ATT_WIDTH), lambda i, j: (i, 0))
    wgt = pl.BlockSpec((ATT_WIDTH, tn), lambda i, j: (0, j))
    out = pl.BlockSpec((tm, tn), lambda i, j: (i, j))
    return pl.pallas_call(
        body, name=name, grid=(T // tm, D_MODEL // tn),
        in_specs=[pl.BlockSpec((tm, D_MODEL), lambda i, j: (i, 0)),
                  pl.BlockSpec((tn, D_MODEL), lambda i, j: (j, 0)),
                  act, act, wgt, wgt,
                  pl.BlockSpec((tm, tn), lambda i, j: (i, j + COL_GA // tn)),
                  pl.BlockSpec((tm, tn), lambda i, j: (i, j + COL_GB // tn))],
        out_specs=[out] * 4,
        out_shape=[jax.ShapeDtypeStruct((T, D_MODEL), BF16)] * 4,
        compiler_params=_params(("parallel", "parallel")),
    )(dx1_bf, w_out, y_att, y_sgu, w_oa, w_ob, proj, proj)


CONV_ROWS = 256
CONV_TN = 256
UP_CONV_ROWS = 256


def _shift_rows(cur, prev8, k):
    rolled = pltpu.roll(cur, k, axis=0)
    head = jnp.where(lax.broadcasted_iota(jnp.int32, prev8.shape, 0) < k, pltpu.roll(prev8, k, axis=0), rolled[:SUBLANES])
    return jnp.concatenate([head, rolled[SUBLANES:]], axis=0)


def _shift_rows_up(cur, next8, k):
    n = cur.shape[0]
    rolled = pltpu.roll(cur, n - k, axis=0)
    tail = jnp.where(lax.broadcasted_iota(jnp.int32, next8.shape, 0) >= SUBLANES - k,
                     pltpu.roll(next8, SUBLANES - k, axis=0), rolled[n - SUBLANES:])
    return jnp.concatenate([rolled[:n - SUBLANES], tail], axis=0)


def _up_conv_fwd(h2, w_up_t, cw_g, cw_v, cb_g, cb_v, *, n_seq, seq, name, deps=()):
    T = n_seq * seq
    tn, rows = CONV_TN, UP_CONV_ROWS

    def body(h_ref, ug_ref, uv_ref, wg_ref, wv_ref, bg_ref, bv_ref, a_ref, zg_ref, zv_ref, cg_ref, cv_ref):
        def conv(cur, prev8, w_ref, b_ref):
            z1 = _shift_rows(cur, prev8, 1)
            z2 = _shift_rows(cur, prev8, 2)
            return b_ref[...] + w_ref[0:1, :] * z2 + w_ref[1:2, :] * z1 + w_ref[2:3, :] * cur

        start = jnp.zeros((SUBLANES, tn), F32)
        prev = (start, start)
        for s in range(seq // rows):
            r = pl.ds(s * rows, rows)
            h = h_ref[r, :]
            zg = _dot_nt(h, ug_ref[...])
            zv = _dot_nt(h, uv_ref[...])
            zg_ref[r, :] = zg.astype(ACT_DTYPE)
            zv_ref[r, :] = zv.astype(ACT_DTYPE)
            g = conv(zg, prev[0], wg_ref, bg_ref)
            v = conv(zv, prev[1], wv_ref, bv_ref)
            a_ref[r, :] = (g * _sigmoid(g) * v).astype(BF16)
            cg_ref[r, :] = g.astype(ACT_DTYPE)
            cv_ref[r, :] = v.astype(ACT_DTYPE)
            prev = (zg[rows - SUBLANES:], zv[rows - SUBLANES:])

    zs = pl.BlockSpec((seq, tn), lambda b, j: (b, j))
    ws = pl.BlockSpec((3, tn), lambda b, j: (0, j))
    bs = pl.BlockSpec((1, tn), lambda b, j: (0, j))
    body, dep_specs, dep_args = _with_deps(body, 7, deps)
    return pl.pallas_call(
        body, name=name, grid=(n_seq, D_FF // tn),
        in_specs=[pl.BlockSpec((seq, D_MODEL), lambda b, j: (b, 0)),
                  pl.BlockSpec((tn, D_MODEL), lambda b, j: (j, 0)),
                  pl.BlockSpec((tn, D_MODEL), lambda b, j: (j + D_FF // tn, 0)), ws, ws, bs, bs] + dep_specs,
        out_specs=[zs] * 5,
        out_shape=[jax.ShapeDtypeStruct((T, D_FF), BF16)] + [jax.ShapeDtypeStruct((T, D_FF), ACT_DTYPE)] * 4,
        compiler_params=_params(("parallel", "parallel")),
    )(h2, w_up_t, w_up_t, cw_g, cw_v, cb_g.reshape(1, D_FF), cb_v.reshape(1, D_FF), *dep_args)


def _conv_bwd(z_g, z_v, c_g, c_v, dx2_bf, w_down, cw_g, cw_v, *, n_seq, seq, name):
    T = n_seq * seq
    tn, rows = CONV_TN, CONV_ROWS
    n_steps = seq // rows

    def body(zg_ref, zv_ref, cg_ref, cv_ref, dx_ref, wd_ref, wg_ref, wv_ref,
             dzg_ref, dzv_ref, dwg_ref, dwv_ref, dbg_ref, dbv_ref, dcg_ref, dcv_ref):
        def colsum(x):
            return jnp.sum(x, axis=0, keepdims=True)

        zero = jnp.zeros((1, tn), F32)
        db = (zero, zero)
        for s in range(n_steps):
            r = pl.ds(s * rows, rows)
            g = cg_ref[r, :].astype(F32)
            v = cv_ref[r, :].astype(F32)
            sg = _sigmoid(g)
            dav = _dot_nt(dx_ref[r, :], wd_ref[...])
            dcg = dav * v * (sg * (1.0 + g * (1.0 - sg)))
            dcv = dav * (g * sg)
            dcg_ref[r, :] = dcg
            dcv_ref[r, :] = dcv
            db = (db[0] + colsum(dcg), db[1] + colsum(dcv))

        def back(s, accs):
            r0 = pl.multiple_of(s * rows, rows)
            last = s == n_steps - 1
            rn = pl.multiple_of(jnp.minimum(r0 + rows, seq - SUBLANES), SUBLANES)
            new = []
            for half, (dc_ref, w_ref, dz_ref, z_ref) in enumerate(((dcg_ref, wg_ref, dzg_ref, zg_ref),
                                                                   (dcv_ref, wv_ref, dzv_ref, zv_ref))):
                cur = dc_ref[pl.ds(r0, rows), :]
                nxt = jnp.where(last, 0.0, dc_ref[pl.ds(rn, SUBLANES), :])
                u1, u2 = _shift_rows_up(cur, nxt, 1), _shift_rows_up(cur, nxt, 2)
                dz_ref[pl.ds(r0, rows), :] = (w_ref[2:3, :] * cur + w_ref[1:2, :] * u1 + w_ref[0:1, :] * u2).astype(BF16)
                z = z_ref[pl.ds(r0, rows), :].astype(F32)
                new += [accs[3 * half] + colsum(u2 * z), accs[3 * half + 1] + colsum(u1 * z),
                        accs[3 * half + 2] + colsum(cur * z)]
            return tuple(new)

        dw = lax.fori_loop(0, n_steps, back, (zero,) * 6)
        first_seq = pl.program_id(1) == 0

        @pl.when(first_seq)
        def _():
            dwg_ref[...] = jnp.concatenate(dw[0:3], axis=0)
            dwv_ref[...] = jnp.concatenate(dw[3:6], axis=0)
            dbg_ref[...], dbv_ref[...] = db

        @pl.when(jnp.logical_not(first_seq))
        def _():
            dwg_ref[...] += jnp.concatenate(dw[0:3], axis=0)
            dwv_ref[...] += jnp.concatenate(dw[3:6], axis=0)
            dbg_ref[...] += db[0]
            dbv_ref[...] += db[1]

    zs = pl.BlockSpec((seq, tn), lambda j, b: (b, j))
    ws = pl.BlockSpec((3, tn), lambda j, b: (0, j))
    bs = pl.BlockSpec((1, tn), lambda j, b: (0, j))
    outs = pl.pallas_call(
        body, name=name, grid=(D_FF // tn, n_seq),
        in_specs=[zs] * 4 + [pl.BlockSpec((seq, D_MODEL), lambda j, b: (b, 0)),
                             pl.BlockSpec((tn, D_MODEL), lambda j, b: (j, 0)), ws, ws],
        out_specs=[zs, zs, ws, ws, bs, bs],
        out_shape=[jax.ShapeDtypeStruct((T, D_FF), BF16)] * 2 + [jax.ShapeDtypeStruct((3, D_FF), F32)] * 2
        + [jax.ShapeDtypeStruct((1, D_FF), F32)] * 2,
        scratch_shapes=[pltpu.VMEM((seq, tn), F32), pltpu.VMEM((seq, tn), F32)],
        compiler_params=_params(("parallel", "arbitrary")),
    )(z_g, z_v, c_g, c_v, dx2_bf, w_down, cw_g, cw_v)
    dz_g, dz_v, dw_g, dw_v, db_g, db_v = outs
    return dz_g, dz_v, dw_g, dw_v, db_g.reshape(D_FF), db_v.reshape(D_FF)


def _layer_fwd(x, h, w, sched, tail, *, n_seq, seq, l):
    tag = f"l{l}"
    deps = sched("fwd_start", l, h)
    proj = _mm(h, w["w_in_t"], mode="nt", out_dtype=ACT_DTYPE, rotate=W_IN_ROTATE, name=f"{tag}_proj", deps=deps)
    y_att = _attention_fwd(proj, w["q_norm"], w["k_norm"], w["sinks"], n_seq=n_seq, seq=seq, name=f"{tag}_att")
    deps = sched("fwd_att", l, y_att)
    y_sgu = _sgu_fwd(proj, w["sgu_norm"], w["w_s"], w["bias_full"], n_seq=n_seq, seq=seq, name=f"{tag}_sgu")
    merged = _merge_fwd(y_att, y_sgu, w["w_oa"], w["w_ob"], proj, name=f"{tag}_merge", deps=deps)
    x1, h2 = _mm_rows(merged, w["w_out"], mode="nn", fn=_residual_then_norm, out_dtypes=(F32, BF16), rows=(x,),
                      vecs=(w["ffn_norm"],), name=f"{tag}_out")
    deps = sched("fwd_mixer_done", l, x1)
    a, z_g, z_v, c_g, c_v = _up_conv_fwd(h2, w["w_up_t"], w["cw_g"], w["cw_v"], w["cb_g"], w["cb_v"], n_seq=n_seq,
                                         seq=seq, name=f"{tag}_up_conv", deps=deps)
    deps = sched("fwd_conv", l, a)
    if tail[0] == "norm":
        out = _mm_rows(a, w["w_down"], mode="nn", fn=_residual_then_norm, out_dtypes=(F32, BF16), rows=(x1,),
                       vecs=(tail[1],), name=f"{tag}_down", deps=deps)
    else:
        out = _mm_rows(a, w["w_down"], mode="nn", fn=_residual_then_loss, out_dtypes=(F32, BF16), rows=(x1, tail[1]),
                       reduce=True, name=f"{tag}_down", deps=deps)
    saved = dict(x=x, h=h, proj=proj, y_att=y_att, y_sgu=y_sgu, merged=merged, x1=x1, h2=h2, z_g=z_g, z_v=z_v,
                 c_g=c_g, c_v=c_v, a=a)
    return out, saved


def _layer_bwd(dx2, dx2_bf, w, s, sched, deps, *, n_seq, seq, l):
    tag = f"l{l}b"
    g = {}
    g["w_down"] = _mm(s["a"], dx2_bf, mode="tn", out_dtype=F32, name=f"{tag}_dw_down", deps=deps)
    dz_g, dz_v, g["cw_g"], g["cw_v"], g["cb_g"], g["cb_v"] = _conv_bwd(
        s["z_g"], s["z_v"], s["c_g"], s["c_v"], dx2_bf, w["w_down"], w["cw_g"], w["cw_v"], n_seq=n_seq, seq=seq,
        name=f"{tag}_conv")
    dw_up_t = _mm(dz_g, s["h2"], mode="tn", out_dtype=F32, out_rows=(0, 2 * D_FF), name=f"{tag}_dw_up_g")
    g["w_up_t"] = _mm(dz_v, s["h2"], mode="tn", out_dtype=F32, out_rows=(D_FF, 2 * D_FF), out_prev=dw_up_t,
                      name=f"{tag}_dw_up_v")
    deps = sched("bwd_ffn_grads", l, dz_v, g)
    dx1, dx1_bf, dgain = _mm_rows((dz_g, dz_v), w["w_up_t"], mode="nn", fn=_rms_bwd_rows, out_dtypes=(F32, BF16),
                                  rows=(s["x1"], dx2), vecs=(w["ffn_norm"],), reduce=True, a_at=(0, D_FF),
                                  name=f"{tag}_dh2", deps=deps)
    g["ffn_norm"] = dgain.reshape(D_MODEL)
    dpa, dpb, dga, dgb = _merge_bwd(dx1_bf, w["w_out"], s["y_att"], s["y_sgu"], w["w_oa"], w["w_ob"], s["proj"],
                                    name=f"{tag}_merge")
    deps = sched("bwd_merge", l, dpa)
    g["w_out"] = _mm(s["merged"], dx1_bf, mode="tn", out_dtype=F32, name=f"{tag}_dw_out",
                     deps=deps)
    dy_att = _mm(dpa, w["w_oa"], mode="nt", out_dtype=BF16, name=f"{tag}_dy_att")
    dy_sgu = _mm(dpb, w["w_ob"], mode="nt", out_dtype=F32, name=f"{tag}_dy_sgu")
    g["w_oa"] = _mm(s["y_att"], dpa, mode="tn", out_dtype=F32, name=f"{tag}_dw_oa")
    g["w_ob"] = _mm(s["y_sgu"], dpb, mode="tn", out_dtype=F32, name=f"{tag}_dw_ob")
    deps = sched("bwd_out_grads", l, dy_att, g)
    dqkv, g["q_norm"], g["k_norm"], g["sinks"] = _attention_bwd(
        s["proj"], dy_att, w["q_norm"], w["k_norm"], w["sinks"], n_seq=n_seq, seq=seq, name=f"{tag}_att", deps=deps)
    deps = sched("bwd_att", l, dqkv)
    dsuv, g["sgu_norm"], g["w_s"], g["b_s"] = _sgu_bwd(
        s["proj"], dy_sgu, w["sgu_norm"], w["w_s"], w["bias_full"], n_seq=n_seq, seq=seq, name=f"{tag}_sgu", deps=deps)
    dproj = (dsuv, dga, dgb, dqkv)
    at = (QKV_WIDTH, QKV_WIDTH + 2 * SGU_WIDTH, QKV_WIDTH + 2 * SGU_WIDTH + D_MODEL, 0)
    g["w_in_t"] = _mm_tn_parts(dproj, at, s["h"], name=f"{tag}_dw_in")
    deps = sched("bwd_w_in_grad", l, dqkv, g)
    dx, dx_bf, dgain = _mm_rows(dproj, w["w_in_t"], mode="nn", fn=_rms_bwd_rows, out_dtypes=(F32, BF16),
                                rows=(s["x"], dx1), vecs=(w["mix_norm"],), reduce=True, a_at=at,
                                name=f"{tag}_dh", deps=deps)
    g["mix_norm"] = dgain.reshape(D_MODEL)
    return dx, dx_bf, g, sched("bwd_dh", l, dx)


def _local_step(x, target, weights, sched, *, n_seq, seq):
    depth = len(weights)
    saved = []
    h = _rms_fwd(x, weights[0]["mix_norm"], name="l0_mix_norm", deps=sched("begin", 0, x))
    for l in range(depth):
        tail = ("norm", weights[l + 1]["mix_norm"]) if l + 1 < depth else ("loss", target)
        out, s = _layer_fwd(x, h, weights[l], sched, tail, n_seq=n_seq, seq=seq, l=l)
        saved.append(s)
        if l + 1 < depth:
            x, h = out
    dy, dy_bf, loss_cols = out
    grads = [None] * depth
    deps = ()
    for l in reversed(range(depth)):
        dy, dy_bf, grads[l], deps = _layer_bwd(dy, dy_bf, weights[l], saved[l], sched, deps, n_seq=n_seq, seq=seq, l=l)
    return jnp.sum(loss_cols), dy, grads, deps


W_IN_SHARD = IN_WIDTH // N_DEV
W_UP_SHARD = 2 * D_FF // N_DEV
COL_MOVE_ROWS = 256


def _w_o_moves():
    return tuple((j, 0, LANES, 0, j * LANES) for j in range(N_DEV))


def _disassemble(mats, w, moves, *, name):
    R = mats[0].shape[0]
    tr = min(R, COL_MOVE_ROWS)
    n = len(mats)

    def body(*refs):
        m_refs, o_ref = refs[:n], refs[n]
        for j, lo, hi, which, at in moves:
            o_ref[j, :, lo:hi] = m_refs[which][:, at:at + hi - lo]

    return pl.pallas_call(
        body, name=name, grid=(R // tr,),
        in_specs=[pl.BlockSpec((tr, m.shape[1]), lambda i: (i, 0)) for m in mats],
        out_specs=pl.BlockSpec((N_DEV, tr, w), lambda i: (0, i, 0)),
        out_shape=jax.ShapeDtypeStruct((N_DEV, R, w), mats[0].dtype),
        compiler_params=_params(("parallel",)),
    )(*mats)


def _my_place():
    return lax.axis_index("x"), lax.axis_index("y"), lax.axis_index("c")


def _gathered_shape(shape, kind):
    r, c = shape
    return {"blocks": (N_DEV, r, c), "rows": (N_DEV * r, c), "cols": (r, N_DEV * c)}[kind]


def _gather_window(ref, kind, shape, j):
    r, c = shape
    if kind == "blocks":
        return ref.at[j]
    if kind == "rows":
        return ref.at[pl.ds(pl.multiple_of(j * r, r), r), :]
    return ref.at[:, pl.ds(pl.multiple_of(j * c, c), c)]


def _gather(srcs, kinds, *, name):
    n = len(srcs)
    shapes = [s.shape for s in srcs]
    per = 7

    def body(*refs):
        src_refs, dst_refs = refs[:n], refs[n:2 * n]
        send_sems, recv_sems, local_sems = refs[2 * n:]
        x, y, c = _my_place()
        me, sibling = (x, y, c), (x, y, 1 - c)
        chips = [(1 - x, y), (x, 1 - y), (1 - x, 1 - y)]

        def at(i, px, py, pc):
            return _gather_window(dst_refs[i], kinds[i], shapes[i], 4 * px + 2 * py + pc)

        def copy(i, k, block, to, src=None):
            return pltpu.make_async_remote_copy(
                src_ref=at(i, *block) if src is None else src, dst_ref=at(i, *block),
                send_sem=send_sems.at[per * i + k], recv_sem=recv_sems.at[per * i + k], device_id=to, device_id_type=MESH)

        mine = [pltpu.make_async_copy(src_refs[i], at(i, *me), local_sems.at[i]) for i in range(n)]
        for cp in mine:
            cp.start()
        started = []
        for i in range(n):
            first = [copy(i, 0, me, sibling, src=src_refs[i])]
            first += [copy(i, 1 + j, me, (*chip, c), src=src_refs[i]) for j, chip in enumerate(chips)]
            for cp in first:
                cp.start()
            started += first
        for i in range(n):
            for j, chip in enumerate(chips):
                copy(i, 1 + j, (*chip, c), me).wait_recv()
                fwd = copy(i, 4 + j, (*chip, c), sibling)
                fwd.start()
                started.append(fwd)
        for i in range(n):
            copy(i, 0, sibling, me).wait_recv()
            for j, chip in enumerate(chips):
                copy(i, 4 + j, (*chip, 1 - c), me).wait_recv()
        for cp in started:
            cp.wait_send()
        for cp in mine:
            cp.wait()

    return pl.pallas_call(
        body, name=name,
        out_shape=[jax.ShapeDtypeStruct(_gathered_shape(s.shape, k), s.dtype) for s, k in zip(srcs, kinds)],
        in_specs=[ANY] * n, out_specs=[ANY] * n,
        scratch_shapes=[pltpu.SemaphoreType.DMA((per * n,)), pltpu.SemaphoreType.DMA((per * n,)),
                        pltpu.SemaphoreType.DMA((n,))],
    )(*srcs)


HBM = pl.BlockSpec(memory_space=pltpu.HBM)
SEM = pl.BlockSpec(memory_space=pltpu.SEMAPHORE)
TOKEN = jax.ShapeDtypeStruct((SUBLANES, LANES), F32)
TOKEN_SPEC = pl.BlockSpec(memory_space=pltpu.VMEM)
SPLIT_PARAMS = pltpu.CompilerParams(has_side_effects=pltpu.SideEffectType.DATAFLOW_SIDE_EFFECTING)


def _in_hbm(x):
    return pltpu.with_memory_space_constraint(x, pltpu.HBM)


def _hbm_like(shape, dtype):
    return pltpu.HBM(shape, dtype)


def _place_own(stacks, layers, kinds, dtypes, *, name, deps=()):
    n = len(stacks)
    shapes = [s.shape[1:] for s in stacks]

    def body(*refs):
        s_refs, land_refs, bufs, sems = refs[:n], refs[n:2 * n], refs[2 * n:3 * n], refs[3 * n]
        x, y, c = _my_place()
        copies = []
        for i in range(n):
            bufs[i][...] = s_refs[i][...].astype(dtypes[i])
            copies.append(pltpu.make_async_copy(
                bufs[i], _gather_window(land_refs[i], kinds[i], shapes[i], 4 * x + 2 * y + c), sems.at[i]))
        for cp in copies:
            cp.start()
        for cp in copies:
            cp.wait()

    def layer_of(shape, l):
        return pl.BlockSpec((None,) + shape, lambda i: (l,) + (0,) * len(shape))

    body, dep_specs, dep_args = _with_deps(body, n, deps)
    return pl.pallas_call(
        body, name=name, grid=(1,),
        out_shape=[jax.ShapeDtypeStruct(_gathered_shape(s, k), d) for s, k, d in zip(shapes, kinds, dtypes)],
        in_specs=[layer_of(s, l) for s, l in zip(shapes, layers)] + dep_specs, out_specs=[ANY] * n,
        scratch_shapes=[pltpu.VMEM(s, d) for s, d in zip(shapes, dtypes)] + [pltpu.SemaphoreType.DMA((n,))],
        compiler_params=_params(("arbitrary",)),
    )(*stacks, *dep_args)


def _gather_start(lands, kinds, shapes, after=(), *, name):
    n = len(lands)
    n_after = len(after)

    def body(*refs):
        land_refs = refs[:n]
        send_sems, recv_sems = refs[n + n_after], refs[n + n_after + 1]
        x, y, c = _my_place()
        targets = [(x, y, 1 - c), (1 - x, y, c), (x, 1 - y, c), (1 - x, 1 - y, c)]
        for i in range(n):
            own = _gather_window(land_refs[i], kinds[i], shapes[i], 4 * x + 2 * y + c)
            for k, to in enumerate(targets):
                pltpu.make_async_remote_copy(
                    src_ref=own, dst_ref=own, send_sem=send_sems.at[4 * i + k], recv_sem=recv_sems.at[4 * i + k],
                    device_id=to, device_id_type=MESH).start()
        refs[-1][...] = jnp.zeros_like(refs[-1])

    outs = pl.pallas_call(
        body, name=name,
        out_shape=[pltpu.SemaphoreType.DMA((4 * n,)), pltpu.SemaphoreType.DMA((4 * n,))]
        + [_hbm_like(a.shape, a.dtype) for a in lands] + [TOKEN],
        in_specs=[HBM] * n + [ANY] * n_after, out_specs=[SEM, SEM] + [HBM] * n + [TOKEN_SPEC],
        input_output_aliases={i: 2 + i for i in range(n)},
        compiler_params=SPLIT_PARAMS,
    )(*[_in_hbm(a) for a in lands], *after)
    return outs[0], outs[1], outs[2:2 + n], outs[-1]


def _gather_forward(recv_sems, lands, kinds, shapes, after, *, name):
    n = len(lands)

    def body(*refs):
        recv_ref, land_refs = refs[0], refs[1:1 + n]
        fwd_send, fwd_recv = refs[2 + n], refs[3 + n]
        token = refs[-1]
        x, y, c = _my_place()
        chips = [(1 - x, y), (x, 1 - y), (1 - x, 1 - y)]
        for i in range(n):
            for j, (px, py) in enumerate(chips):
                block = _gather_window(land_refs[i], kinds[i], shapes[i], 4 * px + 2 * py + c)
                pltpu.make_async_remote_copy(
                    src_ref=block, dst_ref=block, send_sem=fwd_send.at[3 * i + j], recv_sem=recv_ref.at[4 * i + 1 + j],
                    device_id=(px, py, c), device_id_type=MESH).wait_recv()
                pltpu.make_async_remote_copy(
                    src_ref=block, dst_ref=block, send_sem=fwd_send.at[3 * i + j], recv_sem=fwd_recv.at[3 * i + j],
                    device_id=(x, y, 1 - c), device_id_type=MESH).start()
        token[...] = jnp.zeros_like(token)

    outs = pl.pallas_call(
        body, name=name,
        out_shape=[pltpu.SemaphoreType.DMA((3 * n,)), pltpu.SemaphoreType.DMA((3 * n,))]
        + [_hbm_like(a.shape, a.dtype) for a in lands] + [TOKEN],
        in_specs=[SEM] + [HBM] * n + [ANY], out_specs=[SEM, SEM] + [HBM] * n + [TOKEN_SPEC],
        input_output_aliases={1 + i: 2 + i for i in range(n)},
        compiler_params=SPLIT_PARAMS,
    )(recv_sems, *lands, after)
    return outs[0], outs[1], outs[2:2 + n], outs[-1]


def _gather_finish(send_sems, recv_sems, fwd_send, fwd_recv, lands, kinds, shapes, after, *, name):
    n = len(lands)

    def body(*refs):
        send_ref, recv_ref, fsend_ref, frecv_ref = refs[:4]
        land_refs = refs[4:4 + n]
        x, y, c = _my_place()
        chips = [(1 - x, y), (x, 1 - y), (1 - x, 1 - y)]
        sibling = (x, y, 1 - c)
        for i in range(n):
            def window(j):
                return _gather_window(land_refs[i], kinds[i], shapes[i], j)

            mine, theirs = window(4 * x + 2 * y + c), window(4 * x + 2 * y + (1 - c))
            pltpu.make_async_remote_copy(src_ref=mine, dst_ref=theirs, send_sem=send_ref.at[4 * i],
                                         recv_sem=recv_ref.at[4 * i], device_id=sibling, device_id_type=MESH).wait_recv()
            for j, (px, py) in enumerate(chips):
                block = window(4 * px + 2 * py + (1 - c))
                pltpu.make_async_remote_copy(src_ref=block, dst_ref=block, send_sem=fsend_ref.at[3 * i + j],
                                             recv_sem=frecv_ref.at[3 * i + j], device_id=sibling,
                                             device_id_type=MESH).wait_recv()
            for k in range(4):
                pltpu.make_async_remote_copy(src_ref=mine, dst_ref=mine, send_sem=send_ref.at[4 * i + k],
                                             recv_sem=recv_ref.at[4 * i + k], device_id=sibling,
                                             device_id_type=MESH).wait_send()
            for j, (px, py) in enumerate(chips):
                block = window(4 * px + 2 * py + c)
                pltpu.make_async_remote_copy(src_ref=block, dst_ref=block, send_sem=fsend_ref.at[3 * i + j],
                                             recv_sem=frecv_ref.at[3 * i + j], device_id=sibling,
                                             device_id_type=MESH).wait_send()

    return pl.pallas_call(
        body, name=name,
        out_shape=[_hbm_like(a.shape, a.dtype) for a in lands],
        in_specs=[SEM] * 4 + [HBM] * n + [ANY], out_specs=[HBM] * n,
        input_output_aliases={4 + i: i for i in range(n)},
        compiler_params=SPLIT_PARAMS,
    )(send_sems, recv_sems, fwd_send, fwd_recv, *lands, after)


def _pair_plan(src_ref, land_ref, x, y, c):
    return [(src_ref.at[2 * k + (1 - c)], land_ref.at[k], (x, y, 1 - c)) for k in range(N_CHIPS)]


def _chip_plan(src_ref, land_ref, x, y, c):
    chips = [(1 - x, y), (x, 1 - y), (1 - x, 1 - y)]
    return [(src_ref.at[2 * px + py], land_ref.at[k], (px, py, c)) for k, (px, py) in enumerate(chips)]


def _exchange_copies(plan, per, src_refs, land_refs, send_sems, recv_sems):
    x, y, c = _my_place()
    copies = []
    for i, (s_ref, l_ref) in enumerate(zip(src_refs, land_refs)):
        for q, (src, dst, to) in enumerate(plan(s_ref, l_ref, x, y, c)):
            copies.append(pltpu.make_async_remote_copy(
                src_ref=src, dst_ref=dst, send_sem=send_sems.at[per * i + q], recv_sem=recv_sems.at[per * i + q],
                device_id=to, device_id_type=MESH))
    return copies


def _exchange_start(srcs, plan, per, *, name):
    n = len(srcs)

    def body(*refs):
        src_refs, land_refs = refs[:n], refs[n:2 * n]
        send_sems, recv_sems = refs[2 * n], refs[2 * n + 1]
        for cp in _exchange_copies(plan, per, src_refs, land_refs, send_sems, recv_sems):
            cp.start()
        refs[-1][...] = jnp.zeros_like(refs[-1])

    lands = [lax.empty((per,) + s.shape[1:], s.dtype) for s in srcs]
    outs = pl.pallas_call(
        body, name=name,
        out_shape=[pltpu.SemaphoreType.DMA((per * n,)), pltpu.SemaphoreType.DMA((per * n,))]
        + [_hbm_like(s.shape, s.dtype) for s in srcs] + [_hbm_like(a.shape, a.dtype) for a in lands] + [TOKEN],
        in_specs=[HBM] * (2 * n), out_specs=[SEM, SEM] + [HBM] * (2 * n) + [TOKEN_SPEC],
        input_output_aliases={i: 2 + i for i in range(2 * n)},
        compiler_params=SPLIT_PARAMS,
    )(*[_in_hbm(s) for s in srcs], *[_in_hbm(a) for a in lands])
    return outs[0], outs[1], outs[2:2 + n], outs[2 + n:2 + 2 * n], outs[-1]


def _exchange_wait(send_sems, recv_sems, srcs, lands, plan, per, after, *, name):
    n = len(srcs)
    after = list(after) if isinstance(after, (list, tuple)) else [after]

    def body(*refs):
        send_ref, recv_ref = refs[0], refs[1]
        src_refs, land_refs = refs[2:2 + n], refs[2 + n:2 + 2 * n]
        copies = _exchange_copies(plan, per, src_refs, land_refs, send_ref, recv_ref)
        for cp in copies:
            cp.wait_recv()
        for cp in copies:
            cp.wait_send()

    outs = pl.pallas_call(
        body, name=name,
        out_shape=[_hbm_like(s.shape, s.dtype) for s in srcs] + [_hbm_like(a.shape, a.dtype) for a in lands],
        in_specs=[SEM, SEM] + [HBM] * (2 * n) + [ANY] * len(after), out_specs=[HBM] * (2 * n),
        input_output_aliases={2 + i: i for i in range(2 * n)},
        compiler_params=SPLIT_PARAMS,
    )(send_sems, recv_sems, *srcs, *lands, *after)
    return outs[:n], outs[n:]


REDUCE_BLOCK_BYTES = 2 << 20


def _row_tile(r, c):
    row_bytes = 4 * (-(-c // LANES) * LANES)
    best = r
    for d in range(SUBLANES, r, SUBLANES):
        if r % d == 0 and d * row_bytes <= REDUCE_BLOCK_BYTES:
            best = d
    return best if r * row_bytes > REDUCE_BLOCK_BYTES else r


def _reduce_pair_sum(blocked, recv, place, wire_dtype, *, name):
    _, r, c = blocked.shape
    tr = _row_tile(r, c)

    def body(place_ref, g_ref, r_ref, own_ref, send_ref):
        s = g_ref[...] + r_ref[...]
        send_ref[...] = s.astype(wire_dtype)

        @pl.when(pl.program_id(1) == place_ref[1])
        def _():
            own_ref[...] = s

    return pl.pallas_call(
        body, name=name,
        grid_spec=pltpu.PrefetchScalarGridSpec(
            num_scalar_prefetch=1, grid=(r // tr, N_CHIPS),
            in_specs=[pl.BlockSpec((None, None, tr, c), lambda i, k, place_ref: (k, place_ref[0], i, 0)),
                      pl.BlockSpec((None, tr, c), lambda i, k, place_ref: (k, i, 0))],
            out_specs=[pl.BlockSpec((tr, c), lambda i, k, place_ref: (i, 0)),
                       pl.BlockSpec((None, tr, c), lambda i, k, place_ref: (k, i, 0))]),
        out_shape=[jax.ShapeDtypeStruct((r, c), F32), jax.ShapeDtypeStruct((N_CHIPS, r, c), wire_dtype)],
        compiler_params=_params(("parallel", "arbitrary")),
    )(place, blocked.reshape(N_CHIPS, 2, r, c), recv)


def _chip_sum(own_ref, r_ref):
    return ((own_ref[...] + r_ref[0].astype(F32)) + r_ref[1].astype(F32)) + r_ref[2].astype(F32)


def _reduce_chip_sum(own, recv, *, name):
    r, c = own.shape
    tr = _row_tile(r, c)

    def body(own_ref, r_ref, o_ref):
        o_ref[...] = _chip_sum(own_ref, r_ref)

    return pl.pallas_call(
        body, name=name, grid=(r // tr,),
        in_specs=[pl.BlockSpec((tr, c), lambda i: (i, 0)), pl.BlockSpec((N_CHIPS - 1, tr, c), lambda i: (0, i, 0))],
        out_specs=pl.BlockSpec((tr, c), lambda i: (i, 0)),
        out_shape=jax.ShapeDtypeStruct((r, c), F32),
        compiler_params=_params(("parallel",)),
    )(own, recv)


def _adamw_math(w, g, m, v):
    nm = ADAM_B1 * m + (1.0 - ADAM_B1) * g
    nv = ADAM_B2 * v + (1.0 - ADAM_B2) * (g * g)
    m_hat = nm / (1.0 - ADAM_B1 ** ADAM_STEP)
    v_hat = nv / (1.0 - ADAM_B2 ** ADAM_STEP)
    return -ADAM_LR * (m_hat / (jnp.sqrt(v_hat) + ADAM_EPS) + ADAM_WD * w), nm, nv


def _adamw(w, g, m, v, *, name):
    shape = w.shape
    C = shape[-1]
    R = math.prod(shape[:-1])
    tr = _row_tile(R, C)

    def body(w_ref, g_ref, m_ref, v_ref, d_ref, nm_ref, nv_ref):
        d_ref[...], nm_ref[...], nv_ref[...] = _adamw_math(w_ref[...], g_ref[...], m_ref[...], v_ref[...])

    spec = pl.BlockSpec((tr, C), lambda i: (i, 0))
    outs = pl.pallas_call(
        body, name=name, grid=(R // tr,),
        in_specs=[spec] * 4, out_specs=[spec] * 3,
        out_shape=[jax.ShapeDtypeStruct((R, C), F32)] * 3,
        compiler_params=_params(("parallel",)),
    )(*[a.reshape(R, C) for a in (w, g, m, v)])
    return tuple(o.reshape(shape) for o in outs)


def _reduce_adamw(own, recv, w, m, v, layer, prev, *, name):
    r, c = own.shape
    tr = _row_tile(r, c)
    n_prev = 0 if prev is None else len(prev)

    def body(own_ref, r_ref, w_ref, m_ref, v_ref, *rest):
        g_ref, d_ref, nm_ref, nv_ref = rest[n_prev:]
        g = _chip_sum(own_ref, r_ref)
        g_ref[...] = g
        d_ref[...], nm_ref[...], nv_ref[...] = _adamw_math(w_ref[...], g, m_ref[...], v_ref[...])

    slot = pl.BlockSpec((None, tr, c), lambda i: (layer, i, 0))
    return pl.pallas_call(
        body, name=name, grid=(r // tr,),
        in_specs=[pl.BlockSpec((tr, c), lambda i: (i, 0)), pl.BlockSpec((N_CHIPS - 1, tr, c), lambda i: (0, i, 0)),
                  slot, slot, slot] + [ANY] * n_prev,
        out_specs=[slot] * 4,
        out_shape=[jax.ShapeDtypeStruct((DEPTH, r, c), F32)] * 4,
        input_output_aliases={5 + k: k for k in range(n_prev)},
        compiler_params=_params(("parallel",)),
    )(own, recv, w, m, v, *(prev or ()))


REPLICATED = (("mix_norm", (D_MODEL,)), ("q_norm", (HEAD_DIM,)), ("k_norm", (HEAD_DIM,)), ("sinks", (N_Q_HEADS,)),
              ("sgu_norm", (SGU_WIDTH,)), ("w_s", (SGU_GROUPS, BLOCK, BLOCK)), ("b_s", (SGU_GROUPS, BLOCK)),
              ("ffn_norm", (D_MODEL,)), ("conv_b", (2 * D_FF,)))
TRANSPOSED = ("w_in", "w_up")
SHARDED = (("w_in", "rows"), ("w_oa", "cols"), ("w_ob", "cols"), ("w_out", "rows"), ("w_up", "rows"),
           ("conv_w", "blocks"), ("w_down", "rows"))
WEIGHT_ORDER = ("mix_norm", "w_in", "q_norm", "k_norm", "sinks", "sgu_norm", "w_s", "b_s", "w_oa", "w_ob", "w_out",
                "ffn_norm", "w_up", "conv_w", "conv_b", "w_down")
MIXER_WEIGHTS = ["w_in", "w_oa", "w_ob", "w_out"]
FFN_WEIGHTS = ["w_up", "conv_w", "w_down"]


def _small_layout():
    segs, off = {}, 0
    tile = SUBLANES * LANES
    for l in range(DEPTH):
        for name, shape in REPLICATED:
            n = math.prod(shape)
            segs[(l, name)] = (off, n)
            off += -(-n // tile) * tile
    per_dev = -(-(off + 1) // (N_DEV * tile)) * tile
    return segs, off, per_dev


def _pack_small(grads, loss_part):
    ssegs, total, per_dev = _small_layout()
    ends = [o for o, _ in ssegs.values()][1:] + [total]
    pieces = []
    for ((l, name), (o, n)), end in zip(ssegs.items(), ends):
        pieces += [grads[l][name].reshape(-1)] + ([jnp.zeros((end - o - n,), F32)] if end > o + n else [])
    pieces += [loss_part.reshape(1), jnp.zeros((N_DEV * per_dev - total - 1,), F32)]
    return jnp.concatenate(pieces).reshape(N_DEV, per_dev // LANES, LANES)


def _unpack_small(gathered):
    ssegs, total, _ = _small_layout()
    flat = gathered.reshape(-1)
    shapes = dict(REPLICATED)
    small = {name: jnp.stack([flat[ssegs[(l, name)][0]:ssegs[(l, name)][0] + ssegs[(l, name)][1]].reshape(shapes[name])
                              for l in range(DEPTH)]) for name, _ in REPLICATED}
    return small, flat[total]


def kernel(x, mix_norm, w_in, q_norm, k_norm, sinks, sgu_norm, w_s, b_s, w_oa, w_ob, w_out, ffn_norm, w_up, conv_w, conv_b, w_down, loss_target, m_mix_norm, m_w_in, m_q_norm, m_k_norm, m_sinks, m_sgu_norm, m_w_s, m_b_s, m_w_oa, m_w_ob, m_w_out, m_ffn_norm, m_w_up, m_conv_w, m_conv_b, m_w_down, v_mix_norm, v_w_in, v_q_norm, v_k_norm, v_sinks, v_sgu_norm, v_w_s, v_b_s, v_w_oa, v_w_ob, v_w_out, v_ffn_norm, v_w_up, v_conv_w, v_conv_b, v_w_down):
    W = dict(mix_norm=mix_norm, w_in=w_in, q_norm=q_norm, k_norm=k_norm, sinks=sinks, sgu_norm=sgu_norm, w_s=w_s, b_s=b_s,
             w_oa=w_oa, w_ob=w_ob, w_out=w_out, ffn_norm=ffn_norm, w_up=w_up, conv_w=conv_w, conv_b=conv_b, w_down=w_down)
    M = dict(mix_norm=m_mix_norm, w_in=m_w_in, q_norm=m_q_norm, k_norm=m_k_norm, sinks=m_sinks, sgu_norm=m_sgu_norm,
             w_s=m_w_s, b_s=m_b_s, w_oa=m_w_oa, w_ob=m_w_ob, w_out=m_w_out, ffn_norm=m_ffn_norm, w_up=m_w_up,
             conv_w=m_conv_w, conv_b=m_conv_b, w_down=m_w_down)
    V = dict(mix_norm=v_mix_norm, w_in=v_w_in, q_norm=v_q_norm, k_norm=v_k_norm, sinks=v_sinks, sgu_norm=v_sgu_norm,
             w_s=v_w_s, b_s=v_b_s, w_oa=v_w_oa, w_ob=v_w_ob, w_out=v_w_out, ffn_norm=v_ffn_norm, w_up=v_w_up,
             conv_w=v_conv_w, conv_b=v_conv_b, w_down=v_w_down)
    n_seq, seq, d_model = x.shape
    tokens = n_seq * seq
    mx, my, mc = _my_place()
    place = jnp.stack([mc, 2 * mx + my]).astype(jnp.int32)
    half = N_DEV // 2
    kind_of = dict(SHARDED)
    for name in TRANSPOSED:
        W[name], M[name], V[name] = (jnp.swapaxes(t[name], 1, 2) for t in (W, M, V))

    gather_groups = [[(0, MIXER_WEIGHTS[0])], [(0, n) for n in MIXER_WEIGHTS[1:]], [(0, n) for n in FFN_WEIGHTS],
                     [(1, n) for n in MIXER_WEIGHTS], [(1, n) for n in FFN_WEIGHTS]]
    started, in_flight = {}, {}
    weights = []
    for l in range(DEPTH):
        w = {name: W[name][l] for name, _ in REPLICATED}
        w["cb_g"], w["cb_v"] = W["conv_b"][l][:D_FF], W["conv_b"][l][D_FF:]
        w["bias_full"] = jnp.repeat(W["b_s"][l].T, SGU_WIDTH // SGU_GROUPS, axis=1)
        weights.append(w)

    def gather_start(gi, after=()):
        stacks = [W[name] for _, name in gather_groups[gi]]
        kinds = [kind_of[name] for _, name in gather_groups[gi]]
        shapes = [s.shape[1:] for s in stacks]
        lands = _place_own(stacks, [l for l, _ in gather_groups[gi]], kinds,
                           [F32 if name == "conv_w" else BF16 for _, name in gather_groups[gi]],
                           name=f"gather_weights_own_{gi}", deps=after)
        send, recv, lands, token = _gather_start(lands, kinds, shapes, after, name=f"gather_weights_start_{gi}")
        started[gi] = dict(sems=(send, recv), lands=lands, kinds=kinds, shapes=shapes)
        return token

    def gather_forward(gi, after):
        st = started[gi]
        in_flight[gi] = _gather_forward(st["sems"][1], st["lands"], st["kinds"], st["shapes"], after,
                                        name=f"gather_weights_forward_{gi}")
        return in_flight[gi][3]

    def gather_finish(gi, after):
        st = started.pop(gi)
        fwd_send, fwd_recv, lands_g, _ = in_flight.pop(gi)
        whole = _gather_finish(st["sems"][0], st["sems"][1], fwd_send, fwd_recv, lands_g, st["kinds"], st["shapes"], after,
                               name=f"gather_weights_finish_{gi}")
        for (l, name), arr in zip(gather_groups[gi], whole):
            w = weights[l]
            if name in TRANSPOSED:
                w[name + "_t"] = arr
            elif name == "conv_w":
                w["cw_g"] = arr[:half].transpose(1, 0, 2).reshape(3, D_FF)
                w["cw_v"] = arr[half:].transpose(1, 0, 2).reshape(3, D_FF)
            else:
                w[name] = arr

    reduce_state, results = {}, {}
    wire = {"conv_w": F32, "small": F32}

    def reduce_begin(key, names, arrays):
        send, recv, srcs_, lands_, token = _exchange_start(arrays, _pair_plan, N_CHIPS, name=f"reduce_pair_start_{key}")
        reduce_state[key] = dict(names=names, pair=(send, recv, srcs_, lands_))
        return [token]

    def reduce_pair(key, after):
        st = reduce_state[key]
        send, recv, srcs_, lands_ = st.pop("pair")
        blocked_, from_sibling = _exchange_wait(send, recv, srcs_, lands_, _pair_plan, N_CHIPS, after,
                                                name=f"reduce_pair_wait_{key}")
        sums = [_reduce_pair_sum(b, r, place, wire.get(n if isinstance(n, str) else n[1], BF16),
                                 name=f"reduce_pair_sum_{key}_{i}")
                for i, (n, b, r) in enumerate(zip(st["names"], blocked_, from_sibling))]
        st["own"] = [s[0] for s in sums]
        *st["chip"], token = _exchange_start([s[1] for s in sums], _chip_plan, N_CHIPS - 1, name=f"reduce_chip_start_{key}")
        return [token]

    def reduce_end(key, after):
        st = reduce_state.pop(key)
        send, recv, srcs_, lands_ = st["chip"]
        _, from_chips = _exchange_wait(send, recv, srcs_, lands_, _chip_plan, N_CHIPS - 1, after,
                                       name=f"reduce_chip_wait_{key}")
        done = []
        for n, own, got in zip(st["names"], st["own"], from_chips):
            if n == "small":
                results["small"] = _reduce_chip_sum(own, got, name="reduce_chip_sum_small")
            else:
                l, name = n
                results[name] = _reduce_adamw(own, got, W[name], M[name], V[name], l, results.get(name),
                                              name=f"l{l}_reduce_adamw_{name}")
                done.append(results[name][0])
        return done

    def sched(point, l, carry, g=None):
        deps = []
        if point == "begin":
            token = ()
            for gi in range(len(gather_groups)):
                token = [gather_start(gi, token)]
            deps = [gather_forward(0, token[0])]
        elif point == "fwd_start" and l == 0:
            gather_finish(0, carry)
            deps = [gather_forward(1, weights[0]["w_in_t"])]
        elif point == "fwd_att" and l == 0:
            gather_finish(1, carry)
            deps = [gather_forward(2, carry)]
        elif point == "fwd_mixer_done" and l == 0:
            gather_finish(2, carry)
        elif point == "fwd_conv" and l == 0:
            deps = [gather_forward(3, carry)]
        elif point == "fwd_start" and l == 1:
            gather_finish(3, carry)
        elif point == "fwd_att" and l == 1:
            deps = [gather_forward(4, carry)]
        elif point == "fwd_mixer_done" and l == 1:
            gather_finish(4, carry)
        elif point == "bwd_ffn_grads":
            conv_w = jnp.concatenate([g[k].reshape(3, half, W_UP_SHARD).transpose(1, 0, 2) for k in ("cw_g", "cw_v")])
            deps = reduce_begin(
                f"l{l}_ffn", [(l, "w_down"), (l, "w_up"), (l, "conv_w")],
                [g["w_down"].reshape(N_DEV, D_FF // N_DEV, D_MODEL),
                 g["w_up_t"].reshape(N_DEV, W_UP_SHARD, D_MODEL), conv_w])
        elif point == "bwd_merge":
            deps = reduce_pair(f"l{l}_ffn", carry)
        elif point == "bwd_out_grads":
            deps = reduce_begin(
                f"l{l}_out", [(l, "w_out"), (l, "w_oa"), (l, "w_ob")],
                [g["w_out"].reshape(N_DEV, D_MODEL // N_DEV, D_MODEL),
                 _disassemble((g["w_oa"],), LANES, _w_o_moves(), name=f"l{l}_split_dw_oa"),
                 _disassemble((g["w_ob"],), LANES, _w_o_moves(), name=f"l{l}_split_dw_ob")])
        elif point == "bwd_att":
            deps = reduce_pair(f"l{l}_out", carry)
        elif point == "bwd_w_in_grad":
            deps = reduce_begin(f"l{l}_in", [(l, "w_in")], [g["w_in_t"].reshape(N_DEV, W_IN_SHARD, D_MODEL)])
        elif point == "bwd_dh":
            deps = reduce_pair(f"l{l}_in", carry)
        return deps

    loss_part, dx, grads, last_deps = _local_step(x.reshape(tokens, d_model), loss_target.reshape(tokens, d_model),
                                                  weights, sched, n_seq=n_seq, seq=seq)
    for g in grads:
        g["conv_b"] = jnp.concatenate([g["cb_g"], g["cb_v"]])
    after = [dx, *last_deps, *reduce_begin("small", ["small"], [_pack_small(grads, loss_part)])]
    for key in [f"l{l}_{part}" for l in reversed(range(DEPTH)) for part in ("ffn", "out", "in")][:-1]:
        after = reduce_end(key, after)
    after = reduce_end("l0_in", after + reduce_pair("small", after))
    reduce_end("small", after)

    G, delta, new_m, new_v = {}, {}, {}, {}
    for name, _ in SHARDED:
        outs = [jnp.swapaxes(o, 1, 2) for o in results[name]] if name in TRANSPOSED else results[name]
        G[name], delta[name], new_m[name], new_v[name] = outs
    small, loss = _unpack_small(_gather([results["small"]], ["blocks"], name="gather_small_grads")[0])
    G.update(small)
    for name, _ in REPLICATED:
        delta[name], new_m[name], new_v[name] = _adamw(W[name], G[name], M[name], V[name], name=f"adamw_{name}")
    return (loss, dx.reshape(n_seq, seq, d_model), *[G[n] for n in WEIGHT_ORDER], *[delta[n] for n in WEIGHT_ORDER],
            *[new_m[n] for n in WEIGHT_ORDER], *[new_v[n] for n in WEIGHT_ORDER])
```

```python
import math

import jax
import jax.numpy as jnp
from jax import lax
from jax.experimental import pallas as pl
from jax.experimental.pallas import tpu as pltpu

F32 = jnp.float32
BF16 = jnp.bfloat16
ACT_DTYPE = BF16
MESH = pl.DeviceIdType.MESH

DEPTH = 2
D_MODEL = 1024
N_Q_HEADS = 8
HEAD_DIM = 64
ATT_WIDTH = 512
KV_WIDTH = 128
BLOCK = 128
SGU_WIDTH = 512
SGU_GROUPS = 8
IN_WIDTH = 3840
D_FF = 2816
NORM_EPS = 1e-6
NEG_INF = -1e30
ATT_SCALE = HEAD_DIM ** -0.5
ALIBI_SLOPES = tuple(2.0 ** (-(h + 1)) for h in range(N_Q_HEADS))
ADAM_LR, ADAM_B1, ADAM_B2, ADAM_EPS, ADAM_WD, ADAM_STEP = 0.001, 0.9, 0.999, 1e-08, 0.01, 10
N_DEV = 8
N_CHIPS = 4

QKV_WIDTH = ATT_WIDTH + 2 * KV_WIDTH
COL_SUV, COL_GA, COL_GB, COL_QKV = 0, 1024, 2048, 3072
W_IN_ROTATE = (1, IN_WIDTH // QKV_WIDTH)

LANES = 128
SUBLANES = 8
VMEM_LIMIT_V7X = 56 * 1024 * 1024
GELU_C = math.sqrt(2.0 / math.pi)
GELU_K = 0.044715
ANY = pl.BlockSpec(memory_space=pl.ANY)


def _params(sem=None):
    return pltpu.CompilerParams(dimension_semantics=sem, vmem_limit_bytes=VMEM_LIMIT_V7X)


def _sigmoid(x):
    return 1.0 / (1.0 + jnp.exp(-x))


def _gelu(x):
    th = jnp.tanh(GELU_C * (x + GELU_K * x * x * x))
    return 0.5 * x * (1.0 + th)


def _gelu_and_grad(x):
    x2 = x * x
    th = jnp.tanh(GELU_C * (x + GELU_K * x2 * x))
    g = 0.5 * x * (1.0 + th)
    dg = 0.5 * (1.0 + th) + 0.5 * x * (1.0 - th * th) * (GELU_C * (1.0 + 3.0 * GELU_K * x2))
    return g, dg


def _dot(a, b, dims):
    return lax.dot_general(a, b, (dims, ((), ())), preferred_element_type=F32)


def _dot_nn(a, b):
    return _dot(a, b, ((1,), (0,)))


def _dot_nt(a, b):
    return _dot(a, b, ((1,), (1,)))


def _dot_tn(a, b):
    return _dot(a, b, ((0,), (0,)))


def _lo_mask(shape):
    return lax.broadcasted_iota(jnp.int32, shape, len(shape) - 1) < (LANES // 2)


def _half_sums(x, lo):
    s_lo = jnp.sum(jnp.where(lo, x, 0.0), axis=-1, keepdims=True)
    s_all = jnp.sum(x, axis=-1, keepdims=True)
    return jnp.where(lo, s_lo, s_all - s_lo)


def _dup_half(x, half, lo):
    r = pltpu.roll(x, LANES // 2, axis=1)
    return jnp.where(lo, x, r) if half == 0 else jnp.where(lo, r, x)


def _with_deps(body, n_in, deps):
    k = len(deps)
    if not k:
        return body, [], ()

    def skipping(*refs):
        return body(*refs[:n_in], *refs[n_in + k:])

    return skipping, [ANY] * k, tuple(deps)


MM_VMEM_BUDGET = 40 * 1024 * 1024
MM_MAX_TILE = 1408
MM_MAX_TK = 4096
MM_STEP_BYTES = 1 << 20


def _divisors(n, step, cap):
    return [d for d in range(step, min(n, cap) + 1, step) if n % d == 0] or [n]


def _mm_tiles(M, N, K, out_bytes, tm_divides, tn_divides):
    best = None
    for tm in _divisors(M, LANES, MM_MAX_TILE):
        for tn in _divisors(N, LANES, MM_MAX_TILE):
            if tm_divides % tm or tn_divides % tn:
                continue
            for tk in _divisors(K, 4 * LANES, MM_MAX_TK):
                vmem = 4 * (tm * tk + tk * tn) + 2 * tm * tn * out_bytes + (0 if tk == K else 4 * tm * tn)
                if vmem > MM_VMEM_BUDGET:
                    continue
                traffic = 2 * M * K * (N // tn) + 2 * K * N * (M // tm) + M * N * out_bytes
                cost = traffic + (K // tk - 1) * 8 * M * N + (M // tm) * (N // tn) * (K // tk) * MM_STEP_BYTES
                if best is None or cost < best[0]:
                    best = (cost, tm, tn, tk)
    assert best is not None, (M, N, K)
    return best[1:]


def _mm(a, b, *, mode, out_dtype, name, deps=(), b_rows=(0, None), rotate=None, out_rows=(0, None), out_prev=None):
    b_first, b_count = b_rows
    if mode == "nn":
        (M, K), N = a.shape, b.shape[1]
    elif mode == "nt":
        (M, K), N = a.shape, (b.shape[0] if b_count is None else b_count)
    else:
        (K, M), N = a.shape, b.shape[1]
    shift, period = rotate or (0, 1)
    assert period == 1 or mode == "nt"
    out_first, out_total = out_rows[0], (M if out_rows[1] is None else out_rows[1])
    tm, tn, tk = _mm_tiles(M, N, K, jnp.dtype(out_dtype).itemsize, math.gcd(M, out_first),
                           math.gcd(N // period, b_first if mode == "nt" else 0))
    gm, gn, gk = M // tm, N // tn, K // tk

    def turned(j):
        per = N // period // tn
        return ((j // per + shift) % period) * per + j % per if period > 1 else j

    if mode == "nn":
        a_spec = pl.BlockSpec((tm, tk), lambda i, j, k: (i, k))
        b_spec = pl.BlockSpec((tk, tn), lambda i, j, k: (k + b_first // tk, j))
        contract = ((1,), (0,))
    elif mode == "nt":
        a_spec = pl.BlockSpec((tm, tk), lambda i, j, k: (i, k))
        b_spec = pl.BlockSpec((tn, tk), lambda i, j, k: (turned(j) + b_first // tn, k))
        contract = ((1,), (1,))
    else:
        a_spec = pl.BlockSpec((tk, tm), lambda i, j, k: (k, i))
        b_spec = pl.BlockSpec((tk, tn), lambda i, j, k: (k, j))
        contract = ((0,), (0,))
    o_spec = pl.BlockSpec((tm, tn), lambda i, j, k: (i + out_first // tm, j))
    assert b_first % (tk if mode == "nn" else tn) == 0 and out_first % tm == 0, (name, tm, tn, tk)
    n_prev = 0 if out_prev is None else 1

    def body(a_ref, b_ref, *rest):
        o_ref = rest[n_prev]
        part = _dot(a_ref[...].astype(BF16), b_ref[...].astype(BF16), contract)
        if gk == 1:
            o_ref[...] = part.astype(out_dtype)
            return
        acc_ref = rest[n_prev + 1]
        k = pl.program_id(2)

        @pl.when(k == 0)
        def _():
            acc_ref[...] = part

        @pl.when(k > 0)
        def _():
            acc_ref[...] += part

        @pl.when(k == gk - 1)
        def _():
            o_ref[...] = acc_ref[...].astype(out_dtype)

    body, dep_specs, dep_args = _with_deps(body, 2 + n_prev, deps)
    return pl.pallas_call(
        body,
        name=name,
        grid=(gm, gn, gk),
        in_specs=[a_spec, b_spec] + [ANY] * n_prev + dep_specs,
        out_specs=o_spec,
        out_shape=jax.ShapeDtypeStruct((out_total, N), out_dtype),
        input_output_aliases={2: 0} if n_prev else {},
        scratch_shapes=[] if gk == 1 else [pltpu.VMEM((tm, tn), F32)],
        compiler_params=_params(("parallel", "parallel", "arbitrary")),
    )(a, b, *([out_prev] if n_prev else []), *dep_args)


def _mm_tn_parts(parts, at, b, *, name):
    K, N = b.shape
    n = len(parts)
    tm = math.gcd(*[p.shape[1] for p in parts], *at)
    tiles = [p.shape[1] // tm for p in parts]
    first = [sum(tiles[:p]) for p in range(n)]

    def mine(i, p):
        return jnp.logical_and(i >= first[p], i < first[p] + tiles[p])

    def out_tile(i):
        t = 0
        for p in range(n):
            t = jnp.where(mine(i, p), at[p] // tm + i - first[p], t)
        return t

    def body(*refs):
        a_refs, b_ref, o_ref = refs[:n], refs[n], refs[n + 1]
        for p in range(n):
            @pl.when(mine(pl.program_id(0), p))
            def _(p=p):
                o_ref[...] = _dot_tn(a_refs[p][...], b_ref[...])

    return pl.pallas_call(
        body, name=name, grid=(sum(tiles),),
        in_specs=[pl.BlockSpec((K, tm), lambda i, p=p: (0, jnp.clip(i - first[p], 0, tiles[p] - 1))) for p in range(n)]
        + [pl.BlockSpec((K, N), lambda i: (0, 0), pipeline_mode=pl.Buffered(1))],
        out_specs=pl.BlockSpec((tm, N), lambda i: (out_tile(i), 0)),
        out_shape=jax.ShapeDtypeStruct((sum(p.shape[1] for p in parts), N), F32),
        compiler_params=_params(("arbitrary",)),
    )(*parts, b)


def _mm_rows(a, b, *, mode, fn, out_dtypes, rows=(), vecs=(), reduce=False, name, deps=(), b_rows=(0, None), a_at=None):
    parts = a if a_at is not None else (a,)
    starts = a_at if a_at is not None else (0,)
    n_parts = len(parts)
    M, K = parts[0].shape[0], sum(p.shape[1] for p in parts)
    b_first, b_count = b_rows[0], (b.shape[0] if b_rows[1] is None else b_rows[1])
    N = b.shape[1] if mode == "nn" else b_count
    contract = ((1,), (0,)) if mode == "nn" else ((1,), (1,))
    n_rows, n_vecs, n_out = len(rows), len(vecs), len(out_dtypes)
    out_bytes = sum(jnp.dtype(d).itemsize for d in out_dtypes)
    tm = max(t for t in _divisors(M, LANES, MM_MAX_TILE)
             if 4 * t * K + 2 * K * N + 2 * t * N * (4 * n_rows + out_bytes) <= MM_VMEM_BUDGET)
    assert b_first % b_count == 0 and (a_at is None or mode == "nn")

    def body(*refs):
        a_refs, b_ref, rest = refs[:n_parts], refs[n_parts], refs[n_parts + 1:]
        row_refs, vec_refs = rest[:n_rows], rest[n_rows:n_rows + n_vecs]
        out_refs = rest[n_rows + n_vecs:]
        if a_at is None:
            acc = _dot(a_refs[0][...], b_ref[...], contract)
        else:
            acc = sum(_dot(r[...], b_ref[at:at + r.shape[1], :], contract) for r, at in zip(a_refs, starts))
        res = fn(acc, *[r[...] for r in row_refs], *[v[...] for v in vec_refs])
        for o_ref, val in zip(out_refs[:n_out], res):
            o_ref[...] = val.astype(o_ref.dtype)
        if reduce:
            @pl.when(pl.program_id(0) == 0)
            def _():
                out_refs[n_out][...] = res[n_out]

            @pl.when(pl.program_id(0) > 0)
            def _():
                out_refs[n_out][...] += res[n_out]

    row = pl.BlockSpec((tm, N), lambda i: (i, 0))
    vec = pl.BlockSpec((1, N), lambda i: (0, 0))
    body, dep_specs, dep_args = _with_deps(body, n_parts + 1 + n_rows + n_vecs, deps)
    return pl.pallas_call(
        body, name=name, grid=(M // tm,),
        in_specs=[pl.BlockSpec((tm, p.shape[1]), lambda i: (i, 0)) for p in parts]
        + [pl.BlockSpec((b_count, b.shape[1]), lambda i: (b_first // b_count, 0), pipeline_mode=pl.Buffered(1))]
        + [row] * n_rows + [vec] * n_vecs + dep_specs,
        out_specs=[row] * n_out + [vec] * reduce,
        out_shape=[jax.ShapeDtypeStruct((M, N), d) for d in out_dtypes] + [jax.ShapeDtypeStruct((1, N), F32)] * reduce,
        compiler_params=_params(("arbitrary",)),
    )(*parts, b, *rows, *[v.reshape(1, N) for v in vecs], *dep_args)


def _rms(x, gain):
    return x * lax.rsqrt(jnp.mean(x * x, axis=-1, keepdims=True) + NORM_EPS) * gain


def _residual_then_norm(acc, x, gain):
    x_out = x + acc
    return x_out, _rms(x_out, gain)


def _residual_then_loss(acc, x, target):
    err = (x + acc) - target
    dy = err * (1.0 / D_MODEL)
    return dy, dy, jnp.sum(err * err, axis=0, keepdims=True) * (0.5 / D_MODEL)


def _rms_bwd_rows(dh, x, dres, gain):
    r = lax.rsqrt(jnp.mean(x * x, axis=-1, keepdims=True) + NORM_EPS)
    xh = x * r
    dxh = dh * gain
    dx = dres + r * (dxh - xh * jnp.mean(dxh * xh, axis=-1, keepdims=True))
    return dx, dx, jnp.sum(dh * xh, axis=0, keepdims=True)


def _rms_fwd(x, gain, *, name, tm=512, deps=()):
    T, D = x.shape

    def body(x_ref, g_ref, h_ref):
        xv = x_ref[...]
        r = lax.rsqrt(jnp.mean(xv * xv, axis=-1, keepdims=True) + NORM_EPS)
        h_ref[...] = (xv * r * g_ref[...]).astype(BF16)

    body, dep_specs, dep_args = _with_deps(body, 2, deps)
    return pl.pallas_call(
        body, name=name, grid=(T // tm,),
        in_specs=[pl.BlockSpec((tm, D), lambda i: (i, 0)), pl.BlockSpec((1, D), lambda i: (0, 0))] + dep_specs,
        out_specs=pl.BlockSpec((tm, D), lambda i: (i, 0)),
        out_shape=jax.ShapeDtypeStruct((T, D), BF16),
        compiler_params=_params(("parallel",)),
    )(x, gain.reshape(1, D), *dep_args)


def _head_norm(x, gain2, lo):
    ms = _half_sums(x * x, lo) * (1.0 / HEAD_DIM)
    r = lax.rsqrt(ms + NORM_EPS)
    xh = x * r
    return xh * gain2, xh, r


def _head_norm_bwd(xh, r, gain2, dy, lo):
    dxh = dy * gain2
    dx = r * (dxh - xh * (_half_sums(dxh * xh, lo) * (1.0 / HEAD_DIM)))
    return dx, dy * xh


Q_GROUP = N_Q_HEADS // 2
GROUP_ROWS = Q_GROUP * BLOCK
ATT_SCRATCH = (pltpu.VMEM((2, 2, GROUP_ROWS, BLOCK), F32), pltpu.VMEM((2, GROUP_ROWS, 1), F32))


def _att_consts(sink_ref, bias_ref, sinkcol_ref):
    row = lax.broadcasted_iota(jnp.int32, (GROUP_ROWS, BLOCK), 0)
    kj = lax.broadcasted_iota(jnp.int32, (GROUP_ROWS, BLOCK), 1)
    head = row // BLOCK
    head_col = lax.broadcasted_iota(jnp.int32, (GROUP_ROWS, 1), 0) // BLOCK
    d_cur = (row % BLOCK) - kj
    d_prev = d_cur + BLOCK
    for kv in range(2):
        slope = jnp.zeros((GROUP_ROWS, BLOCK), F32)
        sink = jnp.zeros((GROUP_ROWS, 1), F32)
        for r in range(Q_GROUP):
            slope = jnp.where(head == r, ALIBI_SLOPES[Q_GROUP * kv + r], slope)
            sink = jnp.where(head_col == r, sink_ref[Q_GROUP * kv + r], sink)
        bias_ref[kv, 0] = jnp.where(d_cur >= 0, -slope * d_cur.astype(F32), NEG_INF)
        bias_ref[kv, 1] = jnp.where(d_prev < BLOCK, -slope * d_prev.astype(F32), NEG_INF)
        sinkcol_ref[kv] = sink


def _stack_heads(t0, t1, lo):
    z = jnp.zeros_like(t0)
    return jnp.concatenate([jnp.where(lo, t0, z), jnp.where(lo, z, t0), jnp.where(lo, t1, z), jnp.where(lo, z, t1)], axis=0)


def _unstack_heads(x4, lo):
    return (jnp.where(lo, x4[0:BLOCK], x4[BLOCK:2 * BLOCK]), jnp.where(lo, x4[2 * BLOCK:3 * BLOCK], x4[3 * BLOCK:]))


def _att_probs(q4, k2c, k2p, bias_c, bias_p, sink, has_prev):
    s_c = _dot_nt(q4, k2c) * ATT_SCALE + bias_c
    s_p = jnp.where(has_prev, _dot_nt(q4, k2p) * ATT_SCALE + bias_p, NEG_INF)
    m = jnp.maximum(jnp.max(jnp.maximum(s_c, s_p), axis=-1, keepdims=True), sink)
    e_c = jnp.exp(s_c - m)
    e_p = jnp.exp(s_p - m)
    e_s = jnp.exp(sink - m)
    inv = 1.0 / (jnp.sum(e_c + e_p, axis=-1, keepdims=True) + e_s)
    return e_c * inv, e_p * inv, e_s * inv


def _attention_fwd(proj, q_gain, k_gain, sinks, *, n_seq, seq, name):
    T = n_seq * seq
    nb = seq // BLOCK
    qcol, kvcol = COL_QKV // ATT_WIDTH, (COL_QKV + ATT_WIDTH) // (2 * KV_WIDTH)

    def body(q_ref, kv_ref, qg_ref, kg_ref, sink_ref, y_ref, bias_ref, sinkcol_ref):
        lo = _lo_mask((BLOCK, LANES))
        qg, kg = qg_ref[...], kg_ref[...]
        _att_consts(sink_ref, bias_ref, sinkcol_ref)

        def block(i, carry):
            r0 = pl.multiple_of(i * BLOCK, BLOCK)
            rp = pl.multiple_of(jnp.maximum(i - 1, 0) * BLOCK, BLOCK)
            has_prev = i > 0
            kn_c = _head_norm(kv_ref[pl.ds(r0, BLOCK), 0:KV_WIDTH].astype(F32), kg, lo)[0].astype(BF16)
            kn_p = _head_norm(kv_ref[pl.ds(rp, BLOCK), 0:KV_WIDTH].astype(F32), kg, lo)[0].astype(BF16)
            v_c = kv_ref[pl.ds(r0, BLOCK), KV_WIDTH:2 * KV_WIDTH].astype(BF16)
            v_p = kv_ref[pl.ds(rp, BLOCK), KV_WIDTH:2 * KV_WIDTH].astype(BF16)
            for kv in range(2):
                k2c, k2p = _dup_half(kn_c, kv, lo), _dup_half(kn_p, kv, lo)
                v2c, v2p = _dup_half(v_c, kv, lo), _dup_half(v_p, kv, lo)
                cols = [slice((2 * kv + t) * LANES, (2 * kv + t + 1) * LANES) for t in range(2)]
                qn = [_head_norm(q_ref[pl.ds(r0, BLOCK), c].astype(F32), qg, lo)[0] for c in cols]
                q4 = _stack_heads(qn[0], qn[1], lo).astype(BF16)
                p_c, p_p, _ = _att_probs(q4, k2c, k2p, bias_ref[kv, 0], bias_ref[kv, 1], sinkcol_ref[kv], has_prev)
                o4 = _dot_nn(p_c.astype(BF16), v2c) + _dot_nn(p_p.astype(BF16), v2p)
                for c, out in zip(cols, _unstack_heads(o4, lo)):
                    y_ref[pl.ds(r0, BLOCK), c] = out.astype(BF16)
            return carry

        lax.fori_loop(0, nb, block, 0)

    vec = pl.BlockSpec((1, LANES), lambda b: (0, 0))
    return pl.pallas_call(
        body, name=name, grid=(n_seq,),
        in_specs=[pl.BlockSpec((seq, ATT_WIDTH), lambda b: (b, qcol)),
                  pl.BlockSpec((seq, 2 * KV_WIDTH), lambda b: (b, kvcol)),
                  vec, vec, pl.BlockSpec(memory_space=pltpu.SMEM)],
        out_specs=pl.BlockSpec((seq, ATT_WIDTH), lambda b: (b, 0)),
        out_shape=jax.ShapeDtypeStruct((T, ATT_WIDTH), BF16),
        scratch_shapes=list(ATT_SCRATCH),
        compiler_params=_params(("parallel",)),
    )(proj, proj, jnp.tile(q_gain, 2).reshape(1, LANES), jnp.tile(k_gain, 2).reshape(1, LANES), sinks)


def _attention_bwd(proj, dy, q_gain, k_gain, sinks, *, n_seq, seq, name, deps=()):
    T = n_seq * seq
    nb = seq // BLOCK
    qcol, kvcol = COL_QKV // ATT_WIDTH, (COL_QKV + ATT_WIDTH) // (2 * KV_WIDTH)

    def body(q_ref, kv_ref, dy_ref, qg_ref, kg_ref, sink_ref, dqkv_ref, dqg_ref, dkg_ref, dsink_ref,
             dkn_acc, dv_acc, qg_acc, kg_acc, sink_acc, bias_ref, sinkcol_ref):
        lo = _lo_mask((BLOCK, LANES))
        qg, kg = qg_ref[...], kg_ref[...]
        _att_consts(sink_ref, bias_ref, sinkcol_ref)
        first = pl.program_id(0) == 0

        @pl.when(first)
        def _():
            qg_acc[...] = jnp.zeros_like(qg_acc)
            kg_acc[...] = jnp.zeros_like(kg_acc)
            sink_acc[...] = jnp.zeros_like(sink_acc)

        dkn_acc[...] = jnp.zeros_like(dkn_acc)
        dv_acc[...] = jnp.zeros_like(dv_acc)

        def block(i, carry):
            r0 = pl.multiple_of(i * BLOCK, BLOCK)
            rp = pl.multiple_of(jnp.maximum(i - 1, 0) * BLOCK, BLOCK)
            has_prev = i > 0
            kn_c = _head_norm(kv_ref[pl.ds(r0, BLOCK), 0:KV_WIDTH].astype(F32), kg, lo)[0].astype(BF16)
            kn_p = _head_norm(kv_ref[pl.ds(rp, BLOCK), 0:KV_WIDTH].astype(F32), kg, lo)[0].astype(BF16)
            v_c = kv_ref[pl.ds(r0, BLOCK), KV_WIDTH:2 * KV_WIDTH].astype(BF16)
            v_p = kv_ref[pl.ds(rp, BLOCK), KV_WIDTH:2 * KV_WIDTH].astype(BF16)
            dk_c, dk_p, dv_c, dv_p = [], [], [], []
            for kv in range(2):
                k2c, k2p = _dup_half(kn_c, kv, lo), _dup_half(kn_p, kv, lo)
                v2c, v2p = _dup_half(v_c, kv, lo), _dup_half(v_p, kv, lo)
                cols = [slice((2 * kv + t) * LANES, (2 * kv + t + 1) * LANES) for t in range(2)]
                normed = [_head_norm(q_ref[pl.ds(r0, BLOCK), c].astype(F32), qg, lo) for c in cols]
                q4 = _stack_heads(normed[0][0], normed[1][0], lo).astype(BF16)
                do4 = _stack_heads(dy_ref[pl.ds(r0, BLOCK), cols[0]], dy_ref[pl.ds(r0, BLOCK), cols[1]], lo)
                p_c, p_p, p_s = _att_probs(q4, k2c, k2p, bias_ref[kv, 0], bias_ref[kv, 1], sinkcol_ref[kv], has_prev)
                dp_c = _dot_nt(do4, v2c)
                dp_p = _dot_nt(do4, v2p)
                delta = jnp.sum(p_c * dp_c + p_p * dp_p, axis=-1, keepdims=True)
                ds_c = (p_c * (dp_c - delta)).astype(BF16)
                ds_p = (p_p * (dp_p - delta)).astype(BF16)
                sink_acc[kv] += -(p_s * delta)
                dq4 = (_dot_nn(ds_c, k2c) + _dot_nn(ds_p, k2p)) * ATT_SCALE
                for c, (_, qh, qr), dqn in zip(cols, normed, _unstack_heads(dq4, lo)):
                    dq, dg = _head_norm_bwd(qh, qr, qg, dqn, lo)
                    dqkv_ref[pl.ds(r0, BLOCK), c] = dq.astype(BF16)
                    qg_acc[...] += dg
                dk_c.append(_dot_tn(ds_c, q4))
                dk_p.append(_dot_tn(ds_p, q4))
                dv_c.append(_dot_tn(p_c.astype(BF16), do4))
                dv_p.append(_dot_tn(p_p.astype(BF16), do4))

            def fold(parts):
                a = parts[0] + pltpu.roll(parts[0], LANES // 2, axis=1)
                b = parts[1] + pltpu.roll(parts[1], LANES // 2, axis=1)
                return jnp.where(lo, a, b)

            dkn_acc[pl.ds(r0, BLOCK), :] += fold(dk_c) * ATT_SCALE
            dkn_acc[pl.ds(rp, BLOCK), :] += fold(dk_p) * ATT_SCALE
            dv_acc[pl.ds(r0, BLOCK), :] += fold(dv_c)
            dv_acc[pl.ds(rp, BLOCK), :] += fold(dv_p)
            return carry

        lax.fori_loop(0, nb, block, 0)

        def finish(i, carry):
            r0 = pl.multiple_of(i * BLOCK, BLOCK)
            _, kh, kr = _head_norm(kv_ref[pl.ds(r0, BLOCK), 0:KV_WIDTH].astype(F32), kg, lo)
            dk, dg = _head_norm_bwd(kh, kr, kg, dkn_acc[pl.ds(r0, BLOCK), :], lo)
            dqkv_ref[pl.ds(r0, BLOCK), ATT_WIDTH:ATT_WIDTH + KV_WIDTH] = dk.astype(BF16)
            dqkv_ref[pl.ds(r0, BLOCK), ATT_WIDTH + KV_WIDTH:QKV_WIDTH] = dv_acc[pl.ds(r0, BLOCK), :].astype(BF16)
            kg_acc[...] += dg
            return carry

        lax.fori_loop(0, nb, finish, 0)

        @pl.when(pl.program_id(0) == n_seq - 1)
        def _():
            dqg_ref[...] = jnp.sum(qg_acc[...], axis=0, keepdims=True)
            dkg_ref[...] = jnp.sum(kg_acc[...], axis=0, keepdims=True)
            lane = lax.broadcasted_iota(jnp.int32, (1, LANES), 1)
            dsink = jnp.zeros((1, LANES), F32)
            for kv in range(2):
                for r in range(Q_GROUP):
                    total = jnp.sum(sink_acc[kv, r * BLOCK:(r + 1) * BLOCK, :], axis=0, keepdims=True)
                    dsink = jnp.where(lane == Q_GROUP * kv + r, total, dsink)
            dsink_ref[...] = dsink

    vec = pl.BlockSpec((1, LANES), lambda b: (0, 0))
    acc = pltpu.VMEM((BLOCK, LANES), F32)
    body, dep_specs, dep_args = _with_deps(body, 6, deps)
    dqkv, dqg, dkg, dsink = pl.pallas_call(
        body, name=name, grid=(n_seq,),
        in_specs=[pl.BlockSpec((seq, ATT_WIDTH), lambda b: (b, qcol)),
                  pl.BlockSpec((seq, 2 * KV_WIDTH), lambda b: (b, kvcol)),
                  pl.BlockSpec((seq, ATT_WIDTH), lambda b: (b, 0)),
                  vec, vec, pl.BlockSpec(memory_space=pltpu.SMEM)] + dep_specs,
        out_specs=[pl.BlockSpec((seq, QKV_WIDTH), lambda b: (b, 0)), vec, vec, vec],
        out_shape=[jax.ShapeDtypeStruct((T, QKV_WIDTH), BF16)] + [jax.ShapeDtypeStruct((1, LANES), F32)] * 3,
        scratch_shapes=[pltpu.VMEM((seq, KV_WIDTH), F32), pltpu.VMEM((seq, KV_WIDTH), F32), acc, acc,
                        pltpu.VMEM((2, GROUP_ROWS, 1), F32), *ATT_SCRATCH],
        compiler_params=_params(("arbitrary",)),
    )(proj, proj, dy, jnp.tile(q_gain, 2).reshape(1, LANES), jnp.tile(k_gain, 2).reshape(1, LANES), sinks, *dep_args)
    half = LANES // 2
    return dqkv, dqg[0, :half] + dqg[0, half:], dkg[0, :half] + dkg[0, half:], dsink[0, :N_Q_HEADS]


def _sgu_weights(w_ref):
    r = lax.broadcasted_iota(jnp.int32, (BLOCK, BLOCK), 0)
    c = lax.broadcasted_iota(jnp.int32, (BLOCK, BLOCK), 1)
    return [jnp.where(r >= c, w_ref[g], 0.0).astype(BF16) for g in range(SGU_GROUPS)]


def _sgu_fwd(proj, gain, w_s, bias_full, *, n_seq, seq, name):
    T = n_seq * seq
    nc = seq // BLOCK

    def body(suv_ref, g_ref, w_ref, b_ref, y_ref):
        lo = _lo_mask((BLOCK, LANES))
        wm = _sgu_weights(w_ref)
        gain_v = g_ref[...]

        def chunk(c, carry):
            r0 = pl.multiple_of(c * BLOCK, BLOCK)
            gv = _gelu(suv_ref[pl.ds(r0, BLOCK), SGU_WIDTH:2 * SGU_WIDTH].astype(F32))
            r = lax.rsqrt(jnp.mean(gv * gv, axis=-1, keepdims=True) + NORM_EPS)
            vn = (gv * r * gain_v).astype(BF16)
            for p in range(SGU_WIDTH // LANES):
                cols = slice(p * LANES, (p + 1) * LANES)
                vp = vn[:, cols]
                mixed = jnp.where(lo, _dot_nn(wm[2 * p], vp), _dot_nn(wm[2 * p + 1], vp)) + b_ref[:, cols]
                u = _gelu(suv_ref[pl.ds(r0, BLOCK), cols].astype(F32))
                y_ref[pl.ds(r0, BLOCK), cols] = (u * mixed).astype(BF16)
            return carry

        lax.fori_loop(0, nc, chunk, 0)

    return pl.pallas_call(
        body, name=name, grid=(n_seq,),
        in_specs=[pl.BlockSpec((seq, 2 * SGU_WIDTH), lambda b: (b, COL_SUV // (2 * SGU_WIDTH))),
                  pl.BlockSpec((1, SGU_WIDTH), lambda b: (0, 0)),
                  pl.BlockSpec((SGU_GROUPS, BLOCK, BLOCK), lambda b: (0, 0, 0)),
                  pl.BlockSpec((BLOCK, SGU_WIDTH), lambda b: (0, 0))],
        out_specs=pl.BlockSpec((seq, SGU_WIDTH), lambda b: (b, 0)),
        out_shape=jax.ShapeDtypeStruct((T, SGU_WIDTH), BF16),
        compiler_params=_params(("parallel",)),
    )(proj, gain.reshape(1, SGU_WIDTH), w_s, bias_full)


def _sgu_bwd(proj, dy, gain, w_s, bias_full, *, n_seq, seq, name, deps=()):
    T = n_seq * seq
    nc = seq // BLOCK
    n_tiles = SGU_WIDTH // LANES

    def body(suv_ref, dy_ref, g_ref, w_ref, b_ref, dsuv_ref, dg_ref, dw_ref, db_ref, dg_acc, dw_acc, db_acc):
        lo = _lo_mask((BLOCK, LANES))
        hi = jnp.logical_not(lo)
        wm = _sgu_weights(w_ref)
        wmt = [jnp.where(lax.broadcasted_iota(jnp.int32, (BLOCK, BLOCK), 1) >= lax.broadcasted_iota(jnp.int32, (BLOCK, BLOCK), 0),
                         w_ref[g].T, 0.0).astype(BF16) for g in range(SGU_GROUPS)]
        gain_v = g_ref[...]

        @pl.when(pl.program_id(0) == 0)
        def _():
            dg_acc[...] = jnp.zeros_like(dg_acc)
            dw_acc[...] = jnp.zeros_like(dw_acc)
            db_acc[...] = jnp.zeros_like(db_acc)

        def chunk(c, carry):
            r0 = pl.multiple_of(c * BLOCK, BLOCK)
            gv, dgelu_v = _gelu_and_grad(suv_ref[pl.ds(r0, BLOCK), SGU_WIDTH:2 * SGU_WIDTH].astype(F32))
            r = lax.rsqrt(jnp.mean(gv * gv, axis=-1, keepdims=True) + NORM_EPS)
            vh = gv * r
            vn = (vh * gain_v).astype(BF16)
            dvn_tiles = []
            for p in range(n_tiles):
                cols = slice(p * LANES, (p + 1) * LANES)
                vp = vn[:, cols]
                mixed = jnp.where(lo, _dot_nn(wm[2 * p], vp), _dot_nn(wm[2 * p + 1], vp)) + b_ref[:, cols]
                u, dgelu_u = _gelu_and_grad(suv_ref[pl.ds(r0, BLOCK), cols].astype(F32))
                dyv = dy_ref[pl.ds(r0, BLOCK), cols]
                dsuv_ref[pl.ds(r0, BLOCK), cols] = (dyv * mixed * dgelu_u).astype(BF16)
                dm = dyv * u
                db_acc[:, cols] += dm
                dm_bf = dm.astype(BF16)
                dvn_tiles.append(jnp.where(lo, _dot_nn(wmt[2 * p], dm_bf), _dot_nn(wmt[2 * p + 1], dm_bf)))
                dw_acc[2 * p] += _dot_nt(jnp.where(lo, dm, 0.0).astype(BF16), vp)
                dw_acc[2 * p + 1] += _dot_nt(jnp.where(hi, dm, 0.0).astype(BF16), vp)
            dvn = jnp.concatenate(dvn_tiles, axis=1)
            dg_acc[...] += dvn * vh
            dvh = dvn * gain_v
            dgv = r * (dvh - vh * jnp.mean(dvh * vh, axis=-1, keepdims=True))
            dsuv_ref[pl.ds(r0, BLOCK), SGU_WIDTH:2 * SGU_WIDTH] = (dgv * dgelu_v).astype(BF16)
            return carry

        lax.fori_loop(0, nc, chunk, 0)

        @pl.when(pl.program_id(0) == n_seq - 1)
        def _():
            dg_ref[...] = jnp.sum(dg_acc[...], axis=0, keepdims=True)
            r = lax.broadcasted_iota(jnp.int32, (BLOCK, BLOCK), 0)
            c = lax.broadcasted_iota(jnp.int32, (BLOCK, BLOCK), 1)
            for g in range(SGU_GROUPS):
                dw_ref[g] = jnp.where(r >= c, dw_acc[g], 0.0)
            lane = lax.broadcasted_iota(jnp.int32, (BLOCK, LANES), 1)
            out = jnp.zeros((BLOCK, LANES), F32)
            for p in range(n_tiles):
                tile = db_acc[:, p * LANES:(p + 1) * LANES]
                s_lo = jnp.sum(jnp.where(lo, tile, 0.0), axis=-1, keepdims=True)
                s_hi = jnp.sum(jnp.where(hi, tile, 0.0), axis=-1, keepdims=True)
                out = jnp.where(lane == 2 * p, s_lo, out)
                out = jnp.where(lane == 2 * p + 1, s_hi, out)
            db_ref[...] = out

    body, dep_specs, dep_args = _with_deps(body, 5, deps)
    dsuv, dg, dw, db = pl.pallas_call(
        body, name=name, grid=(n_seq,),
        in_specs=[pl.BlockSpec((seq, 2 * SGU_WIDTH), lambda b: (b, COL_SUV // (2 * SGU_WIDTH))),
                  pl.BlockSpec((seq, SGU_WIDTH), lambda b: (b, 0)),
                  pl.BlockSpec((1, SGU_WIDTH), lambda b: (0, 0)),
                  pl.BlockSpec((SGU_GROUPS, BLOCK, BLOCK), lambda b: (0, 0, 0)),
                  pl.BlockSpec((BLOCK, SGU_WIDTH), lambda b: (0, 0))] + dep_specs,
        out_specs=[pl.BlockSpec((seq, 2 * SGU_WIDTH), lambda b: (b, 0)),
                   pl.BlockSpec((1, SGU_WIDTH), lambda b: (0, 0)),
                   pl.BlockSpec((SGU_GROUPS, BLOCK, BLOCK), lambda b: (0, 0, 0)),
                   pl.BlockSpec((BLOCK, LANES), lambda b: (0, 0))],
        out_shape=[jax.ShapeDtypeStruct((T, 2 * SGU_WIDTH), BF16), jax.ShapeDtypeStruct((1, SGU_WIDTH), F32),
                   jax.ShapeDtypeStruct((SGU_GROUPS, BLOCK, BLOCK), F32), jax.ShapeDtypeStruct((BLOCK, LANES), F32)],
        scratch_shapes=[pltpu.VMEM((BLOCK, SGU_WIDTH), F32), pltpu.VMEM((SGU_GROUPS, BLOCK, BLOCK), F32),
                        pltpu.VMEM((BLOCK, SGU_WIDTH), F32)],
        compiler_params=_params(("arbitrary",)),
    )(proj, dy, gain.reshape(1, SGU_WIDTH), w_s, bias_full, *dep_args)
    return dsuv, dg.reshape(SGU_WIDTH), dw, db[:, :SGU_GROUPS].T


def _merge_fwd(y_att, y_sgu, w_oa, w_ob, proj, *, name, tm=1024, tn=512, deps=()):
    T = y_att.shape[0]

    def body(ya_ref, ys_ref, wa_ref, wb_ref, ga_ref, gb_ref, o_ref):
        pa = _dot_nn(ya_ref[...], wa_ref[...])
        pb = _dot_nn(ys_ref[...], wb_ref[...])
        o_ref[...] = (_sigmoid(ga_ref[...].astype(F32)) * pa + _sigmoid(gb_ref[...].astype(F32)) * pb).astype(BF16)

    act = pl.BlockSpec((tm, ATT_WIDTH), lambda i, j: (i, 0))
    wgt = pl.BlockSpec((ATT_WIDTH, tn), lambda i, j: (0, j))
    body, dep_specs, dep_args = _with_deps(body, 6, deps)
    return pl.pallas_call(
        body, name=name, grid=(T // tm, D_MODEL // tn),
        in_specs=[act, act, wgt, wgt,
                  pl.BlockSpec((tm, tn), lambda i, j: (i, j + COL_GA // tn)),
                  pl.BlockSpec((tm, tn), lambda i, j: (i, j + COL_GB // tn))] + dep_specs,
        out_specs=pl.BlockSpec((tm, tn), lambda i, j: (i, j)),
        out_shape=jax.ShapeDtypeStruct((T, D_MODEL), BF16),
        compiler_params=_params(("parallel", "parallel")),
    )(y_att, y_sgu, w_oa, w_ob, proj, proj, *dep_args)


def _merge_bwd(dx1_bf, w_out, y_att, y_sgu, w_oa, w_ob, proj, *, name, tm=1024, tn=512):
    T = y_att.shape[0]

    def body(dx_ref, wo_ref, ya_ref, ys_ref, wa_ref, wb_ref, ga_ref, gb_ref, dpa_ref, dpb_ref, dga_ref, dgb_ref):
        dm = _dot_nt(dx_ref[...], wo_ref[...])
        pa = _dot_nn(ya_ref[...], wa_ref[...])
        pb = _dot_nn(ys_ref[...], wb_ref[...])
        sa = _sigmoid(ga_ref[...].astype(F32))
        sb = _sigmoid(gb_ref[...].astype(F32))
        dpa_ref[...] = (dm * sa).astype(BF16)
        dpb_ref[...] = (dm * sb).astype(BF16)
        dga_ref[...] = (dm * pa * sa * (1.0 - sa)).astype(BF16)
        dgb_ref[...] = (dm * pb * sb * (1.0 - sb)).astype(BF16)

    act = pl.BlockSpec((tm, ATT_WIDTH), lambda i, j: (i, 0))
    wgt = pl.BlockSpec((ATT_WIDTH, tn), lambda i, j: (0, j))
    out = pl.BlockSpec((tm, tn), lambda i, j: (i, j))
    return pl.pallas_call(
        body, name=name, grid=(T // tm, D_MODEL // tn),
        in_specs=[pl.BlockSpec((tm, D_MODEL), lambda i, j: (i, 0)),
                  pl.BlockSpec((tn, D_MODEL), lambda i, j: (j, 0)),
                  act, act, wgt, wgt,
                  pl.BlockSpec((tm, tn), lambda i, j: (i, j + COL_GA // tn)),
                  pl.BlockSpec((tm, tn), lambda i, j: (i, j + COL_GB // tn))],
        out_specs=[out] * 4,
        out_shape=[jax.ShapeDtypeStruct((T, D_MODEL), BF16)] * 4,
        compiler_params=_params(("parallel", "parallel")),
    )(dx1_bf, w_out, y_att, y_sgu, w_oa, w_ob, proj, proj)


CONV_ROWS = 256
CONV_TN = 256


def _shift_rows(cur, prev8, k):
    rolled = pltpu.roll(cur, k, axis=0)
    head = jnp.where(lax.broadcasted_iota(jnp.int32, prev8.shape, 0) < k, pltpu.roll(prev8, k, axis=0), rolled[:SUBLANES])
    return jnp.concatenate([head, rolled[SUBLANES:]], axis=0)


def _shift_rows_up(cur, next8, k):
    n = cur.shape[0]
    rolled = pltpu.roll(cur, n - k, axis=0)
    tail = jnp.where(lax.broadcasted_iota(jnp.int32, next8.shape, 0) >= SUBLANES - k,
                     pltpu.roll(next8, SUBLANES - k, axis=0), rolled[n - SUBLANES:])
    return jnp.concatenate([rolled[:n - SUBLANES], tail], axis=0)


def _up_conv_fwd(h2, w_up_t, cw_g, cw_v, cb_g, cb_v, *, n_seq, seq, name, deps=()):
    T = n_seq * seq
    tn, rows = CONV_TN, CONV_ROWS

    def body(h_ref, ug_ref, uv_ref, wg_ref, wv_ref, bg_ref, bv_ref, a_ref, zg_ref, zv_ref, cg_ref, cv_ref):
        def conv(cur, prev8, w_ref, b_ref):
            z1 = _shift_rows(cur, prev8, 1)
            z2 = _shift_rows(cur, prev8, 2)
            return b_ref[...] + w_ref[0:1, :] * z2 + w_ref[1:2, :] * z1 + w_ref[2:3, :] * cur

        start = jnp.zeros((SUBLANES, tn), F32)
        prev = (start, start)
        for s in range(seq // rows):
            r = pl.ds(s * rows, rows)
            h = h_ref[r, :]
            zg = _dot_nt(h, ug_ref[...])
            zv = _dot_nt(h, uv_ref[...])
            zg_ref[r, :] = zg.astype(ACT_DTYPE)
            zv_ref[r, :] = zv.astype(ACT_DTYPE)
            g = conv(zg, prev[0], wg_ref, bg_ref)
            v = conv(zv, prev[1], wv_ref, bv_ref)
            a_ref[r, :] = (g * _sigmoid(g) * v).astype(BF16)
            cg_ref[r, :] = g.astype(ACT_DTYPE)
            cv_ref[r, :] = v.astype(ACT_DTYPE)
            prev = (zg[rows - SUBLANES:], zv[rows - SUBLANES:])

    zs = pl.BlockSpec((seq, tn), lambda b, j: (b, j))
    ws = pl.BlockSpec((3, tn), lambda b, j: (0, j))
    bs = pl.BlockSpec((1, tn), lambda b, j: (0, j))
    body, dep_specs, dep_args = _with_deps(body, 7, deps)
    return pl.pallas_call(
        body, name=name, grid=(n_seq, D_FF // tn),
        in_specs=[pl.BlockSpec((seq, D_MODEL), lambda b, j: (b, 0)),
                  pl.BlockSpec((tn, D_MODEL), lambda b, j: (j, 0)),
                  pl.BlockSpec((tn, D_MODEL), lambda b, j: (j + D_FF // tn, 0)), ws, ws, bs, bs] + dep_specs,
        out_specs=[zs] * 5,
        out_shape=[jax.ShapeDtypeStruct((T, D_FF), BF16)] + [jax.ShapeDtypeStruct((T, D_FF), ACT_DTYPE)] * 4,
        compiler_params=_params(("parallel", "parallel")),
    )(h2, w_up_t, w_up_t, cw_g, cw_v, cb_g.reshape(1, D_FF), cb_v.reshape(1, D_FF), *dep_args)


def _conv_bwd(z_g, z_v, c_g, c_v, dx2_bf, w_down, cw_g, cw_v, *, n_seq, seq, name):
    T = n_seq * seq
    tn, rows = CONV_TN, CONV_ROWS
    n_steps = seq // rows

    def body(zg_ref, zv_ref, cg_ref, cv_ref, dx_ref, wd_ref, wg_ref, wv_ref,
             dzg_ref, dzv_ref, dwg_ref, dwv_ref, dbg_ref, dbv_ref, dcg_ref, dcv_ref):
        def colsum(x):
            return jnp.sum(x, axis=0, keepdims=True)

        zero = jnp.zeros((1, tn), F32)
        db = (zero, zero)
        for s in range(n_steps):
            r = pl.ds(s * rows, rows)
            g = cg_ref[r, :].astype(F32)
            v = cv_ref[r, :].astype(F32)
            sg = _sigmoid(g)
            dav = _dot_nt(dx_ref[r, :], wd_ref[...])
            dcg = dav * v * (sg * (1.0 + g * (1.0 - sg)))
            dcv = dav * (g * sg)
            dcg_ref[r, :] = dcg
            dcv_ref[r, :] = dcv
            db = (db[0] + colsum(dcg), db[1] + colsum(dcv))

        def back(s, accs):
            r0 = pl.multiple_of(s * rows, rows)
            last = s == n_steps - 1
            rn = pl.multiple_of(jnp.minimum(r0 + rows, seq - SUBLANES), SUBLANES)
            new = []
            for half, (dc_ref, w_ref, dz_ref, z_ref) in enumerate(((dcg_ref, wg_ref, dzg_ref, zg_ref),
                                                                   (dcv_ref, wv_ref, dzv_ref, zv_ref))):
                cur = dc_ref[pl.ds(r0, rows), :]
                nxt = jnp.where(last, 0.0, dc_ref[pl.ds(rn, SUBLANES), :])
                u1, u2 = _shift_rows_up(cur, nxt, 1), _shift_rows_up(cur, nxt, 2)
                dz_ref[pl.ds(r0, rows), :] = (w_ref[2:3, :] * cur + w_ref[1:2, :] * u1 + w_ref[0:1, :] * u2).astype(BF16)
                z = z_ref[pl.ds(r0, rows), :].astype(F32)
                new += [accs[3 * half] + colsum(u2 * z), accs[3 * half + 1] + colsum(u1 * z),
                        accs[3 * half + 2] + colsum(cur * z)]
            return tuple(new)

        dw = lax.fori_loop(0, n_steps, back, (zero,) * 6)
        first_seq = pl.program_id(1) == 0

        @pl.when(first_seq)
        def _():
            dwg_ref[...] = jnp.concatenate(dw[0:3], axis=0)
            dwv_ref[...] = jnp.concatenate(dw[3:6], axis=0)
            dbg_ref[...], dbv_ref[...] = db

        @pl.when(jnp.logical_not(first_seq))
        def _():
            dwg_ref[...] += jnp.concatenate(dw[0:3], axis=0)
            dwv_ref[...] += jnp.concatenate(dw[3:6], axis=0)
            dbg_ref[...] += db[0]
            dbv_ref[...] += db[1]

    zs = pl.BlockSpec((seq, tn), lambda j, b: (b, j))
    ws = pl.BlockSpec((3, tn), lambda j, b: (0, j))
    bs = pl.BlockSpec((1, tn), lambda j, b: (0, j))
    outs = pl.pallas_call(
        body, name=name, grid=(D_FF // tn, n_seq),
        in_specs=[zs] * 4 + [pl.BlockSpec((seq, D_MODEL), lambda j, b: (b, 0)),
                             pl.BlockSpec((tn, D_MODEL), lambda j, b: (j, 0)), ws, ws],
        out_specs=[zs, zs, ws, ws, bs, bs],
        out_shape=[jax.ShapeDtypeStruct((T, D_FF), BF16)] * 2 + [jax.ShapeDtypeStruct((3, D_FF), F32)] * 2
        + [jax.ShapeDtypeStruct((1, D_FF), F32)] * 2,
        scratch_shapes=[pltpu.VMEM((seq, tn), F32), pltpu.VMEM((seq, tn), F32)],
        compiler_params=_params(("parallel", "arbitrary")),
    )(z_g, z_v, c_g, c_v, dx2_bf, w_down, cw_g, cw_v)
    dz_g, dz_v, dw_g, dw_v, db_g, db_v = outs
    return dz_g, dz_v, dw_g, dw_v, db_g.reshape(D_FF), db_v.reshape(D_FF)


def _layer_fwd(x, h, w, sched, tail, *, n_seq, seq, l):
    tag = f"l{l}"
    deps = sched("fwd_start", l, h)
    proj = _mm(h, w["w_in_t"], mode="nt", out_dtype=ACT_DTYPE, rotate=W_IN_ROTATE, name=f"{tag}_proj", deps=deps)
    y_att = _attention_fwd(proj, w["q_norm"], w["k_norm"], w["sinks"], n_seq=n_seq, seq=seq, name=f"{tag}_att")
    deps = sched("fwd_att", l, y_att)
    y_sgu = _sgu_fwd(proj, w["sgu_norm"], w["w_s"], w["bias_full"], n_seq=n_seq, seq=seq, name=f"{tag}_sgu")
    merged = _merge_fwd(y_att, y_sgu, w["w_oa"], w["w_ob"], proj, name=f"{tag}_merge", deps=deps)
    x1, h2 = _mm_rows(merged, w["w_out"], mode="nn", fn=_residual_then_norm, out_dtypes=(F32, BF16), rows=(x,),
                      vecs=(w["ffn_norm"],), name=f"{tag}_out")
    deps = sched("fwd_mixer_done", l, x1)
    a, z_g, z_v, c_g, c_v = _up_conv_fwd(h2, w["w_up_t"], w["cw_g"], w["cw_v"], w["cb_g"], w["cb_v"], n_seq=n_seq,
                                         seq=seq, name=f"{tag}_up_conv", deps=deps)
    deps = sched("fwd_conv", l, a)
    if tail[0] == "norm":
        out = _mm_rows(a, w["w_down"], mode="nn", fn=_residual_then_norm, out_dtypes=(F32, BF16), rows=(x1,),
                       vecs=(tail[1],), name=f"{tag}_down", deps=deps)
    else:
        out = _mm_rows(a, w["w_down"], mode="nn", fn=_residual_then_loss, out_dtypes=(F32, BF16), rows=(x1, tail[1]),
                       reduce=True, name=f"{tag}_down", deps=deps)
    saved = dict(x=x, h=h, proj=proj, y_att=y_att, y_sgu=y_sgu, merged=merged, x1=x1, h2=h2, z_g=z_g, z_v=z_v,
                 c_g=c_g, c_v=c_v, a=a)
    return out, saved


def _layer_bwd(dx2, dx2_bf, w, s, sched, deps, *, n_seq, seq, l):
    tag = f"l{l}b"
    g = {}
    g["w_down"] = _mm(s["a"], dx2_bf, mode="tn", out_dtype=F32, name=f"{tag}_dw_down", deps=deps)
    dz_g, dz_v, g["cw_g"], g["cw_v"], g["cb_g"], g["cb_v"] = _conv_bwd(
        s["z_g"], s["z_v"], s["c_g"], s["c_v"], dx2_bf, w["w_down"], w["cw_g"], w["cw_v"], n_seq=n_seq, seq=seq,
        name=f"{tag}_conv")
    dw_up_t = _mm(dz_g, s["h2"], mode="tn", out_dtype=F32, out_rows=(0, 2 * D_FF), name=f"{tag}_dw_up_g")
    g["w_up_t"] = _mm(dz_v, s["h2"], mode="tn", out_dtype=F32, out_rows=(D_FF, 2 * D_FF), out_prev=dw_up_t,
                      name=f"{tag}_dw_up_v")
    deps = sched("bwd_ffn_grads", l, dz_v, g)
    dx1, dx1_bf, dgain = _mm_rows((dz_g, dz_v), w["w_up_t"], mode="nn", fn=_rms_bwd_rows, out_dtypes=(F32, BF16),
                                  rows=(s["x1"], dx2), vecs=(w["ffn_norm"],), reduce=True, a_at=(0, D_FF),
                                  name=f"{tag}_dh2", deps=deps)
    g["ffn_norm"] = dgain.reshape(D_MODEL)
    dpa, dpb, dga, dgb = _merge_bwd(dx1_bf, w["w_out"], s["y_att"], s["y_sgu"], w["w_oa"], w["w_ob"], s["proj"],
                                    name=f"{tag}_merge")
    deps = sched("bwd_merge", l, dpa)
    g["w_out"] = _mm(s["merged"], dx1_bf, mode="tn", out_dtype=F32, name=f"{tag}_dw_out",
                     deps=deps)
    dy_att = _mm(dpa, w["w_oa"], mode="nt", out_dtype=BF16, name=f"{tag}_dy_att")
    dy_sgu = _mm(dpb, w["w_ob"], mode="nt", out_dtype=F32, name=f"{tag}_dy_sgu")
    g["w_oa"] = _mm(s["y_att"], dpa, mode="tn", out_dtype=F32, name=f"{tag}_dw_oa")
    g["w_ob"] = _mm(s["y_sgu"], dpb, mode="tn", out_dtype=F32, name=f"{tag}_dw_ob")
    deps = sched("bwd_out_grads", l, dy_att, g)
    dqkv, g["q_norm"], g["k_norm"], g["sinks"] = _attention_bwd(
        s["proj"], dy_att, w["q_norm"], w["k_norm"], w["sinks"], n_seq=n_seq, seq=seq, name=f"{tag}_att", deps=deps)
    deps = sched("bwd_att", l, dqkv)
    dsuv, g["sgu_norm"], g["w_s"], g["b_s"] = _sgu_bwd(
        s["proj"], dy_sgu, w["sgu_norm"], w["w_s"], w["bias_full"], n_seq=n_seq, seq=seq, name=f"{tag}_sgu", deps=deps)
    dproj = (dsuv, dga, dgb, dqkv)
    at = (QKV_WIDTH, QKV_WIDTH + 2 * SGU_WIDTH, QKV_WIDTH + 2 * SGU_WIDTH + D_MODEL, 0)
    g["w_in_t"] = _mm_tn_parts(dproj, at, s["h"], name=f"{tag}_dw_in")
    deps = sched("bwd_w_in_grad", l, dqkv, g)
    dx, dx_bf, dgain = _mm_rows(dproj, w["w_in_t"], mode="nn", fn=_rms_bwd_rows, out_dtypes=(F32, BF16),
                                rows=(s["x"], dx1), vecs=(w["mix_norm"],), reduce=True, a_at=at,
                                name=f"{tag}_dh", deps=deps)
    g["mix_norm"] = dgain.reshape(D_MODEL)
    return dx, dx_bf, g, sched("bwd_dh", l, dx)


def _local_step(x, target, weights, sched, *, n_seq, seq):
    depth = len(weights)
    saved = []
    h = _rms_fwd(x, weights[0]["mix_norm"], name="l0_mix_norm", deps=sched("begin", 0, x))
    for l in range(depth):
        tail = ("norm", weights[l + 1]["mix_norm"]) if l + 1 < depth else ("loss", target)
        out, s = _layer_fwd(x, h, weights[l], sched, tail, n_seq=n_seq, seq=seq, l=l)
        saved.append(s)
        if l + 1 < depth:
            x, h = out
    dy, dy_bf, loss_cols = out
    grads = [None] * depth
    deps = ()
    for l in reversed(range(depth)):
        dy, dy_bf, grads[l], deps = _layer_bwd(dy, dy_bf, weights[l], saved[l], sched, deps, n_seq=n_seq, seq=seq, l=l)
    return jnp.sum(loss_cols), dy, grads, deps


W_IN_SHARD = IN_WIDTH // N_DEV
W_UP_SHARD = 2 * D_FF // N_DEV
COL_MOVE_ROWS = 256


def _w_o_moves():
    return tuple((j, 0, LANES, 0, j * LANES) for j in range(N_DEV))


def _disassemble(mats, w, moves, *, name):
    R = mats[0].shape[0]
    tr = min(R, COL_MOVE_ROWS)
    n = len(mats)

    def body(*refs):
        m_refs, o_ref = refs[:n], refs[n]
        for j, lo, hi, which, at in moves:
            o_ref[j, :, lo:hi] = m_refs[which][:, at:at + hi - lo]

    return pl.pallas_call(
        body, name=name, grid=(R // tr,),
        in_specs=[pl.BlockSpec((tr, m.shape[1]), lambda i: (i, 0)) for m in mats],
        out_specs=pl.BlockSpec((N_DEV, tr, w), lambda i: (0, i, 0)),
        out_shape=jax.ShapeDtypeStruct((N_DEV, R, w), mats[0].dtype),
        compiler_params=_params(("parallel",)),
    )(*mats)


def _my_place():
    return lax.axis_index("x"), lax.axis_index("y"), lax.axis_index("c")


def _gathered_shape(shape, kind):
    r, c = shape
    return {"blocks": (N_DEV, r, c), "rows": (N_DEV * r, c), "cols": (r, N_DEV * c)}[kind]


def _gather_window(ref, kind, shape, j):
    r, c = shape
    if kind == "blocks":
        return ref.at[j]
    if kind == "rows":
        return ref.at[pl.ds(pl.multiple_of(j * r, r), r), :]
    return ref.at[:, pl.ds(pl.multiple_of(j * c, c), c)]


def _gather(srcs, kinds, *, name):
    n = len(srcs)
    shapes = [s.shape for s in srcs]
    per = 7

    def body(*refs):
        src_refs, dst_refs = refs[:n], refs[n:2 * n]
        send_sems, recv_sems, local_sems = refs[2 * n:]
        x, y, c = _my_place()
        me, sibling = (x, y, c), (x, y, 1 - c)
        chips = [(1 - x, y), (x, 1 - y), (1 - x, 1 - y)]

        def at(i, px, py, pc):
            return _gather_window(dst_refs[i], kinds[i], shapes[i], 4 * px + 2 * py + pc)

        def copy(i, k, block, to, src=None):
            return pltpu.make_async_remote_copy(
                src_ref=at(i, *block) if src is None else src, dst_ref=at(i, *block),
                send_sem=send_sems.at[per * i + k], recv_sem=recv_sems.at[per * i + k], device_id=to, device_id_type=MESH)

        mine = [pltpu.make_async_copy(src_refs[i], at(i, *me), local_sems.at[i]) for i in range(n)]
        for cp in mine:
            cp.start()
        started = []
        for i in range(n):
            first = [copy(i, 0, me, sibling, src=src_refs[i])]
            first += [copy(i, 1 + j, me, (*chip, c), src=src_refs[i]) for j, chip in enumerate(chips)]
            for cp in first:
                cp.start()
            started += first
        for i in range(n):
            for j, chip in enumerate(chips):
                copy(i, 1 + j, (*chip, c), me).wait_recv()
                fwd = copy(i, 4 + j, (*chip, c), sibling)
                fwd.start()
                started.append(fwd)
        for i in range(n):
            copy(i, 0, sibling, me).wait_recv()
            for j, chip in enumerate(chips):
                copy(i, 4 + j, (*chip, 1 - c), me).wait_recv()
        for cp in started:
            cp.wait_send()
        for cp in mine:
            cp.wait()

    return pl.pallas_call(
        body, name=name,
        out_shape=[jax.ShapeDtypeStruct(_gathered_shape(s.shape, k), s.dtype) for s, k in zip(srcs, kinds)],
        in_specs=[ANY] * n, out_specs=[ANY] * n,
        scratch_shapes=[pltpu.SemaphoreType.DMA((per * n,)), pltpu.SemaphoreType.DMA((per * n,)),
                        pltpu.SemaphoreType.DMA((n,))],
    )(*srcs)


HBM = pl.BlockSpec(memory_space=pltpu.HBM)
SEM = pl.BlockSpec(memory_space=pltpu.SEMAPHORE)
TOKEN = jax.ShapeDtypeStruct((SUBLANES, LANES), F32)
TOKEN_SPEC = pl.BlockSpec(memory_space=pltpu.VMEM)
SPLIT_PARAMS = pltpu.CompilerParams(has_side_effects=pltpu.SideEffectType.DATAFLOW_SIDE_EFFECTING)


def _in_hbm(x):
    return pltpu.with_memory_space_constraint(x, pltpu.HBM)


def _hbm_like(shape, dtype):
    return pltpu.HBM(shape, dtype)


def _place_own(stacks, layers, kinds, dtypes, *, name, deps=()):
    n = len(stacks)
    shapes = [s.shape[1:] for s in stacks]

    def body(*refs):
        s_refs, land_refs, bufs, sems = refs[:n], refs[n:2 * n], refs[2 * n:3 * n], refs[3 * n]
        x, y, c = _my_place()
        copies = []
        for i in range(n):
            bufs[i][...] = s_refs[i][...].astype(dtypes[i])
            copies.append(pltpu.make_async_copy(
                bufs[i], _gather_window(land_refs[i], kinds[i], shapes[i], 4 * x + 2 * y + c), sems.at[i]))
        for cp in copies:
            cp.start()
        for cp in copies:
            cp.wait()

    def layer_of(shape, l):
        return pl.BlockSpec((None,) + shape, lambda i: (l,) + (0,) * len(shape))

    body, dep_specs, dep_args = _with_deps(body, n, deps)
    return pl.pallas_call(
        body, name=name, grid=(1,),
        out_shape=[jax.ShapeDtypeStruct(_gathered_shape(s, k), d) for s, k, d in zip(shapes, kinds, dtypes)],
        in_specs=[layer_of(s, l) for s, l in zip(shapes, layers)] + dep_specs, out_specs=[ANY] * n,
        scratch_shapes=[pltpu.VMEM(s, d) for s, d in zip(shapes, dtypes)] + [pltpu.SemaphoreType.DMA((n,))],
        compiler_params=_params(("arbitrary",)),
    )(*stacks, *dep_args)


def _gather_start(lands, kinds, shapes, after=(), *, name):
    n = len(lands)
    n_after = len(after)

    def body(*refs):
        land_refs = refs[:n]
        send_sems, recv_sems = refs[n + n_after], refs[n + n_after + 1]
        x, y, c = _my_place()
        targets = [(x, y, 1 - c), (1 - x, y, c), (x, 1 - y, c), (1 - x, 1 - y, c)]
        for i in range(n):
            own = _gather_window(land_refs[i], kinds[i], shapes[i], 4 * x + 2 * y + c)
            for k, to in enumerate(targets):
                pltpu.make_async_remote_copy(
                    src_ref=own, dst_ref=own, send_sem=send_sems.at[4 * i + k], recv_sem=recv_sems.at[4 * i + k],
                    device_id=to, device_id_type=MESH).start()
        refs[-1][...] = jnp.zeros_like(refs[-1])

    outs = pl.pallas_call(
        body, name=name,
        out_shape=[pltpu.SemaphoreType.DMA((4 * n,)), pltpu.SemaphoreType.DMA((4 * n,))]
        + [_hbm_like(a.shape, a.dtype) for a in lands] + [TOKEN],
        in_specs=[HBM] * n + [ANY] * n_after, out_specs=[SEM, SEM] + [HBM] * n + [TOKEN_SPEC],
        input_output_aliases={i: 2 + i for i in range(n)},
        compiler_params=SPLIT_PARAMS,
    )(*[_in_hbm(a) for a in lands], *after)
    return outs[0], outs[1], outs[2:2 + n], outs[-1]


def _gather_forward(recv_sems, lands, kinds, shapes, after, *, name):
    n = len(lands)

    def body(*refs):
        recv_ref, land_refs = refs[0], refs[1:1 + n]
        fwd_send, fwd_recv = refs[2 + n], refs[3 + n]
        token = refs[-1]
        x, y, c = _my_place()
        chips = [(1 - x, y), (x, 1 - y), (1 - x, 1 - y)]
        for i in range(n):
            for j, (px, py) in enumerate(chips):
                block = _gather_window(land_refs[i], kinds[i], shapes[i], 4 * px + 2 * py + c)
                pltpu.make_async_remote_copy(
                    src_ref=block, dst_ref=block, send_sem=fwd_send.at[3 * i + j], recv_sem=recv_ref.at[4 * i + 1 + j],
                    device_id=(px, py, c), device_id_type=MESH).wait_recv()
                pltpu.make_async_remote_copy(
                    src_ref=block, dst_ref=block, send_sem=fwd_send.at[3 * i + j], recv_sem=fwd_recv.at[3 * i + j],
                    device_id=(x, y, 1 - c), device_id_type=MESH).start()
        token[...] = jnp.zeros_like(token)

    outs = pl.pallas_call(
        body, name=name,
        out_shape=[pltpu.SemaphoreType.DMA((3 * n,)), pltpu.SemaphoreType.DMA((3 * n,))]
        + [_hbm_like(a.shape, a.dtype) for a in lands] + [TOKEN],
        in_specs=[SEM] + [HBM] * n + [ANY], out_specs=[SEM, SEM] + [HBM] * n + [TOKEN_SPEC],
        input_output_aliases={1 + i: 2 + i for i in range(n)},
        compiler_params=SPLIT_PARAMS,
    )(recv_sems, *lands, after)
    return outs[0], outs[1], outs[2:2 + n], outs[-1]


def _gather_finish(send_sems, recv_sems, fwd_send, fwd_recv, lands, kinds, shapes, after, *, name):
    n = len(lands)

    def body(*refs):
        send_ref, recv_ref, fsend_ref, frecv_ref = refs[:4]
        land_refs = refs[4:4 + n]
        x, y, c = _my_place()
        chips = [(1 - x, y), (x, 1 - y), (1 - x, 1 - y)]
        sibling = (x, y, 1 - c)
        for i in range(n):
            def window(j):
                return _gather_window(land_refs[i], kinds[i], shapes[i], j)

            mine, theirs = window(4 * x + 2 * y + c), window(4 * x + 2 * y + (1 - c))
            pltpu.make_async_remote_copy(src_ref=mine, dst_ref=theirs, send_sem=send_ref.at[4 * i],
                                         recv_sem=recv_ref.at[4 * i], device_id=sibling, device_id_type=MESH).wait_recv()
            for j, (px, py) in enumerate(chips):
                block = window(4 * px + 2 * py + (1 - c))
                pltpu.make_async_remote_copy(src_ref=block, dst_ref=block, send_sem=fsend_ref.at[3 * i + j],
                                             recv_sem=frecv_ref.at[3 * i + j], device_id=sibling,
                                             device_id_type=MESH).wait_recv()
            for k in range(4):
                pltpu.make_async_remote_copy(src_ref=mine, dst_ref=mine, send_sem=send_ref.at[4 * i + k],
                                             recv_sem=recv_ref.at[4 * i + k], device_id=sibling,
                                             device_id_type=MESH).wait_send()
            for j, (px, py) in enumerate(chips):
                block = window(4 * px + 2 * py + c)
                pltpu.make_async_remote_copy(src_ref=block, dst_ref=block, send_sem=fsend_ref.at[3 * i + j],
                                             recv_sem=frecv_ref.at[3 * i + j], device_id=sibling,
                                             device_id_type=MESH).wait_send()

    return pl.pallas_call(
        body, name=name,
        out_shape=[_hbm_like(a.shape, a.dtype) for a in lands],
        in_specs=[SEM] * 4 + [HBM] * n + [ANY], out_specs=[HBM] * n,
        input_output_aliases={4 + i: i for i in range(n)},
        compiler_params=SPLIT_PARAMS,
    )(send_sems, recv_sems, fwd_send, fwd_recv, *lands, after)


def _pair_plan(src_ref, land_ref, x, y, c):
    return [(src_ref.at[2 * k + (1 - c)], land_ref.at[k], (x, y, 1 - c)) for k in range(N_CHIPS)]


def _chip_plan(src_ref, land_ref, x, y, c):
    chips = [(1 - x, y), (x, 1 - y), (1 - x, 1 - y)]
    return [(src_ref.at[2 * px + py], land_ref.at[k], (px, py, c)) for k, (px, py) in enumerate(chips)]


def _exchange_copies(plan, per, src_refs, land_refs, send_sems, recv_sems):
    x, y, c = _my_place()
    copies = []
    for i, (s_ref, l_ref) in enumerate(zip(src_refs, land_refs)):
        for q, (src, dst, to) in enumerate(plan(s_ref, l_ref, x, y, c)):
            copies.append(pltpu.make_async_remote_copy(
                src_ref=src, dst_ref=dst, send_sem=send_sems.at[per * i + q], recv_sem=recv_sems.at[per * i + q],
                device_id=to, device_id_type=MESH))
    return copies


def _exchange_start(srcs, plan, per, *, name):
    n = len(srcs)

    def body(*refs):
        src_refs, land_refs = refs[:n], refs[n:2 * n]
        send_sems, recv_sems = refs[2 * n], refs[2 * n + 1]
        for cp in _exchange_copies(plan, per, src_refs, land_refs, send_sems, recv_sems):
            cp.start()
        refs[-1][...] = jnp.zeros_like(refs[-1])

    lands = [lax.empty((per,) + s.shape[1:], s.dtype) for s in srcs]
    outs = pl.pallas_call(
        body, name=name,
        out_shape=[pltpu.SemaphoreType.DMA((per * n,)), pltpu.SemaphoreType.DMA((per * n,))]
        + [_hbm_like(s.shape, s.dtype) for s in srcs] + [_hbm_like(a.shape, a.dtype) for a in lands] + [TOKEN],
        in_specs=[HBM] * (2 * n), out_specs=[SEM, SEM] + [HBM] * (2 * n) + [TOKEN_SPEC],
        input_output_aliases={i: 2 + i for i in range(2 * n)},
        compiler_params=SPLIT_PARAMS,
    )(*[_in_hbm(s) for s in srcs], *[_in_hbm(a) for a in lands])
    return outs[0], outs[1], outs[2:2 + n], outs[2 + n:2 + 2 * n], outs[-1]


def _exchange_wait(send_sems, recv_sems, srcs, lands, plan, per, after, *, name):
    n = len(srcs)
    after = list(after) if isinstance(after, (list, tuple)) else [after]

    def body(*refs):
        send_ref, recv_ref = refs[0], refs[1]
        src_refs, land_refs = refs[2:2 + n], refs[2 + n:2 + 2 * n]
        copies = _exchange_copies(plan, per, src_refs, land_refs, send_ref, recv_ref)
        for cp in copies:
            cp.wait_recv()
        for cp in copies:
            cp.wait_send()

    outs = pl.pallas_call(
        body, name=name,
        out_shape=[_hbm_like(s.shape, s.dtype) for s in srcs] + [_hbm_like(a.shape, a.dtype) for a in lands],
        in_specs=[SEM, SEM] + [HBM] * (2 * n) + [ANY] * len(after), out_specs=[HBM] * (2 * n),
        input_output_aliases={2 + i: i for i in range(2 * n)},
        compiler_params=SPLIT_PARAMS,
    )(send_sems, recv_sems, *srcs, *lands, *after)
    return outs[:n], outs[n:]


REDUCE_BLOCK_BYTES = 2 << 20


def _row_tile(r, c):
    row_bytes = 4 * (-(-c // LANES) * LANES)
    best = r
    for d in range(SUBLANES, r, SUBLANES):
        if r % d == 0 and d * row_bytes <= REDUCE_BLOCK_BYTES:
            best = d
    return best if r * row_bytes > REDUCE_BLOCK_BYTES else r


def _reduce_pair_sum(blocked, recv, place, wire_dtype, *, name):
    _, r, c = blocked.shape
    tr = _row_tile(r, c)

    def body(place_ref, g_ref, r_ref, own_ref, send_ref):
        s = g_ref[...] + r_ref[...]
        send_ref[...] = s.astype(wire_dtype)

        @pl.when(pl.program_id(1) == place_ref[1])
        def _():
            own_ref[...] = s

    return pl.pallas_call(
        body, name=name,
        grid_spec=pltpu.PrefetchScalarGridSpec(
            num_scalar_prefetch=1, grid=(r // tr, N_CHIPS),
            in_specs=[pl.BlockSpec((None, None, tr, c), lambda i, k, place_ref: (k, place_ref[0], i, 0)),
                      pl.BlockSpec((None, tr, c), lambda i, k, place_ref: (k, i, 0))],
            out_specs=[pl.BlockSpec((tr, c), lambda i, k, place_ref: (i, 0)),
                       pl.BlockSpec((None, tr, c), lambda i, k, place_ref: (k, i, 0))]),
        out_shape=[jax.ShapeDtypeStruct((r, c), F32), jax.ShapeDtypeStruct((N_CHIPS, r, c), wire_dtype)],
        compiler_params=_params(("parallel", "arbitrary")),
    )(place, blocked.reshape(N_CHIPS, 2, r, c), recv)


def _chip_sum(own_ref, r_ref):
    return ((own_ref[...] + r_ref[0].astype(F32)) + r_ref[1].astype(F32)) + r_ref[2].astype(F32)


def _reduce_chip_sum(own, recv, *, name):
    r, c = own.shape
    tr = _row_tile(r, c)

    def body(own_ref, r_ref, o_ref):
        o_ref[...] = _chip_sum(own_ref, r_ref)

    return pl.pallas_call(
        body, name=name, grid=(r // tr,),
        in_specs=[pl.BlockSpec((tr, c), lambda i: (i, 0)), pl.BlockSpec((N_CHIPS - 1, tr, c), lambda i: (0, i, 0))],
        out_specs=pl.BlockSpec((tr, c), lambda i: (i, 0)),
        out_shape=jax.ShapeDtypeStruct((r, c), F32),
        compiler_params=_params(("parallel",)),
    )(own, recv)


def _adamw_math(w, g, m, v):
    nm = ADAM_B1 * m + (1.0 - ADAM_B1) * g
    nv = ADAM_B2 * v + (1.0 - ADAM_B2) * (g * g)
    m_hat = nm / (1.0 - ADAM_B1 ** ADAM_STEP)
    v_hat = nv / (1.0 - ADAM_B2 ** ADAM_STEP)
    return -ADAM_LR * (m_hat / (jnp.sqrt(v_hat) + ADAM_EPS) + ADAM_WD * w), nm, nv


ADAMW_ROWS = 256


def _adamw_small(ws, gs, ms, vs, *, name):
    n = len(ws)

    def rows_of(a):
        return a.reshape(-1, a.shape[-1])

    def body(*refs):
        for i in range(n):
            w_ref, g_ref, m_ref, v_ref = refs[4 * i:4 * i + 4]
            outs = refs[4 * n + 3 * i:4 * n + 3 * i + 3]
            rows = w_ref.shape[0]
            if rows % ADAMW_ROWS:
                outs[0][...], outs[1][...], outs[2][...] = _adamw_math(w_ref[...], g_ref[...], m_ref[...], v_ref[...])
                continue

            def chunk(s, carry, w_ref=w_ref, g_ref=g_ref, m_ref=m_ref, v_ref=v_ref, outs=outs):
                r = pl.ds(pl.multiple_of(s * ADAMW_ROWS, ADAMW_ROWS), ADAMW_ROWS)
                outs[0][r, :], outs[1][r, :], outs[2][r, :] = _adamw_math(w_ref[r, :], g_ref[r, :], m_ref[r, :], v_ref[r, :])
                return carry

            lax.fori_loop(0, rows // ADAMW_ROWS, chunk, 0)

    vmem = pl.BlockSpec(memory_space=pltpu.VMEM)
    outs = pl.pallas_call(
        body, name=name, in_specs=[vmem] * (4 * n), out_specs=[vmem] * (3 * n),
        out_shape=[jax.ShapeDtypeStruct(rows_of(w).shape, F32) for w in ws for _ in range(3)],
        compiler_params=_params(),
    )(*[rows_of(a) for quad in zip(ws, gs, ms, vs) for a in quad])
    return [tuple(o.reshape(w.shape) for o in outs[3 * i:3 * i + 3]) for i, w in enumerate(ws)]


def _reduce_adamw(own, recv, w, m, v, layer, prev, *, name):
    r, c = own.shape
    tr = _row_tile(r, c)
    n_prev = 0 if prev is None else len(prev)

    def body(own_ref, r_ref, w_ref, m_ref, v_ref, *rest):
        g_ref, d_ref, nm_ref, nv_ref = rest[n_prev:]
        g = _chip_sum(own_ref, r_ref)
        g_ref[...] = g
        d_ref[...], nm_ref[...], nv_ref[...] = _adamw_math(w_ref[...], g, m_ref[...], v_ref[...])

    slot = pl.BlockSpec((None, tr, c), lambda i: (layer, i, 0))
    return pl.pallas_call(
        body, name=name, grid=(r // tr,),
        in_specs=[pl.BlockSpec((tr, c), lambda i: (i, 0)), pl.BlockSpec((N_CHIPS - 1, tr, c), lambda i: (0, i, 0)),
                  slot, slot, slot] + [ANY] * n_prev,
        out_specs=[slot] * 4,
        out_shape=[jax.ShapeDtypeStruct((DEPTH, r, c), F32)] * 4,
        input_output_aliases={5 + k: k for k in range(n_prev)},
        compiler_params=_params(("parallel",)),
    )(own, recv, w, m, v, *(prev or ()))


REPLICATED = (("mix_norm", (D_MODEL,)), ("q_norm", (HEAD_DIM,)), ("k_norm", (HEAD_DIM,)), ("sinks", (N_Q_HEADS,)),
              ("sgu_norm", (SGU_WIDTH,)), ("w_s", (SGU_GROUPS, BLOCK, BLOCK)), ("b_s", (SGU_GROUPS, BLOCK)),
              ("ffn_norm", (D_MODEL,)), ("conv_b", (2 * D_FF,)))
TRANSPOSED = ("w_in", "w_up")
SHARDED = (("w_in", "rows"), ("w_oa", "cols"), ("w_ob", "cols"), ("w_out", "rows"), ("w_up", "rows"),
           ("conv_w", "blocks"), ("w_down", "rows"))
WEIGHT_ORDER = ("mix_norm", "w_in", "q_norm", "k_norm", "sinks", "sgu_norm", "w_s", "b_s", "w_oa", "w_ob", "w_out",
                "ffn_norm", "w_up", "conv_w", "conv_b", "w_down")
MIXER_WEIGHTS = ["w_in", "w_oa", "w_ob", "w_out"]
FFN_WEIGHTS = ["w_up", "conv_w", "w_down"]


def _small_layout():
    segs, off = {}, 0
    for l in range(DEPTH):
        for name, shape in REPLICATED:
            n = math.prod(shape)
            segs[(l, name)] = (off, n)
            off += n
    per_dev = -(-off // (N_DEV * SUBLANES * LANES)) * SUBLANES * LANES
    return segs, off, per_dev


def _pack_small(grads, loss_part):
    ssegs, total, per_dev = _small_layout()
    flat = jnp.concatenate([grads[l][name].reshape(-1) for (l, name) in ssegs] + [loss_part.reshape(1)])
    return jnp.pad(flat, (0, N_DEV * per_dev - total - 1)).reshape(N_DEV, per_dev // LANES, LANES)


def _unpack_small(gathered):
    ssegs, total, _ = _small_layout()
    flat = gathered.reshape(-1)
    shapes = dict(REPLICATED)
    small = {name: jnp.stack([flat[ssegs[(l, name)][0]:ssegs[(l, name)][0] + ssegs[(l, name)][1]].reshape(shapes[name])
                              for l in range(DEPTH)]) for name, _ in REPLICATED}
    return small, flat[total]


def kernel(x, mix_norm, w_in, q_norm, k_norm, sinks, sgu_norm, w_s, b_s, w_oa, w_ob, w_out, ffn_norm, w_up, conv_w, conv_b, w_down, loss_target, m_mix_norm, m_w_in, m_q_norm, m_k_norm, m_sinks, m_sgu_norm, m_w_s, m_b_s, m_w_oa, m_w_ob, m_w_out, m_ffn_norm, m_w_up, m_conv_w, m_conv_b, m_w_down, v_mix_norm, v_w_in, v_q_norm, v_k_norm, v_sinks, v_sgu_norm, v_w_s, v_b_s, v_w_oa, v_w_ob, v_w_out, v_ffn_norm, v_w_up, v_conv_w, v_conv_b, v_w_down):
    W = dict(mix_norm=mix_norm, w_in=w_in, q_norm=q_norm, k_norm=k_norm, sinks=sinks, sgu_norm=sgu_norm, w_s=w_s, b_s=b_s,
             w_oa=w_oa, w_ob=w_ob, w_out=w_out, ffn_norm=ffn_norm, w_up=w_up, conv_w=conv_w, conv_b=conv_b, w_down=w_down)
    M = dict(mix_norm=m_mix_norm, w_in=m_w_in, q_norm=m_q_norm, k_norm=m_k_norm, sinks=m_sinks, sgu_norm=m_sgu_norm,
             w_s=m_w_s, b_s=m_b_s, w_oa=m_w_oa, w_ob=m_w_ob, w_out=m_w_out, ffn_norm=m_ffn_norm, w_up=m_w_up,
             conv_w=m_conv_w, conv_b=m_conv_b, w_down=m_w_down)
    V = dict(mix_norm=v_mix_norm, w_in=v_w_in, q_norm=v_q_norm, k_norm=v_k_norm, sinks=v_sinks, sgu_norm=v_sgu_norm,
             w_s=v_w_s, b_s=v_b_s, w_oa=v_w_oa, w_ob=v_w_ob, w_out=v_w_out, ffn_norm=v_ffn_norm, w_up=v_w_up,
             conv_w=v_conv_w, conv_b=v_conv_b, w_down=v_w_down)
    n_seq, seq, d_model = x.shape
    tokens = n_seq * seq
    mx, my, mc = _my_place()
    place = jnp.stack([mc, 2 * mx + my]).astype(jnp.int32)
    half = N_DEV // 2
    kind_of = dict(SHARDED)
    for name in TRANSPOSED:
        W[name], M[name], V[name] = (jnp.swapaxes(t[name], 1, 2) for t in (W, M, V))

    gather_groups = [[(0, MIXER_WEIGHTS[0])], [(0, n) for n in MIXER_WEIGHTS[1:]], [(0, n) for n in FFN_WEIGHTS],
                     [(1, n) for n in MIXER_WEIGHTS], [(1, n) for n in FFN_WEIGHTS]]
    started, in_flight = {}, {}
    weights = []
    for l in range(DEPTH):
        w = {name: W[name][l] for name, _ in REPLICATED}
        w["cb_g"], w["cb_v"] = W["conv_b"][l][:D_FF], W["conv_b"][l][D_FF:]
        w["bias_full"] = jnp.repeat(W["b_s"][l].T, SGU_WIDTH // SGU_GROUPS, axis=1)
        weights.append(w)

    def gather_start(gi, after=()):
        stacks = [W[name] for _, name in gather_groups[gi]]
        kinds = [kind_of[name] for _, name in gather_groups[gi]]
        shapes = [s.shape[1:] for s in stacks]
        lands = _place_own(stacks, [l for l, _ in gather_groups[gi]], kinds,
                           [F32 if name == "conv_w" else BF16 for _, name in gather_groups[gi]],
                           name=f"gather_weights_own_{gi}", deps=after)
        send, recv, lands, token = _gather_start(lands, kinds, shapes, after, name=f"gather_weights_start_{gi}")
        started[gi] = dict(sems=(send, recv), lands=lands, kinds=kinds, shapes=shapes)
        return token

    def gather_forward(gi, after):
        st = started[gi]
        in_flight[gi] = _gather_forward(st["sems"][1], st["lands"], st["kinds"], st["shapes"], after,
                                        name=f"gather_weights_forward_{gi}")
        return in_flight[gi][3]

    def gather_finish(gi, after):
        st = started.pop(gi)
        fwd_send, fwd_recv, lands_g, _ = in_flight.pop(gi)
        whole = _gather_finish(st["sems"][0], st["sems"][1], fwd_send, fwd_recv, lands_g, st["kinds"], st["shapes"], after,
                               name=f"gather_weights_finish_{gi}")
        for (l, name), arr in zip(gather_groups[gi], whole):
            w = weights[l]
            if name in TRANSPOSED:
                w[name + "_t"] = arr
            elif name == "conv_w":
                w["cw_g"] = arr[:half].transpose(1, 0, 2).reshape(3, D_FF)
                w["cw_v"] = arr[half:].transpose(1, 0, 2).reshape(3, D_FF)
            else:
                w[name] = arr

    reduce_state, results = {}, {}
    wire = {"conv_w": F32, "small": F32}

    def reduce_begin(key, names, arrays):
        send, recv, srcs_, lands_, token = _exchange_start(arrays, _pair_plan, N_CHIPS, name=f"reduce_pair_start_{key}")
        reduce_state[key] = dict(names=names, pair=(send, recv, srcs_, lands_))
        return [token]

    def reduce_pair(key, after):
        st = reduce_state[key]
        send, recv, srcs_, lands_ = st.pop("pair")
        blocked_, from_sibling = _exchange_wait(send, recv, srcs_, lands_, _pair_plan, N_CHIPS, after,
                                                name=f"reduce_pair_wait_{key}")
        sums = [_reduce_pair_sum(b, r, place, wire.get(n if isinstance(n, str) else n[1], BF16),
                                 name=f"reduce_pair_sum_{key}_{i}")
                for i, (n, b, r) in enumerate(zip(st["names"], blocked_, from_sibling))]
        st["own"] = [s[0] for s in sums]
        *st["chip"], token = _exchange_start([s[1] for s in sums], _chip_plan, N_CHIPS - 1, name=f"reduce_chip_start_{key}")
        return [token]

    def reduce_end(key, after):
        st = reduce_state.pop(key)
        send, recv, srcs_, lands_ = st["chip"]
        _, from_chips = _exchange_wait(send, recv, srcs_, lands_, _chip_plan, N_CHIPS - 1, after,
                                       name=f"reduce_chip_wait_{key}")
        done = []
        for n, own, got in zip(st["names"], st["own"], from_chips):
            if n == "small":
                results["small"] = _reduce_chip_sum(own, got, name="reduce_chip_sum_small")
            else:
                l, name = n
                results[name] = _reduce_adamw(own, got, W[name], M[name], V[name], l, results.get(name),
                                              name=f"l{l}_reduce_adamw_{name}")
                done.append(results[name][0])
        return done

    def sched(point, l, carry, g=None):
        deps = []
        if point == "begin":
            token = ()
            for gi in range(len(gather_groups)):
                token = [gather_start(gi, token)]
            deps = [gather_forward(0, token[0])]
        elif point == "fwd_start" and l == 0:
            gather_finish(0, carry)
            deps = [gather_forward(1, weights[0]["w_in_t"])]
        elif point == "fwd_att" and l == 0:
            gather_finish(1, carry)
            deps = [gather_forward(2, carry)]
        elif point == "fwd_mixer_done" and l == 0:
            gather_finish(2, carry)
        elif point == "fwd_conv" and l == 0:
            deps = [gather_forward(3, carry)]
        elif point == "fwd_start" and l == 1:
            gather_finish(3, carry)
        elif point == "fwd_att" and l == 1:
            deps = [gather_forward(4, carry)]
        elif point == "fwd_mixer_done" and l == 1:
            gather_finish(4, carry)
        elif point == "bwd_ffn_grads":
            conv_w = jnp.concatenate([g[k].reshape(3, half, W_UP_SHARD).transpose(1, 0, 2) for k in ("cw_g", "cw_v")])
            deps = reduce_begin(
                f"l{l}_ffn", [(l, "w_down"), (l, "w_up"), (l, "conv_w")],
                [g["w_down"].reshape(N_DEV, D_FF // N_DEV, D_MODEL),
                 g["w_up_t"].reshape(N_DEV, W_UP_SHARD, D_MODEL), conv_w])
        elif point == "bwd_merge":
            deps = reduce_pair(f"l{l}_ffn", carry)
        elif point == "bwd_out_grads":
            deps = reduce_begin(
                f"l{l}_out", [(l, "w_out"), (l, "w_oa"), (l, "w_ob")],
                [g["w_out"].reshape(N_DEV, D_MODEL // N_DEV, D_MODEL),
                 _disassemble((g["w_oa"],), LANES, _w_o_moves(), name=f"l{l}_split_dw_oa"),
                 _disassemble((g["w_ob"],), LANES, _w_o_moves(), name=f"l{l}_split_dw_ob")])
        elif point == "bwd_att":
            deps = reduce_pair(f"l{l}_out", carry)
        elif point == "bwd_w_in_grad":
            deps = reduce_begin(f"l{l}_in", [(l, "w_in")], [g["w_in_t"].reshape(N_DEV, W_IN_SHARD, D_MODEL)])
        elif point == "bwd_dh":
            deps = reduce_pair(f"l{l}_in", carry)
        return deps

    loss_part, dx, grads, last_deps = _local_step(x.reshape(tokens, d_model), loss_target.reshape(tokens, d_model),
                                                  weights, sched, n_seq=n_seq, seq=seq)
    for g in grads:
        g["conv_b"] = jnp.concatenate([g["cb_g"], g["cb_v"]])
    after = [dx, *last_deps, *reduce_begin("small", ["small"], [_pack_small(grads, loss_part)])]
    for key in [f"l{l}_{part}" for l in reversed(range(DEPTH)) for part in ("ffn", "out", "in")][:-1]:
        after = reduce_end(key, after)
    after = reduce_end("l0_in", after + reduce_pair("small", after))
    reduce_end("small", after)

    G, delta, new_m, new_v = {}, {}, {}, {}
    for name, _ in SHARDED:
        outs = [jnp.swapaxes(o, 1, 2) for o in results[name]] if name in TRANSPOSED else results[name]
        G[name], delta[name], new_m[name], new_v[name] = outs
    small, loss = _unpack_small(_gather([results["small"]], ["blocks"], name="gather_small_grads")[0])
    G.update(small)
    names = [name for name, _ in REPLICATED]
    stepped = _adamw_small(*[[t[name] for name in names] for t in (W, G, M, V)], name="adamw_replicated")
    for name, stepped_one in zip(names, stepped):
        delta[name], new_m[name], new_v[name] = stepped_one
    return (loss, dx.reshape(n_seq, seq, d_model), *[G[n] for n in WEIGHT_ORDER], *[delta[n] for n in WEIGHT_ORDER],
            *[new_m[n] for n in WEIGHT_ORDER], *[new_v[n] for n in WEIGHT_ORDER])
```

```python
import math

import jax
import jax.numpy as jnp
from jax import lax
from jax.experimental import pallas as pl
from jax.experimental.pallas import tpu as pltpu

F32 = jnp.float32
BF16 = jnp.bfloat16
ACT_DTYPE = BF16
MESH = pl.DeviceIdType.MESH

DEPTH = 2
D_MODEL = 1024
N_Q_HEADS = 8
HEAD_DIM = 64
ATT_WIDTH = 512
KV_WIDTH = 128
BLOCK = 128
SGU_WIDTH = 512
SGU_GROUPS = 8
IN_WIDTH = 3840
D_FF = 2816
NORM_EPS = 1e-6
NEG_INF = -1e30
ATT_SCALE = HEAD_DIM ** -0.5
ALIBI_SLOPES = tuple(2.0 ** (-(h + 1)) for h in range(N_Q_HEADS))
ADAM_LR, ADAM_B1, ADAM_B2, ADAM_EPS, ADAM_WD, ADAM_STEP = 0.001, 0.9, 0.999, 1e-08, 0.01, 10
N_DEV = 8
N_CHIPS = 4

QKV_WIDTH = ATT_WIDTH + 2 * KV_WIDTH
COL_SUV, COL_GA, COL_GB, COL_QKV = 0, 1024, 2048, 3072
W_IN_ROTATE = (1, IN_WIDTH // QKV_WIDTH)

LANES = 128
SUBLANES = 8
VMEM_LIMIT_V7X = 56 * 1024 * 1024
GELU_C = math.sqrt(2.0 / math.pi)
GELU_K = 0.044715
ANY = pl.BlockSpec(memory_space=pl.ANY)


def _params(sem=None):
    return pltpu.CompilerParams(dimension_semantics=sem, vmem_limit_bytes=VMEM_LIMIT_V7X)


def _sigmoid(x):
    return 1.0 / (1.0 + jnp.exp(-x))


def _gelu(x):
    th = jnp.tanh(GELU_C * (x + GELU_K * x * x * x))
    return 0.5 * x * (1.0 + th)


def _gelu_and_grad(x):
    x2 = x * x
    th = jnp.tanh(GELU_C * (x + GELU_K * x2 * x))
    g = 0.5 * x * (1.0 + th)
    dg = 0.5 * (1.0 + th) + 0.5 * x * (1.0 - th * th) * (GELU_C * (1.0 + 3.0 * GELU_K * x2))
    return g, dg


def _dot(a, b, dims):
    return lax.dot_general(a, b, (dims, ((), ())), preferred_element_type=F32)


def _dot_nn(a, b):
    return _dot(a, b, ((1,), (0,)))


def _dot_nt(a, b):
    return _dot(a, b, ((1,), (1,)))


def _dot_tn(a, b):
    return _dot(a, b, ((0,), (0,)))


def _lo_mask(shape):
    return lax.broadcasted_iota(jnp.int32, shape, len(shape) - 1) < (LANES // 2)


def _half_sums(x, lo):
    s_lo = jnp.sum(jnp.where(lo, x, 0.0), axis=-1, keepdims=True)
    s_all = jnp.sum(x, axis=-1, keepdims=True)
    return jnp.where(lo, s_lo, s_all - s_lo)


def _dup_half(x, half, lo):
    r = pltpu.roll(x, LANES // 2, axis=1)
    return jnp.where(lo, x, r) if half == 0 else jnp.where(lo, r, x)


def _with_deps(body, n_in, deps):
    k = len(deps)
    if not k:
        return body, [], ()

    def skipping(*refs):
        return body(*refs[:n_in], *refs[n_in + k:])

    return skipping, [ANY] * k, tuple(deps)


MM_VMEM_BUDGET = 40 * 1024 * 1024
MM_MAX_TILE = 1408
MM_MAX_TK = 4096
MM_STEP_BYTES = 1 << 20


def _divisors(n, step, cap):
    return [d for d in range(step, min(n, cap) + 1, step) if n % d == 0] or [n]


def _mm_tiles(M, N, K, out_bytes, tm_divides, tn_divides):
    best = None
    for tm in _divisors(M, LANES, MM_MAX_TILE):
        for tn in _divisors(N, LANES, MM_MAX_TILE):
            if tm_divides % tm or tn_divides % tn:
                continue
            for tk in _divisors(K, 4 * LANES, MM_MAX_TK):
                vmem = 4 * (tm * tk + tk * tn) + 2 * tm * tn * out_bytes + (0 if tk == K else 4 * tm * tn)
                if vmem > MM_VMEM_BUDGET:
                    continue
                traffic = 2 * M * K * (N // tn) + 2 * K * N * (M // tm) + M * N * out_bytes
                cost = traffic + (K // tk - 1) * 8 * M * N + (M // tm) * (N // tn) * (K // tk) * MM_STEP_BYTES
                if best is None or cost < best[0]:
                    best = (cost, tm, tn, tk)
    assert best is not None, (M, N, K)
    return best[1:]


def _mm(a, b, *, mode, out_dtype, name, deps=(), b_rows=(0, None), rotate=None, out_rows=(0, None), out_prev=None):
    b_first, b_count = b_rows
    if mode == "nn":
        (M, K), N = a.shape, b.shape[1]
    elif mode == "nt":
        (M, K), N = a.shape, (b.shape[0] if b_count is None else b_count)
    else:
        (K, M), N = a.shape, b.shape[1]
    shift, period = rotate or (0, 1)
    assert period == 1 or mode == "nt"
    out_first, out_total = out_rows[0], (M if out_rows[1] is None else out_rows[1])
    tm, tn, tk = _mm_tiles(M, N, K, jnp.dtype(out_dtype).itemsize, math.gcd(M, out_first),
                           math.gcd(N // period, b_first if mode == "nt" else 0))
    gm, gn, gk = M // tm, N // tn, K // tk

    def turned(j):
        per = N // period // tn
        return ((j // per + shift) % period) * per + j % per if period > 1 else j

    if mode == "nn":
        a_spec = pl.BlockSpec((tm, tk), lambda i, j, k: (i, k))
        b_spec = pl.BlockSpec((tk, tn), lambda i, j, k: (k + b_first // tk, j))
        contract = ((1,), (0,))
    elif mode == "nt":
        a_spec = pl.BlockSpec((tm, tk), lambda i, j, k: (i, k))
        b_spec = pl.BlockSpec((tn, tk), lambda i, j, k: (turned(j) + b_first // tn, k))
        contract = ((1,), (1,))
    else:
        a_spec = pl.BlockSpec((tk, tm), lambda i, j, k: (k, i))
        b_spec = pl.BlockSpec((tk, tn), lambda i, j, k: (k, j))
        contract = ((0,), (0,))
    o_spec = pl.BlockSpec((tm, tn), lambda i, j, k: (i + out_first // tm, j))
    assert b_first % (tk if mode == "nn" else tn) == 0 and out_first % tm == 0, (name, tm, tn, tk)
    n_prev = 0 if out_prev is None else 1

    def body(a_ref, b_ref, *rest):
        o_ref = rest[n_prev]
        part = _dot(a_ref[...].astype(BF16), b_ref[...].astype(BF16), contract)
        if gk == 1:
            o_ref[...] = part.astype(out_dtype)
            return
        acc_ref = rest[n_prev + 1]
        k = pl.program_id(2)

        @pl.when(k == 0)
        def _():
            acc_ref[...] = part

        @pl.when(k > 0)
        def _():
            acc_ref[...] += part

        @pl.when(k == gk - 1)
        def _():
            o_ref[...] = acc_ref[...].astype(out_dtype)

    body, dep_specs, dep_args = _with_deps(body, 2 + n_prev, deps)
    return pl.pallas_call(
        body,
        name=name,
        grid=(gm, gn, gk),
        in_specs=[a_spec, b_spec] + [ANY] * n_prev + dep_specs,
        out_specs=o_spec,
        out_shape=jax.ShapeDtypeStruct((out_total, N), out_dtype),
        input_output_aliases={2: 0} if n_prev else {},
        scratch_shapes=[] if gk == 1 else [pltpu.VMEM((tm, tn), F32)],
        compiler_params=_params(("parallel", "parallel", "arbitrary")),
    )(a, b, *([out_prev] if n_prev else []), *dep_args)


def _mm_tn_parts(parts, at, b, *, name):
    K, N = b.shape
    n = len(parts)
    tm = math.gcd(*[p.shape[1] for p in parts], *at)
    tiles = [p.shape[1] // tm for p in parts]
    first = [sum(tiles[:p]) for p in range(n)]

    def mine(i, p):
        return jnp.logical_and(i >= first[p], i < first[p] + tiles[p])

    def out_tile(i):
        t = 0
        for p in range(n):
            t = jnp.where(mine(i, p), at[p] // tm + i - first[p], t)
        return t

    def body(*refs):
        a_refs, b_ref, o_ref = refs[:n], refs[n], refs[n + 1]
        for p in range(n):
            @pl.when(mine(pl.program_id(0), p))
            def _(p=p):
                o_ref[...] = _dot_tn(a_refs[p][...], b_ref[...])

    return pl.pallas_call(
        body, name=name, grid=(sum(tiles),),
        in_specs=[pl.BlockSpec((K, tm), lambda i, p=p: (0, jnp.clip(i - first[p], 0, tiles[p] - 1))) for p in range(n)]
        + [pl.BlockSpec((K, N), lambda i: (0, 0), pipeline_mode=pl.Buffered(1))],
        out_specs=pl.BlockSpec((tm, N), lambda i: (out_tile(i), 0)),
        out_shape=jax.ShapeDtypeStruct((sum(p.shape[1] for p in parts), N), F32),
        compiler_params=_params(("arbitrary",)),
    )(*parts, b)


def _mm_rows(a, b, *, mode, fn, out_dtypes, rows=(), vecs=(), reduce=False, name, deps=(), b_rows=(0, None), a_at=None):
    parts = a if a_at is not None else (a,)
    starts = a_at if a_at is not None else (0,)
    n_parts = len(parts)
    M, K = parts[0].shape[0], sum(p.shape[1] for p in parts)
    b_first, b_count = b_rows[0], (b.shape[0] if b_rows[1] is None else b_rows[1])
    N = b.shape[1] if mode == "nn" else b_count
    contract = ((1,), (0,)) if mode == "nn" else ((1,), (1,))
    n_rows, n_vecs, n_out = len(rows), len(vecs), len(out_dtypes)
    out_bytes = sum(jnp.dtype(d).itemsize for d in out_dtypes)
    tm = max(t for t in _divisors(M, LANES, MM_MAX_TILE)
             if 4 * t * K + 2 * K * N + 2 * t * N * (4 * n_rows + out_bytes) <= MM_VMEM_BUDGET)
    assert b_first % b_count == 0 and (a_at is None or mode == "nn")

    def body(*refs):
        a_refs, b_ref, rest = refs[:n_parts], refs[n_parts], refs[n_parts + 1:]
        row_refs, vec_refs = rest[:n_rows], rest[n_rows:n_rows + n_vecs]
        out_refs = rest[n_rows + n_vecs:]
        if a_at is None:
            acc = _dot(a_refs[0][...], b_ref[...], contract)
        else:
            acc = sum(_dot(r[...], b_ref[at:at + r.shape[1], :], contract) for r, at in zip(a_refs, starts))
        res = fn(acc, *[r[...] for r in row_refs], *[v[...] for v in vec_refs])
        for o_ref, val in zip(out_refs[:n_out], res):
            o_ref[...] = val.astype(o_ref.dtype)
        if reduce:
            @pl.when(pl.program_id(0) == 0)
            def _():
                out_refs[n_out][...] = res[n_out]

            @pl.when(pl.program_id(0) > 0)
            def _():
                out_refs[n_out][...] += res[n_out]

    row = pl.BlockSpec((tm, N), lambda i: (i, 0))
    vec = pl.BlockSpec((1, N), lambda i: (0, 0))
    body, dep_specs, dep_args = _with_deps(body, n_parts + 1 + n_rows + n_vecs, deps)
    return pl.pallas_call(
        body, name=name, grid=(M // tm,),
        in_specs=[pl.BlockSpec((tm, p.shape[1]), lambda i: (i, 0)) for p in parts]
        + [pl.BlockSpec((b_count, b.shape[1]), lambda i: (b_first // b_count, 0), pipeline_mode=pl.Buffered(1))]
        + [row] * n_rows + [vec] * n_vecs + dep_specs,
        out_specs=[row] * n_out + [vec] * reduce,
        out_shape=[jax.ShapeDtypeStruct((M, N), d) for d in out_dtypes] + [jax.ShapeDtypeStruct((1, N), F32)] * reduce,
        compiler_params=_params(("arbitrary",)),
    )(*parts, b, *rows, *[v.reshape(1, N) for v in vecs], *dep_args)


def _rms(x, gain):
    return x * lax.rsqrt(jnp.mean(x * x, axis=-1, keepdims=True) + NORM_EPS) * gain


def _residual_then_norm(acc, x, gain):
    x_out = x + acc
    return x_out, _rms(x_out, gain)


def _residual_then_loss(acc, x, target):
    err = (x + acc) - target
    dy = err * (1.0 / D_MODEL)
    return dy, dy, jnp.sum(err * err, axis=0, keepdims=True) * (0.5 / D_MODEL)


def _rms_bwd_rows(dh, x, dres, gain):
    r = lax.rsqrt(jnp.mean(x * x, axis=-1, keepdims=True) + NORM_EPS)
    xh = x * r
    dxh = dh * gain
    dx = dres + r * (dxh - xh * jnp.mean(dxh * xh, axis=-1, keepdims=True))
    return dx, dx, jnp.sum(dh * xh, axis=0, keepdims=True)


def _rms_fwd(x, gain, *, name, tm=512, deps=()):
    T, D = x.shape

    def body(x_ref, g_ref, h_ref):
        xv = x_ref[...]
        r = lax.rsqrt(jnp.mean(xv * xv, axis=-1, keepdims=True) + NORM_EPS)
        h_ref[...] = (xv * r * g_ref[...]).astype(BF16)

    body, dep_specs, dep_args = _with_deps(body, 2, deps)
    return pl.pallas_call(
        body, name=name, grid=(T // tm,),
        in_specs=[pl.BlockSpec((tm, D), lambda i: (i, 0)), pl.BlockSpec((1, D), lambda i: (0, 0))] + dep_specs,
        out_specs=pl.BlockSpec((tm, D), lambda i: (i, 0)),
        out_shape=jax.ShapeDtypeStruct((T, D), BF16),
        compiler_params=_params(("parallel",)),
    )(x, gain.reshape(1, D), *dep_args)


def _head_norm(x, gain2, lo):
    ms = _half_sums(x * x, lo) * (1.0 / HEAD_DIM)
    r = lax.rsqrt(ms + NORM_EPS)
    xh = x * r
    return xh * gain2, xh, r


def _head_norm_bwd(xh, r, gain2, dy, lo):
    dxh = dy * gain2
    dx = r * (dxh - xh * (_half_sums(dxh * xh, lo) * (1.0 / HEAD_DIM)))
    return dx, dy * xh


Q_GROUP = N_Q_HEADS // 2
GROUP_ROWS = Q_GROUP * BLOCK
ATT_SCRATCH = (pltpu.VMEM((2, 2, GROUP_ROWS, BLOCK), F32), pltpu.VMEM((2, GROUP_ROWS, 1), F32))


def _att_consts(sink_ref, bias_ref, sinkcol_ref):
    row = lax.broadcasted_iota(jnp.int32, (GROUP_ROWS, BLOCK), 0)
    kj = lax.broadcasted_iota(jnp.int32, (GROUP_ROWS, BLOCK), 1)
    head = row // BLOCK
    head_col = lax.broadcasted_iota(jnp.int32, (GROUP_ROWS, 1), 0) // BLOCK
    d_cur = (row % BLOCK) - kj
    d_prev = d_cur + BLOCK
    for kv in range(2):
        slope = jnp.zeros((GROUP_ROWS, BLOCK), F32)
        sink = jnp.zeros((GROUP_ROWS, 1), F32)
        for r in range(Q_GROUP):
            slope = jnp.where(head == r, ALIBI_SLOPES[Q_GROUP * kv + r], slope)
            sink = jnp.where(head_col == r, sink_ref[Q_GROUP * kv + r], sink)
        bias_ref[kv, 0] = jnp.where(d_cur >= 0, -slope * d_cur.astype(F32), NEG_INF)
        bias_ref[kv, 1] = jnp.where(d_prev < BLOCK, -slope * d_prev.astype(F32), NEG_INF)
        sinkcol_ref[kv] = sink


def _stack_heads(t0, t1, lo):
    z = jnp.zeros_like(t0)
    return jnp.concatenate([jnp.where(lo, t0, z), jnp.where(lo, z, t0), jnp.where(lo, t1, z), jnp.where(lo, z, t1)], axis=0)


def _unstack_heads(x4, lo):
    return (jnp.where(lo, x4[0:BLOCK], x4[BLOCK:2 * BLOCK]), jnp.where(lo, x4[2 * BLOCK:3 * BLOCK], x4[3 * BLOCK:]))


def _att_probs(q4, k2c, k2p, bias_c, bias_p, sink, has_prev):
    s_c = _dot_nt(q4, k2c) * ATT_SCALE + bias_c
    s_p = jnp.where(has_prev, _dot_nt(q4, k2p) * ATT_SCALE + bias_p, NEG_INF)
    m = jnp.maximum(jnp.max(jnp.maximum(s_c, s_p), axis=-1, keepdims=True), sink)
    e_c = jnp.exp(s_c - m)
    e_p = jnp.exp(s_p - m)
    e_s = jnp.exp(sink - m)
    inv = 1.0 / (jnp.sum(e_c + e_p, axis=-1, keepdims=True) + e_s)
    return e_c * inv, e_p * inv, e_s * inv


def _attention_fwd(proj, q_gain, k_gain, sinks, *, n_seq, seq, name):
    T = n_seq * seq
    nb = seq // BLOCK
    qcol, kvcol = COL_QKV // ATT_WIDTH, (COL_QKV + ATT_WIDTH) // (2 * KV_WIDTH)

    def body(q_ref, kv_ref, qg_ref, kg_ref, sink_ref, y_ref, bias_ref, sinkcol_ref):
        lo = _lo_mask((BLOCK, LANES))
        qg, kg = qg_ref[...], kg_ref[...]
        _att_consts(sink_ref, bias_ref, sinkcol_ref)

        def block(i, carry):
            r0 = pl.multiple_of(i * BLOCK, BLOCK)
            rp = pl.multiple_of(jnp.maximum(i - 1, 0) * BLOCK, BLOCK)
            has_prev = i > 0
            kn_c = _head_norm(kv_ref[pl.ds(r0, BLOCK), 0:KV_WIDTH].astype(F32), kg, lo)[0].astype(BF16)
            kn_p = _head_norm(kv_ref[pl.ds(rp, BLOCK), 0:KV_WIDTH].astype(F32), kg, lo)[0].astype(BF16)
            v_c = kv_ref[pl.ds(r0, BLOCK), KV_WIDTH:2 * KV_WIDTH].astype(BF16)
            v_p = kv_ref[pl.ds(rp, BLOCK), KV_WIDTH:2 * KV_WIDTH].astype(BF16)
            for kv in range(2):
                k2c, k2p = _dup_half(kn_c, kv, lo), _dup_half(kn_p, kv, lo)
                v2c, v2p = _dup_half(v_c, kv, lo), _dup_half(v_p, kv, lo)
                cols = [slice((2 * kv + t) * LANES, (2 * kv + t + 1) * LANES) for t in range(2)]
                qn = [_head_norm(q_ref[pl.ds(r0, BLOCK), c].astype(F32), qg, lo)[0] for c in cols]
                q4 = _stack_heads(qn[0], qn[1], lo).astype(BF16)
                p_c, p_p, _ = _att_probs(q4, k2c, k2p, bias_ref[kv, 0], bias_ref[kv, 1], sinkcol_ref[kv], has_prev)
                o4 = _dot_nn(p_c.astype(BF16), v2c) + _dot_nn(p_p.astype(BF16), v2p)
                for c, out in zip(cols, _unstack_heads(o4, lo)):
                    y_ref[pl.ds(r0, BLOCK), c] = out.astype(BF16)
            return carry

        lax.fori_loop(0, nb, block, 0)

    vec = pl.BlockSpec((1, LANES), lambda b: (0, 0))
    return pl.pallas_call(
        body, name=name, grid=(n_seq,),
        in_specs=[pl.BlockSpec((seq, ATT_WIDTH), lambda b: (b, qcol)),
                  pl.BlockSpec((seq, 2 * KV_WIDTH), lambda b: (b, kvcol)),
                  vec, vec, pl.BlockSpec(memory_space=pltpu.SMEM)],
        out_specs=pl.BlockSpec((seq, ATT_WIDTH), lambda b: (b, 0)),
        out_shape=jax.ShapeDtypeStruct((T, ATT_WIDTH), BF16),
        scratch_shapes=list(ATT_SCRATCH),
        compiler_params=_params(("parallel",)),
    )(proj, proj, jnp.tile(q_gain, 2).reshape(1, LANES), jnp.tile(k_gain, 2).reshape(1, LANES), sinks)


def _attention_bwd(proj, dy, q_gain, k_gain, sinks, *, n_seq, seq, name, deps=()):
    T = n_seq * seq
    nb = seq // BLOCK
    qcol, kvcol = COL_QKV // ATT_WIDTH, (COL_QKV + ATT_WIDTH) // (2 * KV_WIDTH)

    def body(q_ref, kv_ref, dy_ref, qg_ref, kg_ref, sink_ref, dqkv_ref, dqg_ref, dkg_ref, dsink_ref,
             dkn_acc, dv_acc, qg_acc, kg_acc, sink_acc, bias_ref, sinkcol_ref):
        lo = _lo_mask((BLOCK, LANES))
        qg, kg = qg_ref[...], kg_ref[...]
        _att_consts(sink_ref, bias_ref, sinkcol_ref)
        first = pl.program_id(0) == 0

        @pl.when(first)
        def _():
            qg_acc[...] = jnp.zeros_like(qg_acc)
            kg_acc[...] = jnp.zeros_like(kg_acc)
            sink_acc[...] = jnp.zeros_like(sink_acc)

        dkn_acc[...] = jnp.zeros_like(dkn_acc)
        dv_acc[...] = jnp.zeros_like(dv_acc)

        def block(i, carry):
            r0 = pl.multiple_of(i * BLOCK, BLOCK)
            rp = pl.multiple_of(jnp.maximum(i - 1, 0) * BLOCK, BLOCK)
            has_prev = i > 0
            kn_c = _head_norm(kv_ref[pl.ds(r0, BLOCK), 0:KV_WIDTH].astype(F32), kg, lo)[0].astype(BF16)
            kn_p = _head_norm(kv_ref[pl.ds(rp, BLOCK), 0:KV_WIDTH].astype(F32), kg, lo)[0].astype(BF16)
            v_c = kv_ref[pl.ds(r0, BLOCK), KV_WIDTH:2 * KV_WIDTH].astype(BF16)
            v_p = kv_ref[pl.ds(rp, BLOCK), KV_WIDTH:2 * KV_WIDTH].astype(BF16)
            dk_c, dk_p, dv_c, dv_p = [], [], [], []
            for kv in range(2):
                k2c, k2p = _dup_half(kn_c, kv, lo), _dup_half(kn_p, kv, lo)
                v2c, v2p = _dup_half(v_c, kv, lo), _dup_half(v_p, kv, lo)
                cols = [slice((2 * kv + t) * LANES, (2 * kv + t + 1) * LANES) for t in range(2)]
                normed = [_head_norm(q_ref[pl.ds(r0, BLOCK), c].astype(F32), qg, lo) for c in cols]
                q4 = _stack_heads(normed[0][0], normed[1][0], lo).astype(BF16)
                do4 = _stack_heads(dy_ref[pl.ds(r0, BLOCK), cols[0]], dy_ref[pl.ds(r0, BLOCK), cols[1]], lo)
                p_c, p_p, p_s = _att_probs(q4, k2c, k2p, bias_ref[kv, 0], bias_ref[kv, 1], sinkcol_ref[kv], has_prev)
                dp_c = _dot_nt(do4, v2c)
                dp_p = _dot_nt(do4, v2p)
                delta = jnp.sum(p_c * dp_c + p_p * dp_p, axis=-1, keepdims=True)
                ds_c = (p_c * (dp_c - delta)).astype(BF16)
                ds_p = (p_p * (dp_p - delta)).astype(BF16)
                sink_acc[kv] += -(p_s * delta)
                dq4 = (_dot_nn(ds_c, k2c) + _dot_nn(ds_p, k2p)) * ATT_SCALE
                for c, (_, qh, qr), dqn in zip(cols, normed, _unstack_heads(dq4, lo)):
                    dq, dg = _head_norm_bwd(qh, qr, qg, dqn, lo)
                    dqkv_ref[pl.ds(r0, BLOCK), c] = dq.astype(BF16)
                    qg_acc[...] += dg
                dk_c.append(_dot_tn(ds_c, q4))
                dk_p.append(_dot_tn(ds_p, q4))
                dv_c.append(_dot_tn(p_c.astype(BF16), do4))
                dv_p.append(_dot_tn(p_p.astype(BF16), do4))

            def fold(parts):
                a = parts[0] + pltpu.roll(parts[0], LANES // 2, axis=1)
                b = parts[1] + pltpu.roll(parts[1], LANES // 2, axis=1)
                return jnp.where(lo, a, b)

            dkn_acc[pl.ds(r0, BLOCK), :] += fold(dk_c) * ATT_SCALE
            dkn_acc[pl.ds(rp, BLOCK), :] += fold(dk_p) * ATT_SCALE
            dv_acc[pl.ds(r0, BLOCK), :] += fold(dv_c)
            dv_acc[pl.ds(rp, BLOCK), :] += fold(dv_p)
            return carry

        lax.fori_loop(0, nb, block, 0)

        def finish(i, carry):
            r0 = pl.multiple_of(i * BLOCK, BLOCK)
            _, kh, kr = _head_norm(kv_ref[pl.ds(r0, BLOCK), 0:KV_WIDTH].astype(F32), kg, lo)
            dk, dg = _head_norm_bwd(kh, kr, kg, dkn_acc[pl.ds(r0, BLOCK), :], lo)
            dqkv_ref[pl.ds(r0, BLOCK), ATT_WIDTH:ATT_WIDTH + KV_WIDTH] = dk.astype(BF16)
            dqkv_ref[pl.ds(r0, BLOCK), ATT_WIDTH + KV_WIDTH:QKV_WIDTH] = dv_acc[pl.ds(r0, BLOCK), :].astype(BF16)
            kg_acc[...] += dg
            return carry

        lax.fori_loop(0, nb, finish, 0)

        @pl.when(pl.program_id(0) == n_seq - 1)
        def _():
            dqg_ref[...] = jnp.sum(qg_acc[...], axis=0, keepdims=True)
            dkg_ref[...] = jnp.sum(kg_acc[...], axis=0, keepdims=True)
            lane = lax.broadcasted_iota(jnp.int32, (1, LANES), 1)
            dsink = jnp.zeros((1, LANES), F32)
            for kv in range(2):
                for r in range(Q_GROUP):
                    total = jnp.sum(sink_acc[kv, r * BLOCK:(r + 1) * BLOCK, :], axis=0, keepdims=True)
                    dsink = jnp.where(lane == Q_GROUP * kv + r, total, dsink)
            dsink_ref[...] = dsink

    vec = pl.BlockSpec((1, LANES), lambda b: (0, 0))
    acc = pltpu.VMEM((BLOCK, LANES), F32)
    body, dep_specs, dep_args = _with_deps(body, 6, deps)
    dqkv, dqg, dkg, dsink = pl.pallas_call(
        body, name=name, grid=(n_seq,),
        in_specs=[pl.BlockSpec((seq, ATT_WIDTH), lambda b: (b, qcol)),
                  pl.BlockSpec((seq, 2 * KV_WIDTH), lambda b: (b, kvcol)),
                  pl.BlockSpec((seq, ATT_WIDTH), lambda b: (b, 0)),
                  vec, vec, pl.BlockSpec(memory_space=pltpu.SMEM)] + dep_specs,
        out_specs=[pl.BlockSpec((seq, QKV_WIDTH), lambda b: (b, 0)), vec, vec, vec],
        out_shape=[jax.ShapeDtypeStruct((T, QKV_WIDTH), BF16)] + [jax.ShapeDtypeStruct((1, LANES), F32)] * 3,
        scratch_shapes=[pltpu.VMEM((seq, KV_WIDTH), F32), pltpu.VMEM((seq, KV_WIDTH), F32), acc, acc,
                        pltpu.VMEM((2, GROUP_ROWS, 1), F32), *ATT_SCRATCH],
        compiler_params=_params(("arbitrary",)),
    )(proj, proj, dy, jnp.tile(q_gain, 2).reshape(1, LANES), jnp.tile(k_gain, 2).reshape(1, LANES), sinks, *dep_args)
    half = LANES // 2
    return dqkv, dqg[0, :half] + dqg[0, half:], dkg[0, :half] + dkg[0, half:], dsink[0, :N_Q_HEADS]


def _sgu_weights(w_ref):
    r = lax.broadcasted_iota(jnp.int32, (BLOCK, BLOCK), 0)
    c = lax.broadcasted_iota(jnp.int32, (BLOCK, BLOCK), 1)
    return [jnp.where(r >= c, w_ref[g], 0.0).astype(BF16) for g in range(SGU_GROUPS)]


def _sgu_fwd(proj, gain, w_s, bias_full, *, n_seq, seq, name):
    T = n_seq * seq
    nc = seq // BLOCK

    def body(suv_ref, g_ref, w_ref, b_ref, y_ref):
        lo = _lo_mask((BLOCK, LANES))
        wm = _sgu_weights(w_ref)
        gain_v = g_ref[...]

        def chunk(c, carry):
            r0 = pl.multiple_of(c * BLOCK, BLOCK)
            gv = _gelu(suv_ref[pl.ds(r0, BLOCK), SGU_WIDTH:2 * SGU_WIDTH].astype(F32))
            r = lax.rsqrt(jnp.mean(gv * gv, axis=-1, keepdims=True) + NORM_EPS)
            vn = (gv * r * gain_v).astype(BF16)
            for p in range(SGU_WIDTH // LANES):
                cols = slice(p * LANES, (p + 1) * LANES)
                vp = vn[:, cols]
                mixed = jnp.where(lo, _dot_nn(wm[2 * p], vp), _dot_nn(wm[2 * p + 1], vp)) + b_ref[:, cols]
                u = _gelu(suv_ref[pl.ds(r0, BLOCK), cols].astype(F32))
                y_ref[pl.ds(r0, BLOCK), cols] = (u * mixed).astype(BF16)
            return carry

        lax.fori_loop(0, nc, chunk, 0)

    return pl.pallas_call(
        body, name=name, grid=(n_seq,),
        in_specs=[pl.BlockSpec((seq, 2 * SGU_WIDTH), lambda b: (b, COL_SUV // (2 * SGU_WIDTH))),
                  pl.BlockSpec((1, SGU_WIDTH), lambda b: (0, 0)),
                  pl.BlockSpec((SGU_GROUPS, BLOCK, BLOCK), lambda b: (0, 0, 0)),
                  pl.BlockSpec((BLOCK, SGU_WIDTH), lambda b: (0, 0))],
        out_specs=pl.BlockSpec((seq, SGU_WIDTH), lambda b: (b, 0)),
        out_shape=jax.ShapeDtypeStruct((T, SGU_WIDTH), BF16),
        compiler_params=_params(("parallel",)),
    )(proj, gain.reshape(1, SGU_WIDTH), w_s, bias_full)


def _sgu_bwd(proj, dy, gain, w_s, bias_full, *, n_seq, seq, name, deps=()):
    T = n_seq * seq
    nc = seq // BLOCK
    n_tiles = SGU_WIDTH // LANES

    def body(suv_ref, dy_ref, g_ref, w_ref, b_ref, dsuv_ref, dg_ref, dw_ref, db_ref, dg_acc, dw_acc, db_acc):
        lo = _lo_mask((BLOCK, LANES))
        hi = jnp.logical_not(lo)
        wm = _sgu_weights(w_ref)
        wmt = [jnp.where(lax.broadcasted_iota(jnp.int32, (BLOCK, BLOCK), 1) >= lax.broadcasted_iota(jnp.int32, (BLOCK, BLOCK), 0),
                         w_ref[g].T, 0.0).astype(BF16) for g in range(SGU_GROUPS)]
        gain_v = g_ref[...]

        @pl.when(pl.program_id(0) == 0)
        def _():
            dg_acc[...] = jnp.zeros_like(dg_acc)
            dw_acc[...] = jnp.zeros_like(dw_acc)
            db_acc[...] = jnp.zeros_like(db_acc)

        def chunk(c, carry):
            r0 = pl.multiple_of(c * BLOCK, BLOCK)
            gv, dgelu_v = _gelu_and_grad(suv_ref[pl.ds(r0, BLOCK), SGU_WIDTH:2 * SGU_WIDTH].astype(F32))
            r = lax.rsqrt(jnp.mean(gv * gv, axis=-1, keepdims=True) + NORM_EPS)
            vh = gv * r
            vn = (vh * gain_v).astype(BF16)
            dvn_tiles = []
            for p in range(n_tiles):
                cols = slice(p * LANES, (p + 1) * LANES)
                vp = vn[:, cols]
                mixed = jnp.where(lo, _dot_nn(wm[2 * p], vp), _dot_nn(wm[2 * p + 1], vp)) + b_ref[:, cols]
                u, dgelu_u = _gelu_and_grad(suv_ref[pl.ds(r0, BLOCK), cols].astype(F32))
                dyv = dy_ref[pl.ds(r0, BLOCK), cols]
                dsuv_ref[pl.ds(r0, BLOCK), cols] = (dyv * mixed * dgelu_u).astype(BF16)
                dm = dyv * u
                db_acc[:, cols] += dm
                dm_bf = dm.astype(BF16)
                dvn_tiles.append(jnp.where(lo, _dot_nn(wmt[2 * p], dm_bf), _dot_nn(wmt[2 * p + 1], dm_bf)))
                dw_acc[2 * p] += _dot_nt(jnp.where(lo, dm, 0.0).astype(BF16), vp)
                dw_acc[2 * p + 1] += _dot_nt(jnp.where(hi, dm, 0.0).astype(BF16), vp)
            dvn = jnp.concatenate(dvn_tiles, axis=1)
            dg_acc[...] += dvn * vh
            dvh = dvn * gain_v
            dgv = r * (dvh - vh * jnp.mean(dvh * vh, axis=-1, keepdims=True))
            dsuv_ref[pl.ds(r0, BLOCK), SGU_WIDTH:2 * SGU_WIDTH] = (dgv * dgelu_v).astype(BF16)
            return carry

        lax.fori_loop(0, nc, chunk, 0)

        @pl.when(pl.program_id(0) == n_seq - 1)
        def _():
            dg_ref[...] = jnp.sum(dg_acc[...], axis=0, keepdims=True)
            r = lax.broadcasted_iota(jnp.int32, (BLOCK, BLOCK), 0)
            c = lax.broadcasted_iota(jnp.int32, (BLOCK, BLOCK), 1)
            for g in range(SGU_GROUPS):
                dw_ref[g] = jnp.where(r >= c, dw_acc[g], 0.0)
            lane = lax.broadcasted_iota(jnp.int32, (BLOCK, LANES), 1)
            out = jnp.zeros((BLOCK, LANES), F32)
            for p in range(n_tiles):
                tile = db_acc[:, p * LANES:(p + 1) * LANES]
                s_lo = jnp.sum(jnp.where(lo, tile, 0.0), axis=-1, keepdims=True)
                s_hi = jnp.sum(jnp.where(hi, tile, 0.0), axis=-1, keepdims=True)
                out = jnp.where(lane == 2 * p, s_lo, out)
                out = jnp.where(lane == 2 * p + 1, s_hi, out)
            db_ref[...] = out

    body, dep_specs, dep_args = _with_deps(body, 5, deps)
    dsuv, dg, dw, db = pl.pallas_call(
        body, name=name, grid=(n_seq,),
        in_specs=[pl.BlockSpec((seq, 2 * SGU_WIDTH), lambda b: (b, COL_SUV // (2 * SGU_WIDTH))),
                  pl.BlockSpec((seq, SGU_WIDTH), lambda b: (b, 0)),
                  pl.BlockSpec((1, SGU_WIDTH), lambda b: (0, 0)),
                  pl.BlockSpec((SGU_GROUPS, BLOCK, BLOCK), lambda b: (0, 0, 0)),
                  pl.BlockSpec((BLOCK, SGU_WIDTH), lambda b: (0, 0))] + dep_specs,
        out_specs=[pl.BlockSpec((seq, 2 * SGU_WIDTH), lambda b: (b, 0)),
                   pl.BlockSpec((1, SGU_WIDTH), lambda b: (0, 0)),
                   pl.BlockSpec((SGU_GROUPS, BLOCK, BLOCK), lambda b: (0, 0, 0)),
                   pl.BlockSpec((BLOCK, LANES), lambda b: (0, 0))],
        out_shape=[jax.ShapeDtypeStruct((T, 2 * SGU_WIDTH), BF16), jax.ShapeDtypeStruct((1, SGU_WIDTH), F32),
                   jax.ShapeDtypeStruct((SGU_GROUPS, BLOCK, BLOCK), F32), jax.ShapeDtypeStruct((BLOCK, LANES), F32)],
        scratch_shapes=[pltpu.VMEM((BLOCK, SGU_WIDTH), F32), pltpu.VMEM((SGU_GROUPS, BLOCK, BLOCK), F32),
                        pltpu.VMEM((BLOCK, SGU_WIDTH), F32)],
        compiler_params=_params(("arbitrary",)),
    )(proj, dy, gain.reshape(1, SGU_WIDTH), w_s, bias_full, *dep_args)
    return dsuv, dg.reshape(SGU_WIDTH), dw, db[:, :SGU_GROUPS].T


def _merge_fwd(y_att, y_sgu, w_oa, w_ob, proj, *, name, tm=1024, tn=512, deps=()):
    T = y_att.shape[0]

    def body(ya_ref, ys_ref, wa_ref, wb_ref, ga_ref, gb_ref, o_ref):
        pa = _dot_nn(ya_ref[...], wa_ref[...])
        pb = _dot_nn(ys_ref[...], wb_ref[...])
        o_ref[...] = (_sigmoid(ga_ref[...].astype(F32)) * pa + _sigmoid(gb_ref[...].astype(F32)) * pb).astype(BF16)

    act = pl.BlockSpec((tm, ATT_WIDTH), lambda i, j: (i, 0))
    wgt = pl.BlockSpec((ATT_WIDTH, tn), lambda i, j: (0, j))
    body, dep_specs, dep_args = _with_deps(body, 6, deps)
    return pl.pallas_call(
        body, name=name, grid=(T // tm, D_MODEL // tn),
        in_specs=[act, act, wgt, wgt,
                  pl.BlockSpec((tm, tn), lambda i, j: (i, j + COL_GA // tn)),
                  pl.BlockSpec((tm, tn), lambda i, j: (i, j + COL_GB // tn))] + dep_specs,
        out_specs=pl.BlockSpec((tm, tn), lambda i, j: (i, j)),
        out_shape=jax.ShapeDtypeStruct((T, D_MODEL), BF16),
        compiler_params=_params(("parallel", "parallel")),
    )(y_att, y_sgu, w_oa, w_ob, proj, proj, *dep_args)


def _merge_bwd(dx1_bf, w_out, y_att, y_sgu, w_oa, w_ob, proj, *, name, tm=1024, tn=512):
    T = y_att.shape[0]

    def body(dx_ref, wo_ref, ya_ref, ys_ref, wa_ref, wb_ref, ga_ref, gb_ref, dpa_ref, dpb_ref, dga_ref, dgb_ref):
        dm = _dot_nt(dx_ref[...], wo_ref[...])
        pa = _dot_nn(ya_ref[...], wa_ref[...])
        pb = _dot_nn(ys_ref[...], wb_ref[...])
        sa = _sigmoid(ga_ref[...].astype(F32))
        sb = _sigmoid(gb_ref[...].astype(F32))
        dpa_ref[...] = (dm * sa).astype(BF16)
        dpb_ref[...] = (dm * sb).astype(BF16)
        dga_ref[...] = (dm * pa * sa * (1.0 - sa)).astype(BF16)
        dgb_ref[...] = (dm * pb * sb * (1.0 - sb)).astype(BF16)

    act = pl.BlockSpec((tm, ATT_WIDTH), lambda i, j: (i, 0))
    wgt = pl.BlockSpec((ATT_WIDTH, tn), lambda i, j: (0, j))
    out = pl.BlockSpec((tm, tn), lambda i, j: (i, j))
    return pl.pallas_call(
        body, name=name, grid=(T // tm, D_MODEL // tn),
        in_specs=[pl.BlockSpec((tm, D_MODEL), lambda i, j: (i, 0)),
                  pl.BlockSpec((tn, D_MODEL), lambda i, j: (j, 0)),
                  act, act, wgt, wgt,
                  pl.BlockSpec((tm, tn), lambda i, j: (i, j + COL_GA // tn)),
                  pl.BlockSpec((tm, tn), lambda i, j: (i, j + COL_GB // tn))],
        out_specs=[out] * 4,
        out_shape=[jax.ShapeDtypeStruct((T, D_MODEL), BF16)] * 4,
        compiler_params=_params(("parallel", "parallel")),
    )(dx1_bf, w_out, y_att, y_sgu, w_oa, w_ob, proj, proj)


CONV_ROWS = 256
CONV_TN = 256


def _shift_rows(cur, prev8, k):
    rolled = pltpu.roll(cur, k, axis=0)
    head = jnp.where(lax.broadcasted_iota(jnp.int32, prev8.shape, 0) < k, pltpu.roll(prev8, k, axis=0), rolled[:SUBLANES])
    return jnp.concatenate([head, rolled[SUBLANES:]], axis=0)


def _shift_rows_up(cur, next8, k):
    n = cur.shape[0]
    rolled = pltpu.roll(cur, n - k, axis=0)
    tail = jnp.where(lax.broadcasted_iota(jnp.int32, next8.shape, 0) >= SUBLANES - k,
                     pltpu.roll(next8, SUBLANES - k, axis=0), rolled[n - SUBLANES:])
    return jnp.concatenate([rolled[:n - SUBLANES], tail], axis=0)


def _up_conv_fwd(h2, w_up_t, cw_g, cw_v, cb_g, cb_v, *, n_seq, seq, name, deps=()):
    T = n_seq * seq
    tn, rows = CONV_TN, CONV_ROWS

    def body(h_ref, ug_ref, uv_ref, wg_ref, wv_ref, bg_ref, bv_ref, a_ref, zg_ref, zv_ref, cg_ref, cv_ref):
        def conv(cur, prev8, w_ref, b_ref):
            z1 = _shift_rows(cur, prev8, 1)
            z2 = _shift_rows(cur, prev8, 2)
            return b_ref[...] + w_ref[0:1, :] * z2 + w_ref[1:2, :] * z1 + w_ref[2:3, :] * cur

        start = jnp.zeros((SUBLANES, tn), F32)
        prev = (start, start)
        for s in range(seq // rows):
            r = pl.ds(s * rows, rows)
            h = h_ref[r, :]
            zg = _dot_nt(h, ug_ref[...])
            zv = _dot_nt(h, uv_ref[...])
            zg_ref[r, :] = zg.astype(ACT_DTYPE)
            zv_ref[r, :] = zv.astype(ACT_DTYPE)
            g = conv(zg, prev[0], wg_ref, bg_ref)
            v = conv(zv, prev[1], wv_ref, bv_ref)
            a_ref[r, :] = (g * _sigmoid(g) * v).astype(BF16)
            cg_ref[r, :] = g.astype(ACT_DTYPE)
            cv_ref[r, :] = v.astype(ACT_DTYPE)
            prev = (zg[rows - SUBLANES:], zv[rows - SUBLANES:])

    zs = pl.BlockSpec((seq, tn), lambda b, j: (b, j))
    ws = pl.BlockSpec((3, tn), lambda b, j: (0, j))
    bs = pl.BlockSpec((1, tn), lambda b, j: (0, j))
    body, dep_specs, dep_args = _with_deps(body, 7, deps)
    return pl.pallas_call(
        body, name=name, grid=(n_seq, D_FF // tn),
        in_specs=[pl.BlockSpec((seq, D_MODEL), lambda b, j: (b, 0)),
                  pl.BlockSpec((tn, D_MODEL), lambda b, j: (j, 0)),
                  pl.BlockSpec((tn, D_MODEL), lambda b, j: (j + D_FF // tn, 0)), ws, ws, bs, bs] + dep_specs,
        out_specs=[zs] * 5,
        out_shape=[jax.ShapeDtypeStruct((T, D_FF), BF16)] + [jax.ShapeDtypeStruct((T, D_FF), ACT_DTYPE)] * 4,
        compiler_params=_params(("parallel", "parallel")),
    )(h2, w_up_t, w_up_t, cw_g, cw_v, cb_g.reshape(1, D_FF), cb_v.reshape(1, D_FF), *dep_args)


def _conv_bwd(z_g, z_v, c_g, c_v, dx2_bf, w_down, cw_g, cw_v, *, n_seq, seq, name):
    T = n_seq * seq
    tn, rows = CONV_TN, CONV_ROWS
    n_steps = seq // rows

    def body(zg_ref, zv_ref, cg_ref, cv_ref, dx_ref, wd_ref, wg_ref, wv_ref,
             dzg_ref, dzv_ref, dwg_ref, dwv_ref, dbg_ref, dbv_ref, dcg_ref, dcv_ref):
        def colsum(x):
            return jnp.sum(x, axis=0, keepdims=True)

        zero = jnp.zeros((1, tn), F32)
        db = (zero, zero)
        for s in range(n_steps):
            r = pl.ds(s * rows, rows)
            g = cg_ref[r, :].astype(F32)
            v = cv_ref[r, :].astype(F32)
            sg = _sigmoid(g)
            dav = _dot_nt(dx_ref[r, :], wd_ref[...])
            dcg = dav * v * (sg * (1.0 + g * (1.0 - sg)))
            dcv = dav * (g * sg)
            dcg_ref[r, :] = dcg
            dcv_ref[r, :] = dcv
            db = (db[0] + colsum(dcg), db[1] + colsum(dcv))

        def back(s, accs):
            r0 = pl.multiple_of(s * rows, rows)
            last = s == n_steps - 1
            rn = pl.multiple_of(jnp.minimum(r0 + rows, seq - SUBLANES), SUBLANES)
            new = []
            for half, (dc_ref, w_ref, dz_ref, z_ref) in enumerate(((dcg_ref, wg_ref, dzg_ref, zg_ref),
                                                                   (dcv_ref, wv_ref, dzv_ref, zv_ref))):
                cur = dc_ref[pl.ds(r0, rows), :]
                nxt = jnp.where(last, 0.0, dc_ref[pl.ds(rn, SUBLANES), :])
                u1, u2 = _shift_rows_up(cur, nxt, 1), _shift_rows_up(cur, nxt, 2)
                dz_ref[pl.ds(r0, rows), :] = (w_ref[2:3, :] * cur + w_ref[1:2, :] * u1 + w_ref[0:1, :] * u2).astype(BF16)
                z = z_ref[pl.ds(r0, rows), :].astype(F32)
                new += [accs[3 * half] + colsum(u2 * z), accs[3 * half + 1] + colsum(u1 * z),
                        accs[3 * half + 2] + colsum(cur * z)]
            return tuple(new)

        dw = lax.fori_loop(0, n_steps, back, (zero,) * 6)
        first_seq = pl.program_id(1) == 0

        @pl.when(first_seq)
        def _():
            dwg_ref[...] = jnp.concatenate(dw[0:3], axis=0)
            dwv_ref[...] = jnp.concatenate(dw[3:6], axis=0)
            dbg_ref[...], dbv_ref[...] = db

        @pl.when(jnp.logical_not(first_seq))
        def _():
            dwg_ref[...] += jnp.concatenate(dw[0:3], axis=0)
            dwv_ref[...] += jnp.concatenate(dw[3:6], axis=0)
            dbg_ref[...] += db[0]
            dbv_ref[...] += db[1]

    zs = pl.BlockSpec((seq, tn), lambda j, b: (b, j))
    ws = pl.BlockSpec((3, tn), lambda j, b: (0, j))
    bs = pl.BlockSpec((1, tn), lambda j, b: (0, j))
    outs = pl.pallas_call(
        body, name=name, grid=(D_FF // tn, n_seq),
        in_specs=[zs] * 4 + [pl.BlockSpec((seq, D_MODEL), lambda j, b: (b, 0)),
                             pl.BlockSpec((tn, D_MODEL), lambda j, b: (j, 0)), ws, ws],
        out_specs=[zs, zs, ws, ws, bs, bs],
        out_shape=[jax.ShapeDtypeStruct((T, D_FF), BF16)] * 2 + [jax.ShapeDtypeStruct((3, D_FF), F32)] * 2
        + [jax.ShapeDtypeStruct((1, D_FF), F32)] * 2,
        scratch_shapes=[pltpu.VMEM((seq, tn), F32), pltpu.VMEM((seq, tn), F32)],
        compiler_params=_params(("parallel", "arbitrary")),
    )(z_g, z_v, c_g, c_v, dx2_bf, w_down, cw_g, cw_v)
    dz_g, dz_v, dw_g, dw_v, db_g, db_v = outs
    return dz_g, dz_v, dw_g, dw_v, db_g.reshape(D_FF), db_v.reshape(D_FF)


def _layer_fwd(x, h, w, sched, tail, *, n_seq, seq, l):
    tag = f"l{l}"
    deps = sched("fwd_start", l, h)
    proj = _mm(h, w["w_in_t"], mode="nt", out_dtype=ACT_DTYPE, rotate=W_IN_ROTATE, name=f"{tag}_proj", deps=deps)
    y_att = _attention_fwd(proj, w["q_norm"], w["k_norm"], w["sinks"], n_seq=n_seq, seq=seq, name=f"{tag}_att")
    deps = sched("fwd_att", l, y_att)
    y_sgu = _sgu_fwd(proj, w["sgu_norm"], w["w_s"], w["bias_full"], n_seq=n_seq, seq=seq, name=f"{tag}_sgu")
    merged = _merge_fwd(y_att, y_sgu, w["w_oa"], w["w_ob"], proj, name=f"{tag}_merge", deps=deps)
    x1, h2 = _mm_rows(merged, w["w_out"], mode="nn", fn=_residual_then_norm, out_dtypes=(F32, BF16), rows=(x,),
                      vecs=(w["ffn_norm"],), name=f"{tag}_out")
    deps = sched("fwd_mixer_done", l, x1)
    a, z_g, z_v, c_g, c_v = _up_conv_fwd(h2, w["w_up_t"], w["cw_g"], w["cw_v"], w["cb_g"], w["cb_v"], n_seq=n_seq,
                                         seq=seq, name=f"{tag}_up_conv", deps=deps)
    deps = sched("fwd_conv", l, a)
    if tail[0] == "norm":
        out = _mm_rows(a, w["w_down"], mode="nn", fn=_residual_then_norm, out_dtypes=(F32, BF16), rows=(x1,),
                       vecs=(tail[1],), name=f"{tag}_down", deps=deps)
    else:
        out = _mm_rows(a, w["w_down"], mode="nn", fn=_residual_then_loss, out_dtypes=(F32, BF16), rows=(x1, tail[1]),
                       reduce=True, name=f"{tag}_down", deps=deps)
    saved = dict(x=x, h=h, proj=proj, y_att=y_att, y_sgu=y_sgu, merged=merged, x1=x1, h2=h2, z_g=z_g, z_v=z_v,
                 c_g=c_g, c_v=c_v, a=a)
    return out, saved


def _layer_bwd(dx2, dx2_bf, w, s, sched, deps, *, n_seq, seq, l):
    tag = f"l{l}b"
    g = {}
    g["w_down"] = _mm(s["a"], dx2_bf, mode="tn", out_dtype=F32, name=f"{tag}_dw_down", deps=deps)
    dz_g, dz_v, g["cw_g"], g["cw_v"], g["cb_g"], g["cb_v"] = _conv_bwd(
        s["z_g"], s["z_v"], s["c_g"], s["c_v"], dx2_bf, w["w_down"], w["cw_g"], w["cw_v"], n_seq=n_seq, seq=seq,
        name=f"{tag}_conv")
    dw_up_t = _mm(dz_g, s["h2"], mode="tn", out_dtype=F32, out_rows=(0, 2 * D_FF), name=f"{tag}_dw_up_g")
    g["w_up_t"] = _mm(dz_v, s["h2"], mode="tn", out_dtype=F32, out_rows=(D_FF, 2 * D_FF), out_prev=dw_up_t,
                      name=f"{tag}_dw_up_v")
    deps = sched("bwd_ffn_grads", l, dz_v, g)
    dx1, dx1_bf, dgain = _mm_rows((dz_g, dz_v), w["w_up_t"], mode="nn", fn=_rms_bwd_rows, out_dtypes=(F32, BF16),
                                  rows=(s["x1"], dx2), vecs=(w["ffn_norm"],), reduce=True, a_at=(0, D_FF),
                                  name=f"{tag}_dh2", deps=deps)
    g["ffn_norm"] = dgain.reshape(D_MODEL)
    dpa, dpb, dga, dgb = _merge_bwd(dx1_bf, w["w_out"], s["y_att"], s["y_sgu"], w["w_oa"], w["w_ob"], s["proj"],
                                    name=f"{tag}_merge")
    deps = sched("bwd_merge", l, dpa)
    g["w_out"] = _mm(s["merged"], dx1_bf, mode="tn", out_dtype=F32, name=f"{tag}_dw_out",
                     deps=deps)
    dy_att = _mm(dpa, w["w_oa"], mode="nt", out_dtype=BF16, name=f"{tag}_dy_att")
    dy_sgu = _mm(dpb, w["w_ob"], mode="nt", out_dtype=F32, name=f"{tag}_dy_sgu")
    g["w_oa"] = _mm(s["y_att"], dpa, mode="tn", out_dtype=F32, name=f"{tag}_dw_oa")
    g["w_ob"] = _mm(s["y_sgu"], dpb, mode="tn", out_dtype=F32, name=f"{tag}_dw_ob")
    deps = sched("bwd_out_grads", l, dy_att, g)
    dqkv, g["q_norm"], g["k_norm"], g["sinks"] = _attention_bwd(
        s["proj"], dy_att, w["q_norm"], w["k_norm"], w["sinks"], n_seq=n_seq, seq=seq, name=f"{tag}_att", deps=deps)
    deps = sched("bwd_att", l, dqkv)
    dsuv, g["sgu_norm"], g["w_s"], g["b_s"] = _sgu_bwd(
        s["proj"], dy_sgu, w["sgu_norm"], w["w_s"], w["bias_full"], n_seq=n_seq, seq=seq, name=f"{tag}_sgu", deps=deps)
    dproj = (dsuv, dga, dgb, dqkv)
    at = (QKV_WIDTH, QKV_WIDTH + 2 * SGU_WIDTH, QKV_WIDTH + 2 * SGU_WIDTH + D_MODEL, 0)
    g["w_in_t"] = _mm_tn_parts(dproj, at, s["h"], name=f"{tag}_dw_in")
    deps = sched("bwd_w_in_grad", l, dqkv, g)
    dx, dx_bf, dgain = _mm_rows(dproj, w["w_in_t"], mode="nn", fn=_rms_bwd_rows, out_dtypes=(F32, BF16),
                                rows=(s["x"], dx1), vecs=(w["mix_norm"],), reduce=True, a_at=at,
                                name=f"{tag}_dh", deps=deps)
    g["mix_norm"] = dgain.reshape(D_MODEL)
    return dx, dx_bf, g, sched("bwd_dh", l, dx)


def _local_step(x, target, weights, sched, *, n_seq, seq):
    depth = len(weights)
    saved = []
    h = _rms_fwd(x, weights[0]["mix_norm"], name="l0_mix_norm", deps=sched("begin", 0, x))
    for l in range(depth):
        tail = ("norm", weights[l + 1]["mix_norm"]) if l + 1 < depth else ("loss", target)
        out, s = _layer_fwd(x, h, weights[l], sched, tail, n_seq=n_seq, seq=seq, l=l)
        saved.append(s)
        if l + 1 < depth:
            x, h = out
    dy, dy_bf, loss_cols = out
    grads = [None] * depth
    deps = ()
    for l in reversed(range(depth)):
        dy, dy_bf, grads[l], deps = _layer_bwd(dy, dy_bf, weights[l], saved[l], sched, deps, n_seq=n_seq, seq=seq, l=l)
    return jnp.sum(loss_cols), dy, grads, deps


W_IN_SHARD = IN_WIDTH // N_DEV
W_UP_SHARD = 2 * D_FF // N_DEV
COL_MOVE_ROWS = 256


def _w_o_moves():
    return tuple((j, 0, LANES, 0, j * LANES) for j in range(N_DEV))


def _disassemble(mats, w, moves, *, name):
    R = mats[0].shape[0]
    tr = min(R, COL_MOVE_ROWS)
    n = len(mats)

    def body(*refs):
        m_refs, o_ref = refs[:n], refs[n]
        for j, lo, hi, which, at in moves:
            o_ref[j, :, lo:hi] = m_refs[which][:, at:at + hi - lo]

    return pl.pallas_call(
        body, name=name, grid=(R // tr,),
        in_specs=[pl.BlockSpec((tr, m.shape[1]), lambda i: (i, 0)) for m in mats],
        out_specs=pl.BlockSpec((N_DEV, tr, w), lambda i: (0, i, 0)),
        out_shape=jax.ShapeDtypeStruct((N_DEV, R, w), mats[0].dtype),
        compiler_params=_params(("parallel",)),
    )(*mats)


def _my_place():
    return lax.axis_index("x"), lax.axis_index("y"), lax.axis_index("c")


def _gathered_shape(shape, kind):
    r, c = shape
    return {"blocks": (N_DEV, r, c), "rows": (N_DEV * r, c), "cols": (r, N_DEV * c)}[kind]


def _gather_window(ref, kind, shape, j):
    r, c = shape
    if kind == "blocks":
        return ref.at[j]
    if kind == "rows":
        return ref.at[pl.ds(pl.multiple_of(j * r, r), r), :]
    return ref.at[:, pl.ds(pl.multiple_of(j * c, c), c)]


def _gather(srcs, kinds, *, name):
    n = len(srcs)
    shapes = [s.shape for s in srcs]
    per = 7

    def body(*refs):
        src_refs, dst_refs = refs[:n], refs[n:2 * n]
        send_sems, recv_sems, local_sems = refs[2 * n:]
        x, y, c = _my_place()
        me, sibling = (x, y, c), (x, y, 1 - c)
        chips = [(1 - x, y), (x, 1 - y), (1 - x, 1 - y)]

        def at(i, px, py, pc):
            return _gather_window(dst_refs[i], kinds[i], shapes[i], 4 * px + 2 * py + pc)

        def copy(i, k, block, to, src=None):
            return pltpu.make_async_remote_copy(
                src_ref=at(i, *block) if src is None else src, dst_ref=at(i, *block),
                send_sem=send_sems.at[per * i + k], recv_sem=recv_sems.at[per * i + k], device_id=to, device_id_type=MESH)

        mine = [pltpu.make_async_copy(src_refs[i], at(i, *me), local_sems.at[i]) for i in range(n)]
        for cp in mine:
            cp.start()
        started = []
        for i in range(n):
            first = [copy(i, 0, me, sibling, src=src_refs[i])]
            first += [copy(i, 1 + j, me, (*chip, c), src=src_refs[i]) for j, chip in enumerate(chips)]
            for cp in first:
                cp.start()
            started += first
        for i in range(n):
            for j, chip in enumerate(chips):
                copy(i, 1 + j, (*chip, c), me).wait_recv()
                fwd = copy(i, 4 + j, (*chip, c), sibling)
                fwd.start()
                started.append(fwd)
        for i in range(n):
            copy(i, 0, sibling, me).wait_recv()
            for j, chip in enumerate(chips):
                copy(i, 4 + j, (*chip, 1 - c), me).wait_recv()
        for cp in started:
            cp.wait_send()
        for cp in mine:
            cp.wait()

    return pl.pallas_call(
        body, name=name,
        out_shape=[jax.ShapeDtypeStruct(_gathered_shape(s.shape, k), s.dtype) for s, k in zip(srcs, kinds)],
        in_specs=[ANY] * n, out_specs=[ANY] * n,
        scratch_shapes=[pltpu.SemaphoreType.DMA((per * n,)), pltpu.SemaphoreType.DMA((per * n,)),
                        pltpu.SemaphoreType.DMA((n,))],
    )(*srcs)


HBM = pl.BlockSpec(memory_space=pltpu.HBM)
SEM = pl.BlockSpec(memory_space=pltpu.SEMAPHORE)
TOKEN = jax.ShapeDtypeStruct((SUBLANES, LANES), F32)
TOKEN_SPEC = pl.BlockSpec(memory_space=pltpu.VMEM)
SPLIT_PARAMS = pltpu.CompilerParams(has_side_effects=pltpu.SideEffectType.DATAFLOW_SIDE_EFFECTING)


def _in_hbm(x):
    return pltpu.with_memory_space_constraint(x, pltpu.HBM)


def _hbm_like(shape, dtype):
    return pltpu.HBM(shape, dtype)


def _place_own(stacks, layers, kinds, dtypes, *, name, deps=()):
    n = len(stacks)
    shapes = [s.shape[1:] for s in stacks]

    def body(*refs):
        s_refs, land_refs, bufs, sems = refs[:n], refs[n:2 * n], refs[2 * n:3 * n], refs[3 * n]
        x, y, c = _my_place()
        copies = []
        for i in range(n):
            bufs[i][...] = s_refs[i][...].astype(dtypes[i])
            copies.append(pltpu.make_async_copy(
                bufs[i], _gather_window(land_refs[i], kinds[i], shapes[i], 4 * x + 2 * y + c), sems.at[i]))
        for cp in copies:
            cp.start()
        for cp in copies:
            cp.wait()

    def layer_of(shape, l):
        return pl.BlockSpec((None,) + shape, lambda i: (l,) + (0,) * len(shape))

    body, dep_specs, dep_args = _with_deps(body, n, deps)
    return pl.pallas_call(
        body, name=name, grid=(1,),
        out_shape=[jax.ShapeDtypeStruct(_gathered_shape(s, k), d) for s, k, d in zip(shapes, kinds, dtypes)],
        in_specs=[layer_of(s, l) for s, l in zip(shapes, layers)] + dep_specs, out_specs=[ANY] * n,
        scratch_shapes=[pltpu.VMEM(s, d) for s, d in zip(shapes, dtypes)] + [pltpu.SemaphoreType.DMA((n,))],
        compiler_params=_params(("arbitrary",)),
    )(*stacks, *dep_args)


def _gather_start(lands, kinds, shapes, after=(), *, name):
    n = len(lands)
    n_after = len(after)

    def body(*refs):
        land_refs = refs[:n]
        send_sems, recv_sems = refs[n + n_after], refs[n + n_after + 1]
        x, y, c = _my_place()
        targets = [(x, y, 1 - c), (1 - x, y, c), (x, 1 - y, c), (1 - x, 1 - y, c)]
        for i in range(n):
            own = _gather_window(land_refs[i], kinds[i], shapes[i], 4 * x + 2 * y + c)
            for k, to in enumerate(targets):
                pltpu.make_async_remote_copy(
                    src_ref=own, dst_ref=own, send_sem=send_sems.at[4 * i + k], recv_sem=recv_sems.at[4 * i + k],
                    device_id=to, device_id_type=MESH).start()
        refs[-1][...] = jnp.zeros_like(refs[-1])

    outs = pl.pallas_call(
        body, name=name,
        out_shape=[pltpu.SemaphoreType.DMA((4 * n,)), pltpu.SemaphoreType.DMA((4 * n,))]
        + [_hbm_like(a.shape, a.dtype) for a in lands] + [TOKEN],
        in_specs=[HBM] * n + [ANY] * n_after, out_specs=[SEM, SEM] + [HBM] * n + [TOKEN_SPEC],
        input_output_aliases={i: 2 + i for i in range(n)},
        compiler_params=SPLIT_PARAMS,
    )(*[_in_hbm(a) for a in lands], *after)
    return outs[0], outs[1], outs[2:2 + n], outs[-1]


def _gather_forward(recv_sems, lands, kinds, shapes, after, *, name):
    n = len(lands)

    def body(*refs):
        recv_ref, land_refs = refs[0], refs[1:1 + n]
        fwd_send, fwd_recv = refs[2 + n], refs[3 + n]
        token = refs[-1]
        x, y, c = _my_place()
        chips = [(1 - x, y), (x, 1 - y), (1 - x, 1 - y)]
        for i in range(n):
            for j, (px, py) in enumerate(chips):
                block = _gather_window(land_refs[i], kinds[i], shapes[i], 4 * px + 2 * py + c)
                pltpu.make_async_remote_copy(
                    src_ref=block, dst_ref=block, send_sem=fwd_send.at[3 * i + j], recv_sem=recv_ref.at[4 * i + 1 + j],
                    device_id=(px, py, c), device_id_type=MESH).wait_recv()
                pltpu.make_async_remote_copy(
                    src_ref=block, dst_ref=block, send_sem=fwd_send.at[3 * i + j], recv_sem=fwd_recv.at[3 * i + j],
                    device_id=(x, y, 1 - c), device_id_type=MESH).start()
        token[...] = jnp.zeros_like(token)

    outs = pl.pallas_call(
        body, name=name,
        out_shape=[pltpu.SemaphoreType.DMA((3 * n,)), pltpu.SemaphoreType.DMA((3 * n,))]
        + [_hbm_like(a.shape, a.dtype) for a in lands] + [TOKEN],
        in_specs=[SEM] + [HBM] * n + [ANY], out_specs=[SEM, SEM] + [HBM] * n + [TOKEN_SPEC],
        input_output_aliases={1 + i: 2 + i for i in range(n)},
        compiler_params=SPLIT_PARAMS,
    )(recv_sems, *lands, after)
    return outs[0], outs[1], outs[2:2 + n], outs[-1]


def _gather_finish(send_sems, recv_sems, fwd_send, fwd_recv, lands, kinds, shapes, after, *, name):
    n = len(lands)

    def body(*refs):
        send_ref, recv_ref, fsend_ref, frecv_ref = refs[:4]
        land_refs = refs[4:4 + n]
        x, y, c = _my_place()
        chips = [(1 - x, y), (x, 1 - y), (1 - x, 1 - y)]
        sibling = (x, y, 1 - c)
        for i in range(n):
            def window(j):
                return _gather_window(land_refs[i], kinds[i], shapes[i], j)

            mine, theirs = window(4 * x + 2 * y + c), window(4 * x + 2 * y + (1 - c))
            pltpu.make_async_remote_copy(src_ref=mine, dst_ref=theirs, send_sem=send_ref.at[4 * i],
                                         recv_sem=recv_ref.at[4 * i], device_id=sibling, device_id_type=MESH).wait_recv()
            for j, (px, py) in enumerate(chips):
                block = window(4 * px + 2 * py + (1 - c))
                pltpu.make_async_remote_copy(src_ref=block, dst_ref=block, send_sem=fsend_ref.at[3 * i + j],
                                             recv_sem=frecv_ref.at[3 * i + j], device_id=sibling,
                                             device_id_type=MESH).wait_recv()
            for k in range(4):
                pltpu.make_async_remote_copy(src_ref=mine, dst_ref=mine, send_sem=send_ref.at[4 * i + k],
                                             recv_sem=recv_ref.at[4 * i + k], device_id=sibling,
                                             device_id_type=MESH).wait_send()
            for j, (px, py) in enumerate(chips):
                block = window(4 * px + 2 * py + c)
                pltpu.make_async_remote_copy(src_ref=block, dst_ref=block, send_sem=fsend_ref.at[3 * i + j],
                                             recv_sem=frecv_ref.at[3 * i + j], device_id=sibling,
                                             device_id_type=MESH).wait_send()

    return pl.pallas_call(
        body, name=name,
        out_shape=[_hbm_like(a.shape, a.dtype) for a in lands],
        in_specs=[SEM] * 4 + [HBM] * n + [ANY], out_specs=[HBM] * n,
        input_output_aliases={4 + i: i for i in range(n)},
        compiler_params=SPLIT_PARAMS,
    )(send_sems, recv_sems, fwd_send, fwd_recv, *lands, after)


def _pair_plan(src_ref, land_ref, x, y, c):
    return [(src_ref.at[2 * k + (1 - c)], land_ref.at[k], (x, y, 1 - c)) for k in range(N_CHIPS)]


def _chip_plan(src_ref, land_ref, x, y, c):
    chips = [(1 - x, y), (x, 1 - y), (1 - x, 1 - y)]
    return [(src_ref.at[2 * px + py], land_ref.at[k], (px, py, c)) for k, (px, py) in enumerate(chips)]


def _exchange_copies(plan, per, src_refs, land_refs, send_sems, recv_sems):
    x, y, c = _my_place()
    copies = []
    for i, (s_ref, l_ref) in enumerate(zip(src_refs, land_refs)):
        for q, (src, dst, to) in enumerate(plan(s_ref, l_ref, x, y, c)):
            copies.append(pltpu.make_async_remote_copy(
                src_ref=src, dst_ref=dst, send_sem=send_sems.at[per * i + q], recv_sem=recv_sems.at[per * i + q],
                device_id=to, device_id_type=MESH))
    return copies


def _exchange_start(srcs, plan, per, *, name):
    n = len(srcs)

    def body(*refs):
        src_refs, land_refs = refs[:n], refs[n:2 * n]
        send_sems, recv_sems = refs[2 * n], refs[2 * n + 1]
        for cp in _exchange_copies(plan, per, src_refs, land_refs, send_sems, recv_sems):
            cp.start()
        refs[-1][...] = jnp.zeros_like(refs[-1])

    lands = [lax.empty((per,) + s.shape[1:], s.dtype) for s in srcs]
    outs = pl.pallas_call(
        body, name=name,
        out_shape=[pltpu.SemaphoreType.DMA((per * n,)), pltpu.SemaphoreType.DMA((per * n,))]
        + [_hbm_like(s.shape, s.dtype) for s in srcs] + [_hbm_like(a.shape, a.dtype) for a in lands] + [TOKEN],
        in_specs=[HBM] * (2 * n), out_specs=[SEM, SEM] + [HBM] * (2 * n) + [TOKEN_SPEC],
        input_output_aliases={i: 2 + i for i in range(2 * n)},
        compiler_params=SPLIT_PARAMS,
    )(*[_in_hbm(s) for s in srcs], *[_in_hbm(a) for a in lands])
    return outs[0], outs[1], outs[2:2 + n], outs[2 + n:2 + 2 * n], outs[-1]


def _exchange_wait(send_sems, recv_sems, srcs, lands, plan, per, after, *, name):
    n = len(srcs)
    after = list(after) if isinstance(after, (list, tuple)) else [after]

    def body(*refs):
        send_ref, recv_ref = refs[0], refs[1]
        src_refs, land_refs = refs[2:2 + n], refs[2 + n:2 + 2 * n]
        copies = _exchange_copies(plan, per, src_refs, land_refs, send_ref, recv_ref)
        for cp in copies:
            cp.wait_recv()
        for cp in copies:
            cp.wait_send()

    outs = pl.pallas_call(
        body, name=name,
        out_shape=[_hbm_like(s.shape, s.dtype) for s in srcs] + [_hbm_like(a.shape, a.dtype) for a in lands],
        in_specs=[SEM, SEM] + [HBM] * (2 * n) + [ANY] * len(after), out_specs=[HBM] * (2 * n),
        input_output_aliases={2 + i: i for i in range(2 * n)},
        compiler_params=SPLIT_PARAMS,
    )(send_sems, recv_sems, *srcs, *lands, *after)
    return outs[:n], outs[n:]


REDUCE_BLOCK_BYTES = 2 << 20


def _row_tile(r, c):
    row_bytes = 4 * (-(-c // LANES) * LANES)
    best = r
    for d in range(SUBLANES, r, SUBLANES):
        if r % d == 0 and d * row_bytes <= REDUCE_BLOCK_BYTES:
            best = d
    return best if r * row_bytes > REDUCE_BLOCK_BYTES else r


def _reduce_pair_sum(blocked, recv, place, wire_dtype, *, name):
    _, r, c = blocked.shape
    tr = _row_tile(r, c)

    def body(place_ref, g_ref, r_ref, own_ref, send_ref):
        s = g_ref[...] + r_ref[...]
        send_ref[...] = s.astype(wire_dtype)

        @pl.when(pl.program_id(1) == place_ref[1])
        def _():
            own_ref[...] = s

    return pl.pallas_call(
        body, name=name,
        grid_spec=pltpu.PrefetchScalarGridSpec(
            num_scalar_prefetch=1, grid=(r // tr, N_CHIPS),
            in_specs=[pl.BlockSpec((None, None, tr, c), lambda i, k, place_ref: (k, place_ref[0], i, 0)),
                      pl.BlockSpec((None, tr, c), lambda i, k, place_ref: (k, i, 0))],
            out_specs=[pl.BlockSpec((tr, c), lambda i, k, place_ref: (i, 0)),
                       pl.BlockSpec((None, tr, c), lambda i, k, place_ref: (k, i, 0))]),
        out_shape=[jax.ShapeDtypeStruct((r, c), F32), jax.ShapeDtypeStruct((N_CHIPS, r, c), wire_dtype)],
        compiler_params=_params(("parallel", "arbitrary")),
    )(place, blocked.reshape(N_CHIPS, 2, r, c), recv)


def _chip_sum(own_ref, r_ref):
    return ((own_ref[...] + r_ref[0].astype(F32)) + r_ref[1].astype(F32)) + r_ref[2].astype(F32)


def _reduce_chip_sum(own, recv, *, name):
    r, c = own.shape
    tr = _row_tile(r, c)

    def body(own_ref, r_ref, o_ref):
        o_ref[...] = _chip_sum(own_ref, r_ref)

    return pl.pallas_call(
        body, name=name, grid=(r // tr,),
        in_specs=[pl.BlockSpec((tr, c), lambda i: (i, 0)), pl.BlockSpec((N_CHIPS - 1, tr, c), lambda i: (0, i, 0))],
        out_specs=pl.BlockSpec((tr, c), lambda i: (i, 0)),
        out_shape=jax.ShapeDtypeStruct((r, c), F32),
        compiler_params=_params(("parallel",)),
    )(own, recv)


def _adamw_math(w, g, m, v):
    nm = ADAM_B1 * m + (1.0 - ADAM_B1) * g
    nv = ADAM_B2 * v + (1.0 - ADAM_B2) * (g * g)
    m_hat = nm / (1.0 - ADAM_B1 ** ADAM_STEP)
    v_hat = nv / (1.0 - ADAM_B2 ** ADAM_STEP)
    return -ADAM_LR * (m_hat / (jnp.sqrt(v_hat) + ADAM_EPS) + ADAM_WD * w), nm, nv


ADAMW_ROWS = 256


def _adamw_small(ws, gs, ms, vs, *, name):
    n = len(ws)

    def rows_of(a):
        return a.reshape(-1, a.shape[-1])

    def body(*refs):
        for i in range(n):
            w_ref, g_ref, m_ref, v_ref = refs[4 * i:4 * i + 4]
            outs = refs[4 * n + 3 * i:4 * n + 3 * i + 3]
            rows = w_ref.shape[0]
            if rows % ADAMW_ROWS:
                outs[0][...], outs[1][...], outs[2][...] = _adamw_math(w_ref[...], g_ref[...], m_ref[...], v_ref[...])
                continue

            def chunk(s, carry, w_ref=w_ref, g_ref=g_ref, m_ref=m_ref, v_ref=v_ref, outs=outs):
                r = pl.ds(pl.multiple_of(s * ADAMW_ROWS, ADAMW_ROWS), ADAMW_ROWS)
                outs[0][r, :], outs[1][r, :], outs[2][r, :] = _adamw_math(w_ref[r, :], g_ref[r, :], m_ref[r, :], v_ref[r, :])
                return carry

            lax.fori_loop(0, rows // ADAMW_ROWS, chunk, 0)

    vmem = pl.BlockSpec(memory_space=pltpu.VMEM)
    outs = pl.pallas_call(
        body, name=name, in_specs=[vmem] * (4 * n), out_specs=[vmem] * (3 * n),
        out_shape=[jax.ShapeDtypeStruct(rows_of(w).shape, F32) for w in ws for _ in range(3)],
        compiler_params=_params(),
    )(*[rows_of(a) for quad in zip(ws, gs, ms, vs) for a in quad])
    return [tuple(o.reshape(w.shape) for o in outs[3 * i:3 * i + 3]) for i, w in enumerate(ws)]


def _reduce_adamw(own, recv, w, m, v, layer, prev, *, name):
    r, c = own.shape
    tr = _row_tile(r, c)
    n_prev = 0 if prev is None else len(prev)

    def body(own_ref, r_ref, w_ref, m_ref, v_ref, *rest):
        g_ref, d_ref, nm_ref, nv_ref = rest[n_prev:]
        g = _chip_sum(own_ref, r_ref)
        g_ref[...] = g
        d_ref[...], nm_ref[...], nv_ref[...] = _adamw_math(w_ref[...], g, m_ref[...], v_ref[...])

    slot = pl.BlockSpec((None, tr, c), lambda i: (layer, i, 0))
    return pl.pallas_call(
        body, name=name, grid=(r // tr,),
        in_specs=[pl.BlockSpec((tr, c), lambda i: (i, 0)), pl.BlockSpec((N_CHIPS - 1, tr, c), lambda i: (0, i, 0)),
                  slot, slot, slot] + [ANY] * n_prev,
        out_specs=[slot] * 4,
        out_shape=[jax.ShapeDtypeStruct((DEPTH, r, c), F32)] * 4,
        input_output_aliases={5 + k: k for k in range(n_prev)},
        compiler_params=_params(("parallel",)),
    )(own, recv, w, m, v, *(prev or ()))


REPLICATED = (("mix_norm", (D_MODEL,)), ("q_norm", (HEAD_DIM,)), ("k_norm", (HEAD_DIM,)), ("sinks", (N_Q_HEADS,)),
              ("sgu_norm", (SGU_WIDTH,)), ("w_s", (SGU_GROUPS, BLOCK, BLOCK)), ("b_s", (SGU_GROUPS, BLOCK)),
              ("ffn_norm", (D_MODEL,)), ("conv_b", (2 * D_FF,)))
TRANSPOSED = ("w_in", "w_up")
SHARDED = (("w_in", "rows"), ("w_oa", "cols"), ("w_ob", "cols"), ("w_out", "rows"), ("w_up", "rows"),
           ("conv_w", "blocks"), ("w_down", "rows"))
WEIGHT_ORDER = ("mix_norm", "w_in", "q_norm", "k_norm", "sinks", "sgu_norm", "w_s", "b_s", "w_oa", "w_ob", "w_out",
                "ffn_norm", "w_up", "conv_w", "conv_b", "w_down")
MIXER_WEIGHTS = ["w_in", "w_oa", "w_ob", "w_out"]
FFN_WEIGHTS = ["w_up", "conv_w", "w_down"]


def _small_layout():
    segs, off = {}, 0
    for name, shape in REPLICATED:
        for l in range(DEPTH):
            n = math.prod(shape)
            segs[(l, name)] = (off, n)
            off += n
    per_dev = -(-off // (N_DEV * SUBLANES * LANES)) * SUBLANES * LANES
    return segs, off, per_dev


def _pack_small(grads, loss_part):
    ssegs, total, per_dev = _small_layout()
    flat = jnp.concatenate([grads[l][name].reshape(-1) for (l, name) in ssegs] + [loss_part.reshape(1)])
    return jnp.pad(flat, (0, N_DEV * per_dev - total - 1)).reshape(N_DEV, per_dev // LANES, LANES)


def _unpack_small(gathered):
    ssegs, total, _ = _small_layout()
    flat = gathered.reshape(-1)
    small = {}
    for name, shape in REPLICATED:
        start, n = ssegs[(0, name)]
        small[name] = flat[start:start + DEPTH * n].reshape((DEPTH,) + shape)
    return small, flat[total]


def kernel(x, mix_norm, w_in, q_norm, k_norm, sinks, sgu_norm, w_s, b_s, w_oa, w_ob, w_out, ffn_norm, w_up, conv_w, conv_b, w_down, loss_target, m_mix_norm, m_w_in, m_q_norm, m_k_norm, m_sinks, m_sgu_norm, m_w_s, m_b_s, m_w_oa, m_w_ob, m_w_out, m_ffn_norm, m_w_up, m_conv_w, m_conv_b, m_w_down, v_mix_norm, v_w_in, v_q_norm, v_k_norm, v_sinks, v_sgu_norm, v_w_s, v_b_s, v_w_oa, v_w_ob, v_w_out, v_ffn_norm, v_w_up, v_conv_w, v_conv_b, v_w_down):
    W = dict(mix_norm=mix_norm, w_in=w_in, q_norm=q_norm, k_norm=k_norm, sinks=sinks, sgu_norm=sgu_norm, w_s=w_s, b_s=b_s,
             w_oa=w_oa, w_ob=w_ob, w_out=w_out, ffn_norm=ffn_norm, w_up=w_up, conv_w=conv_w, conv_b=conv_b, w_down=w_down)
    M = dict(mix_norm=m_mix_norm, w_in=m_w_in, q_norm=m_q_norm, k_norm=m_k_norm, sinks=m_sinks, sgu_norm=m_sgu_norm,
             w_s=m_w_s, b_s=m_b_s, w_oa=m_w_oa, w_ob=m_w_ob, w_out=m_w_out, ffn_norm=m_ffn_norm, w_up=m_w_up,
             conv_w=m_conv_w, conv_b=m_conv_b, w_down=m_w_down)
    V = dict(mix_norm=v_mix_norm, w_in=v_w_in, q_norm=v_q_norm, k_norm=v_k_norm, sinks=v_sinks, sgu_norm=v_sgu_norm,
             w_s=v_w_s, b_s=v_b_s, w_oa=v_w_oa, w_ob=v_w_ob, w_out=v_w_out, ffn_norm=v_ffn_norm, w_up=v_w_up,
             conv_w=v_conv_w, conv_b=v_conv_b, w_down=v_w_down)
    n_seq, seq, d_model = x.shape
    tokens = n_seq * seq
    mx, my, mc = _my_place()
    place = jnp.stack([mc, 2 * mx + my]).astype(jnp.int32)
    half = N_DEV // 2
    kind_of = dict(SHARDED)
    for name in TRANSPOSED:
        W[name], M[name], V[name] = (jnp.swapaxes(t[name], 1, 2) for t in (W, M, V))

    gather_groups = [[(0, MIXER_WEIGHTS[0])], [(0, n) for n in MIXER_WEIGHTS[1:]], [(0, n) for n in FFN_WEIGHTS],
                     [(1, n) for n in MIXER_WEIGHTS], [(1, n) for n in FFN_WEIGHTS]]
    started, in_flight = {}, {}
    weights = []
    for l in range(DEPTH):
        w = {name: W[name][l] for name, _ in REPLICATED}
        w["cb_g"], w["cb_v"] = W["conv_b"][l][:D_FF], W["conv_b"][l][D_FF:]
        w["bias_full"] = jnp.repeat(W["b_s"][l].T, SGU_WIDTH // SGU_GROUPS, axis=1)
        weights.append(w)

    def gather_start(gi, after=()):
        stacks = [W[name] for _, name in gather_groups[gi]]
        kinds = [kind_of[name] for _, name in gather_groups[gi]]
        shapes = [s.shape[1:] for s in stacks]
        lands = _place_own(stacks, [l for l, _ in gather_groups[gi]], kinds,
                           [F32 if name == "conv_w" else BF16 for _, name in gather_groups[gi]],
                           name=f"gather_weights_own_{gi}", deps=after)
        send, recv, lands, token = _gather_start(lands, kinds, shapes, after, name=f"gather_weights_start_{gi}")
        started[gi] = dict(sems=(send, recv), lands=lands, kinds=kinds, shapes=shapes)
        return token

    def gather_forward(gi, after):
        st = started[gi]
        in_flight[gi] = _gather_forward(st["sems"][1], st["lands"], st["kinds"], st["shapes"], after,
                                        name=f"gather_weights_forward_{gi}")
        return in_flight[gi][3]

    def gather_finish(gi, after):
        st = started.pop(gi)
        fwd_send, fwd_recv, lands_g, _ = in_flight.pop(gi)
        whole = _gather_finish(st["sems"][0], st["sems"][1], fwd_send, fwd_recv, lands_g, st["kinds"], st["shapes"], after,
                               name=f"gather_weights_finish_{gi}")
        for (l, name), arr in zip(gather_groups[gi], whole):
            w = weights[l]
            if name in TRANSPOSED:
                w[name + "_t"] = arr
            elif name == "conv_w":
                w["cw_g"] = arr[:half].transpose(1, 0, 2).reshape(3, D_FF)
                w["cw_v"] = arr[half:].transpose(1, 0, 2).reshape(3, D_FF)
            else:
                w[name] = arr

    reduce_state, results = {}, {}
    wire = {"conv_w": F32, "small": F32}

    def reduce_begin(key, names, arrays):
        send, recv, srcs_, lands_, token = _exchange_start(arrays, _pair_plan, N_CHIPS, name=f"reduce_pair_start_{key}")
        reduce_state[key] = dict(names=names, pair=(send, recv, srcs_, lands_))
        return [token]

    def reduce_pair(key, after):
        st = reduce_state[key]
        send, recv, srcs_, lands_ = st.pop("pair")
        blocked_, from_sibling = _exchange_wait(send, recv, srcs_, lands_, _pair_plan, N_CHIPS, after,
                                                name=f"reduce_pair_wait_{key}")
        sums = [_reduce_pair_sum(b, r, place, wire.get(n if isinstance(n, str) else n[1], BF16),
                                 name=f"reduce_pair_sum_{key}_{i}")
                for i, (n, b, r) in enumerate(zip(st["names"], blocked_, from_sibling))]
        st["own"] = [s[0] for s in sums]
        *st["chip"], token = _exchange_start([s[1] for s in sums], _chip_plan, N_CHIPS - 1, name=f"reduce_chip_start_{key}")
        return [token]

    def reduce_end(key, after):
        st = reduce_state.pop(key)
        send, recv, srcs_, lands_ = st["chip"]
        _, from_chips = _exchange_wait(send, recv, srcs_, lands_, _chip_plan, N_CHIPS - 1, after,
                                       name=f"reduce_chip_wait_{key}")
        done = []
        for n, own, got in zip(st["names"], st["own"], from_chips):
            if n == "small":
                results["small"] = _reduce_chip_sum(own, got, name="reduce_chip_sum_small")
            else:
                l, name = n
                results[name] = _reduce_adamw(own, got, W[name], M[name], V[name], l, results.get(name),
                                              name=f"l{l}_reduce_adamw_{name}")
                done.append(results[name][0])
        return done

    def sched(point, l, carry, g=None):
        deps = []
        if point == "begin":
            token = ()
            for gi in range(len(gather_groups)):
                token = [gather_start(gi, token)]
            deps = [gather_forward(0, token[0])]
        elif point == "fwd_start" and l == 0:
            gather_finish(0, carry)
            deps = [gather_forward(1, weights[0]["w_in_t"])]
        elif point == "fwd_att" and l == 0:
            gather_finish(1, carry)
            deps = [gather_forward(2, carry)]
        elif point == "fwd_mixer_done" and l == 0:
            gather_finish(2, carry)
        elif point == "fwd_conv" and l == 0:
            deps = [gather_forward(3, carry)]
        elif point == "fwd_start" and l == 1:
            gather_finish(3, carry)
        elif point == "fwd_att" and l == 1:
            deps = [gather_forward(4, carry)]
        elif point == "fwd_mixer_done" and l == 1:
            gather_finish(4, carry)
        elif point == "bwd_ffn_grads":
            conv_w = jnp.concatenate([g[k].reshape(3, half, W_UP_SHARD).transpose(1, 0, 2) for k in ("cw_g", "cw_v")])
            deps = reduce_begin(
                f"l{l}_ffn", [(l, "w_down"), (l, "w_up"), (l, "conv_w")],
                [g["w_down"].reshape(N_DEV, D_FF // N_DEV, D_MODEL),
                 g["w_up_t"].reshape(N_DEV, W_UP_SHARD, D_MODEL), conv_w])
        elif point == "bwd_merge":
            deps = reduce_pair(f"l{l}_ffn", carry)
        elif point == "bwd_out_grads":
            deps = reduce_begin(
                f"l{l}_out", [(l, "w_out"), (l, "w_oa"), (l, "w_ob")],
                [g["w_out"].reshape(N_DEV, D_MODEL // N_DEV, D_MODEL),
                 _disassemble((g["w_oa"],), LANES, _w_o_moves(), name=f"l{l}_split_dw_oa"),
                 _disassemble((g["w_ob"],), LANES, _w_o_moves(), name=f"l{l}_split_dw_ob")])
        elif point == "bwd_att":
            deps = reduce_pair(f"l{l}_out", carry)
        elif point == "bwd_w_in_grad":
            deps = reduce_begin(f"l{l}_in", [(l, "w_in")], [g["w_in_t"].reshape(N_DEV, W_IN_SHARD, D_MODEL)])
        elif point == "bwd_dh":
            deps = reduce_pair(f"l{l}_in", carry)
        return deps

    loss_part, dx, grads, last_deps = _local_step(x.reshape(tokens, d_model), loss_target.reshape(tokens, d_model),
                                                  weights, sched, n_seq=n_seq, seq=seq)
    for g in grads:
        g["conv_b"] = jnp.concatenate([g["cb_g"], g["cb_v"]])
    after = [dx, *last_deps, *reduce_begin("small", ["small"], [_pack_small(grads, loss_part)])]
    for key in [f"l{l}_{part}" for l in reversed(range(DEPTH)) for part in ("ffn", "out", "in")][:-1]:
        after = reduce_end(key, after)
    after = reduce_end("l0_in", after + reduce_pair("small", after))
    reduce_end("small", after)

    G, delta, new_m, new_v = {}, {}, {}, {}
    for name, _ in SHARDED:
        outs = [jnp.swapaxes(o, 1, 2) for o in results[name]] if name in TRANSPOSED else results[name]
        G[name], delta[name], new_m[name], new_v[name] = outs
    small, loss = _unpack_small(_gather([results["small"]], ["blocks"], name="gather_small_grads")[0])
    G.update(small)
    names = [name for name, _ in REPLICATED]
    stepped = _adamw_small(*[[t[name] for name in names] for t in (W, G, M, V)], name="adamw_replicated")
    for name, stepped_one in zip(names, stepped):
        delta[name], new_m[name], new_v[name] = stepped_one
    return (loss, dx.reshape(n_seq, seq, d_model), *[G[n] for n in WEIGHT_ORDER], *[delta[n] for n in WEIGHT_ORDER],
            *[new_m[n] for n in WEIGHT_ORDER], *[new_v[n] for n in WEIGHT_ORDER])
```

```python
import math

import jax
import jax.numpy as jnp
from jax import lax
from jax.experimental import pallas as pl
from jax.experimental.pallas import tpu as pltpu

F32 = jnp.float32
BF16 = jnp.bfloat16
ACT_DTYPE = BF16
MESH = pl.DeviceIdType.MESH

DEPTH = 2
D_MODEL = 1024
N_Q_HEADS = 8
HEAD_DIM = 64
ATT_WIDTH = 512
KV_WIDTH = 128
BLOCK = 128
SGU_WIDTH = 512
SGU_GROUPS = 8
IN_WIDTH = 3840
D_FF = 2816
NORM_EPS = 1e-6
NEG_INF = -1e30
ATT_SCALE = HEAD_DIM ** -0.5
ALIBI_SLOPES = tuple(2.0 ** (-(h + 1)) for h in range(N_Q_HEADS))
ADAM_LR, ADAM_B1, ADAM_B2, ADAM_EPS, ADAM_WD, ADAM_STEP = 0.001, 0.9, 0.999, 1e-08, 0.01, 10
N_DEV = 8
N_CHIPS = 4

QKV_WIDTH = ATT_WIDTH + 2 * KV_WIDTH
COL_SUV, COL_GA, COL_GB, COL_QKV = 0, 1024, 2048, 3072
W_IN_ROTATE = (1, IN_WIDTH // QKV_WIDTH)

LANES = 128
SUBLANES = 8
VMEM_LIMIT_V7X = 56 * 1024 * 1024
GELU_C = math.sqrt(2.0 / math.pi)
GELU_K = 0.044715
ANY = pl.BlockSpec(memory_space=pl.ANY)


def _params(sem=None):
    return pltpu.CompilerParams(dimension_semantics=sem, vmem_limit_bytes=VMEM_LIMIT_V7X)


def _sigmoid(x):
    return 1.0 / (1.0 + jnp.exp(-x))


def _gelu(x):
    th = jnp.tanh(GELU_C * (x + GELU_K * x * x * x))
    return 0.5 * x * (1.0 + th)


def _gelu_and_grad(x):
    x2 = x * x
    th = jnp.tanh(GELU_C * (x + GELU_K * x2 * x))
    g = 0.5 * x * (1.0 + th)
    dg = 0.5 * (1.0 + th) + 0.5 * x * (1.0 - th * th) * (GELU_C * (1.0 + 3.0 * GELU_K * x2))
    return g, dg


def _dot(a, b, dims):
    return lax.dot_general(a, b, (dims, ((), ())), preferred_element_type=F32)


def _dot_nn(a, b):
    return _dot(a, b, ((1,), (0,)))


def _dot_nt(a, b):
    return _dot(a, b, ((1,), (1,)))


def _dot_tn(a, b):
    return _dot(a, b, ((0,), (0,)))


def _lo_mask(shape):
    return lax.broadcasted_iota(jnp.int32, shape, len(shape) - 1) < (LANES // 2)


def _half_sums(x, lo):
    s_lo = jnp.sum(jnp.where(lo, x, 0.0), axis=-1, keepdims=True)
    s_all = jnp.sum(x, axis=-1, keepdims=True)
    return jnp.where(lo, s_lo, s_all - s_lo)


def _dup_half(x, half, lo):
    r = pltpu.roll(x, LANES // 2, axis=1)
    return jnp.where(lo, x, r) if half == 0 else jnp.where(lo, r, x)


def _with_deps(body, n_in, deps):
    k = len(deps)
    if not k:
        return body, [], ()

    def skipping(*refs):
        return body(*refs[:n_in], *refs[n_in + k:])

    return skipping, [ANY] * k, tuple(deps)


MM_VMEM_BUDGET = 40 * 1024 * 1024
MM_MAX_TILE = 1408
MM_MAX_TK = 4096
MM_STEP_BYTES = 1 << 20


def _divisors(n, step, cap):
    return [d for d in range(step, min(n, cap) + 1, step) if n % d == 0] or [n]


def _mm_tiles(M, N, K, out_bytes, tm_divides, tn_divides):
    best = None
    for tm in _divisors(M, LANES, MM_MAX_TILE):
        for tn in _divisors(N, LANES, MM_MAX_TILE):
            if tm_divides % tm or tn_divides % tn:
                continue
            for tk in _divisors(K, 4 * LANES, MM_MAX_TK):
                vmem = 4 * (tm * tk + tk * tn) + 2 * tm * tn * out_bytes + (0 if tk == K else 4 * tm * tn)
                if vmem > MM_VMEM_BUDGET:
                    continue
                traffic = 2 * M * K * (N // tn) + 2 * K * N * (M // tm) + M * N * out_bytes
                cost = traffic + (K // tk - 1) * 8 * M * N + (M // tm) * (N // tn) * (K // tk) * MM_STEP_BYTES
                if best is None or cost < best[0]:
                    best = (cost, tm, tn, tk)
    assert best is not None, (M, N, K)
    return best[1:]


def _mm(a, b, *, mode, out_dtype, name, deps=(), b_rows=(0, None), rotate=None, out_rows=(0, None), out_prev=None):
    b_first, b_count = b_rows
    if mode == "nn":
        (M, K), N = a.shape, b.shape[1]
    elif mode == "nt":
        (M, K), N = a.shape, (b.shape[0] if b_count is None else b_count)
    else:
        (K, M), N = a.shape, b.shape[1]
    shift, period = rotate or (0, 1)
    assert period == 1 or mode == "nt"
    out_first, out_total = out_rows[0], (M if out_rows[1] is None else out_rows[1])
    tm, tn, tk = _mm_tiles(M, N, K, jnp.dtype(out_dtype).itemsize, math.gcd(M, out_first),
                           math.gcd(N // period, b_first if mode == "nt" else 0))
    gm, gn, gk = M // tm, N // tn, K // tk

    def turned(j):
        per = N // period // tn
        return ((j // per + shift) % period) * per + j % per if period > 1 else j

    if mode == "nn":
        a_spec = pl.BlockSpec((tm, tk), lambda i, j, k: (i, k))
        b_spec = pl.BlockSpec((tk, tn), lambda i, j, k: (k + b_first // tk, j))
        contract = ((1,), (0,))
    elif mode == "nt":
        a_spec = pl.BlockSpec((tm, tk), lambda i, j, k: (i, k))
        b_spec = pl.BlockSpec((tn, tk), lambda i, j, k: (turned(j) + b_first // tn, k))
        contract = ((1,), (1,))
    else:
        a_spec = pl.BlockSpec((tk, tm), lambda i, j, k: (k, i))
        b_spec = pl.BlockSpec((tk, tn), lambda i, j, k: (k, j))
        contract = ((0,), (0,))
    o_spec = pl.BlockSpec((tm, tn), lambda i, j, k: (i + out_first // tm, j))
    assert b_first % (tk if mode == "nn" else tn) == 0 and out_first % tm == 0, (name, tm, tn, tk)
    n_prev = 0 if out_prev is None else 1

    def body(a_ref, b_ref, *rest):
        o_ref = rest[n_prev]
        part = _dot(a_ref[...].astype(BF16), b_ref[...].astype(BF16), contract)
        if gk == 1:
            o_ref[...] = part.astype(out_dtype)
            return
        acc_ref = rest[n_prev + 1]
        k = pl.program_id(2)

        @pl.when(k == 0)
        def _():
            acc_ref[...] = part

        @pl.when(k > 0)
        def _():
            acc_ref[...] += part

        @pl.when(k == gk - 1)
        def _():
            o_ref[...] = acc_ref[...].astype(out_dtype)

    body, dep_specs, dep_args = _with_deps(body, 2 + n_prev, deps)
    return pl.pallas_call(
        body,
        name=name,
        grid=(gm, gn, gk),
        in_specs=[a_spec, b_spec] + [ANY] * n_prev + dep_specs,
        out_specs=o_spec,
        out_shape=jax.ShapeDtypeStruct((out_total, N), out_dtype),
        input_output_aliases={2: 0} if n_prev else {},
        scratch_shapes=[] if gk == 1 else [pltpu.VMEM((tm, tn), F32)],
        compiler_params=_params(("parallel", "parallel", "arbitrary")),
    )(a, b, *([out_prev] if n_prev else []), *dep_args)


def _mm_tn_parts(parts, at, b, *, name):
    K, N = b.shape
    n = len(parts)
    tm = math.gcd(*[p.shape[1] for p in parts], *at)
    tiles = [p.shape[1] // tm for p in parts]
    first = [sum(tiles[:p]) for p in range(n)]

    def mine(i, p):
        return jnp.logical_and(i >= first[p], i < first[p] + tiles[p])

    def out_tile(i):
        t = 0
        for p in range(n):
            t = jnp.where(mine(i, p), at[p] // tm + i - first[p], t)
        return t

    def body(*refs):
        a_refs, b_ref, o_ref = refs[:n], refs[n], refs[n + 1]
        for p in range(n):
            @pl.when(mine(pl.program_id(0), p))
            def _(p=p):
                o_ref[...] = _dot_tn(a_refs[p][...], b_ref[...])

    return pl.pallas_call(
        body, name=name, grid=(sum(tiles),),
        in_specs=[pl.BlockSpec((K, tm), lambda i, p=p: (0, jnp.clip(i - first[p], 0, tiles[p] - 1))) for p in range(n)]
        + [pl.BlockSpec((K, N), lambda i: (0, 0), pipeline_mode=pl.Buffered(1))],
        out_specs=pl.BlockSpec((tm, N), lambda i: (out_tile(i), 0)),
        out_shape=jax.ShapeDtypeStruct((sum(p.shape[1] for p in parts), N), F32),
        compiler_params=_params(("arbitrary",)),
    )(*parts, b)


def _mm_rows(a, b, *, mode, fn, out_dtypes, rows=(), vecs=(), reduce=False, name, deps=(), b_rows=(0, None), a_at=None):
    parts = a if a_at is not None else (a,)
    starts = a_at if a_at is not None else (0,)
    n_parts = len(parts)
    M, K = parts[0].shape[0], sum(p.shape[1] for p in parts)
    b_first, b_count = b_rows[0], (b.shape[0] if b_rows[1] is None else b_rows[1])
    N = b.shape[1] if mode == "nn" else b_count
    contract = ((1,), (0,)) if mode == "nn" else ((1,), (1,))
    n_rows, n_vecs, n_out = len(rows), len(vecs), len(out_dtypes)
    out_bytes = sum(jnp.dtype(d).itemsize for d in out_dtypes)
    tm = max(t for t in _divisors(M, LANES, MM_MAX_TILE)
             if 4 * t * K + 2 * K * N + 2 * t * N * (4 * n_rows + out_bytes) <= MM_VMEM_BUDGET)
    assert b_first % b_count == 0 and (a_at is None or mode == "nn")

    def body(*refs):
        a_refs, b_ref, rest = refs[:n_parts], refs[n_parts], refs[n_parts + 1:]
        row_refs, vec_refs = rest[:n_rows], rest[n_rows:n_rows + n_vecs]
        out_refs = rest[n_rows + n_vecs:]
        if a_at is None:
            acc = _dot(a_refs[0][...], b_ref[...], contract)
        else:
            acc = sum(_dot(r[...], b_ref[at:at + r.shape[1], :], contract) for r, at in zip(a_refs, starts))
        res = fn(acc, *[r[...] for r in row_refs], *[v[...] for v in vec_refs])
        for o_ref, val in zip(out_refs[:n_out], res):
            o_ref[...] = val.astype(o_ref.dtype)
        if reduce:
            @pl.when(pl.program_id(0) == 0)
            def _():
                out_refs[n_out][...] = res[n_out]

            @pl.when(pl.program_id(0) > 0)
            def _():
                out_refs[n_out][...] += res[n_out]

    row = pl.BlockSpec((tm, N), lambda i: (i, 0))
    vec = pl.BlockSpec((1, N), lambda i: (0, 0))
    body, dep_specs, dep_args = _with_deps(body, n_parts + 1 + n_rows + n_vecs, deps)
    return pl.pallas_call(
        body, name=name, grid=(M // tm,),
        in_specs=[pl.BlockSpec((tm, p.shape[1]), lambda i: (i, 0)) for p in parts]
        + [pl.BlockSpec((b_count, b.shape[1]), lambda i: (b_first // b_count, 0), pipeline_mode=pl.Buffered(1))]
        + [row] * n_rows + [vec] * n_vecs + dep_specs,
        out_specs=[row] * n_out + [vec] * reduce,
        out_shape=[jax.ShapeDtypeStruct((M, N), d) for d in out_dtypes] + [jax.ShapeDtypeStruct((1, N), F32)] * reduce,
        compiler_params=_params(("arbitrary",)),
    )(*parts, b, *rows, *[v.reshape(1, N) for v in vecs], *dep_args)


def _rms(x, gain):
    return x * lax.rsqrt(jnp.mean(x * x, axis=-1, keepdims=True) + NORM_EPS) * gain


def _residual_then_norm(acc, x, gain):
    x_out = x + acc
    return x_out, _rms(x_out, gain)


def _residual_then_loss(acc, x, target):
    err = (x + acc) - target
    dy = err * (1.0 / D_MODEL)
    return dy, dy, jnp.sum(err * err, axis=0, keepdims=True) * (0.5 / D_MODEL)


def _rms_bwd_rows(dh, x, dres, gain):
    r = lax.rsqrt(jnp.mean(x * x, axis=-1, keepdims=True) + NORM_EPS)
    xh = x * r
    dxh = dh * gain
    dx = dres + r * (dxh - xh * jnp.mean(dxh * xh, axis=-1, keepdims=True))
    return dx, dx, jnp.sum(dh * xh, axis=0, keepdims=True)


def _rms_fwd(x, gain, *, name, tm=512, deps=()):
    T, D = x.shape

    def body(x_ref, g_ref, h_ref):
        xv = x_ref[...]
        r = lax.rsqrt(jnp.mean(xv * xv, axis=-1, keepdims=True) + NORM_EPS)
        h_ref[...] = (xv * r * g_ref[...]).astype(BF16)

    body, dep_specs, dep_args = _with_deps(body, 2, deps)
    return pl.pallas_call(
        body, name=name, grid=(T // tm,),
        in_specs=[pl.BlockSpec((tm, D), lambda i: (i, 0)), pl.BlockSpec((1, D), lambda i: (0, 0))] + dep_specs,
        out_specs=pl.BlockSpec((tm, D), lambda i: (i, 0)),
        out_shape=jax.ShapeDtypeStruct((T, D), BF16),
        compiler_params=_params(("parallel",)),
    )(x, gain.reshape(1, D), *dep_args)


def _head_norm(x, gain2, lo):
    ms = _half_sums(x * x, lo) * (1.0 / HEAD_DIM)
    r = lax.rsqrt(ms + NORM_EPS)
    xh = x * r
    return xh * gain2, xh, r


def _head_norm_bwd(xh, r, gain2, dy, lo):
    dxh = dy * gain2
    dx = r * (dxh - xh * (_half_sums(dxh * xh, lo) * (1.0 / HEAD_DIM)))
    return dx, dy * xh


Q_GROUP = N_Q_HEADS // 2
GROUP_ROWS = Q_GROUP * BLOCK
ATT_SCRATCH = (pltpu.VMEM((2, 2, GROUP_ROWS, BLOCK), F32), pltpu.VMEM((2, GROUP_ROWS, 1), F32))


def _att_consts(sink_ref, bias_ref, sinkcol_ref):
    row = lax.broadcasted_iota(jnp.int32, (GROUP_ROWS, BLOCK), 0)
    kj = lax.broadcasted_iota(jnp.int32, (GROUP_ROWS, BLOCK), 1)
    head = row // BLOCK
    head_col = lax.broadcasted_iota(jnp.int32, (GROUP_ROWS, 1), 0) // BLOCK
    d_cur = (row % BLOCK) - kj
    d_prev = d_cur + BLOCK
    for kv in range(2):
        slope = jnp.zeros((GROUP_ROWS, BLOCK), F32)
        sink = jnp.zeros((GROUP_ROWS, 1), F32)
        for r in range(Q_GROUP):
            slope = jnp.where(head == r, ALIBI_SLOPES[Q_GROUP * kv + r], slope)
            sink = jnp.where(head_col == r, sink_ref[Q_GROUP * kv + r], sink)
        bias_ref[kv, 0] = jnp.where(d_cur >= 0, -slope * d_cur.astype(F32), NEG_INF)
        bias_ref[kv, 1] = jnp.where(d_prev < BLOCK, -slope * d_prev.astype(F32), NEG_INF)
        sinkcol_ref[kv] = sink


def _stack_heads(t0, t1, lo):
    z = jnp.zeros_like(t0)
    return jnp.concatenate([jnp.where(lo, t0, z), jnp.where(lo, z, t0), jnp.where(lo, t1, z), jnp.where(lo, z, t1)], axis=0)


def _unstack_heads(x4, lo):
    return (jnp.where(lo, x4[0:BLOCK], x4[BLOCK:2 * BLOCK]), jnp.where(lo, x4[2 * BLOCK:3 * BLOCK], x4[3 * BLOCK:]))


def _att_probs(q4, k2c, k2p, bias_c, bias_p, sink, has_prev):
    s_c = _dot_nt(q4, k2c) * ATT_SCALE + bias_c
    s_p = jnp.where(has_prev, _dot_nt(q4, k2p) * ATT_SCALE + bias_p, NEG_INF)
    m = jnp.maximum(jnp.max(jnp.maximum(s_c, s_p), axis=-1, keepdims=True), sink)
    e_c = jnp.exp(s_c - m)
    e_p = jnp.exp(s_p - m)
    e_s = jnp.exp(sink - m)
    inv = 1.0 / (jnp.sum(e_c + e_p, axis=-1, keepdims=True) + e_s)
    return e_c * inv, e_p * inv, e_s * inv


def _attention_fwd(proj, q_gain, k_gain, sinks, *, n_seq, seq, name):
    T = n_seq * seq
    nb = seq // BLOCK
    qcol, kvcol = COL_QKV // ATT_WIDTH, (COL_QKV + ATT_WIDTH) // (2 * KV_WIDTH)

    def body(q_ref, kv_ref, qg_ref, kg_ref, sink_ref, y_ref, bias_ref, sinkcol_ref):
        lo = _lo_mask((BLOCK, LANES))
        qg, kg = qg_ref[...], kg_ref[...]
        _att_consts(sink_ref, bias_ref, sinkcol_ref)

        def block(i, carry):
            r0 = pl.multiple_of(i * BLOCK, BLOCK)
            rp = pl.multiple_of(jnp.maximum(i - 1, 0) * BLOCK, BLOCK)
            has_prev = i > 0
            kn_c = _head_norm(kv_ref[pl.ds(r0, BLOCK), 0:KV_WIDTH].astype(F32), kg, lo)[0].astype(BF16)
            kn_p = _head_norm(kv_ref[pl.ds(rp, BLOCK), 0:KV_WIDTH].astype(F32), kg, lo)[0].astype(BF16)
            v_c = kv_ref[pl.ds(r0, BLOCK), KV_WIDTH:2 * KV_WIDTH].astype(BF16)
            v_p = kv_ref[pl.ds(rp, BLOCK), KV_WIDTH:2 * KV_WIDTH].astype(BF16)
            for kv in range(2):
                k2c, k2p = _dup_half(kn_c, kv, lo), _dup_half(kn_p, kv, lo)
                v2c, v2p = _dup_half(v_c, kv, lo), _dup_half(v_p, kv, lo)
                cols = [slice((2 * kv + t) * LANES, (2 * kv + t + 1) * LANES) for t in range(2)]
                qn = [_head_norm(q_ref[pl.ds(r0, BLOCK), c].astype(F32), qg, lo)[0] for c in cols]
                q4 = _stack_heads(qn[0], qn[1], lo).astype(BF16)
                p_c, p_p, _ = _att_probs(q4, k2c, k2p, bias_ref[kv, 0], bias_ref[kv, 1], sinkcol_ref[kv], has_prev)
                o4 = _dot_nn(p_c.astype(BF16), v2c) + _dot_nn(p_p.astype(BF16), v2p)
                for c, out in zip(cols, _unstack_heads(o4, lo)):
                    y_ref[pl.ds(r0, BLOCK), c] = out.astype(BF16)
            return carry

        lax.fori_loop(0, nb, block, 0)

    vec = pl.BlockSpec((1, LANES), lambda b: (0, 0))
    return pl.pallas_call(
        body, name=name, grid=(n_seq,),
        in_specs=[pl.BlockSpec((seq, ATT_WIDTH), lambda b: (b, qcol)),
                  pl.BlockSpec((seq, 2 * KV_WIDTH), lambda b: (b, kvcol)),
                  vec, vec, pl.BlockSpec(memory_space=pltpu.SMEM)],
        out_specs=pl.BlockSpec((seq, ATT_WIDTH), lambda b: (b, 0)),
        out_shape=jax.ShapeDtypeStruct((T, ATT_WIDTH), BF16),
        scratch_shapes=list(ATT_SCRATCH),
        compiler_params=_params(("parallel",)),
    )(proj, proj, jnp.tile(q_gain, 2).reshape(1, LANES), jnp.tile(k_gain, 2).reshape(1, LANES), sinks)


def _attention_bwd(proj, dy, q_gain, k_gain, sinks, *, n_seq, seq, name, deps=()):
    T = n_seq * seq
    nb = seq // BLOCK
    qcol, kvcol = COL_QKV // ATT_WIDTH, (COL_QKV + ATT_WIDTH) // (2 * KV_WIDTH)

    def body(q_ref, kv_ref, dy_ref, qg_ref, kg_ref, sink_ref, dqkv_ref, dqg_ref, dkg_ref, dsink_ref,
             dkn_acc, dv_acc, qg_acc, kg_acc, sink_acc, bias_ref, sinkcol_ref):
        lo = _lo_mask((BLOCK, LANES))
        qg, kg = qg_ref[...], kg_ref[...]
        _att_consts(sink_ref, bias_ref, sinkcol_ref)
        first = pl.program_id(0) == 0

        @pl.when(first)
        def _():
            qg_acc[...] = jnp.zeros_like(qg_acc)
            kg_acc[...] = jnp.zeros_like(kg_acc)
            sink_acc[...] = jnp.zeros_like(sink_acc)

        dkn_acc[...] = jnp.zeros_like(dkn_acc)
        dv_acc[...] = jnp.zeros_like(dv_acc)

        def block(i, carry):
            r0 = pl.multiple_of(i * BLOCK, BLOCK)
            rp = pl.multiple_of(jnp.maximum(i - 1, 0) * BLOCK, BLOCK)
            has_prev = i > 0
            kn_c = _head_norm(kv_ref[pl.ds(r0, BLOCK), 0:KV_WIDTH].astype(F32), kg, lo)[0].astype(BF16)
            kn_p = _head_norm(kv_ref[pl.ds(rp, BLOCK), 0:KV_WIDTH].astype(F32), kg, lo)[0].astype(BF16)
            v_c = kv_ref[pl.ds(r0, BLOCK), KV_WIDTH:2 * KV_WIDTH].astype(BF16)
            v_p = kv_ref[pl.ds(rp, BLOCK), KV_WIDTH:2 * KV_WIDTH].astype(BF16)
            dk_c, dk_p, dv_c, dv_p = [], [], [], []
            for kv in range(2):
                k2c, k2p = _dup_half(kn_c, kv, lo), _dup_half(kn_p, kv, lo)
                v2c, v2p = _dup_half(v_c, kv, lo), _dup_half(v_p, kv, lo)
                cols = [slice((2 * kv + t) * LANES, (2 * kv + t + 1) * LANES) for t in range(2)]
                normed = [_head_norm(q_ref[pl.ds(r0, BLOCK), c].astype(F32), qg, lo) for c in cols]
                q4 = _stack_heads(normed[0][0], normed[1][0], lo).astype(BF16)
                do4 = _stack_heads(dy_ref[pl.ds(r0, BLOCK), cols[0]], dy_ref[pl.ds(r0, BLOCK), cols[1]], lo)
                p_c, p_p, p_s = _att_probs(q4, k2c, k2p, bias_ref[kv, 0], bias_ref[kv, 1], sinkcol_ref[kv], has_prev)
                dp_c = _dot_nt(do4, v2c)
                dp_p = _dot_nt(do4, v2p)
                delta = jnp.sum(p_c * dp_c + p_p * dp_p, axis=-1, keepdims=True)
                ds_c = (p_c * (dp_c - delta)).astype(BF16)
                ds_p = (p_p * (dp_p - delta)).astype(BF16)
                sink_acc[kv] += -(p_s * delta)
                dq4 = (_dot_nn(ds_c, k2c) + _dot_nn(ds_p, k2p)) * ATT_SCALE
                for c, (_, qh, qr), dqn in zip(cols, normed, _unstack_heads(dq4, lo)):
                    dq, dg = _head_norm_bwd(qh, qr, qg, dqn, lo)
                    dqkv_ref[pl.ds(r0, BLOCK), c] = dq.astype(BF16)
                    qg_acc[...] += dg
                dk_c.append(_dot_tn(ds_c, q4))
                dk_p.append(_dot_tn(ds_p, q4))
                dv_c.append(_dot_tn(p_c.astype(BF16), do4))
                dv_p.append(_dot_tn(p_p.astype(BF16), do4))

            def fold(parts):
                a = parts[0] + pltpu.roll(parts[0], LANES // 2, axis=1)
                b = parts[1] + pltpu.roll(parts[1], LANES // 2, axis=1)
                return jnp.where(lo, a, b)

            dkn_acc[pl.ds(r0, BLOCK), :] += fold(dk_c) * ATT_SCALE
            dkn_acc[pl.ds(rp, BLOCK), :] += fold(dk_p) * ATT_SCALE
            dv_acc[pl.ds(r0, BLOCK), :] += fold(dv_c)
            dv_acc[pl.ds(rp, BLOCK), :] += fold(dv_p)
            return carry

        lax.fori_loop(0, nb, block, 0)

        def finish(i, carry):
            r0 = pl.multiple_of(i * BLOCK, BLOCK)
            _, kh, kr = _head_norm(kv_ref[pl.ds(r0, BLOCK), 0:KV_WIDTH].astype(F32), kg, lo)
            dk, dg = _head_norm_bwd(kh, kr, kg, dkn_acc[pl.ds(r0, BLOCK), :], lo)
            dqkv_ref[pl.ds(r0, BLOCK), ATT_WIDTH:ATT_WIDTH + KV_WIDTH] = dk.astype(BF16)
            dqkv_ref[pl.ds(r0, BLOCK), ATT_WIDTH + KV_WIDTH:QKV_WIDTH] = dv_acc[pl.ds(r0, BLOCK), :].astype(BF16)
            kg_acc[...] += dg
            return carry

        lax.fori_loop(0, nb, finish, 0)

        @pl.when(pl.program_id(0) == n_seq - 1)
        def _():
            dqg_ref[...] = jnp.sum(qg_acc[...], axis=0, keepdims=True)
            dkg_ref[...] = jnp.sum(kg_acc[...], axis=0, keepdims=True)
            lane = lax.broadcasted_iota(jnp.int32, (1, LANES), 1)
            dsink = jnp.zeros((1, LANES), F32)
            for kv in range(2):
                for r in range(Q_GROUP):
                    total = jnp.sum(sink_acc[kv, r * BLOCK:(r + 1) * BLOCK, :], axis=0, keepdims=True)
                    dsink = jnp.where(lane == Q_GROUP * kv + r, total, dsink)
            dsink_ref[...] = dsink

    vec = pl.BlockSpec((1, LANES), lambda b: (0, 0))
    acc = pltpu.VMEM((BLOCK, LANES), F32)
    body, dep_specs, dep_args = _with_deps(body, 6, deps)
    dqkv, dqg, dkg, dsink = pl.pallas_call(
        body, name=name, grid=(n_seq,),
        in_specs=[pl.BlockSpec((seq, ATT_WIDTH), lambda b: (b, qcol)),
                  pl.BlockSpec((seq, 2 * KV_WIDTH), lambda b: (b, kvcol)),
                  pl.BlockSpec((seq, ATT_WIDTH), lambda b: (b, 0)),
                  vec, vec, pl.BlockSpec(memory_space=pltpu.SMEM)] + dep_specs,
        out_specs=[pl.BlockSpec((seq, QKV_WIDTH), lambda b: (b, 0)), vec, vec, vec],
        out_shape=[jax.ShapeDtypeStruct((T, QKV_WIDTH), BF16)] + [jax.ShapeDtypeStruct((1, LANES), F32)] * 3,
        scratch_shapes=[pltpu.VMEM((seq, KV_WIDTH), F32), pltpu.VMEM((seq, KV_WIDTH), F32), acc, acc,
                        pltpu.VMEM((2, GROUP_ROWS, 1), F32), *ATT_SCRATCH],
        compiler_params=_params(("arbitrary",)),
    )(proj, proj, dy, jnp.tile(q_gain, 2).reshape(1, LANES), jnp.tile(k_gain, 2).reshape(1, LANES), sinks, *dep_args)
    half = LANES // 2
    return dqkv, dqg[0, :half] + dqg[0, half:], dkg[0, :half] + dkg[0, half:], dsink[0, :N_Q_HEADS]


def _sgu_weights(w_ref):
    r = lax.broadcasted_iota(jnp.int32, (BLOCK, BLOCK), 0)
    c = lax.broadcasted_iota(jnp.int32, (BLOCK, BLOCK), 1)
    return [jnp.where(r >= c, w_ref[g], 0.0).astype(BF16) for g in range(SGU_GROUPS)]


def _sgu_fwd(proj, gain, w_s, bias_full, *, n_seq, seq, name):
    T = n_seq * seq
    nc = seq // BLOCK

    def body(suv_ref, g_ref, w_ref, b_ref, y_ref):
        lo = _lo_mask((BLOCK, LANES))
        wm = _sgu_weights(w_ref)
        gain_v = g_ref[...]

        def chunk(c, carry):
            r0 = pl.multiple_of(c * BLOCK, BLOCK)
            gv = _gelu(suv_ref[pl.ds(r0, BLOCK), SGU_WIDTH:2 * SGU_WIDTH].astype(F32))
            r = lax.rsqrt(jnp.mean(gv * gv, axis=-1, keepdims=True) + NORM_EPS)
            vn = (gv * r * gain_v).astype(BF16)
            for p in range(SGU_WIDTH // LANES):
                cols = slice(p * LANES, (p + 1) * LANES)
                vp = vn[:, cols]
                mixed = jnp.where(lo, _dot_nn(wm[2 * p], vp), _dot_nn(wm[2 * p + 1], vp)) + b_ref[:, cols]
                u = _gelu(suv_ref[pl.ds(r0, BLOCK), cols].astype(F32))
                y_ref[pl.ds(r0, BLOCK), cols] = (u * mixed).astype(BF16)
            return carry

        lax.fori_loop(0, nc, chunk, 0)

    return pl.pallas_call(
        body, name=name, grid=(n_seq,),
        in_specs=[pl.BlockSpec((seq, 2 * SGU_WIDTH), lambda b: (b, COL_SUV // (2 * SGU_WIDTH))),
                  pl.BlockSpec((1, SGU_WIDTH), lambda b: (0, 0)),
                  pl.BlockSpec((SGU_GROUPS, BLOCK, BLOCK), lambda b: (0, 0, 0)),
                  pl.BlockSpec((BLOCK, SGU_WIDTH), lambda b: (0, 0))],
        out_specs=pl.BlockSpec((seq, SGU_WIDTH), lambda b: (b, 0)),
        out_shape=jax.ShapeDtypeStruct((T, SGU_WIDTH), BF16),
        compiler_params=_params(("parallel",)),
    )(proj, gain.reshape(1, SGU_WIDTH), w_s, bias_full)


def _sgu_bwd(proj, dy, gain, w_s, bias_full, *, n_seq, seq, name, deps=()):
    T = n_seq * seq
    nc = seq // BLOCK
    n_tiles = SGU_WIDTH // LANES

    def body(suv_ref, dy_ref, g_ref, w_ref, b_ref, dsuv_ref, dg_ref, dw_ref, db_ref, dg_acc, dw_acc, db_acc):
        lo = _lo_mask((BLOCK, LANES))
        hi = jnp.logical_not(lo)
        wm = _sgu_weights(w_ref)
        wmt = [jnp.where(lax.broadcasted_iota(jnp.int32, (BLOCK, BLOCK), 1) >= lax.broadcasted_iota(jnp.int32, (BLOCK, BLOCK), 0),
                         w_ref[g].T, 0.0).astype(BF16) for g in range(SGU_GROUPS)]
        gain_v = g_ref[...]

        @pl.when(pl.program_id(0) == 0)
        def _():
            dg_acc[...] = jnp.zeros_like(dg_acc)
            dw_acc[...] = jnp.zeros_like(dw_acc)
            db_acc[...] = jnp.zeros_like(db_acc)

        def chunk(c, carry):
            r0 = pl.multiple_of(c * BLOCK, BLOCK)
            gv, dgelu_v = _gelu_and_grad(suv_ref[pl.ds(r0, BLOCK), SGU_WIDTH:2 * SGU_WIDTH].astype(F32))
            r = lax.rsqrt(jnp.mean(gv * gv, axis=-1, keepdims=True) + NORM_EPS)
            vh = gv * r
            vn = (vh * gain_v).astype(BF16)
            dvn_tiles = []
            for p in range(n_tiles):
                cols = slice(p * LANES, (p + 1) * LANES)
                vp = vn[:, cols]
                mixed = jnp.where(lo, _dot_nn(wm[2 * p], vp), _dot_nn(wm[2 * p + 1], vp)) + b_ref[:, cols]
                u, dgelu_u = _gelu_and_grad(suv_ref[pl.ds(r0, BLOCK), cols].astype(F32))
                dyv = dy_ref[pl.ds(r0, BLOCK), cols]
                dsuv_ref[pl.ds(r0, BLOCK), cols] = (dyv * mixed * dgelu_u).astype(BF16)
                dm = dyv * u
                db_acc[:, cols] += dm
                dm_bf = dm.astype(BF16)
                dvn_tiles.append(jnp.where(lo, _dot_nn(wmt[2 * p], dm_bf), _dot_nn(wmt[2 * p + 1], dm_bf)))
                dw_acc[2 * p] += _dot_nt(jnp.where(lo, dm, 0.0).astype(BF16), vp)
                dw_acc[2 * p + 1] += _dot_nt(jnp.where(hi, dm, 0.0).astype(BF16), vp)
            dvn = jnp.concatenate(dvn_tiles, axis=1)
            dg_acc[...] += dvn * vh
            dvh = dvn * gain_v
            dgv = r * (dvh - vh * jnp.mean(dvh * vh, axis=-1, keepdims=True))
            dsuv_ref[pl.ds(r0, BLOCK), SGU_WIDTH:2 * SGU_WIDTH] = (dgv * dgelu_v).astype(BF16)
            return carry

        lax.fori_loop(0, nc, chunk, 0)

        @pl.when(pl.program_id(0) == n_seq - 1)
        def _():
            dg_ref[...] = jnp.sum(dg_acc[...], axis=0, keepdims=True)
            r = lax.broadcasted_iota(jnp.int32, (BLOCK, BLOCK), 0)
            c = lax.broadcasted_iota(jnp.int32, (BLOCK, BLOCK), 1)
            for g in range(SGU_GROUPS):
                dw_ref[g] = jnp.where(r >= c, dw_acc[g], 0.0)
            lane = lax.broadcasted_iota(jnp.int32, (BLOCK, LANES), 1)
            out = jnp.zeros((BLOCK, LANES), F32)
            for p in range(n_tiles):
                tile = db_acc[:, p * LANES:(p + 1) * LANES]
                s_lo = jnp.sum(jnp.where(lo, tile, 0.0), axis=-1, keepdims=True)
                s_hi = jnp.sum(jnp.where(hi, tile, 0.0), axis=-1, keepdims=True)
                out = jnp.where(lane == 2 * p, s_lo, out)
                out = jnp.where(lane == 2 * p + 1, s_hi, out)
            db_ref[...] = out

    body, dep_specs, dep_args = _with_deps(body, 5, deps)
    dsuv, dg, dw, db = pl.pallas_call(
        body, name=name, grid=(n_seq,),
        in_specs=[pl.BlockSpec((seq, 2 * SGU_WIDTH), lambda b: (b, COL_SUV // (2 * SGU_WIDTH))),
                  pl.BlockSpec((seq, SGU_WIDTH), lambda b: (b, 0)),
                  pl.BlockSpec((1, SGU_WIDTH), lambda b: (0, 0)),
                  pl.BlockSpec((SGU_GROUPS, BLOCK, BLOCK), lambda b: (0, 0, 0)),
                  pl.BlockSpec((BLOCK, SGU_WIDTH), lambda b: (0, 0))] + dep_specs,
        out_specs=[pl.BlockSpec((seq, 2 * SGU_WIDTH), lambda b: (b, 0)),
                   pl.BlockSpec((1, SGU_WIDTH), lambda b: (0, 0)),
                   pl.BlockSpec((SGU_GROUPS, BLOCK, BLOCK), lambda b: (0, 0, 0)),
                   pl.BlockSpec((BLOCK, LANES), lambda b: (0, 0))],
        out_shape=[jax.ShapeDtypeStruct((T, 2 * SGU_WIDTH), BF16), jax.ShapeDtypeStruct((1, SGU_WIDTH), F32),
                   jax.ShapeDtypeStruct((SGU_GROUPS, BLOCK, BLOCK), F32), jax.ShapeDtypeStruct((BLOCK, LANES), F32)],
        scratch_shapes=[pltpu.VMEM((BLOCK, SGU_WIDTH), F32), pltpu.VMEM((SGU_GROUPS, BLOCK, BLOCK), F32),
                        pltpu.VMEM((BLOCK, SGU_WIDTH), F32)],
        compiler_params=_params(("arbitrary",)),
    )(proj, dy, gain.reshape(1, SGU_WIDTH), w_s, bias_full, *dep_args)
    return dsuv, dg.reshape(SGU_WIDTH), dw, db[:, :SGU_GROUPS].T


def _merge_fwd(y_att, y_sgu, w_oa, w_ob, proj, *, name, tm=1024, tn=512, deps=()):
    T = y_att.shape[0]

    def body(ya_ref, ys_ref, wa_ref, wb_ref, ga_ref, gb_ref, o_ref):
        pa = _dot_nn(ya_ref[...], wa_ref[...])
        pb = _dot_nn(ys_ref[...], wb_ref[...])
        o_ref[...] = (_sigmoid(ga_ref[...].astype(F32)) * pa + _sigmoid(gb_ref[...].astype(F32)) * pb).astype(BF16)

    act = pl.BlockSpec((tm, ATT_WIDTH), lambda i, j: (i, 0))
    wgt = pl.BlockSpec((ATT_WIDTH, tn), lambda i, j: (0, j))
    body, dep_specs, dep_args = _with_deps(body, 6, deps)
    return pl.pallas_call(
        body, name=name, grid=(T // tm, D_MODEL // tn),
        in_specs=[act, act, wgt, wgt,
                  pl.BlockSpec((tm, tn), lambda i, j: (i, j + COL_GA // tn)),
                  pl.BlockSpec((tm, tn), lambda i, j: (i, j + COL_GB // tn))] + dep_specs,
        out_specs=pl.BlockSpec((tm, tn), lambda i, j: (i, j)),
        out_shape=jax.ShapeDtypeStruct((T, D_MODEL), BF16),
        compiler_params=_params(("parallel", "parallel")),
    )(y_att, y_sgu, w_oa, w_ob, proj, proj, *dep_args)


def _merge_bwd(dx1_bf, w_out, y_att, y_sgu, w_oa, w_ob, proj, *, name, tm=1024, tn=512):
    T = y_att.shape[0]

    def body(dx_ref, wo_ref, ya_ref, ys_ref, wa_ref, wb_ref, ga_ref, gb_ref, dpa_ref, dpb_ref, dga_ref, dgb_ref):
        dm = _dot_nt(dx_ref[...], wo_ref[...])
        pa = _dot_nn(ya_ref[...], wa_ref[...])
        pb = _dot_nn(ys_ref[...], wb_ref[...])
        sa = _sigmoid(ga_ref[...].astype(F32))
        sb = _sigmoid(gb_ref[...].astype(F32))
        dpa_ref[...] = (dm * sa).astype(BF16)
        dpb_ref[...] = (dm * sb).astype(BF16)
        dga_ref[...] = (dm * pa * sa * (1.0 - sa)).astype(BF16)
        dgb_ref[...] = (dm * pb * sb * (1.0 - sb)).astype(BF16)

    act = pl.BlockSpec((tm, ATT_WIDTH), lambda i, j: (i, 0))
    wgt = pl.BlockSpec((ATT_WIDTH, tn), lambda i, j: (0, j))
    out = pl.BlockSpec((tm, tn), lambda i, j: (i, j))
    return pl.pallas_call(
        body, name=name, grid=(T // tm, D_MODEL // tn),
        in_specs=[pl.BlockSpec((tm, D_MODEL), lambda i, j: (i, 0)),
                  pl.BlockSpec((tn, D_MODEL), lambda i, j: (j, 0)),
                  act, act, wgt, wgt,
                  pl.BlockSpec((tm, tn), lambda i, j: (i, j + COL_GA // tn)),
                  pl.BlockSpec((tm, tn), lambda i, j: (i, j + COL_GB // tn))],
        out_specs=[out] * 4,
        out_shape=[jax.ShapeDtypeStruct((T, D_MODEL), BF16)] * 4,
        compiler_params=_params(("parallel", "parallel")),
    )(dx1_bf, w_out, y_att, y_sgu, w_oa, w_ob, proj, proj)


CONV_ROWS = 256
CONV_TN = 256


def _shift_rows(cur, prev8, k):
    rolled = pltpu.roll(cur, k, axis=0)
    head = jnp.where(lax.broadcasted_iota(jnp.int32, prev8.shape, 0) < k, pltpu.roll(prev8, k, axis=0), rolled[:SUBLANES])
    return jnp.concatenate([head, rolled[SUBLANES:]], axis=0)


def _shift_rows_up(cur, next8, k):
    n = cur.shape[0]
    rolled = pltpu.roll(cur, n - k, axis=0)
    tail = jnp.where(lax.broadcasted_iota(jnp.int32, next8.shape, 0) >= SUBLANES - k,
                     pltpu.roll(next8, SUBLANES - k, axis=0), rolled[n - SUBLANES:])
    return jnp.concatenate([rolled[:n - SUBLANES], tail], axis=0)


def _up_conv_fwd(h2, w_up_t, cw_g, cw_v, cb_g, cb_v, *, n_seq, seq, name, deps=()):
    T = n_seq * seq
    tn, rows = CONV_TN, CONV_ROWS

    def body(h_ref, ug_ref, uv_ref, wg_ref, wv_ref, bg_ref, bv_ref, a_ref, zg_ref, zv_ref, cg_ref, cv_ref):
        def conv(cur, prev8, w_ref, b_ref):
            z1 = _shift_rows(cur, prev8, 1)
            z2 = _shift_rows(cur, prev8, 2)
            return b_ref[...] + w_ref[0:1, :] * z2 + w_ref[1:2, :] * z1 + w_ref[2:3, :] * cur

        start = jnp.zeros((SUBLANES, tn), F32)
        prev = (start, start)
        for s in range(seq // rows):
            r = pl.ds(s * rows, rows)
            h = h_ref[r, :]
            zg = _dot_nt(h, ug_ref[...])
            zv = _dot_nt(h, uv_ref[...])
            zg_ref[r, :] = zg.astype(ACT_DTYPE)
            zv_ref[r, :] = zv.astype(ACT_DTYPE)
            g = conv(zg, prev[0], wg_ref, bg_ref)
            v = conv(zv, prev[1], wv_ref, bv_ref)
            a_ref[r, :] = (g * _sigmoid(g) * v).astype(BF16)
            cg_ref[r, :] = g.astype(ACT_DTYPE)
            cv_ref[r, :] = v.astype(ACT_DTYPE)
            prev = (zg[rows - SUBLANES:], zv[rows - SUBLANES:])

    zs = pl.BlockSpec((seq, tn), lambda b, j: (b, j))
    ws = pl.BlockSpec((3, tn), lambda b, j: (0, j))
    bs = pl.BlockSpec((1, tn), lambda b, j: (0, j))
    body, dep_specs, dep_args = _with_deps(body, 7, deps)
    return pl.pallas_call(
        body, name=name, grid=(n_seq, D_FF // tn),
        in_specs=[pl.BlockSpec((seq, D_MODEL), lambda b, j: (b, 0)),
                  pl.BlockSpec((tn, D_MODEL), lambda b, j: (j, 0)),
                  pl.BlockSpec((tn, D_MODEL), lambda b, j: (j + D_FF // tn, 0)), ws, ws, bs, bs] + dep_specs,
        out_specs=[zs] * 5,
        out_shape=[jax.ShapeDtypeStruct((T, D_FF), BF16)] + [jax.ShapeDtypeStruct((T, D_FF), ACT_DTYPE)] * 4,
        compiler_params=_params(("parallel", "parallel")),
    )(h2, w_up_t, w_up_t, cw_g, cw_v, cb_g.reshape(1, D_FF), cb_v.reshape(1, D_FF), *dep_args)


def _conv_bwd(z_g, z_v, c_g, c_v, dx2_bf, w_down, cw_g, cw_v, *, n_seq, seq, name):
    T = n_seq * seq
    tn, rows = CONV_TN, CONV_ROWS
    n_steps = seq // rows

    def body(zg_ref, zv_ref, cg_ref, cv_ref, dx_ref, wd_ref, wg_ref, wv_ref,
             dzg_ref, dzv_ref, dwg_ref, dwv_ref, dbg_ref, dbv_ref, dcg_ref, dcv_ref):
        def colsum(x):
            return jnp.sum(x, axis=0, keepdims=True)

        zero = jnp.zeros((1, tn), F32)
        db = (zero, zero)
        for s in range(n_steps):
            r = pl.ds(s * rows, rows)
            g = cg_ref[r, :].astype(F32)
            v = cv_ref[r, :].astype(F32)
            sg = _sigmoid(g)
            dav = _dot_nt(dx_ref[r, :], wd_ref[...])
            dcg = dav * v * (sg * (1.0 + g * (1.0 - sg)))
            dcv = dav * (g * sg)
            dcg_ref[r, :] = dcg
            dcv_ref[r, :] = dcv
            db = (db[0] + colsum(dcg), db[1] + colsum(dcv))

        def back(s, accs):
            r0 = pl.multiple_of(s * rows, rows)
            last = s == n_steps - 1
            rn = pl.multiple_of(jnp.minimum(r0 + rows, seq - SUBLANES), SUBLANES)
            new = []
            for half, (dc_ref, w_ref, dz_ref, z_ref) in enumerate(((dcg_ref, wg_ref, dzg_ref, zg_ref),
                                                                   (dcv_ref, wv_ref, dzv_ref, zv_ref))):
                cur = dc_ref[pl.ds(r0, rows), :]
                nxt = jnp.where(last, 0.0, dc_ref[pl.ds(rn, SUBLANES), :])
                u1, u2 = _shift_rows_up(cur, nxt, 1), _shift_rows_up(cur, nxt, 2)
                dz_ref[pl.ds(r0, rows), :] = (w_ref[2:3, :] * cur + w_ref[1:2, :] * u1 + w_ref[0:1, :] * u2).astype(BF16)
                z = z_ref[pl.ds(r0, rows), :].astype(F32)
                new += [accs[3 * half] + colsum(u2 * z), accs[3 * half + 1] + colsum(u1 * z),
                        accs[3 * half + 2] + colsum(cur * z)]
            return tuple(new)

        dw = lax.fori_loop(0, n_steps, back, (zero,) * 6)
        first_seq = pl.program_id(1) == 0

        @pl.when(first_seq)
        def _():
            dwg_ref[...] = jnp.concatenate(dw[0:3], axis=0)
            dwv_ref[...] = jnp.concatenate(dw[3:6], axis=0)
            dbg_ref[...], dbv_ref[...] = db

        @pl.when(jnp.logical_not(first_seq))
        def _():
            dwg_ref[...] += jnp.concatenate(dw[0:3], axis=0)
            dwv_ref[...] += jnp.concatenate(dw[3:6], axis=0)
            dbg_ref[...] += db[0]
            dbv_ref[...] += db[1]

    zs = pl.BlockSpec((seq, tn), lambda j, b: (b, j))
    ws = pl.BlockSpec((3, tn), lambda j, b: (0, j))
    bs = pl.BlockSpec((1, tn), lambda j, b: (0, j))
    outs = pl.pallas_call(
        body, name=name, grid=(D_FF // tn, n_seq),
        in_specs=[zs] * 4 + [pl.BlockSpec((seq, D_MODEL), lambda j, b: (b, 0)),
                             pl.BlockSpec((tn, D_MODEL), lambda j, b: (j, 0)), ws, ws],
        out_specs=[zs, zs, ws, ws, bs, bs],
        out_shape=[jax.ShapeDtypeStruct((T, D_FF), BF16)] * 2 + [jax.ShapeDtypeStruct((3, D_FF), F32)] * 2
        + [jax.ShapeDtypeStruct((1, D_FF), F32)] * 2,
        scratch_shapes=[pltpu.VMEM((seq, tn), F32), pltpu.VMEM((seq, tn), F32)],
        compiler_params=_params(("parallel", "arbitrary")),
    )(z_g, z_v, c_g, c_v, dx2_bf, w_down, cw_g, cw_v)
    dz_g, dz_v, dw_g, dw_v, db_g, db_v = outs
    return dz_g, dz_v, dw_g, dw_v, db_g.reshape(D_FF), db_v.reshape(D_FF)


def _layer_fwd(x, h, w, sched, tail, *, n_seq, seq, l):
    tag = f"l{l}"
    deps = sched("fwd_start", l, h)
    proj = _mm(h, w["w_in_t"], mode="nt", out_dtype=ACT_DTYPE, rotate=W_IN_ROTATE, name=f"{tag}_proj", deps=deps)
    y_att = _attention_fwd(proj, w["q_norm"], w["k_norm"], w["sinks"], n_seq=n_seq, seq=seq, name=f"{tag}_att")
    deps = sched("fwd_att", l, y_att)
    y_sgu = _sgu_fwd(proj, w["sgu_norm"], w["w_s"], w["bias_full"], n_seq=n_seq, seq=seq, name=f"{tag}_sgu")
    merged = _merge_fwd(y_att, y_sgu, w["w_oa"], w["w_ob"], proj, name=f"{tag}_merge", deps=deps)
    x1, h2 = _mm_rows(merged, w["w_out"], mode="nn", fn=_residual_then_norm, out_dtypes=(F32, BF16), rows=(x,),
                      vecs=(w["ffn_norm"],), name=f"{tag}_out")
    deps = sched("fwd_mixer_done", l, x1)
    a, z_g, z_v, c_g, c_v = _up_conv_fwd(h2, w["w_up_t"], w["cw_g"], w["cw_v"], w["cb_g"], w["cb_v"], n_seq=n_seq,
                                         seq=seq, name=f"{tag}_up_conv", deps=deps)
    deps = sched("fwd_conv", l, a)
    if tail[0] == "norm":
        out = _mm_rows(a, w["w_down"], mode="nn", fn=_residual_then_norm, out_dtypes=(F32, BF16), rows=(x1,),
                       vecs=(tail[1],), name=f"{tag}_down", deps=deps)
    else:
        out = _mm_rows(a, w["w_down"], mode="nn", fn=_residual_then_loss, out_dtypes=(F32, BF16), rows=(x1, tail[1]),
                       reduce=True, name=f"{tag}_down", deps=deps)
    saved = dict(x=x, h=h, proj=proj, y_att=y_att, y_sgu=y_sgu, merged=merged, x1=x1, h2=h2, z_g=z_g, z_v=z_v,
                 c_g=c_g, c_v=c_v, a=a)
    return out, saved


def _layer_bwd(dx2, dx2_bf, w, s, sched, deps, *, n_seq, seq, l):
    tag = f"l{l}b"
    g = {}
    g["w_down"] = _mm(s["a"], dx2_bf, mode="tn", out_dtype=F32, name=f"{tag}_dw_down", deps=deps)
    dz_g, dz_v, g["cw_g"], g["cw_v"], g["cb_g"], g["cb_v"] = _conv_bwd(
        s["z_g"], s["z_v"], s["c_g"], s["c_v"], dx2_bf, w["w_down"], w["cw_g"], w["cw_v"], n_seq=n_seq, seq=seq,
        name=f"{tag}_conv")
    dw_up_t = _mm(dz_g, s["h2"], mode="tn", out_dtype=F32, out_rows=(0, 2 * D_FF), name=f"{tag}_dw_up_g")
    g["w_up_t"] = _mm(dz_v, s["h2"], mode="tn", out_dtype=F32, out_rows=(D_FF, 2 * D_FF), out_prev=dw_up_t,
                      name=f"{tag}_dw_up_v")
    deps = sched("bwd_ffn_grads", l, dz_v, g)
    dx1, dx1_bf, dgain = _mm_rows((dz_g, dz_v), w["w_up_t"], mode="nn", fn=_rms_bwd_rows, out_dtypes=(F32, BF16),
                                  rows=(s["x1"], dx2), vecs=(w["ffn_norm"],), reduce=True, a_at=(0, D_FF),
                                  name=f"{tag}_dh2", deps=deps)
    g["ffn_norm"] = dgain.reshape(D_MODEL)
    dpa, dpb, dga, dgb = _merge_bwd(dx1_bf, w["w_out"], s["y_att"], s["y_sgu"], w["w_oa"], w["w_ob"], s["proj"],
                                    name=f"{tag}_merge")
    deps = sched("bwd_merge", l, dpa)
    g["w_out"] = _mm(s["merged"], dx1_bf, mode="tn", out_dtype=F32, name=f"{tag}_dw_out",
                     deps=deps)
    dy_att = _mm(dpa, w["w_oa"], mode="nt", out_dtype=BF16, name=f"{tag}_dy_att")
    dy_sgu = _mm(dpb, w["w_ob"], mode="nt", out_dtype=F32, name=f"{tag}_dy_sgu")
    g["w_oa"] = _mm(s["y_att"], dpa, mode="tn", out_dtype=F32, name=f"{tag}_dw_oa")
    g["w_ob"] = _mm(s["y_sgu"], dpb, mode="tn", out_dtype=F32, name=f"{tag}_dw_ob")
    deps = sched("bwd_out_grads", l, dy_att, g)
    dqkv, g["q_norm"], g["k_norm"], g["sinks"] = _attention_bwd(
        s["proj"], dy_att, w["q_norm"], w["k_norm"], w["sinks"], n_seq=n_seq, seq=seq, name=f"{tag}_att", deps=deps)
    deps = sched("bwd_att", l, dqkv)
    dsuv, g["sgu_norm"], g["w_s"], g["b_s"] = _sgu_bwd(
        s["proj"], dy_sgu, w["sgu_norm"], w["w_s"], w["bias_full"], n_seq=n_seq, seq=seq, name=f"{tag}_sgu", deps=deps)
    dproj = (dsuv, dga, dgb, dqkv)
    at = (QKV_WIDTH, QKV_WIDTH + 2 * SGU_WIDTH, QKV_WIDTH + 2 * SGU_WIDTH + D_MODEL, 0)
    g["w_in_t"] = _mm_tn_parts(dproj, at, s["h"], name=f"{tag}_dw_in")
    deps = sched("bwd_w_in_grad", l, dqkv, g)
    dx, dx_bf, dgain = _mm_rows(dproj, w["w_in_t"], mode="nn", fn=_rms_bwd_rows, out_dtypes=(F32, BF16),
                                rows=(s["x"], dx1), vecs=(w["mix_norm"],), reduce=True, a_at=at,
                                name=f"{tag}_dh", deps=deps)
    g["mix_norm"] = dgain.reshape(D_MODEL)
    return dx, dx_bf, g, sched("bwd_dh", l, dx)


def _local_step(x, target, weights, sched, *, n_seq, seq):
    depth = len(weights)
    saved = []
    h = _rms_fwd(x, weights[0]["mix_norm"], name="l0_mix_norm", deps=sched("begin", 0, x))
    for l in range(depth):
        tail = ("norm", weights[l + 1]["mix_norm"]) if l + 1 < depth else ("loss", target)
        out, s = _layer_fwd(x, h, weights[l], sched, tail, n_seq=n_seq, seq=seq, l=l)
        saved.append(s)
        if l + 1 < depth:
            x, h = out
    dy, dy_bf, loss_cols = out
    grads = [None] * depth
    deps = ()
    for l in reversed(range(depth)):
        dy, dy_bf, grads[l], deps = _layer_bwd(dy, dy_bf, weights[l], saved[l], sched, deps, n_seq=n_seq, seq=seq, l=l)
    return jnp.sum(loss_cols), dy, grads, deps


W_IN_SHARD = IN_WIDTH // N_DEV
W_UP_SHARD = 2 * D_FF // N_DEV
COL_MOVE_ROWS = 256


def _w_o_moves():
    return tuple((j, 0, LANES, 0, j * LANES) for j in range(N_DEV))


def _disassemble(mats, w, moves, *, name):
    R = mats[0].shape[0]
    tr = min(R, COL_MOVE_ROWS)
    n = len(mats)

    def body(*refs):
        m_refs, o_ref = refs[:n], refs[n]
        for j, lo, hi, which, at in moves:
            o_ref[j, :, lo:hi] = m_refs[which][:, at:at + hi - lo]

    return pl.pallas_call(
        body, name=name, grid=(R // tr,),
        in_specs=[pl.BlockSpec((tr, m.shape[1]), lambda i: (i, 0)) for m in mats],
        out_specs=pl.BlockSpec((N_DEV, tr, w), lambda i: (0, i, 0)),
        out_shape=jax.ShapeDtypeStruct((N_DEV, R, w), mats[0].dtype),
        compiler_params=_params(("parallel",)),
    )(*mats)


def _my_place():
    return lax.axis_index("x"), lax.axis_index("y"), lax.axis_index("c")


def _gathered_shape(shape, kind):
    r, c = shape
    return {"blocks": (N_DEV, r, c), "rows": (N_DEV * r, c), "cols": (r, N_DEV * c)}[kind]


def _gather_window(ref, kind, shape, j):
    r, c = shape
    if kind == "blocks":
        return ref.at[j]
    if kind == "rows":
        return ref.at[pl.ds(pl.multiple_of(j * r, r), r), :]
    return ref.at[:, pl.ds(pl.multiple_of(j * c, c), c)]


def _gather(srcs, kinds, *, name):
    n = len(srcs)
    shapes = [s.shape for s in srcs]
    per = 7

    def body(*refs):
        src_refs, dst_refs = refs[:n], refs[n:2 * n]
        send_sems, recv_sems, local_sems = refs[2 * n:]
        x, y, c = _my_place()
        me, sibling = (x, y, c), (x, y, 1 - c)
        chips = [(1 - x, y), (x, 1 - y), (1 - x, 1 - y)]

        def at(i, px, py, pc):
            return _gather_window(dst_refs[i], kinds[i], shapes[i], 4 * px + 2 * py + pc)

        def copy(i, k, block, to, src=None):
            return pltpu.make_async_remote_copy(
                src_ref=at(i, *block) if src is None else src, dst_ref=at(i, *block),
                send_sem=send_sems.at[per * i + k], recv_sem=recv_sems.at[per * i + k], device_id=to, device_id_type=MESH)

        mine = [pltpu.make_async_copy(src_refs[i], at(i, *me), local_sems.at[i]) for i in range(n)]
        for cp in mine:
            cp.start()
        started = []
        for i in range(n):
            first = [copy(i, 0, me, sibling, src=src_refs[i])]
            first += [copy(i, 1 + j, me, (*chip, c), src=src_refs[i]) for j, chip in enumerate(chips)]
            for cp in first:
                cp.start()
            started += first
        for i in range(n):
            for j, chip in enumerate(chips):
                copy(i, 1 + j, (*chip, c), me).wait_recv()
                fwd = copy(i, 4 + j, (*chip, c), sibling)
                fwd.start()
                started.append(fwd)
        for i in range(n):
            copy(i, 0, sibling, me).wait_recv()
            for j, chip in enumerate(chips):
                copy(i, 4 + j, (*chip, 1 - c), me).wait_recv()
        for cp in started:
            cp.wait_send()
        for cp in mine:
            cp.wait()

    return pl.pallas_call(
        body, name=name,
        out_shape=[jax.ShapeDtypeStruct(_gathered_shape(s.shape, k), s.dtype) for s, k in zip(srcs, kinds)],
        in_specs=[ANY] * n, out_specs=[ANY] * n,
        scratch_shapes=[pltpu.SemaphoreType.DMA((per * n,)), pltpu.SemaphoreType.DMA((per * n,)),
                        pltpu.SemaphoreType.DMA((n,))],
    )(*srcs)


HBM = pl.BlockSpec(memory_space=pltpu.HBM)
SEM = pl.BlockSpec(memory_space=pltpu.SEMAPHORE)
TOKEN = jax.ShapeDtypeStruct((SUBLANES, LANES), F32)
TOKEN_SPEC = pl.BlockSpec(memory_space=pltpu.VMEM)
SPLIT_PARAMS = pltpu.CompilerParams(has_side_effects=pltpu.SideEffectType.DATAFLOW_SIDE_EFFECTING)


def _in_hbm(x):
    return pltpu.with_memory_space_constraint(x, pltpu.HBM)


def _hbm_like(shape, dtype):
    return pltpu.HBM(shape, dtype)


def _place_own(stacks, layers, kinds, dtypes, *, name, deps=()):
    n = len(stacks)
    shapes = [s.shape[1:] for s in stacks]

    def body(*refs):
        s_refs, land_refs, bufs, sems = refs[:n], refs[n:2 * n], refs[2 * n:3 * n], refs[3 * n]
        x, y, c = _my_place()
        copies = []
        for i in range(n):
            bufs[i][...] = s_refs[i][...].astype(dtypes[i])
            copies.append(pltpu.make_async_copy(
                bufs[i], _gather_window(land_refs[i], kinds[i], shapes[i], 4 * x + 2 * y + c), sems.at[i]))
        for cp in copies:
            cp.start()
        for cp in copies:
            cp.wait()

    def layer_of(shape, l):
        return pl.BlockSpec((None,) + shape, lambda i: (l,) + (0,) * len(shape))

    body, dep_specs, dep_args = _with_deps(body, n, deps)
    return pl.pallas_call(
        body, name=name, grid=(1,),
        out_shape=[jax.ShapeDtypeStruct(_gathered_shape(s, k), d) for s, k, d in zip(shapes, kinds, dtypes)],
        in_specs=[layer_of(s, l) for s, l in zip(shapes, layers)] + dep_specs, out_specs=[ANY] * n,
        scratch_shapes=[pltpu.VMEM(s, d) for s, d in zip(shapes, dtypes)] + [pltpu.SemaphoreType.DMA((n,))],
        compiler_params=_params(("arbitrary",)),
    )(*stacks, *dep_args)


def _gather_start(lands, kinds, shapes, after=(), *, name):
    n = len(lands)
    n_after = len(after)

    def body(*refs):
        land_refs = refs[:n]
        send_sems, recv_sems = refs[n + n_after], refs[n + n_after + 1]
        x, y, c = _my_place()
        targets = [(x, y, 1 - c), (1 - x, y, c), (x, 1 - y, c), (1 - x, 1 - y, c)]
        for i in range(n):
            own = _gather_window(land_refs[i], kinds[i], shapes[i], 4 * x + 2 * y + c)
            for k, to in enumerate(targets):
                pltpu.make_async_remote_copy(
                    src_ref=own, dst_ref=own, send_sem=send_sems.at[4 * i + k], recv_sem=recv_sems.at[4 * i + k],
                    device_id=to, device_id_type=MESH).start()
        refs[-1][...] = jnp.zeros_like(refs[-1])

    outs = pl.pallas_call(
        body, name=name,
        out_shape=[pltpu.SemaphoreType.DMA((4 * n,)), pltpu.SemaphoreType.DMA((4 * n,))]
        + [_hbm_like(a.shape, a.dtype) for a in lands] + [TOKEN],
        in_specs=[HBM] * n + [ANY] * n_after, out_specs=[SEM, SEM] + [HBM] * n + [TOKEN_SPEC],
        input_output_aliases={i: 2 + i for i in range(n)},
        compiler_params=SPLIT_PARAMS,
    )(*[_in_hbm(a) for a in lands], *after)
    return outs[0], outs[1], outs[2:2 + n], outs[-1]


def _gather_forward(recv_sems, lands, kinds, shapes, after, *, name):
    n = len(lands)

    def body(*refs):
        recv_ref, land_refs = refs[0], refs[1:1 + n]
        fwd_send, fwd_recv = refs[2 + n], refs[3 + n]
        token = refs[-1]
        x, y, c = _my_place()
        chips = [(1 - x, y), (x, 1 - y), (1 - x, 1 - y)]
        for i in range(n):
            for j, (px, py) in enumerate(chips):
                block = _gather_window(land_refs[i], kinds[i], shapes[i], 4 * px + 2 * py + c)
                pltpu.make_async_remote_copy(
                    src_ref=block, dst_ref=block, send_sem=fwd_send.at[3 * i + j], recv_sem=recv_ref.at[4 * i + 1 + j],
                    device_id=(px, py, c), device_id_type=MESH).wait_recv()
                pltpu.make_async_remote_copy(
                    src_ref=block, dst_ref=block, send_sem=fwd_send.at[3 * i + j], recv_sem=fwd_recv.at[3 * i + j],
                    device_id=(x, y, 1 - c), device_id_type=MESH).start()
        token[...] = jnp.zeros_like(token)

    outs = pl.pallas_call(
        body, name=name,
        out_shape=[pltpu.SemaphoreType.DMA((3 * n,)), pltpu.SemaphoreType.DMA((3 * n,))]
        + [_hbm_like(a.shape, a.dtype) for a in lands] + [TOKEN],
        in_specs=[SEM] + [HBM] * n + [ANY], out_specs=[SEM, SEM] + [HBM] * n + [TOKEN_SPEC],
        input_output_aliases={1 + i: 2 + i for i in range(n)},
        compiler_params=SPLIT_PARAMS,
    )(recv_sems, *lands, after)
    return outs[0], outs[1], outs[2:2 + n], outs[-1]


def _gather_finish(send_sems, recv_sems, fwd_send, fwd_recv, lands, kinds, shapes, after, *, name):
    n = len(lands)

    def body(*refs):
        send_ref, recv_ref, fsend_ref, frecv_ref = refs[:4]
        land_refs = refs[4:4 + n]
        x, y, c = _my_place()
        chips = [(1 - x, y), (x, 1 - y), (1 - x, 1 - y)]
        sibling = (x, y, 1 - c)
        for i in range(n):
            def window(j):
                return _gather_window(land_refs[i], kinds[i], shapes[i], j)

            mine, theirs = window(4 * x + 2 * y + c), window(4 * x + 2 * y + (1 - c))
            pltpu.make_async_remote_copy(src_ref=mine, dst_ref=theirs, send_sem=send_ref.at[4 * i],
                                         recv_sem=recv_ref.at[4 * i], device_id=sibling, device_id_type=MESH).wait_recv()
            for j, (px, py) in enumerate(chips):
                block = window(4 * px + 2 * py + (1 - c))
                pltpu.make_async_remote_copy(src_ref=block, dst_ref=block, send_sem=fsend_ref.at[3 * i + j],
                                             recv_sem=frecv_ref.at[3 * i + j], device_id=sibling,
                                             device_id_type=MESH).wait_recv()
            for k in range(4):
                pltpu.make_async_remote_copy(src_ref=mine, dst_ref=mine, send_sem=send_ref.at[4 * i + k],
                                             recv_sem=recv_ref.at[4 * i + k], device_id=sibling,
                                             device_id_type=MESH).wait_send()
            for j, (px, py) in enumerate(chips):
                block = window(4 * px + 2 * py + c)
                pltpu.make_async_remote_copy(src_ref=block, dst_ref=block, send_sem=fsend_ref.at[3 * i + j],
                                             recv_sem=frecv_ref.at[3 * i + j], device_id=sibling,
                                             device_id_type=MESH).wait_send()

    return pl.pallas_call(
        body, name=name,
        out_shape=[_hbm_like(a.shape, a.dtype) for a in lands],
        in_specs=[SEM] * 4 + [HBM] * n + [ANY], out_specs=[HBM] * n,
        input_output_aliases={4 + i: i for i in range(n)},
        compiler_params=SPLIT_PARAMS,
    )(send_sems, recv_sems, fwd_send, fwd_recv, *lands, after)


def _pair_plan(src_ref, land_ref, x, y, c):
    return [(src_ref.at[2 * k + (1 - c)], land_ref.at[k], (x, y, 1 - c)) for k in range(N_CHIPS)]


def _chip_plan(src_ref, land_ref, x, y, c):
    chips = [(1 - x, y), (x, 1 - y), (1 - x, 1 - y)]
    return [(src_ref.at[2 * px + py], land_ref.at[k], (px, py, c)) for k, (px, py) in enumerate(chips)]


def _exchange_copies(plan, per, src_refs, land_refs, send_sems, recv_sems):
    x, y, c = _my_place()
    copies = []
    for i, (s_ref, l_ref) in enumerate(zip(src_refs, land_refs)):
        for q, (src, dst, to) in enumerate(plan(s_ref, l_ref, x, y, c)):
            copies.append(pltpu.make_async_remote_copy(
                src_ref=src, dst_ref=dst, send_sem=send_sems.at[per * i + q], recv_sem=recv_sems.at[per * i + q],
                device_id=to, device_id_type=MESH))
    return copies


def _exchange_start(srcs, plan, per, *, name):
    n = len(srcs)

    def body(*refs):
        src_refs, land_refs = refs[:n], refs[n:2 * n]
        send_sems, recv_sems = refs[2 * n], refs[2 * n + 1]
        for cp in _exchange_copies(plan, per, src_refs, land_refs, send_sems, recv_sems):
            cp.start()
        refs[-1][...] = jnp.zeros_like(refs[-1])

    lands = [lax.empty((per,) + s.shape[1:], s.dtype) for s in srcs]
    outs = pl.pallas_call(
        body, name=name,
        out_shape=[pltpu.SemaphoreType.DMA((per * n,)), pltpu.SemaphoreType.DMA((per * n,))]
        + [_hbm_like(s.shape, s.dtype) for s in srcs] + [_hbm_like(a.shape, a.dtype) for a in lands] + [TOKEN],
        in_specs=[HBM] * (2 * n), out_specs=[SEM, SEM] + [HBM] * (2 * n) + [TOKEN_SPEC],
        input_output_aliases={i: 2 + i for i in range(2 * n)},
        compiler_params=SPLIT_PARAMS,
    )(*[_in_hbm(s) for s in srcs], *[_in_hbm(a) for a in lands])
    return outs[0], outs[1], outs[2:2 + n], outs[2 + n:2 + 2 * n], outs[-1]


def _exchange_wait(send_sems, recv_sems, srcs, lands, plan, per, after, *, name):
    n = len(srcs)
    after = list(after) if isinstance(after, (list, tuple)) else [after]

    def body(*refs):
        send_ref, recv_ref = refs[0], refs[1]
        src_refs, land_refs = refs[2:2 + n], refs[2 + n:2 + 2 * n]
        copies = _exchange_copies(plan, per, src_refs, land_refs, send_ref, recv_ref)
        for cp in copies:
            cp.wait_recv()
        for cp in copies:
            cp.wait_send()

    outs = pl.pallas_call(
        body, name=name,
        out_shape=[_hbm_like(s.shape, s.dtype) for s in srcs] + [_hbm_like(a.shape, a.dtype) for a in lands],
        in_specs=[SEM, SEM] + [HBM] * (2 * n) + [ANY] * len(after), out_specs=[HBM] * (2 * n),
        input_output_aliases={2 + i: i for i in range(2 * n)},
        compiler_params=SPLIT_PARAMS,
    )(send_sems, recv_sems, *srcs, *lands, *after)
    return outs[:n], outs[n:]


REDUCE_BLOCK_BYTES = 2 << 20


def _row_tile(r, c):
    row_bytes = 4 * (-(-c // LANES) * LANES)
    best = r
    for d in range(SUBLANES, r, SUBLANES):
        if r % d == 0 and d * row_bytes <= REDUCE_BLOCK_BYTES:
            best = d
    return best if r * row_bytes > REDUCE_BLOCK_BYTES else r


def _reduce_pair_sum(blocked, recv, place, wire_dtype, *, name):
    _, r, c = blocked.shape
    tr = _row_tile(r, c)

    def body(place_ref, g_ref, r_ref, own_ref, send_ref):
        s = g_ref[...] + r_ref[...]
        send_ref[...] = s.astype(wire_dtype)

        @pl.when(pl.program_id(1) == place_ref[1])
        def _():
            own_ref[...] = s

    return pl.pallas_call(
        body, name=name,
        grid_spec=pltpu.PrefetchScalarGridSpec(
            num_scalar_prefetch=1, grid=(r // tr, N_CHIPS),
            in_specs=[pl.BlockSpec((None, None, tr, c), lambda i, k, place_ref: (k, place_ref[0], i, 0)),
                      pl.BlockSpec((None, tr, c), lambda i, k, place_ref: (k, i, 0))],
            out_specs=[pl.BlockSpec((tr, c), lambda i, k, place_ref: (i, 0)),
                       pl.BlockSpec((None, tr, c), lambda i, k, place_ref: (k, i, 0))]),
        out_shape=[jax.ShapeDtypeStruct((r, c), F32), jax.ShapeDtypeStruct((N_CHIPS, r, c), wire_dtype)],
        compiler_params=_params(("parallel", "arbitrary")),
    )(place, blocked.reshape(N_CHIPS, 2, r, c), recv)


def _chip_sum(own_ref, r_ref):
    return ((own_ref[...] + r_ref[0].astype(F32)) + r_ref[1].astype(F32)) + r_ref[2].astype(F32)


def _reduce_chip_sum(own, recv, *, name):
    r, c = own.shape
    tr = _row_tile(r, c)

    def body(own_ref, r_ref, o_ref):
        o_ref[...] = _chip_sum(own_ref, r_ref)

    return pl.pallas_call(
        body, name=name, grid=(r // tr,),
        in_specs=[pl.BlockSpec((tr, c), lambda i: (i, 0)), pl.BlockSpec((N_CHIPS - 1, tr, c), lambda i: (0, i, 0))],
        out_specs=pl.BlockSpec((tr, c), lambda i: (i, 0)),
        out_shape=jax.ShapeDtypeStruct((r, c), F32),
        compiler_params=_params(("parallel",)),
    )(own, recv)


def _adamw_math(w, g, m, v):
    nm = ADAM_B1 * m + (1.0 - ADAM_B1) * g
    nv = ADAM_B2 * v + (1.0 - ADAM_B2) * (g * g)
    m_hat = nm / (1.0 - ADAM_B1 ** ADAM_STEP)
    v_hat = nv / (1.0 - ADAM_B2 ** ADAM_STEP)
    return -ADAM_LR * (m_hat / (jnp.sqrt(v_hat) + ADAM_EPS) + ADAM_WD * w), nm, nv


ADAMW_ROWS = 256


def _adamw_small(ws, gs, ms, vs, *, name):
    n = len(ws)

    def rows_of(a):
        return a.reshape(-1, a.shape[-1])

    def body(*refs):
        for i in range(n):
            w_ref, g_ref, m_ref, v_ref = refs[4 * i:4 * i + 4]
            outs = refs[4 * n + 3 * i:4 * n + 3 * i + 3]
            rows = w_ref.shape[0]
            if rows % ADAMW_ROWS:
                outs[0][...], outs[1][...], outs[2][...] = _adamw_math(w_ref[...], g_ref[...], m_ref[...], v_ref[...])
                continue

            def chunk(s, carry, w_ref=w_ref, g_ref=g_ref, m_ref=m_ref, v_ref=v_ref, outs=outs):
                r = pl.ds(pl.multiple_of(s * ADAMW_ROWS, ADAMW_ROWS), ADAMW_ROWS)
                outs[0][r, :], outs[1][r, :], outs[2][r, :] = _adamw_math(w_ref[r, :], g_ref[r, :], m_ref[r, :], v_ref[r, :])
                return carry

            lax.fori_loop(0, rows // ADAMW_ROWS, chunk, 0)

    vmem = pl.BlockSpec(memory_space=pltpu.VMEM)
    outs = pl.pallas_call(
        body, name=name, in_specs=[vmem] * (4 * n), out_specs=[vmem] * (3 * n),
        out_shape=[jax.ShapeDtypeStruct(rows_of(w).shape, F32) for w in ws for _ in range(3)],
        compiler_params=_params(),
    )(*[rows_of(a) for quad in zip(ws, gs, ms, vs) for a in quad])
    return [tuple(o.reshape(w.shape) for o in outs[3 * i:3 * i + 3]) for i, w in enumerate(ws)]


def _reduce_adamw(own, recv, w, m, v, layer, prev, *, name):
    r, c = own.shape
    tr = _row_tile(r, c)
    n_prev = 0 if prev is None else len(prev)

    def body(own_ref, r_ref, w_ref, m_ref, v_ref, *rest):
        g_ref, d_ref, nm_ref, nv_ref = rest[n_prev:]
        g = _chip_sum(own_ref, r_ref)
        g_ref[...] = g
        d_ref[...], nm_ref[...], nv_ref[...] = _adamw_math(w_ref[...], g, m_ref[...], v_ref[...])

    slot = pl.BlockSpec((None, tr, c), lambda i: (layer, i, 0))
    return pl.pallas_call(
        body, name=name, grid=(r // tr,),
        in_specs=[pl.BlockSpec((tr, c), lambda i: (i, 0)), pl.BlockSpec((N_CHIPS - 1, tr, c), lambda i: (0, i, 0)),
                  slot, slot, slot] + [ANY] * n_prev,
        out_specs=[slot] * 4,
        out_shape=[jax.ShapeDtypeStruct((DEPTH, r, c), F32)] * 4,
        input_output_aliases={5 + k: k for k in range(n_prev)},
        compiler_params=_params(("parallel",)),
    )(own, recv, w, m, v, *(prev or ()))


REPLICATED = (("mix_norm", (D_MODEL,)), ("q_norm", (HEAD_DIM,)), ("k_norm", (HEAD_DIM,)), ("sinks", (N_Q_HEADS,)),
              ("sgu_norm", (SGU_WIDTH,)), ("w_s", (SGU_GROUPS, BLOCK, BLOCK)), ("b_s", (SGU_GROUPS, BLOCK)),
              ("ffn_norm", (D_MODEL,)), ("conv_b", (2 * D_FF,)))
TRANSPOSED = ("w_in", "w_up")
SHARDED = (("w_in", "rows"), ("w_oa", "cols"), ("w_ob", "cols"), ("w_out", "rows"), ("w_up", "rows"),
           ("conv_w", "blocks"), ("w_down", "rows"))
WEIGHT_ORDER = ("mix_norm", "w_in", "q_norm", "k_norm", "sinks", "sgu_norm", "w_s", "b_s", "w_oa", "w_ob", "w_out",
                "ffn_norm", "w_up", "conv_w", "conv_b", "w_down")
MIXER_WEIGHTS = ["w_in", "w_oa", "w_ob", "w_out"]
FFN_WEIGHTS = ["w_up", "conv_w", "w_down"]


def _small_layout():
    segs, off = {}, 0
    for name, shape in sorted(REPLICATED, key=lambda named: -math.prod(named[1])):
        for l in range(DEPTH):
            n = math.prod(shape)
            segs[(l, name)] = (off, n)
            off += n
    per_dev = -(-off // (N_DEV * SUBLANES * LANES)) * SUBLANES * LANES
    return segs, off, per_dev


def _pack_small(grads, loss_part):
    ssegs, total, per_dev = _small_layout()
    flat = jnp.concatenate([grads[l][name].reshape(-1) for (l, name) in ssegs] + [loss_part.reshape(1)])
    return jnp.pad(flat, (0, N_DEV * per_dev - total - 1)).reshape(N_DEV, per_dev // LANES, LANES)


def _unpack_small(gathered):
    ssegs, total, _ = _small_layout()
    flat = gathered.reshape(-1)
    small = {}
    for name, shape in REPLICATED:
        start, n = ssegs[(0, name)]
        small[name] = flat[start:start + DEPTH * n].reshape((DEPTH,) + shape)
    return small, flat[total]


def kernel(x, mix_norm, w_in, q_norm, k_norm, sinks, sgu_norm, w_s, b_s, w_oa, w_ob, w_out, ffn_norm, w_up, conv_w, conv_b, w_down, loss_target, m_mix_norm, m_w_in, m_q_norm, m_k_norm, m_sinks, m_sgu_norm, m_w_s, m_b_s, m_w_oa, m_w_ob, m_w_out, m_ffn_norm, m_w_up, m_conv_w, m_conv_b, m_w_down, v_mix_norm, v_w_in, v_q_norm, v_k_norm, v_sinks, v_sgu_norm, v_w_s, v_b_s, v_w_oa, v_w_ob, v_w_out, v_ffn_norm, v_w_up, v_conv_w, v_conv_b, v_w_down):
    W = dict(mix_norm=mix_norm, w_in=w_in, q_norm=q_norm, k_norm=k_norm, sinks=sinks, sgu_norm=sgu_norm, w_s=w_s, b_s=b_s,
             w_oa=w_oa, w_ob=w_ob, w_out=w_out, ffn_norm=ffn_norm, w_up=w_up, conv_w=conv_w, conv_b=conv_b, w_down=w_down)
    M = dict(mix_norm=m_mix_norm, w_in=m_w_in, q_norm=m_q_norm, k_norm=m_k_norm, sinks=m_sinks, sgu_norm=m_sgu_norm,
             w_s=m_w_s, b_s=m_b_s, w_oa=m_w_oa, w_ob=m_w_ob, w_out=m_w_out, ffn_norm=m_ffn_norm, w_up=m_w_up,
             conv_w=m_conv_w, conv_b=m_conv_b, w_down=m_w_down)
    V = dict(mix_norm=v_mix_norm, w_in=v_w_in, q_norm=v_q_norm, k_norm=v_k_norm, sinks=v_sinks, sgu_norm=v_sgu_norm,
             w_s=v_w_s, b_s=v_b_s, w_oa=v_w_oa, w_ob=v_w_ob, w_out=v_w_out, ffn_norm=v_ffn_norm, w_up=v_w_up,
             conv_w=v_conv_w, conv_b=v_conv_b, w_down=v_w_down)
    n_seq, seq, d_model = x.shape
    tokens = n_seq * seq
    mx, my, mc = _my_place()
    place = jnp.stack([mc, 2 * mx + my]).astype(jnp.int32)
    half = N_DEV // 2
    kind_of = dict(SHARDED)
    for name in TRANSPOSED:
        W[name], M[name], V[name] = (jnp.swapaxes(t[name], 1, 2) for t in (W, M, V))

    gather_groups = [[(0, MIXER_WEIGHTS[0])], [(0, n) for n in MIXER_WEIGHTS[1:]], [(0, n) for n in FFN_WEIGHTS],
                     [(1, n) for n in MIXER_WEIGHTS], [(1, n) for n in FFN_WEIGHTS]]
    started, in_flight = {}, {}
    weights = []
    for l in range(DEPTH):
        w = {name: W[name][l] for name, _ in REPLICATED}
        w["cb_g"], w["cb_v"] = W["conv_b"][l][:D_FF], W["conv_b"][l][D_FF:]
        w["bias_full"] = jnp.repeat(W["b_s"][l].T, SGU_WIDTH // SGU_GROUPS, axis=1)
        weights.append(w)

    def gather_start(gi, after=()):
        stacks = [W[name] for _, name in gather_groups[gi]]
        kinds = [kind_of[name] for _, name in gather_groups[gi]]
        shapes = [s.shape[1:] for s in stacks]
        lands = _place_own(stacks, [l for l, _ in gather_groups[gi]], kinds,
                           [F32 if name == "conv_w" else BF16 for _, name in gather_groups[gi]],
                           name=f"gather_weights_own_{gi}", deps=after)
        send, recv, lands, token = _gather_start(lands, kinds, shapes, after, name=f"gather_weights_start_{gi}")
        started[gi] = dict(sems=(send, recv), lands=lands, kinds=kinds, shapes=shapes)
        return token

    def gather_forward(gi, after):
        st = started[gi]
        in_flight[gi] = _gather_forward(st["sems"][1], st["lands"], st["kinds"], st["shapes"], after,
                                        name=f"gather_weights_forward_{gi}")
        return in_flight[gi][3]

    def gather_finish(gi, after):
        st = started.pop(gi)
        fwd_send, fwd_recv, lands_g, _ = in_flight.pop(gi)
        whole = _gather_finish(st["sems"][0], st["sems"][1], fwd_send, fwd_recv, lands_g, st["kinds"], st["shapes"], after,
                               name=f"gather_weights_finish_{gi}")
        for (l, name), arr in zip(gather_groups[gi], whole):
            w = weights[l]
            if name in TRANSPOSED:
                w[name + "_t"] = arr
            elif name == "conv_w":
                w["cw_g"] = arr[:half].transpose(1, 0, 2).reshape(3, D_FF)
                w["cw_v"] = arr[half:].transpose(1, 0, 2).reshape(3, D_FF)
            else:
                w[name] = arr

    reduce_state, results = {}, {}
    wire = {"conv_w": F32, "small": F32}

    def reduce_begin(key, names, arrays):
        send, recv, srcs_, lands_, token = _exchange_start(arrays, _pair_plan, N_CHIPS, name=f"reduce_pair_start_{key}")
        reduce_state[key] = dict(names=names, pair=(send, recv, srcs_, lands_))
        return [token]

    def reduce_pair(key, after):
        st = reduce_state[key]
        send, recv, srcs_, lands_ = st.pop("pair")
        blocked_, from_sibling = _exchange_wait(send, recv, srcs_, lands_, _pair_plan, N_CHIPS, after,
                                                name=f"reduce_pair_wait_{key}")
        sums = [_reduce_pair_sum(b, r, place, wire.get(n if isinstance(n, str) else n[1], BF16),
                                 name=f"reduce_pair_sum_{key}_{i}")
                for i, (n, b, r) in enumerate(zip(st["names"], blocked_, from_sibling))]
        st["own"] = [s[0] for s in sums]
        *st["chip"], token = _exchange_start([s[1] for s in sums], _chip_plan, N_CHIPS - 1, name=f"reduce_chip_start_{key}")
        return [token]

    def reduce_end(key, after):
        st = reduce_state.pop(key)
        send, recv, srcs_, lands_ = st["chip"]
        _, from_chips = _exchange_wait(send, recv, srcs_, lands_, _chip_plan, N_CHIPS - 1, after,
                                       name=f"reduce_chip_wait_{key}")
        done = []
        for n, own, got in zip(st["names"], st["own"], from_chips):
            if n == "small":
                results["small"] = _reduce_chip_sum(own, got, name="reduce_chip_sum_small")
            else:
                l, name = n
                results[name] = _reduce_adamw(own, got, W[name], M[name], V[name], l, results.get(name),
                                              name=f"l{l}_reduce_adamw_{name}")
                done.append(results[name][0])
        return done

    def sched(point, l, carry, g=None):
        deps = []
        if point == "begin":
            token = ()
            for gi in range(len(gather_groups)):
                token = [gather_start(gi, token)]
            deps = [gather_forward(0, token[0])]
        elif point == "fwd_start" and l == 0:
            gather_finish(0, carry)
            deps = [gather_forward(1, weights[0]["w_in_t"])]
        elif point == "fwd_att" and l == 0:
            gather_finish(1, carry)
            deps = [gather_forward(2, carry)]
        elif point == "fwd_mixer_done" and l == 0:
            gather_finish(2, carry)
        elif point == "fwd_conv" and l == 0:
            deps = [gather_forward(3, carry)]
        elif point == "fwd_start" and l == 1:
            gather_finish(3, carry)
        elif point == "fwd_att" and l == 1:
            deps = [gather_forward(4, carry)]
        elif point == "fwd_mixer_done" and l == 1:
            gather_finish(4, carry)
        elif point == "bwd_ffn_grads":
            conv_w = jnp.concatenate([g[k].reshape(3, half, W_UP_SHARD).transpose(1, 0, 2) for k in ("cw_g", "cw_v")])
            deps = reduce_begin(
                f"l{l}_ffn", [(l, "w_down"), (l, "w_up"), (l, "conv_w")],
                [g["w_down"].reshape(N_DEV, D_FF // N_DEV, D_MODEL),
                 g["w_up_t"].reshape(N_DEV, W_UP_SHARD, D_MODEL), conv_w])
        elif point == "bwd_merge":
            deps = reduce_pair(f"l{l}_ffn", carry)
        elif point == "bwd_out_grads":
            deps = reduce_begin(
                f"l{l}_out", [(l, "w_out"), (l, "w_oa"), (l, "w_ob")],
                [g["w_out"].reshape(N_DEV, D_MODEL // N_DEV, D_MODEL),
                 _disassemble((g["w_oa"],), LANES, _w_o_moves(), name=f"l{l}_split_dw_oa"),
                 _disassemble((g["w_ob"],), LANES, _w_o_moves(), name=f"l{l}_split_dw_ob")])
        elif point == "bwd_att":
            deps = reduce_pair(f"l{l}_out", carry)
        elif point == "bwd_w_in_grad":
            deps = reduce_begin(f"l{l}_in", [(l, "w_in")], [g["w_in_t"].reshape(N_DEV, W_IN_SHARD, D_MODEL)])
        elif point == "bwd_dh":
            deps = reduce_pair(f"l{l}_in", carry)
        return deps

    loss_part, dx, grads, last_deps = _local_step(x.reshape(tokens, d_model), loss_target.reshape(tokens, d_model),
                                                  weights, sched, n_seq=n_seq, seq=seq)
    for g in grads:
        g["conv_b"] = jnp.concatenate([g["cb_g"], g["cb_v"]])
    after = [dx, *last_deps, *reduce_begin("small", ["small"], [_pack_small(grads, loss_part)])]
    for key in [f"l{l}_{part}" for l in reversed(range(DEPTH)) for part in ("ffn", "out", "in")][:-1]:
        after = reduce_end(key, after)
    after = reduce_end("l0_in", after + reduce_pair("small", after))
    reduce_end("small", after)

    G, delta, new_m, new_v = {}, {}, {}, {}
    for name, _ in SHARDED:
        outs = [jnp.swapaxes(o, 1, 2) for o in results[name]] if name in TRANSPOSED else results[name]
        G[name], delta[name], new_m[name], new_v[name] = outs
    small, loss = _unpack_small(_gather([results["small"]], ["blocks"], name="gather_small_grads")[0])
    G.update(small)
    names = [name for name, _ in REPLICATED]
    stepped = _adamw_small(*[[t[name] for name in names] for t in (W, G, M, V)], name="adamw_replicated")
    for name, stepped_one in zip(names, stepped):
        delta[name], new_m[name], new_v[name] = stepped_one
    return (loss, dx.reshape(n_seq, seq, d_model), *[G[n] for n in WEIGHT_ORDER], *[delta[n] for n in WEIGHT_ORDER],
            *[new_m[n] for n in WEIGHT_ORDER], *[new_v[n] for n in WEIGHT_ORDER])
```

```python
import math

import jax
import jax.numpy as jnp
from jax import lax
from jax.experimental import pallas as pl
from jax.experimental.pallas import tpu as pltpu

F32 = jnp.float32
BF16 = jnp.bfloat16
ACT_DTYPE = BF16
MESH = pl.DeviceIdType.MESH

DEPTH = 2
D_MODEL = 1024
N_Q_HEADS = 8
HEAD_DIM = 64
ATT_WIDTH = 512
KV_WIDTH = 128
BLOCK = 128
SGU_WIDTH = 512
SGU_GROUPS = 8
IN_WIDTH = 3840
D_FF = 2816
NORM_EPS = 1e-6
NEG_INF = -1e30
ATT_SCALE = HEAD_DIM ** -0.5
ALIBI_SLOPES = tuple(2.0 ** (-(h + 1)) for h in range(N_Q_HEADS))
ADAM_LR, ADAM_B1, ADAM_B2, ADAM_EPS, ADAM_WD, ADAM_STEP = 0.001, 0.9, 0.999, 1e-08, 0.01, 10
N_DEV = 8
N_CHIPS = 4

QKV_WIDTH = ATT_WIDTH + 2 * KV_WIDTH
COL_SUV, COL_GA, COL_GB, COL_QKV = 0, 1024, 2048, 3072
W_IN_ROTATE = (1, IN_WIDTH // QKV_WIDTH)

LANES = 128
SUBLANES = 8
VMEM_LIMIT_V7X = 56 * 1024 * 1024
GELU_C = math.sqrt(2.0 / math.pi)
GELU_K = 0.044715
ANY = pl.BlockSpec(memory_space=pl.ANY)


def _params(sem=None):
    return pltpu.CompilerParams(dimension_semantics=sem, vmem_limit_bytes=VMEM_LIMIT_V7X)


def _sigmoid(x):
    return 1.0 / (1.0 + jnp.exp(-x))


def _gelu(x):
    th = jnp.tanh(GELU_C * (x + GELU_K * x * x * x))
    return 0.5 * x * (1.0 + th)


def _gelu_and_grad(x):
    x2 = x * x
    th = jnp.tanh(GELU_C * (x + GELU_K * x2 * x))
    g = 0.5 * x * (1.0 + th)
    dg = 0.5 * (1.0 + th) + 0.5 * x * (1.0 - th * th) * (GELU_C * (1.0 + 3.0 * GELU_K * x2))
    return g, dg


def _dot(a, b, dims):
    return lax.dot_general(a, b, (dims, ((), ())), preferred_element_type=F32)


def _dot_nn(a, b):
    return _dot(a, b, ((1,), (0,)))


def _dot_nt(a, b):
    return _dot(a, b, ((1,), (1,)))


def _dot_tn(a, b):
    return _dot(a, b, ((0,), (0,)))


def _lo_mask(shape):
    return lax.broadcasted_iota(jnp.int32, shape, len(shape) - 1) < (LANES // 2)


def _half_sums(x, lo):
    s_lo = jnp.sum(jnp.where(lo, x, 0.0), axis=-1, keepdims=True)
    s_all = jnp.sum(x, axis=-1, keepdims=True)
    return jnp.where(lo, s_lo, s_all - s_lo)


def _dup_half(x, half, lo):
    r = pltpu.roll(x, LANES // 2, axis=1)
    return jnp.where(lo, x, r) if half == 0 else jnp.where(lo, r, x)


def _with_deps(body, n_in, deps):
    k = len(deps)
    if not k:
        return body, [], ()

    def skipping(*refs):
        return body(*refs[:n_in], *refs[n_in + k:])

    return skipping, [ANY] * k, tuple(deps)


MM_VMEM_BUDGET = 40 * 1024 * 1024
MM_MAX_TILE = 1408
MM_MAX_TK = 4096
MM_STEP_BYTES = 1 << 20


def _divisors(n, step, cap):
    return [d for d in range(step, min(n, cap) + 1, step) if n % d == 0] or [n]


def _mm_tiles(M, N, K, out_bytes, tm_divides, tn_divides):
    best = None
    for tm in _divisors(M, LANES, MM_MAX_TILE):
        for tn in _divisors(N, LANES, MM_MAX_TILE):
            if tm_divides % tm or tn_divides % tn:
                continue
            for tk in _divisors(K, 4 * LANES, MM_MAX_TK):
                vmem = 4 * (tm * tk + tk * tn) + 2 * tm * tn * out_bytes + (0 if tk == K else 4 * tm * tn)
                if vmem > MM_VMEM_BUDGET:
                    continue
                traffic = 2 * M * K * (N // tn) + 2 * K * N * (M // tm) + M * N * out_bytes
                cost = traffic + (K // tk - 1) * 8 * M * N + (M // tm) * (N // tn) * (K // tk) * MM_STEP_BYTES
                if best is None or cost < best[0]:
                    best = (cost, tm, tn, tk)
    assert best is not None, (M, N, K)
    return best[1:]


def _mm(a, b, *, mode, out_dtype, name, deps=(), b_rows=(0, None), rotate=None, out_rows=(0, None), out_prev=None):
    b_first, b_count = b_rows
    if mode == "nn":
        (M, K), N = a.shape, b.shape[1]
    elif mode == "nt":
        (M, K), N = a.shape, (b.shape[0] if b_count is None else b_count)
    else:
        (K, M), N = a.shape, b.shape[1]
    shift, period = rotate or (0, 1)
    assert period == 1 or mode == "nt"
    out_first, out_total = out_rows[0], (M if out_rows[1] is None else out_rows[1])
    tm, tn, tk = _mm_tiles(M, N, K, jnp.dtype(out_dtype).itemsize, math.gcd(M, out_first),
                           math.gcd(N // period, b_first if mode == "nt" else 0))
    gm, gn, gk = M // tm, N // tn, K // tk

    def turned(j):
        per = N // period // tn
        return ((j // per + shift) % period) * per + j % per if period > 1 else j

    if mode == "nn":
        a_spec = pl.BlockSpec((tm, tk), lambda i, j, k: (i, k))
        b_spec = pl.BlockSpec((tk, tn), lambda i, j, k: (k + b_first // tk, j))
        contract = ((1,), (0,))
    elif mode == "nt":
        a_spec = pl.BlockSpec((tm, tk), lambda i, j, k: (i, k))
        b_spec = pl.BlockSpec((tn, tk), lambda i, j, k: (turned(j) + b_first // tn, k))
        contract = ((1,), (1,))
    else:
        a_spec = pl.BlockSpec((tk, tm), lambda i, j, k: (k, i))
        b_spec = pl.BlockSpec((tk, tn), lambda i, j, k: (k, j))
        contract = ((0,), (0,))
    o_spec = pl.BlockSpec((tm, tn), lambda i, j, k: (i + out_first // tm, j))
    assert b_first % (tk if mode == "nn" else tn) == 0 and out_first % tm == 0, (name, tm, tn, tk)
    n_prev = 0 if out_prev is None else 1

    def body(a_ref, b_ref, *rest):
        o_ref = rest[n_prev]
        part = _dot(a_ref[...].astype(BF16), b_ref[...].astype(BF16), contract)
        if gk == 1:
            o_ref[...] = part.astype(out_dtype)
            return
        acc_ref = rest[n_prev + 1]
        k = pl.program_id(2)

        @pl.when(k == 0)
        def _():
            acc_ref[...] = part

        @pl.when(k > 0)
        def _():
            acc_ref[...] += part

        @pl.when(k == gk - 1)
        def _():
            o_ref[...] = acc_ref[...].astype(out_dtype)

    body, dep_specs, dep_args = _with_deps(body, 2 + n_prev, deps)
    return pl.pallas_call(
        body,
        name=name,
        grid=(gm, gn, gk),
        in_specs=[a_spec, b_spec] + [ANY] * n_prev + dep_specs,
        out_specs=o_spec,
        out_shape=jax.ShapeDtypeStruct((out_total, N), out_dtype),
        input_output_aliases={2: 0} if n_prev else {},
        scratch_shapes=[] if gk == 1 else [pltpu.VMEM((tm, tn), F32)],
        compiler_params=_params(("parallel", "parallel", "arbitrary")),
    )(a, b, *([out_prev] if n_prev else []), *dep_args)


def _mm_tn_parts(parts, at, b, *, name):
    K, N = b.shape
    n = len(parts)
    tm = math.gcd(*[p.shape[1] for p in parts], *at)
    tiles = [p.shape[1] // tm for p in parts]
    first = [sum(tiles[:p]) for p in range(n)]

    def mine(i, p):
        return jnp.logical_and(i >= first[p], i < first[p] + tiles[p])

    def out_tile(i):
        t = 0
        for p in range(n):
            t = jnp.where(mine(i, p), at[p] // tm + i - first[p], t)
        return t

    def body(*refs):
        a_refs, b_ref, o_ref = refs[:n], refs[n], refs[n + 1]
        for p in range(n):
            @pl.when(mine(pl.program_id(0), p))
            def _(p=p):
                o_ref[...] = _dot_tn(a_refs[p][...], b_ref[...])

    return pl.pallas_call(
        body, name=name, grid=(sum(tiles),),
        in_specs=[pl.BlockSpec((K, tm), lambda i, p=p: (0, jnp.clip(i - first[p], 0, tiles[p] - 1))) for p in range(n)]
        + [pl.BlockSpec((K, N), lambda i: (0, 0), pipeline_mode=pl.Buffered(1))],
        out_specs=pl.BlockSpec((tm, N), lambda i: (out_tile(i), 0)),
        out_shape=jax.ShapeDtypeStruct((sum(p.shape[1] for p in parts), N), F32),
        compiler_params=_params(("arbitrary",)),
    )(*parts, b)


def _mm_rows(a, b, *, mode, fn, out_dtypes, rows=(), vecs=(), reduce=False, name, deps=(), b_rows=(0, None), a_at=None):
    parts = a if a_at is not None else (a,)
    starts = a_at if a_at is not None else (0,)
    n_parts = len(parts)
    M, K = parts[0].shape[0], sum(p.shape[1] for p in parts)
    b_first, b_count = b_rows[0], (b.shape[0] if b_rows[1] is None else b_rows[1])
    N = b.shape[1] if mode == "nn" else b_count
    contract = ((1,), (0,)) if mode == "nn" else ((1,), (1,))
    n_rows, n_vecs, n_out = len(rows), len(vecs), len(out_dtypes)
    out_bytes = sum(jnp.dtype(d).itemsize for d in out_dtypes)
    tm = max(t for t in _divisors(M, LANES, MM_MAX_TILE)
             if 4 * t * K + 2 * K * N + 2 * t * N * (4 * n_rows + out_bytes) <= MM_VMEM_BUDGET)
    assert b_first % b_count == 0 and (a_at is None or mode == "nn")

    def body(*refs):
        a_refs, b_ref, rest = refs[:n_parts], refs[n_parts], refs[n_parts + 1:]
        row_refs, vec_refs = rest[:n_rows], rest[n_rows:n_rows + n_vecs]
        out_refs = rest[n_rows + n_vecs:]
        if a_at is None:
            acc = _dot(a_refs[0][...], b_ref[...], contract)
        else:
            acc = sum(_dot(r[...], b_ref[at:at + r.shape[1], :], contract) for r, at in zip(a_refs, starts))
        res = fn(acc, *[r[...] for r in row_refs], *[v[...] for v in vec_refs])
        for o_ref, val in zip(out_refs[:n_out], res):
            o_ref[...] = val.astype(o_ref.dtype)
        if reduce:
            @pl.when(pl.program_id(0) == 0)
            def _():
                out_refs[n_out][...] = res[n_out]

            @pl.when(pl.program_id(0) > 0)
            def _():
                out_refs[n_out][...] += res[n_out]

    row = pl.BlockSpec((tm, N), lambda i: (i, 0))
    vec = pl.BlockSpec((1, N), lambda i: (0, 0))
    body, dep_specs, dep_args = _with_deps(body, n_parts + 1 + n_rows + n_vecs, deps)
    return pl.pallas_call(
        body, name=name, grid=(M // tm,),
        in_specs=[pl.BlockSpec((tm, p.shape[1]), lambda i: (i, 0)) for p in parts]
        + [pl.BlockSpec((b_count, b.shape[1]), lambda i: (b_first // b_count, 0), pipeline_mode=pl.Buffered(1))]
        + [row] * n_rows + [vec] * n_vecs + dep_specs,
        out_specs=[row] * n_out + [vec] * reduce,
        out_shape=[jax.ShapeDtypeStruct((M, N), d) for d in out_dtypes] + [jax.ShapeDtypeStruct((1, N), F32)] * reduce,
        compiler_params=_params(("arbitrary",)),
    )(*parts, b, *rows, *[v.reshape(1, N) for v in vecs], *dep_args)


def _rms(x, gain):
    return x * lax.rsqrt(jnp.mean(x * x, axis=-1, keepdims=True) + NORM_EPS) * gain


def _residual_then_norm(acc, x, gain):
    x_out = x + acc
    return x_out, _rms(x_out, gain)


def _residual_then_loss(acc, x, target):
    err = (x + acc) - target
    dy = err * (1.0 / D_MODEL)
    return dy, dy, jnp.sum(err * err, axis=0, keepdims=True) * (0.5 / D_MODEL)


def _rms_bwd_rows(dh, x, dres, gain):
    r = lax.rsqrt(jnp.mean(x * x, axis=-1, keepdims=True) + NORM_EPS)
    xh = x * r
    dxh = dh * gain
    dx = dres + r * (dxh - xh * jnp.mean(dxh * xh, axis=-1, keepdims=True))
    return dx, dx, jnp.sum(dh * xh, axis=0, keepdims=True)


def _rms_fwd(x, gain, *, name, tm=512, deps=()):
    T, D = x.shape

    def body(x_ref, g_ref, h_ref):
        xv = x_ref[...]
        r = lax.rsqrt(jnp.mean(xv * xv, axis=-1, keepdims=True) + NORM_EPS)
        h_ref[...] = (xv * r * g_ref[...]).astype(BF16)

    body, dep_specs, dep_args = _with_deps(body, 2, deps)
    return pl.pallas_call(
        body, name=name, grid=(T // tm,),
        in_specs=[pl.BlockSpec((tm, D), lambda i: (i, 0)), pl.BlockSpec((1, D), lambda i: (0, 0))] + dep_specs,
        out_specs=pl.BlockSpec((tm, D), lambda i: (i, 0)),
        out_shape=jax.ShapeDtypeStruct((T, D), BF16),
        compiler_params=_params(("parallel",)),
    )(x, gain.reshape(1, D), *dep_args)


def _head_norm(x, gain2, lo):
    ms = _half_sums(x * x, lo) * (1.0 / HEAD_DIM)
    r = lax.rsqrt(ms + NORM_EPS)
    xh = x * r
    return xh * gain2, xh, r


def _head_norm_bwd(xh, r, gain2, dy, lo):
    dxh = dy * gain2
    dx = r * (dxh - xh * (_half_sums(dxh * xh, lo) * (1.0 / HEAD_DIM)))
    return dx, dy * xh


Q_GROUP = N_Q_HEADS // 2
GROUP_ROWS = Q_GROUP * BLOCK
ATT_SCRATCH = (pltpu.VMEM((2, 2, GROUP_ROWS, BLOCK), F32), pltpu.VMEM((2, GROUP_ROWS, 1), F32))


def _att_consts(sink_ref, bias_ref, sinkcol_ref):
    row = lax.broadcasted_iota(jnp.int32, (GROUP_ROWS, BLOCK), 0)
    kj = lax.broadcasted_iota(jnp.int32, (GROUP_ROWS, BLOCK), 1)
    head = row // BLOCK
    head_col = lax.broadcasted_iota(jnp.int32, (GROUP_ROWS, 1), 0) // BLOCK
    d_cur = (row % BLOCK) - kj
    d_prev = d_cur + BLOCK
    for kv in range(2):
        slope = jnp.zeros((GROUP_ROWS, BLOCK), F32)
        sink = jnp.zeros((GROUP_ROWS, 1), F32)
        for r in range(Q_GROUP):
            slope = jnp.where(head == r, ALIBI_SLOPES[Q_GROUP * kv + r], slope)
            sink = jnp.where(head_col == r, sink_ref[Q_GROUP * kv + r], sink)
        bias_ref[kv, 0] = jnp.where(d_cur >= 0, -slope * d_cur.astype(F32), NEG_INF)
        bias_ref[kv, 1] = jnp.where(d_prev < BLOCK, -slope * d_prev.astype(F32), NEG_INF)
        sinkcol_ref[kv] = sink


def _stack_heads(t0, t1, lo):
    z = jnp.zeros_like(t0)
    return jnp.concatenate([jnp.where(lo, t0, z), jnp.where(lo, z, t0), jnp.where(lo, t1, z), jnp.where(lo, z, t1)], axis=0)


def _unstack_heads(x4, lo):
    return (jnp.where(lo, x4[0:BLOCK], x4[BLOCK:2 * BLOCK]), jnp.where(lo, x4[2 * BLOCK:3 * BLOCK], x4[3 * BLOCK:]))


def _att_probs(q4, k2c, k2p, bias_c, bias_p, sink, has_prev):
    s_c = _dot_nt(q4, k2c) * ATT_SCALE + bias_c
    s_p = jnp.where(has_prev, _dot_nt(q4, k2p) * ATT_SCALE + bias_p, NEG_INF)
    m = jnp.maximum(jnp.max(jnp.maximum(s_c, s_p), axis=-1, keepdims=True), sink)
    e_c = jnp.exp(s_c - m)
    e_p = jnp.exp(s_p - m)
    e_s = jnp.exp(sink - m)
    inv = 1.0 / (jnp.sum(e_c + e_p, axis=-1, keepdims=True) + e_s)
    return e_c * inv, e_p * inv, e_s * inv


def _attention_fwd(proj, q_gain, k_gain, sinks, *, n_seq, seq, name):
    T = n_seq * seq
    nb = seq // BLOCK
    qcol, kvcol = COL_QKV // ATT_WIDTH, (COL_QKV + ATT_WIDTH) // (2 * KV_WIDTH)

    def body(q_ref, kv_ref, qg_ref, kg_ref, sink_ref, y_ref, bias_ref, sinkcol_ref):
        lo = _lo_mask((BLOCK, LANES))
        qg, kg = qg_ref[...], kg_ref[...]
        _att_consts(sink_ref, bias_ref, sinkcol_ref)

        def block(i, carry):
            r0 = pl.multiple_of(i * BLOCK, BLOCK)
            rp = pl.multiple_of(jnp.maximum(i - 1, 0) * BLOCK, BLOCK)
            has_prev = i > 0
            kn_c = _head_norm(kv_ref[pl.ds(r0, BLOCK), 0:KV_WIDTH].astype(F32), kg, lo)[0].astype(BF16)
            kn_p = _head_norm(kv_ref[pl.ds(rp, BLOCK), 0:KV_WIDTH].astype(F32), kg, lo)[0].astype(BF16)
            v_c = kv_ref[pl.ds(r0, BLOCK), KV_WIDTH:2 * KV_WIDTH].astype(BF16)
            v_p = kv_ref[pl.ds(rp, BLOCK), KV_WIDTH:2 * KV_WIDTH].astype(BF16)
            for kv in range(2):
                k2c, k2p = _dup_half(kn_c, kv, lo), _dup_half(kn_p, kv, lo)
                v2c, v2p = _dup_half(v_c, kv, lo), _dup_half(v_p, kv, lo)
                cols = [slice((2 * kv + t) * LANES, (2 * kv + t + 1) * LANES) for t in range(2)]
                qn = [_head_norm(q_ref[pl.ds(r0, BLOCK), c].astype(F32), qg, lo)[0] for c in cols]
                q4 = _stack_heads(qn[0], qn[1], lo).astype(BF16)
                p_c, p_p, _ = _att_probs(q4, k2c, k2p, bias_ref[kv, 0], bias_ref[kv, 1], sinkcol_ref[kv], has_prev)
                o4 = _dot_nn(p_c.astype(BF16), v2c) + _dot_nn(p_p.astype(BF16), v2p)
                for c, out in zip(cols, _unstack_heads(o4, lo)):
                    y_ref[pl.ds(r0, BLOCK), c] = out.astype(BF16)
            return carry

        lax.fori_loop(0, nb, block, 0)

    vec = pl.BlockSpec((1, LANES), lambda b: (0, 0))
    return pl.pallas_call(
        body, name=name, grid=(n_seq,),
        in_specs=[pl.BlockSpec((seq, ATT_WIDTH), lambda b: (b, qcol)),
                  pl.BlockSpec((seq, 2 * KV_WIDTH), lambda b: (b, kvcol)),
                  vec, vec, pl.BlockSpec(memory_space=pltpu.SMEM)],
        out_specs=pl.BlockSpec((seq, ATT_WIDTH), lambda b: (b, 0)),
        out_shape=jax.ShapeDtypeStruct((T, ATT_WIDTH), BF16),
        scratch_shapes=list(ATT_SCRATCH),
        compiler_params=_params(("parallel",)),
    )(proj, proj, jnp.tile(q_gain, 2).reshape(1, LANES), jnp.tile(k_gain, 2).reshape(1, LANES), sinks)


def _attention_bwd(proj, dy, q_gain, k_gain, sinks, *, n_seq, seq, name, deps=()):
    T = n_seq * seq
    nb = seq // BLOCK
    qcol, kvcol = COL_QKV // ATT_WIDTH, (COL_QKV + ATT_WIDTH) // (2 * KV_WIDTH)

    def body(q_ref, kv_ref, dy_ref, qg_ref, kg_ref, sink_ref, dqkv_ref, dqg_ref, dkg_ref, dsink_ref,
             dkn_acc, dv_acc, qg_acc, kg_acc, sink_acc, bias_ref, sinkcol_ref):
        lo = _lo_mask((BLOCK, LANES))
        qg, kg = qg_ref[...], kg_ref[...]
        _att_consts(sink_ref, bias_ref, sinkcol_ref)
        first = pl.program_id(0) == 0

        @pl.when(first)
        def _():
            qg_acc[...] = jnp.zeros_like(qg_acc)
            kg_acc[...] = jnp.zeros_like(kg_acc)
            sink_acc[...] = jnp.zeros_like(sink_acc)

        dkn_acc[...] = jnp.zeros_like(dkn_acc)
        dv_acc[...] = jnp.zeros_like(dv_acc)

        def block(i, carry):
            r0 = pl.multiple_of(i * BLOCK, BLOCK)
            rp = pl.multiple_of(jnp.maximum(i - 1, 0) * BLOCK, BLOCK)
            has_prev = i > 0
            kn_c = _head_norm(kv_ref[pl.ds(r0, BLOCK), 0:KV_WIDTH].astype(F32), kg, lo)[0].astype(BF16)
            kn_p = _head_norm(kv_ref[pl.ds(rp, BLOCK), 0:KV_WIDTH].astype(F32), kg, lo)[0].astype(BF16)
            v_c = kv_ref[pl.ds(r0, BLOCK), KV_WIDTH:2 * KV_WIDTH].astype(BF16)
            v_p = kv_ref[pl.ds(rp, BLOCK), KV_WIDTH:2 * KV_WIDTH].astype(BF16)
            dk_c, dk_p, dv_c, dv_p = [], [], [], []
            for kv in range(2):
                k2c, k2p = _dup_half(kn_c, kv, lo), _dup_half(kn_p, kv, lo)
                v2c, v2p = _dup_half(v_c, kv, lo), _dup_half(v_p, kv, lo)
                cols = [slice((2 * kv + t) * LANES, (2 * kv + t + 1) * LANES) for t in range(2)]
                normed = [_head_norm(q_ref[pl.ds(r0, BLOCK), c].astype(F32), qg, lo) for c in cols]
                q4 = _stack_heads(normed[0][0], normed[1][0], lo).astype(BF16)
                do4 = _stack_heads(dy_ref[pl.ds(r0, BLOCK), cols[0]], dy_ref[pl.ds(r0, BLOCK), cols[1]], lo)
                p_c, p_p, p_s = _att_probs(q4, k2c, k2p, bias_ref[kv, 0], bias_ref[kv, 1], sinkcol_ref[kv], has_prev)
                dp_c = _dot_nt(do4, v2c)
                dp_p = _dot_nt(do4, v2p)
                delta = jnp.sum(p_c * dp_c + p_p * dp_p, axis=-1, keepdims=True)
                ds_c = (p_c * (dp_c - delta)).astype(BF16)
                ds_p = (p_p * (dp_p - delta)).astype(BF16)
                sink_acc[kv] += -(p_s * delta)
                dq4 = (_dot_nn(ds_c, k2c) + _dot_nn(ds_p, k2p)) * ATT_SCALE
                for c, (_, qh, qr), dqn in zip(cols, normed, _unstack_heads(dq4, lo)):
                    dq, dg = _head_norm_bwd(qh, qr, qg, dqn, lo)
                    dqkv_ref[pl.ds(r0, BLOCK), c] = dq.astype(BF16)
                    qg_acc[...] += dg
                dk_c.append(_dot_tn(ds_c, q4))
                dk_p.append(_dot_tn(ds_p, q4))
                dv_c.append(_dot_tn(p_c.astype(BF16), do4))
                dv_p.append(_dot_tn(p_p.astype(BF16), do4))

            def fold(parts):
                a = parts[0] + pltpu.roll(parts[0], LANES // 2, axis=1)
                b = parts[1] + pltpu.roll(parts[1], LANES // 2, axis=1)
                return jnp.where(lo, a, b)

            dkn_acc[pl.ds(r0, BLOCK), :] += fold(dk_c) * ATT_SCALE
            dkn_acc[pl.ds(rp, BLOCK), :] += fold(dk_p) * ATT_SCALE
            dv_acc[pl.ds(r0, BLOCK), :] += fold(dv_c)
            dv_acc[pl.ds(rp, BLOCK), :] += fold(dv_p)
            return carry

        lax.fori_loop(0, nb, block, 0)

        def finish(i, carry):
            r0 = pl.multiple_of(i * BLOCK, BLOCK)
            _, kh, kr = _head_norm(kv_ref[pl.ds(r0, BLOCK), 0:KV_WIDTH].astype(F32), kg, lo)
            dk, dg = _head_norm_bwd(kh, kr, kg, dkn_acc[pl.ds(r0, BLOCK), :], lo)
            dqkv_ref[pl.ds(r0, BLOCK), ATT_WIDTH:ATT_WIDTH + KV_WIDTH] = dk.astype(BF16)
            dqkv_ref[pl.ds(r0, BLOCK), ATT_WIDTH + KV_WIDTH:QKV_WIDTH] = dv_acc[pl.ds(r0, BLOCK), :].astype(BF16)
            kg_acc[...] += dg
            return carry

        lax.fori_loop(0, nb, finish, 0)

        @pl.when(pl.program_id(0) == n_seq - 1)
        def _():
            dqg_ref[...] = jnp.sum(qg_acc[...], axis=0, keepdims=True)
            dkg_ref[...] = jnp.sum(kg_acc[...], axis=0, keepdims=True)
            lane = lax.broadcasted_iota(jnp.int32, (1, LANES), 1)
            dsink = jnp.zeros((1, LANES), F32)
            for kv in range(2):
                for r in range(Q_GROUP):
                    total = jnp.sum(sink_acc[kv, r * BLOCK:(r + 1) * BLOCK, :], axis=0, keepdims=True)
                    dsink = jnp.where(lane == Q_GROUP * kv + r, total, dsink)
            dsink_ref[...] = dsink

    vec = pl.BlockSpec((1, LANES), lambda b: (0, 0))
    acc = pltpu.VMEM((BLOCK, LANES), F32)
    body, dep_specs, dep_args = _with_deps(body, 6, deps)
    dqkv, dqg, dkg, dsink = pl.pallas_call(
        body, name=name, grid=(n_seq,),
        in_specs=[pl.BlockSpec((seq, ATT_WIDTH), lambda b: (b, qcol)),
                  pl.BlockSpec((seq, 2 * KV_WIDTH), lambda b: (b, kvcol)),
                  pl.BlockSpec((seq, ATT_WIDTH), lambda b: (b, 0)),
                  vec, vec, pl.BlockSpec(memory_space=pltpu.SMEM)] + dep_specs,
        out_specs=[pl.BlockSpec((seq, QKV_WIDTH), lambda b: (b, 0)), vec, vec, vec],
        out_shape=[jax.ShapeDtypeStruct((T, QKV_WIDTH), BF16)] + [jax.ShapeDtypeStruct((1, LANES), F32)] * 3,
        scratch_shapes=[pltpu.VMEM((seq, KV_WIDTH), F32), pltpu.VMEM((seq, KV_WIDTH), F32), acc, acc,
                        pltpu.VMEM((2, GROUP_ROWS, 1), F32), *ATT_SCRATCH],
        compiler_params=_params(("arbitrary",)),
    )(proj, proj, dy, jnp.tile(q_gain, 2).reshape(1, LANES), jnp.tile(k_gain, 2).reshape(1, LANES), sinks, *dep_args)
    half = LANES // 2
    return dqkv, dqg[0, :half] + dqg[0, half:], dkg[0, :half] + dkg[0, half:], dsink[0, :N_Q_HEADS]


def _sgu_weights(w_ref):
    r = lax.broadcasted_iota(jnp.int32, (BLOCK, BLOCK), 0)
    c = lax.broadcasted_iota(jnp.int32, (BLOCK, BLOCK), 1)
    return [jnp.where(r >= c, w_ref[g], 0.0).astype(BF16) for g in range(SGU_GROUPS)]


def _sgu_fwd(proj, gain, w_s, bias_full, *, n_seq, seq, name):
    T = n_seq * seq
    nc = seq // BLOCK

    def body(suv_ref, g_ref, w_ref, b_ref, y_ref):
        lo = _lo_mask((BLOCK, LANES))
        wm = _sgu_weights(w_ref)
        gain_v = g_ref[...]

        def chunk(c, carry):
            r0 = pl.multiple_of(c * BLOCK, BLOCK)
            gv = _gelu(suv_ref[pl.ds(r0, BLOCK), SGU_WIDTH:2 * SGU_WIDTH].astype(F32))
            r = lax.rsqrt(jnp.mean(gv * gv, axis=-1, keepdims=True) + NORM_EPS)
            vn = (gv * r * gain_v).astype(BF16)
            for p in range(SGU_WIDTH // LANES):
                cols = slice(p * LANES, (p + 1) * LANES)
                vp = vn[:, cols]
                mixed = jnp.where(lo, _dot_nn(wm[2 * p], vp), _dot_nn(wm[2 * p + 1], vp)) + b_ref[:, cols]
                u = _gelu(suv_ref[pl.ds(r0, BLOCK), cols].astype(F32))
                y_ref[pl.ds(r0, BLOCK), cols] = (u * mixed).astype(BF16)
            return carry

        lax.fori_loop(0, nc, chunk, 0)

    return pl.pallas_call(
        body, name=name, grid=(n_seq,),
        in_specs=[pl.BlockSpec((seq, 2 * SGU_WIDTH), lambda b: (b, COL_SUV // (2 * SGU_WIDTH))),
                  pl.BlockSpec((1, SGU_WIDTH), lambda b: (0, 0)),
                  pl.BlockSpec((SGU_GROUPS, BLOCK, BLOCK), lambda b: (0, 0, 0)),
                  pl.BlockSpec((BLOCK, SGU_WIDTH), lambda b: (0, 0))],
        out_specs=pl.BlockSpec((seq, SGU_WIDTH), lambda b: (b, 0)),
        out_shape=jax.ShapeDtypeStruct((T, SGU_WIDTH), BF16),
        compiler_params=_params(("parallel",)),
    )(proj, gain.reshape(1, SGU_WIDTH), w_s, bias_full)


def _sgu_bwd(proj, dy, gain, w_s, bias_full, *, n_seq, seq, name, deps=()):
    T = n_seq * seq
    nc = seq // BLOCK
    n_tiles = SGU_WIDTH // LANES

    def body(suv_ref, dy_ref, g_ref, w_ref, b_ref, dsuv_ref, dg_ref, dw_ref, db_ref, dg_acc, dw_acc, db_acc):
        lo = _lo_mask((BLOCK, LANES))
        hi = jnp.logical_not(lo)
        wm = _sgu_weights(w_ref)
        wmt = [jnp.where(lax.broadcasted_iota(jnp.int32, (BLOCK, BLOCK), 1) >= lax.broadcasted_iota(jnp.int32, (BLOCK, BLOCK), 0),
                         w_ref[g].T, 0.0).astype(BF16) for g in range(SGU_GROUPS)]
        gain_v = g_ref[...]

        @pl.when(pl.program_id(0) == 0)
        def _():
            dg_acc[...] = jnp.zeros_like(dg_acc)
            dw_acc[...] = jnp.zeros_like(dw_acc)
            db_acc[...] = jnp.zeros_like(db_acc)

        def chunk(c, carry):
            r0 = pl.multiple_of(c * BLOCK, BLOCK)
            gv, dgelu_v = _gelu_and_grad(suv_ref[pl.ds(r0, BLOCK), SGU_WIDTH:2 * SGU_WIDTH].astype(F32))
            r = lax.rsqrt(jnp.mean(gv * gv, axis=-1, keepdims=True) + NORM_EPS)
            vh = gv * r
            vn = (vh * gain_v).astype(BF16)
            dvn_tiles = []
            for p in range(n_tiles):
                cols = slice(p * LANES, (p + 1) * LANES)
                vp = vn[:, cols]
                mixed = jnp.where(lo, _dot_nn(wm[2 * p], vp), _dot_nn(wm[2 * p + 1], vp)) + b_ref[:, cols]
                u, dgelu_u = _gelu_and_grad(suv_ref[pl.ds(r0, BLOCK), cols].astype(F32))
                dyv = dy_ref[pl.ds(r0, BLOCK), cols]
                dsuv_ref[pl.ds(r0, BLOCK), cols] = (dyv * mixed * dgelu_u).astype(BF16)
                dm = dyv * u
                db_acc[:, cols] += dm
                dm_bf = dm.astype(BF16)
                dvn_tiles.append(jnp.where(lo, _dot_nn(wmt[2 * p], dm_bf), _dot_nn(wmt[2 * p + 1], dm_bf)))
                dw_acc[2 * p] += _dot_nt(jnp.where(lo, dm, 0.0).astype(BF16), vp)
                dw_acc[2 * p + 1] += _dot_nt(jnp.where(hi, dm, 0.0).astype(BF16), vp)
            dvn = jnp.concatenate(dvn_tiles, axis=1)
            dg_acc[...] += dvn * vh
            dvh = dvn * gain_v
            dgv = r * (dvh - vh * jnp.mean(dvh * vh, axis=-1, keepdims=True))
            dsuv_ref[pl.ds(r0, BLOCK), SGU_WIDTH:2 * SGU_WIDTH] = (dgv * dgelu_v).astype(BF16)
            return carry

        lax.fori_loop(0, nc, chunk, 0)

        @pl.when(pl.program_id(0) == n_seq - 1)
        def _():
            dg_ref[...] = jnp.sum(dg_acc[...], axis=0, keepdims=True)
            r = lax.broadcasted_iota(jnp.int32, (BLOCK, BLOCK), 0)
            c = lax.broadcasted_iota(jnp.int32, (BLOCK, BLOCK), 1)
            for g in range(SGU_GROUPS):
                dw_ref[g] = jnp.where(r >= c, dw_acc[g], 0.0)
            lane = lax.broadcasted_iota(jnp.int32, (BLOCK, LANES), 1)
            out = jnp.zeros((BLOCK, LANES), F32)
            for p in range(n_tiles):
                tile = db_acc[:, p * LANES:(p + 1) * LANES]
                s_lo = jnp.sum(jnp.where(lo, tile, 0.0), axis=-1, keepdims=True)
                s_hi = jnp.sum(jnp.where(hi, tile, 0.0), axis=-1, keepdims=True)
                out = jnp.where(lane == 2 * p, s_lo, out)
                out = jnp.where(lane == 2 * p + 1, s_hi, out)
            db_ref[...] = out

    body, dep_specs, dep_args = _with_deps(body, 5, deps)
    dsuv, dg, dw, db = pl.pallas_call(
        body, name=name, grid=(n_seq,),
        in_specs=[pl.BlockSpec((seq, 2 * SGU_WIDTH), lambda b: (b, COL_SUV // (2 * SGU_WIDTH))),
                  pl.BlockSpec((seq, SGU_WIDTH), lambda b: (b, 0)),
                  pl.BlockSpec((1, SGU_WIDTH), lambda b: (0, 0)),
                  pl.BlockSpec((SGU_GROUPS, BLOCK, BLOCK), lambda b: (0, 0, 0)),
                  pl.BlockSpec((BLOCK, SGU_WIDTH), lambda b: (0, 0))] + dep_specs,
        out_specs=[pl.BlockSpec((seq, 2 * SGU_WIDTH), lambda b: (b, 0)),
                   pl.BlockSpec((1, SGU_WIDTH), lambda b: (0, 0)),
                   pl.BlockSpec((SGU_GROUPS, BLOCK, BLOCK), lambda b: (0, 0, 0)),
                   pl.BlockSpec((BLOCK, LANES), lambda b: (0, 0))],
        out_shape=[jax.ShapeDtypeStruct((T, 2 * SGU_WIDTH), BF16), jax.ShapeDtypeStruct((1, SGU_WIDTH), F32),
                   jax.ShapeDtypeStruct((SGU_GROUPS, BLOCK, BLOCK), F32), jax.ShapeDtypeStruct((BLOCK, LANES), F32)],
        scratch_shapes=[pltpu.VMEM((BLOCK, SGU_WIDTH), F32), pltpu.VMEM((SGU_GROUPS, BLOCK, BLOCK), F32),
                        pltpu.VMEM((BLOCK, SGU_WIDTH), F32)],
        compiler_params=_params(("arbitrary",)),
    )(proj, dy, gain.reshape(1, SGU_WIDTH), w_s, bias_full, *dep_args)
    return dsuv, dg.reshape(SGU_WIDTH), dw, db[:, :SGU_GROUPS].T


def _merge_fwd(y_att, y_sgu, w_oa, w_ob, proj, *, name, tm=1024, tn=512, deps=()):
    T = y_att.shape[0]

    def body(ya_ref, ys_ref, wa_ref, wb_ref, ga_ref, gb_ref, o_ref):
        pa = _dot_nn(ya_ref[...], wa_ref[...])
        pb = _dot_nn(ys_ref[...], wb_ref[...])
        o_ref[...] = (_sigmoid(ga_ref[...].astype(F32)) * pa + _sigmoid(gb_ref[...].astype(F32)) * pb).astype(BF16)

    act = pl.BlockSpec((tm, ATT_WIDTH), lambda i, j: (i, 0))
    wgt = pl.BlockSpec((ATT_WIDTH, tn), lambda i, j: (0, j))
    body, dep_specs, dep_args = _with_deps(body, 6, deps)
    return pl.pallas_call(
        body, name=name, grid=(T // tm, D_MODEL // tn),
        in_specs=[act, act, wgt, wgt,
                  pl.BlockSpec((tm, tn), lambda i, j: (i, j + COL_GA // tn)),
                  pl.BlockSpec((tm, tn), lambda i, j: (i, j + COL_GB // tn))] + dep_specs,
        out_specs=pl.BlockSpec((tm, tn), lambda i, j: (i, j)),
        out_shape=jax.ShapeDtypeStruct((T, D_MODEL), BF16),
        compiler_params=_params(("parallel", "parallel")),
    )(y_att, y_sgu, w_oa, w_ob, proj, proj, *dep_args)


def _merge_bwd(dx1_bf, w_out, y_att, y_sgu, w_oa, w_ob, proj, *, name, tm=1024, tn=512):
    T = y_att.shape[0]

    def body(dx_ref, wo_ref, ya_ref, ys_ref, wa_ref, wb_ref, ga_ref, gb_ref, dpa_ref, dpb_ref, dga_ref, dgb_ref):
        dm = _dot_nt(dx_ref[...], wo_ref[...])
        pa = _dot_nn(ya_ref[...], wa_ref[...])
        pb = _dot_nn(ys_ref[...], wb_ref[...])
        sa = _sigmoid(ga_ref[...].astype(F32))
        sb = _sigmoid(gb_ref[...].astype(F32))
        dpa_ref[...] = (dm * sa).astype(BF16)
        dpb_ref[...] = (dm * sb).astype(BF16)
        dga_ref[...] = (dm * pa * sa * (1.0 - sa)).astype(BF16)
        dgb_ref[...] = (dm * pb * sb * (1.0 - sb)).astype(BF16)

    act = pl.BlockSpec((tm, ATT_WIDTH), lambda i, j: (i, 0))
    wgt = pl.BlockSpec((ATT_WIDTH, tn), lambda i, j: (0, j))
    out = pl.BlockSpec((tm, tn), lambda i, j: (i, j))
    return pl.pallas_call(
        body, name=name, grid=(T // tm, D_MODEL // tn),
        in_specs=[pl.BlockSpec((tm, D_MODEL), lambda i, j: (i, 0)),
                  pl.BlockSpec((tn, D_MODEL), lambda i, j: (j, 0)),
                  act, act, wgt, wgt,
                  pl.BlockSpec((tm, tn), lambda i, j: (i, j + COL_GA // tn)),
                  pl.BlockSpec((tm, tn), lambda i, j: (i, j + COL_GB // tn))],
        out_specs=[out] * 4,
        out_shape=[jax.ShapeDtypeStruct((T, D_MODEL), BF16)] * 4,
        compiler_params=_params(("parallel", "parallel")),
    )(dx1_bf, w_out, y_att, y_sgu, w_oa, w_ob, proj, proj)


CONV_ROWS = 256
CONV_TN = 256


def _shift_rows(cur, prev8, k):
    rolled = pltpu.roll(cur, k, axis=0)
    head = jnp.where(lax.broadcasted_iota(jnp.int32, prev8.shape, 0) < k, pltpu.roll(prev8, k, axis=0), rolled[:SUBLANES])
    return jnp.concatenate([head, rolled[SUBLANES:]], axis=0)


def _shift_rows_up(cur, next8, k):
    n = cur.shape[0]
    rolled = pltpu.roll(cur, n - k, axis=0)
    tail = jnp.where(lax.broadcasted_iota(jnp.int32, next8.shape, 0) >= SUBLANES - k,
                     pltpu.roll(next8, SUBLANES - k, axis=0), rolled[n - SUBLANES:])
    return jnp.concatenate([rolled[:n - SUBLANES], tail], axis=0)


def _up_conv_fwd(h2, w_up_t, cw_g, cw_v, cb_g, cb_v, *, n_seq, seq, name, deps=()):
    T = n_seq * seq
    tn, rows = CONV_TN, CONV_ROWS

    def body(h_ref, ug_ref, uv_ref, wg_ref, wv_ref, bg_ref, bv_ref, a_ref, zg_ref, zv_ref, cg_ref, cv_ref):
        def conv(cur, prev8, w_ref, b_ref):
            z1 = _shift_rows(cur, prev8, 1)
            z2 = _shift_rows(cur, prev8, 2)
            return b_ref[...] + w_ref[0:1, :] * z2 + w_ref[1:2, :] * z1 + w_ref[2:3, :] * cur

        start = jnp.zeros((SUBLANES, tn), F32)
        prev = (start, start)
        for s in range(seq // rows):
            r = pl.ds(s * rows, rows)
            h = h_ref[r, :]
            zg = _dot_nt(h, ug_ref[...])
            zv = _dot_nt(h, uv_ref[...])
            zg_ref[r, :] = zg.astype(ACT_DTYPE)
            zv_ref[r, :] = zv.astype(ACT_DTYPE)
            g = conv(zg, prev[0], wg_ref, bg_ref)
            v = conv(zv, prev[1], wv_ref, bv_ref)
            a_ref[r, :] = (g * _sigmoid(g) * v).astype(BF16)
            cg_ref[r, :] = g.astype(ACT_DTYPE)
            cv_ref[r, :] = v.astype(ACT_DTYPE)
            prev = (zg[rows - SUBLANES:], zv[rows - SUBLANES:])

    zs = pl.BlockSpec((seq, tn), lambda b, j: (b, j))
    ws = pl.BlockSpec((3, tn), lambda b, j: (0, j))
    bs = pl.BlockSpec((1, tn), lambda b, j: (0, j))
    body, dep_specs, dep_args = _with_deps(body, 7, deps)
    return pl.pallas_call(
        body, name=name, grid=(n_seq, D_FF // tn),
        in_specs=[pl.BlockSpec((seq, D_MODEL), lambda b, j: (b, 0)),
                  pl.BlockSpec((tn, D_MODEL), lambda b, j: (j, 0)),
                  pl.BlockSpec((tn, D_MODEL), lambda b, j: (j + D_FF // tn, 0)), ws, ws, bs, bs] + dep_specs,
        out_specs=[zs] * 5,
        out_shape=[jax.ShapeDtypeStruct((T, D_FF), BF16)] + [jax.ShapeDtypeStruct((T, D_FF), ACT_DTYPE)] * 4,
        compiler_params=_params(("parallel", "parallel")),
    )(h2, w_up_t, w_up_t, cw_g, cw_v, cb_g.reshape(1, D_FF), cb_v.reshape(1, D_FF), *dep_args)


def _conv_bwd(z_g, z_v, c_g, c_v, dx2_bf, w_down, cw_g, cw_v, *, n_seq, seq, name):
    T = n_seq * seq
    tn, rows = CONV_TN, CONV_ROWS
    n_steps = seq // rows

    def body(zg_ref, zv_ref, cg_ref, cv_ref, dx_ref, wd_ref, wg_ref, wv_ref,
             dzg_ref, dzv_ref, dwg_ref, dwv_ref, dbg_ref, dbv_ref, dcg_ref, dcv_ref):
        def colsum(x):
            return jnp.sum(x, axis=0, keepdims=True)

        zero = jnp.zeros((1, tn), F32)
        db = (zero, zero)
        for s in range(n_steps):
            r = pl.ds(s * rows, rows)
            g = cg_ref[r, :].astype(F32)
            v = cv_ref[r, :].astype(F32)
            sg = _sigmoid(g)
            dav = _dot_nt(dx_ref[r, :], wd_ref[...])
            dcg = dav * v * (sg * (1.0 + g * (1.0 - sg)))
            dcv = dav * (g * sg)
            dcg_ref[r, :] = dcg
            dcv_ref[r, :] = dcv
            db = (db[0] + colsum(dcg), db[1] + colsum(dcv))

        def back(s, accs):
            r0 = pl.multiple_of(s * rows, rows)
            last = s == n_steps - 1
            rn = pl.multiple_of(jnp.minimum(r0 + rows, seq - SUBLANES), SUBLANES)
            new = []
            for half, (dc_ref, w_ref, dz_ref, z_ref) in enumerate(((dcg_ref, wg_ref, dzg_ref, zg_ref),
                                                                   (dcv_ref, wv_ref, dzv_ref, zv_ref))):
                cur = dc_ref[pl.ds(r0, rows), :]
                nxt = jnp.where(last, 0.0, dc_ref[pl.ds(rn, SUBLANES), :])
                u1, u2 = _shift_rows_up(cur, nxt, 1), _shift_rows_up(cur, nxt, 2)
                dz_ref[pl.ds(r0, rows), :] = (w_ref[2:3, :] * cur + w_ref[1:2, :] * u1 + w_ref[0:1, :] * u2).astype(BF16)
                z = z_ref[pl.ds(r0, rows), :].astype(F32)
                new += [accs[3 * half] + colsum(u2 * z), accs[3 * half + 1] + colsum(u1 * z),
                        accs[3 * half + 2] + colsum(cur * z)]
            return tuple(new)

        dw = lax.fori_loop(0, n_steps, back, (zero,) * 6)
        first_seq = pl.program_id(1) == 0

        @pl.when(first_seq)
        def _():
            dwg_ref[...] = jnp.concatenate(dw[0:3], axis=0)
            dwv_ref[...] = jnp.concatenate(dw[3:6], axis=0)
            dbg_ref[...], dbv_ref[...] = db

        @pl.when(jnp.logical_not(first_seq))
        def _():
            dwg_ref[...] += jnp.concatenate(dw[0:3], axis=0)
            dwv_ref[...] += jnp.concatenate(dw[3:6], axis=0)
            dbg_ref[...] += db[0]
            dbv_ref[...] += db[1]

    zs = pl.BlockSpec((seq, tn), lambda j, b: (b, j))
    ws = pl.BlockSpec((3, tn), lambda j, b: (0, j))
    bs = pl.BlockSpec((1, tn), lambda j, b: (0, j))
    outs = pl.pallas_call(
        body, name=name, grid=(D_FF // tn, n_seq),
        in_specs=[zs] * 4 + [pl.BlockSpec((seq, D_MODEL), lambda j, b: (b, 0)),
                             pl.BlockSpec((tn, D_MODEL), lambda j, b: (j, 0)), ws, ws],
        out_specs=[zs, zs, ws, ws, bs, bs],
        out_shape=[jax.ShapeDtypeStruct((T, D_FF), BF16)] * 2 + [jax.ShapeDtypeStruct((3, D_FF), F32)] * 2
        + [jax.ShapeDtypeStruct((1, D_FF), F32)] * 2,
        scratch_shapes=[pltpu.VMEM((seq, tn), F32), pltpu.VMEM((seq, tn), F32)],
        compiler_params=_params(("parallel", "arbitrary")),
    )(z_g, z_v, c_g, c_v, dx2_bf, w_down, cw_g, cw_v)
    dz_g, dz_v, dw_g, dw_v, db_g, db_v = outs
    return dz_g, dz_v, dw_g, dw_v, db_g.reshape(D_FF), db_v.reshape(D_FF)


def _layer_fwd(x, h, w, sched, tail, *, n_seq, seq, l):
    tag = f"l{l}"
    deps = sched("fwd_start", l, h)
    proj = _mm(h, w["w_in_t"], mode="nt", out_dtype=ACT_DTYPE, rotate=W_IN_ROTATE, name=f"{tag}_proj", deps=deps)
    y_att = _attention_fwd(proj, w["q_norm"], w["k_norm"], w["sinks"], n_seq=n_seq, seq=seq, name=f"{tag}_att")
    deps = sched("fwd_att", l, y_att)
    y_sgu = _sgu_fwd(proj, w["sgu_norm"], w["w_s"], w["bias_full"], n_seq=n_seq, seq=seq, name=f"{tag}_sgu")
    merged = _merge_fwd(y_att, y_sgu, w["w_oa"], w["w_ob"], proj, name=f"{tag}_merge", deps=deps)
    x1, h2 = _mm_rows(merged, w["w_out"], mode="nn", fn=_residual_then_norm, out_dtypes=(F32, BF16), rows=(x,),
                      vecs=(w["ffn_norm"],), name=f"{tag}_out")
    deps = sched("fwd_mixer_done", l, x1)
    a, z_g, z_v, c_g, c_v = _up_conv_fwd(h2, w["w_up_t"], w["cw_g"], w["cw_v"], w["cb_g"], w["cb_v"], n_seq=n_seq,
                                         seq=seq, name=f"{tag}_up_conv", deps=deps)
    deps = sched("fwd_conv", l, a)
    if tail[0] == "norm":
        out = _mm_rows(a, w["w_down"], mode="nn", fn=_residual_then_norm, out_dtypes=(F32, BF16), rows=(x1,),
                       vecs=(tail[1],), name=f"{tag}_down", deps=deps)
    else:
        out = _mm_rows(a, w["w_down"], mode="nn", fn=_residual_then_loss, out_dtypes=(F32, BF16), rows=(x1, tail[1]),
                       reduce=True, name=f"{tag}_down", deps=deps)
    saved = dict(x=x, h=h, proj=proj, y_att=y_att, y_sgu=y_sgu, merged=merged, x1=x1, h2=h2, z_g=z_g, z_v=z_v,
                 c_g=c_g, c_v=c_v, a=a)
    return out, saved


def _layer_bwd(dx2, dx2_bf, w, s, sched, deps, *, n_seq, seq, l):
    tag = f"l{l}b"
    g = {}
    g["w_down"] = _mm(s["a"], dx2_bf, mode="tn", out_dtype=F32, name=f"{tag}_dw_down", deps=deps)
    dz_g, dz_v, g["cw_g"], g["cw_v"], g["cb_g"], g["cb_v"] = _conv_bwd(
        s["z_g"], s["z_v"], s["c_g"], s["c_v"], dx2_bf, w["w_down"], w["cw_g"], w["cw_v"], n_seq=n_seq, seq=seq,
        name=f"{tag}_conv")
    dw_up_t = _mm(dz_g, s["h2"], mode="tn", out_dtype=F32, out_rows=(0, 2 * D_FF), name=f"{tag}_dw_up_g")
    g["w_up_t"] = _mm(dz_v, s["h2"], mode="tn", out_dtype=F32, out_rows=(D_FF, 2 * D_FF), out_prev=dw_up_t,
                      name=f"{tag}_dw_up_v")
    deps = sched("bwd_ffn_grads", l, dz_v, g)
    dx1, dx1_bf, dgain = _mm_rows((dz_g, dz_v), w["w_up_t"], mode="nn", fn=_rms_bwd_rows, out_dtypes=(F32, BF16),
                                  rows=(s["x1"], dx2), vecs=(w["ffn_norm"],), reduce=True, a_at=(0, D_FF),
                                  name=f"{tag}_dh2", deps=deps)
    g["ffn_norm"] = dgain.reshape(D_MODEL)
    dpa, dpb, dga, dgb = _merge_bwd(dx1_bf, w["w_out"], s["y_att"], s["y_sgu"], w["w_oa"], w["w_ob"], s["proj"],
                                    name=f"{tag}_merge")
    deps = sched("bwd_merge", l, dpa)
    g["w_out"] = _mm(s["merged"], dx1_bf, mode="tn", out_dtype=F32, name=f"{tag}_dw_out",
                     deps=deps)
    dy_att = _mm(dpa, w["w_oa"], mode="nt", out_dtype=BF16, name=f"{tag}_dy_att")
    dy_sgu = _mm(dpb, w["w_ob"], mode="nt", out_dtype=F32, name=f"{tag}_dy_sgu")
    g["w_oa"] = _mm(s["y_att"], dpa, mode="tn", out_dtype=F32, name=f"{tag}_dw_oa")
    g["w_ob"] = _mm(s["y_sgu"], dpb, mode="tn", out_dtype=F32, name=f"{tag}_dw_ob")
    deps = sched("bwd_out_grads", l, dy_att, g)
    dqkv, g["q_norm"], g["k_norm"], g["sinks"] = _attention_bwd(
        s["proj"], dy_att, w["q_norm"], w["k_norm"], w["sinks"], n_seq=n_seq, seq=seq, name=f"{tag}_att", deps=deps)
    deps = sched("bwd_att", l, dqkv)
    dsuv, g["sgu_norm"], g["w_s"], g["b_s"] = _sgu_bwd(
        s["proj"], dy_sgu, w["sgu_norm"], w["w_s"], w["bias_full"], n_seq=n_seq, seq=seq, name=f"{tag}_sgu", deps=deps)
    dproj = (dsuv, dga, dgb, dqkv)
    at = (QKV_WIDTH, QKV_WIDTH + 2 * SGU_WIDTH, QKV_WIDTH + 2 * SGU_WIDTH + D_MODEL, 0)
    g["w_in_t"] = _mm_tn_parts(dproj, at, s["h"], name=f"{tag}_dw_in")
    deps = sched("bwd_w_in_grad", l, dqkv, g)
    dx, dx_bf, dgain = _mm_rows(dproj, w["w_in_t"], mode="nn", fn=_rms_bwd_rows, out_dtypes=(F32, BF16),
                                rows=(s["x"], dx1), vecs=(w["mix_norm"],), reduce=True, a_at=at,
                                name=f"{tag}_dh", deps=deps)
    g["mix_norm"] = dgain.reshape(D_MODEL)
    return dx, dx_bf, g, sched("bwd_dh", l, dx)


def _local_step(x, target, weights, sched, *, n_seq, seq):
    depth = len(weights)
    saved = []
    h = _rms_fwd(x, weights[0]["mix_norm"], name="l0_mix_norm", deps=sched("begin", 0, x))
    for l in range(depth):
        tail = ("norm", weights[l + 1]["mix_norm"]) if l + 1 < depth else ("loss", target)
        out, s = _layer_fwd(x, h, weights[l], sched, tail, n_seq=n_seq, seq=seq, l=l)
        saved.append(s)
        if l + 1 < depth:
            x, h = out
    dy, dy_bf, loss_cols = out
    grads = [None] * depth
    deps = ()
    for l in reversed(range(depth)):
        dy, dy_bf, grads[l], deps = _layer_bwd(dy, dy_bf, weights[l], saved[l], sched, deps, n_seq=n_seq, seq=seq, l=l)
    return jnp.sum(loss_cols), dy, grads, deps


W_IN_SHARD = IN_WIDTH // N_DEV
W_UP_SHARD = 2 * D_FF // N_DEV
COL_MOVE_ROWS = 256


def _w_o_moves():
    return tuple((j, 0, LANES, 0, j * LANES) for j in range(N_DEV))


def _disassemble(mats, w, moves, *, name):
    R = mats[0].shape[0]
    tr = min(R, COL_MOVE_ROWS)
    n = len(mats)

    def body(*refs):
        m_refs, o_ref = refs[:n], refs[n]
        for j, lo, hi, which, at in moves:
            o_ref[j, :, lo:hi] = m_refs[which][:, at:at + hi - lo]

    return pl.pallas_call(
        body, name=name, grid=(R // tr,),
        in_specs=[pl.BlockSpec((tr, m.shape[1]), lambda i: (i, 0)) for m in mats],
        out_specs=pl.BlockSpec((N_DEV, tr, w), lambda i: (0, i, 0)),
        out_shape=jax.ShapeDtypeStruct((N_DEV, R, w), mats[0].dtype),
        compiler_params=_params(("parallel",)),
    )(*mats)


def _my_place():
    return lax.axis_index("x"), lax.axis_index("y"), lax.axis_index("c")


def _gathered_shape(shape, kind):
    r, c = shape
    return {"blocks": (N_DEV, r, c), "rows": (N_DEV * r, c), "cols": (r, N_DEV * c)}[kind]


def _gather_window(ref, kind, shape, j):
    r, c = shape
    if kind == "blocks":
        return ref.at[j]
    if kind == "rows":
        return ref.at[pl.ds(pl.multiple_of(j * r, r), r), :]
    return ref.at[:, pl.ds(pl.multiple_of(j * c, c), c)]


def _gather(srcs, kinds, *, name):
    n = len(srcs)
    shapes = [s.shape for s in srcs]
    per = 7

    def body(*refs):
        src_refs, dst_refs = refs[:n], refs[n:2 * n]
        send_sems, recv_sems, local_sems = refs[2 * n:]
        x, y, c = _my_place()
        me, sibling = (x, y, c), (x, y, 1 - c)
        chips = [(1 - x, y), (x, 1 - y), (1 - x, 1 - y)]

        def at(i, px, py, pc):
            return _gather_window(dst_refs[i], kinds[i], shapes[i], 4 * px + 2 * py + pc)

        def copy(i, k, block, to, src=None):
            return pltpu.make_async_remote_copy(
                src_ref=at(i, *block) if src is None else src, dst_ref=at(i, *block),
                send_sem=send_sems.at[per * i + k], recv_sem=recv_sems.at[per * i + k], device_id=to, device_id_type=MESH)

        mine = [pltpu.make_async_copy(src_refs[i], at(i, *me), local_sems.at[i]) for i in range(n)]
        for cp in mine:
            cp.start()
        started = []
        for i in range(n):
            first = [copy(i, 0, me, sibling, src=src_refs[i])]
            first += [copy(i, 1 + j, me, (*chip, c), src=src_refs[i]) for j, chip in enumerate(chips)]
            for cp in first:
                cp.start()
            started += first
        for i in range(n):
            for j, chip in enumerate(chips):
                copy(i, 1 + j, (*chip, c), me).wait_recv()
                fwd = copy(i, 4 + j, (*chip, c), sibling)
                fwd.start()
                started.append(fwd)
        for i in range(n):
            copy(i, 0, sibling, me).wait_recv()
            for j, chip in enumerate(chips):
                copy(i, 4 + j, (*chip, 1 - c), me).wait_recv()
        for cp in started:
            cp.wait_send()
        for cp in mine:
            cp.wait()

    return pl.pallas_call(
        body, name=name,
        out_shape=[jax.ShapeDtypeStruct(_gathered_shape(s.shape, k), s.dtype) for s, k in zip(srcs, kinds)],
        in_specs=[ANY] * n, out_specs=[ANY] * n,
        scratch_shapes=[pltpu.SemaphoreType.DMA((per * n,)), pltpu.SemaphoreType.DMA((per * n,)),
                        pltpu.SemaphoreType.DMA((n,))],
    )(*srcs)


HBM = pl.BlockSpec(memory_space=pltpu.HBM)
SEM = pl.BlockSpec(memory_space=pltpu.SEMAPHORE)
TOKEN = jax.ShapeDtypeStruct((SUBLANES, LANES), F32)
TOKEN_SPEC = pl.BlockSpec(memory_space=pltpu.VMEM)
SPLIT_PARAMS = pltpu.CompilerParams(has_side_effects=pltpu.SideEffectType.DATAFLOW_SIDE_EFFECTING)


def _in_hbm(x):
    return pltpu.with_memory_space_constraint(x, pltpu.HBM)


def _hbm_like(shape, dtype):
    return pltpu.HBM(shape, dtype)


def _place_own(stacks, layers, kinds, dtypes, *, name, deps=()):
    n = len(stacks)
    shapes = [s.shape[1:] for s in stacks]

    def body(*refs):
        s_refs, land_refs, bufs, sems = refs[:n], refs[n:2 * n], refs[2 * n:3 * n], refs[3 * n]
        x, y, c = _my_place()
        copies = []
        for i in range(n):
            bufs[i][...] = s_refs[i][...].astype(dtypes[i])
            copies.append(pltpu.make_async_copy(
                bufs[i], _gather_window(land_refs[i], kinds[i], shapes[i], 4 * x + 2 * y + c), sems.at[i]))
        for cp in copies:
            cp.start()
        for cp in copies:
            cp.wait()

    def layer_of(shape, l):
        return pl.BlockSpec((None,) + shape, lambda i: (l,) + (0,) * len(shape))

    body, dep_specs, dep_args = _with_deps(body, n, deps)
    return pl.pallas_call(
        body, name=name, grid=(1,),
        out_shape=[jax.ShapeDtypeStruct(_gathered_shape(s, k), d) for s, k, d in zip(shapes, kinds, dtypes)],
        in_specs=[layer_of(s, l) for s, l in zip(shapes, layers)] + dep_specs, out_specs=[ANY] * n,
        scratch_shapes=[pltpu.VMEM(s, d) for s, d in zip(shapes, dtypes)] + [pltpu.SemaphoreType.DMA((n,))],
        compiler_params=_params(("arbitrary",)),
    )(*stacks, *dep_args)


def _gather_start(lands, kinds, shapes, after=(), *, name):
    n = len(lands)
    n_after = len(after)

    def body(*refs):
        land_refs = refs[:n]
        send_sems, recv_sems = refs[n + n_after], refs[n + n_after + 1]
        x, y, c = _my_place()
        targets = [(x, y, 1 - c), (1 - x, y, c), (x, 1 - y, c), (1 - x, 1 - y, c)]
        for i in range(n):
            own = _gather_window(land_refs[i], kinds[i], shapes[i], 4 * x + 2 * y + c)
            for k, to in enumerate(targets):
                pltpu.make_async_remote_copy(
                    src_ref=own, dst_ref=own, send_sem=send_sems.at[4 * i + k], recv_sem=recv_sems.at[4 * i + k],
                    device_id=to, device_id_type=MESH).start()
        refs[-1][...] = jnp.zeros_like(refs[-1])

    outs = pl.pallas_call(
        body, name=name,
        out_shape=[pltpu.SemaphoreType.DMA((4 * n,)), pltpu.SemaphoreType.DMA((4 * n,))]
        + [_hbm_like(a.shape, a.dtype) for a in lands] + [TOKEN],
        in_specs=[HBM] * n + [ANY] * n_after, out_specs=[SEM, SEM] + [HBM] * n + [TOKEN_SPEC],
        input_output_aliases={i: 2 + i for i in range(n)},
        compiler_params=SPLIT_PARAMS,
    )(*[_in_hbm(a) for a in lands], *after)
    return outs[0], outs[1], outs[2:2 + n], outs[-1]


def _gather_forward(recv_sems, lands, kinds, shapes, after, *, name):
    n = len(lands)

    def body(*refs):
        recv_ref, land_refs = refs[0], refs[1:1 + n]
        fwd_send, fwd_recv = refs[2 + n], refs[3 + n]
        token = refs[-1]
        x, y, c = _my_place()
        chips = [(1 - x, y), (x, 1 - y), (1 - x, 1 - y)]
        for i in range(n):
            for j, (px, py) in enumerate(chips):
                block = _gather_window(land_refs[i], kinds[i], shapes[i], 4 * px + 2 * py + c)
                pltpu.make_async_remote_copy(
                    src_ref=block, dst_ref=block, send_sem=fwd_send.at[3 * i + j], recv_sem=recv_ref.at[4 * i + 1 + j],
                    device_id=(px, py, c), device_id_type=MESH).wait_recv()
                pltpu.make_async_remote_copy(
                    src_ref=block, dst_ref=block, send_sem=fwd_send.at[3 * i + j], recv_sem=fwd_recv.at[3 * i + j],
                    device_id=(x, y, 1 - c), device_id_type=MESH).start()
        token[...] = jnp.zeros_like(token)

    outs = pl.pallas_call(
        body, name=name,
        out_shape=[pltpu.SemaphoreType.DMA((3 * n,)), pltpu.SemaphoreType.DMA((3 * n,))]
        + [_hbm_like(a.shape, a.dtype) for a in lands] + [TOKEN],
        in_specs=[SEM] + [HBM] * n + [ANY], out_specs=[SEM, SEM] + [HBM] * n + [TOKEN_SPEC],
        input_output_aliases={1 + i: 2 + i for i in range(n)},
        compiler_params=SPLIT_PARAMS,
    )(recv_sems, *lands, after)
    return outs[0], outs[1], outs[2:2 + n], outs[-1]


def _gather_finish(send_sems, recv_sems, fwd_send, fwd_recv, lands, kinds, shapes, after, *, name):
    n = len(lands)

    def body(*refs):
        send_ref, recv_ref, fsend_ref, frecv_ref = refs[:4]
        land_refs = refs[4:4 + n]
        x, y, c = _my_place()
        chips = [(1 - x, y), (x, 1 - y), (1 - x, 1 - y)]
        sibling = (x, y, 1 - c)
        for i in range(n):
            def window(j):
                return _gather_window(land_refs[i], kinds[i], shapes[i], j)

            mine, theirs = window(4 * x + 2 * y + c), window(4 * x + 2 * y + (1 - c))
            pltpu.make_async_remote_copy(src_ref=mine, dst_ref=theirs, send_sem=send_ref.at[4 * i],
                                         recv_sem=recv_ref.at[4 * i], device_id=sibling, device_id_type=MESH).wait_recv()
            for j, (px, py) in enumerate(chips):
                block = window(4 * px + 2 * py + (1 - c))
                pltpu.make_async_remote_copy(src_ref=block, dst_ref=block, send_sem=fsend_ref.at[3 * i + j],
                                             recv_sem=frecv_ref.at[3 * i + j], device_id=sibling,
                                             device_id_type=MESH).wait_recv()
            for k in range(4):
                pltpu.make_async_remote_copy(src_ref=mine, dst_ref=mine, send_sem=send_ref.at[4 * i + k],
                                             recv_sem=recv_ref.at[4 * i + k], device_id=sibling,
                                             device_id_type=MESH).wait_send()
            for j, (px, py) in enumerate(chips):
                block = window(4 * px + 2 * py + c)
                pltpu.make_async_remote_copy(src_ref=block, dst_ref=block, send_sem=fsend_ref.at[3 * i + j],
                                             recv_sem=frecv_ref.at[3 * i + j], device_id=sibling,
                                             device_id_type=MESH).wait_send()

    return pl.pallas_call(
        body, name=name,
        out_shape=[_hbm_like(a.shape, a.dtype) for a in lands],
        in_specs=[SEM] * 4 + [HBM] * n + [ANY], out_specs=[HBM] * n,
        input_output_aliases={4 + i: i for i in range(n)},
        compiler_params=SPLIT_PARAMS,
    )(send_sems, recv_sems, fwd_send, fwd_recv, *lands, after)


def _pair_plan(src_ref, land_ref, x, y, c):
    return [(src_ref.at[2 * k + (1 - c)], land_ref.at[k], (x, y, 1 - c)) for k in range(N_CHIPS)]


def _chip_plan(src_ref, land_ref, x, y, c):
    chips = [(1 - x, y), (x, 1 - y), (1 - x, 1 - y)]
    return [(src_ref.at[2 * px + py], land_ref.at[k], (px, py, c)) for k, (px, py) in enumerate(chips)]


def _exchange_copies(plan, per, src_refs, land_refs, send_sems, recv_sems):
    x, y, c = _my_place()
    copies = []
    for i, (s_ref, l_ref) in enumerate(zip(src_refs, land_refs)):
        for q, (src, dst, to) in enumerate(plan(s_ref, l_ref, x, y, c)):
            copies.append(pltpu.make_async_remote_copy(
                src_ref=src, dst_ref=dst, send_sem=send_sems.at[per * i + q], recv_sem=recv_sems.at[per * i + q],
                device_id=to, device_id_type=MESH))
    return copies


def _exchange_start(srcs, plan, per, *, name):
    n = len(srcs)

    def body(*refs):
        src_refs, land_refs = refs[:n], refs[n:2 * n]
        send_sems, recv_sems = refs[2 * n], refs[2 * n + 1]
        for cp in _exchange_copies(plan, per, src_refs, land_refs, send_sems, recv_sems):
            cp.start()
        refs[-1][...] = jnp.zeros_like(refs[-1])

    lands = [lax.empty((per,) + s.shape[1:], s.dtype) for s in srcs]
    outs = pl.pallas_call(
        body, name=name,
        out_shape=[pltpu.SemaphoreType.DMA((per * n,)), pltpu.SemaphoreType.DMA((per * n,))]
        + [_hbm_like(s.shape, s.dtype) for s in srcs] + [_hbm_like(a.shape, a.dtype) for a in lands] + [TOKEN],
        in_specs=[HBM] * (2 * n), out_specs=[SEM, SEM] + [HBM] * (2 * n) + [TOKEN_SPEC],
        input_output_aliases={i: 2 + i for i in range(2 * n)},
        compiler_params=SPLIT_PARAMS,
    )(*[_in_hbm(s) for s in srcs], *[_in_hbm(a) for a in lands])
    return outs[0], outs[1], outs[2:2 + n], outs[2 + n:2 + 2 * n], outs[-1]


def _exchange_wait(send_sems, recv_sems, srcs, lands, plan, per, after, *, name):
    n = len(srcs)
    after = list(after) if isinstance(after, (list, tuple)) else [after]

    def body(*refs):
        send_ref, recv_ref = refs[0], refs[1]
        src_refs, land_refs = refs[2:2 + n], refs[2 + n:2 + 2 * n]
        copies = _exchange_copies(plan, per, src_refs, land_refs, send_ref, recv_ref)
        for cp in copies:
            cp.wait_recv()
        for cp in copies:
            cp.wait_send()

    outs = pl.pallas_call(
        body, name=name,
        out_shape=[_hbm_like(s.shape, s.dtype) for s in srcs] + [_hbm_like(a.shape, a.dtype) for a in lands],
        in_specs=[SEM, SEM] + [HBM] * (2 * n) + [ANY] * len(after), out_specs=[HBM] * (2 * n),
        input_output_aliases={2 + i: i for i in range(2 * n)},
        compiler_params=SPLIT_PARAMS,
    )(send_sems, recv_sems, *srcs, *lands, *after)
    return outs[:n], outs[n:]


REDUCE_BLOCK_BYTES = 2 << 20


def _row_tile(r, c):
    row_bytes = 4 * (-(-c // LANES) * LANES)
    best = r
    for d in range(SUBLANES, r, SUBLANES):
        if r % d == 0 and d * row_bytes <= REDUCE_BLOCK_BYTES:
            best = d
    return best if r * row_bytes > REDUCE_BLOCK_BYTES else r


def _reduce_pair_sum(blocked, recv, place, wire_dtype, *, name):
    _, r, c = blocked.shape
    tr = _row_tile(r, c)

    def body(place_ref, g_ref, r_ref, own_ref, send_ref):
        s = g_ref[...] + r_ref[...]
        send_ref[...] = s.astype(wire_dtype)

        @pl.when(pl.program_id(1) == place_ref[1])
        def _():
            own_ref[...] = s

    return pl.pallas_call(
        body, name=name,
        grid_spec=pltpu.PrefetchScalarGridSpec(
            num_scalar_prefetch=1, grid=(r // tr, N_CHIPS),
            in_specs=[pl.BlockSpec((None, None, tr, c), lambda i, k, place_ref: (k, place_ref[0], i, 0)),
                      pl.BlockSpec((None, tr, c), lambda i, k, place_ref: (k, i, 0))],
            out_specs=[pl.BlockSpec((tr, c), lambda i, k, place_ref: (i, 0)),
                       pl.BlockSpec((None, tr, c), lambda i, k, place_ref: (k, i, 0))]),
        out_shape=[jax.ShapeDtypeStruct((r, c), F32), jax.ShapeDtypeStruct((N_CHIPS, r, c), wire_dtype)],
        compiler_params=_params(("parallel", "arbitrary")),
    )(place, blocked.reshape(N_CHIPS, 2, r, c), recv)


def _chip_sum(own_ref, r_ref):
    return ((own_ref[...] + r_ref[0].astype(F32)) + r_ref[1].astype(F32)) + r_ref[2].astype(F32)


def _reduce_chip_sum(own, recv, *, name):
    r, c = own.shape
    tr = _row_tile(r, c)

    def body(own_ref, r_ref, o_ref):
        o_ref[...] = _chip_sum(own_ref, r_ref)

    return pl.pallas_call(
        body, name=name, grid=(r // tr,),
        in_specs=[pl.BlockSpec((tr, c), lambda i: (i, 0)), pl.BlockSpec((N_CHIPS - 1, tr, c), lambda i: (0, i, 0))],
        out_specs=pl.BlockSpec((tr, c), lambda i: (i, 0)),
        out_shape=jax.ShapeDtypeStruct((r, c), F32),
        compiler_params=_params(("parallel",)),
    )(own, recv)


def _adamw_math(w, g, m, v):
    nm = ADAM_B1 * m + (1.0 - ADAM_B1) * g
    nv = ADAM_B2 * v + (1.0 - ADAM_B2) * (g * g)
    m_hat = nm / (1.0 - ADAM_B1 ** ADAM_STEP)
    v_hat = nv / (1.0 - ADAM_B2 ** ADAM_STEP)
    return -ADAM_LR * (m_hat / (jnp.sqrt(v_hat) + ADAM_EPS) + ADAM_WD * w), nm, nv


ADAMW_ROWS = 256


def _adamw_small(ws, gs, ms, vs, *, name):
    n = len(ws)

    def rows_of(a):
        return a.reshape(-1, a.shape[-1])

    def body(*refs):
        for i in range(n):
            w_ref, g_ref, m_ref, v_ref = refs[4 * i:4 * i + 4]
            outs = refs[4 * n + 3 * i:4 * n + 3 * i + 3]
            rows = w_ref.shape[0]
            if rows % ADAMW_ROWS:
                outs[0][...], outs[1][...], outs[2][...] = _adamw_math(w_ref[...], g_ref[...], m_ref[...], v_ref[...])
                continue

            def chunk(s, carry, w_ref=w_ref, g_ref=g_ref, m_ref=m_ref, v_ref=v_ref, outs=outs):
                r = pl.ds(pl.multiple_of(s * ADAMW_ROWS, ADAMW_ROWS), ADAMW_ROWS)
                outs[0][r, :], outs[1][r, :], outs[2][r, :] = _adamw_math(w_ref[r, :], g_ref[r, :], m_ref[r, :], v_ref[r, :])
                return carry

            lax.fori_loop(0, rows // ADAMW_ROWS, chunk, 0)

    vmem = pl.BlockSpec(memory_space=pltpu.VMEM)
    outs = pl.pallas_call(
        body, name=name, in_specs=[vmem] * (4 * n), out_specs=[vmem] * (3 * n),
        out_shape=[jax.ShapeDtypeStruct(rows_of(w).shape, F32) for w in ws for _ in range(3)],
        compiler_params=_params(),
    )(*[rows_of(a) for quad in zip(ws, gs, ms, vs) for a in quad])
    return [tuple(o.reshape(w.shape) for o in outs[3 * i:3 * i + 3]) for i, w in enumerate(ws)]


def _reduce_adamw(own, recv, w, m, v, layer, prev, *, name):
    r, c = own.shape
    tr = _row_tile(r, c)
    n_prev = 0 if prev is None else len(prev)

    def body(own_ref, r_ref, w_ref, m_ref, v_ref, *rest):
        g_ref, d_ref, nm_ref, nv_ref = rest[n_prev:]
        g = _chip_sum(own_ref, r_ref)
        g_ref[...] = g
        d_ref[...], nm_ref[...], nv_ref[...] = _adamw_math(w_ref[...], g, m_ref[...], v_ref[...])

    slot = pl.BlockSpec((None, tr, c), lambda i: (layer, i, 0))
    return pl.pallas_call(
        body, name=name, grid=(r // tr,),
        in_specs=[pl.BlockSpec((tr, c), lambda i: (i, 0)), pl.BlockSpec((N_CHIPS - 1, tr, c), lambda i: (0, i, 0)),
                  slot, slot, slot] + [ANY] * n_prev,
        out_specs=[slot] * 4,
        out_shape=[jax.ShapeDtypeStruct((DEPTH, r, c), F32)] * 4,
        input_output_aliases={5 + k: k for k in range(n_prev)},
        compiler_params=_params(("parallel",)),
    )(own, recv, w, m, v, *(prev or ()))


REPLICATED = (("mix_norm", (D_MODEL,)), ("q_norm", (HEAD_DIM,)), ("k_norm", (HEAD_DIM,)), ("sinks", (N_Q_HEADS,)),
              ("sgu_norm", (SGU_WIDTH,)), ("w_s", (SGU_GROUPS, BLOCK, BLOCK)), ("b_s", (SGU_GROUPS, BLOCK)),
              ("ffn_norm", (D_MODEL,)), ("conv_b", (2 * D_FF,)))
TRANSPOSED = ("w_in", "w_up")
SHARDED = (("w_in", "rows"), ("w_oa", "cols"), ("w_ob", "cols"), ("w_out", "rows"), ("w_up", "rows"),
           ("conv_w", "blocks"), ("w_down", "rows"))
WEIGHT_ORDER = ("mix_norm", "w_in", "q_norm", "k_norm", "sinks", "sgu_norm", "w_s", "b_s", "w_oa", "w_ob", "w_out",
                "ffn_norm", "w_up", "conv_w", "conv_b", "w_down")
MIXER_WEIGHTS = ["w_in", "w_oa", "w_ob", "w_out"]
FFN_WEIGHTS = ["w_up", "conv_w", "w_down"]


def _small_layout():
    segs, off = {}, 0
    for name, shape in sorted(REPLICATED, key=lambda named: -math.prod(named[1])):
        for l in range(DEPTH):
            n = math.prod(shape)
            segs[(l, name)] = (off, n)
            off += n
    per_dev = -(-off // (N_DEV * SUBLANES * LANES)) * SUBLANES * LANES
    return segs, off, per_dev


def _pack_small(grads, loss_part):
    ssegs, total, per_dev = _small_layout()
    flat = jnp.concatenate([grads[l][name].reshape(-1) for (l, name) in ssegs] + [loss_part.reshape(1)])
    return jnp.pad(flat, (0, N_DEV * per_dev - total - 1)).reshape(N_DEV, per_dev // LANES, LANES)


def _unpack_small(gathered):
    ssegs, total, _ = _small_layout()
    flat = gathered.reshape(-1)
    small = {}
    for name, shape in REPLICATED:
        start, n = ssegs[(0, name)]
        small[name] = lax.optimization_barrier(flat[start:start + DEPTH * n]).reshape((DEPTH,) + shape)
    return small, flat[total]


def kernel(x, mix_norm, w_in, q_norm, k_norm, sinks, sgu_norm, w_s, b_s, w_oa, w_ob, w_out, ffn_norm, w_up, conv_w, conv_b, w_down, loss_target, m_mix_norm, m_w_in, m_q_norm, m_k_norm, m_sinks, m_sgu_norm, m_w_s, m_b_s, m_w_oa, m_w_ob, m_w_out, m_ffn_norm, m_w_up, m_conv_w, m_conv_b, m_w_down, v_mix_norm, v_w_in, v_q_norm, v_k_norm, v_sinks, v_sgu_norm, v_w_s, v_b_s, v_w_oa, v_w_ob, v_w_out, v_ffn_norm, v_w_up, v_conv_w, v_conv_b, v_w_down):
    W = dict(mix_norm=mix_norm, w_in=w_in, q_norm=q_norm, k_norm=k_norm, sinks=sinks, sgu_norm=sgu_norm, w_s=w_s, b_s=b_s,
             w_oa=w_oa, w_ob=w_ob, w_out=w_out, ffn_norm=ffn_norm, w_up=w_up, conv_w=conv_w, conv_b=conv_b, w_down=w_down)
    M = dict(mix_norm=m_mix_norm, w_in=m_w_in, q_norm=m_q_norm, k_norm=m_k_norm, sinks=m_sinks, sgu_norm=m_sgu_norm,
             w_s=m_w_s, b_s=m_b_s, w_oa=m_w_oa, w_ob=m_w_ob, w_out=m_w_out, ffn_norm=m_ffn_norm, w_up=m_w_up,
             conv_w=m_conv_w, conv_b=m_conv_b, w_down=m_w_down)
    V = dict(mix_norm=v_mix_norm, w_in=v_w_in, q_norm=v_q_norm, k_norm=v_k_norm, sinks=v_sinks, sgu_norm=v_sgu_norm,
             w_s=v_w_s, b_s=v_b_s, w_oa=v_w_oa, w_ob=v_w_ob, w_out=v_w_out, ffn_norm=v_ffn_norm, w_up=v_w_up,
             conv_w=v_conv_w, conv_b=v_conv_b, w_down=v_w_down)
    n_seq, seq, d_model = x.shape
    tokens = n_seq * seq
    mx, my, mc = _my_place()
    place = jnp.stack([mc, 2 * mx + my]).astype(jnp.int32)
    half = N_DEV // 2
    kind_of = dict(SHARDED)
    for name in TRANSPOSED:
        W[name], M[name], V[name] = (jnp.swapaxes(t[name], 1, 2) for t in (W, M, V))

    gather_groups = [[(0, MIXER_WEIGHTS[0])], [(0, n) for n in MIXER_WEIGHTS[1:]], [(0, n) for n in FFN_WEIGHTS],
                     [(1, n) for n in MIXER_WEIGHTS], [(1, n) for n in FFN_WEIGHTS]]
    started, in_flight = {}, {}
    weights = []
    for l in range(DEPTH):
        w = {name: W[name][l] for name, _ in REPLICATED}
        w["cb_g"], w["cb_v"] = W["conv_b"][l][:D_FF], W["conv_b"][l][D_FF:]
        w["bias_full"] = jnp.repeat(W["b_s"][l].T, SGU_WIDTH // SGU_GROUPS, axis=1)
        weights.append(w)

    def gather_start(gi, after=()):
        stacks = [W[name] for _, name in gather_groups[gi]]
        kinds = [kind_of[name] for _, name in gather_groups[gi]]
        shapes = [s.shape[1:] for s in stacks]
        lands = _place_own(stacks, [l for l, _ in gather_groups[gi]], kinds,
                           [F32 if name == "conv_w" else BF16 for _, name in gather_groups[gi]],
                           name=f"gather_weights_own_{gi}", deps=after)
        send, recv, lands, token = _gather_start(lands, kinds, shapes, after, name=f"gather_weights_start_{gi}")
        started[gi] = dict(sems=(send, recv), lands=lands, kinds=kinds, shapes=shapes)
        return token

    def gather_forward(gi, after):
        st = started[gi]
        in_flight[gi] = _gather_forward(st["sems"][1], st["lands"], st["kinds"], st["shapes"], after,
                                        name=f"gather_weights_forward_{gi}")
        return in_flight[gi][3]

    def gather_finish(gi, after):
        st = started.pop(gi)
        fwd_send, fwd_recv, lands_g, _ = in_flight.pop(gi)
        whole = _gather_finish(st["sems"][0], st["sems"][1], fwd_send, fwd_recv, lands_g, st["kinds"], st["shapes"], after,
                               name=f"gather_weights_finish_{gi}")
        for (l, name), arr in zip(gather_groups[gi], whole):
            w = weights[l]
            if name in TRANSPOSED:
                w[name + "_t"] = arr
            elif name == "conv_w":
                w["cw_g"] = arr[:half].transpose(1, 0, 2).reshape(3, D_FF)
                w["cw_v"] = arr[half:].transpose(1, 0, 2).reshape(3, D_FF)
            else:
                w[name] = arr

    reduce_state, results = {}, {}
    wire = {"conv_w": F32, "small": F32}

    def reduce_begin(key, names, arrays):
        send, recv, srcs_, lands_, token = _exchange_start(arrays, _pair_plan, N_CHIPS, name=f"reduce_pair_start_{key}")
        reduce_state[key] = dict(names=names, pair=(send, recv, srcs_, lands_))
        return [token]

    def reduce_pair(key, after):
        st = reduce_state[key]
        send, recv, srcs_, lands_ = st.pop("pair")
        blocked_, from_sibling = _exchange_wait(send, recv, srcs_, lands_, _pair_plan, N_CHIPS, after,
                                                name=f"reduce_pair_wait_{key}")
        sums = [_reduce_pair_sum(b, r, place, wire.get(n if isinstance(n, str) else n[1], BF16),
                                 name=f"reduce_pair_sum_{key}_{i}")
                for i, (n, b, r) in enumerate(zip(st["names"], blocked_, from_sibling))]
        st["own"] = [s[0] for s in sums]
        *st["chip"], token = _exchange_start([s[1] for s in sums], _chip_plan, N_CHIPS - 1, name=f"reduce_chip_start_{key}")
        return [token]

    def reduce_end(key, after):
        st = reduce_state.pop(key)
        send, recv, srcs_, lands_ = st["chip"]
        _, from_chips = _exchange_wait(send, recv, srcs_, lands_, _chip_plan, N_CHIPS - 1, after,
                                       name=f"reduce_chip_wait_{key}")
        done = []
        for n, own, got in zip(st["names"], st["own"], from_chips):
            if n == "small":
                results["small"] = _reduce_chip_sum(own, got, name="reduce_chip_sum_small")
            else:
                l, name = n
                results[name] = _reduce_adamw(own, got, W[name], M[name], V[name], l, results.get(name),
                                              name=f"l{l}_reduce_adamw_{name}")
                done.append(results[name][0])
        return done

    def sched(point, l, carry, g=None):
        deps = []
        if point == "begin":
            token = ()
            for gi in range(len(gather_groups)):
                token = [gather_start(gi, token)]
            deps = [gather_forward(0, token[0])]
        elif point == "fwd_start" and l == 0:
            gather_finish(0, carry)
            deps = [gather_forward(1, weights[0]["w_in_t"])]
        elif point == "fwd_att" and l == 0:
            gather_finish(1, carry)
            deps = [gather_forward(2, carry)]
        elif point == "fwd_mixer_done" and l == 0:
            gather_finish(2, carry)
        elif point == "fwd_conv" and l == 0:
            deps = [gather_forward(3, carry)]
        elif point == "fwd_start" and l == 1:
            gather_finish(3, carry)
        elif point == "fwd_att" and l == 1:
            deps = [gather_forward(4, carry)]
        elif point == "fwd_mixer_done" and l == 1:
            gather_finish(4, carry)
        elif point == "bwd_ffn_grads":
            conv_w = jnp.concatenate([g[k].reshape(3, half, W_UP_SHARD).transpose(1, 0, 2) for k in ("cw_g", "cw_v")])
            deps = reduce_begin(
                f"l{l}_ffn", [(l, "w_down"), (l, "w_up"), (l, "conv_w")],
                [g["w_down"].reshape(N_DEV, D_FF // N_DEV, D_MODEL),
                 g["w_up_t"].reshape(N_DEV, W_UP_SHARD, D_MODEL), conv_w])
        elif point == "bwd_merge":
            deps = reduce_pair(f"l{l}_ffn", carry)
        elif point == "bwd_out_grads":
            deps = reduce_begin(
                f"l{l}_out", [(l, "w_out"), (l, "w_oa"), (l, "w_ob")],
                [g["w_out"].reshape(N_DEV, D_MODEL // N_DEV, D_MODEL),
                 _disassemble((g["w_oa"],), LANES, _w_o_moves(), name=f"l{l}_split_dw_oa"),
                 _disassemble((g["w_ob"],), LANES, _w_o_moves(), name=f"l{l}_split_dw_ob")])
        elif point == "bwd_att":
            deps = reduce_pair(f"l{l}_out", carry)
        elif point == "bwd_w_in_grad":
            deps = reduce_begin(f"l{l}_in", [(l, "w_in")], [g["w_in_t"].reshape(N_DEV, W_IN_SHARD, D_MODEL)])
        elif point == "bwd_dh":
            deps = reduce_pair(f"l{l}_in", carry)
        return deps

    loss_part, dx, grads, last_deps = _local_step(x.reshape(tokens, d_model), loss_target.reshape(tokens, d_model),
                                                  weights, sched, n_seq=n_seq, seq=seq)
    for g in grads:
        g["conv_b"] = jnp.concatenate([g["cb_g"], g["cb_v"]])
    after = [dx, *last_deps, *reduce_begin("small", ["small"], [_pack_small(grads, loss_part)])]
    for key in [f"l{l}_{part}" for l in reversed(range(DEPTH)) for part in ("ffn", "out", "in")][:-1]:
        after = reduce_end(key, after)
    after = reduce_end("l0_in", after + reduce_pair("small", after))
    reduce_end("small", after)

    G, delta, new_m, new_v = {}, {}, {}, {}
    for name, _ in SHARDED:
        outs = [jnp.swapaxes(o, 1, 2) for o in results[name]] if name in TRANSPOSED else results[name]
        G[name], delta[name], new_m[name], new_v[name] = outs
    small, loss = _unpack_small(_gather([results["small"]], ["blocks"], name="gather_small_grads")[0])
    G.update(small)
    names = [name for name, _ in REPLICATED]
    stepped = _adamw_small(*[[t[name] for name in names] for t in (W, G, M, V)], name="adamw_replicated")
    for name, stepped_one in zip(names, stepped):
        delta[name], new_m[name], new_v[name] = stepped_one
    return (loss, dx.reshape(n_seq, seq, d_model), *[G[n] for n in WEIGHT_ORDER], *[delta[n] for n in WEIGHT_ORDER],
            *[new_m[n] for n in WEIGHT_ORDER], *[new_v[n] for n in WEIGHT_ORDER])
```

```python
import math

import jax
import jax.numpy as jnp
from jax import lax
from jax.experimental import pallas as pl
from jax.experimental.pallas import tpu as pltpu

F32 = jnp.float32
BF16 = jnp.bfloat16
ACT_DTYPE = BF16
MESH = pl.DeviceIdType.MESH

DEPTH = 2
D_MODEL = 1024
N_Q_HEADS = 8
HEAD_DIM = 64
ATT_WIDTH = 512
KV_WIDTH = 128
BLOCK = 128
SGU_WIDTH = 512
SGU_GROUPS = 8
IN_WIDTH = 3840
D_FF = 2816
NORM_EPS = 1e-6
NEG_INF = -1e30
ATT_SCALE = HEAD_DIM ** -0.5
ALIBI_SLOPES = tuple(2.0 ** (-(h + 1)) for h in range(N_Q_HEADS))
ADAM_LR, ADAM_B1, ADAM_B2, ADAM_EPS, ADAM_WD, ADAM_STEP = 0.001, 0.9, 0.999, 1e-08, 0.01, 10
N_DEV = 8
N_CHIPS = 4

QKV_WIDTH = ATT_WIDTH + 2 * KV_WIDTH
COL_SUV, COL_GA, COL_GB, COL_QKV = 0, 1024, 2048, 3072
W_IN_ROTATE = (1, IN_WIDTH // QKV_WIDTH)

LANES = 128
SUBLANES = 8
VMEM_LIMIT_V7X = 56 * 1024 * 1024
GELU_C = math.sqrt(2.0 / math.pi)
GELU_K = 0.044715
ANY = pl.BlockSpec(memory_space=pl.ANY)


def _params(sem=None):
    return pltpu.CompilerParams(dimension_semantics=sem, vmem_limit_bytes=VMEM_LIMIT_V7X)


def _sigmoid(x):
    return 1.0 / (1.0 + jnp.exp(-x))


def _gelu(x):
    th = jnp.tanh(GELU_C * (x + GELU_K * x * x * x))
    return 0.5 * x * (1.0 + th)


def _gelu_and_grad(x):
    x2 = x * x
    th = jnp.tanh(GELU_C * (x + GELU_K * x2 * x))
    g = 0.5 * x * (1.0 + th)
    dg = 0.5 * (1.0 + th) + 0.5 * x * (1.0 - th * th) * (GELU_C * (1.0 + 3.0 * GELU_K * x2))
    return g, dg


def _dot(a, b, dims):
    return lax.dot_general(a, b, (dims, ((), ())), preferred_element_type=F32)


def _dot_nn(a, b):
    return _dot(a, b, ((1,), (0,)))


def _dot_nt(a, b):
    return _dot(a, b, ((1,), (1,)))


def _dot_tn(a, b):
    return _dot(a, b, ((0,), (0,)))


def _lo_mask(shape):
    return lax.broadcasted_iota(jnp.int32, shape, len(shape) - 1) < (LANES // 2)


def _half_sums(x, lo):
    s_lo = jnp.sum(jnp.where(lo, x, 0.0), axis=-1, keepdims=True)
    s_all = jnp.sum(x, axis=-1, keepdims=True)
    return jnp.where(lo, s_lo, s_all - s_lo)


def _dup_half(x, half, lo):
    r = pltpu.roll(x, LANES // 2, axis=1)
    return jnp.where(lo, x, r) if half == 0 else jnp.where(lo, r, x)


def _with_deps(body, n_in, deps):
    k = len(deps)
    if not k:
        return body, [], ()

    def skipping(*refs):
        return body(*refs[:n_in], *refs[n_in + k:])

    return skipping, [ANY] * k, tuple(deps)


MM_VMEM_BUDGET = 40 * 1024 * 1024
MM_MAX_TILE = 1408
MM_MAX_TK = 4096
MM_STEP_BYTES = 1 << 20


def _divisors(n, step, cap):
    return [d for d in range(step, min(n, cap) + 1, step) if n % d == 0] or [n]


def _mm_tiles(M, N, K, out_bytes, tm_divides, tn_divides):
    best = None
    for tm in _divisors(M, LANES, MM_MAX_TILE):
        for tn in _divisors(N, LANES, MM_MAX_TILE):
            if tm_divides % tm or tn_divides % tn:
                continue
            for tk in _divisors(K, 4 * LANES, MM_MAX_TK):
                vmem = 4 * (tm * tk + tk * tn) + 2 * tm * tn * out_bytes + (0 if tk == K else 4 * tm * tn)
                if vmem > MM_VMEM_BUDGET:
                    continue
                traffic = 2 * M * K * (N // tn) + 2 * K * N * (M // tm) + M * N * out_bytes
                cost = traffic + (K // tk - 1) * 8 * M * N + (M // tm) * (N // tn) * (K // tk) * MM_STEP_BYTES
                if best is None or cost < best[0]:
                    best = (cost, tm, tn, tk)
    assert best is not None, (M, N, K)
    return best[1:]


def _mm(a, b, *, mode, out_dtype, name, deps=(), b_rows=(0, None), rotate=None, out_rows=(0, None), out_prev=None):
    b_first, b_count = b_rows
    if mode == "nn":
        (M, K), N = a.shape, b.shape[1]
    elif mode == "nt":
        (M, K), N = a.shape, (b.shape[0] if b_count is None else b_count)
    else:
        (K, M), N = a.shape, b.shape[1]
    shift, period = rotate or (0, 1)
    assert period == 1 or mode == "nt"
    out_first, out_total = out_rows[0], (M if out_rows[1] is None else out_rows[1])
    tm, tn, tk = _mm_tiles(M, N, K, jnp.dtype(out_dtype).itemsize, math.gcd(M, out_first),
                           math.gcd(N // period, b_first if mode == "nt" else 0))
    gm, gn, gk = M // tm, N // tn, K // tk

    def turned(j):
        per = N // period // tn
        return ((j // per + shift) % period) * per + j % per if period > 1 else j

    if mode == "nn":
        a_spec = pl.BlockSpec((tm, tk), lambda i, j, k: (i, k))
        b_spec = pl.BlockSpec((tk, tn), lambda i, j, k: (k + b_first // tk, j))
        contract = ((1,), (0,))
    elif mode == "nt":
        a_spec = pl.BlockSpec((tm, tk), lambda i, j, k: (i, k))
        b_spec = pl.BlockSpec((tn, tk), lambda i, j, k: (turned(j) + b_first // tn, k))
        contract = ((1,), (1,))
    else:
        a_spec = pl.BlockSpec((tk, tm), lambda i, j, k: (k, i))
        b_spec = pl.BlockSpec((tk, tn), lambda i, j, k: (k, j))
        contract = ((0,), (0,))
    o_spec = pl.BlockSpec((tm, tn), lambda i, j, k: (i + out_first // tm, j))
    assert b_first % (tk if mode == "nn" else tn) == 0 and out_first % tm == 0, (name, tm, tn, tk)
    n_prev = 0 if out_prev is None else 1

    def body(a_ref, b_ref, *rest):
        o_ref = rest[n_prev]
        part = _dot(a_ref[...].astype(BF16), b_ref[...].astype(BF16), contract)
        if gk == 1:
            o_ref[...] = part.astype(out_dtype)
            return
        acc_ref = rest[n_prev + 1]
        k = pl.program_id(2)

        @pl.when(k == 0)
        def _():
            acc_ref[...] = part

        @pl.when(k > 0)
        def _():
            acc_ref[...] += part

        @pl.when(k == gk - 1)
        def _():
            o_ref[...] = acc_ref[...].astype(out_dtype)

    body, dep_specs, dep_args = _with_deps(body, 2 + n_prev, deps)
    return pl.pallas_call(
        body,
        name=name,
        grid=(gm, gn, gk),
        in_specs=[a_spec, b_spec] + [ANY] * n_prev + dep_specs,
        out_specs=o_spec,
        out_shape=jax.ShapeDtypeStruct((out_total, N), out_dtype),
        input_output_aliases={2: 0} if n_prev else {},
        scratch_shapes=[] if gk == 1 else [pltpu.VMEM((tm, tn), F32)],
        compiler_params=_params(("parallel", "parallel", "arbitrary")),
    )(a, b, *([out_prev] if n_prev else []), *dep_args)


def _mm_tn_parts(parts, at, b, *, name):
    K, N = b.shape
    n = len(parts)
    tm = math.gcd(*[p.shape[1] for p in parts], *at)
    tiles = [p.shape[1] // tm for p in parts]
    first = [sum(tiles[:p]) for p in range(n)]

    def mine(i, p):
        return jnp.logical_and(i >= first[p], i < first[p] + tiles[p])

    def out_tile(i):
        t = 0
        for p in range(n):
            t = jnp.where(mine(i, p), at[p] // tm + i - first[p], t)
        return t

    def body(*refs):
        a_refs, b_ref, o_ref = refs[:n], refs[n], refs[n + 1]
        for p in range(n):
            @pl.when(mine(pl.program_id(0), p))
            def _(p=p):
                o_ref[...] = _dot_tn(a_refs[p][...], b_ref[...])

    return pl.pallas_call(
        body, name=name, grid=(sum(tiles),),
        in_specs=[pl.BlockSpec((K, tm), lambda i, p=p: (0, jnp.clip(i - first[p], 0, tiles[p] - 1))) for p in range(n)]
        + [pl.BlockSpec((K, N), lambda i: (0, 0), pipeline_mode=pl.Buffered(1))],
        out_specs=pl.BlockSpec((tm, N), lambda i: (out_tile(i), 0)),
        out_shape=jax.ShapeDtypeStruct((sum(p.shape[1] for p in parts), N), F32),
        compiler_params=_params(("arbitrary",)),
    )(*parts, b)


def _mm_rows(a, b, *, mode, fn, out_dtypes, rows=(), vecs=(), reduce=False, name, deps=(), b_rows=(0, None), a_at=None):
    parts = a if a_at is not None else (a,)
    starts = a_at if a_at is not None else (0,)
    n_parts = len(parts)
    M, K = parts[0].shape[0], sum(p.shape[1] for p in parts)
    b_first, b_count = b_rows[0], (b.shape[0] if b_rows[1] is None else b_rows[1])
    N = b.shape[1] if mode == "nn" else b_count
    contract = ((1,), (0,)) if mode == "nn" else ((1,), (1,))
    n_rows, n_vecs, n_out = len(rows), len(vecs), len(out_dtypes)
    out_bytes = sum(jnp.dtype(d).itemsize for d in out_dtypes)
    tm = max(t for t in _divisors(M, LANES, MM_MAX_TILE)
             if 4 * t * K + 2 * K * N + 2 * t * N * (4 * n_rows + out_bytes) <= MM_VMEM_BUDGET)
    assert b_first % b_count == 0 and (a_at is None or mode == "nn")

    def body(*refs):
        a_refs, b_ref, rest = refs[:n_parts], refs[n_parts], refs[n_parts + 1:]
        row_refs, vec_refs = rest[:n_rows], rest[n_rows:n_rows + n_vecs]
        out_refs = rest[n_rows + n_vecs:]
        if a_at is None:
            acc = _dot(a_refs[0][...], b_ref[...], contract)
        else:
            acc = sum(_dot(r[...], b_ref[at:at + r.shape[1], :], contract) for r, at in zip(a_refs, starts))
        res = fn(acc, *[r[...] for r in row_refs], *[v[...] for v in vec_refs])
        for o_ref, val in zip(out_refs[:n_out], res):
            o_ref[...] = val.astype(o_ref.dtype)
        if reduce:
            @pl.when(pl.program_id(0) == 0)
            def _():
                out_refs[n_out][...] = res[n_out]

            @pl.when(pl.program_id(0) > 0)
            def _():
                out_refs[n_out][...] += res[n_out]

    row = pl.BlockSpec((tm, N), lambda i: (i, 0))
    vec = pl.BlockSpec((1, N), lambda i: (0, 0))
    body, dep_specs, dep_args = _with_deps(body, n_parts + 1 + n_rows + n_vecs, deps)
    return pl.pallas_call(
        body, name=name, grid=(M // tm,),
        in_specs=[pl.BlockSpec((tm, p.shape[1]), lambda i: (i, 0)) for p in parts]
        + [pl.BlockSpec((b_count, b.shape[1]), lambda i: (b_first // b_count, 0), pipeline_mode=pl.Buffered(1))]
        + [row] * n_rows + [vec] * n_vecs + dep_specs,
        out_specs=[row] * n_out + [vec] * reduce,
        out_shape=[jax.ShapeDtypeStruct((M, N), d) for d in out_dtypes] + [jax.ShapeDtypeStruct((1, N), F32)] * reduce,
        compiler_params=_params(("arbitrary",)),
    )(*parts, b, *rows, *[v.reshape(1, N) for v in vecs], *dep_args)


def _rms(x, gain):
    return x * lax.rsqrt(jnp.mean(x * x, axis=-1, keepdims=True) + NORM_EPS) * gain


def _residual_then_norm(acc, x, gain):
    x_out = x + acc
    return x_out, _rms(x_out, gain)


def _residual_then_loss(acc, x, target):
    err = (x + acc) - target
    dy = err * (1.0 / D_MODEL)
    return dy, dy, jnp.sum(err * err, axis=0, keepdims=True) * (0.5 / D_MODEL)


def _rms_bwd_rows(dh, x, dres, gain):
    r = lax.rsqrt(jnp.mean(x * x, axis=-1, keepdims=True) + NORM_EPS)
    xh = x * r
    dxh = dh * gain
    dx = dres + r * (dxh - xh * jnp.mean(dxh * xh, axis=-1, keepdims=True))
    return dx, dx, jnp.sum(dh * xh, axis=0, keepdims=True)


def _rms_fwd(x, gain, *, name, tm=512, deps=()):
    T, D = x.shape

    def body(x_ref, g_ref, h_ref):
        xv = x_ref[...]
        r = lax.rsqrt(jnp.mean(xv * xv, axis=-1, keepdims=True) + NORM_EPS)
        h_ref[...] = (xv * r * g_ref[...]).astype(BF16)

    body, dep_specs, dep_args = _with_deps(body, 2, deps)
    return pl.pallas_call(
        body, name=name, grid=(T // tm,),
        in_specs=[pl.BlockSpec((tm, D), lambda i: (i, 0)), pl.BlockSpec((1, D), lambda i: (0, 0))] + dep_specs,
        out_specs=pl.BlockSpec((tm, D), lambda i: (i, 0)),
        out_shape=jax.ShapeDtypeStruct((T, D), BF16),
        compiler_params=_params(("parallel",)),
    )(x, gain.reshape(1, D), *dep_args)


def _head_norm(x, gain2, lo):
    ms = _half_sums(x * x, lo) * (1.0 / HEAD_DIM)
    r = lax.rsqrt(ms + NORM_EPS)
    xh = x * r
    return xh * gain2, xh, r


def _head_norm_bwd(xh, r, gain2, dy, lo):
    dxh = dy * gain2
    dx = r * (dxh - xh * (_half_sums(dxh * xh, lo) * (1.0 / HEAD_DIM)))
    return dx, dy * xh


Q_GROUP = N_Q_HEADS // 2
GROUP_ROWS = Q_GROUP * BLOCK
ATT_SCRATCH = (pltpu.VMEM((2, 2, GROUP_ROWS, BLOCK), F32), pltpu.VMEM((2, GROUP_ROWS, 1), F32))


def _att_consts(sink_ref, bias_ref, sinkcol_ref):
    row = lax.broadcasted_iota(jnp.int32, (GROUP_ROWS, BLOCK), 0)
    kj = lax.broadcasted_iota(jnp.int32, (GROUP_ROWS, BLOCK), 1)
    head = row // BLOCK
    head_col = lax.broadcasted_iota(jnp.int32, (GROUP_ROWS, 1), 0) // BLOCK
    d_cur = (row % BLOCK) - kj
    d_prev = d_cur + BLOCK
    for kv in range(2):
        slope = jnp.zeros((GROUP_ROWS, BLOCK), F32)
        sink = jnp.zeros((GROUP_ROWS, 1), F32)
        for r in range(Q_GROUP):
            slope = jnp.where(head == r, ALIBI_SLOPES[Q_GROUP * kv + r], slope)
            sink = jnp.where(head_col == r, sink_ref[Q_GROUP * kv + r], sink)
        bias_ref[kv, 0] = jnp.where(d_cur >= 0, -slope * d_cur.astype(F32), NEG_INF)
        bias_ref[kv, 1] = jnp.where(d_prev < BLOCK, -slope * d_prev.astype(F32), NEG_INF)
        sinkcol_ref[kv] = sink


def _stack_heads(t0, t1, lo):
    z = jnp.zeros_like(t0)
    return jnp.concatenate([jnp.where(lo, t0, z), jnp.where(lo, z, t0), jnp.where(lo, t1, z), jnp.where(lo, z, t1)], axis=0)


def _unstack_heads(x4, lo):
    return (jnp.where(lo, x4[0:BLOCK], x4[BLOCK:2 * BLOCK]), jnp.where(lo, x4[2 * BLOCK:3 * BLOCK], x4[3 * BLOCK:]))


def _att_probs(q4, k2c, k2p, bias_c, bias_p, sink, has_prev):
    s_c = _dot_nt(q4, k2c) * ATT_SCALE + bias_c
    s_p = jnp.where(has_prev, _dot_nt(q4, k2p) * ATT_SCALE + bias_p, NEG_INF)
    m = jnp.maximum(jnp.max(jnp.maximum(s_c, s_p), axis=-1, keepdims=True), sink)
    e_c = jnp.exp(s_c - m)
    e_p = jnp.exp(s_p - m)
    e_s = jnp.exp(sink - m)
    inv = 1.0 / (jnp.sum(e_c + e_p, axis=-1, keepdims=True) + e_s)
    return e_c * inv, e_p * inv, e_s * inv


def _attention_fwd(proj, q_gain, k_gain, sinks, *, n_seq, seq, name, deps=()):
    T = n_seq * seq
    nb = seq // BLOCK
    qcol, kvcol = COL_QKV // ATT_WIDTH, (COL_QKV + ATT_WIDTH) // (2 * KV_WIDTH)

    def body(q_ref, kv_ref, qg_ref, kg_ref, sink_ref, y_ref, bias_ref, sinkcol_ref):
        lo = _lo_mask((BLOCK, LANES))
        qg, kg = qg_ref[...], kg_ref[...]
        _att_consts(sink_ref, bias_ref, sinkcol_ref)

        def block(i, carry):
            r0 = pl.multiple_of(i * BLOCK, BLOCK)
            rp = pl.multiple_of(jnp.maximum(i - 1, 0) * BLOCK, BLOCK)
            has_prev = i > 0
            kn_c = _head_norm(kv_ref[pl.ds(r0, BLOCK), 0:KV_WIDTH].astype(F32), kg, lo)[0].astype(BF16)
            kn_p = _head_norm(kv_ref[pl.ds(rp, BLOCK), 0:KV_WIDTH].astype(F32), kg, lo)[0].astype(BF16)
            v_c = kv_ref[pl.ds(r0, BLOCK), KV_WIDTH:2 * KV_WIDTH].astype(BF16)
            v_p = kv_ref[pl.ds(rp, BLOCK), KV_WIDTH:2 * KV_WIDTH].astype(BF16)
            for kv in range(2):
                k2c, k2p = _dup_half(kn_c, kv, lo), _dup_half(kn_p, kv, lo)
                v2c, v2p = _dup_half(v_c, kv, lo), _dup_half(v_p, kv, lo)
                cols = [slice((2 * kv + t) * LANES, (2 * kv + t + 1) * LANES) for t in range(2)]
                qn = [_head_norm(q_ref[pl.ds(r0, BLOCK), c].astype(F32), qg, lo)[0] for c in cols]
                q4 = _stack_heads(qn[0], qn[1], lo).astype(BF16)
                p_c, p_p, _ = _att_probs(q4, k2c, k2p, bias_ref[kv, 0], bias_ref[kv, 1], sinkcol_ref[kv], has_prev)
                o4 = _dot_nn(p_c.astype(BF16), v2c) + _dot_nn(p_p.astype(BF16), v2p)
                for c, out in zip(cols, _unstack_heads(o4, lo)):
                    y_ref[pl.ds(r0, BLOCK), c] = out.astype(BF16)
            return carry

        lax.fori_loop(0, nb, block, 0)

    vec = pl.BlockSpec((1, LANES), lambda b: (0, 0))
    body, dep_specs, dep_args = _with_deps(body, 5, deps)
    return pl.pallas_call(
        body, name=name, grid=(n_seq,),
        in_specs=[pl.BlockSpec((seq, ATT_WIDTH), lambda b: (b, qcol)),
                  pl.BlockSpec((seq, 2 * KV_WIDTH), lambda b: (b, kvcol)),
                  vec, vec, pl.BlockSpec(memory_space=pltpu.SMEM)] + dep_specs,
        out_specs=pl.BlockSpec((seq, ATT_WIDTH), lambda b: (b, 0)),
        out_shape=jax.ShapeDtypeStruct((T, ATT_WIDTH), BF16),
        scratch_shapes=list(ATT_SCRATCH),
        compiler_params=_params(("parallel",)),
    )(proj, proj, jnp.tile(q_gain, 2).reshape(1, LANES), jnp.tile(k_gain, 2).reshape(1, LANES), sinks, *dep_args)


def _attention_bwd(proj, dy, q_gain, k_gain, sinks, *, n_seq, seq, name, deps=()):
    T = n_seq * seq
    nb = seq // BLOCK
    qcol, kvcol = COL_QKV // ATT_WIDTH, (COL_QKV + ATT_WIDTH) // (2 * KV_WIDTH)

    def body(q_ref, kv_ref, dy_ref, qg_ref, kg_ref, sink_ref, dqkv_ref, dqg_ref, dkg_ref, dsink_ref,
             dkn_acc, dv_acc, qg_acc, kg_acc, sink_acc, bias_ref, sinkcol_ref):
        lo = _lo_mask((BLOCK, LANES))
        qg, kg = qg_ref[...], kg_ref[...]
        _att_consts(sink_ref, bias_ref, sinkcol_ref)
        first = pl.program_id(0) == 0

        @pl.when(first)
        def _():
            qg_acc[...] = jnp.zeros_like(qg_acc)
            kg_acc[...] = jnp.zeros_like(kg_acc)
            sink_acc[...] = jnp.zeros_like(sink_acc)

        dkn_acc[...] = jnp.zeros_like(dkn_acc)
        dv_acc[...] = jnp.zeros_like(dv_acc)

        def block(i, carry):
            r0 = pl.multiple_of(i * BLOCK, BLOCK)
            rp = pl.multiple_of(jnp.maximum(i - 1, 0) * BLOCK, BLOCK)
            has_prev = i > 0
            kn_c = _head_norm(kv_ref[pl.ds(r0, BLOCK), 0:KV_WIDTH].astype(F32), kg, lo)[0].astype(BF16)
            kn_p = _head_norm(kv_ref[pl.ds(rp, BLOCK), 0:KV_WIDTH].astype(F32), kg, lo)[0].astype(BF16)
            v_c = kv_ref[pl.ds(r0, BLOCK), KV_WIDTH:2 * KV_WIDTH].astype(BF16)
            v_p = kv_ref[pl.ds(rp, BLOCK), KV_WIDTH:2 * KV_WIDTH].astype(BF16)
            dk_c, dk_p, dv_c, dv_p = [], [], [], []
            for kv in range(2):
                k2c, k2p = _dup_half(kn_c, kv, lo), _dup_half(kn_p, kv, lo)
                v2c, v2p = _dup_half(v_c, kv, lo), _dup_half(v_p, kv, lo)
                cols = [slice((2 * kv + t) * LANES, (2 * kv + t + 1) * LANES) for t in range(2)]
                normed = [_head_norm(q_ref[pl.ds(r0, BLOCK), c].astype(F32), qg, lo) for c in cols]
                q4 = _stack_heads(normed[0][0], normed[1][0], lo).astype(BF16)
                do4 = _stack_heads(dy_ref[pl.ds(r0, BLOCK), cols[0]], dy_ref[pl.ds(r0, BLOCK), cols[1]], lo)
                p_c, p_p, p_s = _att_probs(q4, k2c, k2p, bias_ref[kv, 0], bias_ref[kv, 1], sinkcol_ref[kv], has_prev)
                dp_c = _dot_nt(do4, v2c)
                dp_p = _dot_nt(do4, v2p)
                delta = jnp.sum(p_c * dp_c + p_p * dp_p, axis=-1, keepdims=True)
                ds_c = (p_c * (dp_c - delta)).astype(BF16)
                ds_p = (p_p * (dp_p - delta)).astype(BF16)
                sink_acc[kv] += -(p_s * delta)
                dq4 = (_dot_nn(ds_c, k2c) + _dot_nn(ds_p, k2p)) * ATT_SCALE
                for c, (_, qh, qr), dqn in zip(cols, normed, _unstack_heads(dq4, lo)):
                    dq, dg = _head_norm_bwd(qh, qr, qg, dqn, lo)
                    dqkv_ref[pl.ds(r0, BLOCK), c] = dq.astype(BF16)
                    qg_acc[...] += dg
                dk_c.append(_dot_tn(ds_c, q4))
                dk_p.append(_dot_tn(ds_p, q4))
                dv_c.append(_dot_tn(p_c.astype(BF16), do4))
                dv_p.append(_dot_tn(p_p.astype(BF16), do4))

            def fold(parts):
                a = parts[0] + pltpu.roll(parts[0], LANES // 2, axis=1)
                b = parts[1] + pltpu.roll(parts[1], LANES // 2, axis=1)
                return jnp.where(lo, a, b)

            dkn_acc[pl.ds(r0, BLOCK), :] += fold(dk_c) * ATT_SCALE
            dkn_acc[pl.ds(rp, BLOCK), :] += fold(dk_p) * ATT_SCALE
            dv_acc[pl.ds(r0, BLOCK), :] += fold(dv_c)
            dv_acc[pl.ds(rp, BLOCK), :] += fold(dv_p)
            return carry

        lax.fori_loop(0, nb, block, 0)

        def finish(i, carry):
            r0 = pl.multiple_of(i * BLOCK, BLOCK)
            _, kh, kr = _head_norm(kv_ref[pl.ds(r0, BLOCK), 0:KV_WIDTH].astype(F32), kg, lo)
            dk, dg = _head_norm_bwd(kh, kr, kg, dkn_acc[pl.ds(r0, BLOCK), :], lo)
            dqkv_ref[pl.ds(r0, BLOCK), ATT_WIDTH:ATT_WIDTH + KV_WIDTH] = dk.astype(BF16)
            dqkv_ref[pl.ds(r0, BLOCK), ATT_WIDTH + KV_WIDTH:QKV_WIDTH] = dv_acc[pl.ds(r0, BLOCK), :].astype(BF16)
            kg_acc[...] += dg
            return carry

        lax.fori_loop(0, nb, finish, 0)

        @pl.when(pl.program_id(0) == n_seq - 1)
        def _():
            dqg_ref[...] = jnp.sum(qg_acc[...], axis=0, keepdims=True)
            dkg_ref[...] = jnp.sum(kg_acc[...], axis=0, keepdims=True)
            lane = lax.broadcasted_iota(jnp.int32, (1, LANES), 1)
            dsink = jnp.zeros((1, LANES), F32)
            for kv in range(2):
                for r in range(Q_GROUP):
                    total = jnp.sum(sink_acc[kv, r * BLOCK:(r + 1) * BLOCK, :], axis=0, keepdims=True)
                    dsink = jnp.where(lane == Q_GROUP * kv + r, total, dsink)
            dsink_ref[...] = dsink

    vec = pl.BlockSpec((1, LANES), lambda b: (0, 0))
    acc = pltpu.VMEM((BLOCK, LANES), F32)
    body, dep_specs, dep_args = _with_deps(body, 6, deps)
    dqkv, dqg, dkg, dsink = pl.pallas_call(
        body, name=name, grid=(n_seq,),
        in_specs=[pl.BlockSpec((seq, ATT_WIDTH), lambda b: (b, qcol)),
                  pl.BlockSpec((seq, 2 * KV_WIDTH), lambda b: (b, kvcol)),
                  pl.BlockSpec((seq, ATT_WIDTH), lambda b: (b, 0)),
                  vec, vec, pl.BlockSpec(memory_space=pltpu.SMEM)] + dep_specs,
        out_specs=[pl.BlockSpec((seq, QKV_WIDTH), lambda b: (b, 0)), vec, vec, vec],
        out_shape=[jax.ShapeDtypeStruct((T, QKV_WIDTH), BF16)] + [jax.ShapeDtypeStruct((1, LANES), F32)] * 3,
        scratch_shapes=[pltpu.VMEM((seq, KV_WIDTH), F32), pltpu.VMEM((seq, KV_WIDTH), F32), acc, acc,
                        pltpu.VMEM((2, GROUP_ROWS, 1), F32), *ATT_SCRATCH],
        compiler_params=_params(("arbitrary",)),
    )(proj, proj, dy, jnp.tile(q_gain, 2).reshape(1, LANES), jnp.tile(k_gain, 2).reshape(1, LANES), sinks, *dep_args)
    half = LANES // 2
    return dqkv, dqg[0, :half] + dqg[0, half:], dkg[0, :half] + dkg[0, half:], dsink[0, :N_Q_HEADS]


def _sgu_weights(w_ref):
    r = lax.broadcasted_iota(jnp.int32, (BLOCK, BLOCK), 0)
    c = lax.broadcasted_iota(jnp.int32, (BLOCK, BLOCK), 1)
    return [jnp.where(r >= c, w_ref[g], 0.0).astype(BF16) for g in range(SGU_GROUPS)]


def _sgu_fwd(proj, gain, w_s, bias_full, *, n_seq, seq, name):
    T = n_seq * seq
    nc = seq // BLOCK

    def body(suv_ref, g_ref, w_ref, b_ref, y_ref):
        lo = _lo_mask((BLOCK, LANES))
        wm = _sgu_weights(w_ref)
        gain_v = g_ref[...]

        def chunk(c, carry):
            r0 = pl.multiple_of(c * BLOCK, BLOCK)
            gv = _gelu(suv_ref[pl.ds(r0, BLOCK), SGU_WIDTH:2 * SGU_WIDTH].astype(F32))
            r = lax.rsqrt(jnp.mean(gv * gv, axis=-1, keepdims=True) + NORM_EPS)
            vn = (gv * r * gain_v).astype(BF16)
            for p in range(SGU_WIDTH // LANES):
                cols = slice(p * LANES, (p + 1) * LANES)
                vp = vn[:, cols]
                mixed = jnp.where(lo, _dot_nn(wm[2 * p], vp), _dot_nn(wm[2 * p + 1], vp)) + b_ref[:, cols]
                u = _gelu(suv_ref[pl.ds(r0, BLOCK), cols].astype(F32))
                y_ref[pl.ds(r0, BLOCK), cols] = (u * mixed).astype(BF16)
            return carry

        lax.fori_loop(0, nc, chunk, 0)

    return pl.pallas_call(
        body, name=name, grid=(n_seq,),
        in_specs=[pl.BlockSpec((seq, 2 * SGU_WIDTH), lambda b: (b, COL_SUV // (2 * SGU_WIDTH))),
                  pl.BlockSpec((1, SGU_WIDTH), lambda b: (0, 0)),
                  pl.BlockSpec((SGU_GROUPS, BLOCK, BLOCK), lambda b: (0, 0, 0)),
                  pl.BlockSpec((BLOCK, SGU_WIDTH), lambda b: (0, 0))],
        out_specs=pl.BlockSpec((seq, SGU_WIDTH), lambda b: (b, 0)),
        out_shape=jax.ShapeDtypeStruct((T, SGU_WIDTH), BF16),
        compiler_params=_params(("parallel",)),
    )(proj, gain.reshape(1, SGU_WIDTH), w_s, bias_full)


def _sgu_bwd(proj, dy, gain, w_s, bias_full, *, n_seq, seq, name, deps=()):
    T = n_seq * seq
    nc = seq // BLOCK
    n_tiles = SGU_WIDTH // LANES

    def body(suv_ref, dy_ref, g_ref, w_ref, b_ref, dsuv_ref, dg_ref, dw_ref, db_ref, dg_acc, dw_acc, db_acc):
        lo = _lo_mask((BLOCK, LANES))
        hi = jnp.logical_not(lo)
        wm = _sgu_weights(w_ref)
        wmt = [jnp.where(lax.broadcasted_iota(jnp.int32, (BLOCK, BLOCK), 1) >= lax.broadcasted_iota(jnp.int32, (BLOCK, BLOCK), 0),
                         w_ref[g].T, 0.0).astype(BF16) for g in range(SGU_GROUPS)]
        gain_v = g_ref[...]

        @pl.when(pl.program_id(0) == 0)
        def _():
            dg_acc[...] = jnp.zeros_like(dg_acc)
            dw_acc[...] = jnp.zeros_like(dw_acc)
            db_acc[...] = jnp.zeros_like(db_acc)

        def chunk(c, carry):
            r0 = pl.multiple_of(c * BLOCK, BLOCK)
            gv, dgelu_v = _gelu_and_grad(suv_ref[pl.ds(r0, BLOCK), SGU_WIDTH:2 * SGU_WIDTH].astype(F32))
            r = lax.rsqrt(jnp.mean(gv * gv, axis=-1, keepdims=True) + NORM_EPS)
            vh = gv * r
            vn = (vh * gain_v).astype(BF16)
            dvn_tiles = []
            for p in range(n_tiles):
                cols = slice(p * LANES, (p + 1) * LANES)
                vp = vn[:, cols]
                mixed = jnp.where(lo, _dot_nn(wm[2 * p], vp), _dot_nn(wm[2 * p + 1], vp)) + b_ref[:, cols]
                u, dgelu_u = _gelu_and_grad(suv_ref[pl.ds(r0, BLOCK), cols].astype(F32))
                dyv = dy_ref[pl.ds(r0, BLOCK), cols]
                dsuv_ref[pl.ds(r0, BLOCK), cols] = (dyv * mixed * dgelu_u).astype(BF16)
                dm = dyv * u
                db_acc[:, cols] += dm
                dm_bf = dm.astype(BF16)
                dvn_tiles.append(jnp.where(lo, _dot_nn(wmt[2 * p], dm_bf), _dot_nn(wmt[2 * p + 1], dm_bf)))
                dw_acc[2 * p] += _dot_nt(jnp.where(lo, dm, 0.0).astype(BF16), vp)
                dw_acc[2 * p + 1] += _dot_nt(jnp.where(hi, dm, 0.0).astype(BF16), vp)
            dvn = jnp.concatenate(dvn_tiles, axis=1)
            dg_acc[...] += dvn * vh
            dvh = dvn * gain_v
            dgv = r * (dvh - vh * jnp.mean(dvh * vh, axis=-1, keepdims=True))
            dsuv_ref[pl.ds(r0, BLOCK), SGU_WIDTH:2 * SGU_WIDTH] = (dgv * dgelu_v).astype(BF16)
            return carry

        lax.fori_loop(0, nc, chunk, 0)

        @pl.when(pl.program_id(0) == n_seq - 1)
        def _():
            dg_ref[...] = jnp.sum(dg_acc[...], axis=0, keepdims=True)
            r = lax.broadcasted_iota(jnp.int32, (BLOCK, BLOCK), 0)
            c = lax.broadcasted_iota(jnp.int32, (BLOCK, BLOCK), 1)
            for g in range(SGU_GROUPS):
                dw_ref[g] = jnp.where(r >= c, dw_acc[g], 0.0)
            lane = lax.broadcasted_iota(jnp.int32, (BLOCK, LANES), 1)
            out = jnp.zeros((BLOCK, LANES), F32)
            for p in range(n_tiles):
                tile = db_acc[:, p * LANES:(p + 1) * LANES]
                s_lo = jnp.sum(jnp.where(lo, tile, 0.0), axis=-1, keepdims=True)
                s_hi = jnp.sum(jnp.where(hi, tile, 0.0), axis=-1, keepdims=True)
                out = jnp.where(lane == 2 * p, s_lo, out)
                out = jnp.where(lane == 2 * p + 1, s_hi, out)
            db_ref[...] = out

    body, dep_specs, dep_args = _with_deps(body, 5, deps)
    dsuv, dg, dw, db = pl.pallas_call(
        body, name=name, grid=(n_seq,),
        in_specs=[pl.BlockSpec((seq, 2 * SGU_WIDTH), lambda b: (b, COL_SUV // (2 * SGU_WIDTH))),
                  pl.BlockSpec((seq, SGU_WIDTH), lambda b: (b, 0)),
                  pl.BlockSpec((1, SGU_WIDTH), lambda b: (0, 0)),
                  pl.BlockSpec((SGU_GROUPS, BLOCK, BLOCK), lambda b: (0, 0, 0)),
                  pl.BlockSpec((BLOCK, SGU_WIDTH), lambda b: (0, 0))] + dep_specs,
        out_specs=[pl.BlockSpec((seq, 2 * SGU_WIDTH), lambda b: (b, 0)),
                   pl.BlockSpec((1, SGU_WIDTH), lambda b: (0, 0)),
                   pl.BlockSpec((SGU_GROUPS, BLOCK, BLOCK), lambda b: (0, 0, 0)),
                   pl.BlockSpec((BLOCK, LANES), lambda b: (0, 0))],
        out_shape=[jax.ShapeDtypeStruct((T, 2 * SGU_WIDTH), BF16), jax.ShapeDtypeStruct((1, SGU_WIDTH), F32),
                   jax.ShapeDtypeStruct((SGU_GROUPS, BLOCK, BLOCK), F32), jax.ShapeDtypeStruct((BLOCK, LANES), F32)],
        scratch_shapes=[pltpu.VMEM((BLOCK, SGU_WIDTH), F32), pltpu.VMEM((SGU_GROUPS, BLOCK, BLOCK), F32),
                        pltpu.VMEM((BLOCK, SGU_WIDTH), F32)],
        compiler_params=_params(("arbitrary",)),
    )(proj, dy, gain.reshape(1, SGU_WIDTH), w_s, bias_full, *dep_args)
    return dsuv, dg.reshape(SGU_WIDTH), dw, db[:, :SGU_GROUPS].T


def _merge_fwd(y_att, y_sgu, w_oa, w_ob, proj, *, name, tm=1024, tn=512, deps=()):
    T = y_att.shape[0]

    def body(ya_ref, ys_ref, wa_ref, wb_ref, ga_ref, gb_ref, o_ref):
        pa = _dot_nn(ya_ref[...], wa_ref[...])
        pb = _dot_nn(ys_ref[...], wb_ref[...])
        o_ref[...] = (_sigmoid(ga_ref[...].astype(F32)) * pa + _sigmoid(gb_ref[...].astype(F32)) * pb).astype(BF16)

    act = pl.BlockSpec((tm, ATT_WIDTH), lambda i, j: (i, 0))
    wgt = pl.BlockSpec((ATT_WIDTH, tn), lambda i, j: (0, j))
    body, dep_specs, dep_args = _with_deps(body, 6, deps)
    return pl.pallas_call(
        body, name=name, grid=(T // tm, D_MODEL // tn),
        in_specs=[act, act, wgt, wgt,
                  pl.BlockSpec((tm, tn), lambda i, j: (i, j + COL_GA // tn)),
                  pl.BlockSpec((tm, tn), lambda i, j: (i, j + COL_GB // tn))] + dep_specs,
        out_specs=pl.BlockSpec((tm, tn), lambda i, j: (i, j)),
        out_shape=jax.ShapeDtypeStruct((T, D_MODEL), BF16),
        compiler_params=_params(("parallel", "parallel")),
    )(y_att, y_sgu, w_oa, w_ob, proj, proj, *dep_args)


def _merge_bwd(dx1_bf, w_out, y_att, y_sgu, w_oa, w_ob, proj, *, name, tm=1024, tn=512):
    T = y_att.shape[0]

    def body(dx_ref, wo_ref, ya_ref, ys_ref, wa_ref, wb_ref, ga_ref, gb_ref, dpa_ref, dpb_ref, dga_ref, dgb_ref):
        dm = _dot_nt(dx_ref[...], wo_ref[...])
        pa = _dot_nn(ya_ref[...], wa_ref[...])
        pb = _dot_nn(ys_ref[...], wb_ref[...])
        sa = _sigmoid(ga_ref[...].astype(F32))
        sb = _sigmoid(gb_ref[...].astype(F32))
        dpa_ref[...] = (dm * sa).astype(BF16)
        dpb_ref[...] = (dm * sb).astype(BF16)
        dga_ref[...] = (dm * pa * sa * (1.0 - sa)).astype(BF16)
        dgb_ref[...] = (dm * pb * sb * (1.0 - sb)).astype(BF16)

    act = pl.BlockSpec((tm, ATT_WIDTH), lambda i, j: (i, 0))
    wgt = pl.BlockSpec((ATT_WIDTH, tn), lambda i, j: (0, j))
    out = pl.BlockSpec((tm, tn), lambda i, j: (i, j))
    return pl.pallas_call(
        body, name=name, grid=(T // tm, D_MODEL // tn),
        in_specs=[pl.BlockSpec((tm, D_MODEL), lambda i, j: (i, 0)),
                  pl.BlockSpec((tn, D_MODEL), lambda i, j: (j, 0)),
                  act, act, wgt, wgt,
                  pl.BlockSpec((tm, tn), lambda i, j: (i, j + COL_GA // tn)),
                  pl.BlockSpec((tm, tn), lambda i, j: (i, j + COL_GB // tn))],
        out_specs=[out] * 4,
        out_shape=[jax.ShapeDtypeStruct((T, D_MODEL), BF16)] * 4,
        compiler_params=_params(("parallel", "parallel")),
    )(dx1_bf, w_out, y_att, y_sgu, w_oa, w_ob, proj, proj)


CONV_ROWS = 256
CONV_TN = 256


def _shift_rows(cur, prev8, k):
    rolled = pltpu.roll(cur, k, axis=0)
    head = jnp.where(lax.broadcasted_iota(jnp.int32, prev8.shape, 0) < k, pltpu.roll(prev8, k, axis=0), rolled[:SUBLANES])
    return jnp.concatenate([head, rolled[SUBLANES:]], axis=0)


def _shift_rows_up(cur, next8, k):
    n = cur.shape[0]
    rolled = pltpu.roll(cur, n - k, axis=0)
    tail = jnp.where(lax.broadcasted_iota(jnp.int32, next8.shape, 0) >= SUBLANES - k,
                     pltpu.roll(next8, SUBLANES - k, axis=0), rolled[n - SUBLANES:])
    return jnp.concatenate([rolled[:n - SUBLANES], tail], axis=0)


def _up_conv_fwd(h2, w_up_t, cw_g, cw_v, cb_g, cb_v, *, n_seq, seq, name, deps=()):
    T = n_seq * seq
    tn, rows = CONV_TN, CONV_ROWS

    def body(h_ref, ug_ref, uv_ref, wg_ref, wv_ref, bg_ref, bv_ref, a_ref, zg_ref, zv_ref, cg_ref, cv_ref):
        def conv(cur, prev8, w_ref, b_ref):
            z1 = _shift_rows(cur, prev8, 1)
            z2 = _shift_rows(cur, prev8, 2)
            return b_ref[...] + w_ref[0:1, :] * z2 + w_ref[1:2, :] * z1 + w_ref[2:3, :] * cur

        start = jnp.zeros((SUBLANES, tn), F32)
        prev = (start, start)
        for s in range(seq // rows):
            r = pl.ds(s * rows, rows)
            h = h_ref[r, :]
            zg = _dot_nt(h, ug_ref[...])
            zv = _dot_nt(h, uv_ref[...])
            zg_ref[r, :] = zg.astype(ACT_DTYPE)
            zv_ref[r, :] = zv.astype(ACT_DTYPE)
            g = conv(zg, prev[0], wg_ref, bg_ref)
            v = conv(zv, prev[1], wv_ref, bv_ref)
            a_ref[r, :] = (g * _sigmoid(g) * v).astype(BF16)
            cg_ref[r, :] = g.astype(ACT_DTYPE)
            cv_ref[r, :] = v.astype(ACT_DTYPE)
            prev = (zg[rows - SUBLANES:], zv[rows - SUBLANES:])

    zs = pl.BlockSpec((seq, tn), lambda b, j: (b, j))
    ws = pl.BlockSpec((3, tn), lambda b, j: (0, j))
    bs = pl.BlockSpec((1, tn), lambda b, j: (0, j))
    body, dep_specs, dep_args = _with_deps(body, 7, deps)
    return pl.pallas_call(
        body, name=name, grid=(n_seq, D_FF // tn),
        in_specs=[pl.BlockSpec((seq, D_MODEL), lambda b, j: (b, 0)),
                  pl.BlockSpec((tn, D_MODEL), lambda b, j: (j, 0)),
                  pl.BlockSpec((tn, D_MODEL), lambda b, j: (j + D_FF // tn, 0)), ws, ws, bs, bs] + dep_specs,
        out_specs=[zs] * 5,
        out_shape=[jax.ShapeDtypeStruct((T, D_FF), BF16)] + [jax.ShapeDtypeStruct((T, D_FF), ACT_DTYPE)] * 4,
        compiler_params=_params(("parallel", "parallel")),
    )(h2, w_up_t, w_up_t, cw_g, cw_v, cb_g.reshape(1, D_FF), cb_v.reshape(1, D_FF), *dep_args)


def _conv_bwd(z_g, z_v, c_g, c_v, dx2_bf, w_down, cw_g, cw_v, *, n_seq, seq, name):
    T = n_seq * seq
    tn, rows = CONV_TN, CONV_ROWS
    n_steps = seq // rows

    def body(zg_ref, zv_ref, cg_ref, cv_ref, dx_ref, wd_ref, wg_ref, wv_ref,
             dzg_ref, dzv_ref, dwg_ref, dwv_ref, dbg_ref, dbv_ref, dcg_ref, dcv_ref):
        def colsum(x):
            return jnp.sum(x, axis=0, keepdims=True)

        zero = jnp.zeros((1, tn), F32)
        db = (zero, zero)
        for s in range(n_steps):
            r = pl.ds(s * rows, rows)
            g = cg_ref[r, :].astype(F32)
            v = cv_ref[r, :].astype(F32)
            sg = _sigmoid(g)
            dav = _dot_nt(dx_ref[r, :], wd_ref[...])
            dcg = dav * v * (sg * (1.0 + g * (1.0 - sg)))
            dcv = dav * (g * sg)
            dcg_ref[r, :] = dcg
            dcv_ref[r, :] = dcv
            db = (db[0] + colsum(dcg), db[1] + colsum(dcv))

        def back(s, accs):
            r0 = pl.multiple_of(s * rows, rows)
            last = s == n_steps - 1
            rn = pl.multiple_of(jnp.minimum(r0 + rows, seq - SUBLANES), SUBLANES)
            new = []
            for half, (dc_ref, w_ref, dz_ref, z_ref) in enumerate(((dcg_ref, wg_ref, dzg_ref, zg_ref),
                                                                   (dcv_ref, wv_ref, dzv_ref, zv_ref))):
                cur = dc_ref[pl.ds(r0, rows), :]
                nxt = jnp.where(last, 0.0, dc_ref[pl.ds(rn, SUBLANES), :])
                u1, u2 = _shift_rows_up(cur, nxt, 1), _shift_rows_up(cur, nxt, 2)
                dz_ref[pl.ds(r0, rows), :] = (w_ref[2:3, :] * cur + w_ref[1:2, :] * u1 + w_ref[0:1, :] * u2).astype(BF16)
                z = z_ref[pl.ds(r0, rows), :].astype(F32)
                new += [accs[3 * half] + colsum(u2 * z), accs[3 * half + 1] + colsum(u1 * z),
                        accs[3 * half + 2] + colsum(cur * z)]
            return tuple(new)

        dw = lax.fori_loop(0, n_steps, back, (zero,) * 6)
        first_seq = pl.program_id(1) == 0

        @pl.when(first_seq)
        def _():
            dwg_ref[...] = jnp.concatenate(dw[0:3], axis=0)
            dwv_ref[...] = jnp.concatenate(dw[3:6], axis=0)
            dbg_ref[...], dbv_ref[...] = db

        @pl.when(jnp.logical_not(first_seq))
        def _():
            dwg_ref[...] += jnp.concatenate(dw[0:3], axis=0)
            dwv_ref[...] += jnp.concatenate(dw[3:6], axis=0)
            dbg_ref[...] += db[0]
            dbv_ref[...] += db[1]

    zs = pl.BlockSpec((seq, tn), lambda j, b: (b, j))
    ws = pl.BlockSpec((3, tn), lambda j, b: (0, j))
    bs = pl.BlockSpec((1, tn), lambda j, b: (0, j))
    outs = pl.pallas_call(
        body, name=name, grid=(D_FF // tn, n_seq),
        in_specs=[zs] * 4 + [pl.BlockSpec((seq, D_MODEL), lambda j, b: (b, 0)),
                             pl.BlockSpec((tn, D_MODEL), lambda j, b: (j, 0)), ws, ws],
        out_specs=[zs, zs, ws, ws, bs, bs],
        out_shape=[jax.ShapeDtypeStruct((T, D_FF), BF16)] * 2 + [jax.ShapeDtypeStruct((3, D_FF), F32)] * 2
        + [jax.ShapeDtypeStruct((1, D_FF), F32)] * 2,
        scratch_shapes=[pltpu.VMEM((seq, tn), F32), pltpu.VMEM((seq, tn), F32)],
        compiler_params=_params(("parallel", "arbitrary")),
    )(z_g, z_v, c_g, c_v, dx2_bf, w_down, cw_g, cw_v)
    dz_g, dz_v, dw_g, dw_v, db_g, db_v = outs
    return dz_g, dz_v, dw_g, dw_v, db_g.reshape(D_FF), db_v.reshape(D_FF)


def _layer_fwd(x, h, w, sched, tail, *, n_seq, seq, l):
    tag = f"l{l}"
    deps = sched("fwd_start", l, h)
    proj = _mm(h, w["w_in_t"], mode="nt", out_dtype=ACT_DTYPE, rotate=W_IN_ROTATE, name=f"{tag}_proj", deps=deps)
    y_att = _attention_fwd(proj, w["q_norm"], w["k_norm"], w["sinks"], n_seq=n_seq, seq=seq, name=f"{tag}_att",
                           deps=sched("fwd_proj", l, proj))
    deps = sched("fwd_att", l, y_att)
    y_sgu = _sgu_fwd(proj, w["sgu_norm"], w["w_s"], w["bias_full"], n_seq=n_seq, seq=seq, name=f"{tag}_sgu")
    merged = _merge_fwd(y_att, y_sgu, w["w_oa"], w["w_ob"], proj, name=f"{tag}_merge", deps=deps)
    x1, h2 = _mm_rows(merged, w["w_out"], mode="nn", fn=_residual_then_norm, out_dtypes=(F32, BF16), rows=(x,),
                      vecs=(w["ffn_norm"],), name=f"{tag}_out")
    deps = sched("fwd_mixer_done", l, x1)
    a, z_g, z_v, c_g, c_v = _up_conv_fwd(h2, w["w_up_t"], w["cw_g"], w["cw_v"], w["cb_g"], w["cb_v"], n_seq=n_seq,
                                         seq=seq, name=f"{tag}_up_conv", deps=deps)
    deps = sched("fwd_conv", l, a)
    if tail[0] == "norm":
        out = _mm_rows(a, w["w_down"], mode="nn", fn=_residual_then_norm, out_dtypes=(F32, BF16), rows=(x1,),
                       vecs=(tail[1],), name=f"{tag}_down", deps=deps)
    else:
        out = _mm_rows(a, w["w_down"], mode="nn", fn=_residual_then_loss, out_dtypes=(F32, BF16), rows=(x1, tail[1]),
                       reduce=True, name=f"{tag}_down", deps=deps)
    saved = dict(x=x, h=h, proj=proj, y_att=y_att, y_sgu=y_sgu, merged=merged, x1=x1, h2=h2, z_g=z_g, z_v=z_v,
                 c_g=c_g, c_v=c_v, a=a)
    return out, saved


def _layer_bwd(dx2, dx2_bf, w, s, sched, deps, *, n_seq, seq, l):
    tag = f"l{l}b"
    g = {}
    g["w_down"] = _mm(s["a"], dx2_bf, mode="tn", out_dtype=F32, name=f"{tag}_dw_down", deps=deps)
    dz_g, dz_v, g["cw_g"], g["cw_v"], g["cb_g"], g["cb_v"] = _conv_bwd(
        s["z_g"], s["z_v"], s["c_g"], s["c_v"], dx2_bf, w["w_down"], w["cw_g"], w["cw_v"], n_seq=n_seq, seq=seq,
        name=f"{tag}_conv")
    dw_up_t = _mm(dz_g, s["h2"], mode="tn", out_dtype=F32, out_rows=(0, 2 * D_FF), name=f"{tag}_dw_up_g")
    g["w_up_t"] = _mm(dz_v, s["h2"], mode="tn", out_dtype=F32, out_rows=(D_FF, 2 * D_FF), out_prev=dw_up_t,
                      name=f"{tag}_dw_up_v")
    deps = sched("bwd_ffn_grads", l, dz_v, g)
    dx1, dx1_bf, dgain = _mm_rows((dz_g, dz_v), w["w_up_t"], mode="nn", fn=_rms_bwd_rows, out_dtypes=(F32, BF16),
                                  rows=(s["x1"], dx2), vecs=(w["ffn_norm"],), reduce=True, a_at=(0, D_FF),
                                  name=f"{tag}_dh2", deps=deps)
    g["ffn_norm"] = dgain.reshape(D_MODEL)
    dpa, dpb, dga, dgb = _merge_bwd(dx1_bf, w["w_out"], s["y_att"], s["y_sgu"], w["w_oa"], w["w_ob"], s["proj"],
                                    name=f"{tag}_merge")
    deps = sched("bwd_merge", l, dpa)
    g["w_out"] = _mm(s["merged"], dx1_bf, mode="tn", out_dtype=F32, name=f"{tag}_dw_out",
                     deps=deps)
    dy_att = _mm(dpa, w["w_oa"], mode="nt", out_dtype=BF16, name=f"{tag}_dy_att")
    dy_sgu = _mm(dpb, w["w_ob"], mode="nt", out_dtype=F32, name=f"{tag}_dy_sgu")
    g["w_oa"] = _mm(s["y_att"], dpa, mode="tn", out_dtype=F32, name=f"{tag}_dw_oa")
    g["w_ob"] = _mm(s["y_sgu"], dpb, mode="tn", out_dtype=F32, name=f"{tag}_dw_ob")
    deps = sched("bwd_out_grads", l, dy_att, g)
    dqkv, g["q_norm"], g["k_norm"], g["sinks"] = _attention_bwd(
        s["proj"], dy_att, w["q_norm"], w["k_norm"], w["sinks"], n_seq=n_seq, seq=seq, name=f"{tag}_att", deps=deps)
    deps = sched("bwd_att", l, dqkv)
    dsuv, g["sgu_norm"], g["w_s"], g["b_s"] = _sgu_bwd(
        s["proj"], dy_sgu, w["sgu_norm"], w["w_s"], w["bias_full"], n_seq=n_seq, seq=seq, name=f"{tag}_sgu", deps=deps)
    dproj = (dsuv, dga, dgb, dqkv)
    at = (QKV_WIDTH, QKV_WIDTH + 2 * SGU_WIDTH, QKV_WIDTH + 2 * SGU_WIDTH + D_MODEL, 0)
    g["w_in_t"] = _mm_tn_parts(dproj, at, s["h"], name=f"{tag}_dw_in")
    deps = sched("bwd_w_in_grad", l, dqkv, g)
    dx, dx_bf, dgain = _mm_rows(dproj, w["w_in_t"], mode="nn", fn=_rms_bwd_rows, out_dtypes=(F32, BF16),
                                rows=(s["x"], dx1), vecs=(w["mix_norm"],), reduce=True, a_at=at,
                                name=f"{tag}_dh", deps=deps)
    g["mix_norm"] = dgain.reshape(D_MODEL)
    return dx, dx_bf, g, sched("bwd_dh", l, dx)


def _local_step(x, target, weights, sched, *, n_seq, seq):
    depth = len(weights)
    saved = []
    h = _rms_fwd(x, weights[0]["mix_norm"], name="l0_mix_norm", deps=sched("begin", 0, x))
    for l in range(depth):
        tail = ("norm", weights[l + 1]["mix_norm"]) if l + 1 < depth else ("loss", target)
        out, s = _layer_fwd(x, h, weights[l], sched, tail, n_seq=n_seq, seq=seq, l=l)
        saved.append(s)
        if l + 1 < depth:
            x, h = out
    dy, dy_bf, loss_cols = out
    grads = [None] * depth
    deps = ()
    for l in reversed(range(depth)):
        dy, dy_bf, grads[l], deps = _layer_bwd(dy, dy_bf, weights[l], saved[l], sched, deps, n_seq=n_seq, seq=seq, l=l)
    return jnp.sum(loss_cols), dy, grads, deps


W_IN_SHARD = IN_WIDTH // N_DEV
W_UP_SHARD = 2 * D_FF // N_DEV
COL_MOVE_ROWS = 256


def _w_o_moves():
    return tuple((j, 0, LANES, 0, j * LANES) for j in range(N_DEV))


def _disassemble(mats, w, moves, *, name):
    R = mats[0].shape[0]
    tr = min(R, COL_MOVE_ROWS)
    n = len(mats)

    def body(*refs):
        m_refs, o_ref = refs[:n], refs[n]
        for j, lo, hi, which, at in moves:
            o_ref[j, :, lo:hi] = m_refs[which][:, at:at + hi - lo]

    return pl.pallas_call(
        body, name=name, grid=(R // tr,),
        in_specs=[pl.BlockSpec((tr, m.shape[1]), lambda i: (i, 0)) for m in mats],
        out_specs=pl.BlockSpec((N_DEV, tr, w), lambda i: (0, i, 0)),
        out_shape=jax.ShapeDtypeStruct((N_DEV, R, w), mats[0].dtype),
        compiler_params=_params(("parallel",)),
    )(*mats)


def _my_place():
    return lax.axis_index("x"), lax.axis_index("y"), lax.axis_index("c")


def _gathered_shape(shape, kind):
    r, c = shape
    return {"blocks": (N_DEV, r, c), "rows": (N_DEV * r, c), "cols": (r, N_DEV * c)}[kind]


def _gather_window(ref, kind, shape, j):
    r, c = shape
    if kind == "blocks":
        return ref.at[j]
    if kind == "rows":
        return ref.at[pl.ds(pl.multiple_of(j * r, r), r), :]
    return ref.at[:, pl.ds(pl.multiple_of(j * c, c), c)]


def _gather(srcs, kinds, *, name):
    n = len(srcs)
    shapes = [s.shape for s in srcs]
    per = 7

    def body(*refs):
        src_refs, dst_refs = refs[:n], refs[n:2 * n]
        send_sems, recv_sems, local_sems = refs[2 * n:]
        x, y, c = _my_place()
        me, sibling = (x, y, c), (x, y, 1 - c)
        chips = [(1 - x, y), (x, 1 - y), (1 - x, 1 - y)]

        def at(i, px, py, pc):
            return _gather_window(dst_refs[i], kinds[i], shapes[i], 4 * px + 2 * py + pc)

        def copy(i, k, block, to, src=None):
            return pltpu.make_async_remote_copy(
                src_ref=at(i, *block) if src is None else src, dst_ref=at(i, *block),
                send_sem=send_sems.at[per * i + k], recv_sem=recv_sems.at[per * i + k], device_id=to, device_id_type=MESH)

        mine = [pltpu.make_async_copy(src_refs[i], at(i, *me), local_sems.at[i]) for i in range(n)]
        for cp in mine:
            cp.start()
        started = []
        for i in range(n):
            first = [copy(i, 0, me, sibling, src=src_refs[i])]
            first += [copy(i, 1 + j, me, (*chip, c), src=src_refs[i]) for j, chip in enumerate(chips)]
            for cp in first:
                cp.start()
            started += first
        for i in range(n):
            for j, chip in enumerate(chips):
                copy(i, 1 + j, (*chip, c), me).wait_recv()
                fwd = copy(i, 4 + j, (*chip, c), sibling)
                fwd.start()
                started.append(fwd)
        for i in range(n):
            copy(i, 0, sibling, me).wait_recv()
            for j, chip in enumerate(chips):
                copy(i, 4 + j, (*chip, 1 - c), me).wait_recv()
        for cp in started:
            cp.wait_send()
        for cp in mine:
            cp.wait()

    return pl.pallas_call(
        body, name=name,
        out_shape=[jax.ShapeDtypeStruct(_gathered_shape(s.shape, k), s.dtype) for s, k in zip(srcs, kinds)],
        in_specs=[ANY] * n, out_specs=[ANY] * n,
        scratch_shapes=[pltpu.SemaphoreType.DMA((per * n,)), pltpu.SemaphoreType.DMA((per * n,)),
                        pltpu.SemaphoreType.DMA((n,))],
    )(*srcs)


HBM = pl.BlockSpec(memory_space=pltpu.HBM)
SEM = pl.BlockSpec(memory_space=pltpu.SEMAPHORE)
TOKEN = jax.ShapeDtypeStruct((SUBLANES, LANES), F32)
TOKEN_SPEC = pl.BlockSpec(memory_space=pltpu.VMEM)
SPLIT_PARAMS = pltpu.CompilerParams(has_side_effects=pltpu.SideEffectType.DATAFLOW_SIDE_EFFECTING)


def _in_hbm(x):
    return pltpu.with_memory_space_constraint(x, pltpu.HBM)


def _hbm_like(shape, dtype):
    return pltpu.HBM(shape, dtype)


def _place_own(stacks, layers, kinds, dtypes, *, name, deps=()):
    n = len(stacks)
    shapes = [s.shape[1:] for s in stacks]

    def body(*refs):
        s_refs, land_refs, bufs, sems = refs[:n], refs[n:2 * n], refs[2 * n:3 * n], refs[3 * n]
        x, y, c = _my_place()
        copies = []
        for i in range(n):
            bufs[i][...] = s_refs[i][...].astype(dtypes[i])
            copies.append(pltpu.make_async_copy(
                bufs[i], _gather_window(land_refs[i], kinds[i], shapes[i], 4 * x + 2 * y + c), sems.at[i]))
        for cp in copies:
            cp.start()
        for cp in copies:
            cp.wait()

    def layer_of(shape, l):
        return pl.BlockSpec((None,) + shape, lambda i: (l,) + (0,) * len(shape))

    body, dep_specs, dep_args = _with_deps(body, n, deps)
    return pl.pallas_call(
        body, name=name, grid=(1,),
        out_shape=[jax.ShapeDtypeStruct(_gathered_shape(s, k), d) for s, k, d in zip(shapes, kinds, dtypes)],
        in_specs=[layer_of(s, l) for s, l in zip(shapes, layers)] + dep_specs, out_specs=[ANY] * n,
        scratch_shapes=[pltpu.VMEM(s, d) for s, d in zip(shapes, dtypes)] + [pltpu.SemaphoreType.DMA((n,))],
        compiler_params=_params(("arbitrary",)),
    )(*stacks, *dep_args)


def _gather_start(lands, kinds, shapes, after=(), *, name):
    n = len(lands)
    n_after = len(after)

    def body(*refs):
        land_refs = refs[:n]
        send_sems, recv_sems = refs[n + n_after], refs[n + n_after + 1]
        x, y, c = _my_place()
        targets = [(x, y, 1 - c), (1 - x, y, c), (x, 1 - y, c), (1 - x, 1 - y, c)]
        for i in range(n):
            own = _gather_window(land_refs[i], kinds[i], shapes[i], 4 * x + 2 * y + c)
            for k, to in enumerate(targets):
                pltpu.make_async_remote_copy(
                    src_ref=own, dst_ref=own, send_sem=send_sems.at[4 * i + k], recv_sem=recv_sems.at[4 * i + k],
                    device_id=to, device_id_type=MESH).start()
        refs[-1][...] = jnp.zeros_like(refs[-1])

    outs = pl.pallas_call(
        body, name=name,
        out_shape=[pltpu.SemaphoreType.DMA((4 * n,)), pltpu.SemaphoreType.DMA((4 * n,))]
        + [_hbm_like(a.shape, a.dtype) for a in lands] + [TOKEN],
        in_specs=[HBM] * n + [ANY] * n_after, out_specs=[SEM, SEM] + [HBM] * n + [TOKEN_SPEC],
        input_output_aliases={i: 2 + i for i in range(n)},
        compiler_params=SPLIT_PARAMS,
    )(*[_in_hbm(a) for a in lands], *after)
    return outs[0], outs[1], outs[2:2 + n], outs[-1]


def _gather_forward(recv_sems, lands, kinds, shapes, after, *, name):
    n = len(lands)

    def body(*refs):
        recv_ref, land_refs = refs[0], refs[1:1 + n]
        fwd_send, fwd_recv = refs[2 + n], refs[3 + n]
        token = refs[-1]
        x, y, c = _my_place()
        chips = [(1 - x, y), (x, 1 - y), (1 - x, 1 - y)]
        for i in range(n):
            for j, (px, py) in enumerate(chips):
                block = _gather_window(land_refs[i], kinds[i], shapes[i], 4 * px + 2 * py + c)
                pltpu.make_async_remote_copy(
                    src_ref=block, dst_ref=block, send_sem=fwd_send.at[3 * i + j], recv_sem=recv_ref.at[4 * i + 1 + j],
                    device_id=(px, py, c), device_id_type=MESH).wait_recv()
                pltpu.make_async_remote_copy(
                    src_ref=block, dst_ref=block, send_sem=fwd_send.at[3 * i + j], recv_sem=fwd_recv.at[3 * i + j],
                    device_id=(x, y, 1 - c), device_id_type=MESH).start()
        token[...] = jnp.zeros_like(token)

    outs = pl.pallas_call(
        body, name=name,
        out_shape=[pltpu.SemaphoreType.DMA((3 * n,)), pltpu.SemaphoreType.DMA((3 * n,))]
        + [_hbm_like(a.shape, a.dtype) for a in lands] + [TOKEN],
        in_specs=[SEM] + [HBM] * n + [ANY], out_specs=[SEM, SEM] + [HBM] * n + [TOKEN_SPEC],
        input_output_aliases={1 + i: 2 + i for i in range(n)},
        compiler_params=SPLIT_PARAMS,
    )(recv_sems, *lands, after)
    return outs[0], outs[1], outs[2:2 + n], outs[-1]


def _gather_finish(send_sems, recv_sems, fwd_send, fwd_recv, lands, kinds, shapes, after, *, name):
    n = len(lands)

    def body(*refs):
        send_ref, recv_ref, fsend_ref, frecv_ref = refs[:4]
        land_refs = refs[4:4 + n]
        x, y, c = _my_place()
        chips = [(1 - x, y), (x, 1 - y), (1 - x, 1 - y)]
        sibling = (x, y, 1 - c)
        for i in range(n):
            def window(j):
                return _gather_window(land_refs[i], kinds[i], shapes[i], j)

            mine, theirs = window(4 * x + 2 * y + c), window(4 * x + 2 * y + (1 - c))
            pltpu.make_async_remote_copy(src_ref=mine, dst_ref=theirs, send_sem=send_ref.at[4 * i],
                                         recv_sem=recv_ref.at[4 * i], device_id=sibling, device_id_type=MESH).wait_recv()
            for j, (px, py) in enumerate(chips):
                block = window(4 * px + 2 * py + (1 - c))
                pltpu.make_async_remote_copy(src_ref=block, dst_ref=block, send_sem=fsend_ref.at[3 * i + j],
                                             recv_sem=frecv_ref.at[3 * i + j], device_id=sibling,
                                             device_id_type=MESH).wait_recv()
            for k in range(4):
                pltpu.make_async_remote_copy(src_ref=mine, dst_ref=mine, send_sem=send_ref.at[4 * i + k],
                                             recv_sem=recv_ref.at[4 * i + k], device_id=sibling,
                                             device_id_type=MESH).wait_send()
            for j, (px, py) in enumerate(chips):
                block = window(4 * px + 2 * py + c)
                pltpu.make_async_remote_copy(src_ref=block, dst_ref=block, send_sem=fsend_ref.at[3 * i + j],
                                             recv_sem=frecv_ref.at[3 * i + j], device_id=sibling,
                                             device_id_type=MESH).wait_send()

    return pl.pallas_call(
        body, name=name,
        out_shape=[_hbm_like(a.shape, a.dtype) for a in lands],
        in_specs=[SEM] * 4 + [HBM] * n + [ANY], out_specs=[HBM] * n,
        input_output_aliases={4 + i: i for i in range(n)},
        compiler_params=SPLIT_PARAMS,
    )(send_sems, recv_sems, fwd_send, fwd_recv, *lands, after)


def _pair_plan(src_ref, land_ref, x, y, c):
    return [(src_ref.at[2 * k + (1 - c)], land_ref.at[k], (x, y, 1 - c)) for k in range(N_CHIPS)]


def _chip_plan(src_ref, land_ref, x, y, c):
    chips = [(1 - x, y), (x, 1 - y), (1 - x, 1 - y)]
    return [(src_ref.at[2 * px + py], land_ref.at[k], (px, py, c)) for k, (px, py) in enumerate(chips)]


def _exchange_copies(plan, per, src_refs, land_refs, send_sems, recv_sems):
    x, y, c = _my_place()
    copies = []
    for i, (s_ref, l_ref) in enumerate(zip(src_refs, land_refs)):
        for q, (src, dst, to) in enumerate(plan(s_ref, l_ref, x, y, c)):
            copies.append(pltpu.make_async_remote_copy(
                src_ref=src, dst_ref=dst, send_sem=send_sems.at[per * i + q], recv_sem=recv_sems.at[per * i + q],
                device_id=to, device_id_type=MESH))
    return copies


def _exchange_start(srcs, plan, per, *, name):
    n = len(srcs)

    def body(*refs):
        src_refs, land_refs = refs[:n], refs[n:2 * n]
        send_sems, recv_sems = refs[2 * n], refs[2 * n + 1]
        for cp in _exchange_copies(plan, per, src_refs, land_refs, send_sems, recv_sems):
            cp.start()
        refs[-1][...] = jnp.zeros_like(refs[-1])

    lands = [lax.empty((per,) + s.shape[1:], s.dtype) for s in srcs]
    outs = pl.pallas_call(
        body, name=name,
        out_shape=[pltpu.SemaphoreType.DMA((per * n,)), pltpu.SemaphoreType.DMA((per * n,))]
        + [_hbm_like(s.shape, s.dtype) for s in srcs] + [_hbm_like(a.shape, a.dtype) for a in lands] + [TOKEN],
        in_specs=[HBM] * (2 * n), out_specs=[SEM, SEM] + [HBM] * (2 * n) + [TOKEN_SPEC],
        input_output_aliases={i: 2 + i for i in range(2 * n)},
        compiler_params=SPLIT_PARAMS,
    )(*[_in_hbm(s) for s in srcs], *[_in_hbm(a) for a in lands])
    return outs[0], outs[1], outs[2:2 + n], outs[2 + n:2 + 2 * n], outs[-1]


def _exchange_wait(send_sems, recv_sems, srcs, lands, plan, per, after, *, name):
    n = len(srcs)
    after = list(after) if isinstance(after, (list, tuple)) else [after]

    def body(*refs):
        send_ref, recv_ref = refs[0], refs[1]
        src_refs, land_refs = refs[2:2 + n], refs[2 + n:2 + 2 * n]
        copies = _exchange_copies(plan, per, src_refs, land_refs, send_ref, recv_ref)
        for cp in copies:
            cp.wait_recv()
        for cp in copies:
            cp.wait_send()

    outs = pl.pallas_call(
        body, name=name,
        out_shape=[_hbm_like(s.shape, s.dtype) for s in srcs] + [_hbm_like(a.shape, a.dtype) for a in lands],
        in_specs=[SEM, SEM] + [HBM] * (2 * n) + [ANY] * len(after), out_specs=[HBM] * (2 * n),
        input_output_aliases={2 + i: i for i in range(2 * n)},
        compiler_params=SPLIT_PARAMS,
    )(send_sems, recv_sems, *srcs, *lands, *after)
    return outs[:n], outs[n:]


REDUCE_BLOCK_BYTES = 2 << 20


def _row_tile(r, c):
    row_bytes = 4 * (-(-c // LANES) * LANES)
    best = r
    for d in range(SUBLANES, r, SUBLANES):
        if r % d == 0 and d * row_bytes <= REDUCE_BLOCK_BYTES:
            best = d
    return best if r * row_bytes > REDUCE_BLOCK_BYTES else r


def _reduce_pair_sum(blocked, recv, place, wire_dtype, *, name):
    _, r, c = blocked.shape
    tr = _row_tile(r, c)

    def body(place_ref, g_ref, r_ref, own_ref, send_ref):
        s = g_ref[...] + r_ref[...]
        send_ref[...] = s.astype(wire_dtype)

        @pl.when(pl.program_id(1) == place_ref[1])
        def _():
            own_ref[...] = s

    return pl.pallas_call(
        body, name=name,
        grid_spec=pltpu.PrefetchScalarGridSpec(
            num_scalar_prefetch=1, grid=(r // tr, N_CHIPS),
            in_specs=[pl.BlockSpec((None, None, tr, c), lambda i, k, place_ref: (k, place_ref[0], i, 0)),
                      pl.BlockSpec((None, tr, c), lambda i, k, place_ref: (k, i, 0))],
            out_specs=[pl.BlockSpec((tr, c), lambda i, k, place_ref: (i, 0)),
                       pl.BlockSpec((None, tr, c), lambda i, k, place_ref: (k, i, 0))]),
        out_shape=[jax.ShapeDtypeStruct((r, c), F32), jax.ShapeDtypeStruct((N_CHIPS, r, c), wire_dtype)],
        compiler_params=_params(("parallel", "arbitrary")),
    )(place, blocked.reshape(N_CHIPS, 2, r, c), recv)


def _chip_sum(own_ref, r_ref):
    return ((own_ref[...] + r_ref[0].astype(F32)) + r_ref[1].astype(F32)) + r_ref[2].astype(F32)


def _reduce_chip_sum(own, recv, *, name):
    r, c = own.shape
    tr = _row_tile(r, c)

    def body(own_ref, r_ref, o_ref):
        o_ref[...] = _chip_sum(own_ref, r_ref)

    return pl.pallas_call(
        body, name=name, grid=(r // tr,),
        in_specs=[pl.BlockSpec((tr, c), lambda i: (i, 0)), pl.BlockSpec((N_CHIPS - 1, tr, c), lambda i: (0, i, 0))],
        out_specs=pl.BlockSpec((tr, c), lambda i: (i, 0)),
        out_shape=jax.ShapeDtypeStruct((r, c), F32),
        compiler_params=_params(("parallel",)),
    )(own, recv)


def _adamw_math(w, g, m, v):
    nm = ADAM_B1 * m + (1.0 - ADAM_B1) * g
    nv = ADAM_B2 * v + (1.0 - ADAM_B2) * (g * g)
    m_hat = nm / (1.0 - ADAM_B1 ** ADAM_STEP)
    v_hat = nv / (1.0 - ADAM_B2 ** ADAM_STEP)
    return -ADAM_LR * (m_hat / (jnp.sqrt(v_hat) + ADAM_EPS) + ADAM_WD * w), nm, nv


ADAMW_ROWS = 256


def _adamw_small(ws, gs, ms, vs, *, name):
    n = len(ws)

    def rows_of(a):
        return a.reshape(-1, a.shape[-1])

    def body(*refs):
        for i in range(n):
            w_ref, g_ref, m_ref, v_ref = refs[4 * i:4 * i + 4]
            outs = refs[4 * n + 3 * i:4 * n + 3 * i + 3]
            rows = w_ref.shape[0]
            if rows % ADAMW_ROWS:
                outs[0][...], outs[1][...], outs[2][...] = _adamw_math(w_ref[...], g_ref[...], m_ref[...], v_ref[...])
                continue

            def chunk(s, carry, w_ref=w_ref, g_ref=g_ref, m_ref=m_ref, v_ref=v_ref, outs=outs):
                r = pl.ds(pl.multiple_of(s * ADAMW_ROWS, ADAMW_ROWS), ADAMW_ROWS)
                outs[0][r, :], outs[1][r, :], outs[2][r, :] = _adamw_math(w_ref[r, :], g_ref[r, :], m_ref[r, :], v_ref[r, :])
                return carry

            lax.fori_loop(0, rows // ADAMW_ROWS, chunk, 0)

    vmem = pl.BlockSpec(memory_space=pltpu.VMEM)
    outs = pl.pallas_call(
        body, name=name, in_specs=[vmem] * (4 * n), out_specs=[vmem] * (3 * n),
        out_shape=[jax.ShapeDtypeStruct(rows_of(w).shape, F32) for w in ws for _ in range(3)],
        compiler_params=_params(),
    )(*[rows_of(a) for quad in zip(ws, gs, ms, vs) for a in quad])
    return [tuple(o.reshape(w.shape) for o in outs[3 * i:3 * i + 3]) for i, w in enumerate(ws)]


def _reduce_adamw(own, recv, w, m, v, layer, prev, *, name):
    r, c = own.shape
    tr = _row_tile(r, c)
    n_prev = 0 if prev is None else len(prev)

    def body(own_ref, r_ref, w_ref, m_ref, v_ref, *rest):
        g_ref, d_ref, nm_ref, nv_ref = rest[n_prev:]
        g = _chip_sum(own_ref, r_ref)
        g_ref[...] = g
        d_ref[...], nm_ref[...], nv_ref[...] = _adamw_math(w_ref[...], g, m_ref[...], v_ref[...])

    slot = pl.BlockSpec((None, tr, c), lambda i: (layer, i, 0))
    return pl.pallas_call(
        body, name=name, grid=(r // tr,),
        in_specs=[pl.BlockSpec((tr, c), lambda i: (i, 0)), pl.BlockSpec((N_CHIPS - 1, tr, c), lambda i: (0, i, 0)),
                  slot, slot, slot] + [ANY] * n_prev,
        out_specs=[slot] * 4,
        out_shape=[jax.ShapeDtypeStruct((DEPTH, r, c), F32)] * 4,
        input_output_aliases={5 + k: k for k in range(n_prev)},
        compiler_params=_params(("parallel",)),
    )(own, recv, w, m, v, *(prev or ()))


REPLICATED = (("mix_norm", (D_MODEL,)), ("q_norm", (HEAD_DIM,)), ("k_norm", (HEAD_DIM,)), ("sinks", (N_Q_HEADS,)),
              ("sgu_norm", (SGU_WIDTH,)), ("w_s", (SGU_GROUPS, BLOCK, BLOCK)), ("b_s", (SGU_GROUPS, BLOCK)),
              ("ffn_norm", (D_MODEL,)), ("conv_b", (2 * D_FF,)))
TRANSPOSED = ("w_in", "w_up")
SHARDED = (("w_in", "rows"), ("w_oa", "cols"), ("w_ob", "cols"), ("w_out", "rows"), ("w_up", "rows"),
           ("conv_w", "blocks"), ("w_down", "rows"))
WEIGHT_ORDER = ("mix_norm", "w_in", "q_norm", "k_norm", "sinks", "sgu_norm", "w_s", "b_s", "w_oa", "w_ob", "w_out",
                "ffn_norm", "w_up", "conv_w", "conv_b", "w_down")
MIXER_WEIGHTS = ["w_in", "w_oa", "w_ob", "w_out"]
FFN_WEIGHTS = ["w_up", "conv_w", "w_down"]


def _small_layout():
    segs, off = {}, 0
    for name, shape in sorted(REPLICATED, key=lambda named: -math.prod(named[1])):
        for l in range(DEPTH):
            n = math.prod(shape)
            segs[(l, name)] = (off, n)
            off += n
    per_dev = -(-off // (N_DEV * SUBLANES * LANES)) * SUBLANES * LANES
    return segs, off, per_dev


def _pack_small(grads, loss_part):
    ssegs, total, per_dev = _small_layout()
    flat = jnp.concatenate([grads[l][name].reshape(-1) for (l, name) in ssegs] + [loss_part.reshape(1)])
    return jnp.pad(flat, (0, N_DEV * per_dev - total - 1)).reshape(N_DEV, per_dev // LANES, LANES)


def _unpack_small(gathered):
    ssegs, total, _ = _small_layout()
    flat = gathered.reshape(-1)
    small = {}
    for name, shape in REPLICATED:
        start, n = ssegs[(0, name)]
        small[name] = lax.optimization_barrier(flat[start:start + DEPTH * n]).reshape((DEPTH,) + shape)
    return small, flat[total]


def kernel(x, mix_norm, w_in, q_norm, k_norm, sinks, sgu_norm, w_s, b_s, w_oa, w_ob, w_out, ffn_norm, w_up, conv_w, conv_b, w_down, loss_target, m_mix_norm, m_w_in, m_q_norm, m_k_norm, m_sinks, m_sgu_norm, m_w_s, m_b_s, m_w_oa, m_w_ob, m_w_out, m_ffn_norm, m_w_up, m_conv_w, m_conv_b, m_w_down, v_mix_norm, v_w_in, v_q_norm, v_k_norm, v_sinks, v_sgu_norm, v_w_s, v_b_s, v_w_oa, v_w_ob, v_w_out, v_ffn_norm, v_w_up, v_conv_w, v_conv_b, v_w_down):
    W = dict(mix_norm=mix_norm, w_in=w_in, q_norm=q_norm, k_norm=k_norm, sinks=sinks, sgu_norm=sgu_norm, w_s=w_s, b_s=b_s,
             w_oa=w_oa, w_ob=w_ob, w_out=w_out, ffn_norm=ffn_norm, w_up=w_up, conv_w=conv_w, conv_b=conv_b, w_down=w_down)
    M = dict(mix_norm=m_mix_norm, w_in=m_w_in, q_norm=m_q_norm, k_norm=m_k_norm, sinks=m_sinks, sgu_norm=m_sgu_norm,
             w_s=m_w_s, b_s=m_b_s, w_oa=m_w_oa, w_ob=m_w_ob, w_out=m_w_out, ffn_norm=m_ffn_norm, w_up=m_w_up,
             conv_w=m_conv_w, conv_b=m_conv_b, w_down=m_w_down)
    V = dict(mix_norm=v_mix_norm, w_in=v_w_in, q_norm=v_q_norm, k_norm=v_k_norm, sinks=v_sinks, sgu_norm=v_sgu_norm,
             w_s=v_w_s, b_s=v_b_s, w_oa=v_w_oa, w_ob=v_w_ob, w_out=v_w_out, ffn_norm=v_ffn_norm, w_up=v_w_up,
             conv_w=v_conv_w, conv_b=v_conv_b, w_down=v_w_down)
    n_seq, seq, d_model = x.shape
    tokens = n_seq * seq
    mx, my, mc = _my_place()
    place = jnp.stack([mc, 2 * mx + my]).astype(jnp.int32)
    half = N_DEV // 2
    kind_of = dict(SHARDED)
    for name in TRANSPOSED:
        W[name], M[name], V[name] = (jnp.swapaxes(t[name], 1, 2) for t in (W, M, V))

    gather_groups = [[(0, MIXER_WEIGHTS[0])], [(0, n) for n in MIXER_WEIGHTS[1:]], [(0, n) for n in FFN_WEIGHTS],
                     [(1, n) for n in MIXER_WEIGHTS], [(1, n) for n in FFN_WEIGHTS]]
    started, in_flight = {}, {}
    weights = []
    for l in range(DEPTH):
        w = {name: W[name][l] for name, _ in REPLICATED}
        w["cb_g"], w["cb_v"] = W["conv_b"][l][:D_FF], W["conv_b"][l][D_FF:]
        w["bias_full"] = jnp.repeat(W["b_s"][l].T, SGU_WIDTH // SGU_GROUPS, axis=1)
        weights.append(w)

    def gather_start(gi, after=()):
        stacks = [W[name] for _, name in gather_groups[gi]]
        kinds = [kind_of[name] for _, name in gather_groups[gi]]
        shapes = [s.shape[1:] for s in stacks]
        lands = _place_own(stacks, [l for l, _ in gather_groups[gi]], kinds,
                           [F32 if name == "conv_w" else BF16 for _, name in gather_groups[gi]],
                           name=f"gather_weights_own_{gi}", deps=after)
        send, recv, lands, token = _gather_start(lands, kinds, shapes, after, name=f"gather_weights_start_{gi}")
        started[gi] = dict(sems=(send, recv), lands=lands, kinds=kinds, shapes=shapes)
        return token

    def gather_forward(gi, after):
        st = started[gi]
        in_flight[gi] = _gather_forward(st["sems"][1], st["lands"], st["kinds"], st["shapes"], after,
                                        name=f"gather_weights_forward_{gi}")
        return in_flight[gi][3]

    def gather_finish(gi, after):
        st = started.pop(gi)
        fwd_send, fwd_recv, lands_g, _ = in_flight.pop(gi)
        whole = _gather_finish(st["sems"][0], st["sems"][1], fwd_send, fwd_recv, lands_g, st["kinds"], st["shapes"], after,
                               name=f"gather_weights_finish_{gi}")
        for (l, name), arr in zip(gather_groups[gi], whole):
            w = weights[l]
            if name in TRANSPOSED:
                w[name + "_t"] = arr
            elif name == "conv_w":
                w["cw_g"] = arr[:half].transpose(1, 0, 2).reshape(3, D_FF)
                w["cw_v"] = arr[half:].transpose(1, 0, 2).reshape(3, D_FF)
            else:
                w[name] = arr

    reduce_state, results = {}, {}
    wire = {"conv_w": F32, "small": F32}

    def reduce_begin(key, names, arrays):
        send, recv, srcs_, lands_, token = _exchange_start(arrays, _pair_plan, N_CHIPS, name=f"reduce_pair_start_{key}")
        reduce_state[key] = dict(names=names, pair=(send, recv, srcs_, lands_))
        return [token]

    def reduce_pair(key, after):
        st = reduce_state[key]
        send, recv, srcs_, lands_ = st.pop("pair")
        blocked_, from_sibling = _exchange_wait(send, recv, srcs_, lands_, _pair_plan, N_CHIPS, after,
                                                name=f"reduce_pair_wait_{key}")
        sums = [_reduce_pair_sum(b, r, place, wire.get(n if isinstance(n, str) else n[1], BF16),
                                 name=f"reduce_pair_sum_{key}_{i}")
                for i, (n, b, r) in enumerate(zip(st["names"], blocked_, from_sibling))]
        st["own"] = [s[0] for s in sums]
        *st["chip"], token = _exchange_start([s[1] for s in sums], _chip_plan, N_CHIPS - 1, name=f"reduce_chip_start_{key}")
        return [token]

    def reduce_end(key, after):
        st = reduce_state.pop(key)
        send, recv, srcs_, lands_ = st["chip"]
        _, from_chips = _exchange_wait(send, recv, srcs_, lands_, _chip_plan, N_CHIPS - 1, after,
                                       name=f"reduce_chip_wait_{key}")
        done = []
        for n, own, got in zip(st["names"], st["own"], from_chips):
            if n == "small":
                results["small"] = _reduce_chip_sum(own, got, name="reduce_chip_sum_small")
            else:
                l, name = n
                results[name] = _reduce_adamw(own, got, W[name], M[name], V[name], l, results.get(name),
                                              name=f"l{l}_reduce_adamw_{name}")
                done.append(results[name][0])
        return done

    def sched(point, l, carry, g=None):
        deps = []
        if point == "begin":
            token = ()
            for gi in range(len(gather_groups)):
                token = [gather_start(gi, token)]
            deps = token
        elif point == "fwd_start" and l == 0:
            gather_finish(0, gather_forward(0, carry))
        elif point == "fwd_proj" and l == 0:
            deps = [gather_forward(1, carry)]
        elif point == "fwd_att" and l == 0:
            gather_finish(1, carry)
            deps = [gather_forward(2, carry)]
        elif point == "fwd_mixer_done" and l == 0:
            gather_finish(2, carry)
        elif point == "fwd_conv" and l == 0:
            deps = [gather_forward(3, carry)]
        elif point == "fwd_start" and l == 1:
            gather_finish(3, carry)
        elif point == "fwd_att" and l == 1:
            deps = [gather_forward(4, carry)]
        elif point == "fwd_mixer_done" and l == 1:
            gather_finish(4, carry)
        elif point == "bwd_ffn_grads":
            conv_w = jnp.concatenate([g[k].reshape(3, half, W_UP_SHARD).transpose(1, 0, 2) for k in ("cw_g", "cw_v")])
            deps = reduce_begin(
                f"l{l}_ffn", [(l, "w_down"), (l, "w_up"), (l, "conv_w")],
                [g["w_down"].reshape(N_DEV, D_FF // N_DEV, D_MODEL),
                 g["w_up_t"].reshape(N_DEV, W_UP_SHARD, D_MODEL), conv_w])
        elif point == "bwd_merge":
            deps = reduce_pair(f"l{l}_ffn", carry)
        elif point == "bwd_out_grads":
            deps = reduce_begin(
                f"l{l}_out", [(l, "w_out"), (l, "w_oa"), (l, "w_ob")],
                [g["w_out"].reshape(N_DEV, D_MODEL // N_DEV, D_MODEL),
                 _disassemble((g["w_oa"],), LANES, _w_o_moves(), name=f"l{l}_split_dw_oa"),
                 _disassemble((g["w_ob"],), LANES, _w_o_moves(), name=f"l{l}_split_dw_ob")])
        elif point == "bwd_att":
            deps = reduce_pair(f"l{l}_out", carry)
        elif point == "bwd_w_in_grad":
            deps = reduce_begin(f"l{l}_in", [(l, "w_in")], [g["w_in_t"].reshape(N_DEV, W_IN_SHARD, D_MODEL)])
        elif point == "bwd_dh":
            deps = reduce_pair(f"l{l}_in", carry)
        return deps

    loss_part, dx, grads, last_deps = _local_step(x.reshape(tokens, d_model), loss_target.reshape(tokens, d_model),
                                                  weights, sched, n_seq=n_seq, seq=seq)
    for g in grads:
        g["conv_b"] = jnp.concatenate([g["cb_g"], g["cb_v"]])
    after = [dx, *last_deps, *reduce_begin("small", ["small"], [_pack_small(grads, loss_part)])]
    for key in [f"l{l}_{part}" for l in reversed(range(DEPTH)) for part in ("ffn", "out", "in")][:-1]:
        after = reduce_end(key, after)
    after = reduce_end("l0_in", after + reduce_pair("small", after))
    reduce_end("small", after)

    G, delta, new_m, new_v = {}, {}, {}, {}
    for name, _ in SHARDED:
        outs = [jnp.swapaxes(o, 1, 2) for o in results[name]] if name in TRANSPOSED else results[name]
        G[name], delta[name], new_m[name], new_v[name] = outs
    small, loss = _unpack_small(_gather([results["small"]], ["blocks"], name="gather_small_grads")[0])
    G.update(small)
    names = [name for name, _ in REPLICATED]
    stepped = _adamw_small(*[[t[name] for name in names] for t in (W, G, M, V)], name="adamw_replicated")
    for name, stepped_one in zip(names, stepped):
        delta[name], new_m[name], new_v[name] = stepped_one
    return (loss, dx.reshape(n_seq, seq, d_model), *[G[n] for n in WEIGHT_ORDER], *[delta[n] for n in WEIGHT_ORDER],
            *[new_m[n] for n in WEIGHT_ORDER], *[new_v[n] for n in WEIGHT_ORDER])
```
